```python
import math
import jax, jax.numpy as jnp
from jax import lax
import numpy as np

D_MODEL = 1024
BATCH = 8
SEQ = 4096
DEPTH = 4

HEAD_DIM = 128
N_MIX_HEADS = D_MODEL // HEAD_DIM
N_DN_HEADS = N_MIX_HEADS // 2
N_FOX_HEADS = N_MIX_HEADS - N_DN_HEADS
N_SB_HEADS = N_MIX_HEADS
D_DN = N_DN_HEADS * HEAD_DIM
D_FOX = N_FOX_HEADS * HEAD_DIM
CONV_WIDTH = 4
CHUNK = 64
Q_BLOCK = 128
D_FF = 2816
N_EVEN = (DEPTH + 1) // 2
N_ODD = DEPTH // 2
EPS = 1e-6
EVEN_SPLITS = (3 * D_DN, D_DN, N_DN_HEADS, N_DN_HEADS, D_FOX, D_FOX, D_FOX, D_FOX, N_FOX_HEADS)
D_IN_EVEN = 3 * D_DN + D_DN + 2 * N_DN_HEADS + 4 * D_FOX + N_FOX_HEADS
D_IN_ODD = 3 * N_SB_HEADS * HEAD_DIM

kernel_name = 'hybrid_deltanet_fox_stickbreak_macaron'


def _split(t, sizes):
    out, off = [], 0
    for s in sizes:
        out.append(t[..., off:off + s])
        off += s
    return out


def rmsnorm(x, g):
    xf = x.astype(jnp.float32)
    y = xf * lax.rsqrt(jnp.mean(xf * xf, axis=-1, keepdims=True) + EPS)
    return (y * g.astype(jnp.float32)).astype(x.dtype)


def l2norm(t):
    tf = t.astype(jnp.float32)
    return tf * lax.rsqrt(jnp.sum(tf * tf, axis=-1, keepdims=True) + EPS)


def swiglu_ffn(x, w_gu, w_down):
    g, u = jnp.split(x @ w_gu, 2, axis=-1)
    return (jax.nn.silu(g) * u) @ w_down


def split_heads(t, n):
    b, s, _ = t.shape
    return t.reshape(b, s, n, -1).transpose(0, 2, 1, 3)


def merge_heads(t):
    b, n, s, d = t.shape
    return t.transpose(0, 2, 1, 3).reshape(b, s, n * d)


def causal_conv_silu(x, w):
    s = x.shape[1]
    xp = jnp.pad(x, ((0, 0), (CONV_WIDTH - 1, 0), (0, 0)))
    y = sum(xp[:, i:i + s, :] * w[i] for i in range(CONV_WIDTH))
    return jax.nn.silu(y)


def gated_delta_rule(q, k, v, beta, g):
    b, h, s, d = q.shape
    n = s // CHUNK
    q = q * d ** -0.5
    rc = lambda t: t.reshape(b, h, n, CHUNK, *t.shape[3:])
    q, k, v, beta, g = (rc(t) for t in (q, k, v, beta, g))
    gc = jnp.cumsum(g, axis=-1)
    tri_incl = jnp.tril(jnp.ones((CHUNK, CHUNK), bool))
    tri_strict = jnp.tril(jnp.ones((CHUNK, CHUNK), bool), -1)
    decay_mat = jnp.where(tri_incl, jnp.exp(jnp.where(tri_incl, gc[..., :, None] - gc[..., None, :], 0.0)), 0.0)
    k_beta = k * beta[..., None]
    v_beta = v * beta[..., None]
    a = jnp.where(tri_strict, jnp.einsum('bhnid,bhnjd->bhnij', k_beta, k) * decay_mat, 0.0)
    lhs = jnp.eye(CHUNK, dtype=jnp.float32) + a
    rhs = jnp.concatenate([v_beta, k_beta * jnp.exp(gc)[..., None]], axis=-1)
    sol = lax.linalg.triangular_solve(lhs, rhs, left_side=True, lower=True)
    u, w = sol[..., :d], sol[..., d:]
    attn_intra = jnp.einsum('bhnid,bhnjd->bhnij', q, k) * decay_mat
    q_dec = q * jnp.exp(gc)[..., None]
    k_dec = k * jnp.exp(gc[..., -1:] - gc)[..., None]
    g_last = jnp.exp(gc[..., -1])

    def step(state, inp):
        u_c, w_c, qd_c, kd_c, at_c, gl_c = inp
        v_new = u_c - jnp.einsum('bhcd,bhde->bhce', w_c, state)
        o = jnp.einsum('bhcd,bhde->bhce', qd_c, state) + jnp.einsum('bhij,bhje->bhie', at_c, v_new)
        state = state * gl_c[..., None, None] + jnp.einsum('bhcd,bhce->bhde', kd_c, v_new)
        return state, o

    xs = tuple(jnp.moveaxis(t, 2, 0) for t in (u, w, q_dec, k_dec, attn_intra, g_last))
    _, o = lax.scan(step, jnp.zeros((b, h, d, d), jnp.float32), xs)
    return jnp.moveaxis(o, 0, 2).reshape(b, h, s, d)


def forgetting_attention(q, k, v, log_f):
    b, h, s, d = q.shape
    nb = s // Q_BLOCK
    c = jnp.cumsum(log_f, axis=-1)
    qb = q.reshape(b, h, nb, Q_BLOCK, d).transpose(2, 0, 1, 3, 4)
    cb = c.reshape(b, h, nb, Q_BLOCK).transpose(2, 0, 1, 3)
    pos_k = jnp.arange(s)

    def block(args):
        i, q_i, c_i = args
        pos_q = i * Q_BLOCK + jnp.arange(Q_BLOCK)
        logits = jnp.einsum('bhqd,bhkd->bhqk', q_i, k).astype(jnp.float32) * d ** -0.5
        logits = logits + (c_i[..., :, None] - c[..., None, :])
        logits = jnp.where(pos_k[None, :] <= pos_q[:, None], logits, -jnp.inf)
        p = jax.nn.softmax(logits, axis=-1)
        return jnp.einsum('bhqk,bhkd->bhqd', p.astype(v.dtype), v)

    o = lax.map(block, (jnp.arange(nb), qb, cb))
    return o.transpose(1, 2, 0, 3, 4).reshape(b, h, s, d)


def stick_breaking_attention(q, k, v):
    b, h, s, d = q.shape
    nb = s // Q_BLOCK
    qb = q.reshape(b, h, nb, Q_BLOCK, d).transpose(2, 0, 1, 3, 4)
    pos_k = jnp.arange(s)

    def block(args):
        i, q_i = args
        pos_q = i * Q_BLOCK + jnp.arange(Q_BLOCK)
        z = jnp.einsum('bhqd,bhkd->bhqk', q_i, k).astype(jnp.float32) * d ** -0.5
        before = pos_k[None, :] < pos_q[:, None]
        log_1m = jnp.where(before, jax.nn.log_sigmoid(-z), 0.0)
        tail = lax.cumsum(log_1m, axis=3, reverse=True) - log_1m
        a = jnp.where(before, jnp.exp(jax.nn.log_sigmoid(z) + tail), 0.0)
        return jnp.einsum('bhqk,bhkd->bhqd', a.astype(v.dtype), v)

    o = lax.map(block, (jnp.arange(nb), qb))
    return o.transpose(1, 2, 0, 3, 4).reshape(b, h, s, d)


def deltanet_fox_mixer(h, w_in, conv_w, a_log, dt_bias, dn_norm_g, q_norm_g, k_norm_g, f_bias, w_out):
    dn_qkv, dn_gate, dn_b, dn_a, fq, fk, fv, f_gate, f_pre = _split(h @ w_in, EVEN_SPLITS)
    dq, dk, dv = jnp.split(causal_conv_silu(dn_qkv, conv_w), 3, axis=-1)
    dq = l2norm(split_heads(dq, N_DN_HEADS))
    dk = l2norm(split_heads(dk, N_DN_HEADS))
    dv = split_heads(dv, N_DN_HEADS).astype(jnp.float32)
    beta = jax.nn.sigmoid(dn_b.astype(jnp.float32)).transpose(0, 2, 1)
    g = (-jnp.exp(a_log.astype(jnp.float32)) * jax.nn.softplus(dn_a.astype(jnp.float32) + dt_bias.astype(jnp.float32))).transpose(0, 2, 1)
    o_dn = gated_delta_rule(dq, dk, dv, beta, g).astype(h.dtype)
    o_dn = merge_heads(rmsnorm(o_dn, dn_norm_g)) * jax.nn.silu(dn_gate)
    fq = rmsnorm(split_heads(fq, N_FOX_HEADS), q_norm_g)
    fk = rmsnorm(split_heads(fk, N_FOX_HEADS), k_norm_g)
    fv = split_heads(fv, N_FOX_HEADS)
    log_f = jax.nn.log_sigmoid(f_pre.astype(jnp.float32) + f_bias.astype(jnp.float32)).transpose(0, 2, 1)
    o_fox = merge_heads(forgetting_attention(fq, fk, fv, log_f)) * jax.nn.sigmoid(f_gate)
    return jnp.concatenate([o_dn, o_fox], axis=-1) @ w_out


def stick_breaking_mixer(h, w_in, w_out):
    q, k, v = jnp.split(h @ w_in, 3, axis=-1)
    o = stick_breaking_attention(split_heads(q, N_SB_HEADS), split_heads(k, N_SB_HEADS), split_heads(v, N_SB_HEADS))
    return merge_heads(o) @ w_out


def _fwd_setup_inputs(seed: int = 0) -> dict:
    key = jax.random.key(seed)
    ks = jax.random.split(key, 20)
    f32 = jnp.float32

    def w(k, shape, fan_in):
        return jax.random.normal(k, shape, f32) * fan_in ** -0.5

    def gain(k, shape):
        return 1.0 + 0.02 * jax.random.normal(k, shape, f32)

    dt = jnp.exp(jax.random.uniform(ks[8], (N_EVEN, N_DN_HEADS), f32, math.log(1e-3), math.log(1e-1)))
    return {
        'x': jax.random.normal(ks[0], (BATCH, SEQ, D_MODEL), f32),
        'norm_ffn1': gain(ks[1], (DEPTH, D_MODEL)),
        'ffn1_w_gu': w(ks[2], (DEPTH, D_MODEL, 2 * D_FF), D_MODEL),
        'ffn1_w_down': w(ks[3], (DEPTH, D_FF, D_MODEL), D_FF),
        'norm_mix': gain(ks[4], (DEPTH, D_MODEL)),
        'w_in_even': w(ks[5], (N_EVEN, D_MODEL, D_IN_EVEN), D_MODEL),
        'dn_conv_w': w(ks[6], (N_EVEN, CONV_WIDTH, 3 * D_DN), CONV_WIDTH),
        'dn_a_log': jnp.log(jax.random.uniform(ks[7], (N_EVEN, N_DN_HEADS), f32, 1.0, 16.0)),
        'dn_dt_bias': dt + jnp.log(-jnp.expm1(-dt)),
        'dn_norm_g': gain(ks[9], (N_EVEN, HEAD_DIM)),
        'fox_q_norm_g': gain(ks[10], (N_EVEN, HEAD_DIM)),
        'fox_k_norm_g': gain(ks[11], (N_EVEN, HEAD_DIM)),
        'fox_f_bias': jax.random.uniform(ks[12], (N_EVEN, N_FOX_HEADS), f32, 1.0, 4.0),
        'w_out_even': w(ks[13], (N_EVEN, D_DN + D_FOX, D_MODEL), D_DN + D_FOX),
        'w_in_odd': w(ks[14], (N_ODD, D_MODEL, D_IN_ODD), D_MODEL),
        'w_out_odd': w(ks[15], (N_ODD, N_SB_HEADS * HEAD_DIM, D_MODEL), N_SB_HEADS * HEAD_DIM),
        'norm_ffn2': gain(ks[16], (DEPTH, D_MODEL)),
        'ffn2_w_gu': w(ks[17], (DEPTH, D_MODEL, 2 * D_FF), D_MODEL),
        'ffn2_w_down': w(ks[18], (DEPTH, D_FF, D_MODEL), D_FF),
    }


def _fwd_reference(x, norm_ffn1, ffn1_w_gu, ffn1_w_down, norm_mix, w_in_even, dn_conv_w, dn_a_log,
              dn_dt_bias, dn_norm_g, fox_q_norm_g, fox_k_norm_g, fox_f_bias, w_out_even,
              w_in_odd, w_out_odd, norm_ffn2, ffn2_w_gu, ffn2_w_down):
    for l in range(DEPTH):
        x = x + 0.5 * swiglu_ffn(rmsnorm(x, norm_ffn1[l]), ffn1_w_gu[l], ffn1_w_down[l])
        h = rmsnorm(x, norm_mix[l])
        j = l // 2
        if l % 2 == 0:
            x = x + deltanet_fox_mixer(h, w_in_even[j], dn_conv_w[j], dn_a_log[j], dn_dt_bias[j],
                                       dn_norm_g[j], fox_q_norm_g[j], fox_k_norm_g[j], fox_f_bias[j],
                                       w_out_even[j])
        else:
            x = x + stick_breaking_mixer(h, w_in_odd[j], w_out_odd[j])
        x = x + 0.5 * swiglu_ffn(rmsnorm(x, norm_ffn2[l]), ffn2_w_gu[l], ffn2_w_down[l])
    return x


import jax as _jax
import jax.numpy as _jnp

TWIN_FORMAT = 'train_step'
FWD_PARAMS = ['x', 'norm_ffn1', 'ffn1_w_gu', 'ffn1_w_down', 'norm_mix', 'w_in_even', 'dn_conv_w', 'dn_a_log', 'dn_dt_bias', 'dn_norm_g', 'fox_q_norm_g', 'fox_k_norm_g', 'fox_f_bias', 'w_out_even', 'w_in_odd', 'w_out_odd', 'norm_ffn2', 'ffn2_w_gu', 'ffn2_w_down']
TWIN_WEIGHTS = ['norm_ffn1', 'ffn1_w_gu', 'ffn1_w_down', 'norm_mix', 'w_in_even', 'dn_conv_w', 'dn_a_log', 'dn_dt_bias', 'dn_norm_g', 'fox_q_norm_g', 'fox_k_norm_g', 'fox_f_bias', 'w_out_even', 'w_in_odd', 'w_out_odd', 'norm_ffn2', 'ffn2_w_gu', 'ffn2_w_down']
TWIN_DIFF_INPUT = 'x'
TWIN_INPUTS = ['x', 'norm_ffn1', 'ffn1_w_gu', 'ffn1_w_down', 'norm_mix', 'w_in_even', 'dn_conv_w', 'dn_a_log', 'dn_dt_bias', 'dn_norm_g', 'fox_q_norm_g', 'fox_k_norm_g', 'fox_f_bias', 'w_out_even', 'w_in_odd', 'w_out_odd', 'norm_ffn2', 'ffn2_w_gu', 'ffn2_w_down', 'loss_target', 'm_norm_ffn1', 'm_ffn1_w_gu', 'm_ffn1_w_down', 'm_norm_mix', 'm_w_in_even', 'm_dn_conv_w', 'm_dn_a_log', 'm_dn_dt_bias', 'm_dn_norm_g', 'm_fox_q_norm_g', 'm_fox_k_norm_g', 'm_fox_f_bias', 'm_w_out_even', 'm_w_in_odd', 'm_w_out_odd', 'm_norm_ffn2', 'm_ffn2_w_gu', 'm_ffn2_w_down', 'v_norm_ffn1', 'v_ffn1_w_gu', 'v_ffn1_w_down', 'v_norm_mix', 'v_w_in_even', 'v_dn_conv_w', 'v_dn_a_log', 'v_dn_dt_bias', 'v_dn_norm_g', 'v_fox_q_norm_g', 'v_fox_k_norm_g', 'v_fox_f_bias', 'v_w_out_even', 'v_w_in_odd', 'v_w_out_odd', 'v_norm_ffn2', 'v_ffn2_w_gu', 'v_ffn2_w_down']
TWIN_OUTPUTS = ['loss', 'grad_x', 'grad_norm_ffn1', 'grad_ffn1_w_gu', 'grad_ffn1_w_down', 'grad_norm_mix', 'grad_w_in_even', 'grad_dn_conv_w', 'grad_dn_a_log', 'grad_dn_dt_bias', 'grad_dn_norm_g', 'grad_fox_q_norm_g', 'grad_fox_k_norm_g', 'grad_fox_f_bias', 'grad_w_out_even', 'grad_w_in_odd', 'grad_w_out_odd', 'grad_norm_ffn2', 'grad_ffn2_w_gu', 'grad_ffn2_w_down', 'delta_norm_ffn1', 'delta_ffn1_w_gu', 'delta_ffn1_w_down', 'delta_norm_mix', 'delta_w_in_even', 'delta_dn_conv_w', 'delta_dn_a_log', 'delta_dn_dt_bias', 'delta_dn_norm_g', 'delta_fox_q_norm_g', 'delta_fox_k_norm_g', 'delta_fox_f_bias', 'delta_w_out_even', 'delta_w_in_odd', 'delta_w_out_odd', 'delta_norm_ffn2', 'delta_ffn2_w_gu', 'delta_ffn2_w_down', 'new_m_norm_ffn1', 'new_m_ffn1_w_gu', 'new_m_ffn1_w_down', 'new_m_norm_mix', 'new_m_w_in_even', 'new_m_dn_conv_w', 'new_m_dn_a_log', 'new_m_dn_dt_bias', 'new_m_dn_norm_g', 'new_m_fox_q_norm_g', 'new_m_fox_k_norm_g', 'new_m_fox_f_bias', 'new_m_w_out_even', 'new_m_w_in_odd', 'new_m_w_out_odd', 'new_m_norm_ffn2', 'new_m_ffn2_w_gu', 'new_m_ffn2_w_down', 'new_v_norm_ffn1', 'new_v_ffn1_w_gu', 'new_v_ffn1_w_down', 'new_v_norm_mix', 'new_v_w_in_even', 'new_v_dn_conv_w', 'new_v_dn_a_log', 'new_v_dn_dt_bias', 'new_v_dn_norm_g', 'new_v_fox_q_norm_g', 'new_v_fox_k_norm_g', 'new_v_fox_f_bias', 'new_v_w_out_even', 'new_v_w_in_odd', 'new_v_w_out_odd', 'new_v_norm_ffn2', 'new_v_ffn2_w_gu', 'new_v_ffn2_w_down']
TWIN_LEAF_KINDS = {'loss': 'loss', 'grad_x': 'grad_x', 'grad_norm_ffn1': 'grad_w', 'grad_ffn1_w_gu': 'grad_w', 'grad_ffn1_w_down': 'grad_w', 'grad_norm_mix': 'grad_w', 'grad_w_in_even': 'grad_w', 'grad_dn_conv_w': 'grad_w', 'grad_dn_a_log': 'grad_w', 'grad_dn_dt_bias': 'grad_w', 'grad_dn_norm_g': 'grad_w', 'grad_fox_q_norm_g': 'grad_w', 'grad_fox_k_norm_g': 'grad_w', 'grad_fox_f_bias': 'grad_w', 'grad_w_out_even': 'grad_w', 'grad_w_in_odd': 'grad_w', 'grad_w_out_odd': 'grad_w', 'grad_norm_ffn2': 'grad_w', 'grad_ffn2_w_gu': 'grad_w', 'grad_ffn2_w_down': 'grad_w', 'delta_norm_ffn1': 'delta_w', 'delta_ffn1_w_gu': 'delta_w', 'delta_ffn1_w_down': 'delta_w', 'delta_norm_mix': 'delta_w', 'delta_w_in_even': 'delta_w', 'delta_dn_conv_w': 'delta_w', 'delta_dn_a_log': 'delta_w', 'delta_dn_dt_bias': 'delta_w', 'delta_dn_norm_g': 'delta_w', 'delta_fox_q_norm_g': 'delta_w', 'delta_fox_k_norm_g': 'delta_w', 'delta_fox_f_bias': 'delta_w', 'delta_w_out_even': 'delta_w', 'delta_w_in_odd': 'delta_w', 'delta_w_out_odd': 'delta_w', 'delta_norm_ffn2': 'delta_w', 'delta_ffn2_w_gu': 'delta_w', 'delta_ffn2_w_down': 'delta_w', 'new_m_norm_ffn1': 'new_m', 'new_m_ffn1_w_gu': 'new_m', 'new_m_ffn1_w_down': 'new_m', 'new_m_norm_mix': 'new_m', 'new_m_w_in_even': 'new_m', 'new_m_dn_conv_w': 'new_m', 'new_m_dn_a_log': 'new_m', 'new_m_dn_dt_bias': 'new_m', 'new_m_dn_norm_g': 'new_m', 'new_m_fox_q_norm_g': 'new_m', 'new_m_fox_k_norm_g': 'new_m', 'new_m_fox_f_bias': 'new_m', 'new_m_w_out_even': 'new_m', 'new_m_w_in_odd': 'new_m', 'new_m_w_out_odd': 'new_m', 'new_m_norm_ffn2': 'new_m', 'new_m_ffn2_w_gu': 'new_m', 'new_m_ffn2_w_down': 'new_m', 'new_v_norm_ffn1': 'new_v', 'new_v_ffn1_w_gu': 'new_v', 'new_v_ffn1_w_down': 'new_v', 'new_v_norm_mix': 'new_v', 'new_v_w_in_even': 'new_v', 'new_v_dn_conv_w': 'new_v', 'new_v_dn_a_log': 'new_v', 'new_v_dn_dt_bias': 'new_v', 'new_v_dn_norm_g': 'new_v', 'new_v_fox_q_norm_g': 'new_v', 'new_v_fox_k_norm_g': 'new_v', 'new_v_fox_f_bias': 'new_v', 'new_v_w_out_even': 'new_v', 'new_v_w_in_odd': 'new_v', 'new_v_w_out_odd': 'new_v', 'new_v_norm_ffn2': 'new_v', 'new_v_ffn2_w_gu': 'new_v', 'new_v_ffn2_w_down': 'new_v'}


def _forward(args):
    return _fwd_reference(*[args[k] for k in FWD_PARAMS])


def _output_shape():
    def fwd():
        inp = _fwd_setup_inputs(0)
        return _fwd_reference(*[inp[k] for k in FWD_PARAMS])
    out = _jax.eval_shape(fwd)
    return out.shape, out.dtype

N_MICROBATCH = 1
ADAM_LR = 0.001
ADAM_B1 = 0.9
ADAM_B2 = 0.999
ADAM_EPS = 1e-08
ADAM_WD = 0.01
ADAM_STEP = 10
PER_EXAMPLE_BATCH_AXIS = {'x': 0, 'loss_target': 0}
SHARED_INPUTS = []
_WEIGHT_DTYPES = {'norm_ffn1': _jnp.float32, 'ffn1_w_gu': _jnp.float32, 'ffn1_w_down': _jnp.float32, 'norm_mix': _jnp.float32, 'w_in_even': _jnp.float32, 'dn_conv_w': _jnp.float32, 'dn_a_log': _jnp.float32, 'dn_dt_bias': _jnp.float32, 'dn_norm_g': _jnp.float32, 'fox_q_norm_g': _jnp.float32, 'fox_k_norm_g': _jnp.float32, 'fox_f_bias': _jnp.float32, 'w_out_even': _jnp.float32, 'w_in_odd': _jnp.float32, 'w_out_odd': _jnp.float32, 'norm_ffn2': _jnp.float32, 'ffn2_w_gu': _jnp.float32, 'ffn2_w_down': _jnp.float32}
MOMENT_SCALE = {'norm_ffn1': 5.997927e+00, 'ffn1_w_gu': 1.741514e-01, 'ffn1_w_down': 2.978740e-01, 'norm_mix': 1.411348e+01, 'w_in_even': 4.724213e-01, 'dn_conv_w': 8.687854e-01, 'dn_a_log': 1.206497e+01, 'dn_dt_bias': 1.171186e+01, 'dn_norm_g': 5.168428e+01, 'fox_q_norm_g': 1.711127e+00, 'fox_k_norm_g': 1.710896e+00, 'fox_f_bias': 5.764934e+01, 'w_out_even': 1.364620e+00, 'w_in_odd': 6.656595e-01, 'w_out_odd': 1.026298e+00, 'norm_ffn2': 6.092400e+00, 'ffn2_w_gu': 1.476334e-01, 'ffn2_w_down': 2.557422e-01}


def _to_microbatches(a, axis):
    t = _jnp.moveaxis(a, axis, 0)
    t = t.reshape((N_MICROBATCH, t.shape[0] // N_MICROBATCH) + t.shape[1:])
    return _jnp.moveaxis(t, 1, axis + 1)


def setup_inputs(seed: int = 0) -> dict:
    inp = _fwd_setup_inputs(seed)
    key = _jax.random.fold_in(_jax.random.key(seed), 7919)
    shape, _ = _output_shape()
    out = dict(inp)
    out["loss_target"] = _jax.random.normal(_jax.random.fold_in(key, 0), shape, _jnp.float32)
    for i, name in enumerate(TWIN_WEIGHTS):
        w = inp[name].astype(_jnp.float32)
        if MOMENT_SCALE is None:
            s = _jnp.sqrt(_jnp.mean(_jnp.square(w)) + 1e-30)
        else:
            s = MOMENT_SCALE[name]
        km, kv = _jax.random.split(_jax.random.fold_in(key, i + 1))
        out[name] = w
        out["m_" + name] = s * _jax.random.normal(km, w.shape, _jnp.float32)
        out["v_" + name] = (s * s) * _jax.random.uniform(kv, w.shape, _jnp.float32, 0.5, 1.5)
    if N_MICROBATCH > 1:
        for name, axis in PER_EXAMPLE_BATCH_AXIS.items():
            out[name] = _to_microbatches(out[name], axis)
    return {'x': out['x'], 'norm_ffn1': out['norm_ffn1'], 'ffn1_w_gu': out['ffn1_w_gu'], 'ffn1_w_down': out['ffn1_w_down'], 'norm_mix': out['norm_mix'], 'w_in_even': out['w_in_even'], 'dn_conv_w': out['dn_conv_w'], 'dn_a_log': out['dn_a_log'], 'dn_dt_bias': out['dn_dt_bias'], 'dn_norm_g': out['dn_norm_g'], 'fox_q_norm_g': out['fox_q_norm_g'], 'fox_k_norm_g': out['fox_k_norm_g'], 'fox_f_bias': out['fox_f_bias'], 'w_out_even': out['w_out_even'], 'w_in_odd': out['w_in_odd'], 'w_out_odd': out['w_out_odd'], 'norm_ffn2': out['norm_ffn2'], 'ffn2_w_gu': out['ffn2_w_gu'], 'ffn2_w_down': out['ffn2_w_down'], 'loss_target': out['loss_target'], 'm_norm_ffn1': out['m_norm_ffn1'], 'm_ffn1_w_gu': out['m_ffn1_w_gu'], 'm_ffn1_w_down': out['m_ffn1_w_down'], 'm_norm_mix': out['m_norm_mix'], 'm_w_in_even': out['m_w_in_even'], 'm_dn_conv_w': out['m_dn_conv_w'], 'm_dn_a_log': out['m_dn_a_log'], 'm_dn_dt_bias': out['m_dn_dt_bias'], 'm_dn_norm_g': out['m_dn_norm_g'], 'm_fox_q_norm_g': out['m_fox_q_norm_g'], 'm_fox_k_norm_g': out['m_fox_k_norm_g'], 'm_fox_f_bias': out['m_fox_f_bias'], 'm_w_out_even': out['m_w_out_even'], 'm_w_in_odd': out['m_w_in_odd'], 'm_w_out_odd': out['m_w_out_odd'], 'm_norm_ffn2': out['m_norm_ffn2'], 'm_ffn2_w_gu': out['m_ffn2_w_gu'], 'm_ffn2_w_down': out['m_ffn2_w_down'], 'v_norm_ffn1': out['v_norm_ffn1'], 'v_ffn1_w_gu': out['v_ffn1_w_gu'], 'v_ffn1_w_down': out['v_ffn1_w_down'], 'v_norm_mix': out['v_norm_mix'], 'v_w_in_even': out['v_w_in_even'], 'v_dn_conv_w': out['v_dn_conv_w'], 'v_dn_a_log': out['v_dn_a_log'], 'v_dn_dt_bias': out['v_dn_dt_bias'], 'v_dn_norm_g': out['v_dn_norm_g'], 'v_fox_q_norm_g': out['v_fox_q_norm_g'], 'v_fox_k_norm_g': out['v_fox_k_norm_g'], 'v_fox_f_bias': out['v_fox_f_bias'], 'v_w_out_even': out['v_w_out_even'], 'v_w_in_odd': out['v_w_in_odd'], 'v_w_out_odd': out['v_w_out_odd'], 'v_norm_ffn2': out['v_norm_ffn2'], 'v_ffn2_w_gu': out['v_ffn2_w_gu'], 'v_ffn2_w_down': out['v_ffn2_w_down']}


def _loss(weights, diff, rest, loss_target):
    with _jax.named_scope("forward"):
        args = {**rest, TWIN_DIFF_INPUT: diff, **{k: w.astype(_WEIGHT_DTYPES[k]) for k, w in weights.items()}}
        y = _forward(args)
    with _jax.named_scope("loss_head"):
        err = _jnp.square(y.astype(_jnp.float32) - loss_target)
        return 0.5 * _jnp.sum(_jnp.mean(err, axis=-1)) if err.ndim else 0.5 * err


def _adamw(w, g, m, v):
    m = ADAM_B1 * m + (1.0 - ADAM_B1) * g
    v = ADAM_B2 * v + (1.0 - ADAM_B2) * _jnp.square(g)
    m_hat = m / (1.0 - ADAM_B1 ** ADAM_STEP)
    v_hat = v / (1.0 - ADAM_B2 ** ADAM_STEP)
    delta = -ADAM_LR * (m_hat / (_jnp.sqrt(v_hat) + ADAM_EPS) + ADAM_WD * w)
    return delta, m, v


def reference(x, norm_ffn1, ffn1_w_gu, ffn1_w_down, norm_mix, w_in_even, dn_conv_w, dn_a_log, dn_dt_bias, dn_norm_g, fox_q_norm_g, fox_k_norm_g, fox_f_bias, w_out_even, w_in_odd, w_out_odd, norm_ffn2, ffn2_w_gu, ffn2_w_down, loss_target, m_norm_ffn1, m_ffn1_w_gu, m_ffn1_w_down, m_norm_mix, m_w_in_even, m_dn_conv_w, m_dn_a_log, m_dn_dt_bias, m_dn_norm_g, m_fox_q_norm_g, m_fox_k_norm_g, m_fox_f_bias, m_w_out_even, m_w_in_odd, m_w_out_odd, m_norm_ffn2, m_ffn2_w_gu, m_ffn2_w_down, v_norm_ffn1, v_ffn1_w_gu, v_ffn1_w_down, v_norm_mix, v_w_in_even, v_dn_conv_w, v_dn_a_log, v_dn_dt_bias, v_dn_norm_g, v_fox_q_norm_g, v_fox_k_norm_g, v_fox_f_bias, v_w_out_even, v_w_in_odd, v_w_out_odd, v_norm_ffn2, v_ffn2_w_gu, v_ffn2_w_down):
    given = dict(x=x, norm_ffn1=norm_ffn1, ffn1_w_gu=ffn1_w_gu, ffn1_w_down=ffn1_w_down, norm_mix=norm_mix, w_in_even=w_in_even, dn_conv_w=dn_conv_w, dn_a_log=dn_a_log, dn_dt_bias=dn_dt_bias, dn_norm_g=dn_norm_g, fox_q_norm_g=fox_q_norm_g, fox_k_norm_g=fox_k_norm_g, fox_f_bias=fox_f_bias, w_out_even=w_out_even, w_in_odd=w_in_odd, w_out_odd=w_out_odd, norm_ffn2=norm_ffn2, ffn2_w_gu=ffn2_w_gu, ffn2_w_down=ffn2_w_down, loss_target=loss_target, m_norm_ffn1=m_norm_ffn1, m_ffn1_w_gu=m_ffn1_w_gu, m_ffn1_w_down=m_ffn1_w_down, m_norm_mix=m_norm_mix, m_w_in_even=m_w_in_even, m_dn_conv_w=m_dn_conv_w, m_dn_a_log=m_dn_a_log, m_dn_dt_bias=m_dn_dt_bias, m_dn_norm_g=m_dn_norm_g, m_fox_q_norm_g=m_fox_q_norm_g, m_fox_k_norm_g=m_fox_k_norm_g, m_fox_f_bias=m_fox_f_bias, m_w_out_even=m_w_out_even, m_w_in_odd=m_w_in_odd, m_w_out_odd=m_w_out_odd, m_norm_ffn2=m_norm_ffn2, m_ffn2_w_gu=m_ffn2_w_gu, m_ffn2_w_down=m_ffn2_w_down, v_norm_ffn1=v_norm_ffn1, v_ffn1_w_gu=v_ffn1_w_gu, v_ffn1_w_down=v_ffn1_w_down, v_norm_mix=v_norm_mix, v_w_in_even=v_w_in_even, v_dn_conv_w=v_dn_conv_w, v_dn_a_log=v_dn_a_log, v_dn_dt_bias=v_dn_dt_bias, v_dn_norm_g=v_dn_norm_g, v_fox_q_norm_g=v_fox_q_norm_g, v_fox_k_norm_g=v_fox_k_norm_g, v_fox_f_bias=v_fox_f_bias, v_w_out_even=v_w_out_even, v_w_in_odd=v_w_in_odd, v_w_out_odd=v_w_out_odd, v_norm_ffn2=v_norm_ffn2, v_ffn2_w_gu=v_ffn2_w_gu, v_ffn2_w_down=v_ffn2_w_down)
    weights = {n: given[n] for n in TWIN_WEIGHTS}
    shared = {n: given[n] for n in SHARED_INPUTS}
    per_example = {n: given[n] for n in ['x']}
    grad_fn = _jax.value_and_grad(_loss, argnums=(0, 1))

    def one_microbatch(ex, loss_target):
        ex = dict(ex)
        diff = ex.pop(TWIN_DIFF_INPUT)
        return grad_fn(weights, diff, {**shared, **ex}, loss_target)

    if N_MICROBATCH == 1:
        loss, (grad_w, grad_x) = one_microbatch(per_example, given["loss_target"])
    else:
        def body(carry, xs):
            loss_sum, grad_sum = carry
            l_k, (gw_k, gx_k) = one_microbatch(xs[0], xs[1])
            with _jax.named_scope("update"):
                return (loss_sum + l_k, _jax.tree.map(_jnp.add, grad_sum, gw_k)), gx_k

        init = (_jnp.zeros((), _jnp.float32), _jax.tree.map(_jnp.zeros_like, weights))
        (loss, grad_w), grad_x = _jax.lax.scan(body, init, (per_example, given["loss_target"]))
    with _jax.named_scope("update"):
        delta_w, new_m, new_v = {}, {}, {}
        for n in TWIN_WEIGHTS:
            delta_w[n], new_m[n], new_v[n] = _adamw(weights[n], grad_w[n], given["m_" + n], given["v_" + n])
    return (loss, grad_x, *[grad_w[n] for n in TWIN_WEIGHTS], *[delta_w[n] for n in TWIN_WEIGHTS],
            *[new_m[n] for n in TWIN_WEIGHTS], *[new_v[n] for n in TWIN_WEIGHTS])
```

```python
import functools
import math

import jax
import jax.numpy as jnp
from jax import lax
from jax.experimental import pallas as pl
from jax.experimental.pallas import tpu as pltpu

F32 = jnp.float32
BF16 = jnp.bfloat16
HI = lax.Precision.HIGHEST

HEAD_DIM = 128
N_DN_HEADS = 4
N_FOX_HEADS = 4
N_SB_HEADS = 8
D_DN = N_DN_HEADS * HEAD_DIM
D_FOX = N_FOX_HEADS * HEAD_DIM
CONV_WIDTH = 4
DN_CHUNK = 64
EPS = 1e-6
ATT_SCALE = HEAD_DIM ** -0.5
ADAM_LR, ADAM_B1, ADAM_B2, ADAM_EPS, ADAM_WD, ADAM_STEP = 0.001, 0.9, 0.999, 1e-08, 0.01, 10

V7X_VMEM_LIMIT = 56 * 1024 * 1024
LANES = 128
ATT_TQ = 256
ATT_TK = 128

LANE_BETA, LANE_DECAY, LANE_FORGET = 0, 4, 8


def _cparams(*sem):
    return pltpu.CompilerParams(dimension_semantics=sem, vmem_limit_bytes=V7X_VMEM_LIMIT)


def _sigmoid(x):
    return 1.0 / (1.0 + jnp.exp(-x))


def _softplus(x):
    return jnp.maximum(x, 0.0) + jnp.log(1.0 + jnp.exp(-jnp.abs(x)))


def _silu_grad(y, sg):
    return sg * (1.0 + y * (1.0 - sg))


def _rowwise(fn, rows, bcast, outs, sums, *, tile, name):
    rows = [r if isinstance(r, tuple) else (r, r.shape[1], 0) for r in rows]
    s = rows[0][0].shape[0]
    assert s % tile == 0
    n_in, n_b, n_out, n_sum = len(rows), len(bcast), len(outs), len(sums)

    def body(*refs):
        ins = [r[...] for r in refs[:n_in + n_b]]
        res = fn(*ins)
        if not isinstance(res, (tuple, list)):
            res = (res,)
        out_refs = refs[n_in + n_b:n_in + n_b + n_out]
        sum_refs = refs[n_in + n_b + n_out:]
        for o_ref, val in zip(out_refs, res[:n_out]):
            o_ref[...] = val.astype(o_ref.dtype)
        if n_sum:
            @pl.when(pl.program_id(0) == 0)
            def _():
                for s_ref in sum_refs:
                    s_ref[...] = jnp.zeros_like(s_ref)
            for s_ref, val in zip(sum_refs, res[n_out:]):
                s_ref[...] += val

    in_specs = [pl.BlockSpec((tile, w), lambda i, cb=cb: (i, cb)) for _, w, cb in rows]
    in_specs += [pl.BlockSpec(b.shape, lambda i, nd=b.ndim: (0,) * nd) for b in bcast]
    out_specs = [pl.BlockSpec((tile, c), lambda i: (i, 0)) for c, _ in outs]
    out_specs += [pl.BlockSpec(sh, lambda i: (0, 0)) for sh in sums]
    out_shape = [jax.ShapeDtypeStruct((s, c), dt) for c, dt in outs]
    out_shape += [jax.ShapeDtypeStruct(sh, F32) for sh in sums]
    return pl.pallas_call(
        body, name=name, grid=(s // tile,), in_specs=in_specs, out_specs=out_specs, out_shape=out_shape,
        compiler_params=_cparams("arbitrary" if n_sum else "parallel"),
    )(*[r[0] for r in rows], *bcast)


def _rms_fwd(x, gain, name):
    def fn(xb, g):
        r = lax.rsqrt(jnp.mean(xb * xb, axis=-1, keepdims=True) + EPS)
        return (xb * r * g,)
    return _rowwise(fn, [x], [gain], [(x.shape[1], BF16)], [], tile=512, name=name)[0]


def _rms_bwd(x, gain, dn, dres, name):
    def fn(xb, dnb, drb, g):
        r = lax.rsqrt(jnp.mean(xb * xb, axis=-1, keepdims=True) + EPS)
        xh = xb * r
        dxh = dnb * g
        dx = drb + r * (dxh - xh * jnp.mean(dxh * xh, axis=-1, keepdims=True))
        return dx, dx, jnp.sum(dnb * xh, axis=0, keepdims=True)
    d = x.shape[1]
    return _rowwise(fn, [x, dn, dres], [gain], [(d, F32), (d, BF16)], [(1, d)], tile=512, name=name)


_DIMS = {"nn": (((1,), (0,)), ((), ())), "nt": (((1,), (1,)), ((), ())), "tn": (((0,), (0,)), ((), ()))}


def _dot(a, b, kind):
    return lax.dot_general(a.astype(BF16), b.astype(BF16), _DIMS[kind], preferred_element_type=F32)


def _dot32(a, b, kind="nn"):
    return lax.dot_general(a, b, _DIMS[kind], precision=HI, preferred_element_type=F32)


def _mm(a, b, kind, *, tm, tn, out_dtype, name, scale=None, residual=None, a_lead=(), b_lead=(),
        b_spec=None, n=None, into=None):
    ash, bsh = a.shape[len(a_lead):], b.shape[len(b_lead):]
    m = ash[1] if kind == "tn" else ash[0]
    k = ash[0] if kind == "tn" else ash[1]
    if b_spec is None:
        n = bsh[0] if kind == "nt" else bsh[1]
        assert k == (bsh[1] if kind == "nt" else bsh[0]), (ash, bsh, kind)
    assert m % tm == 0 and n % tn == 0, (m, tm, n, tn)
    la, lb = (None,) * len(a_lead), (None,) * len(b_lead)
    if kind == "tn":
        a_spec = pl.BlockSpec(la + (k, tm), lambda i, j: a_lead + (0, i))
    else:
        a_spec = pl.BlockSpec(la + (tm, k), lambda i, j: a_lead + (i, 0))
    if b_spec is None:
        if kind == "nt":
            b_spec = pl.BlockSpec(lb + (tn, k), lambda i, j: b_lead + (j, 0))
        else:
            b_spec = pl.BlockSpec(lb + (k, tn), lambda i, j: b_lead + (0, j))
    in_specs, args = [a_spec, b_spec], [a, b]
    if residual is not None:
        in_specs.append(pl.BlockSpec((tm, tn), lambda i, j: (i, j)))
        args.append(residual)
    aliases = {}
    if into is not None:
        buf, layer = into
        in_specs.append(pl.BlockSpec(memory_space=pl.ANY))
        args.append(buf)
        aliases = {len(args) - 1: 0}
        out_spec = pl.BlockSpec((None, tm, tn), lambda i, j: (layer, i, j))
        out_shape = jax.ShapeDtypeStruct(buf.shape, buf.dtype)
    else:
        out_spec = pl.BlockSpec((tm, tn), lambda i, j: (i, j))
        out_shape = jax.ShapeDtypeStruct((m, n), out_dtype)

    def body(a_ref, b_ref, *rest):
        acc = _dot(a_ref[...], b_ref[...], kind)
        if scale is not None:
            acc = acc * scale
        if residual is not None:
            acc = acc + rest[0][...]
        rest[-1][...] = acc.astype(rest[-1].dtype)

    return pl.pallas_call(
        body, name=name, grid=(m // tm, n // tn), in_specs=in_specs, out_specs=out_spec, out_shape=out_shape,
        input_output_aliases=aliases, compiler_params=_cparams("parallel", "parallel"),
    )(*args)


def _ffn_up(n, w_gu, layer, name):
    s, d = n.shape
    f = w_gu.shape[2] // 2
    tm, tn = 512, f // 2
    nj = f // tn

    def body(n_ref, wg_ref, wu_ref, gu_ref, a_ref):
        nv = n_ref[...]
        g = _dot(nv, wg_ref[...], "nn")
        u = _dot(nv, wu_ref[...], "nn")
        gu_ref[0] = g.astype(BF16)
        gu_ref[1] = u.astype(BF16)
        a_ref[...] = (g * _sigmoid(g) * u).astype(BF16)

    return pl.pallas_call(
        body, name=name, grid=(s // tm, nj),
        in_specs=[pl.BlockSpec((tm, d), lambda i, j: (i, 0)),
                  pl.BlockSpec((None, d, tn), lambda i, j: (layer, 0, j)),
                  pl.BlockSpec((None, d, tn), lambda i, j: (layer, 0, j + nj))],
        out_specs=[pl.BlockSpec((2, tm, tn), lambda i, j: (0, i, j)),
                   pl.BlockSpec((tm, tn), lambda i, j: (i, j))],
        out_shape=[jax.ShapeDtypeStruct((2, s, f), BF16), jax.ShapeDtypeStruct((s, f), BF16)],
        compiler_params=_cparams("parallel", "parallel"),
    )(n, w_gu, w_gu)


def _ffn_down_bwd(dxo, w_down, gu, layer, name):
    s, d = dxo.shape
    f = w_down.shape[1]
    tm, tn = 512, f // 2

    def body(dx_ref, w_ref, gu_ref, dgu_ref):
        da = 0.5 * _dot(dx_ref[...], w_ref[...], "nt")
        g = gu_ref[0].astype(F32)
        u = gu_ref[1].astype(F32)
        sg = _sigmoid(g)
        dgu_ref[0] = (da * u * _silu_grad(g, sg)).astype(BF16)
        dgu_ref[1] = (da * g * sg).astype(BF16)

    return pl.pallas_call(
        body, name=name, grid=(s // tm, f // tn),
        in_specs=[pl.BlockSpec((tm, d), lambda i, j: (i, 0)),
                  pl.BlockSpec((None, tn, d), lambda i, j: (layer, j, 0)),
                  pl.BlockSpec((2, tm, tn), lambda i, j: (0, i, j))],
        out_specs=pl.BlockSpec((2, tm, tn), lambda i, j: (0, i, j)),
        out_shape=jax.ShapeDtypeStruct((2, s, f), BF16),
        compiler_params=_cparams("parallel", "parallel"),
    )(dxo, w_down, gu)


def _ffn_dn(dgu, w_gu, layer, name):
    _, s, f = dgu.shape
    d = w_gu.shape[1]
    tm, tn = 512, 512

    def body(dgu_ref, wg_ref, wu_ref, o_ref):
        o_ref[...] = _dot(dgu_ref[0], wg_ref[...], "nt") + _dot(dgu_ref[1], wu_ref[...], "nt")

    return pl.pallas_call(
        body, name=name, grid=(s // tm, d // tn),
        in_specs=[pl.BlockSpec((2, tm, f), lambda i, j: (0, i, 0)),
                  pl.BlockSpec((None, tn, f), lambda i, j: (layer, j, 0)),
                  pl.BlockSpec((None, tn, f), lambda i, j: (layer, j, 1))],
        out_specs=pl.BlockSpec((tm, tn), lambda i, j: (i, j)),
        out_shape=jax.ShapeDtypeStruct((s, d), F32),
        compiler_params=_cparams("parallel", "parallel"),
    )(dgu, w_gu, w_gu)


def _ffn_fwd(x, gain, w_gu, w_down, layer, tag):
    n = _rms_fwd(x, gain, f"{tag}_norm")
    gu, a = _ffn_up(n, w_gu, layer, f"{tag}_up")
    x2 = _mm(a, w_down, "nn", tm=512, tn=512, out_dtype=F32, name=f"{tag}_down", scale=0.5, residual=x,
             b_lead=(layer,))
    return x2, (x, n, gu, a)


def _ffn_bwd(dxo, dxo16, saved, gain, w_gu, w_down, layer, tag, g_gu, g_down):
    x, n, gu, a = saved
    s, f = a.shape
    dgu = _ffn_down_bwd(dxo16, w_down, gu, layer, f"{tag}_down_bwd")
    g_down = _mm(a, dxo16, "tn", tm=256, tn=dxo16.shape[1], out_dtype=F32, name=f"{tag}_down_dw", scale=0.5,
                 into=(g_down, layer))
    dn = _ffn_dn(dgu, w_gu, layer, f"{tag}_up_bwd")
    tn = f // 2
    nj = f // tn
    g_gu = _mm(n, dgu, "tn", tm=512, tn=tn, out_dtype=F32, name=f"{tag}_up_dw", into=(g_gu, layer), n=2 * f,
               b_spec=pl.BlockSpec((None, s, tn), lambda i, j: (j // nj, 0, j % nj)))
    dx, dx16, dgain = _rms_bwd(x, gain, dn, dxo, f"{tag}_norm_bwd")
    return dx, dx16, dgain, g_gu, g_down


def _lane_col(blk, lane):
    li = lax.broadcasted_iota(jnp.int32, blk.shape, 1)
    return jnp.sum(jnp.where(li == lane, blk, 0.0), axis=1, keepdims=True)


def _split_dot(x, tri):
    hi = x.astype(BF16)
    lo = (x - hi.astype(F32)).astype(BF16)
    return (lax.dot_general(hi, tri, _DIMS["nn"], preferred_element_type=F32)
            + lax.dot_general(lo, tri, _DIMS["nn"], preferred_element_type=F32))


def _att_specs(n_heads, s):
    q_spec = pl.BlockSpec((ATT_TQ, HEAD_DIM), lambda h, i: (i, h))
    k_spec = pl.BlockSpec((s, HEAD_DIM), lambda h, i: (0, n_heads + h))
    v_spec = pl.BlockSpec((s, HEAD_DIM), lambda h, i: (0, 2 * n_heads + h))
    return q_spec, k_spec, v_spec


def _att_iotas(i):
    row = i * ATT_TQ + lax.broadcasted_iota(jnp.int32, (ATT_TQ, ATT_TK), 0)
    col = lax.broadcasted_iota(jnp.int32, (ATT_TQ, ATT_TK), 1)
    jr = lax.broadcasted_iota(jnp.int32, (ATT_TK, ATT_TK), 0)
    jc = lax.broadcasted_iota(jnp.int32, (ATT_TK, ATT_TK), 1)
    return row, col, jr, jc


def _sb_fwd(qkv, n_heads, name):
    s = qkv.shape[0]
    ratio = ATT_TQ // ATT_TK

    def body(q_ref, k_ref, v_ref, o16_ref, o32_ref):
        i = pl.program_id(1)
        q = q_ref[...]
        row, col, jr, jc = _att_iotas(i)
        later = (jr > jc).astype(BF16)

        def step(it, carry):
            c_sp, acc = carry
            j = (i + 1) * ratio - 1 - it
            off = pl.multiple_of(j * ATT_TK, ATT_TK)
            k = k_ref[pl.ds(off, ATT_TK), :]
            v = v_ref[pl.ds(off, ATT_TK), :]
            z = _dot(q, k, "nt") * ATT_SCALE
            before = (col + j * ATT_TK) < row
            sp = _softplus(z)
            spm = jnp.where(before, sp, 0.0)
            tail = c_sp + _split_dot(spm, later)
            a = jnp.where(before, jnp.exp(z - sp - tail), 0.0)
            return c_sp + jnp.sum(spm, axis=1, keepdims=True), acc + _split_dot(a, v)

        init = (jnp.zeros((ATT_TQ, 1), F32), jnp.zeros((ATT_TQ, HEAD_DIM), F32))
        _, acc = lax.fori_loop(0, (i + 1) * ratio, step, init)
        o16_ref[...] = acc.astype(BF16)
        o32_ref[...] = acc

    q_spec, k_spec, v_spec = _att_specs(n_heads, s)
    o_spec = pl.BlockSpec((ATT_TQ, HEAD_DIM), lambda h, i: (i, h))
    return pl.pallas_call(
        body, name=name, grid=(n_heads, s // ATT_TQ), in_specs=[q_spec, k_spec, v_spec],
        out_specs=[o_spec, o_spec],
        out_shape=[jax.ShapeDtypeStruct((s, n_heads * HEAD_DIM), BF16),
                   jax.ShapeDtypeStruct((s, n_heads * HEAD_DIM), F32)],
        compiler_params=_cparams("parallel", "arbitrary"),
    )(qkv, qkv, qkv)


def _sb_bwd(qkv, o32, do, n_heads, name):
    s = qkv.shape[0]
    ratio = ATT_TQ // ATT_TK

    def body(q_ref, k_ref, v_ref, o_ref, do_ref, dq_ref, dk_ref, dv_ref):
        i = pl.program_id(1)

        @pl.when(i == 0)
        def _():
            dk_ref[...] = jnp.zeros_like(dk_ref)
            dv_ref[...] = jnp.zeros_like(dv_ref)

        q = q_ref[...]
        do = do_ref[...]
        total = jnp.sum(do.astype(F32) * o_ref[...], axis=1, keepdims=True)
        row, col, jr, jc = _att_iotas(i)
        later = (jr > jc).astype(BF16)
        not_before = (jr >= jc).astype(BF16)

        def step(it, carry):
            c_sp, c_e, dq = carry
            j = (i + 1) * ratio - 1 - it
            off = pl.multiple_of(j * ATT_TK, ATT_TK)
            k = k_ref[pl.ds(off, ATT_TK), :]
            v = v_ref[pl.ds(off, ATT_TK), :]
            z = _dot(q, k, "nt") * ATT_SCALE
            before = (col + j * ATT_TK) < row
            sp = _softplus(z)
            spm = jnp.where(before, sp, 0.0)
            tail = c_sp + _split_dot(spm, later)
            sig = jnp.exp(z - sp)
            a = jnp.where(before, sig * jnp.exp(-tail), 0.0)
            e = a * _dot(do, v, "nt")
            left = total - c_e - _split_dot(e, not_before)
            dz = jnp.where(before, e * (1.0 - sig) - left * sig, 0.0) * ATT_SCALE
            dk_ref[pl.ds(off, ATT_TK), :] += _dot(dz, q, "tn")
            dv_ref[pl.ds(off, ATT_TK), :] += _dot(a, do, "tn")
            return (c_sp + jnp.sum(spm, axis=1, keepdims=True), c_e + jnp.sum(e, axis=1, keepdims=True),
                    dq + _dot(dz, k, "nn"))

        zero = jnp.zeros((ATT_TQ, 1), F32)
        _, _, dq = lax.fori_loop(0, (i + 1) * ratio, step, (zero, zero, jnp.zeros((ATT_TQ, HEAD_DIM), F32)))
        dq_ref[...] = dq.astype(BF16)

    q_spec, k_spec, v_spec = _att_specs(n_heads, s)
    blk = pl.BlockSpec((ATT_TQ, HEAD_DIM), lambda h, i: (i, h))
    full = pl.BlockSpec((s, HEAD_DIM), lambda h, i: (0, h))
    wide = (s, n_heads * HEAD_DIM)
    return pl.pallas_call(
        body, name=name, grid=(n_heads, s // ATT_TQ), in_specs=[q_spec, k_spec, v_spec, blk, blk],
        out_specs=[blk, full, full],
        out_shape=[jax.ShapeDtypeStruct(wide, BF16), jax.ShapeDtypeStruct(wide, F32), jax.ShapeDtypeStruct(wide, F32)],
        compiler_params=_cparams("parallel", "arbitrary"),
    )(qkv, qkv, qkv, o32, do)


def _fox_logits(q, k, cq, ck_ref, off, row, col, j):
    sc = _dot(q, k, "nt") * ATT_SCALE + (cq - ck_ref[:, pl.ds(off, ATT_TK)])
    valid = (col + j * ATT_TK) <= row
    return sc, valid


def _fox_fwd(qkv, c, ct, name):
    s = qkv.shape[0]
    ratio = ATT_TQ // ATT_TK
    n_heads = N_FOX_HEADS

    def body(q_ref, k_ref, v_ref, c_ref, ct_ref, o_ref, lse_ref):
        h, i = pl.program_id(0), pl.program_id(1)
        q = q_ref[...]
        cq = _lane_col(c_ref[...], LANE_FORGET + h)
        row, col, _, _ = _att_iotas(i)

        def step(j, carry):
            m, l, acc = carry
            off = pl.multiple_of(j * ATT_TK, ATT_TK)
            k = k_ref[pl.ds(off, ATT_TK), :]
            v = v_ref[pl.ds(off, ATT_TK), :]
            sc, valid = _fox_logits(q, k, cq, ct_ref, off, row, col, j)
            sc = jnp.where(valid, sc, -1e30)
            m_new = jnp.maximum(m, jnp.max(sc, axis=1, keepdims=True))
            p = jnp.where(valid, jnp.exp(sc - m_new), 0.0)
            w = jnp.exp(m - m_new)
            return m_new, l * w + jnp.sum(p, axis=1, keepdims=True), acc * w + _split_dot(p, v)

        init = (jnp.full((ATT_TQ, 1), -1e30, F32), jnp.zeros((ATT_TQ, 1), F32), jnp.zeros((ATT_TQ, HEAD_DIM), F32))
        m, l, acc = lax.fori_loop(0, (i + 1) * ratio, step, init)
        o_ref[...] = acc / l
        lse_ref[...] = jnp.broadcast_to(m + jnp.log(l), (ATT_TQ, LANES))

    q_spec, k_spec, v_spec = _att_specs(n_heads, s)
    return pl.pallas_call(
        body, name=name, grid=(n_heads, s // ATT_TQ),
        in_specs=[q_spec, k_spec, v_spec, pl.BlockSpec((ATT_TQ, LANES), lambda h, i: (i, 0)),
                  pl.BlockSpec((None, 1, s), lambda h, i: (h, 0, 0))],
        out_specs=[pl.BlockSpec((ATT_TQ, HEAD_DIM), lambda h, i: (i, h)),
                   pl.BlockSpec((None, ATT_TQ, LANES), lambda h, i: (h, i, 0))],
        out_shape=[jax.ShapeDtypeStruct((s, n_heads * HEAD_DIM), F32),
                   jax.ShapeDtypeStruct((n_heads, s, LANES), F32)],
        compiler_params=_cparams("parallel", "arbitrary"),
    )(qkv, qkv, qkv, c, ct)


def _fox_bwd(qkv, c, ct, o, lse, do, name):
    s = qkv.shape[0]
    ratio = ATT_TQ // ATT_TK
    n_heads = N_FOX_HEADS

    def body(q_ref, k_ref, v_ref, c_ref, ct_ref, o_ref, lse_ref, do_ref, dq_ref, dk_ref, dv_ref, dct_ref):
        h, i = pl.program_id(0), pl.program_id(1)

        @pl.when(i == 0)
        def _():
            dk_ref[...] = jnp.zeros_like(dk_ref)
            dv_ref[...] = jnp.zeros_like(dv_ref)
            dct_ref[...] = jnp.zeros_like(dct_ref)

        q = q_ref[...]
        do = do_ref[...]
        do16 = do.astype(BF16)
        delta = jnp.sum(do16.astype(F32) * o_ref[...], axis=1, keepdims=True)
        lse_col = lse_ref[:, 0:1]
        cq = _lane_col(c_ref[...], LANE_FORGET + h)
        row, col, _, _ = _att_iotas(i)

        def step(j, dq):
            off = pl.multiple_of(j * ATT_TK, ATT_TK)
            k = k_ref[pl.ds(off, ATT_TK), :]
            v = v_ref[pl.ds(off, ATT_TK), :]
            sc, valid = _fox_logits(q, k, cq, ct_ref, off, row, col, j)
            p = jnp.where(valid, jnp.exp(jnp.where(valid, sc, 0.0) - lse_col), 0.0)
            ds = p * (_dot(do16, v, "nt") - delta)
            dct_ref[:, pl.ds(off, ATT_TK)] -= jnp.sum(ds, axis=0, keepdims=True)
            dss = ds * ATT_SCALE
            dk_ref[pl.ds(off, ATT_TK), :] += _dot(dss, q, "tn")
            dv_ref[pl.ds(off, ATT_TK), :] += _dot(p, do16, "tn")
            return dq + _dot(dss, k, "nn")

        dq_ref[...] = lax.fori_loop(0, (i + 1) * ratio, step, jnp.zeros((ATT_TQ, HEAD_DIM), F32))

    q_spec, k_spec, v_spec = _att_specs(n_heads, s)
    blk = pl.BlockSpec((ATT_TQ, HEAD_DIM), lambda h, i: (i, h))
    full = pl.BlockSpec((s, HEAD_DIM), lambda h, i: (0, h))
    wide = jax.ShapeDtypeStruct((s, n_heads * HEAD_DIM), F32)
    return pl.pallas_call(
        body, name=name, grid=(n_heads, s // ATT_TQ),
        in_specs=[q_spec, k_spec, v_spec, pl.BlockSpec((ATT_TQ, LANES), lambda h, i: (i, 0)),
                  pl.BlockSpec((None, 1, s), lambda h, i: (h, 0, 0)), blk,
                  pl.BlockSpec((None, ATT_TQ, LANES), lambda h, i: (h, i, 0)), blk],
        out_specs=[blk, full, full, pl.BlockSpec((None, 1, s), lambda h, i: (h, 0, 0))],
        out_shape=[wide, wide, wide, jax.ShapeDtypeStruct((n_heads, 1, s), F32)],
        compiler_params=_cparams("parallel", "arbitrary"),
    )(qkv, qkv, qkv, c, ct, o, lse, do)


def _cumsum_rows(x, reverse, name):
    s = x.shape[0]
    nb = s // LANES

    def body(x_ref, o_ref):
        r = lax.broadcasted_iota(jnp.int32, (LANES, LANES), 0)
        c = lax.broadcasted_iota(jnp.int32, (LANES, LANES), 1)
        tri = ((r <= c) if reverse else (r >= c)).astype(F32)

        def step(it, carry):
            b = (nb - 1 - it) if reverse else it
            off = pl.multiple_of(b * LANES, LANES)
            blk = x_ref[pl.ds(off, LANES), :]
            o_ref[pl.ds(off, LANES), :] = _dot32(tri, blk) + carry
            return carry + jnp.sum(blk, axis=0, keepdims=True)

        lax.fori_loop(0, nb, step, jnp.zeros((1, LANES), F32))

    return pl.pallas_call(body, name=name, out_shape=jax.ShapeDtypeStruct(x.shape, F32),
                          compiler_params=pltpu.CompilerParams(vmem_limit_bytes=V7X_VMEM_LIMIT))(x)


def _unit_lower_inverse(m, ri, ci):
    c = m.shape[0]
    t = jnp.where(ri == ci, 1.0, 0.0) - jnp.where(ri // 2 == ci // 2, m, 0.0)
    b = 4
    while b <= c:
        off_diag = (ri // b == ci // b) & (ri % b >= b // 2) & (ci % b < b // 2)
        t = t - _dot32(_dot32(t, jnp.where(off_diag, m, 0.0)), t)
        b *= 2
    return t


def _dn_gates(g, ri, ci):
    eye = ri == ci
    incl = ri >= ci
    g_row = jnp.sum(jnp.where(eye, g, 0.0), axis=0, keepdims=True)
    gc = jnp.sum(jnp.where(incl, g_row, 0.0), axis=1, keepdims=True)
    gc_row = jnp.sum(jnp.where(eye, gc, 0.0), axis=0, keepdims=True)
    dmat = jnp.where(incl, jnp.exp(jnp.where(incl, gc - gc_row, 0.0)), 0.0)
    gc_last = jnp.sum(g, axis=0, keepdims=True)
    return gc, dmat, jnp.exp(gc), jnp.exp(gc_last - gc), jnp.exp(gc_last)


def _dn_fwd(qkv, act, name):
    s = qkv.shape[0]
    c, d, nh = DN_CHUNK, HEAD_DIM, N_DN_HEADS
    nc = s // c

    def body(q_ref, k_ref, v_ref, act_ref, o_ref, s_ref, t_ref, state):
        n, h = pl.program_id(0), pl.program_id(1)

        @pl.when(n == 0)
        def _():
            state[h] = jnp.zeros((d, d), F32)

        q, k, v = q_ref[...], k_ref[...], v_ref[...]
        ri = lax.broadcasted_iota(jnp.int32, (c, c), 0)
        ci = lax.broadcasted_iota(jnp.int32, (c, c), 1)
        beta = _lane_col(act_ref[...], LANE_BETA + h)
        _, dmat, e, r, gl = _dn_gates(_lane_col(act_ref[...], LANE_DECAY + h), ri, ci)
        s0 = state[h]
        kb = beta * k
        m = jnp.where(ri > ci, _dot32(kb, k, "nt") * dmat, 0.0)
        t = _unit_lower_inverse(m, ri, ci)
        vn = _dot32(t, beta * v) - _dot32(_dot32(t, kb * e), s0)
        o_ref[...] = _dot32(q * e, s0) + _dot32(_dot32(q, k, "nt") * dmat, vn)
        state[h] = gl * s0 + _dot32(k * r, vn, "tn")
        s_ref[...] = s0
        t_ref[...] = t

    col = lambda off: pl.BlockSpec((c, d), lambda n, h: (n, off + h))
    return pl.pallas_call(
        body, name=name, grid=(nc, nh),
        in_specs=[col(0), col(nh), col(2 * nh), pl.BlockSpec((c, LANES), lambda n, h: (n, 0))],
        out_specs=[col(0), pl.BlockSpec((None, None, d, d), lambda n, h: (h, n, 0, 0)),
                   pl.BlockSpec((None, None, c, c), lambda n, h: (h, n, 0, 0))],
        out_shape=[jax.ShapeDtypeStruct((s, nh * d), F32), jax.ShapeDtypeStruct((nh, nc, d, d), F32),
                   jax.ShapeDtypeStruct((nh, nc, c, c), F32)],
        scratch_shapes=[pltpu.VMEM((nh, d, d), F32)],
        compiler_params=_cparams("arbitrary", "arbitrary"),
    )(qkv, qkv, qkv, act)


def _dn_bwd(qkv, act, states, tinv, do, name):
    s = qkv.shape[0]
    c, d, nh = DN_CHUNK, HEAD_DIM, N_DN_HEADS
    nc = s // c

    def body(q_ref, k_ref, v_ref, act_ref, s_ref, t_ref, do_ref, dq_ref, dk_ref, dv_ref, dact_ref, dstate):
        n, h = pl.program_id(0), pl.program_id(1)

        @pl.when(n == 0)
        def _():
            dstate[h] = jnp.zeros((d, d), F32)

        @pl.when(h == 0)
        def _():
            dact_ref[...] = jnp.zeros_like(dact_ref)

        q, k, v, do = q_ref[...], k_ref[...], v_ref[...], do_ref[...]
        s0, t = s_ref[...], t_ref[...]
        ri = lax.broadcasted_iota(jnp.int32, (c, c), 0)
        ci = lax.broadcasted_iota(jnp.int32, (c, c), 1)
        eye, incl, strict = ri == ci, ri >= ci, ri > ci
        beta = _lane_col(act_ref[...], LANE_BETA + h)
        gc, dmat, e, r, gl = _dn_gates(_lane_col(act_ref[...], LANE_DECAY + h), ri, ci)
        rowsum = lambda x: jnp.sum(x, axis=1, keepdims=True)
        to_col = lambda row: jnp.sum(jnp.where(eye, row, 0.0), axis=1, keepdims=True)
        to_row = lambda colv: jnp.sum(jnp.where(eye, colv, 0.0), axis=0, keepdims=True)

        kb, vb = beta * k, beta * v
        kbe = kb * e
        u, w = _dot32(t, vb), _dot32(t, kbe)
        vn = u - _dot32(w, s0)
        qk = _dot32(q, k, "nt")
        p = qk * dmat
        gram = _dot32(k, k, "nt")
        kr, qe = k * r, q * e

        ds_out = dstate[h]
        d_kr = _dot32(vn, ds_out, "nt")
        dvn = _dot32(kr, ds_out)
        total = lambda x: jnp.sum(rowsum(x), axis=0, keepdims=True)
        dgl = total(s0 * ds_out)
        ds_in = gl * ds_out
        dk = d_kr * r
        dr = rowsum(d_kr * k)
        d_qe = _dot32(do, s0, "nt")
        ds_in = ds_in + _dot32(qe, do, "tn")
        dp = jnp.where(incl, _dot32(do, vn, "nt"), 0.0)
        dvn = dvn + _dot32(p, do, "tn")
        dq = d_qe * e
        de = rowsum(d_qe * q)
        dqk = dp * dmat
        dq = dq + _dot32(dqk, k)
        dk = dk + _dot32(dqk, q, "tn")
        dd = dp * qk
        dw = -_dot32(dvn, s0, "nt")
        ds_in = ds_in - _dot32(w, dvn, "tn")
        dvb = _dot32(t, dvn, "tn")
        dkbe = _dot32(t, dw, "tn")
        dm = -jnp.where(strict, _dot32(dvb, u, "nt") + _dot32(dkbe, w, "nt"), 0.0)
        dbeta = rowsum(dm * gram * dmat)
        dgram = dm * beta * dmat
        dd = dd + dm * beta * gram
        dk = dk + _dot32(dgram, k) + _dot32(dgram, k, "tn")
        dkb = dkbe * e
        de = de + rowsum(dkbe * kb)
        dk = dk + beta * dkb
        dbeta = dbeta + rowsum(dkb * k) + rowsum(dvb * v)
        dv = beta * dvb
        wd = dd * dmat
        dgc = rowsum(wd) - to_col(jnp.sum(wd, axis=0, keepdims=True)) + de * e - dr * r
        dgc_last = total(dr * r) + dgl * gl
        dgc = dgc + jnp.where(ri[:, 0:1] == c - 1, dgc_last, 0.0)
        dg = jnp.sum(jnp.where(ri <= ci, to_row(dgc), 0.0), axis=1, keepdims=True)

        dstate[h] = ds_in
        dq_ref[...], dk_ref[...], dv_ref[...] = dq, dk, dv
        lane = lax.broadcasted_iota(jnp.int32, (c, LANES), 1)
        dact_ref[...] += jnp.where(lane == LANE_BETA + h, dbeta, 0.0) + jnp.where(lane == LANE_DECAY + h, dg, 0.0)

    col = lambda off: pl.BlockSpec((c, d), lambda n, h: (nc - 1 - n, off + h))
    per = lambda a, b: pl.BlockSpec((None, None, a, b), lambda n, h: (h, nc - 1 - n, 0, 0))
    wide = jax.ShapeDtypeStruct((s, nh * d), F32)
    act_spec = pl.BlockSpec((c, LANES), lambda n, h: (nc - 1 - n, 0))
    return pl.pallas_call(
        body, name=name, grid=(nc, nh),
        in_specs=[col(0), col(nh), col(2 * nh), act_spec, per(d, d), per(c, c), col(0)],
        out_specs=[col(0), col(0), col(0), act_spec],
        out_shape=[wide, wide, wide, jax.ShapeDtypeStruct((s, LANES), F32)],
        scratch_shapes=[pltpu.VMEM((nh, d, d), F32)],
        compiler_params=_cparams("arbitrary", "arbitrary"),
    )(qkv, qkv, qkv, act, states, tinv, do)


EVEN_DN_QKV, EVEN_FOX_QKV, EVEN_DN_GATE, EVEN_FOX_GATE, EVEN_NARROW = 0, 1536, 3072, 3584, 4096
EVEN_WIDTH = 4224
CONV_TILE = 256
CONV_HALO = 8


def _conv_fwd(proj, w, name):
    s = proj.shape[0]
    t, cw = CONV_TILE, 3 * D_DN

    def body(cur_ref, prev_ref, w_ref, y_ref, xs):
        i = pl.program_id(0)
        xs[0:CONV_HALO, :] = jnp.where(i > 0, prev_ref[...], 0.0)
        xs[CONV_HALO:, :] = cur_ref[...]
        y = jnp.zeros((t, cw), F32)
        for tap in range(CONV_WIDTH):
            y = y + w_ref[tap:tap + 1, :] * xs[pl.ds(CONV_HALO - CONV_WIDTH + 1 + tap, t), :]
        y_ref[...] = y

    per = t // CONV_HALO
    return pl.pallas_call(
        body, name=name, grid=(s // t,),
        in_specs=[pl.BlockSpec((t, cw), lambda i: (i, 0)),
                  pl.BlockSpec((CONV_HALO, cw), lambda i: (jnp.maximum(i * per - 1, 0), 0)),
                  pl.BlockSpec((CONV_WIDTH, cw), lambda i: (0, 0))],
        out_specs=pl.BlockSpec((t, cw), lambda i: (i, 0)),
        out_shape=jax.ShapeDtypeStruct((s, cw), F32),
        scratch_shapes=[pltpu.VMEM((t + CONV_HALO, cw), F32)],
        compiler_params=_cparams("parallel"),
    )(proj, proj, w)


def _conv_bwd(proj, w, dy, name):
    s = proj.shape[0]
    t, cw = CONV_TILE, 3 * D_DN
    nt = s // t

    def body(cur_ref, prev_ref, w_ref, dy_ref, nxt_ref, dx_ref, dw_ref, xs, dys):
        i = pl.program_id(0)

        @pl.when(i == 0)
        def _():
            dw_ref[...] = jnp.zeros_like(dw_ref)

        xs[0:CONV_HALO, :] = jnp.where(i > 0, prev_ref[...], 0.0)
        xs[CONV_HALO:, :] = cur_ref[...]
        dys[0:t, :] = dy_ref[...]
        dys[t:, :] = jnp.where(i < nt - 1, nxt_ref[...], 0.0)
        dy = dy_ref[...]
        dx = jnp.zeros((t, cw), F32)
        for tap in range(CONV_WIDTH):
            dx = dx + w_ref[tap:tap + 1, :] * dys[pl.ds(CONV_WIDTH - 1 - tap, t), :]
            dw_ref[tap:tap + 1, :] += jnp.sum(dy * xs[pl.ds(CONV_HALO - CONV_WIDTH + 1 + tap, t), :], axis=0,
                                              keepdims=True)
        dx_ref[...] = dx.astype(BF16)

    per = t // CONV_HALO
    last = s // CONV_HALO - 1
    return pl.pallas_call(
        body, name=name, grid=(nt,),
        in_specs=[pl.BlockSpec((t, cw), lambda i: (i, 0)),
                  pl.BlockSpec((CONV_HALO, cw), lambda i: (jnp.maximum(i * per - 1, 0), 0)),
                  pl.BlockSpec((CONV_WIDTH, cw), lambda i: (0, 0)),
                  pl.BlockSpec((t, cw), lambda i: (i, 0)),
                  pl.BlockSpec((CONV_HALO, cw), lambda i: (jnp.minimum((i + 1) * per, last), 0))],
        out_specs=[pl.BlockSpec((t, cw), lambda i: (i, 0)), pl.BlockSpec((CONV_WIDTH, cw), lambda i: (0, 0))],
        out_shape=[jax.ShapeDtypeStruct((s, cw), BF16), jax.ShapeDtypeStruct((CONV_WIDTH, cw), F32)],
        scratch_shapes=[pltpu.VMEM((t + CONV_HALO, cw), F32), pltpu.VMEM((t + CONV_HALO, cw), F32)],
        compiler_params=_cparams("arbitrary"),
    )(proj, proj, w, dy, dy)


def _heads(x, n):
    return [x[:, HEAD_DIM * h:HEAD_DIM * (h + 1)] for h in range(n)]


def _dn_pre_fwd(y, name):
    def fn(yb):
        cs = yb * _sigmoid(yb)
        out = []
        for idx, xh in enumerate(_heads(cs, 3 * N_DN_HEADS)):
            if idx < 2 * N_DN_HEADS:
                xh = xh * lax.rsqrt(jnp.sum(xh * xh, axis=-1, keepdims=True) + EPS)
                if idx < N_DN_HEADS:
                    xh = xh * ATT_SCALE
            out.append(xh)
        return (jnp.concatenate(out, axis=1),)
    return _rowwise(fn, [y], [], [(y.shape[1], F32)], [], tile=256, name=name)[0]


def _dn_pre_bwd(y, dq, dk, dv, name):
    def fn(yb, dqb, dkb, dvb):
        sg = _sigmoid(yb)
        cs = yb * sg
        dout = _heads(dqb, N_DN_HEADS) + _heads(dkb, N_DN_HEADS) + _heads(dvb, N_DN_HEADS)
        dcs = []
        for idx, (xh, dh) in enumerate(zip(_heads(cs, 3 * N_DN_HEADS), dout)):
            if idx < 2 * N_DN_HEADS:
                if idx < N_DN_HEADS:
                    dh = dh * ATT_SCALE
                r = lax.rsqrt(jnp.sum(xh * xh, axis=-1, keepdims=True) + EPS)
                xhat = xh * r
                dh = r * (dh - xhat * jnp.sum(xhat * dh, axis=-1, keepdims=True))
            dcs.append(dh)
        return (jnp.concatenate(dcs, axis=1) * _silu_grad(yb, sg),)
    return _rowwise(fn, [y, dq, dk, dv], [], [(y.shape[1], F32)], [], tile=256, name=name)[0]


def _narrow_params(a_log, dt_bias, f_bias):
    lanes = lambda a, first: jnp.pad(a.reshape(1, -1), ((0, 0), (first, LANES - first - a.shape[0])))
    return jnp.concatenate([lanes(a_log, LANE_DECAY), lanes(dt_bias, LANE_DECAY), lanes(f_bias, LANE_FORGET),
                            jnp.zeros((5, LANES), F32)], axis=0)


def _narrow_masks(shape):
    lane = lax.broadcasted_iota(jnp.int32, shape, 1)
    is_beta = lane < LANE_DECAY
    is_decay = (lane >= LANE_DECAY) & (lane < LANE_FORGET)
    is_forget = (lane >= LANE_FORGET) & (lane < LANE_FORGET + N_FOX_HEADS)
    return is_beta, is_decay, is_forget


def _narrow_fwd(proj, params, name):
    def fn(sm, pk):
        is_beta, is_decay, is_forget = _narrow_masks(sm.shape)
        g = -jnp.exp(pk[0:1, :]) * _softplus(sm + pk[1:2, :])
        logf = -_softplus(-(sm + pk[2:3, :]))
        return (jnp.where(is_beta, _sigmoid(sm), jnp.where(is_decay, g, jnp.where(is_forget, logf, 0.0))),)
    return _rowwise(fn, [(proj, LANES, EVEN_NARROW // LANES)], [params], [(LANES, F32)], [], tile=512, name=name)[0]


def _narrow_bwd(proj, params, act, dact, dlogf, name):
    def fn(sm, ab, da, dl, pk):
        is_beta, is_decay, is_forget = _narrow_masks(sm.shape)
        db = jnp.where(is_forget, dl, da)
        d_beta = db * ab * (1.0 - ab)
        d_decay = db * (-jnp.exp(pk[0:1, :])) * _sigmoid(sm + pk[1:2, :])
        d_forget = db * _sigmoid(-(sm + pk[2:3, :]))
        dsm = jnp.where(is_beta, d_beta, jnp.where(is_decay, d_decay, jnp.where(is_forget, d_forget, 0.0)))
        col = lambda x: jnp.sum(x, axis=0, keepdims=True)
        return (dsm, col(jnp.where(is_decay, db * ab, 0.0)), col(jnp.where(is_decay, dsm, 0.0)),
                col(jnp.where(is_forget, dsm, 0.0)))
    return _rowwise(fn, [(proj, LANES, EVEN_NARROW // LANES), act, dact, dlogf], [params], [(LANES, BF16)],
                    [(1, LANES)] * 3, tile=512, name=name)


def _head_rms(xh):
    r = lax.rsqrt(jnp.mean(xh * xh, axis=-1, keepdims=True) + EPS)
    return xh * r, r


def _fox_pre_fwd(proj, qg, kg, name):
    def fn(pf, qgb, kgb):
        out = []
        for idx, xh in enumerate(_heads(pf, 3 * N_FOX_HEADS)):
            if idx < 2 * N_FOX_HEADS:
                xh = _head_rms(xh)[0] * (qgb if idx < N_FOX_HEADS else kgb)
            out.append(xh)
        return (jnp.concatenate(out, axis=1),)
    return _rowwise(fn, [(proj, 3 * D_FOX, EVEN_FOX_QKV // (3 * D_FOX))], [qg, kg], [(3 * D_FOX, BF16)], [],
                    tile=256, name=name)[0]


def _fox_pre_bwd(proj, qg, kg, dq, dk, dv, name):
    def fn(pf, dqb, dkb, dvb, qgb, kgb):
        dout = _heads(dqb, N_FOX_HEADS) + _heads(dkb, N_FOX_HEADS) + _heads(dvb, N_FOX_HEADS)
        dg = [jnp.zeros((1, HEAD_DIM), F32), jnp.zeros((1, HEAD_DIM), F32)]
        dx = []
        for idx, (xh, dh) in enumerate(zip(_heads(pf, 3 * N_FOX_HEADS), dout)):
            if idx < 2 * N_FOX_HEADS:
                which = 0 if idx < N_FOX_HEADS else 1
                xhat, r = _head_rms(xh)
                dg[which] = dg[which] + jnp.sum(dh * xhat, axis=0, keepdims=True)
                dxh = dh * (qgb if which == 0 else kgb)
                dh = r * (dxh - xhat * jnp.mean(dxh * xhat, axis=-1, keepdims=True))
            dx.append(dh)
        return jnp.concatenate(dx, axis=1), dg[0], dg[1]
    return _rowwise(fn, [(proj, 3 * D_FOX, EVEN_FOX_QKV // (3 * D_FOX)), dq, dk, dv], [qg, kg],
                    [(3 * D_FOX, BF16)], [(1, HEAD_DIM)] * 2, tile=256, name=name)


def _mix_gate_fwd(proj, o_dn, o_fox, ng, name):
    def fn(gd, gf, od, of, ngb):
        dn = [_head_rms(xh)[0] * ngb for xh in _heads(od, N_DN_HEADS)]
        return (jnp.concatenate([jnp.concatenate(dn, axis=1) * gd * _sigmoid(gd), of * _sigmoid(gf)], axis=1),)
    return _rowwise(fn, [(proj, D_DN, EVEN_DN_GATE // D_DN), (proj, D_FOX, EVEN_FOX_GATE // D_FOX), o_dn, o_fox],
                    [ng], [(D_DN + D_FOX, BF16)], [], tile=256, name=name)[0]


def _mix_gate_bwd(proj, o_dn, o_fox, ng, dom, name):
    def fn(gd, gf, od, of, dm, ngb):
        d_dn, d_fox = dm[:, :D_DN], dm[:, D_DN:]
        sgd, sgf = _sigmoid(gd), _sigmoid(gf)
        don = d_dn * gd * sgd
        dng = jnp.zeros((1, HEAD_DIM), F32)
        dod, normed = [], []
        for xh, dh in zip(_heads(od, N_DN_HEADS), _heads(don, N_DN_HEADS)):
            xhat, r = _head_rms(xh)
            dng = dng + jnp.sum(dh * xhat, axis=0, keepdims=True)
            dxh = dh * ngb
            dod.append(r * (dxh - xhat * jnp.mean(dxh * xhat, axis=-1, keepdims=True)))
            normed.append(xhat * ngb)
        d_gd = d_dn * jnp.concatenate(normed, axis=1) * _silu_grad(gd, sgd)
        d_gf = d_fox * of * sgf * (1.0 - sgf)
        return jnp.concatenate(dod, axis=1), d_fox * sgf, d_gd, d_gf, dng
    return _rowwise(fn, [(proj, D_DN, EVEN_DN_GATE // D_DN), (proj, D_FOX, EVEN_FOX_GATE // D_FOX), o_dn, o_fox, dom],
                    [ng], [(D_DN, F32), (D_FOX, F32), (D_DN, BF16), (D_FOX, BF16)], [(1, HEAD_DIM)], tile=256,
                    name=name)


def _loss_grad(y, target, name):
    d = y.shape[1]

    def fn(yb, tb):
        diff = yb - tb
        part = jnp.sum(jnp.sum(diff * diff, axis=1, keepdims=True), axis=0, keepdims=True) * (0.5 / d)
        g = diff * (1.0 / d)
        return g, g, part
    return _rowwise(fn, [y, target], [], [(d, F32), (d, BF16)], [(1, 1)], tile=512, name=name)


_REF_EVEN = {"dn_qkv": (0, 1536), "dn_gate": (1536, 2048), "dn_ba": (2048, 2056), "fox_qkv": (2056, 3592),
             "fox_gate": (3592, 4104), "f_pre": (4104, 4108)}
D_IN_EVEN = 4108


def _even_to_kernel_layout(w):
    cut = lambda name: w[..., _REF_EVEN[name][0]:_REF_EVEN[name][1]]
    pad = jnp.zeros(w.shape[:-1] + (EVEN_WIDTH - EVEN_NARROW - 12,), w.dtype)
    return jnp.concatenate([cut("dn_qkv"), cut("fox_qkv"), cut("dn_gate"), cut("fox_gate"), cut("dn_ba"),
                            cut("f_pre"), pad], axis=-1)


def _even_from_kernel_layout(g):
    return jnp.concatenate([g[..., EVEN_DN_QKV:EVEN_FOX_QKV], g[..., EVEN_DN_GATE:EVEN_FOX_GATE],
                            g[..., EVEN_NARROW:EVEN_NARROW + 8], g[..., EVEN_FOX_QKV:EVEN_DN_GATE],
                            g[..., EVEN_FOX_GATE:EVEN_NARROW], g[..., EVEN_NARROW + 8:EVEN_NARROW + 12]], axis=-1)


def _forget_rows(c):
    return c[:, LANE_FORGET:LANE_FORGET + N_FOX_HEADS].T.reshape(N_FOX_HEADS, 1, c.shape[0])


def _forget_lanes(rows):
    s = rows.shape[2]
    return jnp.pad(rows.reshape(-1, s).T, ((0, 0), (LANE_FORGET, LANES - LANE_FORGET - N_FOX_HEADS)))


def _even_fwd(x, gain, w_in, w_out, j, p, tag):
    h = _rms_fwd(x, gain, f"{tag}_norm")
    proj = _mm(h, w_in, "nn", tm=512, tn=384, out_dtype=F32, name=f"{tag}_in", b_lead=(j,))
    y = _conv_fwd(proj, p["conv_w"], f"{tag}_conv")
    dn_qkv = _dn_pre_fwd(y, f"{tag}_dn_pre")
    act = _narrow_fwd(proj, p["narrow"], f"{tag}_narrow")
    o_dn, states, tinv = _dn_fwd(dn_qkv, act, f"{tag}_delta")
    fox_qkv = _fox_pre_fwd(proj, p["q_g"], p["k_g"], f"{tag}_fox_pre")
    c = _cumsum_rows(act, False, f"{tag}_cumsum")
    ct = _forget_rows(c)
    o_fox, lse = _fox_fwd(fox_qkv, c, ct, f"{tag}_fox")
    om = _mix_gate_fwd(proj, o_dn, o_fox, p["dn_norm_g"], f"{tag}_gate")
    x2 = _mm(om, w_out, "nn", tm=512, tn=512, out_dtype=F32, name=f"{tag}_out", residual=x, b_lead=(j,))
    return x2, (x, h, proj, y, dn_qkv, act, states, tinv, o_dn, fox_qkv, c, ct, o_fox, lse, om)


def _even_bwd(dxo, dxo16, saved, gain, w_in, w_out, j, p, tag, g_in, g_out):
    x, h, proj, y, dn_qkv, act, states, tinv, o_dn, fox_qkv, c, ct, o_fox, lse, om = saved
    d = x.shape[1]
    dom = _mm(dxo16, w_out, "nt", tm=512, tn=512, out_dtype=F32, name=f"{tag}_out_bwd", b_lead=(j,))
    g_out = _mm(om, dxo16, "tn", tm=512, tn=d, out_dtype=F32, name=f"{tag}_out_dw", into=(g_out, j))
    d_odn, d_ofox, d_gd, d_gf, d_ng = _mix_gate_bwd(proj, o_dn, o_fox, p["dn_norm_g"], dom, f"{tag}_gate_bwd")
    dq, dk, dv, dct = _fox_bwd(fox_qkv, c, ct, o_fox, lse, d_ofox, f"{tag}_fox_bwd")
    d_fox_qkv, d_qg, d_kg = _fox_pre_bwd(proj, p["q_g"], p["k_g"], dq, dk, dv, f"{tag}_fox_pre_bwd")
    dlogf = _cumsum_rows(_forget_lanes(dct), True, f"{tag}_cumsum_bwd")
    dq, dk, dv, dact = _dn_bwd(dn_qkv, act, states, tinv, d_odn, f"{tag}_delta_bwd")
    dy = _dn_pre_bwd(y, dq, dk, dv, f"{tag}_dn_pre_bwd")
    d_dn_qkv, d_conv = _conv_bwd(proj, p["conv_w"], dy, f"{tag}_conv_bwd")
    d_narrow, s_alog, s_dt, s_fb = _narrow_bwd(proj, p["narrow"], act, dact, dlogf, f"{tag}_narrow_bwd")
    dproj = jnp.concatenate([d_dn_qkv, d_fox_qkv, d_gd, d_gf, d_narrow], axis=1)
    dh = _mm(dproj, w_in, "nt", tm=512, tn=512, out_dtype=F32, name=f"{tag}_in_bwd", b_lead=(j,))
    g_in = _mm(h, dproj, "tn", tm=512, tn=384, out_dtype=F32, name=f"{tag}_in_dw", into=(g_in, j))
    dx, dx16, d_gain = _rms_bwd(x, gain, dh, dxo, f"{tag}_norm_bwd")
    small = {"conv_w": d_conv, "a_log": s_alog, "dt_bias": s_dt, "f_bias": s_fb, "dn_norm_g": d_ng, "q_g": d_qg,
             "k_g": d_kg}
    return dx, dx16, d_gain, small, g_in, g_out


def _odd_fwd(x, gain, w_in, w_out, j, tag):
    h = _rms_fwd(x, gain, f"{tag}_norm")
    qkv = _mm(h, w_in, "nn", tm=512, tn=768, out_dtype=BF16, name=f"{tag}_in", b_lead=(j,))
    o16, o32 = _sb_fwd(qkv, N_SB_HEADS, f"{tag}_sb")
    x2 = _mm(o16, w_out, "nn", tm=512, tn=512, out_dtype=F32, name=f"{tag}_out", residual=x, b_lead=(j,))
    return x2, (x, h, qkv, o16, o32)


def _odd_bwd(dxo, dxo16, saved, gain, w_in, w_out, j, tag, g_in, g_out):
    x, h, qkv, o16, o32 = saved
    d = x.shape[1]
    do = _mm(dxo16, w_out, "nt", tm=512, tn=512, out_dtype=BF16, name=f"{tag}_out_bwd", b_lead=(j,))
    g_out = _mm(o16, dxo16, "tn", tm=512, tn=d, out_dtype=F32, name=f"{tag}_out_dw", into=(g_out, j))
    dq, dk, dv = _sb_bwd(qkv, o32, do, N_SB_HEADS, f"{tag}_sb_bwd")
    dqkv = jnp.concatenate([dq, dk.astype(BF16), dv.astype(BF16)], axis=1)
    dh = _mm(dqkv, w_in, "nt", tm=512, tn=512, out_dtype=F32, name=f"{tag}_in_bwd", b_lead=(j,))
    g_in = _mm(h, dqkv, "tn", tm=512, tn=768, out_dtype=F32, name=f"{tag}_in_dw", into=(g_in, j))
    dx, dx16, d_gain = _rms_bwd(x, gain, dh, dxo, f"{tag}_norm_bwd")
    return dx, dx16, d_gain, g_in, g_out


def _forward_backward(x, target, w):
    depth = w["norm_ffn1"].shape[0]
    row = lambda a, l: a[l][None]

    def even_small(j):
        return {"conv_w": w["dn_conv_w"][j], "narrow": _narrow_params(w["dn_a_log"][j], w["dn_dt_bias"][j],
                                                                     w["fox_f_bias"][j]),
                "dn_norm_g": row(w["dn_norm_g"], j), "q_g": row(w["fox_q_norm_g"], j),
                "k_g": row(w["fox_k_norm_g"], j)}

    saved = []
    for l in range(depth):
        x, s1 = _ffn_fwd(x, row(w["norm_ffn1"], l), w["ffn1_w_gu"], w["ffn1_w_down"], l, "ffn1")
        if l % 2 == 0:
            x, s2 = _even_fwd(x, row(w["norm_mix"], l), w["w_in_even"], w["w_out_even"], l // 2, even_small(l // 2),
                              "even")
        else:
            x, s2 = _odd_fwd(x, row(w["norm_mix"], l), w["w_in_odd"], w["w_out_odd"], l // 2, "odd")
        x, s3 = _ffn_fwd(x, row(w["norm_ffn2"], l), w["ffn2_w_gu"], w["ffn2_w_down"], l, "ffn2")
        saved.append((s1, s2, s3))

    dx, dx16, loss = _loss_grad(x, target, "loss")

    big = {k: lax.empty(w[k].shape, F32) for k in ("ffn1_w_gu", "ffn1_w_down", "ffn2_w_gu", "ffn2_w_down",
                                                   "w_in_even", "w_out_even", "w_in_odd", "w_out_odd")}
    d_norm = {k: [None] * depth for k in ("norm_ffn1", "norm_mix", "norm_ffn2")}
    d_even = [None] * ((depth + 1) // 2)
    for l in reversed(range(depth)):
        s1, s2, s3 = saved[l]
        dx, dx16, d_norm["norm_ffn2"][l], big["ffn2_w_gu"], big["ffn2_w_down"] = _ffn_bwd(
            dx, dx16, s3, row(w["norm_ffn2"], l), w["ffn2_w_gu"], w["ffn2_w_down"], l, "ffn2", big["ffn2_w_gu"],
            big["ffn2_w_down"])
        if l % 2 == 0:
            dx, dx16, d_norm["norm_mix"][l], d_even[l // 2], big["w_in_even"], big["w_out_even"] = _even_bwd(
                dx, dx16, s2, row(w["norm_mix"], l), w["w_in_even"], w["w_out_even"], l // 2, even_small(l // 2),
                "even", big["w_in_even"], big["w_out_even"])
        else:
            dx, dx16, d_norm["norm_mix"][l], big["w_in_odd"], big["w_out_odd"] = _odd_bwd(
                dx, dx16, s2, row(w["norm_mix"], l), w["w_in_odd"], w["w_out_odd"], l // 2, "odd", big["w_in_odd"],
                big["w_out_odd"])
        dx, dx16, d_norm["norm_ffn1"][l], big["ffn1_w_gu"], big["ffn1_w_down"] = _ffn_bwd(
            dx, dx16, s1, row(w["norm_ffn1"], l), w["ffn1_w_gu"], w["ffn1_w_down"], l, "ffn1", big["ffn1_w_gu"],
            big["ffn1_w_down"])

    small = {k: jnp.concatenate(v, axis=0) for k, v in d_norm.items()}
    dec = slice(LANE_DECAY, LANE_DECAY + N_DN_HEADS)
    fgt = slice(LANE_FORGET, LANE_FORGET + N_FOX_HEADS)
    small["dn_conv_w"] = jnp.stack([e["conv_w"] for e in d_even])
    small["dn_a_log"] = jnp.concatenate([e["a_log"][:, dec] for e in d_even], axis=0)
    small["dn_dt_bias"] = jnp.concatenate([e["dt_bias"][:, dec] for e in d_even], axis=0)
    small["fox_f_bias"] = jnp.concatenate([e["f_bias"][:, fgt] for e in d_even], axis=0)
    small["dn_norm_g"] = jnp.concatenate([e["dn_norm_g"] for e in d_even], axis=0)
    small["fox_q_norm_g"] = jnp.concatenate([e["q_g"] for e in d_even], axis=0)
    small["fox_k_norm_g"] = jnp.concatenate([e["k_g"] for e in d_even], axis=0)
    return loss, dx, big, small


MESH = pl.DeviceIdType.MESH
ANY = pl.BlockSpec(memory_space=pl.ANY)


def _place():
    x, y, c = lax.axis_index("x"), lax.axis_index("y"), lax.axis_index("c")
    return x, y, c, [(1 - x, y), (x, 1 - y), (1 - x, 1 - y)]


def _remote(src, dst, send_sem, recv_sem, to):
    return pltpu.make_async_remote_copy(src_ref=src, dst_ref=dst, send_sem=send_sem, recv_sem=recv_sem,
                                        device_id=to, device_id_type=MESH)


def _aligned(start, multiple):
    return start if isinstance(start, int) else pl.multiple_of(start, multiple)


def _quarter(ref, kind, chip, half, rows, cols):
    k = 2 * chip[0] + chip[1]
    hr = rows // 2
    assert hr % 16 == 0 and cols % LANES == 0
    if kind == "col":
        return ref.at[:, pl.ds(_aligned(half * hr, 16), hr), pl.ds(_aligned(k * cols, LANES), cols)]
    return ref.at[:, pl.ds(_aligned(k * rows + half * hr, 16), hr), :]


def _gather_weights(shards, kinds):
    n = len(shards)

    def body(*refs):
        ins, outs = refs[:n], refs[n:2 * n]
        send_sems, recv_sems, local_sems = refs[2 * n:]
        x, y, c, chips = _place()
        sibling = (x, y, 1 - c)
        local, first, passed = [], [], []
        for t in range(n):
            _, rows, cols = ins[t].shape
            hr = rows // 2
            win = functools.partial(_quarter, outs[t], kinds[t], rows=rows, cols=cols)
            for half in (0, 1):
                cp = pltpu.make_async_copy(ins[t].at[:, pl.ds(half * hr, hr), :], win((x, y), half),
                                           local_sems.at[t, half])
                cp.start()
                local.append(cp)
            mine = ins[t].at[:, pl.ds(pl.multiple_of(c * hr, 16), hr), :]
            for j, chip in enumerate(chips):
                cp = _remote(mine, win((x, y), c), send_sems.at[t, j], recv_sems.at[t, j], (*chip, c))
                cp.start()
                first.append(cp)
        for j, chip in enumerate(chips):
            for t in range(n):
                _, rows, cols = ins[t].shape
                got = _quarter(outs[t], kinds[t], chip, c, rows, cols)
                _remote(got, got, send_sems.at[t, j], recv_sems.at[t, j], (*chip, c)).wait_recv()
                cp = _remote(got, got, send_sems.at[t, 3 + j], recv_sems.at[t, 3 + j], sibling)
                cp.start()
                passed.append(cp)
        for j, chip in enumerate(chips):
            for t in range(n):
                _, rows, cols = ins[t].shape
                got = _quarter(outs[t], kinds[t], chip, 1 - c, rows, cols)
                _remote(got, got, send_sems.at[t, 3 + j], recv_sems.at[t, 3 + j], sibling).wait_recv()
        for cp in first + passed:
            cp.wait_send()
        for cp in local:
            cp.wait()

    def whole(a, kind):
        l, rows, cols = a.shape
        return jax.ShapeDtypeStruct((l, rows, 4 * cols) if kind == "col" else (l, 4 * rows, cols), a.dtype)

    return pl.pallas_call(
        body, name="gather_weights", in_specs=[ANY] * n, out_specs=[ANY] * n,
        out_shape=[whole(a, k) for a, k in zip(shards, kinds)],
        scratch_shapes=[pltpu.SemaphoreType.DMA((n, 6)), pltpu.SemaphoreType.DMA((n, 6)),
                        pltpu.SemaphoreType.DMA((n, 2))],
        compiler_params=pltpu.CompilerParams(has_side_effects=True),
    )(*shards)


def _canonical(a, kind):
    l, r, c = a.shape
    return a.reshape(l, 1, r, c) if kind == "col" else a.reshape(l, 4, r // 4, c)


def _rs_sibling(parts):
    n = len(parts)

    def body(*refs):
        ins, outs = refs[:n], refs[n:2 * n]
        send_sems, recv_sems = refs[2 * n:]
        x, y, c, _ = _place()
        copies = []
        for t in range(n):
            hr = ins[t].shape[2] // 2
            src = ins[t].at[:, :, pl.ds(pl.multiple_of((1 - c) * hr, 8), hr), :]
            cp = _remote(src, outs[t], send_sems.at[t], recv_sems.at[t], (x, y, 1 - c))
            cp.start()
            copies.append(cp)
        for cp in copies:
            cp.wait()

    half = lambda a: jax.ShapeDtypeStruct(a.shape[:2] + (a.shape[2] // 2, a.shape[3]), a.dtype)
    return pl.pallas_call(
        body, name="reduce_sibling", in_specs=[ANY] * n, out_specs=[ANY] * n, out_shape=[half(a) for a in parts],
        scratch_shapes=[pltpu.SemaphoreType.DMA((n,)), pltpu.SemaphoreType.DMA((n,))],
        compiler_params=pltpu.CompilerParams(has_side_effects=True),
    )(*parts)


def _add_tile(rows, cols):
    tc = cols if cols <= 1536 else cols // 4
    tr = rows
    while tr * tc * 4 > (1 << 20) and tr % 16 == 0:
        tr //= 2
    return tr, tc


def _rs_add_sibling(part, got, c, name):
    l, a, hr, cols = got.shape
    tr, tc = _add_tile(hr, cols)
    nr = hr // tr

    def body(c_ref, p_ref, g_ref, o32_ref, o16_ref):
        s = p_ref[...] + g_ref[...]
        o32_ref[...] = s
        o16_ref[...] = s.astype(BF16)

    blk = (None, None, tr, tc)
    spec = pl.BlockSpec(blk, lambda li, ai, i, j, c_ref: (li, ai, i, j))
    return pl.pallas_call(
        body, name=name,
        grid_spec=pltpu.PrefetchScalarGridSpec(
            num_scalar_prefetch=1, grid=(l, a, nr, cols // tc),
            in_specs=[pl.BlockSpec(blk, lambda li, ai, i, j, c_ref: (li, ai, c_ref[0] * nr + i, j)), spec],
            out_specs=[spec, spec]),
        out_shape=[jax.ShapeDtypeStruct(got.shape, F32), jax.ShapeDtypeStruct(got.shape, BF16)],
        compiler_params=_cparams("parallel", "parallel", "parallel", "parallel"),
    )(c, part, got)


def _quarter4(ref, kind, chip, cols):
    k = 2 * chip[0] + chip[1]
    if kind == "col":
        return ref.at[:, :, :, pl.ds(pl.multiple_of(k * cols, LANES), cols)]
    return ref.at[:, pl.ds(k, 1), :, :]


def _rs_chips(sums16, kinds):
    n = len(sums16)

    def qshape(a, kind):
        l, na, hr, cols = a.shape
        return (l, 1, hr, cols // 4 if kind == "col" else cols)

    def body(*refs):
        ins, outs = refs[:n], refs[n:2 * n]
        send_sems, recv_sems = refs[2 * n:]
        x, y, c, chips = _place()
        copies = []
        for t in range(n):
            cols = qshape(ins[t], kinds[t])[3]
            for j, chip in enumerate(chips):
                cp = _remote(_quarter4(ins[t], kinds[t], chip, cols), outs[t].at[j], send_sems.at[t, j],
                             recv_sems.at[t, j], (*chip, c))
                cp.start()
                copies.append(cp)
        for cp in copies:
            cp.wait()

    return pl.pallas_call(
        body, name="reduce_chips", in_specs=[ANY] * n, out_specs=[ANY] * n,
        out_shape=[jax.ShapeDtypeStruct((3,) + qshape(a, k), a.dtype) for a, k in zip(sums16, kinds)],
        scratch_shapes=[pltpu.SemaphoreType.DMA((n, 3)), pltpu.SemaphoreType.DMA((n, 3))],
        compiler_params=pltpu.CompilerParams(has_side_effects=True),
    )(*sums16)


def _rs_add_chips(sum32, got, kind, k, name):
    _, l, _, hr, cols = got.shape
    tr, tc = _add_tile(hr, cols)
    if kind == "col":
        own = pl.BlockSpec((None, None, tr, cols), lambda li, i, k_ref: (li, 0, i, k_ref[0]))
    else:
        own = pl.BlockSpec((None, None, tr, cols), lambda li, i, k_ref: (li, k_ref[0], i, 0))

    def body(k_ref, own_ref, got_ref, o_ref):
        o_ref[...] = ((own_ref[...] + got_ref[0].astype(F32)) + got_ref[1].astype(F32)) + got_ref[2].astype(F32)

    return pl.pallas_call(
        body, name=name,
        grid_spec=pltpu.PrefetchScalarGridSpec(
            num_scalar_prefetch=1, grid=(l, hr // tr),
            in_specs=[own, pl.BlockSpec((3, None, None, tr, cols), lambda li, i, k_ref: (0, li, 0, i, 0))],
            out_specs=pl.BlockSpec((None, tr, cols), lambda li, i, k_ref: (li, i, 0))),
        out_shape=jax.ShapeDtypeStruct((l, hr, cols), F32),
        compiler_params=_cparams("parallel", "parallel"),
    )(k, sum32, got)


def _rs_finish(halves):
    n = len(halves)

    def body(*refs):
        ins, outs = refs[:n], refs[n:2 * n]
        send_sems, recv_sems, local_sems = refs[2 * n:]
        x, y, c, _ = _place()
        copies = []
        for t in range(n):
            hr = ins[t].shape[1]
            dst = outs[t].at[:, pl.ds(pl.multiple_of(c * hr, 8), hr), :]
            lc = pltpu.make_async_copy(ins[t], dst, local_sems.at[t])
            lc.start()
            cp = _remote(ins[t], dst, send_sems.at[t], recv_sems.at[t], (x, y, 1 - c))
            cp.start()
            copies += [lc, cp]
        for cp in copies:
            cp.wait()

    return pl.pallas_call(
        body, name="reduce_finish", in_specs=[ANY] * n, out_specs=[ANY] * n,
        out_shape=[jax.ShapeDtypeStruct((a.shape[0], 2 * a.shape[1], a.shape[2]), a.dtype) for a in halves],
        scratch_shapes=[pltpu.SemaphoreType.DMA((n,)), pltpu.SemaphoreType.DMA((n,)), pltpu.SemaphoreType.DMA((n,))],
        compiler_params=pltpu.CompilerParams(has_side_effects=True),
    )(*halves)


def _reduce_scatter(parts, kinds, tags):
    x, y, c = lax.axis_index("x"), lax.axis_index("y"), lax.axis_index("c")
    c_arr = jnp.reshape(c, (1,)).astype(jnp.int32)
    k_arr = jnp.reshape(2 * x + y, (1,)).astype(jnp.int32)
    canon = [_canonical(p, kind) for p, kind in zip(parts, kinds)]
    from_sibling = _rs_sibling(canon)
    sums = [_rs_add_sibling(p, g, c_arr, f"reduce_add_sibling_{tag}") for p, g, tag in zip(canon, from_sibling, tags)]
    from_chips = _rs_chips([s16 for _, s16 in sums], kinds)
    halves = [_rs_add_chips(s32, g, kind, k_arr, f"reduce_add_chips_{tag}")
              for (s32, _), g, kind, tag in zip(sums, from_chips, kinds, tags)]
    return _rs_finish(halves)


SMALL_PEERS = 7


def _small_exchange(pack):
    rows = pack.shape[0]

    def body(p_ref, slots_ref, total_ref, send_sems, recv_sems):
        x, y, c, _ = _place()
        me = 4 * x + 2 * y + c
        slots_ref[me] = p_ref[...]
        copies = []
        for p in range(1, SMALL_PEERS + 1):
            px, py, pc = (p >> 2) & 1, (p >> 1) & 1, p & 1
            peer = (1 - x if px else x, 1 - y if py else y, 1 - c if pc else c)
            cp = _remote(p_ref, slots_ref.at[me], send_sems.at[p - 1], recv_sems.at[p - 1], peer)
            cp.start()
            copies.append(cp)
        for cp in copies:
            cp.wait()
        total = slots_ref[0]
        for i in range(1, SMALL_PEERS + 1):
            total = total + slots_ref[i]
        total_ref[...] = total

    vmem = pl.BlockSpec(memory_space=pltpu.VMEM)
    return pl.pallas_call(
        body, name="small_exchange", in_specs=[vmem], out_specs=[vmem, vmem],
        out_shape=[jax.ShapeDtypeStruct((SMALL_PEERS + 1, rows, LANES), F32), jax.ShapeDtypeStruct((rows, LANES), F32)],
        scratch_shapes=[pltpu.SemaphoreType.DMA((SMALL_PEERS,)), pltpu.SemaphoreType.DMA((SMALL_PEERS,))],
        compiler_params=pltpu.CompilerParams(has_side_effects=True),
    )(pack)


def _pack(arrays):
    rows = []
    for a in arrays:
        flat = a.reshape(-1).astype(F32)
        rows.append(jnp.pad(flat, (0, (-flat.shape[0]) % LANES)).reshape(-1, LANES))
    out = jnp.concatenate(rows, axis=0)
    return jnp.pad(out, ((0, (-out.shape[0]) % 8), (0, 0)))


def _unpack(pack, shapes):
    out, r = [], 0
    for sh in shapes:
        size = math.prod(sh)
        nr = -(-size // LANES)
        out.append(pack[r:r + nr].reshape(-1)[:size].reshape(sh))
        r += nr
    return out


def _adamw(w, g, m, v, name):
    shape = w.shape
    to2d = lambda a: a.reshape(-1, shape[-1])
    rows = math.prod(shape[:-1])
    tile = 256 if rows % 256 == 0 else rows

    def fn(wb, gb, mb, vb):
        m2 = ADAM_B1 * mb + (1.0 - ADAM_B1) * gb
        v2 = ADAM_B2 * vb + (1.0 - ADAM_B2) * (gb * gb)
        m_hat = m2 / (1.0 - ADAM_B1 ** ADAM_STEP)
        v_hat = v2 / (1.0 - ADAM_B2 ** ADAM_STEP)
        return -ADAM_LR * (m_hat / (jnp.sqrt(v_hat) + ADAM_EPS) + ADAM_WD * wb), m2, v2

    res = _rowwise(fn, [to2d(w), to2d(g), to2d(m), to2d(v)], [], [(shape[-1], F32)] * 3, [], tile=tile, name=name)
    return [r.reshape(shape) for r in res]


BIG = (("ffn1_w_gu", "col"), ("ffn1_w_down", "row"), ("w_in_even", "col"), ("w_out_even", "row"),
       ("w_in_odd", "col"), ("w_out_odd", "row"), ("ffn2_w_gu", "col"), ("ffn2_w_down", "row"))
SMALL = ("norm_ffn1", "norm_mix", "dn_conv_w", "dn_a_log", "dn_dt_bias", "dn_norm_g", "fox_q_norm_g", "fox_k_norm_g",
         "fox_f_bias", "norm_ffn2")
WEIGHTS = ("norm_ffn1", "ffn1_w_gu", "ffn1_w_down", "norm_mix", "w_in_even", "dn_conv_w", "dn_a_log", "dn_dt_bias",
           "dn_norm_g", "fox_q_norm_g", "fox_k_norm_g", "fox_f_bias", "w_out_even", "w_in_odd", "w_out_odd",
           "norm_ffn2", "ffn2_w_gu", "ffn2_w_down")
EVEN_QUARTER = 1027
EVEN_QUARTER_PAD = 1152


def _step(x, target, w, m, v):
    k = 2 * lax.axis_index("x") + lax.axis_index("y")
    n_conv = w["dn_conv_w"].shape[2]

    shards = []
    for name, _ in BIG:
        a = w[name]
        if name == "w_in_even":
            a = jnp.pad(a, ((0, 0), (0, 0), (0, EVEN_QUARTER_PAD - EVEN_QUARTER)))
        shards.append(a.astype(BF16))
    whole = dict(zip([n for n, _ in BIG], _gather_weights(shards, [kind for _, kind in BIG])))
    padded = whole["w_in_even"]
    ref_order = jnp.concatenate([padded[..., q * EVEN_QUARTER_PAD:q * EVEN_QUARTER_PAD + EVEN_QUARTER]
                                 for q in range(4)], axis=-1)
    whole["w_in_even"] = _even_to_kernel_layout(ref_order)
    conv_slots, _ = _small_exchange(_pack([w["dn_conv_w"]]))
    conv_rows = math.prod(w["dn_conv_w"].shape) // LANES
    quarters = [conv_slots[2 * q, :conv_rows].reshape(w["dn_conv_w"].shape) for q in range(4)]
    whole["dn_conv_w"] = jnp.concatenate(quarters, axis=-1)
    for name in SMALL:
        if name != "dn_conv_w":
            whole[name] = w[name]

    loss, dx, big, small = _forward_backward(x, target, whole)

    g_even = _even_from_kernel_layout(big["w_in_even"])
    big["w_in_even"] = jnp.concatenate(
        [jnp.pad(g_even[..., q * EVEN_QUARTER:(q + 1) * EVEN_QUARTER],
                 ((0, 0), (0, 0), (0, EVEN_QUARTER_PAD - EVEN_QUARTER))) for q in range(4)], axis=-1)
    names = [n for n, _ in BIG]
    reduced = dict(zip(names, _reduce_scatter([big[n] for n in names], [kind for _, kind in BIG], names)))
    reduced["w_in_even"] = reduced["w_in_even"][..., :EVEN_QUARTER]
    _, small_sum = _small_exchange(_pack([small[n] for n in SMALL]))
    grads = dict(zip(SMALL, _unpack(small_sum, [small[n].shape for n in SMALL])))
    grads["dn_conv_w"] = lax.dynamic_slice_in_dim(grads["dn_conv_w"], k * n_conv, n_conv, axis=2)
    grads.update(reduced)

    delta, new_m, new_v = {}, {}, {}
    for name, _ in BIG:
        delta[name], new_m[name], new_v[name] = _adamw(w[name], grads[name], m[name], v[name], f"adamw_{name}")
    packs = [_pack([d[n] for n in SMALL]) for d in (w, grads, m, v)]
    shapes = [w[n].shape for n in SMALL]
    for out, res in zip((delta, new_m, new_v), _adamw(*packs, "adamw_small")):
        out.update(zip(SMALL, _unpack(res, shapes)))
    total_loss = lax.psum(loss[0, 0], ("x", "y", "c"))
    return total_loss, dx, grads, delta, new_m, new_v


def kernel(x, norm_ffn1, ffn1_w_gu, ffn1_w_down, norm_mix, w_in_even, dn_conv_w, dn_a_log, dn_dt_bias, dn_norm_g, fox_q_norm_g, fox_k_norm_g, fox_f_bias, w_out_even, w_in_odd, w_out_odd, norm_ffn2, ffn2_w_gu, ffn2_w_down, loss_target, m_norm_ffn1, m_ffn1_w_gu, m_ffn1_w_down, m_norm_mix, m_w_in_even, m_dn_conv_w, m_dn_a_log, m_dn_dt_bias, m_dn_norm_g, m_fox_q_norm_g, m_fox_k_norm_g, m_fox_f_bias, m_w_out_even, m_w_in_odd, m_w_out_odd, m_norm_ffn2, m_ffn2_w_gu, m_ffn2_w_down, v_norm_ffn1, v_ffn1_w_gu, v_ffn1_w_down, v_norm_mix, v_w_in_even, v_dn_conv_w, v_dn_a_log, v_dn_dt_bias, v_dn_norm_g, v_fox_q_norm_g, v_fox_k_norm_g, v_fox_f_bias, v_w_out_even, v_w_in_odd, v_w_out_odd, v_norm_ffn2, v_ffn2_w_gu, v_ffn2_w_down):
    w = dict(zip(WEIGHTS, (norm_ffn1, ffn1_w_gu, ffn1_w_down, norm_mix, w_in_even, dn_conv_w, dn_a_log, dn_dt_bias,
                           dn_norm_g, fox_q_norm_g, fox_k_norm_g, fox_f_bias, w_out_even, w_in_odd, w_out_odd,
                           norm_ffn2, ffn2_w_gu, ffn2_w_down)))
    m = dict(zip(WEIGHTS, (m_norm_ffn1, m_ffn1_w_gu, m_ffn1_w_down, m_norm_mix, m_w_in_even, m_dn_conv_w, m_dn_a_log,
                           m_dn_dt_bias, m_dn_norm_g, m_fox_q_norm_g, m_fox_k_norm_g, m_fox_f_bias, m_w_out_even,
                           m_w_in_odd, m_w_out_odd, m_norm_ffn2, m_ffn2_w_gu, m_ffn2_w_down)))
    v = dict(zip(WEIGHTS, (v_norm_ffn1, v_ffn1_w_gu, v_ffn1_w_down, v_norm_mix, v_w_in_even, v_dn_conv_w, v_dn_a_log,
                           v_dn_dt_bias, v_dn_norm_g, v_fox_q_norm_g, v_fox_k_norm_g, v_fox_f_bias, v_w_out_even,
                           v_w_in_odd, v_w_out_odd, v_norm_ffn2, v_ffn2_w_gu, v_ffn2_w_down)))
    loss, dx, grads, delta, new_m, new_v = _step(x[0], loss_target[0], w, m, v)
    return (loss, dx[None], *[grads[n] for n in WEIGHTS], *[delta[n] for n in WEIGHTS],
            *[new_m[n] for n in WEIGHTS], *[new_v[n] for n in WEIGHTS])
```

```python
import functools
import math

import jax
import jax.numpy as jnp
from jax import lax
from jax.experimental import pallas as pl
from jax.experimental.pallas import tpu as pltpu

F32 = jnp.float32
BF16 = jnp.bfloat16
HI = lax.Precision.HIGHEST

HEAD_DIM = 128
N_DN_HEADS = 4
N_FOX_HEADS = 4
N_SB_HEADS = 8
D_DN = N_DN_HEADS * HEAD_DIM
D_FOX = N_FOX_HEADS * HEAD_DIM
CONV_WIDTH = 4
DN_CHUNK = 64
EPS = 1e-6
ATT_SCALE = HEAD_DIM ** -0.5
ADAM_LR, ADAM_B1, ADAM_B2, ADAM_EPS, ADAM_WD, ADAM_STEP = 0.001, 0.9, 0.999, 1e-08, 0.01, 10

V7X_VMEM_LIMIT = 56 * 1024 * 1024
LANES = 128
ATT_TQ = 256
ATT_TK = 128
ATT_SUB = ATT_TQ // ATT_TK

LANE_BETA, LANE_DECAY, LANE_FORGET = 0, 4, 8


def _cparams(*sem):
    return pltpu.CompilerParams(dimension_semantics=sem, vmem_limit_bytes=V7X_VMEM_LIMIT)


def _sigmoid(x):
    return 1.0 / (1.0 + jnp.exp(-x))


def _softplus(x):
    return jnp.maximum(x, 0.0) + jnp.log(1.0 + jnp.exp(-jnp.abs(x)))


def _silu_grad(y, sg):
    return sg * (1.0 + y * (1.0 - sg))


def _rowwise(fn, rows, bcast, outs, sums, *, tile, name):
    rows = [r if isinstance(r, tuple) else (r, r.shape[1], 0) for r in rows]
    s = rows[0][0].shape[0]
    assert s % tile == 0
    n_in, n_b, n_out, n_sum = len(rows), len(bcast), len(outs), len(sums)

    def body(*refs):
        ins = [r[...] for r in refs[:n_in + n_b]]
        res = fn(*ins)
        if not isinstance(res, (tuple, list)):
            res = (res,)
        out_refs = refs[n_in + n_b:n_in + n_b + n_out]
        sum_refs = refs[n_in + n_b + n_out:]
        for o_ref, val in zip(out_refs, res[:n_out]):
            o_ref[...] = val.astype(o_ref.dtype)
        if n_sum:
            @pl.when(pl.program_id(0) == 0)
            def _():
                for s_ref in sum_refs:
                    s_ref[...] = jnp.zeros_like(s_ref)
            for s_ref, val in zip(sum_refs, res[n_out:]):
                s_ref[...] += val

    in_specs = [pl.BlockSpec((tile, w), lambda i, cb=cb: (i, cb)) for _, w, cb in rows]
    in_specs += [pl.BlockSpec(b.shape, lambda i, nd=b.ndim: (0,) * nd) for b in bcast]
    out_specs = [pl.BlockSpec((tile, c), lambda i: (i, 0)) for c, _ in outs]
    out_specs += [pl.BlockSpec(sh, lambda i: (0, 0)) for sh in sums]
    out_shape = [jax.ShapeDtypeStruct((s, c), dt) for c, dt in outs]
    out_shape += [jax.ShapeDtypeStruct(sh, F32) for sh in sums]
    return pl.pallas_call(
        body, name=name, grid=(s // tile,), in_specs=in_specs, out_specs=out_specs, out_shape=out_shape,
        compiler_params=_cparams("arbitrary" if n_sum else "parallel"),
    )(*[r[0] for r in rows], *bcast)


def _rms_fwd(x, gain, name):
    def fn(xb, g):
        r = lax.rsqrt(jnp.mean(xb * xb, axis=-1, keepdims=True) + EPS)
        return (xb * r * g,)
    return _rowwise(fn, [x], [gain], [(x.shape[1], BF16)], [], tile=512, name=name)[0]


def _rms_bwd(x, gain, dn, dres, name):
    def fn(xb, dnb, drb, g):
        r = lax.rsqrt(jnp.mean(xb * xb, axis=-1, keepdims=True) + EPS)
        xh = xb * r
        dxh = dnb * g
        dx = drb + r * (dxh - xh * jnp.mean(dxh * xh, axis=-1, keepdims=True))
        return dx, dx, jnp.sum(dnb * xh, axis=0, keepdims=True)
    d = x.shape[1]
    return _rowwise(fn, [x, dn, dres], [gain], [(d, F32), (d, BF16)], [(1, d)], tile=512, name=name)


_DIMS = {"nn": (((1,), (0,)), ((), ())), "nt": (((1,), (1,)), ((), ())), "tn": (((0,), (0,)), ((), ()))}


def _dot(a, b, kind):
    return lax.dot_general(a.astype(BF16), b.astype(BF16), _DIMS[kind], preferred_element_type=F32)


def _dot32(a, b, kind="nn"):
    return lax.dot_general(a, b, _DIMS[kind], precision=HI, preferred_element_type=F32)


def _mm(a, b, kind, *, tm, tn, out_dtype, name, scale=None, residual=None, a_lead=(), b_lead=(),
        b_spec=None, n=None, into=None):
    ash, bsh = a.shape[len(a_lead):], b.shape[len(b_lead):]
    m = ash[1] if kind == "tn" else ash[0]
    k = ash[0] if kind == "tn" else ash[1]
    if b_spec is None:
        n = bsh[0] if kind == "nt" else bsh[1]
        assert k == (bsh[1] if kind == "nt" else bsh[0]), (ash, bsh, kind)
    assert m % tm == 0 and n % tn == 0, (m, tm, n, tn)
    la, lb = (None,) * len(a_lead), (None,) * len(b_lead)
    if kind == "tn":
        a_spec = pl.BlockSpec(la + (k, tm), lambda i, j: a_lead + (0, i))
    else:
        a_spec = pl.BlockSpec(la + (tm, k), lambda i, j: a_lead + (i, 0))
    if b_spec is None:
        if kind == "nt":
            b_spec = pl.BlockSpec(lb + (tn, k), lambda i, j: b_lead + (j, 0))
        else:
            b_spec = pl.BlockSpec(lb + (k, tn), lambda i, j: b_lead + (0, j))
    in_specs, args = [a_spec, b_spec], [a, b]
    if residual is not None:
        in_specs.append(pl.BlockSpec((tm, tn), lambda i, j: (i, j)))
        args.append(residual)
    aliases = {}
    if into is not None:
        buf, layer = into
        in_specs.append(pl.BlockSpec(memory_space=pl.ANY))
        args.append(buf)
        aliases = {len(args) - 1: 0}
        out_spec = pl.BlockSpec((None, tm, tn), lambda i, j: (layer, i, j))
        out_shape = jax.ShapeDtypeStruct(buf.shape, buf.dtype)
    else:
        out_spec = pl.BlockSpec((tm, tn), lambda i, j: (i, j))
        out_shape = jax.ShapeDtypeStruct((m, n), out_dtype)

    def body(a_ref, b_ref, *rest):
        acc = _dot(a_ref[...], b_ref[...], kind)
        if scale is not None:
            acc = acc * scale
        if residual is not None:
            acc = acc + rest[0][...]
        rest[-1][...] = acc.astype(rest[-1].dtype)

    return pl.pallas_call(
        body, name=name, grid=(m // tm, n // tn), in_specs=in_specs, out_specs=out_spec, out_shape=out_shape,
        input_output_aliases=aliases, compiler_params=_cparams("parallel", "parallel"),
    )(*args)


def _ffn_up(n, w_gu, layer, name):
    s, d = n.shape
    f = w_gu.shape[2] // 2
    tm, tn = 512, f // 2
    nj = f // tn

    def body(n_ref, wg_ref, wu_ref, gu_ref, a_ref):
        nv = n_ref[...]
        g = _dot(nv, wg_ref[...], "nn")
        u = _dot(nv, wu_ref[...], "nn")
        gu_ref[0] = g.astype(BF16)
        gu_ref[1] = u.astype(BF16)
        a_ref[...] = (g * _sigmoid(g) * u).astype(BF16)

    return pl.pallas_call(
        body, name=name, grid=(s // tm, nj),
        in_specs=[pl.BlockSpec((tm, d), lambda i, j: (i, 0)),
                  pl.BlockSpec((None, d, tn), lambda i, j: (layer, 0, j)),
                  pl.BlockSpec((None, d, tn), lambda i, j: (layer, 0, j + nj))],
        out_specs=[pl.BlockSpec((2, tm, tn), lambda i, j: (0, i, j)),
                   pl.BlockSpec((tm, tn), lambda i, j: (i, j))],
        out_shape=[jax.ShapeDtypeStruct((2, s, f), BF16), jax.ShapeDtypeStruct((s, f), BF16)],
        compiler_params=_cparams("parallel", "parallel"),
    )(n, w_gu, w_gu)


def _ffn_down_bwd(dxo, w_down, gu, layer, name):
    s, d = dxo.shape
    f = w_down.shape[1]
    tm, tn = 512, f // 2

    def body(dx_ref, w_ref, gu_ref, dgu_ref):
        da = 0.5 * _dot(dx_ref[...], w_ref[...], "nt")
        g = gu_ref[0].astype(F32)
        u = gu_ref[1].astype(F32)
        sg = _sigmoid(g)
        dgu_ref[0] = (da * u * _silu_grad(g, sg)).astype(BF16)
        dgu_ref[1] = (da * g * sg).astype(BF16)

    return pl.pallas_call(
        body, name=name, grid=(s // tm, f // tn),
        in_specs=[pl.BlockSpec((tm, d), lambda i, j: (i, 0)),
                  pl.BlockSpec((None, tn, d), lambda i, j: (layer, j, 0)),
                  pl.BlockSpec((2, tm, tn), lambda i, j: (0, i, j))],
        out_specs=pl.BlockSpec((2, tm, tn), lambda i, j: (0, i, j)),
        out_shape=jax.ShapeDtypeStruct((2, s, f), BF16),
        compiler_params=_cparams("parallel", "parallel"),
    )(dxo, w_down, gu)


def _ffn_dn(dgu, w_gu, layer, name):
    _, s, f = dgu.shape
    d = w_gu.shape[1]
    tm, tn = 512, 512

    def body(dgu_ref, wg_ref, wu_ref, o_ref):
        o_ref[...] = _dot(dgu_ref[0], wg_ref[...], "nt") + _dot(dgu_ref[1], wu_ref[...], "nt")

    return pl.pallas_call(
        body, name=name, grid=(s // tm, d // tn),
        in_specs=[pl.BlockSpec((2, tm, f), lambda i, j: (0, i, 0)),
                  pl.BlockSpec((None, tn, f), lambda i, j: (layer, j, 0)),
                  pl.BlockSpec((None, tn, f), lambda i, j: (layer, j, 1))],
        out_specs=pl.BlockSpec((tm, tn), lambda i, j: (i, j)),
        out_shape=jax.ShapeDtypeStruct((s, d), F32),
        compiler_params=_cparams("parallel", "parallel"),
    )(dgu, w_gu, w_gu)


def _ffn_fwd(x, gain, w_gu, w_down, layer, tag):
    n = _rms_fwd(x, gain, f"{tag}_norm")
    gu, a = _ffn_up(n, w_gu, layer, f"{tag}_up")
    x2 = _mm(a, w_down, "nn", tm=512, tn=512, out_dtype=F32, name=f"{tag}_down", scale=0.5, residual=x,
             b_lead=(layer,))
    return x2, (x, n, gu, a)


def _ffn_bwd(dxo, dxo16, saved, gain, w_gu, w_down, layer, tag, g_gu, g_down):
    x, n, gu, a = saved
    s, f = a.shape
    dgu = _ffn_down_bwd(dxo16, w_down, gu, layer, f"{tag}_down_bwd")
    g_down = _mm(a, dxo16, "tn", tm=256, tn=dxo16.shape[1], out_dtype=F32, name=f"{tag}_down_dw", scale=0.5,
                 into=(g_down, layer))
    dn = _ffn_dn(dgu, w_gu, layer, f"{tag}_up_bwd")
    tn = f // 2
    nj = f // tn
    g_gu = _mm(n, dgu, "tn", tm=512, tn=tn, out_dtype=F32, name=f"{tag}_up_dw", into=(g_gu, layer), n=2 * f,
               b_spec=pl.BlockSpec((None, s, tn), lambda i, j: (j // nj, 0, j % nj)))
    dx, dx16, dgain = _rms_bwd(x, gain, dn, dxo, f"{tag}_norm_bwd")
    return dx, dx16, dgain, g_gu, g_down


def _lane_col(blk, lane):
    li = lax.broadcasted_iota(jnp.int32, blk.shape, 1)
    return jnp.sum(jnp.where(li == lane, blk, 0.0), axis=1, keepdims=True)


def _split_dot(x, tri):
    hi = x.astype(BF16)
    lo = (x - hi.astype(F32)).astype(BF16)
    return (lax.dot_general(hi, tri, _DIMS["nn"], preferred_element_type=F32)
            + lax.dot_general(lo, tri, _DIMS["nn"], preferred_element_type=F32))


def _att_specs(n_heads, s):
    q_spec = pl.BlockSpec((ATT_TQ, HEAD_DIM), lambda h, i: (i, h))
    k_spec = pl.BlockSpec((s, HEAD_DIM), lambda h, i: (0, n_heads + h))
    v_spec = pl.BlockSpec((s, HEAD_DIM), lambda h, i: (0, 2 * n_heads + h))
    return q_spec, k_spec, v_spec


def _att_iotas():
    row = lax.broadcasted_iota(jnp.int32, (ATT_TQ, ATT_TK), 0)
    col = lax.broadcasted_iota(jnp.int32, (ATT_TQ, ATT_TK), 1)
    jr = lax.broadcasted_iota(jnp.int32, (ATT_TK, ATT_TK), 0)
    jc = lax.broadcasted_iota(jnp.int32, (ATT_TK, ATT_TK), 1)
    return row, col, jr, jc


def _sb_fwd(qkv, n_heads, name):
    s = qkv.shape[0]

    def body(q_ref, k_ref, v_ref, o16_ref, o32_ref):
        i = pl.program_id(1)
        q = q_ref[...]
        row, col, jr, jc = _att_iotas()
        later = (jr > jc).astype(BF16)

        def step(jb, carry, diagonal):
            c_sp, acc = carry
            work = []
            for sub in reversed(range(ATT_SUB)):
                off = pl.multiple_of(jb * ATT_TQ + sub * ATT_TK, ATT_TK)
                z = _dot(q, k_ref[pl.ds(off, ATT_TK), :], "nt") * ATT_SCALE
                sp = _softplus(z)
                before = (col + sub * ATT_TK) < row if diagonal else None
                spm = jnp.where(before, sp, 0.0) if diagonal else sp
                work.append((off, z - sp, spm, _split_dot(spm, later), before))
            for off, logsig, spm, within, before in work:
                a = jnp.exp(logsig - (c_sp + within))
                if diagonal:
                    a = jnp.where(before, a, 0.0)
                acc = acc + _split_dot(a, v_ref[pl.ds(off, ATT_TK), :])
                c_sp = c_sp + jnp.sum(spm, axis=1, keepdims=True)
            return c_sp, acc

        carry = step(i, (jnp.zeros((ATT_TQ, 1), F32), jnp.zeros((ATT_TQ, HEAD_DIM), F32)), True)
        _, acc = lax.fori_loop(0, i, lambda it, cr: step(i - 1 - it, cr, False), carry)
        o16_ref[...] = acc.astype(BF16)
        o32_ref[...] = acc

    q_spec, k_spec, v_spec = _att_specs(n_heads, s)
    o_spec = pl.BlockSpec((ATT_TQ, HEAD_DIM), lambda h, i: (i, h))
    return pl.pallas_call(
        body, name=name, grid=(n_heads, s // ATT_TQ), in_specs=[q_spec, k_spec, v_spec],
        out_specs=[o_spec, o_spec],
        out_shape=[jax.ShapeDtypeStruct((s, n_heads * HEAD_DIM), BF16),
                   jax.ShapeDtypeStruct((s, n_heads * HEAD_DIM), F32)],
        compiler_params=_cparams("parallel", "arbitrary"),
    )(qkv, qkv, qkv)


def _sb_bwd(qkv, o32, do, n_heads, name):
    s = qkv.shape[0]

    def body(q_ref, k_ref, v_ref, o_ref, do_ref, dq_ref, dk_ref, dv_ref):
        i = pl.program_id(1)

        @pl.when(i == 0)
        def _():
            dk_ref[...] = jnp.zeros_like(dk_ref)
            dv_ref[...] = jnp.zeros_like(dv_ref)

        q = q_ref[...]
        do = do_ref[...]
        total = jnp.sum(do.astype(F32) * o_ref[...], axis=1, keepdims=True)
        row, col, jr, jc = _att_iotas()
        later = (jr > jc).astype(BF16)
        not_before = (jr >= jc).astype(BF16)

        def step(jb, carry, diagonal):
            c_sp, c_e, dq = carry
            work = []
            for sub in reversed(range(ATT_SUB)):
                off = pl.multiple_of(jb * ATT_TQ + sub * ATT_TK, ATT_TK)
                z = _dot(q, k_ref[pl.ds(off, ATT_TK), :], "nt") * ATT_SCALE
                sp = _softplus(z)
                before = (col + sub * ATT_TK) < row if diagonal else None
                spm = jnp.where(before, sp, 0.0) if diagonal else sp
                work.append((off, jnp.exp(z - sp), spm, _split_dot(spm, later),
                             _dot(do, v_ref[pl.ds(off, ATT_TK), :], "nt"), before))
            for off, sig, spm, within, da, before in work:
                a = sig * jnp.exp(-(c_sp + within))
                if diagonal:
                    a = jnp.where(before, a, 0.0)
                e = a * da
                left = total - c_e - _split_dot(e, not_before)
                dz = (e * (1.0 - sig) - left * sig) * ATT_SCALE
                if diagonal:
                    dz = jnp.where(before, dz, 0.0)
                dk_ref[pl.ds(off, ATT_TK), :] += _dot(dz, q, "tn")
                dv_ref[pl.ds(off, ATT_TK), :] += _dot(a, do, "tn")
                dq = dq + _dot(dz, k_ref[pl.ds(off, ATT_TK), :], "nn")
                c_sp = c_sp + jnp.sum(spm, axis=1, keepdims=True)
                c_e = c_e + jnp.sum(e, axis=1, keepdims=True)
            return c_sp, c_e, dq

        zero = jnp.zeros((ATT_TQ, 1), F32)
        carry = step(i, (zero, zero, jnp.zeros((ATT_TQ, HEAD_DIM), F32)), True)
        _, _, dq = lax.fori_loop(0, i, lambda it, cr: step(i - 1 - it, cr, False), carry)
        dq_ref[...] = dq.astype(BF16)

    q_spec, k_spec, v_spec = _att_specs(n_heads, s)
    blk = pl.BlockSpec((ATT_TQ, HEAD_DIM), lambda h, i: (i, h))
    full = pl.BlockSpec((s, HEAD_DIM), lambda h, i: (0, h))
    wide = (s, n_heads * HEAD_DIM)
    return pl.pallas_call(
        body, name=name, grid=(n_heads, s // ATT_TQ), in_specs=[q_spec, k_spec, v_spec, blk, blk],
        out_specs=[blk, full, full],
        out_shape=[jax.ShapeDtypeStruct(wide, BF16), jax.ShapeDtypeStruct(wide, F32), jax.ShapeDtypeStruct(wide, F32)],
        compiler_params=_cparams("parallel", "arbitrary"),
    )(qkv, qkv, qkv, o32, do)


def _fox_logits(q, k_ref, cq, ck_ref, off):
    return _dot(q, k_ref[pl.ds(off, ATT_TK), :], "nt") * ATT_SCALE + (cq - ck_ref[:, pl.ds(off, ATT_TK)])


def _fox_fwd(qkv, c, ct, name):
    s = qkv.shape[0]
    n_heads = N_FOX_HEADS

    def body(q_ref, k_ref, v_ref, c_ref, ct_ref, o_ref, lse_ref):
        h, i = pl.program_id(0), pl.program_id(1)
        q = q_ref[...]
        cq = _lane_col(c_ref[...], LANE_FORGET + h)
        row, col, _, _ = _att_iotas()

        def step(jb, carry, diagonal):
            m, l, acc = carry
            work = []
            m_new = m
            for sub in range(ATT_SUB):
                off = pl.multiple_of(jb * ATT_TQ + sub * ATT_TK, ATT_TK)
                sc = _fox_logits(q, k_ref, cq, ct_ref, off)
                valid = (col + sub * ATT_TK) <= row if diagonal else None
                if diagonal:
                    sc = jnp.where(valid, sc, -1e30)
                m_new = jnp.maximum(m_new, jnp.max(sc, axis=1, keepdims=True))
                work.append((off, sc, valid))
            w = jnp.exp(m - m_new)
            l, acc = l * w, acc * w
            for off, sc, valid in work:
                p = jnp.exp(sc - m_new)
                if diagonal:
                    p = jnp.where(valid, p, 0.0)
                l = l + jnp.sum(p, axis=1, keepdims=True)
                acc = acc + _split_dot(p, v_ref[pl.ds(off, ATT_TK), :])
            return m_new, l, acc

        init = (jnp.full((ATT_TQ, 1), -1e30, F32), jnp.zeros((ATT_TQ, 1), F32), jnp.zeros((ATT_TQ, HEAD_DIM), F32))
        m, l, acc = lax.fori_loop(0, i, lambda jb, cr: step(jb, cr, False), step(i, init, True))
        o_ref[...] = acc / l
        lse_ref[...] = jnp.broadcast_to(m + jnp.log(l), (ATT_TQ, LANES))

    q_spec, k_spec, v_spec = _att_specs(n_heads, s)
    return pl.pallas_call(
        body, name=name, grid=(n_heads, s // ATT_TQ),
        in_specs=[q_spec, k_spec, v_spec, pl.BlockSpec((ATT_TQ, LANES), lambda h, i: (i, 0)),
                  pl.BlockSpec((None, 1, s), lambda h, i: (h, 0, 0))],
        out_specs=[pl.BlockSpec((ATT_TQ, HEAD_DIM), lambda h, i: (i, h)),
                   pl.BlockSpec((None, ATT_TQ, LANES), lambda h, i: (h, i, 0))],
        out_shape=[jax.ShapeDtypeStruct((s, n_heads * HEAD_DIM), F32),
                   jax.ShapeDtypeStruct((n_heads, s, LANES), F32)],
        compiler_params=_cparams("parallel", "arbitrary"),
    )(qkv, qkv, qkv, c, ct)


def _fox_bwd(qkv, c, ct, o, lse, do, name):
    s = qkv.shape[0]
    n_heads = N_FOX_HEADS

    def body(q_ref, k_ref, v_ref, c_ref, ct_ref, o_ref, lse_ref, do_ref, dq_ref, dk_ref, dv_ref, dct_ref):
        h, i = pl.program_id(0), pl.program_id(1)

        @pl.when(i == 0)
        def _():
            dk_ref[...] = jnp.zeros_like(dk_ref)
            dv_ref[...] = jnp.zeros_like(dv_ref)
            dct_ref[...] = jnp.zeros_like(dct_ref)

        q = q_ref[...]
        do = do_ref[...]
        do16 = do.astype(BF16)
        delta = jnp.sum(do16.astype(F32) * o_ref[...], axis=1, keepdims=True)
        lse_col = lse_ref[:, 0:1]
        cq = _lane_col(c_ref[...], LANE_FORGET + h)
        row, col, _, _ = _att_iotas()

        def step(jb, dq, diagonal):
            for sub in range(ATT_SUB):
                off = pl.multiple_of(jb * ATT_TQ + sub * ATT_TK, ATT_TK)
                sc = _fox_logits(q, k_ref, cq, ct_ref, off)
                if diagonal:
                    valid = (col + sub * ATT_TK) <= row
                    p = jnp.where(valid, jnp.exp(jnp.where(valid, sc, 0.0) - lse_col), 0.0)
                else:
                    p = jnp.exp(sc - lse_col)
                ds = p * (_dot(do16, v_ref[pl.ds(off, ATT_TK), :], "nt") - delta)
                dct_ref[:, pl.ds(off, ATT_TK)] -= jnp.sum(ds, axis=0, keepdims=True)
                dss = ds * ATT_SCALE
                dk_ref[pl.ds(off, ATT_TK), :] += _dot(dss, q, "tn")
                dv_ref[pl.ds(off, ATT_TK), :] += _dot(p, do16, "tn")
                dq = dq + _dot(dss, k_ref[pl.ds(off, ATT_TK), :], "nn")
            return dq

        dq0 = step(i, jnp.zeros((ATT_TQ, HEAD_DIM), F32), True)
        dq_ref[...] = lax.fori_loop(0, i, lambda jb, dq: step(jb, dq, False), dq0)

    q_spec, k_spec, v_spec = _att_specs(n_heads, s)
    blk = pl.BlockSpec((ATT_TQ, HEAD_DIM), lambda h, i: (i, h))
    full = pl.BlockSpec((s, HEAD_DIM), lambda h, i: (0, h))
    wide = jax.ShapeDtypeStruct((s, n_heads * HEAD_DIM), F32)
    return pl.pallas_call(
        body, name=name, grid=(n_heads, s // ATT_TQ),
        in_specs=[q_spec, k_spec, v_spec, pl.BlockSpec((ATT_TQ, LANES), lambda h, i: (i, 0)),
                  pl.BlockSpec((None, 1, s), lambda h, i: (h, 0, 0)), blk,
                  pl.BlockSpec((None, ATT_TQ, LANES), lambda h, i: (h, i, 0)), blk],
        out_specs=[blk, full, full, pl.BlockSpec((None, 1, s), lambda h, i: (h, 0, 0))],
        out_shape=[wide, wide, wide, jax.ShapeDtypeStruct((n_heads, 1, s), F32)],
        compiler_params=_cparams("parallel", "arbitrary"),
    )(qkv, qkv, qkv, c, ct, o, lse, do)


def _cumsum_rows(x, reverse, name):
    s = x.shape[0]
    nb = s // LANES

    def body(x_ref, o_ref):
        r = lax.broadcasted_iota(jnp.int32, (LANES, LANES), 0)
        c = lax.broadcasted_iota(jnp.int32, (LANES, LANES), 1)
        tri = ((r <= c) if reverse else (r >= c)).astype(F32)

        def step(it, carry):
            b = (nb - 1 - it) if reverse else it
            off = pl.multiple_of(b * LANES, LANES)
            blk = x_ref[pl.ds(off, LANES), :]
            o_ref[pl.ds(off, LANES), :] = _dot32(tri, blk) + carry
            return carry + jnp.sum(blk, axis=0, keepdims=True)

        lax.fori_loop(0, nb, step, jnp.zeros((1, LANES), F32))

    return pl.pallas_call(body, name=name, out_shape=jax.ShapeDtypeStruct(x.shape, F32),
                          compiler_params=pltpu.CompilerParams(vmem_limit_bytes=V7X_VMEM_LIMIT))(x)


def _unit_lower_inverse(m, ri, ci):
    c = m.shape[0]
    t = jnp.where(ri == ci, 1.0, 0.0) - jnp.where(ri // 2 == ci // 2, m, 0.0)
    b = 4
    while b <= c:
        off_diag = (ri // b == ci // b) & (ri % b >= b // 2) & (ci % b < b // 2)
        t = t - _dot32(_dot32(t, jnp.where(off_diag, m, 0.0)), t)
        b *= 2
    return t


def _dn_gates(g, ri, ci):
    eye = ri == ci
    incl = ri >= ci
    g_row = jnp.sum(jnp.where(eye, g, 0.0), axis=0, keepdims=True)
    gc = jnp.sum(jnp.where(incl, g_row, 0.0), axis=1, keepdims=True)
    gc_row = jnp.sum(jnp.where(eye, gc, 0.0), axis=0, keepdims=True)
    dmat = jnp.where(incl, jnp.exp(jnp.where(incl, gc - gc_row, 0.0)), 0.0)
    gc_last = jnp.sum(g, axis=0, keepdims=True)
    return gc, dmat, jnp.exp(gc), jnp.exp(gc_last - gc), jnp.exp(gc_last)


def _dn_fwd(qkv, act, name):
    s = qkv.shape[0]
    c, d, nh = DN_CHUNK, HEAD_DIM, N_DN_HEADS
    nc = s // c

    def body(q_ref, k_ref, v_ref, act_ref, o_ref, s_ref, t_ref, state):
        @pl.when(pl.program_id(0) == 0)
        def _():
            state[...] = jnp.zeros_like(state)

        ri = lax.broadcasted_iota(jnp.int32, (c, c), 0)
        ci = lax.broadcasted_iota(jnp.int32, (c, c), 1)
        act = act_ref[...]
        for h in range(nh):
            cols = slice(h * d, (h + 1) * d)
            q, k, v = q_ref[:, cols], k_ref[:, cols], v_ref[:, cols]
            beta = _lane_col(act, LANE_BETA + h)
            _, dmat, e, r, gl = _dn_gates(_lane_col(act, LANE_DECAY + h), ri, ci)
            s0 = state[h]
            kb = beta * k
            m = jnp.where(ri > ci, _dot32(kb, k, "nt") * dmat, 0.0)
            t = _unit_lower_inverse(m, ri, ci)
            vn = _dot32(t, beta * v) - _dot32(_dot32(t, kb * e), s0)
            o_ref[:, cols] = _dot32(q * e, s0) + _dot32(_dot32(q, k, "nt") * dmat, vn)
            state[h] = gl * s0 + _dot32(k * r, vn, "tn")
            s_ref[h] = s0
            t_ref[h] = t

    wide = lambda part: pl.BlockSpec((c, nh * d), lambda n: (n, part))
    return pl.pallas_call(
        body, name=name, grid=(nc,),
        in_specs=[wide(0), wide(1), wide(2), pl.BlockSpec((c, LANES), lambda n: (n, 0))],
        out_specs=[wide(0), pl.BlockSpec((nh, None, d, d), lambda n: (0, n, 0, 0)),
                   pl.BlockSpec((nh, None, c, c), lambda n: (0, n, 0, 0))],
        out_shape=[jax.ShapeDtypeStruct((s, nh * d), F32), jax.ShapeDtypeStruct((nh, nc, d, d), F32),
                   jax.ShapeDtypeStruct((nh, nc, c, c), F32)],
        scratch_shapes=[pltpu.VMEM((nh, d, d), F32)],
        compiler_params=_cparams("arbitrary"),
    )(qkv, qkv, qkv, act)


def _dn_bwd(qkv, act, states, tinv, do, name):
    s = qkv.shape[0]
    c, d, nh = DN_CHUNK, HEAD_DIM, N_DN_HEADS
    nc = s // c

    def one_head(q, k, v, do, beta, g, s0, t, ds_out):
        ri = lax.broadcasted_iota(jnp.int32, (c, c), 0)
        ci = lax.broadcasted_iota(jnp.int32, (c, c), 1)
        eye, incl, strict = ri == ci, ri >= ci, ri > ci
        gc, dmat, e, r, gl = _dn_gates(g, ri, ci)
        rowsum = lambda x: jnp.sum(x, axis=1, keepdims=True)
        to_col = lambda row: jnp.sum(jnp.where(eye, row, 0.0), axis=1, keepdims=True)
        to_row = lambda colv: jnp.sum(jnp.where(eye, colv, 0.0), axis=0, keepdims=True)

        kb, vb = beta * k, beta * v
        kbe = kb * e
        u, w = _dot32(t, vb), _dot32(t, kbe)
        vn = u - _dot32(w, s0)
        qk = _dot32(q, k, "nt")
        p = qk * dmat
        gram = _dot32(k, k, "nt")
        kr, qe = k * r, q * e

        d_kr = _dot32(vn, ds_out, "nt")
        dvn = _dot32(kr, ds_out)
        total = lambda x: jnp.sum(rowsum(x), axis=0, keepdims=True)
        dgl = total(s0 * ds_out)
        ds_in = gl * ds_out
        dk = d_kr * r
        dr = rowsum(d_kr * k)
        d_qe = _dot32(do, s0, "nt")
        ds_in = ds_in + _dot32(qe, do, "tn")
        dp = jnp.where(incl, _dot32(do, vn, "nt"), 0.0)
        dvn = dvn + _dot32(p, do, "tn")
        dq = d_qe * e
        de = rowsum(d_qe * q)
        dqk = dp * dmat
        dq = dq + _dot32(dqk, k)
        dk = dk + _dot32(dqk, q, "tn")
        dd = dp * qk
        dw = -_dot32(dvn, s0, "nt")
        ds_in = ds_in - _dot32(w, dvn, "tn")
        dvb = _dot32(t, dvn, "tn")
        dkbe = _dot32(t, dw, "tn")
        dm = -jnp.where(strict, _dot32(dvb, u, "nt") + _dot32(dkbe, w, "nt"), 0.0)
        dbeta = rowsum(dm * gram * dmat)
        dgram = dm * beta * dmat
        dd = dd + dm * beta * gram
        dk = dk + _dot32(dgram, k) + _dot32(dgram, k, "tn")
        dkb = dkbe * e
        de = de + rowsum(dkbe * kb)
        dk = dk + beta * dkb
        dbeta = dbeta + rowsum(dkb * k) + rowsum(dvb * v)
        dv = beta * dvb
        wd = dd * dmat
        dgc = rowsum(wd) - to_col(jnp.sum(wd, axis=0, keepdims=True)) + de * e - dr * r
        dgc_last = total(dr * r) + dgl * gl
        dgc = dgc + jnp.where(ri[:, 0:1] == c - 1, dgc_last, 0.0)
        dg = jnp.sum(jnp.where(ri <= ci, to_row(dgc), 0.0), axis=1, keepdims=True)
        return dq, dk, dv, dbeta, dg, ds_in

    def body(q_ref, k_ref, v_ref, act_ref, s_ref, t_ref, do_ref, dq_ref, dk_ref, dv_ref, dact_ref, dstate):
        @pl.when(pl.program_id(0) == 0)
        def _():
            dstate[...] = jnp.zeros_like(dstate)

        act = act_ref[...]
        lane = lax.broadcasted_iota(jnp.int32, (c, LANES), 1)
        dact = jnp.zeros((c, LANES), F32)
        for h in range(nh):
            cols = slice(h * d, (h + 1) * d)
            dq, dk, dv, dbeta, dg, ds_in = one_head(
                q_ref[:, cols], k_ref[:, cols], v_ref[:, cols], do_ref[:, cols], _lane_col(act, LANE_BETA + h),
                _lane_col(act, LANE_DECAY + h), s_ref[h], t_ref[h], dstate[h])
            dstate[h] = ds_in
            dq_ref[:, cols], dk_ref[:, cols], dv_ref[:, cols] = dq, dk, dv
            dact = dact + jnp.where(lane == LANE_BETA + h, dbeta, 0.0) + jnp.where(lane == LANE_DECAY + h, dg, 0.0)
        dact_ref[...] = dact

    part = lambda p: pl.BlockSpec((c, nh * d), lambda n: (nc - 1 - n, p))
    per = lambda a, b: pl.BlockSpec((nh, None, a, b), lambda n: (0, nc - 1 - n, 0, 0))
    wide = jax.ShapeDtypeStruct((s, nh * d), F32)
    act_spec = pl.BlockSpec((c, LANES), lambda n: (nc - 1 - n, 0))
    return pl.pallas_call(
        body, name=name, grid=(nc,),
        in_specs=[part(0), part(1), part(2), act_spec, per(d, d), per(c, c), part(0)],
        out_specs=[part(0), part(0), part(0), act_spec],
        out_shape=[wide, wide, wide, jax.ShapeDtypeStruct((s, LANES), F32)],
        scratch_shapes=[pltpu.VMEM((nh, d, d), F32)],
        compiler_params=_cparams("arbitrary"),
    )(qkv, qkv, qkv, act, states, tinv, do)


EVEN_DN_QKV, EVEN_FOX_QKV, EVEN_DN_GATE, EVEN_FOX_GATE, EVEN_NARROW = 0, 1536, 3072, 3584, 4096
EVEN_WIDTH = 4224
CONV_TILE = 256
CONV_HALO = 8


def _conv_fwd(proj, w, name):
    s = proj.shape[0]
    t, cw = CONV_TILE, 3 * D_DN

    def body(cur_ref, prev_ref, w_ref, y_ref, xs):
        i = pl.program_id(0)
        xs[0:CONV_HALO, :] = jnp.where(i > 0, prev_ref[...], 0.0)
        xs[CONV_HALO:, :] = cur_ref[...]
        y = jnp.zeros((t, cw), F32)
        for tap in range(CONV_WIDTH):
            y = y + w_ref[tap:tap + 1, :] * xs[pl.ds(CONV_HALO - CONV_WIDTH + 1 + tap, t), :]
        y_ref[...] = y

    per = t // CONV_HALO
    return pl.pallas_call(
        body, name=name, grid=(s // t,),
        in_specs=[pl.BlockSpec((t, cw), lambda i: (i, 0)),
                  pl.BlockSpec((CONV_HALO, cw), lambda i: (jnp.maximum(i * per - 1, 0), 0)),
                  pl.BlockSpec((CONV_WIDTH, cw), lambda i: (0, 0))],
        out_specs=pl.BlockSpec((t, cw), lambda i: (i, 0)),
        out_shape=jax.ShapeDtypeStruct((s, cw), F32),
        scratch_shapes=[pltpu.VMEM((t + CONV_HALO, cw), F32)],
        compiler_params=_cparams("parallel"),
    )(proj, proj, w)


def _conv_bwd(proj, w, dy, name):
    s = proj.shape[0]
    t, cw = CONV_TILE, 3 * D_DN
    nt = s // t

    def body(cur_ref, prev_ref, w_ref, dy_ref, nxt_ref, dx_ref, dw_ref, xs, dys):
        i = pl.program_id(0)

        @pl.when(i == 0)
        def _():
            dw_ref[...] = jnp.zeros_like(dw_ref)

        xs[0:CONV_HALO, :] = jnp.where(i > 0, prev_ref[...], 0.0)
        xs[CONV_HALO:, :] = cur_ref[...]
        dys[0:t, :] = dy_ref[...]
        dys[t:, :] = jnp.where(i < nt - 1, nxt_ref[...], 0.0)
        dy = dy_ref[...]
        dx = jnp.zeros((t, cw), F32)
        for tap in range(CONV_WIDTH):
            dx = dx + w_ref[tap:tap + 1, :] * dys[pl.ds(CONV_WIDTH - 1 - tap, t), :]
            dw_ref[tap:tap + 1, :] += jnp.sum(dy * xs[pl.ds(CONV_HALO - CONV_WIDTH + 1 + tap, t), :], axis=0,
                                              keepdims=True)
        dx_ref[...] = dx.astype(BF16)

    per = t // CONV_HALO
    last = s // CONV_HALO - 1
    return pl.pallas_call(
        body, name=name, grid=(nt,),
        in_specs=[pl.BlockSpec((t, cw), lambda i: (i, 0)),
                  pl.BlockSpec((CONV_HALO, cw), lambda i: (jnp.maximum(i * per - 1, 0), 0)),
                  pl.BlockSpec((CONV_WIDTH, cw), lambda i: (0, 0)),
                  pl.BlockSpec((t, cw), lambda i: (i, 0)),
                  pl.BlockSpec((CONV_HALO, cw), lambda i: (jnp.minimum((i + 1) * per, last), 0))],
        out_specs=[pl.BlockSpec((t, cw), lambda i: (i, 0)), pl.BlockSpec((CONV_WIDTH, cw), lambda i: (0, 0))],
        out_shape=[jax.ShapeDtypeStruct((s, cw), BF16), jax.ShapeDtypeStruct((CONV_WIDTH, cw), F32)],
        scratch_shapes=[pltpu.VMEM((t + CONV_HALO, cw), F32), pltpu.VMEM((t + CONV_HALO, cw), F32)],
        compiler_params=_cparams("arbitrary"),
    )(proj, proj, w, dy, dy)


def _heads(x, n):
    return [x[:, HEAD_DIM * h:HEAD_DIM * (h + 1)] for h in range(n)]


def _dn_pre_fwd(y, name):
    def fn(yb):
        cs = yb * _sigmoid(yb)
        out = []
        for idx, xh in enumerate(_heads(cs, 3 * N_DN_HEADS)):
            if idx < 2 * N_DN_HEADS:
                xh = xh * lax.rsqrt(jnp.sum(xh * xh, axis=-1, keepdims=True) + EPS)
                if idx < N_DN_HEADS:
                    xh = xh * ATT_SCALE
            out.append(xh)
        return (jnp.concatenate(out, axis=1),)
    return _rowwise(fn, [y], [], [(y.shape[1], F32)], [], tile=256, name=name)[0]


def _dn_pre_bwd(y, dq, dk, dv, name):
    def fn(yb, dqb, dkb, dvb):
        sg = _sigmoid(yb)
        cs = yb * sg
        dout = _heads(dqb, N_DN_HEADS) + _heads(dkb, N_DN_HEADS) + _heads(dvb, N_DN_HEADS)
        dcs = []
        for idx, (xh, dh) in enumerate(zip(_heads(cs, 3 * N_DN_HEADS), dout)):
            if idx < 2 * N_DN_HEADS:
                if idx < N_DN_HEADS:
                    dh = dh * ATT_SCALE
                r = lax.rsqrt(jnp.sum(xh * xh, axis=-1, keepdims=True) + EPS)
                xhat = xh * r
                dh = r * (dh - xhat * jnp.sum(xhat * dh, axis=-1, keepdims=True))
            dcs.append(dh)
        return (jnp.concatenate(dcs, axis=1) * _silu_grad(yb, sg),)
    return _rowwise(fn, [y, dq, dk, dv], [], [(y.shape[1], F32)], [], tile=256, name=name)[0]


def _narrow_params(a_log, dt_bias, f_bias):
    lanes = lambda a, first: jnp.pad(a.reshape(1, -1), ((0, 0), (first, LANES - first - a.shape[0])))
    return jnp.concatenate([lanes(a_log, LANE_DECAY), lanes(dt_bias, LANE_DECAY), lanes(f_bias, LANE_FORGET),
                            jnp.zeros((5, LANES), F32)], axis=0)


def _narrow_masks(shape):
    lane = lax.broadcasted_iota(jnp.int32, shape, 1)
    is_beta = lane < LANE_DECAY
    is_decay = (lane >= LANE_DECAY) & (lane < LANE_FORGET)
    is_forget = (lane >= LANE_FORGET) & (lane < LANE_FORGET + N_FOX_HEADS)
    return is_beta, is_decay, is_forget


def _narrow_fwd(proj, params, name):
    def fn(sm, pk):
        is_beta, is_decay, is_forget = _narrow_masks(sm.shape)
        g = -jnp.exp(pk[0:1, :]) * _softplus(sm + pk[1:2, :])
        logf = -_softplus(-(sm + pk[2:3, :]))
        return (jnp.where(is_beta, _sigmoid(sm), jnp.where(is_decay, g, jnp.where(is_forget, logf, 0.0))),)
    return _rowwise(fn, [(proj, LANES, EVEN_NARROW // LANES)], [params], [(LANES, F32)], [], tile=512, name=name)[0]


def _narrow_bwd(proj, params, act, dact, dlogf, name):
    def fn(sm, ab, da, dl, pk):
        is_beta, is_decay, is_forget = _narrow_masks(sm.shape)
        db = jnp.where(is_forget, dl, da)
        d_beta = db * ab * (1.0 - ab)
        d_decay = db * (-jnp.exp(pk[0:1, :])) * _sigmoid(sm + pk[1:2, :])
        d_forget = db * _sigmoid(-(sm + pk[2:3, :]))
        dsm = jnp.where(is_beta, d_beta, jnp.where(is_decay, d_decay, jnp.where(is_forget, d_forget, 0.0)))
        col = lambda x: jnp.sum(x, axis=0, keepdims=True)
        return (dsm, col(jnp.where(is_decay, db * ab, 0.0)), col(jnp.where(is_decay, dsm, 0.0)),
                col(jnp.where(is_forget, dsm, 0.0)))
    return _rowwise(fn, [(proj, LANES, EVEN_NARROW // LANES), act, dact, dlogf], [params], [(LANES, BF16)],
                    [(1, LANES)] * 3, tile=512, name=name)


def _head_rms(xh):
    r = lax.rsqrt(jnp.mean(xh * xh, axis=-1, keepdims=True) + EPS)
    return xh * r, r


def _fox_pre_fwd(proj, qg, kg, name):
    def fn(pf, qgb, kgb):
        out = []
        for idx, xh in enumerate(_heads(pf, 3 * N_FOX_HEADS)):
            if idx < 2 * N_FOX_HEADS:
                xh = _head_rms(xh)[0] * (qgb if idx < N_FOX_HEADS else kgb)
            out.append(xh)
        return (jnp.concatenate(out, axis=1),)
    return _rowwise(fn, [(proj, 3 * D_FOX, EVEN_FOX_QKV // (3 * D_FOX))], [qg, kg], [(3 * D_FOX, BF16)], [],
                    tile=256, name=name)[0]


def _fox_pre_bwd(proj, qg, kg, dq, dk, dv, name):
    def fn(pf, dqb, dkb, dvb, qgb, kgb):
        dout = _heads(dqb, N_FOX_HEADS) + _heads(dkb, N_FOX_HEADS) + _heads(dvb, N_FOX_HEADS)
        dg = [jnp.zeros((1, HEAD_DIM), F32), jnp.zeros((1, HEAD_DIM), F32)]
        dx = []
        for idx, (xh, dh) in enumerate(zip(_heads(pf, 3 * N_FOX_HEADS), dout)):
            if idx < 2 * N_FOX_HEADS:
                which = 0 if idx < N_FOX_HEADS else 1
                xhat, r = _head_rms(xh)
                dg[which] = dg[which] + jnp.sum(dh * xhat, axis=0, keepdims=True)
                dxh = dh * (qgb if which == 0 else kgb)
                dh = r * (dxh - xhat * jnp.mean(dxh * xhat, axis=-1, keepdims=True))
            dx.append(dh)
        return jnp.concatenate(dx, axis=1), dg[0], dg[1]
    return _rowwise(fn, [(proj, 3 * D_FOX, EVEN_FOX_QKV // (3 * D_FOX)), dq, dk, dv], [qg, kg],
                    [(3 * D_FOX, BF16)], [(1, HEAD_DIM)] * 2, tile=256, name=name)


def _mix_gate_fwd(proj, o_dn, o_fox, ng, name):
    def fn(gd, gf, od, of, ngb):
        dn = [_head_rms(xh)[0] * ngb for xh in _heads(od, N_DN_HEADS)]
        return (jnp.concatenate([jnp.concatenate(dn, axis=1) * gd * _sigmoid(gd), of * _sigmoid(gf)], axis=1),)
    return _rowwise(fn, [(proj, D_DN, EVEN_DN_GATE // D_DN), (proj, D_FOX, EVEN_FOX_GATE // D_FOX), o_dn, o_fox],
                    [ng], [(D_DN + D_FOX, BF16)], [], tile=256, name=name)[0]


def _mix_gate_bwd(proj, o_dn, o_fox, ng, dom, name):
    def fn(gd, gf, od, of, dm, ngb):
        d_dn, d_fox = dm[:, :D_DN], dm[:, D_DN:]
        sgd, sgf = _sigmoid(gd), _sigmoid(gf)
        don = d_dn * gd * sgd
        dng = jnp.zeros((1, HEAD_DIM), F32)
        dod, normed = [], []
        for xh, dh in zip(_heads(od, N_DN_HEADS), _heads(don, N_DN_HEADS)):
            xhat, r = _head_rms(xh)
            dng = dng + jnp.sum(dh * xhat, axis=0, keepdims=True)
            dxh = dh * ngb
            dod.append(r * (dxh - xhat * jnp.mean(dxh * xhat, axis=-1, keepdims=True)))
            normed.append(xhat * ngb)
        d_gd = d_dn * jnp.concatenate(normed, axis=1) * _silu_grad(gd, sgd)
        d_gf = d_fox * of * sgf * (1.0 - sgf)
        return jnp.concatenate(dod, axis=1), d_fox * sgf, d_gd, d_gf, dng
    return _rowwise(fn, [(proj, D_DN, EVEN_DN_GATE // D_DN), (proj, D_FOX, EVEN_FOX_GATE // D_FOX), o_dn, o_fox, dom],
                    [ng], [(D_DN, F32), (D_FOX, F32), (D_DN, BF16), (D_FOX, BF16)], [(1, HEAD_DIM)], tile=256,
                    name=name)


def _loss_grad(y, target, name):
    d = y.shape[1]

    def fn(yb, tb):
        diff = yb - tb
        part = jnp.sum(jnp.sum(diff * diff, axis=1, keepdims=True), axis=0, keepdims=True) * (0.5 / d)
        g = diff * (1.0 / d)
        return g, g, part
    return _rowwise(fn, [y, target], [], [(d, F32), (d, BF16)], [(1, 1)], tile=512, name=name)


_REF_EVEN = {"dn_qkv": (0, 1536), "dn_gate": (1536, 2048), "dn_ba": (2048, 2056), "fox_qkv": (2056, 3592),
             "fox_gate": (3592, 4104), "f_pre": (4104, 4108)}
D_IN_EVEN = 4108


def _even_to_kernel_layout(w):
    cut = lambda name: w[..., _REF_EVEN[name][0]:_REF_EVEN[name][1]]
    pad = jnp.zeros(w.shape[:-1] + (EVEN_WIDTH - EVEN_NARROW - 12,), w.dtype)
    return jnp.concatenate([cut("dn_qkv"), cut("fox_qkv"), cut("dn_gate"), cut("fox_gate"), cut("dn_ba"),
                            cut("f_pre"), pad], axis=-1)


def _even_from_kernel_layout(g):
    return jnp.concatenate([g[..., EVEN_DN_QKV:EVEN_FOX_QKV], g[..., EVEN_DN_GATE:EVEN_FOX_GATE],
                            g[..., EVEN_NARROW:EVEN_NARROW + 8], g[..., EVEN_FOX_QKV:EVEN_DN_GATE],
                            g[..., EVEN_FOX_GATE:EVEN_NARROW], g[..., EVEN_NARROW + 8:EVEN_NARROW + 12]], axis=-1)


def _forget_rows(c):
    return c[:, LANE_FORGET:LANE_FORGET + N_FOX_HEADS].T.reshape(N_FOX_HEADS, 1, c.shape[0])


def _forget_lanes(rows):
    s = rows.shape[2]
    return jnp.pad(rows.reshape(-1, s).T, ((0, 0), (LANE_FORGET, LANES - LANE_FORGET - N_FOX_HEADS)))


def _even_fwd(x, gain, w_in, w_out, j, p, tag):
    h = _rms_fwd(x, gain, f"{tag}_norm")
    proj = _mm(h, w_in, "nn", tm=512, tn=384, out_dtype=F32, name=f"{tag}_in", b_lead=(j,))
    y = _conv_fwd(proj, p["conv_w"], f"{tag}_conv")
    dn_qkv = _dn_pre_fwd(y, f"{tag}_dn_pre")
    act = _narrow_fwd(proj, p["narrow"], f"{tag}_narrow")
    o_dn, states, tinv = _dn_fwd(dn_qkv, act, f"{tag}_delta")
    fox_qkv = _fox_pre_fwd(proj, p["q_g"], p["k_g"], f"{tag}_fox_pre")
    c = _cumsum_rows(act, False, f"{tag}_cumsum")
    ct = _forget_rows(c)
    o_fox, lse = _fox_fwd(fox_qkv, c, ct, f"{tag}_fox")
    om = _mix_gate_fwd(proj, o_dn, o_fox, p["dn_norm_g"], f"{tag}_gate")
    x2 = _mm(om, w_out, "nn", tm=512, tn=512, out_dtype=F32, name=f"{tag}_out", residual=x, b_lead=(j,))
    return x2, (x, h, proj, y, dn_qkv, act, states, tinv, o_dn, fox_qkv, c, ct, o_fox, lse, om)


def _even_bwd(dxo, dxo16, saved, gain, w_in, w_out, j, p, tag, g_in, g_out):
    x, h, proj, y, dn_qkv, act, states, tinv, o_dn, fox_qkv, c, ct, o_fox, lse, om = saved
    d = x.shape[1]
    dom = _mm(dxo16, w_out, "nt", tm=512, tn=512, out_dtype=F32, name=f"{tag}_out_bwd", b_lead=(j,))
    g_out = _mm(om, dxo16, "tn", tm=512, tn=d, out_dtype=F32, name=f"{tag}_out_dw", into=(g_out, j))
    d_odn, d_ofox, d_gd, d_gf, d_ng = _mix_gate_bwd(proj, o_dn, o_fox, p["dn_norm_g"], dom, f"{tag}_gate_bwd")
    dq, dk, dv, dct = _fox_bwd(fox_qkv, c, ct, o_fox, lse, d_ofox, f"{tag}_fox_bwd")
    d_fox_qkv, d_qg, d_kg = _fox_pre_bwd(proj, p["q_g"], p["k_g"], dq, dk, dv, f"{tag}_fox_pre_bwd")
    dlogf = _cumsum_rows(_forget_lanes(dct), True, f"{tag}_cumsum_bwd")
    dq, dk, dv, dact = _dn_bwd(dn_qkv, act, states, tinv, d_odn, f"{tag}_delta_bwd")
    dy = _dn_pre_bwd(y, dq, dk, dv, f"{tag}_dn_pre_bwd")
    d_dn_qkv, d_conv = _conv_bwd(proj, p["conv_w"], dy, f"{tag}_conv_bwd")
    d_narrow, s_alog, s_dt, s_fb = _narrow_bwd(proj, p["narrow"], act, dact, dlogf, f"{tag}_narrow_bwd")
    dproj = jnp.concatenate([d_dn_qkv, d_fox_qkv, d_gd, d_gf, d_narrow], axis=1)
    dh = _mm(dproj, w_in, "nt", tm=512, tn=512, out_dtype=F32, name=f"{tag}_in_bwd", b_lead=(j,))
    g_in = _mm(h, dproj, "tn", tm=512, tn=384, out_dtype=F32, name=f"{tag}_in_dw", into=(g_in, j))
    dx, dx16, d_gain = _rms_bwd(x, gain, dh, dxo, f"{tag}_norm_bwd")
    small = {"conv_w": d_conv, "a_log": s_alog, "dt_bias": s_dt, "f_bias": s_fb, "dn_norm_g": d_ng, "q_g": d_qg,
             "k_g": d_kg}
    return dx, dx16, d_gain, small, g_in, g_out


def _odd_fwd(x, gain, w_in, w_out, j, tag):
    h = _rms_fwd(x, gain, f"{tag}_norm")
    qkv = _mm(h, w_in, "nn", tm=512, tn=768, out_dtype=BF16, name=f"{tag}_in", b_lead=(j,))
    o16, o32 = _sb_fwd(qkv, N_SB_HEADS, f"{tag}_sb")
    x2 = _mm(o16, w_out, "nn", tm=512, tn=512, out_dtype=F32, name=f"{tag}_out", residual=x, b_lead=(j,))
    return x2, (x, h, qkv, o16, o32)


def _odd_bwd(dxo, dxo16, saved, gain, w_in, w_out, j, tag, g_in, g_out):
    x, h, qkv, o16, o32 = saved
    d = x.shape[1]
    do = _mm(dxo16, w_out, "nt", tm=512, tn=512, out_dtype=BF16, name=f"{tag}_out_bwd", b_lead=(j,))
    g_out = _mm(o16, dxo16, "tn", tm=512, tn=d, out_dtype=F32, name=f"{tag}_out_dw", into=(g_out, j))
    dq, dk, dv = _sb_bwd(qkv, o32, do, N_SB_HEADS, f"{tag}_sb_bwd")
    dqkv = jnp.concatenate([dq, dk.astype(BF16), dv.astype(BF16)], axis=1)
    dh = _mm(dqkv, w_in, "nt", tm=512, tn=512, out_dtype=F32, name=f"{tag}_in_bwd", b_lead=(j,))
    g_in = _mm(h, dqkv, "tn", tm=512, tn=768, out_dtype=F32, name=f"{tag}_in_dw", into=(g_in, j))
    dx, dx16, d_gain = _rms_bwd(x, gain, dh, dxo, f"{tag}_norm_bwd")
    return dx, dx16, d_gain, g_in, g_out


def _forward_backward(x, target, w):
    depth = w["norm_ffn1"].shape[0]
    row = lambda a, l: a[l][None]

    def even_small(j):
        return {"conv_w": w["dn_conv_w"][j], "narrow": _narrow_params(w["dn_a_log"][j], w["dn_dt_bias"][j],
                                                                     w["fox_f_bias"][j]),
                "dn_norm_g": row(w["dn_norm_g"], j), "q_g": row(w["fox_q_norm_g"], j),
                "k_g": row(w["fox_k_norm_g"], j)}

    saved = []
    for l in range(depth):
        x, s1 = _ffn_fwd(x, row(w["norm_ffn1"], l), w["ffn1_w_gu"], w["ffn1_w_down"], l, "ffn1")
        if l % 2 == 0:
            x, s2 = _even_fwd(x, row(w["norm_mix"], l), w["w_in_even"], w["w_out_even"], l // 2, even_small(l // 2),
                              "even")
        else:
            x, s2 = _odd_fwd(x, row(w["norm_mix"], l), w["w_in_odd"], w["w_out_odd"], l // 2, "odd")
        x, s3 = _ffn_fwd(x, row(w["norm_ffn2"], l), w["ffn2_w_gu"], w["ffn2_w_down"], l, "ffn2")
        saved.append((s1, s2, s3))

    dx, dx16, loss = _loss_grad(x, target, "loss")

    big = {k: lax.empty(w[k].shape, F32) for k in ("ffn1_w_gu", "ffn1_w_down", "ffn2_w_gu", "ffn2_w_down",
                                                   "w_in_even", "w_out_even", "w_in_odd", "w_out_odd")}
    d_norm = {k: [None] * depth for k in ("norm_ffn1", "norm_mix", "norm_ffn2")}
    d_even = [None] * ((depth + 1) // 2)
    for l in reversed(range(depth)):
        s1, s2, s3 = saved[l]
        dx, dx16, d_norm["norm_ffn2"][l], big["ffn2_w_gu"], big["ffn2_w_down"] = _ffn_bwd(
            dx, dx16, s3, row(w["norm_ffn2"], l), w["ffn2_w_gu"], w["ffn2_w_down"], l, "ffn2", big["ffn2_w_gu"],
            big["ffn2_w_down"])
        if l % 2 == 0:
            dx, dx16, d_norm["norm_mix"][l], d_even[l // 2], big["w_in_even"], big["w_out_even"] = _even_bwd(
                dx, dx16, s2, row(w["norm_mix"], l), w["w_in_even"], w["w_out_even"], l // 2, even_small(l // 2),
                "even", big["w_in_even"], big["w_out_even"])
        else:
            dx, dx16, d_norm["norm_mix"][l], big["w_in_odd"], big["w_out_odd"] = _odd_bwd(
                dx, dx16, s2, row(w["norm_mix"], l), w["w_in_odd"], w["w_out_odd"], l // 2, "odd", big["w_in_odd"],
                big["w_out_odd"])
        dx, dx16, d_norm["norm_ffn1"][l], big["ffn1_w_gu"], big["ffn1_w_down"] = _ffn_bwd(
            dx, dx16, s1, row(w["norm_ffn1"], l), w["ffn1_w_gu"], w["ffn1_w_down"], l, "ffn1", big["ffn1_w_gu"],
            big["ffn1_w_down"])

    small = {k: jnp.concatenate(v, axis=0) for k, v in d_norm.items()}
    dec = slice(LANE_DECAY, LANE_DECAY + N_DN_HEADS)
    fgt = slice(LANE_FORGET, LANE_FORGET + N_FOX_HEADS)
    small["dn_conv_w"] = jnp.stack([e["conv_w"] for e in d_even])
    small["dn_a_log"] = jnp.concatenate([e["a_log"][:, dec] for e in d_even], axis=0)
    small["dn_dt_bias"] = jnp.concatenate([e["dt_bias"][:, dec] for e in d_even], axis=0)
    small["fox_f_bias"] = jnp.concatenate([e["f_bias"][:, fgt] for e in d_even], axis=0)
    small["dn_norm_g"] = jnp.concatenate([e["dn_norm_g"] for e in d_even], axis=0)
    small["fox_q_norm_g"] = jnp.concatenate([e["q_g"] for e in d_even], axis=0)
    small["fox_k_norm_g"] = jnp.concatenate([e["k_g"] for e in d_even], axis=0)
    return loss, dx, big, small


MESH = pl.DeviceIdType.MESH
ANY = pl.BlockSpec(memory_space=pl.ANY)


def _place():
    x, y, c = lax.axis_index("x"), lax.axis_index("y"), lax.axis_index("c")
    return x, y, c, [(1 - x, y), (x, 1 - y), (1 - x, 1 - y)]


def _remote(src, dst, send_sem, recv_sem, to):
    return pltpu.make_async_remote_copy(src_ref=src, dst_ref=dst, send_sem=send_sem, recv_sem=recv_sem,
                                        device_id=to, device_id_type=MESH)


def _aligned(start, multiple):
    return start if isinstance(start, int) else pl.multiple_of(start, multiple)


def _quarter(ref, kind, chip, half, rows, cols):
    k = 2 * chip[0] + chip[1]
    hr = rows // 2
    assert hr % 16 == 0 and cols % LANES == 0
    if kind == "col":
        return ref.at[:, pl.ds(_aligned(half * hr, 16), hr), pl.ds(_aligned(k * cols, LANES), cols)]
    return ref.at[:, pl.ds(_aligned(k * rows + half * hr, 16), hr), :]


def _place_quarter(shard, kind, kc, name):
    l, rows, cols = shard.shape
    tr = rows
    while tr * cols * 4 > (2 << 20) and tr % 32 == 0:
        tr //= 2
    nr = rows // tr
    if kind == "col":
        out_spec = pl.BlockSpec((None, tr, cols), lambda li, i, kc_ref: (li, i, kc_ref[0]))
        out_shape = (l, rows, 4 * cols)
    else:
        out_spec = pl.BlockSpec((None, tr, cols), lambda li, i, kc_ref: (li, kc_ref[0] * nr + i, 0))
        out_shape = (l, 4 * rows, cols)

    def body(kc_ref, x_ref, o_ref):
        o_ref[...] = x_ref[...].astype(BF16)

    return pl.pallas_call(
        body, name=name,
        grid_spec=pltpu.PrefetchScalarGridSpec(
            num_scalar_prefetch=1, grid=(l, nr),
            in_specs=[pl.BlockSpec((None, tr, cols), lambda li, i, kc_ref: (li, i, 0))], out_specs=out_spec),
        out_shape=jax.ShapeDtypeStruct(out_shape, BF16),
        compiler_params=_cparams("parallel", "parallel"),
    )(kc, shard)


def _gather_weights(wholes, kinds):
    n = len(wholes)

    def dims(ref, kind):
        _, r, cc = ref.shape
        return (r, cc // 4) if kind == "col" else (r // 4, cc)

    def body(*refs):
        bufs = refs[n:2 * n]
        send_sems, recv_sems = refs[2 * n:]
        x, y, c, chips = _place()
        sibling = (x, y, 1 - c)
        first, passed = [], []
        for t in range(n):
            rows, cols = dims(bufs[t], kinds[t])
            mine = _quarter(bufs[t], kinds[t], (x, y), c, rows, cols)
            for j, chip in enumerate(chips):
                cp = _remote(mine, mine, send_sems.at[t, j], recv_sems.at[t, j], (*chip, c))
                cp.start()
                first.append(cp)
        for j, chip in enumerate(chips):
            for t in range(n):
                rows, cols = dims(bufs[t], kinds[t])
                got = _quarter(bufs[t], kinds[t], chip, c, rows, cols)
                _remote(got, got, send_sems.at[t, j], recv_sems.at[t, j], (*chip, c)).wait_recv()
                cp = _remote(got, got, send_sems.at[t, 3 + j], recv_sems.at[t, 3 + j], sibling)
                cp.start()
                passed.append(cp)
        for j, chip in enumerate(chips):
            for t in range(n):
                rows, cols = dims(bufs[t], kinds[t])
                got = _quarter(bufs[t], kinds[t], chip, 1 - c, rows, cols)
                _remote(got, got, send_sems.at[t, 3 + j], recv_sems.at[t, 3 + j], sibling).wait_recv()
        for cp in first + passed:
            cp.wait_send()

    return pl.pallas_call(
        body, name="gather_weights", in_specs=[ANY] * n, out_specs=[ANY] * n,
        out_shape=[jax.ShapeDtypeStruct(a.shape, a.dtype) for a in wholes],
        input_output_aliases={t: t for t in range(n)},
        scratch_shapes=[pltpu.SemaphoreType.DMA((n, 6)), pltpu.SemaphoreType.DMA((n, 6))],
        compiler_params=pltpu.CompilerParams(has_side_effects=True),
    )(*wholes)


def _canonical(a, kind):
    l, r, c = a.shape
    return a.reshape(l, 1, r, c) if kind == "col" else a.reshape(l, 4, r // 4, c)


def _rs_sibling(parts):
    n = len(parts)

    def body(*refs):
        ins, outs = refs[:n], refs[n:2 * n]
        send_sems, recv_sems = refs[2 * n:]
        x, y, c, _ = _place()
        copies = []
        for t in range(n):
            hr = ins[t].shape[2] // 2
            src = ins[t].at[:, :, pl.ds(pl.multiple_of((1 - c) * hr, 8), hr), :]
            cp = _remote(src, outs[t], send_sems.at[t], recv_sems.at[t], (x, y, 1 - c))
            cp.start()
            copies.append(cp)
        for cp in copies:
            cp.wait()

    half = lambda a: jax.ShapeDtypeStruct(a.shape[:2] + (a.shape[2] // 2, a.shape[3]), a.dtype)
    return pl.pallas_call(
        body, name="reduce_sibling", in_specs=[ANY] * n, out_specs=[ANY] * n, out_shape=[half(a) for a in parts],
        scratch_shapes=[pltpu.SemaphoreType.DMA((n,)), pltpu.SemaphoreType.DMA((n,))],
        compiler_params=pltpu.CompilerParams(has_side_effects=True),
    )(*parts)


def _add_tile(rows, cols):
    tc = cols if cols <= 1536 else cols // 4
    tr = rows
    while tr * tc * 4 > (1 << 20) and tr % 16 == 0:
        tr //= 2
    return tr, tc


def _rs_add_sibling(part, got, c, name):
    l, a, hr, cols = got.shape
    tr, tc = _add_tile(hr, cols)
    nr = hr // tr

    def body(c_ref, p_ref, g_ref, o32_ref, o16_ref):
        s = p_ref[...] + g_ref[...]
        o32_ref[...] = s
        o16_ref[...] = s.astype(BF16)

    blk = (None, None, tr, tc)
    spec = pl.BlockSpec(blk, lambda li, ai, i, j, c_ref: (li, ai, i, j))
    return pl.pallas_call(
        body, name=name,
        grid_spec=pltpu.PrefetchScalarGridSpec(
            num_scalar_prefetch=1, grid=(l, a, nr, cols // tc),
            in_specs=[pl.BlockSpec(blk, lambda li, ai, i, j, c_ref: (li, ai, c_ref[0] * nr + i, j)), spec],
            out_specs=[spec, spec]),
        out_shape=[jax.ShapeDtypeStruct(got.shape, F32), jax.ShapeDtypeStruct(got.shape, BF16)],
        compiler_params=_cparams("parallel", "parallel", "parallel", "parallel"),
    )(c, part, got)


def _quarter4(ref, kind, chip, cols):
    k = 2 * chip[0] + chip[1]
    if kind == "col":
        return ref.at[:, :, :, pl.ds(pl.multiple_of(k * cols, LANES), cols)]
    return ref.at[:, pl.ds(k, 1), :, :]


def _rs_chips(sums16, kinds):
    n = len(sums16)

    def qshape(a, kind):
        l, na, hr, cols = a.shape
        return (l, 1, hr, cols // 4 if kind == "col" else cols)

    def body(*refs):
        ins, outs = refs[:n], refs[n:2 * n]
        send_sems, recv_sems = refs[2 * n:]
        x, y, c, chips = _place()
        copies = []
        for t in range(n):
            cols = qshape(ins[t], kinds[t])[3]
            for j, chip in enumerate(chips):
                cp = _remote(_quarter4(ins[t], kinds[t], chip, cols), outs[t].at[j], send_sems.at[t, j],
                             recv_sems.at[t, j], (*chip, c))
                cp.start()
                copies.append(cp)
        for cp in copies:
            cp.wait()

    return pl.pallas_call(
        body, name="reduce_chips", in_specs=[ANY] * n, out_specs=[ANY] * n,
        out_shape=[jax.ShapeDtypeStruct((3,) + qshape(a, k), a.dtype) for a, k in zip(sums16, kinds)],
        scratch_shapes=[pltpu.SemaphoreType.DMA((n, 3)), pltpu.SemaphoreType.DMA((n, 3))],
        compiler_params=pltpu.CompilerParams(has_side_effects=True),
    )(*sums16)


def _rs_add_chips(sum32, got, kind, kc, name):
    _, l, _, hr, cols = got.shape
    tr, _ = _add_tile(hr, cols)
    nr = hr // tr
    k_arr, c_arr = kc
    if kind == "col":
        own = pl.BlockSpec((None, None, tr, cols), lambda li, i, k_ref, c_ref: (li, 0, i, k_ref[0]))
    else:
        own = pl.BlockSpec((None, None, tr, cols), lambda li, i, k_ref, c_ref: (li, k_ref[0], i, 0))

    def body(k_ref, c_ref, own_ref, got_ref, o_ref):
        o_ref[...] = ((own_ref[...] + got_ref[0].astype(F32)) + got_ref[1].astype(F32)) + got_ref[2].astype(F32)

    return pl.pallas_call(
        body, name=name,
        grid_spec=pltpu.PrefetchScalarGridSpec(
            num_scalar_prefetch=2, grid=(l, nr),
            in_specs=[own, pl.BlockSpec((3, None, None, tr, cols), lambda li, i, k_ref, c_ref: (0, li, 0, i, 0))],
            out_specs=pl.BlockSpec((None, tr, cols), lambda li, i, k_ref, c_ref: (li, c_ref[0] * nr + i, 0))),
        out_shape=jax.ShapeDtypeStruct((l, 2 * hr, cols), F32),
        compiler_params=_cparams("parallel", "parallel"),
    )(k_arr, c_arr, sum32, got)


def _rs_finish(quarters):
    n = len(quarters)

    def body(*refs):
        bufs = refs[n:2 * n]
        send_sems, recv_sems = refs[2 * n:]
        x, y, c, _ = _place()
        copies = []
        for t in range(n):
            hr = bufs[t].shape[1] // 2
            mine = bufs[t].at[:, pl.ds(pl.multiple_of(c * hr, 8), hr), :]
            cp = _remote(mine, mine, send_sems.at[t], recv_sems.at[t], (x, y, 1 - c))
            cp.start()
            copies.append(cp)
        for cp in copies:
            cp.wait()

    return pl.pallas_call(
        body, name="reduce_finish", in_specs=[ANY] * n, out_specs=[ANY] * n,
        out_shape=[jax.ShapeDtypeStruct(a.shape, a.dtype) for a in quarters],
        input_output_aliases={t: t for t in range(n)},
        scratch_shapes=[pltpu.SemaphoreType.DMA((n,)), pltpu.SemaphoreType.DMA((n,))],
        compiler_params=pltpu.CompilerParams(has_side_effects=True),
    )(*quarters)


def _reduce_scatter(parts, kinds, tags):
    x, y, c = lax.axis_index("x"), lax.axis_index("y"), lax.axis_index("c")
    c_arr = jnp.reshape(c, (1,)).astype(jnp.int32)
    k_arr = (jnp.reshape(2 * x + y, (1,)).astype(jnp.int32), c_arr)
    canon = [_canonical(p, kind) for p, kind in zip(parts, kinds)]
    from_sibling = _rs_sibling(canon)
    sums = [_rs_add_sibling(p, g, c_arr, f"reduce_add_sibling_{tag}") for p, g, tag in zip(canon, from_sibling, tags)]
    from_chips = _rs_chips([s16 for _, s16 in sums], kinds)
    halves = [_rs_add_chips(s32, g, kind, k_arr, f"reduce_add_chips_{tag}")
              for (s32, _), g, kind, tag in zip(sums, from_chips, kinds, tags)]
    return _rs_finish(halves)


SMALL_PEERS = 7


def _small_exchange(pack):
    rows = pack.shape[0]

    def body(p_ref, slots_ref, total_ref, send_sems, recv_sems):
        x, y, c, _ = _place()
        me = 4 * x + 2 * y + c
        slots_ref[me] = p_ref[...]
        copies = []
        for p in range(1, SMALL_PEERS + 1):
            px, py, pc = (p >> 2) & 1, (p >> 1) & 1, p & 1
            peer = (1 - x if px else x, 1 - y if py else y, 1 - c if pc else c)
            cp = _remote(p_ref, slots_ref.at[me], send_sems.at[p - 1], recv_sems.at[p - 1], peer)
            cp.start()
            copies.append(cp)
        for cp in copies:
            cp.wait()
        total = slots_ref[0]
        for i in range(1, SMALL_PEERS + 1):
            total = total + slots_ref[i]
        total_ref[...] = total

    vmem = pl.BlockSpec(memory_space=pltpu.VMEM)
    return pl.pallas_call(
        body, name="small_exchange", in_specs=[vmem], out_specs=[vmem, vmem],
        out_shape=[jax.ShapeDtypeStruct((SMALL_PEERS + 1, rows, LANES), F32), jax.ShapeDtypeStruct((rows, LANES), F32)],
        scratch_shapes=[pltpu.SemaphoreType.DMA((SMALL_PEERS,)), pltpu.SemaphoreType.DMA((SMALL_PEERS,))],
        compiler_params=pltpu.CompilerParams(has_side_effects=True),
    )(pack)


def _pack(arrays):
    rows = []
    for a in arrays:
        flat = a.reshape(-1).astype(F32)
        rows.append(jnp.pad(flat, (0, (-flat.shape[0]) % LANES)).reshape(-1, LANES))
    out = jnp.concatenate(rows, axis=0)
    return jnp.pad(out, ((0, (-out.shape[0]) % 8), (0, 0)))


def _unpack(pack, shapes):
    out, r = [], 0
    for sh in shapes:
        size = math.prod(sh)
        nr = -(-size // LANES)
        out.append(pack[r:r + nr].reshape(-1)[:size].reshape(sh))
        r += nr
    return out


def _adamw(w, g, m, v, name):
    shape = w.shape
    to2d = lambda a: a.reshape(-1, shape[-1])
    rows = math.prod(shape[:-1])
    tile = 256 if rows % 256 == 0 else rows

    def fn(wb, gb, mb, vb):
        m2 = ADAM_B1 * mb + (1.0 - ADAM_B1) * gb
        v2 = ADAM_B2 * vb + (1.0 - ADAM_B2) * (gb * gb)
        m_hat = m2 / (1.0 - ADAM_B1 ** ADAM_STEP)
        v_hat = v2 / (1.0 - ADAM_B2 ** ADAM_STEP)
        return -ADAM_LR * (m_hat / (jnp.sqrt(v_hat) + ADAM_EPS) + ADAM_WD * wb), m2, v2

    res = _rowwise(fn, [to2d(w), to2d(g), to2d(m), to2d(v)], [], [(shape[-1], F32)] * 3, [], tile=tile, name=name)
    return [r.reshape(shape) for r in res]


BIG = (("ffn1_w_gu", "col"), ("ffn1_w_down", "row"), ("w_in_even", "col"), ("w_out_even", "row"),
       ("w_in_odd", "col"), ("w_out_odd", "row"), ("ffn2_w_gu", "col"), ("ffn2_w_down", "row"))
SMALL = ("norm_ffn1", "norm_mix", "dn_conv_w", "dn_a_log", "dn_dt_bias", "dn_norm_g", "fox_q_norm_g", "fox_k_norm_g",
         "fox_f_bias", "norm_ffn2")
WEIGHTS = ("norm_ffn1", "ffn1_w_gu", "ffn1_w_down", "norm_mix", "w_in_even", "dn_conv_w", "dn_a_log", "dn_dt_bias",
           "dn_norm_g", "fox_q_norm_g", "fox_k_norm_g", "fox_f_bias", "w_out_even", "w_in_odd", "w_out_odd",
           "norm_ffn2", "ffn2_w_gu", "ffn2_w_down")
EVEN_QUARTER = 1027
EVEN_QUARTER_PAD = 1152


def _step(x, target, w, m, v):
    k = 2 * lax.axis_index("x") + lax.axis_index("y")
    n_conv = w["dn_conv_w"].shape[2]

    kc = jnp.reshape(k, (1,)).astype(jnp.int32)
    placed = []
    for name, kind in BIG:
        a = w[name]
        if name == "w_in_even":
            a = jnp.pad(a, ((0, 0), (0, 0), (0, EVEN_QUARTER_PAD - EVEN_QUARTER)))
        placed.append(_place_quarter(a, kind, kc, f"place_{name}"))
    whole = dict(zip([n for n, _ in BIG], _gather_weights(placed, [kind for _, kind in BIG])))
    padded = whole["w_in_even"]
    ref_order = jnp.concatenate([padded[..., q * EVEN_QUARTER_PAD:q * EVEN_QUARTER_PAD + EVEN_QUARTER]
                                 for q in range(4)], axis=-1)
    whole["w_in_even"] = _even_to_kernel_layout(ref_order)
    conv_slots, _ = _small_exchange(_pack([w["dn_conv_w"]]))
    conv_rows = math.prod(w["dn_conv_w"].shape) // LANES
    quarters = [conv_slots[2 * q, :conv_rows].reshape(w["dn_conv_w"].shape) for q in range(4)]
    whole["dn_conv_w"] = jnp.concatenate(quarters, axis=-1)
    for name in SMALL:
        if name != "dn_conv_w":
            whole[name] = w[name]

    loss, dx, big, small = _forward_backward(x, target, whole)

    g_even = _even_from_kernel_layout(big["w_in_even"])
    big["w_in_even"] = jnp.concatenate(
        [jnp.pad(g_even[..., q * EVEN_QUARTER:(q + 1) * EVEN_QUARTER],
                 ((0, 0), (0, 0), (0, EVEN_QUARTER_PAD - EVEN_QUARTER))) for q in range(4)], axis=-1)
    names = [n for n, _ in BIG]
    reduced = dict(zip(names, _reduce_scatter([big[n] for n in names], [kind for _, kind in BIG], names)))
    reduced["w_in_even"] = reduced["w_in_even"][..., :EVEN_QUARTER]
    _, small_sum = _small_exchange(_pack([small[n] for n in SMALL]))
    grads = dict(zip(SMALL, _unpack(small_sum, [small[n].shape for n in SMALL])))
    grads["dn_conv_w"] = lax.dynamic_slice_in_dim(grads["dn_conv_w"], k * n_conv, n_conv, axis=2)
    grads.update(reduced)

    delta, new_m, new_v = {}, {}, {}
    for name, _ in BIG:
        delta[name], new_m[name], new_v[name] = _adamw(w[name], grads[name], m[name], v[name], f"adamw_{name}")
    packs = [_pack([d[n] for n in SMALL]) for d in (w, grads, m, v)]
    shapes = [w[n].shape for n in SMALL]
    for out, res in zip((delta, new_m, new_v), _adamw(*packs, "adamw_small")):
        out.update(zip(SMALL, _unpack(res, shapes)))
    total_loss = lax.psum(loss[0, 0], ("x", "y", "c"))
    return total_loss, dx, grads, delta, new_m, new_v


def kernel(x, norm_ffn1, ffn1_w_gu, ffn1_w_down, norm_mix, w_in_even, dn_conv_w, dn_a_log, dn_dt_bias, dn_norm_g, fox_q_norm_g, fox_k_norm_g, fox_f_bias, w_out_even, w_in_odd, w_out_odd, norm_ffn2, ffn2_w_gu, ffn2_w_down, loss_target, m_norm_ffn1, m_ffn1_w_gu, m_ffn1_w_down, m_norm_mix, m_w_in_even, m_dn_conv_w, m_dn_a_log, m_dn_dt_bias, m_dn_norm_g, m_fox_q_norm_g, m_fox_k_norm_g, m_fox_f_bias, m_w_out_even, m_w_in_odd, m_w_out_odd, m_norm_ffn2, m_ffn2_w_gu, m_ffn2_w_down, v_norm_ffn1, v_ffn1_w_gu, v_ffn1_w_down, v_norm_mix, v_w_in_even, v_dn_conv_w, v_dn_a_log, v_dn_dt_bias, v_dn_norm_g, v_fox_q_norm_g, v_fox_k_norm_g, v_fox_f_bias, v_w_out_even, v_w_in_odd, v_w_out_odd, v_norm_ffn2, v_ffn2_w_gu, v_ffn2_w_down):
    w = dict(zip(WEIGHTS, (norm_ffn1, ffn1_w_gu, ffn1_w_down, norm_mix, w_in_even, dn_conv_w, dn_a_log, dn_dt_bias,
                           dn_norm_g, fox_q_norm_g, fox_k_norm_g, fox_f_bias, w_out_even, w_in_odd, w_out_odd,
                           norm_ffn2, ffn2_w_gu, ffn2_w_down)))
    m = dict(zip(WEIGHTS, (m_norm_ffn1, m_ffn1_w_gu, m_ffn1_w_down, m_norm_mix, m_w_in_even, m_dn_conv_w, m_dn_a_log,
                           m_dn_dt_bias, m_dn_norm_g, m_fox_q_norm_g, m_fox_k_norm_g, m_fox_f_bias, m_w_out_even,
                           m_w_in_odd, m_w_out_odd, m_norm_ffn2, m_ffn2_w_gu, m_ffn2_w_down)))
    v = dict(zip(WEIGHTS, (v_norm_ffn1, v_ffn1_w_gu, v_ffn1_w_down, v_norm_mix, v_w_in_even, v_dn_conv_w, v_dn_a_log,
                           v_dn_dt_bias, v_dn_norm_g, v_fox_q_norm_g, v_fox_k_norm_g, v_fox_f_bias, v_w_out_even,
                           v_w_in_odd, v_w_out_odd, v_norm_ffn2, v_ffn2_w_gu, v_ffn2_w_down)))
    loss, dx, grads, delta, new_m, new_v = _step(x[0], loss_target[0], w, m, v)
    return (loss, dx[None], *[grads[n] for n in WEIGHTS], *[delta[n] for n in WEIGHTS],
            *[new_m[n] for n in WEIGHTS], *[new_v[n] for n in WEIGHTS])
```

```python
import functools
import math

import jax
import jax.numpy as jnp
from jax import lax
from jax.experimental import pallas as pl
from jax.experimental.pallas import tpu as pltpu

F32 = jnp.float32
BF16 = jnp.bfloat16
HI = lax.Precision.HIGHEST

HEAD_DIM = 128
N_DN_HEADS = 4
N_FOX_HEADS = 4
N_SB_HEADS = 8
D_DN = N_DN_HEADS * HEAD_DIM
D_FOX = N_FOX_HEADS * HEAD_DIM
CONV_WIDTH = 4
DN_CHUNK = 64
EPS = 1e-6
ATT_SCALE = HEAD_DIM ** -0.5
ADAM_LR, ADAM_B1, ADAM_B2, ADAM_EPS, ADAM_WD, ADAM_STEP = 0.001, 0.9, 0.999, 1e-08, 0.01, 10

V7X_VMEM_LIMIT = 56 * 1024 * 1024
LANES = 128
ATT_TQ = 256
ATT_TK = 128
ATT_SUB = ATT_TQ // ATT_TK

LANE_BETA, LANE_DECAY, LANE_FORGET = 0, 4, 8


def _cparams(*sem):
    return pltpu.CompilerParams(dimension_semantics=sem, vmem_limit_bytes=V7X_VMEM_LIMIT)


def _sigmoid(x):
    return 1.0 / (1.0 + jnp.exp(-x))


def _softplus(x):
    return jnp.maximum(x, 0.0) + jnp.log(1.0 + jnp.exp(-jnp.abs(x)))


def _silu_grad(y, sg):
    return sg * (1.0 + y * (1.0 - sg))


def _rowwise(fn, rows, bcast, outs, sums, *, tile, name):
    rows = [r if isinstance(r, tuple) else (r, r.shape[1], 0) for r in rows]
    s = rows[0][0].shape[0]
    assert s % tile == 0
    n_in, n_b, n_out, n_sum = len(rows), len(bcast), len(outs), len(sums)

    def body(*refs):
        ins = [r[...] for r in refs[:n_in + n_b]]
        res = fn(*ins)
        if not isinstance(res, (tuple, list)):
            res = (res,)
        out_refs = refs[n_in + n_b:n_in + n_b + n_out]
        sum_refs = refs[n_in + n_b + n_out:]
        for o_ref, val in zip(out_refs, res[:n_out]):
            o_ref[...] = val.astype(o_ref.dtype)
        if n_sum:
            @pl.when(pl.program_id(0) == 0)
            def _():
                for s_ref in sum_refs:
                    s_ref[...] = jnp.zeros_like(s_ref)
            for s_ref, val in zip(sum_refs, res[n_out:]):
                s_ref[...] += val

    in_specs = [pl.BlockSpec((tile, w), lambda i, cb=cb: (i, cb)) for _, w, cb in rows]
    in_specs += [pl.BlockSpec(b.shape, lambda i, nd=b.ndim: (0,) * nd) for b in bcast]
    out_specs = [pl.BlockSpec((tile, c), lambda i: (i, 0)) for c, _ in outs]
    out_specs += [pl.BlockSpec(sh, lambda i: (0, 0)) for sh in sums]
    out_shape = [jax.ShapeDtypeStruct((s, c), dt) for c, dt in outs]
    out_shape += [jax.ShapeDtypeStruct(sh, F32) for sh in sums]
    return pl.pallas_call(
        body, name=name, grid=(s // tile,), in_specs=in_specs, out_specs=out_specs, out_shape=out_shape,
        compiler_params=_cparams("arbitrary" if n_sum else "parallel"),
    )(*[r[0] for r in rows], *bcast)


def _rms_fwd(x, gain, name):
    def fn(xb, g):
        r = lax.rsqrt(jnp.mean(xb * xb, axis=-1, keepdims=True) + EPS)
        return (xb * r * g,)
    return _rowwise(fn, [x], [gain], [(x.shape[1], BF16)], [], tile=512, name=name)[0]


def _rms_bwd(x, gain, dn, dres, name):
    def fn(xb, dnb, drb, g):
        r = lax.rsqrt(jnp.mean(xb * xb, axis=-1, keepdims=True) + EPS)
        xh = xb * r
        dxh = dnb * g
        dx = drb + r * (dxh - xh * jnp.mean(dxh * xh, axis=-1, keepdims=True))
        return dx, dx, jnp.sum(dnb * xh, axis=0, keepdims=True)
    d = x.shape[1]
    return _rowwise(fn, [x, dn, dres], [gain], [(d, F32), (d, BF16)], [(1, d)], tile=512, name=name)


_DIMS = {"nn": (((1,), (0,)), ((), ())), "nt": (((1,), (1,)), ((), ())), "tn": (((0,), (0,)), ((), ()))}


def _dot(a, b, kind):
    return lax.dot_general(a.astype(BF16), b.astype(BF16), _DIMS[kind], preferred_element_type=F32)


def _dot32(a, b, kind="nn"):
    return lax.dot_general(a, b, _DIMS[kind], precision=HI, preferred_element_type=F32)


def _mm(a, b, kind, *, tm, tn, out_dtype, name, scale=None, residual=None, a_lead=(), b_lead=(),
        b_spec=None, n=None, into=None):
    ash, bsh = a.shape[len(a_lead):], b.shape[len(b_lead):]
    m = ash[1] if kind == "tn" else ash[0]
    k = ash[0] if kind == "tn" else ash[1]
    if b_spec is None:
        n = bsh[0] if kind == "nt" else bsh[1]
        assert k == (bsh[1] if kind == "nt" else bsh[0]), (ash, bsh, kind)
    assert m % tm == 0 and n % tn == 0, (m, tm, n, tn)
    la, lb = (None,) * len(a_lead), (None,) * len(b_lead)
    if kind == "tn":
        a_spec = pl.BlockSpec(la + (k, tm), lambda i, j: a_lead + (0, i))
    else:
        a_spec = pl.BlockSpec(la + (tm, k), lambda i, j: a_lead + (i, 0))
    if b_spec is None:
        if kind == "nt":
            b_spec = pl.BlockSpec(lb + (tn, k), lambda i, j: b_lead + (j, 0))
        else:
            b_spec = pl.BlockSpec(lb + (k, tn), lambda i, j: b_lead + (0, j))
    in_specs, args = [a_spec, b_spec], [a, b]
    if residual is not None:
        in_specs.append(pl.BlockSpec((tm, tn), lambda i, j: (i, j)))
        args.append(residual)
    aliases = {}
    if into is not None:
        buf, layer = into
        in_specs.append(pl.BlockSpec(memory_space=pl.ANY))
        args.append(buf)
        aliases = {len(args) - 1: 0}
        out_spec = pl.BlockSpec((None, tm, tn), lambda i, j: (layer, i, j))
        out_shape = jax.ShapeDtypeStruct(buf.shape, buf.dtype)
    else:
        out_spec = pl.BlockSpec((tm, tn), lambda i, j: (i, j))
        out_shape = jax.ShapeDtypeStruct((m, n), out_dtype)

    def body(a_ref, b_ref, *rest):
        acc = _dot(a_ref[...], b_ref[...], kind)
        if scale is not None:
            acc = acc * scale
        if residual is not None:
            acc = acc + rest[0][...]
        rest[-1][...] = acc.astype(rest[-1].dtype)

    return pl.pallas_call(
        body, name=name, grid=(m // tm, n // tn), in_specs=in_specs, out_specs=out_spec, out_shape=out_shape,
        input_output_aliases=aliases, compiler_params=_cparams("parallel", "parallel"),
    )(*args)


def _ffn_up(n, w_gu, layer, name):
    s, d = n.shape
    f = w_gu.shape[2] // 2
    tm, tn = 512, f // 2
    nj = f // tn

    def body(n_ref, wg_ref, wu_ref, gu_ref, a_ref):
        nv = n_ref[...]
        g = _dot(nv, wg_ref[...], "nn")
        u = _dot(nv, wu_ref[...], "nn")
        gu_ref[0] = g.astype(BF16)
        gu_ref[1] = u.astype(BF16)
        a_ref[...] = (g * _sigmoid(g) * u).astype(BF16)

    return pl.pallas_call(
        body, name=name, grid=(s // tm, nj),
        in_specs=[pl.BlockSpec((tm, d), lambda i, j: (i, 0)),
                  pl.BlockSpec((None, d, tn), lambda i, j: (layer, 0, j)),
                  pl.BlockSpec((None, d, tn), lambda i, j: (layer, 0, j + nj))],
        out_specs=[pl.BlockSpec((2, tm, tn), lambda i, j: (0, i, j)),
                   pl.BlockSpec((tm, tn), lambda i, j: (i, j))],
        out_shape=[jax.ShapeDtypeStruct((2, s, f), BF16), jax.ShapeDtypeStruct((s, f), BF16)],
        compiler_params=_cparams("parallel", "parallel"),
    )(n, w_gu, w_gu)


def _ffn_down_bwd(dxo, w_down, gu, layer, name):
    s, d = dxo.shape
    f = w_down.shape[1]
    tm, tn = 512, f // 2

    def body(dx_ref, w_ref, gu_ref, dgu_ref):
        da = 0.5 * _dot(dx_ref[...], w_ref[...], "nt")
        g = gu_ref[0].astype(F32)
        u = gu_ref[1].astype(F32)
        sg = _sigmoid(g)
        dgu_ref[0] = (da * u * _silu_grad(g, sg)).astype(BF16)
        dgu_ref[1] = (da * g * sg).astype(BF16)

    return pl.pallas_call(
        body, name=name, grid=(s // tm, f // tn),
        in_specs=[pl.BlockSpec((tm, d), lambda i, j: (i, 0)),
                  pl.BlockSpec((None, tn, d), lambda i, j: (layer, j, 0)),
                  pl.BlockSpec((2, tm, tn), lambda i, j: (0, i, j))],
        out_specs=pl.BlockSpec((2, tm, tn), lambda i, j: (0, i, j)),
        out_shape=jax.ShapeDtypeStruct((2, s, f), BF16),
        compiler_params=_cparams("parallel", "parallel"),
    )(dxo, w_down, gu)


def _ffn_dn(dgu, w_gu, layer, name):
    _, s, f = dgu.shape
    d = w_gu.shape[1]
    tm, tn = 512, 512

    def body(dgu_ref, wg_ref, wu_ref, o_ref):
        o_ref[...] = _dot(dgu_ref[0], wg_ref[...], "nt") + _dot(dgu_ref[1], wu_ref[...], "nt")

    return pl.pallas_call(
        body, name=name, grid=(s // tm, d // tn),
        in_specs=[pl.BlockSpec((2, tm, f), lambda i, j: (0, i, 0)),
                  pl.BlockSpec((None, tn, f), lambda i, j: (layer, j, 0)),
                  pl.BlockSpec((None, tn, f), lambda i, j: (layer, j, 1))],
        out_specs=pl.BlockSpec((tm, tn), lambda i, j: (i, j)),
        out_shape=jax.ShapeDtypeStruct((s, d), F32),
        compiler_params=_cparams("parallel", "parallel"),
    )(dgu, w_gu, w_gu)


def _ffn_fwd(x, gain, w_gu, w_down, layer, tag):
    n = _rms_fwd(x, gain, f"{tag}_norm")
    gu, a = _ffn_up(n, w_gu, layer, f"{tag}_up")
    x2 = _mm(a, w_down, "nn", tm=512, tn=512, out_dtype=F32, name=f"{tag}_down", scale=0.5, residual=x,
             b_lead=(layer,))
    return x2, (x, n, gu, a)


def _ffn_bwd(dxo, dxo16, saved, gain, w_gu, w_down, layer, tag, g_gu, g_down):
    x, n, gu, a = saved
    s, f = a.shape
    dgu = _ffn_down_bwd(dxo16, w_down, gu, layer, f"{tag}_down_bwd")
    g_down = _mm(a, dxo16, "tn", tm=256, tn=dxo16.shape[1], out_dtype=F32, name=f"{tag}_down_dw", scale=0.5,
                 into=(g_down, layer))
    dn = _ffn_dn(dgu, w_gu, layer, f"{tag}_up_bwd")
    tn = f // 2
    nj = f // tn
    g_gu = _mm(n, dgu, "tn", tm=512, tn=tn, out_dtype=F32, name=f"{tag}_up_dw", into=(g_gu, layer), n=2 * f,
               b_spec=pl.BlockSpec((None, s, tn), lambda i, j: (j // nj, 0, j % nj)))
    dx, dx16, dgain = _rms_bwd(x, gain, dn, dxo, f"{tag}_norm_bwd")
    return dx, dx16, dgain, g_gu, g_down


def _lane_col(blk, lane):
    li = lax.broadcasted_iota(jnp.int32, blk.shape, 1)
    return jnp.sum(jnp.where(li == lane, blk, 0.0), axis=1, keepdims=True)


def _split_dot(x, tri):
    hi = x.astype(BF16)
    lo = (x - hi.astype(F32)).astype(BF16)
    return (lax.dot_general(hi, tri, _DIMS["nn"], preferred_element_type=F32)
            + lax.dot_general(lo, tri, _DIMS["nn"], preferred_element_type=F32))


class _Each:
    def __init__(self, vals):
        self.vals = list(vals)

    def _with(self, other, op):
        others = other.vals if isinstance(other, _Each) else [other] * len(self.vals)
        return _Each(op(a, b) for a, b in zip(self.vals, others))

    def __add__(self, other):
        return self._with(other, lambda a, b: a + b)

    def __sub__(self, other):
        return self._with(other, lambda a, b: a - b)

    def __mul__(self, other):
        return self._with(other, lambda a, b: a * b)

    def __neg__(self):
        return _Each(-a for a in self.vals)


def _each(fn, *args):
    n = max(len(a.vals) for a in args if isinstance(a, _Each))
    res = [fn(*xs) for xs in zip(*[a.vals if isinstance(a, _Each) else [a] * n for a in args])]
    if isinstance(res[0], tuple):
        return tuple(_Each(r) for r in zip(*res))
    return _Each(res)


def _keep(cond, x):
    return _each(lambda v: jnp.where(cond, v, 0.0), x)


def _rowsum(x):
    return _each(lambda v: jnp.sum(v, axis=1, keepdims=True), x)


ATT_HEADS = 2
ATT_WIDTH = ATT_HEADS * HEAD_DIM
_HEAD_COLS = [slice(h * HEAD_DIM, (h + 1) * HEAD_DIM) for h in range(ATT_HEADS)]


def _att_specs(n_heads, s):
    groups = n_heads // ATT_HEADS
    q_spec = pl.BlockSpec((ATT_TQ, ATT_WIDTH), lambda g, i: (i, g))
    k_spec = pl.BlockSpec((s, ATT_WIDTH), lambda g, i: (0, groups + g))
    v_spec = pl.BlockSpec((s, ATT_WIDTH), lambda g, i: (0, 2 * groups + g))
    return q_spec, k_spec, v_spec


def _heads_of(ref, rows=None):
    return _Each(ref[:, cs] if rows is None else ref[rows, cs] for cs in _HEAD_COLS)


def _dot_each(a, b, kind):
    return _each(lambda x, y: _dot(x, y, kind), a, b)


def _att_iotas():
    row = lax.broadcasted_iota(jnp.int32, (ATT_TQ, ATT_TK), 0)
    col = lax.broadcasted_iota(jnp.int32, (ATT_TQ, ATT_TK), 1)
    jr = lax.broadcasted_iota(jnp.int32, (ATT_TK, ATT_TK), 0)
    jc = lax.broadcasted_iota(jnp.int32, (ATT_TK, ATT_TK), 1)
    return row, col, jr, jc


def _sb_fwd(qkv, n_heads, name):
    s = qkv.shape[0]

    def body(q_ref, k_ref, v_ref, o16_ref, o32_ref):
        i = pl.program_id(1)
        q = _heads_of(q_ref)
        row, col, jr, jc = _att_iotas()
        later = (jr > jc).astype(BF16)

        def step(jb, carry, diagonal):
            c_sp, acc = (_Each(part) for part in carry)
            work = []
            for sub in reversed(range(ATT_SUB)):
                keys = pl.ds(pl.multiple_of(jb * ATT_TQ + sub * ATT_TK, ATT_TK), ATT_TK)
                z = _dot_each(q, _heads_of(k_ref, keys), "nt") * ATT_SCALE
                sp = _each(_softplus, z)
                before = (col + sub * ATT_TK) < row if diagonal else None
                spm = _keep(before, sp) if diagonal else sp
                work.append((keys, z - sp, spm, _each(lambda x: _split_dot(x, later), spm), before))
            for keys, logsig, spm, within, before in work:
                a = _each(jnp.exp, logsig - (c_sp + within))
                if diagonal:
                    a = _keep(before, a)
                acc = acc + _each(_split_dot, a, _heads_of(v_ref, keys))
                c_sp = c_sp + _rowsum(spm)
            return tuple(c_sp.vals), tuple(acc.vals)

        zeros = lambda width: tuple(jnp.zeros((ATT_TQ, width), F32) for _ in range(ATT_HEADS))
        carry = step(i, (zeros(1), zeros(HEAD_DIM)), True)
        _, acc = lax.fori_loop(0, i, lambda it, cr: step(i - 1 - it, cr, False), carry)
        for cs, acc_h in zip(_HEAD_COLS, acc):
            o16_ref[:, cs] = acc_h.astype(BF16)
            o32_ref[:, cs] = acc_h

    q_spec, k_spec, v_spec = _att_specs(n_heads, s)
    o_spec = pl.BlockSpec((ATT_TQ, ATT_WIDTH), lambda g, i: (i, g))
    return pl.pallas_call(
        body, name=name, grid=(n_heads // ATT_HEADS, s // ATT_TQ), in_specs=[q_spec, k_spec, v_spec],
        out_specs=[o_spec, o_spec],
        out_shape=[jax.ShapeDtypeStruct((s, n_heads * HEAD_DIM), BF16),
                   jax.ShapeDtypeStruct((s, n_heads * HEAD_DIM), F32)],
        compiler_params=_cparams("parallel", "arbitrary"),
    )(qkv, qkv, qkv)


def _sb_bwd(qkv, o32, do, n_heads, name):
    s = qkv.shape[0]

    def body(q_ref, k_ref, v_ref, o_ref, do_ref, dq_ref, dk_ref, dv_ref):
        i = pl.program_id(1)

        @pl.when(i == 0)
        def _():
            dk_ref[...] = jnp.zeros_like(dk_ref)
            dv_ref[...] = jnp.zeros_like(dv_ref)

        q, do = _heads_of(q_ref), _heads_of(do_ref)
        total = _rowsum(_each(lambda a, b: a.astype(F32) * b, do, _heads_of(o_ref)))
        row, col, jr, jc = _att_iotas()
        later = (jr > jc).astype(BF16)
        not_before = (jr >= jc).astype(BF16)

        def step(jb, carry, diagonal):
            c_sp, c_e, dq = (_Each(part) for part in carry)
            work = []
            for sub in reversed(range(ATT_SUB)):
                keys = pl.ds(pl.multiple_of(jb * ATT_TQ + sub * ATT_TK, ATT_TK), ATT_TK)
                k = _heads_of(k_ref, keys)
                z = _dot_each(q, k, "nt") * ATT_SCALE
                sp = _each(_softplus, z)
                before = (col + sub * ATT_TK) < row if diagonal else None
                spm = _keep(before, sp) if diagonal else sp
                work.append((keys, k, _each(jnp.exp, z - sp), spm, _each(lambda x: _split_dot(x, later), spm),
                             _dot_each(do, _heads_of(v_ref, keys), "nt"), before))
            for keys, k, sig, spm, within, da, before in work:
                a = sig * _each(lambda x: jnp.exp(-x), c_sp + within)
                if diagonal:
                    a = _keep(before, a)
                e = a * da
                left = total - c_e - _each(lambda x: _split_dot(x, not_before), e)
                dz = (e - (e + left) * sig) * ATT_SCALE
                if diagonal:
                    dz = _keep(before, dz)
                dk, dv = _dot_each(dz, q, "tn"), _dot_each(a, do, "tn")
                for cs, dk_h, dv_h in zip(_HEAD_COLS, dk.vals, dv.vals):
                    dk_ref[keys, cs] += dk_h
                    dv_ref[keys, cs] += dv_h
                dq = dq + _dot_each(dz, k, "nn")
                c_sp = c_sp + _rowsum(spm)
                c_e = c_e + _rowsum(e)
            return tuple(c_sp.vals), tuple(c_e.vals), tuple(dq.vals)

        zeros = lambda width: tuple(jnp.zeros((ATT_TQ, width), F32) for _ in range(ATT_HEADS))
        carry = step(i, (zeros(1), zeros(1), zeros(HEAD_DIM)), True)
        _, _, dq = lax.fori_loop(0, i, lambda it, cr: step(i - 1 - it, cr, False), carry)
        for cs, dq_h in zip(_HEAD_COLS, dq):
            dq_ref[:, cs] = dq_h.astype(BF16)

    q_spec, k_spec, v_spec = _att_specs(n_heads, s)
    blk = pl.BlockSpec((ATT_TQ, ATT_WIDTH), lambda g, i: (i, g))
    full = pl.BlockSpec((s, ATT_WIDTH), lambda g, i: (0, g))
    wide = (s, n_heads * HEAD_DIM)
    return pl.pallas_call(
        body, name=name, grid=(n_heads // ATT_HEADS, s // ATT_TQ), in_specs=[q_spec, k_spec, v_spec, blk, blk],
        out_specs=[blk, full, full],
        out_shape=[jax.ShapeDtypeStruct(wide, BF16), jax.ShapeDtypeStruct(wide, F32), jax.ShapeDtypeStruct(wide, F32)],
        compiler_params=_cparams("parallel", "arbitrary"),
    )(qkv, qkv, qkv, o32, do)


def _fox_logits(q, k, cq, ct_ref, keys):
    ck = _Each(ct_ref[h, :, keys] for h in range(ATT_HEADS))
    return _dot_each(q, k, "nt") * ATT_SCALE + (cq - ck)


def _fox_cq(c_ref, group):
    c = c_ref[...]
    return _Each(_lane_col(c, LANE_FORGET + group * ATT_HEADS + h) for h in range(ATT_HEADS))


def _fox_fwd(qkv, c, ct, name):
    s = qkv.shape[0]
    n_heads = N_FOX_HEADS

    def body(q_ref, k_ref, v_ref, c_ref, ct_ref, o_ref, lse_ref):
        g, i = pl.program_id(0), pl.program_id(1)
        q = _heads_of(q_ref)
        cq = _fox_cq(c_ref, g)
        row, col, _, _ = _att_iotas()

        def step(jb, carry, diagonal):
            m, l, acc = (_Each(part) for part in carry)
            work = []
            m_new = m
            for sub in range(ATT_SUB):
                keys = pl.ds(pl.multiple_of(jb * ATT_TQ + sub * ATT_TK, ATT_TK), ATT_TK)
                sc = _fox_logits(q, _heads_of(k_ref, keys), cq, ct_ref, keys)
                valid = (col + sub * ATT_TK) <= row if diagonal else None
                if diagonal:
                    sc = _each(lambda x: jnp.where(valid, x, -1e30), sc)
                m_new = _each(lambda a, x: jnp.maximum(a, jnp.max(x, axis=1, keepdims=True)), m_new, sc)
                work.append((keys, sc, valid))
            w = _each(jnp.exp, m - m_new)
            l, acc = l * w, acc * w
            for keys, sc, valid in work:
                p = _each(jnp.exp, sc - m_new)
                if diagonal:
                    p = _keep(valid, p)
                l = l + _rowsum(p)
                acc = acc + _each(_split_dot, p, _heads_of(v_ref, keys))
            return tuple(m_new.vals), tuple(l.vals), tuple(acc.vals)

        per_head = lambda width, value: tuple(jnp.full((ATT_TQ, width), value, F32) for _ in range(ATT_HEADS))
        init = (per_head(1, -1e30), per_head(1, 0.0), per_head(HEAD_DIM, 0.0))
        m, l, acc = lax.fori_loop(0, i, lambda jb, cr: step(jb, cr, False), step(i, init, True))
        for h, cs in enumerate(_HEAD_COLS):
            o_ref[:, cs] = acc[h] / l[h]
            lse_ref[h] = jnp.broadcast_to(m[h] + jnp.log(l[h]), (ATT_TQ, LANES))

    q_spec, k_spec, v_spec = _att_specs(n_heads, s)
    return pl.pallas_call(
        body, name=name, grid=(n_heads // ATT_HEADS, s // ATT_TQ),
        in_specs=[q_spec, k_spec, v_spec, pl.BlockSpec((ATT_TQ, LANES), lambda g, i: (i, 0)),
                  pl.BlockSpec((ATT_HEADS, 1, s), lambda g, i: (g, 0, 0))],
        out_specs=[pl.BlockSpec((ATT_TQ, ATT_WIDTH), lambda g, i: (i, g)),
                   pl.BlockSpec((ATT_HEADS, ATT_TQ, LANES), lambda g, i: (g, i, 0))],
        out_shape=[jax.ShapeDtypeStruct((s, n_heads * HEAD_DIM), F32),
                   jax.ShapeDtypeStruct((n_heads, s, LANES), F32)],
        compiler_params=_cparams("parallel", "arbitrary"),
    )(qkv, qkv, qkv, c, ct)


def _fox_bwd(qkv, c, ct, o, lse, do, name):
    s = qkv.shape[0]
    n_heads = N_FOX_HEADS

    def body(q_ref, k_ref, v_ref, c_ref, ct_ref, o_ref, lse_ref, do_ref, dq_ref, dk_ref, dv_ref, dct_ref):
        g, i = pl.program_id(0), pl.program_id(1)

        @pl.when(i == 0)
        def _():
            dk_ref[...] = jnp.zeros_like(dk_ref)
            dv_ref[...] = jnp.zeros_like(dv_ref)
            dct_ref[...] = jnp.zeros_like(dct_ref)

        q = _heads_of(q_ref)
        do16 = _each(lambda x: x.astype(BF16), _heads_of(do_ref))
        delta = _rowsum(_each(lambda a, b: a.astype(F32) * b, do16, _heads_of(o_ref)))
        lse_col = _Each(lse_ref[h, :, 0:1] for h in range(ATT_HEADS))
        cq = _fox_cq(c_ref, g)
        row, col, _, _ = _att_iotas()

        def step(jb, dq, diagonal):
            dq = _Each(dq)
            for sub in range(ATT_SUB):
                keys = pl.ds(pl.multiple_of(jb * ATT_TQ + sub * ATT_TK, ATT_TK), ATT_TK)
                k = _heads_of(k_ref, keys)
                sc = _fox_logits(q, k, cq, ct_ref, keys)
                if diagonal:
                    valid = (col + sub * ATT_TK) <= row
                    p = _keep(valid, _each(jnp.exp, _keep(valid, sc) - lse_col))
                else:
                    p = _each(jnp.exp, sc - lse_col)
                ds = p * (_dot_each(do16, _heads_of(v_ref, keys), "nt") - delta)
                dss = ds * ATT_SCALE
                dk, dv = _dot_each(dss, q, "tn"), _dot_each(p, do16, "tn")
                for h, cs in enumerate(_HEAD_COLS):
                    dct_ref[h, :, keys] -= jnp.sum(ds.vals[h], axis=0, keepdims=True)
                    dk_ref[keys, cs] += dk.vals[h]
                    dv_ref[keys, cs] += dv.vals[h]
                dq = dq + _dot_each(dss, k, "nn")
            return tuple(dq.vals)

        dq0 = step(i, tuple(jnp.zeros((ATT_TQ, HEAD_DIM), F32) for _ in range(ATT_HEADS)), True)
        dq = lax.fori_loop(0, i, lambda jb, dq: step(jb, dq, False), dq0)
        for cs, dq_h in zip(_HEAD_COLS, dq):
            dq_ref[:, cs] = dq_h

    q_spec, k_spec, v_spec = _att_specs(n_heads, s)
    blk = pl.BlockSpec((ATT_TQ, ATT_WIDTH), lambda g, i: (i, g))
    full = pl.BlockSpec((s, ATT_WIDTH), lambda g, i: (0, g))
    wide = jax.ShapeDtypeStruct((s, n_heads * HEAD_DIM), F32)
    return pl.pallas_call(
        body, name=name, grid=(n_heads // ATT_HEADS, s // ATT_TQ),
        in_specs=[q_spec, k_spec, v_spec, pl.BlockSpec((ATT_TQ, LANES), lambda g, i: (i, 0)),
                  pl.BlockSpec((ATT_HEADS, 1, s), lambda g, i: (g, 0, 0)), blk,
                  pl.BlockSpec((ATT_HEADS, ATT_TQ, LANES), lambda g, i: (g, i, 0)), blk],
        out_specs=[blk, full, full, pl.BlockSpec((ATT_HEADS, 1, s), lambda g, i: (g, 0, 0))],
        out_shape=[wide, wide, wide, jax.ShapeDtypeStruct((n_heads, 1, s), F32)],
        compiler_params=_cparams("parallel", "arbitrary"),
    )(qkv, qkv, qkv, c, ct, o, lse, do)


def _cumsum_rows(x, reverse, name):
    s = x.shape[0]
    nb = s // LANES

    def body(x_ref, o_ref):
        r = lax.broadcasted_iota(jnp.int32, (LANES, LANES), 0)
        c = lax.broadcasted_iota(jnp.int32, (LANES, LANES), 1)
        tri = ((r <= c) if reverse else (r >= c)).astype(F32)

        def step(it, carry):
            b = (nb - 1 - it) if reverse else it
            off = pl.multiple_of(b * LANES, LANES)
            blk = x_ref[pl.ds(off, LANES), :]
            o_ref[pl.ds(off, LANES), :] = _dot32(tri, blk) + carry
            return carry + jnp.sum(blk, axis=0, keepdims=True)

        lax.fori_loop(0, nb, step, jnp.zeros((1, LANES), F32))

    return pl.pallas_call(body, name=name, out_shape=jax.ShapeDtypeStruct(x.shape, F32),
                          compiler_params=pltpu.CompilerParams(vmem_limit_bytes=V7X_VMEM_LIMIT))(x)


def _dot32_each(a, b, kind="nn"):
    return _each(lambda x, y: _dot32(x, y, kind), a, b)


def _unit_lower_inverse(m, ri, ci):
    c = ri.shape[0]
    t = -_keep(ri // 2 == ci // 2, m) + jnp.where(ri == ci, 1.0, 0.0)
    b = 4
    while b <= c:
        off_diag = (ri // b == ci // b) & (ri % b >= b // 2) & (ci % b < b // 2)
        t = t - _dot32_each(_dot32_each(t, _keep(off_diag, m)), t)
        b *= 2
    return t


def _dn_gates(g, ri, ci):
    eye = ri == ci
    incl = ri >= ci
    g_row = jnp.sum(jnp.where(eye, g, 0.0), axis=0, keepdims=True)
    gc = jnp.sum(jnp.where(incl, g_row, 0.0), axis=1, keepdims=True)
    gc_row = jnp.sum(jnp.where(eye, gc, 0.0), axis=0, keepdims=True)
    dmat = jnp.where(incl, jnp.exp(jnp.where(incl, gc - gc_row, 0.0)), 0.0)
    gc_last = jnp.sum(g, axis=0, keepdims=True)
    return gc, dmat, jnp.exp(gc), jnp.exp(gc_last - gc), jnp.exp(gc_last)


def _dn_fwd(qkv, act, name):
    s = qkv.shape[0]
    c, d, nh = DN_CHUNK, HEAD_DIM, N_DN_HEADS
    nc = s // c

    def body(q_ref, k_ref, v_ref, act_ref, o_ref, s_ref, t_ref, state):
        @pl.when(pl.program_id(0) == 0)
        def _():
            state[...] = jnp.zeros_like(state)

        ri = lax.broadcasted_iota(jnp.int32, (c, c), 0)
        ci = lax.broadcasted_iota(jnp.int32, (c, c), 1)
        act = act_ref[...]
        heads = range(nh)
        cols = [slice(h * d, (h + 1) * d) for h in heads]
        q, k, v = (_Each(ref[:, cs] for cs in cols) for ref in (q_ref, k_ref, v_ref))
        beta = _Each(_lane_col(act, LANE_BETA + h) for h in heads)
        g = _Each(_lane_col(act, LANE_DECAY + h) for h in heads)
        _, dmat, e, r, gl = _each(lambda gh: _dn_gates(gh, ri, ci), g)
        s0 = _Each(state[h] for h in heads)
        kb = beta * k
        t = _unit_lower_inverse(_keep(ri > ci, _dot32_each(kb, k, "nt") * dmat), ri, ci)
        vn = _dot32_each(t, beta * v) - _dot32_each(_dot32_each(t, kb * e), s0)
        o = _dot32_each(q * e, s0) + _dot32_each(_dot32_each(q, k, "nt") * dmat, vn)
        s1 = s0 * gl + _dot32_each(k * r, vn, "tn")
        for h in heads:
            o_ref[:, cols[h]] = o.vals[h]
            state[h] = s1.vals[h]
            s_ref[h] = s0.vals[h]
            t_ref[h] = t.vals[h]

    wide = lambda part: pl.BlockSpec((c, nh * d), lambda n: (n, part))
    return pl.pallas_call(
        body, name=name, grid=(nc,),
        in_specs=[wide(0), wide(1), wide(2), pl.BlockSpec((c, LANES), lambda n: (n, 0))],
        out_specs=[wide(0), pl.BlockSpec((nh, None, d, d), lambda n: (0, n, 0, 0)),
                   pl.BlockSpec((nh, None, c, c), lambda n: (0, n, 0, 0))],
        out_shape=[jax.ShapeDtypeStruct((s, nh * d), F32), jax.ShapeDtypeStruct((nh, nc, d, d), F32),
                   jax.ShapeDtypeStruct((nh, nc, c, c), F32)],
        scratch_shapes=[pltpu.VMEM((nh, d, d), F32)],
        compiler_params=_cparams("arbitrary"),
    )(qkv, qkv, qkv, act)


def _dn_bwd(qkv, act, states, tinv, do, name):
    s = qkv.shape[0]
    c, d, nh = DN_CHUNK, HEAD_DIM, N_DN_HEADS
    nc = s // c

    def chunk_bwd(q, k, v, do, beta, g, s0, t, ds_out):
        ri = lax.broadcasted_iota(jnp.int32, (c, c), 0)
        ci = lax.broadcasted_iota(jnp.int32, (c, c), 1)
        eye, incl, strict = ri == ci, ri >= ci, ri > ci
        gc, dmat, e, r, gl = _each(lambda gh: _dn_gates(gh, ri, ci), g)
        dot = _dot32_each
        rowsum = lambda x: _each(lambda a: jnp.sum(a, axis=1, keepdims=True), x)
        colsum = lambda x: _each(lambda a: jnp.sum(a, axis=0, keepdims=True), x)
        total = lambda x: colsum(rowsum(x))
        to_col = lambda row: rowsum(_keep(eye, row))
        to_row = lambda colv: colsum(_keep(eye, colv))

        kb, vb = beta * k, beta * v
        kbe = kb * e
        u, w = dot(t, vb), dot(t, kbe)
        vn = u - dot(w, s0)
        qk = dot(q, k, "nt")
        p = qk * dmat
        gram = dot(k, k, "nt")
        kr, qe = k * r, q * e

        d_kr = dot(vn, ds_out, "nt")
        dvn = dot(kr, ds_out)
        dgl = total(s0 * ds_out)
        ds_in = ds_out * gl
        dk = d_kr * r
        dr = rowsum(d_kr * k)
        d_qe = dot(do, s0, "nt")
        ds_in = ds_in + dot(qe, do, "tn")
        dp = _keep(incl, dot(do, vn, "nt"))
        dvn = dvn + dot(p, do, "tn")
        dq = d_qe * e
        de = rowsum(d_qe * q)
        dqk = dp * dmat
        dq = dq + dot(dqk, k)
        dk = dk + dot(dqk, q, "tn")
        dd = dp * qk
        dw = -dot(dvn, s0, "nt")
        ds_in = ds_in - dot(w, dvn, "tn")
        dvb = dot(t, dvn, "tn")
        dkbe = dot(t, dw, "tn")
        dm = -_keep(strict, dot(dvb, u, "nt") + dot(dkbe, w, "nt"))
        dbeta = rowsum(dm * gram * dmat)
        dgram = dm * beta * dmat
        dd = dd + dm * beta * gram
        dk = dk + dot(dgram, k) + dot(dgram, k, "tn")
        dkb = dkbe * e
        de = de + rowsum(dkbe * kb)
        dk = dk + beta * dkb
        dbeta = dbeta + rowsum(dkb * k) + rowsum(dvb * v)
        dv = beta * dvb
        wd = dd * dmat
        dgc = rowsum(wd) - to_col(colsum(wd)) + de * e - dr * r
        dgc_last = total(dr * r) + dgl * gl
        dgc = dgc + _keep(ri[:, 0:1] == c - 1, dgc_last)
        dg = rowsum(_keep(ri <= ci, to_row(dgc)))
        return dq, dk, dv, dbeta, dg, ds_in

    def body(q_ref, k_ref, v_ref, act_ref, s_ref, t_ref, do_ref, dq_ref, dk_ref, dv_ref, dact_ref, dstate):
        @pl.when(pl.program_id(0) == 0)
        def _():
            dstate[...] = jnp.zeros_like(dstate)

        act = act_ref[...]
        heads = range(nh)
        cols = [slice(h * d, (h + 1) * d) for h in heads]
        q, k, v, do = (_Each(ref[:, cs] for cs in cols) for ref in (q_ref, k_ref, v_ref, do_ref))
        dq, dk, dv, dbeta, dg, ds_in = chunk_bwd(
            q, k, v, do, _Each(_lane_col(act, LANE_BETA + h) for h in heads),
            _Each(_lane_col(act, LANE_DECAY + h) for h in heads), _Each(s_ref[h] for h in heads),
            _Each(t_ref[h] for h in heads), _Each(dstate[h] for h in heads))
        lane = lax.broadcasted_iota(jnp.int32, (c, LANES), 1)
        dact = jnp.zeros((c, LANES), F32)
        for h in heads:
            dstate[h] = ds_in.vals[h]
            dq_ref[:, cols[h]], dk_ref[:, cols[h]], dv_ref[:, cols[h]] = dq.vals[h], dk.vals[h], dv.vals[h]
            dact = (dact + jnp.where(lane == LANE_BETA + h, dbeta.vals[h], 0.0)
                    + jnp.where(lane == LANE_DECAY + h, dg.vals[h], 0.0))
        dact_ref[...] = dact

    part = lambda p: pl.BlockSpec((c, nh * d), lambda n: (nc - 1 - n, p))
    per = lambda a, b: pl.BlockSpec((nh, None, a, b), lambda n: (0, nc - 1 - n, 0, 0))
    wide = jax.ShapeDtypeStruct((s, nh * d), F32)
    act_spec = pl.BlockSpec((c, LANES), lambda n: (nc - 1 - n, 0))
    return pl.pallas_call(
        body, name=name, grid=(nc,),
        in_specs=[part(0), part(1), part(2), act_spec, per(d, d), per(c, c), part(0)],
        out_specs=[part(0), part(0), part(0), act_spec],
        out_shape=[wide, wide, wide, jax.ShapeDtypeStruct((s, LANES), F32)],
        scratch_shapes=[pltpu.VMEM((nh, d, d), F32)],
        compiler_params=_cparams("arbitrary"),
    )(qkv, qkv, qkv, act, states, tinv, do)


EVEN_DN_QKV, EVEN_FOX_QKV, EVEN_DN_GATE, EVEN_FOX_GATE, EVEN_NARROW = 0, 1536, 3072, 3584, 4096
EVEN_WIDTH = 4224
CONV_TILE = 256
CONV_HALO = 8


def _conv_fwd(proj, w, name):
    s = proj.shape[0]
    t, cw = CONV_TILE, 3 * D_DN

    def body(cur_ref, prev_ref, w_ref, y_ref, xs):
        i = pl.program_id(0)
        xs[0:CONV_HALO, :] = jnp.where(i > 0, prev_ref[...], 0.0)
        xs[CONV_HALO:, :] = cur_ref[...]
        y = jnp.zeros((t, cw), F32)
        for tap in range(CONV_WIDTH):
            y = y + w_ref[tap:tap + 1, :] * xs[pl.ds(CONV_HALO - CONV_WIDTH + 1 + tap, t), :]
        y_ref[...] = y

    per = t // CONV_HALO
    return pl.pallas_call(
        body, name=name, grid=(s // t,),
        in_specs=[pl.BlockSpec((t, cw), lambda i: (i, 0)),
                  pl.BlockSpec((CONV_HALO, cw), lambda i: (jnp.maximum(i * per - 1, 0), 0)),
                  pl.BlockSpec((CONV_WIDTH, cw), lambda i: (0, 0))],
        out_specs=pl.BlockSpec((t, cw), lambda i: (i, 0)),
        out_shape=jax.ShapeDtypeStruct((s, cw), F32),
        scratch_shapes=[pltpu.VMEM((t + CONV_HALO, cw), F32)],
        compiler_params=_cparams("parallel"),
    )(proj, proj, w)


def _conv_bwd(proj, w, dy, name):
    s = proj.shape[0]
    t, cw = CONV_TILE, 3 * D_DN
    nt = s // t

    def body(cur_ref, prev_ref, w_ref, dy_ref, nxt_ref, dx_ref, dw_ref, xs, dys):
        i = pl.program_id(0)

        @pl.when(i == 0)
        def _():
            dw_ref[...] = jnp.zeros_like(dw_ref)

        xs[0:CONV_HALO, :] = jnp.where(i > 0, prev_ref[...], 0.0)
        xs[CONV_HALO:, :] = cur_ref[...]
        dys[0:t, :] = dy_ref[...]
        dys[t:, :] = jnp.where(i < nt - 1, nxt_ref[...], 0.0)
        dy = dy_ref[...]
        dx = jnp.zeros((t, cw), F32)
        for tap in range(CONV_WIDTH):
            dx = dx + w_ref[tap:tap + 1, :] * dys[pl.ds(CONV_WIDTH - 1 - tap, t), :]
            dw_ref[tap:tap + 1, :] += jnp.sum(dy * xs[pl.ds(CONV_HALO - CONV_WIDTH + 1 + tap, t), :], axis=0,
                                              keepdims=True)
        dx_ref[...] = dx.astype(BF16)

    per = t // CONV_HALO
    last = s // CONV_HALO - 1
    return pl.pallas_call(
        body, name=name, grid=(nt,),
        in_specs=[pl.BlockSpec((t, cw), lambda i: (i, 0)),
                  pl.BlockSpec((CONV_HALO, cw), lambda i: (jnp.maximum(i * per - 1, 0), 0)),
                  pl.BlockSpec((CONV_WIDTH, cw), lambda i: (0, 0)),
                  pl.BlockSpec((t, cw), lambda i: (i, 0)),
                  pl.BlockSpec((CONV_HALO, cw), lambda i: (jnp.minimum((i + 1) * per, last), 0))],
        out_specs=[pl.BlockSpec((t, cw), lambda i: (i, 0)), pl.BlockSpec((CONV_WIDTH, cw), lambda i: (0, 0))],
        out_shape=[jax.ShapeDtypeStruct((s, cw), BF16), jax.ShapeDtypeStruct((CONV_WIDTH, cw), F32)],
        scratch_shapes=[pltpu.VMEM((t + CONV_HALO, cw), F32), pltpu.VMEM((t + CONV_HALO, cw), F32)],
        compiler_params=_cparams("arbitrary"),
    )(proj, proj, w, dy, dy)


def _heads(x, n):
    return [x[:, HEAD_DIM * h:HEAD_DIM * (h + 1)] for h in range(n)]


def _dn_pre_fwd(y, name):
    def fn(yb):
        cs = yb * _sigmoid(yb)
        out = []
        for idx, xh in enumerate(_heads(cs, 3 * N_DN_HEADS)):
            if idx < 2 * N_DN_HEADS:
                xh = xh * lax.rsqrt(jnp.sum(xh * xh, axis=-1, keepdims=True) + EPS)
                if idx < N_DN_HEADS:
                    xh = xh * ATT_SCALE
            out.append(xh)
        return (jnp.concatenate(out, axis=1),)
    return _rowwise(fn, [y], [], [(y.shape[1], F32)], [], tile=256, name=name)[0]


def _dn_pre_bwd(y, dq, dk, dv, name):
    def fn(yb, dqb, dkb, dvb):
        sg = _sigmoid(yb)
        cs = yb * sg
        dout = _heads(dqb, N_DN_HEADS) + _heads(dkb, N_DN_HEADS) + _heads(dvb, N_DN_HEADS)
        dcs = []
        for idx, (xh, dh) in enumerate(zip(_heads(cs, 3 * N_DN_HEADS), dout)):
            if idx < 2 * N_DN_HEADS:
                if idx < N_DN_HEADS:
                    dh = dh * ATT_SCALE
                r = lax.rsqrt(jnp.sum(xh * xh, axis=-1, keepdims=True) + EPS)
                xhat = xh * r
                dh = r * (dh - xhat * jnp.sum(xhat * dh, axis=-1, keepdims=True))
            dcs.append(dh)
        return (jnp.concatenate(dcs, axis=1) * _silu_grad(yb, sg),)
    return _rowwise(fn, [y, dq, dk, dv], [], [(y.shape[1], F32)], [], tile=256, name=name)[0]


def _narrow_params(a_log, dt_bias, f_bias):
    lanes = lambda a, first: jnp.pad(a.reshape(1, -1), ((0, 0), (first, LANES - first - a.shape[0])))
    return jnp.concatenate([lanes(a_log, LANE_DECAY), lanes(dt_bias, LANE_DECAY), lanes(f_bias, LANE_FORGET),
                            jnp.zeros((5, LANES), F32)], axis=0)


def _narrow_masks(shape):
    lane = lax.broadcasted_iota(jnp.int32, shape, 1)
    is_beta = lane < LANE_DECAY
    is_decay = (lane >= LANE_DECAY) & (lane < LANE_FORGET)
    is_forget = (lane >= LANE_FORGET) & (lane < LANE_FORGET + N_FOX_HEADS)
    return is_beta, is_decay, is_forget


def _narrow_fwd(proj, params, name):
    def fn(sm, pk):
        is_beta, is_decay, is_forget = _narrow_masks(sm.shape)
        g = -jnp.exp(pk[0:1, :]) * _softplus(sm + pk[1:2, :])
        logf = -_softplus(-(sm + pk[2:3, :]))
        return (jnp.where(is_beta, _sigmoid(sm), jnp.where(is_decay, g, jnp.where(is_forget, logf, 0.0))),)
    return _rowwise(fn, [(proj, LANES, EVEN_NARROW // LANES)], [params], [(LANES, F32)], [], tile=512, name=name)[0]


def _narrow_bwd(proj, params, act, dact, dlogf, name):
    def fn(sm, ab, da, dl, pk):
        is_beta, is_decay, is_forget = _narrow_masks(sm.shape)
        db = jnp.where(is_forget, dl, da)
        d_beta = db * ab * (1.0 - ab)
        d_decay = db * (-jnp.exp(pk[0:1, :])) * _sigmoid(sm + pk[1:2, :])
        d_forget = db * _sigmoid(-(sm + pk[2:3, :]))
        dsm = jnp.where(is_beta, d_beta, jnp.where(is_decay, d_decay, jnp.where(is_forget, d_forget, 0.0)))
        col = lambda x: jnp.sum(x, axis=0, keepdims=True)
        return (dsm, col(jnp.where(is_decay, db * ab, 0.0)), col(jnp.where(is_decay, dsm, 0.0)),
                col(jnp.where(is_forget, dsm, 0.0)))
    return _rowwise(fn, [(proj, LANES, EVEN_NARROW // LANES), act, dact, dlogf], [params], [(LANES, BF16)],
                    [(1, LANES)] * 3, tile=512, name=name)


def _head_rms(xh):
    r = lax.rsqrt(jnp.mean(xh * xh, axis=-1, keepdims=True) + EPS)
    return xh * r, r


def _fox_pre_fwd(proj, qg, kg, name):
    def fn(pf, qgb, kgb):
        out = []
        for idx, xh in enumerate(_heads(pf, 3 * N_FOX_HEADS)):
            if idx < 2 * N_FOX_HEADS:
                xh = _head_rms(xh)[0] * (qgb if idx < N_FOX_HEADS else kgb)
            out.append(xh)
        return (jnp.concatenate(out, axis=1),)
    return _rowwise(fn, [(proj, 3 * D_FOX, EVEN_FOX_QKV // (3 * D_FOX))], [qg, kg], [(3 * D_FOX, BF16)], [],
                    tile=256, name=name)[0]


def _fox_pre_bwd(proj, qg, kg, dq, dk, dv, name):
    def fn(pf, dqb, dkb, dvb, qgb, kgb):
        dout = _heads(dqb, N_FOX_HEADS) + _heads(dkb, N_FOX_HEADS) + _heads(dvb, N_FOX_HEADS)
        dg = [jnp.zeros((1, HEAD_DIM), F32), jnp.zeros((1, HEAD_DIM), F32)]
        dx = []
        for idx, (xh, dh) in enumerate(zip(_heads(pf, 3 * N_FOX_HEADS), dout)):
            if idx < 2 * N_FOX_HEADS:
                which = 0 if idx < N_FOX_HEADS else 1
                xhat, r = _head_rms(xh)
                dg[which] = dg[which] + jnp.sum(dh * xhat, axis=0, keepdims=True)
                dxh = dh * (qgb if which == 0 else kgb)
                dh = r * (dxh - xhat * jnp.mean(dxh * xhat, axis=-1, keepdims=True))
            dx.append(dh)
        return jnp.concatenate(dx, axis=1), dg[0], dg[1]
    return _rowwise(fn, [(proj, 3 * D_FOX, EVEN_FOX_QKV // (3 * D_FOX)), dq, dk, dv], [qg, kg],
                    [(3 * D_FOX, BF16)], [(1, HEAD_DIM)] * 2, tile=256, name=name)


def _mix_gate_fwd(proj, o_dn, o_fox, ng, name):
    def fn(gd, gf, od, of, ngb):
        dn = [_head_rms(xh)[0] * ngb for xh in _heads(od, N_DN_HEADS)]
        return (jnp.concatenate([jnp.concatenate(dn, axis=1) * gd * _sigmoid(gd), of * _sigmoid(gf)], axis=1),)
    return _rowwise(fn, [(proj, D_DN, EVEN_DN_GATE // D_DN), (proj, D_FOX, EVEN_FOX_GATE // D_FOX), o_dn, o_fox],
                    [ng], [(D_DN + D_FOX, BF16)], [], tile=256, name=name)[0]


def _mix_gate_bwd(proj, o_dn, o_fox, ng, dom, name):
    def fn(gd, gf, od, of, dm, ngb):
        d_dn, d_fox = dm[:, :D_DN], dm[:, D_DN:]
        sgd, sgf = _sigmoid(gd), _sigmoid(gf)
        don = d_dn * gd * sgd
        dng = jnp.zeros((1, HEAD_DIM), F32)
        dod, normed = [], []
        for xh, dh in zip(_heads(od, N_DN_HEADS), _heads(don, N_DN_HEADS)):
            xhat, r = _head_rms(xh)
            dng = dng + jnp.sum(dh * xhat, axis=0, keepdims=True)
            dxh = dh * ngb
            dod.append(r * (dxh - xhat * jnp.mean(dxh * xhat, axis=-1, keepdims=True)))
            normed.append(xhat * ngb)
        d_gd = d_dn * jnp.concatenate(normed, axis=1) * _silu_grad(gd, sgd)
        d_gf = d_fox * of * sgf * (1.0 - sgf)
        return jnp.concatenate(dod, axis=1), d_fox * sgf, d_gd, d_gf, dng
    return _rowwise(fn, [(proj, D_DN, EVEN_DN_GATE // D_DN), (proj, D_FOX, EVEN_FOX_GATE // D_FOX), o_dn, o_fox, dom],
                    [ng], [(D_DN, F32), (D_FOX, F32), (D_DN, BF16), (D_FOX, BF16)], [(1, HEAD_DIM)], tile=256,
                    name=name)


def _loss_grad(y, target, name):
    d = y.shape[1]

    def fn(yb, tb):
        diff = yb - tb
        part = jnp.sum(jnp.sum(diff * diff, axis=1, keepdims=True), axis=0, keepdims=True) * (0.5 / d)
        g = diff * (1.0 / d)
        return g, g, part
    return _rowwise(fn, [y, target], [], [(d, F32), (d, BF16)], [(1, 1)], tile=512, name=name)


_REF_EVEN = {"dn_qkv": (0, 1536), "dn_gate": (1536, 2048), "dn_ba": (2048, 2056), "fox_qkv": (2056, 3592),
             "fox_gate": (3592, 4104), "f_pre": (4104, 4108)}
D_IN_EVEN = 4108


def _even_to_kernel_layout(w):
    cut = lambda name: w[..., _REF_EVEN[name][0]:_REF_EVEN[name][1]]
    pad = jnp.zeros(w.shape[:-1] + (EVEN_WIDTH - EVEN_NARROW - 12,), w.dtype)
    return jnp.concatenate([cut("dn_qkv"), cut("fox_qkv"), cut("dn_gate"), cut("fox_gate"), cut("dn_ba"),
                            cut("f_pre"), pad], axis=-1)


def _even_from_kernel_layout(g):
    return jnp.concatenate([g[..., EVEN_DN_QKV:EVEN_FOX_QKV], g[..., EVEN_DN_GATE:EVEN_FOX_GATE],
                            g[..., EVEN_NARROW:EVEN_NARROW + 8], g[..., EVEN_FOX_QKV:EVEN_DN_GATE],
                            g[..., EVEN_FOX_GATE:EVEN_NARROW], g[..., EVEN_NARROW + 8:EVEN_NARROW + 12]], axis=-1)


def _forget_rows(c):
    return c[:, LANE_FORGET:LANE_FORGET + N_FOX_HEADS].T.reshape(N_FOX_HEADS, 1, c.shape[0])


def _forget_lanes(rows):
    s = rows.shape[2]
    return jnp.pad(rows.reshape(-1, s).T, ((0, 0), (LANE_FORGET, LANES - LANE_FORGET - N_FOX_HEADS)))


def _even_fwd(x, gain, w_in, w_out, j, p, tag):
    h = _rms_fwd(x, gain, f"{tag}_norm")
    proj = _mm(h, w_in, "nn", tm=512, tn=384, out_dtype=F32, name=f"{tag}_in", b_lead=(j,))
    y = _conv_fwd(proj, p["conv_w"], f"{tag}_conv")
    dn_qkv = _dn_pre_fwd(y, f"{tag}_dn_pre")
    act = _narrow_fwd(proj, p["narrow"], f"{tag}_narrow")
    o_dn, states, tinv = _dn_fwd(dn_qkv, act, f"{tag}_delta")
    fox_qkv = _fox_pre_fwd(proj, p["q_g"], p["k_g"], f"{tag}_fox_pre")
    c = _cumsum_rows(act, False, f"{tag}_cumsum")
    ct = _forget_rows(c)
    o_fox, lse = _fox_fwd(fox_qkv, c, ct, f"{tag}_fox")
    om = _mix_gate_fwd(proj, o_dn, o_fox, p["dn_norm_g"], f"{tag}_gate")
    x2 = _mm(om, w_out, "nn", tm=512, tn=512, out_dtype=F32, name=f"{tag}_out", residual=x, b_lead=(j,))
    return x2, (x, h, proj, y, dn_qkv, act, states, tinv, o_dn, fox_qkv, c, ct, o_fox, lse, om)


def _even_bwd(dxo, dxo16, saved, gain, w_in, w_out, j, p, tag, g_in, g_out):
    x, h, proj, y, dn_qkv, act, states, tinv, o_dn, fox_qkv, c, ct, o_fox, lse, om = saved
    d = x.shape[1]
    dom = _mm(dxo16, w_out, "nt", tm=512, tn=512, out_dtype=F32, name=f"{tag}_out_bwd", b_lead=(j,))
    g_out = _mm(om, dxo16, "tn", tm=512, tn=d, out_dtype=F32, name=f"{tag}_out_dw", into=(g_out, j))
    d_odn, d_ofox, d_gd, d_gf, d_ng = _mix_gate_bwd(proj, o_dn, o_fox, p["dn_norm_g"], dom, f"{tag}_gate_bwd")
    dq, dk, dv, dct = _fox_bwd(fox_qkv, c, ct, o_fox, lse, d_ofox, f"{tag}_fox_bwd")
    d_fox_qkv, d_qg, d_kg = _fox_pre_bwd(proj, p["q_g"], p["k_g"], dq, dk, dv, f"{tag}_fox_pre_bwd")
    dlogf = _cumsum_rows(_forget_lanes(dct), True, f"{tag}_cumsum_bwd")
    dq, dk, dv, dact = _dn_bwd(dn_qkv, act, states, tinv, d_odn, f"{tag}_delta_bwd")
    dy = _dn_pre_bwd(y, dq, dk, dv, f"{tag}_dn_pre_bwd")
    d_dn_qkv, d_conv = _conv_bwd(proj, p["conv_w"], dy, f"{tag}_conv_bwd")
    d_narrow, s_alog, s_dt, s_fb = _narrow_bwd(proj, p["narrow"], act, dact, dlogf, f"{tag}_narrow_bwd")
    dproj = jnp.concatenate([d_dn_qkv, d_fox_qkv, d_gd, d_gf, d_narrow], axis=1)
    dh = _mm(dproj, w_in, "nt", tm=512, tn=512, out_dtype=F32, name=f"{tag}_in_bwd", b_lead=(j,))
    g_in = _mm(h, dproj, "tn", tm=512, tn=384, out_dtype=F32, name=f"{tag}_in_dw", into=(g_in, j))
    dx, dx16, d_gain = _rms_bwd(x, gain, dh, dxo, f"{tag}_norm_bwd")
    small = {"conv_w": d_conv, "a_log": s_alog, "dt_bias": s_dt, "f_bias": s_fb, "dn_norm_g": d_ng, "q_g": d_qg,
             "k_g": d_kg}
    return dx, dx16, d_gain, small, g_in, g_out


def _odd_fwd(x, gain, w_in, w_out, j, tag):
    h = _rms_fwd(x, gain, f"{tag}_norm")
    qkv = _mm(h, w_in, "nn", tm=512, tn=768, out_dtype=BF16, name=f"{tag}_in", b_lead=(j,))
    o16, o32 = _sb_fwd(qkv, N_SB_HEADS, f"{tag}_sb")
    x2 = _mm(o16, w_out, "nn", tm=512, tn=512, out_dtype=F32, name=f"{tag}_out", residual=x, b_lead=(j,))
    return x2, (x, h, qkv, o16, o32)


def _odd_bwd(dxo, dxo16, saved, gain, w_in, w_out, j, tag, g_in, g_out):
    x, h, qkv, o16, o32 = saved
    d = x.shape[1]
    do = _mm(dxo16, w_out, "nt", tm=512, tn=512, out_dtype=BF16, name=f"{tag}_out_bwd", b_lead=(j,))
    g_out = _mm(o16, dxo16, "tn", tm=512, tn=d, out_dtype=F32, name=f"{tag}_out_dw", into=(g_out, j))
    dq, dk, dv = _sb_bwd(qkv, o32, do, N_SB_HEADS, f"{tag}_sb_bwd")
    dqkv = jnp.concatenate([dq, dk.astype(BF16), dv.astype(BF16)], axis=1)
    dh = _mm(dqkv, w_in, "nt", tm=512, tn=512, out_dtype=F32, name=f"{tag}_in_bwd", b_lead=(j,))
    g_in = _mm(h, dqkv, "tn", tm=512, tn=768, out_dtype=F32, name=f"{tag}_in_dw", into=(g_in, j))
    dx, dx16, d_gain = _rms_bwd(x, gain, dh, dxo, f"{tag}_norm_bwd")
    return dx, dx16, d_gain, g_in, g_out


def _forward_backward(x, target, w):
    depth = w["norm_ffn1"].shape[0]
    row = lambda a, l: a[l][None]

    def even_small(j):
        return {"conv_w": w["dn_conv_w"][j], "narrow": _narrow_params(w["dn_a_log"][j], w["dn_dt_bias"][j],
                                                                     w["fox_f_bias"][j]),
                "dn_norm_g": row(w["dn_norm_g"], j), "q_g": row(w["fox_q_norm_g"], j),
                "k_g": row(w["fox_k_norm_g"], j)}

    saved = []
    for l in range(depth):
        x, s1 = _ffn_fwd(x, row(w["norm_ffn1"], l), w["ffn1_w_gu"], w["ffn1_w_down"], l, "ffn1")
        if l % 2 == 0:
            x, s2 = _even_fwd(x, row(w["norm_mix"], l), w["w_in_even"], w["w_out_even"], l // 2, even_small(l // 2),
                              "even")
        else:
            x, s2 = _odd_fwd(x, row(w["norm_mix"], l), w["w_in_odd"], w["w_out_odd"], l // 2, "odd")
        x, s3 = _ffn_fwd(x, row(w["norm_ffn2"], l), w["ffn2_w_gu"], w["ffn2_w_down"], l, "ffn2")
        saved.append((s1, s2, s3))

    dx, dx16, loss = _loss_grad(x, target, "loss")

    big = {k: lax.empty(w[k].shape, F32) for k in ("ffn1_w_gu", "ffn1_w_down", "ffn2_w_gu", "ffn2_w_down",
                                                   "w_in_even", "w_out_even", "w_in_odd", "w_out_odd")}
    d_norm = {k: [None] * depth for k in ("norm_ffn1", "norm_mix", "norm_ffn2")}
    d_even = [None] * ((depth + 1) // 2)
    for l in reversed(range(depth)):
        s1, s2, s3 = saved[l]
        dx, dx16, d_norm["norm_ffn2"][l], big["ffn2_w_gu"], big["ffn2_w_down"] = _ffn_bwd(
            dx, dx16, s3, row(w["norm_ffn2"], l), w["ffn2_w_gu"], w["ffn2_w_down"], l, "ffn2", big["ffn2_w_gu"],
            big["ffn2_w_down"])
        if l % 2 == 0:
            dx, dx16, d_norm["norm_mix"][l], d_even[l // 2], big["w_in_even"], big["w_out_even"] = _even_bwd(
                dx, dx16, s2, row(w["norm_mix"], l), w["w_in_even"], w["w_out_even"], l // 2, even_small(l // 2),
                "even", big["w_in_even"], big["w_out_even"])
        else:
            dx, dx16, d_norm["norm_mix"][l], big["w_in_odd"], big["w_out_odd"] = _odd_bwd(
                dx, dx16, s2, row(w["norm_mix"], l), w["w_in_odd"], w["w_out_odd"], l // 2, "odd", big["w_in_odd"],
                big["w_out_odd"])
        dx, dx16, d_norm["norm_ffn1"][l], big["ffn1_w_gu"], big["ffn1_w_down"] = _ffn_bwd(
            dx, dx16, s1, row(w["norm_ffn1"], l), w["ffn1_w_gu"], w["ffn1_w_down"], l, "ffn1", big["ffn1_w_gu"],
            big["ffn1_w_down"])

    small = {k: jnp.concatenate(v, axis=0) for k, v in d_norm.items()}
    dec = slice(LANE_DECAY, LANE_DECAY + N_DN_HEADS)
    fgt = slice(LANE_FORGET, LANE_FORGET + N_FOX_HEADS)
    small["dn_conv_w"] = jnp.stack([e["conv_w"] for e in d_even])
    small["dn_a_log"] = jnp.concatenate([e["a_log"][:, dec] for e in d_even], axis=0)
    small["dn_dt_bias"] = jnp.concatenate([e["dt_bias"][:, dec] for e in d_even], axis=0)
    small["fox_f_bias"] = jnp.concatenate([e["f_bias"][:, fgt] for e in d_even], axis=0)
    small["dn_norm_g"] = jnp.concatenate([e["dn_norm_g"] for e in d_even], axis=0)
    small["fox_q_norm_g"] = jnp.concatenate([e["q_g"] for e in d_even], axis=0)
    small["fox_k_norm_g"] = jnp.concatenate([e["k_g"] for e in d_even], axis=0)
    return loss, dx, big, small


MESH = pl.DeviceIdType.MESH
ANY = pl.BlockSpec(memory_space=pl.ANY)


def _place():
    x, y, c = lax.axis_index("x"), lax.axis_index("y"), lax.axis_index("c")
    return x, y, c, [(1 - x, y), (x, 1 - y), (1 - x, 1 - y)]


def _remote(src, dst, send_sem, recv_sem, to):
    return pltpu.make_async_remote_copy(src_ref=src, dst_ref=dst, send_sem=send_sem, recv_sem=recv_sem,
                                        device_id=to, device_id_type=MESH)


def _aligned(start, multiple):
    return start if isinstance(start, int) else pl.multiple_of(start, multiple)


def _quarter(ref, kind, chip, half, rows, cols):
    k = 2 * chip[0] + chip[1]
    hr = rows // 2
    assert hr % 16 == 0 and cols % LANES == 0
    if kind == "col":
        return ref.at[:, pl.ds(_aligned(half * hr, 16), hr), pl.ds(_aligned(k * cols, LANES), cols)]
    return ref.at[:, pl.ds(_aligned(k * rows + half * hr, 16), hr), :]


def _place_quarter(shard, kind, kc, name):
    l, rows, cols = shard.shape
    tr = rows
    while tr * cols * 4 > (2 << 20) and tr % 32 == 0:
        tr //= 2
    nr = rows // tr
    if kind == "col":
        out_spec = pl.BlockSpec((None, tr, cols), lambda li, i, kc_ref: (li, i, kc_ref[0]))
        out_shape = (l, rows, 4 * cols)
    else:
        out_spec = pl.BlockSpec((None, tr, cols), lambda li, i, kc_ref: (li, kc_ref[0] * nr + i, 0))
        out_shape = (l, 4 * rows, cols)

    def body(kc_ref, x_ref, o_ref):
        o_ref[...] = x_ref[...].astype(BF16)

    return pl.pallas_call(
        body, name=name,
        grid_spec=pltpu.PrefetchScalarGridSpec(
            num_scalar_prefetch=1, grid=(l, nr),
            in_specs=[pl.BlockSpec((None, tr, cols), lambda li, i, kc_ref: (li, i, 0))], out_specs=out_spec),
        out_shape=jax.ShapeDtypeStruct(out_shape, BF16),
        compiler_params=_cparams("parallel", "parallel"),
    )(kc, shard)


def _gather_weights(wholes, kinds):
    n = len(wholes)

    def dims(ref, kind):
        _, r, cc = ref.shape
        return (r, cc // 4) if kind == "col" else (r // 4, cc)

    def body(*refs):
        bufs = refs[n:2 * n]
        send_sems, recv_sems = refs[2 * n:]
        x, y, c, chips = _place()
        sibling = (x, y, 1 - c)
        first, passed = [], []
        for t in range(n):
            rows, cols = dims(bufs[t], kinds[t])
            mine = _quarter(bufs[t], kinds[t], (x, y), c, rows, cols)
            for j, chip in enumerate(chips):
                cp = _remote(mine, mine, send_sems.at[t, j], recv_sems.at[t, j], (*chip, c))
                cp.start()
                first.append(cp)
        for j, chip in enumerate(chips):
            for t in range(n):
                rows, cols = dims(bufs[t], kinds[t])
                got = _quarter(bufs[t], kinds[t], chip, c, rows, cols)
                _remote(got, got, send_sems.at[t, j], recv_sems.at[t, j], (*chip, c)).wait_recv()
                cp = _remote(got, got, send_sems.at[t, 3 + j], recv_sems.at[t, 3 + j], sibling)
                cp.start()
                passed.append(cp)
        for j, chip in enumerate(chips):
            for t in range(n):
                rows, cols = dims(bufs[t], kinds[t])
                got = _quarter(bufs[t], kinds[t], chip, 1 - c, rows, cols)
                _remote(got, got, send_sems.at[t, 3 + j], recv_sems.at[t, 3 + j], sibling).wait_recv()
        for cp in first + passed:
            cp.wait_send()

    return pl.pallas_call(
        body, name="gather_weights", in_specs=[ANY] * n, out_specs=[ANY] * n,
        out_shape=[jax.ShapeDtypeStruct(a.shape, a.dtype) for a in wholes],
        input_output_aliases={t: t for t in range(n)},
        scratch_shapes=[pltpu.SemaphoreType.DMA((n, 6)), pltpu.SemaphoreType.DMA((n, 6))],
        compiler_params=pltpu.CompilerParams(has_side_effects=True),
    )(*wholes)


def _canonical(a, kind):
    l, r, c = a.shape
    return a.reshape(l, 1, r, c) if kind == "col" else a.reshape(l, 4, r // 4, c)


def _rs_sibling(parts):
    n = len(parts)

    def body(*refs):
        ins, outs = refs[:n], refs[n:2 * n]
        send_sems, recv_sems = refs[2 * n:]
        x, y, c, _ = _place()
        copies = []
        for t in range(n):
            hr = ins[t].shape[2] // 2
            src = ins[t].at[:, :, pl.ds(pl.multiple_of((1 - c) * hr, 8), hr), :]
            cp = _remote(src, outs[t], send_sems.at[t], recv_sems.at[t], (x, y, 1 - c))
            cp.start()
            copies.append(cp)
        for cp in copies:
            cp.wait()

    half = lambda a: jax.ShapeDtypeStruct(a.shape[:2] + (a.shape[2] // 2, a.shape[3]), a.dtype)
    return pl.pallas_call(
        body, name="reduce_sibling", in_specs=[ANY] * n, out_specs=[ANY] * n, out_shape=[half(a) for a in parts],
        scratch_shapes=[pltpu.SemaphoreType.DMA((n,)), pltpu.SemaphoreType.DMA((n,))],
        compiler_params=pltpu.CompilerParams(has_side_effects=True),
    )(*parts)


def _add_tile(rows, cols):
    tc = cols if cols <= 1536 else cols // 4
    tr = rows
    while tr * tc * 4 > (1 << 20) and tr % 16 == 0:
        tr //= 2
    return tr, tc


def _rs_add_sibling(part, got, c, name):
    l, a, hr, cols = got.shape
    tr, tc = _add_tile(hr, cols)
    nr = hr // tr

    def body(c_ref, p_ref, g_ref, o32_ref, o16_ref):
        s = p_ref[...] + g_ref[...]
        o32_ref[...] = s
        o16_ref[...] = s.astype(BF16)

    blk = (None, None, tr, tc)
    spec = pl.BlockSpec(blk, lambda li, ai, i, j, c_ref: (li, ai, i, j))
    return pl.pallas_call(
        body, name=name,
        grid_spec=pltpu.PrefetchScalarGridSpec(
            num_scalar_prefetch=1, grid=(l, a, nr, cols // tc),
            in_specs=[pl.BlockSpec(blk, lambda li, ai, i, j, c_ref: (li, ai, c_ref[0] * nr + i, j)), spec],
            out_specs=[spec, spec]),
        out_shape=[jax.ShapeDtypeStruct(got.shape, F32), jax.ShapeDtypeStruct(got.shape, BF16)],
        compiler_params=_cparams("parallel", "parallel", "parallel", "parallel"),
    )(c, part, got)


def _quarter4(ref, kind, chip, cols):
    k = 2 * chip[0] + chip[1]
    if kind == "col":
        return ref.at[:, :, :, pl.ds(pl.multiple_of(k * cols, LANES), cols)]
    return ref.at[:, pl.ds(k, 1), :, :]


def _rs_chips(sums16, kinds):
    n = len(sums16)

    def qshape(a, kind):
        l, na, hr, cols = a.shape
        return (l, 1, hr, cols // 4 if kind == "col" else cols)

    def body(*refs):
        ins, outs = refs[:n], refs[n:2 * n]
        send_sems, recv_sems = refs[2 * n:]
        x, y, c, chips = _place()
        copies = []
        for t in range(n):
            cols = qshape(ins[t], kinds[t])[3]
            for j, chip in enumerate(chips):
                cp = _remote(_quarter4(ins[t], kinds[t], chip, cols), outs[t].at[j], send_sems.at[t, j],
                             recv_sems.at[t, j], (*chip, c))
                cp.start()
                copies.append(cp)
        for cp in copies:
            cp.wait()

    return pl.pallas_call(
        body, name="reduce_chips", in_specs=[ANY] * n, out_specs=[ANY] * n,
        out_shape=[jax.ShapeDtypeStruct((3,) + qshape(a, k), a.dtype) for a, k in zip(sums16, kinds)],
        scratch_shapes=[pltpu.SemaphoreType.DMA((n, 3)), pltpu.SemaphoreType.DMA((n, 3))],
        compiler_params=pltpu.CompilerParams(has_side_effects=True),
    )(*sums16)


def _rs_add_chips(sum32, got, kind, kc, name):
    _, l, _, hr, cols = got.shape
    tr, _ = _add_tile(hr, cols)
    nr = hr // tr
    k_arr, c_arr = kc
    if kind == "col":
        own = pl.BlockSpec((None, None, tr, cols), lambda li, i, k_ref, c_ref: (li, 0, i, k_ref[0]))
    else:
        own = pl.BlockSpec((None, None, tr, cols), lambda li, i, k_ref, c_ref: (li, k_ref[0], i, 0))

    def body(k_ref, c_ref, own_ref, got_ref, o_ref):
        o_ref[...] = ((own_ref[...] + got_ref[0].astype(F32)) + got_ref[1].astype(F32)) + got_ref[2].astype(F32)

    return pl.pallas_call(
        body, name=name,
        grid_spec=pltpu.PrefetchScalarGridSpec(
            num_scalar_prefetch=2, grid=(l, nr),
            in_specs=[own, pl.BlockSpec((3, None, None, tr, cols), lambda li, i, k_ref, c_ref: (0, li, 0, i, 0))],
            out_specs=pl.BlockSpec((None, tr, cols), lambda li, i, k_ref, c_ref: (li, c_ref[0] * nr + i, 0))),
        out_shape=jax.ShapeDtypeStruct((l, 2 * hr, cols), F32),
        compiler_params=_cparams("parallel", "parallel"),
    )(k_arr, c_arr, sum32, got)


def _rs_finish(quarters):
    n = len(quarters)

    def body(*refs):
        bufs = refs[n:2 * n]
        send_sems, recv_sems = refs[2 * n:]
        x, y, c, _ = _place()
        copies = []
        for t in range(n):
            hr = bufs[t].shape[1] // 2
            mine = bufs[t].at[:, pl.ds(pl.multiple_of(c * hr, 8), hr), :]
            cp = _remote(mine, mine, send_sems.at[t], recv_sems.at[t], (x, y, 1 - c))
            cp.start()
            copies.append(cp)
        for cp in copies:
            cp.wait()

    return pl.pallas_call(
        body, name="reduce_finish", in_specs=[ANY] * n, out_specs=[ANY] * n,
        out_shape=[jax.ShapeDtypeStruct(a.shape, a.dtype) for a in quarters],
        input_output_aliases={t: t for t in range(n)},
        scratch_shapes=[pltpu.SemaphoreType.DMA((n,)), pltpu.SemaphoreType.DMA((n,))],
        compiler_params=pltpu.CompilerParams(has_side_effects=True),
    )(*quarters)


def _reduce_scatter(parts, kinds, tags):
    x, y, c = lax.axis_index("x"), lax.axis_index("y"), lax.axis_index("c")
    c_arr = jnp.reshape(c, (1,)).astype(jnp.int32)
    k_arr = (jnp.reshape(2 * x + y, (1,)).astype(jnp.int32), c_arr)
    canon = [_canonical(p, kind) for p, kind in zip(parts, kinds)]
    from_sibling = _rs_sibling(canon)
    sums = [_rs_add_sibling(p, g, c_arr, f"reduce_add_sibling_{tag}") for p, g, tag in zip(canon, from_sibling, tags)]
    from_chips = _rs_chips([s16 for _, s16 in sums], kinds)
    halves = [_rs_add_chips(s32, g, kind, k_arr, f"reduce_add_chips_{tag}")
              for (s32, _), g, kind, tag in zip(sums, from_chips, kinds, tags)]
    return _rs_finish(halves)


SMALL_PEERS = 7


def _small_exchange(pack):
    rows = pack.shape[0]

    def body(p_ref, slots_ref, total_ref, send_sems, recv_sems):
        x, y, c, _ = _place()
        me = 4 * x + 2 * y + c
        slots_ref[me] = p_ref[...]
        copies = []
        for p in range(1, SMALL_PEERS + 1):
            px, py, pc = (p >> 2) & 1, (p >> 1) & 1, p & 1
            peer = (1 - x if px else x, 1 - y if py else y, 1 - c if pc else c)
            cp = _remote(p_ref, slots_ref.at[me], send_sems.at[p - 1], recv_sems.at[p - 1], peer)
            cp.start()
            copies.append(cp)
        for cp in copies:
            cp.wait()
        total = slots_ref[0]
        for i in range(1, SMALL_PEERS + 1):
            total = total + slots_ref[i]
        total_ref[...] = total

    vmem = pl.BlockSpec(memory_space=pltpu.VMEM)
    return pl.pallas_call(
        body, name="small_exchange", in_specs=[vmem], out_specs=[vmem, vmem],
        out_shape=[jax.ShapeDtypeStruct((SMALL_PEERS + 1, rows, LANES), F32), jax.ShapeDtypeStruct((rows, LANES), F32)],
        scratch_shapes=[pltpu.SemaphoreType.DMA((SMALL_PEERS,)), pltpu.SemaphoreType.DMA((SMALL_PEERS,))],
        compiler_params=pltpu.CompilerParams(has_side_effects=True),
    )(pack)


def _pack(arrays):
    rows = []
    for a in arrays:
        flat = a.reshape(-1).astype(F32)
        rows.append(jnp.pad(flat, (0, (-flat.shape[0]) % LANES)).reshape(-1, LANES))
    out = jnp.concatenate(rows, axis=0)
    return jnp.pad(out, ((0, (-out.shape[0]) % 8), (0, 0)))


def _unpack(pack, shapes):
    out, r = [], 0
    for sh in shapes:
        size = math.prod(sh)
        nr = -(-size // LANES)
        out.append(pack[r:r + nr].reshape(-1)[:size].reshape(sh))
        r += nr
    return out


def _adamw(w, g, m, v, name):
    shape = w.shape
    to2d = lambda a: a.reshape(-1, shape[-1])
    rows = math.prod(shape[:-1])
    tile = 256 if rows % 256 == 0 else rows

    def fn(wb, gb, mb, vb):
        m2 = ADAM_B1 * mb + (1.0 - ADAM_B1) * gb
        v2 = ADAM_B2 * vb + (1.0 - ADAM_B2) * (gb * gb)
        m_hat = m2 / (1.0 - ADAM_B1 ** ADAM_STEP)
        v_hat = v2 / (1.0 - ADAM_B2 ** ADAM_STEP)
        return -ADAM_LR * (m_hat / (jnp.sqrt(v_hat) + ADAM_EPS) + ADAM_WD * wb), m2, v2

    res = _rowwise(fn, [to2d(w), to2d(g), to2d(m), to2d(v)], [], [(shape[-1], F32)] * 3, [], tile=tile, name=name)
    return [r.reshape(shape) for r in res]


BIG = (("ffn1_w_gu", "col"), ("ffn1_w_down", "row"), ("w_in_even", "col"), ("w_out_even", "row"),
       ("w_in_odd", "col"), ("w_out_odd", "row"), ("ffn2_w_gu", "col"), ("ffn2_w_down", "row"))
SMALL = ("norm_ffn1", "norm_mix", "dn_conv_w", "dn_a_log", "dn_dt_bias", "dn_norm_g", "fox_q_norm_g", "fox_k_norm_g",
         "fox_f_bias", "norm_ffn2")
WEIGHTS = ("norm_ffn1", "ffn1_w_gu", "ffn1_w_down", "norm_mix", "w_in_even", "dn_conv_w", "dn_a_log", "dn_dt_bias",
           "dn_norm_g", "fox_q_norm_g", "fox_k_norm_g", "fox_f_bias", "w_out_even", "w_in_odd", "w_out_odd",
           "norm_ffn2", "ffn2_w_gu", "ffn2_w_down")
EVEN_QUARTER = 1027
EVEN_QUARTER_PAD = 1152


def _step(x, target, w, m, v):
    k = 2 * lax.axis_index("x") + lax.axis_index("y")
    n_conv = w["dn_conv_w"].shape[2]

    kc = jnp.reshape(k, (1,)).astype(jnp.int32)
    placed = []
    for name, kind in BIG:
        a = w[name]
        if name == "w_in_even":
            a = jnp.pad(a, ((0, 0), (0, 0), (0, EVEN_QUARTER_PAD - EVEN_QUARTER)))
        placed.append(_place_quarter(a, kind, kc, f"place_{name}"))
    whole = dict(zip([n for n, _ in BIG], _gather_weights(placed, [kind for _, kind in BIG])))
    padded = whole["w_in_even"]
    ref_order = jnp.concatenate([padded[..., q * EVEN_QUARTER_PAD:q * EVEN_QUARTER_PAD + EVEN_QUARTER]
                                 for q in range(4)], axis=-1)
    whole["w_in_even"] = _even_to_kernel_layout(ref_order)
    conv_slots, _ = _small_exchange(_pack([w["dn_conv_w"]]))
    conv_rows = math.prod(w["dn_conv_w"].shape) // LANES
    quarters = [conv_slots[2 * q, :conv_rows].reshape(w["dn_conv_w"].shape) for q in range(4)]
    whole["dn_conv_w"] = jnp.concatenate(quarters, axis=-1)
    for name in SMALL:
        if name != "dn_conv_w":
            whole[name] = w[name]

    loss, dx, big, small = _forward_backward(x, target, whole)

    g_even = _even_from_kernel_layout(big["w_in_even"])
    big["w_in_even"] = jnp.concatenate(
        [jnp.pad(g_even[..., q * EVEN_QUARTER:(q + 1) * EVEN_QUARTER],
                 ((0, 0), (0, 0), (0, EVEN_QUARTER_PAD - EVEN_QUARTER))) for q in range(4)], axis=-1)
    names = [n for n, _ in BIG]
    reduced = dict(zip(names, _reduce_scatter([big[n] for n in names], [kind for _, kind in BIG], names)))
    reduced["w_in_even"] = reduced["w_in_even"][..., :EVEN_QUARTER]
    _, small_sum = _small_exchange(_pack([small[n] for n in SMALL]))
    grads = dict(zip(SMALL, _unpack(small_sum, [small[n].shape for n in SMALL])))
    grads["dn_conv_w"] = lax.dynamic_slice_in_dim(grads["dn_conv_w"], k * n_conv, n_conv, axis=2)
    grads.update(reduced)

    delta, new_m, new_v = {}, {}, {}
    for name, _ in BIG:
        delta[name], new_m[name], new_v[name] = _adamw(w[name], grads[name], m[name], v[name], f"adamw_{name}")
    packs = [_pack([d[n] for n in SMALL]) for d in (w, grads, m, v)]
    shapes = [w[n].shape for n in SMALL]
    for out, res in zip((delta, new_m, new_v), _adamw(*packs, "adamw_small")):
        out.update(zip(SMALL, _unpack(res, shapes)))
    total_loss = lax.psum(loss[0, 0], ("x", "y", "c"))
    return total_loss, dx, grads, delta, new_m, new_v


def kernel(x, norm_ffn1, ffn1_w_gu, ffn1_w_down, norm_mix, w_in_even, dn_conv_w, dn_a_log, dn_dt_bias, dn_norm_g, fox_q_norm_g, fox_k_norm_g, fox_f_bias, w_out_even, w_in_odd, w_out_odd, norm_ffn2, ffn2_w_gu, ffn2_w_down, loss_target, m_norm_ffn1, m_ffn1_w_gu, m_ffn1_w_down, m_norm_mix, m_w_in_even, m_dn_conv_w, m_dn_a_log, m_dn_dt_bias, m_dn_norm_g, m_fox_q_norm_g, m_fox_k_norm_g, m_fox_f_bias, m_w_out_even, m_w_in_odd, m_w_out_odd, m_norm_ffn2, m_ffn2_w_gu, m_ffn2_w_down, v_norm_ffn1, v_ffn1_w_gu, v_ffn1_w_down, v_norm_mix, v_w_in_even, v_dn_conv_w, v_dn_a_log, v_dn_dt_bias, v_dn_norm_g, v_fox_q_norm_g, v_fox_k_norm_g, v_fox_f_bias, v_w_out_even, v_w_in_odd, v_w_out_odd, v_norm_ffn2, v_ffn2_w_gu, v_ffn2_w_down):
    w = dict(zip(WEIGHTS, (norm_ffn1, ffn1_w_gu, ffn1_w_down, norm_mix, w_in_even, dn_conv_w, dn_a_log, dn_dt_bias,
                           dn_norm_g, fox_q_norm_g, fox_k_norm_g, fox_f_bias, w_out_even, w_in_odd, w_out_odd,
                           norm_ffn2, ffn2_w_gu, ffn2_w_down)))
    m = dict(zip(WEIGHTS, (m_norm_ffn1, m_ffn1_w_gu, m_ffn1_w_down, m_norm_mix, m_w_in_even, m_dn_conv_w, m_dn_a_log,
                           m_dn_dt_bias, m_dn_norm_g, m_fox_q_norm_g, m_fox_k_norm_g, m_fox_f_bias, m_w_out_even,
                           m_w_in_odd, m_w_out_odd, m_norm_ffn2, m_ffn2_w_gu, m_ffn2_w_down)))
    v = dict(zip(WEIGHTS, (v_norm_ffn1, v_ffn1_w_gu, v_ffn1_w_down, v_norm_mix, v_w_in_even, v_dn_conv_w, v_dn_a_log,
                           v_dn_dt_bias, v_dn_norm_g, v_fox_q_norm_g, v_fox_k_norm_g, v_fox_f_bias, v_w_out_even,
                           v_w_in_odd, v_w_out_odd, v_norm_ffn2, v_ffn2_w_gu, v_ffn2_w_down)))
    loss, dx, grads, delta, new_m, new_v = _step(x[0], loss_target[0], w, m, v)
    return (loss, dx[None], *[grads[n] for n in WEIGHTS], *[delta[n] for n in WEIGHTS],
            *[new_m[n] for n in WEIGHTS], *[new_v[n] for n in WEIGHTS])
```

```python
import functools
import math

import jax
import jax.numpy as jnp
from jax import lax
from jax.experimental import pallas as pl
from jax.experimental.pallas import tpu as pltpu

F32 = jnp.float32
BF16 = jnp.bfloat16
HI = lax.Precision.HIGHEST

HEAD_DIM = 128
N_DN_HEADS = 4
N_FOX_HEADS = 4
N_SB_HEADS = 8
D_DN = N_DN_HEADS * HEAD_DIM
D_FOX = N_FOX_HEADS * HEAD_DIM
CONV_WIDTH = 4
DN_CHUNK = 64
EPS = 1e-6
ATT_SCALE = HEAD_DIM ** -0.5
ADAM_LR, ADAM_B1, ADAM_B2, ADAM_EPS, ADAM_WD, ADAM_STEP = 0.001, 0.9, 0.999, 1e-08, 0.01, 10

V7X_VMEM_LIMIT = 56 * 1024 * 1024
LANES = 128
ATT_TQ = 256
ATT_TK = 128
ATT_SUB = ATT_TQ // ATT_TK

LANE_BETA, LANE_DECAY, LANE_FORGET = 0, 4, 8


def _cparams(*sem):
    return pltpu.CompilerParams(dimension_semantics=sem, vmem_limit_bytes=V7X_VMEM_LIMIT)


def _sigmoid(x):
    return 1.0 / (1.0 + jnp.exp(-x))


def _softplus(x):
    return jnp.maximum(x, 0.0) + jnp.log(1.0 + jnp.exp(-jnp.abs(x)))


def _silu_grad(y, sg):
    return sg * (1.0 + y * (1.0 - sg))


def _rowwise(fn, rows, bcast, outs, sums, *, tile, name):
    rows = [r if isinstance(r, tuple) else (r, r.shape[1], 0) for r in rows]
    s = rows[0][0].shape[0]
    assert s % tile == 0
    n_in, n_b, n_out, n_sum = len(rows), len(bcast), len(outs), len(sums)

    def body(*refs):
        ins = [r[...] for r in refs[:n_in + n_b]]
        res = fn(*ins)
        if not isinstance(res, (tuple, list)):
            res = (res,)
        out_refs = refs[n_in + n_b:n_in + n_b + n_out]
        sum_refs = refs[n_in + n_b + n_out:]
        for o_ref, val in zip(out_refs, res[:n_out]):
            o_ref[...] = val.astype(o_ref.dtype)
        if n_sum:
            @pl.when(pl.program_id(0) == 0)
            def _():
                for s_ref in sum_refs:
                    s_ref[...] = jnp.zeros_like(s_ref)
            for s_ref, val in zip(sum_refs, res[n_out:]):
                s_ref[...] += val

    in_specs = [pl.BlockSpec((tile, w), lambda i, cb=cb: (i, cb)) for _, w, cb in rows]
    in_specs += [pl.BlockSpec(b.shape, lambda i, nd=b.ndim: (0,) * nd) for b in bcast]
    out_specs = [pl.BlockSpec((tile, c), lambda i: (i, 0)) for c, _ in outs]
    out_specs += [pl.BlockSpec(sh, lambda i: (0, 0)) for sh in sums]
    out_shape = [jax.ShapeDtypeStruct((s, c), dt) for c, dt in outs]
    out_shape += [jax.ShapeDtypeStruct(sh, F32) for sh in sums]
    return pl.pallas_call(
        body, name=name, grid=(s // tile,), in_specs=in_specs, out_specs=out_specs, out_shape=out_shape,
        compiler_params=_cparams("arbitrary" if n_sum else "parallel"),
    )(*[r[0] for r in rows], *bcast)


def _rms_fwd(x, gain, name):
    def fn(xb, g):
        r = lax.rsqrt(jnp.mean(xb * xb, axis=-1, keepdims=True) + EPS)
        return (xb * r * g,)
    return _rowwise(fn, [x], [gain], [(x.shape[1], BF16)], [], tile=512, name=name)[0]


def _rms_bwd(x, gain, dn, dres, name):
    def fn(xb, dnb, drb, g):
        r = lax.rsqrt(jnp.mean(xb * xb, axis=-1, keepdims=True) + EPS)
        xh = xb * r
        dxh = dnb * g
        dx = drb + r * (dxh - xh * jnp.mean(dxh * xh, axis=-1, keepdims=True))
        return dx, dx, jnp.sum(dnb * xh, axis=0, keepdims=True)
    d = x.shape[1]
    return _rowwise(fn, [x, dn, dres], [gain], [(d, F32), (d, BF16)], [(1, d)], tile=512, name=name)


_DIMS = {"nn": (((1,), (0,)), ((), ())), "nt": (((1,), (1,)), ((), ())), "tn": (((0,), (0,)), ((), ()))}


def _dot(a, b, kind):
    return lax.dot_general(a.astype(BF16), b.astype(BF16), _DIMS[kind], preferred_element_type=F32)


def _dot32(a, b, kind="nn"):
    return lax.dot_general(a, b, _DIMS[kind], precision=HI, preferred_element_type=F32)


def _mm(a, b, kind, *, tm, tn, out_dtype, name, scale=None, residual=None, a_lead=(), b_lead=(),
        b_spec=None, n=None, into=None):
    ash, bsh = a.shape[len(a_lead):], b.shape[len(b_lead):]
    m = ash[1] if kind == "tn" else ash[0]
    k = ash[0] if kind == "tn" else ash[1]
    if b_spec is None:
        n = bsh[0] if kind == "nt" else bsh[1]
        assert k == (bsh[1] if kind == "nt" else bsh[0]), (ash, bsh, kind)
    assert m % tm == 0 and n % tn == 0, (m, tm, n, tn)
    la, lb = (None,) * len(a_lead), (None,) * len(b_lead)
    if kind == "tn":
        a_spec = pl.BlockSpec(la + (k, tm), lambda j, i: a_lead + (0, i))
    else:
        a_spec = pl.BlockSpec(la + (tm, k), lambda j, i: a_lead + (i, 0))
    if b_spec is None:
        if kind == "nt":
            b_spec = pl.BlockSpec(lb + (tn, k), lambda j, i: b_lead + (j, 0))
        else:
            b_spec = pl.BlockSpec(lb + (k, tn), lambda j, i: b_lead + (0, j))
    in_specs, args = [a_spec, b_spec], [a, b]
    if residual is not None:
        in_specs.append(pl.BlockSpec((tm, tn), lambda j, i: (i, j)))
        args.append(residual)
    aliases = {}
    if into is not None:
        buf, layer = into
        in_specs.append(pl.BlockSpec(memory_space=pl.ANY))
        args.append(buf)
        aliases = {len(args) - 1: 0}
        out_spec = pl.BlockSpec((None, tm, tn), lambda j, i: (layer, i, j))
        out_shape = jax.ShapeDtypeStruct(buf.shape, buf.dtype)
    else:
        out_spec = pl.BlockSpec((tm, tn), lambda j, i: (i, j))
        out_shape = jax.ShapeDtypeStruct((m, n), out_dtype)

    def body(a_ref, b_ref, *rest):
        acc = _dot(a_ref[...], b_ref[...], kind)
        if scale is not None:
            acc = acc * scale
        if residual is not None:
            acc = acc + rest[0][...]
        rest[-1][...] = acc.astype(rest[-1].dtype)

    return pl.pallas_call(
        body, name=name, grid=(n // tn, m // tm), in_specs=in_specs, out_specs=out_spec, out_shape=out_shape,
        input_output_aliases=aliases, compiler_params=_cparams("parallel", "parallel"),
    )(*args)


def _ffn_up(n, w_gu, layer, name):
    s, d = n.shape
    f = w_gu.shape[2] // 2
    tm, tn = 512, f // 2
    nj = f // tn

    def body(n_ref, wg_ref, wu_ref, gu_ref, a_ref):
        nv = n_ref[...]
        g = _dot(nv, wg_ref[...], "nn")
        u = _dot(nv, wu_ref[...], "nn")
        gu_ref[0] = g.astype(BF16)
        gu_ref[1] = u.astype(BF16)
        a_ref[...] = (g * _sigmoid(g) * u).astype(BF16)

    return pl.pallas_call(
        body, name=name, grid=(nj, s // tm),
        in_specs=[pl.BlockSpec((tm, d), lambda j, i: (i, 0)),
                  pl.BlockSpec((None, d, tn), lambda j, i: (layer, 0, j)),
                  pl.BlockSpec((None, d, tn), lambda j, i: (layer, 0, j + nj))],
        out_specs=[pl.BlockSpec((2, tm, tn), lambda j, i: (0, i, j)),
                   pl.BlockSpec((tm, tn), lambda j, i: (i, j))],
        out_shape=[jax.ShapeDtypeStruct((2, s, f), BF16), jax.ShapeDtypeStruct((s, f), BF16)],
        compiler_params=_cparams("parallel", "parallel"),
    )(n, w_gu, w_gu)


def _ffn_down_bwd(dxo, w_down, gu, layer, name):
    s, d = dxo.shape
    f = w_down.shape[1]
    tm, tn = 512, f // 2

    def body(dx_ref, w_ref, gu_ref, dgu_ref):
        da = 0.5 * _dot(dx_ref[...], w_ref[...], "nt")
        g = gu_ref[0].astype(F32)
        u = gu_ref[1].astype(F32)
        sg = _sigmoid(g)
        dgu_ref[0] = (da * u * _silu_grad(g, sg)).astype(BF16)
        dgu_ref[1] = (da * g * sg).astype(BF16)

    return pl.pallas_call(
        body, name=name, grid=(f // tn, s // tm),
        in_specs=[pl.BlockSpec((tm, d), lambda j, i: (i, 0)),
                  pl.BlockSpec((None, tn, d), lambda j, i: (layer, j, 0)),
                  pl.BlockSpec((2, tm, tn), lambda j, i: (0, i, j))],
        out_specs=pl.BlockSpec((2, tm, tn), lambda j, i: (0, i, j)),
        out_shape=jax.ShapeDtypeStruct((2, s, f), BF16),
        compiler_params=_cparams("parallel", "parallel"),
    )(dxo, w_down, gu)


def _ffn_dn(dgu, w_gu, layer, name):
    _, s, f = dgu.shape
    d = w_gu.shape[1]
    tm, tn = 512, d

    def body(dgu_ref, wg_ref, wu_ref, o_ref):
        o_ref[...] = _dot(dgu_ref[0], wg_ref[...], "nt") + _dot(dgu_ref[1], wu_ref[...], "nt")

    return pl.pallas_call(
        body, name=name, grid=(s // tm, d // tn),
        in_specs=[pl.BlockSpec((2, tm, f), lambda i, j: (0, i, 0)),
                  pl.BlockSpec((None, tn, f), lambda i, j: (layer, j, 0)),
                  pl.BlockSpec((None, tn, f), lambda i, j: (layer, j, 1))],
        out_specs=pl.BlockSpec((tm, tn), lambda i, j: (i, j)),
        out_shape=jax.ShapeDtypeStruct((s, d), F32),
        compiler_params=_cparams("parallel", "parallel"),
    )(dgu, w_gu, w_gu)


def _ffn_fwd(x, gain, w_gu, w_down, layer, tag):
    n = _rms_fwd(x, gain, f"{tag}_norm")
    gu, a = _ffn_up(n, w_gu, layer, f"{tag}_up")
    x2 = _mm(a, w_down, "nn", tm=512, tn=x.shape[1], out_dtype=F32, name=f"{tag}_down", scale=0.5, residual=x,
             b_lead=(layer,))
    return x2, (x, n, gu, a)


def _ffn_bwd(dxo, dxo16, saved, gain, w_gu, w_down, layer, tag, g_gu, g_down):
    x, n, gu, a = saved
    s, f = a.shape
    dgu = _ffn_down_bwd(dxo16, w_down, gu, layer, f"{tag}_down_bwd")
    g_down = _mm(a, dxo16, "tn", tm=256, tn=dxo16.shape[1], out_dtype=F32, name=f"{tag}_down_dw", scale=0.5,
                 into=(g_down, layer))
    dn = _ffn_dn(dgu, w_gu, layer, f"{tag}_up_bwd")
    tn = f // 2
    nj = f // tn
    g_gu = _mm(n, dgu, "tn", tm=512, tn=tn, out_dtype=F32, name=f"{tag}_up_dw", into=(g_gu, layer), n=2 * f,
               b_spec=pl.BlockSpec((None, s, tn), lambda j, i: (j // nj, 0, j % nj)))
    dx, dx16, dgain = _rms_bwd(x, gain, dn, dxo, f"{tag}_norm_bwd")
    return dx, dx16, dgain, g_gu, g_down


def _lane_col(blk, lane):
    li = lax.broadcasted_iota(jnp.int32, blk.shape, 1)
    return jnp.sum(jnp.where(li == lane, blk, 0.0), axis=1, keepdims=True)


def _split_dot(x, tri):
    hi = x.astype(BF16)
    lo = (x - hi.astype(F32)).astype(BF16)
    return (lax.dot_general(hi, tri, _DIMS["nn"], preferred_element_type=F32)
            + lax.dot_general(lo, tri, _DIMS["nn"], preferred_element_type=F32))


class _Each:
    def __init__(self, vals):
        self.vals = list(vals)

    def _with(self, other, op):
        others = other.vals if isinstance(other, _Each) else [other] * len(self.vals)
        return _Each(op(a, b) for a, b in zip(self.vals, others))

    def __add__(self, other):
        return self._with(other, lambda a, b: a + b)

    def __sub__(self, other):
        return self._with(other, lambda a, b: a - b)

    def __mul__(self, other):
        return self._with(other, lambda a, b: a * b)

    def __neg__(self):
        return _Each(-a for a in self.vals)


def _each(fn, *args):
    n = max(len(a.vals) for a in args if isinstance(a, _Each))
    res = [fn(*xs) for xs in zip(*[a.vals if isinstance(a, _Each) else [a] * n for a in args])]
    if isinstance(res[0], tuple):
        return tuple(_Each(r) for r in zip(*res))
    return _Each(res)


def _keep(cond, x):
    return _each(lambda v: jnp.where(cond, v, 0.0), x)


def _rowsum(x):
    return _each(lambda v: jnp.sum(v, axis=1, keepdims=True), x)


ATT_HEADS = 2
ATT_WIDTH = ATT_HEADS * HEAD_DIM
_HEAD_COLS = [slice(h * HEAD_DIM, (h + 1) * HEAD_DIM) for h in range(ATT_HEADS)]


def _att_specs(n_heads, s):
    groups = n_heads // ATT_HEADS
    q_spec = pl.BlockSpec((ATT_TQ, ATT_WIDTH), lambda g, i: (i, g))
    k_spec = pl.BlockSpec((s, ATT_WIDTH), lambda g, i: (0, groups + g))
    v_spec = pl.BlockSpec((s, ATT_WIDTH), lambda g, i: (0, 2 * groups + g))
    return q_spec, k_spec, v_spec


def _heads_of(ref, rows=None):
    return _Each(ref[:, cs] if rows is None else ref[rows, cs] for cs in _HEAD_COLS)


def _dot_each(a, b, kind):
    return _each(lambda x, y: _dot(x, y, kind), a, b)


def _att_iotas():
    row = lax.broadcasted_iota(jnp.int32, (ATT_TQ, ATT_TK), 0)
    col = lax.broadcasted_iota(jnp.int32, (ATT_TQ, ATT_TK), 1)
    jr = lax.broadcasted_iota(jnp.int32, (ATT_TK, ATT_TK), 0)
    jc = lax.broadcasted_iota(jnp.int32, (ATT_TK, ATT_TK), 1)
    return row, col, jr, jc


def _sb_fwd(qkv, n_heads, name):
    s = qkv.shape[0]

    def body(q_ref, k_ref, v_ref, o16_ref, o32_ref):
        i = pl.program_id(1)
        q = _heads_of(q_ref)
        row, col, jr, jc = _att_iotas()
        later = (jr > jc).astype(BF16)

        def step(jb, carry, diagonal):
            c_sp, acc = (_Each(part) for part in carry)
            work = []
            for sub in reversed(range(ATT_SUB)):
                keys = pl.ds(pl.multiple_of(jb * ATT_TQ + sub * ATT_TK, ATT_TK), ATT_TK)
                z = _dot_each(q, _heads_of(k_ref, keys), "nt") * ATT_SCALE
                sp = _each(_softplus, z)
                before = (col + sub * ATT_TK) < row if diagonal else None
                spm = _keep(before, sp) if diagonal else sp
                work.append((keys, z - sp, spm, _each(lambda x: _dot(x, later, "nn"), spm), before))
            for keys, logsig, spm, within, before in work:
                a = _each(jnp.exp, logsig - (c_sp + within))
                if diagonal:
                    a = _keep(before, a)
                acc = acc + _each(_split_dot, a, _heads_of(v_ref, keys))
                c_sp = c_sp + _rowsum(spm)
            return tuple(c_sp.vals), tuple(acc.vals)

        zeros = lambda width: tuple(jnp.zeros((ATT_TQ, width), F32) for _ in range(ATT_HEADS))
        carry = step(i, (zeros(1), zeros(HEAD_DIM)), True)
        _, acc = lax.fori_loop(0, i, lambda it, cr: step(i - 1 - it, cr, False), carry)
        for cs, acc_h in zip(_HEAD_COLS, acc):
            o16_ref[:, cs] = acc_h.astype(BF16)
            o32_ref[:, cs] = acc_h

    q_spec, k_spec, v_spec = _att_specs(n_heads, s)
    o_spec = pl.BlockSpec((ATT_TQ, ATT_WIDTH), lambda g, i: (i, g))
    return pl.pallas_call(
        body, name=name, grid=(n_heads // ATT_HEADS, s // ATT_TQ), in_specs=[q_spec, k_spec, v_spec],
        out_specs=[o_spec, o_spec],
        out_shape=[jax.ShapeDtypeStruct((s, n_heads * HEAD_DIM), BF16),
                   jax.ShapeDtypeStruct((s, n_heads * HEAD_DIM), F32)],
        compiler_params=_cparams("parallel", "arbitrary"),
    )(qkv, qkv, qkv)


def _sb_bwd(qkv, o32, do, n_heads, name):
    s = qkv.shape[0]

    def body(q_ref, k_ref, v_ref, o_ref, do_ref, dq_ref, dk_ref, dv_ref):
        i = pl.program_id(1)

        @pl.when(i == 0)
        def _():
            dk_ref[...] = jnp.zeros_like(dk_ref)
            dv_ref[...] = jnp.zeros_like(dv_ref)

        q, do = _heads_of(q_ref), _heads_of(do_ref)
        total = _rowsum(_each(lambda a, b: a.astype(F32) * b, do, _heads_of(o_ref)))
        row, col, jr, jc = _att_iotas()
        later = (jr > jc).astype(BF16)
        not_before = (jr >= jc).astype(BF16)

        def step(jb, carry, diagonal):
            c_sp, c_e, dq = (_Each(part) for part in carry)
            work = []
            for sub in reversed(range(ATT_SUB)):
                keys = pl.ds(pl.multiple_of(jb * ATT_TQ + sub * ATT_TK, ATT_TK), ATT_TK)
                k = _heads_of(k_ref, keys)
                z = _dot_each(q, k, "nt") * ATT_SCALE
                sp = _each(_softplus, z)
                before = (col + sub * ATT_TK) < row if diagonal else None
                spm = _keep(before, sp) if diagonal else sp
                work.append((keys, k, _each(jnp.exp, z - sp), spm, _each(lambda x: _dot(x, later, "nn"), spm),
                             _dot_each(do, _heads_of(v_ref, keys), "nt"), before))
            for keys, k, sig, spm, within, da, before in work:
                a = sig * _each(lambda x: jnp.exp(-x), c_sp + within)
                if diagonal:
                    a = _keep(before, a)
                e = a * da
                left = total - c_e - _each(lambda x: _split_dot(x, not_before), e)
                dz = (e - (e + left) * sig) * ATT_SCALE
                if diagonal:
                    dz = _keep(before, dz)
                dk, dv = _dot_each(dz, q, "tn"), _dot_each(a, do, "tn")
                for cs, dk_h, dv_h in zip(_HEAD_COLS, dk.vals, dv.vals):
                    dk_ref[keys, cs] += dk_h
                    dv_ref[keys, cs] += dv_h
                dq = dq + _dot_each(dz, k, "nn")
                c_sp = c_sp + _rowsum(spm)
                c_e = c_e + _rowsum(e)
            return tuple(c_sp.vals), tuple(c_e.vals), tuple(dq.vals)

        zeros = lambda width: tuple(jnp.zeros((ATT_TQ, width), F32) for _ in range(ATT_HEADS))
        carry = step(i, (zeros(1), zeros(1), zeros(HEAD_DIM)), True)
        _, _, dq = lax.fori_loop(0, i, lambda it, cr: step(i - 1 - it, cr, False), carry)
        for cs, dq_h in zip(_HEAD_COLS, dq):
            dq_ref[:, cs] = dq_h.astype(BF16)

    q_spec, k_spec, v_spec = _att_specs(n_heads, s)
    blk = pl.BlockSpec((ATT_TQ, ATT_WIDTH), lambda g, i: (i, g))
    full = pl.BlockSpec((s, ATT_WIDTH), lambda g, i: (0, g))
    wide = (s, n_heads * HEAD_DIM)
    return pl.pallas_call(
        body, name=name, grid=(n_heads // ATT_HEADS, s // ATT_TQ), in_specs=[q_spec, k_spec, v_spec, blk, blk],
        out_specs=[blk, full, full],
        out_shape=[jax.ShapeDtypeStruct(wide, BF16), jax.ShapeDtypeStruct(wide, F32), jax.ShapeDtypeStruct(wide, F32)],
        compiler_params=_cparams("parallel", "arbitrary"),
    )(qkv, qkv, qkv, o32, do)


def _fox_logits(q, k, cq, ct_ref, keys):
    ck = _Each(ct_ref[h, :, keys] for h in range(ATT_HEADS))
    return _dot_each(q, k, "nt") * ATT_SCALE + (cq - ck)


def _fox_cq(c_ref, group):
    c = c_ref[...]
    return _Each(_lane_col(c, LANE_FORGET + group * ATT_HEADS + h) for h in range(ATT_HEADS))


def _fox_fwd(qkv, c, ct, name):
    s = qkv.shape[0]
    n_heads = N_FOX_HEADS

    def body(q_ref, k_ref, v_ref, c_ref, ct_ref, o_ref, lse_ref):
        g, i = pl.program_id(0), pl.program_id(1)
        q = _heads_of(q_ref)
        cq = _fox_cq(c_ref, g)
        row, col, _, _ = _att_iotas()

        def step(jb, carry, diagonal):
            m, l, acc = (_Each(part) for part in carry)
            work = []
            m_new = m
            for sub in range(ATT_SUB):
                keys = pl.ds(pl.multiple_of(jb * ATT_TQ + sub * ATT_TK, ATT_TK), ATT_TK)
                sc = _fox_logits(q, _heads_of(k_ref, keys), cq, ct_ref, keys)
                valid = (col + sub * ATT_TK) <= row if diagonal else None
                if diagonal:
                    sc = _each(lambda x: jnp.where(valid, x, -1e30), sc)
                m_new = _each(lambda a, x: jnp.maximum(a, jnp.max(x, axis=1, keepdims=True)), m_new, sc)
                work.append((keys, sc, valid))
            w = _each(jnp.exp, m - m_new)
            l, acc = l * w, acc * w
            for keys, sc, valid in work:
                p = _each(jnp.exp, sc - m_new)
                if diagonal:
                    p = _keep(valid, p)
                l = l + _rowsum(p)
                acc = acc + _each(_split_dot, p, _heads_of(v_ref, keys))
            return tuple(m_new.vals), tuple(l.vals), tuple(acc.vals)

        per_head = lambda width, value: tuple(jnp.full((ATT_TQ, width), value, F32) for _ in range(ATT_HEADS))
        init = (per_head(1, -1e30), per_head(1, 0.0), per_head(HEAD_DIM, 0.0))
        m, l, acc = lax.fori_loop(0, i, lambda jb, cr: step(jb, cr, False), step(i, init, True))
        for h, cs in enumerate(_HEAD_COLS):
            o_ref[:, cs] = acc[h] / l[h]
            lse_ref[h] = jnp.broadcast_to(m[h] + jnp.log(l[h]), (ATT_TQ, LANES))

    q_spec, k_spec, v_spec = _att_specs(n_heads, s)
    return pl.pallas_call(
        body, name=name, grid=(n_heads // ATT_HEADS, s // ATT_TQ),
        in_specs=[q_spec, k_spec, v_spec, pl.BlockSpec((ATT_TQ, LANES), lambda g, i: (i, 0)),
                  pl.BlockSpec((ATT_HEADS, 1, s), lambda g, i: (g, 0, 0))],
        out_specs=[pl.BlockSpec((ATT_TQ, ATT_WIDTH), lambda g, i: (i, g)),
                   pl.BlockSpec((ATT_HEADS, ATT_TQ, LANES), lambda g, i: (g, i, 0))],
        out_shape=[jax.ShapeDtypeStruct((s, n_heads * HEAD_DIM), F32),
                   jax.ShapeDtypeStruct((n_heads, s, LANES), F32)],
        compiler_params=_cparams("parallel", "arbitrary"),
    )(qkv, qkv, qkv, c, ct)


def _fox_bwd(qkv, c, ct, o, lse, do, name):
    s = qkv.shape[0]
    n_heads = N_FOX_HEADS

    def body(q_ref, k_ref, v_ref, c_ref, ct_ref, o_ref, lse_ref, do_ref, dq_ref, dk_ref, dv_ref, dct_ref):
        g, i = pl.program_id(0), pl.program_id(1)

        @pl.when(i == 0)
        def _():
            dk_ref[...] = jnp.zeros_like(dk_ref)
            dv_ref[...] = jnp.zeros_like(dv_ref)
            dct_ref[...] = jnp.zeros_like(dct_ref)

        q = _heads_of(q_ref)
        do16 = _each(lambda x: x.astype(BF16), _heads_of(do_ref))
        delta = _rowsum(_each(lambda a, b: a.astype(F32) * b, do16, _heads_of(o_ref)))
        lse_col = _Each(lse_ref[h, :, 0:1] for h in range(ATT_HEADS))
        cq = _fox_cq(c_ref, g)
        row, col, _, _ = _att_iotas()

        def step(jb, dq, diagonal):
            dq = _Each(dq)
            for sub in range(ATT_SUB):
                keys = pl.ds(pl.multiple_of(jb * ATT_TQ + sub * ATT_TK, ATT_TK), ATT_TK)
                k = _heads_of(k_ref, keys)
                sc = _fox_logits(q, k, cq, ct_ref, keys)
                if diagonal:
                    valid = (col + sub * ATT_TK) <= row
                    p = _keep(valid, _each(jnp.exp, _keep(valid, sc) - lse_col))
                else:
                    p = _each(jnp.exp, sc - lse_col)
                ds = p * (_dot_each(do16, _heads_of(v_ref, keys), "nt") - delta)
                dss = ds * ATT_SCALE
                dk, dv = _dot_each(dss, q, "tn"), _dot_each(p, do16, "tn")
                for h, cs in enumerate(_HEAD_COLS):
                    dct_ref[h, :, keys] -= jnp.sum(ds.vals[h], axis=0, keepdims=True)
                    dk_ref[keys, cs] += dk.vals[h]
                    dv_ref[keys, cs] += dv.vals[h]
                dq = dq + _dot_each(dss, k, "nn")
            return tuple(dq.vals)

        dq0 = step(i, tuple(jnp.zeros((ATT_TQ, HEAD_DIM), F32) for _ in range(ATT_HEADS)), True)
        dq = lax.fori_loop(0, i, lambda jb, dq: step(jb, dq, False), dq0)
        for cs, dq_h in zip(_HEAD_COLS, dq):
            dq_ref[:, cs] = dq_h

    q_spec, k_spec, v_spec = _att_specs(n_heads, s)
    blk = pl.BlockSpec((ATT_TQ, ATT_WIDTH), lambda g, i: (i, g))
    full = pl.BlockSpec((s, ATT_WIDTH), lambda g, i: (0, g))
    wide = jax.ShapeDtypeStruct((s, n_heads * HEAD_DIM), F32)
    return pl.pallas_call(
        body, name=name, grid=(n_heads // ATT_HEADS, s // ATT_TQ),
        in_specs=[q_spec, k_spec, v_spec, pl.BlockSpec((ATT_TQ, LANES), lambda g, i: (i, 0)),
                  pl.BlockSpec((ATT_HEADS, 1, s), lambda g, i: (g, 0, 0)), blk,
                  pl.BlockSpec((ATT_HEADS, ATT_TQ, LANES), lambda g, i: (g, i, 0)), blk],
        out_specs=[blk, full, full, pl.BlockSpec((ATT_HEADS, 1, s), lambda g, i: (g, 0, 0))],
        out_shape=[wide, wide, wide, jax.ShapeDtypeStruct((n_heads, 1, s), F32)],
        compiler_params=_cparams("parallel", "arbitrary"),
    )(qkv, qkv, qkv, c, ct, o, lse, do)


def _cumsum_rows(x, reverse, name):
    s = x.shape[0]
    nb = s // LANES

    def body(x_ref, o_ref):
        r = lax.broadcasted_iota(jnp.int32, (LANES, LANES), 0)
        c = lax.broadcasted_iota(jnp.int32, (LANES, LANES), 1)
        tri = ((r <= c) if reverse else (r >= c)).astype(F32)

        def step(it, carry):
            b = (nb - 1 - it) if reverse else it
            off = pl.multiple_of(b * LANES, LANES)
            blk = x_ref[pl.ds(off, LANES), :]
            o_ref[pl.ds(off, LANES), :] = _dot32(tri, blk) + carry
            return carry + jnp.sum(blk, axis=0, keepdims=True)

        lax.fori_loop(0, nb, step, jnp.zeros((1, LANES), F32))

    return pl.pallas_call(body, name=name, out_shape=jax.ShapeDtypeStruct(x.shape, F32),
                          compiler_params=pltpu.CompilerParams(vmem_limit_bytes=V7X_VMEM_LIMIT))(x)


def _dot32_each(a, b, kind="nn"):
    return _each(lambda x, y: _dot32(x, y, kind), a, b)


def _unit_lower_inverse(m, ri, ci):
    c = ri.shape[0]
    t = -_keep(ri // 2 == ci // 2, m) + jnp.where(ri == ci, 1.0, 0.0)
    b = 4
    while b <= c:
        off_diag = (ri // b == ci // b) & (ri % b >= b // 2) & (ci % b < b // 2)
        t = t - _dot32_each(_dot32_each(t, _keep(off_diag, m)), t)
        b *= 2
    return t


def _dn_gates(g, ri, ci):
    eye = ri == ci
    incl = ri >= ci
    g_row = jnp.sum(jnp.where(eye, g, 0.0), axis=0, keepdims=True)
    gc = jnp.sum(jnp.where(incl, g_row, 0.0), axis=1, keepdims=True)
    gc_row = jnp.sum(jnp.where(eye, gc, 0.0), axis=0, keepdims=True)
    dmat = jnp.where(incl, jnp.exp(jnp.where(incl, gc - gc_row, 0.0)), 0.0)
    gc_last = jnp.sum(g, axis=0, keepdims=True)
    return gc, dmat, jnp.exp(gc), jnp.exp(gc_last - gc), jnp.exp(gc_last)


def _dn_fwd(qkv, act, name):
    s = qkv.shape[0]
    c, d, nh = DN_CHUNK, HEAD_DIM, N_DN_HEADS
    nc = s // c

    def body(q_ref, k_ref, v_ref, act_ref, o_ref, s_ref, t_ref, state):
        @pl.when(pl.program_id(0) == 0)
        def _():
            state[...] = jnp.zeros_like(state)

        ri = lax.broadcasted_iota(jnp.int32, (c, c), 0)
        ci = lax.broadcasted_iota(jnp.int32, (c, c), 1)
        act = act_ref[...]
        heads = range(nh)
        cols = [slice(h * d, (h + 1) * d) for h in heads]
        q, k, v = (_Each(ref[:, cs] for cs in cols) for ref in (q_ref, k_ref, v_ref))
        beta = _Each(_lane_col(act, LANE_BETA + h) for h in heads)
        g = _Each(_lane_col(act, LANE_DECAY + h) for h in heads)
        _, dmat, e, r, gl = _each(lambda gh: _dn_gates(gh, ri, ci), g)
        s0 = _Each(state[h] for h in heads)
        kb = beta * k
        t = _unit_lower_inverse(_keep(ri > ci, _dot32_each(kb, k, "nt") * dmat), ri, ci)
        vn = _dot32_each(t, beta * v) - _dot32_each(_dot32_each(t, kb * e), s0)
        o = _dot32_each(q * e, s0) + _dot32_each(_dot32_each(q, k, "nt") * dmat, vn)
        s1 = s0 * gl + _dot32_each(k * r, vn, "tn")
        for h in heads:
            o_ref[:, cols[h]] = o.vals[h]
            state[h] = s1.vals[h]
            s_ref[h] = s0.vals[h]
            t_ref[h] = t.vals[h]

    wide = lambda part: pl.BlockSpec((c, nh * d), lambda n: (n, part))
    return pl.pallas_call(
        body, name=name, grid=(nc,),
        in_specs=[wide(0), wide(1), wide(2), pl.BlockSpec((c, LANES), lambda n: (n, 0))],
        out_specs=[wide(0), pl.BlockSpec((nh, None, d, d), lambda n: (0, n, 0, 0)),
                   pl.BlockSpec((nh, None, c, c), lambda n: (0, n, 0, 0))],
        out_shape=[jax.ShapeDtypeStruct((s, nh * d), F32), jax.ShapeDtypeStruct((nh, nc, d, d), F32),
                   jax.ShapeDtypeStruct((nh, nc, c, c), F32)],
        scratch_shapes=[pltpu.VMEM((nh, d, d), F32)],
        compiler_params=_cparams("arbitrary"),
    )(qkv, qkv, qkv, act)


def _dn_bwd(qkv, act, states, tinv, do, name):
    s = qkv.shape[0]
    c, d, nh = DN_CHUNK, HEAD_DIM, N_DN_HEADS
    nc = s // c

    def chunk_bwd(q, k, v, do, beta, g, s0, t, ds_out):
        ri = lax.broadcasted_iota(jnp.int32, (c, c), 0)
        ci = lax.broadcasted_iota(jnp.int32, (c, c), 1)
        eye, incl, strict = ri == ci, ri >= ci, ri > ci
        gc, dmat, e, r, gl = _each(lambda gh: _dn_gates(gh, ri, ci), g)
        dot = _dot32_each
        rowsum = lambda x: _each(lambda a: jnp.sum(a, axis=1, keepdims=True), x)
        colsum = lambda x: _each(lambda a: jnp.sum(a, axis=0, keepdims=True), x)
        total = lambda x: colsum(rowsum(x))
        to_col = lambda row: rowsum(_keep(eye, row))
        to_row = lambda colv: colsum(_keep(eye, colv))

        kb, vb = beta * k, beta * v
        kbe = kb * e
        u, w = dot(t, vb), dot(t, kbe)
        vn = u - dot(w, s0)
        qk = dot(q, k, "nt")
        p = qk * dmat
        gram = dot(k, k, "nt")
        kr, qe = k * r, q * e

        d_kr = dot(vn, ds_out, "nt")
        dvn = dot(kr, ds_out)
        dgl = total(s0 * ds_out)
        ds_in = ds_out * gl
        dk = d_kr * r
        dr = rowsum(d_kr * k)
        d_qe = dot(do, s0, "nt")
        ds_in = ds_in + dot(qe, do, "tn")
        dp = _keep(incl, dot(do, vn, "nt"))
        dvn = dvn + dot(p, do, "tn")
        dq = d_qe * e
        de = rowsum(d_qe * q)
        dqk = dp * dmat
        dq = dq + dot(dqk, k)
        dk = dk + dot(dqk, q, "tn")
        dd = dp * qk
        dw = -dot(dvn, s0, "nt")
        ds_in = ds_in - dot(w, dvn, "tn")
        dvb = dot(t, dvn, "tn")
        dkbe = dot(t, dw, "tn")
        dm = -_keep(strict, dot(dvb, u, "nt") + dot(dkbe, w, "nt"))
        dbeta = rowsum(dm * gram * dmat)
        dgram = dm * beta * dmat
        dd = dd + dm * beta * gram
        dk = dk + dot(dgram, k) + dot(dgram, k, "tn")
        dkb = dkbe * e
        de = de + rowsum(dkbe * kb)
        dk = dk + beta * dkb
        dbeta = dbeta + rowsum(dkb * k) + rowsum(dvb * v)
        dv = beta * dvb
        wd = dd * dmat
        dgc = rowsum(wd) - to_col(colsum(wd)) + de * e - dr * r
        dgc_last = total(dr * r) + dgl * gl
        dgc = dgc + _keep(ri[:, 0:1] == c - 1, dgc_last)
        dg = rowsum(_keep(ri <= ci, to_row(dgc)))
        return dq, dk, dv, dbeta, dg, ds_in

    def body(q_ref, k_ref, v_ref, act_ref, s_ref, t_ref, do_ref, dq_ref, dk_ref, dv_ref, dact_ref, dstate):
        @pl.when(pl.program_id(0) == 0)
        def _():
            dstate[...] = jnp.zeros_like(dstate)

        act = act_ref[...]
        heads = range(nh)
        cols = [slice(h * d, (h + 1) * d) for h in heads]
        q, k, v, do = (_Each(ref[:, cs] for cs in cols) for ref in (q_ref, k_ref, v_ref, do_ref))
        dq, dk, dv, dbeta, dg, ds_in = chunk_bwd(
            q, k, v, do, _Each(_lane_col(act, LANE_BETA + h) for h in heads),
            _Each(_lane_col(act, LANE_DECAY + h) for h in heads), _Each(s_ref[h] for h in heads),
            _Each(t_ref[h] for h in heads), _Each(dstate[h] for h in heads))
        lane = lax.broadcasted_iota(jnp.int32, (c, LANES), 1)
        dact = jnp.zeros((c, LANES), F32)
        for h in heads:
            dstate[h] = ds_in.vals[h]
            dq_ref[:, cols[h]], dk_ref[:, cols[h]], dv_ref[:, cols[h]] = dq.vals[h], dk.vals[h], dv.vals[h]
            dact = (dact + jnp.where(lane == LANE_BETA + h, dbeta.vals[h], 0.0)
                    + jnp.where(lane == LANE_DECAY + h, dg.vals[h], 0.0))
        dact_ref[...] = dact

    part = lambda p: pl.BlockSpec((c, nh * d), lambda n: (nc - 1 - n, p))
    per = lambda a, b: pl.BlockSpec((nh, None, a, b), lambda n: (0, nc - 1 - n, 0, 0))
    wide = jax.ShapeDtypeStruct((s, nh * d), F32)
    act_spec = pl.BlockSpec((c, LANES), lambda n: (nc - 1 - n, 0))
    return pl.pallas_call(
        body, name=name, grid=(nc,),
        in_specs=[part(0), part(1), part(2), act_spec, per(d, d), per(c, c), part(0)],
        out_specs=[part(0), part(0), part(0), act_spec],
        out_shape=[wide, wide, wide, jax.ShapeDtypeStruct((s, LANES), F32)],
        scratch_shapes=[pltpu.VMEM((nh, d, d), F32)],
        compiler_params=_cparams("arbitrary"),
    )(qkv, qkv, qkv, act, states, tinv, do)


EVEN_DN_QKV, EVEN_FOX_QKV, EVEN_DN_GATE, EVEN_FOX_GATE, EVEN_NARROW = 0, 1536, 3072, 3584, 4096
EVEN_WIDTH = 4224
CONV_TILE = 256
CONV_HALO = 8


def _conv_fwd(proj, w, name):
    s = proj.shape[0]
    t, cw = CONV_TILE, 3 * D_DN

    def body(cur_ref, prev_ref, w_ref, y_ref, xs):
        i = pl.program_id(0)
        xs[0:CONV_HALO, :] = jnp.where(i > 0, prev_ref[...], 0.0)
        xs[CONV_HALO:, :] = cur_ref[...]
        y = jnp.zeros((t, cw), F32)
        for tap in range(CONV_WIDTH):
            y = y + w_ref[tap:tap + 1, :] * xs[pl.ds(CONV_HALO - CONV_WIDTH + 1 + tap, t), :]
        y_ref[...] = y

    per = t // CONV_HALO
    return pl.pallas_call(
        body, name=name, grid=(s // t,),
        in_specs=[pl.BlockSpec((t, cw), lambda i: (i, 0)),
                  pl.BlockSpec((CONV_HALO, cw), lambda i: (jnp.maximum(i * per - 1, 0), 0)),
                  pl.BlockSpec((CONV_WIDTH, cw), lambda i: (0, 0))],
        out_specs=pl.BlockSpec((t, cw), lambda i: (i, 0)),
        out_shape=jax.ShapeDtypeStruct((s, cw), F32),
        scratch_shapes=[pltpu.VMEM((t + CONV_HALO, cw), F32)],
        compiler_params=_cparams("parallel"),
    )(proj, proj, w)


def _conv_bwd(proj, w, dy, name):
    s = proj.shape[0]
    t, cw = CONV_TILE, 3 * D_DN
    nt = s // t

    def body(cur_ref, prev_ref, w_ref, dy_ref, nxt_ref, dx_ref, dw_ref, xs, dys):
        i = pl.program_id(0)

        @pl.when(i == 0)
        def _():
            dw_ref[...] = jnp.zeros_like(dw_ref)

        xs[0:CONV_HALO, :] = jnp.where(i > 0, prev_ref[...], 0.0)
        xs[CONV_HALO:, :] = cur_ref[...]
        dys[0:t, :] = dy_ref[...]
        dys[t:, :] = jnp.where(i < nt - 1, nxt_ref[...], 0.0)
        dy = dy_ref[...]
        dx = jnp.zeros((t, cw), F32)
        for tap in range(CONV_WIDTH):
            dx = dx + w_ref[tap:tap + 1, :] * dys[pl.ds(CONV_WIDTH - 1 - tap, t), :]
            dw_ref[tap:tap + 1, :] += jnp.sum(dy * xs[pl.ds(CONV_HALO - CONV_WIDTH + 1 + tap, t), :], axis=0,
                                              keepdims=True)
        dx_ref[...] = dx.astype(BF16)

    per = t // CONV_HALO
    last = s // CONV_HALO - 1
    return pl.pallas_call(
        body, name=name, grid=(nt,),
        in_specs=[pl.BlockSpec((t, cw), lambda i: (i, 0)),
                  pl.BlockSpec((CONV_HALO, cw), lambda i: (jnp.maximum(i * per - 1, 0), 0)),
                  pl.BlockSpec((CONV_WIDTH, cw), lambda i: (0, 0)),
                  pl.BlockSpec((t, cw), lambda i: (i, 0)),
                  pl.BlockSpec((CONV_HALO, cw), lambda i: (jnp.minimum((i + 1) * per, last), 0))],
        out_specs=[pl.BlockSpec((t, cw), lambda i: (i, 0)), pl.BlockSpec((CONV_WIDTH, cw), lambda i: (0, 0))],
        out_shape=[jax.ShapeDtypeStruct((s, cw), BF16), jax.ShapeDtypeStruct((CONV_WIDTH, cw), F32)],
        scratch_shapes=[pltpu.VMEM((t + CONV_HALO, cw), F32), pltpu.VMEM((t + CONV_HALO, cw), F32)],
        compiler_params=_cparams("arbitrary"),
    )(proj, proj, w, dy, dy)


def _heads(x, n):
    return [x[:, HEAD_DIM * h:HEAD_DIM * (h + 1)] for h in range(n)]


def _dn_pre_fwd(y, name):
    def fn(yb):
        cs = yb * _sigmoid(yb)
        out = []
        for idx, xh in enumerate(_heads(cs, 3 * N_DN_HEADS)):
            if idx < 2 * N_DN_HEADS:
                xh = xh * lax.rsqrt(jnp.sum(xh * xh, axis=-1, keepdims=True) + EPS)
                if idx < N_DN_HEADS:
                    xh = xh * ATT_SCALE
            out.append(xh)
        return (jnp.concatenate(out, axis=1),)
    return _rowwise(fn, [y], [], [(y.shape[1], F32)], [], tile=256, name=name)[0]


def _dn_pre_bwd(y, dq, dk, dv, name):
    def fn(yb, dqb, dkb, dvb):
        sg = _sigmoid(yb)
        cs = yb * sg
        dout = _heads(dqb, N_DN_HEADS) + _heads(dkb, N_DN_HEADS) + _heads(dvb, N_DN_HEADS)
        dcs = []
        for idx, (xh, dh) in enumerate(zip(_heads(cs, 3 * N_DN_HEADS), dout)):
            if idx < 2 * N_DN_HEADS:
                if idx < N_DN_HEADS:
                    dh = dh * ATT_SCALE
                r = lax.rsqrt(jnp.sum(xh * xh, axis=-1, keepdims=True) + EPS)
                xhat = xh * r
                dh = r * (dh - xhat * jnp.sum(xhat * dh, axis=-1, keepdims=True))
            dcs.append(dh)
        return (jnp.concatenate(dcs, axis=1) * _silu_grad(yb, sg),)
    return _rowwise(fn, [y, dq, dk, dv], [], [(y.shape[1], F32)], [], tile=256, name=name)[0]


def _narrow_params(a_log, dt_bias, f_bias):
    lanes = lambda a, first: jnp.pad(a.reshape(1, -1), ((0, 0), (first, LANES - first - a.shape[0])))
    return jnp.concatenate([lanes(a_log, LANE_DECAY), lanes(dt_bias, LANE_DECAY), lanes(f_bias, LANE_FORGET),
                            jnp.zeros((5, LANES), F32)], axis=0)


def _narrow_masks(shape):
    lane = lax.broadcasted_iota(jnp.int32, shape, 1)
    is_beta = lane < LANE_DECAY
    is_decay = (lane >= LANE_DECAY) & (lane < LANE_FORGET)
    is_forget = (lane >= LANE_FORGET) & (lane < LANE_FORGET + N_FOX_HEADS)
    return is_beta, is_decay, is_forget


def _narrow_fwd(proj, params, name):
    def fn(sm, pk):
        is_beta, is_decay, is_forget = _narrow_masks(sm.shape)
        g = -jnp.exp(pk[0:1, :]) * _softplus(sm + pk[1:2, :])
        logf = -_softplus(-(sm + pk[2:3, :]))
        return (jnp.where(is_beta, _sigmoid(sm), jnp.where(is_decay, g, jnp.where(is_forget, logf, 0.0))),)
    return _rowwise(fn, [(proj, LANES, EVEN_NARROW // LANES)], [params], [(LANES, F32)], [], tile=512, name=name)[0]


def _narrow_bwd(proj, params, act, dact, dlogf, name):
    def fn(sm, ab, da, dl, pk):
        is_beta, is_decay, is_forget = _narrow_masks(sm.shape)
        db = jnp.where(is_forget, dl, da)
        d_beta = db * ab * (1.0 - ab)
        d_decay = db * (-jnp.exp(pk[0:1, :])) * _sigmoid(sm + pk[1:2, :])
        d_forget = db * _sigmoid(-(sm + pk[2:3, :]))
        dsm = jnp.where(is_beta, d_beta, jnp.where(is_decay, d_decay, jnp.where(is_forget, d_forget, 0.0)))
        col = lambda x: jnp.sum(x, axis=0, keepdims=True)
        return (dsm, col(jnp.where(is_decay, db * ab, 0.0)), col(jnp.where(is_decay, dsm, 0.0)),
                col(jnp.where(is_forget, dsm, 0.0)))
    return _rowwise(fn, [(proj, LANES, EVEN_NARROW // LANES), act, dact, dlogf], [params], [(LANES, BF16)],
                    [(1, LANES)] * 3, tile=512, name=name)


def _head_rms(xh):
    r = lax.rsqrt(jnp.mean(xh * xh, axis=-1, keepdims=True) + EPS)
    return xh * r, r


def _fox_pre_fwd(proj, qg, kg, name):
    def fn(pf, qgb, kgb):
        out = []
        for idx, xh in enumerate(_heads(pf, 3 * N_FOX_HEADS)):
            if idx < 2 * N_FOX_HEADS:
                xh = _head_rms(xh)[0] * (qgb if idx < N_FOX_HEADS else kgb)
            out.append(xh)
        return (jnp.concatenate(out, axis=1),)
    return _rowwise(fn, [(proj, 3 * D_FOX, EVEN_FOX_QKV // (3 * D_FOX))], [qg, kg], [(3 * D_FOX, BF16)], [],
                    tile=256, name=name)[0]


def _fox_pre_bwd(proj, qg, kg, dq, dk, dv, name):
    def fn(pf, dqb, dkb, dvb, qgb, kgb):
        dout = _heads(dqb, N_FOX_HEADS) + _heads(dkb, N_FOX_HEADS) + _heads(dvb, N_FOX_HEADS)
        dg = [jnp.zeros((1, HEAD_DIM), F32), jnp.zeros((1, HEAD_DIM), F32)]
        dx = []
        for idx, (xh, dh) in enumerate(zip(_heads(pf, 3 * N_FOX_HEADS), dout)):
            if idx < 2 * N_FOX_HEADS:
                which = 0 if idx < N_FOX_HEADS else 1
                xhat, r = _head_rms(xh)
                dg[which] = dg[which] + jnp.sum(dh * xhat, axis=0, keepdims=True)
                dxh = dh * (qgb if which == 0 else kgb)
                dh = r * (dxh - xhat * jnp.mean(dxh * xhat, axis=-1, keepdims=True))
            dx.append(dh)
        return jnp.concatenate(dx, axis=1), dg[0], dg[1]
    return _rowwise(fn, [(proj, 3 * D_FOX, EVEN_FOX_QKV // (3 * D_FOX)), dq, dk, dv], [qg, kg],
                    [(3 * D_FOX, BF16)], [(1, HEAD_DIM)] * 2, tile=256, name=name)


def _mix_gate_fwd(proj, o_dn, o_fox, ng, name):
    def fn(gd, gf, od, of, ngb):
        dn = [_head_rms(xh)[0] * ngb for xh in _heads(od, N_DN_HEADS)]
        return (jnp.concatenate([jnp.concatenate(dn, axis=1) * gd * _sigmoid(gd), of * _sigmoid(gf)], axis=1),)
    return _rowwise(fn, [(proj, D_DN, EVEN_DN_GATE // D_DN), (proj, D_FOX, EVEN_FOX_GATE // D_FOX), o_dn, o_fox],
                    [ng], [(D_DN + D_FOX, BF16)], [], tile=256, name=name)[0]


def _mix_gate_bwd(proj, o_dn, o_fox, ng, dom, name):
    def fn(gd, gf, od, of, dm, ngb):
        d_dn, d_fox = dm[:, :D_DN], dm[:, D_DN:]
        sgd, sgf = _sigmoid(gd), _sigmoid(gf)
        don = d_dn * gd * sgd
        dng = jnp.zeros((1, HEAD_DIM), F32)
        dod, normed = [], []
        for xh, dh in zip(_heads(od, N_DN_HEADS), _heads(don, N_DN_HEADS)):
            xhat, r = _head_rms(xh)
            dng = dng + jnp.sum(dh * xhat, axis=0, keepdims=True)
            dxh = dh * ngb
            dod.append(r * (dxh - xhat * jnp.mean(dxh * xhat, axis=-1, keepdims=True)))
            normed.append(xhat * ngb)
        d_gd = d_dn * jnp.concatenate(normed, axis=1) * _silu_grad(gd, sgd)
        d_gf = d_fox * of * sgf * (1.0 - sgf)
        return jnp.concatenate(dod, axis=1), d_fox * sgf, d_gd, d_gf, dng
    return _rowwise(fn, [(proj, D_DN, EVEN_DN_GATE // D_DN), (proj, D_FOX, EVEN_FOX_GATE // D_FOX), o_dn, o_fox, dom],
                    [ng], [(D_DN, F32), (D_FOX, F32), (D_DN, BF16), (D_FOX, BF16)], [(1, HEAD_DIM)], tile=256,
                    name=name)


def _loss_grad(y, target, name):
    d = y.shape[1]

    def fn(yb, tb):
        diff = yb - tb
        part = jnp.sum(jnp.sum(diff * diff, axis=1, keepdims=True), axis=0, keepdims=True) * (0.5 / d)
        g = diff * (1.0 / d)
        return g, g, part
    return _rowwise(fn, [y, target], [], [(d, F32), (d, BF16)], [(1, 1)], tile=512, name=name)


_REF_EVEN = {"dn_qkv": (0, 1536), "dn_gate": (1536, 2048), "dn_ba": (2048, 2056), "fox_qkv": (2056, 3592),
             "fox_gate": (3592, 4104), "f_pre": (4104, 4108)}
D_IN_EVEN = 4108


def _even_to_kernel_layout(w):
    cut = lambda name: w[..., _REF_EVEN[name][0]:_REF_EVEN[name][1]]
    pad = jnp.zeros(w.shape[:-1] + (EVEN_WIDTH - EVEN_NARROW - 12,), w.dtype)
    return jnp.concatenate([cut("dn_qkv"), cut("fox_qkv"), cut("dn_gate"), cut("fox_gate"), cut("dn_ba"),
                            cut("f_pre"), pad], axis=-1)


def _even_from_kernel_layout(g):
    return jnp.concatenate([g[..., EVEN_DN_QKV:EVEN_FOX_QKV], g[..., EVEN_DN_GATE:EVEN_FOX_GATE],
                            g[..., EVEN_NARROW:EVEN_NARROW + 8], g[..., EVEN_FOX_QKV:EVEN_DN_GATE],
                            g[..., EVEN_FOX_GATE:EVEN_NARROW], g[..., EVEN_NARROW + 8:EVEN_NARROW + 12]], axis=-1)


def _forget_rows(c):
    return c[:, LANE_FORGET:LANE_FORGET + N_FOX_HEADS].T.reshape(N_FOX_HEADS, 1, c.shape[0])


def _forget_lanes(rows):
    s = rows.shape[2]
    return jnp.pad(rows.reshape(-1, s).T, ((0, 0), (LANE_FORGET, LANES - LANE_FORGET - N_FOX_HEADS)))


def _even_fwd(x, gain, w_in, w_out, j, p, tag):
    h = _rms_fwd(x, gain, f"{tag}_norm")
    proj = _mm(h, w_in, "nn", tm=512, tn=EVEN_WIDTH // 3, out_dtype=F32, name=f"{tag}_in", b_lead=(j,))
    y = _conv_fwd(proj, p["conv_w"], f"{tag}_conv")
    dn_qkv = _dn_pre_fwd(y, f"{tag}_dn_pre")
    act = _narrow_fwd(proj, p["narrow"], f"{tag}_narrow")
    o_dn, states, tinv = _dn_fwd(dn_qkv, act, f"{tag}_delta")
    fox_qkv = _fox_pre_fwd(proj, p["q_g"], p["k_g"], f"{tag}_fox_pre")
    c = _cumsum_rows(act, False, f"{tag}_cumsum")
    ct = _forget_rows(c)
    o_fox, lse = _fox_fwd(fox_qkv, c, ct, f"{tag}_fox")
    om = _mix_gate_fwd(proj, o_dn, o_fox, p["dn_norm_g"], f"{tag}_gate")
    x2 = _mm(om, w_out, "nn", tm=512, tn=x.shape[1], out_dtype=F32, name=f"{tag}_out", residual=x, b_lead=(j,))
    return x2, (x, h, proj, y, dn_qkv, act, states, tinv, o_dn, fox_qkv, c, ct, o_fox, lse, om)


def _even_bwd(dxo, dxo16, saved, gain, w_in, w_out, j, p, tag, g_in, g_out):
    x, h, proj, y, dn_qkv, act, states, tinv, o_dn, fox_qkv, c, ct, o_fox, lse, om = saved
    d = x.shape[1]
    dom = _mm(dxo16, w_out, "nt", tm=512, tn=d, out_dtype=F32, name=f"{tag}_out_bwd", b_lead=(j,))
    g_out = _mm(om, dxo16, "tn", tm=512, tn=d, out_dtype=F32, name=f"{tag}_out_dw", into=(g_out, j))
    d_odn, d_ofox, d_gd, d_gf, d_ng = _mix_gate_bwd(proj, o_dn, o_fox, p["dn_norm_g"], dom, f"{tag}_gate_bwd")
    dq, dk, dv, dct = _fox_bwd(fox_qkv, c, ct, o_fox, lse, d_ofox, f"{tag}_fox_bwd")
    d_fox_qkv, d_qg, d_kg = _fox_pre_bwd(proj, p["q_g"], p["k_g"], dq, dk, dv, f"{tag}_fox_pre_bwd")
    dlogf = _cumsum_rows(_forget_lanes(dct), True, f"{tag}_cumsum_bwd")
    dq, dk, dv, dact = _dn_bwd(dn_qkv, act, states, tinv, d_odn, f"{tag}_delta_bwd")
    dy = _dn_pre_bwd(y, dq, dk, dv, f"{tag}_dn_pre_bwd")
    d_dn_qkv, d_conv = _conv_bwd(proj, p["conv_w"], dy, f"{tag}_conv_bwd")
    d_narrow, s_alog, s_dt, s_fb = _narrow_bwd(proj, p["narrow"], act, dact, dlogf, f"{tag}_narrow_bwd")
    dproj = jnp.concatenate([d_dn_qkv, d_fox_qkv, d_gd, d_gf, d_narrow], axis=1)
    dh = _mm(dproj, w_in, "nt", tm=512, tn=d, out_dtype=F32, name=f"{tag}_in_bwd", b_lead=(j,))
    g_in = _mm(h, dproj, "tn", tm=512, tn=EVEN_WIDTH // 3, out_dtype=F32, name=f"{tag}_in_dw", into=(g_in, j))
    dx, dx16, d_gain = _rms_bwd(x, gain, dh, dxo, f"{tag}_norm_bwd")
    small = {"conv_w": d_conv, "a_log": s_alog, "dt_bias": s_dt, "f_bias": s_fb, "dn_norm_g": d_ng, "q_g": d_qg,
             "k_g": d_kg}
    return dx, dx16, d_gain, small, g_in, g_out


def _odd_fwd(x, gain, w_in, w_out, j, tag):
    h = _rms_fwd(x, gain, f"{tag}_norm")
    qkv = _mm(h, w_in, "nn", tm=512, tn=w_in.shape[2] // 2, out_dtype=BF16, name=f"{tag}_in", b_lead=(j,))
    o16, o32 = _sb_fwd(qkv, N_SB_HEADS, f"{tag}_sb")
    x2 = _mm(o16, w_out, "nn", tm=512, tn=x.shape[1], out_dtype=F32, name=f"{tag}_out", residual=x, b_lead=(j,))
    return x2, (x, h, qkv, o16, o32)


def _odd_bwd(dxo, dxo16, saved, gain, w_in, w_out, j, tag, g_in, g_out):
    x, h, qkv, o16, o32 = saved
    d = x.shape[1]
    do = _mm(dxo16, w_out, "nt", tm=512, tn=d, out_dtype=BF16, name=f"{tag}_out_bwd", b_lead=(j,))
    g_out = _mm(o16, dxo16, "tn", tm=512, tn=d, out_dtype=F32, name=f"{tag}_out_dw", into=(g_out, j))
    dq, dk, dv = _sb_bwd(qkv, o32, do, N_SB_HEADS, f"{tag}_sb_bwd")
    dqkv = jnp.concatenate([dq, dk.astype(BF16), dv.astype(BF16)], axis=1)
    dh = _mm(dqkv, w_in, "nt", tm=512, tn=d, out_dtype=F32, name=f"{tag}_in_bwd", b_lead=(j,))
    g_in = _mm(h, dqkv, "tn", tm=512, tn=w_in.shape[2] // 2, out_dtype=F32, name=f"{tag}_in_dw", into=(g_in, j))
    dx, dx16, d_gain = _rms_bwd(x, gain, dh, dxo, f"{tag}_norm_bwd")
    return dx, dx16, d_gain, g_in, g_out


def _forward_backward(x, target, w):
    depth = w["norm_ffn1"].shape[0]
    row = lambda a, l: a[l][None]

    def even_small(j):
        return {"conv_w": w["dn_conv_w"][j], "narrow": _narrow_params(w["dn_a_log"][j], w["dn_dt_bias"][j],
                                                                     w["fox_f_bias"][j]),
                "dn_norm_g": row(w["dn_norm_g"], j), "q_g": row(w["fox_q_norm_g"], j),
                "k_g": row(w["fox_k_norm_g"], j)}

    saved = []
    for l in range(depth):
        x, s1 = _ffn_fwd(x, row(w["norm_ffn1"], l), w["ffn1_w_gu"], w["ffn1_w_down"], l, "ffn1")
        if l % 2 == 0:
            x, s2 = _even_fwd(x, row(w["norm_mix"], l), w["w_in_even"], w["w_out_even"], l // 2, even_small(l // 2),
                              "even")
        else:
            x, s2 = _odd_fwd(x, row(w["norm_mix"], l), w["w_in_odd"], w["w_out_odd"], l // 2, "odd")
        x, s3 = _ffn_fwd(x, row(w["norm_ffn2"], l), w["ffn2_w_gu"], w["ffn2_w_down"], l, "ffn2")
        saved.append((s1, s2, s3))

    dx, dx16, loss = _loss_grad(x, target, "loss")

    big = {k: lax.empty(w[k].shape, F32) for k in ("ffn1_w_gu", "ffn1_w_down", "ffn2_w_gu", "ffn2_w_down",
                                                   "w_in_even", "w_out_even", "w_in_odd", "w_out_odd")}
    d_norm = {k: [None] * depth for k in ("norm_ffn1", "norm_mix", "norm_ffn2")}
    d_even = [None] * ((depth + 1) // 2)
    for l in reversed(range(depth)):
        s1, s2, s3 = saved[l]
        dx, dx16, d_norm["norm_ffn2"][l], big["ffn2_w_gu"], big["ffn2_w_down"] = _ffn_bwd(
            dx, dx16, s3, row(w["norm_ffn2"], l), w["ffn2_w_gu"], w["ffn2_w_down"], l, "ffn2", big["ffn2_w_gu"],
            big["ffn2_w_down"])
        if l % 2 == 0:
            dx, dx16, d_norm["norm_mix"][l], d_even[l // 2], big["w_in_even"], big["w_out_even"] = _even_bwd(
                dx, dx16, s2, row(w["norm_mix"], l), w["w_in_even"], w["w_out_even"], l // 2, even_small(l // 2),
                "even", big["w_in_even"], big["w_out_even"])
        else:
            dx, dx16, d_norm["norm_mix"][l], big["w_in_odd"], big["w_out_odd"] = _odd_bwd(
                dx, dx16, s2, row(w["norm_mix"], l), w["w_in_odd"], w["w_out_odd"], l // 2, "odd", big["w_in_odd"],
                big["w_out_odd"])
        dx, dx16, d_norm["norm_ffn1"][l], big["ffn1_w_gu"], big["ffn1_w_down"] = _ffn_bwd(
            dx, dx16, s1, row(w["norm_ffn1"], l), w["ffn1_w_gu"], w["ffn1_w_down"], l, "ffn1", big["ffn1_w_gu"],
            big["ffn1_w_down"])

    small = {k: jnp.concatenate(v, axis=0) for k, v in d_norm.items()}
    dec = slice(LANE_DECAY, LANE_DECAY + N_DN_HEADS)
    fgt = slice(LANE_FORGET, LANE_FORGET + N_FOX_HEADS)
    small["dn_conv_w"] = jnp.stack([e["conv_w"] for e in d_even])
    small["dn_a_log"] = jnp.concatenate([e["a_log"][:, dec] for e in d_even], axis=0)
    small["dn_dt_bias"] = jnp.concatenate([e["dt_bias"][:, dec] for e in d_even], axis=0)
    small["fox_f_bias"] = jnp.concatenate([e["f_bias"][:, fgt] for e in d_even], axis=0)
    small["dn_norm_g"] = jnp.concatenate([e["dn_norm_g"] for e in d_even], axis=0)
    small["fox_q_norm_g"] = jnp.concatenate([e["q_g"] for e in d_even], axis=0)
    small["fox_k_norm_g"] = jnp.concatenate([e["k_g"] for e in d_even], axis=0)
    return loss, dx, big, small


MESH = pl.DeviceIdType.MESH
ANY = pl.BlockSpec(memory_space=pl.ANY)


def _place():
    x, y, c = lax.axis_index("x"), lax.axis_index("y"), lax.axis_index("c")
    return x, y, c, [(1 - x, y), (x, 1 - y), (1 - x, 1 - y)]


def _remote(src, dst, send_sem, recv_sem, to):
    return pltpu.make_async_remote_copy(src_ref=src, dst_ref=dst, send_sem=send_sem, recv_sem=recv_sem,
                                        device_id=to, device_id_type=MESH)


def _aligned(start, multiple):
    return start if isinstance(start, int) else pl.multiple_of(start, multiple)


def _quarter(ref, kind, chip, half, rows, cols):
    k = 2 * chip[0] + chip[1]
    hr = rows // 2
    assert hr % 16 == 0 and cols % LANES == 0
    if kind == "col":
        return ref.at[:, pl.ds(_aligned(half * hr, 16), hr), pl.ds(_aligned(k * cols, LANES), cols)]
    return ref.at[:, pl.ds(_aligned(k * rows + half * hr, 16), hr), :]


def _place_quarter(shard, kind, kc, name):
    l, rows, cols = shard.shape
    tr = rows
    while tr * cols * 4 > (2 << 20) and tr % 32 == 0:
        tr //= 2
    nr = rows // tr
    if kind == "col":
        out_spec = pl.BlockSpec((None, tr, cols), lambda li, i, kc_ref: (li, i, kc_ref[0]))
        out_shape = (l, rows, 4 * cols)
    else:
        out_spec = pl.BlockSpec((None, tr, cols), lambda li, i, kc_ref: (li, kc_ref[0] * nr + i, 0))
        out_shape = (l, 4 * rows, cols)

    def body(kc_ref, x_ref, o_ref):
        o_ref[...] = x_ref[...].astype(BF16)

    return pl.pallas_call(
        body, name=name,
        grid_spec=pltpu.PrefetchScalarGridSpec(
            num_scalar_prefetch=1, grid=(l, nr),
            in_specs=[pl.BlockSpec((None, tr, cols), lambda li, i, kc_ref: (li, i, 0))], out_specs=out_spec),
        out_shape=jax.ShapeDtypeStruct(out_shape, BF16),
        compiler_params=_cparams("parallel", "parallel"),
    )(kc, shard)


def _gather_weights(wholes, kinds):
    n = len(wholes)

    def dims(ref, kind):
        _, r, cc = ref.shape
        return (r, cc // 4) if kind == "col" else (r // 4, cc)

    def body(*refs):
        bufs = refs[n:2 * n]
        send_sems, recv_sems = refs[2 * n:]
        x, y, c, chips = _place()
        sibling = (x, y, 1 - c)
        first, passed = [], []
        for t in range(n):
            rows, cols = dims(bufs[t], kinds[t])
            mine = _quarter(bufs[t], kinds[t], (x, y), c, rows, cols)
            for j, chip in enumerate(chips):
                cp = _remote(mine, mine, send_sems.at[t, j], recv_sems.at[t, j], (*chip, c))
                cp.start()
                first.append(cp)
        for j, chip in enumerate(chips):
            for t in range(n):
                rows, cols = dims(bufs[t], kinds[t])
                got = _quarter(bufs[t], kinds[t], chip, c, rows, cols)
                _remote(got, got, send_sems.at[t, j], recv_sems.at[t, j], (*chip, c)).wait_recv()
                cp = _remote(got, got, send_sems.at[t, 3 + j], recv_sems.at[t, 3 + j], sibling)
                cp.start()
                passed.append(cp)
        for j, chip in enumerate(chips):
            for t in range(n):
                rows, cols = dims(bufs[t], kinds[t])
                got = _quarter(bufs[t], kinds[t], chip, 1 - c, rows, cols)
                _remote(got, got, send_sems.at[t, 3 + j], recv_sems.at[t, 3 + j], sibling).wait_recv()
        for cp in first + passed:
            cp.wait_send()

    return pl.pallas_call(
        body, name="gather_weights", in_specs=[ANY] * n, out_specs=[ANY] * n,
        out_shape=[jax.ShapeDtypeStruct(a.shape, a.dtype) for a in wholes],
        input_output_aliases={t: t for t in range(n)},
        scratch_shapes=[pltpu.SemaphoreType.DMA((n, 6)), pltpu.SemaphoreType.DMA((n, 6))],
        compiler_params=pltpu.CompilerParams(has_side_effects=True),
    )(*wholes)


def _canonical(a, kind):
    l, r, c = a.shape
    return a.reshape(l, 1, r, c) if kind == "col" else a.reshape(l, 4, r // 4, c)


def _rs_sibling(parts):
    n = len(parts)

    def body(*refs):
        ins, outs = refs[:n], refs[n:2 * n]
        send_sems, recv_sems = refs[2 * n:]
        x, y, c, _ = _place()
        copies = []
        for t in range(n):
            hr = ins[t].shape[2] // 2
            src = ins[t].at[:, :, pl.ds(pl.multiple_of((1 - c) * hr, 8), hr), :]
            cp = _remote(src, outs[t], send_sems.at[t], recv_sems.at[t], (x, y, 1 - c))
            cp.start()
            copies.append(cp)
        for cp in copies:
            cp.wait()

    half = lambda a: jax.ShapeDtypeStruct(a.shape[:2] + (a.shape[2] // 2, a.shape[3]), a.dtype)
    return pl.pallas_call(
        body, name="reduce_sibling", in_specs=[ANY] * n, out_specs=[ANY] * n, out_shape=[half(a) for a in parts],
        scratch_shapes=[pltpu.SemaphoreType.DMA((n,)), pltpu.SemaphoreType.DMA((n,))],
        compiler_params=pltpu.CompilerParams(has_side_effects=True),
    )(*parts)


def _add_tile(rows, cols):
    tc = cols if cols <= 1536 else cols // 4
    tr = rows
    while tr * tc * 4 > (1 << 20) and tr % 16 == 0:
        tr //= 2
    return tr, tc


def _rs_add_sibling(part, got, c, name):
    l, a, hr, cols = got.shape
    tr, tc = _add_tile(hr, cols)
    nr = hr // tr

    def body(c_ref, p_ref, g_ref, o32_ref, o16_ref):
        s = p_ref[...] + g_ref[...]
        o32_ref[...] = s
        o16_ref[...] = s.astype(BF16)

    blk = (None, None, tr, tc)
    spec = pl.BlockSpec(blk, lambda li, ai, i, j, c_ref: (li, ai, i, j))
    return pl.pallas_call(
        body, name=name,
        grid_spec=pltpu.PrefetchScalarGridSpec(
            num_scalar_prefetch=1, grid=(l, a, nr, cols // tc),
            in_specs=[pl.BlockSpec(blk, lambda li, ai, i, j, c_ref: (li, ai, c_ref[0] * nr + i, j)), spec],
            out_specs=[spec, spec]),
        out_shape=[jax.ShapeDtypeStruct(got.shape, F32), jax.ShapeDtypeStruct(got.shape, BF16)],
        compiler_params=_cparams("parallel", "parallel", "parallel", "parallel"),
    )(c, part, got)


def _quarter4(ref, kind, chip, cols):
    k = 2 * chip[0] + chip[1]
    if kind == "col":
        return ref.at[:, :, :, pl.ds(pl.multiple_of(k * cols, LANES), cols)]
    return ref.at[:, pl.ds(k, 1), :, :]


def _rs_chips(sums16, kinds):
    n = len(sums16)

    def qshape(a, kind):
        l, na, hr, cols = a.shape
        return (l, 1, hr, cols // 4 if kind == "col" else cols)

    def body(*refs):
        ins, outs = refs[:n], refs[n:2 * n]
        send_sems, recv_sems = refs[2 * n:]
        x, y, c, chips = _place()
        copies = []
        for t in range(n):
            cols = qshape(ins[t], kinds[t])[3]
            for j, chip in enumerate(chips):
                cp = _remote(_quarter4(ins[t], kinds[t], chip, cols), outs[t].at[j], send_sems.at[t, j],
                             recv_sems.at[t, j], (*chip, c))
                cp.start()
                copies.append(cp)
        for cp in copies:
            cp.wait()

    return pl.pallas_call(
        body, name="reduce_chips", in_specs=[ANY] * n, out_specs=[ANY] * n,
        out_shape=[jax.ShapeDtypeStruct((3,) + qshape(a, k), a.dtype) for a, k in zip(sums16, kinds)],
        scratch_shapes=[pltpu.SemaphoreType.DMA((n, 3)), pltpu.SemaphoreType.DMA((n, 3))],
        compiler_params=pltpu.CompilerParams(has_side_effects=True),
    )(*sums16)


def _rs_add_chips(sum32, got, kind, kc, name):
    _, l, _, hr, cols = got.shape
    tr, _ = _add_tile(hr, cols)
    nr = hr // tr
    k_arr, c_arr = kc
    if kind == "col":
        own = pl.BlockSpec((None, None, tr, cols), lambda li, i, k_ref, c_ref: (li, 0, i, k_ref[0]))
    else:
        own = pl.BlockSpec((None, None, tr, cols), lambda li, i, k_ref, c_ref: (li, k_ref[0], i, 0))

    def body(k_ref, c_ref, own_ref, got_ref, o_ref):
        o_ref[...] = ((own_ref[...] + got_ref[0].astype(F32)) + got_ref[1].astype(F32)) + got_ref[2].astype(F32)

    return pl.pallas_call(
        body, name=name,
        grid_spec=pltpu.PrefetchScalarGridSpec(
            num_scalar_prefetch=2, grid=(l, nr),
            in_specs=[own, pl.BlockSpec((3, None, None, tr, cols), lambda li, i, k_ref, c_ref: (0, li, 0, i, 0))],
            out_specs=pl.BlockSpec((None, tr, cols), lambda li, i, k_ref, c_ref: (li, c_ref[0] * nr + i, 0))),
        out_shape=jax.ShapeDtypeStruct((l, 2 * hr, cols), F32),
        compiler_params=_cparams("parallel", "parallel"),
    )(k_arr, c_arr, sum32, got)


def _rs_finish(quarters):
    n = len(quarters)

    def body(*refs):
        bufs = refs[n:2 * n]
        send_sems, recv_sems = refs[2 * n:]
        x, y, c, _ = _place()
        copies = []
        for t in range(n):
            hr = bufs[t].shape[1] // 2
            mine = bufs[t].at[:, pl.ds(pl.multiple_of(c * hr, 8), hr), :]
            cp = _remote(mine, mine, send_sems.at[t], recv_sems.at[t], (x, y, 1 - c))
            cp.start()
            copies.append(cp)
        for cp in copies:
            cp.wait()

    return pl.pallas_call(
        body, name="reduce_finish", in_specs=[ANY] * n, out_specs=[ANY] * n,
        out_shape=[jax.ShapeDtypeStruct(a.shape, a.dtype) for a in quarters],
        input_output_aliases={t: t for t in range(n)},
        scratch_shapes=[pltpu.SemaphoreType.DMA((n,)), pltpu.SemaphoreType.DMA((n,))],
        compiler_params=pltpu.CompilerParams(has_side_effects=True),
    )(*quarters)


def _reduce_scatter(parts, kinds, tags):
    x, y, c = lax.axis_index("x"), lax.axis_index("y"), lax.axis_index("c")
    c_arr = jnp.reshape(c, (1,)).astype(jnp.int32)
    k_arr = (jnp.reshape(2 * x + y, (1,)).astype(jnp.int32), c_arr)
    canon = [_canonical(p, kind) for p, kind in zip(parts, kinds)]
    from_sibling = _rs_sibling(canon)
    sums = [_rs_add_sibling(p, g, c_arr, f"reduce_add_sibling_{tag}") for p, g, tag in zip(canon, from_sibling, tags)]
    from_chips = _rs_chips([s16 for _, s16 in sums], kinds)
    halves = [_rs_add_chips(s32, g, kind, k_arr, f"reduce_add_chips_{tag}")
              for (s32, _), g, kind, tag in zip(sums, from_chips, kinds, tags)]
    return _rs_finish(halves)


SMALL_PEERS = 7


def _small_exchange(pack):
    rows = pack.shape[0]

    def body(p_ref, slots_ref, total_ref, send_sems, recv_sems):
        x, y, c, _ = _place()
        me = 4 * x + 2 * y + c
        slots_ref[me] = p_ref[...]
        copies = []
        for p in range(1, SMALL_PEERS + 1):
            px, py, pc = (p >> 2) & 1, (p >> 1) & 1, p & 1
            peer = (1 - x if px else x, 1 - y if py else y, 1 - c if pc else c)
            cp = _remote(p_ref, slots_ref.at[me], send_sems.at[p - 1], recv_sems.at[p - 1], peer)
            cp.start()
            copies.append(cp)
        for cp in copies:
            cp.wait()
        total = slots_ref[0]
        for i in range(1, SMALL_PEERS + 1):
            total = total + slots_ref[i]
        total_ref[...] = total

    vmem = pl.BlockSpec(memory_space=pltpu.VMEM)
    return pl.pallas_call(
        body, name="small_exchange", in_specs=[vmem], out_specs=[vmem, vmem],
        out_shape=[jax.ShapeDtypeStruct((SMALL_PEERS + 1, rows, LANES), F32), jax.ShapeDtypeStruct((rows, LANES), F32)],
        scratch_shapes=[pltpu.SemaphoreType.DMA((SMALL_PEERS,)), pltpu.SemaphoreType.DMA((SMALL_PEERS,))],
        compiler_params=pltpu.CompilerParams(has_side_effects=True),
    )(pack)


def _pack(arrays):
    rows = []
    for a in arrays:
        flat = a.reshape(-1).astype(F32)
        rows.append(jnp.pad(flat, (0, (-flat.shape[0]) % LANES)).reshape(-1, LANES))
    out = jnp.concatenate(rows, axis=0)
    return jnp.pad(out, ((0, (-out.shape[0]) % 8), (0, 0)))


def _unpack(pack, shapes):
    out, r = [], 0
    for sh in shapes:
        size = math.prod(sh)
        nr = -(-size // LANES)
        out.append(pack[r:r + nr].reshape(-1)[:size].reshape(sh))
        r += nr
    return out


def _adamw(w, g, m, v, name):
    shape = w.shape
    to2d = lambda a: a.reshape(-1, shape[-1])
    rows = math.prod(shape[:-1])
    tile = 256 if rows % 256 == 0 else rows

    def fn(wb, gb, mb, vb):
        m2 = ADAM_B1 * mb + (1.0 - ADAM_B1) * gb
        v2 = ADAM_B2 * vb + (1.0 - ADAM_B2) * (gb * gb)
        m_hat = m2 / (1.0 - ADAM_B1 ** ADAM_STEP)
        v_hat = v2 / (1.0 - ADAM_B2 ** ADAM_STEP)
        return -ADAM_LR * (m_hat / (jnp.sqrt(v_hat) + ADAM_EPS) + ADAM_WD * wb), m2, v2

    res = _rowwise(fn, [to2d(w), to2d(g), to2d(m), to2d(v)], [], [(shape[-1], F32)] * 3, [], tile=tile, name=name)
    return [r.reshape(shape) for r in res]


BIG = (("ffn1_w_gu", "col"), ("ffn1_w_down", "row"), ("w_in_even", "col"), ("w_out_even", "row"),
       ("w_in_odd", "col"), ("w_out_odd", "row"), ("ffn2_w_gu", "col"), ("ffn2_w_down", "row"))
SMALL = ("norm_ffn1", "norm_mix", "dn_conv_w", "dn_a_log", "dn_dt_bias", "dn_norm_g", "fox_q_norm_g", "fox_k_norm_g",
         "fox_f_bias", "norm_ffn2")
WEIGHTS = ("norm_ffn1", "ffn1_w_gu", "ffn1_w_down", "norm_mix", "w_in_even", "dn_conv_w", "dn_a_log", "dn_dt_bias",
           "dn_norm_g", "fox_q_norm_g", "fox_k_norm_g", "fox_f_bias", "w_out_even", "w_in_odd", "w_out_odd",
           "norm_ffn2", "ffn2_w_gu", "ffn2_w_down")
EVEN_QUARTER = 1027
EVEN_QUARTER_PAD = 1152


def _step(x, target, w, m, v):
    k = 2 * lax.axis_index("x") + lax.axis_index("y")
    n_conv = w["dn_conv_w"].shape[2]

    kc = jnp.reshape(k, (1,)).astype(jnp.int32)
    placed = []
    for name, kind in BIG:
        a = w[name]
        if name == "w_in_even":
            a = jnp.pad(a, ((0, 0), (0, 0), (0, EVEN_QUARTER_PAD - EVEN_QUARTER)))
        placed.append(_place_quarter(a, kind, kc, f"place_{name}"))
    whole = dict(zip([n for n, _ in BIG], _gather_weights(placed, [kind for _, kind in BIG])))
    padded = whole["w_in_even"]
    ref_order = jnp.concatenate([padded[..., q * EVEN_QUARTER_PAD:q * EVEN_QUARTER_PAD + EVEN_QUARTER]
                                 for q in range(4)], axis=-1)
    whole["w_in_even"] = _even_to_kernel_layout(ref_order)
    conv_slots, _ = _small_exchange(_pack([w["dn_conv_w"]]))
    conv_rows = math.prod(w["dn_conv_w"].shape) // LANES
    quarters = [conv_slots[2 * q, :conv_rows].reshape(w["dn_conv_w"].shape) for q in range(4)]
    whole["dn_conv_w"] = jnp.concatenate(quarters, axis=-1)
    for name in SMALL:
        if name != "dn_conv_w":
            whole[name] = w[name]

    loss, dx, big, small = _forward_backward(x, target, whole)

    g_even = _even_from_kernel_layout(big["w_in_even"])
    big["w_in_even"] = jnp.concatenate(
        [jnp.pad(g_even[..., q * EVEN_QUARTER:(q + 1) * EVEN_QUARTER],
                 ((0, 0), (0, 0), (0, EVEN_QUARTER_PAD - EVEN_QUARTER))) for q in range(4)], axis=-1)
    names = [n for n, _ in BIG]
    reduced = dict(zip(names, _reduce_scatter([big[n] for n in names], [kind for _, kind in BIG], names)))
    reduced["w_in_even"] = reduced["w_in_even"][..., :EVEN_QUARTER]
    _, small_sum = _small_exchange(_pack([small[n] for n in SMALL]))
    grads = dict(zip(SMALL, _unpack(small_sum, [small[n].shape for n in SMALL])))
    grads["dn_conv_w"] = lax.dynamic_slice_in_dim(grads["dn_conv_w"], k * n_conv, n_conv, axis=2)
    grads.update(reduced)

    delta, new_m, new_v = {}, {}, {}
    for name, _ in BIG:
        delta[name], new_m[name], new_v[name] = _adamw(w[name], grads[name], m[name], v[name], f"adamw_{name}")
    packs = [_pack([d[n] for n in SMALL]) for d in (w, grads, m, v)]
    shapes = [w[n].shape for n in SMALL]
    for out, res in zip((delta, new_m, new_v), _adamw(*packs, "adamw_small")):
        out.update(zip(SMALL, _unpack(res, shapes)))
    total_loss = lax.psum(loss[0, 0], ("x", "y", "c"))
    return total_loss, dx, grads, delta, new_m, new_v


def kernel(x, norm_ffn1, ffn1_w_gu, ffn1_w_down, norm_mix, w_in_even, dn_conv_w, dn_a_log, dn_dt_bias, dn_norm_g, fox_q_norm_g, fox_k_norm_g, fox_f_bias, w_out_even, w_in_odd, w_out_odd, norm_ffn2, ffn2_w_gu, ffn2_w_down, loss_target, m_norm_ffn1, m_ffn1_w_gu, m_ffn1_w_down, m_norm_mix, m_w_in_even, m_dn_conv_w, m_dn_a_log, m_dn_dt_bias, m_dn_norm_g, m_fox_q_norm_g, m_fox_k_norm_g, m_fox_f_bias, m_w_out_even, m_w_in_odd, m_w_out_odd, m_norm_ffn2, m_ffn2_w_gu, m_ffn2_w_down, v_norm_ffn1, v_ffn1_w_gu, v_ffn1_w_down, v_norm_mix, v_w_in_even, v_dn_conv_w, v_dn_a_log, v_dn_dt_bias, v_dn_norm_g, v_fox_q_norm_g, v_fox_k_norm_g, v_fox_f_bias, v_w_out_even, v_w_in_odd, v_w_out_odd, v_norm_ffn2, v_ffn2_w_gu, v_ffn2_w_down):
    w = dict(zip(WEIGHTS, (norm_ffn1, ffn1_w_gu, ffn1_w_down, norm_mix, w_in_even, dn_conv_w, dn_a_log, dn_dt_bias,
                           dn_norm_g, fox_q_norm_g, fox_k_norm_g, fox_f_bias, w_out_even, w_in_odd, w_out_odd,
                           norm_ffn2, ffn2_w_gu, ffn2_w_down)))
    m = dict(zip(WEIGHTS, (m_norm_ffn1, m_ffn1_w_gu, m_ffn1_w_down, m_norm_mix, m_w_in_even, m_dn_conv_w, m_dn_a_log,
                           m_dn_dt_bias, m_dn_norm_g, m_fox_q_norm_g, m_fox_k_norm_g, m_fox_f_bias, m_w_out_even,
                           m_w_in_odd, m_w_out_odd, m_norm_ffn2, m_ffn2_w_gu, m_ffn2_w_down)))
    v = dict(zip(WEIGHTS, (v_norm_ffn1, v_ffn1_w_gu, v_ffn1_w_down, v_norm_mix, v_w_in_even, v_dn_conv_w, v_dn_a_log,
                           v_dn_dt_bias, v_dn_norm_g, v_fox_q_norm_g, v_fox_k_norm_g, v_fox_f_bias, v_w_out_even,
                           v_w_in_odd, v_w_out_odd, v_norm_ffn2, v_ffn2_w_gu, v_ffn2_w_down)))
    loss, dx, grads, delta, new_m, new_v = _step(x[0], loss_target[0], w, m, v)
    return (loss, dx[None], *[grads[n] for n in WEIGHTS], *[delta[n] for n in WEIGHTS],
            *[new_m[n] for n in WEIGHTS], *[new_v[n] for n in WEIGHTS])
```

```python
import functools
import math

import jax
import jax.numpy as jnp
from jax import lax
from jax.experimental import pallas as pl
from jax.experimental.pallas import tpu as pltpu

F32 = jnp.float32
BF16 = jnp.bfloat16
HI = lax.Precision.HIGHEST

HEAD_DIM = 128
N_DN_HEADS = 4
N_FOX_HEADS = 4
N_SB_HEADS = 8
D_DN = N_DN_HEADS * HEAD_DIM
D_FOX = N_FOX_HEADS * HEAD_DIM
CONV_WIDTH = 4
DN_CHUNK = 64
EPS = 1e-6
ATT_SCALE = HEAD_DIM ** -0.5
ADAM_LR, ADAM_B1, ADAM_B2, ADAM_EPS, ADAM_WD, ADAM_STEP = 0.001, 0.9, 0.999, 1e-08, 0.01, 10

V7X_VMEM_LIMIT = 56 * 1024 * 1024
LANES = 128
ATT_TQ = 256
ATT_TK = 128
ATT_SUB = ATT_TQ // ATT_TK

LANE_BETA, LANE_DECAY, LANE_FORGET = 0, 4, 8


def _cparams(*sem):
    return pltpu.CompilerParams(dimension_semantics=sem, vmem_limit_bytes=V7X_VMEM_LIMIT)


def _sigmoid(x):
    return 1.0 / (1.0 + jnp.exp(-x))


def _softplus(x):
    return jnp.maximum(x, 0.0) + jnp.log(1.0 + jnp.exp(-jnp.abs(x)))


def _silu_grad(y, sg):
    return sg * (1.0 + y * (1.0 - sg))


def _rowwise(fn, rows, bcast, outs, sums, *, tile, name):
    rows = [r if isinstance(r, tuple) else (r, r.shape[1], 0) for r in rows]
    s = rows[0][0].shape[0]
    assert s % tile == 0
    n_in, n_b, n_out, n_sum = len(rows), len(bcast), len(outs), len(sums)

    def body(*refs):
        ins = [r[...] for r in refs[:n_in + n_b]]
        res = fn(*ins)
        if not isinstance(res, (tuple, list)):
            res = (res,)
        out_refs = refs[n_in + n_b:n_in + n_b + n_out]
        sum_refs = refs[n_in + n_b + n_out:]
        for o_ref, val in zip(out_refs, res[:n_out]):
            o_ref[...] = val.astype(o_ref.dtype)
        if n_sum:
            @pl.when(pl.program_id(0) == 0)
            def _():
                for s_ref in sum_refs:
                    s_ref[...] = jnp.zeros_like(s_ref)
            for s_ref, val in zip(sum_refs, res[n_out:]):
                s_ref[...] += val

    in_specs = [pl.BlockSpec((tile, w), lambda i, cb=cb: (i, cb)) for _, w, cb in rows]
    in_specs += [pl.BlockSpec(b.shape, lambda i, nd=b.ndim: (0,) * nd) for b in bcast]
    out_specs = [pl.BlockSpec((tile, c), lambda i: (i, 0)) for c, _ in outs]
    out_specs += [pl.BlockSpec(sh, lambda i: (0, 0)) for sh in sums]
    out_shape = [jax.ShapeDtypeStruct((s, c), dt) for c, dt in outs]
    out_shape += [jax.ShapeDtypeStruct(sh, F32) for sh in sums]
    return pl.pallas_call(
        body, name=name, grid=(s // tile,), in_specs=in_specs, out_specs=out_specs, out_shape=out_shape,
        compiler_params=_cparams("arbitrary" if n_sum else "parallel"),
    )(*[r[0] for r in rows], *bcast)


def _rms_fwd(x, gain, name):
    def fn(xb, g):
        r = lax.rsqrt(jnp.mean(xb * xb, axis=-1, keepdims=True) + EPS)
        return (xb * r * g,)
    return _rowwise(fn, [x], [gain], [(x.shape[1], BF16)], [], tile=512, name=name)[0]


def _rms_bwd(x, gain, dn, dres, name):
    def fn(xb, dnb, drb, g):
        r = lax.rsqrt(jnp.mean(xb * xb, axis=-1, keepdims=True) + EPS)
        xh = xb * r
        dxh = dnb * g
        dx = drb + r * (dxh - xh * jnp.mean(dxh * xh, axis=-1, keepdims=True))
        return dx, dx, jnp.sum(dnb * xh, axis=0, keepdims=True)
    d = x.shape[1]
    return _rowwise(fn, [x, dn, dres], [gain], [(d, F32), (d, BF16)], [(1, d)], tile=512, name=name)


_DIMS = {"nn": (((1,), (0,)), ((), ())), "nt": (((1,), (1,)), ((), ())), "tn": (((0,), (0,)), ((), ()))}


def _dot(a, b, kind):
    return lax.dot_general(a.astype(BF16), b.astype(BF16), _DIMS[kind], preferred_element_type=F32)


def _dot32(a, b, kind="nn"):
    return lax.dot_general(a, b, _DIMS[kind], precision=HI, preferred_element_type=F32)


def _mm(a, b, kind, *, tm, tn, out_dtype, name, scale=None, residual=None, a_lead=(), b_lead=(),
        b_spec=None, n=None, into=None):
    ash, bsh = a.shape[len(a_lead):], b.shape[len(b_lead):]
    m = ash[1] if kind == "tn" else ash[0]
    k = ash[0] if kind == "tn" else ash[1]
    if b_spec is None:
        n = bsh[0] if kind == "nt" else bsh[1]
        assert k == (bsh[1] if kind == "nt" else bsh[0]), (ash, bsh, kind)
    assert m % tm == 0 and n % tn == 0, (m, tm, n, tn)
    la, lb = (None,) * len(a_lead), (None,) * len(b_lead)
    if kind == "tn":
        a_spec = pl.BlockSpec(la + (k, tm), lambda j, i: a_lead + (0, i))
    else:
        a_spec = pl.BlockSpec(la + (tm, k), lambda j, i: a_lead + (i, 0))
    if b_spec is None:
        if kind == "nt":
            b_spec = pl.BlockSpec(lb + (tn, k), lambda j, i: b_lead + (j, 0))
        else:
            b_spec = pl.BlockSpec(lb + (k, tn), lambda j, i: b_lead + (0, j))
    in_specs, args = [a_spec, b_spec], [a, b]
    if residual is not None:
        in_specs.append(pl.BlockSpec((tm, tn), lambda j, i: (i, j)))
        args.append(residual)
    aliases = {}
    if into is not None:
        buf, layer = into
        in_specs.append(pl.BlockSpec(memory_space=pl.ANY))
        args.append(buf)
        aliases = {len(args) - 1: 0}
        out_spec = pl.BlockSpec((None, tm, tn), lambda j, i: (layer, i, j))
        out_shape = jax.ShapeDtypeStruct(buf.shape, buf.dtype)
    else:
        out_spec = pl.BlockSpec((tm, tn), lambda j, i: (i, j))
        out_shape = jax.ShapeDtypeStruct((m, n), out_dtype)

    def body(a_ref, b_ref, *rest):
        acc = _dot(a_ref[...], b_ref[...], kind)
        if scale is not None:
            acc = acc * scale
        if residual is not None:
            acc = acc + rest[0][...]
        rest[-1][...] = acc.astype(rest[-1].dtype)

    return pl.pallas_call(
        body, name=name, grid=(n // tn, m // tm), in_specs=in_specs, out_specs=out_spec, out_shape=out_shape,
        input_output_aliases=aliases, compiler_params=_cparams("parallel", "parallel"),
    )(*args)


def _ffn_up(n, w_gu, layer, name):
    s, d = n.shape
    f = w_gu.shape[2] // 2
    tm, tn = 512, f // 2
    nj = f // tn

    def body(n_ref, wg_ref, wu_ref, gu_ref, a_ref):
        nv = n_ref[...]
        g = _dot(nv, wg_ref[...], "nn")
        u = _dot(nv, wu_ref[...], "nn")
        gu_ref[0] = g.astype(BF16)
        gu_ref[1] = u.astype(BF16)
        a_ref[...] = (g * _sigmoid(g) * u).astype(BF16)

    return pl.pallas_call(
        body, name=name, grid=(nj, s // tm),
        in_specs=[pl.BlockSpec((tm, d), lambda j, i: (i, 0)),
                  pl.BlockSpec((None, d, tn), lambda j, i: (layer, 0, j)),
                  pl.BlockSpec((None, d, tn), lambda j, i: (layer, 0, j + nj))],
        out_specs=[pl.BlockSpec((2, tm, tn), lambda j, i: (0, i, j)),
                   pl.BlockSpec((tm, tn), lambda j, i: (i, j))],
        out_shape=[jax.ShapeDtypeStruct((2, s, f), BF16), jax.ShapeDtypeStruct((s, f), BF16)],
        compiler_params=_cparams("parallel", "parallel"),
    )(n, w_gu, w_gu)


def _ffn_down_bwd(dxo, w_down, gu, layer, name, after=None):
    s, d = dxo.shape
    f = w_down.shape[1]
    tm, tn = 512, f // 2
    extra_specs, extra = ([ANY], [after]) if after is not None else ([], [])

    def body(dx_ref, w_ref, gu_ref, *rest):
        dgu_ref = rest[-1]
        da = 0.5 * _dot(dx_ref[...], w_ref[...], "nt")
        g = gu_ref[0].astype(F32)
        u = gu_ref[1].astype(F32)
        sg = _sigmoid(g)
        dgu_ref[0] = (da * u * _silu_grad(g, sg)).astype(BF16)
        dgu_ref[1] = (da * g * sg).astype(BF16)

    return pl.pallas_call(
        body, name=name, grid=(f // tn, s // tm),
        in_specs=[pl.BlockSpec((tm, d), lambda j, i: (i, 0)),
                  pl.BlockSpec((None, tn, d), lambda j, i: (layer, j, 0)),
                  pl.BlockSpec((2, tm, tn), lambda j, i: (0, i, j))] + extra_specs,
        out_specs=pl.BlockSpec((2, tm, tn), lambda j, i: (0, i, j)),
        out_shape=jax.ShapeDtypeStruct((2, s, f), BF16),
        compiler_params=_cparams("parallel", "parallel"),
    )(dxo, w_down, gu, *extra)


def _ffn_dn(dgu, w_gu, layer, name):
    _, s, f = dgu.shape
    d = w_gu.shape[1]
    tm, tn = 512, d

    def body(dgu_ref, wg_ref, wu_ref, o_ref):
        o_ref[...] = _dot(dgu_ref[0], wg_ref[...], "nt") + _dot(dgu_ref[1], wu_ref[...], "nt")

    return pl.pallas_call(
        body, name=name, grid=(s // tm, d // tn),
        in_specs=[pl.BlockSpec((2, tm, f), lambda i, j: (0, i, 0)),
                  pl.BlockSpec((None, tn, f), lambda i, j: (layer, j, 0)),
                  pl.BlockSpec((None, tn, f), lambda i, j: (layer, j, 1))],
        out_specs=pl.BlockSpec((tm, tn), lambda i, j: (i, j)),
        out_shape=jax.ShapeDtypeStruct((s, d), F32),
        compiler_params=_cparams("parallel", "parallel"),
    )(dgu, w_gu, w_gu)


def _ffn_fwd(x, gain, w_gu, w_down, layer, tag):
    n = _rms_fwd(x, gain, f"{tag}_norm")
    gu, a = _ffn_up(n, w_gu, layer, f"{tag}_up")
    x2 = _mm(a, w_down, "nn", tm=512, tn=x.shape[1], out_dtype=F32, name=f"{tag}_down", scale=0.5, residual=x,
             b_lead=(layer,))
    return x2, (x, n, gu, a)


def _ffn_bwd(dxo, dxo16, saved, gain, w_gu, w_down, layer, tag, g_gu, g_down, after=None):
    x, n, gu, a = saved
    s, f = a.shape
    dgu = _ffn_down_bwd(dxo16, w_down, gu, layer, f"{tag}_down_bwd", after)
    g_down = _mm(a, dxo16, "tn", tm=256, tn=dxo16.shape[1], out_dtype=F32, name=f"{tag}_down_dw", scale=0.5,
                 into=(g_down, 0))
    dn = _ffn_dn(dgu, w_gu, layer, f"{tag}_up_bwd")
    tn = f // 2
    nj = f // tn
    g_gu = _mm(n, dgu, "tn", tm=512, tn=tn, out_dtype=F32, name=f"{tag}_up_dw", into=(g_gu, 0), n=2 * f,
               b_spec=pl.BlockSpec((None, s, tn), lambda j, i: (j // nj, 0, j % nj)))
    dx, dx16, dgain = _rms_bwd(x, gain, dn, dxo, f"{tag}_norm_bwd")
    return dx, dx16, dgain, g_gu, g_down


def _lane_col(blk, lane):
    li = lax.broadcasted_iota(jnp.int32, blk.shape, 1)
    return jnp.sum(jnp.where(li == lane, blk, 0.0), axis=1, keepdims=True)


def _split_dot(x, tri):
    hi = x.astype(BF16)
    lo = (x - hi.astype(F32)).astype(BF16)
    return (lax.dot_general(hi, tri, _DIMS["nn"], preferred_element_type=F32)
            + lax.dot_general(lo, tri, _DIMS["nn"], preferred_element_type=F32))


class _Each:
    def __init__(self, vals):
        self.vals = list(vals)

    def _with(self, other, op):
        others = other.vals if isinstance(other, _Each) else [other] * len(self.vals)
        return _Each(op(a, b) for a, b in zip(self.vals, others))

    def __add__(self, other):
        return self._with(other, lambda a, b: a + b)

    def __sub__(self, other):
        return self._with(other, lambda a, b: a - b)

    def __mul__(self, other):
        return self._with(other, lambda a, b: a * b)

    def __neg__(self):
        return _Each(-a for a in self.vals)


def _each(fn, *args):
    n = max(len(a.vals) for a in args if isinstance(a, _Each))
    res = [fn(*xs) for xs in zip(*[a.vals if isinstance(a, _Each) else [a] * n for a in args])]
    if isinstance(res[0], tuple):
        return tuple(_Each(r) for r in zip(*res))
    return _Each(res)


def _keep(cond, x):
    return _each(lambda v: jnp.where(cond, v, 0.0), x)


def _rowsum(x):
    return _each(lambda v: jnp.sum(v, axis=1, keepdims=True), x)


ATT_HEADS = 2
ATT_WIDTH = ATT_HEADS * HEAD_DIM
_HEAD_COLS = [slice(h * HEAD_DIM, (h + 1) * HEAD_DIM) for h in range(ATT_HEADS)]


def _att_specs(n_heads, s):
    groups = n_heads // ATT_HEADS
    q_spec = pl.BlockSpec((ATT_TQ, ATT_WIDTH), lambda g, i: (i, g))
    k_spec = pl.BlockSpec((s, ATT_WIDTH), lambda g, i: (0, groups + g))
    v_spec = pl.BlockSpec((s, ATT_WIDTH), lambda g, i: (0, 2 * groups + g))
    return q_spec, k_spec, v_spec


def _heads_of(ref, rows=None):
    return _Each(ref[:, cs] if rows is None else ref[rows, cs] for cs in _HEAD_COLS)


def _dot_each(a, b, kind):
    return _each(lambda x, y: _dot(x, y, kind), a, b)


def _att_iotas():
    row = lax.broadcasted_iota(jnp.int32, (ATT_TQ, ATT_TK), 0)
    col = lax.broadcasted_iota(jnp.int32, (ATT_TQ, ATT_TK), 1)
    jr = lax.broadcasted_iota(jnp.int32, (ATT_TK, ATT_TK), 0)
    jc = lax.broadcasted_iota(jnp.int32, (ATT_TK, ATT_TK), 1)
    return row, col, jr, jc


def _sb_fwd(qkv, n_heads, name):
    s = qkv.shape[0]

    def body(q_ref, k_ref, v_ref, o16_ref, o32_ref):
        i = pl.program_id(1)
        q = _heads_of(q_ref)
        row, col, jr, jc = _att_iotas()
        later = (jr > jc).astype(BF16)

        def step(jb, carry, diagonal):
            c_sp, acc = (_Each(part) for part in carry)
            work = []
            for sub in reversed(range(ATT_SUB)):
                keys = pl.ds(pl.multiple_of(jb * ATT_TQ + sub * ATT_TK, ATT_TK), ATT_TK)
                z = _dot_each(q, _heads_of(k_ref, keys), "nt") * ATT_SCALE
                sp = _each(_softplus, z)
                before = (col + sub * ATT_TK) < row if diagonal else None
                spm = _keep(before, sp) if diagonal else sp
                work.append((keys, z - sp, spm, _each(lambda x: _dot(x, later, "nn"), spm), before))
            for keys, logsig, spm, within, before in work:
                a = _each(jnp.exp, logsig - (c_sp + within))
                if diagonal:
                    a = _keep(before, a)
                acc = acc + _each(_split_dot, a, _heads_of(v_ref, keys))
                c_sp = c_sp + _rowsum(spm)
            return tuple(c_sp.vals), tuple(acc.vals)

        zeros = lambda width: tuple(jnp.zeros((ATT_TQ, width), F32) for _ in range(ATT_HEADS))
        carry = step(i, (zeros(1), zeros(HEAD_DIM)), True)
        _, acc = lax.fori_loop(0, i, lambda it, cr: step(i - 1 - it, cr, False), carry)
        for cs, acc_h in zip(_HEAD_COLS, acc):
            o16_ref[:, cs] = acc_h.astype(BF16)
            o32_ref[:, cs] = acc_h

    q_spec, k_spec, v_spec = _att_specs(n_heads, s)
    o_spec = pl.BlockSpec((ATT_TQ, ATT_WIDTH), lambda g, i: (i, g))
    return pl.pallas_call(
        body, name=name, grid=(n_heads // ATT_HEADS, s // ATT_TQ), in_specs=[q_spec, k_spec, v_spec],
        out_specs=[o_spec, o_spec],
        out_shape=[jax.ShapeDtypeStruct((s, n_heads * HEAD_DIM), BF16),
                   jax.ShapeDtypeStruct((s, n_heads * HEAD_DIM), F32)],
        compiler_params=_cparams("parallel", "arbitrary"),
    )(qkv, qkv, qkv)


def _sb_bwd(qkv, o32, do, n_heads, name):
    s = qkv.shape[0]

    def body(q_ref, k_ref, v_ref, o_ref, do_ref, dq_ref, dk_ref, dv_ref):
        i = pl.program_id(1)

        @pl.when(i == 0)
        def _():
            dk_ref[...] = jnp.zeros_like(dk_ref)
            dv_ref[...] = jnp.zeros_like(dv_ref)

        q, do = _heads_of(q_ref), _heads_of(do_ref)
        total = _rowsum(_each(lambda a, b: a.astype(F32) * b, do, _heads_of(o_ref)))
        row, col, jr, jc = _att_iotas()
        later = (jr > jc).astype(BF16)
        not_before = (jr >= jc).astype(BF16)

        def step(jb, carry, diagonal):
            c_sp, c_e, dq = (_Each(part) for part in carry)
            work = []
            for sub in reversed(range(ATT_SUB)):
                keys = pl.ds(pl.multiple_of(jb * ATT_TQ + sub * ATT_TK, ATT_TK), ATT_TK)
                k = _heads_of(k_ref, keys)
                z = _dot_each(q, k, "nt") * ATT_SCALE
                sp = _each(_softplus, z)
                before = (col + sub * ATT_TK) < row if diagonal else None
                spm = _keep(before, sp) if diagonal else sp
                work.append((keys, k, _each(jnp.exp, z - sp), spm, _each(lambda x: _dot(x, later, "nn"), spm),
                             _dot_each(do, _heads_of(v_ref, keys), "nt"), before))
            for keys, k, sig, spm, within, da, before in work:
                a = sig * _each(lambda x: jnp.exp(-x), c_sp + within)
                if diagonal:
                    a = _keep(before, a)
                e = a * da
                left = total - c_e - _each(lambda x: _split_dot(x, not_before), e)
                dz = (e - (e + left) * sig) * ATT_SCALE
                if diagonal:
                    dz = _keep(before, dz)
                dk, dv = _dot_each(dz, q, "tn"), _dot_each(a, do, "tn")
                for cs, dk_h, dv_h in zip(_HEAD_COLS, dk.vals, dv.vals):
                    dk_ref[keys, cs] += dk_h
                    dv_ref[keys, cs] += dv_h
                dq = dq + _dot_each(dz, k, "nn")
                c_sp = c_sp + _rowsum(spm)
                c_e = c_e + _rowsum(e)
            return tuple(c_sp.vals), tuple(c_e.vals), tuple(dq.vals)

        zeros = lambda width: tuple(jnp.zeros((ATT_TQ, width), F32) for _ in range(ATT_HEADS))
        carry = step(i, (zeros(1), zeros(1), zeros(HEAD_DIM)), True)
        _, _, dq = lax.fori_loop(0, i, lambda it, cr: step(i - 1 - it, cr, False), carry)
        for cs, dq_h in zip(_HEAD_COLS, dq):
            dq_ref[:, cs] = dq_h.astype(BF16)

    q_spec, k_spec, v_spec = _att_specs(n_heads, s)
    blk = pl.BlockSpec((ATT_TQ, ATT_WIDTH), lambda g, i: (i, g))
    full = pl.BlockSpec((s, ATT_WIDTH), lambda g, i: (0, g))
    wide = (s, n_heads * HEAD_DIM)
    return pl.pallas_call(
        body, name=name, grid=(n_heads // ATT_HEADS, s // ATT_TQ), in_specs=[q_spec, k_spec, v_spec, blk, blk],
        out_specs=[blk, full, full],
        out_shape=[jax.ShapeDtypeStruct(wide, BF16), jax.ShapeDtypeStruct(wide, F32), jax.ShapeDtypeStruct(wide, F32)],
        compiler_params=_cparams("parallel", "arbitrary"),
    )(qkv, qkv, qkv, o32, do)


def _fox_logits(q, k, cq, ct_ref, keys):
    ck = _Each(ct_ref[h, :, keys] for h in range(ATT_HEADS))
    return _dot_each(q, k, "nt") * ATT_SCALE + (cq - ck)


def _fox_cq(c_ref, group):
    c = c_ref[...]
    return _Each(_lane_col(c, LANE_FORGET + group * ATT_HEADS + h) for h in range(ATT_HEADS))


def _fox_fwd(qkv, c, ct, name):
    s = qkv.shape[0]
    n_heads = N_FOX_HEADS

    def body(q_ref, k_ref, v_ref, c_ref, ct_ref, o_ref, lse_ref):
        g, i = pl.program_id(0), pl.program_id(1)
        q = _heads_of(q_ref)
        cq = _fox_cq(c_ref, g)
        row, col, _, _ = _att_iotas()

        def step(jb, carry, diagonal):
            m, l, acc = (_Each(part) for part in carry)
            work = []
            m_new = m
            for sub in range(ATT_SUB):
                keys = pl.ds(pl.multiple_of(jb * ATT_TQ + sub * ATT_TK, ATT_TK), ATT_TK)
                sc = _fox_logits(q, _heads_of(k_ref, keys), cq, ct_ref, keys)
                valid = (col + sub * ATT_TK) <= row if diagonal else None
                if diagonal:
                    sc = _each(lambda x: jnp.where(valid, x, -1e30), sc)
                m_new = _each(lambda a, x: jnp.maximum(a, jnp.max(x, axis=1, keepdims=True)), m_new, sc)
                work.append((keys, sc, valid))
            w = _each(jnp.exp, m - m_new)
            l, acc = l * w, acc * w
            for keys, sc, valid in work:
                p = _each(jnp.exp, sc - m_new)
                if diagonal:
                    p = _keep(valid, p)
                l = l + _rowsum(p)
                acc = acc + _each(_split_dot, p, _heads_of(v_ref, keys))
            return tuple(m_new.vals), tuple(l.vals), tuple(acc.vals)

        per_head = lambda width, value: tuple(jnp.full((ATT_TQ, width), value, F32) for _ in range(ATT_HEADS))
        init = (per_head(1, -1e30), per_head(1, 0.0), per_head(HEAD_DIM, 0.0))
        m, l, acc = lax.fori_loop(0, i, lambda jb, cr: step(jb, cr, False), step(i, init, True))
        for h, cs in enumerate(_HEAD_COLS):
            o_ref[:, cs] = acc[h] / l[h]
            lse_ref[h] = jnp.broadcast_to(m[h] + jnp.log(l[h]), (ATT_TQ, LANES))

    q_spec, k_spec, v_spec = _att_specs(n_heads, s)
    return pl.pallas_call(
        body, name=name, grid=(n_heads // ATT_HEADS, s // ATT_TQ),
        in_specs=[q_spec, k_spec, v_spec, pl.BlockSpec((ATT_TQ, LANES), lambda g, i: (i, 0)),
                  pl.BlockSpec((ATT_HEADS, 1, s), lambda g, i: (g, 0, 0))],
        out_specs=[pl.BlockSpec((ATT_TQ, ATT_WIDTH), lambda g, i: (i, g)),
                   pl.BlockSpec((ATT_HEADS, ATT_TQ, LANES), lambda g, i: (g, i, 0))],
        out_shape=[jax.ShapeDtypeStruct((s, n_heads * HEAD_DIM), F32),
                   jax.ShapeDtypeStruct((n_heads, s, LANES), F32)],
        compiler_params=_cparams("parallel", "arbitrary"),
    )(qkv, qkv, qkv, c, ct)


def _fox_bwd(qkv, c, ct, o, lse, do, name):
    s = qkv.shape[0]
    n_heads = N_FOX_HEADS

    def body(q_ref, k_ref, v_ref, c_ref, ct_ref, o_ref, lse_ref, do_ref, dq_ref, dk_ref, dv_ref, dct_ref):
        g, i = pl.program_id(0), pl.program_id(1)

        @pl.when(i == 0)
        def _():
            dk_ref[...] = jnp.zeros_like(dk_ref)
            dv_ref[...] = jnp.zeros_like(dv_ref)
            dct_ref[...] = jnp.zeros_like(dct_ref)

        q = _heads_of(q_ref)
        do16 = _each(lambda x: x.astype(BF16), _heads_of(do_ref))
        delta = _rowsum(_each(lambda a, b: a.astype(F32) * b, do16, _heads_of(o_ref)))
        lse_col = _Each(lse_ref[h, :, 0:1] for h in range(ATT_HEADS))
        cq = _fox_cq(c_ref, g)
        row, col, _, _ = _att_iotas()

        def step(jb, dq, diagonal):
            dq = _Each(dq)
            for sub in range(ATT_SUB):
                keys = pl.ds(pl.multiple_of(jb * ATT_TQ + sub * ATT_TK, ATT_TK), ATT_TK)
                k = _heads_of(k_ref, keys)
                sc = _fox_logits(q, k, cq, ct_ref, keys)
                if diagonal:
                    valid = (col + sub * ATT_TK) <= row
                    p = _keep(valid, _each(jnp.exp, _keep(valid, sc) - lse_col))
                else:
                    p = _each(jnp.exp, sc - lse_col)
                ds = p * (_dot_each(do16, _heads_of(v_ref, keys), "nt") - delta)
                dss = ds * ATT_SCALE
                dk, dv = _dot_each(dss, q, "tn"), _dot_each(p, do16, "tn")
                for h, cs in enumerate(_HEAD_COLS):
                    dct_ref[h, :, keys] -= jnp.sum(ds.vals[h], axis=0, keepdims=True)
                    dk_ref[keys, cs] += dk.vals[h]
                    dv_ref[keys, cs] += dv.vals[h]
                dq = dq + _dot_each(dss, k, "nn")
            return tuple(dq.vals)

        dq0 = step(i, tuple(jnp.zeros((ATT_TQ, HEAD_DIM), F32) for _ in range(ATT_HEADS)), True)
        dq = lax.fori_loop(0, i, lambda jb, dq: step(jb, dq, False), dq0)
        for cs, dq_h in zip(_HEAD_COLS, dq):
            dq_ref[:, cs] = dq_h

    q_spec, k_spec, v_spec = _att_specs(n_heads, s)
    blk = pl.BlockSpec((ATT_TQ, ATT_WIDTH), lambda g, i: (i, g))
    full = pl.BlockSpec((s, ATT_WIDTH), lambda g, i: (0, g))
    wide = jax.ShapeDtypeStruct((s, n_heads * HEAD_DIM), F32)
    return pl.pallas_call(
        body, name=name, grid=(n_heads // ATT_HEADS, s // ATT_TQ),
        in_specs=[q_spec, k_spec, v_spec, pl.BlockSpec((ATT_TQ, LANES), lambda g, i: (i, 0)),
                  pl.BlockSpec((ATT_HEADS, 1, s), lambda g, i: (g, 0, 0)), blk,
                  pl.BlockSpec((ATT_HEADS, ATT_TQ, LANES), lambda g, i: (g, i, 0)), blk],
        out_specs=[blk, full, full, pl.BlockSpec((ATT_HEADS, 1, s), lambda g, i: (g, 0, 0))],
        out_shape=[wide, wide, wide, jax.ShapeDtypeStruct((n_heads, 1, s), F32)],
        compiler_params=_cparams("parallel", "arbitrary"),
    )(qkv, qkv, qkv, c, ct, o, lse, do)


def _cumsum_rows(x, reverse, name):
    s = x.shape[0]
    nb = s // LANES

    def body(x_ref, o_ref):
        r = lax.broadcasted_iota(jnp.int32, (LANES, LANES), 0)
        c = lax.broadcasted_iota(jnp.int32, (LANES, LANES), 1)
        tri = ((r <= c) if reverse else (r >= c)).astype(F32)

        def step(it, carry):
            b = (nb - 1 - it) if reverse else it
            off = pl.multiple_of(b * LANES, LANES)
            blk = x_ref[pl.ds(off, LANES), :]
            o_ref[pl.ds(off, LANES), :] = _dot32(tri, blk) + carry
            return carry + jnp.sum(blk, axis=0, keepdims=True)

        lax.fori_loop(0, nb, step, jnp.zeros((1, LANES), F32))

    return pl.pallas_call(body, name=name, out_shape=jax.ShapeDtypeStruct(x.shape, F32),
                          compiler_params=pltpu.CompilerParams(vmem_limit_bytes=V7X_VMEM_LIMIT))(x)


def _dot32_each(a, b, kind="nn"):
    return _each(lambda x, y: _dot32(x, y, kind), a, b)


def _unit_lower_inverse(m, ri, ci):
    c = ri.shape[0]
    t = -_keep(ri // 2 == ci // 2, m) + jnp.where(ri == ci, 1.0, 0.0)
    b = 4
    while b <= c:
        off_diag = (ri // b == ci // b) & (ri % b >= b // 2) & (ci % b < b // 2)
        t = t - _dot32_each(_dot32_each(t, _keep(off_diag, m)), t)
        b *= 2
    return t


def _dn_gates(g, ri, ci):
    eye = ri == ci
    incl = ri >= ci
    g_row = jnp.sum(jnp.where(eye, g, 0.0), axis=0, keepdims=True)
    gc = jnp.sum(jnp.where(incl, g_row, 0.0), axis=1, keepdims=True)
    gc_row = jnp.sum(jnp.where(eye, gc, 0.0), axis=0, keepdims=True)
    dmat = jnp.where(incl, jnp.exp(jnp.where(incl, gc - gc_row, 0.0)), 0.0)
    gc_last = jnp.sum(g, axis=0, keepdims=True)
    return gc, dmat, jnp.exp(gc), jnp.exp(gc_last - gc), jnp.exp(gc_last)


def _dn_fwd(qkv, act, name):
    s = qkv.shape[0]
    c, d, nh = DN_CHUNK, HEAD_DIM, N_DN_HEADS
    nc = s // c

    def body(q_ref, k_ref, v_ref, act_ref, o_ref, s_ref, t_ref, state):
        @pl.when(pl.program_id(0) == 0)
        def _():
            state[...] = jnp.zeros_like(state)

        ri = lax.broadcasted_iota(jnp.int32, (c, c), 0)
        ci = lax.broadcasted_iota(jnp.int32, (c, c), 1)
        act = act_ref[...]
        heads = range(nh)
        cols = [slice(h * d, (h + 1) * d) for h in heads]
        q, k, v = (_Each(ref[:, cs] for cs in cols) for ref in (q_ref, k_ref, v_ref))
        beta = _Each(_lane_col(act, LANE_BETA + h) for h in heads)
        g = _Each(_lane_col(act, LANE_DECAY + h) for h in heads)
        _, dmat, e, r, gl = _each(lambda gh: _dn_gates(gh, ri, ci), g)
        s0 = _Each(state[h] for h in heads)
        kb = beta * k
        t = _unit_lower_inverse(_keep(ri > ci, _dot32_each(kb, k, "nt") * dmat), ri, ci)
        vn = _dot32_each(t, beta * v) - _dot32_each(_dot32_each(t, kb * e), s0)
        o = _dot32_each(q * e, s0) + _dot32_each(_dot32_each(q, k, "nt") * dmat, vn)
        s1 = s0 * gl + _dot32_each(k * r, vn, "tn")
        for h in heads:
            o_ref[:, cols[h]] = o.vals[h]
            state[h] = s1.vals[h]
            s_ref[h] = s0.vals[h]
            t_ref[h] = t.vals[h]

    wide = lambda part: pl.BlockSpec((c, nh * d), lambda n: (n, part))
    return pl.pallas_call(
        body, name=name, grid=(nc,),
        in_specs=[wide(0), wide(1), wide(2), pl.BlockSpec((c, LANES), lambda n: (n, 0))],
        out_specs=[wide(0), pl.BlockSpec((nh, None, d, d), lambda n: (0, n, 0, 0)),
                   pl.BlockSpec((nh, None, c, c), lambda n: (0, n, 0, 0))],
        out_shape=[jax.ShapeDtypeStruct((s, nh * d), F32), jax.ShapeDtypeStruct((nh, nc, d, d), F32),
                   jax.ShapeDtypeStruct((nh, nc, c, c), F32)],
        scratch_shapes=[pltpu.VMEM((nh, d, d), F32)],
        compiler_params=_cparams("arbitrary"),
    )(qkv, qkv, qkv, act)


def _dn_bwd(qkv, act, states, tinv, do, name):
    s = qkv.shape[0]
    c, d, nh = DN_CHUNK, HEAD_DIM, N_DN_HEADS
    nc = s // c

    def chunk_bwd(q, k, v, do, beta, g, s0, t, ds_out):
        ri = lax.broadcasted_iota(jnp.int32, (c, c), 0)
        ci = lax.broadcasted_iota(jnp.int32, (c, c), 1)
        eye, incl, strict = ri == ci, ri >= ci, ri > ci
        gc, dmat, e, r, gl = _each(lambda gh: _dn_gates(gh, ri, ci), g)
        dot = _dot32_each
        rowsum = lambda x: _each(lambda a: jnp.sum(a, axis=1, keepdims=True), x)
        colsum = lambda x: _each(lambda a: jnp.sum(a, axis=0, keepdims=True), x)
        total = lambda x: colsum(rowsum(x))
        to_col = lambda row: rowsum(_keep(eye, row))
        to_row = lambda colv: colsum(_keep(eye, colv))

        kb, vb = beta * k, beta * v
        kbe = kb * e
        u, w = dot(t, vb), dot(t, kbe)
        vn = u - dot(w, s0)
        qk = dot(q, k, "nt")
        p = qk * dmat
        gram = dot(k, k, "nt")
        kr, qe = k * r, q * e

        d_kr = dot(vn, ds_out, "nt")
        dvn = dot(kr, ds_out)
        dgl = total(s0 * ds_out)
        ds_in = ds_out * gl
        dk = d_kr * r
        dr = rowsum(d_kr * k)
        d_qe = dot(do, s0, "nt")
        ds_in = ds_in + dot(qe, do, "tn")
        dp = _keep(incl, dot(do, vn, "nt"))
        dvn = dvn + dot(p, do, "tn")
        dq = d_qe * e
        de = rowsum(d_qe * q)
        dqk = dp * dmat
        dq = dq + dot(dqk, k)
        dk = dk + dot(dqk, q, "tn")
        dd = dp * qk
        dw = -dot(dvn, s0, "nt")
        ds_in = ds_in - dot(w, dvn, "tn")
        dvb = dot(t, dvn, "tn")
        dkbe = dot(t, dw, "tn")
        dm = -_keep(strict, dot(dvb, u, "nt") + dot(dkbe, w, "nt"))
        dbeta = rowsum(dm * gram * dmat)
        dgram = dm * beta * dmat
        dd = dd + dm * beta * gram
        dk = dk + dot(dgram, k) + dot(dgram, k, "tn")
        dkb = dkbe * e
        de = de + rowsum(dkbe * kb)
        dk = dk + beta * dkb
        dbeta = dbeta + rowsum(dkb * k) + rowsum(dvb * v)
        dv = beta * dvb
        wd = dd * dmat
        dgc = rowsum(wd) - to_col(colsum(wd)) + de * e - dr * r
        dgc_last = total(dr * r) + dgl * gl
        dgc = dgc + _keep(ri[:, 0:1] == c - 1, dgc_last)
        dg = rowsum(_keep(ri <= ci, to_row(dgc)))
        return dq, dk, dv, dbeta, dg, ds_in

    def body(q_ref, k_ref, v_ref, act_ref, s_ref, t_ref, do_ref, dq_ref, dk_ref, dv_ref, dact_ref, dstate):
        @pl.when(pl.program_id(0) == 0)
        def _():
            dstate[...] = jnp.zeros_like(dstate)

        act = act_ref[...]
        heads = range(nh)
        cols = [slice(h * d, (h + 1) * d) for h in heads]
        q, k, v, do = (_Each(ref[:, cs] for cs in cols) for ref in (q_ref, k_ref, v_ref, do_ref))
        dq, dk, dv, dbeta, dg, ds_in = chunk_bwd(
            q, k, v, do, _Each(_lane_col(act, LANE_BETA + h) for h in heads),
            _Each(_lane_col(act, LANE_DECAY + h) for h in heads), _Each(s_ref[h] for h in heads),
            _Each(t_ref[h] for h in heads), _Each(dstate[h] for h in heads))
        lane = lax.broadcasted_iota(jnp.int32, (c, LANES), 1)
        dact = jnp.zeros((c, LANES), F32)
        for h in heads:
            dstate[h] = ds_in.vals[h]
            dq_ref[:, cols[h]], dk_ref[:, cols[h]], dv_ref[:, cols[h]] = dq.vals[h], dk.vals[h], dv.vals[h]
            dact = (dact + jnp.where(lane == LANE_BETA + h, dbeta.vals[h], 0.0)
                    + jnp.where(lane == LANE_DECAY + h, dg.vals[h], 0.0))
        dact_ref[...] = dact

    part = lambda p: pl.BlockSpec((c, nh * d), lambda n: (nc - 1 - n, p))
    per = lambda a, b: pl.BlockSpec((nh, None, a, b), lambda n: (0, nc - 1 - n, 0, 0))
    wide = jax.ShapeDtypeStruct((s, nh * d), F32)
    act_spec = pl.BlockSpec((c, LANES), lambda n: (nc - 1 - n, 0))
    return pl.pallas_call(
        body, name=name, grid=(nc,),
        in_specs=[part(0), part(1), part(2), act_spec, per(d, d), per(c, c), part(0)],
        out_specs=[part(0), part(0), part(0), act_spec],
        out_shape=[wide, wide, wide, jax.ShapeDtypeStruct((s, LANES), F32)],
        scratch_shapes=[pltpu.VMEM((nh, d, d), F32)],
        compiler_params=_cparams("arbitrary"),
    )(qkv, qkv, qkv, act, states, tinv, do)


EVEN_DN_QKV, EVEN_FOX_QKV, EVEN_DN_GATE, EVEN_FOX_GATE, EVEN_NARROW = 0, 1536, 3072, 3584, 4096
EVEN_WIDTH = 4224
CONV_TILE = 256
CONV_HALO = 8


def _conv_fwd(proj, w, name):
    s = proj.shape[0]
    t, cw = CONV_TILE, 3 * D_DN

    def body(cur_ref, prev_ref, w_ref, y_ref, xs):
        i = pl.program_id(0)
        xs[0:CONV_HALO, :] = jnp.where(i > 0, prev_ref[...], 0.0)
        xs[CONV_HALO:, :] = cur_ref[...]
        y = jnp.zeros((t, cw), F32)
        for tap in range(CONV_WIDTH):
            y = y + w_ref[tap:tap + 1, :] * xs[pl.ds(CONV_HALO - CONV_WIDTH + 1 + tap, t), :]
        y_ref[...] = y

    per = t // CONV_HALO
    return pl.pallas_call(
        body, name=name, grid=(s // t,),
        in_specs=[pl.BlockSpec((t, cw), lambda i: (i, 0)),
                  pl.BlockSpec((CONV_HALO, cw), lambda i: (jnp.maximum(i * per - 1, 0), 0)),
                  pl.BlockSpec((CONV_WIDTH, cw), lambda i: (0, 0))],
        out_specs=pl.BlockSpec((t, cw), lambda i: (i, 0)),
        out_shape=jax.ShapeDtypeStruct((s, cw), F32),
        scratch_shapes=[pltpu.VMEM((t + CONV_HALO, cw), F32)],
        compiler_params=_cparams("parallel"),
    )(proj, proj, w)


def _conv_bwd(proj, w, dy, name):
    s = proj.shape[0]
    t, cw = CONV_TILE, 3 * D_DN
    nt = s // t

    def body(cur_ref, prev_ref, w_ref, dy_ref, nxt_ref, dx_ref, dw_ref, xs, dys):
        i = pl.program_id(0)

        @pl.when(i == 0)
        def _():
            dw_ref[...] = jnp.zeros_like(dw_ref)

        xs[0:CONV_HALO, :] = jnp.where(i > 0, prev_ref[...], 0.0)
        xs[CONV_HALO:, :] = cur_ref[...]
        dys[0:t, :] = dy_ref[...]
        dys[t:, :] = jnp.where(i < nt - 1, nxt_ref[...], 0.0)
        dy = dy_ref[...]
        dx = jnp.zeros((t, cw), F32)
        for tap in range(CONV_WIDTH):
            dx = dx + w_ref[tap:tap + 1, :] * dys[pl.ds(CONV_WIDTH - 1 - tap, t), :]
            dw_ref[tap:tap + 1, :] += jnp.sum(dy * xs[pl.ds(CONV_HALO - CONV_WIDTH + 1 + tap, t), :], axis=0,
                                              keepdims=True)
        dx_ref[...] = dx.astype(BF16)

    per = t // CONV_HALO
    last = s // CONV_HALO - 1
    return pl.pallas_call(
        body, name=name, grid=(nt,),
        in_specs=[pl.BlockSpec((t, cw), lambda i: (i, 0)),
                  pl.BlockSpec((CONV_HALO, cw), lambda i: (jnp.maximum(i * per - 1, 0), 0)),
                  pl.BlockSpec((CONV_WIDTH, cw), lambda i: (0, 0)),
                  pl.BlockSpec((t, cw), lambda i: (i, 0)),
                  pl.BlockSpec((CONV_HALO, cw), lambda i: (jnp.minimum((i + 1) * per, last), 0))],
        out_specs=[pl.BlockSpec((t, cw), lambda i: (i, 0)), pl.BlockSpec((CONV_WIDTH, cw), lambda i: (0, 0))],
        out_shape=[jax.ShapeDtypeStruct((s, cw), BF16), jax.ShapeDtypeStruct((CONV_WIDTH, cw), F32)],
        scratch_shapes=[pltpu.VMEM((t + CONV_HALO, cw), F32), pltpu.VMEM((t + CONV_HALO, cw), F32)],
        compiler_params=_cparams("arbitrary"),
    )(proj, proj, w, dy, dy)


def _heads(x, n):
    return [x[:, HEAD_DIM * h:HEAD_DIM * (h + 1)] for h in range(n)]


def _dn_pre_fwd(y, name):
    def fn(yb):
        cs = yb * _sigmoid(yb)
        out = []
        for idx, xh in enumerate(_heads(cs, 3 * N_DN_HEADS)):
            if idx < 2 * N_DN_HEADS:
                xh = xh * lax.rsqrt(jnp.sum(xh * xh, axis=-1, keepdims=True) + EPS)
                if idx < N_DN_HEADS:
                    xh = xh * ATT_SCALE
            out.append(xh)
        return (jnp.concatenate(out, axis=1),)
    return _rowwise(fn, [y], [], [(y.shape[1], F32)], [], tile=256, name=name)[0]


def _dn_pre_bwd(y, dq, dk, dv, name):
    def fn(yb, dqb, dkb, dvb):
        sg = _sigmoid(yb)
        cs = yb * sg
        dout = _heads(dqb, N_DN_HEADS) + _heads(dkb, N_DN_HEADS) + _heads(dvb, N_DN_HEADS)
        dcs = []
        for idx, (xh, dh) in enumerate(zip(_heads(cs, 3 * N_DN_HEADS), dout)):
            if idx < 2 * N_DN_HEADS:
                if idx < N_DN_HEADS:
                    dh = dh * ATT_SCALE
                r = lax.rsqrt(jnp.sum(xh * xh, axis=-1, keepdims=True) + EPS)
                xhat = xh * r
                dh = r * (dh - xhat * jnp.sum(xhat * dh, axis=-1, keepdims=True))
            dcs.append(dh)
        return (jnp.concatenate(dcs, axis=1) * _silu_grad(yb, sg),)
    return _rowwise(fn, [y, dq, dk, dv], [], [(y.shape[1], F32)], [], tile=256, name=name)[0]


def _narrow_params(a_log, dt_bias, f_bias):
    lanes = lambda a, first: jnp.pad(a.reshape(1, -1), ((0, 0), (first, LANES - first - a.shape[0])))
    return jnp.concatenate([lanes(a_log, LANE_DECAY), lanes(dt_bias, LANE_DECAY), lanes(f_bias, LANE_FORGET),
                            jnp.zeros((5, LANES), F32)], axis=0)


def _narrow_masks(shape):
    lane = lax.broadcasted_iota(jnp.int32, shape, 1)
    is_beta = lane < LANE_DECAY
    is_decay = (lane >= LANE_DECAY) & (lane < LANE_FORGET)
    is_forget = (lane >= LANE_FORGET) & (lane < LANE_FORGET + N_FOX_HEADS)
    return is_beta, is_decay, is_forget


def _narrow_fwd(proj, params, name):
    def fn(sm, pk):
        is_beta, is_decay, is_forget = _narrow_masks(sm.shape)
        g = -jnp.exp(pk[0:1, :]) * _softplus(sm + pk[1:2, :])
        logf = -_softplus(-(sm + pk[2:3, :]))
        return (jnp.where(is_beta, _sigmoid(sm), jnp.where(is_decay, g, jnp.where(is_forget, logf, 0.0))),)
    return _rowwise(fn, [(proj, LANES, EVEN_NARROW // LANES)], [params], [(LANES, F32)], [], tile=512, name=name)[0]


def _narrow_bwd(proj, params, act, dact, dlogf, name):
    def fn(sm, ab, da, dl, pk):
        is_beta, is_decay, is_forget = _narrow_masks(sm.shape)
        db = jnp.where(is_forget, dl, da)
        d_beta = db * ab * (1.0 - ab)
        d_decay = db * (-jnp.exp(pk[0:1, :])) * _sigmoid(sm + pk[1:2, :])
        d_forget = db * _sigmoid(-(sm + pk[2:3, :]))
        dsm = jnp.where(is_beta, d_beta, jnp.where(is_decay, d_decay, jnp.where(is_forget, d_forget, 0.0)))
        col = lambda x: jnp.sum(x, axis=0, keepdims=True)
        return (dsm, col(jnp.where(is_decay, db * ab, 0.0)), col(jnp.where(is_decay, dsm, 0.0)),
                col(jnp.where(is_forget, dsm, 0.0)))
    return _rowwise(fn, [(proj, LANES, EVEN_NARROW // LANES), act, dact, dlogf], [params], [(LANES, BF16)],
                    [(1, LANES)] * 3, tile=512, name=name)


def _head_rms(xh):
    r = lax.rsqrt(jnp.mean(xh * xh, axis=-1, keepdims=True) + EPS)
    return xh * r, r


def _fox_pre_fwd(proj, qg, kg, name):
    def fn(pf, qgb, kgb):
        out = []
        for idx, xh in enumerate(_heads(pf, 3 * N_FOX_HEADS)):
            if idx < 2 * N_FOX_HEADS:
                xh = _head_rms(xh)[0] * (qgb if idx < N_FOX_HEADS else kgb)
            out.append(xh)
        return (jnp.concatenate(out, axis=1),)
    return _rowwise(fn, [(proj, 3 * D_FOX, EVEN_FOX_QKV // (3 * D_FOX))], [qg, kg], [(3 * D_FOX, BF16)], [],
                    tile=256, name=name)[0]


def _fox_pre_bwd(proj, qg, kg, dq, dk, dv, name):
    def fn(pf, dqb, dkb, dvb, qgb, kgb):
        dout = _heads(dqb, N_FOX_HEADS) + _heads(dkb, N_FOX_HEADS) + _heads(dvb, N_FOX_HEADS)
        dg = [jnp.zeros((1, HEAD_DIM), F32), jnp.zeros((1, HEAD_DIM), F32)]
        dx = []
        for idx, (xh, dh) in enumerate(zip(_heads(pf, 3 * N_FOX_HEADS), dout)):
            if idx < 2 * N_FOX_HEADS:
                which = 0 if idx < N_FOX_HEADS else 1
                xhat, r = _head_rms(xh)
                dg[which] = dg[which] + jnp.sum(dh * xhat, axis=0, keepdims=True)
                dxh = dh * (qgb if which == 0 else kgb)
                dh = r * (dxh - xhat * jnp.mean(dxh * xhat, axis=-1, keepdims=True))
            dx.append(dh)
        return jnp.concatenate(dx, axis=1), dg[0], dg[1]
    return _rowwise(fn, [(proj, 3 * D_FOX, EVEN_FOX_QKV // (3 * D_FOX)), dq, dk, dv], [qg, kg],
                    [(3 * D_FOX, BF16)], [(1, HEAD_DIM)] * 2, tile=256, name=name)


def _mix_gate_fwd(proj, o_dn, o_fox, ng, name):
    def fn(gd, gf, od, of, ngb):
        dn = [_head_rms(xh)[0] * ngb for xh in _heads(od, N_DN_HEADS)]
        return (jnp.concatenate([jnp.concatenate(dn, axis=1) * gd * _sigmoid(gd), of * _sigmoid(gf)], axis=1),)
    return _rowwise(fn, [(proj, D_DN, EVEN_DN_GATE // D_DN), (proj, D_FOX, EVEN_FOX_GATE // D_FOX), o_dn, o_fox],
                    [ng], [(D_DN + D_FOX, BF16)], [], tile=256, name=name)[0]


def _mix_gate_bwd(proj, o_dn, o_fox, ng, dom, name):
    def fn(gd, gf, od, of, dm, ngb):
        d_dn, d_fox = dm[:, :D_DN], dm[:, D_DN:]
        sgd, sgf = _sigmoid(gd), _sigmoid(gf)
        don = d_dn * gd * sgd
        dng = jnp.zeros((1, HEAD_DIM), F32)
        dod, normed = [], []
        for xh, dh in zip(_heads(od, N_DN_HEADS), _heads(don, N_DN_HEADS)):
            xhat, r = _head_rms(xh)
            dng = dng + jnp.sum(dh * xhat, axis=0, keepdims=True)
            dxh = dh * ngb
            dod.append(r * (dxh - xhat * jnp.mean(dxh * xhat, axis=-1, keepdims=True)))
            normed.append(xhat * ngb)
        d_gd = d_dn * jnp.concatenate(normed, axis=1) * _silu_grad(gd, sgd)
        d_gf = d_fox * of * sgf * (1.0 - sgf)
        return jnp.concatenate(dod, axis=1), d_fox * sgf, d_gd, d_gf, dng
    return _rowwise(fn, [(proj, D_DN, EVEN_DN_GATE // D_DN), (proj, D_FOX, EVEN_FOX_GATE // D_FOX), o_dn, o_fox, dom],
                    [ng], [(D_DN, F32), (D_FOX, F32), (D_DN, BF16), (D_FOX, BF16)], [(1, HEAD_DIM)], tile=256,
                    name=name)


def _loss_grad(y, target, name):
    d = y.shape[1]

    def fn(yb, tb):
        diff = yb - tb
        part = jnp.sum(jnp.sum(diff * diff, axis=1, keepdims=True), axis=0, keepdims=True) * (0.5 / d)
        g = diff * (1.0 / d)
        return g, g, part
    return _rowwise(fn, [y, target], [], [(d, F32), (d, BF16)], [(1, 1)], tile=512, name=name)


_REF_EVEN = {"dn_qkv": (0, 1536), "dn_gate": (1536, 2048), "dn_ba": (2048, 2056), "fox_qkv": (2056, 3592),
             "fox_gate": (3592, 4104), "f_pre": (4104, 4108)}
D_IN_EVEN = 4108


def _even_to_kernel_layout(w):
    cut = lambda name: w[..., _REF_EVEN[name][0]:_REF_EVEN[name][1]]
    pad = jnp.zeros(w.shape[:-1] + (EVEN_WIDTH - EVEN_NARROW - 12,), w.dtype)
    return jnp.concatenate([cut("dn_qkv"), cut("fox_qkv"), cut("dn_gate"), cut("fox_gate"), cut("dn_ba"),
                            cut("f_pre"), pad], axis=-1)


def _even_from_kernel_layout(g):
    return jnp.concatenate([g[..., EVEN_DN_QKV:EVEN_FOX_QKV], g[..., EVEN_DN_GATE:EVEN_FOX_GATE],
                            g[..., EVEN_NARROW:EVEN_NARROW + 8], g[..., EVEN_FOX_QKV:EVEN_DN_GATE],
                            g[..., EVEN_FOX_GATE:EVEN_NARROW], g[..., EVEN_NARROW + 8:EVEN_NARROW + 12]], axis=-1)


EVEN_QUARTER = 1027
EVEN_QUARTER_PAD = 1152


def _even_grad_quarters(g):
    g = _even_from_kernel_layout(g)
    pad = [(0, 0)] * (g.ndim - 1) + [(0, EVEN_QUARTER_PAD - EVEN_QUARTER)]
    return jnp.concatenate([jnp.pad(g[..., q * EVEN_QUARTER:(q + 1) * EVEN_QUARTER], pad) for q in range(4)], axis=-1)


def _forget_rows(c):
    return c[:, LANE_FORGET:LANE_FORGET + N_FOX_HEADS].T.reshape(N_FOX_HEADS, 1, c.shape[0])


def _forget_lanes(rows):
    s = rows.shape[2]
    return jnp.pad(rows.reshape(-1, s).T, ((0, 0), (LANE_FORGET, LANES - LANE_FORGET - N_FOX_HEADS)))


def _even_fwd(x, gain, w_in, w_out, j, p, tag):
    h = _rms_fwd(x, gain, f"{tag}_norm")
    proj = _mm(h, w_in, "nn", tm=512, tn=EVEN_WIDTH // 3, out_dtype=F32, name=f"{tag}_in", b_lead=(j,))
    y = _conv_fwd(proj, p["conv_w"], f"{tag}_conv")
    dn_qkv = _dn_pre_fwd(y, f"{tag}_dn_pre")
    act = _narrow_fwd(proj, p["narrow"], f"{tag}_narrow")
    o_dn, states, tinv = _dn_fwd(dn_qkv, act, f"{tag}_delta")
    fox_qkv = _fox_pre_fwd(proj, p["q_g"], p["k_g"], f"{tag}_fox_pre")
    c = _cumsum_rows(act, False, f"{tag}_cumsum")
    ct = _forget_rows(c)
    o_fox, lse = _fox_fwd(fox_qkv, c, ct, f"{tag}_fox")
    om = _mix_gate_fwd(proj, o_dn, o_fox, p["dn_norm_g"], f"{tag}_gate")
    x2 = _mm(om, w_out, "nn", tm=512, tn=x.shape[1], out_dtype=F32, name=f"{tag}_out", residual=x, b_lead=(j,))
    return x2, (x, h, proj, y, dn_qkv, act, states, tinv, o_dn, fox_qkv, c, ct, o_fox, lse, om)


def _even_bwd(dxo, dxo16, saved, gain, w_in, w_out, j, p, tag, g_in, g_out):
    x, h, proj, y, dn_qkv, act, states, tinv, o_dn, fox_qkv, c, ct, o_fox, lse, om = saved
    d = x.shape[1]
    dom = _mm(dxo16, w_out, "nt", tm=512, tn=d, out_dtype=F32, name=f"{tag}_out_bwd", b_lead=(j,))
    g_out = _mm(om, dxo16, "tn", tm=512, tn=d, out_dtype=F32, name=f"{tag}_out_dw", into=(g_out, 0))
    d_odn, d_ofox, d_gd, d_gf, d_ng = _mix_gate_bwd(proj, o_dn, o_fox, p["dn_norm_g"], dom, f"{tag}_gate_bwd")
    dq, dk, dv, dct = _fox_bwd(fox_qkv, c, ct, o_fox, lse, d_ofox, f"{tag}_fox_bwd")
    d_fox_qkv, d_qg, d_kg = _fox_pre_bwd(proj, p["q_g"], p["k_g"], dq, dk, dv, f"{tag}_fox_pre_bwd")
    dlogf = _cumsum_rows(_forget_lanes(dct), True, f"{tag}_cumsum_bwd")
    dq, dk, dv, dact = _dn_bwd(dn_qkv, act, states, tinv, d_odn, f"{tag}_delta_bwd")
    dy = _dn_pre_bwd(y, dq, dk, dv, f"{tag}_dn_pre_bwd")
    d_dn_qkv, d_conv = _conv_bwd(proj, p["conv_w"], dy, f"{tag}_conv_bwd")
    d_narrow, s_alog, s_dt, s_fb = _narrow_bwd(proj, p["narrow"], act, dact, dlogf, f"{tag}_narrow_bwd")
    dproj = jnp.concatenate([d_dn_qkv, d_fox_qkv, d_gd, d_gf, d_narrow], axis=1)
    dh = _mm(dproj, w_in, "nt", tm=512, tn=d, out_dtype=F32, name=f"{tag}_in_bwd", b_lead=(j,))
    g_in = _mm(h, dproj, "tn", tm=512, tn=EVEN_WIDTH // 3, out_dtype=F32, name=f"{tag}_in_dw", into=(g_in, 0))
    dx, dx16, d_gain = _rms_bwd(x, gain, dh, dxo, f"{tag}_norm_bwd")
    small = {"conv_w": d_conv, "a_log": s_alog, "dt_bias": s_dt, "f_bias": s_fb, "dn_norm_g": d_ng, "q_g": d_qg,
             "k_g": d_kg}
    return dx, dx16, d_gain, small, g_in, g_out


def _odd_fwd(x, gain, w_in, w_out, j, tag):
    h = _rms_fwd(x, gain, f"{tag}_norm")
    qkv = _mm(h, w_in, "nn", tm=512, tn=w_in.shape[2] // 2, out_dtype=BF16, name=f"{tag}_in", b_lead=(j,))
    o16, o32 = _sb_fwd(qkv, N_SB_HEADS, f"{tag}_sb")
    x2 = _mm(o16, w_out, "nn", tm=512, tn=x.shape[1], out_dtype=F32, name=f"{tag}_out", residual=x, b_lead=(j,))
    return x2, (x, h, qkv, o16, o32)


def _odd_bwd(dxo, dxo16, saved, gain, w_in, w_out, j, tag, g_in, g_out):
    x, h, qkv, o16, o32 = saved
    d = x.shape[1]
    do = _mm(dxo16, w_out, "nt", tm=512, tn=d, out_dtype=BF16, name=f"{tag}_out_bwd", b_lead=(j,))
    g_out = _mm(o16, dxo16, "tn", tm=512, tn=d, out_dtype=F32, name=f"{tag}_out_dw", into=(g_out, 0))
    dq, dk, dv = _sb_bwd(qkv, o32, do, N_SB_HEADS, f"{tag}_sb_bwd")
    dqkv = jnp.concatenate([dq, dk.astype(BF16), dv.astype(BF16)], axis=1)
    dh = _mm(dqkv, w_in, "nt", tm=512, tn=d, out_dtype=F32, name=f"{tag}_in_bwd", b_lead=(j,))
    g_in = _mm(h, dqkv, "tn", tm=512, tn=w_in.shape[2] // 2, out_dtype=F32, name=f"{tag}_in_dw", into=(g_in, 0))
    dx, dx16, d_gain = _rms_bwd(x, gain, dh, dxo, f"{tag}_norm_bwd")
    return dx, dx16, d_gain, g_in, g_out


def _forward_backward(x, target, w):
    depth = w["norm_ffn1"].shape[0]
    row = lambda a, l: a[l][None]

    def even_small(j):
        return {"conv_w": w["dn_conv_w"][j], "narrow": _narrow_params(w["dn_a_log"][j], w["dn_dt_bias"][j],
                                                                     w["fox_f_bias"][j]),
                "dn_norm_g": row(w["dn_norm_g"], j), "q_g": row(w["fox_q_norm_g"], j),
                "k_g": row(w["fox_k_norm_g"], j)}

    saved = []
    for l in range(depth):
        x, s1 = _ffn_fwd(x, row(w["norm_ffn1"], l), w["ffn1_w_gu"], w["ffn1_w_down"], l, "ffn1")
        if l % 2 == 0:
            x, s2 = _even_fwd(x, row(w["norm_mix"], l), w["w_in_even"], w["w_out_even"], l // 2, even_small(l // 2),
                              "even")
        else:
            x, s2 = _odd_fwd(x, row(w["norm_mix"], l), w["w_in_odd"], w["w_out_odd"], l // 2, "odd")
        x, s3 = _ffn_fwd(x, row(w["norm_ffn2"], l), w["ffn2_w_gu"], w["ffn2_w_down"], l, "ffn2")
        saved.append((s1, s2, s3))

    dx, dx16, loss = _loss_grad(x, target, "loss")

    kind_of = dict(BIG)
    d_norm = {k: [None] * depth for k in ("norm_ffn1", "norm_mix", "norm_ffn2")}
    d_even = [None] * ((depth + 1) // 2)
    reduced = {name: [] for name in kind_of}
    pending, token = None, None
    for l in reversed(range(depth)):
        s1, s2, s3 = saved[l]
        mixer = ("w_in_even", "w_out_even") if l % 2 == 0 else ("w_in_odd", "w_out_odd")
        names = ["ffn1_w_gu", "ffn1_w_down", *mixer, "ffn2_w_gu", "ffn2_w_down"]
        g = {name: lax.empty((1,) + w[name].shape[1:], F32) for name in names}
        dx, dx16, d_norm["norm_ffn2"][l], g["ffn2_w_gu"], g["ffn2_w_down"] = _ffn_bwd(
            dx, dx16, s3, row(w["norm_ffn2"], l), w["ffn2_w_gu"], w["ffn2_w_down"], l, "ffn2", g["ffn2_w_gu"],
            g["ffn2_w_down"], after=token)
        if l % 2 == 0:
            dx, dx16, d_norm["norm_mix"][l], d_even[l // 2], g["w_in_even"], g["w_out_even"] = _even_bwd(
                dx, dx16, s2, row(w["norm_mix"], l), w["w_in_even"], w["w_out_even"], l // 2, even_small(l // 2),
                "even", g["w_in_even"], g["w_out_even"])
            g["w_in_even"] = _even_grad_quarters(g["w_in_even"])
        else:
            dx, dx16, d_norm["norm_mix"][l], g["w_in_odd"], g["w_out_odd"] = _odd_bwd(
                dx, dx16, s2, row(w["norm_mix"], l), w["w_in_odd"], w["w_out_odd"], l // 2, "odd", g["w_in_odd"],
                g["w_out_odd"])
        dx, dx16, d_norm["norm_ffn1"][l], g["ffn1_w_gu"], g["ffn1_w_down"] = _ffn_bwd(
            dx, dx16, s1, row(w["norm_ffn1"], l), w["ffn1_w_gu"], w["ffn1_w_down"], l, "ffn1", g["ffn1_w_gu"],
            g["ffn1_w_down"])
        if pending is not None:
            for name, quarter in zip(pending[-2], _reduce_finish(pending, dx)):
                reduced[name].insert(0, quarter)
        pending, token = _reduce_start([g[name] for name in names], [kind_of[name] for name in names], names,
                                       f"layer{l}")
    for name, quarter in zip(pending[-2], _reduce_finish(pending, dx)):
        reduced[name].insert(0, quarter)
    big = {name: jnp.concatenate(parts, axis=0) for name, parts in reduced.items()}

    small = {k: jnp.concatenate(v, axis=0) for k, v in d_norm.items()}
    dec = slice(LANE_DECAY, LANE_DECAY + N_DN_HEADS)
    fgt = slice(LANE_FORGET, LANE_FORGET + N_FOX_HEADS)
    small["dn_conv_w"] = jnp.stack([e["conv_w"] for e in d_even])
    small["dn_a_log"] = jnp.concatenate([e["a_log"][:, dec] for e in d_even], axis=0)
    small["dn_dt_bias"] = jnp.concatenate([e["dt_bias"][:, dec] for e in d_even], axis=0)
    small["fox_f_bias"] = jnp.concatenate([e["f_bias"][:, fgt] for e in d_even], axis=0)
    small["dn_norm_g"] = jnp.concatenate([e["dn_norm_g"] for e in d_even], axis=0)
    small["fox_q_norm_g"] = jnp.concatenate([e["q_g"] for e in d_even], axis=0)
    small["fox_k_norm_g"] = jnp.concatenate([e["k_g"] for e in d_even], axis=0)
    return loss, dx, big, small


MESH = pl.DeviceIdType.MESH
ANY = pl.BlockSpec(memory_space=pl.ANY)


def _place():
    x, y, c = lax.axis_index("x"), lax.axis_index("y"), lax.axis_index("c")
    return x, y, c, [(1 - x, y), (x, 1 - y), (1 - x, 1 - y)]


def _remote(src, dst, send_sem, recv_sem, to):
    return pltpu.make_async_remote_copy(src_ref=src, dst_ref=dst, send_sem=send_sem, recv_sem=recv_sem,
                                        device_id=to, device_id_type=MESH)


def _aligned(start, multiple):
    return start if isinstance(start, int) else pl.multiple_of(start, multiple)


def _quarter(ref, kind, chip, half, rows, cols):
    k = 2 * chip[0] + chip[1]
    hr = rows // 2
    assert hr % 16 == 0 and cols % LANES == 0
    if kind == "col":
        return ref.at[:, pl.ds(_aligned(half * hr, 16), hr), pl.ds(_aligned(k * cols, LANES), cols)]
    return ref.at[:, pl.ds(_aligned(k * rows + half * hr, 16), hr), :]


def _place_quarter(shard, kind, kc, name):
    l, rows, cols = shard.shape
    tr = rows
    while tr * cols * 4 > (2 << 20) and tr % 32 == 0:
        tr //= 2
    nr = rows // tr
    if kind == "col":
        out_spec = pl.BlockSpec((None, tr, cols), lambda li, i, kc_ref: (li, i, kc_ref[0]))
        out_shape = (l, rows, 4 * cols)
    else:
        out_spec = pl.BlockSpec((None, tr, cols), lambda li, i, kc_ref: (li, kc_ref[0] * nr + i, 0))
        out_shape = (l, 4 * rows, cols)

    def body(kc_ref, x_ref, o_ref):
        o_ref[...] = x_ref[...].astype(BF16)

    return pl.pallas_call(
        body, name=name,
        grid_spec=pltpu.PrefetchScalarGridSpec(
            num_scalar_prefetch=1, grid=(l, nr),
            in_specs=[pl.BlockSpec((None, tr, cols), lambda li, i, kc_ref: (li, i, 0))], out_specs=out_spec),
        out_shape=jax.ShapeDtypeStruct(out_shape, BF16),
        compiler_params=_cparams("parallel", "parallel"),
    )(kc, shard)


def _gather_weights(wholes, kinds):
    n = len(wholes)

    def dims(ref, kind):
        _, r, cc = ref.shape
        return (r, cc // 4) if kind == "col" else (r // 4, cc)

    def body(*refs):
        bufs = refs[n:2 * n]
        send_sems, recv_sems = refs[2 * n:]
        x, y, c, chips = _place()
        sibling = (x, y, 1 - c)
        first, passed = [], []
        for t in range(n):
            rows, cols = dims(bufs[t], kinds[t])
            mine = _quarter(bufs[t], kinds[t], (x, y), c, rows, cols)
            for j, chip in enumerate(chips):
                cp = _remote(mine, mine, send_sems.at[t, j], recv_sems.at[t, j], (*chip, c))
                cp.start()
                first.append(cp)
        for j, chip in enumerate(chips):
            for t in range(n):
                rows, cols = dims(bufs[t], kinds[t])
                got = _quarter(bufs[t], kinds[t], chip, c, rows, cols)
                _remote(got, got, send_sems.at[t, j], recv_sems.at[t, j], (*chip, c)).wait_recv()
                cp = _remote(got, got, send_sems.at[t, 3 + j], recv_sems.at[t, 3 + j], sibling)
                cp.start()
                passed.append(cp)
        for j, chip in enumerate(chips):
            for t in range(n):
                rows, cols = dims(bufs[t], kinds[t])
                got = _quarter(bufs[t], kinds[t], chip, 1 - c, rows, cols)
                _remote(got, got, send_sems.at[t, 3 + j], recv_sems.at[t, 3 + j], sibling).wait_recv()
        for cp in first + passed:
            cp.wait_send()

    return pl.pallas_call(
        body, name="gather_weights", in_specs=[ANY] * n, out_specs=[ANY] * n,
        out_shape=[jax.ShapeDtypeStruct(a.shape, a.dtype) for a in wholes],
        input_output_aliases={t: t for t in range(n)},
        scratch_shapes=[pltpu.SemaphoreType.DMA((n, 6)), pltpu.SemaphoreType.DMA((n, 6))],
        compiler_params=pltpu.CompilerParams(has_side_effects=True),
    )(*wholes)


def _canonical(a, kind):
    l, r, c = a.shape
    return a.reshape(l, 1, r, c) if kind == "col" else a.reshape(l, 4, r // 4, c)


def _rs_sibling(parts):
    n = len(parts)

    def body(*refs):
        ins, outs = refs[:n], refs[n:2 * n]
        send_sems, recv_sems = refs[2 * n:]
        x, y, c, _ = _place()
        copies = []
        for t in range(n):
            hr = ins[t].shape[2] // 2
            src = ins[t].at[:, :, pl.ds(pl.multiple_of((1 - c) * hr, 8), hr), :]
            cp = _remote(src, outs[t], send_sems.at[t], recv_sems.at[t], (x, y, 1 - c))
            cp.start()
            copies.append(cp)
        for cp in copies:
            cp.wait()

    half = lambda a: jax.ShapeDtypeStruct(a.shape[:2] + (a.shape[2] // 2, a.shape[3]), a.dtype)
    return pl.pallas_call(
        body, name="reduce_sibling", in_specs=[ANY] * n, out_specs=[ANY] * n, out_shape=[half(a) for a in parts],
        scratch_shapes=[pltpu.SemaphoreType.DMA((n,)), pltpu.SemaphoreType.DMA((n,))],
        compiler_params=pltpu.CompilerParams(has_side_effects=True),
    )(*parts)


def _add_tile(rows, cols):
    tc = cols if cols <= 1536 else cols // 4
    tr = rows
    while tr * tc * 4 > (1 << 20) and tr % 16 == 0:
        tr //= 2
    return tr, tc


def _rs_add_sibling(part, got, c, name):
    l, a, hr, cols = got.shape
    tr, tc = _add_tile(hr, cols)
    nr = hr // tr

    def body(c_ref, p_ref, g_ref, o32_ref, o16_ref):
        s = p_ref[...] + g_ref[...]
        o32_ref[...] = s
        o16_ref[...] = s.astype(BF16)

    blk = (None, None, tr, tc)
    spec = pl.BlockSpec(blk, lambda li, ai, i, j, c_ref: (li, ai, i, j))
    return pl.pallas_call(
        body, name=name,
        grid_spec=pltpu.PrefetchScalarGridSpec(
            num_scalar_prefetch=1, grid=(l, a, nr, cols // tc),
            in_specs=[pl.BlockSpec(blk, lambda li, ai, i, j, c_ref: (li, ai, c_ref[0] * nr + i, j)), spec],
            out_specs=[spec, spec]),
        out_shape=[jax.ShapeDtypeStruct(got.shape, F32), jax.ShapeDtypeStruct(got.shape, BF16)],
        compiler_params=_cparams("parallel", "parallel", "parallel", "parallel"),
    )(c, part, got)


def _quarter4(ref, kind, chip, cols):
    k = 2 * chip[0] + chip[1]
    if kind == "col":
        return ref.at[:, :, :, pl.ds(pl.multiple_of(k * cols, LANES), cols)]
    return ref.at[:, pl.ds(k, 1), :, :]


HBM = pl.BlockSpec(memory_space=pltpu.HBM)
SEM = pl.BlockSpec(memory_space=pltpu.SEMAPHORE)
SPLIT_COPY = pltpu.SideEffectType.DATAFLOW_SIDE_EFFECTING
OTHER_CHIPS = 3


def _quarter4_shape(a, kind):
    l, _, hr, cols = a.shape
    return (l, 1, hr, cols // 4 if kind == "col" else cols)


def _rs_chips_copies(srcs, lands, sems, kinds):
    x, y, c, chips = _place()
    copies = []
    for t, (src, land) in enumerate(zip(srcs, lands)):
        cols = _quarter4_shape(src, kinds[t])[3]
        for j, chip in enumerate(chips):
            pair = 2 * (OTHER_CHIPS * t + j)
            copies.append(_remote(_quarter4(src, kinds[t], chip, cols), land.at[j], sems[pair], sems[pair + 1],
                                  (*chip, c)))
    return copies


def _rs_chips_start(sums16, kinds, tag):
    n = len(sums16)
    n_sems = 2 * OTHER_CHIPS * n

    def body(*refs):
        srcs, lands, sems = refs[:n], refs[n:2 * n], refs[4 * n:4 * n + n_sems]
        for cp in _rs_chips_copies(srcs, lands, sems, kinds):
            cp.start()
        refs[-1][...] = jnp.zeros_like(refs[-1])

    lands = [lax.empty((OTHER_CHIPS,) + _quarter4_shape(a, k), a.dtype) for a, k in zip(sums16, kinds)]
    held = [pltpu.with_memory_space_constraint(a, pltpu.HBM) for a in (*sums16, *lands)]
    out = pl.pallas_call(
        body, name=f"reduce_chips_start_{tag}", in_specs=[HBM] * (2 * n),
        out_specs=(*[HBM] * (2 * n), *[SEM] * n_sems, pl.BlockSpec(memory_space=pltpu.VMEM)),
        out_shape=(*[pltpu.HBM(a.shape, a.dtype) for a in held], *[pltpu.SemaphoreType.DMA(())] * n_sems,
                   jax.ShapeDtypeStruct((8, LANES), F32)),
        input_output_aliases={i: i for i in range(2 * n)},
        compiler_params=pltpu.CompilerParams(has_side_effects=SPLIT_COPY),
    )(*held)
    return out[2 * n:2 * n + n_sems], out[:n], out[n:2 * n], out[-1]


def _rs_chips_wait(sems, srcs, lands, kinds, after, tag):
    n = len(srcs)

    def body(*refs):
        for cp in _rs_chips_copies(refs[:n], refs[n:2 * n], refs[2 * n:2 * n + len(sems)], kinds):
            cp.wait_send()
            cp.wait_recv()

    out = pl.pallas_call(
        body, name=f"reduce_chips_wait_{tag}", in_specs=[HBM] * (2 * n) + [SEM] * len(sems) + [ANY],
        out_specs=tuple([HBM] * (2 * n)),
        out_shape=tuple(pltpu.HBM(a.shape, a.dtype) for a in (*srcs, *lands)),
        input_output_aliases={i: i for i in range(2 * n)},
        compiler_params=pltpu.CompilerParams(has_side_effects=SPLIT_COPY),
    )(*srcs, *lands, *sems, after)
    return out[n:]


def _rs_add_chips(sum32, got, kind, kc, name):
    _, l, _, hr, cols = got.shape
    tr, _ = _add_tile(hr, cols)
    nr = hr // tr
    k_arr, c_arr = kc
    if kind == "col":
        own = pl.BlockSpec((None, None, tr, cols), lambda li, i, k_ref, c_ref: (li, 0, i, k_ref[0]))
    else:
        own = pl.BlockSpec((None, None, tr, cols), lambda li, i, k_ref, c_ref: (li, k_ref[0], i, 0))

    def body(k_ref, c_ref, own_ref, got_ref, o_ref):
        o_ref[...] = ((own_ref[...] + got_ref[0].astype(F32)) + got_ref[1].astype(F32)) + got_ref[2].astype(F32)

    return pl.pallas_call(
        body, name=name,
        grid_spec=pltpu.PrefetchScalarGridSpec(
            num_scalar_prefetch=2, grid=(l, nr),
            in_specs=[own, pl.BlockSpec((3, None, None, tr, cols), lambda li, i, k_ref, c_ref: (0, li, 0, i, 0))],
            out_specs=pl.BlockSpec((None, tr, cols), lambda li, i, k_ref, c_ref: (li, c_ref[0] * nr + i, 0))),
        out_shape=jax.ShapeDtypeStruct((l, 2 * hr, cols), F32),
        compiler_params=_cparams("parallel", "parallel"),
    )(k_arr, c_arr, sum32, got)


def _rs_finish(quarters):
    n = len(quarters)

    def body(*refs):
        bufs = refs[n:2 * n]
        send_sems, recv_sems = refs[2 * n:]
        x, y, c, _ = _place()
        copies = []
        for t in range(n):
            hr = bufs[t].shape[1] // 2
            mine = bufs[t].at[:, pl.ds(pl.multiple_of(c * hr, 8), hr), :]
            cp = _remote(mine, mine, send_sems.at[t], recv_sems.at[t], (x, y, 1 - c))
            cp.start()
            copies.append(cp)
        for cp in copies:
            cp.wait()

    return pl.pallas_call(
        body, name="reduce_finish", in_specs=[ANY] * n, out_specs=[ANY] * n,
        out_shape=[jax.ShapeDtypeStruct(a.shape, a.dtype) for a in quarters],
        input_output_aliases={t: t for t in range(n)},
        scratch_shapes=[pltpu.SemaphoreType.DMA((n,)), pltpu.SemaphoreType.DMA((n,))],
        compiler_params=pltpu.CompilerParams(has_side_effects=True),
    )(*quarters)


def _reduce_start(parts, kinds, names, tag):
    c_arr = jnp.reshape(lax.axis_index("c"), (1,)).astype(jnp.int32)
    canon = [_canonical(p, kind) for p, kind in zip(parts, kinds)]
    from_sibling = _rs_sibling(canon)
    sums = [_rs_add_sibling(p, g, c_arr, f"reduce_add_sibling_{nm}") for p, g, nm in zip(canon, from_sibling, names)]
    sems, srcs, lands, token = _rs_chips_start([s16 for _, s16 in sums], kinds, tag)
    return (sems, srcs, lands, [s32 for s32, _ in sums], kinds, names, tag), token


def _reduce_finish(state, after):
    sems, srcs, lands, sums32, kinds, names, tag = state
    x, y, c = lax.axis_index("x"), lax.axis_index("y"), lax.axis_index("c")
    kc = (jnp.reshape(2 * x + y, (1,)).astype(jnp.int32), jnp.reshape(c, (1,)).astype(jnp.int32))
    from_chips = _rs_chips_wait(sems, srcs, lands, kinds, after, tag)
    halves = [_rs_add_chips(s32, g, kind, kc, f"reduce_add_chips_{nm}")
              for s32, g, kind, nm in zip(sums32, from_chips, kinds, names)]
    return _rs_finish(halves)


SMALL_PEERS = 7


def _small_exchange(pack):
    rows = pack.shape[0]

    def body(p_ref, slots_ref, total_ref, send_sems, recv_sems):
        x, y, c, _ = _place()
        me = 4 * x + 2 * y + c
        slots_ref[me] = p_ref[...]
        copies = []
        for p in range(1, SMALL_PEERS + 1):
            px, py, pc = (p >> 2) & 1, (p >> 1) & 1, p & 1
            peer = (1 - x if px else x, 1 - y if py else y, 1 - c if pc else c)
            cp = _remote(p_ref, slots_ref.at[me], send_sems.at[p - 1], recv_sems.at[p - 1], peer)
            cp.start()
            copies.append(cp)
        for cp in copies:
            cp.wait()
        total = slots_ref[0]
        for i in range(1, SMALL_PEERS + 1):
            total = total + slots_ref[i]
        total_ref[...] = total

    vmem = pl.BlockSpec(memory_space=pltpu.VMEM)
    return pl.pallas_call(
        body, name="small_exchange", in_specs=[vmem], out_specs=[vmem, vmem],
        out_shape=[jax.ShapeDtypeStruct((SMALL_PEERS + 1, rows, LANES), F32), jax.ShapeDtypeStruct((rows, LANES), F32)],
        scratch_shapes=[pltpu.SemaphoreType.DMA((SMALL_PEERS,)), pltpu.SemaphoreType.DMA((SMALL_PEERS,))],
        compiler_params=pltpu.CompilerParams(has_side_effects=True),
    )(pack)


def _pack(arrays):
    rows = []
    for a in arrays:
        flat = a.reshape(-1).astype(F32)
        rows.append(jnp.pad(flat, (0, (-flat.shape[0]) % LANES)).reshape(-1, LANES))
    out = jnp.concatenate(rows, axis=0)
    return jnp.pad(out, ((0, (-out.shape[0]) % 8), (0, 0)))


def _unpack(pack, shapes):
    out, r = [], 0
    for sh in shapes:
        size = math.prod(sh)
        nr = -(-size // LANES)
        out.append(pack[r:r + nr].reshape(-1)[:size].reshape(sh))
        r += nr
    return out


def _adamw(w, g, m, v, name):
    shape = w.shape
    to2d = lambda a: a.reshape(-1, shape[-1])
    rows = math.prod(shape[:-1])
    tile = 256 if rows % 256 == 0 else rows

    def fn(wb, gb, mb, vb):
        m2 = ADAM_B1 * mb + (1.0 - ADAM_B1) * gb
        v2 = ADAM_B2 * vb + (1.0 - ADAM_B2) * (gb * gb)
        m_hat = m2 / (1.0 - ADAM_B1 ** ADAM_STEP)
        v_hat = v2 / (1.0 - ADAM_B2 ** ADAM_STEP)
        return -ADAM_LR * (m_hat / (jnp.sqrt(v_hat) + ADAM_EPS) + ADAM_WD * wb), m2, v2

    res = _rowwise(fn, [to2d(w), to2d(g), to2d(m), to2d(v)], [], [(shape[-1], F32)] * 3, [], tile=tile, name=name)
    return [r.reshape(shape) for r in res]


BIG = (("ffn1_w_gu", "col"), ("ffn1_w_down", "row"), ("w_in_even", "col"), ("w_out_even", "row"),
       ("w_in_odd", "col"), ("w_out_odd", "row"), ("ffn2_w_gu", "col"), ("ffn2_w_down", "row"))
SMALL = ("norm_ffn1", "norm_mix", "dn_conv_w", "dn_a_log", "dn_dt_bias", "dn_norm_g", "fox_q_norm_g", "fox_k_norm_g",
         "fox_f_bias", "norm_ffn2")
WEIGHTS = ("norm_ffn1", "ffn1_w_gu", "ffn1_w_down", "norm_mix", "w_in_even", "dn_conv_w", "dn_a_log", "dn_dt_bias",
           "dn_norm_g", "fox_q_norm_g", "fox_k_norm_g", "fox_f_bias", "w_out_even", "w_in_odd", "w_out_odd",
           "norm_ffn2", "ffn2_w_gu", "ffn2_w_down")


def _step(x, target, w, m, v):
    k = 2 * lax.axis_index("x") + lax.axis_index("y")
    n_conv = w["dn_conv_w"].shape[2]

    kc = jnp.reshape(k, (1,)).astype(jnp.int32)
    placed = []
    for name, kind in BIG:
        a = w[name]
        if name == "w_in_even":
            a = jnp.pad(a, ((0, 0), (0, 0), (0, EVEN_QUARTER_PAD - EVEN_QUARTER)))
        placed.append(_place_quarter(a, kind, kc, f"place_{name}"))
    whole = dict(zip([n for n, _ in BIG], _gather_weights(placed, [kind for _, kind in BIG])))
    padded = whole["w_in_even"]
    ref_order = jnp.concatenate([padded[..., q * EVEN_QUARTER_PAD:q * EVEN_QUARTER_PAD + EVEN_QUARTER]
                                 for q in range(4)], axis=-1)
    whole["w_in_even"] = _even_to_kernel_layout(ref_order)
    conv_slots, _ = _small_exchange(_pack([w["dn_conv_w"]]))
    conv_rows = math.prod(w["dn_conv_w"].shape) // LANES
    quarters = [conv_slots[2 * q, :conv_rows].reshape(w["dn_conv_w"].shape) for q in range(4)]
    whole["dn_conv_w"] = jnp.concatenate(quarters, axis=-1)
    for name in SMALL:
        if name != "dn_conv_w":
            whole[name] = w[name]

    loss, dx, reduced, small = _forward_backward(x, target, whole)

    reduced["w_in_even"] = reduced["w_in_even"][..., :EVEN_QUARTER]
    _, small_sum = _small_exchange(_pack([small[n] for n in SMALL]))
    grads = dict(zip(SMALL, _unpack(small_sum, [small[n].shape for n in SMALL])))
    grads["dn_conv_w"] = lax.dynamic_slice_in_dim(grads["dn_conv_w"], k * n_conv, n_conv, axis=2)
    grads.update(reduced)

    delta, new_m, new_v = {}, {}, {}
    for name, _ in BIG:
        delta[name], new_m[name], new_v[name] = _adamw(w[name], grads[name], m[name], v[name], f"adamw_{name}")
    packs = [_pack([d[n] for n in SMALL]) for d in (w, grads, m, v)]
    shapes = [w[n].shape for n in SMALL]
    for out, res in zip((delta, new_m, new_v), _adamw(*packs, "adamw_small")):
        out.update(zip(SMALL, _unpack(res, shapes)))
    total_loss = lax.psum(loss[0, 0], ("x", "y", "c"))
    return total_loss, dx, grads, delta, new_m, new_v


def kernel(x, norm_ffn1, ffn1_w_gu, ffn1_w_down, norm_mix, w_in_even, dn_conv_w, dn_a_log, dn_dt_bias, dn_norm_g, fox_q_norm_g, fox_k_norm_g, fox_f_bias, w_out_even, w_in_odd, w_out_odd, norm_ffn2, ffn2_w_gu, ffn2_w_down, loss_target, m_norm_ffn1, m_ffn1_w_gu, m_ffn1_w_down, m_norm_mix, m_w_in_even, m_dn_conv_w, m_dn_a_log, m_dn_dt_bias, m_dn_norm_g, m_fox_q_norm_g, m_fox_k_norm_g, m_fox_f_bias, m_w_out_even, m_w_in_odd, m_w_out_odd, m_norm_ffn2, m_ffn2_w_gu, m_ffn2_w_down, v_norm_ffn1, v_ffn1_w_gu, v_ffn1_w_down, v_norm_mix, v_w_in_even, v_dn_conv_w, v_dn_a_log, v_dn_dt_bias, v_dn_norm_g, v_fox_q_norm_g, v_fox_k_norm_g, v_fox_f_bias, v_w_out_even, v_w_in_odd, v_w_out_odd, v_norm_ffn2, v_ffn2_w_gu, v_ffn2_w_down):
    w = dict(zip(WEIGHTS, (norm_ffn1, ffn1_w_gu, ffn1_w_down, norm_mix, w_in_even, dn_conv_w, dn_a_log, dn_dt_bias,
                           dn_norm_g, fox_q_norm_g, fox_k_norm_g, fox_f_bias, w_out_even, w_in_odd, w_out_odd,
                           norm_ffn2, ffn2_w_gu, ffn2_w_down)))
    m = dict(zip(WEIGHTS, (m_norm_ffn1, m_ffn1_w_gu, m_ffn1_w_down, m_norm_mix, m_w_in_even, m_dn_conv_w, m_dn_a_log,
                           m_dn_dt_bias, m_dn_norm_g, m_fox_q_norm_g, m_fox_k_norm_g, m_fox_f_bias, m_w_out_even,
                           m_w_in_odd, m_w_out_odd, m_norm_ffn2, m_ffn2_w_gu, m_ffn2_w_down)))
    v = dict(zip(WEIGHTS, (v_norm_ffn1, v_ffn1_w_gu, v_ffn1_w_down, v_norm_mix, v_w_in_even, v_dn_conv_w, v_dn_a_log,
                           v_dn_dt_bias, v_dn_norm_g, v_fox_q_norm_g, v_fox_k_norm_g, v_fox_f_bias, v_w_out_even,
                           v_w_in_odd, v_w_out_odd, v_norm_ffn2, v_ffn2_w_gu, v_ffn2_w_down)))
    loss, dx, grads, delta, new_m, new_v = _step(x[0], loss_target[0], w, m, v)
    return (loss, dx[None], *[grads[n] for n in WEIGHTS], *[delta[n] for n in WEIGHTS],
            *[new_m[n] for n in WEIGHTS], *[new_v[n] for n in WEIGHTS])
```

```python
import functools
import math

import jax
import jax.numpy as jnp
from jax import lax
from jax.experimental import pallas as pl
from jax.experimental.pallas import tpu as pltpu

F32 = jnp.float32
BF16 = jnp.bfloat16
HI = lax.Precision.HIGHEST

HEAD_DIM = 128
N_DN_HEADS = 4
N_FOX_HEADS = 4
N_SB_HEADS = 8
D_DN = N_DN_HEADS * HEAD_DIM
D_FOX = N_FOX_HEADS * HEAD_DIM
CONV_WIDTH = 4
DN_CHUNK = 64
EPS = 1e-6
ATT_SCALE = HEAD_DIM ** -0.5
ADAM_LR, ADAM_B1, ADAM_B2, ADAM_EPS, ADAM_WD, ADAM_STEP = 0.001, 0.9, 0.999, 1e-08, 0.01, 10

V7X_VMEM_LIMIT = 56 * 1024 * 1024
LANES = 128
ATT_TQ = 256
ATT_TK = 128
ATT_SUB = ATT_TQ // ATT_TK

LANE_BETA, LANE_DECAY, LANE_FORGET = 0, 4, 8


def _cparams(*sem):
    return pltpu.CompilerParams(dimension_semantics=sem, vmem_limit_bytes=V7X_VMEM_LIMIT)


def _sigmoid(x):
    return 1.0 / (1.0 + jnp.exp(-x))


def _softplus(x):
    return jnp.maximum(x, 0.0) + jnp.log(1.0 + jnp.exp(-jnp.abs(x)))


def _silu_grad(y, sg):
    return sg * (1.0 + y * (1.0 - sg))


def _rowwise(fn, rows, bcast, outs, sums, *, tile, name):
    rows = [r if isinstance(r, tuple) else (r, r.shape[1], 0) for r in rows]
    s = rows[0][0].shape[0]
    assert s % tile == 0
    n_in, n_b, n_out, n_sum = len(rows), len(bcast), len(outs), len(sums)

    def body(*refs):
        ins = [r[...] for r in refs[:n_in + n_b]]
        res = fn(*ins)
        if not isinstance(res, (tuple, list)):
            res = (res,)
        out_refs = refs[n_in + n_b:n_in + n_b + n_out]
        sum_refs = refs[n_in + n_b + n_out:]
        for o_ref, val in zip(out_refs, res[:n_out]):
            o_ref[...] = val.astype(o_ref.dtype)
        if n_sum:
            @pl.when(pl.program_id(0) == 0)
            def _():
                for s_ref in sum_refs:
                    s_ref[...] = jnp.zeros_like(s_ref)
            for s_ref, val in zip(sum_refs, res[n_out:]):
                s_ref[...] += val

    in_specs = [pl.BlockSpec((tile, w), lambda i, cb=cb: (i, cb)) for _, w, cb in rows]
    in_specs += [pl.BlockSpec(b.shape, lambda i, nd=b.ndim: (0,) * nd) for b in bcast]
    out_specs = [pl.BlockSpec((tile, c), lambda i: (i, 0)) for c, _ in outs]
    out_specs += [pl.BlockSpec(sh, lambda i: (0, 0)) for sh in sums]
    out_shape = [jax.ShapeDtypeStruct((s, c), dt) for c, dt in outs]
    out_shape += [jax.ShapeDtypeStruct(sh, F32) for sh in sums]
    return pl.pallas_call(
        body, name=name, grid=(s // tile,), in_specs=in_specs, out_specs=out_specs, out_shape=out_shape,
        compiler_params=_cparams("arbitrary" if n_sum else "parallel"),
    )(*[r[0] for r in rows], *bcast)


def _rms_fwd(x, gain, name):
    def fn(xb, g):
        r = lax.rsqrt(jnp.mean(xb * xb, axis=-1, keepdims=True) + EPS)
        return (xb * r * g,)
    return _rowwise(fn, [x], [gain], [(x.shape[1], BF16)], [], tile=512, name=name)[0]


def _rms_bwd(x, gain, dn, dres, name):
    def fn(xb, dnb, drb, g):
        r = lax.rsqrt(jnp.mean(xb * xb, axis=-1, keepdims=True) + EPS)
        xh = xb * r
        dxh = dnb * g
        dx = drb + r * (dxh - xh * jnp.mean(dxh * xh, axis=-1, keepdims=True))
        return dx, dx, jnp.sum(dnb * xh, axis=0, keepdims=True)
    d = x.shape[1]
    return _rowwise(fn, [x, dn, dres], [gain], [(d, F32), (d, BF16)], [(1, d)], tile=512, name=name)


_DIMS = {"nn": (((1,), (0,)), ((), ())), "nt": (((1,), (1,)), ((), ())), "tn": (((0,), (0,)), ((), ()))}


def _dot(a, b, kind):
    return lax.dot_general(a.astype(BF16), b.astype(BF16), _DIMS[kind], preferred_element_type=F32)


def _dot32(a, b, kind="nn"):
    return lax.dot_general(a, b, _DIMS[kind], precision=HI, preferred_element_type=F32)


def _mm(a, b, kind, *, tm, tn, out_dtype, name, scale=None, residual=None, a_lead=(), b_lead=(),
        b_spec=None, n=None, into=None):
    ash, bsh = a.shape[len(a_lead):], b.shape[len(b_lead):]
    m = ash[1] if kind == "tn" else ash[0]
    k = ash[0] if kind == "tn" else ash[1]
    if b_spec is None:
        n = bsh[0] if kind == "nt" else bsh[1]
        assert k == (bsh[1] if kind == "nt" else bsh[0]), (ash, bsh, kind)
    assert m % tm == 0 and n % tn == 0, (m, tm, n, tn)
    la, lb = (None,) * len(a_lead), (None,) * len(b_lead)
    if kind == "tn":
        a_spec = pl.BlockSpec(la + (k, tm), lambda j, i: a_lead + (0, i))
    else:
        a_spec = pl.BlockSpec(la + (tm, k), lambda j, i: a_lead + (i, 0))
    if b_spec is None:
        if kind == "nt":
            b_spec = pl.BlockSpec(lb + (tn, k), lambda j, i: b_lead + (j, 0))
        else:
            b_spec = pl.BlockSpec(lb + (k, tn), lambda j, i: b_lead + (0, j))
    in_specs, args = [a_spec, b_spec], [a, b]
    if residual is not None:
        in_specs.append(pl.BlockSpec((tm, tn), lambda j, i: (i, j)))
        args.append(residual)
    aliases = {}
    if into is not None:
        buf, layer = into
        in_specs.append(pl.BlockSpec(memory_space=pl.ANY))
        args.append(buf)
        aliases = {len(args) - 1: 0}
        out_spec = pl.BlockSpec((None, tm, tn), lambda j, i: (layer, i, j))
        out_shape = jax.ShapeDtypeStruct(buf.shape, buf.dtype)
    else:
        out_spec = pl.BlockSpec((tm, tn), lambda j, i: (i, j))
        out_shape = jax.ShapeDtypeStruct((m, n), out_dtype)

    def body(a_ref, b_ref, *rest):
        acc = _dot(a_ref[...], b_ref[...], kind)
        if scale is not None:
            acc = acc * scale
        if residual is not None:
            acc = acc + rest[0][...]
        rest[-1][...] = acc.astype(rest[-1].dtype)

    return pl.pallas_call(
        body, name=name, grid=(n // tn, m // tm), in_specs=in_specs, out_specs=out_spec, out_shape=out_shape,
        input_output_aliases=aliases, compiler_params=_cparams("parallel", "parallel"),
    )(*args)


def _ffn_up(n, w_gu, layer, name):
    s, d = n.shape
    f = w_gu.shape[2] // 2
    tm, tn = 512, f // 2
    nj = f // tn

    def body(n_ref, wg_ref, wu_ref, gu_ref, a_ref):
        nv = n_ref[...]
        g = _dot(nv, wg_ref[...], "nn")
        u = _dot(nv, wu_ref[...], "nn")
        gu_ref[0] = g.astype(BF16)
        gu_ref[1] = u.astype(BF16)
        a_ref[...] = (g * _sigmoid(g) * u).astype(BF16)

    return pl.pallas_call(
        body, name=name, grid=(nj, s // tm),
        in_specs=[pl.BlockSpec((tm, d), lambda j, i: (i, 0)),
                  pl.BlockSpec((None, d, tn), lambda j, i: (layer, 0, j)),
                  pl.BlockSpec((None, d, tn), lambda j, i: (layer, 0, j + nj))],
        out_specs=[pl.BlockSpec((2, tm, tn), lambda j, i: (0, i, j)),
                   pl.BlockSpec((tm, tn), lambda j, i: (i, j))],
        out_shape=[jax.ShapeDtypeStruct((2, s, f), BF16), jax.ShapeDtypeStruct((s, f), BF16)],
        compiler_params=_cparams("parallel", "parallel"),
    )(n, w_gu, w_gu)


def _ffn_down_bwd(dxo, w_down, gu, layer, name, after=None):
    s, d = dxo.shape
    f = w_down.shape[1]
    tm, tn = 512, f // 2
    extra_specs, extra = ([ANY], [after]) if after is not None else ([], [])

    def body(dx_ref, w_ref, gu_ref, *rest):
        dgu_ref = rest[-1]
        da = 0.5 * _dot(dx_ref[...], w_ref[...], "nt")
        g = gu_ref[0].astype(F32)
        u = gu_ref[1].astype(F32)
        sg = _sigmoid(g)
        dgu_ref[0] = (da * u * _silu_grad(g, sg)).astype(BF16)
        dgu_ref[1] = (da * g * sg).astype(BF16)

    return pl.pallas_call(
        body, name=name, grid=(f // tn, s // tm),
        in_specs=[pl.BlockSpec((tm, d), lambda j, i: (i, 0)),
                  pl.BlockSpec((None, tn, d), lambda j, i: (layer, j, 0)),
                  pl.BlockSpec((2, tm, tn), lambda j, i: (0, i, j))] + extra_specs,
        out_specs=pl.BlockSpec((2, tm, tn), lambda j, i: (0, i, j)),
        out_shape=jax.ShapeDtypeStruct((2, s, f), BF16),
        compiler_params=_cparams("parallel", "parallel"),
    )(dxo, w_down, gu, *extra)


def _ffn_dn(dgu, w_gu, layer, name):
    _, s, f = dgu.shape
    d = w_gu.shape[1]
    tm, tn = 512, d

    def body(dgu_ref, wg_ref, wu_ref, o_ref):
        o_ref[...] = _dot(dgu_ref[0], wg_ref[...], "nt") + _dot(dgu_ref[1], wu_ref[...], "nt")

    return pl.pallas_call(
        body, name=name, grid=(s // tm, d // tn),
        in_specs=[pl.BlockSpec((2, tm, f), lambda i, j: (0, i, 0)),
                  pl.BlockSpec((None, tn, f), lambda i, j: (layer, j, 0)),
                  pl.BlockSpec((None, tn, f), lambda i, j: (layer, j, 1))],
        out_specs=pl.BlockSpec((tm, tn), lambda i, j: (i, j)),
        out_shape=jax.ShapeDtypeStruct((s, d), F32),
        compiler_params=_cparams("parallel", "parallel"),
    )(dgu, w_gu, w_gu)


def _ffn_fwd(x, gain, w_gu, w_down, layer, tag):
    n = _rms_fwd(x, gain, f"{tag}_norm")
    gu, a = _ffn_up(n, w_gu, layer, f"{tag}_up")
    x2 = _mm(a, w_down, "nn", tm=512, tn=x.shape[1], out_dtype=F32, name=f"{tag}_down", scale=0.5, residual=x,
             b_lead=(layer,))
    return x2, (x, n, gu, a)


def _ffn_bwd(dxo, dxo16, saved, gain, w_gu, w_down, layer, tag, g_gu, g_down, after=None):
    x, n, gu, a = saved
    s, f = a.shape
    dgu = _ffn_down_bwd(dxo16, w_down, gu, layer, f"{tag}_down_bwd", after)
    g_down = _mm(a, dxo16, "tn", tm=256, tn=dxo16.shape[1], out_dtype=F32, name=f"{tag}_down_dw", scale=0.5,
                 into=(g_down, 0))
    dn = _ffn_dn(dgu, w_gu, layer, f"{tag}_up_bwd")
    tn = f // 2
    nj = f // tn
    g_gu = _mm(n, dgu, "tn", tm=512, tn=tn, out_dtype=F32, name=f"{tag}_up_dw", into=(g_gu, 0), n=2 * f,
               b_spec=pl.BlockSpec((None, s, tn), lambda j, i: (j // nj, 0, j % nj)))
    dx, dx16, dgain = _rms_bwd(x, gain, dn, dxo, f"{tag}_norm_bwd")
    return dx, dx16, dgain, g_gu, g_down


def _lane_col(blk, lane):
    li = lax.broadcasted_iota(jnp.int32, blk.shape, 1)
    return jnp.sum(jnp.where(li == lane, blk, 0.0), axis=1, keepdims=True)


def _split_dot(x, tri):
    hi = x.astype(BF16)
    lo = (x - hi.astype(F32)).astype(BF16)
    return (lax.dot_general(hi, tri, _DIMS["nn"], preferred_element_type=F32)
            + lax.dot_general(lo, tri, _DIMS["nn"], preferred_element_type=F32))


class _Each:
    def __init__(self, vals):
        self.vals = list(vals)

    def _with(self, other, op):
        others = other.vals if isinstance(other, _Each) else [other] * len(self.vals)
        return _Each(op(a, b) for a, b in zip(self.vals, others))

    def __add__(self, other):
        return self._with(other, lambda a, b: a + b)

    def __sub__(self, other):
        return self._with(other, lambda a, b: a - b)

    def __mul__(self, other):
        return self._with(other, lambda a, b: a * b)

    def __neg__(self):
        return _Each(-a for a in self.vals)


def _each(fn, *args):
    n = max(len(a.vals) for a in args if isinstance(a, _Each))
    res = [fn(*xs) for xs in zip(*[a.vals if isinstance(a, _Each) else [a] * n for a in args])]
    if isinstance(res[0], tuple):
        return tuple(_Each(r) for r in zip(*res))
    return _Each(res)


def _keep(cond, x):
    return _each(lambda v: jnp.where(cond, v, 0.0), x)


def _rowsum(x):
    return _each(lambda v: jnp.sum(v, axis=1, keepdims=True), x)


ATT_HEADS = 2
ATT_WIDTH = ATT_HEADS * HEAD_DIM
_HEAD_COLS = [slice(h * HEAD_DIM, (h + 1) * HEAD_DIM) for h in range(ATT_HEADS)]


def _att_specs(n_heads, s):
    groups = n_heads // ATT_HEADS
    q_spec = pl.BlockSpec((ATT_TQ, ATT_WIDTH), lambda g, i: (i, g))
    k_spec = pl.BlockSpec((s, ATT_WIDTH), lambda g, i: (0, groups + g))
    v_spec = pl.BlockSpec((s, ATT_WIDTH), lambda g, i: (0, 2 * groups + g))
    return q_spec, k_spec, v_spec


def _heads_of(ref, rows=None):
    return _Each(ref[:, cs] if rows is None else ref[rows, cs] for cs in _HEAD_COLS)


def _dot_each(a, b, kind):
    return _each(lambda x, y: _dot(x, y, kind), a, b)


def _att_iotas():
    row = lax.broadcasted_iota(jnp.int32, (ATT_TQ, ATT_TK), 0)
    col = lax.broadcasted_iota(jnp.int32, (ATT_TQ, ATT_TK), 1)
    jr = lax.broadcasted_iota(jnp.int32, (ATT_TK, ATT_TK), 0)
    jc = lax.broadcasted_iota(jnp.int32, (ATT_TK, ATT_TK), 1)
    return row, col, jr, jc


def _sb_fwd(qkv, n_heads, name):
    s = qkv.shape[0]

    def body(q_ref, k_ref, v_ref, o16_ref, o32_ref):
        i = pl.program_id(1)
        q = _heads_of(q_ref)
        row, col, jr, jc = _att_iotas()
        later = (jr > jc).astype(BF16)

        def step(jb, carry, diagonal):
            c_sp, acc = (_Each(part) for part in carry)
            work = []
            for sub in reversed(range(ATT_SUB)):
                keys = pl.ds(pl.multiple_of(jb * ATT_TQ + sub * ATT_TK, ATT_TK), ATT_TK)
                z = _dot_each(q, _heads_of(k_ref, keys), "nt") * ATT_SCALE
                sp = _each(_softplus, z)
                before = (col + sub * ATT_TK) < row if diagonal else None
                spm = _keep(before, sp) if diagonal else sp
                work.append((keys, z - sp, spm, _each(lambda x: _dot(x, later, "nn"), spm), before))
            for keys, logsig, spm, within, before in work:
                a = _each(jnp.exp, logsig - (c_sp + within))
                if diagonal:
                    a = _keep(before, a)
                acc = acc + _each(_split_dot, a, _heads_of(v_ref, keys))
                c_sp = c_sp + _rowsum(spm)
            return tuple(c_sp.vals), tuple(acc.vals)

        zeros = lambda width: tuple(jnp.zeros((ATT_TQ, width), F32) for _ in range(ATT_HEADS))
        carry = step(i, (zeros(1), zeros(HEAD_DIM)), True)
        _, acc = lax.fori_loop(0, i, lambda it, cr: step(i - 1 - it, cr, False), carry)
        for cs, acc_h in zip(_HEAD_COLS, acc):
            o16_ref[:, cs] = acc_h.astype(BF16)
            o32_ref[:, cs] = acc_h

    q_spec, k_spec, v_spec = _att_specs(n_heads, s)
    o_spec = pl.BlockSpec((ATT_TQ, ATT_WIDTH), lambda g, i: (i, g))
    return pl.pallas_call(
        body, name=name, grid=(n_heads // ATT_HEADS, s // ATT_TQ), in_specs=[q_spec, k_spec, v_spec],
        out_specs=[o_spec, o_spec],
        out_shape=[jax.ShapeDtypeStruct((s, n_heads * HEAD_DIM), BF16),
                   jax.ShapeDtypeStruct((s, n_heads * HEAD_DIM), F32)],
        compiler_params=_cparams("parallel", "arbitrary"),
    )(qkv, qkv, qkv)


def _sb_bwd(qkv, o32, do, n_heads, name):
    s = qkv.shape[0]

    def body(q_ref, k_ref, v_ref, o_ref, do_ref, dq_ref, dk_ref, dv_ref):
        i = pl.program_id(1)

        @pl.when(i == 0)
        def _():
            dk_ref[...] = jnp.zeros_like(dk_ref)
            dv_ref[...] = jnp.zeros_like(dv_ref)

        q, do = _heads_of(q_ref), _heads_of(do_ref)
        total = _rowsum(_each(lambda a, b: a.astype(F32) * b, do, _heads_of(o_ref)))
        row, col, jr, jc = _att_iotas()
        later = (jr > jc).astype(BF16)
        not_before = (jr >= jc).astype(BF16)

        def step(jb, carry, diagonal):
            c_sp, c_e, dq = (_Each(part) for part in carry)
            work = []
            for sub in reversed(range(ATT_SUB)):
                keys = pl.ds(pl.multiple_of(jb * ATT_TQ + sub * ATT_TK, ATT_TK), ATT_TK)
                k = _heads_of(k_ref, keys)
                z = _dot_each(q, k, "nt") * ATT_SCALE
                sp = _each(_softplus, z)
                before = (col + sub * ATT_TK) < row if diagonal else None
                spm = _keep(before, sp) if diagonal else sp
                work.append((keys, k, _each(jnp.exp, z - sp), spm, _each(lambda x: _dot(x, later, "nn"), spm),
                             _dot_each(do, _heads_of(v_ref, keys), "nt"), before))
            for keys, k, sig, spm, within, da, before in work:
                a = sig * _each(lambda x: jnp.exp(-x), c_sp + within)
                if diagonal:
                    a = _keep(before, a)
                e = a * da
                left = total - c_e - _each(lambda x: _split_dot(x, not_before), e)
                dz = (e - (e + left) * sig) * ATT_SCALE
                if diagonal:
                    dz = _keep(before, dz)
                dk, dv = _dot_each(dz, q, "tn"), _dot_each(a, do, "tn")
                for cs, dk_h, dv_h in zip(_HEAD_COLS, dk.vals, dv.vals):
                    dk_ref[keys, cs] += dk_h
                    dv_ref[keys, cs] += dv_h
                dq = dq + _dot_each(dz, k, "nn")
                c_sp = c_sp + _rowsum(spm)
                c_e = c_e + _rowsum(e)
            return tuple(c_sp.vals), tuple(c_e.vals), tuple(dq.vals)

        zeros = lambda width: tuple(jnp.zeros((ATT_TQ, width), F32) for _ in range(ATT_HEADS))
        carry = step(i, (zeros(1), zeros(1), zeros(HEAD_DIM)), True)
        _, _, dq = lax.fori_loop(0, i, lambda it, cr: step(i - 1 - it, cr, False), carry)
        for cs, dq_h in zip(_HEAD_COLS, dq):
            dq_ref[:, cs] = dq_h.astype(BF16)

    q_spec, k_spec, v_spec = _att_specs(n_heads, s)
    blk = pl.BlockSpec((ATT_TQ, ATT_WIDTH), lambda g, i: (i, g))
    full = pl.BlockSpec((s, ATT_WIDTH), lambda g, i: (0, g))
    wide = (s, n_heads * HEAD_DIM)
    return pl.pallas_call(
        body, name=name, grid=(n_heads // ATT_HEADS, s // ATT_TQ), in_specs=[q_spec, k_spec, v_spec, blk, blk],
        out_specs=[blk, full, full],
        out_shape=[jax.ShapeDtypeStruct(wide, BF16), jax.ShapeDtypeStruct(wide, F32), jax.ShapeDtypeStruct(wide, F32)],
        compiler_params=_cparams("parallel", "arbitrary"),
    )(qkv, qkv, qkv, o32, do)


def _fox_logits(q, k, cq, ct_ref, keys):
    ck = _Each(ct_ref[h, :, keys] for h in range(ATT_HEADS))
    return _dot_each(q, k, "nt") * ATT_SCALE + (cq - ck)


def _fox_cq(c_ref, group):
    c = c_ref[...]
    return _Each(_lane_col(c, LANE_FORGET + group * ATT_HEADS + h) for h in range(ATT_HEADS))


def _fox_fwd(qkv, c, ct, name):
    s = qkv.shape[0]
    n_heads = N_FOX_HEADS

    def body(q_ref, k_ref, v_ref, c_ref, ct_ref, o_ref, lse_ref):
        g, i = pl.program_id(0), pl.program_id(1)
        q = _heads_of(q_ref)
        cq = _fox_cq(c_ref, g)
        row, col, _, _ = _att_iotas()

        def step(jb, carry, diagonal):
            m, l, acc = (_Each(part) for part in carry)
            work = []
            m_new = m
            for sub in range(ATT_SUB):
                keys = pl.ds(pl.multiple_of(jb * ATT_TQ + sub * ATT_TK, ATT_TK), ATT_TK)
                sc = _fox_logits(q, _heads_of(k_ref, keys), cq, ct_ref, keys)
                valid = (col + sub * ATT_TK) <= row if diagonal else None
                if diagonal:
                    sc = _each(lambda x: jnp.where(valid, x, -1e30), sc)
                m_new = _each(lambda a, x: jnp.maximum(a, jnp.max(x, axis=1, keepdims=True)), m_new, sc)
                work.append((keys, sc, valid))
            w = _each(jnp.exp, m - m_new)
            l, acc = l * w, acc * w
            for keys, sc, valid in work:
                p = _each(jnp.exp, sc - m_new)
                if diagonal:
                    p = _keep(valid, p)
                l = l + _rowsum(p)
                acc = acc + _each(_split_dot, p, _heads_of(v_ref, keys))
            return tuple(m_new.vals), tuple(l.vals), tuple(acc.vals)

        per_head = lambda width, value: tuple(jnp.full((ATT_TQ, width), value, F32) for _ in range(ATT_HEADS))
        init = (per_head(1, -1e30), per_head(1, 0.0), per_head(HEAD_DIM, 0.0))
        m, l, acc = lax.fori_loop(0, i, lambda jb, cr: step(jb, cr, False), step(i, init, True))
        for h, cs in enumerate(_HEAD_COLS):
            o_ref[:, cs] = acc[h] / l[h]
            lse_ref[h] = jnp.broadcast_to(m[h] + jnp.log(l[h]), (ATT_TQ, LANES))

    q_spec, k_spec, v_spec = _att_specs(n_heads, s)
    return pl.pallas_call(
        body, name=name, grid=(n_heads // ATT_HEADS, s // ATT_TQ),
        in_specs=[q_spec, k_spec, v_spec, pl.BlockSpec((ATT_TQ, LANES), lambda g, i: (i, 0)),
                  pl.BlockSpec((ATT_HEADS, 1, s), lambda g, i: (g, 0, 0))],
        out_specs=[pl.BlockSpec((ATT_TQ, ATT_WIDTH), lambda g, i: (i, g)),
                   pl.BlockSpec((ATT_HEADS, ATT_TQ, LANES), lambda g, i: (g, i, 0))],
        out_shape=[jax.ShapeDtypeStruct((s, n_heads * HEAD_DIM), F32),
                   jax.ShapeDtypeStruct((n_heads, s, LANES), F32)],
        compiler_params=_cparams("parallel", "arbitrary"),
    )(qkv, qkv, qkv, c, ct)


def _fox_bwd(qkv, c, ct, o, lse, do, name):
    s = qkv.shape[0]
    n_heads = N_FOX_HEADS

    def body(q_ref, k_ref, v_ref, c_ref, ct_ref, o_ref, lse_ref, do_ref, dq_ref, dk_ref, dv_ref, dct_ref):
        g, i = pl.program_id(0), pl.program_id(1)

        @pl.when(i == 0)
        def _():
            dk_ref[...] = jnp.zeros_like(dk_ref)
            dv_ref[...] = jnp.zeros_like(dv_ref)
            dct_ref[...] = jnp.zeros_like(dct_ref)

        q = _heads_of(q_ref)
        do16 = _each(lambda x: x.astype(BF16), _heads_of(do_ref))
        delta = _rowsum(_each(lambda a, b: a.astype(F32) * b, do16, _heads_of(o_ref)))
        lse_col = _Each(lse_ref[h, :, 0:1] for h in range(ATT_HEADS))
        cq = _fox_cq(c_ref, g)
        row, col, _, _ = _att_iotas()

        def step(jb, dq, diagonal):
            dq = _Each(dq)
            for sub in range(ATT_SUB):
                keys = pl.ds(pl.multiple_of(jb * ATT_TQ + sub * ATT_TK, ATT_TK), ATT_TK)
                k = _heads_of(k_ref, keys)
                sc = _fox_logits(q, k, cq, ct_ref, keys)
                if diagonal:
                    valid = (col + sub * ATT_TK) <= row
                    p = _keep(valid, _each(jnp.exp, _keep(valid, sc) - lse_col))
                else:
                    p = _each(jnp.exp, sc - lse_col)
                ds = p * (_dot_each(do16, _heads_of(v_ref, keys), "nt") - delta)
                dss = ds * ATT_SCALE
                dk, dv = _dot_each(dss, q, "tn"), _dot_each(p, do16, "tn")
                for h, cs in enumerate(_HEAD_COLS):
                    dct_ref[h, :, keys] -= jnp.sum(ds.vals[h], axis=0, keepdims=True)
                    dk_ref[keys, cs] += dk.vals[h]
                    dv_ref[keys, cs] += dv.vals[h]
                dq = dq + _dot_each(dss, k, "nn")
            return tuple(dq.vals)

        dq0 = step(i, tuple(jnp.zeros((ATT_TQ, HEAD_DIM), F32) for _ in range(ATT_HEADS)), True)
        dq = lax.fori_loop(0, i, lambda jb, dq: step(jb, dq, False), dq0)
        for cs, dq_h in zip(_HEAD_COLS, dq):
            dq_ref[:, cs] = dq_h

    q_spec, k_spec, v_spec = _att_specs(n_heads, s)
    blk = pl.BlockSpec((ATT_TQ, ATT_WIDTH), lambda g, i: (i, g))
    full = pl.BlockSpec((s, ATT_WIDTH), lambda g, i: (0, g))
    wide = jax.ShapeDtypeStruct((s, n_heads * HEAD_DIM), F32)
    return pl.pallas_call(
        body, name=name, grid=(n_heads // ATT_HEADS, s // ATT_TQ),
        in_specs=[q_spec, k_spec, v_spec, pl.BlockSpec((ATT_TQ, LANES), lambda g, i: (i, 0)),
                  pl.BlockSpec((ATT_HEADS, 1, s), lambda g, i: (g, 0, 0)), blk,
                  pl.BlockSpec((ATT_HEADS, ATT_TQ, LANES), lambda g, i: (g, i, 0)), blk],
        out_specs=[blk, full, full, pl.BlockSpec((ATT_HEADS, 1, s), lambda g, i: (g, 0, 0))],
        out_shape=[wide, wide, wide, jax.ShapeDtypeStruct((n_heads, 1, s), F32)],
        compiler_params=_cparams("parallel", "arbitrary"),
    )(qkv, qkv, qkv, c, ct, o, lse, do)


def _cumsum_rows(x, reverse, name):
    s = x.shape[0]
    nb = s // LANES

    def body(x_ref, o_ref):
        r = lax.broadcasted_iota(jnp.int32, (LANES, LANES), 0)
        c = lax.broadcasted_iota(jnp.int32, (LANES, LANES), 1)
        tri = ((r <= c) if reverse else (r >= c)).astype(F32)

        def step(it, carry):
            b = (nb - 1 - it) if reverse else it
            off = pl.multiple_of(b * LANES, LANES)
            blk = x_ref[pl.ds(off, LANES), :]
            o_ref[pl.ds(off, LANES), :] = _dot32(tri, blk) + carry
            return carry + jnp.sum(blk, axis=0, keepdims=True)

        lax.fori_loop(0, nb, step, jnp.zeros((1, LANES), F32))

    return pl.pallas_call(body, name=name, out_shape=jax.ShapeDtypeStruct(x.shape, F32),
                          compiler_params=pltpu.CompilerParams(vmem_limit_bytes=V7X_VMEM_LIMIT))(x)


def _dot32_each(a, b, kind="nn"):
    return _each(lambda x, y: _dot32(x, y, kind), a, b)


def _unit_lower_inverse(m, ri, ci):
    c = ri.shape[0]
    t = -_keep(ri // 2 == ci // 2, m) + jnp.where(ri == ci, 1.0, 0.0)
    b = 4
    while b <= c:
        off_diag = (ri // b == ci // b) & (ri % b >= b // 2) & (ci % b < b // 2)
        t = t - _dot32_each(_dot32_each(t, _keep(off_diag, m)), t)
        b *= 2
    return t


def _dn_gates(g, ri, ci):
    eye = ri == ci
    incl = ri >= ci
    g_row = jnp.sum(jnp.where(eye, g, 0.0), axis=0, keepdims=True)
    gc = jnp.sum(jnp.where(incl, g_row, 0.0), axis=1, keepdims=True)
    gc_row = jnp.sum(jnp.where(eye, gc, 0.0), axis=0, keepdims=True)
    dmat = jnp.where(incl, jnp.exp(jnp.where(incl, gc - gc_row, 0.0)), 0.0)
    gc_last = jnp.sum(g, axis=0, keepdims=True)
    return gc, dmat, jnp.exp(gc), jnp.exp(gc_last - gc), jnp.exp(gc_last)


def _dn_fwd(qkv, act, name):
    s = qkv.shape[0]
    c, d, nh = DN_CHUNK, HEAD_DIM, N_DN_HEADS
    nc = s // c

    def body(q_ref, k_ref, v_ref, act_ref, o_ref, s_ref, t_ref, state):
        @pl.when(pl.program_id(0) == 0)
        def _():
            state[...] = jnp.zeros_like(state)

        ri = lax.broadcasted_iota(jnp.int32, (c, c), 0)
        ci = lax.broadcasted_iota(jnp.int32, (c, c), 1)
        act = act_ref[...]
        heads = range(nh)
        cols = [slice(h * d, (h + 1) * d) for h in heads]
        q, k, v = (_Each(ref[:, cs] for cs in cols) for ref in (q_ref, k_ref, v_ref))
        beta = _Each(_lane_col(act, LANE_BETA + h) for h in heads)
        g = _Each(_lane_col(act, LANE_DECAY + h) for h in heads)
        _, dmat, e, r, gl = _each(lambda gh: _dn_gates(gh, ri, ci), g)
        s0 = _Each(state[h] for h in heads)
        kb = beta * k
        t = _unit_lower_inverse(_keep(ri > ci, _dot32_each(kb, k, "nt") * dmat), ri, ci)
        vn = _dot32_each(t, beta * v) - _dot32_each(_dot32_each(t, kb * e), s0)
        o = _dot32_each(q * e, s0) + _dot32_each(_dot32_each(q, k, "nt") * dmat, vn)
        s1 = s0 * gl + _dot32_each(k * r, vn, "tn")
        for h in heads:
            o_ref[:, cols[h]] = o.vals[h]
            state[h] = s1.vals[h]
            s_ref[h] = s0.vals[h]
            t_ref[h] = t.vals[h]

    wide = lambda part: pl.BlockSpec((c, nh * d), lambda n: (n, part))
    return pl.pallas_call(
        body, name=name, grid=(nc,),
        in_specs=[wide(0), wide(1), wide(2), pl.BlockSpec((c, LANES), lambda n: (n, 0))],
        out_specs=[wide(0), pl.BlockSpec((nh, None, d, d), lambda n: (0, n, 0, 0)),
                   pl.BlockSpec((nh, None, c, c), lambda n: (0, n, 0, 0))],
        out_shape=[jax.ShapeDtypeStruct((s, nh * d), F32), jax.ShapeDtypeStruct((nh, nc, d, d), F32),
                   jax.ShapeDtypeStruct((nh, nc, c, c), F32)],
        scratch_shapes=[pltpu.VMEM((nh, d, d), F32)],
        compiler_params=_cparams("arbitrary"),
    )(qkv, qkv, qkv, act)


def _dn_bwd(qkv, act, states, tinv, do, name):
    s = qkv.shape[0]
    c, d, nh = DN_CHUNK, HEAD_DIM, N_DN_HEADS
    nc = s // c

    def chunk_bwd(q, k, v, do, beta, g, s0, t, ds_out):
        ri = lax.broadcasted_iota(jnp.int32, (c, c), 0)
        ci = lax.broadcasted_iota(jnp.int32, (c, c), 1)
        eye, incl, strict = ri == ci, ri >= ci, ri > ci
        gc, dmat, e, r, gl = _each(lambda gh: _dn_gates(gh, ri, ci), g)
        dot = _dot32_each
        rowsum = lambda x: _each(lambda a: jnp.sum(a, axis=1, keepdims=True), x)
        colsum = lambda x: _each(lambda a: jnp.sum(a, axis=0, keepdims=True), x)
        total = lambda x: colsum(rowsum(x))
        to_col = lambda row: rowsum(_keep(eye, row))
        to_row = lambda colv: colsum(_keep(eye, colv))

        kb, vb = beta * k, beta * v
        kbe = kb * e
        u, w = dot(t, vb), dot(t, kbe)
        vn = u - dot(w, s0)
        qk = dot(q, k, "nt")
        p = qk * dmat
        gram = dot(k, k, "nt")
        kr, qe = k * r, q * e

        d_kr = dot(vn, ds_out, "nt")
        dvn = dot(kr, ds_out)
        dgl = total(s0 * ds_out)
        ds_in = ds_out * gl
        dk = d_kr * r
        dr = rowsum(d_kr * k)
        d_qe = dot(do, s0, "nt")
        ds_in = ds_in + dot(qe, do, "tn")
        dp = _keep(incl, dot(do, vn, "nt"))
        dvn = dvn + dot(p, do, "tn")
        dq = d_qe * e
        de = rowsum(d_qe * q)
        dqk = dp * dmat
        dq = dq + dot(dqk, k)
        dk = dk + dot(dqk, q, "tn")
        dd = dp * qk
        dw = -dot(dvn, s0, "nt")
        ds_in = ds_in - dot(w, dvn, "tn")
        dvb = dot(t, dvn, "tn")
        dkbe = dot(t, dw, "tn")
        dm = -_keep(strict, dot(dvb, u, "nt") + dot(dkbe, w, "nt"))
        dbeta = rowsum(dm * gram * dmat)
        dgram = dm * beta * dmat
        dd = dd + dm * beta * gram
        dk = dk + dot(dgram, k) + dot(dgram, k, "tn")
        dkb = dkbe * e
        de = de + rowsum(dkbe * kb)
        dk = dk + beta * dkb
        dbeta = dbeta + rowsum(dkb * k) + rowsum(dvb * v)
        dv = beta * dvb
        wd = dd * dmat
        dgc = rowsum(wd) - to_col(colsum(wd)) + de * e - dr * r
        dgc_last = total(dr * r) + dgl * gl
        dgc = dgc + _keep(ri[:, 0:1] == c - 1, dgc_last)
        dg = rowsum(_keep(ri <= ci, to_row(dgc)))
        return dq, dk, dv, dbeta, dg, ds_in

    def body(q_ref, k_ref, v_ref, act_ref, s_ref, t_ref, do_ref, dq_ref, dk_ref, dv_ref, dact_ref, dstate):
        @pl.when(pl.program_id(0) == 0)
        def _():
            dstate[...] = jnp.zeros_like(dstate)

        act = act_ref[...]
        heads = range(nh)
        cols = [slice(h * d, (h + 1) * d) for h in heads]
        q, k, v, do = (_Each(ref[:, cs] for cs in cols) for ref in (q_ref, k_ref, v_ref, do_ref))
        dq, dk, dv, dbeta, dg, ds_in = chunk_bwd(
            q, k, v, do, _Each(_lane_col(act, LANE_BETA + h) for h in heads),
            _Each(_lane_col(act, LANE_DECAY + h) for h in heads), _Each(s_ref[h] for h in heads),
            _Each(t_ref[h] for h in heads), _Each(dstate[h] for h in heads))
        lane = lax.broadcasted_iota(jnp.int32, (c, LANES), 1)
        dact = jnp.zeros((c, LANES), F32)
        for h in heads:
            dstate[h] = ds_in.vals[h]
            dq_ref[:, cols[h]], dk_ref[:, cols[h]], dv_ref[:, cols[h]] = dq.vals[h], dk.vals[h], dv.vals[h]
            dact = (dact + jnp.where(lane == LANE_BETA + h, dbeta.vals[h], 0.0)
                    + jnp.where(lane == LANE_DECAY + h, dg.vals[h], 0.0))
        dact_ref[...] = dact

    part = lambda p: pl.BlockSpec((c, nh * d), lambda n: (nc - 1 - n, p))
    per = lambda a, b: pl.BlockSpec((nh, None, a, b), lambda n: (0, nc - 1 - n, 0, 0))
    wide = jax.ShapeDtypeStruct((s, nh * d), F32)
    act_spec = pl.BlockSpec((c, LANES), lambda n: (nc - 1 - n, 0))
    return pl.pallas_call(
        body, name=name, grid=(nc,),
        in_specs=[part(0), part(1), part(2), act_spec, per(d, d), per(c, c), part(0)],
        out_specs=[part(0), part(0), part(0), act_spec],
        out_shape=[wide, wide, wide, jax.ShapeDtypeStruct((s, LANES), F32)],
        scratch_shapes=[pltpu.VMEM((nh, d, d), F32)],
        compiler_params=_cparams("arbitrary"),
    )(qkv, qkv, qkv, act, states, tinv, do)


EVEN_DN_QKV, EVEN_FOX_QKV, EVEN_DN_GATE, EVEN_FOX_GATE, EVEN_NARROW = 0, 1536, 3072, 3584, 4096
EVEN_WIDTH = 4224
CONV_TILE = 256
CONV_HALO = 8


def _conv_fwd(proj, w, name):
    s = proj.shape[0]
    t, cw = CONV_TILE, 3 * D_DN

    def body(cur_ref, prev_ref, w_ref, y_ref, xs):
        i = pl.program_id(0)
        xs[0:CONV_HALO, :] = jnp.where(i > 0, prev_ref[...], 0.0)
        xs[CONV_HALO:, :] = cur_ref[...]
        y = jnp.zeros((t, cw), F32)
        for tap in range(CONV_WIDTH):
            y = y + w_ref[tap:tap + 1, :] * xs[pl.ds(CONV_HALO - CONV_WIDTH + 1 + tap, t), :]
        y_ref[...] = y

    per = t // CONV_HALO
    return pl.pallas_call(
        body, name=name, grid=(s // t,),
        in_specs=[pl.BlockSpec((t, cw), lambda i: (i, 0)),
                  pl.BlockSpec((CONV_HALO, cw), lambda i: (jnp.maximum(i * per - 1, 0), 0)),
                  pl.BlockSpec((CONV_WIDTH, cw), lambda i: (0, 0))],
        out_specs=pl.BlockSpec((t, cw), lambda i: (i, 0)),
        out_shape=jax.ShapeDtypeStruct((s, cw), F32),
        scratch_shapes=[pltpu.VMEM((t + CONV_HALO, cw), F32)],
        compiler_params=_cparams("parallel"),
    )(proj, proj, w)


def _conv_bwd(proj, w, dy, name):
    s = proj.shape[0]
    t, cw = CONV_TILE, 3 * D_DN
    nt = s // t

    def body(cur_ref, prev_ref, w_ref, dy_ref, nxt_ref, dx_ref, dw_ref, xs, dys):
        i = pl.program_id(0)

        @pl.when(i == 0)
        def _():
            dw_ref[...] = jnp.zeros_like(dw_ref)

        xs[0:CONV_HALO, :] = jnp.where(i > 0, prev_ref[...], 0.0)
        xs[CONV_HALO:, :] = cur_ref[...]
        dys[0:t, :] = dy_ref[...]
        dys[t:, :] = jnp.where(i < nt - 1, nxt_ref[...], 0.0)
        dy = dy_ref[...]
        dx = jnp.zeros((t, cw), F32)
        for tap in range(CONV_WIDTH):
            dx = dx + w_ref[tap:tap + 1, :] * dys[pl.ds(CONV_WIDTH - 1 - tap, t), :]
            dw_ref[tap:tap + 1, :] += jnp.sum(dy * xs[pl.ds(CONV_HALO - CONV_WIDTH + 1 + tap, t), :], axis=0,
                                              keepdims=True)
        dx_ref[...] = dx.astype(BF16)

    per = t // CONV_HALO
    last = s // CONV_HALO - 1
    return pl.pallas_call(
        body, name=name, grid=(nt,),
        in_specs=[pl.BlockSpec((t, cw), lambda i: (i, 0)),
                  pl.BlockSpec((CONV_HALO, cw), lambda i: (jnp.maximum(i * per - 1, 0), 0)),
                  pl.BlockSpec((CONV_WIDTH, cw), lambda i: (0, 0)),
                  pl.BlockSpec((t, cw), lambda i: (i, 0)),
                  pl.BlockSpec((CONV_HALO, cw), lambda i: (jnp.minimum((i + 1) * per, last), 0))],
        out_specs=[pl.BlockSpec((t, cw), lambda i: (i, 0)), pl.BlockSpec((CONV_WIDTH, cw), lambda i: (0, 0))],
        out_shape=[jax.ShapeDtypeStruct((s, cw), BF16), jax.ShapeDtypeStruct((CONV_WIDTH, cw), F32)],
        scratch_shapes=[pltpu.VMEM((t + CONV_HALO, cw), F32), pltpu.VMEM((t + CONV_HALO, cw), F32)],
        compiler_params=_cparams("arbitrary"),
    )(proj, proj, w, dy, dy)


def _heads(x, n):
    return [x[:, HEAD_DIM * h:HEAD_DIM * (h + 1)] for h in range(n)]


def _dn_pre_fwd(y, name):
    def fn(yb):
        cs = yb * _sigmoid(yb)
        out = []
        for idx, xh in enumerate(_heads(cs, 3 * N_DN_HEADS)):
            if idx < 2 * N_DN_HEADS:
                xh = xh * lax.rsqrt(jnp.sum(xh * xh, axis=-1, keepdims=True) + EPS)
                if idx < N_DN_HEADS:
                    xh = xh * ATT_SCALE
            out.append(xh)
        return (jnp.concatenate(out, axis=1),)
    return _rowwise(fn, [y], [], [(y.shape[1], F32)], [], tile=256, name=name)[0]


def _dn_pre_bwd(y, dq, dk, dv, name):
    def fn(yb, dqb, dkb, dvb):
        sg = _sigmoid(yb)
        cs = yb * sg
        dout = _heads(dqb, N_DN_HEADS) + _heads(dkb, N_DN_HEADS) + _heads(dvb, N_DN_HEADS)
        dcs = []
        for idx, (xh, dh) in enumerate(zip(_heads(cs, 3 * N_DN_HEADS), dout)):
            if idx < 2 * N_DN_HEADS:
                if idx < N_DN_HEADS:
                    dh = dh * ATT_SCALE
                r = lax.rsqrt(jnp.sum(xh * xh, axis=-1, keepdims=True) + EPS)
                xhat = xh * r
                dh = r * (dh - xhat * jnp.sum(xhat * dh, axis=-1, keepdims=True))
            dcs.append(dh)
        return (jnp.concatenate(dcs, axis=1) * _silu_grad(yb, sg),)
    return _rowwise(fn, [y, dq, dk, dv], [], [(y.shape[1], F32)], [], tile=256, name=name)[0]


def _narrow_params(a_log, dt_bias, f_bias):
    lanes = lambda a, first: jnp.pad(a.reshape(1, -1), ((0, 0), (first, LANES - first - a.shape[0])))
    return jnp.concatenate([lanes(a_log, LANE_DECAY), lanes(dt_bias, LANE_DECAY), lanes(f_bias, LANE_FORGET),
                            jnp.zeros((5, LANES), F32)], axis=0)


def _narrow_masks(shape):
    lane = lax.broadcasted_iota(jnp.int32, shape, 1)
    is_beta = lane < LANE_DECAY
    is_decay = (lane >= LANE_DECAY) & (lane < LANE_FORGET)
    is_forget = (lane >= LANE_FORGET) & (lane < LANE_FORGET + N_FOX_HEADS)
    return is_beta, is_decay, is_forget


def _narrow_fwd(proj, params, name):
    def fn(sm, pk):
        is_beta, is_decay, is_forget = _narrow_masks(sm.shape)
        g = -jnp.exp(pk[0:1, :]) * _softplus(sm + pk[1:2, :])
        logf = -_softplus(-(sm + pk[2:3, :]))
        return (jnp.where(is_beta, _sigmoid(sm), jnp.where(is_decay, g, jnp.where(is_forget, logf, 0.0))),)
    return _rowwise(fn, [(proj, LANES, EVEN_NARROW // LANES)], [params], [(LANES, F32)], [], tile=512, name=name)[0]


def _narrow_bwd(proj, params, act, dact, dlogf, name):
    def fn(sm, ab, da, dl, pk):
        is_beta, is_decay, is_forget = _narrow_masks(sm.shape)
        db = jnp.where(is_forget, dl, da)
        d_beta = db * ab * (1.0 - ab)
        d_decay = db * (-jnp.exp(pk[0:1, :])) * _sigmoid(sm + pk[1:2, :])
        d_forget = db * _sigmoid(-(sm + pk[2:3, :]))
        dsm = jnp.where(is_beta, d_beta, jnp.where(is_decay, d_decay, jnp.where(is_forget, d_forget, 0.0)))
        col = lambda x: jnp.sum(x, axis=0, keepdims=True)
        return (dsm, col(jnp.where(is_decay, db * ab, 0.0)), col(jnp.where(is_decay, dsm, 0.0)),
                col(jnp.where(is_forget, dsm, 0.0)))
    return _rowwise(fn, [(proj, LANES, EVEN_NARROW // LANES), act, dact, dlogf], [params], [(LANES, BF16)],
                    [(1, LANES)] * 3, tile=512, name=name)


def _head_rms(xh):
    r = lax.rsqrt(jnp.mean(xh * xh, axis=-1, keepdims=True) + EPS)
    return xh * r, r


def _fox_pre_fwd(proj, qg, kg, name):
    def fn(pf, qgb, kgb):
        out = []
        for idx, xh in enumerate(_heads(pf, 3 * N_FOX_HEADS)):
            if idx < 2 * N_FOX_HEADS:
                xh = _head_rms(xh)[0] * (qgb if idx < N_FOX_HEADS else kgb)
            out.append(xh)
        return (jnp.concatenate(out, axis=1),)
    return _rowwise(fn, [(proj, 3 * D_FOX, EVEN_FOX_QKV // (3 * D_FOX))], [qg, kg], [(3 * D_FOX, BF16)], [],
                    tile=256, name=name)[0]


def _fox_pre_bwd(proj, qg, kg, dq, dk, dv, name):
    def fn(pf, dqb, dkb, dvb, qgb, kgb):
        dout = _heads(dqb, N_FOX_HEADS) + _heads(dkb, N_FOX_HEADS) + _heads(dvb, N_FOX_HEADS)
        dg = [jnp.zeros((1, HEAD_DIM), F32), jnp.zeros((1, HEAD_DIM), F32)]
        dx = []
        for idx, (xh, dh) in enumerate(zip(_heads(pf, 3 * N_FOX_HEADS), dout)):
            if idx < 2 * N_FOX_HEADS:
                which = 0 if idx < N_FOX_HEADS else 1
                xhat, r = _head_rms(xh)
                dg[which] = dg[which] + jnp.sum(dh * xhat, axis=0, keepdims=True)
                dxh = dh * (qgb if which == 0 else kgb)
                dh = r * (dxh - xhat * jnp.mean(dxh * xhat, axis=-1, keepdims=True))
            dx.append(dh)
        return jnp.concatenate(dx, axis=1), dg[0], dg[1]
    return _rowwise(fn, [(proj, 3 * D_FOX, EVEN_FOX_QKV // (3 * D_FOX)), dq, dk, dv], [qg, kg],
                    [(3 * D_FOX, BF16)], [(1, HEAD_DIM)] * 2, tile=256, name=name)


def _mix_gate_fwd(proj, o_dn, o_fox, ng, name):
    def fn(gd, gf, od, of, ngb):
        dn = [_head_rms(xh)[0] * ngb for xh in _heads(od, N_DN_HEADS)]
        return (jnp.concatenate([jnp.concatenate(dn, axis=1) * gd * _sigmoid(gd), of * _sigmoid(gf)], axis=1),)
    return _rowwise(fn, [(proj, D_DN, EVEN_DN_GATE // D_DN), (proj, D_FOX, EVEN_FOX_GATE // D_FOX), o_dn, o_fox],
                    [ng], [(D_DN + D_FOX, BF16)], [], tile=256, name=name)[0]


def _mix_gate_bwd(proj, o_dn, o_fox, ng, dom, name):
    def fn(gd, gf, od, of, dm, ngb):
        d_dn, d_fox = dm[:, :D_DN], dm[:, D_DN:]
        sgd, sgf = _sigmoid(gd), _sigmoid(gf)
        don = d_dn * gd * sgd
        dng = jnp.zeros((1, HEAD_DIM), F32)
        dod, normed = [], []
        for xh, dh in zip(_heads(od, N_DN_HEADS), _heads(don, N_DN_HEADS)):
            xhat, r = _head_rms(xh)
            dng = dng + jnp.sum(dh * xhat, axis=0, keepdims=True)
            dxh = dh * ngb
            dod.append(r * (dxh - xhat * jnp.mean(dxh * xhat, axis=-1, keepdims=True)))
            normed.append(xhat * ngb)
        d_gd = d_dn * jnp.concatenate(normed, axis=1) * _silu_grad(gd, sgd)
        d_gf = d_fox * of * sgf * (1.0 - sgf)
        return jnp.concatenate(dod, axis=1), d_fox * sgf, d_gd, d_gf, dng
    return _rowwise(fn, [(proj, D_DN, EVEN_DN_GATE // D_DN), (proj, D_FOX, EVEN_FOX_GATE // D_FOX), o_dn, o_fox, dom],
                    [ng], [(D_DN, F32), (D_FOX, F32), (D_DN, BF16), (D_FOX, BF16)], [(1, HEAD_DIM)], tile=256,
                    name=name)


def _loss_grad(y, target, name):
    d = y.shape[1]

    def fn(yb, tb):
        diff = yb - tb
        part = jnp.sum(jnp.sum(diff * diff, axis=1, keepdims=True), axis=0, keepdims=True) * (0.5 / d)
        g = diff * (1.0 / d)
        return g, g, part
    return _rowwise(fn, [y, target], [], [(d, F32), (d, BF16)], [(1, 1)], tile=512, name=name)


_REF_EVEN = {"dn_qkv": (0, 1536), "dn_gate": (1536, 2048), "dn_ba": (2048, 2056), "fox_qkv": (2056, 3592),
             "fox_gate": (3592, 4104), "f_pre": (4104, 4108)}
D_IN_EVEN = 4108


def _even_to_kernel_layout(w):
    cut = lambda name: w[..., _REF_EVEN[name][0]:_REF_EVEN[name][1]]
    pad = jnp.zeros(w.shape[:-1] + (EVEN_WIDTH - EVEN_NARROW - 12,), w.dtype)
    return jnp.concatenate([cut("dn_qkv"), cut("fox_qkv"), cut("dn_gate"), cut("fox_gate"), cut("dn_ba"),
                            cut("f_pre"), pad], axis=-1)


def _even_from_kernel_layout(g):
    return jnp.concatenate([g[..., EVEN_DN_QKV:EVEN_FOX_QKV], g[..., EVEN_DN_GATE:EVEN_FOX_GATE],
                            g[..., EVEN_NARROW:EVEN_NARROW + 8], g[..., EVEN_FOX_QKV:EVEN_DN_GATE],
                            g[..., EVEN_FOX_GATE:EVEN_NARROW], g[..., EVEN_NARROW + 8:EVEN_NARROW + 12]], axis=-1)


EVEN_QUARTER = 1027
EVEN_QUARTER_PAD = 1152


def _even_grad_quarters(g):
    g = _even_from_kernel_layout(g)
    pad = [(0, 0)] * (g.ndim - 1) + [(0, EVEN_QUARTER_PAD - EVEN_QUARTER)]
    return jnp.concatenate([jnp.pad(g[..., q * EVEN_QUARTER:(q + 1) * EVEN_QUARTER], pad) for q in range(4)], axis=-1)


def _forget_rows(c):
    return c[:, LANE_FORGET:LANE_FORGET + N_FOX_HEADS].T.reshape(N_FOX_HEADS, 1, c.shape[0])


def _forget_lanes(rows):
    s = rows.shape[2]
    return jnp.pad(rows.reshape(-1, s).T, ((0, 0), (LANE_FORGET, LANES - LANE_FORGET - N_FOX_HEADS)))


def _even_fwd(x, gain, w_in, w_out, j, p, tag):
    h = _rms_fwd(x, gain, f"{tag}_norm")
    proj = _mm(h, w_in, "nn", tm=512, tn=EVEN_WIDTH // 3, out_dtype=F32, name=f"{tag}_in", b_lead=(j,))
    y = _conv_fwd(proj, p["conv_w"], f"{tag}_conv")
    dn_qkv = _dn_pre_fwd(y, f"{tag}_dn_pre")
    act = _narrow_fwd(proj, p["narrow"], f"{tag}_narrow")
    o_dn, states, tinv = _dn_fwd(dn_qkv, act, f"{tag}_delta")
    fox_qkv = _fox_pre_fwd(proj, p["q_g"], p["k_g"], f"{tag}_fox_pre")
    c = _cumsum_rows(act, False, f"{tag}_cumsum")
    ct = _forget_rows(c)
    o_fox, lse = _fox_fwd(fox_qkv, c, ct, f"{tag}_fox")
    om = _mix_gate_fwd(proj, o_dn, o_fox, p["dn_norm_g"], f"{tag}_gate")
    x2 = _mm(om, w_out, "nn", tm=512, tn=x.shape[1], out_dtype=F32, name=f"{tag}_out", residual=x, b_lead=(j,))
    return x2, (x, h, proj, y, dn_qkv, act, states, tinv, o_dn, fox_qkv, c, ct, o_fox, lse, om)


def _even_bwd(dxo, dxo16, saved, gain, w_in, w_out, j, p, tag, g_in, g_out):
    x, h, proj, y, dn_qkv, act, states, tinv, o_dn, fox_qkv, c, ct, o_fox, lse, om = saved
    d = x.shape[1]
    dom = _mm(dxo16, w_out, "nt", tm=512, tn=d, out_dtype=F32, name=f"{tag}_out_bwd", b_lead=(j,))
    g_out = _mm(om, dxo16, "tn", tm=512, tn=d, out_dtype=F32, name=f"{tag}_out_dw", into=(g_out, 0))
    d_odn, d_ofox, d_gd, d_gf, d_ng = _mix_gate_bwd(proj, o_dn, o_fox, p["dn_norm_g"], dom, f"{tag}_gate_bwd")
    dq, dk, dv, dct = _fox_bwd(fox_qkv, c, ct, o_fox, lse, d_ofox, f"{tag}_fox_bwd")
    d_fox_qkv, d_qg, d_kg = _fox_pre_bwd(proj, p["q_g"], p["k_g"], dq, dk, dv, f"{tag}_fox_pre_bwd")
    dlogf = _cumsum_rows(_forget_lanes(dct), True, f"{tag}_cumsum_bwd")
    dq, dk, dv, dact = _dn_bwd(dn_qkv, act, states, tinv, d_odn, f"{tag}_delta_bwd")
    dy = _dn_pre_bwd(y, dq, dk, dv, f"{tag}_dn_pre_bwd")
    d_dn_qkv, d_conv = _conv_bwd(proj, p["conv_w"], dy, f"{tag}_conv_bwd")
    d_narrow, s_alog, s_dt, s_fb = _narrow_bwd(proj, p["narrow"], act, dact, dlogf, f"{tag}_narrow_bwd")
    dproj = jnp.concatenate([d_dn_qkv, d_fox_qkv, d_gd, d_gf, d_narrow], axis=1)
    dh = _mm(dproj, w_in, "nt", tm=512, tn=d, out_dtype=F32, name=f"{tag}_in_bwd", b_lead=(j,))
    g_in = _mm(h, dproj, "tn", tm=512, tn=EVEN_WIDTH // 3, out_dtype=F32, name=f"{tag}_in_dw", into=(g_in, 0))
    dx, dx16, d_gain = _rms_bwd(x, gain, dh, dxo, f"{tag}_norm_bwd")
    small = {"conv_w": d_conv, "a_log": s_alog, "dt_bias": s_dt, "f_bias": s_fb, "dn_norm_g": d_ng, "q_g": d_qg,
             "k_g": d_kg}
    return dx, dx16, d_gain, small, g_in, g_out


def _odd_fwd(x, gain, w_in, w_out, j, tag):
    h = _rms_fwd(x, gain, f"{tag}_norm")
    qkv = _mm(h, w_in, "nn", tm=512, tn=w_in.shape[2] // 2, out_dtype=BF16, name=f"{tag}_in", b_lead=(j,))
    o16, o32 = _sb_fwd(qkv, N_SB_HEADS, f"{tag}_sb")
    x2 = _mm(o16, w_out, "nn", tm=512, tn=x.shape[1], out_dtype=F32, name=f"{tag}_out", residual=x, b_lead=(j,))
    return x2, (x, h, qkv, o16, o32)


def _odd_bwd(dxo, dxo16, saved, gain, w_in, w_out, j, tag, g_in, g_out):
    x, h, qkv, o16, o32 = saved
    d = x.shape[1]
    do = _mm(dxo16, w_out, "nt", tm=512, tn=d, out_dtype=BF16, name=f"{tag}_out_bwd", b_lead=(j,))
    g_out = _mm(o16, dxo16, "tn", tm=512, tn=d, out_dtype=F32, name=f"{tag}_out_dw", into=(g_out, 0))
    dq, dk, dv = _sb_bwd(qkv, o32, do, N_SB_HEADS, f"{tag}_sb_bwd")
    dqkv = jnp.concatenate([dq, dk.astype(BF16), dv.astype(BF16)], axis=1)
    dh = _mm(dqkv, w_in, "nt", tm=512, tn=d, out_dtype=F32, name=f"{tag}_in_bwd", b_lead=(j,))
    g_in = _mm(h, dqkv, "tn", tm=512, tn=w_in.shape[2] // 2, out_dtype=F32, name=f"{tag}_in_dw", into=(g_in, 0))
    dx, dx16, d_gain = _rms_bwd(x, gain, dh, dxo, f"{tag}_norm_bwd")
    return dx, dx16, d_gain, g_in, g_out


def _forward_backward(x, target, w, first, rest_after, token):
    depth = w["norm_ffn1"].shape[0]
    row = lambda a, l: a[l][None]
    rest = {}

    def mats(names, j):
        if j == 0 and names[0] in first:
            return [first[name] for name in names] + [0]
        return [rest[name] for name in names] + [j - (1 if names[0] in first else 0)]

    def even_small(j):
        return {"conv_w": w["dn_conv_w"][j], "narrow": _narrow_params(w["dn_a_log"][j], w["dn_dt_bias"][j],
                                                                     w["fox_f_bias"][j]),
                "dn_norm_g": row(w["dn_norm_g"], j), "q_g": row(w["fox_q_norm_g"], j),
                "k_g": row(w["fox_k_norm_g"], j)}

    saved = []
    for l in range(depth):
        if l == 1:
            rest.update(rest_after(x))
        gain = row(w["norm_ffn1"], l) + token[0:1, 0:1] if l == 0 else row(w["norm_ffn1"], l)
        x, s1 = _ffn_fwd(x, gain, *mats(("ffn1_w_gu", "ffn1_w_down"), l), "ffn1")
        if l % 2 == 0:
            x, s2 = _even_fwd(x, row(w["norm_mix"], l), *mats(("w_in_even", "w_out_even"), l // 2),
                              even_small(l // 2), "even")
        else:
            x, s2 = _odd_fwd(x, row(w["norm_mix"], l), *mats(("w_in_odd", "w_out_odd"), l // 2), "odd")
        x, s3 = _ffn_fwd(x, row(w["norm_ffn2"], l), *mats(("ffn2_w_gu", "ffn2_w_down"), l), "ffn2")
        saved.append((s1, s2, s3))

    dx, dx16, loss = _loss_grad(x, target, "loss")

    kind_of = dict(BIG)
    d_norm = {k: [None] * depth for k in ("norm_ffn1", "norm_mix", "norm_ffn2")}
    d_even = [None] * ((depth + 1) // 2)
    reduced = {name: [] for name in kind_of}
    pending, token = None, None
    for l in reversed(range(depth)):
        s1, s2, s3 = saved[l]
        mixer = ("w_in_even", "w_out_even") if l % 2 == 0 else ("w_in_odd", "w_out_odd")
        names = ["ffn1_w_gu", "ffn1_w_down", *mixer, "ffn2_w_gu", "ffn2_w_down"]
        g = {name: lax.empty((1,) + rest[name].shape[1:], F32) for name in names}
        dx, dx16, d_norm["norm_ffn2"][l], g["ffn2_w_gu"], g["ffn2_w_down"] = _ffn_bwd(
            dx, dx16, s3, row(w["norm_ffn2"], l), *mats(("ffn2_w_gu", "ffn2_w_down"), l), "ffn2", g["ffn2_w_gu"],
            g["ffn2_w_down"], after=token)
        if l % 2 == 0:
            dx, dx16, d_norm["norm_mix"][l], d_even[l // 2], g["w_in_even"], g["w_out_even"] = _even_bwd(
                dx, dx16, s2, row(w["norm_mix"], l), *mats(("w_in_even", "w_out_even"), l // 2), even_small(l // 2),
                "even", g["w_in_even"], g["w_out_even"])
            g["w_in_even"] = _even_grad_quarters(g["w_in_even"])
        else:
            dx, dx16, d_norm["norm_mix"][l], g["w_in_odd"], g["w_out_odd"] = _odd_bwd(
                dx, dx16, s2, row(w["norm_mix"], l), *mats(("w_in_odd", "w_out_odd"), l // 2), "odd", g["w_in_odd"],
                g["w_out_odd"])
        dx, dx16, d_norm["norm_ffn1"][l], g["ffn1_w_gu"], g["ffn1_w_down"] = _ffn_bwd(
            dx, dx16, s1, row(w["norm_ffn1"], l), *mats(("ffn1_w_gu", "ffn1_w_down"), l), "ffn1", g["ffn1_w_gu"],
            g["ffn1_w_down"])
        if pending is not None:
            for name, quarter in zip(pending[-2], _reduce_finish(pending, dx)):
                reduced[name].insert(0, quarter)
        pending, token = _reduce_start([g[name] for name in names], [kind_of[name] for name in names], names,
                                       f"layer{l}")
    for name, quarter in zip(pending[-2], _reduce_finish(pending, dx)):
        reduced[name].insert(0, quarter)
    big = {name: jnp.concatenate(parts, axis=0) for name, parts in reduced.items()}

    small = {k: jnp.concatenate(v, axis=0) for k, v in d_norm.items()}
    dec = slice(LANE_DECAY, LANE_DECAY + N_DN_HEADS)
    fgt = slice(LANE_FORGET, LANE_FORGET + N_FOX_HEADS)
    small["dn_conv_w"] = jnp.stack([e["conv_w"] for e in d_even])
    small["dn_a_log"] = jnp.concatenate([e["a_log"][:, dec] for e in d_even], axis=0)
    small["dn_dt_bias"] = jnp.concatenate([e["dt_bias"][:, dec] for e in d_even], axis=0)
    small["fox_f_bias"] = jnp.concatenate([e["f_bias"][:, fgt] for e in d_even], axis=0)
    small["dn_norm_g"] = jnp.concatenate([e["dn_norm_g"] for e in d_even], axis=0)
    small["fox_q_norm_g"] = jnp.concatenate([e["q_g"] for e in d_even], axis=0)
    small["fox_k_norm_g"] = jnp.concatenate([e["k_g"] for e in d_even], axis=0)
    return loss, dx, big, small


MESH = pl.DeviceIdType.MESH
ANY = pl.BlockSpec(memory_space=pl.ANY)


def _place():
    x, y, c = lax.axis_index("x"), lax.axis_index("y"), lax.axis_index("c")
    return x, y, c, [(1 - x, y), (x, 1 - y), (1 - x, 1 - y)]


def _remote(src, dst, send_sem, recv_sem, to):
    return pltpu.make_async_remote_copy(src_ref=src, dst_ref=dst, send_sem=send_sem, recv_sem=recv_sem,
                                        device_id=to, device_id_type=MESH)


def _aligned(start, multiple):
    return start if isinstance(start, int) else pl.multiple_of(start, multiple)


def _quarter(ref, kind, chip, half, rows, cols):
    k = 2 * chip[0] + chip[1]
    hr = rows // 2
    assert hr % 16 == 0 and cols % LANES == 0
    if kind == "col":
        return ref.at[:, pl.ds(_aligned(half * hr, 16), hr), pl.ds(_aligned(k * cols, LANES), cols)]
    return ref.at[:, pl.ds(_aligned(k * rows + half * hr, 16), hr), :]


def _place_quarter(shard, kind, kc, name, first=0, count=None):
    l, rows, cols = shard.shape
    l = l - first if count is None else count
    tr = rows
    while tr * cols * 4 > (2 << 20) and tr % 32 == 0:
        tr //= 2
    nr = rows // tr
    if kind == "col":
        out_spec = pl.BlockSpec((None, tr, cols), lambda li, i, kc_ref: (li, i, kc_ref[0]))
        out_shape = (l, rows, 4 * cols)
    else:
        out_spec = pl.BlockSpec((None, tr, cols), lambda li, i, kc_ref: (li, kc_ref[0] * nr + i, 0))
        out_shape = (l, 4 * rows, cols)

    def body(kc_ref, x_ref, o_ref):
        o_ref[...] = x_ref[...].astype(BF16)

    return pl.pallas_call(
        body, name=name,
        grid_spec=pltpu.PrefetchScalarGridSpec(
            num_scalar_prefetch=1, grid=(l, nr),
            in_specs=[pl.BlockSpec((None, tr, cols), lambda li, i, kc_ref: (li + first, i, 0))],
            out_specs=out_spec),
        out_shape=jax.ShapeDtypeStruct(out_shape, BF16),
        compiler_params=_cparams("parallel", "parallel"),
    )(kc, shard)


def _gather_weights(wholes, kinds):
    n = len(wholes)

    def dims(ref, kind):
        _, r, cc = ref.shape
        return (r, cc // 4) if kind == "col" else (r // 4, cc)

    def body(*refs):
        bufs = refs[n:2 * n]
        send_sems, recv_sems = refs[2 * n:]
        x, y, c, chips = _place()
        sibling = (x, y, 1 - c)
        first, passed = [], []
        for t in range(n):
            rows, cols = dims(bufs[t], kinds[t])
            mine = _quarter(bufs[t], kinds[t], (x, y), c, rows, cols)
            for j, chip in enumerate(chips):
                cp = _remote(mine, mine, send_sems.at[t, j], recv_sems.at[t, j], (*chip, c))
                cp.start()
                first.append(cp)
        for j, chip in enumerate(chips):
            for t in range(n):
                rows, cols = dims(bufs[t], kinds[t])
                got = _quarter(bufs[t], kinds[t], chip, c, rows, cols)
                _remote(got, got, send_sems.at[t, j], recv_sems.at[t, j], (*chip, c)).wait_recv()
                cp = _remote(got, got, send_sems.at[t, 3 + j], recv_sems.at[t, 3 + j], sibling)
                cp.start()
                passed.append(cp)
        for j, chip in enumerate(chips):
            for t in range(n):
                rows, cols = dims(bufs[t], kinds[t])
                got = _quarter(bufs[t], kinds[t], chip, 1 - c, rows, cols)
                _remote(got, got, send_sems.at[t, 3 + j], recv_sems.at[t, 3 + j], sibling).wait_recv()
        for cp in first + passed:
            cp.wait_send()

    return pl.pallas_call(
        body, name="gather_weights", in_specs=[ANY] * n, out_specs=[ANY] * n,
        out_shape=[jax.ShapeDtypeStruct(a.shape, a.dtype) for a in wholes],
        input_output_aliases={t: t for t in range(n)},
        scratch_shapes=[pltpu.SemaphoreType.DMA((n, 6)), pltpu.SemaphoreType.DMA((n, 6))],
        compiler_params=pltpu.CompilerParams(has_side_effects=True),
    )(*wholes)


def _quarter_dims(ref, kind):
    _, r, cc = ref.shape
    return (r, cc // 4) if kind == "col" else (r // 4, cc)


def _gather_chips_copies(bufs, sems, kinds):
    x, y, c, chips = _place()
    copies = []
    for t, buf in enumerate(bufs):
        rows, cols = _quarter_dims(buf, kinds[t])
        mine = _quarter(buf, kinds[t], (x, y), c, rows, cols)
        for j, chip in enumerate(chips):
            pair = 2 * (OTHER_CHIPS * t + j)
            copies.append(_remote(mine, mine, sems[pair], sems[pair + 1], (*chip, c)))
    return copies


def _gather_start(wholes, kinds, tag):
    n = len(wholes)
    n_sems = 2 * OTHER_CHIPS * n

    def body(*refs):
        for cp in _gather_chips_copies(refs[:n], refs[2 * n:2 * n + n_sems], kinds):
            cp.start()
        refs[-1][...] = jnp.zeros_like(refs[-1])

    held = [pltpu.with_memory_space_constraint(a, pltpu.HBM) for a in wholes]
    out = pl.pallas_call(
        body, name=f"gather_start_{tag}", in_specs=[HBM] * n,
        out_specs=(*[HBM] * n, *[SEM] * n_sems, pl.BlockSpec(memory_space=pltpu.VMEM)),
        out_shape=(*[pltpu.HBM(a.shape, a.dtype) for a in held], *[pltpu.SemaphoreType.DMA(())] * n_sems,
                   jax.ShapeDtypeStruct((8, LANES), F32)),
        input_output_aliases={i: i for i in range(n)},
        compiler_params=pltpu.CompilerParams(has_side_effects=SPLIT_COPY),
    )(*held)
    return out[n:n + n_sems], out[:n], out[-1]


def _gather_wait(sems, wholes, kinds, after, tag):
    n = len(wholes)

    def body(*refs):
        for cp in _gather_chips_copies(refs[:n], refs[n:n + len(sems)], kinds):
            cp.wait_send()
            cp.wait_recv()

    return pl.pallas_call(
        body, name=f"gather_wait_{tag}", in_specs=[HBM] * n + [SEM] * len(sems) + [ANY],
        out_specs=tuple([HBM] * n), out_shape=tuple(pltpu.HBM(a.shape, a.dtype) for a in wholes),
        input_output_aliases={i: i for i in range(n)},
        compiler_params=pltpu.CompilerParams(has_side_effects=SPLIT_COPY),
    )(*wholes, *sems, after)


def _gather_forward(wholes, kinds, tag):
    n = len(wholes)

    def body(*refs):
        bufs = refs[n:2 * n]
        send_sems, recv_sems = refs[2 * n:]
        x, y, c, chips = _place()
        copies = []
        for t in range(n):
            rows, cols = _quarter_dims(bufs[t], kinds[t])
            for j, chip in enumerate(chips):
                got = _quarter(bufs[t], kinds[t], chip, c, rows, cols)
                cp = _remote(got, got, send_sems.at[t, j], recv_sems.at[t, j], (x, y, 1 - c))
                cp.start()
                copies.append(cp)
        for cp in copies:
            cp.wait_send()
        for t in range(n):
            rows, cols = _quarter_dims(bufs[t], kinds[t])
            for j, chip in enumerate(chips):
                got = _quarter(bufs[t], kinds[t], chip, 1 - c, rows, cols)
                _remote(got, got, send_sems.at[t, j], recv_sems.at[t, j], (x, y, 1 - c)).wait_recv()

    return pl.pallas_call(
        body, name=f"gather_forward_{tag}", in_specs=[ANY] * n, out_specs=[ANY] * n,
        out_shape=[jax.ShapeDtypeStruct(a.shape, a.dtype) for a in wholes],
        input_output_aliases={t: t for t in range(n)},
        scratch_shapes=[pltpu.SemaphoreType.DMA((n, OTHER_CHIPS)), pltpu.SemaphoreType.DMA((n, OTHER_CHIPS))],
        compiler_params=pltpu.CompilerParams(has_side_effects=True),
    )(*wholes)


def _canonical(a, kind):
    l, r, c = a.shape
    return a.reshape(l, 1, r, c) if kind == "col" else a.reshape(l, 4, r // 4, c)


def _rs_sibling(parts):
    n = len(parts)

    def body(*refs):
        ins, outs = refs[:n], refs[n:2 * n]
        send_sems, recv_sems = refs[2 * n:]
        x, y, c, _ = _place()
        copies = []
        for t in range(n):
            hr = ins[t].shape[2] // 2
            src = ins[t].at[:, :, pl.ds(pl.multiple_of((1 - c) * hr, 8), hr), :]
            cp = _remote(src, outs[t], send_sems.at[t], recv_sems.at[t], (x, y, 1 - c))
            cp.start()
            copies.append(cp)
        for cp in copies:
            cp.wait()

    half = lambda a: jax.ShapeDtypeStruct(a.shape[:2] + (a.shape[2] // 2, a.shape[3]), a.dtype)
    return pl.pallas_call(
        body, name="reduce_sibling", in_specs=[ANY] * n, out_specs=[ANY] * n, out_shape=[half(a) for a in parts],
        scratch_shapes=[pltpu.SemaphoreType.DMA((n,)), pltpu.SemaphoreType.DMA((n,))],
        compiler_params=pltpu.CompilerParams(has_side_effects=True),
    )(*parts)


def _add_tile(rows, cols):
    tc = cols if cols <= 1536 else cols // 4
    tr = rows
    while tr * tc * 4 > (1 << 20) and tr % 16 == 0:
        tr //= 2
    return tr, tc


def _rs_add_sibling(part, got, c, name):
    l, a, hr, cols = got.shape
    tr, tc = _add_tile(hr, cols)
    nr = hr // tr

    def body(c_ref, p_ref, g_ref, o32_ref, o16_ref):
        s = p_ref[...] + g_ref[...]
        o32_ref[...] = s
        o16_ref[...] = s.astype(BF16)

    blk = (None, None, tr, tc)
    spec = pl.BlockSpec(blk, lambda li, ai, i, j, c_ref: (li, ai, i, j))
    return pl.pallas_call(
        body, name=name,
        grid_spec=pltpu.PrefetchScalarGridSpec(
            num_scalar_prefetch=1, grid=(l, a, nr, cols // tc),
            in_specs=[pl.BlockSpec(blk, lambda li, ai, i, j, c_ref: (li, ai, c_ref[0] * nr + i, j)), spec],
            out_specs=[spec, spec]),
        out_shape=[jax.ShapeDtypeStruct(got.shape, F32), jax.ShapeDtypeStruct(got.shape, BF16)],
        compiler_params=_cparams("parallel", "parallel", "parallel", "parallel"),
    )(c, part, got)


def _quarter4(ref, kind, chip, cols):
    k = 2 * chip[0] + chip[1]
    if kind == "col":
        return ref.at[:, :, :, pl.ds(pl.multiple_of(k * cols, LANES), cols)]
    return ref.at[:, pl.ds(k, 1), :, :]


HBM = pl.BlockSpec(memory_space=pltpu.HBM)
SEM = pl.BlockSpec(memory_space=pltpu.SEMAPHORE)
SPLIT_COPY = pltpu.SideEffectType.DATAFLOW_SIDE_EFFECTING
OTHER_CHIPS = 3


def _quarter4_shape(a, kind):
    l, _, hr, cols = a.shape
    return (l, 1, hr, cols // 4 if kind == "col" else cols)


def _rs_chips_copies(srcs, lands, sems, kinds):
    x, y, c, chips = _place()
    copies = []
    for t, (src, land) in enumerate(zip(srcs, lands)):
        cols = _quarter4_shape(src, kinds[t])[3]
        for j, chip in enumerate(chips):
            pair = 2 * (OTHER_CHIPS * t + j)
            copies.append(_remote(_quarter4(src, kinds[t], chip, cols), land.at[j], sems[pair], sems[pair + 1],
                                  (*chip, c)))
    return copies


def _rs_chips_start(sums16, kinds, tag):
    n = len(sums16)
    n_sems = 2 * OTHER_CHIPS * n

    def body(*refs):
        srcs, lands, sems = refs[:n], refs[n:2 * n], refs[4 * n:4 * n + n_sems]
        for cp in _rs_chips_copies(srcs, lands, sems, kinds):
            cp.start()
        refs[-1][...] = jnp.zeros_like(refs[-1])

    lands = [lax.empty((OTHER_CHIPS,) + _quarter4_shape(a, k), a.dtype) for a, k in zip(sums16, kinds)]
    held = [pltpu.with_memory_space_constraint(a, pltpu.HBM) for a in (*sums16, *lands)]
    out = pl.pallas_call(
        body, name=f"reduce_chips_start_{tag}", in_specs=[HBM] * (2 * n),
        out_specs=(*[HBM] * (2 * n), *[SEM] * n_sems, pl.BlockSpec(memory_space=pltpu.VMEM)),
        out_shape=(*[pltpu.HBM(a.shape, a.dtype) for a in held], *[pltpu.SemaphoreType.DMA(())] * n_sems,
                   jax.ShapeDtypeStruct((8, LANES), F32)),
        input_output_aliases={i: i for i in range(2 * n)},
        compiler_params=pltpu.CompilerParams(has_side_effects=SPLIT_COPY),
    )(*held)
    return out[2 * n:2 * n + n_sems], out[:n], out[n:2 * n], out[-1]


def _rs_chips_wait(sems, srcs, lands, kinds, after, tag):
    n = len(srcs)

    def body(*refs):
        for cp in _rs_chips_copies(refs[:n], refs[n:2 * n], refs[2 * n:2 * n + len(sems)], kinds):
            cp.wait_send()
            cp.wait_recv()

    out = pl.pallas_call(
        body, name=f"reduce_chips_wait_{tag}", in_specs=[HBM] * (2 * n) + [SEM] * len(sems) + [ANY],
        out_specs=tuple([HBM] * (2 * n)),
        out_shape=tuple(pltpu.HBM(a.shape, a.dtype) for a in (*srcs, *lands)),
        input_output_aliases={i: i for i in range(2 * n)},
        compiler_params=pltpu.CompilerParams(has_side_effects=SPLIT_COPY),
    )(*srcs, *lands, *sems, after)
    return out[n:]


def _rs_add_chips(sum32, got, kind, kc, name):
    _, l, _, hr, cols = got.shape
    tr, _ = _add_tile(hr, cols)
    nr = hr // tr
    k_arr, c_arr = kc
    if kind == "col":
        own = pl.BlockSpec((None, None, tr, cols), lambda li, i, k_ref, c_ref: (li, 0, i, k_ref[0]))
    else:
        own = pl.BlockSpec((None, None, tr, cols), lambda li, i, k_ref, c_ref: (li, k_ref[0], i, 0))

    def body(k_ref, c_ref, own_ref, got_ref, o_ref):
        o_ref[...] = ((own_ref[...] + got_ref[0].astype(F32)) + got_ref[1].astype(F32)) + got_ref[2].astype(F32)

    return pl.pallas_call(
        body, name=name,
        grid_spec=pltpu.PrefetchScalarGridSpec(
            num_scalar_prefetch=2, grid=(l, nr),
            in_specs=[own, pl.BlockSpec((3, None, None, tr, cols), lambda li, i, k_ref, c_ref: (0, li, 0, i, 0))],
            out_specs=pl.BlockSpec((None, tr, cols), lambda li, i, k_ref, c_ref: (li, c_ref[0] * nr + i, 0))),
        out_shape=jax.ShapeDtypeStruct((l, 2 * hr, cols), F32),
        compiler_params=_cparams("parallel", "parallel"),
    )(k_arr, c_arr, sum32, got)


def _rs_finish(quarters):
    n = len(quarters)

    def body(*refs):
        bufs = refs[n:2 * n]
        send_sems, recv_sems = refs[2 * n:]
        x, y, c, _ = _place()
        copies = []
        for t in range(n):
            hr = bufs[t].shape[1] // 2
            mine = bufs[t].at[:, pl.ds(pl.multiple_of(c * hr, 8), hr), :]
            cp = _remote(mine, mine, send_sems.at[t], recv_sems.at[t], (x, y, 1 - c))
            cp.start()
            copies.append(cp)
        for cp in copies:
            cp.wait()

    return pl.pallas_call(
        body, name="reduce_finish", in_specs=[ANY] * n, out_specs=[ANY] * n,
        out_shape=[jax.ShapeDtypeStruct(a.shape, a.dtype) for a in quarters],
        input_output_aliases={t: t for t in range(n)},
        scratch_shapes=[pltpu.SemaphoreType.DMA((n,)), pltpu.SemaphoreType.DMA((n,))],
        compiler_params=pltpu.CompilerParams(has_side_effects=True),
    )(*quarters)


def _reduce_start(parts, kinds, names, tag):
    c_arr = jnp.reshape(lax.axis_index("c"), (1,)).astype(jnp.int32)
    canon = [_canonical(p, kind) for p, kind in zip(parts, kinds)]
    from_sibling = _rs_sibling(canon)
    sums = [_rs_add_sibling(p, g, c_arr, f"reduce_add_sibling_{nm}") for p, g, nm in zip(canon, from_sibling, names)]
    sems, srcs, lands, token = _rs_chips_start([s16 for _, s16 in sums], kinds, tag)
    return (sems, srcs, lands, [s32 for s32, _ in sums], kinds, names, tag), token


def _reduce_finish(state, after):
    sems, srcs, lands, sums32, kinds, names, tag = state
    x, y, c = lax.axis_index("x"), lax.axis_index("y"), lax.axis_index("c")
    kc = (jnp.reshape(2 * x + y, (1,)).astype(jnp.int32), jnp.reshape(c, (1,)).astype(jnp.int32))
    from_chips = _rs_chips_wait(sems, srcs, lands, kinds, after, tag)
    halves = [_rs_add_chips(s32, g, kind, kc, f"reduce_add_chips_{nm}")
              for s32, g, kind, nm in zip(sums32, from_chips, kinds, names)]
    return _rs_finish(halves)


SMALL_PEERS = 7


def _small_exchange(pack):
    rows = pack.shape[0]

    def body(p_ref, slots_ref, total_ref, send_sems, recv_sems):
        x, y, c, _ = _place()
        me = 4 * x + 2 * y + c
        slots_ref[me] = p_ref[...]
        copies = []
        for p in range(1, SMALL_PEERS + 1):
            px, py, pc = (p >> 2) & 1, (p >> 1) & 1, p & 1
            peer = (1 - x if px else x, 1 - y if py else y, 1 - c if pc else c)
            cp = _remote(p_ref, slots_ref.at[me], send_sems.at[p - 1], recv_sems.at[p - 1], peer)
            cp.start()
            copies.append(cp)
        for cp in copies:
            cp.wait()
        total = slots_ref[0]
        for i in range(1, SMALL_PEERS + 1):
            total = total + slots_ref[i]
        total_ref[...] = total

    vmem = pl.BlockSpec(memory_space=pltpu.VMEM)
    return pl.pallas_call(
        body, name="small_exchange", in_specs=[vmem], out_specs=[vmem, vmem],
        out_shape=[jax.ShapeDtypeStruct((SMALL_PEERS + 1, rows, LANES), F32), jax.ShapeDtypeStruct((rows, LANES), F32)],
        scratch_shapes=[pltpu.SemaphoreType.DMA((SMALL_PEERS,)), pltpu.SemaphoreType.DMA((SMALL_PEERS,))],
        compiler_params=pltpu.CompilerParams(has_side_effects=True),
    )(pack)


def _pack(arrays):
    rows = []
    for a in arrays:
        flat = a.reshape(-1).astype(F32)
        rows.append(jnp.pad(flat, (0, (-flat.shape[0]) % LANES)).reshape(-1, LANES))
    out = jnp.concatenate(rows, axis=0)
    return jnp.pad(out, ((0, (-out.shape[0]) % 8), (0, 0)))


def _unpack(pack, shapes):
    out, r = [], 0
    for sh in shapes:
        size = math.prod(sh)
        nr = -(-size // LANES)
        out.append(pack[r:r + nr].reshape(-1)[:size].reshape(sh))
        r += nr
    return out


def _adamw(w, g, m, v, name):
    shape = w.shape
    to2d = lambda a: a.reshape(-1, shape[-1])
    rows = math.prod(shape[:-1])
    tile = 256 if rows % 256 == 0 else rows

    def fn(wb, gb, mb, vb):
        m2 = ADAM_B1 * mb + (1.0 - ADAM_B1) * gb
        v2 = ADAM_B2 * vb + (1.0 - ADAM_B2) * (gb * gb)
        m_hat = m2 / (1.0 - ADAM_B1 ** ADAM_STEP)
        v_hat = v2 / (1.0 - ADAM_B2 ** ADAM_STEP)
        return -ADAM_LR * (m_hat / (jnp.sqrt(v_hat) + ADAM_EPS) + ADAM_WD * wb), m2, v2

    res = _rowwise(fn, [to2d(w), to2d(g), to2d(m), to2d(v)], [], [(shape[-1], F32)] * 3, [], tile=tile, name=name)
    return [r.reshape(shape) for r in res]


BIG = (("ffn1_w_gu", "col"), ("ffn1_w_down", "row"), ("w_in_even", "col"), ("w_out_even", "row"),
       ("w_in_odd", "col"), ("w_out_odd", "row"), ("ffn2_w_gu", "col"), ("ffn2_w_down", "row"))
SMALL = ("norm_ffn1", "norm_mix", "dn_conv_w", "dn_a_log", "dn_dt_bias", "dn_norm_g", "fox_q_norm_g", "fox_k_norm_g",
         "fox_f_bias", "norm_ffn2")
WEIGHTS = ("norm_ffn1", "ffn1_w_gu", "ffn1_w_down", "norm_mix", "w_in_even", "dn_conv_w", "dn_a_log", "dn_dt_bias",
           "dn_norm_g", "fox_q_norm_g", "fox_k_norm_g", "fox_f_bias", "w_out_even", "w_in_odd", "w_out_odd",
           "norm_ffn2", "ffn2_w_gu", "ffn2_w_down")


def _step(x, target, w, m, v):
    k = 2 * lax.axis_index("x") + lax.axis_index("y")
    n_conv = w["dn_conv_w"].shape[2]

    kc = jnp.reshape(k, (1,)).astype(jnp.int32)
    kinds = dict(BIG)
    quarters = {name: w[name] for name in kinds}
    quarters["w_in_even"] = jnp.pad(w["w_in_even"], ((0, 0), (0, 0), (0, EVEN_QUARTER_PAD - EVEN_QUARTER)))
    first_names = [name for name in kinds if name not in ("w_in_odd", "w_out_odd")]
    rest_names = list(kinds)

    def even_columns(whole):
        padded = whole["w_in_even"]
        ref_order = jnp.concatenate([padded[..., q * EVEN_QUARTER_PAD:q * EVEN_QUARTER_PAD + EVEN_QUARTER]
                                     for q in range(4)], axis=-1)
        return {**whole, "w_in_even": _even_to_kernel_layout(ref_order)}

    placed = [_place_quarter(quarters[name], kinds[name], kc, f"place_first_{name}", 0, 1) for name in first_names]
    first = even_columns(dict(zip(first_names, _gather_weights(placed, [kinds[name] for name in first_names]))))
    placed = [_place_quarter(quarters[name], kinds[name], kc, f"place_rest_{name}", 1 if name in first_names else 0)
              for name in rest_names]
    rest_kinds = [kinds[name] for name in rest_names]
    sems, on_their_way, token = _gather_start(placed, rest_kinds, "rest")

    def rest_after(value):
        landed = _gather_wait(sems, on_their_way, rest_kinds, value, "rest")
        return even_columns(dict(zip(rest_names, _gather_forward(landed, rest_kinds, "rest"))))

    whole = {}
    conv_slots, _ = _small_exchange(_pack([w["dn_conv_w"]]))
    conv_rows = math.prod(w["dn_conv_w"].shape) // LANES
    quarters = [conv_slots[2 * q, :conv_rows].reshape(w["dn_conv_w"].shape) for q in range(4)]
    whole["dn_conv_w"] = jnp.concatenate(quarters, axis=-1)
    for name in SMALL:
        if name != "dn_conv_w":
            whole[name] = w[name]

    loss, dx, reduced, small = _forward_backward(x, target, whole, first, rest_after, token)

    reduced["w_in_even"] = reduced["w_in_even"][..., :EVEN_QUARTER]
    _, small_sum = _small_exchange(_pack([small[n] for n in SMALL]))
    grads = dict(zip(SMALL, _unpack(small_sum, [small[n].shape for n in SMALL])))
    grads["dn_conv_w"] = lax.dynamic_slice_in_dim(grads["dn_conv_w"], k * n_conv, n_conv, axis=2)
    grads.update(reduced)

    delta, new_m, new_v = {}, {}, {}
    for name, _ in BIG:
        delta[name], new_m[name], new_v[name] = _adamw(w[name], grads[name], m[name], v[name], f"adamw_{name}")
    packs = [_pack([d[n] for n in SMALL]) for d in (w, grads, m, v)]
    shapes = [w[n].shape for n in SMALL]
    for out, res in zip((delta, new_m, new_v), _adamw(*packs, "adamw_small")):
        out.update(zip(SMALL, _unpack(res, shapes)))
    total_loss = lax.psum(loss[0, 0], ("x", "y", "c"))
    return total_loss, dx, grads, delta, new_m, new_v


def kernel(x, norm_ffn1, ffn1_w_gu, ffn1_w_down, norm_mix, w_in_even, dn_conv_w, dn_a_log, dn_dt_bias, dn_norm_g, fox_q_norm_g, fox_k_norm_g, fox_f_bias, w_out_even, w_in_odd, w_out_odd, norm_ffn2, ffn2_w_gu, ffn2_w_down, loss_target, m_norm_ffn1, m_ffn1_w_gu, m_ffn1_w_down, m_norm_mix, m_w_in_even, m_dn_conv_w, m_dn_a_log, m_dn_dt_bias, m_dn_norm_g, m_fox_q_norm_g, m_fox_k_norm_g, m_fox_f_bias, m_w_out_even, m_w_in_odd, m_w_out_odd, m_norm_ffn2, m_ffn2_w_gu, m_ffn2_w_down, v_norm_ffn1, v_ffn1_w_gu, v_ffn1_w_down, v_norm_mix, v_w_in_even, v_dn_conv_w, v_dn_a_log, v_dn_dt_bias, v_dn_norm_g, v_fox_q_norm_g, v_fox_k_norm_g, v_fox_f_bias, v_w_out_even, v_w_in_odd, v_w_out_odd, v_norm_ffn2, v_ffn2_w_gu, v_ffn2_w_down):
    w = dict(zip(WEIGHTS, (norm_ffn1, ffn1_w_gu, ffn1_w_down, norm_mix, w_in_even, dn_conv_w, dn_a_log, dn_dt_bias,
                           dn_norm_g, fox_q_norm_g, fox_k_norm_g, fox_f_bias, w_out_even, w_in_odd, w_out_odd,
                           norm_ffn2, ffn2_w_gu, ffn2_w_down)))
    m = dict(zip(WEIGHTS, (m_norm_ffn1, m_ffn1_w_gu, m_ffn1_w_down, m_norm_mix, m_w_in_even, m_dn_conv_w, m_dn_a_log,
                           m_dn_dt_bias, m_dn_norm_g, m_fox_q_norm_g, m_fox_k_norm_g, m_fox_f_bias, m_w_out_even,
                           m_w_in_odd, m_w_out_odd, m_norm_ffn2, m_ffn2_w_gu, m_ffn2_w_down)))
    v = dict(zip(WEIGHTS, (v_norm_ffn1, v_ffn1_w_gu, v_ffn1_w_down, v_norm_mix, v_w_in_even, v_dn_conv_w, v_dn_a_log,
                           v_dn_dt_bias, v_dn_norm_g, v_fox_q_norm_g, v_fox_k_norm_g, v_fox_f_bias, v_w_out_even,
                           v_w_in_odd, v_w_out_odd, v_norm_ffn2, v_ffn2_w_gu, v_ffn2_w_down)))
    loss, dx, grads, delta, new_m, new_v = _step(x[0], loss_target[0], w, m, v)
    return (loss, dx[None], *[grads[n] for n in WEIGHTS], *[delta[n] for n in WEIGHTS],
            *[new_m[n] for n in WEIGHTS], *[new_v[n] for n in WEIGHTS])
```

```python
import functools
import math

import jax
import jax.numpy as jnp
from jax import lax
from jax.experimental import pallas as pl
from jax.experimental.pallas import tpu as pltpu

F32 = jnp.float32
BF16 = jnp.bfloat16
HI = lax.Precision.HIGHEST

HEAD_DIM = 128
N_DN_HEADS = 4
N_FOX_HEADS = 4
N_SB_HEADS = 8
D_DN = N_DN_HEADS * HEAD_DIM
D_FOX = N_FOX_HEADS * HEAD_DIM
CONV_WIDTH = 4
DN_CHUNK = 64
EPS = 1e-6
ATT_SCALE = HEAD_DIM ** -0.5
ADAM_LR, ADAM_B1, ADAM_B2, ADAM_EPS, ADAM_WD, ADAM_STEP = 0.001, 0.9, 0.999, 1e-08, 0.01, 10

V7X_VMEM_LIMIT = 56 * 1024 * 1024
LANES = 128
ATT_TQ = 256
ATT_TK = 128
ATT_SUB = ATT_TQ // ATT_TK

LANE_BETA, LANE_DECAY, LANE_FORGET = 0, 4, 8


def _cparams(*sem):
    return pltpu.CompilerParams(dimension_semantics=sem, vmem_limit_bytes=V7X_VMEM_LIMIT)


def _sigmoid(x):
    return 1.0 / (1.0 + jnp.exp(-x))


def _softplus(x):
    return jnp.maximum(x, 0.0) + jnp.log(1.0 + jnp.exp(-jnp.abs(x)))


def _silu_grad(y, sg):
    return sg * (1.0 + y * (1.0 - sg))


def _rowwise(fn, rows, bcast, outs, sums, *, tile, name):
    rows = [r if isinstance(r, tuple) else (r, r.shape[1], 0) for r in rows]
    s = rows[0][0].shape[0]
    assert s % tile == 0
    n_in, n_b, n_out, n_sum = len(rows), len(bcast), len(outs), len(sums)

    def body(*refs):
        ins = [r[...] for r in refs[:n_in + n_b]]
        res = fn(*ins)
        if not isinstance(res, (tuple, list)):
            res = (res,)
        out_refs = refs[n_in + n_b:n_in + n_b + n_out]
        sum_refs = refs[n_in + n_b + n_out:]
        for o_ref, val in zip(out_refs, res[:n_out]):
            o_ref[...] = val.astype(o_ref.dtype)
        if n_sum:
            @pl.when(pl.program_id(0) == 0)
            def _():
                for s_ref in sum_refs:
                    s_ref[...] = jnp.zeros_like(s_ref)
            for s_ref, val in zip(sum_refs, res[n_out:]):
                s_ref[...] += val

    in_specs = [pl.BlockSpec((tile, w), lambda i, cb=cb: (i, cb)) for _, w, cb in rows]
    in_specs += [pl.BlockSpec(b.shape, lambda i, nd=b.ndim: (0,) * nd) for b in bcast]
    out_specs = [pl.BlockSpec((tile, c), lambda i: (i, 0)) for c, _ in outs]
    out_specs += [pl.BlockSpec(sh, lambda i: (0, 0)) for sh in sums]
    out_shape = [jax.ShapeDtypeStruct((s, c), dt) for c, dt in outs]
    out_shape += [jax.ShapeDtypeStruct(sh, F32) for sh in sums]
    return pl.pallas_call(
        body, name=name, grid=(s // tile,), in_specs=in_specs, out_specs=out_specs, out_shape=out_shape,
        compiler_params=_cparams("arbitrary" if n_sum else "parallel"),
    )(*[r[0] for r in rows], *bcast)


def _rms_fwd(x, gain, name):
    def fn(xb, g):
        r = lax.rsqrt(jnp.mean(xb * xb, axis=-1, keepdims=True) + EPS)
        return (xb * r * g,)
    return _rowwise(fn, [x], [gain], [(x.shape[1], BF16)], [], tile=512, name=name)[0]


def _rms_bwd(x, gain, dn, dres, name):
    def fn(xb, dnb, drb, g):
        r = lax.rsqrt(jnp.mean(xb * xb, axis=-1, keepdims=True) + EPS)
        xh = xb * r
        dxh = dnb * g
        dx = drb + r * (dxh - xh * jnp.mean(dxh * xh, axis=-1, keepdims=True))
        return dx, dx, jnp.sum(dnb * xh, axis=0, keepdims=True)
    d = x.shape[1]
    return _rowwise(fn, [x, dn, dres], [gain], [(d, F32), (d, BF16)], [(1, d)], tile=512, name=name)


_DIMS = {"nn": (((1,), (0,)), ((), ())), "nt": (((1,), (1,)), ((), ())), "tn": (((0,), (0,)), ((), ()))}


def _dot(a, b, kind):
    return lax.dot_general(a.astype(BF16), b.astype(BF16), _DIMS[kind], preferred_element_type=F32)


def _dot32(a, b, kind="nn"):
    return lax.dot_general(a, b, _DIMS[kind], precision=HI, preferred_element_type=F32)


def _mm(a, b, kind, *, tm, tn, out_dtype, name, scale=None, residual=None, a_lead=(), b_lead=(),
        b_spec=None, n=None, into=None):
    ash, bsh = a.shape[len(a_lead):], b.shape[len(b_lead):]
    m = ash[1] if kind == "tn" else ash[0]
    k = ash[0] if kind == "tn" else ash[1]
    if b_spec is None:
        n = bsh[0] if kind == "nt" else bsh[1]
        assert k == (bsh[1] if kind == "nt" else bsh[0]), (ash, bsh, kind)
    assert m % tm == 0 and n % tn == 0, (m, tm, n, tn)
    la, lb = (None,) * len(a_lead), (None,) * len(b_lead)
    if kind == "tn":
        a_spec = pl.BlockSpec(la + (k, tm), lambda j, i: a_lead + (0, i))
    else:
        a_spec = pl.BlockSpec(la + (tm, k), lambda j, i: a_lead + (i, 0))
    if b_spec is None:
        if kind == "nt":
            b_spec = pl.BlockSpec(lb + (tn, k), lambda j, i: b_lead + (j, 0))
        else:
            b_spec = pl.BlockSpec(lb + (k, tn), lambda j, i: b_lead + (0, j))
    in_specs, args = [a_spec, b_spec], [a, b]
    if residual is not None:
        in_specs.append(pl.BlockSpec((tm, tn), lambda j, i: (i, j)))
        args.append(residual)
    aliases = {}
    if into is not None:
        buf, layer = into
        in_specs.append(pl.BlockSpec(memory_space=pl.ANY))
        args.append(buf)
        aliases = {len(args) - 1: 0}
        out_spec = pl.BlockSpec((None, tm, tn), lambda j, i: (layer, i, j))
        out_shape = jax.ShapeDtypeStruct(buf.shape, buf.dtype)
    else:
        out_spec = pl.BlockSpec((tm, tn), lambda j, i: (i, j))
        out_shape = jax.ShapeDtypeStruct((m, n), out_dtype)

    def body(a_ref, b_ref, *rest):
        acc = _dot(a_ref[...], b_ref[...], kind)
        if scale is not None:
            acc = acc * scale
        if residual is not None:
            acc = acc + rest[0][...]
        rest[-1][...] = acc.astype(rest[-1].dtype)

    return pl.pallas_call(
        body, name=name, grid=(n // tn, m // tm), in_specs=in_specs, out_specs=out_spec, out_shape=out_shape,
        input_output_aliases=aliases, compiler_params=_cparams("parallel", "parallel"),
    )(*args)


def _ffn_up(n, w_gu, layer, name):
    s, d = n.shape
    f = w_gu.shape[2] // 2
    tm, tn = 512, f // 2
    nj = f // tn

    def body(n_ref, wg_ref, wu_ref, gu_ref, a_ref):
        nv = n_ref[...]
        g = _dot(nv, wg_ref[...], "nn")
        u = _dot(nv, wu_ref[...], "nn")
        gu_ref[0] = g.astype(BF16)
        gu_ref[1] = u.astype(BF16)
        a_ref[...] = (g * _sigmoid(g) * u).astype(BF16)

    return pl.pallas_call(
        body, name=name, grid=(nj, s // tm),
        in_specs=[pl.BlockSpec((tm, d), lambda j, i: (i, 0)),
                  pl.BlockSpec((None, d, tn), lambda j, i: (layer, 0, j)),
                  pl.BlockSpec((None, d, tn), lambda j, i: (layer, 0, j + nj))],
        out_specs=[pl.BlockSpec((2, tm, tn), lambda j, i: (0, i, j)),
                   pl.BlockSpec((tm, tn), lambda j, i: (i, j))],
        out_shape=[jax.ShapeDtypeStruct((2, s, f), BF16), jax.ShapeDtypeStruct((s, f), BF16)],
        compiler_params=_cparams("parallel", "parallel"),
    )(n, w_gu, w_gu)


def _ffn_down_bwd(dxo, w_down, gu, layer, name, after=None):
    s, d = dxo.shape
    f = w_down.shape[1]
    tm, tn = 512, f // 2
    extra_specs, extra = ([ANY], [after]) if after is not None else ([], [])

    def body(dx_ref, w_ref, gu_ref, *rest):
        dgu_ref = rest[-1]
        da = 0.5 * _dot(dx_ref[...], w_ref[...], "nt")
        g = gu_ref[0].astype(F32)
        u = gu_ref[1].astype(F32)
        sg = _sigmoid(g)
        dgu_ref[0] = (da * u * _silu_grad(g, sg)).astype(BF16)
        dgu_ref[1] = (da * g * sg).astype(BF16)

    return pl.pallas_call(
        body, name=name, grid=(f // tn, s // tm),
        in_specs=[pl.BlockSpec((tm, d), lambda j, i: (i, 0)),
                  pl.BlockSpec((None, tn, d), lambda j, i: (layer, j, 0)),
                  pl.BlockSpec((2, tm, tn), lambda j, i: (0, i, j))] + extra_specs,
        out_specs=pl.BlockSpec((2, tm, tn), lambda j, i: (0, i, j)),
        out_shape=jax.ShapeDtypeStruct((2, s, f), BF16),
        compiler_params=_cparams("parallel", "parallel"),
    )(dxo, w_down, gu, *extra)


def _ffn_dn(dgu, w_gu, layer, name):
    _, s, f = dgu.shape
    d = w_gu.shape[1]
    tm, tn = 512, d

    def body(dgu_ref, wg_ref, wu_ref, o_ref):
        o_ref[...] = _dot(dgu_ref[0], wg_ref[...], "nt") + _dot(dgu_ref[1], wu_ref[...], "nt")

    return pl.pallas_call(
        body, name=name, grid=(s // tm, d // tn),
        in_specs=[pl.BlockSpec((2, tm, f), lambda i, j: (0, i, 0)),
                  pl.BlockSpec((None, tn, f), lambda i, j: (layer, j, 0)),
                  pl.BlockSpec((None, tn, f), lambda i, j: (layer, j, 1))],
        out_specs=pl.BlockSpec((tm, tn), lambda i, j: (i, j)),
        out_shape=jax.ShapeDtypeStruct((s, d), F32),
        compiler_params=_cparams("parallel", "parallel"),
    )(dgu, w_gu, w_gu)


def _ffn_fwd(x, gain, w_gu, w_down, layer, tag):
    n = _rms_fwd(x, gain, f"{tag}_norm")
    gu, a = _ffn_up(n, w_gu, layer, f"{tag}_up")
    x2 = _mm(a, w_down, "nn", tm=512, tn=x.shape[1], out_dtype=F32, name=f"{tag}_down", scale=0.5, residual=x,
             b_lead=(layer,))
    return x2, (x, n, gu, a)


def _ffn_bwd(dxo, dxo16, saved, gain, w_gu, w_down, layer, tag, g_gu, g_down, after=None):
    x, n, gu, a = saved
    s, f = a.shape
    dgu = _ffn_down_bwd(dxo16, w_down, gu, layer, f"{tag}_down_bwd", after)
    g_down = _mm(a, dxo16, "tn", tm=256, tn=dxo16.shape[1], out_dtype=F32, name=f"{tag}_down_dw", scale=0.5,
                 into=(g_down, 0))
    dn = _ffn_dn(dgu, w_gu, layer, f"{tag}_up_bwd")
    tn = f // 2
    nj = f // tn
    g_gu = _mm(n, dgu, "tn", tm=512, tn=tn, out_dtype=F32, name=f"{tag}_up_dw", into=(g_gu, 0), n=2 * f,
               b_spec=pl.BlockSpec((None, s, tn), lambda j, i: (j // nj, 0, j % nj)))
    dx, dx16, dgain = _rms_bwd(x, gain, dn, dxo, f"{tag}_norm_bwd")
    return dx, dx16, dgain, g_gu, g_down


def _lane_col(blk, lane):
    li = lax.broadcasted_iota(jnp.int32, blk.shape, 1)
    return jnp.sum(jnp.where(li == lane, blk, 0.0), axis=1, keepdims=True)


def _split_dot(x, tri):
    hi = x.astype(BF16)
    lo = (x - hi.astype(F32)).astype(BF16)
    return (lax.dot_general(hi, tri, _DIMS["nn"], preferred_element_type=F32)
            + lax.dot_general(lo, tri, _DIMS["nn"], preferred_element_type=F32))


class _Each:
    def __init__(self, vals):
        self.vals = list(vals)

    def _with(self, other, op):
        others = other.vals if isinstance(other, _Each) else [other] * len(self.vals)
        return _Each(op(a, b) for a, b in zip(self.vals, others))

    def __add__(self, other):
        return self._with(other, lambda a, b: a + b)

    def __sub__(self, other):
        return self._with(other, lambda a, b: a - b)

    def __mul__(self, other):
        return self._with(other, lambda a, b: a * b)

    def __neg__(self):
        return _Each(-a for a in self.vals)


def _each(fn, *args):
    n = max(len(a.vals) for a in args if isinstance(a, _Each))
    res = [fn(*xs) for xs in zip(*[a.vals if isinstance(a, _Each) else [a] * n for a in args])]
    if isinstance(res[0], tuple):
        return tuple(_Each(r) for r in zip(*res))
    return _Each(res)


def _keep(cond, x):
    return _each(lambda v: jnp.where(cond, v, 0.0), x)


def _rowsum(x):
    return _each(lambda v: jnp.sum(v, axis=1, keepdims=True), x)


ATT_HEADS = 2
ATT_WIDTH = ATT_HEADS * HEAD_DIM
_HEAD_COLS = [slice(h * HEAD_DIM, (h + 1) * HEAD_DIM) for h in range(ATT_HEADS)]


def _att_specs(n_heads, s):
    groups = n_heads // ATT_HEADS
    q_spec = pl.BlockSpec((ATT_TQ, ATT_WIDTH), lambda g, i: (i, g))
    k_spec = pl.BlockSpec((s, ATT_WIDTH), lambda g, i: (0, groups + g))
    v_spec = pl.BlockSpec((s, ATT_WIDTH), lambda g, i: (0, 2 * groups + g))
    return q_spec, k_spec, v_spec


def _heads_of(ref, rows=None):
    return _Each(ref[:, cs] if rows is None else ref[rows, cs] for cs in _HEAD_COLS)


def _dot_each(a, b, kind):
    return _each(lambda x, y: _dot(x, y, kind), a, b)


def _att_iotas():
    row = lax.broadcasted_iota(jnp.int32, (ATT_TQ, ATT_TK), 0)
    col = lax.broadcasted_iota(jnp.int32, (ATT_TQ, ATT_TK), 1)
    jr = lax.broadcasted_iota(jnp.int32, (ATT_TK, ATT_TK), 0)
    jc = lax.broadcasted_iota(jnp.int32, (ATT_TK, ATT_TK), 1)
    return row, col, jr, jc


def _sb_fwd(qkv, n_heads, name):
    s = qkv.shape[0]

    def body(q_ref, k_ref, v_ref, o16_ref, o32_ref):
        i = pl.program_id(1)
        q = _heads_of(q_ref)
        row, col, jr, jc = _att_iotas()
        later = (jr > jc).astype(BF16)

        def step(jb, carry, diagonal):
            c_sp, acc = (_Each(part) for part in carry)
            work = []
            for sub in reversed(range(ATT_SUB)):
                keys = pl.ds(pl.multiple_of(jb * ATT_TQ + sub * ATT_TK, ATT_TK), ATT_TK)
                z = _dot_each(q, _heads_of(k_ref, keys), "nt") * ATT_SCALE
                sp = _each(_softplus, z)
                before = (col + sub * ATT_TK) < row if diagonal else None
                spm = _keep(before, sp) if diagonal else sp
                work.append((keys, z - sp, spm, _each(lambda x: _dot(x, later, "nn"), spm), before))
            for keys, logsig, spm, within, before in work:
                a = _each(jnp.exp, logsig - (c_sp + within))
                if diagonal:
                    a = _keep(before, a)
                acc = acc + _each(_split_dot, a, _heads_of(v_ref, keys))
                c_sp = c_sp + _rowsum(spm)
            return tuple(c_sp.vals), tuple(acc.vals)

        zeros = lambda width: tuple(jnp.zeros((ATT_TQ, width), F32) for _ in range(ATT_HEADS))
        carry = step(i, (zeros(1), zeros(HEAD_DIM)), True)
        _, acc = lax.fori_loop(0, i, lambda it, cr: step(i - 1 - it, cr, False), carry)
        for cs, acc_h in zip(_HEAD_COLS, acc):
            o16_ref[:, cs] = acc_h.astype(BF16)
            o32_ref[:, cs] = acc_h

    q_spec, k_spec, v_spec = _att_specs(n_heads, s)
    o_spec = pl.BlockSpec((ATT_TQ, ATT_WIDTH), lambda g, i: (i, g))
    return pl.pallas_call(
        body, name=name, grid=(n_heads // ATT_HEADS, s // ATT_TQ), in_specs=[q_spec, k_spec, v_spec],
        out_specs=[o_spec, o_spec],
        out_shape=[jax.ShapeDtypeStruct((s, n_heads * HEAD_DIM), BF16),
                   jax.ShapeDtypeStruct((s, n_heads * HEAD_DIM), F32)],
        compiler_params=_cparams("parallel", "arbitrary"),
    )(qkv, qkv, qkv)


def _sb_bwd(qkv, o32, do, n_heads, name):
    s = qkv.shape[0]

    def body(q_ref, k_ref, v_ref, o_ref, do_ref, dq_ref, dk_ref, dv_ref):
        i = pl.program_id(1)

        @pl.when(i == 0)
        def _():
            dk_ref[...] = jnp.zeros_like(dk_ref)
            dv_ref[...] = jnp.zeros_like(dv_ref)

        q, do = _heads_of(q_ref), _heads_of(do_ref)
        total = _rowsum(_each(lambda a, b: a.astype(F32) * b, do, _heads_of(o_ref)))
        row, col, jr, jc = _att_iotas()
        later = (jr > jc).astype(BF16)
        not_before = (jr >= jc).astype(BF16)

        def step(jb, carry, diagonal):
            c_sp, c_e, dq = (_Each(part) for part in carry)
            work = []
            for sub in reversed(range(ATT_SUB)):
                keys = pl.ds(pl.multiple_of(jb * ATT_TQ + sub * ATT_TK, ATT_TK), ATT_TK)
                k = _heads_of(k_ref, keys)
                z = _dot_each(q, k, "nt") * ATT_SCALE
                sp = _each(_softplus, z)
                before = (col + sub * ATT_TK) < row if diagonal else None
                spm = _keep(before, sp) if diagonal else sp
                work.append((keys, k, _each(jnp.exp, z - sp), spm, _each(lambda x: _dot(x, later, "nn"), spm),
                             _dot_each(do, _heads_of(v_ref, keys), "nt"), before))
            for keys, k, sig, spm, within, da, before in work:
                a = sig * _each(lambda x: jnp.exp(-x), c_sp + within)
                if diagonal:
                    a = _keep(before, a)
                e = a * da
                left = total - c_e - _each(lambda x: _split_dot(x, not_before), e)
                dz = (e - (e + left) * sig) * ATT_SCALE
                if diagonal:
                    dz = _keep(before, dz)
                dk, dv = _dot_each(dz, q, "tn"), _dot_each(a, do, "tn")
                for cs, dk_h, dv_h in zip(_HEAD_COLS, dk.vals, dv.vals):
                    dk_ref[keys, cs] += dk_h
                    dv_ref[keys, cs] += dv_h
                dq = dq + _dot_each(dz, k, "nn")
                c_sp = c_sp + _rowsum(spm)
                c_e = c_e + _rowsum(e)
            return tuple(c_sp.vals), tuple(c_e.vals), tuple(dq.vals)

        zeros = lambda width: tuple(jnp.zeros((ATT_TQ, width), F32) for _ in range(ATT_HEADS))
        carry = step(i, (zeros(1), zeros(1), zeros(HEAD_DIM)), True)
        _, _, dq = lax.fori_loop(0, i, lambda it, cr: step(i - 1 - it, cr, False), carry)
        for cs, dq_h in zip(_HEAD_COLS, dq):
            dq_ref[:, cs] = dq_h.astype(BF16)

    q_spec, k_spec, v_spec = _att_specs(n_heads, s)
    blk = pl.BlockSpec((ATT_TQ, ATT_WIDTH), lambda g, i: (i, g))
    full = pl.BlockSpec((s, ATT_WIDTH), lambda g, i: (0, g))
    wide = (s, n_heads * HEAD_DIM)
    return pl.pallas_call(
        body, name=name, grid=(n_heads // ATT_HEADS, s // ATT_TQ), in_specs=[q_spec, k_spec, v_spec, blk, blk],
        out_specs=[blk, full, full],
        out_shape=[jax.ShapeDtypeStruct(wide, BF16), jax.ShapeDtypeStruct(wide, F32), jax.ShapeDtypeStruct(wide, F32)],
        compiler_params=_cparams("parallel", "arbitrary"),
    )(qkv, qkv, qkv, o32, do)


def _fox_logits(q, k, cq, ct_ref, keys):
    ck = _Each(ct_ref[h, :, keys] for h in range(ATT_HEADS))
    return _dot_each(q, k, "nt") * ATT_SCALE + (cq - ck)


def _fox_cq(c_ref, group):
    c = c_ref[...]
    return _Each(_lane_col(c, LANE_FORGET + group * ATT_HEADS + h) for h in range(ATT_HEADS))


def _fox_fwd(qkv, c, ct, name):
    s = qkv.shape[0]
    n_heads = N_FOX_HEADS

    def body(q_ref, k_ref, v_ref, c_ref, ct_ref, o_ref, lse_ref):
        g, i = pl.program_id(0), pl.program_id(1)
        q = _heads_of(q_ref)
        cq = _fox_cq(c_ref, g)
        row, col, _, _ = _att_iotas()

        def step(jb, carry, diagonal):
            m, l, acc = (_Each(part) for part in carry)
            work = []
            m_new = m
            for sub in range(ATT_SUB):
                keys = pl.ds(pl.multiple_of(jb * ATT_TQ + sub * ATT_TK, ATT_TK), ATT_TK)
                sc = _fox_logits(q, _heads_of(k_ref, keys), cq, ct_ref, keys)
                valid = (col + sub * ATT_TK) <= row if diagonal else None
                if diagonal:
                    sc = _each(lambda x: jnp.where(valid, x, -1e30), sc)
                m_new = _each(lambda a, x: jnp.maximum(a, jnp.max(x, axis=1, keepdims=True)), m_new, sc)
                work.append((keys, sc, valid))
            w = _each(jnp.exp, m - m_new)
            l, acc = l * w, acc * w
            for keys, sc, valid in work:
                p = _each(jnp.exp, sc - m_new)
                if diagonal:
                    p = _keep(valid, p)
                l = l + _rowsum(p)
                acc = acc + _each(_split_dot, p, _heads_of(v_ref, keys))
            return tuple(m_new.vals), tuple(l.vals), tuple(acc.vals)

        per_head = lambda width, value: tuple(jnp.full((ATT_TQ, width), value, F32) for _ in range(ATT_HEADS))
        init = (per_head(1, -1e30), per_head(1, 0.0), per_head(HEAD_DIM, 0.0))
        m, l, acc = lax.fori_loop(0, i, lambda jb, cr: step(jb, cr, False), step(i, init, True))
        for h, cs in enumerate(_HEAD_COLS):
            o_ref[:, cs] = acc[h] / l[h]
            lse_ref[h] = jnp.broadcast_to(m[h] + jnp.log(l[h]), (ATT_TQ, LANES))

    q_spec, k_spec, v_spec = _att_specs(n_heads, s)
    return pl.pallas_call(
        body, name=name, grid=(n_heads // ATT_HEADS, s // ATT_TQ),
        in_specs=[q_spec, k_spec, v_spec, pl.BlockSpec((ATT_TQ, LANES), lambda g, i: (i, 0)),
                  pl.BlockSpec((ATT_HEADS, 1, s), lambda g, i: (g, 0, 0))],
        out_specs=[pl.BlockSpec((ATT_TQ, ATT_WIDTH), lambda g, i: (i, g)),
                   pl.BlockSpec((ATT_HEADS, ATT_TQ, LANES), lambda g, i: (g, i, 0))],
        out_shape=[jax.ShapeDtypeStruct((s, n_heads * HEAD_DIM), F32),
                   jax.ShapeDtypeStruct((n_heads, s, LANES), F32)],
        compiler_params=_cparams("parallel", "arbitrary"),
    )(qkv, qkv, qkv, c, ct)


def _fox_bwd(qkv, c, ct, o, lse, do, name):
    s = qkv.shape[0]
    n_heads = N_FOX_HEADS

    def body(q_ref, k_ref, v_ref, c_ref, ct_ref, o_ref, lse_ref, do_ref, dq_ref, dk_ref, dv_ref, dct_ref):
        g, i = pl.program_id(0), pl.program_id(1)

        @pl.when(i == 0)
        def _():
            dk_ref[...] = jnp.zeros_like(dk_ref)
            dv_ref[...] = jnp.zeros_like(dv_ref)
            dct_ref[...] = jnp.zeros_like(dct_ref)

        q = _heads_of(q_ref)
        do16 = _each(lambda x: x.astype(BF16), _heads_of(do_ref))
        delta = _rowsum(_each(lambda a, b: a.astype(F32) * b, do16, _heads_of(o_ref)))
        lse_col = _Each(lse_ref[h, :, 0:1] for h in range(ATT_HEADS))
        cq = _fox_cq(c_ref, g)
        row, col, _, _ = _att_iotas()

        def step(jb, dq, diagonal):
            dq = _Each(dq)
            for sub in range(ATT_SUB):
                keys = pl.ds(pl.multiple_of(jb * ATT_TQ + sub * ATT_TK, ATT_TK), ATT_TK)
                k = _heads_of(k_ref, keys)
                sc = _fox_logits(q, k, cq, ct_ref, keys)
                if diagonal:
                    valid = (col + sub * ATT_TK) <= row
                    p = _keep(valid, _each(jnp.exp, _keep(valid, sc) - lse_col))
                else:
                    p = _each(jnp.exp, sc - lse_col)
                ds = p * (_dot_each(do16, _heads_of(v_ref, keys), "nt") - delta)
                dss = ds * ATT_SCALE
                dk, dv = _dot_each(dss, q, "tn"), _dot_each(p, do16, "tn")
                for h, cs in enumerate(_HEAD_COLS):
                    dct_ref[h, :, keys] -= jnp.sum(ds.vals[h], axis=0, keepdims=True)
                    dk_ref[keys, cs] += dk.vals[h]
                    dv_ref[keys, cs] += dv.vals[h]
                dq = dq + _dot_each(dss, k, "nn")
            return tuple(dq.vals)

        dq0 = step(i, tuple(jnp.zeros((ATT_TQ, HEAD_DIM), F32) for _ in range(ATT_HEADS)), True)
        dq = lax.fori_loop(0, i, lambda jb, dq: step(jb, dq, False), dq0)
        for cs, dq_h in zip(_HEAD_COLS, dq):
            dq_ref[:, cs] = dq_h

    q_spec, k_spec, v_spec = _att_specs(n_heads, s)
    blk = pl.BlockSpec((ATT_TQ, ATT_WIDTH), lambda g, i: (i, g))
    full = pl.BlockSpec((s, ATT_WIDTH), lambda g, i: (0, g))
    wide = jax.ShapeDtypeStruct((s, n_heads * HEAD_DIM), F32)
    return pl.pallas_call(
        body, name=name, grid=(n_heads // ATT_HEADS, s // ATT_TQ),
        in_specs=[q_spec, k_spec, v_spec, pl.BlockSpec((ATT_TQ, LANES), lambda g, i: (i, 0)),
                  pl.BlockSpec((ATT_HEADS, 1, s), lambda g, i: (g, 0, 0)), blk,
                  pl.BlockSpec((ATT_HEADS, ATT_TQ, LANES), lambda g, i: (g, i, 0)), blk],
        out_specs=[blk, full, full, pl.BlockSpec((ATT_HEADS, 1, s), lambda g, i: (g, 0, 0))],
        out_shape=[wide, wide, wide, jax.ShapeDtypeStruct((n_heads, 1, s), F32)],
        compiler_params=_cparams("parallel", "arbitrary"),
    )(qkv, qkv, qkv, c, ct, o, lse, do)


def _cumsum_rows(x, reverse, name):
    s = x.shape[0]
    nb = s // LANES

    def body(x_ref, o_ref):
        r = lax.broadcasted_iota(jnp.int32, (LANES, LANES), 0)
        c = lax.broadcasted_iota(jnp.int32, (LANES, LANES), 1)
        tri = ((r <= c) if reverse else (r >= c)).astype(F32)

        def step(it, carry):
            b = (nb - 1 - it) if reverse else it
            off = pl.multiple_of(b * LANES, LANES)
            blk = x_ref[pl.ds(off, LANES), :]
            o_ref[pl.ds(off, LANES), :] = _dot32(tri, blk) + carry
            return carry + jnp.sum(blk, axis=0, keepdims=True)

        lax.fori_loop(0, nb, step, jnp.zeros((1, LANES), F32))

    return pl.pallas_call(body, name=name, out_shape=jax.ShapeDtypeStruct(x.shape, F32),
                          compiler_params=pltpu.CompilerParams(vmem_limit_bytes=V7X_VMEM_LIMIT))(x)


def _dot32_each(a, b, kind="nn"):
    return _each(lambda x, y: _dot32(x, y, kind), a, b)


def _unit_lower_inverse(m, ri, ci):
    c = ri.shape[0]
    t = -_keep(ri // 2 == ci // 2, m) + jnp.where(ri == ci, 1.0, 0.0)
    b = 4
    while b <= c:
        off_diag = (ri // b == ci // b) & (ri % b >= b // 2) & (ci % b < b // 2)
        t = t - _dot32_each(_dot32_each(t, _keep(off_diag, m)), t)
        b *= 2
    return t


def _dn_gates(g, ri, ci):
    eye = ri == ci
    incl = ri >= ci
    g_row = jnp.sum(jnp.where(eye, g, 0.0), axis=0, keepdims=True)
    gc = jnp.sum(jnp.where(incl, g_row, 0.0), axis=1, keepdims=True)
    gc_row = jnp.sum(jnp.where(eye, gc, 0.0), axis=0, keepdims=True)
    dmat = jnp.where(incl, jnp.exp(jnp.where(incl, gc - gc_row, 0.0)), 0.0)
    gc_last = jnp.sum(g, axis=0, keepdims=True)
    return gc, dmat, jnp.exp(gc), jnp.exp(gc_last - gc), jnp.exp(gc_last)


def _dn_fwd(qkv, act, name):
    s = qkv.shape[0]
    c, d, nh = DN_CHUNK, HEAD_DIM, N_DN_HEADS
    nc = s // c

    def body(q_ref, k_ref, v_ref, act_ref, o_ref, s_ref, t_ref, state):
        @pl.when(pl.program_id(0) == 0)
        def _():
            state[...] = jnp.zeros_like(state)

        ri = lax.broadcasted_iota(jnp.int32, (c, c), 0)
        ci = lax.broadcasted_iota(jnp.int32, (c, c), 1)
        act = act_ref[...]
        heads = range(nh)
        cols = [slice(h * d, (h + 1) * d) for h in heads]
        q, k, v = (_Each(ref[:, cs] for cs in cols) for ref in (q_ref, k_ref, v_ref))
        beta = _Each(_lane_col(act, LANE_BETA + h) for h in heads)
        g = _Each(_lane_col(act, LANE_DECAY + h) for h in heads)
        _, dmat, e, r, gl = _each(lambda gh: _dn_gates(gh, ri, ci), g)
        s0 = _Each(state[h] for h in heads)
        kb = beta * k
        t = _unit_lower_inverse(_keep(ri > ci, _dot32_each(kb, k, "nt") * dmat), ri, ci)
        vn = _dot32_each(t, beta * v) - _dot32_each(_dot32_each(t, kb * e), s0)
        o = _dot32_each(q * e, s0) + _dot32_each(_dot32_each(q, k, "nt") * dmat, vn)
        s1 = s0 * gl + _dot32_each(k * r, vn, "tn")
        for h in heads:
            o_ref[:, cols[h]] = o.vals[h]
            state[h] = s1.vals[h]
            s_ref[h] = s0.vals[h]
            t_ref[h] = t.vals[h]

    wide = lambda part: pl.BlockSpec((c, nh * d), lambda n: (n, part))
    return pl.pallas_call(
        body, name=name, grid=(nc,),
        in_specs=[wide(0), wide(1), wide(2), pl.BlockSpec((c, LANES), lambda n: (n, 0))],
        out_specs=[wide(0), pl.BlockSpec((nh, None, d, d), lambda n: (0, n, 0, 0)),
                   pl.BlockSpec((nh, None, c, c), lambda n: (0, n, 0, 0))],
        out_shape=[jax.ShapeDtypeStruct((s, nh * d), F32), jax.ShapeDtypeStruct((nh, nc, d, d), F32),
                   jax.ShapeDtypeStruct((nh, nc, c, c), F32)],
        scratch_shapes=[pltpu.VMEM((nh, d, d), F32)],
        compiler_params=_cparams("arbitrary"),
    )(qkv, qkv, qkv, act)


def _dn_bwd(qkv, act, states, tinv, do, name):
    s = qkv.shape[0]
    c, d, nh = DN_CHUNK, HEAD_DIM, N_DN_HEADS
    nc = s // c

    def chunk_bwd(q, k, v, do, beta, g, s0, t, ds_out):
        ri = lax.broadcasted_iota(jnp.int32, (c, c), 0)
        ci = lax.broadcasted_iota(jnp.int32, (c, c), 1)
        eye, incl, strict = ri == ci, ri >= ci, ri > ci
        gc, dmat, e, r, gl = _each(lambda gh: _dn_gates(gh, ri, ci), g)
        dot = _dot32_each
        rowsum = lambda x: _each(lambda a: jnp.sum(a, axis=1, keepdims=True), x)
        colsum = lambda x: _each(lambda a: jnp.sum(a, axis=0, keepdims=True), x)
        total = lambda x: colsum(rowsum(x))
        to_col = lambda row: rowsum(_keep(eye, row))
        to_row = lambda colv: colsum(_keep(eye, colv))

        kb, vb = beta * k, beta * v
        kbe = kb * e
        u, w = dot(t, vb), dot(t, kbe)
        vn = u - dot(w, s0)
        qk = dot(q, k, "nt")
        p = qk * dmat
        gram = dot(k, k, "nt")
        kr, qe = k * r, q * e

        d_kr = dot(vn, ds_out, "nt")
        dvn = dot(kr, ds_out)
        dgl = total(s0 * ds_out)
        ds_in = ds_out * gl
        dk = d_kr * r
        dr = rowsum(d_kr * k)
        d_qe = dot(do, s0, "nt")
        ds_in = ds_in + dot(qe, do, "tn")
        dp = _keep(incl, dot(do, vn, "nt"))
        dvn = dvn + dot(p, do, "tn")
        dq = d_qe * e
        de = rowsum(d_qe * q)
        dqk = dp * dmat
        dq = dq + dot(dqk, k)
        dk = dk + dot(dqk, q, "tn")
        dd = dp * qk
        dw = -dot(dvn, s0, "nt")
        ds_in = ds_in - dot(w, dvn, "tn")
        dvb = dot(t, dvn, "tn")
        dkbe = dot(t, dw, "tn")
        dm = -_keep(strict, dot(dvb, u, "nt") + dot(dkbe, w, "nt"))
        dbeta = rowsum(dm * gram * dmat)
        dgram = dm * beta * dmat
        dd = dd + dm * beta * gram
        dk = dk + dot(dgram, k) + dot(dgram, k, "tn")
        dkb = dkbe * e
        de = de + rowsum(dkbe * kb)
        dk = dk + beta * dkb
        dbeta = dbeta + rowsum(dkb * k) + rowsum(dvb * v)
        dv = beta * dvb
        wd = dd * dmat
        dgc = rowsum(wd) - to_col(colsum(wd)) + de * e - dr * r
        dgc_last = total(dr * r) + dgl * gl
        dgc = dgc + _keep(ri[:, 0:1] == c - 1, dgc_last)
        dg = rowsum(_keep(ri <= ci, to_row(dgc)))
        return dq, dk, dv, dbeta, dg, ds_in

    def body(q_ref, k_ref, v_ref, act_ref, s_ref, t_ref, do_ref, dq_ref, dk_ref, dv_ref, dact_ref, dstate):
        @pl.when(pl.program_id(0) == 0)
        def _():
            dstate[...] = jnp.zeros_like(dstate)

        act = act_ref[...]
        heads = range(nh)
        cols = [slice(h * d, (h + 1) * d) for h in heads]
        q, k, v, do = (_Each(ref[:, cs] for cs in cols) for ref in (q_ref, k_ref, v_ref, do_ref))
        dq, dk, dv, dbeta, dg, ds_in = chunk_bwd(
            q, k, v, do, _Each(_lane_col(act, LANE_BETA + h) for h in heads),
            _Each(_lane_col(act, LANE_DECAY + h) for h in heads), _Each(s_ref[h] for h in heads),
            _Each(t_ref[h] for h in heads), _Each(dstate[h] for h in heads))
        lane = lax.broadcasted_iota(jnp.int32, (c, LANES), 1)
        dact = jnp.zeros((c, LANES), F32)
        for h in heads:
            dstate[h] = ds_in.vals[h]
            dq_ref[:, cols[h]], dk_ref[:, cols[h]], dv_ref[:, cols[h]] = dq.vals[h], dk.vals[h], dv.vals[h]
            dact = (dact + jnp.where(lane == LANE_BETA + h, dbeta.vals[h], 0.0)
                    + jnp.where(lane == LANE_DECAY + h, dg.vals[h], 0.0))
        dact_ref[...] = dact

    part = lambda p: pl.BlockSpec((c, nh * d), lambda n: (nc - 1 - n, p))
    per = lambda a, b: pl.BlockSpec((nh, None, a, b), lambda n: (0, nc - 1 - n, 0, 0))
    wide = jax.ShapeDtypeStruct((s, nh * d), F32)
    act_spec = pl.BlockSpec((c, LANES), lambda n: (nc - 1 - n, 0))
    return pl.pallas_call(
        body, name=name, grid=(nc,),
        in_specs=[part(0), part(1), part(2), act_spec, per(d, d), per(c, c), part(0)],
        out_specs=[part(0), part(0), part(0), act_spec],
        out_shape=[wide, wide, wide, jax.ShapeDtypeStruct((s, LANES), F32)],
        scratch_shapes=[pltpu.VMEM((nh, d, d), F32)],
        compiler_params=_cparams("arbitrary"),
    )(qkv, qkv, qkv, act, states, tinv, do)


EVEN_DN_QKV, EVEN_FOX_QKV, EVEN_DN_GATE, EVEN_FOX_GATE, EVEN_NARROW = 0, 1536, 3072, 3584, 4096
EVEN_WIDTH = 4224
CONV_TILE = 256
CONV_HALO = 8


def _conv_fwd(proj, w, name):
    s = proj.shape[0]
    t, cw = CONV_TILE, 3 * D_DN

    def body(cur_ref, prev_ref, w_ref, y_ref, xs):
        i = pl.program_id(0)
        xs[0:CONV_HALO, :] = jnp.where(i > 0, prev_ref[...], 0.0)
        xs[CONV_HALO:, :] = cur_ref[...]
        y = jnp.zeros((t, cw), F32)
        for tap in range(CONV_WIDTH):
            y = y + w_ref[tap:tap + 1, :] * xs[pl.ds(CONV_HALO - CONV_WIDTH + 1 + tap, t), :]
        y_ref[...] = y

    per = t // CONV_HALO
    return pl.pallas_call(
        body, name=name, grid=(s // t,),
        in_specs=[pl.BlockSpec((t, cw), lambda i: (i, 0)),
                  pl.BlockSpec((CONV_HALO, cw), lambda i: (jnp.maximum(i * per - 1, 0), 0)),
                  pl.BlockSpec((CONV_WIDTH, cw), lambda i: (0, 0))],
        out_specs=pl.BlockSpec((t, cw), lambda i: (i, 0)),
        out_shape=jax.ShapeDtypeStruct((s, cw), F32),
        scratch_shapes=[pltpu.VMEM((t + CONV_HALO, cw), F32)],
        compiler_params=_cparams("parallel"),
    )(proj, proj, w)


def _conv_bwd(proj, w, dy, name):
    s = proj.shape[0]
    t, cw = CONV_TILE, 3 * D_DN
    nt = s // t

    def body(cur_ref, prev_ref, w_ref, dy_ref, nxt_ref, dx_ref, dw_ref, xs, dys):
        i = pl.program_id(0)

        @pl.when(i == 0)
        def _():
            dw_ref[...] = jnp.zeros_like(dw_ref)

        xs[0:CONV_HALO, :] = jnp.where(i > 0, prev_ref[...], 0.0)
        xs[CONV_HALO:, :] = cur_ref[...]
        dys[0:t, :] = dy_ref[...]
        dys[t:, :] = jnp.where(i < nt - 1, nxt_ref[...], 0.0)
        dy = dy_ref[...]
        dx = jnp.zeros((t, cw), F32)
        for tap in range(CONV_WIDTH):
            dx = dx + w_ref[tap:tap + 1, :] * dys[pl.ds(CONV_WIDTH - 1 - tap, t), :]
            dw_ref[tap:tap + 1, :] += jnp.sum(dy * xs[pl.ds(CONV_HALO - CONV_WIDTH + 1 + tap, t), :], axis=0,
                                              keepdims=True)
        dx_ref[...] = dx.astype(BF16)

    per = t // CONV_HALO
    last = s // CONV_HALO - 1
    return pl.pallas_call(
        body, name=name, grid=(nt,),
        in_specs=[pl.BlockSpec((t, cw), lambda i: (i, 0)),
                  pl.BlockSpec((CONV_HALO, cw), lambda i: (jnp.maximum(i * per - 1, 0), 0)),
                  pl.BlockSpec((CONV_WIDTH, cw), lambda i: (0, 0)),
                  pl.BlockSpec((t, cw), lambda i: (i, 0)),
                  pl.BlockSpec((CONV_HALO, cw), lambda i: (jnp.minimum((i + 1) * per, last), 0))],
        out_specs=[pl.BlockSpec((t, cw), lambda i: (i, 0)), pl.BlockSpec((CONV_WIDTH, cw), lambda i: (0, 0))],
        out_shape=[jax.ShapeDtypeStruct((s, cw), BF16), jax.ShapeDtypeStruct((CONV_WIDTH, cw), F32)],
        scratch_shapes=[pltpu.VMEM((t + CONV_HALO, cw), F32), pltpu.VMEM((t + CONV_HALO, cw), F32)],
        compiler_params=_cparams("arbitrary"),
    )(proj, proj, w, dy, dy)


def _heads(x, n):
    return [x[:, HEAD_DIM * h:HEAD_DIM * (h + 1)] for h in range(n)]


def _dn_pre_fwd(y, name):
    def fn(yb):
        cs = yb * _sigmoid(yb)
        out = []
        for idx, xh in enumerate(_heads(cs, 3 * N_DN_HEADS)):
            if idx < 2 * N_DN_HEADS:
                xh = xh * lax.rsqrt(jnp.sum(xh * xh, axis=-1, keepdims=True) + EPS)
                if idx < N_DN_HEADS:
                    xh = xh * ATT_SCALE
            out.append(xh)
        return (jnp.concatenate(out, axis=1),)
    return _rowwise(fn, [y], [], [(y.shape[1], F32)], [], tile=256, name=name)[0]


def _dn_pre_bwd(y, dq, dk, dv, name):
    def fn(yb, dqb, dkb, dvb):
        sg = _sigmoid(yb)
        cs = yb * sg
        dout = _heads(dqb, N_DN_HEADS) + _heads(dkb, N_DN_HEADS) + _heads(dvb, N_DN_HEADS)
        dcs = []
        for idx, (xh, dh) in enumerate(zip(_heads(cs, 3 * N_DN_HEADS), dout)):
            if idx < 2 * N_DN_HEADS:
                if idx < N_DN_HEADS:
                    dh = dh * ATT_SCALE
                r = lax.rsqrt(jnp.sum(xh * xh, axis=-1, keepdims=True) + EPS)
                xhat = xh * r
                dh = r * (dh - xhat * jnp.sum(xhat * dh, axis=-1, keepdims=True))
            dcs.append(dh)
        return (jnp.concatenate(dcs, axis=1) * _silu_grad(yb, sg),)
    return _rowwise(fn, [y, dq, dk, dv], [], [(y.shape[1], F32)], [], tile=256, name=name)[0]


def _narrow_params(a_log, dt_bias, f_bias):
    lanes = lambda a, first: jnp.pad(a.reshape(1, -1), ((0, 0), (first, LANES - first - a.shape[0])))
    return jnp.concatenate([lanes(a_log, LANE_DECAY), lanes(dt_bias, LANE_DECAY), lanes(f_bias, LANE_FORGET),
                            jnp.zeros((5, LANES), F32)], axis=0)


def _narrow_masks(shape):
    lane = lax.broadcasted_iota(jnp.int32, shape, 1)
    is_beta = lane < LANE_DECAY
    is_decay = (lane >= LANE_DECAY) & (lane < LANE_FORGET)
    is_forget = (lane >= LANE_FORGET) & (lane < LANE_FORGET + N_FOX_HEADS)
    return is_beta, is_decay, is_forget


def _narrow_fwd(proj, params, name):
    def fn(sm, pk):
        is_beta, is_decay, is_forget = _narrow_masks(sm.shape)
        g = -jnp.exp(pk[0:1, :]) * _softplus(sm + pk[1:2, :])
        logf = -_softplus(-(sm + pk[2:3, :]))
        return (jnp.where(is_beta, _sigmoid(sm), jnp.where(is_decay, g, jnp.where(is_forget, logf, 0.0))),)
    return _rowwise(fn, [(proj, LANES, EVEN_NARROW // LANES)], [params], [(LANES, F32)], [], tile=512, name=name)[0]


def _narrow_bwd(proj, params, act, dact, dlogf, name):
    def fn(sm, ab, da, dl, pk):
        is_beta, is_decay, is_forget = _narrow_masks(sm.shape)
        db = jnp.where(is_forget, dl, da)
        d_beta = db * ab * (1.0 - ab)
        d_decay = db * (-jnp.exp(pk[0:1, :])) * _sigmoid(sm + pk[1:2, :])
        d_forget = db * _sigmoid(-(sm + pk[2:3, :]))
        dsm = jnp.where(is_beta, d_beta, jnp.where(is_decay, d_decay, jnp.where(is_forget, d_forget, 0.0)))
        col = lambda x: jnp.sum(x, axis=0, keepdims=True)
        return (dsm, col(jnp.where(is_decay, db * ab, 0.0)), col(jnp.where(is_decay, dsm, 0.0)),
                col(jnp.where(is_forget, dsm, 0.0)))
    return _rowwise(fn, [(proj, LANES, EVEN_NARROW // LANES), act, dact, dlogf], [params], [(LANES, BF16)],
                    [(1, LANES)] * 3, tile=512, name=name)


def _head_rms(xh):
    r = lax.rsqrt(jnp.mean(xh * xh, axis=-1, keepdims=True) + EPS)
    return xh * r, r


def _fox_pre_fwd(proj, qg, kg, name):
    def fn(pf, qgb, kgb):
        out = []
        for idx, xh in enumerate(_heads(pf, 3 * N_FOX_HEADS)):
            if idx < 2 * N_FOX_HEADS:
                xh = _head_rms(xh)[0] * (qgb if idx < N_FOX_HEADS else kgb)
            out.append(xh)
        return (jnp.concatenate(out, axis=1),)
    return _rowwise(fn, [(proj, 3 * D_FOX, EVEN_FOX_QKV // (3 * D_FOX))], [qg, kg], [(3 * D_FOX, BF16)], [],
                    tile=256, name=name)[0]


def _fox_pre_bwd(proj, qg, kg, dq, dk, dv, name):
    def fn(pf, dqb, dkb, dvb, qgb, kgb):
        dout = _heads(dqb, N_FOX_HEADS) + _heads(dkb, N_FOX_HEADS) + _heads(dvb, N_FOX_HEADS)
        dg = [jnp.zeros((1, HEAD_DIM), F32), jnp.zeros((1, HEAD_DIM), F32)]
        dx = []
        for idx, (xh, dh) in enumerate(zip(_heads(pf, 3 * N_FOX_HEADS), dout)):
            if idx < 2 * N_FOX_HEADS:
                which = 0 if idx < N_FOX_HEADS else 1
                xhat, r = _head_rms(xh)
                dg[which] = dg[which] + jnp.sum(dh * xhat, axis=0, keepdims=True)
                dxh = dh * (qgb if which == 0 else kgb)
                dh = r * (dxh - xhat * jnp.mean(dxh * xhat, axis=-1, keepdims=True))
            dx.append(dh)
        return jnp.concatenate(dx, axis=1), dg[0], dg[1]
    return _rowwise(fn, [(proj, 3 * D_FOX, EVEN_FOX_QKV // (3 * D_FOX)), dq, dk, dv], [qg, kg],
                    [(3 * D_FOX, BF16)], [(1, HEAD_DIM)] * 2, tile=256, name=name)


def _mix_gate_fwd(proj, o_dn, o_fox, ng, name):
    def fn(gd, gf, od, of, ngb):
        dn = [_head_rms(xh)[0] * ngb for xh in _heads(od, N_DN_HEADS)]
        return (jnp.concatenate([jnp.concatenate(dn, axis=1) * gd * _sigmoid(gd), of * _sigmoid(gf)], axis=1),)
    return _rowwise(fn, [(proj, D_DN, EVEN_DN_GATE // D_DN), (proj, D_FOX, EVEN_FOX_GATE // D_FOX), o_dn, o_fox],
                    [ng], [(D_DN + D_FOX, BF16)], [], tile=256, name=name)[0]


def _mix_gate_bwd(proj, o_dn, o_fox, ng, dom, name):
    def fn(gd, gf, od, of, dm, ngb):
        d_dn, d_fox = dm[:, :D_DN], dm[:, D_DN:]
        sgd, sgf = _sigmoid(gd), _sigmoid(gf)
        don = d_dn * gd * sgd
        dng = jnp.zeros((1, HEAD_DIM), F32)
        dod, normed = [], []
        for xh, dh in zip(_heads(od, N_DN_HEADS), _heads(don, N_DN_HEADS)):
            xhat, r = _head_rms(xh)
            dng = dng + jnp.sum(dh * xhat, axis=0, keepdims=True)
            dxh = dh * ngb
            dod.append(r * (dxh - xhat * jnp.mean(dxh * xhat, axis=-1, keepdims=True)))
            normed.append(xhat * ngb)
        d_gd = d_dn * jnp.concatenate(normed, axis=1) * _silu_grad(gd, sgd)
        d_gf = d_fox * of * sgf * (1.0 - sgf)
        return jnp.concatenate(dod, axis=1), d_fox * sgf, d_gd, d_gf, dng
    return _rowwise(fn, [(proj, D_DN, EVEN_DN_GATE // D_DN), (proj, D_FOX, EVEN_FOX_GATE // D_FOX), o_dn, o_fox, dom],
                    [ng], [(D_DN, F32), (D_FOX, F32), (D_DN, BF16), (D_FOX, BF16)], [(1, HEAD_DIM)], tile=256,
                    name=name)


def _loss_grad(y, target, name):
    d = y.shape[1]

    def fn(yb, tb):
        diff = yb - tb
        part = jnp.sum(jnp.sum(diff * diff, axis=1, keepdims=True), axis=0, keepdims=True) * (0.5 / d)
        g = diff * (1.0 / d)
        return g, g, part
    return _rowwise(fn, [y, target], [], [(d, F32), (d, BF16)], [(1, 1)], tile=512, name=name)


_REF_EVEN = {"dn_qkv": (0, 1536), "dn_gate": (1536, 2048), "dn_ba": (2048, 2056), "fox_qkv": (2056, 3592),
             "fox_gate": (3592, 4104), "f_pre": (4104, 4108)}
D_IN_EVEN = 4108


def _even_to_kernel_layout(w):
    cut = lambda name: w[..., _REF_EVEN[name][0]:_REF_EVEN[name][1]]
    pad = jnp.zeros(w.shape[:-1] + (EVEN_WIDTH - EVEN_NARROW - 12,), w.dtype)
    return jnp.concatenate([cut("dn_qkv"), cut("fox_qkv"), cut("dn_gate"), cut("fox_gate"), cut("dn_ba"),
                            cut("f_pre"), pad], axis=-1)


def _even_from_kernel_layout(g):
    return jnp.concatenate([g[..., EVEN_DN_QKV:EVEN_FOX_QKV], g[..., EVEN_DN_GATE:EVEN_FOX_GATE],
                            g[..., EVEN_NARROW:EVEN_NARROW + 8], g[..., EVEN_FOX_QKV:EVEN_DN_GATE],
                            g[..., EVEN_FOX_GATE:EVEN_NARROW], g[..., EVEN_NARROW + 8:EVEN_NARROW + 12]], axis=-1)


EVEN_QUARTER = 1027
EVEN_QUARTER_PAD = 1152


def _even_grad_quarters(g):
    g = _even_from_kernel_layout(g)
    pad = [(0, 0)] * (g.ndim - 1) + [(0, EVEN_QUARTER_PAD - EVEN_QUARTER)]
    return jnp.concatenate([jnp.pad(g[..., q * EVEN_QUARTER:(q + 1) * EVEN_QUARTER], pad) for q in range(4)], axis=-1)


def _forget_rows(c):
    return c[:, LANE_FORGET:LANE_FORGET + N_FOX_HEADS].T.reshape(N_FOX_HEADS, 1, c.shape[0])


def _forget_lanes(rows):
    s = rows.shape[2]
    return jnp.pad(rows.reshape(-1, s).T, ((0, 0), (LANE_FORGET, LANES - LANE_FORGET - N_FOX_HEADS)))


def _even_fwd(x, gain, w_in, w_out, j, p, tag):
    h = _rms_fwd(x, gain, f"{tag}_norm")
    proj = _mm(h, w_in, "nn", tm=512, tn=EVEN_WIDTH // 3, out_dtype=F32, name=f"{tag}_in", b_lead=(j,))
    y = _conv_fwd(proj, p["conv_w"], f"{tag}_conv")
    dn_qkv = _dn_pre_fwd(y, f"{tag}_dn_pre")
    act = _narrow_fwd(proj, p["narrow"], f"{tag}_narrow")
    o_dn, states, tinv = _dn_fwd(dn_qkv, act, f"{tag}_delta")
    fox_qkv = _fox_pre_fwd(proj, p["q_g"], p["k_g"], f"{tag}_fox_pre")
    c = _cumsum_rows(act, False, f"{tag}_cumsum")
    ct = _forget_rows(c)
    o_fox, lse = _fox_fwd(fox_qkv, c, ct, f"{tag}_fox")
    om = _mix_gate_fwd(proj, o_dn, o_fox, p["dn_norm_g"], f"{tag}_gate")
    x2 = _mm(om, w_out, "nn", tm=512, tn=x.shape[1], out_dtype=F32, name=f"{tag}_out", residual=x, b_lead=(j,))
    return x2, (x, h, proj, y, dn_qkv, act, states, tinv, o_dn, fox_qkv, c, ct, o_fox, lse, om)


def _even_bwd(dxo, dxo16, saved, gain, w_in, w_out, j, p, tag, g_in, g_out):
    x, h, proj, y, dn_qkv, act, states, tinv, o_dn, fox_qkv, c, ct, o_fox, lse, om = saved
    d = x.shape[1]
    dom = _mm(dxo16, w_out, "nt", tm=512, tn=d, out_dtype=F32, name=f"{tag}_out_bwd", b_lead=(j,))
    g_out = _mm(om, dxo16, "tn", tm=512, tn=d, out_dtype=F32, name=f"{tag}_out_dw", into=(g_out, 0))
    d_odn, d_ofox, d_gd, d_gf, d_ng = _mix_gate_bwd(proj, o_dn, o_fox, p["dn_norm_g"], dom, f"{tag}_gate_bwd")
    dq, dk, dv, dct = _fox_bwd(fox_qkv, c, ct, o_fox, lse, d_ofox, f"{tag}_fox_bwd")
    d_fox_qkv, d_qg, d_kg = _fox_pre_bwd(proj, p["q_g"], p["k_g"], dq, dk, dv, f"{tag}_fox_pre_bwd")
    dlogf = _cumsum_rows(_forget_lanes(dct), True, f"{tag}_cumsum_bwd")
    dq, dk, dv, dact = _dn_bwd(dn_qkv, act, states, tinv, d_odn, f"{tag}_delta_bwd")
    dy = _dn_pre_bwd(y, dq, dk, dv, f"{tag}_dn_pre_bwd")
    d_dn_qkv, d_conv = _conv_bwd(proj, p["conv_w"], dy, f"{tag}_conv_bwd")
    d_narrow, s_alog, s_dt, s_fb = _narrow_bwd(proj, p["narrow"], act, dact, dlogf, f"{tag}_narrow_bwd")
    dproj = jnp.concatenate([d_dn_qkv, d_fox_qkv, d_gd, d_gf, d_narrow], axis=1)
    dh = _mm(dproj, w_in, "nt", tm=512, tn=d, out_dtype=F32, name=f"{tag}_in_bwd", b_lead=(j,))
    g_in = _mm(h, dproj, "tn", tm=512, tn=EVEN_WIDTH // 3, out_dtype=F32, name=f"{tag}_in_dw", into=(g_in, 0))
    dx, dx16, d_gain = _rms_bwd(x, gain, dh, dxo, f"{tag}_norm_bwd")
    small = {"conv_w": d_conv, "a_log": s_alog, "dt_bias": s_dt, "f_bias": s_fb, "dn_norm_g": d_ng, "q_g": d_qg,
             "k_g": d_kg}
    return dx, dx16, d_gain, small, g_in, g_out


def _odd_fwd(x, gain, w_in, w_out, j, tag):
    h = _rms_fwd(x, gain, f"{tag}_norm")
    qkv = _mm(h, w_in, "nn", tm=512, tn=w_in.shape[2] // 2, out_dtype=BF16, name=f"{tag}_in", b_lead=(j,))
    o16, o32 = _sb_fwd(qkv, N_SB_HEADS, f"{tag}_sb")
    x2 = _mm(o16, w_out, "nn", tm=512, tn=x.shape[1], out_dtype=F32, name=f"{tag}_out", residual=x, b_lead=(j,))
    return x2, (x, h, qkv, o16, o32)


def _odd_bwd(dxo, dxo16, saved, gain, w_in, w_out, j, tag, g_in, g_out):
    x, h, qkv, o16, o32 = saved
    d = x.shape[1]
    do = _mm(dxo16, w_out, "nt", tm=512, tn=d, out_dtype=BF16, name=f"{tag}_out_bwd", b_lead=(j,))
    g_out = _mm(o16, dxo16, "tn", tm=512, tn=d, out_dtype=F32, name=f"{tag}_out_dw", into=(g_out, 0))
    dq, dk, dv = _sb_bwd(qkv, o32, do, N_SB_HEADS, f"{tag}_sb_bwd")
    dqkv = jnp.concatenate([dq, dk.astype(BF16), dv.astype(BF16)], axis=1)
    dh = _mm(dqkv, w_in, "nt", tm=512, tn=d, out_dtype=F32, name=f"{tag}_in_bwd", b_lead=(j,))
    g_in = _mm(h, dqkv, "tn", tm=512, tn=w_in.shape[2] // 2, out_dtype=F32, name=f"{tag}_in_dw", into=(g_in, 0))
    dx, dx16, d_gain = _rms_bwd(x, gain, dh, dxo, f"{tag}_norm_bwd")
    return dx, dx16, d_gain, g_in, g_out


def _forward_backward(x, target, w, first, rest_after, token):
    depth = w["norm_ffn1"].shape[0]
    row = lambda a, l: a[l][None]
    rest = {}

    def mats(names, j):
        if j == 0 and names[0] in first:
            return [first[name] for name in names] + [0]
        return [rest[name] for name in names] + [j - (1 if names[0] in first else 0)]

    def even_small(j):
        return {"conv_w": w["dn_conv_w"][j], "narrow": _narrow_params(w["dn_a_log"][j], w["dn_dt_bias"][j],
                                                                     w["fox_f_bias"][j]),
                "dn_norm_g": row(w["dn_norm_g"], j), "q_g": row(w["fox_q_norm_g"], j),
                "k_g": row(w["fox_k_norm_g"], j)}

    saved = []
    for l in range(depth):
        if l == 1:
            rest.update(rest_after(x))
        gain = row(w["norm_ffn1"], l) + token[0:1, 0:1] if l == 0 else row(w["norm_ffn1"], l)
        x, s1 = _ffn_fwd(x, gain, *mats(("ffn1_w_gu", "ffn1_w_down"), l), "ffn1")
        if l % 2 == 0:
            x, s2 = _even_fwd(x, row(w["norm_mix"], l), *mats(("w_in_even", "w_out_even"), l // 2),
                              even_small(l // 2), "even")
        else:
            x, s2 = _odd_fwd(x, row(w["norm_mix"], l), *mats(("w_in_odd", "w_out_odd"), l // 2), "odd")
        x, s3 = _ffn_fwd(x, row(w["norm_ffn2"], l), *mats(("ffn2_w_gu", "ffn2_w_down"), l), "ffn2")
        saved.append((s1, s2, s3))

    dx, dx16, loss = _loss_grad(x, target, "loss")

    kind_of = dict(BIG)
    d_norm = {k: [None] * depth for k in ("norm_ffn1", "norm_mix", "norm_ffn2")}
    d_even = [None] * ((depth + 1) // 2)
    reduced = {name: [] for name in kind_of}
    pending, token = None, None
    for l in reversed(range(depth)):
        s1, s2, s3 = saved[l]
        mixer = ("w_in_even", "w_out_even") if l % 2 == 0 else ("w_in_odd", "w_out_odd")
        names = ["ffn1_w_gu", "ffn1_w_down", *mixer, "ffn2_w_gu", "ffn2_w_down"]
        g = {name: lax.empty((1,) + rest[name].shape[1:], F32) for name in names}
        dx, dx16, d_norm["norm_ffn2"][l], g["ffn2_w_gu"], g["ffn2_w_down"] = _ffn_bwd(
            dx, dx16, s3, row(w["norm_ffn2"], l), *mats(("ffn2_w_gu", "ffn2_w_down"), l), "ffn2", g["ffn2_w_gu"],
            g["ffn2_w_down"], after=token)
        if l % 2 == 0:
            dx, dx16, d_norm["norm_mix"][l], d_even[l // 2], g["w_in_even"], g["w_out_even"] = _even_bwd(
                dx, dx16, s2, row(w["norm_mix"], l), *mats(("w_in_even", "w_out_even"), l // 2), even_small(l // 2),
                "even", g["w_in_even"], g["w_out_even"])
            g["w_in_even"] = _even_grad_quarters(g["w_in_even"])
        else:
            dx, dx16, d_norm["norm_mix"][l], g["w_in_odd"], g["w_out_odd"] = _odd_bwd(
                dx, dx16, s2, row(w["norm_mix"], l), *mats(("w_in_odd", "w_out_odd"), l // 2), "odd", g["w_in_odd"],
                g["w_out_odd"])
        dx, dx16, d_norm["norm_ffn1"][l], g["ffn1_w_gu"], g["ffn1_w_down"] = _ffn_bwd(
            dx, dx16, s1, row(w["norm_ffn1"], l), *mats(("ffn1_w_gu", "ffn1_w_down"), l), "ffn1", g["ffn1_w_gu"],
            g["ffn1_w_down"])
        if pending is not None:
            for name, quarter in zip(pending[-2], _reduce_finish(pending, dx)):
                reduced[name].insert(0, quarter)
        pending, token = _reduce_start([g[name] for name in names], [kind_of[name] for name in names], names,
                                       f"layer{l}")
    for name, quarter in zip(pending[-2], _reduce_finish(pending, dx)):
        reduced[name].insert(0, quarter)
    big = {name: jnp.concatenate(parts, axis=0) for name, parts in reduced.items()}

    small = {k: jnp.concatenate(v, axis=0) for k, v in d_norm.items()}
    dec = slice(LANE_DECAY, LANE_DECAY + N_DN_HEADS)
    fgt = slice(LANE_FORGET, LANE_FORGET + N_FOX_HEADS)
    small["dn_conv_w"] = jnp.stack([e["conv_w"] for e in d_even])
    small["dn_a_log"] = jnp.concatenate([e["a_log"][:, dec] for e in d_even], axis=0)
    small["dn_dt_bias"] = jnp.concatenate([e["dt_bias"][:, dec] for e in d_even], axis=0)
    small["fox_f_bias"] = jnp.concatenate([e["f_bias"][:, fgt] for e in d_even], axis=0)
    small["dn_norm_g"] = jnp.concatenate([e["dn_norm_g"] for e in d_even], axis=0)
    small["fox_q_norm_g"] = jnp.concatenate([e["q_g"] for e in d_even], axis=0)
    small["fox_k_norm_g"] = jnp.concatenate([e["k_g"] for e in d_even], axis=0)
    return loss, dx, big, small


MESH = pl.DeviceIdType.MESH
ANY = pl.BlockSpec(memory_space=pl.ANY)


def _place():
    x, y, c = lax.axis_index("x"), lax.axis_index("y"), lax.axis_index("c")
    return x, y, c, [(1 - x, y), (x, 1 - y), (1 - x, 1 - y)]


def _remote(src, dst, send_sem, recv_sem, to):
    return pltpu.make_async_remote_copy(src_ref=src, dst_ref=dst, send_sem=send_sem, recv_sem=recv_sem,
                                        device_id=to, device_id_type=MESH)


def _aligned(start, multiple):
    return start if isinstance(start, int) else pl.multiple_of(start, multiple)


def _quarter(ref, kind, chip, half, rows, cols):
    k = 2 * chip[0] + chip[1]
    hr = rows // 2
    assert hr % 16 == 0 and cols % LANES == 0
    if kind == "col":
        return ref.at[:, pl.ds(_aligned(half * hr, 16), hr), pl.ds(_aligned(k * cols, LANES), cols)]
    return ref.at[:, pl.ds(_aligned(k * rows + half * hr, 16), hr), :]


def _place_quarter(shard, kind, kc, name, first=0, count=None):
    l, rows, cols = shard.shape
    l = l - first if count is None else count
    tr = rows
    while tr * cols * 4 > (2 << 20) and tr % 32 == 0:
        tr //= 2
    nr = rows // tr
    if kind == "col":
        out_spec = pl.BlockSpec((None, tr, cols), lambda li, i, kc_ref: (li, i, kc_ref[0]))
        out_shape = (l, rows, 4 * cols)
    else:
        out_spec = pl.BlockSpec((None, tr, cols), lambda li, i, kc_ref: (li, kc_ref[0] * nr + i, 0))
        out_shape = (l, 4 * rows, cols)

    def body(kc_ref, x_ref, o_ref):
        o_ref[...] = x_ref[...].astype(BF16)

    return pl.pallas_call(
        body, name=name,
        grid_spec=pltpu.PrefetchScalarGridSpec(
            num_scalar_prefetch=1, grid=(l, nr),
            in_specs=[pl.BlockSpec((None, tr, cols), lambda li, i, kc_ref: (li + first, i, 0))],
            out_specs=out_spec),
        out_shape=jax.ShapeDtypeStruct(out_shape, BF16),
        compiler_params=_cparams("parallel", "parallel"),
    )(kc, shard)


def _gather_weights(wholes, kinds):
    n = len(wholes)

    def dims(ref, kind):
        _, r, cc = ref.shape
        return (r, cc // 4) if kind == "col" else (r // 4, cc)

    def body(*refs):
        bufs = refs[n:2 * n]
        send_sems, recv_sems = refs[2 * n:]
        x, y, c, chips = _place()
        sibling = (x, y, 1 - c)
        first, passed = [], []
        for t in range(n):
            rows, cols = dims(bufs[t], kinds[t])
            mine = _quarter(bufs[t], kinds[t], (x, y), c, rows, cols)
            for j, chip in enumerate(chips):
                cp = _remote(mine, mine, send_sems.at[t, j], recv_sems.at[t, j], (*chip, c))
                cp.start()
                first.append(cp)
        for j, chip in enumerate(chips):
            for t in range(n):
                rows, cols = dims(bufs[t], kinds[t])
                got = _quarter(bufs[t], kinds[t], chip, c, rows, cols)
                _remote(got, got, send_sems.at[t, j], recv_sems.at[t, j], (*chip, c)).wait_recv()
                cp = _remote(got, got, send_sems.at[t, 3 + j], recv_sems.at[t, 3 + j], sibling)
                cp.start()
                passed.append(cp)
        for j, chip in enumerate(chips):
            for t in range(n):
                rows, cols = dims(bufs[t], kinds[t])
                got = _quarter(bufs[t], kinds[t], chip, 1 - c, rows, cols)
                _remote(got, got, send_sems.at[t, 3 + j], recv_sems.at[t, 3 + j], sibling).wait_recv()
        for cp in first + passed:
            cp.wait_send()

    return pl.pallas_call(
        body, name="gather_weights", in_specs=[ANY] * n, out_specs=[ANY] * n,
        out_shape=[jax.ShapeDtypeStruct(a.shape, a.dtype) for a in wholes],
        input_output_aliases={t: t for t in range(n)},
        scratch_shapes=[pltpu.SemaphoreType.DMA((n, 6)), pltpu.SemaphoreType.DMA((n, 6))],
        compiler_params=pltpu.CompilerParams(has_side_effects=True),
    )(*wholes)


def _quarter_dims(ref, kind):
    _, r, cc = ref.shape
    return (r, cc // 4) if kind == "col" else (r // 4, cc)


def _gather_chips_copies(bufs, sems, kinds):
    x, y, c, chips = _place()
    copies = []
    for t, buf in enumerate(bufs):
        rows, cols = _quarter_dims(buf, kinds[t])
        mine = _quarter(buf, kinds[t], (x, y), c, rows, cols)
        for j, chip in enumerate(chips):
            pair = 2 * (OTHER_CHIPS * t + j)
            copies.append(_remote(mine, mine, sems[pair], sems[pair + 1], (*chip, c)))
    return copies


def _gather_start(wholes, kinds, after, tag):
    n = len(wholes)
    n_sems = 2 * OTHER_CHIPS * n
    n_in = n + len(after)

    def body(*refs):
        for cp in _gather_chips_copies(refs[:n], refs[n_in + n:n_in + n + n_sems], kinds):
            cp.start()
        refs[-1][...] = jnp.zeros_like(refs[-1])

    held = [pltpu.with_memory_space_constraint(a, pltpu.HBM) for a in wholes]
    out = pl.pallas_call(
        body, name=f"gather_start_{tag}", in_specs=[HBM] * n + [ANY] * len(after),
        out_specs=(*[HBM] * n, *[SEM] * n_sems, pl.BlockSpec(memory_space=pltpu.VMEM)),
        out_shape=(*[pltpu.HBM(a.shape, a.dtype) for a in held], *[pltpu.SemaphoreType.DMA(())] * n_sems,
                   jax.ShapeDtypeStruct((8, LANES), F32)),
        input_output_aliases={i: i for i in range(n)},
        compiler_params=pltpu.CompilerParams(has_side_effects=SPLIT_COPY),
    )(*held, *after)
    return out[n:n + n_sems], out[:n], out[-1]


def _gather_wait(sems, wholes, kinds, after, tag):
    n = len(wholes)

    def body(*refs):
        for cp in _gather_chips_copies(refs[:n], refs[n:n + len(sems)], kinds):
            cp.wait_send()
            cp.wait_recv()

    return pl.pallas_call(
        body, name=f"gather_wait_{tag}", in_specs=[HBM] * n + [SEM] * len(sems) + [ANY],
        out_specs=tuple([HBM] * n), out_shape=tuple(pltpu.HBM(a.shape, a.dtype) for a in wholes),
        input_output_aliases={i: i for i in range(n)},
        compiler_params=pltpu.CompilerParams(has_side_effects=SPLIT_COPY),
    )(*wholes, *sems, after)


def _gather_forward(wholes, kinds, tag):
    n = len(wholes)

    def body(*refs):
        bufs = refs[n:2 * n]
        send_sems, recv_sems = refs[2 * n:]
        x, y, c, chips = _place()
        copies = []
        for t in range(n):
            rows, cols = _quarter_dims(bufs[t], kinds[t])
            for j, chip in enumerate(chips):
                got = _quarter(bufs[t], kinds[t], chip, c, rows, cols)
                cp = _remote(got, got, send_sems.at[t, j], recv_sems.at[t, j], (x, y, 1 - c))
                cp.start()
                copies.append(cp)
        for cp in copies:
            cp.wait_send()
        for t in range(n):
            rows, cols = _quarter_dims(bufs[t], kinds[t])
            for j, chip in enumerate(chips):
                got = _quarter(bufs[t], kinds[t], chip, 1 - c, rows, cols)
                _remote(got, got, send_sems.at[t, j], recv_sems.at[t, j], (x, y, 1 - c)).wait_recv()

    return pl.pallas_call(
        body, name=f"gather_forward_{tag}", in_specs=[ANY] * n, out_specs=[ANY] * n,
        out_shape=[jax.ShapeDtypeStruct(a.shape, a.dtype) for a in wholes],
        input_output_aliases={t: t for t in range(n)},
        scratch_shapes=[pltpu.SemaphoreType.DMA((n, OTHER_CHIPS)), pltpu.SemaphoreType.DMA((n, OTHER_CHIPS))],
        compiler_params=pltpu.CompilerParams(has_side_effects=True),
    )(*wholes)


def _canonical(a, kind):
    l, r, c = a.shape
    return a.reshape(l, 1, r, c) if kind == "col" else a.reshape(l, 4, r // 4, c)


def _rs_sibling(parts):
    n = len(parts)

    def body(*refs):
        ins, outs = refs[:n], refs[n:2 * n]
        send_sems, recv_sems = refs[2 * n:]
        x, y, c, _ = _place()
        copies = []
        for t in range(n):
            hr = ins[t].shape[2] // 2
            src = ins[t].at[:, :, pl.ds(pl.multiple_of((1 - c) * hr, 8), hr), :]
            cp = _remote(src, outs[t], send_sems.at[t], recv_sems.at[t], (x, y, 1 - c))
            cp.start()
            copies.append(cp)
        for cp in copies:
            cp.wait()

    half = lambda a: jax.ShapeDtypeStruct(a.shape[:2] + (a.shape[2] // 2, a.shape[3]), a.dtype)
    return pl.pallas_call(
        body, name="reduce_sibling", in_specs=[ANY] * n, out_specs=[ANY] * n, out_shape=[half(a) for a in parts],
        scratch_shapes=[pltpu.SemaphoreType.DMA((n,)), pltpu.SemaphoreType.DMA((n,))],
        compiler_params=pltpu.CompilerParams(has_side_effects=True),
    )(*parts)


def _add_tile(rows, cols):
    tc = cols if cols <= 1536 else cols // 4
    tr = rows
    while tr * tc * 4 > (1 << 20) and tr % 16 == 0:
        tr //= 2
    return tr, tc


def _rs_add_sibling(part, got, c, name):
    l, a, hr, cols = got.shape
    tr, tc = _add_tile(hr, cols)
    nr = hr // tr

    def body(c_ref, p_ref, g_ref, o32_ref, o16_ref):
        s = p_ref[...] + g_ref[...]
        o32_ref[...] = s
        o16_ref[...] = s.astype(BF16)

    blk = (None, None, tr, tc)
    spec = pl.BlockSpec(blk, lambda li, ai, i, j, c_ref: (li, ai, i, j))
    return pl.pallas_call(
        body, name=name,
        grid_spec=pltpu.PrefetchScalarGridSpec(
            num_scalar_prefetch=1, grid=(l, a, nr, cols // tc),
            in_specs=[pl.BlockSpec(blk, lambda li, ai, i, j, c_ref: (li, ai, c_ref[0] * nr + i, j)), spec],
            out_specs=[spec, spec]),
        out_shape=[jax.ShapeDtypeStruct(got.shape, F32), jax.ShapeDtypeStruct(got.shape, BF16)],
        compiler_params=_cparams("parallel", "parallel", "parallel", "parallel"),
    )(c, part, got)


def _quarter4(ref, kind, chip, cols):
    k = 2 * chip[0] + chip[1]
    if kind == "col":
        return ref.at[:, :, :, pl.ds(pl.multiple_of(k * cols, LANES), cols)]
    return ref.at[:, pl.ds(k, 1), :, :]


HBM = pl.BlockSpec(memory_space=pltpu.HBM)
SEM = pl.BlockSpec(memory_space=pltpu.SEMAPHORE)
SPLIT_COPY = pltpu.SideEffectType.DATAFLOW_SIDE_EFFECTING
OTHER_CHIPS = 3


def _quarter4_shape(a, kind):
    l, _, hr, cols = a.shape
    return (l, 1, hr, cols // 4 if kind == "col" else cols)


def _rs_chips_copies(srcs, lands, sems, kinds):
    x, y, c, chips = _place()
    copies = []
    for t, (src, land) in enumerate(zip(srcs, lands)):
        cols = _quarter4_shape(src, kinds[t])[3]
        for j, chip in enumerate(chips):
            pair = 2 * (OTHER_CHIPS * t + j)
            copies.append(_remote(_quarter4(src, kinds[t], chip, cols), land.at[j], sems[pair], sems[pair + 1],
                                  (*chip, c)))
    return copies


def _rs_chips_start(sums16, kinds, tag):
    n = len(sums16)
    n_sems = 2 * OTHER_CHIPS * n

    def body(*refs):
        srcs, lands, sems = refs[:n], refs[n:2 * n], refs[4 * n:4 * n + n_sems]
        for cp in _rs_chips_copies(srcs, lands, sems, kinds):
            cp.start()
        refs[-1][...] = jnp.zeros_like(refs[-1])

    lands = [lax.empty((OTHER_CHIPS,) + _quarter4_shape(a, k), a.dtype) for a, k in zip(sums16, kinds)]
    held = [pltpu.with_memory_space_constraint(a, pltpu.HBM) for a in (*sums16, *lands)]
    out = pl.pallas_call(
        body, name=f"reduce_chips_start_{tag}", in_specs=[HBM] * (2 * n),
        out_specs=(*[HBM] * (2 * n), *[SEM] * n_sems, pl.BlockSpec(memory_space=pltpu.VMEM)),
        out_shape=(*[pltpu.HBM(a.shape, a.dtype) for a in held], *[pltpu.SemaphoreType.DMA(())] * n_sems,
                   jax.ShapeDtypeStruct((8, LANES), F32)),
        input_output_aliases={i: i for i in range(2 * n)},
        compiler_params=pltpu.CompilerParams(has_side_effects=SPLIT_COPY),
    )(*held)
    return out[2 * n:2 * n + n_sems], out[:n], out[n:2 * n], out[-1]


def _rs_chips_wait(sems, srcs, lands, kinds, after, tag):
    n = len(srcs)

    def body(*refs):
        for cp in _rs_chips_copies(refs[:n], refs[n:2 * n], refs[2 * n:2 * n + len(sems)], kinds):
            cp.wait_send()
            cp.wait_recv()

    out = pl.pallas_call(
        body, name=f"reduce_chips_wait_{tag}", in_specs=[HBM] * (2 * n) + [SEM] * len(sems) + [ANY],
        out_specs=tuple([HBM] * (2 * n)),
        out_shape=tuple(pltpu.HBM(a.shape, a.dtype) for a in (*srcs, *lands)),
        input_output_aliases={i: i for i in range(2 * n)},
        compiler_params=pltpu.CompilerParams(has_side_effects=SPLIT_COPY),
    )(*srcs, *lands, *sems, after)
    return out[n:]


def _rs_add_chips(sum32, got, kind, kc, name):
    _, l, _, hr, cols = got.shape
    tr, _ = _add_tile(hr, cols)
    nr = hr // tr
    k_arr, c_arr = kc
    if kind == "col":
        own = pl.BlockSpec((None, None, tr, cols), lambda li, i, k_ref, c_ref: (li, 0, i, k_ref[0]))
    else:
        own = pl.BlockSpec((None, None, tr, cols), lambda li, i, k_ref, c_ref: (li, k_ref[0], i, 0))

    def body(k_ref, c_ref, own_ref, got_ref, o_ref):
        o_ref[...] = ((own_ref[...] + got_ref[0].astype(F32)) + got_ref[1].astype(F32)) + got_ref[2].astype(F32)

    return pl.pallas_call(
        body, name=name,
        grid_spec=pltpu.PrefetchScalarGridSpec(
            num_scalar_prefetch=2, grid=(l, nr),
            in_specs=[own, pl.BlockSpec((3, None, None, tr, cols), lambda li, i, k_ref, c_ref: (0, li, 0, i, 0))],
            out_specs=pl.BlockSpec((None, tr, cols), lambda li, i, k_ref, c_ref: (li, c_ref[0] * nr + i, 0))),
        out_shape=jax.ShapeDtypeStruct((l, 2 * hr, cols), F32),
        compiler_params=_cparams("parallel", "parallel"),
    )(k_arr, c_arr, sum32, got)


def _rs_finish(quarters):
    n = len(quarters)

    def body(*refs):
        bufs = refs[n:2 * n]
        send_sems, recv_sems = refs[2 * n:]
        x, y, c, _ = _place()
        copies = []
        for t in range(n):
            hr = bufs[t].shape[1] // 2
            mine = bufs[t].at[:, pl.ds(pl.multiple_of(c * hr, 8), hr), :]
            cp = _remote(mine, mine, send_sems.at[t], recv_sems.at[t], (x, y, 1 - c))
            cp.start()
            copies.append(cp)
        for cp in copies:
            cp.wait()

    return pl.pallas_call(
        body, name="reduce_finish", in_specs=[ANY] * n, out_specs=[ANY] * n,
        out_shape=[jax.ShapeDtypeStruct(a.shape, a.dtype) for a in quarters],
        input_output_aliases={t: t for t in range(n)},
        scratch_shapes=[pltpu.SemaphoreType.DMA((n,)), pltpu.SemaphoreType.DMA((n,))],
        compiler_params=pltpu.CompilerParams(has_side_effects=True),
    )(*quarters)


def _reduce_start(parts, kinds, names, tag):
    c_arr = jnp.reshape(lax.axis_index("c"), (1,)).astype(jnp.int32)
    canon = [_canonical(p, kind) for p, kind in zip(parts, kinds)]
    from_sibling = _rs_sibling(canon)
    sums = [_rs_add_sibling(p, g, c_arr, f"reduce_add_sibling_{nm}") for p, g, nm in zip(canon, from_sibling, names)]
    sems, srcs, lands, token = _rs_chips_start([s16 for _, s16 in sums], kinds, tag)
    return (sems, srcs, lands, [s32 for s32, _ in sums], kinds, names, tag), token


def _reduce_finish(state, after):
    sems, srcs, lands, sums32, kinds, names, tag = state
    x, y, c = lax.axis_index("x"), lax.axis_index("y"), lax.axis_index("c")
    kc = (jnp.reshape(2 * x + y, (1,)).astype(jnp.int32), jnp.reshape(c, (1,)).astype(jnp.int32))
    from_chips = _rs_chips_wait(sems, srcs, lands, kinds, after, tag)
    halves = [_rs_add_chips(s32, g, kind, kc, f"reduce_add_chips_{nm}")
              for s32, g, kind, nm in zip(sums32, from_chips, kinds, names)]
    return _rs_finish(halves)


SMALL_PEERS = 7


def _small_exchange(pack):
    rows = pack.shape[0]

    def body(p_ref, slots_ref, total_ref, send_sems, recv_sems):
        x, y, c, _ = _place()
        me = 4 * x + 2 * y + c
        slots_ref[me] = p_ref[...]
        copies = []
        for p in range(1, SMALL_PEERS + 1):
            px, py, pc = (p >> 2) & 1, (p >> 1) & 1, p & 1
            peer = (1 - x if px else x, 1 - y if py else y, 1 - c if pc else c)
            cp = _remote(p_ref, slots_ref.at[me], send_sems.at[p - 1], recv_sems.at[p - 1], peer)
            cp.start()
            copies.append(cp)
        for cp in copies:
            cp.wait()
        total = slots_ref[0]
        for i in range(1, SMALL_PEERS + 1):
            total = total + slots_ref[i]
        total_ref[...] = total

    vmem = pl.BlockSpec(memory_space=pltpu.VMEM)
    return pl.pallas_call(
        body, name="small_exchange", in_specs=[vmem], out_specs=[vmem, vmem],
        out_shape=[jax.ShapeDtypeStruct((SMALL_PEERS + 1, rows, LANES), F32), jax.ShapeDtypeStruct((rows, LANES), F32)],
        scratch_shapes=[pltpu.SemaphoreType.DMA((SMALL_PEERS,)), pltpu.SemaphoreType.DMA((SMALL_PEERS,))],
        compiler_params=pltpu.CompilerParams(has_side_effects=True),
    )(pack)


def _pack(arrays):
    rows = []
    for a in arrays:
        flat = a.reshape(-1).astype(F32)
        rows.append(jnp.pad(flat, (0, (-flat.shape[0]) % LANES)).reshape(-1, LANES))
    out = jnp.concatenate(rows, axis=0)
    return jnp.pad(out, ((0, (-out.shape[0]) % 8), (0, 0)))


def _unpack(pack, shapes):
    out, r = [], 0
    for sh in shapes:
        size = math.prod(sh)
        nr = -(-size // LANES)
        out.append(pack[r:r + nr].reshape(-1)[:size].reshape(sh))
        r += nr
    return out


def _adamw(w, g, m, v, name):
    shape = w.shape
    to2d = lambda a: a.reshape(-1, shape[-1])
    rows = math.prod(shape[:-1])
    tile = 256 if rows % 256 == 0 else rows

    def fn(wb, gb, mb, vb):
        m2 = ADAM_B1 * mb + (1.0 - ADAM_B1) * gb
        v2 = ADAM_B2 * vb + (1.0 - ADAM_B2) * (gb * gb)
        m_hat = m2 / (1.0 - ADAM_B1 ** ADAM_STEP)
        v_hat = v2 / (1.0 - ADAM_B2 ** ADAM_STEP)
        return -ADAM_LR * (m_hat / (jnp.sqrt(v_hat) + ADAM_EPS) + ADAM_WD * wb), m2, v2

    res = _rowwise(fn, [to2d(w), to2d(g), to2d(m), to2d(v)], [], [(shape[-1], F32)] * 3, [], tile=tile, name=name)
    return [r.reshape(shape) for r in res]


BIG = (("ffn1_w_gu", "col"), ("ffn1_w_down", "row"), ("w_in_even", "col"), ("w_out_even", "row"),
       ("w_in_odd", "col"), ("w_out_odd", "row"), ("ffn2_w_gu", "col"), ("ffn2_w_down", "row"))
SMALL = ("norm_ffn1", "norm_mix", "dn_conv_w", "dn_a_log", "dn_dt_bias", "dn_norm_g", "fox_q_norm_g", "fox_k_norm_g",
         "fox_f_bias", "norm_ffn2")
WEIGHTS = ("norm_ffn1", "ffn1_w_gu", "ffn1_w_down", "norm_mix", "w_in_even", "dn_conv_w", "dn_a_log", "dn_dt_bias",
           "dn_norm_g", "fox_q_norm_g", "fox_k_norm_g", "fox_f_bias", "w_out_even", "w_in_odd", "w_out_odd",
           "norm_ffn2", "ffn2_w_gu", "ffn2_w_down")


def _step(x, target, w, m, v):
    k = 2 * lax.axis_index("x") + lax.axis_index("y")
    n_conv = w["dn_conv_w"].shape[2]

    kc = jnp.reshape(k, (1,)).astype(jnp.int32)
    kinds = dict(BIG)
    quarters = {name: w[name] for name in kinds}
    quarters["w_in_even"] = jnp.pad(w["w_in_even"], ((0, 0), (0, 0), (0, EVEN_QUARTER_PAD - EVEN_QUARTER)))
    first_names = [name for name in kinds if name not in ("w_in_odd", "w_out_odd")]
    rest_names = list(kinds)

    def even_columns(whole):
        padded = whole["w_in_even"]
        ref_order = jnp.concatenate([padded[..., q * EVEN_QUARTER_PAD:q * EVEN_QUARTER_PAD + EVEN_QUARTER]
                                     for q in range(4)], axis=-1)
        return {**whole, "w_in_even": _even_to_kernel_layout(ref_order)}

    conv_slots, _ = _small_exchange(_pack([w["dn_conv_w"]]))
    placed = [_place_quarter(quarters[name], kinds[name], kc, f"place_first_{name}", 0, 1) for name in first_names]
    gathered = _gather_weights(placed, [kinds[name] for name in first_names])
    first = even_columns(dict(zip(first_names, gathered)))
    placed = [_place_quarter(quarters[name], kinds[name], kc, f"place_rest_{name}", 1 if name in first_names else 0)
              for name in rest_names]
    rest_kinds = [kinds[name] for name in rest_names]
    sems, on_their_way, token = _gather_start(placed, rest_kinds, [conv_slots, *gathered], "rest")

    def rest_after(value):
        landed = _gather_wait(sems, on_their_way, rest_kinds, value, "rest")
        return even_columns(dict(zip(rest_names, _gather_forward(landed, rest_kinds, "rest"))))

    whole = {}
    conv_rows = math.prod(w["dn_conv_w"].shape) // LANES
    conv_quarters = [conv_slots[2 * q, :conv_rows].reshape(w["dn_conv_w"].shape) for q in range(4)]
    whole["dn_conv_w"] = jnp.concatenate(conv_quarters, axis=-1)
    for name in SMALL:
        if name != "dn_conv_w":
            whole[name] = w[name]

    loss, dx, reduced, small = _forward_backward(x, target, whole, first, rest_after, token)

    reduced["w_in_even"] = reduced["w_in_even"][..., :EVEN_QUARTER]
    _, small_sum = _small_exchange(_pack([small[n] for n in SMALL]))
    grads = dict(zip(SMALL, _unpack(small_sum, [small[n].shape for n in SMALL])))
    grads["dn_conv_w"] = lax.dynamic_slice_in_dim(grads["dn_conv_w"], k * n_conv, n_conv, axis=2)
    grads.update(reduced)

    delta, new_m, new_v = {}, {}, {}
    for name, _ in BIG:
        delta[name], new_m[name], new_v[name] = _adamw(w[name], grads[name], m[name], v[name], f"adamw_{name}")
    packs = [_pack([d[n] for n in SMALL]) for d in (w, grads, m, v)]
    shapes = [w[n].shape for n in SMALL]
    for out, res in zip((delta, new_m, new_v), _adamw(*packs, "adamw_small")):
        out.update(zip(SMALL, _unpack(res, shapes)))
    total_loss = lax.psum(loss[0, 0], ("x", "y", "c"))
    return total_loss, dx, grads, delta, new_m, new_v


def kernel(x, norm_ffn1, ffn1_w_gu, ffn1_w_down, norm_mix, w_in_even, dn_conv_w, dn_a_log, dn_dt_bias, dn_norm_g, fox_q_norm_g, fox_k_norm_g, fox_f_bias, w_out_even, w_in_odd, w_out_odd, norm_ffn2, ffn2_w_gu, ffn2_w_down, loss_target, m_norm_ffn1, m_ffn1_w_gu, m_ffn1_w_down, m_norm_mix, m_w_in_even, m_dn_conv_w, m_dn_a_log, m_dn_dt_bias, m_dn_norm_g, m_fox_q_norm_g, m_fox_k_norm_g, m_fox_f_bias, m_w_out_even, m_w_in_odd, m_w_out_odd, m_norm_ffn2, m_ffn2_w_gu, m_ffn2_w_down, v_norm_ffn1, v_ffn1_w_gu, v_ffn1_w_down, v_norm_mix, v_w_in_even, v_dn_conv_w, v_dn_a_log, v_dn_dt_bias, v_dn_norm_g, v_fox_q_norm_g, v_fox_k_norm_g, v_fox_f_bias, v_w_out_even, v_w_in_odd, v_w_out_odd, v_norm_ffn2, v_ffn2_w_gu, v_ffn2_w_down):
    w = dict(zip(WEIGHTS, (norm_ffn1, ffn1_w_gu, ffn1_w_down, norm_mix, w_in_even, dn_conv_w, dn_a_log, dn_dt_bias,
                           dn_norm_g, fox_q_norm_g, fox_k_norm_g, fox_f_bias, w_out_even, w_in_odd, w_out_odd,
                           norm_ffn2, ffn2_w_gu, ffn2_w_down)))
    m = dict(zip(WEIGHTS, (m_norm_ffn1, m_ffn1_w_gu, m_ffn1_w_down, m_norm_mix, m_w_in_even, m_dn_conv_w, m_dn_a_log,
                           m_dn_dt_bias, m_dn_norm_g, m_fox_q_norm_g, m_fox_k_norm_g, m_fox_f_bias, m_w_out_even,
                           m_w_in_odd, m_w_out_odd, m_norm_ffn2, m_ffn2_w_gu, m_ffn2_w_down)))
    v = dict(zip(WEIGHTS, (v_norm_ffn1, v_ffn1_w_gu, v_ffn1_w_down, v_norm_mix, v_w_in_even, v_dn_conv_w, v_dn_a_log,
                           v_dn_dt_bias, v_dn_norm_g, v_fox_q_norm_g, v_fox_k_norm_g, v_fox_f_bias, v_w_out_even,
                           v_w_in_odd, v_w_out_odd, v_norm_ffn2, v_ffn2_w_gu, v_ffn2_w_down)))
    loss, dx, grads, delta, new_m, new_v = _step(x[0], loss_target[0], w, m, v)
    return (loss, dx[None], *[grads[n] for n in WEIGHTS], *[delta[n] for n in WEIGHTS],
            *[new_m[n] for n in WEIGHTS], *[new_v[n] for n in WEIGHTS])
```

```python
import functools
import math

import jax
import jax.numpy as jnp
from jax import lax
from jax.experimental import pallas as pl
from jax.experimental.pallas import tpu as pltpu

F32 = jnp.float32
BF16 = jnp.bfloat16
HI = lax.Precision.HIGHEST

HEAD_DIM = 128
N_DN_HEADS = 4
N_FOX_HEADS = 4
N_SB_HEADS = 8
D_DN = N_DN_HEADS * HEAD_DIM
D_FOX = N_FOX_HEADS * HEAD_DIM
CONV_WIDTH = 4
DN_CHUNK = 64
EPS = 1e-6
ATT_SCALE = HEAD_DIM ** -0.5
ADAM_LR, ADAM_B1, ADAM_B2, ADAM_EPS, ADAM_WD, ADAM_STEP = 0.001, 0.9, 0.999, 1e-08, 0.01, 10

V7X_VMEM_LIMIT = 56 * 1024 * 1024
LANES = 128
ATT_TQ = 256
ATT_TK = 128
ATT_SUB = ATT_TQ // ATT_TK

LANE_BETA, LANE_DECAY, LANE_FORGET = 0, 4, 8


def _cparams(*sem):
    return pltpu.CompilerParams(dimension_semantics=sem, vmem_limit_bytes=V7X_VMEM_LIMIT)


def _sigmoid(x):
    return 1.0 / (1.0 + jnp.exp(-x))


def _softplus(x):
    return jnp.maximum(x, 0.0) + jnp.log(1.0 + jnp.exp(-jnp.abs(x)))


def _silu_grad(y, sg):
    return sg * (1.0 + y * (1.0 - sg))


def _rowwise(fn, rows, bcast, outs, sums, *, tile, name):
    rows = [r if isinstance(r, tuple) else (r, r.shape[1], 0) for r in rows]
    s = rows[0][0].shape[0]
    assert s % tile == 0
    n_in, n_b, n_out, n_sum = len(rows), len(bcast), len(outs), len(sums)

    def body(*refs):
        ins = [r[...] for r in refs[:n_in + n_b]]
        res = fn(*ins)
        if not isinstance(res, (tuple, list)):
            res = (res,)
        out_refs = refs[n_in + n_b:n_in + n_b + n_out]
        sum_refs = refs[n_in + n_b + n_out:]
        for o_ref, val in zip(out_refs, res[:n_out]):
            o_ref[...] = val.astype(o_ref.dtype)
        if n_sum:
            @pl.when(pl.program_id(0) == 0)
            def _():
                for s_ref in sum_refs:
                    s_ref[...] = jnp.zeros_like(s_ref)
            for s_ref, val in zip(sum_refs, res[n_out:]):
                s_ref[...] += val

    in_specs = [pl.BlockSpec((tile, w), lambda i, cb=cb: (i, cb)) for _, w, cb in rows]
    in_specs += [pl.BlockSpec(b.shape, lambda i, nd=b.ndim: (0,) * nd) for b in bcast]
    out_specs = [pl.BlockSpec((tile, c), lambda i: (i, 0)) for c, _ in outs]
    out_specs += [pl.BlockSpec(sh, lambda i: (0, 0)) for sh in sums]
    out_shape = [jax.ShapeDtypeStruct((s, c), dt) for c, dt in outs]
    out_shape += [jax.ShapeDtypeStruct(sh, F32) for sh in sums]
    return pl.pallas_call(
        body, name=name, grid=(s // tile,), in_specs=in_specs, out_specs=out_specs, out_shape=out_shape,
        compiler_params=_cparams("arbitrary" if n_sum else "parallel"),
    )(*[r[0] for r in rows], *bcast)


def _rms_fwd(x, gain, name):
    def fn(xb, g):
        r = lax.rsqrt(jnp.mean(xb * xb, axis=-1, keepdims=True) + EPS)
        return (xb * r * g,)
    return _rowwise(fn, [x], [gain], [(x.shape[1], BF16)], [], tile=512, name=name)[0]


def _rms_bwd(x, gain, dn, dres, name):
    def fn(xb, dnb, drb, g):
        r = lax.rsqrt(jnp.mean(xb * xb, axis=-1, keepdims=True) + EPS)
        xh = xb * r
        dxh = dnb * g
        dx = drb + r * (dxh - xh * jnp.mean(dxh * xh, axis=-1, keepdims=True))
        return dx, dx, jnp.sum(dnb * xh, axis=0, keepdims=True)
    d = x.shape[1]
    return _rowwise(fn, [x, dn, dres], [gain], [(d, F32), (d, BF16)], [(1, d)], tile=512, name=name)


_DIMS = {"nn": (((1,), (0,)), ((), ())), "nt": (((1,), (1,)), ((), ())), "tn": (((0,), (0,)), ((), ()))}


def _dot(a, b, kind):
    return lax.dot_general(a.astype(BF16), b.astype(BF16), _DIMS[kind], preferred_element_type=F32)


def _dot32(a, b, kind="nn"):
    return lax.dot_general(a, b, _DIMS[kind], precision=HI, preferred_element_type=F32)


def _mm(a, b, kind, *, tm, tn, out_dtype, name, scale=None, residual=None, a_lead=(), b_lead=(),
        b_spec=None, n=None, into=None):
    ash, bsh = a.shape[len(a_lead):], b.shape[len(b_lead):]
    m = ash[1] if kind == "tn" else ash[0]
    k = ash[0] if kind == "tn" else ash[1]
    if b_spec is None:
        n = bsh[0] if kind == "nt" else bsh[1]
        assert k == (bsh[1] if kind == "nt" else bsh[0]), (ash, bsh, kind)
    assert m % tm == 0 and n % tn == 0, (m, tm, n, tn)
    la, lb = (None,) * len(a_lead), (None,) * len(b_lead)
    if kind == "tn":
        a_spec = pl.BlockSpec(la + (k, tm), lambda j, i: a_lead + (0, i))
    else:
        a_spec = pl.BlockSpec(la + (tm, k), lambda j, i: a_lead + (i, 0))
    if b_spec is None:
        if kind == "nt":
            b_spec = pl.BlockSpec(lb + (tn, k), lambda j, i: b_lead + (j, 0))
        else:
            b_spec = pl.BlockSpec(lb + (k, tn), lambda j, i: b_lead + (0, j))
    in_specs, args = [a_spec, b_spec], [a, b]
    if residual is not None:
        in_specs.append(pl.BlockSpec((tm, tn), lambda j, i: (i, j)))
        args.append(residual)
    aliases = {}
    if into is not None:
        buf, layer = into
        in_specs.append(pl.BlockSpec(memory_space=pl.ANY))
        args.append(buf)
        aliases = {len(args) - 1: 0}
        out_spec = pl.BlockSpec((None, tm, tn), lambda j, i: (layer, i, j))
        out_shape = jax.ShapeDtypeStruct(buf.shape, buf.dtype)
    else:
        out_spec = pl.BlockSpec((tm, tn), lambda j, i: (i, j))
        out_shape = jax.ShapeDtypeStruct((m, n), out_dtype)

    def body(a_ref, b_ref, *rest):
        acc = _dot(a_ref[...], b_ref[...], kind)
        if scale is not None:
            acc = acc * scale
        if residual is not None:
            acc = acc + rest[0][...]
        rest[-1][...] = acc.astype(rest[-1].dtype)

    return pl.pallas_call(
        body, name=name, grid=(n // tn, m // tm), in_specs=in_specs, out_specs=out_spec, out_shape=out_shape,
        input_output_aliases=aliases, compiler_params=_cparams("parallel", "parallel"),
    )(*args)


def _ffn_up(n, w_gu, layer, name):
    s, d = n.shape
    f = w_gu.shape[2] // 2
    tm, tn = 512, f // 2
    nj = f // tn

    def body(n_ref, wg_ref, wu_ref, gu_ref, a_ref):
        nv = n_ref[...]
        g = _dot(nv, wg_ref[...], "nn")
        u = _dot(nv, wu_ref[...], "nn")
        gu_ref[0] = g.astype(BF16)
        gu_ref[1] = u.astype(BF16)
        a_ref[...] = (g * _sigmoid(g) * u).astype(BF16)

    return pl.pallas_call(
        body, name=name, grid=(nj, s // tm),
        in_specs=[pl.BlockSpec((tm, d), lambda j, i: (i, 0)),
                  pl.BlockSpec((None, d, tn), lambda j, i: (layer, 0, j)),
                  pl.BlockSpec((None, d, tn), lambda j, i: (layer, 0, j + nj))],
        out_specs=[pl.BlockSpec((2, tm, tn), lambda j, i: (0, i, j)),
                   pl.BlockSpec((tm, tn), lambda j, i: (i, j))],
        out_shape=[jax.ShapeDtypeStruct((2, s, f), BF16), jax.ShapeDtypeStruct((s, f), BF16)],
        compiler_params=_cparams("parallel", "parallel"),
    )(n, w_gu, w_gu)


def _ffn_down_bwd(dxo, w_down, gu, layer, name, after=None):
    s, d = dxo.shape
    f = w_down.shape[1]
    tm, tn = 512, f // 2
    extra_specs, extra = ([ANY], [after]) if after is not None else ([], [])

    def body(dx_ref, w_ref, gu_ref, *rest):
        dgu_ref = rest[-1]
        da = 0.5 * _dot(dx_ref[...], w_ref[...], "nt")
        g = gu_ref[0].astype(F32)
        u = gu_ref[1].astype(F32)
        sg = _sigmoid(g)
        dgu_ref[0] = (da * u * _silu_grad(g, sg)).astype(BF16)
        dgu_ref[1] = (da * g * sg).astype(BF16)

    return pl.pallas_call(
        body, name=name, grid=(f // tn, s // tm),
        in_specs=[pl.BlockSpec((tm, d), lambda j, i: (i, 0)),
                  pl.BlockSpec((None, tn, d), lambda j, i: (layer, j, 0)),
                  pl.BlockSpec((2, tm, tn), lambda j, i: (0, i, j))] + extra_specs,
        out_specs=pl.BlockSpec((2, tm, tn), lambda j, i: (0, i, j)),
        out_shape=jax.ShapeDtypeStruct((2, s, f), BF16),
        compiler_params=_cparams("parallel", "parallel"),
    )(dxo, w_down, gu, *extra)


def _ffn_dn(dgu, w_gu, layer, name):
    _, s, f = dgu.shape
    d = w_gu.shape[1]
    tm, tn = 512, d

    def body(dgu_ref, wg_ref, wu_ref, o_ref):
        o_ref[...] = _dot(dgu_ref[0], wg_ref[...], "nt") + _dot(dgu_ref[1], wu_ref[...], "nt")

    return pl.pallas_call(
        body, name=name, grid=(s // tm, d // tn),
        in_specs=[pl.BlockSpec((2, tm, f), lambda i, j: (0, i, 0)),
                  pl.BlockSpec((None, tn, f), lambda i, j: (layer, j, 0)),
                  pl.BlockSpec((None, tn, f), lambda i, j: (layer, j, 1))],
        out_specs=pl.BlockSpec((tm, tn), lambda i, j: (i, j)),
        out_shape=jax.ShapeDtypeStruct((s, d), F32),
        compiler_params=_cparams("parallel", "parallel"),
    )(dgu, w_gu, w_gu)


def _ffn_fwd(x, gain, w_gu, w_down, layer, tag):
    n = _rms_fwd(x, gain, f"{tag}_norm")
    gu, a = _ffn_up(n, w_gu, layer, f"{tag}_up")
    x2 = _mm(a, w_down, "nn", tm=512, tn=x.shape[1], out_dtype=F32, name=f"{tag}_down", scale=0.5, residual=x,
             b_lead=(layer,))
    return x2, (x, n, gu, a)


def _ffn_bwd(dxo, dxo16, saved, gain, w_gu, w_down, layer, tag, g_gu, g_down, after=None):
    x, n, gu, a = saved
    s, f = a.shape
    dgu = _ffn_down_bwd(dxo16, w_down, gu, layer, f"{tag}_down_bwd", after)
    g_down = _mm(a, dxo16, "tn", tm=256, tn=dxo16.shape[1], out_dtype=F32, name=f"{tag}_down_dw", scale=0.5,
                 into=(g_down, 0))
    dn = _ffn_dn(dgu, w_gu, layer, f"{tag}_up_bwd")
    tn = f // 2
    nj = f // tn
    g_gu = _mm(n, dgu, "tn", tm=512, tn=tn, out_dtype=F32, name=f"{tag}_up_dw", into=(g_gu, 0), n=2 * f,
               b_spec=pl.BlockSpec((None, s, tn), lambda j, i: (j // nj, 0, j % nj)))
    dx, dx16, dgain = _rms_bwd(x, gain, dn, dxo, f"{tag}_norm_bwd")
    return dx, dx16, dgain, g_gu, g_down


def _lane_col(blk, lane):
    li = lax.broadcasted_iota(jnp.int32, blk.shape, 1)
    return jnp.sum(jnp.where(li == lane, blk, 0.0), axis=1, keepdims=True)


def _split_dot(x, tri):
    hi = x.astype(BF16)
    lo = (x - hi.astype(F32)).astype(BF16)
    return (lax.dot_general(hi, tri, _DIMS["nn"], preferred_element_type=F32)
            + lax.dot_general(lo, tri, _DIMS["nn"], preferred_element_type=F32))


class _Each:
    def __init__(self, vals):
        self.vals = list(vals)

    def _with(self, other, op):
        others = other.vals if isinstance(other, _Each) else [other] * len(self.vals)
        return _Each(op(a, b) for a, b in zip(self.vals, others))

    def __add__(self, other):
        return self._with(other, lambda a, b: a + b)

    def __sub__(self, other):
        return self._with(other, lambda a, b: a - b)

    def __mul__(self, other):
        return self._with(other, lambda a, b: a * b)

    def __neg__(self):
        return _Each(-a for a in self.vals)


def _each(fn, *args):
    n = max(len(a.vals) for a in args if isinstance(a, _Each))
    res = [fn(*xs) for xs in zip(*[a.vals if isinstance(a, _Each) else [a] * n for a in args])]
    if isinstance(res[0], tuple):
        return tuple(_Each(r) for r in zip(*res))
    return _Each(res)


def _keep(cond, x):
    return _each(lambda v: jnp.where(cond, v, 0.0), x)


def _rowsum(x):
    return _each(lambda v: jnp.sum(v, axis=1, keepdims=True), x)


ATT_HEADS = 2
ATT_WIDTH = ATT_HEADS * HEAD_DIM
_HEAD_COLS = [slice(h * HEAD_DIM, (h + 1) * HEAD_DIM) for h in range(ATT_HEADS)]


def _att_specs(n_heads, s):
    groups = n_heads // ATT_HEADS
    q_spec = pl.BlockSpec((ATT_TQ, ATT_WIDTH), lambda g, i: (i, g))
    k_spec = pl.BlockSpec((s, ATT_WIDTH), lambda g, i: (0, groups + g))
    v_spec = pl.BlockSpec((s, ATT_WIDTH), lambda g, i: (0, 2 * groups + g))
    return q_spec, k_spec, v_spec


def _heads_of(ref, rows=None):
    return _Each(ref[:, cs] if rows is None else ref[rows, cs] for cs in _HEAD_COLS)


def _dot_each(a, b, kind):
    return _each(lambda x, y: _dot(x, y, kind), a, b)


def _att_iotas():
    row = lax.broadcasted_iota(jnp.int32, (ATT_TQ, ATT_TK), 0)
    col = lax.broadcasted_iota(jnp.int32, (ATT_TQ, ATT_TK), 1)
    jr = lax.broadcasted_iota(jnp.int32, (ATT_TK, ATT_TK), 0)
    jc = lax.broadcasted_iota(jnp.int32, (ATT_TK, ATT_TK), 1)
    return row, col, jr, jc


def _sb_fwd(qkv, n_heads, name):
    s = qkv.shape[0]

    def body(q_ref, k_ref, v_ref, o16_ref, o32_ref):
        i = pl.program_id(1)
        q = _heads_of(q_ref)
        row, col, jr, jc = _att_iotas()
        later = (jr > jc).astype(BF16)

        def step(jb, carry, diagonal):
            c_sp, acc = (_Each(part) for part in carry)
            work = []
            for sub in reversed(range(ATT_SUB)):
                keys = pl.ds(pl.multiple_of(jb * ATT_TQ + sub * ATT_TK, ATT_TK), ATT_TK)
                z = _dot_each(q, _heads_of(k_ref, keys), "nt") * ATT_SCALE
                sp = _each(_softplus, z)
                before = (col + sub * ATT_TK) < row if diagonal else None
                spm = _keep(before, sp) if diagonal else sp
                work.append((keys, z - sp, spm, _each(lambda x: _dot(x, later, "nn"), spm), before))
            for keys, logsig, spm, within, before in work:
                a = _each(jnp.exp, logsig - (c_sp + within))
                if diagonal:
                    a = _keep(before, a)
                acc = acc + _each(_split_dot, a, _heads_of(v_ref, keys))
                c_sp = c_sp + _rowsum(spm)
            return tuple(c_sp.vals), tuple(acc.vals)

        zeros = lambda width: tuple(jnp.zeros((ATT_TQ, width), F32) for _ in range(ATT_HEADS))
        carry = step(i, (zeros(1), zeros(HEAD_DIM)), True)
        _, acc = lax.fori_loop(0, i, lambda it, cr: step(i - 1 - it, cr, False), carry)
        for cs, acc_h in zip(_HEAD_COLS, acc):
            o16_ref[:, cs] = acc_h.astype(BF16)
            o32_ref[:, cs] = acc_h

    q_spec, k_spec, v_spec = _att_specs(n_heads, s)
    o_spec = pl.BlockSpec((ATT_TQ, ATT_WIDTH), lambda g, i: (i, g))
    return pl.pallas_call(
        body, name=name, grid=(n_heads // ATT_HEADS, s // ATT_TQ), in_specs=[q_spec, k_spec, v_spec],
        out_specs=[o_spec, o_spec],
        out_shape=[jax.ShapeDtypeStruct((s, n_heads * HEAD_DIM), BF16),
                   jax.ShapeDtypeStruct((s, n_heads * HEAD_DIM), F32)],
        compiler_params=_cparams("parallel", "arbitrary"),
    )(qkv, qkv, qkv)


def _sb_bwd(qkv, o32, do, n_heads, name):
    s = qkv.shape[0]

    def body(q_ref, k_ref, v_ref, o_ref, do_ref, dq_ref, dk_ref, dv_ref):
        i = pl.program_id(1)

        @pl.when(i == 0)
        def _():
            dk_ref[...] = jnp.zeros_like(dk_ref)
            dv_ref[...] = jnp.zeros_like(dv_ref)

        q, do = _heads_of(q_ref), _heads_of(do_ref)
        total = _rowsum(_each(lambda a, b: a.astype(F32) * b, do, _heads_of(o_ref)))
        row, col, jr, jc = _att_iotas()
        later = (jr > jc).astype(BF16)
        not_before = (jr >= jc).astype(BF16)

        def step(jb, carry, diagonal):
            c_sp, c_e, dq = (_Each(part) for part in carry)
            work = []
            for sub in reversed(range(ATT_SUB)):
                keys = pl.ds(pl.multiple_of(jb * ATT_TQ + sub * ATT_TK, ATT_TK), ATT_TK)
                k = _heads_of(k_ref, keys)
                z = _dot_each(q, k, "nt") * ATT_SCALE
                sp = _each(_softplus, z)
                before = (col + sub * ATT_TK) < row if diagonal else None
                spm = _keep(before, sp) if diagonal else sp
                work.append((keys, k, _each(jnp.exp, z - sp), spm, _each(lambda x: _dot(x, later, "nn"), spm),
                             _dot_each(do, _heads_of(v_ref, keys), "nt"), before))
            for keys, k, sig, spm, within, da, before in work:
                a = sig * _each(lambda x: jnp.exp(-x), c_sp + within)
                if diagonal:
                    a = _keep(before, a)
                e = a * da
                left = total - c_e - _each(lambda x: _split_dot(x, not_before), e)
                dz = (e - (e + left) * sig) * ATT_SCALE
                if diagonal:
                    dz = _keep(before, dz)
                dk, dv = _dot_each(dz, q, "tn"), _dot_each(a, do, "tn")
                for cs, dk_h, dv_h in zip(_HEAD_COLS, dk.vals, dv.vals):
                    dk_ref[keys, cs] += dk_h
                    dv_ref[keys, cs] += dv_h
                dq = dq + _dot_each(dz, k, "nn")
                c_sp = c_sp + _rowsum(spm)
                c_e = c_e + _rowsum(e)
            return tuple(c_sp.vals), tuple(c_e.vals), tuple(dq.vals)

        zeros = lambda width: tuple(jnp.zeros((ATT_TQ, width), F32) for _ in range(ATT_HEADS))
        carry = step(i, (zeros(1), zeros(1), zeros(HEAD_DIM)), True)
        _, _, dq = lax.fori_loop(0, i, lambda it, cr: step(i - 1 - it, cr, False), carry)
        for cs, dq_h in zip(_HEAD_COLS, dq):
            dq_ref[:, cs] = dq_h.astype(BF16)

    q_spec, k_spec, v_spec = _att_specs(n_heads, s)
    blk = pl.BlockSpec((ATT_TQ, ATT_WIDTH), lambda g, i: (i, g))
    full = pl.BlockSpec((s, ATT_WIDTH), lambda g, i: (0, g))
    wide = (s, n_heads * HEAD_DIM)
    return pl.pallas_call(
        body, name=name, grid=(n_heads // ATT_HEADS, s // ATT_TQ), in_specs=[q_spec, k_spec, v_spec, blk, blk],
        out_specs=[blk, full, full],
        out_shape=[jax.ShapeDtypeStruct(wide, BF16), jax.ShapeDtypeStruct(wide, F32), jax.ShapeDtypeStruct(wide, F32)],
        compiler_params=_cparams("parallel", "arbitrary"),
    )(qkv, qkv, qkv, o32, do)


def _fox_logits(q, k, cq, ct_ref, keys):
    ck = _Each(ct_ref[h, :, keys] for h in range(ATT_HEADS))
    return _dot_each(q, k, "nt") * ATT_SCALE + (cq - ck)


def _fox_cq(c_ref, group):
    c = c_ref[...]
    return _Each(_lane_col(c, LANE_FORGET + group * ATT_HEADS + h) for h in range(ATT_HEADS))


def _fox_fwd(qkv, c, ct, name):
    s = qkv.shape[0]
    n_heads = N_FOX_HEADS

    def body(q_ref, k_ref, v_ref, c_ref, ct_ref, o_ref, lse_ref):
        g, i = pl.program_id(0), pl.program_id(1)
        q = _heads_of(q_ref)
        cq = _fox_cq(c_ref, g)
        row, col, _, _ = _att_iotas()

        def step(jb, carry, diagonal):
            m, l, acc = (_Each(part) for part in carry)
            work = []
            m_new = m
            for sub in range(ATT_SUB):
                keys = pl.ds(pl.multiple_of(jb * ATT_TQ + sub * ATT_TK, ATT_TK), ATT_TK)
                sc = _fox_logits(q, _heads_of(k_ref, keys), cq, ct_ref, keys)
                valid = (col + sub * ATT_TK) <= row if diagonal else None
                if diagonal:
                    sc = _each(lambda x: jnp.where(valid, x, -1e30), sc)
                m_new = _each(lambda a, x: jnp.maximum(a, jnp.max(x, axis=1, keepdims=True)), m_new, sc)
                work.append((keys, sc, valid))
            w = _each(jnp.exp, m - m_new)
            l, acc = l * w, acc * w
            for keys, sc, valid in work:
                p = _each(jnp.exp, sc - m_new)
                if diagonal:
                    p = _keep(valid, p)
                l = l + _rowsum(p)
                acc = acc + _each(_split_dot, p, _heads_of(v_ref, keys))
            return tuple(m_new.vals), tuple(l.vals), tuple(acc.vals)

        per_head = lambda width, value: tuple(jnp.full((ATT_TQ, width), value, F32) for _ in range(ATT_HEADS))
        init = (per_head(1, -1e30), per_head(1, 0.0), per_head(HEAD_DIM, 0.0))
        m, l, acc = lax.fori_loop(0, i, lambda jb, cr: step(jb, cr, False), step(i, init, True))
        for h, cs in enumerate(_HEAD_COLS):
            o_ref[:, cs] = acc[h] / l[h]
            lse_ref[h] = jnp.broadcast_to(m[h] + jnp.log(l[h]), (ATT_TQ, LANES))

    q_spec, k_spec, v_spec = _att_specs(n_heads, s)
    return pl.pallas_call(
        body, name=name, grid=(n_heads // ATT_HEADS, s // ATT_TQ),
        in_specs=[q_spec, k_spec, v_spec, pl.BlockSpec((ATT_TQ, LANES), lambda g, i: (i, 0)),
                  pl.BlockSpec((ATT_HEADS, 1, s), lambda g, i: (g, 0, 0))],
        out_specs=[pl.BlockSpec((ATT_TQ, ATT_WIDTH), lambda g, i: (i, g)),
                   pl.BlockSpec((ATT_HEADS, ATT_TQ, LANES), lambda g, i: (g, i, 0))],
        out_shape=[jax.ShapeDtypeStruct((s, n_heads * HEAD_DIM), F32),
                   jax.ShapeDtypeStruct((n_heads, s, LANES), F32)],
        compiler_params=_cparams("parallel", "arbitrary"),
    )(qkv, qkv, qkv, c, ct)


def _fox_bwd(qkv, c, ct, o, lse, do, name):
    s = qkv.shape[0]
    n_heads = N_FOX_HEADS

    def body(q_ref, k_ref, v_ref, c_ref, ct_ref, o_ref, lse_ref, do_ref, dq_ref, dk_ref, dv_ref, dct_ref):
        g, i = pl.program_id(0), pl.program_id(1)

        @pl.when(i == 0)
        def _():
            dk_ref[...] = jnp.zeros_like(dk_ref)
            dv_ref[...] = jnp.zeros_like(dv_ref)
            dct_ref[...] = jnp.zeros_like(dct_ref)

        q = _heads_of(q_ref)
        do16 = _each(lambda x: x.astype(BF16), _heads_of(do_ref))
        delta = _rowsum(_each(lambda a, b: a.astype(F32) * b, do16, _heads_of(o_ref)))
        lse_col = _Each(lse_ref[h, :, 0:1] for h in range(ATT_HEADS))
        cq = _fox_cq(c_ref, g)
        row, col, _, _ = _att_iotas()

        def step(jb, dq, diagonal):
            dq = _Each(dq)
            for sub in range(ATT_SUB):
                keys = pl.ds(pl.multiple_of(jb * ATT_TQ + sub * ATT_TK, ATT_TK), ATT_TK)
                k = _heads_of(k_ref, keys)
                sc = _fox_logits(q, k, cq, ct_ref, keys)
                if diagonal:
                    valid = (col + sub * ATT_TK) <= row
                    p = _keep(valid, _each(jnp.exp, _keep(valid, sc) - lse_col))
                else:
                    p = _each(jnp.exp, sc - lse_col)
                ds = p * (_dot_each(do16, _heads_of(v_ref, keys), "nt") - delta)
                dss = ds * ATT_SCALE
                dk, dv = _dot_each(dss, q, "tn"), _dot_each(p, do16, "tn")
                for h, cs in enumerate(_HEAD_COLS):
                    dct_ref[h, :, keys] -= jnp.sum(ds.vals[h], axis=0, keepdims=True)
                    dk_ref[keys, cs] += dk.vals[h]
                    dv_ref[keys, cs] += dv.vals[h]
                dq = dq + _dot_each(dss, k, "nn")
            return tuple(dq.vals)

        dq0 = step(i, tuple(jnp.zeros((ATT_TQ, HEAD_DIM), F32) for _ in range(ATT_HEADS)), True)
        dq = lax.fori_loop(0, i, lambda jb, dq: step(jb, dq, False), dq0)
        for cs, dq_h in zip(_HEAD_COLS, dq):
            dq_ref[:, cs] = dq_h

    q_spec, k_spec, v_spec = _att_specs(n_heads, s)
    blk = pl.BlockSpec((ATT_TQ, ATT_WIDTH), lambda g, i: (i, g))
    full = pl.BlockSpec((s, ATT_WIDTH), lambda g, i: (0, g))
    wide = jax.ShapeDtypeStruct((s, n_heads * HEAD_DIM), F32)
    return pl.pallas_call(
        body, name=name, grid=(n_heads // ATT_HEADS, s // ATT_TQ),
        in_specs=[q_spec, k_spec, v_spec, pl.BlockSpec((ATT_TQ, LANES), lambda g, i: (i, 0)),
                  pl.BlockSpec((ATT_HEADS, 1, s), lambda g, i: (g, 0, 0)), blk,
                  pl.BlockSpec((ATT_HEADS, ATT_TQ, LANES), lambda g, i: (g, i, 0)), blk],
        out_specs=[blk, full, full, pl.BlockSpec((ATT_HEADS, 1, s), lambda g, i: (g, 0, 0))],
        out_shape=[wide, wide, wide, jax.ShapeDtypeStruct((n_heads, 1, s), F32)],
        compiler_params=_cparams("parallel", "arbitrary"),
    )(qkv, qkv, qkv, c, ct, o, lse, do)


def _cumsum_rows(x, reverse, name):
    s = x.shape[0]
    nb = s // LANES

    def body(x_ref, o_ref):
        r = lax.broadcasted_iota(jnp.int32, (LANES, LANES), 0)
        c = lax.broadcasted_iota(jnp.int32, (LANES, LANES), 1)
        tri = ((r <= c) if reverse else (r >= c)).astype(F32)

        def step(it, carry):
            b = (nb - 1 - it) if reverse else it
            off = pl.multiple_of(b * LANES, LANES)
            blk = x_ref[pl.ds(off, LANES), :]
            o_ref[pl.ds(off, LANES), :] = _dot32(tri, blk) + carry
            return carry + jnp.sum(blk, axis=0, keepdims=True)

        lax.fori_loop(0, nb, step, jnp.zeros((1, LANES), F32))

    return pl.pallas_call(body, name=name, out_shape=jax.ShapeDtypeStruct(x.shape, F32),
                          compiler_params=pltpu.CompilerParams(vmem_limit_bytes=V7X_VMEM_LIMIT))(x)


def _dot32_each(a, b, kind="nn"):
    return _each(lambda x, y: _dot32(x, y, kind), a, b)


def _unit_lower_inverse(m, ri, ci):
    c = ri.shape[0]
    t = -_keep(ri // 2 == ci // 2, m) + jnp.where(ri == ci, 1.0, 0.0)
    b = 4
    while b <= c:
        off_diag = (ri // b == ci // b) & (ri % b >= b // 2) & (ci % b < b // 2)
        t = t - _dot32_each(_dot32_each(t, _keep(off_diag, m)), t)
        b *= 2
    return t


def _dn_gates(g, ri, ci):
    eye = ri == ci
    incl = ri >= ci
    g_row = jnp.sum(jnp.where(eye, g, 0.0), axis=0, keepdims=True)
    gc = jnp.sum(jnp.where(incl, g_row, 0.0), axis=1, keepdims=True)
    gc_row = jnp.sum(jnp.where(eye, gc, 0.0), axis=0, keepdims=True)
    dmat = jnp.where(incl, jnp.exp(jnp.where(incl, gc - gc_row, 0.0)), 0.0)
    gc_last = jnp.sum(g, axis=0, keepdims=True)
    return gc, dmat, jnp.exp(gc), jnp.exp(gc_last - gc), jnp.exp(gc_last)


def _dn_fwd(qkv, act, name):
    s = qkv.shape[0]
    c, d, nh = DN_CHUNK, HEAD_DIM, N_DN_HEADS
    nc = s // c

    def body(q_ref, k_ref, v_ref, act_ref, o_ref, s_ref, t_ref, state):
        @pl.when(pl.program_id(0) == 0)
        def _():
            state[...] = jnp.zeros_like(state)

        ri = lax.broadcasted_iota(jnp.int32, (c, c), 0)
        ci = lax.broadcasted_iota(jnp.int32, (c, c), 1)
        act = act_ref[...]
        heads = range(nh)
        cols = [slice(h * d, (h + 1) * d) for h in heads]
        q, k, v = (_Each(ref[:, cs] for cs in cols) for ref in (q_ref, k_ref, v_ref))
        beta = _Each(_lane_col(act, LANE_BETA + h) for h in heads)
        g = _Each(_lane_col(act, LANE_DECAY + h) for h in heads)
        _, dmat, e, r, gl = _each(lambda gh: _dn_gates(gh, ri, ci), g)
        s0 = _Each(state[h] for h in heads)
        kb = beta * k
        t = _unit_lower_inverse(_keep(ri > ci, _dot32_each(kb, k, "nt") * dmat), ri, ci)
        vn = _dot32_each(t, beta * v) - _dot32_each(_dot32_each(t, kb * e), s0)
        o = _dot32_each(q * e, s0) + _dot32_each(_dot32_each(q, k, "nt") * dmat, vn)
        s1 = s0 * gl + _dot32_each(k * r, vn, "tn")
        for h in heads:
            o_ref[:, cols[h]] = o.vals[h]
            state[h] = s1.vals[h]
            s_ref[h] = s0.vals[h]
            t_ref[h] = t.vals[h]

    wide = lambda part: pl.BlockSpec((c, nh * d), lambda n: (n, part))
    return pl.pallas_call(
        body, name=name, grid=(nc,),
        in_specs=[wide(0), wide(1), wide(2), pl.BlockSpec((c, LANES), lambda n: (n, 0))],
        out_specs=[wide(0), pl.BlockSpec((nh, None, d, d), lambda n: (0, n, 0, 0)),
                   pl.BlockSpec((nh, None, c, c), lambda n: (0, n, 0, 0))],
        out_shape=[jax.ShapeDtypeStruct((s, nh * d), F32), jax.ShapeDtypeStruct((nh, nc, d, d), F32),
                   jax.ShapeDtypeStruct((nh, nc, c, c), F32)],
        scratch_shapes=[pltpu.VMEM((nh, d, d), F32)],
        compiler_params=_cparams("arbitrary"),
    )(qkv, qkv, qkv, act)


def _dn_bwd(qkv, act, states, tinv, do, name):
    s = qkv.shape[0]
    c, d, nh = DN_CHUNK, HEAD_DIM, N_DN_HEADS
    nc = s // c

    def chunk_bwd(q, k, v, do, beta, g, s0, t, ds_out):
        ri = lax.broadcasted_iota(jnp.int32, (c, c), 0)
        ci = lax.broadcasted_iota(jnp.int32, (c, c), 1)
        eye, incl, strict = ri == ci, ri >= ci, ri > ci
        gc, dmat, e, r, gl = _each(lambda gh: _dn_gates(gh, ri, ci), g)
        dot = _dot32_each
        rowsum = lambda x: _each(lambda a: jnp.sum(a, axis=1, keepdims=True), x)
        colsum = lambda x: _each(lambda a: jnp.sum(a, axis=0, keepdims=True), x)
        total = lambda x: colsum(rowsum(x))
        to_col = lambda row: rowsum(_keep(eye, row))
        to_row = lambda colv: colsum(_keep(eye, colv))

        kb, vb = beta * k, beta * v
        kbe = kb * e
        u, w = dot(t, vb), dot(t, kbe)
        vn = u - dot(w, s0)
        qk = dot(q, k, "nt")
        p = qk * dmat
        gram = dot(k, k, "nt")
        kr, qe = k * r, q * e

        d_kr = dot(vn, ds_out, "nt")
        dvn = dot(kr, ds_out)
        dgl = total(s0 * ds_out)
        ds_in = ds_out * gl
        dk = d_kr * r
        dr = rowsum(d_kr * k)
        d_qe = dot(do, s0, "nt")
        ds_in = ds_in + dot(qe, do, "tn")
        dp = _keep(incl, dot(do, vn, "nt"))
        dvn = dvn + dot(p, do, "tn")
        dq = d_qe * e
        de = rowsum(d_qe * q)
        dqk = dp * dmat
        dq = dq + dot(dqk, k)
        dk = dk + dot(dqk, q, "tn")
        dd = dp * qk
        dw = -dot(dvn, s0, "nt")
        ds_in = ds_in - dot(w, dvn, "tn")
        dvb = dot(t, dvn, "tn")
        dkbe = dot(t, dw, "tn")
        dm = -_keep(strict, dot(dvb, u, "nt") + dot(dkbe, w, "nt"))
        dbeta = rowsum(dm * gram * dmat)
        dgram = dm * beta * dmat
        dd = dd + dm * beta * gram
        dk = dk + dot(dgram, k) + dot(dgram, k, "tn")
        dkb = dkbe * e
        de = de + rowsum(dkbe * kb)
        dk = dk + beta * dkb
        dbeta = dbeta + rowsum(dkb * k) + rowsum(dvb * v)
        dv = beta * dvb
        wd = dd * dmat
        dgc = rowsum(wd) - to_col(colsum(wd)) + de * e - dr * r
        dgc_last = total(dr * r) + dgl * gl
        dgc = dgc + _keep(ri[:, 0:1] == c - 1, dgc_last)
        dg = rowsum(_keep(ri <= ci, to_row(dgc)))
        return dq, dk, dv, dbeta, dg, ds_in

    def body(q_ref, k_ref, v_ref, act_ref, s_ref, t_ref, do_ref, dq_ref, dk_ref, dv_ref, dact_ref, dstate):
        @pl.when(pl.program_id(0) == 0)
        def _():
            dstate[...] = jnp.zeros_like(dstate)

        act = act_ref[...]
        heads = range(nh)
        cols = [slice(h * d, (h + 1) * d) for h in heads]
        q, k, v, do = (_Each(ref[:, cs] for cs in cols) for ref in (q_ref, k_ref, v_ref, do_ref))
        dq, dk, dv, dbeta, dg, ds_in = chunk_bwd(
            q, k, v, do, _Each(_lane_col(act, LANE_BETA + h) for h in heads),
            _Each(_lane_col(act, LANE_DECAY + h) for h in heads), _Each(s_ref[h] for h in heads),
            _Each(t_ref[h] for h in heads), _Each(dstate[h] for h in heads))
        lane = lax.broadcasted_iota(jnp.int32, (c, LANES), 1)
        dact = jnp.zeros((c, LANES), F32)
        for h in heads:
            dstate[h] = ds_in.vals[h]
            dq_ref[:, cols[h]], dk_ref[:, cols[h]], dv_ref[:, cols[h]] = dq.vals[h], dk.vals[h], dv.vals[h]
            dact = (dact + jnp.where(lane == LANE_BETA + h, dbeta.vals[h], 0.0)
                    + jnp.where(lane == LANE_DECAY + h, dg.vals[h], 0.0))
        dact_ref[...] = dact

    part = lambda p: pl.BlockSpec((c, nh * d), lambda n: (nc - 1 - n, p))
    per = lambda a, b: pl.BlockSpec((nh, None, a, b), lambda n: (0, nc - 1 - n, 0, 0))
    wide = jax.ShapeDtypeStruct((s, nh * d), F32)
    act_spec = pl.BlockSpec((c, LANES), lambda n: (nc - 1 - n, 0))
    return pl.pallas_call(
        body, name=name, grid=(nc,),
        in_specs=[part(0), part(1), part(2), act_spec, per(d, d), per(c, c), part(0)],
        out_specs=[part(0), part(0), part(0), act_spec],
        out_shape=[wide, wide, wide, jax.ShapeDtypeStruct((s, LANES), F32)],
        scratch_shapes=[pltpu.VMEM((nh, d, d), F32)],
        compiler_params=_cparams("arbitrary"),
    )(qkv, qkv, qkv, act, states, tinv, do)


EVEN_DN_QKV, EVEN_FOX_QKV, EVEN_DN_GATE, EVEN_FOX_GATE, EVEN_NARROW = 0, 1536, 3072, 3584, 4096
EVEN_WIDTH = 4224
CONV_TILE = 256
CONV_HALO = 8


def _conv_fwd(proj, w, name):
    s = proj.shape[0]
    t, cw = CONV_TILE, 3 * D_DN

    def body(cur_ref, prev_ref, w_ref, y_ref, xs):
        i = pl.program_id(0)
        xs[0:CONV_HALO, :] = jnp.where(i > 0, prev_ref[...], 0.0)
        xs[CONV_HALO:, :] = cur_ref[...]
        y = jnp.zeros((t, cw), F32)
        for tap in range(CONV_WIDTH):
            y = y + w_ref[tap:tap + 1, :] * xs[pl.ds(CONV_HALO - CONV_WIDTH + 1 + tap, t), :]
        y_ref[...] = y

    per = t // CONV_HALO
    return pl.pallas_call(
        body, name=name, grid=(s // t,),
        in_specs=[pl.BlockSpec((t, cw), lambda i: (i, 0)),
                  pl.BlockSpec((CONV_HALO, cw), lambda i: (jnp.maximum(i * per - 1, 0), 0)),
                  pl.BlockSpec((CONV_WIDTH, cw), lambda i: (0, 0))],
        out_specs=pl.BlockSpec((t, cw), lambda i: (i, 0)),
        out_shape=jax.ShapeDtypeStruct((s, cw), F32),
        scratch_shapes=[pltpu.VMEM((t + CONV_HALO, cw), F32)],
        compiler_params=_cparams("parallel"),
    )(proj, proj, w)


def _conv_bwd(proj, w, dy, name):
    s = proj.shape[0]
    t, cw = CONV_TILE, 3 * D_DN
    nt = s // t

    def body(cur_ref, prev_ref, w_ref, dy_ref, nxt_ref, dx_ref, dw_ref, xs, dys):
        i = pl.program_id(0)

        @pl.when(i == 0)
        def _():
            dw_ref[...] = jnp.zeros_like(dw_ref)

        xs[0:CONV_HALO, :] = jnp.where(i > 0, prev_ref[...], 0.0)
        xs[CONV_HALO:, :] = cur_ref[...]
        dys[0:t, :] = dy_ref[...]
        dys[t:, :] = jnp.where(i < nt - 1, nxt_ref[...], 0.0)
        dy = dy_ref[...]
        dx = jnp.zeros((t, cw), F32)
        for tap in range(CONV_WIDTH):
            dx = dx + w_ref[tap:tap + 1, :] * dys[pl.ds(CONV_WIDTH - 1 - tap, t), :]
            dw_ref[tap:tap + 1, :] += jnp.sum(dy * xs[pl.ds(CONV_HALO - CONV_WIDTH + 1 + tap, t), :], axis=0,
                                              keepdims=True)
        dx_ref[...] = dx.astype(BF16)

    per = t // CONV_HALO
    last = s // CONV_HALO - 1
    return pl.pallas_call(
        body, name=name, grid=(nt,),
        in_specs=[pl.BlockSpec((t, cw), lambda i: (i, 0)),
                  pl.BlockSpec((CONV_HALO, cw), lambda i: (jnp.maximum(i * per - 1, 0), 0)),
                  pl.BlockSpec((CONV_WIDTH, cw), lambda i: (0, 0)),
                  pl.BlockSpec((t, cw), lambda i: (i, 0)),
                  pl.BlockSpec((CONV_HALO, cw), lambda i: (jnp.minimum((i + 1) * per, last), 0))],
        out_specs=[pl.BlockSpec((t, cw), lambda i: (i, 0)), pl.BlockSpec((CONV_WIDTH, cw), lambda i: (0, 0))],
        out_shape=[jax.ShapeDtypeStruct((s, cw), BF16), jax.ShapeDtypeStruct((CONV_WIDTH, cw), F32)],
        scratch_shapes=[pltpu.VMEM((t + CONV_HALO, cw), F32), pltpu.VMEM((t + CONV_HALO, cw), F32)],
        compiler_params=_cparams("arbitrary"),
    )(proj, proj, w, dy, dy)


def _heads(x, n):
    return [x[:, HEAD_DIM * h:HEAD_DIM * (h + 1)] for h in range(n)]


def _dn_pre_fwd(y, name):
    def fn(yb):
        cs = yb * _sigmoid(yb)
        out = []
        for idx, xh in enumerate(_heads(cs, 3 * N_DN_HEADS)):
            if idx < 2 * N_DN_HEADS:
                xh = xh * lax.rsqrt(jnp.sum(xh * xh, axis=-1, keepdims=True) + EPS)
                if idx < N_DN_HEADS:
                    xh = xh * ATT_SCALE
            out.append(xh)
        return (jnp.concatenate(out, axis=1),)
    return _rowwise(fn, [y], [], [(y.shape[1], F32)], [], tile=256, name=name)[0]


def _dn_pre_bwd(y, dq, dk, dv, name):
    def fn(yb, dqb, dkb, dvb):
        sg = _sigmoid(yb)
        cs = yb * sg
        dout = _heads(dqb, N_DN_HEADS) + _heads(dkb, N_DN_HEADS) + _heads(dvb, N_DN_HEADS)
        dcs = []
        for idx, (xh, dh) in enumerate(zip(_heads(cs, 3 * N_DN_HEADS), dout)):
            if idx < 2 * N_DN_HEADS:
                if idx < N_DN_HEADS:
                    dh = dh * ATT_SCALE
                r = lax.rsqrt(jnp.sum(xh * xh, axis=-1, keepdims=True) + EPS)
                xhat = xh * r
                dh = r * (dh - xhat * jnp.sum(xhat * dh, axis=-1, keepdims=True))
            dcs.append(dh)
        return (jnp.concatenate(dcs, axis=1) * _silu_grad(yb, sg),)
    return _rowwise(fn, [y, dq, dk, dv], [], [(y.shape[1], F32)], [], tile=256, name=name)[0]


def _narrow_params(a_log, dt_bias, f_bias):
    lanes = lambda a, first: jnp.pad(a.reshape(1, -1), ((0, 0), (first, LANES - first - a.shape[0])))
    return jnp.concatenate([lanes(a_log, LANE_DECAY), lanes(dt_bias, LANE_DECAY), lanes(f_bias, LANE_FORGET),
                            jnp.zeros((5, LANES), F32)], axis=0)


def _narrow_masks(shape):
    lane = lax.broadcasted_iota(jnp.int32, shape, 1)
    is_beta = lane < LANE_DECAY
    is_decay = (lane >= LANE_DECAY) & (lane < LANE_FORGET)
    is_forget = (lane >= LANE_FORGET) & (lane < LANE_FORGET + N_FOX_HEADS)
    return is_beta, is_decay, is_forget


def _narrow_fwd(proj, params, name):
    def fn(sm, pk):
        is_beta, is_decay, is_forget = _narrow_masks(sm.shape)
        g = -jnp.exp(pk[0:1, :]) * _softplus(sm + pk[1:2, :])
        logf = -_softplus(-(sm + pk[2:3, :]))
        return (jnp.where(is_beta, _sigmoid(sm), jnp.where(is_decay, g, jnp.where(is_forget, logf, 0.0))),)
    return _rowwise(fn, [(proj, LANES, EVEN_NARROW // LANES)], [params], [(LANES, F32)], [], tile=512, name=name)[0]


def _narrow_bwd(proj, params, act, dact, dlogf, name):
    def fn(sm, ab, da, dl, pk):
        is_beta, is_decay, is_forget = _narrow_masks(sm.shape)
        db = jnp.where(is_forget, dl, da)
        d_beta = db * ab * (1.0 - ab)
        d_decay = db * (-jnp.exp(pk[0:1, :])) * _sigmoid(sm + pk[1:2, :])
        d_forget = db * _sigmoid(-(sm + pk[2:3, :]))
        dsm = jnp.where(is_beta, d_beta, jnp.where(is_decay, d_decay, jnp.where(is_forget, d_forget, 0.0)))
        col = lambda x: jnp.sum(x, axis=0, keepdims=True)
        return (dsm, col(jnp.where(is_decay, db * ab, 0.0)), col(jnp.where(is_decay, dsm, 0.0)),
                col(jnp.where(is_forget, dsm, 0.0)))
    return _rowwise(fn, [(proj, LANES, EVEN_NARROW // LANES), act, dact, dlogf], [params], [(LANES, BF16)],
                    [(1, LANES)] * 3, tile=512, name=name)


def _head_rms(xh):
    r = lax.rsqrt(jnp.mean(xh * xh, axis=-1, keepdims=True) + EPS)
    return xh * r, r


def _fox_pre_fwd(proj, qg, kg, name):
    def fn(pf, qgb, kgb):
        out = []
        for idx, xh in enumerate(_heads(pf, 3 * N_FOX_HEADS)):
            if idx < 2 * N_FOX_HEADS:
                xh = _head_rms(xh)[0] * (qgb if idx < N_FOX_HEADS else kgb)
            out.append(xh)
        return (jnp.concatenate(out, axis=1),)
    return _rowwise(fn, [(proj, 3 * D_FOX, EVEN_FOX_QKV // (3 * D_FOX))], [qg, kg], [(3 * D_FOX, BF16)], [],
                    tile=256, name=name)[0]


def _fox_pre_bwd(proj, qg, kg, dq, dk, dv, name):
    def fn(pf, dqb, dkb, dvb, qgb, kgb):
        dout = _heads(dqb, N_FOX_HEADS) + _heads(dkb, N_FOX_HEADS) + _heads(dvb, N_FOX_HEADS)
        dg = [jnp.zeros((1, HEAD_DIM), F32), jnp.zeros((1, HEAD_DIM), F32)]
        dx = []
        for idx, (xh, dh) in enumerate(zip(_heads(pf, 3 * N_FOX_HEADS), dout)):
            if idx < 2 * N_FOX_HEADS:
                which = 0 if idx < N_FOX_HEADS else 1
                xhat, r = _head_rms(xh)
                dg[which] = dg[which] + jnp.sum(dh * xhat, axis=0, keepdims=True)
                dxh = dh * (qgb if which == 0 else kgb)
                dh = r * (dxh - xhat * jnp.mean(dxh * xhat, axis=-1, keepdims=True))
            dx.append(dh)
        return jnp.concatenate(dx, axis=1), dg[0], dg[1]
    return _rowwise(fn, [(proj, 3 * D_FOX, EVEN_FOX_QKV // (3 * D_FOX)), dq, dk, dv], [qg, kg],
                    [(3 * D_FOX, BF16)], [(1, HEAD_DIM)] * 2, tile=256, name=name)


def _mix_gate_fwd(proj, o_dn, o_fox, ng, name):
    def fn(gd, gf, od, of, ngb):
        dn = [_head_rms(xh)[0] * ngb for xh in _heads(od, N_DN_HEADS)]
        return (jnp.concatenate([jnp.concatenate(dn, axis=1) * gd * _sigmoid(gd), of * _sigmoid(gf)], axis=1),)
    return _rowwise(fn, [(proj, D_DN, EVEN_DN_GATE // D_DN), (proj, D_FOX, EVEN_FOX_GATE // D_FOX), o_dn, o_fox],
                    [ng], [(D_DN + D_FOX, BF16)], [], tile=256, name=name)[0]


def _mix_gate_bwd(proj, o_dn, o_fox, ng, dom, name):
    def fn(gd, gf, od, of, dm, ngb):
        d_dn, d_fox = dm[:, :D_DN], dm[:, D_DN:]
        sgd, sgf = _sigmoid(gd), _sigmoid(gf)
        don = d_dn * gd * sgd
        dng = jnp.zeros((1, HEAD_DIM), F32)
        dod, normed = [], []
        for xh, dh in zip(_heads(od, N_DN_HEADS), _heads(don, N_DN_HEADS)):
            xhat, r = _head_rms(xh)
            dng = dng + jnp.sum(dh * xhat, axis=0, keepdims=True)
            dxh = dh * ngb
            dod.append(r * (dxh - xhat * jnp.mean(dxh * xhat, axis=-1, keepdims=True)))
            normed.append(xhat * ngb)
        d_gd = d_dn * jnp.concatenate(normed, axis=1) * _silu_grad(gd, sgd)
        d_gf = d_fox * of * sgf * (1.0 - sgf)
        return jnp.concatenate(dod, axis=1), d_fox * sgf, d_gd, d_gf, dng
    return _rowwise(fn, [(proj, D_DN, EVEN_DN_GATE // D_DN), (proj, D_FOX, EVEN_FOX_GATE // D_FOX), o_dn, o_fox, dom],
                    [ng], [(D_DN, F32), (D_FOX, F32), (D_DN, BF16), (D_FOX, BF16)], [(1, HEAD_DIM)], tile=256,
                    name=name)


def _loss_grad(y, target, name):
    d = y.shape[1]

    def fn(yb, tb):
        diff = yb - tb
        part = jnp.sum(jnp.sum(diff * diff, axis=1, keepdims=True), axis=0, keepdims=True) * (0.5 / d)
        g = diff * (1.0 / d)
        return g, g, part
    return _rowwise(fn, [y, target], [], [(d, F32), (d, BF16)], [(1, 1)], tile=512, name=name)


_REF_EVEN = {"dn_qkv": (0, 1536), "dn_gate": (1536, 2048), "dn_ba": (2048, 2056), "fox_qkv": (2056, 3592),
             "fox_gate": (3592, 4104), "f_pre": (4104, 4108)}
D_IN_EVEN = 4108


def _even_to_kernel_layout(w):
    cut = lambda name: w[..., _REF_EVEN[name][0]:_REF_EVEN[name][1]]
    pad = jnp.zeros(w.shape[:-1] + (EVEN_WIDTH - EVEN_NARROW - 12,), w.dtype)
    return jnp.concatenate([cut("dn_qkv"), cut("fox_qkv"), cut("dn_gate"), cut("fox_gate"), cut("dn_ba"),
                            cut("f_pre"), pad], axis=-1)


def _even_from_kernel_layout(g):
    return jnp.concatenate([g[..., EVEN_DN_QKV:EVEN_FOX_QKV], g[..., EVEN_DN_GATE:EVEN_FOX_GATE],
                            g[..., EVEN_NARROW:EVEN_NARROW + 8], g[..., EVEN_FOX_QKV:EVEN_DN_GATE],
                            g[..., EVEN_FOX_GATE:EVEN_NARROW], g[..., EVEN_NARROW + 8:EVEN_NARROW + 12]], axis=-1)


EVEN_QUARTER = 1027
EVEN_QUARTER_PAD = 1152


def _even_grad_quarters(g):
    g = _even_from_kernel_layout(g)
    pad = [(0, 0)] * (g.ndim - 1) + [(0, EVEN_QUARTER_PAD - EVEN_QUARTER)]
    return jnp.concatenate([jnp.pad(g[..., q * EVEN_QUARTER:(q + 1) * EVEN_QUARTER], pad) for q in range(4)], axis=-1)


def _forget_rows(c):
    return c[:, LANE_FORGET:LANE_FORGET + N_FOX_HEADS].T.reshape(N_FOX_HEADS, 1, c.shape[0])


def _forget_lanes(rows):
    s = rows.shape[2]
    return jnp.pad(rows.reshape(-1, s).T, ((0, 0), (LANE_FORGET, LANES - LANE_FORGET - N_FOX_HEADS)))


def _even_fwd(x, gain, w_in, w_out, j, p, tag):
    h = _rms_fwd(x, gain, f"{tag}_norm")
    proj = _mm(h, w_in, "nn", tm=512, tn=EVEN_WIDTH // 3, out_dtype=F32, name=f"{tag}_in", b_lead=(j,))
    y = _conv_fwd(proj, p["conv_w"], f"{tag}_conv")
    dn_qkv = _dn_pre_fwd(y, f"{tag}_dn_pre")
    act = _narrow_fwd(proj, p["narrow"], f"{tag}_narrow")
    o_dn, states, tinv = _dn_fwd(dn_qkv, act, f"{tag}_delta")
    fox_qkv = _fox_pre_fwd(proj, p["q_g"], p["k_g"], f"{tag}_fox_pre")
    c = _cumsum_rows(act, False, f"{tag}_cumsum")
    ct = _forget_rows(c)
    o_fox, lse = _fox_fwd(fox_qkv, c, ct, f"{tag}_fox")
    om = _mix_gate_fwd(proj, o_dn, o_fox, p["dn_norm_g"], f"{tag}_gate")
    x2 = _mm(om, w_out, "nn", tm=512, tn=x.shape[1], out_dtype=F32, name=f"{tag}_out", residual=x, b_lead=(j,))
    return x2, (x, h, proj, y, dn_qkv, act, states, tinv, o_dn, fox_qkv, c, ct, o_fox, lse, om)


def _even_bwd(dxo, dxo16, saved, gain, w_in, w_out, j, p, tag, g_in, g_out):
    x, h, proj, y, dn_qkv, act, states, tinv, o_dn, fox_qkv, c, ct, o_fox, lse, om = saved
    d = x.shape[1]
    dom = _mm(dxo16, w_out, "nt", tm=512, tn=d, out_dtype=F32, name=f"{tag}_out_bwd", b_lead=(j,))
    g_out = _mm(om, dxo16, "tn", tm=512, tn=d, out_dtype=F32, name=f"{tag}_out_dw", into=(g_out, 0))
    d_odn, d_ofox, d_gd, d_gf, d_ng = _mix_gate_bwd(proj, o_dn, o_fox, p["dn_norm_g"], dom, f"{tag}_gate_bwd")
    dq, dk, dv, dct = _fox_bwd(fox_qkv, c, ct, o_fox, lse, d_ofox, f"{tag}_fox_bwd")
    d_fox_qkv, d_qg, d_kg = _fox_pre_bwd(proj, p["q_g"], p["k_g"], dq, dk, dv, f"{tag}_fox_pre_bwd")
    dlogf = _cumsum_rows(_forget_lanes(dct), True, f"{tag}_cumsum_bwd")
    dq, dk, dv, dact = _dn_bwd(dn_qkv, act, states, tinv, d_odn, f"{tag}_delta_bwd")
    dy = _dn_pre_bwd(y, dq, dk, dv, f"{tag}_dn_pre_bwd")
    d_dn_qkv, d_conv = _conv_bwd(proj, p["conv_w"], dy, f"{tag}_conv_bwd")
    d_narrow, s_alog, s_dt, s_fb = _narrow_bwd(proj, p["narrow"], act, dact, dlogf, f"{tag}_narrow_bwd")
    dproj = jnp.concatenate([d_dn_qkv, d_fox_qkv, d_gd, d_gf, d_narrow], axis=1)
    dh = _mm(dproj, w_in, "nt", tm=512, tn=d, out_dtype=F32, name=f"{tag}_in_bwd", b_lead=(j,))
    g_in = _mm(h, dproj, "tn", tm=512, tn=EVEN_WIDTH // 3, out_dtype=F32, name=f"{tag}_in_dw", into=(g_in, 0))
    dx, dx16, d_gain = _rms_bwd(x, gain, dh, dxo, f"{tag}_norm_bwd")
    small = {"conv_w": d_conv, "a_log": s_alog, "dt_bias": s_dt, "f_bias": s_fb, "dn_norm_g": d_ng, "q_g": d_qg,
             "k_g": d_kg}
    return dx, dx16, d_gain, small, g_in, g_out


def _odd_fwd(x, gain, w_in, w_out, j, tag):
    h = _rms_fwd(x, gain, f"{tag}_norm")
    qkv = _mm(h, w_in, "nn", tm=512, tn=w_in.shape[2] // 2, out_dtype=BF16, name=f"{tag}_in", b_lead=(j,))
    o16, o32 = _sb_fwd(qkv, N_SB_HEADS, f"{tag}_sb")
    x2 = _mm(o16, w_out, "nn", tm=512, tn=x.shape[1], out_dtype=F32, name=f"{tag}_out", residual=x, b_lead=(j,))
    return x2, (x, h, qkv, o16, o32)


def _odd_bwd(dxo, dxo16, saved, gain, w_in, w_out, j, tag, g_in, g_out):
    x, h, qkv, o16, o32 = saved
    d = x.shape[1]
    do = _mm(dxo16, w_out, "nt", tm=512, tn=d, out_dtype=BF16, name=f"{tag}_out_bwd", b_lead=(j,))
    g_out = _mm(o16, dxo16, "tn", tm=512, tn=d, out_dtype=F32, name=f"{tag}_out_dw", into=(g_out, 0))
    dq, dk, dv = _sb_bwd(qkv, o32, do, N_SB_HEADS, f"{tag}_sb_bwd")
    dqkv = jnp.concatenate([dq, dk.astype(BF16), dv.astype(BF16)], axis=1)
    dh = _mm(dqkv, w_in, "nt", tm=512, tn=d, out_dtype=F32, name=f"{tag}_in_bwd", b_lead=(j,))
    g_in = _mm(h, dqkv, "tn", tm=512, tn=w_in.shape[2] // 2, out_dtype=F32, name=f"{tag}_in_dw", into=(g_in, 0))
    dx, dx16, d_gain = _rms_bwd(x, gain, dh, dxo, f"{tag}_norm_bwd")
    return dx, dx16, d_gain, g_in, g_out


def _forward_backward(x, target, w, first, rest_after, token, on_reduced):
    depth = w["norm_ffn1"].shape[0]
    row = lambda a, l: a[l][None]
    rest = {}

    def mats(names, j):
        if j == 0 and names[0] in first:
            return [first[name] for name in names] + [0]
        return [rest[name] for name in names] + [j - (1 if names[0] in first else 0)]

    def even_small(j):
        return {"conv_w": w["dn_conv_w"][j], "narrow": _narrow_params(w["dn_a_log"][j], w["dn_dt_bias"][j],
                                                                     w["fox_f_bias"][j]),
                "dn_norm_g": row(w["dn_norm_g"], j), "q_g": row(w["fox_q_norm_g"], j),
                "k_g": row(w["fox_k_norm_g"], j)}

    saved = []
    for l in range(depth):
        if l == 1:
            rest.update(rest_after(x))
        gain = row(w["norm_ffn1"], l) + token[0:1, 0:1] if l == 0 else row(w["norm_ffn1"], l)
        x, s1 = _ffn_fwd(x, gain, *mats(("ffn1_w_gu", "ffn1_w_down"), l), "ffn1")
        if l % 2 == 0:
            x, s2 = _even_fwd(x, row(w["norm_mix"], l), *mats(("w_in_even", "w_out_even"), l // 2),
                              even_small(l // 2), "even")
        else:
            x, s2 = _odd_fwd(x, row(w["norm_mix"], l), *mats(("w_in_odd", "w_out_odd"), l // 2), "odd")
        x, s3 = _ffn_fwd(x, row(w["norm_ffn2"], l), *mats(("ffn2_w_gu", "ffn2_w_down"), l), "ffn2")
        saved.append((s1, s2, s3))

    dx, dx16, loss = _loss_grad(x, target, "loss")

    kind_of = dict(BIG)
    d_norm = {k: [None] * depth for k in ("norm_ffn1", "norm_mix", "norm_ffn2")}
    d_even = [None] * ((depth + 1) // 2)
    pending, token = None, None
    for l in reversed(range(depth)):
        s1, s2, s3 = saved[l]
        mixer = ("w_in_even", "w_out_even") if l % 2 == 0 else ("w_in_odd", "w_out_odd")
        names = ["ffn1_w_gu", "ffn1_w_down", *mixer, "ffn2_w_gu", "ffn2_w_down"]
        g = {name: lax.empty((1,) + rest[name].shape[1:], F32) for name in names}
        dx, dx16, d_norm["norm_ffn2"][l], g["ffn2_w_gu"], g["ffn2_w_down"] = _ffn_bwd(
            dx, dx16, s3, row(w["norm_ffn2"], l), *mats(("ffn2_w_gu", "ffn2_w_down"), l), "ffn2", g["ffn2_w_gu"],
            g["ffn2_w_down"], after=token)
        if l % 2 == 0:
            dx, dx16, d_norm["norm_mix"][l], d_even[l // 2], g["w_in_even"], g["w_out_even"] = _even_bwd(
                dx, dx16, s2, row(w["norm_mix"], l), *mats(("w_in_even", "w_out_even"), l // 2), even_small(l // 2),
                "even", g["w_in_even"], g["w_out_even"])
            g["w_in_even"] = _even_grad_quarters(g["w_in_even"])
        else:
            dx, dx16, d_norm["norm_mix"][l], g["w_in_odd"], g["w_out_odd"] = _odd_bwd(
                dx, dx16, s2, row(w["norm_mix"], l), *mats(("w_in_odd", "w_out_odd"), l // 2), "odd", g["w_in_odd"],
                g["w_out_odd"])
        dx, dx16, d_norm["norm_ffn1"][l], g["ffn1_w_gu"], g["ffn1_w_down"] = _ffn_bwd(
            dx, dx16, s1, row(w["norm_ffn1"], l), *mats(("ffn1_w_gu", "ffn1_w_down"), l), "ffn1", g["ffn1_w_gu"],
            g["ffn1_w_down"])
        above = pending
        pending, token = _reduce_start([g[name] for name in names], [kind_of[name] for name in names], names,
                                       f"layer{l}")
        if above is not None:
            on_reduced(l + 1, dict(zip(above[-2], _reduce_finish(above, dx))))
    on_reduced(0, dict(zip(pending[-2], _reduce_finish(pending, dx))))

    small = {k: jnp.concatenate(v, axis=0) for k, v in d_norm.items()}
    dec = slice(LANE_DECAY, LANE_DECAY + N_DN_HEADS)
    fgt = slice(LANE_FORGET, LANE_FORGET + N_FOX_HEADS)
    small["dn_conv_w"] = jnp.stack([e["conv_w"] for e in d_even])
    small["dn_a_log"] = jnp.concatenate([e["a_log"][:, dec] for e in d_even], axis=0)
    small["dn_dt_bias"] = jnp.concatenate([e["dt_bias"][:, dec] for e in d_even], axis=0)
    small["fox_f_bias"] = jnp.concatenate([e["f_bias"][:, fgt] for e in d_even], axis=0)
    small["dn_norm_g"] = jnp.concatenate([e["dn_norm_g"] for e in d_even], axis=0)
    small["fox_q_norm_g"] = jnp.concatenate([e["q_g"] for e in d_even], axis=0)
    small["fox_k_norm_g"] = jnp.concatenate([e["k_g"] for e in d_even], axis=0)
    return loss, dx, small


MESH = pl.DeviceIdType.MESH
ANY = pl.BlockSpec(memory_space=pl.ANY)


def _place():
    x, y, c = lax.axis_index("x"), lax.axis_index("y"), lax.axis_index("c")
    return x, y, c, [(1 - x, y), (x, 1 - y), (1 - x, 1 - y)]


def _remote(src, dst, send_sem, recv_sem, to):
    return pltpu.make_async_remote_copy(src_ref=src, dst_ref=dst, send_sem=send_sem, recv_sem=recv_sem,
                                        device_id=to, device_id_type=MESH)


def _aligned(start, multiple):
    return start if isinstance(start, int) else pl.multiple_of(start, multiple)


def _quarter(ref, kind, chip, half, rows, cols):
    k = 2 * chip[0] + chip[1]
    hr = rows // 2
    assert hr % 16 == 0 and cols % LANES == 0
    if kind == "col":
        return ref.at[:, pl.ds(_aligned(half * hr, 16), hr), pl.ds(_aligned(k * cols, LANES), cols)]
    return ref.at[:, pl.ds(_aligned(k * rows + half * hr, 16), hr), :]


def _place_quarter(shard, kind, kc, name, first=0, count=None):
    l, rows, cols = shard.shape
    l = l - first if count is None else count
    tr = rows
    while tr * cols * 4 > (2 << 20) and tr % 32 == 0:
        tr //= 2
    nr = rows // tr
    if kind == "col":
        out_spec = pl.BlockSpec((None, tr, cols), lambda li, i, kc_ref: (li, i, kc_ref[0]))
        out_shape = (l, rows, 4 * cols)
    else:
        out_spec = pl.BlockSpec((None, tr, cols), lambda li, i, kc_ref: (li, kc_ref[0] * nr + i, 0))
        out_shape = (l, 4 * rows, cols)

    def body(kc_ref, x_ref, o_ref):
        o_ref[...] = x_ref[...].astype(BF16)

    return pl.pallas_call(
        body, name=name,
        grid_spec=pltpu.PrefetchScalarGridSpec(
            num_scalar_prefetch=1, grid=(l, nr),
            in_specs=[pl.BlockSpec((None, tr, cols), lambda li, i, kc_ref: (li + first, i, 0))],
            out_specs=out_spec),
        out_shape=jax.ShapeDtypeStruct(out_shape, BF16),
        compiler_params=_cparams("parallel", "parallel"),
    )(kc, shard)


def _gather_weights(wholes, kinds):
    n = len(wholes)

    def dims(ref, kind):
        _, r, cc = ref.shape
        return (r, cc // 4) if kind == "col" else (r // 4, cc)

    def body(*refs):
        bufs = refs[n:2 * n]
        send_sems, recv_sems = refs[2 * n:]
        x, y, c, chips = _place()
        sibling = (x, y, 1 - c)
        first, passed = [], []
        for t in range(n):
            rows, cols = dims(bufs[t], kinds[t])
            mine = _quarter(bufs[t], kinds[t], (x, y), c, rows, cols)
            for j, chip in enumerate(chips):
                cp = _remote(mine, mine, send_sems.at[t, j], recv_sems.at[t, j], (*chip, c))
                cp.start()
                first.append(cp)
        for j, chip in enumerate(chips):
            for t in range(n):
                rows, cols = dims(bufs[t], kinds[t])
                got = _quarter(bufs[t], kinds[t], chip, c, rows, cols)
                _remote(got, got, send_sems.at[t, j], recv_sems.at[t, j], (*chip, c)).wait_recv()
                cp = _remote(got, got, send_sems.at[t, 3 + j], recv_sems.at[t, 3 + j], sibling)
                cp.start()
                passed.append(cp)
        for j, chip in enumerate(chips):
            for t in range(n):
                rows, cols = dims(bufs[t], kinds[t])
                got = _quarter(bufs[t], kinds[t], chip, 1 - c, rows, cols)
                _remote(got, got, send_sems.at[t, 3 + j], recv_sems.at[t, 3 + j], sibling).wait_recv()
        for cp in first + passed:
            cp.wait_send()

    return pl.pallas_call(
        body, name="gather_weights", in_specs=[ANY] * n, out_specs=[ANY] * n,
        out_shape=[jax.ShapeDtypeStruct(a.shape, a.dtype) for a in wholes],
        input_output_aliases={t: t for t in range(n)},
        scratch_shapes=[pltpu.SemaphoreType.DMA((n, 6)), pltpu.SemaphoreType.DMA((n, 6))],
        compiler_params=pltpu.CompilerParams(has_side_effects=True),
    )(*wholes)


def _quarter_dims(ref, kind):
    _, r, cc = ref.shape
    return (r, cc // 4) if kind == "col" else (r // 4, cc)


def _gather_chips_copies(bufs, sems, kinds):
    x, y, c, chips = _place()
    copies = []
    for t, buf in enumerate(bufs):
        rows, cols = _quarter_dims(buf, kinds[t])
        mine = _quarter(buf, kinds[t], (x, y), c, rows, cols)
        for j, chip in enumerate(chips):
            pair = 2 * (OTHER_CHIPS * t + j)
            copies.append(_remote(mine, mine, sems[pair], sems[pair + 1], (*chip, c)))
    return copies


def _gather_start(wholes, kinds, after, tag):
    n = len(wholes)
    n_sems = 2 * OTHER_CHIPS * n
    n_in = n + len(after)

    def body(*refs):
        for cp in _gather_chips_copies(refs[:n], refs[n_in + n:n_in + n + n_sems], kinds):
            cp.start()
        refs[-1][...] = jnp.zeros_like(refs[-1])

    held = [pltpu.with_memory_space_constraint(a, pltpu.HBM) for a in wholes]
    out = pl.pallas_call(
        body, name=f"gather_start_{tag}", in_specs=[HBM] * n + [ANY] * len(after),
        out_specs=(*[HBM] * n, *[SEM] * n_sems, pl.BlockSpec(memory_space=pltpu.VMEM)),
        out_shape=(*[pltpu.HBM(a.shape, a.dtype) for a in held], *[pltpu.SemaphoreType.DMA(())] * n_sems,
                   jax.ShapeDtypeStruct((8, LANES), F32)),
        input_output_aliases={i: i for i in range(n)},
        compiler_params=pltpu.CompilerParams(has_side_effects=SPLIT_COPY),
    )(*held, *after)
    return out[n:n + n_sems], out[:n], out[-1]


def _gather_wait(sems, wholes, kinds, after, tag):
    n = len(wholes)

    def body(*refs):
        for cp in _gather_chips_copies(refs[:n], refs[n:n + len(sems)], kinds):
            cp.wait_send()
            cp.wait_recv()

    return pl.pallas_call(
        body, name=f"gather_wait_{tag}", in_specs=[HBM] * n + [SEM] * len(sems) + [ANY],
        out_specs=tuple([HBM] * n), out_shape=tuple(pltpu.HBM(a.shape, a.dtype) for a in wholes),
        input_output_aliases={i: i for i in range(n)},
        compiler_params=pltpu.CompilerParams(has_side_effects=SPLIT_COPY),
    )(*wholes, *sems, after)


def _gather_forward(wholes, kinds, tag):
    n = len(wholes)

    def body(*refs):
        bufs = refs[n:2 * n]
        send_sems, recv_sems = refs[2 * n:]
        x, y, c, chips = _place()
        copies = []
        for t in range(n):
            rows, cols = _quarter_dims(bufs[t], kinds[t])
            for j, chip in enumerate(chips):
                got = _quarter(bufs[t], kinds[t], chip, c, rows, cols)
                cp = _remote(got, got, send_sems.at[t, j], recv_sems.at[t, j], (x, y, 1 - c))
                cp.start()
                copies.append(cp)
        for cp in copies:
            cp.wait_send()
        for t in range(n):
            rows, cols = _quarter_dims(bufs[t], kinds[t])
            for j, chip in enumerate(chips):
                got = _quarter(bufs[t], kinds[t], chip, 1 - c, rows, cols)
                _remote(got, got, send_sems.at[t, j], recv_sems.at[t, j], (x, y, 1 - c)).wait_recv()

    return pl.pallas_call(
        body, name=f"gather_forward_{tag}", in_specs=[ANY] * n, out_specs=[ANY] * n,
        out_shape=[jax.ShapeDtypeStruct(a.shape, a.dtype) for a in wholes],
        input_output_aliases={t: t for t in range(n)},
        scratch_shapes=[pltpu.SemaphoreType.DMA((n, OTHER_CHIPS)), pltpu.SemaphoreType.DMA((n, OTHER_CHIPS))],
        compiler_params=pltpu.CompilerParams(has_side_effects=True),
    )(*wholes)


def _canonical(a, kind):
    l, r, c = a.shape
    return a.reshape(l, 1, r, c) if kind == "col" else a.reshape(l, 4, r // 4, c)


def _rs_sibling(parts):
    n = len(parts)

    def body(*refs):
        ins, outs = refs[:n], refs[n:2 * n]
        send_sems, recv_sems = refs[2 * n:]
        x, y, c, _ = _place()
        copies = []
        for t in range(n):
            hr = ins[t].shape[2] // 2
            src = ins[t].at[:, :, pl.ds(pl.multiple_of((1 - c) * hr, 8), hr), :]
            cp = _remote(src, outs[t], send_sems.at[t], recv_sems.at[t], (x, y, 1 - c))
            cp.start()
            copies.append(cp)
        for cp in copies:
            cp.wait()

    half = lambda a: jax.ShapeDtypeStruct(a.shape[:2] + (a.shape[2] // 2, a.shape[3]), a.dtype)
    return pl.pallas_call(
        body, name="reduce_sibling", in_specs=[ANY] * n, out_specs=[ANY] * n, out_shape=[half(a) for a in parts],
        scratch_shapes=[pltpu.SemaphoreType.DMA((n,)), pltpu.SemaphoreType.DMA((n,))],
        compiler_params=pltpu.CompilerParams(has_side_effects=True),
    )(*parts)


def _add_tile(rows, cols):
    tc = cols if cols <= 1536 else cols // 4
    tr = rows
    while tr * tc * 4 > (1 << 20) and tr % 16 == 0:
        tr //= 2
    return tr, tc


def _rs_add_sibling(part, got, c, name):
    l, a, hr, cols = got.shape
    tr, tc = _add_tile(hr, cols)
    nr = hr // tr

    def body(c_ref, p_ref, g_ref, o32_ref, o16_ref):
        s = p_ref[...] + g_ref[...]
        o32_ref[...] = s
        o16_ref[...] = s.astype(BF16)

    blk = (None, None, tr, tc)
    spec = pl.BlockSpec(blk, lambda li, ai, i, j, c_ref: (li, ai, i, j))
    return pl.pallas_call(
        body, name=name,
        grid_spec=pltpu.PrefetchScalarGridSpec(
            num_scalar_prefetch=1, grid=(l, a, nr, cols // tc),
            in_specs=[pl.BlockSpec(blk, lambda li, ai, i, j, c_ref: (li, ai, c_ref[0] * nr + i, j)), spec],
            out_specs=[spec, spec]),
        out_shape=[jax.ShapeDtypeStruct(got.shape, F32), jax.ShapeDtypeStruct(got.shape, BF16)],
        compiler_params=_cparams("parallel", "parallel", "parallel", "parallel"),
    )(c, part, got)


def _quarter4(ref, kind, chip, cols):
    k = 2 * chip[0] + chip[1]
    if kind == "col":
        return ref.at[:, :, :, pl.ds(pl.multiple_of(k * cols, LANES), cols)]
    return ref.at[:, pl.ds(k, 1), :, :]


HBM = pl.BlockSpec(memory_space=pltpu.HBM)
SEM = pl.BlockSpec(memory_space=pltpu.SEMAPHORE)
SPLIT_COPY = pltpu.SideEffectType.DATAFLOW_SIDE_EFFECTING
OTHER_CHIPS = 3


def _quarter4_shape(a, kind):
    l, _, hr, cols = a.shape
    return (l, 1, hr, cols // 4 if kind == "col" else cols)


def _rs_chips_copies(srcs, lands, sems, kinds):
    x, y, c, chips = _place()
    copies = []
    for t, (src, land) in enumerate(zip(srcs, lands)):
        cols = _quarter4_shape(src, kinds[t])[3]
        for j, chip in enumerate(chips):
            pair = 2 * (OTHER_CHIPS * t + j)
            copies.append(_remote(_quarter4(src, kinds[t], chip, cols), land.at[j], sems[pair], sems[pair + 1],
                                  (*chip, c)))
    return copies


def _rs_chips_start(sums16, kinds, tag):
    n = len(sums16)
    n_sems = 2 * OTHER_CHIPS * n

    def body(*refs):
        srcs, lands, sems = refs[:n], refs[n:2 * n], refs[4 * n:4 * n + n_sems]
        for cp in _rs_chips_copies(srcs, lands, sems, kinds):
            cp.start()
        refs[-1][...] = jnp.zeros_like(refs[-1])

    lands = [lax.empty((OTHER_CHIPS,) + _quarter4_shape(a, k), a.dtype) for a, k in zip(sums16, kinds)]
    held = [pltpu.with_memory_space_constraint(a, pltpu.HBM) for a in (*sums16, *lands)]
    out = pl.pallas_call(
        body, name=f"reduce_chips_start_{tag}", in_specs=[HBM] * (2 * n),
        out_specs=(*[HBM] * (2 * n), *[SEM] * n_sems, pl.BlockSpec(memory_space=pltpu.VMEM)),
        out_shape=(*[pltpu.HBM(a.shape, a.dtype) for a in held], *[pltpu.SemaphoreType.DMA(())] * n_sems,
                   jax.ShapeDtypeStruct((8, LANES), F32)),
        input_output_aliases={i: i for i in range(2 * n)},
        compiler_params=pltpu.CompilerParams(has_side_effects=SPLIT_COPY),
    )(*held)
    return out[2 * n:2 * n + n_sems], out[:n], out[n:2 * n], out[-1]


def _rs_chips_wait(sems, srcs, lands, kinds, after, tag):
    n = len(srcs)

    def body(*refs):
        for cp in _rs_chips_copies(refs[:n], refs[n:2 * n], refs[2 * n:2 * n + len(sems)], kinds):
            cp.wait_send()
            cp.wait_recv()

    out = pl.pallas_call(
        body, name=f"reduce_chips_wait_{tag}", in_specs=[HBM] * (2 * n) + [SEM] * len(sems) + [ANY],
        out_specs=tuple([HBM] * (2 * n)),
        out_shape=tuple(pltpu.HBM(a.shape, a.dtype) for a in (*srcs, *lands)),
        input_output_aliases={i: i for i in range(2 * n)},
        compiler_params=pltpu.CompilerParams(has_side_effects=SPLIT_COPY),
    )(*srcs, *lands, *sems, after)
    return out[n:]


def _rs_add_chips(sum32, got, kind, kc, name):
    _, l, _, hr, cols = got.shape
    tr, _ = _add_tile(hr, cols)
    nr = hr // tr
    k_arr, c_arr = kc
    if kind == "col":
        own = pl.BlockSpec((None, None, tr, cols), lambda li, i, k_ref, c_ref: (li, 0, i, k_ref[0]))
    else:
        own = pl.BlockSpec((None, None, tr, cols), lambda li, i, k_ref, c_ref: (li, k_ref[0], i, 0))

    def body(k_ref, c_ref, own_ref, got_ref, o_ref):
        o_ref[...] = ((own_ref[...] + got_ref[0].astype(F32)) + got_ref[1].astype(F32)) + got_ref[2].astype(F32)

    return pl.pallas_call(
        body, name=name,
        grid_spec=pltpu.PrefetchScalarGridSpec(
            num_scalar_prefetch=2, grid=(l, nr),
            in_specs=[own, pl.BlockSpec((3, None, None, tr, cols), lambda li, i, k_ref, c_ref: (0, li, 0, i, 0))],
            out_specs=pl.BlockSpec((None, tr, cols), lambda li, i, k_ref, c_ref: (li, c_ref[0] * nr + i, 0))),
        out_shape=jax.ShapeDtypeStruct((l, 2 * hr, cols), F32),
        compiler_params=_cparams("parallel", "parallel"),
    )(k_arr, c_arr, sum32, got)


def _rs_finish(quarters):
    n = len(quarters)

    def body(*refs):
        bufs = refs[n:2 * n]
        send_sems, recv_sems = refs[2 * n:]
        x, y, c, _ = _place()
        copies = []
        for t in range(n):
            hr = bufs[t].shape[1] // 2
            mine = bufs[t].at[:, pl.ds(pl.multiple_of(c * hr, 8), hr), :]
            cp = _remote(mine, mine, send_sems.at[t], recv_sems.at[t], (x, y, 1 - c))
            cp.start()
            copies.append(cp)
        for cp in copies:
            cp.wait()

    return pl.pallas_call(
        body, name="reduce_finish", in_specs=[ANY] * n, out_specs=[ANY] * n,
        out_shape=[jax.ShapeDtypeStruct(a.shape, a.dtype) for a in quarters],
        input_output_aliases={t: t for t in range(n)},
        scratch_shapes=[pltpu.SemaphoreType.DMA((n,)), pltpu.SemaphoreType.DMA((n,))],
        compiler_params=pltpu.CompilerParams(has_side_effects=True),
    )(*quarters)


def _reduce_start(parts, kinds, names, tag):
    c_arr = jnp.reshape(lax.axis_index("c"), (1,)).astype(jnp.int32)
    canon = [_canonical(p, kind) for p, kind in zip(parts, kinds)]
    from_sibling = _rs_sibling(canon)
    sums = [_rs_add_sibling(p, g, c_arr, f"reduce_add_sibling_{nm}") for p, g, nm in zip(canon, from_sibling, names)]
    sems, srcs, lands, token = _rs_chips_start([s16 for _, s16 in sums], kinds, tag)
    return (sems, srcs, lands, [s32 for s32, _ in sums], kinds, names, tag), token


def _reduce_finish(state, after):
    sems, srcs, lands, sums32, kinds, names, tag = state
    x, y, c = lax.axis_index("x"), lax.axis_index("y"), lax.axis_index("c")
    kc = (jnp.reshape(2 * x + y, (1,)).astype(jnp.int32), jnp.reshape(c, (1,)).astype(jnp.int32))
    from_chips = _rs_chips_wait(sems, srcs, lands, kinds, after, tag)
    halves = [_rs_add_chips(s32, g, kind, kc, f"reduce_add_chips_{nm}")
              for s32, g, kind, nm in zip(sums32, from_chips, kinds, names)]
    return _rs_finish(halves)


SMALL_PEERS = 7


def _small_exchange(pack):
    rows = pack.shape[0]

    def body(p_ref, slots_ref, total_ref, send_sems, recv_sems):
        x, y, c, _ = _place()
        me = 4 * x + 2 * y + c
        slots_ref[me] = p_ref[...]
        copies = []
        for p in range(1, SMALL_PEERS + 1):
            px, py, pc = (p >> 2) & 1, (p >> 1) & 1, p & 1
            peer = (1 - x if px else x, 1 - y if py else y, 1 - c if pc else c)
            cp = _remote(p_ref, slots_ref.at[me], send_sems.at[p - 1], recv_sems.at[p - 1], peer)
            cp.start()
            copies.append(cp)
        for cp in copies:
            cp.wait()
        total = slots_ref[0]
        for i in range(1, SMALL_PEERS + 1):
            total = total + slots_ref[i]
        total_ref[...] = total

    vmem = pl.BlockSpec(memory_space=pltpu.VMEM)
    return pl.pallas_call(
        body, name="small_exchange", in_specs=[vmem], out_specs=[vmem, vmem],
        out_shape=[jax.ShapeDtypeStruct((SMALL_PEERS + 1, rows, LANES), F32), jax.ShapeDtypeStruct((rows, LANES), F32)],
        scratch_shapes=[pltpu.SemaphoreType.DMA((SMALL_PEERS,)), pltpu.SemaphoreType.DMA((SMALL_PEERS,))],
        compiler_params=pltpu.CompilerParams(has_side_effects=True),
    )(pack)


def _pack(arrays):
    rows = []
    for a in arrays:
        flat = a.reshape(-1).astype(F32)
        rows.append(jnp.pad(flat, (0, (-flat.shape[0]) % LANES)).reshape(-1, LANES))
    out = jnp.concatenate(rows, axis=0)
    return jnp.pad(out, ((0, (-out.shape[0]) % 8), (0, 0)))


def _unpack(pack, shapes):
    out, r = [], 0
    for sh in shapes:
        size = math.prod(sh)
        nr = -(-size // LANES)
        out.append(pack[r:r + nr].reshape(-1)[:size].reshape(sh))
        r += nr
    return out


def _adamw(w, g, m, v, name):
    shape = w.shape
    to2d = lambda a: a.reshape(-1, shape[-1])
    rows = math.prod(shape[:-1])
    tile = 256 if rows % 256 == 0 else rows

    def fn(wb, gb, mb, vb):
        m2 = ADAM_B1 * mb + (1.0 - ADAM_B1) * gb
        v2 = ADAM_B2 * vb + (1.0 - ADAM_B2) * (gb * gb)
        m_hat = m2 / (1.0 - ADAM_B1 ** ADAM_STEP)
        v_hat = v2 / (1.0 - ADAM_B2 ** ADAM_STEP)
        return -ADAM_LR * (m_hat / (jnp.sqrt(v_hat) + ADAM_EPS) + ADAM_WD * wb), m2, v2

    res = _rowwise(fn, [to2d(w), to2d(g), to2d(m), to2d(v)], [], [(shape[-1], F32)] * 3, [], tile=tile, name=name)
    return [r.reshape(shape) for r in res]


def _adamw_layer(w, g, m, v, layer, outs, name):
    _, rows, cols = w.shape
    tile = rows
    while tile * cols * 4 > (1 << 20) and tile % 16 == 0:
        tile //= 2

    def body(w_ref, g_ref, m_ref, v_ref, *rest):
        g_out, d_out, m_out, v_out = rest[-4:]
        gb = g_ref[...]
        m2 = ADAM_B1 * m_ref[...] + (1.0 - ADAM_B1) * gb
        v2 = ADAM_B2 * v_ref[...] + (1.0 - ADAM_B2) * (gb * gb)
        m_hat = m2 / (1.0 - ADAM_B1 ** ADAM_STEP)
        v_hat = v2 / (1.0 - ADAM_B2 ** ADAM_STEP)
        g_out[...] = gb
        d_out[...] = -ADAM_LR * (m_hat / (jnp.sqrt(v_hat) + ADAM_EPS) + ADAM_WD * w_ref[...])
        m_out[...] = m2
        v_out[...] = v2

    stacked = pl.BlockSpec((None, tile, cols), lambda i: (layer, i, 0))
    return pl.pallas_call(
        body, name=name, grid=(rows // tile,),
        in_specs=[stacked, pl.BlockSpec((None, tile, cols), lambda i: (0, i, 0)), stacked, stacked] + [ANY] * 4,
        out_specs=[stacked] * 4, out_shape=[jax.ShapeDtypeStruct(w.shape, F32)] * 4,
        input_output_aliases={4 + i: i for i in range(4)}, compiler_params=_cparams("parallel"),
    )(w, g, m, v, *outs)


BIG = (("ffn1_w_gu", "col"), ("ffn1_w_down", "row"), ("w_in_even", "col"), ("w_out_even", "row"),
       ("w_in_odd", "col"), ("w_out_odd", "row"), ("ffn2_w_gu", "col"), ("ffn2_w_down", "row"))
SMALL = ("norm_ffn1", "norm_mix", "dn_conv_w", "dn_a_log", "dn_dt_bias", "dn_norm_g", "fox_q_norm_g", "fox_k_norm_g",
         "fox_f_bias", "norm_ffn2")
WEIGHTS = ("norm_ffn1", "ffn1_w_gu", "ffn1_w_down", "norm_mix", "w_in_even", "dn_conv_w", "dn_a_log", "dn_dt_bias",
           "dn_norm_g", "fox_q_norm_g", "fox_k_norm_g", "fox_f_bias", "w_out_even", "w_in_odd", "w_out_odd",
           "norm_ffn2", "ffn2_w_gu", "ffn2_w_down")


def _step(x, target, w, m, v):
    k = 2 * lax.axis_index("x") + lax.axis_index("y")
    n_conv = w["dn_conv_w"].shape[2]

    kc = jnp.reshape(k, (1,)).astype(jnp.int32)
    kinds = dict(BIG)
    quarters = {name: w[name] for name in kinds}
    quarters["w_in_even"] = jnp.pad(w["w_in_even"], ((0, 0), (0, 0), (0, EVEN_QUARTER_PAD - EVEN_QUARTER)))
    first_names = [name for name in kinds if name not in ("w_in_odd", "w_out_odd")]
    rest_names = list(kinds)

    def even_columns(whole):
        padded = whole["w_in_even"]
        ref_order = jnp.concatenate([padded[..., q * EVEN_QUARTER_PAD:q * EVEN_QUARTER_PAD + EVEN_QUARTER]
                                     for q in range(4)], axis=-1)
        return {**whole, "w_in_even": _even_to_kernel_layout(ref_order)}

    conv_slots, _ = _small_exchange(_pack([w["dn_conv_w"]]))
    placed = [_place_quarter(quarters[name], kinds[name], kc, f"place_first_{name}", 0, 1) for name in first_names]
    gathered = _gather_weights(placed, [kinds[name] for name in first_names])
    first = even_columns(dict(zip(first_names, gathered)))
    placed = [_place_quarter(quarters[name], kinds[name], kc, f"place_rest_{name}", 1 if name in first_names else 0)
              for name in rest_names]
    rest_kinds = [kinds[name] for name in rest_names]
    sems, on_their_way, token = _gather_start(placed, rest_kinds, [conv_slots, *gathered], "rest")

    def rest_after(value):
        landed = _gather_wait(sems, on_their_way, rest_kinds, value, "rest")
        return even_columns(dict(zip(rest_names, _gather_forward(landed, rest_kinds, "rest"))))

    whole = {}
    conv_rows = math.prod(w["dn_conv_w"].shape) // LANES
    conv_quarters = [conv_slots[2 * q, :conv_rows].reshape(w["dn_conv_w"].shape) for q in range(4)]
    whole["dn_conv_w"] = jnp.concatenate(conv_quarters, axis=-1)
    for name in SMALL:
        if name != "dn_conv_w":
            whole[name] = w[name]

    updated = {name: [lax.empty(w[name].shape, F32) for _ in range(4)] for name in kinds}

    def on_reduced(layer, layer_grads):
        for name, g in layer_grads.items():
            if name == "w_in_even":
                g = g[..., :EVEN_QUARTER]
            stacked_layer = layer if w[name].shape[0] == w["norm_mix"].shape[0] else layer // 2
            updated[name] = _adamw_layer(w[name], g, m[name], v[name], stacked_layer, updated[name], f"adamw_{name}")

    loss, dx, small = _forward_backward(x, target, whole, first, rest_after, token, on_reduced)

    _, small_sum = _small_exchange(_pack([small[n] for n in SMALL]))
    grads = dict(zip(SMALL, _unpack(small_sum, [small[n].shape for n in SMALL])))
    grads["dn_conv_w"] = lax.dynamic_slice_in_dim(grads["dn_conv_w"], k * n_conv, n_conv, axis=2)
    delta, new_m, new_v = {}, {}, {}
    for name in kinds:
        grads[name], delta[name], new_m[name], new_v[name] = updated[name]
    packs = [_pack([d[n] for n in SMALL]) for d in (w, grads, m, v)]
    shapes = [w[n].shape for n in SMALL]
    for out, res in zip((delta, new_m, new_v), _adamw(*packs, "adamw_small")):
        out.update(zip(SMALL, _unpack(res, shapes)))
    total_loss = lax.psum(loss[0, 0], ("x", "y", "c"))
    return total_loss, dx, grads, delta, new_m, new_v


def kernel(x, norm_ffn1, ffn1_w_gu, ffn1_w_down, norm_mix, w_in_even, dn_conv_w, dn_a_log, dn_dt_bias, dn_norm_g, fox_q_norm_g, fox_k_norm_g, fox_f_bias, w_out_even, w_in_odd, w_out_odd, norm_ffn2, ffn2_w_gu, ffn2_w_down, loss_target, m_norm_ffn1, m_ffn1_w_gu, m_ffn1_w_down, m_norm_mix, m_w_in_even, m_dn_conv_w, m_dn_a_log, m_dn_dt_bias, m_dn_norm_g, m_fox_q_norm_g, m_fox_k_norm_g, m_fox_f_bias, m_w_out_even, m_w_in_odd, m_w_out_odd, m_norm_ffn2, m_ffn2_w_gu, m_ffn2_w_down, v_norm_ffn1, v_ffn1_w_gu, v_ffn1_w_down, v_norm_mix, v_w_in_even, v_dn_conv_w, v_dn_a_log, v_dn_dt_bias, v_dn_norm_g, v_fox_q_norm_g, v_fox_k_norm_g, v_fox_f_bias, v_w_out_even, v_w_in_odd, v_w_out_odd, v_norm_ffn2, v_ffn2_w_gu, v_ffn2_w_down):
    w = dict(zip(WEIGHTS, (norm_ffn1, ffn1_w_gu, ffn1_w_down, norm_mix, w_in_even, dn_conv_w, dn_a_log, dn_dt_bias,
                           dn_norm_g, fox_q_norm_g, fox_k_norm_g, fox_f_bias, w_out_even, w_in_odd, w_out_odd,
                           norm_ffn2, ffn2_w_gu, ffn2_w_down)))
    m = dict(zip(WEIGHTS, (m_norm_ffn1, m_ffn1_w_gu, m_ffn1_w_down, m_norm_mix, m_w_in_even, m_dn_conv_w, m_dn_a_log,
                           m_dn_dt_bias, m_dn_norm_g, m_fox_q_norm_g, m_fox_k_norm_g, m_fox_f_bias, m_w_out_even,
                           m_w_in_odd, m_w_out_odd, m_norm_ffn2, m_ffn2_w_gu, m_ffn2_w_down)))
    v = dict(zip(WEIGHTS, (v_norm_ffn1, v_ffn1_w_gu, v_ffn1_w_down, v_norm_mix, v_w_in_even, v_dn_conv_w, v_dn_a_log,
                           v_dn_dt_bias, v_dn_norm_g, v_fox_q_norm_g, v_fox_k_norm_g, v_fox_f_bias, v_w_out_even,
                           v_w_in_odd, v_w_out_odd, v_norm_ffn2, v_ffn2_w_gu, v_ffn2_w_down)))
    loss, dx, grads, delta, new_m, new_v = _step(x[0], loss_target[0], w, m, v)
    return (loss, dx[None], *[grads[n] for n in WEIGHTS], *[delta[n] for n in WEIGHTS],
            *[new_m[n] for n in WEIGHTS], *[new_v[n] for n in WEIGHTS])
```

```python
import functools
import math

import jax
import jax.numpy as jnp
from jax import lax
from jax.experimental import pallas as pl
from jax.experimental.pallas import tpu as pltpu

F32 = jnp.float32
BF16 = jnp.bfloat16
HI = lax.Precision.HIGHEST

HEAD_DIM = 128
N_DN_HEADS = 4
N_FOX_HEADS = 4
N_SB_HEADS = 8
D_DN = N_DN_HEADS * HEAD_DIM
D_FOX = N_FOX_HEADS * HEAD_DIM
CONV_WIDTH = 4
DN_CHUNK = 64
EPS = 1e-6
ATT_SCALE = HEAD_DIM ** -0.5
ADAM_LR, ADAM_B1, ADAM_B2, ADAM_EPS, ADAM_WD, ADAM_STEP = 0.001, 0.9, 0.999, 1e-08, 0.01, 10

V7X_VMEM_LIMIT = 56 * 1024 * 1024
LANES = 128
ATT_TQ = 256
ATT_TK = 128
ATT_SUB = ATT_TQ // ATT_TK

LANE_BETA, LANE_DECAY, LANE_FORGET = 0, 4, 8


def _cparams(*sem):
    return pltpu.CompilerParams(dimension_semantics=sem, vmem_limit_bytes=V7X_VMEM_LIMIT)


def _sigmoid(x):
    return 1.0 / (1.0 + jnp.exp(-x))


def _softplus(x):
    return jnp.maximum(x, 0.0) + jnp.log(1.0 + jnp.exp(-jnp.abs(x)))


def _silu_grad(y, sg):
    return sg * (1.0 + y * (1.0 - sg))


def _rowwise(fn, rows, bcast, outs, sums, *, tile, name):
    rows = [r if isinstance(r, tuple) else (r, r.shape[1], 0) for r in rows]
    s = rows[0][0].shape[0]
    assert s % tile == 0
    n_in, n_b, n_out, n_sum = len(rows), len(bcast), len(outs), len(sums)

    def body(*refs):
        ins = [r[...] for r in refs[:n_in + n_b]]
        res = fn(*ins)
        if not isinstance(res, (tuple, list)):
            res = (res,)
        out_refs = refs[n_in + n_b:n_in + n_b + n_out]
        sum_refs = refs[n_in + n_b + n_out:]
        for o_ref, val in zip(out_refs, res[:n_out]):
            o_ref[...] = val.astype(o_ref.dtype)
        if n_sum:
            @pl.when(pl.program_id(0) == 0)
            def _():
                for s_ref in sum_refs:
                    s_ref[...] = jnp.zeros_like(s_ref)
            for s_ref, val in zip(sum_refs, res[n_out:]):
                s_ref[...] += val

    in_specs = [pl.BlockSpec((tile, w), lambda i, cb=cb: (i, cb)) for _, w, cb in rows]
    in_specs += [pl.BlockSpec(b.shape, lambda i, nd=b.ndim: (0,) * nd) for b in bcast]
    out_specs = [pl.BlockSpec((tile, c), lambda i: (i, 0)) for c, _ in outs]
    out_specs += [pl.BlockSpec(sh, lambda i: (0, 0)) for sh in sums]
    out_shape = [jax.ShapeDtypeStruct((s, c), dt) for c, dt in outs]
    out_shape += [jax.ShapeDtypeStruct(sh, F32) for sh in sums]
    return pl.pallas_call(
        body, name=name, grid=(s // tile,), in_specs=in_specs, out_specs=out_specs, out_shape=out_shape,
        compiler_params=_cparams("arbitrary" if n_sum else "parallel"),
    )(*[r[0] for r in rows], *bcast)


def _rms_fwd(x, gain, name):
    def fn(xb, g):
        r = lax.rsqrt(jnp.mean(xb * xb, axis=-1, keepdims=True) + EPS)
        return (xb * r * g,)
    return _rowwise(fn, [x], [gain], [(x.shape[1], BF16)], [], tile=512, name=name)[0]


_DIMS = {"nn": (((1,), (0,)), ((), ())), "nt": (((1,), (1,)), ((), ())), "tn": (((0,), (0,)), ((), ()))}


def _dot(a, b, kind):
    return lax.dot_general(a.astype(BF16), b.astype(BF16), _DIMS[kind], preferred_element_type=F32)


def _dot32(a, b, kind="nn"):
    return lax.dot_general(a, b, _DIMS[kind], precision=HI, preferred_element_type=F32)


def _mm(a, b, kind, *, tm, tn, out_dtype, name, scale=None, residual=None, a_lead=(), b_lead=(),
        b_spec=None, n=None, into=None):
    ash, bsh = a.shape[len(a_lead):], b.shape[len(b_lead):]
    m = ash[1] if kind == "tn" else ash[0]
    k = ash[0] if kind == "tn" else ash[1]
    if b_spec is None:
        n = bsh[0] if kind == "nt" else bsh[1]
        assert k == (bsh[1] if kind == "nt" else bsh[0]), (ash, bsh, kind)
    assert m % tm == 0 and n % tn == 0, (m, tm, n, tn)
    la, lb = (None,) * len(a_lead), (None,) * len(b_lead)
    if kind == "tn":
        a_spec = pl.BlockSpec(la + (k, tm), lambda j, i: a_lead + (0, i))
    else:
        a_spec = pl.BlockSpec(la + (tm, k), lambda j, i: a_lead + (i, 0))
    if b_spec is None:
        if kind == "nt":
            b_spec = pl.BlockSpec(lb + (tn, k), lambda j, i: b_lead + (j, 0))
        else:
            b_spec = pl.BlockSpec(lb + (k, tn), lambda j, i: b_lead + (0, j))
    in_specs, args = [a_spec, b_spec], [a, b]
    if residual is not None:
        in_specs.append(pl.BlockSpec((tm, tn), lambda j, i: (i, j)))
        args.append(residual)
    aliases = {}
    if into is not None:
        buf, layer = into
        in_specs.append(pl.BlockSpec(memory_space=pl.ANY))
        args.append(buf)
        aliases = {len(args) - 1: 0}
        out_spec = pl.BlockSpec((None, tm, tn), lambda j, i: (layer, i, j))
        out_shape = jax.ShapeDtypeStruct(buf.shape, buf.dtype)
    else:
        out_spec = pl.BlockSpec((tm, tn), lambda j, i: (i, j))
        out_shape = jax.ShapeDtypeStruct((m, n), out_dtype)

    def body(a_ref, b_ref, *rest):
        acc = _dot(a_ref[...], b_ref[...], kind)
        if scale is not None:
            acc = acc * scale
        if residual is not None:
            acc = acc + rest[0][...]
        rest[-1][...] = acc.astype(rest[-1].dtype)

    return pl.pallas_call(
        body, name=name, grid=(n // tn, m // tm), in_specs=in_specs, out_specs=out_spec, out_shape=out_shape,
        input_output_aliases=aliases, compiler_params=_cparams("parallel", "parallel"),
    )(*args)


def _ffn_up(n, w_gu, layer, name):
    s, d = n.shape
    f = w_gu.shape[2] // 2
    tm, tn = 512, f // 2
    nj = f // tn

    def body(n_ref, wg_ref, wu_ref, gu_ref, a_ref):
        nv = n_ref[...]
        g = _dot(nv, wg_ref[...], "nn")
        u = _dot(nv, wu_ref[...], "nn")
        gu_ref[0] = g.astype(BF16)
        gu_ref[1] = u.astype(BF16)
        a_ref[...] = (g * _sigmoid(g) * u).astype(BF16)

    return pl.pallas_call(
        body, name=name, grid=(nj, s // tm),
        in_specs=[pl.BlockSpec((tm, d), lambda j, i: (i, 0)),
                  pl.BlockSpec((None, d, tn), lambda j, i: (layer, 0, j)),
                  pl.BlockSpec((None, d, tn), lambda j, i: (layer, 0, j + nj))],
        out_specs=[pl.BlockSpec((2, tm, tn), lambda j, i: (0, i, j)),
                   pl.BlockSpec((tm, tn), lambda j, i: (i, j))],
        out_shape=[jax.ShapeDtypeStruct((2, s, f), BF16), jax.ShapeDtypeStruct((s, f), BF16)],
        compiler_params=_cparams("parallel", "parallel"),
    )(n, w_gu, w_gu)


def _ffn_down_bwd(dxo, w_down, gu, layer, name, after=None):
    s, d = dxo.shape
    f = w_down.shape[1]
    tm, tn = 512, f // 2
    extra_specs, extra = ([ANY], [after]) if after is not None else ([], [])

    def body(dx_ref, w_ref, gu_ref, *rest):
        dgu_ref = rest[-1]
        da = 0.5 * _dot(dx_ref[...], w_ref[...], "nt")
        g = gu_ref[0].astype(F32)
        u = gu_ref[1].astype(F32)
        sg = _sigmoid(g)
        dgu_ref[0] = (da * u * _silu_grad(g, sg)).astype(BF16)
        dgu_ref[1] = (da * g * sg).astype(BF16)

    return pl.pallas_call(
        body, name=name, grid=(f // tn, s // tm),
        in_specs=[pl.BlockSpec((tm, d), lambda j, i: (i, 0)),
                  pl.BlockSpec((None, tn, d), lambda j, i: (layer, j, 0)),
                  pl.BlockSpec((2, tm, tn), lambda j, i: (0, i, j))] + extra_specs,
        out_specs=pl.BlockSpec((2, tm, tn), lambda j, i: (0, i, j)),
        out_shape=jax.ShapeDtypeStruct((2, s, f), BF16),
        compiler_params=_cparams("parallel", "parallel"),
    )(dxo, w_down, gu, *extra)


NORM_BWD_TM = 256


def _norm_bwd_after(terms, operands, specs, x, dres, gain, name):
    s, d = x.shape
    tm = NORM_BWD_TM
    n_op = len(operands)

    def body(*refs):
        x_ref, dres_ref, g_ref = refs[n_op:n_op + 3]
        dx_ref, dx16_ref, dgain_ref = refs[n_op + 3:]
        dn = None
        for a, b in terms(*refs[:n_op]):
            dn = _dot(a, b, "nt") if dn is None else dn + _dot(a, b, "nt")
        xb = x_ref[...]
        r = lax.rsqrt(jnp.mean(xb * xb, axis=-1, keepdims=True) + EPS)
        xh = xb * r
        dxh = dn * g_ref[...]
        dx = dres_ref[...] + r * (dxh - xh * jnp.mean(dxh * xh, axis=-1, keepdims=True))
        dx_ref[...] = dx
        dx16_ref[...] = dx.astype(BF16)

        @pl.when(pl.program_id(0) == 0)
        def _():
            dgain_ref[...] = jnp.zeros_like(dgain_ref)
        dgain_ref[...] += jnp.sum(dn * xh, axis=0, keepdims=True)

    rows = pl.BlockSpec((tm, d), lambda i: (i, 0))
    return pl.pallas_call(
        body, name=name, grid=(s // tm,),
        in_specs=list(specs) + [rows, rows, pl.BlockSpec((1, d), lambda i: (0, 0))],
        out_specs=[rows, rows, pl.BlockSpec((1, d), lambda i: (0, 0))],
        out_shape=[jax.ShapeDtypeStruct((s, d), F32), jax.ShapeDtypeStruct((s, d), BF16),
                   jax.ShapeDtypeStruct((1, d), F32)],
        compiler_params=_cparams("arbitrary"),
    )(*operands, x, dres, gain)


def _ffn_up_bwd(dgu, w_gu, layer, x, dres, gain, name):
    _, s, f = dgu.shape
    d = w_gu.shape[1]
    specs = [pl.BlockSpec((2, NORM_BWD_TM, f), lambda i: (0, i, 0)),
             pl.BlockSpec((None, d, f), lambda i: (layer, 0, 0)),
             pl.BlockSpec((None, d, f), lambda i: (layer, 0, 1))]
    terms = lambda dgu_ref, wg_ref, wu_ref: [(dgu_ref[0], wg_ref[...]), (dgu_ref[1], wu_ref[...])]
    return _norm_bwd_after(terms, [dgu, w_gu, w_gu], specs, x, dres, gain, name)


def _in_proj_bwd(dproj, w_in, j, x, dres, gain, name):
    k = dproj.shape[1]
    d = w_in.shape[1]
    specs = [pl.BlockSpec((NORM_BWD_TM, k), lambda i: (i, 0)), pl.BlockSpec((None, d, k), lambda i: (j, 0, 0))]
    terms = lambda a_ref, b_ref: [(a_ref[...], b_ref[...])]
    return _norm_bwd_after(terms, [dproj, w_in], specs, x, dres, gain, name)


def _ffn_fwd(x, gain, w_gu, w_down, layer, tag):
    n = _rms_fwd(x, gain, f"{tag}_norm")
    gu, a = _ffn_up(n, w_gu, layer, f"{tag}_up")
    x2 = _mm(a, w_down, "nn", tm=512, tn=x.shape[1], out_dtype=F32, name=f"{tag}_down", scale=0.5, residual=x,
             b_lead=(layer,))
    return x2, (x, n, gu, a)


def _ffn_bwd(dxo, dxo16, saved, gain, w_gu, w_down, layer, tag, g_gu, g_down, after=None):
    x, n, gu, a = saved
    s, f = a.shape
    dgu = _ffn_down_bwd(dxo16, w_down, gu, layer, f"{tag}_down_bwd", after)
    g_down = _mm(a, dxo16, "tn", tm=256, tn=dxo16.shape[1], out_dtype=F32, name=f"{tag}_down_dw", scale=0.5,
                 into=(g_down, 0))
    tn = f // 2
    nj = f // tn
    g_gu = _mm(n, dgu, "tn", tm=512, tn=tn, out_dtype=F32, name=f"{tag}_up_dw", into=(g_gu, 0), n=2 * f,
               b_spec=pl.BlockSpec((None, s, tn), lambda j, i: (j // nj, 0, j % nj)))
    dx, dx16, dgain = _ffn_up_bwd(dgu, w_gu, layer, x, dxo, gain, f"{tag}_up_bwd")
    return dx, dx16, dgain, g_gu, g_down


def _lane_col(blk, lane):
    li = lax.broadcasted_iota(jnp.int32, blk.shape, 1)
    return jnp.sum(jnp.where(li == lane, blk, 0.0), axis=1, keepdims=True)


def _split_dot(x, tri):
    hi = x.astype(BF16)
    lo = (x - hi.astype(F32)).astype(BF16)
    return (lax.dot_general(hi, tri, _DIMS["nn"], preferred_element_type=F32)
            + lax.dot_general(lo, tri, _DIMS["nn"], preferred_element_type=F32))


class _Each:
    def __init__(self, vals):
        self.vals = list(vals)

    def _with(self, other, op):
        others = other.vals if isinstance(other, _Each) else [other] * len(self.vals)
        return _Each(op(a, b) for a, b in zip(self.vals, others))

    def __add__(self, other):
        return self._with(other, lambda a, b: a + b)

    def __sub__(self, other):
        return self._with(other, lambda a, b: a - b)

    def __mul__(self, other):
        return self._with(other, lambda a, b: a * b)

    def __neg__(self):
        return _Each(-a for a in self.vals)


def _each(fn, *args):
    n = max(len(a.vals) for a in args if isinstance(a, _Each))
    res = [fn(*xs) for xs in zip(*[a.vals if isinstance(a, _Each) else [a] * n for a in args])]
    if isinstance(res[0], tuple):
        return tuple(_Each(r) for r in zip(*res))
    return _Each(res)


def _keep(cond, x):
    return _each(lambda v: jnp.where(cond, v, 0.0), x)


def _rowsum(x):
    return _each(lambda v: jnp.sum(v, axis=1, keepdims=True), x)


ATT_HEADS = 2
ATT_WIDTH = ATT_HEADS * HEAD_DIM
_HEAD_COLS = [slice(h * HEAD_DIM, (h + 1) * HEAD_DIM) for h in range(ATT_HEADS)]


def _att_specs(n_heads, s):
    groups = n_heads // ATT_HEADS
    q_spec = pl.BlockSpec((ATT_TQ, ATT_WIDTH), lambda g, i: (i, g))
    k_spec = pl.BlockSpec((s, ATT_WIDTH), lambda g, i: (0, groups + g))
    v_spec = pl.BlockSpec((s, ATT_WIDTH), lambda g, i: (0, 2 * groups + g))
    return q_spec, k_spec, v_spec


def _heads_of(ref, rows=None):
    return _Each(ref[:, cs] if rows is None else ref[rows, cs] for cs in _HEAD_COLS)


def _dot_each(a, b, kind):
    return _each(lambda x, y: _dot(x, y, kind), a, b)


def _att_iotas():
    row = lax.broadcasted_iota(jnp.int32, (ATT_TQ, ATT_TK), 0)
    col = lax.broadcasted_iota(jnp.int32, (ATT_TQ, ATT_TK), 1)
    jr = lax.broadcasted_iota(jnp.int32, (ATT_TK, ATT_TK), 0)
    jc = lax.broadcasted_iota(jnp.int32, (ATT_TK, ATT_TK), 1)
    return row, col, jr, jc


def _sb_fwd(qkv, n_heads, name):
    s = qkv.shape[0]

    def body(q_ref, k_ref, v_ref, o16_ref, o32_ref):
        i = pl.program_id(1)
        q = _heads_of(q_ref)
        row, col, jr, jc = _att_iotas()
        later = (jr > jc).astype(BF16)

        def step(jb, carry, diagonal):
            c_sp, acc = (_Each(part) for part in carry)
            work = []
            for sub in reversed(range(ATT_SUB)):
                keys = pl.ds(pl.multiple_of(jb * ATT_TQ + sub * ATT_TK, ATT_TK), ATT_TK)
                z = _dot_each(q, _heads_of(k_ref, keys), "nt") * ATT_SCALE
                sp = _each(_softplus, z)
                before = (col + sub * ATT_TK) < row if diagonal else None
                spm = _keep(before, sp) if diagonal else sp
                work.append((keys, z - sp, spm, _each(lambda x: _dot(x, later, "nn"), spm), before))
            for keys, logsig, spm, within, before in work:
                a = _each(jnp.exp, logsig - (c_sp + within))
                if diagonal:
                    a = _keep(before, a)
                acc = acc + _each(_split_dot, a, _heads_of(v_ref, keys))
                c_sp = c_sp + _rowsum(spm)
            return tuple(c_sp.vals), tuple(acc.vals)

        zeros = lambda width: tuple(jnp.zeros((ATT_TQ, width), F32) for _ in range(ATT_HEADS))
        carry = step(i, (zeros(1), zeros(HEAD_DIM)), True)
        _, acc = lax.fori_loop(0, i, lambda it, cr: step(i - 1 - it, cr, False), carry)
        for cs, acc_h in zip(_HEAD_COLS, acc):
            o16_ref[:, cs] = acc_h.astype(BF16)
            o32_ref[:, cs] = acc_h

    q_spec, k_spec, v_spec = _att_specs(n_heads, s)
    o_spec = pl.BlockSpec((ATT_TQ, ATT_WIDTH), lambda g, i: (i, g))
    return pl.pallas_call(
        body, name=name, grid=(n_heads // ATT_HEADS, s // ATT_TQ), in_specs=[q_spec, k_spec, v_spec],
        out_specs=[o_spec, o_spec],
        out_shape=[jax.ShapeDtypeStruct((s, n_heads * HEAD_DIM), BF16),
                   jax.ShapeDtypeStruct((s, n_heads * HEAD_DIM), F32)],
        compiler_params=_cparams("parallel", "arbitrary"),
    )(qkv, qkv, qkv)


def _sb_bwd(qkv, o32, do, n_heads, name):
    s = qkv.shape[0]

    def body(q_ref, k_ref, v_ref, o_ref, do_ref, dq_ref, dk_ref, dv_ref):
        i = pl.program_id(1)

        @pl.when(i == 0)
        def _():
            dk_ref[...] = jnp.zeros_like(dk_ref)
            dv_ref[...] = jnp.zeros_like(dv_ref)

        q, do = _heads_of(q_ref), _heads_of(do_ref)
        total = _rowsum(_each(lambda a, b: a.astype(F32) * b, do, _heads_of(o_ref)))
        row, col, jr, jc = _att_iotas()
        later = (jr > jc).astype(BF16)
        not_before = (jr >= jc).astype(BF16)

        def step(jb, carry, diagonal):
            c_sp, c_e, dq = (_Each(part) for part in carry)
            work = []
            for sub in reversed(range(ATT_SUB)):
                keys = pl.ds(pl.multiple_of(jb * ATT_TQ + sub * ATT_TK, ATT_TK), ATT_TK)
                k = _heads_of(k_ref, keys)
                z = _dot_each(q, k, "nt") * ATT_SCALE
                sp = _each(_softplus, z)
                before = (col + sub * ATT_TK) < row if diagonal else None
                spm = _keep(before, sp) if diagonal else sp
                work.append((keys, k, _each(jnp.exp, z - sp), spm, _each(lambda x: _dot(x, later, "nn"), spm),
                             _dot_each(do, _heads_of(v_ref, keys), "nt"), before))
            for keys, k, sig, spm, within, da, before in work:
                a = sig * _each(lambda x: jnp.exp(-x), c_sp + within)
                if diagonal:
                    a = _keep(before, a)
                e = a * da
                left = total - c_e - _each(lambda x: _split_dot(x, not_before), e)
                dz = (e - (e + left) * sig) * ATT_SCALE
                if diagonal:
                    dz = _keep(before, dz)
                dk, dv = _dot_each(dz, q, "tn"), _dot_each(a, do, "tn")
                for cs, dk_h, dv_h in zip(_HEAD_COLS, dk.vals, dv.vals):
                    dk_ref[keys, cs] += dk_h
                    dv_ref[keys, cs] += dv_h
                dq = dq + _dot_each(dz, k, "nn")
                c_sp = c_sp + _rowsum(spm)
                c_e = c_e + _rowsum(e)
            return tuple(c_sp.vals), tuple(c_e.vals), tuple(dq.vals)

        zeros = lambda width: tuple(jnp.zeros((ATT_TQ, width), F32) for _ in range(ATT_HEADS))
        carry = step(i, (zeros(1), zeros(1), zeros(HEAD_DIM)), True)
        _, _, dq = lax.fori_loop(0, i, lambda it, cr: step(i - 1 - it, cr, False), carry)
        for cs, dq_h in zip(_HEAD_COLS, dq):
            dq_ref[:, cs] = dq_h.astype(BF16)

    q_spec, k_spec, v_spec = _att_specs(n_heads, s)
    blk = pl.BlockSpec((ATT_TQ, ATT_WIDTH), lambda g, i: (i, g))
    full = pl.BlockSpec((s, ATT_WIDTH), lambda g, i: (0, g))
    wide = (s, n_heads * HEAD_DIM)
    return pl.pallas_call(
        body, name=name, grid=(n_heads // ATT_HEADS, s // ATT_TQ), in_specs=[q_spec, k_spec, v_spec, blk, blk],
        out_specs=[blk, full, full],
        out_shape=[jax.ShapeDtypeStruct(wide, BF16), jax.ShapeDtypeStruct(wide, F32), jax.ShapeDtypeStruct(wide, F32)],
        compiler_params=_cparams("parallel", "arbitrary"),
    )(qkv, qkv, qkv, o32, do)


def _fox_logits(q, k, cq, ct_ref, keys):
    ck = _Each(ct_ref[h, :, keys] for h in range(ATT_HEADS))
    return _dot_each(q, k, "nt") * ATT_SCALE + (cq - ck)


def _fox_cq(c_ref, group):
    c = c_ref[...]
    return _Each(_lane_col(c, LANE_FORGET + group * ATT_HEADS + h) for h in range(ATT_HEADS))


def _fox_fwd(qkv, c, ct, name):
    s = qkv.shape[0]
    n_heads = N_FOX_HEADS

    def body(q_ref, k_ref, v_ref, c_ref, ct_ref, o_ref, lse_ref):
        g, i = pl.program_id(0), pl.program_id(1)
        q = _heads_of(q_ref)
        cq = _fox_cq(c_ref, g)
        row, col, _, _ = _att_iotas()

        def step(jb, carry, diagonal):
            m, l, acc = (_Each(part) for part in carry)
            work = []
            m_new = m
            for sub in range(ATT_SUB):
                keys = pl.ds(pl.multiple_of(jb * ATT_TQ + sub * ATT_TK, ATT_TK), ATT_TK)
                sc = _fox_logits(q, _heads_of(k_ref, keys), cq, ct_ref, keys)
                valid = (col + sub * ATT_TK) <= row if diagonal else None
                if diagonal:
                    sc = _each(lambda x: jnp.where(valid, x, -1e30), sc)
                m_new = _each(lambda a, x: jnp.maximum(a, jnp.max(x, axis=1, keepdims=True)), m_new, sc)
                work.append((keys, sc, valid))
            w = _each(jnp.exp, m - m_new)
            l, acc = l * w, acc * w
            for keys, sc, valid in work:
                p = _each(jnp.exp, sc - m_new)
                if diagonal:
                    p = _keep(valid, p)
                l = l + _rowsum(p)
                acc = acc + _each(_split_dot, p, _heads_of(v_ref, keys))
            return tuple(m_new.vals), tuple(l.vals), tuple(acc.vals)

        per_head = lambda width, value: tuple(jnp.full((ATT_TQ, width), value, F32) for _ in range(ATT_HEADS))
        init = (per_head(1, -1e30), per_head(1, 0.0), per_head(HEAD_DIM, 0.0))
        m, l, acc = lax.fori_loop(0, i, lambda jb, cr: step(jb, cr, False), step(i, init, True))
        for h, cs in enumerate(_HEAD_COLS):
            o_ref[:, cs] = acc[h] / l[h]
            lse_ref[h] = jnp.broadcast_to(m[h] + jnp.log(l[h]), (ATT_TQ, LANES))

    q_spec, k_spec, v_spec = _att_specs(n_heads, s)
    return pl.pallas_call(
        body, name=name, grid=(n_heads // ATT_HEADS, s // ATT_TQ),
        in_specs=[q_spec, k_spec, v_spec, pl.BlockSpec((ATT_TQ, LANES), lambda g, i: (i, 0)),
                  pl.BlockSpec((ATT_HEADS, 1, s), lambda g, i: (g, 0, 0))],
        out_specs=[pl.BlockSpec((ATT_TQ, ATT_WIDTH), lambda g, i: (i, g)),
                   pl.BlockSpec((ATT_HEADS, ATT_TQ, LANES), lambda g, i: (g, i, 0))],
        out_shape=[jax.ShapeDtypeStruct((s, n_heads * HEAD_DIM), F32),
                   jax.ShapeDtypeStruct((n_heads, s, LANES), F32)],
        compiler_params=_cparams("parallel", "arbitrary"),
    )(qkv, qkv, qkv, c, ct)


def _fox_bwd(qkv, c, ct, o, lse, do, name):
    s = qkv.shape[0]
    n_heads = N_FOX_HEADS

    def body(q_ref, k_ref, v_ref, c_ref, ct_ref, o_ref, lse_ref, do_ref, dq_ref, dk_ref, dv_ref, dct_ref):
        g, i = pl.program_id(0), pl.program_id(1)

        @pl.when(i == 0)
        def _():
            dk_ref[...] = jnp.zeros_like(dk_ref)
            dv_ref[...] = jnp.zeros_like(dv_ref)
            dct_ref[...] = jnp.zeros_like(dct_ref)

        q = _heads_of(q_ref)
        do16 = _each(lambda x: x.astype(BF16), _heads_of(do_ref))
        delta = _rowsum(_each(lambda a, b: a.astype(F32) * b, do16, _heads_of(o_ref)))
        lse_col = _Each(lse_ref[h, :, 0:1] for h in range(ATT_HEADS))
        cq = _fox_cq(c_ref, g)
        row, col, _, _ = _att_iotas()

        def step(jb, dq, diagonal):
            dq = _Each(dq)
            for sub in range(ATT_SUB):
                keys = pl.ds(pl.multiple_of(jb * ATT_TQ + sub * ATT_TK, ATT_TK), ATT_TK)
                k = _heads_of(k_ref, keys)
                sc = _fox_logits(q, k, cq, ct_ref, keys)
                if diagonal:
                    valid = (col + sub * ATT_TK) <= row
                    p = _keep(valid, _each(jnp.exp, _keep(valid, sc) - lse_col))
                else:
                    p = _each(jnp.exp, sc - lse_col)
                ds = p * (_dot_each(do16, _heads_of(v_ref, keys), "nt") - delta)
                dss = ds * ATT_SCALE
                dk, dv = _dot_each(dss, q, "tn"), _dot_each(p, do16, "tn")
                for h, cs in enumerate(_HEAD_COLS):
                    dct_ref[h, :, keys] -= jnp.sum(ds.vals[h], axis=0, keepdims=True)
                    dk_ref[keys, cs] += dk.vals[h]
                    dv_ref[keys, cs] += dv.vals[h]
                dq = dq + _dot_each(dss, k, "nn")
            return tuple(dq.vals)

        dq0 = step(i, tuple(jnp.zeros((ATT_TQ, HEAD_DIM), F32) for _ in range(ATT_HEADS)), True)
        dq = lax.fori_loop(0, i, lambda jb, dq: step(jb, dq, False), dq0)
        for cs, dq_h in zip(_HEAD_COLS, dq):
            dq_ref[:, cs] = dq_h

    q_spec, k_spec, v_spec = _att_specs(n_heads, s)
    blk = pl.BlockSpec((ATT_TQ, ATT_WIDTH), lambda g, i: (i, g))
    full = pl.BlockSpec((s, ATT_WIDTH), lambda g, i: (0, g))
    wide = jax.ShapeDtypeStruct((s, n_heads * HEAD_DIM), F32)
    return pl.pallas_call(
        body, name=name, grid=(n_heads // ATT_HEADS, s // ATT_TQ),
        in_specs=[q_spec, k_spec, v_spec, pl.BlockSpec((ATT_TQ, LANES), lambda g, i: (i, 0)),
                  pl.BlockSpec((ATT_HEADS, 1, s), lambda g, i: (g, 0, 0)), blk,
                  pl.BlockSpec((ATT_HEADS, ATT_TQ, LANES), lambda g, i: (g, i, 0)), blk],
        out_specs=[blk, full, full, pl.BlockSpec((ATT_HEADS, 1, s), lambda g, i: (g, 0, 0))],
        out_shape=[wide, wide, wide, jax.ShapeDtypeStruct((n_heads, 1, s), F32)],
        compiler_params=_cparams("parallel", "arbitrary"),
    )(qkv, qkv, qkv, c, ct, o, lse, do)


def _cumsum_rows(x, reverse, name):
    s = x.shape[0]
    nb = s // LANES

    def body(x_ref, o_ref):
        r = lax.broadcasted_iota(jnp.int32, (LANES, LANES), 0)
        c = lax.broadcasted_iota(jnp.int32, (LANES, LANES), 1)
        tri = ((r <= c) if reverse else (r >= c)).astype(F32)

        def step(it, carry):
            b = (nb - 1 - it) if reverse else it
            off = pl.multiple_of(b * LANES, LANES)
            blk = x_ref[pl.ds(off, LANES), :]
            o_ref[pl.ds(off, LANES), :] = _dot32(tri, blk) + carry
            return carry + jnp.sum(blk, axis=0, keepdims=True)

        lax.fori_loop(0, nb, step, jnp.zeros((1, LANES), F32))

    return pl.pallas_call(body, name=name, out_shape=jax.ShapeDtypeStruct(x.shape, F32),
                          compiler_params=pltpu.CompilerParams(vmem_limit_bytes=V7X_VMEM_LIMIT))(x)


def _dot32_each(a, b, kind="nn"):
    return _each(lambda x, y: _dot32(x, y, kind), a, b)


def _unit_lower_inverse(m, ri, ci):
    c = ri.shape[0]
    t = -_keep(ri // 2 == ci // 2, m) + jnp.where(ri == ci, 1.0, 0.0)
    b = 4
    while b <= c:
        off_diag = (ri // b == ci // b) & (ri % b >= b // 2) & (ci % b < b // 2)
        t = t - _dot32_each(_dot32_each(t, _keep(off_diag, m)), t)
        b *= 2
    return t


def _dn_gates(g, ri, ci):
    eye = ri == ci
    incl = ri >= ci
    g_row = jnp.sum(jnp.where(eye, g, 0.0), axis=0, keepdims=True)
    gc = jnp.sum(jnp.where(incl, g_row, 0.0), axis=1, keepdims=True)
    gc_row = jnp.sum(jnp.where(eye, gc, 0.0), axis=0, keepdims=True)
    dmat = jnp.where(incl, jnp.exp(jnp.where(incl, gc - gc_row, 0.0)), 0.0)
    gc_last = jnp.sum(g, axis=0, keepdims=True)
    return gc, dmat, jnp.exp(gc), jnp.exp(gc_last - gc), jnp.exp(gc_last)


def _dn_fwd(qkv, act, name):
    s = qkv.shape[0]
    c, d, nh = DN_CHUNK, HEAD_DIM, N_DN_HEADS
    nc = s // c

    def body(q_ref, k_ref, v_ref, act_ref, o_ref, s_ref, t_ref, state):
        @pl.when(pl.program_id(0) == 0)
        def _():
            state[...] = jnp.zeros_like(state)

        ri = lax.broadcasted_iota(jnp.int32, (c, c), 0)
        ci = lax.broadcasted_iota(jnp.int32, (c, c), 1)
        act = act_ref[...]
        heads = range(nh)
        cols = [slice(h * d, (h + 1) * d) for h in heads]
        q, k, v = (_Each(ref[:, cs] for cs in cols) for ref in (q_ref, k_ref, v_ref))
        beta = _Each(_lane_col(act, LANE_BETA + h) for h in heads)
        g = _Each(_lane_col(act, LANE_DECAY + h) for h in heads)
        _, dmat, e, r, gl = _each(lambda gh: _dn_gates(gh, ri, ci), g)
        s0 = _Each(state[h] for h in heads)
        kb = beta * k
        t = _unit_lower_inverse(_keep(ri > ci, _dot32_each(kb, k, "nt") * dmat), ri, ci)
        vn = _dot32_each(t, beta * v) - _dot32_each(_dot32_each(t, kb * e), s0)
        o = _dot32_each(q * e, s0) + _dot32_each(_dot32_each(q, k, "nt") * dmat, vn)
        s1 = s0 * gl + _dot32_each(k * r, vn, "tn")
        for h in heads:
            o_ref[:, cols[h]] = o.vals[h]
            state[h] = s1.vals[h]
            s_ref[h] = s0.vals[h]
            t_ref[h] = t.vals[h]

    wide = lambda part: pl.BlockSpec((c, nh * d), lambda n: (n, part))
    return pl.pallas_call(
        body, name=name, grid=(nc,),
        in_specs=[wide(0), wide(1), wide(2), pl.BlockSpec((c, LANES), lambda n: (n, 0))],
        out_specs=[wide(0), pl.BlockSpec((nh, None, d, d), lambda n: (0, n, 0, 0)),
                   pl.BlockSpec((nh, None, c, c), lambda n: (0, n, 0, 0))],
        out_shape=[jax.ShapeDtypeStruct((s, nh * d), F32), jax.ShapeDtypeStruct((nh, nc, d, d), F32),
                   jax.ShapeDtypeStruct((nh, nc, c, c), F32)],
        scratch_shapes=[pltpu.VMEM((nh, d, d), F32)],
        compiler_params=_cparams("arbitrary"),
    )(qkv, qkv, qkv, act)


def _dn_bwd(qkv, act, states, tinv, do, name):
    s = qkv.shape[0]
    c, d, nh = DN_CHUNK, HEAD_DIM, N_DN_HEADS
    nc = s // c

    def chunk_bwd(q, k, v, do, beta, g, s0, t, ds_out):
        ri = lax.broadcasted_iota(jnp.int32, (c, c), 0)
        ci = lax.broadcasted_iota(jnp.int32, (c, c), 1)
        eye, incl, strict = ri == ci, ri >= ci, ri > ci
        gc, dmat, e, r, gl = _each(lambda gh: _dn_gates(gh, ri, ci), g)
        dot = _dot32_each
        rowsum = lambda x: _each(lambda a: jnp.sum(a, axis=1, keepdims=True), x)
        colsum = lambda x: _each(lambda a: jnp.sum(a, axis=0, keepdims=True), x)
        total = lambda x: colsum(rowsum(x))
        to_col = lambda row: rowsum(_keep(eye, row))
        to_row = lambda colv: colsum(_keep(eye, colv))

        kb, vb = beta * k, beta * v
        kbe = kb * e
        u, w = dot(t, vb), dot(t, kbe)
        vn = u - dot(w, s0)
        qk = dot(q, k, "nt")
        p = qk * dmat
        gram = dot(k, k, "nt")
        kr, qe = k * r, q * e

        d_kr = dot(vn, ds_out, "nt")
        dvn = dot(kr, ds_out)
        dgl = total(s0 * ds_out)
        ds_in = ds_out * gl
        dk = d_kr * r
        dr = rowsum(d_kr * k)
        d_qe = dot(do, s0, "nt")
        ds_in = ds_in + dot(qe, do, "tn")
        dp = _keep(incl, dot(do, vn, "nt"))
        dvn = dvn + dot(p, do, "tn")
        dq = d_qe * e
        de = rowsum(d_qe * q)
        dqk = dp * dmat
        dq = dq + dot(dqk, k)
        dk = dk + dot(dqk, q, "tn")
        dd = dp * qk
        dw = -dot(dvn, s0, "nt")
        ds_in = ds_in - dot(w, dvn, "tn")
        dvb = dot(t, dvn, "tn")
        dkbe = dot(t, dw, "tn")
        dm = -_keep(strict, dot(dvb, u, "nt") + dot(dkbe, w, "nt"))
        dbeta = rowsum(dm * gram * dmat)
        dgram = dm * beta * dmat
        dd = dd + dm * beta * gram
        dk = dk + dot(dgram, k) + dot(dgram, k, "tn")
        dkb = dkbe * e
        de = de + rowsum(dkbe * kb)
        dk = dk + beta * dkb
        dbeta = dbeta + rowsum(dkb * k) + rowsum(dvb * v)
        dv = beta * dvb
        wd = dd * dmat
        dgc = rowsum(wd) - to_col(colsum(wd)) + de * e - dr * r
        dgc_last = total(dr * r) + dgl * gl
        dgc = dgc + _keep(ri[:, 0:1] == c - 1, dgc_last)
        dg = rowsum(_keep(ri <= ci, to_row(dgc)))
        return dq, dk, dv, dbeta, dg, ds_in

    def body(q_ref, k_ref, v_ref, act_ref, s_ref, t_ref, do_ref, dq_ref, dk_ref, dv_ref, dact_ref, dstate):
        @pl.when(pl.program_id(0) == 0)
        def _():
            dstate[...] = jnp.zeros_like(dstate)

        act = act_ref[...]
        heads = range(nh)
        cols = [slice(h * d, (h + 1) * d) for h in heads]
        q, k, v, do = (_Each(ref[:, cs] for cs in cols) for ref in (q_ref, k_ref, v_ref, do_ref))
        dq, dk, dv, dbeta, dg, ds_in = chunk_bwd(
            q, k, v, do, _Each(_lane_col(act, LANE_BETA + h) for h in heads),
            _Each(_lane_col(act, LANE_DECAY + h) for h in heads), _Each(s_ref[h] for h in heads),
            _Each(t_ref[h] for h in heads), _Each(dstate[h] for h in heads))
        lane = lax.broadcasted_iota(jnp.int32, (c, LANES), 1)
        dact = jnp.zeros((c, LANES), F32)
        for h in heads:
            dstate[h] = ds_in.vals[h]
            dq_ref[:, cols[h]], dk_ref[:, cols[h]], dv_ref[:, cols[h]] = dq.vals[h], dk.vals[h], dv.vals[h]
            dact = (dact + jnp.where(lane == LANE_BETA + h, dbeta.vals[h], 0.0)
                    + jnp.where(lane == LANE_DECAY + h, dg.vals[h], 0.0))
        dact_ref[...] = dact

    part = lambda p: pl.BlockSpec((c, nh * d), lambda n: (nc - 1 - n, p))
    per = lambda a, b: pl.BlockSpec((nh, None, a, b), lambda n: (0, nc - 1 - n, 0, 0))
    wide = jax.ShapeDtypeStruct((s, nh * d), F32)
    act_spec = pl.BlockSpec((c, LANES), lambda n: (nc - 1 - n, 0))
    return pl.pallas_call(
        body, name=name, grid=(nc,),
        in_specs=[part(0), part(1), part(2), act_spec, per(d, d), per(c, c), part(0)],
        out_specs=[part(0), part(0), part(0), act_spec],
        out_shape=[wide, wide, wide, jax.ShapeDtypeStruct((s, LANES), F32)],
        scratch_shapes=[pltpu.VMEM((nh, d, d), F32)],
        compiler_params=_cparams("arbitrary"),
    )(qkv, qkv, qkv, act, states, tinv, do)


EVEN_DN_QKV, EVEN_FOX_QKV, EVEN_DN_GATE, EVEN_FOX_GATE, EVEN_NARROW = 0, 1536, 3072, 3584, 4096
EVEN_WIDTH = 4224
CONV_TILE = 256
CONV_HALO = 8


def _conv_fwd(proj, w, name):
    s = proj.shape[0]
    t, cw = CONV_TILE, 3 * D_DN

    def body(cur_ref, prev_ref, w_ref, y_ref, xs):
        i = pl.program_id(0)
        xs[0:CONV_HALO, :] = jnp.where(i > 0, prev_ref[...], 0.0)
        xs[CONV_HALO:, :] = cur_ref[...]
        y = jnp.zeros((t, cw), F32)
        for tap in range(CONV_WIDTH):
            y = y + w_ref[tap:tap + 1, :] * xs[pl.ds(CONV_HALO - CONV_WIDTH + 1 + tap, t), :]
        y_ref[...] = y

    per = t // CONV_HALO
    return pl.pallas_call(
        body, name=name, grid=(s // t,),
        in_specs=[pl.BlockSpec((t, cw), lambda i: (i, 0)),
                  pl.BlockSpec((CONV_HALO, cw), lambda i: (jnp.maximum(i * per - 1, 0), 0)),
                  pl.BlockSpec((CONV_WIDTH, cw), lambda i: (0, 0))],
        out_specs=pl.BlockSpec((t, cw), lambda i: (i, 0)),
        out_shape=jax.ShapeDtypeStruct((s, cw), F32),
        scratch_shapes=[pltpu.VMEM((t + CONV_HALO, cw), F32)],
        compiler_params=_cparams("parallel"),
    )(proj, proj, w)


def _conv_bwd(proj, w, dy, name):
    s = proj.shape[0]
    t, cw = CONV_TILE, 3 * D_DN
    nt = s // t

    def body(cur_ref, prev_ref, w_ref, dy_ref, nxt_ref, dx_ref, dw_ref, xs, dys):
        i = pl.program_id(0)

        @pl.when(i == 0)
        def _():
            dw_ref[...] = jnp.zeros_like(dw_ref)

        xs[0:CONV_HALO, :] = jnp.where(i > 0, prev_ref[...], 0.0)
        xs[CONV_HALO:, :] = cur_ref[...]
        dys[0:t, :] = dy_ref[...]
        dys[t:, :] = jnp.where(i < nt - 1, nxt_ref[...], 0.0)
        dy = dy_ref[...]
        dx = jnp.zeros((t, cw), F32)
        for tap in range(CONV_WIDTH):
            dx = dx + w_ref[tap:tap + 1, :] * dys[pl.ds(CONV_WIDTH - 1 - tap, t), :]
            dw_ref[tap:tap + 1, :] += jnp.sum(dy * xs[pl.ds(CONV_HALO - CONV_WIDTH + 1 + tap, t), :], axis=0,
                                              keepdims=True)
        dx_ref[...] = dx.astype(BF16)

    per = t // CONV_HALO
    last = s // CONV_HALO - 1
    return pl.pallas_call(
        body, name=name, grid=(nt,),
        in_specs=[pl.BlockSpec((t, cw), lambda i: (i, 0)),
                  pl.BlockSpec((CONV_HALO, cw), lambda i: (jnp.maximum(i * per - 1, 0), 0)),
                  pl.BlockSpec((CONV_WIDTH, cw), lambda i: (0, 0)),
                  pl.BlockSpec((t, cw), lambda i: (i, 0)),
                  pl.BlockSpec((CONV_HALO, cw), lambda i: (jnp.minimum((i + 1) * per, last), 0))],
        out_specs=[pl.BlockSpec((t, cw), lambda i: (i, 0)), pl.BlockSpec((CONV_WIDTH, cw), lambda i: (0, 0))],
        out_shape=[jax.ShapeDtypeStruct((s, cw), BF16), jax.ShapeDtypeStruct((CONV_WIDTH, cw), F32)],
        scratch_shapes=[pltpu.VMEM((t + CONV_HALO, cw), F32), pltpu.VMEM((t + CONV_HALO, cw), F32)],
        compiler_params=_cparams("arbitrary"),
    )(proj, proj, w, dy, dy)


def _heads(x, n):
    return [x[:, HEAD_DIM * h:HEAD_DIM * (h + 1)] for h in range(n)]


def _dn_pre_fwd(y, name):
    def fn(yb):
        cs = yb * _sigmoid(yb)
        out = []
        for idx, xh in enumerate(_heads(cs, 3 * N_DN_HEADS)):
            if idx < 2 * N_DN_HEADS:
                xh = xh * lax.rsqrt(jnp.sum(xh * xh, axis=-1, keepdims=True) + EPS)
                if idx < N_DN_HEADS:
                    xh = xh * ATT_SCALE
            out.append(xh)
        return (jnp.concatenate(out, axis=1),)
    return _rowwise(fn, [y], [], [(y.shape[1], F32)], [], tile=256, name=name)[0]


def _dn_pre_bwd(y, dq, dk, dv, name):
    def fn(yb, dqb, dkb, dvb):
        sg = _sigmoid(yb)
        cs = yb * sg
        dout = _heads(dqb, N_DN_HEADS) + _heads(dkb, N_DN_HEADS) + _heads(dvb, N_DN_HEADS)
        dcs = []
        for idx, (xh, dh) in enumerate(zip(_heads(cs, 3 * N_DN_HEADS), dout)):
            if idx < 2 * N_DN_HEADS:
                if idx < N_DN_HEADS:
                    dh = dh * ATT_SCALE
                r = lax.rsqrt(jnp.sum(xh * xh, axis=-1, keepdims=True) + EPS)
                xhat = xh * r
                dh = r * (dh - xhat * jnp.sum(xhat * dh, axis=-1, keepdims=True))
            dcs.append(dh)
        return (jnp.concatenate(dcs, axis=1) * _silu_grad(yb, sg),)
    return _rowwise(fn, [y, dq, dk, dv], [], [(y.shape[1], F32)], [], tile=256, name=name)[0]


def _narrow_params(a_log, dt_bias, f_bias):
    lanes = lambda a, first: jnp.pad(a.reshape(1, -1), ((0, 0), (first, LANES - first - a.shape[0])))
    return jnp.concatenate([lanes(a_log, LANE_DECAY), lanes(dt_bias, LANE_DECAY), lanes(f_bias, LANE_FORGET),
                            jnp.zeros((5, LANES), F32)], axis=0)


def _narrow_masks(shape):
    lane = lax.broadcasted_iota(jnp.int32, shape, 1)
    is_beta = lane < LANE_DECAY
    is_decay = (lane >= LANE_DECAY) & (lane < LANE_FORGET)
    is_forget = (lane >= LANE_FORGET) & (lane < LANE_FORGET + N_FOX_HEADS)
    return is_beta, is_decay, is_forget


def _narrow_fwd(proj, params, name):
    def fn(sm, pk):
        is_beta, is_decay, is_forget = _narrow_masks(sm.shape)
        g = -jnp.exp(pk[0:1, :]) * _softplus(sm + pk[1:2, :])
        logf = -_softplus(-(sm + pk[2:3, :]))
        return (jnp.where(is_beta, _sigmoid(sm), jnp.where(is_decay, g, jnp.where(is_forget, logf, 0.0))),)
    return _rowwise(fn, [(proj, LANES, EVEN_NARROW // LANES)], [params], [(LANES, F32)], [], tile=512, name=name)[0]


def _narrow_bwd(proj, params, act, dact, dlogf, name):
    def fn(sm, ab, da, dl, pk):
        is_beta, is_decay, is_forget = _narrow_masks(sm.shape)
        db = jnp.where(is_forget, dl, da)
        d_beta = db * ab * (1.0 - ab)
        d_decay = db * (-jnp.exp(pk[0:1, :])) * _sigmoid(sm + pk[1:2, :])
        d_forget = db * _sigmoid(-(sm + pk[2:3, :]))
        dsm = jnp.where(is_beta, d_beta, jnp.where(is_decay, d_decay, jnp.where(is_forget, d_forget, 0.0)))
        col = lambda x: jnp.sum(x, axis=0, keepdims=True)
        return (dsm, col(jnp.where(is_decay, db * ab, 0.0)), col(jnp.where(is_decay, dsm, 0.0)),
                col(jnp.where(is_forget, dsm, 0.0)))
    return _rowwise(fn, [(proj, LANES, EVEN_NARROW // LANES), act, dact, dlogf], [params], [(LANES, BF16)],
                    [(1, LANES)] * 3, tile=512, name=name)


def _head_rms(xh):
    r = lax.rsqrt(jnp.mean(xh * xh, axis=-1, keepdims=True) + EPS)
    return xh * r, r


def _fox_pre_fwd(proj, qg, kg, name):
    def fn(pf, qgb, kgb):
        out = []
        for idx, xh in enumerate(_heads(pf, 3 * N_FOX_HEADS)):
            if idx < 2 * N_FOX_HEADS:
                xh = _head_rms(xh)[0] * (qgb if idx < N_FOX_HEADS else kgb)
            out.append(xh)
        return (jnp.concatenate(out, axis=1),)
    return _rowwise(fn, [(proj, 3 * D_FOX, EVEN_FOX_QKV // (3 * D_FOX))], [qg, kg], [(3 * D_FOX, BF16)], [],
                    tile=256, name=name)[0]


def _fox_pre_bwd(proj, qg, kg, dq, dk, dv, name):
    def fn(pf, dqb, dkb, dvb, qgb, kgb):
        dout = _heads(dqb, N_FOX_HEADS) + _heads(dkb, N_FOX_HEADS) + _heads(dvb, N_FOX_HEADS)
        dg = [jnp.zeros((1, HEAD_DIM), F32), jnp.zeros((1, HEAD_DIM), F32)]
        dx = []
        for idx, (xh, dh) in enumerate(zip(_heads(pf, 3 * N_FOX_HEADS), dout)):
            if idx < 2 * N_FOX_HEADS:
                which = 0 if idx < N_FOX_HEADS else 1
                xhat, r = _head_rms(xh)
                dg[which] = dg[which] + jnp.sum(dh * xhat, axis=0, keepdims=True)
                dxh = dh * (qgb if which == 0 else kgb)
                dh = r * (dxh - xhat * jnp.mean(dxh * xhat, axis=-1, keepdims=True))
            dx.append(dh)
        return jnp.concatenate(dx, axis=1), dg[0], dg[1]
    return _rowwise(fn, [(proj, 3 * D_FOX, EVEN_FOX_QKV // (3 * D_FOX)), dq, dk, dv], [qg, kg],
                    [(3 * D_FOX, BF16)], [(1, HEAD_DIM)] * 2, tile=256, name=name)


def _mix_gate_fwd(proj, o_dn, o_fox, ng, name):
    def fn(gd, gf, od, of, ngb):
        dn = [_head_rms(xh)[0] * ngb for xh in _heads(od, N_DN_HEADS)]
        return (jnp.concatenate([jnp.concatenate(dn, axis=1) * gd * _sigmoid(gd), of * _sigmoid(gf)], axis=1),)
    return _rowwise(fn, [(proj, D_DN, EVEN_DN_GATE // D_DN), (proj, D_FOX, EVEN_FOX_GATE // D_FOX), o_dn, o_fox],
                    [ng], [(D_DN + D_FOX, BF16)], [], tile=256, name=name)[0]


def _mix_gate_bwd(proj, o_dn, o_fox, ng, dom, name):
    def fn(gd, gf, od, of, dm, ngb):
        d_dn, d_fox = dm[:, :D_DN], dm[:, D_DN:]
        sgd, sgf = _sigmoid(gd), _sigmoid(gf)
        don = d_dn * gd * sgd
        dng = jnp.zeros((1, HEAD_DIM), F32)
        dod, normed = [], []
        for xh, dh in zip(_heads(od, N_DN_HEADS), _heads(don, N_DN_HEADS)):
            xhat, r = _head_rms(xh)
            dng = dng + jnp.sum(dh * xhat, axis=0, keepdims=True)
            dxh = dh * ngb
            dod.append(r * (dxh - xhat * jnp.mean(dxh * xhat, axis=-1, keepdims=True)))
            normed.append(xhat * ngb)
        d_gd = d_dn * jnp.concatenate(normed, axis=1) * _silu_grad(gd, sgd)
        d_gf = d_fox * of * sgf * (1.0 - sgf)
        return jnp.concatenate(dod, axis=1), d_fox * sgf, d_gd, d_gf, dng
    return _rowwise(fn, [(proj, D_DN, EVEN_DN_GATE // D_DN), (proj, D_FOX, EVEN_FOX_GATE // D_FOX), o_dn, o_fox, dom],
                    [ng], [(D_DN, F32), (D_FOX, F32), (D_DN, BF16), (D_FOX, BF16)], [(1, HEAD_DIM)], tile=256,
                    name=name)


def _loss_grad(y, target, name):
    d = y.shape[1]

    def fn(yb, tb):
        diff = yb - tb
        part = jnp.sum(jnp.sum(diff * diff, axis=1, keepdims=True), axis=0, keepdims=True) * (0.5 / d)
        g = diff * (1.0 / d)
        return g, g, part
    return _rowwise(fn, [y, target], [], [(d, F32), (d, BF16)], [(1, 1)], tile=512, name=name)


_REF_EVEN = {"dn_qkv": (0, 1536), "dn_gate": (1536, 2048), "dn_ba": (2048, 2056), "fox_qkv": (2056, 3592),
             "fox_gate": (3592, 4104), "f_pre": (4104, 4108)}
D_IN_EVEN = 4108


def _even_to_kernel_layout(w):
    cut = lambda name: w[..., _REF_EVEN[name][0]:_REF_EVEN[name][1]]
    pad = jnp.zeros(w.shape[:-1] + (EVEN_WIDTH - EVEN_NARROW - 12,), w.dtype)
    return jnp.concatenate([cut("dn_qkv"), cut("fox_qkv"), cut("dn_gate"), cut("fox_gate"), cut("dn_ba"),
                            cut("f_pre"), pad], axis=-1)


def _even_from_kernel_layout(g):
    return jnp.concatenate([g[..., EVEN_DN_QKV:EVEN_FOX_QKV], g[..., EVEN_DN_GATE:EVEN_FOX_GATE],
                            g[..., EVEN_NARROW:EVEN_NARROW + 8], g[..., EVEN_FOX_QKV:EVEN_DN_GATE],
                            g[..., EVEN_FOX_GATE:EVEN_NARROW], g[..., EVEN_NARROW + 8:EVEN_NARROW + 12]], axis=-1)


EVEN_QUARTER = 1027
EVEN_QUARTER_PAD = 1152


def _even_grad_quarters(g):
    g = _even_from_kernel_layout(g)
    pad = [(0, 0)] * (g.ndim - 1) + [(0, EVEN_QUARTER_PAD - EVEN_QUARTER)]
    return jnp.concatenate([jnp.pad(g[..., q * EVEN_QUARTER:(q + 1) * EVEN_QUARTER], pad) for q in range(4)], axis=-1)


def _forget_rows(c):
    return c[:, LANE_FORGET:LANE_FORGET + N_FOX_HEADS].T.reshape(N_FOX_HEADS, 1, c.shape[0])


def _forget_lanes(rows):
    s = rows.shape[2]
    return jnp.pad(rows.reshape(-1, s).T, ((0, 0), (LANE_FORGET, LANES - LANE_FORGET - N_FOX_HEADS)))


def _even_fwd(x, gain, w_in, w_out, j, p, tag):
    h = _rms_fwd(x, gain, f"{tag}_norm")
    proj = _mm(h, w_in, "nn", tm=512, tn=EVEN_WIDTH // 3, out_dtype=F32, name=f"{tag}_in", b_lead=(j,))
    y = _conv_fwd(proj, p["conv_w"], f"{tag}_conv")
    dn_qkv = _dn_pre_fwd(y, f"{tag}_dn_pre")
    act = _narrow_fwd(proj, p["narrow"], f"{tag}_narrow")
    o_dn, states, tinv = _dn_fwd(dn_qkv, act, f"{tag}_delta")
    fox_qkv = _fox_pre_fwd(proj, p["q_g"], p["k_g"], f"{tag}_fox_pre")
    c = _cumsum_rows(act, False, f"{tag}_cumsum")
    ct = _forget_rows(c)
    o_fox, lse = _fox_fwd(fox_qkv, c, ct, f"{tag}_fox")
    om = _mix_gate_fwd(proj, o_dn, o_fox, p["dn_norm_g"], f"{tag}_gate")
    x2 = _mm(om, w_out, "nn", tm=512, tn=x.shape[1], out_dtype=F32, name=f"{tag}_out", residual=x, b_lead=(j,))
    return x2, (x, h, proj, y, dn_qkv, act, states, tinv, o_dn, fox_qkv, c, ct, o_fox, lse, om)


def _even_bwd(dxo, dxo16, saved, gain, w_in, w_out, j, p, tag, g_in, g_out):
    x, h, proj, y, dn_qkv, act, states, tinv, o_dn, fox_qkv, c, ct, o_fox, lse, om = saved
    d = x.shape[1]
    dom = _mm(dxo16, w_out, "nt", tm=512, tn=d, out_dtype=F32, name=f"{tag}_out_bwd", b_lead=(j,))
    g_out = _mm(om, dxo16, "tn", tm=512, tn=d, out_dtype=F32, name=f"{tag}_out_dw", into=(g_out, 0))
    d_odn, d_ofox, d_gd, d_gf, d_ng = _mix_gate_bwd(proj, o_dn, o_fox, p["dn_norm_g"], dom, f"{tag}_gate_bwd")
    dq, dk, dv, dct = _fox_bwd(fox_qkv, c, ct, o_fox, lse, d_ofox, f"{tag}_fox_bwd")
    d_fox_qkv, d_qg, d_kg = _fox_pre_bwd(proj, p["q_g"], p["k_g"], dq, dk, dv, f"{tag}_fox_pre_bwd")
    dlogf = _cumsum_rows(_forget_lanes(dct), True, f"{tag}_cumsum_bwd")
    dq, dk, dv, dact = _dn_bwd(dn_qkv, act, states, tinv, d_odn, f"{tag}_delta_bwd")
    dy = _dn_pre_bwd(y, dq, dk, dv, f"{tag}_dn_pre_bwd")
    d_dn_qkv, d_conv = _conv_bwd(proj, p["conv_w"], dy, f"{tag}_conv_bwd")
    d_narrow, s_alog, s_dt, s_fb = _narrow_bwd(proj, p["narrow"], act, dact, dlogf, f"{tag}_narrow_bwd")
    dproj = jnp.concatenate([d_dn_qkv, d_fox_qkv, d_gd, d_gf, d_narrow], axis=1)
    g_in = _mm(h, dproj, "tn", tm=512, tn=EVEN_WIDTH // 3, out_dtype=F32, name=f"{tag}_in_dw", into=(g_in, 0))
    dx, dx16, d_gain = _in_proj_bwd(dproj, w_in, j, x, dxo, gain, f"{tag}_in_bwd")
    small = {"conv_w": d_conv, "a_log": s_alog, "dt_bias": s_dt, "f_bias": s_fb, "dn_norm_g": d_ng, "q_g": d_qg,
             "k_g": d_kg}
    return dx, dx16, d_gain, small, g_in, g_out


def _odd_fwd(x, gain, w_in, w_out, j, tag):
    h = _rms_fwd(x, gain, f"{tag}_norm")
    qkv = _mm(h, w_in, "nn", tm=512, tn=w_in.shape[2] // 2, out_dtype=BF16, name=f"{tag}_in", b_lead=(j,))
    o16, o32 = _sb_fwd(qkv, N_SB_HEADS, f"{tag}_sb")
    x2 = _mm(o16, w_out, "nn", tm=512, tn=x.shape[1], out_dtype=F32, name=f"{tag}_out", residual=x, b_lead=(j,))
    return x2, (x, h, qkv, o16, o32)


def _odd_bwd(dxo, dxo16, saved, gain, w_in, w_out, j, tag, g_in, g_out):
    x, h, qkv, o16, o32 = saved
    d = x.shape[1]
    do = _mm(dxo16, w_out, "nt", tm=512, tn=d, out_dtype=BF16, name=f"{tag}_out_bwd", b_lead=(j,))
    g_out = _mm(o16, dxo16, "tn", tm=512, tn=d, out_dtype=F32, name=f"{tag}_out_dw", into=(g_out, 0))
    dq, dk, dv = _sb_bwd(qkv, o32, do, N_SB_HEADS, f"{tag}_sb_bwd")
    dqkv = jnp.concatenate([dq, dk.astype(BF16), dv.astype(BF16)], axis=1)
    g_in = _mm(h, dqkv, "tn", tm=512, tn=w_in.shape[2] // 2, out_dtype=F32, name=f"{tag}_in_dw", into=(g_in, 0))
    dx, dx16, d_gain = _in_proj_bwd(dqkv, w_in, j, x, dxo, gain, f"{tag}_in_bwd")
    return dx, dx16, d_gain, g_in, g_out


def _forward_backward(x, target, w, first, rest_after, token, on_reduced):
    depth = w["norm_ffn1"].shape[0]
    row = lambda a, l: a[l][None]
    rest = {}

    def mats(names, j):
        if j == 0 and names[0] in first:
            return [first[name] for name in names] + [0]
        return [rest[name] for name in names] + [j - (1 if names[0] in first else 0)]

    def even_small(j):
        return {"conv_w": w["dn_conv_w"][j], "narrow": _narrow_params(w["dn_a_log"][j], w["dn_dt_bias"][j],
                                                                     w["fox_f_bias"][j]),
                "dn_norm_g": row(w["dn_norm_g"], j), "q_g": row(w["fox_q_norm_g"], j),
                "k_g": row(w["fox_k_norm_g"], j)}

    saved = []
    for l in range(depth):
        if l == 1:
            rest.update(rest_after(x))
        gain = row(w["norm_ffn1"], l) + token[0:1, 0:1] if l == 0 else row(w["norm_ffn1"], l)
        x, s1 = _ffn_fwd(x, gain, *mats(("ffn1_w_gu", "ffn1_w_down"), l), "ffn1")
        if l % 2 == 0:
            x, s2 = _even_fwd(x, row(w["norm_mix"], l), *mats(("w_in_even", "w_out_even"), l // 2),
                              even_small(l // 2), "even")
        else:
            x, s2 = _odd_fwd(x, row(w["norm_mix"], l), *mats(("w_in_odd", "w_out_odd"), l // 2), "odd")
        x, s3 = _ffn_fwd(x, row(w["norm_ffn2"], l), *mats(("ffn2_w_gu", "ffn2_w_down"), l), "ffn2")
        saved.append((s1, s2, s3))

    dx, dx16, loss = _loss_grad(x, target, "loss")

    kind_of = dict(BIG)
    d_norm = {k: [None] * depth for k in ("norm_ffn1", "norm_mix", "norm_ffn2")}
    d_even = [None] * ((depth + 1) // 2)
    pending, token = None, None
    for l in reversed(range(depth)):
        s1, s2, s3 = saved[l]
        mixer = ("w_in_even", "w_out_even") if l % 2 == 0 else ("w_in_odd", "w_out_odd")
        names = ["ffn1_w_gu", "ffn1_w_down", *mixer, "ffn2_w_gu", "ffn2_w_down"]
        g = {name: lax.empty((1,) + rest[name].shape[1:], F32) for name in names}
        dx, dx16, d_norm["norm_ffn2"][l], g["ffn2_w_gu"], g["ffn2_w_down"] = _ffn_bwd(
            dx, dx16, s3, row(w["norm_ffn2"], l), *mats(("ffn2_w_gu", "ffn2_w_down"), l), "ffn2", g["ffn2_w_gu"],
            g["ffn2_w_down"], after=token)
        if l % 2 == 0:
            dx, dx16, d_norm["norm_mix"][l], d_even[l // 2], g["w_in_even"], g["w_out_even"] = _even_bwd(
                dx, dx16, s2, row(w["norm_mix"], l), *mats(("w_in_even", "w_out_even"), l // 2), even_small(l // 2),
                "even", g["w_in_even"], g["w_out_even"])
            g["w_in_even"] = _even_grad_quarters(g["w_in_even"])
        else:
            dx, dx16, d_norm["norm_mix"][l], g["w_in_odd"], g["w_out_odd"] = _odd_bwd(
                dx, dx16, s2, row(w["norm_mix"], l), *mats(("w_in_odd", "w_out_odd"), l // 2), "odd", g["w_in_odd"],
                g["w_out_odd"])
        dx, dx16, d_norm["norm_ffn1"][l], g["ffn1_w_gu"], g["ffn1_w_down"] = _ffn_bwd(
            dx, dx16, s1, row(w["norm_ffn1"], l), *mats(("ffn1_w_gu", "ffn1_w_down"), l), "ffn1", g["ffn1_w_gu"],
            g["ffn1_w_down"])
        above = pending
        pending, token = _reduce_start([g[name] for name in names], [kind_of[name] for name in names], names,
                                       f"layer{l}")
        if above is not None:
            on_reduced(l + 1, dict(zip(above[-2], _reduce_finish(above, dx))))
    on_reduced(0, dict(zip(pending[-2], _reduce_finish(pending, dx))))

    small = {k: jnp.concatenate(v, axis=0) for k, v in d_norm.items()}
    dec = slice(LANE_DECAY, LANE_DECAY + N_DN_HEADS)
    fgt = slice(LANE_FORGET, LANE_FORGET + N_FOX_HEADS)
    small["dn_conv_w"] = jnp.stack([e["conv_w"] for e in d_even])
    small["dn_a_log"] = jnp.concatenate([e["a_log"][:, dec] for e in d_even], axis=0)
    small["dn_dt_bias"] = jnp.concatenate([e["dt_bias"][:, dec] for e in d_even], axis=0)
    small["fox_f_bias"] = jnp.concatenate([e["f_bias"][:, fgt] for e in d_even], axis=0)
    small["dn_norm_g"] = jnp.concatenate([e["dn_norm_g"] for e in d_even], axis=0)
    small["fox_q_norm_g"] = jnp.concatenate([e["q_g"] for e in d_even], axis=0)
    small["fox_k_norm_g"] = jnp.concatenate([e["k_g"] for e in d_even], axis=0)
    return loss, dx, small


MESH = pl.DeviceIdType.MESH
ANY = pl.BlockSpec(memory_space=pl.ANY)


def _place():
    x, y, c = lax.axis_index("x"), lax.axis_index("y"), lax.axis_index("c")
    return x, y, c, [(1 - x, y), (x, 1 - y), (1 - x, 1 - y)]


def _remote(src, dst, send_sem, recv_sem, to):
    return pltpu.make_async_remote_copy(src_ref=src, dst_ref=dst, send_sem=send_sem, recv_sem=recv_sem,
                                        device_id=to, device_id_type=MESH)


def _aligned(start, multiple):
    return start if isinstance(start, int) else pl.multiple_of(start, multiple)


def _quarter(ref, kind, chip, half, rows, cols):
    k = 2 * chip[0] + chip[1]
    hr = rows // 2
    assert hr % 16 == 0 and cols % LANES == 0
    if kind == "col":
        return ref.at[:, pl.ds(_aligned(half * hr, 16), hr), pl.ds(_aligned(k * cols, LANES), cols)]
    return ref.at[:, pl.ds(_aligned(k * rows + half * hr, 16), hr), :]


def _place_quarter(shard, kind, kc, name, first=0, count=None):
    l, rows, cols = shard.shape
    l = l - first if count is None else count
    tr = rows
    while tr * cols * 4 > (2 << 20) and tr % 32 == 0:
        tr //= 2
    nr = rows // tr
    if kind == "col":
        out_spec = pl.BlockSpec((None, tr, cols), lambda li, i, kc_ref: (li, i, kc_ref[0]))
        out_shape = (l, rows, 4 * cols)
    else:
        out_spec = pl.BlockSpec((None, tr, cols), lambda li, i, kc_ref: (li, kc_ref[0] * nr + i, 0))
        out_shape = (l, 4 * rows, cols)

    def body(kc_ref, x_ref, o_ref):
        o_ref[...] = x_ref[...].astype(BF16)

    return pl.pallas_call(
        body, name=name,
        grid_spec=pltpu.PrefetchScalarGridSpec(
            num_scalar_prefetch=1, grid=(l, nr),
            in_specs=[pl.BlockSpec((None, tr, cols), lambda li, i, kc_ref: (li + first, i, 0))],
            out_specs=out_spec),
        out_shape=jax.ShapeDtypeStruct(out_shape, BF16),
        compiler_params=_cparams("parallel", "parallel"),
    )(kc, shard)


def _gather_weights(wholes, kinds):
    n = len(wholes)

    def dims(ref, kind):
        _, r, cc = ref.shape
        return (r, cc // 4) if kind == "col" else (r // 4, cc)

    def body(*refs):
        bufs = refs[n:2 * n]
        send_sems, recv_sems = refs[2 * n:]
        x, y, c, chips = _place()
        sibling = (x, y, 1 - c)
        first, passed = [], []
        for t in range(n):
            rows, cols = dims(bufs[t], kinds[t])
            mine = _quarter(bufs[t], kinds[t], (x, y), c, rows, cols)
            for j, chip in enumerate(chips):
                cp = _remote(mine, mine, send_sems.at[t, j], recv_sems.at[t, j], (*chip, c))
                cp.start()
                first.append(cp)
        for j, chip in enumerate(chips):
            for t in range(n):
                rows, cols = dims(bufs[t], kinds[t])
                got = _quarter(bufs[t], kinds[t], chip, c, rows, cols)
                _remote(got, got, send_sems.at[t, j], recv_sems.at[t, j], (*chip, c)).wait_recv()
                cp = _remote(got, got, send_sems.at[t, 3 + j], recv_sems.at[t, 3 + j], sibling)
                cp.start()
                passed.append(cp)
        for j, chip in enumerate(chips):
            for t in range(n):
                rows, cols = dims(bufs[t], kinds[t])
                got = _quarter(bufs[t], kinds[t], chip, 1 - c, rows, cols)
                _remote(got, got, send_sems.at[t, 3 + j], recv_sems.at[t, 3 + j], sibling).wait_recv()
        for cp in first + passed:
            cp.wait_send()

    return pl.pallas_call(
        body, name="gather_weights", in_specs=[ANY] * n, out_specs=[ANY] * n,
        out_shape=[jax.ShapeDtypeStruct(a.shape, a.dtype) for a in wholes],
        input_output_aliases={t: t for t in range(n)},
        scratch_shapes=[pltpu.SemaphoreType.DMA((n, 6)), pltpu.SemaphoreType.DMA((n, 6))],
        compiler_params=pltpu.CompilerParams(has_side_effects=True),
    )(*wholes)


def _quarter_dims(ref, kind):
    _, r, cc = ref.shape
    return (r, cc // 4) if kind == "col" else (r // 4, cc)


def _gather_chips_copies(bufs, sems, kinds):
    x, y, c, chips = _place()
    copies = []
    for t, buf in enumerate(bufs):
        rows, cols = _quarter_dims(buf, kinds[t])
        mine = _quarter(buf, kinds[t], (x, y), c, rows, cols)
        for j, chip in enumerate(chips):
            pair = 2 * (OTHER_CHIPS * t + j)
            copies.append(_remote(mine, mine, sems[pair], sems[pair + 1], (*chip, c)))
    return copies


def _gather_start(wholes, kinds, after, tag):
    n = len(wholes)
    n_sems = 2 * OTHER_CHIPS * n
    n_in = n + len(after)

    def body(*refs):
        for cp in _gather_chips_copies(refs[:n], refs[n_in + n:n_in + n + n_sems], kinds):
            cp.start()
        refs[-1][...] = jnp.zeros_like(refs[-1])

    held = [pltpu.with_memory_space_constraint(a, pltpu.HBM) for a in wholes]
    out = pl.pallas_call(
        body, name=f"gather_start_{tag}", in_specs=[HBM] * n + [ANY] * len(after),
        out_specs=(*[HBM] * n, *[SEM] * n_sems, pl.BlockSpec(memory_space=pltpu.VMEM)),
        out_shape=(*[pltpu.HBM(a.shape, a.dtype) for a in held], *[pltpu.SemaphoreType.DMA(())] * n_sems,
                   jax.ShapeDtypeStruct((8, LANES), F32)),
        input_output_aliases={i: i for i in range(n)},
        compiler_params=pltpu.CompilerParams(has_side_effects=SPLIT_COPY),
    )(*held, *after)
    return out[n:n + n_sems], out[:n], out[-1]


def _gather_wait(sems, wholes, kinds, after, tag):
    n = len(wholes)

    def body(*refs):
        for cp in _gather_chips_copies(refs[:n], refs[n:n + len(sems)], kinds):
            cp.wait_send()
            cp.wait_recv()

    return pl.pallas_call(
        body, name=f"gather_wait_{tag}", in_specs=[HBM] * n + [SEM] * len(sems) + [ANY],
        out_specs=tuple([HBM] * n), out_shape=tuple(pltpu.HBM(a.shape, a.dtype) for a in wholes),
        input_output_aliases={i: i for i in range(n)},
        compiler_params=pltpu.CompilerParams(has_side_effects=SPLIT_COPY),
    )(*wholes, *sems, after)


def _gather_forward(wholes, kinds, tag):
    n = len(wholes)

    def body(*refs):
        bufs = refs[n:2 * n]
        send_sems, recv_sems = refs[2 * n:]
        x, y, c, chips = _place()
        copies = []
        for t in range(n):
            rows, cols = _quarter_dims(bufs[t], kinds[t])
            for j, chip in enumerate(chips):
                got = _quarter(bufs[t], kinds[t], chip, c, rows, cols)
                cp = _remote(got, got, send_sems.at[t, j], recv_sems.at[t, j], (x, y, 1 - c))
                cp.start()
                copies.append(cp)
        for cp in copies:
            cp.wait_send()
        for t in range(n):
            rows, cols = _quarter_dims(bufs[t], kinds[t])
            for j, chip in enumerate(chips):
                got = _quarter(bufs[t], kinds[t], chip, 1 - c, rows, cols)
                _remote(got, got, send_sems.at[t, j], recv_sems.at[t, j], (x, y, 1 - c)).wait_recv()

    return pl.pallas_call(
        body, name=f"gather_forward_{tag}", in_specs=[ANY] * n, out_specs=[ANY] * n,
        out_shape=[jax.ShapeDtypeStruct(a.shape, a.dtype) for a in wholes],
        input_output_aliases={t: t for t in range(n)},
        scratch_shapes=[pltpu.SemaphoreType.DMA((n, OTHER_CHIPS)), pltpu.SemaphoreType.DMA((n, OTHER_CHIPS))],
        compiler_params=pltpu.CompilerParams(has_side_effects=True),
    )(*wholes)


def _canonical(a, kind):
    l, r, c = a.shape
    return a.reshape(l, 1, r, c) if kind == "col" else a.reshape(l, 4, r // 4, c)


def _rs_sibling(parts):
    n = len(parts)

    def body(*refs):
        ins, outs = refs[:n], refs[n:2 * n]
        send_sems, recv_sems = refs[2 * n:]
        x, y, c, _ = _place()
        copies = []
        for t in range(n):
            hr = ins[t].shape[2] // 2
            src = ins[t].at[:, :, pl.ds(pl.multiple_of((1 - c) * hr, 8), hr), :]
            cp = _remote(src, outs[t], send_sems.at[t], recv_sems.at[t], (x, y, 1 - c))
            cp.start()
            copies.append(cp)
        for cp in copies:
            cp.wait()

    half = lambda a: jax.ShapeDtypeStruct(a.shape[:2] + (a.shape[2] // 2, a.shape[3]), a.dtype)
    return pl.pallas_call(
        body, name="reduce_sibling", in_specs=[ANY] * n, out_specs=[ANY] * n, out_shape=[half(a) for a in parts],
        scratch_shapes=[pltpu.SemaphoreType.DMA((n,)), pltpu.SemaphoreType.DMA((n,))],
        compiler_params=pltpu.CompilerParams(has_side_effects=True),
    )(*parts)


def _add_tile(rows, cols):
    tc = cols if cols <= 1536 else cols // 4
    tr = rows
    while tr * tc * 4 > (1 << 20) and tr % 16 == 0:
        tr //= 2
    return tr, tc


def _rs_add_sibling(part, got, c, name):
    l, a, hr, cols = got.shape
    tr, tc = _add_tile(hr, cols)
    nr = hr // tr

    def body(c_ref, p_ref, g_ref, o32_ref, o16_ref):
        s = p_ref[...] + g_ref[...]
        o32_ref[...] = s
        o16_ref[...] = s.astype(BF16)

    blk = (None, None, tr, tc)
    spec = pl.BlockSpec(blk, lambda li, ai, i, j, c_ref: (li, ai, i, j))
    return pl.pallas_call(
        body, name=name,
        grid_spec=pltpu.PrefetchScalarGridSpec(
            num_scalar_prefetch=1, grid=(l, a, nr, cols // tc),
            in_specs=[pl.BlockSpec(blk, lambda li, ai, i, j, c_ref: (li, ai, c_ref[0] * nr + i, j)), spec],
            out_specs=[spec, spec]),
        out_shape=[jax.ShapeDtypeStruct(got.shape, F32), jax.ShapeDtypeStruct(got.shape, BF16)],
        compiler_params=_cparams("parallel", "parallel", "parallel", "parallel"),
    )(c, part, got)


def _quarter4(ref, kind, chip, cols):
    k = 2 * chip[0] + chip[1]
    if kind == "col":
        return ref.at[:, :, :, pl.ds(pl.multiple_of(k * cols, LANES), cols)]
    return ref.at[:, pl.ds(k, 1), :, :]


HBM = pl.BlockSpec(memory_space=pltpu.HBM)
SEM = pl.BlockSpec(memory_space=pltpu.SEMAPHORE)
SPLIT_COPY = pltpu.SideEffectType.DATAFLOW_SIDE_EFFECTING
OTHER_CHIPS = 3


def _quarter4_shape(a, kind):
    l, _, hr, cols = a.shape
    return (l, 1, hr, cols // 4 if kind == "col" else cols)


def _rs_chips_copies(srcs, lands, sems, kinds):
    x, y, c, chips = _place()
    copies = []
    for t, (src, land) in enumerate(zip(srcs, lands)):
        cols = _quarter4_shape(src, kinds[t])[3]
        for j, chip in enumerate(chips):
            pair = 2 * (OTHER_CHIPS * t + j)
            copies.append(_remote(_quarter4(src, kinds[t], chip, cols), land.at[j], sems[pair], sems[pair + 1],
                                  (*chip, c)))
    return copies


def _rs_chips_start(sums16, kinds, tag):
    n = len(sums16)
    n_sems = 2 * OTHER_CHIPS * n

    def body(*refs):
        srcs, lands, sems = refs[:n], refs[n:2 * n], refs[4 * n:4 * n + n_sems]
        for cp in _rs_chips_copies(srcs, lands, sems, kinds):
            cp.start()
        refs[-1][...] = jnp.zeros_like(refs[-1])

    lands = [lax.empty((OTHER_CHIPS,) + _quarter4_shape(a, k), a.dtype) for a, k in zip(sums16, kinds)]
    held = [pltpu.with_memory_space_constraint(a, pltpu.HBM) for a in (*sums16, *lands)]
    out = pl.pallas_call(
        body, name=f"reduce_chips_start_{tag}", in_specs=[HBM] * (2 * n),
        out_specs=(*[HBM] * (2 * n), *[SEM] * n_sems, pl.BlockSpec(memory_space=pltpu.VMEM)),
        out_shape=(*[pltpu.HBM(a.shape, a.dtype) for a in held], *[pltpu.SemaphoreType.DMA(())] * n_sems,
                   jax.ShapeDtypeStruct((8, LANES), F32)),
        input_output_aliases={i: i for i in range(2 * n)},
        compiler_params=pltpu.CompilerParams(has_side_effects=SPLIT_COPY),
    )(*held)
    return out[2 * n:2 * n + n_sems], out[:n], out[n:2 * n], out[-1]


def _rs_chips_wait(sems, srcs, lands, kinds, after, tag):
    n = len(srcs)

    def body(*refs):
        for cp in _rs_chips_copies(refs[:n], refs[n:2 * n], refs[2 * n:2 * n + len(sems)], kinds):
            cp.wait_send()
            cp.wait_recv()

    out = pl.pallas_call(
        body, name=f"reduce_chips_wait_{tag}", in_specs=[HBM] * (2 * n) + [SEM] * len(sems) + [ANY],
        out_specs=tuple([HBM] * (2 * n)),
        out_shape=tuple(pltpu.HBM(a.shape, a.dtype) for a in (*srcs, *lands)),
        input_output_aliases={i: i for i in range(2 * n)},
        compiler_params=pltpu.CompilerParams(has_side_effects=SPLIT_COPY),
    )(*srcs, *lands, *sems, after)
    return out[n:]


def _rs_add_chips(sum32, got, kind, kc, name):
    _, l, _, hr, cols = got.shape
    tr, _ = _add_tile(hr, cols)
    nr = hr // tr
    k_arr, c_arr = kc
    if kind == "col":
        own = pl.BlockSpec((None, None, tr, cols), lambda li, i, k_ref, c_ref: (li, 0, i, k_ref[0]))
    else:
        own = pl.BlockSpec((None, None, tr, cols), lambda li, i, k_ref, c_ref: (li, k_ref[0], i, 0))

    def body(k_ref, c_ref, own_ref, got_ref, o_ref):
        o_ref[...] = ((own_ref[...] + got_ref[0].astype(F32)) + got_ref[1].astype(F32)) + got_ref[2].astype(F32)

    return pl.pallas_call(
        body, name=name,
        grid_spec=pltpu.PrefetchScalarGridSpec(
            num_scalar_prefetch=2, grid=(l, nr),
            in_specs=[own, pl.BlockSpec((3, None, None, tr, cols), lambda li, i, k_ref, c_ref: (0, li, 0, i, 0))],
            out_specs=pl.BlockSpec((None, tr, cols), lambda li, i, k_ref, c_ref: (li, c_ref[0] * nr + i, 0))),
        out_shape=jax.ShapeDtypeStruct((l, 2 * hr, cols), F32),
        compiler_params=_cparams("parallel", "parallel"),
    )(k_arr, c_arr, sum32, got)


def _rs_finish(quarters):
    n = len(quarters)

    def body(*refs):
        bufs = refs[n:2 * n]
        send_sems, recv_sems = refs[2 * n:]
        x, y, c, _ = _place()
        copies = []
        for t in range(n):
            hr = bufs[t].shape[1] // 2
            mine = bufs[t].at[:, pl.ds(pl.multiple_of(c * hr, 8), hr), :]
            cp = _remote(mine, mine, send_sems.at[t], recv_sems.at[t], (x, y, 1 - c))
            cp.start()
            copies.append(cp)
        for cp in copies:
            cp.wait()

    return pl.pallas_call(
        body, name="reduce_finish", in_specs=[ANY] * n, out_specs=[ANY] * n,
        out_shape=[jax.ShapeDtypeStruct(a.shape, a.dtype) for a in quarters],
        input_output_aliases={t: t for t in range(n)},
        scratch_shapes=[pltpu.SemaphoreType.DMA((n,)), pltpu.SemaphoreType.DMA((n,))],
        compiler_params=pltpu.CompilerParams(has_side_effects=True),
    )(*quarters)


def _reduce_start(parts, kinds, names, tag):
    c_arr = jnp.reshape(lax.axis_index("c"), (1,)).astype(jnp.int32)
    canon = [_canonical(p, kind) for p, kind in zip(parts, kinds)]
    from_sibling = _rs_sibling(canon)
    sums = [_rs_add_sibling(p, g, c_arr, f"reduce_add_sibling_{nm}") for p, g, nm in zip(canon, from_sibling, names)]
    sems, srcs, lands, token = _rs_chips_start([s16 for _, s16 in sums], kinds, tag)
    return (sems, srcs, lands, [s32 for s32, _ in sums], kinds, names, tag), token


def _reduce_finish(state, after):
    sems, srcs, lands, sums32, kinds, names, tag = state
    x, y, c = lax.axis_index("x"), lax.axis_index("y"), lax.axis_index("c")
    kc = (jnp.reshape(2 * x + y, (1,)).astype(jnp.int32), jnp.reshape(c, (1,)).astype(jnp.int32))
    from_chips = _rs_chips_wait(sems, srcs, lands, kinds, after, tag)
    halves = [_rs_add_chips(s32, g, kind, kc, f"reduce_add_chips_{nm}")
              for s32, g, kind, nm in zip(sums32, from_chips, kinds, names)]
    return _rs_finish(halves)


SMALL_PEERS = 7


def _small_exchange(pack):
    rows = pack.shape[0]

    def body(p_ref, slots_ref, total_ref, send_sems, recv_sems):
        x, y, c, _ = _place()
        me = 4 * x + 2 * y + c
        slots_ref[me] = p_ref[...]
        copies = []
        for p in range(1, SMALL_PEERS + 1):
            px, py, pc = (p >> 2) & 1, (p >> 1) & 1, p & 1
            peer = (1 - x if px else x, 1 - y if py else y, 1 - c if pc else c)
            cp = _remote(p_ref, slots_ref.at[me], send_sems.at[p - 1], recv_sems.at[p - 1], peer)
            cp.start()
            copies.append(cp)
        for cp in copies:
            cp.wait()
        total = slots_ref[0]
        for i in range(1, SMALL_PEERS + 1):
            total = total + slots_ref[i]
        total_ref[...] = total

    vmem = pl.BlockSpec(memory_space=pltpu.VMEM)
    return pl.pallas_call(
        body, name="small_exchange", in_specs=[vmem], out_specs=[vmem, vmem],
        out_shape=[jax.ShapeDtypeStruct((SMALL_PEERS + 1, rows, LANES), F32), jax.ShapeDtypeStruct((rows, LANES), F32)],
        scratch_shapes=[pltpu.SemaphoreType.DMA((SMALL_PEERS,)), pltpu.SemaphoreType.DMA((SMALL_PEERS,))],
        compiler_params=pltpu.CompilerParams(has_side_effects=True),
    )(pack)


def _pack(arrays):
    rows = []
    for a in arrays:
        flat = a.reshape(-1).astype(F32)
        rows.append(jnp.pad(flat, (0, (-flat.shape[0]) % LANES)).reshape(-1, LANES))
    out = jnp.concatenate(rows, axis=0)
    return jnp.pad(out, ((0, (-out.shape[0]) % 8), (0, 0)))


def _unpack(pack, shapes):
    out, r = [], 0
    for sh in shapes:
        size = math.prod(sh)
        nr = -(-size // LANES)
        out.append(pack[r:r + nr].reshape(-1)[:size].reshape(sh))
        r += nr
    return out


def _adamw(w, g, m, v, name):
    shape = w.shape
    to2d = lambda a: a.reshape(-1, shape[-1])
    rows = math.prod(shape[:-1])
    tile = 256 if rows % 256 == 0 else rows

    def fn(wb, gb, mb, vb):
        m2 = ADAM_B1 * mb + (1.0 - ADAM_B1) * gb
        v2 = ADAM_B2 * vb + (1.0 - ADAM_B2) * (gb * gb)
        m_hat = m2 / (1.0 - ADAM_B1 ** ADAM_STEP)
        v_hat = v2 / (1.0 - ADAM_B2 ** ADAM_STEP)
        return -ADAM_LR * (m_hat / (jnp.sqrt(v_hat) + ADAM_EPS) + ADAM_WD * wb), m2, v2

    res = _rowwise(fn, [to2d(w), to2d(g), to2d(m), to2d(v)], [], [(shape[-1], F32)] * 3, [], tile=tile, name=name)
    return [r.reshape(shape) for r in res]


def _adamw_layer(w, g, m, v, layer, outs, name):
    _, rows, cols = w.shape
    tile = rows
    while tile * cols * 4 > (1 << 20) and tile % 16 == 0:
        tile //= 2

    def body(w_ref, g_ref, m_ref, v_ref, *rest):
        g_out, d_out, m_out, v_out = rest[-4:]
        gb = g_ref[...]
        m2 = ADAM_B1 * m_ref[...] + (1.0 - ADAM_B1) * gb
        v2 = ADAM_B2 * v_ref[...] + (1.0 - ADAM_B2) * (gb * gb)
        m_hat = m2 / (1.0 - ADAM_B1 ** ADAM_STEP)
        v_hat = v2 / (1.0 - ADAM_B2 ** ADAM_STEP)
        g_out[...] = gb
        d_out[...] = -ADAM_LR * (m_hat / (jnp.sqrt(v_hat) + ADAM_EPS) + ADAM_WD * w_ref[...])
        m_out[...] = m2
        v_out[...] = v2

    stacked = pl.BlockSpec((None, tile, cols), lambda i: (layer, i, 0))
    return pl.pallas_call(
        body, name=name, grid=(rows // tile,),
        in_specs=[stacked, pl.BlockSpec((None, tile, cols), lambda i: (0, i, 0)), stacked, stacked] + [ANY] * 4,
        out_specs=[stacked] * 4, out_shape=[jax.ShapeDtypeStruct(w.shape, F32)] * 4,
        input_output_aliases={4 + i: i for i in range(4)}, compiler_params=_cparams("parallel"),
    )(w, g, m, v, *outs)


BIG = (("ffn1_w_gu", "col"), ("ffn1_w_down", "row"), ("w_in_even", "col"), ("w_out_even", "row"),
       ("w_in_odd", "col"), ("w_out_odd", "row"), ("ffn2_w_gu", "col"), ("ffn2_w_down", "row"))
SMALL = ("norm_ffn1", "norm_mix", "dn_conv_w", "dn_a_log", "dn_dt_bias", "dn_norm_g", "fox_q_norm_g", "fox_k_norm_g",
         "fox_f_bias", "norm_ffn2")
WEIGHTS = ("norm_ffn1", "ffn1_w_gu", "ffn1_w_down", "norm_mix", "w_in_even", "dn_conv_w", "dn_a_log", "dn_dt_bias",
           "dn_norm_g", "fox_q_norm_g", "fox_k_norm_g", "fox_f_bias", "w_out_even", "w_in_odd", "w_out_odd",
           "norm_ffn2", "ffn2_w_gu", "ffn2_w_down")


def _step(x, target, w, m, v):
    k = 2 * lax.axis_index("x") + lax.axis_index("y")
    n_conv = w["dn_conv_w"].shape[2]

    kc = jnp.reshape(k, (1,)).astype(jnp.int32)
    kinds = dict(BIG)
    quarters = {name: w[name] for name in kinds}
    quarters["w_in_even"] = jnp.pad(w["w_in_even"], ((0, 0), (0, 0), (0, EVEN_QUARTER_PAD - EVEN_QUARTER)))
    first_names = [name for name in kinds if name not in ("w_in_odd", "w_out_odd")]
    rest_names = list(kinds)

    def even_columns(whole):
        padded = whole["w_in_even"]
        ref_order = jnp.concatenate([padded[..., q * EVEN_QUARTER_PAD:q * EVEN_QUARTER_PAD + EVEN_QUARTER]
                                     for q in range(4)], axis=-1)
        return {**whole, "w_in_even": _even_to_kernel_layout(ref_order)}

    conv_slots, _ = _small_exchange(_pack([w["dn_conv_w"]]))
    placed = [_place_quarter(quarters[name], kinds[name], kc, f"place_first_{name}", 0, 1) for name in first_names]
    gathered = _gather_weights(placed, [kinds[name] for name in first_names])
    first = even_columns(dict(zip(first_names, gathered)))
    placed = [_place_quarter(quarters[name], kinds[name], kc, f"place_rest_{name}", 1 if name in first_names else 0)
              for name in rest_names]
    rest_kinds = [kinds[name] for name in rest_names]
    sems, on_their_way, token = _gather_start(placed, rest_kinds, [conv_slots, *gathered], "rest")

    def rest_after(value):
        landed = _gather_wait(sems, on_their_way, rest_kinds, value, "rest")
        return even_columns(dict(zip(rest_names, _gather_forward(landed, rest_kinds, "rest"))))

    whole = {}
    conv_rows = math.prod(w["dn_conv_w"].shape) // LANES
    conv_quarters = [conv_slots[2 * q, :conv_rows].reshape(w["dn_conv_w"].shape) for q in range(4)]
    whole["dn_conv_w"] = jnp.concatenate(conv_quarters, axis=-1)
    for name in SMALL:
        if name != "dn_conv_w":
            whole[name] = w[name]

    updated = {name: [lax.empty(w[name].shape, F32) for _ in range(4)] for name in kinds}

    def on_reduced(layer, layer_grads):
        for name, g in layer_grads.items():
            if name == "w_in_even":
                g = g[..., :EVEN_QUARTER]
            stacked_layer = layer if w[name].shape[0] == w["norm_mix"].shape[0] else layer // 2
            updated[name] = _adamw_layer(w[name], g, m[name], v[name], stacked_layer, updated[name], f"adamw_{name}")

    loss, dx, small = _forward_backward(x, target, whole, first, rest_after, token, on_reduced)

    _, small_sum = _small_exchange(_pack([small[n] for n in SMALL]))
    grads = dict(zip(SMALL, _unpack(small_sum, [small[n].shape for n in SMALL])))
    grads["dn_conv_w"] = lax.dynamic_slice_in_dim(grads["dn_conv_w"], k * n_conv, n_conv, axis=2)
    delta, new_m, new_v = {}, {}, {}
    for name in kinds:
        grads[name], delta[name], new_m[name], new_v[name] = updated[name]
    packs = [_pack([d[n] for n in SMALL]) for d in (w, grads, m, v)]
    shapes = [w[n].shape for n in SMALL]
    for out, res in zip((delta, new_m, new_v), _adamw(*packs, "adamw_small")):
        out.update(zip(SMALL, _unpack(res, shapes)))
    total_loss = lax.psum(loss[0, 0], ("x", "y", "c"))
    return total_loss, dx, grads, delta, new_m, new_v


def kernel(x, norm_ffn1, ffn1_w_gu, ffn1_w_down, norm_mix, w_in_even, dn_conv_w, dn_a_log, dn_dt_bias, dn_norm_g, fox_q_norm_g, fox_k_norm_g, fox_f_bias, w_out_even, w_in_odd, w_out_odd, norm_ffn2, ffn2_w_gu, ffn2_w_down, loss_target, m_norm_ffn1, m_ffn1_w_gu, m_ffn1_w_down, m_norm_mix, m_w_in_even, m_dn_conv_w, m_dn_a_log, m_dn_dt_bias, m_dn_norm_g, m_fox_q_norm_g, m_fox_k_norm_g, m_fox_f_bias, m_w_out_even, m_w_in_odd, m_w_out_odd, m_norm_ffn2, m_ffn2_w_gu, m_ffn2_w_down, v_norm_ffn1, v_ffn1_w_gu, v_ffn1_w_down, v_norm_mix, v_w_in_even, v_dn_conv_w, v_dn_a_log, v_dn_dt_bias, v_dn_norm_g, v_fox_q_norm_g, v_fox_k_norm_g, v_fox_f_bias, v_w_out_even, v_w_in_odd, v_w_out_odd, v_norm_ffn2, v_ffn2_w_gu, v_ffn2_w_down):
    w = dict(zip(WEIGHTS, (norm_ffn1, ffn1_w_gu, ffn1_w_down, norm_mix, w_in_even, dn_conv_w, dn_a_log, dn_dt_bias,
                           dn_norm_g, fox_q_norm_g, fox_k_norm_g, fox_f_bias, w_out_even, w_in_odd, w_out_odd,
                           norm_ffn2, ffn2_w_gu, ffn2_w_down)))
    m = dict(zip(WEIGHTS, (m_norm_ffn1, m_ffn1_w_gu, m_ffn1_w_down, m_norm_mix, m_w_in_even, m_dn_conv_w, m_dn_a_log,
                           m_dn_dt_bias, m_dn_norm_g, m_fox_q_norm_g, m_fox_k_norm_g, m_fox_f_bias, m_w_out_even,
                           m_w_in_odd, m_w_out_odd, m_norm_ffn2, m_ffn2_w_gu, m_ffn2_w_down)))
    v = dict(zip(WEIGHTS, (v_norm_ffn1, v_ffn1_w_gu, v_ffn1_w_down, v_norm_mix, v_w_in_even, v_dn_conv_w, v_dn_a_log,
                           v_dn_dt_bias, v_dn_norm_g, v_fox_q_norm_g, v_fox_k_norm_g, v_fox_f_bias, v_w_out_even,
                           v_w_in_odd, v_w_out_odd, v_norm_ffn2, v_ffn2_w_gu, v_ffn2_w_down)))
    loss, dx, grads, delta, new_m, new_v = _step(x[0], loss_target[0], w, m, v)
    return (loss, dx[None], *[grads[n] for n in WEIGHTS], *[delta[n] for n in WEIGHTS],
            *[new_m[n] for n in WEIGHTS], *[new_v[n] for n in WEIGHTS])
```

```python
import functools
import math

import jax
import jax.numpy as jnp
from jax import lax
from jax.experimental import pallas as pl
from jax.experimental.pallas import tpu as pltpu

F32 = jnp.float32
BF16 = jnp.bfloat16
HI = lax.Precision.HIGH

HEAD_DIM = 128
N_DN_HEADS = 4
N_FOX_HEADS = 4
N_SB_HEADS = 8
D_DN = N_DN_HEADS * HEAD_DIM
D_FOX = N_FOX_HEADS * HEAD_DIM
CONV_WIDTH = 4
DN_CHUNK = 64
EPS = 1e-6
ATT_SCALE = HEAD_DIM ** -0.5
ADAM_LR, ADAM_B1, ADAM_B2, ADAM_EPS, ADAM_WD, ADAM_STEP = 0.001, 0.9, 0.999, 1e-08, 0.01, 10

V7X_VMEM_LIMIT = 56 * 1024 * 1024
LANES = 128
ATT_TQ = 256
ATT_TK = 128
ATT_SUB = ATT_TQ // ATT_TK

LANE_BETA, LANE_DECAY, LANE_FORGET = 0, 4, 8


def _cparams(*sem):
    return pltpu.CompilerParams(dimension_semantics=sem, vmem_limit_bytes=V7X_VMEM_LIMIT)


def _sigmoid(x):
    return 1.0 / (1.0 + jnp.exp(-x))


def _softplus(x):
    return jnp.maximum(x, 0.0) + jnp.log(1.0 + jnp.exp(-jnp.abs(x)))


def _silu_grad(y, sg):
    return sg * (1.0 + y * (1.0 - sg))


def _rowwise(fn, rows, bcast, outs, sums, *, tile, name):
    rows = [r if isinstance(r, tuple) else (r, r.shape[1], 0) for r in rows]
    s = rows[0][0].shape[0]
    assert s % tile == 0
    n_in, n_b, n_out, n_sum = len(rows), len(bcast), len(outs), len(sums)

    def body(*refs):
        ins = [r[...] for r in refs[:n_in + n_b]]
        res = fn(*ins)
        if not isinstance(res, (tuple, list)):
            res = (res,)
        out_refs = refs[n_in + n_b:n_in + n_b + n_out]
        sum_refs = refs[n_in + n_b + n_out:]
        for o_ref, val in zip(out_refs, res[:n_out]):
            o_ref[...] = val.astype(o_ref.dtype)
        if n_sum:
            @pl.when(pl.program_id(0) == 0)
            def _():
                for s_ref in sum_refs:
                    s_ref[...] = jnp.zeros_like(s_ref)
            for s_ref, val in zip(sum_refs, res[n_out:]):
                s_ref[...] += val

    in_specs = [pl.BlockSpec((tile, w), lambda i, cb=cb: (i, cb)) for _, w, cb in rows]
    in_specs += [pl.BlockSpec(b.shape, lambda i, nd=b.ndim: (0,) * nd) for b in bcast]
    out_specs = [pl.BlockSpec((tile, c), lambda i: (i, 0)) for c, _ in outs]
    out_specs += [pl.BlockSpec(sh, lambda i: (0, 0)) for sh in sums]
    out_shape = [jax.ShapeDtypeStruct((s, c), dt) for c, dt in outs]
    out_shape += [jax.ShapeDtypeStruct(sh, F32) for sh in sums]
    return pl.pallas_call(
        body, name=name, grid=(s // tile,), in_specs=in_specs, out_specs=out_specs, out_shape=out_shape,
        compiler_params=_cparams("arbitrary" if n_sum else "parallel"),
    )(*[r[0] for r in rows], *bcast)


def _rms_fwd(x, gain, name):
    def fn(xb, g):
        r = lax.rsqrt(jnp.mean(xb * xb, axis=-1, keepdims=True) + EPS)
        return (xb * r * g,)
    return _rowwise(fn, [x], [gain], [(x.shape[1], BF16)], [], tile=512, name=name)[0]


_DIMS = {"nn": (((1,), (0,)), ((), ())), "nt": (((1,), (1,)), ((), ())), "tn": (((0,), (0,)), ((), ()))}


def _dot(a, b, kind):
    return lax.dot_general(a.astype(BF16), b.astype(BF16), _DIMS[kind], preferred_element_type=F32)


def _dot32(a, b, kind="nn"):
    return lax.dot_general(a, b, _DIMS[kind], precision=HI, preferred_element_type=F32)


def _mm(a, b, kind, *, tm, tn, out_dtype, name, scale=None, residual=None, a_lead=(), b_lead=(),
        b_spec=None, n=None, into=None):
    ash, bsh = a.shape[len(a_lead):], b.shape[len(b_lead):]
    m = ash[1] if kind == "tn" else ash[0]
    k = ash[0] if kind == "tn" else ash[1]
    if b_spec is None:
        n = bsh[0] if kind == "nt" else bsh[1]
        assert k == (bsh[1] if kind == "nt" else bsh[0]), (ash, bsh, kind)
    assert m % tm == 0 and n % tn == 0, (m, tm, n, tn)
    la, lb = (None,) * len(a_lead), (None,) * len(b_lead)
    if kind == "tn":
        a_spec = pl.BlockSpec(la + (k, tm), lambda j, i: a_lead + (0, i))
    else:
        a_spec = pl.BlockSpec(la + (tm, k), lambda j, i: a_lead + (i, 0))
    if b_spec is None:
        if kind == "nt":
            b_spec = pl.BlockSpec(lb + (tn, k), lambda j, i: b_lead + (j, 0))
        else:
            b_spec = pl.BlockSpec(lb + (k, tn), lambda j, i: b_lead + (0, j))
    in_specs, args = [a_spec, b_spec], [a, b]
    if residual is not None:
        in_specs.append(pl.BlockSpec((tm, tn), lambda j, i: (i, j)))
        args.append(residual)
    aliases = {}
    if into is not None:
        buf, layer = into
        in_specs.append(pl.BlockSpec(memory_space=pl.ANY))
        args.append(buf)
        aliases = {len(args) - 1: 0}
        out_spec = pl.BlockSpec((None, tm, tn), lambda j, i: (layer, i, j))
        out_shape = jax.ShapeDtypeStruct(buf.shape, buf.dtype)
    else:
        out_spec = pl.BlockSpec((tm, tn), lambda j, i: (i, j))
        out_shape = jax.ShapeDtypeStruct((m, n), out_dtype)

    def body(a_ref, b_ref, *rest):
        acc = _dot(a_ref[...], b_ref[...], kind)
        if scale is not None:
            acc = acc * scale
        if residual is not None:
            acc = acc + rest[0][...]
        rest[-1][...] = acc.astype(rest[-1].dtype)

    return pl.pallas_call(
        body, name=name, grid=(n // tn, m // tm), in_specs=in_specs, out_specs=out_spec, out_shape=out_shape,
        input_output_aliases=aliases, compiler_params=_cparams("parallel", "parallel"),
    )(*args)


def _ffn_up(n, w_gu, layer, name):
    s, d = n.shape
    f = w_gu.shape[2] // 2
    tm, tn = 512, f // 2
    nj = f // tn

    def body(n_ref, wg_ref, wu_ref, gu_ref, a_ref):
        nv = n_ref[...]
        g = _dot(nv, wg_ref[...], "nn")
        u = _dot(nv, wu_ref[...], "nn")
        gu_ref[0] = g.astype(BF16)
        gu_ref[1] = u.astype(BF16)
        a_ref[...] = (g * _sigmoid(g) * u).astype(BF16)

    return pl.pallas_call(
        body, name=name, grid=(nj, s // tm),
        in_specs=[pl.BlockSpec((tm, d), lambda j, i: (i, 0)),
                  pl.BlockSpec((None, d, tn), lambda j, i: (layer, 0, j)),
                  pl.BlockSpec((None, d, tn), lambda j, i: (layer, 0, j + nj))],
        out_specs=[pl.BlockSpec((2, tm, tn), lambda j, i: (0, i, j)),
                   pl.BlockSpec((tm, tn), lambda j, i: (i, j))],
        out_shape=[jax.ShapeDtypeStruct((2, s, f), BF16), jax.ShapeDtypeStruct((s, f), BF16)],
        compiler_params=_cparams("parallel", "parallel"),
    )(n, w_gu, w_gu)


def _ffn_down_bwd(dxo, w_down, gu, layer, name, after=None):
    s, d = dxo.shape
    f = w_down.shape[1]
    tm, tn = 512, f // 2
    extra_specs, extra = ([ANY], [after]) if after is not None else ([], [])

    def body(dx_ref, w_ref, gu_ref, *rest):
        dgu_ref = rest[-1]
        da = 0.5 * _dot(dx_ref[...], w_ref[...], "nt")
        g = gu_ref[0].astype(F32)
        u = gu_ref[1].astype(F32)
        sg = _sigmoid(g)
        dgu_ref[0] = (da * u * _silu_grad(g, sg)).astype(BF16)
        dgu_ref[1] = (da * g * sg).astype(BF16)

    return pl.pallas_call(
        body, name=name, grid=(f // tn, s // tm),
        in_specs=[pl.BlockSpec((tm, d), lambda j, i: (i, 0)),
                  pl.BlockSpec((None, tn, d), lambda j, i: (layer, j, 0)),
                  pl.BlockSpec((2, tm, tn), lambda j, i: (0, i, j))] + extra_specs,
        out_specs=pl.BlockSpec((2, tm, tn), lambda j, i: (0, i, j)),
        out_shape=jax.ShapeDtypeStruct((2, s, f), BF16),
        compiler_params=_cparams("parallel", "parallel"),
    )(dxo, w_down, gu, *extra)


NORM_BWD_TM = 256


def _norm_bwd_after(terms, operands, specs, x, dres, gain, name):
    s, d = x.shape
    tm = NORM_BWD_TM
    n_op = len(operands)

    def body(*refs):
        x_ref, dres_ref, g_ref = refs[n_op:n_op + 3]
        dx_ref, dx16_ref, dgain_ref = refs[n_op + 3:]
        dn = None
        for a, b in terms(*refs[:n_op]):
            dn = _dot(a, b, "nt") if dn is None else dn + _dot(a, b, "nt")
        xb = x_ref[...]
        r = lax.rsqrt(jnp.mean(xb * xb, axis=-1, keepdims=True) + EPS)
        xh = xb * r
        dxh = dn * g_ref[...]
        dx = dres_ref[...] + r * (dxh - xh * jnp.mean(dxh * xh, axis=-1, keepdims=True))
        dx_ref[...] = dx
        dx16_ref[...] = dx.astype(BF16)

        @pl.when(pl.program_id(0) == 0)
        def _():
            dgain_ref[...] = jnp.zeros_like(dgain_ref)
        dgain_ref[...] += jnp.sum(dn * xh, axis=0, keepdims=True)

    rows = pl.BlockSpec((tm, d), lambda i: (i, 0))
    return pl.pallas_call(
        body, name=name, grid=(s // tm,),
        in_specs=list(specs) + [rows, rows, pl.BlockSpec((1, d), lambda i: (0, 0))],
        out_specs=[rows, rows, pl.BlockSpec((1, d), lambda i: (0, 0))],
        out_shape=[jax.ShapeDtypeStruct((s, d), F32), jax.ShapeDtypeStruct((s, d), BF16),
                   jax.ShapeDtypeStruct((1, d), F32)],
        compiler_params=_cparams("arbitrary"),
    )(*operands, x, dres, gain)


def _ffn_up_bwd(dgu, w_gu, layer, x, dres, gain, name):
    _, s, f = dgu.shape
    d = w_gu.shape[1]
    specs = [pl.BlockSpec((2, NORM_BWD_TM, f), lambda i: (0, i, 0)),
             pl.BlockSpec((None, d, f), lambda i: (layer, 0, 0)),
             pl.BlockSpec((None, d, f), lambda i: (layer, 0, 1))]
    terms = lambda dgu_ref, wg_ref, wu_ref: [(dgu_ref[0], wg_ref[...]), (dgu_ref[1], wu_ref[...])]
    return _norm_bwd_after(terms, [dgu, w_gu, w_gu], specs, x, dres, gain, name)


def _in_proj_bwd(dproj, w_in, j, x, dres, gain, name):
    k = dproj.shape[1]
    d = w_in.shape[1]
    specs = [pl.BlockSpec((NORM_BWD_TM, k), lambda i: (i, 0)), pl.BlockSpec((None, d, k), lambda i: (j, 0, 0))]
    terms = lambda a_ref, b_ref: [(a_ref[...], b_ref[...])]
    return _norm_bwd_after(terms, [dproj, w_in], specs, x, dres, gain, name)


def _ffn_fwd(x, gain, w_gu, w_down, layer, tag):
    n = _rms_fwd(x, gain, f"{tag}_norm")
    gu, a = _ffn_up(n, w_gu, layer, f"{tag}_up")
    x2 = _mm(a, w_down, "nn", tm=512, tn=x.shape[1], out_dtype=F32, name=f"{tag}_down", scale=0.5, residual=x,
             b_lead=(layer,))
    return x2, (x, n, gu, a)


def _ffn_bwd(dxo, dxo16, saved, gain, w_gu, w_down, layer, tag, g_gu, g_down, after=None):
    x, n, gu, a = saved
    s, f = a.shape
    dgu = _ffn_down_bwd(dxo16, w_down, gu, layer, f"{tag}_down_bwd", after)
    g_down = _mm(a, dxo16, "tn", tm=256, tn=dxo16.shape[1], out_dtype=F32, name=f"{tag}_down_dw", scale=0.5,
                 into=(g_down, 0))
    tn = f // 2
    nj = f // tn
    g_gu = _mm(n, dgu, "tn", tm=512, tn=tn, out_dtype=F32, name=f"{tag}_up_dw", into=(g_gu, 0), n=2 * f,
               b_spec=pl.BlockSpec((None, s, tn), lambda j, i: (j // nj, 0, j % nj)))
    dx, dx16, dgain = _ffn_up_bwd(dgu, w_gu, layer, x, dxo, gain, f"{tag}_up_bwd")
    return dx, dx16, dgain, g_gu, g_down


def _lane_col(blk, lane):
    li = lax.broadcasted_iota(jnp.int32, blk.shape, 1)
    return jnp.sum(jnp.where(li == lane, blk, 0.0), axis=1, keepdims=True)


def _split_dot(x, tri):
    hi = x.astype(BF16)
    lo = (x - hi.astype(F32)).astype(BF16)
    return (lax.dot_general(hi, tri, _DIMS["nn"], preferred_element_type=F32)
            + lax.dot_general(lo, tri, _DIMS["nn"], preferred_element_type=F32))


class _Each:
    def __init__(self, vals):
        self.vals = list(vals)

    def _with(self, other, op):
        others = other.vals if isinstance(other, _Each) else [other] * len(self.vals)
        return _Each(op(a, b) for a, b in zip(self.vals, others))

    def __add__(self, other):
        return self._with(other, lambda a, b: a + b)

    def __sub__(self, other):
        return self._with(other, lambda a, b: a - b)

    def __mul__(self, other):
        return self._with(other, lambda a, b: a * b)

    def __neg__(self):
        return _Each(-a for a in self.vals)


def _each(fn, *args):
    n = max(len(a.vals) for a in args if isinstance(a, _Each))
    res = [fn(*xs) for xs in zip(*[a.vals if isinstance(a, _Each) else [a] * n for a in args])]
    if isinstance(res[0], tuple):
        return tuple(_Each(r) for r in zip(*res))
    return _Each(res)


def _keep(cond, x):
    return _each(lambda v: jnp.where(cond, v, 0.0), x)


def _rowsum(x):
    return _each(lambda v: jnp.sum(v, axis=1, keepdims=True), x)


ATT_HEADS = 2
ATT_WIDTH = ATT_HEADS * HEAD_DIM
_HEAD_COLS = [slice(h * HEAD_DIM, (h + 1) * HEAD_DIM) for h in range(ATT_HEADS)]


def _att_specs(n_heads, s):
    groups = n_heads // ATT_HEADS
    q_spec = pl.BlockSpec((ATT_TQ, ATT_WIDTH), lambda g, i: (i, g))
    k_spec = pl.BlockSpec((s, ATT_WIDTH), lambda g, i: (0, groups + g))
    v_spec = pl.BlockSpec((s, ATT_WIDTH), lambda g, i: (0, 2 * groups + g))
    return q_spec, k_spec, v_spec


def _heads_of(ref, rows=None):
    return _Each(ref[:, cs] if rows is None else ref[rows, cs] for cs in _HEAD_COLS)


def _dot_each(a, b, kind):
    return _each(lambda x, y: _dot(x, y, kind), a, b)


def _att_iotas():
    row = lax.broadcasted_iota(jnp.int32, (ATT_TQ, ATT_TK), 0)
    col = lax.broadcasted_iota(jnp.int32, (ATT_TQ, ATT_TK), 1)
    jr = lax.broadcasted_iota(jnp.int32, (ATT_TK, ATT_TK), 0)
    jc = lax.broadcasted_iota(jnp.int32, (ATT_TK, ATT_TK), 1)
    return row, col, jr, jc


def _sb_fwd(qkv, n_heads, name):
    s = qkv.shape[0]

    def body(q_ref, k_ref, v_ref, o16_ref, o32_ref):
        i = pl.program_id(1)
        q = _heads_of(q_ref)
        row, col, jr, jc = _att_iotas()
        later = (jr > jc).astype(BF16)

        def step(jb, carry, diagonal):
            c_sp, acc = (_Each(part) for part in carry)
            work = []
            for sub in reversed(range(ATT_SUB)):
                keys = pl.ds(pl.multiple_of(jb * ATT_TQ + sub * ATT_TK, ATT_TK), ATT_TK)
                z = _dot_each(q, _heads_of(k_ref, keys), "nt") * ATT_SCALE
                sp = _each(_softplus, z)
                before = (col + sub * ATT_TK) < row if diagonal else None
                spm = _keep(before, sp) if diagonal else sp
                work.append((keys, z - sp, spm, _each(lambda x: _dot(x, later, "nn"), spm), before))
            for keys, logsig, spm, within, before in work:
                a = _each(jnp.exp, logsig - (c_sp + within))
                if diagonal:
                    a = _keep(before, a)
                acc = acc + _each(_split_dot, a, _heads_of(v_ref, keys))
                c_sp = c_sp + _rowsum(spm)
            return tuple(c_sp.vals), tuple(acc.vals)

        zeros = lambda width: tuple(jnp.zeros((ATT_TQ, width), F32) for _ in range(ATT_HEADS))
        carry = step(i, (zeros(1), zeros(HEAD_DIM)), True)
        _, acc = lax.fori_loop(0, i, lambda it, cr: step(i - 1 - it, cr, False), carry)
        for cs, acc_h in zip(_HEAD_COLS, acc):
            o16_ref[:, cs] = acc_h.astype(BF16)
            o32_ref[:, cs] = acc_h

    q_spec, k_spec, v_spec = _att_specs(n_heads, s)
    o_spec = pl.BlockSpec((ATT_TQ, ATT_WIDTH), lambda g, i: (i, g))
    return pl.pallas_call(
        body, name=name, grid=(n_heads // ATT_HEADS, s // ATT_TQ), in_specs=[q_spec, k_spec, v_spec],
        out_specs=[o_spec, o_spec],
        out_shape=[jax.ShapeDtypeStruct((s, n_heads * HEAD_DIM), BF16),
                   jax.ShapeDtypeStruct((s, n_heads * HEAD_DIM), F32)],
        compiler_params=_cparams("parallel", "arbitrary"),
    )(qkv, qkv, qkv)


def _sb_bwd(qkv, o32, do, n_heads, name):
    s = qkv.shape[0]

    def body(q_ref, k_ref, v_ref, o_ref, do_ref, dq_ref, dk_ref, dv_ref):
        i = pl.program_id(1)

        @pl.when(i == 0)
        def _():
            dk_ref[...] = jnp.zeros_like(dk_ref)
            dv_ref[...] = jnp.zeros_like(dv_ref)

        q, do = _heads_of(q_ref), _heads_of(do_ref)
        total = _rowsum(_each(lambda a, b: a.astype(F32) * b, do, _heads_of(o_ref)))
        row, col, jr, jc = _att_iotas()
        later = (jr > jc).astype(BF16)
        not_before = (jr >= jc).astype(BF16)

        def step(jb, carry, diagonal):
            c_sp, c_e, dq = (_Each(part) for part in carry)
            work = []
            for sub in reversed(range(ATT_SUB)):
                keys = pl.ds(pl.multiple_of(jb * ATT_TQ + sub * ATT_TK, ATT_TK), ATT_TK)
                k = _heads_of(k_ref, keys)
                z = _dot_each(q, k, "nt") * ATT_SCALE
                sp = _each(_softplus, z)
                before = (col + sub * ATT_TK) < row if diagonal else None
                spm = _keep(before, sp) if diagonal else sp
                work.append((keys, k, _each(jnp.exp, z - sp), spm, _each(lambda x: _dot(x, later, "nn"), spm),
                             _dot_each(do, _heads_of(v_ref, keys), "nt"), before))
            for keys, k, sig, spm, within, da, before in work:
                a = sig * _each(lambda x: jnp.exp(-x), c_sp + within)
                if diagonal:
                    a = _keep(before, a)
                e = a * da
                left = total - c_e - _each(lambda x: _split_dot(x, not_before), e)
                dz = (e - (e + left) * sig) * ATT_SCALE
                if diagonal:
                    dz = _keep(before, dz)
                dk, dv = _dot_each(dz, q, "tn"), _dot_each(a, do, "tn")
                for cs, dk_h, dv_h in zip(_HEAD_COLS, dk.vals, dv.vals):
                    dk_ref[keys, cs] += dk_h
                    dv_ref[keys, cs] += dv_h
                dq = dq + _dot_each(dz, k, "nn")
                c_sp = c_sp + _rowsum(spm)
                c_e = c_e + _rowsum(e)
            return tuple(c_sp.vals), tuple(c_e.vals), tuple(dq.vals)

        zeros = lambda width: tuple(jnp.zeros((ATT_TQ, width), F32) for _ in range(ATT_HEADS))
        carry = step(i, (zeros(1), zeros(1), zeros(HEAD_DIM)), True)
        _, _, dq = lax.fori_loop(0, i, lambda it, cr: step(i - 1 - it, cr, False), carry)
        for cs, dq_h in zip(_HEAD_COLS, dq):
            dq_ref[:, cs] = dq_h.astype(BF16)

    q_spec, k_spec, v_spec = _att_specs(n_heads, s)
    blk = pl.BlockSpec((ATT_TQ, ATT_WIDTH), lambda g, i: (i, g))
    full = pl.BlockSpec((s, ATT_WIDTH), lambda g, i: (0, g))
    wide = (s, n_heads * HEAD_DIM)
    return pl.pallas_call(
        body, name=name, grid=(n_heads // ATT_HEADS, s // ATT_TQ), in_specs=[q_spec, k_spec, v_spec, blk, blk],
        out_specs=[blk, full, full],
        out_shape=[jax.ShapeDtypeStruct(wide, BF16), jax.ShapeDtypeStruct(wide, F32), jax.ShapeDtypeStruct(wide, F32)],
        compiler_params=_cparams("parallel", "arbitrary"),
    )(qkv, qkv, qkv, o32, do)


def _fox_logits(q, k, cq, ct_ref, keys):
    ck = _Each(ct_ref[h, :, keys] for h in range(ATT_HEADS))
    return _dot_each(q, k, "nt") * ATT_SCALE + (cq - ck)


def _fox_cq(c_ref, group):
    c = c_ref[...]
    return _Each(_lane_col(c, LANE_FORGET + group * ATT_HEADS + h) for h in range(ATT_HEADS))


def _fox_fwd(qkv, c, ct, name):
    s = qkv.shape[0]
    n_heads = N_FOX_HEADS

    def body(q_ref, k_ref, v_ref, c_ref, ct_ref, o_ref, lse_ref):
        g, i = pl.program_id(0), pl.program_id(1)
        q = _heads_of(q_ref)
        cq = _fox_cq(c_ref, g)
        row, col, _, _ = _att_iotas()

        def step(jb, carry, diagonal):
            m, l, acc = (_Each(part) for part in carry)
            work = []
            m_new = m
            for sub in range(ATT_SUB):
                keys = pl.ds(pl.multiple_of(jb * ATT_TQ + sub * ATT_TK, ATT_TK), ATT_TK)
                sc = _fox_logits(q, _heads_of(k_ref, keys), cq, ct_ref, keys)
                valid = (col + sub * ATT_TK) <= row if diagonal else None
                if diagonal:
                    sc = _each(lambda x: jnp.where(valid, x, -1e30), sc)
                m_new = _each(lambda a, x: jnp.maximum(a, jnp.max(x, axis=1, keepdims=True)), m_new, sc)
                work.append((keys, sc, valid))
            w = _each(jnp.exp, m - m_new)
            l, acc = l * w, acc * w
            for keys, sc, valid in work:
                p = _each(jnp.exp, sc - m_new)
                if diagonal:
                    p = _keep(valid, p)
                l = l + _rowsum(p)
                acc = acc + _each(_split_dot, p, _heads_of(v_ref, keys))
            return tuple(m_new.vals), tuple(l.vals), tuple(acc.vals)

        per_head = lambda width, value: tuple(jnp.full((ATT_TQ, width), value, F32) for _ in range(ATT_HEADS))
        init = (per_head(1, -1e30), per_head(1, 0.0), per_head(HEAD_DIM, 0.0))
        m, l, acc = lax.fori_loop(0, i, lambda jb, cr: step(jb, cr, False), step(i, init, True))
        for h, cs in enumerate(_HEAD_COLS):
            o_ref[:, cs] = acc[h] / l[h]
            lse_ref[h] = jnp.broadcast_to(m[h] + jnp.log(l[h]), (ATT_TQ, LANES))

    q_spec, k_spec, v_spec = _att_specs(n_heads, s)
    return pl.pallas_call(
        body, name=name, grid=(n_heads // ATT_HEADS, s // ATT_TQ),
        in_specs=[q_spec, k_spec, v_spec, pl.BlockSpec((ATT_TQ, LANES), lambda g, i: (i, 0)),
                  pl.BlockSpec((ATT_HEADS, 1, s), lambda g, i: (g, 0, 0))],
        out_specs=[pl.BlockSpec((ATT_TQ, ATT_WIDTH), lambda g, i: (i, g)),
                   pl.BlockSpec((ATT_HEADS, ATT_TQ, LANES), lambda g, i: (g, i, 0))],
        out_shape=[jax.ShapeDtypeStruct((s, n_heads * HEAD_DIM), F32),
                   jax.ShapeDtypeStruct((n_heads, s, LANES), F32)],
        compiler_params=_cparams("parallel", "arbitrary"),
    )(qkv, qkv, qkv, c, ct)


def _fox_bwd(qkv, c, ct, o, lse, do, name):
    s = qkv.shape[0]
    n_heads = N_FOX_HEADS

    def body(q_ref, k_ref, v_ref, c_ref, ct_ref, o_ref, lse_ref, do_ref, dq_ref, dk_ref, dv_ref, dct_ref):
        g, i = pl.program_id(0), pl.program_id(1)

        @pl.when(i == 0)
        def _():
            dk_ref[...] = jnp.zeros_like(dk_ref)
            dv_ref[...] = jnp.zeros_like(dv_ref)
            dct_ref[...] = jnp.zeros_like(dct_ref)

        q = _heads_of(q_ref)
        do16 = _each(lambda x: x.astype(BF16), _heads_of(do_ref))
        delta = _rowsum(_each(lambda a, b: a.astype(F32) * b, do16, _heads_of(o_ref)))
        lse_col = _Each(lse_ref[h, :, 0:1] for h in range(ATT_HEADS))
        cq = _fox_cq(c_ref, g)
        row, col, _, _ = _att_iotas()

        def step(jb, dq, diagonal):
            dq = _Each(dq)
            for sub in range(ATT_SUB):
                keys = pl.ds(pl.multiple_of(jb * ATT_TQ + sub * ATT_TK, ATT_TK), ATT_TK)
                k = _heads_of(k_ref, keys)
                sc = _fox_logits(q, k, cq, ct_ref, keys)
                if diagonal:
                    valid = (col + sub * ATT_TK) <= row
                    p = _keep(valid, _each(jnp.exp, _keep(valid, sc) - lse_col))
                else:
                    p = _each(jnp.exp, sc - lse_col)
                ds = p * (_dot_each(do16, _heads_of(v_ref, keys), "nt") - delta)
                dss = ds * ATT_SCALE
                dk, dv = _dot_each(dss, q, "tn"), _dot_each(p, do16, "tn")
                for h, cs in enumerate(_HEAD_COLS):
                    dct_ref[h, :, keys] -= jnp.sum(ds.vals[h], axis=0, keepdims=True)
                    dk_ref[keys, cs] += dk.vals[h]
                    dv_ref[keys, cs] += dv.vals[h]
                dq = dq + _dot_each(dss, k, "nn")
            return tuple(dq.vals)

        dq0 = step(i, tuple(jnp.zeros((ATT_TQ, HEAD_DIM), F32) for _ in range(ATT_HEADS)), True)
        dq = lax.fori_loop(0, i, lambda jb, dq: step(jb, dq, False), dq0)
        for cs, dq_h in zip(_HEAD_COLS, dq):
            dq_ref[:, cs] = dq_h

    q_spec, k_spec, v_spec = _att_specs(n_heads, s)
    blk = pl.BlockSpec((ATT_TQ, ATT_WIDTH), lambda g, i: (i, g))
    full = pl.BlockSpec((s, ATT_WIDTH), lambda g, i: (0, g))
    wide = jax.ShapeDtypeStruct((s, n_heads * HEAD_DIM), F32)
    return pl.pallas_call(
        body, name=name, grid=(n_heads // ATT_HEADS, s // ATT_TQ),
        in_specs=[q_spec, k_spec, v_spec, pl.BlockSpec((ATT_TQ, LANES), lambda g, i: (i, 0)),
                  pl.BlockSpec((ATT_HEADS, 1, s), lambda g, i: (g, 0, 0)), blk,
                  pl.BlockSpec((ATT_HEADS, ATT_TQ, LANES), lambda g, i: (g, i, 0)), blk],
        out_specs=[blk, full, full, pl.BlockSpec((ATT_HEADS, 1, s), lambda g, i: (g, 0, 0))],
        out_shape=[wide, wide, wide, jax.ShapeDtypeStruct((n_heads, 1, s), F32)],
        compiler_params=_cparams("parallel", "arbitrary"),
    )(qkv, qkv, qkv, c, ct, o, lse, do)


def _cumsum_rows(x, reverse, name):
    s = x.shape[0]
    nb = s // LANES

    def body(x_ref, o_ref):
        r = lax.broadcasted_iota(jnp.int32, (LANES, LANES), 0)
        c = lax.broadcasted_iota(jnp.int32, (LANES, LANES), 1)
        tri = ((r <= c) if reverse else (r >= c)).astype(F32)

        def step(it, carry):
            b = (nb - 1 - it) if reverse else it
            off = pl.multiple_of(b * LANES, LANES)
            blk = x_ref[pl.ds(off, LANES), :]
            o_ref[pl.ds(off, LANES), :] = _dot32(tri, blk) + carry
            return carry + jnp.sum(blk, axis=0, keepdims=True)

        lax.fori_loop(0, nb, step, jnp.zeros((1, LANES), F32))

    return pl.pallas_call(body, name=name, out_shape=jax.ShapeDtypeStruct(x.shape, F32),
                          compiler_params=pltpu.CompilerParams(vmem_limit_bytes=V7X_VMEM_LIMIT))(x)


def _dot32_each(a, b, kind="nn"):
    return _each(lambda x, y: _dot32(x, y, kind), a, b)


def _unit_lower_inverse(m, ri, ci):
    c = ri.shape[0]
    t = -_keep(ri // 2 == ci // 2, m) + jnp.where(ri == ci, 1.0, 0.0)
    b = 4
    while b <= c:
        off_diag = (ri // b == ci // b) & (ri % b >= b // 2) & (ci % b < b // 2)
        t = t - _dot32_each(_dot32_each(t, _keep(off_diag, m)), t)
        b *= 2
    return t


def _dn_gates(g, ri, ci):
    eye = ri == ci
    incl = ri >= ci
    g_row = jnp.sum(jnp.where(eye, g, 0.0), axis=0, keepdims=True)
    gc = jnp.sum(jnp.where(incl, g_row, 0.0), axis=1, keepdims=True)
    gc_row = jnp.sum(jnp.where(eye, gc, 0.0), axis=0, keepdims=True)
    dmat = jnp.where(incl, jnp.exp(jnp.where(incl, gc - gc_row, 0.0)), 0.0)
    gc_last = jnp.sum(g, axis=0, keepdims=True)
    return gc, dmat, jnp.exp(gc), jnp.exp(gc_last - gc), jnp.exp(gc_last)


def _dn_fwd(qkv, act, name):
    s = qkv.shape[0]
    c, d, nh = DN_CHUNK, HEAD_DIM, N_DN_HEADS
    nc = s // c

    def body(q_ref, k_ref, v_ref, act_ref, o_ref, s_ref, t_ref, state):
        @pl.when(pl.program_id(0) == 0)
        def _():
            state[...] = jnp.zeros_like(state)

        ri = lax.broadcasted_iota(jnp.int32, (c, c), 0)
        ci = lax.broadcasted_iota(jnp.int32, (c, c), 1)
        act = act_ref[...]
        heads = range(nh)
        cols = [slice(h * d, (h + 1) * d) for h in heads]
        q, k, v = (_Each(ref[:, cs] for cs in cols) for ref in (q_ref, k_ref, v_ref))
        beta = _Each(_lane_col(act, LANE_BETA + h) for h in heads)
        g = _Each(_lane_col(act, LANE_DECAY + h) for h in heads)
        _, dmat, e, r, gl = _each(lambda gh: _dn_gates(gh, ri, ci), g)
        s0 = _Each(state[h] for h in heads)
        kb = beta * k
        t = _unit_lower_inverse(_keep(ri > ci, _dot32_each(kb, k, "nt") * dmat), ri, ci)
        vn = _dot32_each(t, beta * v) - _dot32_each(_dot32_each(t, kb * e), s0)
        o = _dot32_each(q * e, s0) + _dot32_each(_dot32_each(q, k, "nt") * dmat, vn)
        s1 = s0 * gl + _dot32_each(k * r, vn, "tn")
        for h in heads:
            o_ref[:, cols[h]] = o.vals[h]
            state[h] = s1.vals[h]
            s_ref[h] = s0.vals[h]
            t_ref[h] = t.vals[h]

    wide = lambda part: pl.BlockSpec((c, nh * d), lambda n: (n, part))
    return pl.pallas_call(
        body, name=name, grid=(nc,),
        in_specs=[wide(0), wide(1), wide(2), pl.BlockSpec((c, LANES), lambda n: (n, 0))],
        out_specs=[wide(0), pl.BlockSpec((nh, None, d, d), lambda n: (0, n, 0, 0)),
                   pl.BlockSpec((nh, None, c, c), lambda n: (0, n, 0, 0))],
        out_shape=[jax.ShapeDtypeStruct((s, nh * d), F32), jax.ShapeDtypeStruct((nh, nc, d, d), F32),
                   jax.ShapeDtypeStruct((nh, nc, c, c), F32)],
        scratch_shapes=[pltpu.VMEM((nh, d, d), F32)],
        compiler_params=_cparams("arbitrary"),
    )(qkv, qkv, qkv, act)


def _dn_bwd(qkv, act, states, tinv, do, name):
    s = qkv.shape[0]
    c, d, nh = DN_CHUNK, HEAD_DIM, N_DN_HEADS
    nc = s // c

    def chunk_bwd(q, k, v, do, beta, g, s0, t, ds_out):
        ri = lax.broadcasted_iota(jnp.int32, (c, c), 0)
        ci = lax.broadcasted_iota(jnp.int32, (c, c), 1)
        eye, incl, strict = ri == ci, ri >= ci, ri > ci
        gc, dmat, e, r, gl = _each(lambda gh: _dn_gates(gh, ri, ci), g)
        dot = _dot32_each
        rowsum = lambda x: _each(lambda a: jnp.sum(a, axis=1, keepdims=True), x)
        colsum = lambda x: _each(lambda a: jnp.sum(a, axis=0, keepdims=True), x)
        total = lambda x: colsum(rowsum(x))
        to_col = lambda row: rowsum(_keep(eye, row))
        to_row = lambda colv: colsum(_keep(eye, colv))

        kb, vb = beta * k, beta * v
        kbe = kb * e
        u, w = dot(t, vb), dot(t, kbe)
        vn = u - dot(w, s0)
        qk = dot(q, k, "nt")
        p = qk * dmat
        gram = dot(k, k, "nt")
        kr, qe = k * r, q * e

        d_kr = dot(vn, ds_out, "nt")
        dvn = dot(kr, ds_out)
        dgl = total(s0 * ds_out)
        ds_in = ds_out * gl
        dk = d_kr * r
        dr = rowsum(d_kr * k)
        d_qe = dot(do, s0, "nt")
        ds_in = ds_in + dot(qe, do, "tn")
        dp = _keep(incl, dot(do, vn, "nt"))
        dvn = dvn + dot(p, do, "tn")
        dq = d_qe * e
        de = rowsum(d_qe * q)
        dqk = dp * dmat
        dq = dq + dot(dqk, k)
        dk = dk + dot(dqk, q, "tn")
        dd = dp * qk
        dw = -dot(dvn, s0, "nt")
        ds_in = ds_in - dot(w, dvn, "tn")
        dvb = dot(t, dvn, "tn")
        dkbe = dot(t, dw, "tn")
        dm = -_keep(strict, dot(dvb, u, "nt") + dot(dkbe, w, "nt"))
        dbeta = rowsum(dm * gram * dmat)
        dgram = dm * beta * dmat
        dd = dd + dm * beta * gram
        dk = dk + dot(dgram, k) + dot(dgram, k, "tn")
        dkb = dkbe * e
        de = de + rowsum(dkbe * kb)
        dk = dk + beta * dkb
        dbeta = dbeta + rowsum(dkb * k) + rowsum(dvb * v)
        dv = beta * dvb
        wd = dd * dmat
        dgc = rowsum(wd) - to_col(colsum(wd)) + de * e - dr * r
        dgc_last = total(dr * r) + dgl * gl
        dgc = dgc + _keep(ri[:, 0:1] == c - 1, dgc_last)
        dg = rowsum(_keep(ri <= ci, to_row(dgc)))
        return dq, dk, dv, dbeta, dg, ds_in

    def body(q_ref, k_ref, v_ref, act_ref, s_ref, t_ref, do_ref, dq_ref, dk_ref, dv_ref, dact_ref, dstate):
        @pl.when(pl.program_id(0) == 0)
        def _():
            dstate[...] = jnp.zeros_like(dstate)

        act = act_ref[...]
        heads = range(nh)
        cols = [slice(h * d, (h + 1) * d) for h in heads]
        q, k, v, do = (_Each(ref[:, cs] for cs in cols) for ref in (q_ref, k_ref, v_ref, do_ref))
        dq, dk, dv, dbeta, dg, ds_in = chunk_bwd(
            q, k, v, do, _Each(_lane_col(act, LANE_BETA + h) for h in heads),
            _Each(_lane_col(act, LANE_DECAY + h) for h in heads), _Each(s_ref[h] for h in heads),
            _Each(t_ref[h] for h in heads), _Each(dstate[h] for h in heads))
        lane = lax.broadcasted_iota(jnp.int32, (c, LANES), 1)
        dact = jnp.zeros((c, LANES), F32)
        for h in heads:
            dstate[h] = ds_in.vals[h]
            dq_ref[:, cols[h]], dk_ref[:, cols[h]], dv_ref[:, cols[h]] = dq.vals[h], dk.vals[h], dv.vals[h]
            dact = (dact + jnp.where(lane == LANE_BETA + h, dbeta.vals[h], 0.0)
                    + jnp.where(lane == LANE_DECAY + h, dg.vals[h], 0.0))
        dact_ref[...] = dact

    part = lambda p: pl.BlockSpec((c, nh * d), lambda n: (nc - 1 - n, p))
    per = lambda a, b: pl.BlockSpec((nh, None, a, b), lambda n: (0, nc - 1 - n, 0, 0))
    wide = jax.ShapeDtypeStruct((s, nh * d), F32)
    act_spec = pl.BlockSpec((c, LANES), lambda n: (nc - 1 - n, 0))
    return pl.pallas_call(
        body, name=name, grid=(nc,),
        in_specs=[part(0), part(1), part(2), act_spec, per(d, d), per(c, c), part(0)],
        out_specs=[part(0), part(0), part(0), act_spec],
        out_shape=[wide, wide, wide, jax.ShapeDtypeStruct((s, LANES), F32)],
        scratch_shapes=[pltpu.VMEM((nh, d, d), F32)],
        compiler_params=_cparams("arbitrary"),
    )(qkv, qkv, qkv, act, states, tinv, do)


EVEN_DN_QKV, EVEN_FOX_QKV, EVEN_DN_GATE, EVEN_FOX_GATE, EVEN_NARROW = 0, 1536, 3072, 3584, 4096
EVEN_WIDTH = 4224
CONV_TILE = 256
CONV_HALO = 8


def _conv_fwd(proj, w, name):
    s = proj.shape[0]
    t, cw = CONV_TILE, 3 * D_DN

    def body(cur_ref, prev_ref, w_ref, y_ref, xs):
        i = pl.program_id(0)
        xs[0:CONV_HALO, :] = jnp.where(i > 0, prev_ref[...], 0.0)
        xs[CONV_HALO:, :] = cur_ref[...]
        y = jnp.zeros((t, cw), F32)
        for tap in range(CONV_WIDTH):
            y = y + w_ref[tap:tap + 1, :] * xs[pl.ds(CONV_HALO - CONV_WIDTH + 1 + tap, t), :]
        y_ref[...] = y

    per = t // CONV_HALO
    return pl.pallas_call(
        body, name=name, grid=(s // t,),
        in_specs=[pl.BlockSpec((t, cw), lambda i: (i, 0)),
                  pl.BlockSpec((CONV_HALO, cw), lambda i: (jnp.maximum(i * per - 1, 0), 0)),
                  pl.BlockSpec((CONV_WIDTH, cw), lambda i: (0, 0))],
        out_specs=pl.BlockSpec((t, cw), lambda i: (i, 0)),
        out_shape=jax.ShapeDtypeStruct((s, cw), F32),
        scratch_shapes=[pltpu.VMEM((t + CONV_HALO, cw), F32)],
        compiler_params=_cparams("parallel"),
    )(proj, proj, w)


def _conv_bwd(proj, w, dy, name):
    s = proj.shape[0]
    t, cw = CONV_TILE, 3 * D_DN
    nt = s // t

    def body(cur_ref, prev_ref, w_ref, dy_ref, nxt_ref, dx_ref, dw_ref, xs, dys):
        i = pl.program_id(0)

        @pl.when(i == 0)
        def _():
            dw_ref[...] = jnp.zeros_like(dw_ref)

        xs[0:CONV_HALO, :] = jnp.where(i > 0, prev_ref[...], 0.0)
        xs[CONV_HALO:, :] = cur_ref[...]
        dys[0:t, :] = dy_ref[...]
        dys[t:, :] = jnp.where(i < nt - 1, nxt_ref[...], 0.0)
        dy = dy_ref[...]
        dx = jnp.zeros((t, cw), F32)
        for tap in range(CONV_WIDTH):
            dx = dx + w_ref[tap:tap + 1, :] * dys[pl.ds(CONV_WIDTH - 1 - tap, t), :]
            dw_ref[tap:tap + 1, :] += jnp.sum(dy * xs[pl.ds(CONV_HALO - CONV_WIDTH + 1 + tap, t), :], axis=0,
                                              keepdims=True)
        dx_ref[...] = dx.astype(BF16)

    per = t // CONV_HALO
    last = s // CONV_HALO - 1
    return pl.pallas_call(
        body, name=name, grid=(nt,),
        in_specs=[pl.BlockSpec((t, cw), lambda i: (i, 0)),
                  pl.BlockSpec((CONV_HALO, cw), lambda i: (jnp.maximum(i * per - 1, 0), 0)),
                  pl.BlockSpec((CONV_WIDTH, cw), lambda i: (0, 0)),
                  pl.BlockSpec((t, cw), lambda i: (i, 0)),
                  pl.BlockSpec((CONV_HALO, cw), lambda i: (jnp.minimum((i + 1) * per, last), 0))],
        out_specs=[pl.BlockSpec((t, cw), lambda i: (i, 0)), pl.BlockSpec((CONV_WIDTH, cw), lambda i: (0, 0))],
        out_shape=[jax.ShapeDtypeStruct((s, cw), BF16), jax.ShapeDtypeStruct((CONV_WIDTH, cw), F32)],
        scratch_shapes=[pltpu.VMEM((t + CONV_HALO, cw), F32), pltpu.VMEM((t + CONV_HALO, cw), F32)],
        compiler_params=_cparams("arbitrary"),
    )(proj, proj, w, dy, dy)


def _heads(x, n):
    return [x[:, HEAD_DIM * h:HEAD_DIM * (h + 1)] for h in range(n)]


def _dn_pre_fwd(y, name):
    def fn(yb):
        cs = yb * _sigmoid(yb)
        out = []
        for idx, xh in enumerate(_heads(cs, 3 * N_DN_HEADS)):
            if idx < 2 * N_DN_HEADS:
                xh = xh * lax.rsqrt(jnp.sum(xh * xh, axis=-1, keepdims=True) + EPS)
                if idx < N_DN_HEADS:
                    xh = xh * ATT_SCALE
            out.append(xh)
        return (jnp.concatenate(out, axis=1),)
    return _rowwise(fn, [y], [], [(y.shape[1], F32)], [], tile=256, name=name)[0]


def _dn_pre_bwd(y, dq, dk, dv, name):
    def fn(yb, dqb, dkb, dvb):
        sg = _sigmoid(yb)
        cs = yb * sg
        dout = _heads(dqb, N_DN_HEADS) + _heads(dkb, N_DN_HEADS) + _heads(dvb, N_DN_HEADS)
        dcs = []
        for idx, (xh, dh) in enumerate(zip(_heads(cs, 3 * N_DN_HEADS), dout)):
            if idx < 2 * N_DN_HEADS:
                if idx < N_DN_HEADS:
                    dh = dh * ATT_SCALE
                r = lax.rsqrt(jnp.sum(xh * xh, axis=-1, keepdims=True) + EPS)
                xhat = xh * r
                dh = r * (dh - xhat * jnp.sum(xhat * dh, axis=-1, keepdims=True))
            dcs.append(dh)
        return (jnp.concatenate(dcs, axis=1) * _silu_grad(yb, sg),)
    return _rowwise(fn, [y, dq, dk, dv], [], [(y.shape[1], F32)], [], tile=256, name=name)[0]


def _narrow_params(a_log, dt_bias, f_bias):
    lanes = lambda a, first: jnp.pad(a.reshape(1, -1), ((0, 0), (first, LANES - first - a.shape[0])))
    return jnp.concatenate([lanes(a_log, LANE_DECAY), lanes(dt_bias, LANE_DECAY), lanes(f_bias, LANE_FORGET),
                            jnp.zeros((5, LANES), F32)], axis=0)


def _narrow_masks(shape):
    lane = lax.broadcasted_iota(jnp.int32, shape, 1)
    is_beta = lane < LANE_DECAY
    is_decay = (lane >= LANE_DECAY) & (lane < LANE_FORGET)
    is_forget = (lane >= LANE_FORGET) & (lane < LANE_FORGET + N_FOX_HEADS)
    return is_beta, is_decay, is_forget


def _narrow_fwd(proj, params, name):
    def fn(sm, pk):
        is_beta, is_decay, is_forget = _narrow_masks(sm.shape)
        g = -jnp.exp(pk[0:1, :]) * _softplus(sm + pk[1:2, :])
        logf = -_softplus(-(sm + pk[2:3, :]))
        return (jnp.where(is_beta, _sigmoid(sm), jnp.where(is_decay, g, jnp.where(is_forget, logf, 0.0))),)
    return _rowwise(fn, [(proj, LANES, EVEN_NARROW // LANES)], [params], [(LANES, F32)], [], tile=512, name=name)[0]


def _narrow_bwd(proj, params, act, dact, dlogf, name):
    def fn(sm, ab, da, dl, pk):
        is_beta, is_decay, is_forget = _narrow_masks(sm.shape)
        db = jnp.where(is_forget, dl, da)
        d_beta = db * ab * (1.0 - ab)
        d_decay = db * (-jnp.exp(pk[0:1, :])) * _sigmoid(sm + pk[1:2, :])
        d_forget = db * _sigmoid(-(sm + pk[2:3, :]))
        dsm = jnp.where(is_beta, d_beta, jnp.where(is_decay, d_decay, jnp.where(is_forget, d_forget, 0.0)))
        col = lambda x: jnp.sum(x, axis=0, keepdims=True)
        return (dsm, col(jnp.where(is_decay, db * ab, 0.0)), col(jnp.where(is_decay, dsm, 0.0)),
                col(jnp.where(is_forget, dsm, 0.0)))
    return _rowwise(fn, [(proj, LANES, EVEN_NARROW // LANES), act, dact, dlogf], [params], [(LANES, BF16)],
                    [(1, LANES)] * 3, tile=512, name=name)


def _head_rms(xh):
    r = lax.rsqrt(jnp.mean(xh * xh, axis=-1, keepdims=True) + EPS)
    return xh * r, r


def _fox_pre_fwd(proj, qg, kg, name):
    def fn(pf, qgb, kgb):
        out = []
        for idx, xh in enumerate(_heads(pf, 3 * N_FOX_HEADS)):
            if idx < 2 * N_FOX_HEADS:
                xh = _head_rms(xh)[0] * (qgb if idx < N_FOX_HEADS else kgb)
            out.append(xh)
        return (jnp.concatenate(out, axis=1),)
    return _rowwise(fn, [(proj, 3 * D_FOX, EVEN_FOX_QKV // (3 * D_FOX))], [qg, kg], [(3 * D_FOX, BF16)], [],
                    tile=256, name=name)[0]


def _fox_pre_bwd(proj, qg, kg, dq, dk, dv, name):
    def fn(pf, dqb, dkb, dvb, qgb, kgb):
        dout = _heads(dqb, N_FOX_HEADS) + _heads(dkb, N_FOX_HEADS) + _heads(dvb, N_FOX_HEADS)
        dg = [jnp.zeros((1, HEAD_DIM), F32), jnp.zeros((1, HEAD_DIM), F32)]
        dx = []
        for idx, (xh, dh) in enumerate(zip(_heads(pf, 3 * N_FOX_HEADS), dout)):
            if idx < 2 * N_FOX_HEADS:
                which = 0 if idx < N_FOX_HEADS else 1
                xhat, r = _head_rms(xh)
                dg[which] = dg[which] + jnp.sum(dh * xhat, axis=0, keepdims=True)
                dxh = dh * (qgb if which == 0 else kgb)
                dh = r * (dxh - xhat * jnp.mean(dxh * xhat, axis=-1, keepdims=True))
            dx.append(dh)
        return jnp.concatenate(dx, axis=1), dg[0], dg[1]
    return _rowwise(fn, [(proj, 3 * D_FOX, EVEN_FOX_QKV // (3 * D_FOX)), dq, dk, dv], [qg, kg],
                    [(3 * D_FOX, BF16)], [(1, HEAD_DIM)] * 2, tile=256, name=name)


def _mix_gate_fwd(proj, o_dn, o_fox, ng, name):
    def fn(gd, gf, od, of, ngb):
        dn = [_head_rms(xh)[0] * ngb for xh in _heads(od, N_DN_HEADS)]
        return (jnp.concatenate([jnp.concatenate(dn, axis=1) * gd * _sigmoid(gd), of * _sigmoid(gf)], axis=1),)
    return _rowwise(fn, [(proj, D_DN, EVEN_DN_GATE // D_DN), (proj, D_FOX, EVEN_FOX_GATE // D_FOX), o_dn, o_fox],
                    [ng], [(D_DN + D_FOX, BF16)], [], tile=256, name=name)[0]


def _mix_gate_bwd(proj, o_dn, o_fox, ng, dom, name):
    def fn(gd, gf, od, of, dm, ngb):
        d_dn, d_fox = dm[:, :D_DN], dm[:, D_DN:]
        sgd, sgf = _sigmoid(gd), _sigmoid(gf)
        don = d_dn * gd * sgd
        dng = jnp.zeros((1, HEAD_DIM), F32)
        dod, normed = [], []
        for xh, dh in zip(_heads(od, N_DN_HEADS), _heads(don, N_DN_HEADS)):
            xhat, r = _head_rms(xh)
            dng = dng + jnp.sum(dh * xhat, axis=0, keepdims=True)
            dxh = dh * ngb
            dod.append(r * (dxh - xhat * jnp.mean(dxh * xhat, axis=-1, keepdims=True)))
            normed.append(xhat * ngb)
        d_gd = d_dn * jnp.concatenate(normed, axis=1) * _silu_grad(gd, sgd)
        d_gf = d_fox * of * sgf * (1.0 - sgf)
        return jnp.concatenate(dod, axis=1), d_fox * sgf, d_gd, d_gf, dng
    return _rowwise(fn, [(proj, D_DN, EVEN_DN_GATE // D_DN), (proj, D_FOX, EVEN_FOX_GATE // D_FOX), o_dn, o_fox, dom],
                    [ng], [(D_DN, F32), (D_FOX, F32), (D_DN, BF16), (D_FOX, BF16)], [(1, HEAD_DIM)], tile=256,
                    name=name)


def _loss_grad(y, target, name):
    d = y.shape[1]

    def fn(yb, tb):
        diff = yb - tb
        part = jnp.sum(jnp.sum(diff * diff, axis=1, keepdims=True), axis=0, keepdims=True) * (0.5 / d)
        g = diff * (1.0 / d)
        return g, g, part
    return _rowwise(fn, [y, target], [], [(d, F32), (d, BF16)], [(1, 1)], tile=512, name=name)


_REF_EVEN = {"dn_qkv": (0, 1536), "dn_gate": (1536, 2048), "dn_ba": (2048, 2056), "fox_qkv": (2056, 3592),
             "fox_gate": (3592, 4104), "f_pre": (4104, 4108)}
D_IN_EVEN = 4108


def _even_to_kernel_layout(w):
    cut = lambda name: w[..., _REF_EVEN[name][0]:_REF_EVEN[name][1]]
    pad = jnp.zeros(w.shape[:-1] + (EVEN_WIDTH - EVEN_NARROW - 12,), w.dtype)
    return jnp.concatenate([cut("dn_qkv"), cut("fox_qkv"), cut("dn_gate"), cut("fox_gate"), cut("dn_ba"),
                            cut("f_pre"), pad], axis=-1)


def _even_from_kernel_layout(g):
    return jnp.concatenate([g[..., EVEN_DN_QKV:EVEN_FOX_QKV], g[..., EVEN_DN_GATE:EVEN_FOX_GATE],
                            g[..., EVEN_NARROW:EVEN_NARROW + 8], g[..., EVEN_FOX_QKV:EVEN_DN_GATE],
                            g[..., EVEN_FOX_GATE:EVEN_NARROW], g[..., EVEN_NARROW + 8:EVEN_NARROW + 12]], axis=-1)


EVEN_QUARTER = 1027
EVEN_QUARTER_PAD = 1152


def _even_grad_quarters(g):
    g = _even_from_kernel_layout(g)
    pad = [(0, 0)] * (g.ndim - 1) + [(0, EVEN_QUARTER_PAD - EVEN_QUARTER)]
    return jnp.concatenate([jnp.pad(g[..., q * EVEN_QUARTER:(q + 1) * EVEN_QUARTER], pad) for q in range(4)], axis=-1)


def _forget_rows(c):
    return c[:, LANE_FORGET:LANE_FORGET + N_FOX_HEADS].T.reshape(N_FOX_HEADS, 1, c.shape[0])


def _forget_lanes(rows):
    s = rows.shape[2]
    return jnp.pad(rows.reshape(-1, s).T, ((0, 0), (LANE_FORGET, LANES - LANE_FORGET - N_FOX_HEADS)))


def _even_fwd(x, gain, w_in, w_out, j, p, tag):
    h = _rms_fwd(x, gain, f"{tag}_norm")
    proj = _mm(h, w_in, "nn", tm=512, tn=EVEN_WIDTH // 3, out_dtype=F32, name=f"{tag}_in", b_lead=(j,))
    y = _conv_fwd(proj, p["conv_w"], f"{tag}_conv")
    dn_qkv = _dn_pre_fwd(y, f"{tag}_dn_pre")
    act = _narrow_fwd(proj, p["narrow"], f"{tag}_narrow")
    o_dn, states, tinv = _dn_fwd(dn_qkv, act, f"{tag}_delta")
    fox_qkv = _fox_pre_fwd(proj, p["q_g"], p["k_g"], f"{tag}_fox_pre")
    c = _cumsum_rows(act, False, f"{tag}_cumsum")
    ct = _forget_rows(c)
    o_fox, lse = _fox_fwd(fox_qkv, c, ct, f"{tag}_fox")
    om = _mix_gate_fwd(proj, o_dn, o_fox, p["dn_norm_g"], f"{tag}_gate")
    x2 = _mm(om, w_out, "nn", tm=512, tn=x.shape[1], out_dtype=F32, name=f"{tag}_out", residual=x, b_lead=(j,))
    return x2, (x, h, proj, y, dn_qkv, act, states, tinv, o_dn, fox_qkv, c, ct, o_fox, lse, om)


def _even_bwd(dxo, dxo16, saved, gain, w_in, w_out, j, p, tag, g_in, g_out):
    x, h, proj, y, dn_qkv, act, states, tinv, o_dn, fox_qkv, c, ct, o_fox, lse, om = saved
    d = x.shape[1]
    dom = _mm(dxo16, w_out, "nt", tm=512, tn=d, out_dtype=F32, name=f"{tag}_out_bwd", b_lead=(j,))
    g_out = _mm(om, dxo16, "tn", tm=512, tn=d, out_dtype=F32, name=f"{tag}_out_dw", into=(g_out, 0))
    d_odn, d_ofox, d_gd, d_gf, d_ng = _mix_gate_bwd(proj, o_dn, o_fox, p["dn_norm_g"], dom, f"{tag}_gate_bwd")
    dq, dk, dv, dct = _fox_bwd(fox_qkv, c, ct, o_fox, lse, d_ofox, f"{tag}_fox_bwd")
    d_fox_qkv, d_qg, d_kg = _fox_pre_bwd(proj, p["q_g"], p["k_g"], dq, dk, dv, f"{tag}_fox_pre_bwd")
    dlogf = _cumsum_rows(_forget_lanes(dct), True, f"{tag}_cumsum_bwd")
    dq, dk, dv, dact = _dn_bwd(dn_qkv, act, states, tinv, d_odn, f"{tag}_delta_bwd")
    dy = _dn_pre_bwd(y, dq, dk, dv, f"{tag}_dn_pre_bwd")
    d_dn_qkv, d_conv = _conv_bwd(proj, p["conv_w"], dy, f"{tag}_conv_bwd")
    d_narrow, s_alog, s_dt, s_fb = _narrow_bwd(proj, p["narrow"], act, dact, dlogf, f"{tag}_narrow_bwd")
    dproj = jnp.concatenate([d_dn_qkv, d_fox_qkv, d_gd, d_gf, d_narrow], axis=1)
    g_in = _mm(h, dproj, "tn", tm=512, tn=EVEN_WIDTH // 3, out_dtype=F32, name=f"{tag}_in_dw", into=(g_in, 0))
    dx, dx16, d_gain = _in_proj_bwd(dproj, w_in, j, x, dxo, gain, f"{tag}_in_bwd")
    small = {"conv_w": d_conv, "a_log": s_alog, "dt_bias": s_dt, "f_bias": s_fb, "dn_norm_g": d_ng, "q_g": d_qg,
             "k_g": d_kg}
    return dx, dx16, d_gain, small, g_in, g_out


def _odd_fwd(x, gain, w_in, w_out, j, tag):
    h = _rms_fwd(x, gain, f"{tag}_norm")
    qkv = _mm(h, w_in, "nn", tm=512, tn=w_in.shape[2] // 2, out_dtype=BF16, name=f"{tag}_in", b_lead=(j,))
    o16, o32 = _sb_fwd(qkv, N_SB_HEADS, f"{tag}_sb")
    x2 = _mm(o16, w_out, "nn", tm=512, tn=x.shape[1], out_dtype=F32, name=f"{tag}_out", residual=x, b_lead=(j,))
    return x2, (x, h, qkv, o16, o32)


def _odd_bwd(dxo, dxo16, saved, gain, w_in, w_out, j, tag, g_in, g_out):
    x, h, qkv, o16, o32 = saved
    d = x.shape[1]
    do = _mm(dxo16, w_out, "nt", tm=512, tn=d, out_dtype=BF16, name=f"{tag}_out_bwd", b_lead=(j,))
    g_out = _mm(o16, dxo16, "tn", tm=512, tn=d, out_dtype=F32, name=f"{tag}_out_dw", into=(g_out, 0))
    dq, dk, dv = _sb_bwd(qkv, o32, do, N_SB_HEADS, f"{tag}_sb_bwd")
    dqkv = jnp.concatenate([dq, dk.astype(BF16), dv.astype(BF16)], axis=1)
    g_in = _mm(h, dqkv, "tn", tm=512, tn=w_in.shape[2] // 2, out_dtype=F32, name=f"{tag}_in_dw", into=(g_in, 0))
    dx, dx16, d_gain = _in_proj_bwd(dqkv, w_in, j, x, dxo, gain, f"{tag}_in_bwd")
    return dx, dx16, d_gain, g_in, g_out


def _forward_backward(x, target, w, first, rest_after, token, on_reduced):
    depth = w["norm_ffn1"].shape[0]
    row = lambda a, l: a[l][None]
    rest = {}

    def mats(names, j):
        if j == 0 and names[0] in first:
            return [first[name] for name in names] + [0]
        return [rest[name] for name in names] + [j - (1 if names[0] in first else 0)]

    def even_small(j):
        return {"conv_w": w["dn_conv_w"][j], "narrow": _narrow_params(w["dn_a_log"][j], w["dn_dt_bias"][j],
                                                                     w["fox_f_bias"][j]),
                "dn_norm_g": row(w["dn_norm_g"], j), "q_g": row(w["fox_q_norm_g"], j),
                "k_g": row(w["fox_k_norm_g"], j)}

    saved = []
    for l in range(depth):
        if l == 1:
            rest.update(rest_after(x))
        gain = row(w["norm_ffn1"], l) + token[0:1, 0:1] if l == 0 else row(w["norm_ffn1"], l)
        x, s1 = _ffn_fwd(x, gain, *mats(("ffn1_w_gu", "ffn1_w_down"), l), "ffn1")
        if l % 2 == 0:
            x, s2 = _even_fwd(x, row(w["norm_mix"], l), *mats(("w_in_even", "w_out_even"), l // 2),
                              even_small(l // 2), "even")
        else:
            x, s2 = _odd_fwd(x, row(w["norm_mix"], l), *mats(("w_in_odd", "w_out_odd"), l // 2), "odd")
        x, s3 = _ffn_fwd(x, row(w["norm_ffn2"], l), *mats(("ffn2_w_gu", "ffn2_w_down"), l), "ffn2")
        saved.append((s1, s2, s3))

    dx, dx16, loss = _loss_grad(x, target, "loss")

    kind_of = dict(BIG)
    d_norm = {k: [None] * depth for k in ("norm_ffn1", "norm_mix", "norm_ffn2")}
    d_even = [None] * ((depth + 1) // 2)
    pending, token = None, None
    for l in reversed(range(depth)):
        s1, s2, s3 = saved[l]
        mixer = ("w_in_even", "w_out_even") if l % 2 == 0 else ("w_in_odd", "w_out_odd")
        names = ["ffn1_w_gu", "ffn1_w_down", *mixer, "ffn2_w_gu", "ffn2_w_down"]
        g = {name: lax.empty((1,) + rest[name].shape[1:], F32) for name in names}
        dx, dx16, d_norm["norm_ffn2"][l], g["ffn2_w_gu"], g["ffn2_w_down"] = _ffn_bwd(
            dx, dx16, s3, row(w["norm_ffn2"], l), *mats(("ffn2_w_gu", "ffn2_w_down"), l), "ffn2", g["ffn2_w_gu"],
            g["ffn2_w_down"], after=token)
        if l % 2 == 0:
            dx, dx16, d_norm["norm_mix"][l], d_even[l // 2], g["w_in_even"], g["w_out_even"] = _even_bwd(
                dx, dx16, s2, row(w["norm_mix"], l), *mats(("w_in_even", "w_out_even"), l // 2), even_small(l // 2),
                "even", g["w_in_even"], g["w_out_even"])
            g["w_in_even"] = _even_grad_quarters(g["w_in_even"])
        else:
            dx, dx16, d_norm["norm_mix"][l], g["w_in_odd"], g["w_out_odd"] = _odd_bwd(
                dx, dx16, s2, row(w["norm_mix"], l), *mats(("w_in_odd", "w_out_odd"), l // 2), "odd", g["w_in_odd"],
                g["w_out_odd"])
        dx, dx16, d_norm["norm_ffn1"][l], g["ffn1_w_gu"], g["ffn1_w_down"] = _ffn_bwd(
            dx, dx16, s1, row(w["norm_ffn1"], l), *mats(("ffn1_w_gu", "ffn1_w_down"), l), "ffn1", g["ffn1_w_gu"],
            g["ffn1_w_down"])
        above = pending
        pending, token = _reduce_start([g[name] for name in names], [kind_of[name] for name in names], names,
                                       f"layer{l}")
        if above is not None:
            on_reduced(l + 1, dict(zip(above[-2], _reduce_finish(above, dx))))
    on_reduced(0, dict(zip(pending[-2], _reduce_finish(pending, dx))))

    small = {k: jnp.concatenate(v, axis=0) for k, v in d_norm.items()}
    dec = slice(LANE_DECAY, LANE_DECAY + N_DN_HEADS)
    fgt = slice(LANE_FORGET, LANE_FORGET + N_FOX_HEADS)
    small["dn_conv_w"] = jnp.stack([e["conv_w"] for e in d_even])
    small["dn_a_log"] = jnp.concatenate([e["a_log"][:, dec] for e in d_even], axis=0)
    small["dn_dt_bias"] = jnp.concatenate([e["dt_bias"][:, dec] for e in d_even], axis=0)
    small["fox_f_bias"] = jnp.concatenate([e["f_bias"][:, fgt] for e in d_even], axis=0)
    small["dn_norm_g"] = jnp.concatenate([e["dn_norm_g"] for e in d_even], axis=0)
    small["fox_q_norm_g"] = jnp.concatenate([e["q_g"] for e in d_even], axis=0)
    small["fox_k_norm_g"] = jnp.concatenate([e["k_g"] for e in d_even], axis=0)
    return loss, dx, small


MESH = pl.DeviceIdType.MESH
ANY = pl.BlockSpec(memory_space=pl.ANY)


def _place():
    x, y, c = lax.axis_index("x"), lax.axis_index("y"), lax.axis_index("c")
    return x, y, c, [(1 - x, y), (x, 1 - y), (1 - x, 1 - y)]


def _remote(src, dst, send_sem, recv_sem, to):
    return pltpu.make_async_remote_copy(src_ref=src, dst_ref=dst, send_sem=send_sem, recv_sem=recv_sem,
                                        device_id=to, device_id_type=MESH)


def _aligned(start, multiple):
    return start if isinstance(start, int) else pl.multiple_of(start, multiple)


def _quarter(ref, kind, chip, half, rows, cols):
    k = 2 * chip[0] + chip[1]
    hr = rows // 2
    assert hr % 16 == 0 and cols % LANES == 0
    if kind == "col":
        return ref.at[:, pl.ds(_aligned(half * hr, 16), hr), pl.ds(_aligned(k * cols, LANES), cols)]
    return ref.at[:, pl.ds(_aligned(k * rows + half * hr, 16), hr), :]


def _place_quarter(shard, kind, kc, name, first=0, count=None):
    l, rows, cols = shard.shape
    l = l - first if count is None else count
    tr = rows
    while tr * cols * 4 > (2 << 20) and tr % 32 == 0:
        tr //= 2
    nr = rows // tr
    if kind == "col":
        out_spec = pl.BlockSpec((None, tr, cols), lambda li, i, kc_ref: (li, i, kc_ref[0]))
        out_shape = (l, rows, 4 * cols)
    else:
        out_spec = pl.BlockSpec((None, tr, cols), lambda li, i, kc_ref: (li, kc_ref[0] * nr + i, 0))
        out_shape = (l, 4 * rows, cols)

    def body(kc_ref, x_ref, o_ref):
        o_ref[...] = x_ref[...].astype(BF16)

    return pl.pallas_call(
        body, name=name,
        grid_spec=pltpu.PrefetchScalarGridSpec(
            num_scalar_prefetch=1, grid=(l, nr),
            in_specs=[pl.BlockSpec((None, tr, cols), lambda li, i, kc_ref: (li + first, i, 0))],
            out_specs=out_spec),
        out_shape=jax.ShapeDtypeStruct(out_shape, BF16),
        compiler_params=_cparams("parallel", "parallel"),
    )(kc, shard)


def _gather_weights(wholes, kinds):
    n = len(wholes)

    def dims(ref, kind):
        _, r, cc = ref.shape
        return (r, cc // 4) if kind == "col" else (r // 4, cc)

    def body(*refs):
        bufs = refs[n:2 * n]
        send_sems, recv_sems = refs[2 * n:]
        x, y, c, chips = _place()
        sibling = (x, y, 1 - c)
        first, passed = [], []
        for t in range(n):
            rows, cols = dims(bufs[t], kinds[t])
            mine = _quarter(bufs[t], kinds[t], (x, y), c, rows, cols)
            for j, chip in enumerate(chips):
                cp = _remote(mine, mine, send_sems.at[t, j], recv_sems.at[t, j], (*chip, c))
                cp.start()
                first.append(cp)
        for j, chip in enumerate(chips):
            for t in range(n):
                rows, cols = dims(bufs[t], kinds[t])
                got = _quarter(bufs[t], kinds[t], chip, c, rows, cols)
                _remote(got, got, send_sems.at[t, j], recv_sems.at[t, j], (*chip, c)).wait_recv()
                cp = _remote(got, got, send_sems.at[t, 3 + j], recv_sems.at[t, 3 + j], sibling)
                cp.start()
                passed.append(cp)
        for j, chip in enumerate(chips):
            for t in range(n):
                rows, cols = dims(bufs[t], kinds[t])
                got = _quarter(bufs[t], kinds[t], chip, 1 - c, rows, cols)
                _remote(got, got, send_sems.at[t, 3 + j], recv_sems.at[t, 3 + j], sibling).wait_recv()
        for cp in first + passed:
            cp.wait_send()

    return pl.pallas_call(
        body, name="gather_weights", in_specs=[ANY] * n, out_specs=[ANY] * n,
        out_shape=[jax.ShapeDtypeStruct(a.shape, a.dtype) for a in wholes],
        input_output_aliases={t: t for t in range(n)},
        scratch_shapes=[pltpu.SemaphoreType.DMA((n, 6)), pltpu.SemaphoreType.DMA((n, 6))],
        compiler_params=pltpu.CompilerParams(has_side_effects=True),
    )(*wholes)


def _quarter_dims(ref, kind):
    _, r, cc = ref.shape
    return (r, cc // 4) if kind == "col" else (r // 4, cc)


def _gather_chips_copies(bufs, sems, kinds):
    x, y, c, chips = _place()
    copies = []
    for t, buf in enumerate(bufs):
        rows, cols = _quarter_dims(buf, kinds[t])
        mine = _quarter(buf, kinds[t], (x, y), c, rows, cols)
        for j, chip in enumerate(chips):
            pair = 2 * (OTHER_CHIPS * t + j)
            copies.append(_remote(mine, mine, sems[pair], sems[pair + 1], (*chip, c)))
    return copies


def _gather_start(wholes, kinds, after, tag):
    n = len(wholes)
    n_sems = 2 * OTHER_CHIPS * n
    n_in = n + len(after)

    def body(*refs):
        for cp in _gather_chips_copies(refs[:n], refs[n_in + n:n_in + n + n_sems], kinds):
            cp.start()
        refs[-1][...] = jnp.zeros_like(refs[-1])

    held = [pltpu.with_memory_space_constraint(a, pltpu.HBM) for a in wholes]
    out = pl.pallas_call(
        body, name=f"gather_start_{tag}", in_specs=[HBM] * n + [ANY] * len(after),
        out_specs=(*[HBM] * n, *[SEM] * n_sems, pl.BlockSpec(memory_space=pltpu.VMEM)),
        out_shape=(*[pltpu.HBM(a.shape, a.dtype) for a in held], *[pltpu.SemaphoreType.DMA(())] * n_sems,
                   jax.ShapeDtypeStruct((8, LANES), F32)),
        input_output_aliases={i: i for i in range(n)},
        compiler_params=pltpu.CompilerParams(has_side_effects=SPLIT_COPY),
    )(*held, *after)
    return out[n:n + n_sems], out[:n], out[-1]


def _gather_wait(sems, wholes, kinds, after, tag):
    n = len(wholes)

    def body(*refs):
        for cp in _gather_chips_copies(refs[:n], refs[n:n + len(sems)], kinds):
            cp.wait_send()
            cp.wait_recv()

    return pl.pallas_call(
        body, name=f"gather_wait_{tag}", in_specs=[HBM] * n + [SEM] * len(sems) + [ANY],
        out_specs=tuple([HBM] * n), out_shape=tuple(pltpu.HBM(a.shape, a.dtype) for a in wholes),
        input_output_aliases={i: i for i in range(n)},
        compiler_params=pltpu.CompilerParams(has_side_effects=SPLIT_COPY),
    )(*wholes, *sems, after)


def _gather_forward(wholes, kinds, tag):
    n = len(wholes)

    def body(*refs):
        bufs = refs[n:2 * n]
        send_sems, recv_sems = refs[2 * n:]
        x, y, c, chips = _place()
        copies = []
        for t in range(n):
            rows, cols = _quarter_dims(bufs[t], kinds[t])
            for j, chip in enumerate(chips):
                got = _quarter(bufs[t], kinds[t], chip, c, rows, cols)
                cp = _remote(got, got, send_sems.at[t, j], recv_sems.at[t, j], (x, y, 1 - c))
                cp.start()
                copies.append(cp)
        for cp in copies:
            cp.wait_send()
        for t in range(n):
            rows, cols = _quarter_dims(bufs[t], kinds[t])
            for j, chip in enumerate(chips):
                got = _quarter(bufs[t], kinds[t], chip, 1 - c, rows, cols)
                _remote(got, got, send_sems.at[t, j], recv_sems.at[t, j], (x, y, 1 - c)).wait_recv()

    return pl.pallas_call(
        body, name=f"gather_forward_{tag}", in_specs=[ANY] * n, out_specs=[ANY] * n,
        out_shape=[jax.ShapeDtypeStruct(a.shape, a.dtype) for a in wholes],
        input_output_aliases={t: t for t in range(n)},
        scratch_shapes=[pltpu.SemaphoreType.DMA((n, OTHER_CHIPS)), pltpu.SemaphoreType.DMA((n, OTHER_CHIPS))],
        compiler_params=pltpu.CompilerParams(has_side_effects=True),
    )(*wholes)


def _canonical(a, kind):
    l, r, c = a.shape
    return a.reshape(l, 1, r, c) if kind == "col" else a.reshape(l, 4, r // 4, c)


def _rs_sibling(parts):
    n = len(parts)

    def body(*refs):
        ins, outs = refs[:n], refs[n:2 * n]
        send_sems, recv_sems = refs[2 * n:]
        x, y, c, _ = _place()
        copies = []
        for t in range(n):
            hr = ins[t].shape[2] // 2
            src = ins[t].at[:, :, pl.ds(pl.multiple_of((1 - c) * hr, 8), hr), :]
            cp = _remote(src, outs[t], send_sems.at[t], recv_sems.at[t], (x, y, 1 - c))
            cp.start()
            copies.append(cp)
        for cp in copies:
            cp.wait()

    half = lambda a: jax.ShapeDtypeStruct(a.shape[:2] + (a.shape[2] // 2, a.shape[3]), a.dtype)
    return pl.pallas_call(
        body, name="reduce_sibling", in_specs=[ANY] * n, out_specs=[ANY] * n, out_shape=[half(a) for a in parts],
        scratch_shapes=[pltpu.SemaphoreType.DMA((n,)), pltpu.SemaphoreType.DMA((n,))],
        compiler_params=pltpu.CompilerParams(has_side_effects=True),
    )(*parts)


def _add_tile(rows, cols):
    tc = cols if cols <= 1536 else cols // 4
    tr = rows
    while tr * tc * 4 > (1 << 20) and tr % 16 == 0:
        tr //= 2
    return tr, tc


def _rs_add_sibling(part, got, c, name):
    l, a, hr, cols = got.shape
    tr, tc = _add_tile(hr, cols)
    nr = hr // tr

    def body(c_ref, p_ref, g_ref, o32_ref, o16_ref):
        s = p_ref[...] + g_ref[...]
        o32_ref[...] = s
        o16_ref[...] = s.astype(BF16)

    blk = (None, None, tr, tc)
    spec = pl.BlockSpec(blk, lambda li, ai, i, j, c_ref: (li, ai, i, j))
    return pl.pallas_call(
        body, name=name,
        grid_spec=pltpu.PrefetchScalarGridSpec(
            num_scalar_prefetch=1, grid=(l, a, nr, cols // tc),
            in_specs=[pl.BlockSpec(blk, lambda li, ai, i, j, c_ref: (li, ai, c_ref[0] * nr + i, j)), spec],
            out_specs=[spec, spec]),
        out_shape=[jax.ShapeDtypeStruct(got.shape, F32), jax.ShapeDtypeStruct(got.shape, BF16)],
        compiler_params=_cparams("parallel", "parallel", "parallel", "parallel"),
    )(c, part, got)


def _quarter4(ref, kind, chip, cols):
    k = 2 * chip[0] + chip[1]
    if kind == "col":
        return ref.at[:, :, :, pl.ds(pl.multiple_of(k * cols, LANES), cols)]
    return ref.at[:, pl.ds(k, 1), :, :]


HBM = pl.BlockSpec(memory_space=pltpu.HBM)
SEM = pl.BlockSpec(memory_space=pltpu.SEMAPHORE)
SPLIT_COPY = pltpu.SideEffectType.DATAFLOW_SIDE_EFFECTING
OTHER_CHIPS = 3


def _quarter4_shape(a, kind):
    l, _, hr, cols = a.shape
    return (l, 1, hr, cols // 4 if kind == "col" else cols)


def _rs_chips_copies(srcs, lands, sems, kinds):
    x, y, c, chips = _place()
    copies = []
    for t, (src, land) in enumerate(zip(srcs, lands)):
        cols = _quarter4_shape(src, kinds[t])[3]
        for j, chip in enumerate(chips):
            pair = 2 * (OTHER_CHIPS * t + j)
            copies.append(_remote(_quarter4(src, kinds[t], chip, cols), land.at[j], sems[pair], sems[pair + 1],
                                  (*chip, c)))
    return copies


def _rs_chips_start(sums16, kinds, tag):
    n = len(sums16)
    n_sems = 2 * OTHER_CHIPS * n

    def body(*refs):
        srcs, lands, sems = refs[:n], refs[n:2 * n], refs[4 * n:4 * n + n_sems]
        for cp in _rs_chips_copies(srcs, lands, sems, kinds):
            cp.start()
        refs[-1][...] = jnp.zeros_like(refs[-1])

    lands = [lax.empty((OTHER_CHIPS,) + _quarter4_shape(a, k), a.dtype) for a, k in zip(sums16, kinds)]
    held = [pltpu.with_memory_space_constraint(a, pltpu.HBM) for a in (*sums16, *lands)]
    out = pl.pallas_call(
        body, name=f"reduce_chips_start_{tag}", in_specs=[HBM] * (2 * n),
        out_specs=(*[HBM] * (2 * n), *[SEM] * n_sems, pl.BlockSpec(memory_space=pltpu.VMEM)),
        out_shape=(*[pltpu.HBM(a.shape, a.dtype) for a in held], *[pltpu.SemaphoreType.DMA(())] * n_sems,
                   jax.ShapeDtypeStruct((8, LANES), F32)),
        input_output_aliases={i: i for i in range(2 * n)},
        compiler_params=pltpu.CompilerParams(has_side_effects=SPLIT_COPY),
    )(*held)
    return out[2 * n:2 * n + n_sems], out[:n], out[n:2 * n], out[-1]


def _rs_chips_wait(sems, srcs, lands, kinds, after, tag):
    n = len(srcs)

    def body(*refs):
        for cp in _rs_chips_copies(refs[:n], refs[n:2 * n], refs[2 * n:2 * n + len(sems)], kinds):
            cp.wait_send()
            cp.wait_recv()

    out = pl.pallas_call(
        body, name=f"reduce_chips_wait_{tag}", in_specs=[HBM] * (2 * n) + [SEM] * len(sems) + [ANY],
        out_specs=tuple([HBM] * (2 * n)),
        out_shape=tuple(pltpu.HBM(a.shape, a.dtype) for a in (*srcs, *lands)),
        input_output_aliases={i: i for i in range(2 * n)},
        compiler_params=pltpu.CompilerParams(has_side_effects=SPLIT_COPY),
    )(*srcs, *lands, *sems, after)
    return out[n:]


def _rs_add_chips(sum32, got, kind, kc, name):
    _, l, _, hr, cols = got.shape
    tr, _ = _add_tile(hr, cols)
    nr = hr // tr
    k_arr, c_arr = kc
    if kind == "col":
        own = pl.BlockSpec((None, None, tr, cols), lambda li, i, k_ref, c_ref: (li, 0, i, k_ref[0]))
    else:
        own = pl.BlockSpec((None, None, tr, cols), lambda li, i, k_ref, c_ref: (li, k_ref[0], i, 0))

    def body(k_ref, c_ref, own_ref, got_ref, o_ref):
        o_ref[...] = ((own_ref[...] + got_ref[0].astype(F32)) + got_ref[1].astype(F32)) + got_ref[2].astype(F32)

    return pl.pallas_call(
        body, name=name,
        grid_spec=pltpu.PrefetchScalarGridSpec(
            num_scalar_prefetch=2, grid=(l, nr),
            in_specs=[own, pl.BlockSpec((3, None, None, tr, cols), lambda li, i, k_ref, c_ref: (0, li, 0, i, 0))],
            out_specs=pl.BlockSpec((None, tr, cols), lambda li, i, k_ref, c_ref: (li, c_ref[0] * nr + i, 0))),
        out_shape=jax.ShapeDtypeStruct((l, 2 * hr, cols), F32),
        compiler_params=_cparams("parallel", "parallel"),
    )(k_arr, c_arr, sum32, got)


def _rs_finish(quarters):
    n = len(quarters)

    def body(*refs):
        bufs = refs[n:2 * n]
        send_sems, recv_sems = refs[2 * n:]
        x, y, c, _ = _place()
        copies = []
        for t in range(n):
            hr = bufs[t].shape[1] // 2
            mine = bufs[t].at[:, pl.ds(pl.multiple_of(c * hr, 8), hr), :]
            cp = _remote(mine, mine, send_sems.at[t], recv_sems.at[t], (x, y, 1 - c))
            cp.start()
            copies.append(cp)
        for cp in copies:
            cp.wait()

    return pl.pallas_call(
        body, name="reduce_finish", in_specs=[ANY] * n, out_specs=[ANY] * n,
        out_shape=[jax.ShapeDtypeStruct(a.shape, a.dtype) for a in quarters],
        input_output_aliases={t: t for t in range(n)},
        scratch_shapes=[pltpu.SemaphoreType.DMA((n,)), pltpu.SemaphoreType.DMA((n,))],
        compiler_params=pltpu.CompilerParams(has_side_effects=True),
    )(*quarters)


def _reduce_start(parts, kinds, names, tag):
    c_arr = jnp.reshape(lax.axis_index("c"), (1,)).astype(jnp.int32)
    canon = [_canonical(p, kind) for p, kind in zip(parts, kinds)]
    from_sibling = _rs_sibling(canon)
    sums = [_rs_add_sibling(p, g, c_arr, f"reduce_add_sibling_{nm}") for p, g, nm in zip(canon, from_sibling, names)]
    sems, srcs, lands, token = _rs_chips_start([s16 for _, s16 in sums], kinds, tag)
    return (sems, srcs, lands, [s32 for s32, _ in sums], kinds, names, tag), token


def _reduce_finish(state, after):
    sems, srcs, lands, sums32, kinds, names, tag = state
    x, y, c = lax.axis_index("x"), lax.axis_index("y"), lax.axis_index("c")
    kc = (jnp.reshape(2 * x + y, (1,)).astype(jnp.int32), jnp.reshape(c, (1,)).astype(jnp.int32))
    from_chips = _rs_chips_wait(sems, srcs, lands, kinds, after, tag)
    halves = [_rs_add_chips(s32, g, kind, kc, f"reduce_add_chips_{nm}")
              for s32, g, kind, nm in zip(sums32, from_chips, kinds, names)]
    return _rs_finish(halves)


SMALL_PEERS = 7


def _small_exchange(pack):
    rows = pack.shape[0]

    def body(p_ref, slots_ref, total_ref, send_sems, recv_sems):
        x, y, c, _ = _place()
        me = 4 * x + 2 * y + c
        slots_ref[me] = p_ref[...]
        copies = []
        for p in range(1, SMALL_PEERS + 1):
            px, py, pc = (p >> 2) & 1, (p >> 1) & 1, p & 1
            peer = (1 - x if px else x, 1 - y if py else y, 1 - c if pc else c)
            cp = _remote(p_ref, slots_ref.at[me], send_sems.at[p - 1], recv_sems.at[p - 1], peer)
            cp.start()
            copies.append(cp)
        for cp in copies:
            cp.wait()
        total = slots_ref[0]
        for i in range(1, SMALL_PEERS + 1):
            total = total + slots_ref[i]
        total_ref[...] = total

    vmem = pl.BlockSpec(memory_space=pltpu.VMEM)
    return pl.pallas_call(
        body, name="small_exchange", in_specs=[vmem], out_specs=[vmem, vmem],
        out_shape=[jax.ShapeDtypeStruct((SMALL_PEERS + 1, rows, LANES), F32), jax.ShapeDtypeStruct((rows, LANES), F32)],
        scratch_shapes=[pltpu.SemaphoreType.DMA((SMALL_PEERS,)), pltpu.SemaphoreType.DMA((SMALL_PEERS,))],
        compiler_params=pltpu.CompilerParams(has_side_effects=True),
    )(pack)


def _pack(arrays):
    rows = []
    for a in arrays:
        flat = a.reshape(-1).astype(F32)
        rows.append(jnp.pad(flat, (0, (-flat.shape[0]) % LANES)).reshape(-1, LANES))
    out = jnp.concatenate(rows, axis=0)
    return jnp.pad(out, ((0, (-out.shape[0]) % 8), (0, 0)))


def _unpack(pack, shapes):
    out, r = [], 0
    for sh in shapes:
        size = math.prod(sh)
        nr = -(-size // LANES)
        out.append(pack[r:r + nr].reshape(-1)[:size].reshape(sh))
        r += nr
    return out


def _adamw(w, g, m, v, name):
    shape = w.shape
    to2d = lambda a: a.reshape(-1, shape[-1])
    rows = math.prod(shape[:-1])
    tile = 256 if rows % 256 == 0 else rows

    def fn(wb, gb, mb, vb):
        m2 = ADAM_B1 * mb + (1.0 - ADAM_B1) * gb
        v2 = ADAM_B2 * vb + (1.0 - ADAM_B2) * (gb * gb)
        m_hat = m2 / (1.0 - ADAM_B1 ** ADAM_STEP)
        v_hat = v2 / (1.0 - ADAM_B2 ** ADAM_STEP)
        return -ADAM_LR * (m_hat / (jnp.sqrt(v_hat) + ADAM_EPS) + ADAM_WD * wb), m2, v2

    res = _rowwise(fn, [to2d(w), to2d(g), to2d(m), to2d(v)], [], [(shape[-1], F32)] * 3, [], tile=tile, name=name)
    return [r.reshape(shape) for r in res]


def _adamw_layer(w, g, m, v, layer, outs, name):
    _, rows, cols = w.shape
    tile = rows
    while tile * cols * 4 > (1 << 20) and tile % 16 == 0:
        tile //= 2

    def body(w_ref, g_ref, m_ref, v_ref, *rest):
        g_out, d_out, m_out, v_out = rest[-4:]
        gb = g_ref[...]
        m2 = ADAM_B1 * m_ref[...] + (1.0 - ADAM_B1) * gb
        v2 = ADAM_B2 * v_ref[...] + (1.0 - ADAM_B2) * (gb * gb)
        m_hat = m2 / (1.0 - ADAM_B1 ** ADAM_STEP)
        v_hat = v2 / (1.0 - ADAM_B2 ** ADAM_STEP)
        g_out[...] = gb
        d_out[...] = -ADAM_LR * (m_hat / (jnp.sqrt(v_hat) + ADAM_EPS) + ADAM_WD * w_ref[...])
        m_out[...] = m2
        v_out[...] = v2

    stacked = pl.BlockSpec((None, tile, cols), lambda i: (layer, i, 0))
    return pl.pallas_call(
        body, name=name, grid=(rows // tile,),
        in_specs=[stacked, pl.BlockSpec((None, tile, cols), lambda i: (0, i, 0)), stacked, stacked] + [ANY] * 4,
        out_specs=[stacked] * 4, out_shape=[jax.ShapeDtypeStruct(w.shape, F32)] * 4,
        input_output_aliases={4 + i: i for i in range(4)}, compiler_params=_cparams("parallel"),
    )(w, g, m, v, *outs)


BIG = (("ffn1_w_gu", "col"), ("ffn1_w_down", "row"), ("w_in_even", "col"), ("w_out_even", "row"),
       ("w_in_odd", "col"), ("w_out_odd", "row"), ("ffn2_w_gu", "col"), ("ffn2_w_down", "row"))
SMALL = ("norm_ffn1", "norm_mix", "dn_conv_w", "dn_a_log", "dn_dt_bias", "dn_norm_g", "fox_q_norm_g", "fox_k_norm_g",
         "fox_f_bias", "norm_ffn2")
WEIGHTS = ("norm_ffn1", "ffn1_w_gu", "ffn1_w_down", "norm_mix", "w_in_even", "dn_conv_w", "dn_a_log", "dn_dt_bias",
           "dn_norm_g", "fox_q_norm_g", "fox_k_norm_g", "fox_f_bias", "w_out_even", "w_in_odd", "w_out_odd",
           "norm_ffn2", "ffn2_w_gu", "ffn2_w_down")


def _step(x, target, w, m, v):
    k = 2 * lax.axis_index("x") + lax.axis_index("y")
    n_conv = w["dn_conv_w"].shape[2]

    kc = jnp.reshape(k, (1,)).astype(jnp.int32)
    kinds = dict(BIG)
    quarters = {name: w[name] for name in kinds}
    quarters["w_in_even"] = jnp.pad(w["w_in_even"], ((0, 0), (0, 0), (0, EVEN_QUARTER_PAD - EVEN_QUARTER)))
    first_names = [name for name in kinds if name not in ("w_in_odd", "w_out_odd")]
    rest_names = list(kinds)

    def even_columns(whole):
        padded = whole["w_in_even"]
        ref_order = jnp.concatenate([padded[..., q * EVEN_QUARTER_PAD:q * EVEN_QUARTER_PAD + EVEN_QUARTER]
                                     for q in range(4)], axis=-1)
        return {**whole, "w_in_even": _even_to_kernel_layout(ref_order)}

    conv_slots, _ = _small_exchange(_pack([w["dn_conv_w"]]))
    placed = [_place_quarter(quarters[name], kinds[name], kc, f"place_first_{name}", 0, 1) for name in first_names]
    gathered = _gather_weights(placed, [kinds[name] for name in first_names])
    first = even_columns(dict(zip(first_names, gathered)))
    placed = [_place_quarter(quarters[name], kinds[name], kc, f"place_rest_{name}", 1 if name in first_names else 0)
              for name in rest_names]
    rest_kinds = [kinds[name] for name in rest_names]
    sems, on_their_way, token = _gather_start(placed, rest_kinds, [conv_slots, *gathered], "rest")

    def rest_after(value):
        landed = _gather_wait(sems, on_their_way, rest_kinds, value, "rest")
        return even_columns(dict(zip(rest_names, _gather_forward(landed, rest_kinds, "rest"))))

    whole = {}
    conv_rows = math.prod(w["dn_conv_w"].shape) // LANES
    conv_quarters = [conv_slots[2 * q, :conv_rows].reshape(w["dn_conv_w"].shape) for q in range(4)]
    whole["dn_conv_w"] = jnp.concatenate(conv_quarters, axis=-1)
    for name in SMALL:
        if name != "dn_conv_w":
            whole[name] = w[name]

    updated = {name: [lax.empty(w[name].shape, F32) for _ in range(4)] for name in kinds}

    def on_reduced(layer, layer_grads):
        for name, g in layer_grads.items():
            if name == "w_in_even":
                g = g[..., :EVEN_QUARTER]
            stacked_layer = layer if w[name].shape[0] == w["norm_mix"].shape[0] else layer // 2
            updated[name] = _adamw_layer(w[name], g, m[name], v[name], stacked_layer, updated[name], f"adamw_{name}")

    loss, dx, small = _forward_backward(x, target, whole, first, rest_after, token, on_reduced)

    _, small_sum = _small_exchange(_pack([small[n] for n in SMALL]))
    grads = dict(zip(SMALL, _unpack(small_sum, [small[n].shape for n in SMALL])))
    grads["dn_conv_w"] = lax.dynamic_slice_in_dim(grads["dn_conv_w"], k * n_conv, n_conv, axis=2)
    delta, new_m, new_v = {}, {}, {}
    for name in kinds:
        grads[name], delta[name], new_m[name], new_v[name] = updated[name]
    packs = [_pack([d[n] for n in SMALL]) for d in (w, grads, m, v)]
    shapes = [w[n].shape for n in SMALL]
    for out, res in zip((delta, new_m, new_v), _adamw(*packs, "adamw_small")):
        out.update(zip(SMALL, _unpack(res, shapes)))
    total_loss = lax.psum(loss[0, 0], ("x", "y", "c"))
    return total_loss, dx, grads, delta, new_m, new_v


def kernel(x, norm_ffn1, ffn1_w_gu, ffn1_w_down, norm_mix, w_in_even, dn_conv_w, dn_a_log, dn_dt_bias, dn_norm_g, fox_q_norm_g, fox_k_norm_g, fox_f_bias, w_out_even, w_in_odd, w_out_odd, norm_ffn2, ffn2_w_gu, ffn2_w_down, loss_target, m_norm_ffn1, m_ffn1_w_gu, m_ffn1_w_down, m_norm_mix, m_w_in_even, m_dn_conv_w, m_dn_a_log, m_dn_dt_bias, m_dn_norm_g, m_fox_q_norm_g, m_fox_k_norm_g, m_fox_f_bias, m_w_out_even, m_w_in_odd, m_w_out_odd, m_norm_ffn2, m_ffn2_w_gu, m_ffn2_w_down, v_norm_ffn1, v_ffn1_w_gu, v_ffn1_w_down, v_norm_mix, v_w_in_even, v_dn_conv_w, v_dn_a_log, v_dn_dt_bias, v_dn_norm_g, v_fox_q_norm_g, v_fox_k_norm_g, v_fox_f_bias, v_w_out_even, v_w_in_odd, v_w_out_odd, v_norm_ffn2, v_ffn2_w_gu, v_ffn2_w_down):
    w = dict(zip(WEIGHTS, (norm_ffn1, ffn1_w_gu, ffn1_w_down, norm_mix, w_in_even, dn_conv_w, dn_a_log, dn_dt_bias,
                           dn_norm_g, fox_q_norm_g, fox_k_norm_g, fox_f_bias, w_out_even, w_in_odd, w_out_odd,
                           norm_ffn2, ffn2_w_gu, ffn2_w_down)))
    m = dict(zip(WEIGHTS, (m_norm_ffn1, m_ffn1_w_gu, m_ffn1_w_down, m_norm_mix, m_w_in_even, m_dn_conv_w, m_dn_a_log,
                           m_dn_dt_bias, m_dn_norm_g, m_fox_q_norm_g, m_fox_k_norm_g, m_fox_f_bias, m_w_out_even,
                           m_w_in_odd, m_w_out_odd, m_norm_ffn2, m_ffn2_w_gu, m_ffn2_w_down)))
    v = dict(zip(WEIGHTS, (v_norm_ffn1, v_ffn1_w_gu, v_ffn1_w_down, v_norm_mix, v_w_in_even, v_dn_conv_w, v_dn_a_log,
                           v_dn_dt_bias, v_dn_norm_g, v_fox_q_norm_g, v_fox_k_norm_g, v_fox_f_bias, v_w_out_even,
                           v_w_in_odd, v_w_out_odd, v_norm_ffn2, v_ffn2_w_gu, v_ffn2_w_down)))
    loss, dx, grads, delta, new_m, new_v = _step(x[0], loss_target[0], w, m, v)
    return (loss, dx[None], *[grads[n] for n in WEIGHTS], *[delta[n] for n in WEIGHTS],
            *[new_m[n] for n in WEIGHTS], *[new_v[n] for n in WEIGHTS])
```

```python
import functools
import math

import jax
import jax.numpy as jnp
from jax import lax
from jax.experimental import pallas as pl
from jax.experimental.pallas import tpu as pltpu

F32 = jnp.float32
BF16 = jnp.bfloat16
HI = lax.Precision.HIGH

HEAD_DIM = 128
N_DN_HEADS = 4
N_FOX_HEADS = 4
N_SB_HEADS = 8
D_DN = N_DN_HEADS * HEAD_DIM
D_FOX = N_FOX_HEADS * HEAD_DIM
CONV_WIDTH = 4
DN_CHUNK = 64
EPS = 1e-6
ATT_SCALE = HEAD_DIM ** -0.5
ADAM_LR, ADAM_B1, ADAM_B2, ADAM_EPS, ADAM_WD, ADAM_STEP = 0.001, 0.9, 0.999, 1e-08, 0.01, 10

V7X_VMEM_LIMIT = 56 * 1024 * 1024
LANES = 128
ATT_TQ = 256
ATT_TK = 128
ATT_SUB = ATT_TQ // ATT_TK

LANE_BETA, LANE_DECAY, LANE_FORGET = 0, 4, 8


def _cparams(*sem):
    return pltpu.CompilerParams(dimension_semantics=sem, vmem_limit_bytes=V7X_VMEM_LIMIT)


def _sigmoid(x):
    return 1.0 / (1.0 + jnp.exp(-x))


def _softplus(x):
    return jnp.maximum(x, 0.0) + jnp.log(1.0 + jnp.exp(-jnp.abs(x)))


def _silu_grad(y, sg):
    return sg * (1.0 + y * (1.0 - sg))


def _rowwise(fn, rows, bcast, outs, sums, *, tile, name):
    rows = [r if isinstance(r, tuple) else (r, r.shape[1], 0) for r in rows]
    s = rows[0][0].shape[0]
    assert s % tile == 0
    n_in, n_b, n_out, n_sum = len(rows), len(bcast), len(outs), len(sums)

    def body(*refs):
        ins = [r[...] for r in refs[:n_in + n_b]]
        res = fn(*ins)
        if not isinstance(res, (tuple, list)):
            res = (res,)
        out_refs = refs[n_in + n_b:n_in + n_b + n_out]
        sum_refs = refs[n_in + n_b + n_out:]
        for o_ref, val in zip(out_refs, res[:n_out]):
            o_ref[...] = val.astype(o_ref.dtype)
        if n_sum:
            @pl.when(pl.program_id(0) == 0)
            def _():
                for s_ref in sum_refs:
                    s_ref[...] = jnp.zeros_like(s_ref)
            for s_ref, val in zip(sum_refs, res[n_out:]):
                s_ref[...] += val

    in_specs = [pl.BlockSpec((tile, w), lambda i, cb=cb: (i, cb)) for _, w, cb in rows]
    in_specs += [pl.BlockSpec(b.shape, lambda i, nd=b.ndim: (0,) * nd) for b in bcast]
    out_specs = [pl.BlockSpec((tile, c), lambda i: (i, 0)) for c, _ in outs]
    out_specs += [pl.BlockSpec(sh, lambda i: (0, 0)) for sh in sums]
    out_shape = [jax.ShapeDtypeStruct((s, c), dt) for c, dt in outs]
    out_shape += [jax.ShapeDtypeStruct(sh, F32) for sh in sums]
    return pl.pallas_call(
        body, name=name, grid=(s // tile,), in_specs=in_specs, out_specs=out_specs, out_shape=out_shape,
        compiler_params=_cparams("arbitrary" if n_sum else "parallel"),
    )(*[r[0] for r in rows], *bcast)


def _rms_fwd(x, gain, name):
    def fn(xb, g):
        r = lax.rsqrt(jnp.mean(xb * xb, axis=-1, keepdims=True) + EPS)
        return (xb * r * g,)
    return _rowwise(fn, [x], [gain], [(x.shape[1], BF16)], [], tile=512, name=name)[0]


_DIMS = {"nn": (((1,), (0,)), ((), ())), "nt": (((1,), (1,)), ((), ())), "tn": (((0,), (0,)), ((), ()))}


def _dot(a, b, kind):
    return lax.dot_general(a.astype(BF16), b.astype(BF16), _DIMS[kind], preferred_element_type=F32)


def _dot32(a, b, kind="nn"):
    return lax.dot_general(a, b, _DIMS[kind], precision=HI, preferred_element_type=F32)


def _mm(a, b, kind, *, tm, tn, out_dtype, name, scale=None, residual=None, a_lead=(), b_lead=(),
        b_spec=None, n=None, into=None):
    ash, bsh = a.shape[len(a_lead):], b.shape[len(b_lead):]
    m = ash[1] if kind == "tn" else ash[0]
    k = ash[0] if kind == "tn" else ash[1]
    if b_spec is None:
        n = bsh[0] if kind == "nt" else bsh[1]
        assert k == (bsh[1] if kind == "nt" else bsh[0]), (ash, bsh, kind)
    assert m % tm == 0 and n % tn == 0, (m, tm, n, tn)
    la, lb = (None,) * len(a_lead), (None,) * len(b_lead)
    if kind == "tn":
        a_spec = pl.BlockSpec(la + (k, tm), lambda j, i: a_lead + (0, i))
    else:
        a_spec = pl.BlockSpec(la + (tm, k), lambda j, i: a_lead + (i, 0))
    if b_spec is None:
        if kind == "nt":
            b_spec = pl.BlockSpec(lb + (tn, k), lambda j, i: b_lead + (j, 0))
        else:
            b_spec = pl.BlockSpec(lb + (k, tn), lambda j, i: b_lead + (0, j))
    in_specs, args = [a_spec, b_spec], [a, b]
    if residual is not None:
        in_specs.append(pl.BlockSpec((tm, tn), lambda j, i: (i, j)))
        args.append(residual)
    aliases = {}
    if into is not None:
        buf, layer = into
        in_specs.append(pl.BlockSpec(memory_space=pl.ANY))
        args.append(buf)
        aliases = {len(args) - 1: 0}
        out_spec = pl.BlockSpec((None, tm, tn), lambda j, i: (layer, i, j))
        out_shape = jax.ShapeDtypeStruct(buf.shape, buf.dtype)
    else:
        out_spec = pl.BlockSpec((tm, tn), lambda j, i: (i, j))
        out_shape = jax.ShapeDtypeStruct((m, n), out_dtype)

    def body(a_ref, b_ref, *rest):
        acc = _dot(a_ref[...], b_ref[...], kind)
        if scale is not None:
            acc = acc * scale
        if residual is not None:
            acc = acc + rest[0][...]
        rest[-1][...] = acc.astype(rest[-1].dtype)

    return pl.pallas_call(
        body, name=name, grid=(n // tn, m // tm), in_specs=in_specs, out_specs=out_spec, out_shape=out_shape,
        input_output_aliases=aliases, compiler_params=_cparams("parallel", "parallel"),
    )(*args)


def _ffn_up(n, w_gu, layer, name):
    s, d = n.shape
    f = w_gu.shape[2] // 2
    tm, tn = 512, f // 2
    nj = f // tn

    def body(n_ref, wg_ref, wu_ref, gu_ref, a_ref):
        nv = n_ref[...]
        g = _dot(nv, wg_ref[...], "nn")
        u = _dot(nv, wu_ref[...], "nn")
        gu_ref[0] = g.astype(BF16)
        gu_ref[1] = u.astype(BF16)
        a_ref[...] = (g * _sigmoid(g) * u).astype(BF16)

    return pl.pallas_call(
        body, name=name, grid=(nj, s // tm),
        in_specs=[pl.BlockSpec((tm, d), lambda j, i: (i, 0)),
                  pl.BlockSpec((None, d, tn), lambda j, i: (layer, 0, j)),
                  pl.BlockSpec((None, d, tn), lambda j, i: (layer, 0, j + nj))],
        out_specs=[pl.BlockSpec((2, tm, tn), lambda j, i: (0, i, j)),
                   pl.BlockSpec((tm, tn), lambda j, i: (i, j))],
        out_shape=[jax.ShapeDtypeStruct((2, s, f), BF16), jax.ShapeDtypeStruct((s, f), BF16)],
        compiler_params=_cparams("parallel", "parallel"),
    )(n, w_gu, w_gu)


def _ffn_down_bwd(dxo, w_down, gu, layer, name, after=None):
    s, d = dxo.shape
    f = w_down.shape[1]
    tm, tn = 512, f // 2
    extra_specs, extra = ([ANY], [after]) if after is not None else ([], [])

    def body(dx_ref, w_ref, gu_ref, *rest):
        dgu_ref = rest[-1]
        da = 0.5 * _dot(dx_ref[...], w_ref[...], "nt")
        g = gu_ref[0].astype(F32)
        u = gu_ref[1].astype(F32)
        sg = _sigmoid(g)
        dgu_ref[0] = (da * u * _silu_grad(g, sg)).astype(BF16)
        dgu_ref[1] = (da * g * sg).astype(BF16)

    return pl.pallas_call(
        body, name=name, grid=(f // tn, s // tm),
        in_specs=[pl.BlockSpec((tm, d), lambda j, i: (i, 0)),
                  pl.BlockSpec((None, tn, d), lambda j, i: (layer, j, 0)),
                  pl.BlockSpec((2, tm, tn), lambda j, i: (0, i, j))] + extra_specs,
        out_specs=pl.BlockSpec((2, tm, tn), lambda j, i: (0, i, j)),
        out_shape=jax.ShapeDtypeStruct((2, s, f), BF16),
        compiler_params=_cparams("parallel", "parallel"),
    )(dxo, w_down, gu, *extra)


NORM_BWD_TM = 256


def _norm_bwd_after(terms, operands, specs, x, dres, gain, name):
    s, d = x.shape
    tm = NORM_BWD_TM
    n_op = len(operands)

    def body(*refs):
        x_ref, dres_ref, g_ref = refs[n_op:n_op + 3]
        dx_ref, dx16_ref, dgain_ref = refs[n_op + 3:]
        dn = None
        for a, b in terms(*refs[:n_op]):
            dn = _dot(a, b, "nt") if dn is None else dn + _dot(a, b, "nt")
        xb = x_ref[...]
        r = lax.rsqrt(jnp.mean(xb * xb, axis=-1, keepdims=True) + EPS)
        xh = xb * r
        dxh = dn * g_ref[...]
        dx = dres_ref[...] + r * (dxh - xh * jnp.mean(dxh * xh, axis=-1, keepdims=True))
        dx_ref[...] = dx
        dx16_ref[...] = dx.astype(BF16)

        @pl.when(pl.program_id(0) == 0)
        def _():
            dgain_ref[...] = jnp.zeros_like(dgain_ref)
        dgain_ref[...] += jnp.sum(dn * xh, axis=0, keepdims=True)

    rows = pl.BlockSpec((tm, d), lambda i: (i, 0))
    return pl.pallas_call(
        body, name=name, grid=(s // tm,),
        in_specs=list(specs) + [rows, rows, pl.BlockSpec((1, d), lambda i: (0, 0))],
        out_specs=[rows, rows, pl.BlockSpec((1, d), lambda i: (0, 0))],
        out_shape=[jax.ShapeDtypeStruct((s, d), F32), jax.ShapeDtypeStruct((s, d), BF16),
                   jax.ShapeDtypeStruct((1, d), F32)],
        compiler_params=_cparams("arbitrary"),
    )(*operands, x, dres, gain)


def _ffn_up_bwd(dgu, w_gu, layer, x, dres, gain, name):
    _, s, f = dgu.shape
    d = w_gu.shape[1]
    specs = [pl.BlockSpec((2, NORM_BWD_TM, f), lambda i: (0, i, 0)),
             pl.BlockSpec((None, d, f), lambda i: (layer, 0, 0)),
             pl.BlockSpec((None, d, f), lambda i: (layer, 0, 1))]
    terms = lambda dgu_ref, wg_ref, wu_ref: [(dgu_ref[0], wg_ref[...]), (dgu_ref[1], wu_ref[...])]
    return _norm_bwd_after(terms, [dgu, w_gu, w_gu], specs, x, dres, gain, name)


def _in_proj_bwd(dproj, w_in, j, x, dres, gain, name):
    k = dproj.shape[1]
    d = w_in.shape[1]
    specs = [pl.BlockSpec((NORM_BWD_TM, k), lambda i: (i, 0)), pl.BlockSpec((None, d, k), lambda i: (j, 0, 0))]
    terms = lambda a_ref, b_ref: [(a_ref[...], b_ref[...])]
    return _norm_bwd_after(terms, [dproj, w_in], specs, x, dres, gain, name)


def _ffn_fwd(x, gain, w_gu, w_down, layer, tag):
    n = _rms_fwd(x, gain, f"{tag}_norm")
    gu, a = _ffn_up(n, w_gu, layer, f"{tag}_up")
    x2 = _mm(a, w_down, "nn", tm=512, tn=x.shape[1], out_dtype=F32, name=f"{tag}_down", scale=0.5, residual=x,
             b_lead=(layer,))
    return x2, (x, n, gu, a)


def _ffn_bwd(dxo, dxo16, saved, gain, w_gu, w_down, layer, tag, g_gu, g_down, after=None):
    x, n, gu, a = saved
    s, f = a.shape
    dgu = _ffn_down_bwd(dxo16, w_down, gu, layer, f"{tag}_down_bwd", after)
    g_down = _mm(a, dxo16, "tn", tm=256, tn=dxo16.shape[1], out_dtype=F32, name=f"{tag}_down_dw", scale=0.5,
                 into=(g_down, 0))
    tn = f // 2
    nj = f // tn
    g_gu = _mm(n, dgu, "tn", tm=512, tn=tn, out_dtype=F32, name=f"{tag}_up_dw", into=(g_gu, 0), n=2 * f,
               b_spec=pl.BlockSpec((None, s, tn), lambda j, i: (j // nj, 0, j % nj)))
    dx, dx16, dgain = _ffn_up_bwd(dgu, w_gu, layer, x, dxo, gain, f"{tag}_up_bwd")
    return dx, dx16, dgain, g_gu, g_down


def _lane_col(blk, lane):
    li = lax.broadcasted_iota(jnp.int32, blk.shape, 1)
    return jnp.sum(jnp.where(li == lane, blk, 0.0), axis=1, keepdims=True)


def _split_dot(x, tri):
    hi = x.astype(BF16)
    lo = (x - hi.astype(F32)).astype(BF16)
    return (lax.dot_general(hi, tri, _DIMS["nn"], preferred_element_type=F32)
            + lax.dot_general(lo, tri, _DIMS["nn"], preferred_element_type=F32))


class _Each:
    def __init__(self, vals):
        self.vals = list(vals)

    def _with(self, other, op):
        others = other.vals if isinstance(other, _Each) else [other] * len(self.vals)
        return _Each(op(a, b) for a, b in zip(self.vals, others))

    def __add__(self, other):
        return self._with(other, lambda a, b: a + b)

    def __sub__(self, other):
        return self._with(other, lambda a, b: a - b)

    def __mul__(self, other):
        return self._with(other, lambda a, b: a * b)

    def __neg__(self):
        return _Each(-a for a in self.vals)


def _each(fn, *args):
    n = max(len(a.vals) for a in args if isinstance(a, _Each))
    res = [fn(*xs) for xs in zip(*[a.vals if isinstance(a, _Each) else [a] * n for a in args])]
    if isinstance(res[0], tuple):
        return tuple(_Each(r) for r in zip(*res))
    return _Each(res)


def _keep(cond, x):
    return _each(lambda v: jnp.where(cond, v, 0.0), x)


def _rowsum(x):
    return _each(lambda v: jnp.sum(v, axis=1, keepdims=True), x)


ATT_HEADS = 2
ATT_WIDTH = ATT_HEADS * HEAD_DIM
_HEAD_COLS = [slice(h * HEAD_DIM, (h + 1) * HEAD_DIM) for h in range(ATT_HEADS)]


def _att_specs(n_heads, s):
    groups = n_heads // ATT_HEADS
    q_spec = pl.BlockSpec((ATT_TQ, ATT_WIDTH), lambda g, i: (i, g))
    k_spec = pl.BlockSpec((s, ATT_WIDTH), lambda g, i: (0, groups + g))
    v_spec = pl.BlockSpec((s, ATT_WIDTH), lambda g, i: (0, 2 * groups + g))
    return q_spec, k_spec, v_spec


def _heads_of(ref, rows=None):
    return _Each(ref[:, cs] if rows is None else ref[rows, cs] for cs in _HEAD_COLS)


def _dot_each(a, b, kind):
    return _each(lambda x, y: _dot(x, y, kind), a, b)


def _att_iotas():
    row = lax.broadcasted_iota(jnp.int32, (ATT_TQ, ATT_TK), 0)
    col = lax.broadcasted_iota(jnp.int32, (ATT_TQ, ATT_TK), 1)
    jr = lax.broadcasted_iota(jnp.int32, (ATT_TK, ATT_TK), 0)
    jc = lax.broadcasted_iota(jnp.int32, (ATT_TK, ATT_TK), 1)
    return row, col, jr, jc


def _sb_fwd(qkv, n_heads, name):
    s = qkv.shape[0]

    def body(q_ref, k_ref, v_ref, o16_ref, o32_ref):
        i = pl.program_id(1)
        q = _heads_of(q_ref)
        row, col, jr, jc = _att_iotas()
        later = (jr > jc).astype(BF16)

        def step(jb, carry, diagonal):
            c_sp, acc = (_Each(part) for part in carry)
            work = []
            for sub in reversed(range(ATT_SUB)):
                keys = pl.ds(pl.multiple_of(jb * ATT_TQ + sub * ATT_TK, ATT_TK), ATT_TK)
                z = _dot_each(q, _heads_of(k_ref, keys), "nt") * ATT_SCALE
                sp = _each(_softplus, z)
                before = (col + sub * ATT_TK) < row if diagonal else None
                spm = _keep(before, sp) if diagonal else sp
                work.append((keys, z - sp, spm, _each(lambda x: _dot(x, later, "nn"), spm), before))
            for keys, logsig, spm, within, before in work:
                a = _each(jnp.exp, logsig - (c_sp + within))
                if diagonal:
                    a = _keep(before, a)
                acc = acc + _each(_split_dot, a, _heads_of(v_ref, keys))
                c_sp = c_sp + _rowsum(spm)
            return tuple(c_sp.vals), tuple(acc.vals)

        zeros = lambda width: tuple(jnp.zeros((ATT_TQ, width), F32) for _ in range(ATT_HEADS))
        carry = step(i, (zeros(1), zeros(HEAD_DIM)), True)
        _, acc = lax.fori_loop(0, i, lambda it, cr: step(i - 1 - it, cr, False), carry)
        for cs, acc_h in zip(_HEAD_COLS, acc):
            o16_ref[:, cs] = acc_h.astype(BF16)
            o32_ref[:, cs] = acc_h

    q_spec, k_spec, v_spec = _att_specs(n_heads, s)
    o_spec = pl.BlockSpec((ATT_TQ, ATT_WIDTH), lambda g, i: (i, g))
    return pl.pallas_call(
        body, name=name, grid=(n_heads // ATT_HEADS, s // ATT_TQ), in_specs=[q_spec, k_spec, v_spec],
        out_specs=[o_spec, o_spec],
        out_shape=[jax.ShapeDtypeStruct((s, n_heads * HEAD_DIM), BF16),
                   jax.ShapeDtypeStruct((s, n_heads * HEAD_DIM), F32)],
        compiler_params=_cparams("parallel", "arbitrary"),
    )(qkv, qkv, qkv)


def _sb_bwd(qkv, o32, do, n_heads, name):
    s = qkv.shape[0]

    def body(q_ref, k_ref, v_ref, o_ref, do_ref, dq_ref, dk_ref, dv_ref):
        i = pl.program_id(1)

        @pl.when(i == 0)
        def _():
            dk_ref[...] = jnp.zeros_like(dk_ref)
            dv_ref[...] = jnp.zeros_like(dv_ref)

        q, do = _heads_of(q_ref), _heads_of(do_ref)
        total = _rowsum(_each(lambda a, b: a.astype(F32) * b, do, _heads_of(o_ref)))
        row, col, jr, jc = _att_iotas()
        later = (jr > jc).astype(BF16)
        not_before = (jr >= jc).astype(BF16)

        def step(jb, carry, diagonal):
            c_sp, c_e, dq = (_Each(part) for part in carry)
            work = []
            for sub in reversed(range(ATT_SUB)):
                keys = pl.ds(pl.multiple_of(jb * ATT_TQ + sub * ATT_TK, ATT_TK), ATT_TK)
                k = _heads_of(k_ref, keys)
                z = _dot_each(q, k, "nt") * ATT_SCALE
                sp = _each(_softplus, z)
                before = (col + sub * ATT_TK) < row if diagonal else None
                spm = _keep(before, sp) if diagonal else sp
                work.append((keys, k, _each(jnp.exp, z - sp), spm, _each(lambda x: _dot(x, later, "nn"), spm),
                             _dot_each(do, _heads_of(v_ref, keys), "nt"), before))
            for keys, k, sig, spm, within, da, before in work:
                a = sig * _each(lambda x: jnp.exp(-x), c_sp + within)
                if diagonal:
                    a = _keep(before, a)
                e = a * da
                left = total - c_e - _each(lambda x: _split_dot(x, not_before), e)
                dz = (e - (e + left) * sig) * ATT_SCALE
                if diagonal:
                    dz = _keep(before, dz)
                dk, dv = _dot_each(dz, q, "tn"), _dot_each(a, do, "tn")
                for cs, dk_h, dv_h in zip(_HEAD_COLS, dk.vals, dv.vals):
                    dk_ref[keys, cs] += dk_h
                    dv_ref[keys, cs] += dv_h
                dq = dq + _dot_each(dz, k, "nn")
                c_sp = c_sp + _rowsum(spm)
                c_e = c_e + _rowsum(e)
            return tuple(c_sp.vals), tuple(c_e.vals), tuple(dq.vals)

        zeros = lambda width: tuple(jnp.zeros((ATT_TQ, width), F32) for _ in range(ATT_HEADS))
        carry = step(i, (zeros(1), zeros(1), zeros(HEAD_DIM)), True)
        _, _, dq = lax.fori_loop(0, i, lambda it, cr: step(i - 1 - it, cr, False), carry)
        for cs, dq_h in zip(_HEAD_COLS, dq):
            dq_ref[:, cs] = dq_h.astype(BF16)

    q_spec, k_spec, v_spec = _att_specs(n_heads, s)
    blk = pl.BlockSpec((ATT_TQ, ATT_WIDTH), lambda g, i: (i, g))
    full = pl.BlockSpec((s, ATT_WIDTH), lambda g, i: (0, g))
    wide = (s, n_heads * HEAD_DIM)
    return pl.pallas_call(
        body, name=name, grid=(n_heads // ATT_HEADS, s // ATT_TQ), in_specs=[q_spec, k_spec, v_spec, blk, blk],
        out_specs=[blk, full, full],
        out_shape=[jax.ShapeDtypeStruct(wide, BF16), jax.ShapeDtypeStruct(wide, F32), jax.ShapeDtypeStruct(wide, F32)],
        compiler_params=_cparams("parallel", "arbitrary"),
    )(qkv, qkv, qkv, o32, do)


def _fox_logits(q, k, cq, ct_ref, keys):
    ck = _Each(ct_ref[h, :, keys] for h in range(ATT_HEADS))
    return _dot_each(q, k, "nt") * ATT_SCALE + (cq - ck)


def _fox_cq(c_ref, group):
    c = c_ref[...]
    return _Each(_lane_col(c, LANE_FORGET + group * ATT_HEADS + h) for h in range(ATT_HEADS))


def _fox_fwd(qkv, c, ct, name):
    s = qkv.shape[0]
    n_heads = N_FOX_HEADS

    def body(q_ref, k_ref, v_ref, c_ref, ct_ref, o_ref, lse_ref):
        g, i = pl.program_id(0), pl.program_id(1)
        q = _heads_of(q_ref)
        cq = _fox_cq(c_ref, g)
        row, col, _, _ = _att_iotas()

        def step(jb, carry, diagonal):
            m, l, acc = (_Each(part) for part in carry)
            work = []
            m_new = m
            for sub in range(ATT_SUB):
                keys = pl.ds(pl.multiple_of(jb * ATT_TQ + sub * ATT_TK, ATT_TK), ATT_TK)
                sc = _fox_logits(q, _heads_of(k_ref, keys), cq, ct_ref, keys)
                valid = (col + sub * ATT_TK) <= row if diagonal else None
                if diagonal:
                    sc = _each(lambda x: jnp.where(valid, x, -1e30), sc)
                m_new = _each(lambda a, x: jnp.maximum(a, jnp.max(x, axis=1, keepdims=True)), m_new, sc)
                work.append((keys, sc, valid))
            w = _each(jnp.exp, m - m_new)
            l, acc = l * w, acc * w
            for keys, sc, valid in work:
                p = _each(jnp.exp, sc - m_new)
                if diagonal:
                    p = _keep(valid, p)
                l = l + _rowsum(p)
                acc = acc + _each(_split_dot, p, _heads_of(v_ref, keys))
            return tuple(m_new.vals), tuple(l.vals), tuple(acc.vals)

        per_head = lambda width, value: tuple(jnp.full((ATT_TQ, width), value, F32) for _ in range(ATT_HEADS))
        init = (per_head(1, -1e30), per_head(1, 0.0), per_head(HEAD_DIM, 0.0))
        m, l, acc = lax.fori_loop(0, i, lambda jb, cr: step(jb, cr, False), step(i, init, True))
        for h, cs in enumerate(_HEAD_COLS):
            o_ref[:, cs] = acc[h] / l[h]
            lse_ref[h] = jnp.broadcast_to(m[h] + jnp.log(l[h]), (ATT_TQ, LANES))

    q_spec, k_spec, v_spec = _att_specs(n_heads, s)
    return pl.pallas_call(
        body, name=name, grid=(n_heads // ATT_HEADS, s // ATT_TQ),
        in_specs=[q_spec, k_spec, v_spec, pl.BlockSpec((ATT_TQ, LANES), lambda g, i: (i, 0)),
                  pl.BlockSpec((ATT_HEADS, 1, s), lambda g, i: (g, 0, 0))],
        out_specs=[pl.BlockSpec((ATT_TQ, ATT_WIDTH), lambda g, i: (i, g)),
                   pl.BlockSpec((ATT_HEADS, ATT_TQ, LANES), lambda g, i: (g, i, 0))],
        out_shape=[jax.ShapeDtypeStruct((s, n_heads * HEAD_DIM), F32),
                   jax.ShapeDtypeStruct((n_heads, s, LANES), F32)],
        compiler_params=_cparams("parallel", "arbitrary"),
    )(qkv, qkv, qkv, c, ct)


def _fox_bwd(qkv, c, ct, o, lse, do, name):
    s = qkv.shape[0]
    n_heads = N_FOX_HEADS

    def body(q_ref, k_ref, v_ref, c_ref, ct_ref, o_ref, lse_ref, do_ref, dq_ref, dk_ref, dv_ref, dct_ref):
        g, i = pl.program_id(0), pl.program_id(1)

        @pl.when(i == 0)
        def _():
            dk_ref[...] = jnp.zeros_like(dk_ref)
            dv_ref[...] = jnp.zeros_like(dv_ref)
            dct_ref[...] = jnp.zeros_like(dct_ref)

        q = _heads_of(q_ref)
        do16 = _each(lambda x: x.astype(BF16), _heads_of(do_ref))
        delta = _rowsum(_each(lambda a, b: a.astype(F32) * b, do16, _heads_of(o_ref)))
        lse_col = _Each(lse_ref[h, :, 0:1] for h in range(ATT_HEADS))
        cq = _fox_cq(c_ref, g)
        row, col, _, _ = _att_iotas()

        def step(jb, dq, diagonal):
            dq = _Each(dq)
            for sub in range(ATT_SUB):
                keys = pl.ds(pl.multiple_of(jb * ATT_TQ + sub * ATT_TK, ATT_TK), ATT_TK)
                k = _heads_of(k_ref, keys)
                sc = _fox_logits(q, k, cq, ct_ref, keys)
                if diagonal:
                    valid = (col + sub * ATT_TK) <= row
                    p = _keep(valid, _each(jnp.exp, _keep(valid, sc) - lse_col))
                else:
                    p = _each(jnp.exp, sc - lse_col)
                ds = p * (_dot_each(do16, _heads_of(v_ref, keys), "nt") - delta)
                dss = ds * ATT_SCALE
                dk, dv = _dot_each(dss, q, "tn"), _dot_each(p, do16, "tn")
                for h, cs in enumerate(_HEAD_COLS):
                    dct_ref[h, :, keys] -= jnp.sum(ds.vals[h], axis=0, keepdims=True)
                    dk_ref[keys, cs] += dk.vals[h]
                    dv_ref[keys, cs] += dv.vals[h]
                dq = dq + _dot_each(dss, k, "nn")
            return tuple(dq.vals)

        dq0 = step(i, tuple(jnp.zeros((ATT_TQ, HEAD_DIM), F32) for _ in range(ATT_HEADS)), True)
        dq = lax.fori_loop(0, i, lambda jb, dq: step(jb, dq, False), dq0)
        for cs, dq_h in zip(_HEAD_COLS, dq):
            dq_ref[:, cs] = dq_h

    q_spec, k_spec, v_spec = _att_specs(n_heads, s)
    blk = pl.BlockSpec((ATT_TQ, ATT_WIDTH), lambda g, i: (i, g))
    full = pl.BlockSpec((s, ATT_WIDTH), lambda g, i: (0, g))
    wide = jax.ShapeDtypeStruct((s, n_heads * HEAD_DIM), F32)
    return pl.pallas_call(
        body, name=name, grid=(n_heads // ATT_HEADS, s // ATT_TQ),
        in_specs=[q_spec, k_spec, v_spec, pl.BlockSpec((ATT_TQ, LANES), lambda g, i: (i, 0)),
                  pl.BlockSpec((ATT_HEADS, 1, s), lambda g, i: (g, 0, 0)), blk,
                  pl.BlockSpec((ATT_HEADS, ATT_TQ, LANES), lambda g, i: (g, i, 0)), blk],
        out_specs=[blk, full, full, pl.BlockSpec((ATT_HEADS, 1, s), lambda g, i: (g, 0, 0))],
        out_shape=[wide, wide, wide, jax.ShapeDtypeStruct((n_heads, 1, s), F32)],
        compiler_params=_cparams("parallel", "arbitrary"),
    )(qkv, qkv, qkv, c, ct, o, lse, do)


def _cumsum_rows(x, reverse, name):
    s = x.shape[0]
    nb = s // LANES

    def body(x_ref, o_ref):
        r = lax.broadcasted_iota(jnp.int32, (LANES, LANES), 0)
        c = lax.broadcasted_iota(jnp.int32, (LANES, LANES), 1)
        tri = ((r <= c) if reverse else (r >= c)).astype(F32)

        def step(it, carry):
            b = (nb - 1 - it) if reverse else it
            off = pl.multiple_of(b * LANES, LANES)
            blk = x_ref[pl.ds(off, LANES), :]
            o_ref[pl.ds(off, LANES), :] = _dot32(tri, blk) + carry
            return carry + jnp.sum(blk, axis=0, keepdims=True)

        lax.fori_loop(0, nb, step, jnp.zeros((1, LANES), F32))

    return pl.pallas_call(body, name=name, out_shape=jax.ShapeDtypeStruct(x.shape, F32),
                          compiler_params=pltpu.CompilerParams(vmem_limit_bytes=V7X_VMEM_LIMIT))(x)


def _dot32_each(a, b, kind="nn"):
    return _each(lambda x, y: _dot32(x, y, kind), a, b)


def _unit_lower_inverse(m, ri, ci):
    c = ri.shape[0]
    t = -_keep(ri // 2 == ci // 2, m) + jnp.where(ri == ci, 1.0, 0.0)
    b = 4
    while b <= c:
        off_diag = (ri // b == ci // b) & (ri % b >= b // 2) & (ci % b < b // 2)
        t = t - _dot32_each(_dot32_each(t, _keep(off_diag, m)), t)
        b *= 2
    return t


def _dn_gates(g, ri, ci):
    eye = ri == ci
    incl = ri >= ci
    g_row = jnp.sum(jnp.where(eye, g, 0.0), axis=0, keepdims=True)
    gc = jnp.sum(jnp.where(incl, g_row, 0.0), axis=1, keepdims=True)
    gc_row = jnp.sum(jnp.where(eye, gc, 0.0), axis=0, keepdims=True)
    dmat = jnp.where(incl, jnp.exp(jnp.where(incl, gc - gc_row, 0.0)), 0.0)
    gc_last = jnp.sum(g, axis=0, keepdims=True)
    return gc, dmat, jnp.exp(gc), jnp.exp(gc_last - gc), jnp.exp(gc_last)


def _dn_fwd(qkv, act, name):
    s = qkv.shape[0]
    c, d, nh = DN_CHUNK, HEAD_DIM, N_DN_HEADS
    nc = s // c

    def body(q_ref, k_ref, v_ref, act_ref, o_ref, s_ref, t_ref, state):
        @pl.when(pl.program_id(0) == 0)
        def _():
            state[...] = jnp.zeros_like(state)

        ri = lax.broadcasted_iota(jnp.int32, (c, c), 0)
        ci = lax.broadcasted_iota(jnp.int32, (c, c), 1)
        act = act_ref[...]
        heads = range(nh)
        cols = [slice(h * d, (h + 1) * d) for h in heads]
        q, k, v = (_Each(ref[:, cs] for cs in cols) for ref in (q_ref, k_ref, v_ref))
        beta = _Each(_lane_col(act, LANE_BETA + h) for h in heads)
        g = _Each(_lane_col(act, LANE_DECAY + h) for h in heads)
        _, dmat, e, r, gl = _each(lambda gh: _dn_gates(gh, ri, ci), g)
        s0 = _Each(state[h] for h in heads)
        kb = beta * k
        t = _unit_lower_inverse(_keep(ri > ci, _dot32_each(kb, k, "nt") * dmat), ri, ci)
        vn = _dot32_each(t, beta * v) - _dot32_each(_dot32_each(t, kb * e), s0)
        o = _dot32_each(q * e, s0) + _dot32_each(_dot32_each(q, k, "nt") * dmat, vn)
        s1 = s0 * gl + _dot32_each(k * r, vn, "tn")
        for h in heads:
            o_ref[:, cols[h]] = o.vals[h]
            state[h] = s1.vals[h]
            s_ref[h] = s0.vals[h]
            t_ref[h] = t.vals[h]

    wide = lambda part: pl.BlockSpec((c, nh * d), lambda n: (n, part))
    return pl.pallas_call(
        body, name=name, grid=(nc,),
        in_specs=[wide(0), wide(1), wide(2), pl.BlockSpec((c, LANES), lambda n: (n, 0))],
        out_specs=[wide(0), pl.BlockSpec((nh, None, d, d), lambda n: (0, n, 0, 0)),
                   pl.BlockSpec((nh, None, c, c), lambda n: (0, n, 0, 0))],
        out_shape=[jax.ShapeDtypeStruct((s, nh * d), F32), jax.ShapeDtypeStruct((nh, nc, d, d), F32),
                   jax.ShapeDtypeStruct((nh, nc, c, c), F32)],
        scratch_shapes=[pltpu.VMEM((nh, d, d), F32)],
        compiler_params=_cparams("arbitrary"),
    )(qkv, qkv, qkv, act)


def _dn_bwd(qkv, act, states, tinv, do, name):
    s = qkv.shape[0]
    c, d, nh = DN_CHUNK, HEAD_DIM, N_DN_HEADS
    nc = s // c

    def chunk_bwd(q, k, v, do, beta, g, s0, t, ds_out):
        ri = lax.broadcasted_iota(jnp.int32, (c, c), 0)
        ci = lax.broadcasted_iota(jnp.int32, (c, c), 1)
        eye, incl, strict = ri == ci, ri >= ci, ri > ci
        gc, dmat, e, r, gl = _each(lambda gh: _dn_gates(gh, ri, ci), g)
        dot = _dot32_each
        rowsum = lambda x: _each(lambda a: jnp.sum(a, axis=1, keepdims=True), x)
        colsum = lambda x: _each(lambda a: jnp.sum(a, axis=0, keepdims=True), x)
        total = lambda x: colsum(rowsum(x))
        to_col = lambda row: rowsum(_keep(eye, row))
        to_row = lambda colv: colsum(_keep(eye, colv))

        kb, vb = beta * k, beta * v
        kbe = kb * e
        u, w = dot(t, vb), dot(t, kbe)
        vn = u - dot(w, s0)
        qk = dot(q, k, "nt")
        p = qk * dmat
        gram = dot(k, k, "nt")
        kr, qe = k * r, q * e

        d_kr = dot(vn, ds_out, "nt")
        dvn = dot(kr, ds_out)
        dgl = total(s0 * ds_out)
        ds_in = ds_out * gl
        dk = d_kr * r
        dr = rowsum(d_kr * k)
        d_qe = dot(do, s0, "nt")
        ds_in = ds_in + dot(qe, do, "tn")
        dp = _keep(incl, dot(do, vn, "nt"))
        dvn = dvn + dot(p, do, "tn")
        dq = d_qe * e
        de = rowsum(d_qe * q)
        dqk = dp * dmat
        dq = dq + dot(dqk, k)
        dk = dk + dot(dqk, q, "tn")
        dd = dp * qk
        dw = -dot(dvn, s0, "nt")
        ds_in = ds_in - dot(w, dvn, "tn")
        dvb = dot(t, dvn, "tn")
        dkbe = dot(t, dw, "tn")
        dm = -_keep(strict, dot(dvb, u, "nt") + dot(dkbe, w, "nt"))
        dbeta = rowsum(dm * gram * dmat)
        dgram = dm * beta * dmat
        dd = dd + dm * beta * gram
        dk = dk + dot(dgram, k) + dot(dgram, k, "tn")
        dkb = dkbe * e
        de = de + rowsum(dkbe * kb)
        dk = dk + beta * dkb
        dbeta = dbeta + rowsum(dkb * k) + rowsum(dvb * v)
        dv = beta * dvb
        wd = dd * dmat
        dgc = rowsum(wd) - to_col(colsum(wd)) + de * e - dr * r
        dgc_last = total(dr * r) + dgl * gl
        dgc = dgc + _keep(ri[:, 0:1] == c - 1, dgc_last)
        dg = rowsum(_keep(ri <= ci, to_row(dgc)))
        return dq, dk, dv, dbeta, dg, ds_in

    def body(q_ref, k_ref, v_ref, act_ref, s_ref, t_ref, do_ref, dq_ref, dk_ref, dv_ref, dact_ref, dstate):
        @pl.when(pl.program_id(0) == 0)
        def _():
            dstate[...] = jnp.zeros_like(dstate)

        act = act_ref[...]
        heads = range(nh)
        cols = [slice(h * d, (h + 1) * d) for h in heads]
        q, k, v, do = (_Each(ref[:, cs] for cs in cols) for ref in (q_ref, k_ref, v_ref, do_ref))
        dq, dk, dv, dbeta, dg, ds_in = chunk_bwd(
            q, k, v, do, _Each(_lane_col(act, LANE_BETA + h) for h in heads),
            _Each(_lane_col(act, LANE_DECAY + h) for h in heads), _Each(s_ref[h] for h in heads),
            _Each(t_ref[h] for h in heads), _Each(dstate[h] for h in heads))
        lane = lax.broadcasted_iota(jnp.int32, (c, LANES), 1)
        dact = jnp.zeros((c, LANES), F32)
        for h in heads:
            dstate[h] = ds_in.vals[h]
            dq_ref[:, cols[h]], dk_ref[:, cols[h]], dv_ref[:, cols[h]] = dq.vals[h], dk.vals[h], dv.vals[h]
            dact = (dact + jnp.where(lane == LANE_BETA + h, dbeta.vals[h], 0.0)
                    + jnp.where(lane == LANE_DECAY + h, dg.vals[h], 0.0))
        dact_ref[...] = dact

    part = lambda p: pl.BlockSpec((c, nh * d), lambda n: (nc - 1 - n, p))
    per = lambda a, b: pl.BlockSpec((nh, None, a, b), lambda n: (0, nc - 1 - n, 0, 0))
    wide = jax.ShapeDtypeStruct((s, nh * d), F32)
    act_spec = pl.BlockSpec((c, LANES), lambda n: (nc - 1 - n, 0))
    return pl.pallas_call(
        body, name=name, grid=(nc,),
        in_specs=[part(0), part(1), part(2), act_spec, per(d, d), per(c, c), part(0)],
        out_specs=[part(0), part(0), part(0), act_spec],
        out_shape=[wide, wide, wide, jax.ShapeDtypeStruct((s, LANES), F32)],
        scratch_shapes=[pltpu.VMEM((nh, d, d), F32)],
        compiler_params=_cparams("arbitrary"),
    )(qkv, qkv, qkv, act, states, tinv, do)


EVEN_DN_QKV, EVEN_FOX_QKV, EVEN_DN_GATE, EVEN_FOX_GATE, EVEN_NARROW = 0, 1536, 3072, 3584, 4096
EVEN_WIDTH = 4224
CONV_TILE = 256
CONV_HALO = 8


def _conv_fwd(proj, w, name):
    s = proj.shape[0]
    t, cw = CONV_TILE, 3 * D_DN

    def body(cur_ref, prev_ref, w_ref, y_ref, xs):
        i = pl.program_id(0)
        xs[0:CONV_HALO, :] = jnp.where(i > 0, prev_ref[...], 0.0)
        xs[CONV_HALO:, :] = cur_ref[...]
        y = jnp.zeros((t, cw), F32)
        for tap in range(CONV_WIDTH):
            y = y + w_ref[tap:tap + 1, :] * xs[pl.ds(CONV_HALO - CONV_WIDTH + 1 + tap, t), :]
        y_ref[...] = y

    per = t // CONV_HALO
    return pl.pallas_call(
        body, name=name, grid=(s // t,),
        in_specs=[pl.BlockSpec((t, cw), lambda i: (i, 0)),
                  pl.BlockSpec((CONV_HALO, cw), lambda i: (jnp.maximum(i * per - 1, 0), 0)),
                  pl.BlockSpec((CONV_WIDTH, cw), lambda i: (0, 0))],
        out_specs=pl.BlockSpec((t, cw), lambda i: (i, 0)),
        out_shape=jax.ShapeDtypeStruct((s, cw), F32),
        scratch_shapes=[pltpu.VMEM((t + CONV_HALO, cw), F32)],
        compiler_params=_cparams("parallel"),
    )(proj, proj, w)


def _conv_bwd(proj, w, dy, name):
    s = proj.shape[0]
    t, cw = CONV_TILE, 3 * D_DN
    nt = s // t

    def body(cur_ref, prev_ref, w_ref, dy_ref, nxt_ref, dx_ref, dw_ref, xs, dys):
        i = pl.program_id(0)

        @pl.when(i == 0)
        def _():
            dw_ref[...] = jnp.zeros_like(dw_ref)

        xs[0:CONV_HALO, :] = jnp.where(i > 0, prev_ref[...], 0.0)
        xs[CONV_HALO:, :] = cur_ref[...]
        dys[0:t, :] = dy_ref[...]
        dys[t:, :] = jnp.where(i < nt - 1, nxt_ref[...], 0.0)
        dy = dy_ref[...]
        dx = jnp.zeros((t, cw), F32)
        for tap in range(CONV_WIDTH):
            dx = dx + w_ref[tap:tap + 1, :] * dys[pl.ds(CONV_WIDTH - 1 - tap, t), :]
            dw_ref[tap:tap + 1, :] += jnp.sum(dy * xs[pl.ds(CONV_HALO - CONV_WIDTH + 1 + tap, t), :], axis=0,
                                              keepdims=True)
        dx_ref[...] = dx.astype(BF16)

    per = t // CONV_HALO
    last = s // CONV_HALO - 1
    return pl.pallas_call(
        body, name=name, grid=(nt,),
        in_specs=[pl.BlockSpec((t, cw), lambda i: (i, 0)),
                  pl.BlockSpec((CONV_HALO, cw), lambda i: (jnp.maximum(i * per - 1, 0), 0)),
                  pl.BlockSpec((CONV_WIDTH, cw), lambda i: (0, 0)),
                  pl.BlockSpec((t, cw), lambda i: (i, 0)),
                  pl.BlockSpec((CONV_HALO, cw), lambda i: (jnp.minimum((i + 1) * per, last), 0))],
        out_specs=[pl.BlockSpec((t, cw), lambda i: (i, 0)), pl.BlockSpec((CONV_WIDTH, cw), lambda i: (0, 0))],
        out_shape=[jax.ShapeDtypeStruct((s, cw), BF16), jax.ShapeDtypeStruct((CONV_WIDTH, cw), F32)],
        scratch_shapes=[pltpu.VMEM((t + CONV_HALO, cw), F32), pltpu.VMEM((t + CONV_HALO, cw), F32)],
        compiler_params=_cparams("arbitrary"),
    )(proj, proj, w, dy, dy)


def _heads(x, n):
    return [x[:, HEAD_DIM * h:HEAD_DIM * (h + 1)] for h in range(n)]


def _dn_pre_fwd(y, name):
    def fn(yb):
        cs = yb * _sigmoid(yb)
        out = []
        for idx, xh in enumerate(_heads(cs, 3 * N_DN_HEADS)):
            if idx < 2 * N_DN_HEADS:
                xh = xh * lax.rsqrt(jnp.sum(xh * xh, axis=-1, keepdims=True) + EPS)
                if idx < N_DN_HEADS:
                    xh = xh * ATT_SCALE
            out.append(xh)
        return (jnp.concatenate(out, axis=1),)
    return _rowwise(fn, [y], [], [(y.shape[1], F32)], [], tile=256, name=name)[0]


def _dn_pre_bwd(y, dq, dk, dv, name):
    def fn(yb, dqb, dkb, dvb):
        sg = _sigmoid(yb)
        cs = yb * sg
        dout = _heads(dqb, N_DN_HEADS) + _heads(dkb, N_DN_HEADS) + _heads(dvb, N_DN_HEADS)
        dcs = []
        for idx, (xh, dh) in enumerate(zip(_heads(cs, 3 * N_DN_HEADS), dout)):
            if idx < 2 * N_DN_HEADS:
                if idx < N_DN_HEADS:
                    dh = dh * ATT_SCALE
                r = lax.rsqrt(jnp.sum(xh * xh, axis=-1, keepdims=True) + EPS)
                xhat = xh * r
                dh = r * (dh - xhat * jnp.sum(xhat * dh, axis=-1, keepdims=True))
            dcs.append(dh)
        return (jnp.concatenate(dcs, axis=1) * _silu_grad(yb, sg),)
    return _rowwise(fn, [y, dq, dk, dv], [], [(y.shape[1], F32)], [], tile=256, name=name)[0]


def _narrow_params(a_log, dt_bias, f_bias):
    lanes = lambda a, first: jnp.pad(a.reshape(1, -1), ((0, 0), (first, LANES - first - a.shape[0])))
    return jnp.concatenate([lanes(a_log, LANE_DECAY), lanes(dt_bias, LANE_DECAY), lanes(f_bias, LANE_FORGET),
                            jnp.zeros((5, LANES), F32)], axis=0)


def _narrow_masks(shape):
    lane = lax.broadcasted_iota(jnp.int32, shape, 1)
    is_beta = lane < LANE_DECAY
    is_decay = (lane >= LANE_DECAY) & (lane < LANE_FORGET)
    is_forget = (lane >= LANE_FORGET) & (lane < LANE_FORGET + N_FOX_HEADS)
    return is_beta, is_decay, is_forget


def _narrow_fwd(proj, params, name):
    def fn(sm, pk):
        is_beta, is_decay, is_forget = _narrow_masks(sm.shape)
        g = -jnp.exp(pk[0:1, :]) * _softplus(sm + pk[1:2, :])
        logf = -_softplus(-(sm + pk[2:3, :]))
        return (jnp.where(is_beta, _sigmoid(sm), jnp.where(is_decay, g, jnp.where(is_forget, logf, 0.0))),)
    return _rowwise(fn, [(proj, LANES, EVEN_NARROW // LANES)], [params], [(LANES, F32)], [], tile=512, name=name)[0]


def _narrow_bwd(proj, params, act, dact, dlogf, name):
    def fn(sm, ab, da, dl, pk):
        is_beta, is_decay, is_forget = _narrow_masks(sm.shape)
        db = jnp.where(is_forget, dl, da)
        d_beta = db * ab * (1.0 - ab)
        d_decay = db * (-jnp.exp(pk[0:1, :])) * _sigmoid(sm + pk[1:2, :])
        d_forget = db * _sigmoid(-(sm + pk[2:3, :]))
        dsm = jnp.where(is_beta, d_beta, jnp.where(is_decay, d_decay, jnp.where(is_forget, d_forget, 0.0)))
        col = lambda x: jnp.sum(x, axis=0, keepdims=True)
        return (dsm, col(jnp.where(is_decay, db * ab, 0.0)), col(jnp.where(is_decay, dsm, 0.0)),
                col(jnp.where(is_forget, dsm, 0.0)))
    return _rowwise(fn, [(proj, LANES, EVEN_NARROW // LANES), act, dact, dlogf], [params], [(LANES, BF16)],
                    [(1, LANES)] * 3, tile=512, name=name)


def _head_rms(xh):
    r = lax.rsqrt(jnp.mean(xh * xh, axis=-1, keepdims=True) + EPS)
    return xh * r, r


def _fox_pre_fwd(proj, qg, kg, name):
    def fn(pf, qgb, kgb):
        out = []
        for idx, xh in enumerate(_heads(pf, 3 * N_FOX_HEADS)):
            if idx < 2 * N_FOX_HEADS:
                xh = _head_rms(xh)[0] * (qgb if idx < N_FOX_HEADS else kgb)
            out.append(xh)
        return (jnp.concatenate(out, axis=1),)
    return _rowwise(fn, [(proj, 3 * D_FOX, EVEN_FOX_QKV // (3 * D_FOX))], [qg, kg], [(3 * D_FOX, BF16)], [],
                    tile=256, name=name)[0]


def _fox_pre_bwd(proj, qg, kg, dq, dk, dv, name):
    def fn(pf, dqb, dkb, dvb, qgb, kgb):
        dout = _heads(dqb, N_FOX_HEADS) + _heads(dkb, N_FOX_HEADS) + _heads(dvb, N_FOX_HEADS)
        dg = [jnp.zeros((1, HEAD_DIM), F32), jnp.zeros((1, HEAD_DIM), F32)]
        dx = []
        for idx, (xh, dh) in enumerate(zip(_heads(pf, 3 * N_FOX_HEADS), dout)):
            if idx < 2 * N_FOX_HEADS:
                which = 0 if idx < N_FOX_HEADS else 1
                xhat, r = _head_rms(xh)
                dg[which] = dg[which] + jnp.sum(dh * xhat, axis=0, keepdims=True)
                dxh = dh * (qgb if which == 0 else kgb)
                dh = r * (dxh - xhat * jnp.mean(dxh * xhat, axis=-1, keepdims=True))
            dx.append(dh)
        return jnp.concatenate(dx, axis=1), dg[0], dg[1]
    return _rowwise(fn, [(proj, 3 * D_FOX, EVEN_FOX_QKV // (3 * D_FOX)), dq, dk, dv], [qg, kg],
                    [(3 * D_FOX, BF16)], [(1, HEAD_DIM)] * 2, tile=256, name=name)


def _mix_gate_fwd(proj, o_dn, o_fox, ng, name):
    def fn(gd, gf, od, of, ngb):
        dn = [_head_rms(xh)[0] * ngb for xh in _heads(od, N_DN_HEADS)]
        return (jnp.concatenate([jnp.concatenate(dn, axis=1) * gd * _sigmoid(gd), of * _sigmoid(gf)], axis=1),)
    return _rowwise(fn, [(proj, D_DN, EVEN_DN_GATE // D_DN), (proj, D_FOX, EVEN_FOX_GATE // D_FOX), o_dn, o_fox],
                    [ng], [(D_DN + D_FOX, BF16)], [], tile=256, name=name)[0]


def _mix_gate_bwd(proj, o_dn, o_fox, ng, dom, name):
    def fn(gd, gf, od, of, dm, ngb):
        d_dn, d_fox = dm[:, :D_DN], dm[:, D_DN:]
        sgd, sgf = _sigmoid(gd), _sigmoid(gf)
        don = d_dn * gd * sgd
        dng = jnp.zeros((1, HEAD_DIM), F32)
        dod, normed = [], []
        for xh, dh in zip(_heads(od, N_DN_HEADS), _heads(don, N_DN_HEADS)):
            xhat, r = _head_rms(xh)
            dng = dng + jnp.sum(dh * xhat, axis=0, keepdims=True)
            dxh = dh * ngb
            dod.append(r * (dxh - xhat * jnp.mean(dxh * xhat, axis=-1, keepdims=True)))
            normed.append(xhat * ngb)
        d_gd = d_dn * jnp.concatenate(normed, axis=1) * _silu_grad(gd, sgd)
        d_gf = d_fox * of * sgf * (1.0 - sgf)
        return jnp.concatenate(dod, axis=1), d_fox * sgf, d_gd, d_gf, dng
    return _rowwise(fn, [(proj, D_DN, EVEN_DN_GATE // D_DN), (proj, D_FOX, EVEN_FOX_GATE // D_FOX), o_dn, o_fox, dom],
                    [ng], [(D_DN, F32), (D_FOX, F32), (D_DN, BF16), (D_FOX, BF16)], [(1, HEAD_DIM)], tile=256,
                    name=name)


def _loss_grad(y, target, name):
    d = y.shape[1]

    def fn(yb, tb):
        diff = yb - tb
        part = jnp.sum(jnp.sum(diff * diff, axis=1, keepdims=True), axis=0, keepdims=True) * (0.5 / d)
        g = diff * (1.0 / d)
        return g, g, part
    return _rowwise(fn, [y, target], [], [(d, F32), (d, BF16)], [(1, 1)], tile=512, name=name)


_REF_EVEN = {"dn_qkv": (0, 1536), "dn_gate": (1536, 2048), "dn_ba": (2048, 2056), "fox_qkv": (2056, 3592),
             "fox_gate": (3592, 4104), "f_pre": (4104, 4108)}
D_IN_EVEN = 4108


def _even_to_kernel_layout(w):
    cut = lambda name: w[..., _REF_EVEN[name][0]:_REF_EVEN[name][1]]
    pad = jnp.zeros(w.shape[:-1] + (EVEN_WIDTH - EVEN_NARROW - 12,), w.dtype)
    return jnp.concatenate([cut("dn_qkv"), cut("fox_qkv"), cut("dn_gate"), cut("fox_gate"), cut("dn_ba"),
                            cut("f_pre"), pad], axis=-1)


def _even_from_kernel_layout(g):
    return jnp.concatenate([g[..., EVEN_DN_QKV:EVEN_FOX_QKV], g[..., EVEN_DN_GATE:EVEN_FOX_GATE],
                            g[..., EVEN_NARROW:EVEN_NARROW + 8], g[..., EVEN_FOX_QKV:EVEN_DN_GATE],
                            g[..., EVEN_FOX_GATE:EVEN_NARROW], g[..., EVEN_NARROW + 8:EVEN_NARROW + 12]], axis=-1)


EVEN_QUARTER = 1027
EVEN_QUARTER_PAD = 1152


def _even_grad_quarters(g):
    g = _even_from_kernel_layout(g)
    pad = [(0, 0)] * (g.ndim - 1) + [(0, EVEN_QUARTER_PAD - EVEN_QUARTER)]
    return jnp.concatenate([jnp.pad(g[..., q * EVEN_QUARTER:(q + 1) * EVEN_QUARTER], pad) for q in range(4)], axis=-1)


def _forget_rows(c):
    return c[:, LANE_FORGET:LANE_FORGET + N_FOX_HEADS].T.reshape(N_FOX_HEADS, 1, c.shape[0])


def _forget_lanes(rows):
    s = rows.shape[2]
    return jnp.pad(rows.reshape(-1, s).T, ((0, 0), (LANE_FORGET, LANES - LANE_FORGET - N_FOX_HEADS)))


def _even_fwd(x, gain, w_in, w_out, j, p, tag):
    h = _rms_fwd(x, gain, f"{tag}_norm")
    proj = _mm(h, w_in, "nn", tm=512, tn=EVEN_WIDTH // 3, out_dtype=F32, name=f"{tag}_in", b_lead=(j,))
    y = _conv_fwd(proj, p["conv_w"], f"{tag}_conv")
    dn_qkv = _dn_pre_fwd(y, f"{tag}_dn_pre")
    act = _narrow_fwd(proj, p["narrow"], f"{tag}_narrow")
    o_dn, states, tinv = _dn_fwd(dn_qkv, act, f"{tag}_delta")
    fox_qkv = _fox_pre_fwd(proj, p["q_g"], p["k_g"], f"{tag}_fox_pre")
    c = _cumsum_rows(act, False, f"{tag}_cumsum")
    ct = _forget_rows(c)
    o_fox, lse = _fox_fwd(fox_qkv, c, ct, f"{tag}_fox")
    om = _mix_gate_fwd(proj, o_dn, o_fox, p["dn_norm_g"], f"{tag}_gate")
    x2 = _mm(om, w_out, "nn", tm=512, tn=x.shape[1], out_dtype=F32, name=f"{tag}_out", residual=x, b_lead=(j,))
    return x2, (x, h, proj, y, dn_qkv, act, states, tinv, o_dn, fox_qkv, c, ct, o_fox, lse, om)


def _even_bwd(dxo, dxo16, saved, gain, w_in, w_out, j, p, tag, g_in, g_out):
    x, h, proj, y, dn_qkv, act, states, tinv, o_dn, fox_qkv, c, ct, o_fox, lse, om = saved
    d = x.shape[1]
    dom = _mm(dxo16, w_out, "nt", tm=512, tn=d, out_dtype=F32, name=f"{tag}_out_bwd", b_lead=(j,))
    g_out = _mm(om, dxo16, "tn", tm=512, tn=d, out_dtype=F32, name=f"{tag}_out_dw", into=(g_out, 0))
    d_odn, d_ofox, d_gd, d_gf, d_ng = _mix_gate_bwd(proj, o_dn, o_fox, p["dn_norm_g"], dom, f"{tag}_gate_bwd")
    dq, dk, dv, dct = _fox_bwd(fox_qkv, c, ct, o_fox, lse, d_ofox, f"{tag}_fox_bwd")
    d_fox_qkv, d_qg, d_kg = _fox_pre_bwd(proj, p["q_g"], p["k_g"], dq, dk, dv, f"{tag}_fox_pre_bwd")
    dlogf = _cumsum_rows(_forget_lanes(dct), True, f"{tag}_cumsum_bwd")
    dq, dk, dv, dact = _dn_bwd(dn_qkv, act, states, tinv, d_odn, f"{tag}_delta_bwd")
    dy = _dn_pre_bwd(y, dq, dk, dv, f"{tag}_dn_pre_bwd")
    d_dn_qkv, d_conv = _conv_bwd(proj, p["conv_w"], dy, f"{tag}_conv_bwd")
    d_narrow, s_alog, s_dt, s_fb = _narrow_bwd(proj, p["narrow"], act, dact, dlogf, f"{tag}_narrow_bwd")
    dproj = jnp.concatenate([d_dn_qkv, d_fox_qkv, d_gd, d_gf, d_narrow], axis=1)
    g_in = _mm(h, dproj, "tn", tm=512, tn=EVEN_WIDTH // 3, out_dtype=F32, name=f"{tag}_in_dw", into=(g_in, 0))
    dx, dx16, d_gain = _in_proj_bwd(dproj, w_in, j, x, dxo, gain, f"{tag}_in_bwd")
    small = {"conv_w": d_conv, "a_log": s_alog, "dt_bias": s_dt, "f_bias": s_fb, "dn_norm_g": d_ng, "q_g": d_qg,
             "k_g": d_kg}
    return dx, dx16, d_gain, small, g_in, g_out


def _odd_fwd(x, gain, w_in, w_out, j, tag):
    h = _rms_fwd(x, gain, f"{tag}_norm")
    qkv = _mm(h, w_in, "nn", tm=512, tn=w_in.shape[2] // 2, out_dtype=BF16, name=f"{tag}_in", b_lead=(j,))
    o16, o32 = _sb_fwd(qkv, N_SB_HEADS, f"{tag}_sb")
    x2 = _mm(o16, w_out, "nn", tm=512, tn=x.shape[1], out_dtype=F32, name=f"{tag}_out", residual=x, b_lead=(j,))
    return x2, (x, h, qkv, o16, o32)


def _odd_bwd(dxo, dxo16, saved, gain, w_in, w_out, j, tag, g_in, g_out):
    x, h, qkv, o16, o32 = saved
    d = x.shape[1]
    do = _mm(dxo16, w_out, "nt", tm=512, tn=d, out_dtype=BF16, name=f"{tag}_out_bwd", b_lead=(j,))
    g_out = _mm(o16, dxo16, "tn", tm=512, tn=d, out_dtype=F32, name=f"{tag}_out_dw", into=(g_out, 0))
    dq, dk, dv = _sb_bwd(qkv, o32, do, N_SB_HEADS, f"{tag}_sb_bwd")
    dqkv = jnp.concatenate([dq, dk.astype(BF16), dv.astype(BF16)], axis=1)
    g_in = _mm(h, dqkv, "tn", tm=512, tn=w_in.shape[2] // 2, out_dtype=F32, name=f"{tag}_in_dw", into=(g_in, 0))
    dx, dx16, d_gain = _in_proj_bwd(dqkv, w_in, j, x, dxo, gain, f"{tag}_in_bwd")
    return dx, dx16, d_gain, g_in, g_out


def _forward_backward(x, target, w, first, rest_after, token, on_reduced):
    depth = w["norm_ffn1"].shape[0]
    row = lambda a, l: a[l][None]
    rest = {}

    def mats(names, j):
        if j == 0 and names[0] in first:
            return [first[name] for name in names] + [0]
        return [rest[name] for name in names] + [j - (1 if names[0] in first else 0)]

    def even_small(j):
        return {"conv_w": w["dn_conv_w"][j], "narrow": _narrow_params(w["dn_a_log"][j], w["dn_dt_bias"][j],
                                                                     w["fox_f_bias"][j]),
                "dn_norm_g": row(w["dn_norm_g"], j), "q_g": row(w["fox_q_norm_g"], j),
                "k_g": row(w["fox_k_norm_g"], j)}

    saved = []
    for l in range(depth):
        if l == 1:
            rest.update(rest_after(x))
        gain = row(w["norm_ffn1"], l) + token[0:1, 0:1] if l == 0 else row(w["norm_ffn1"], l)
        x, s1 = _ffn_fwd(x, gain, *mats(("ffn1_w_gu", "ffn1_w_down"), l), "ffn1")
        if l % 2 == 0:
            x, s2 = _even_fwd(x, row(w["norm_mix"], l), *mats(("w_in_even", "w_out_even"), l // 2),
                              even_small(l // 2), "even")
        else:
            x, s2 = _odd_fwd(x, row(w["norm_mix"], l), *mats(("w_in_odd", "w_out_odd"), l // 2), "odd")
        x, s3 = _ffn_fwd(x, row(w["norm_ffn2"], l), *mats(("ffn2_w_gu", "ffn2_w_down"), l), "ffn2")
        saved.append((s1, s2, s3))

    dx, dx16, loss = _loss_grad(x, target, "loss")

    kind_of = dict(BIG)
    d_norm = {k: [None] * depth for k in ("norm_ffn1", "norm_mix", "norm_ffn2")}
    d_even = [None] * ((depth + 1) // 2)
    to_sibling, between_chips, token = None, None, None
    for l in reversed(range(depth)):
        s1, s2, s3 = saved[l]
        mixer = ("w_in_even", "w_out_even") if l % 2 == 0 else ("w_in_odd", "w_out_odd")
        names = ["ffn1_w_gu", "ffn1_w_down", *mixer, "ffn2_w_gu", "ffn2_w_down"]
        g = {name: lax.empty((1,) + rest[name].shape[1:], F32) for name in names}
        dx, dx16, d_norm["norm_ffn2"][l], g["ffn2_w_gu"], g["ffn2_w_down"] = _ffn_bwd(
            dx, dx16, s3, row(w["norm_ffn2"], l), *mats(("ffn2_w_gu", "ffn2_w_down"), l), "ffn2", g["ffn2_w_gu"],
            g["ffn2_w_down"], after=token)
        if to_sibling is not None:
            between_chips, _ = _reduce_middle(to_sibling, dx)
        if l % 2 == 0:
            dx, dx16, d_norm["norm_mix"][l], d_even[l // 2], g["w_in_even"], g["w_out_even"] = _even_bwd(
                dx, dx16, s2, row(w["norm_mix"], l), *mats(("w_in_even", "w_out_even"), l // 2), even_small(l // 2),
                "even", g["w_in_even"], g["w_out_even"])
            g["w_in_even"] = _even_grad_quarters(g["w_in_even"])
        else:
            dx, dx16, d_norm["norm_mix"][l], g["w_in_odd"], g["w_out_odd"] = _odd_bwd(
                dx, dx16, s2, row(w["norm_mix"], l), *mats(("w_in_odd", "w_out_odd"), l // 2), "odd", g["w_in_odd"],
                g["w_out_odd"])
        dx, dx16, d_norm["norm_ffn1"][l], g["ffn1_w_gu"], g["ffn1_w_down"] = _ffn_bwd(
            dx, dx16, s1, row(w["norm_ffn1"], l), *mats(("ffn1_w_gu", "ffn1_w_down"), l), "ffn1", g["ffn1_w_gu"],
            g["ffn1_w_down"])
        to_sibling, token = _reduce_start([g[name] for name in names], [kind_of[name] for name in names], names,
                                          f"layer{l}")
        if between_chips is not None:
            on_reduced(l + 1, dict(zip(between_chips[-2], _reduce_finish(between_chips, dx))))
    between_chips, _ = _reduce_middle(to_sibling, dx)
    on_reduced(0, dict(zip(between_chips[-2], _reduce_finish(between_chips, dx))))

    small = {k: jnp.concatenate(v, axis=0) for k, v in d_norm.items()}
    dec = slice(LANE_DECAY, LANE_DECAY + N_DN_HEADS)
    fgt = slice(LANE_FORGET, LANE_FORGET + N_FOX_HEADS)
    small["dn_conv_w"] = jnp.stack([e["conv_w"] for e in d_even])
    small["dn_a_log"] = jnp.concatenate([e["a_log"][:, dec] for e in d_even], axis=0)
    small["dn_dt_bias"] = jnp.concatenate([e["dt_bias"][:, dec] for e in d_even], axis=0)
    small["fox_f_bias"] = jnp.concatenate([e["f_bias"][:, fgt] for e in d_even], axis=0)
    small["dn_norm_g"] = jnp.concatenate([e["dn_norm_g"] for e in d_even], axis=0)
    small["fox_q_norm_g"] = jnp.concatenate([e["q_g"] for e in d_even], axis=0)
    small["fox_k_norm_g"] = jnp.concatenate([e["k_g"] for e in d_even], axis=0)
    return loss, dx, small


MESH = pl.DeviceIdType.MESH
ANY = pl.BlockSpec(memory_space=pl.ANY)


def _place():
    x, y, c = lax.axis_index("x"), lax.axis_index("y"), lax.axis_index("c")
    return x, y, c, [(1 - x, y), (x, 1 - y), (1 - x, 1 - y)]


def _remote(src, dst, send_sem, recv_sem, to):
    return pltpu.make_async_remote_copy(src_ref=src, dst_ref=dst, send_sem=send_sem, recv_sem=recv_sem,
                                        device_id=to, device_id_type=MESH)


def _aligned(start, multiple):
    return start if isinstance(start, int) else pl.multiple_of(start, multiple)


def _quarter(ref, kind, chip, half, rows, cols):
    k = 2 * chip[0] + chip[1]
    hr = rows // 2
    assert hr % 16 == 0 and cols % LANES == 0
    if kind == "col":
        return ref.at[:, pl.ds(_aligned(half * hr, 16), hr), pl.ds(_aligned(k * cols, LANES), cols)]
    return ref.at[:, pl.ds(_aligned(k * rows + half * hr, 16), hr), :]


def _place_quarter(shard, kind, kc, name, first=0, count=None):
    l, rows, cols = shard.shape
    l = l - first if count is None else count
    tr = rows
    while tr * cols * 4 > (2 << 20) and tr % 32 == 0:
        tr //= 2
    nr = rows // tr
    if kind == "col":
        out_spec = pl.BlockSpec((None, tr, cols), lambda li, i, kc_ref: (li, i, kc_ref[0]))
        out_shape = (l, rows, 4 * cols)
    else:
        out_spec = pl.BlockSpec((None, tr, cols), lambda li, i, kc_ref: (li, kc_ref[0] * nr + i, 0))
        out_shape = (l, 4 * rows, cols)

    def body(kc_ref, x_ref, o_ref):
        o_ref[...] = x_ref[...].astype(BF16)

    return pl.pallas_call(
        body, name=name,
        grid_spec=pltpu.PrefetchScalarGridSpec(
            num_scalar_prefetch=1, grid=(l, nr),
            in_specs=[pl.BlockSpec((None, tr, cols), lambda li, i, kc_ref: (li + first, i, 0))],
            out_specs=out_spec),
        out_shape=jax.ShapeDtypeStruct(out_shape, BF16),
        compiler_params=_cparams("parallel", "parallel"),
    )(kc, shard)


def _gather_weights(wholes, kinds):
    n = len(wholes)

    def dims(ref, kind):
        _, r, cc = ref.shape
        return (r, cc // 4) if kind == "col" else (r // 4, cc)

    def body(*refs):
        bufs = refs[n:2 * n]
        send_sems, recv_sems = refs[2 * n:]
        x, y, c, chips = _place()
        sibling = (x, y, 1 - c)
        first, passed = [], []
        for t in range(n):
            rows, cols = dims(bufs[t], kinds[t])
            mine = _quarter(bufs[t], kinds[t], (x, y), c, rows, cols)
            for j, chip in enumerate(chips):
                cp = _remote(mine, mine, send_sems.at[t, j], recv_sems.at[t, j], (*chip, c))
                cp.start()
                first.append(cp)
        for j, chip in enumerate(chips):
            for t in range(n):
                rows, cols = dims(bufs[t], kinds[t])
                got = _quarter(bufs[t], kinds[t], chip, c, rows, cols)
                _remote(got, got, send_sems.at[t, j], recv_sems.at[t, j], (*chip, c)).wait_recv()
                cp = _remote(got, got, send_sems.at[t, 3 + j], recv_sems.at[t, 3 + j], sibling)
                cp.start()
                passed.append(cp)
        for j, chip in enumerate(chips):
            for t in range(n):
                rows, cols = dims(bufs[t], kinds[t])
                got = _quarter(bufs[t], kinds[t], chip, 1 - c, rows, cols)
                _remote(got, got, send_sems.at[t, 3 + j], recv_sems.at[t, 3 + j], sibling).wait_recv()
        for cp in first + passed:
            cp.wait_send()

    return pl.pallas_call(
        body, name="gather_weights", in_specs=[ANY] * n, out_specs=[ANY] * n,
        out_shape=[jax.ShapeDtypeStruct(a.shape, a.dtype) for a in wholes],
        input_output_aliases={t: t for t in range(n)},
        scratch_shapes=[pltpu.SemaphoreType.DMA((n, 6)), pltpu.SemaphoreType.DMA((n, 6))],
        compiler_params=pltpu.CompilerParams(has_side_effects=True),
    )(*wholes)


def _quarter_dims(ref, kind):
    _, r, cc = ref.shape
    return (r, cc // 4) if kind == "col" else (r // 4, cc)


def _gather_chips_copies(bufs, sems, kinds):
    x, y, c, chips = _place()
    copies = []
    for t, buf in enumerate(bufs):
        rows, cols = _quarter_dims(buf, kinds[t])
        mine = _quarter(buf, kinds[t], (x, y), c, rows, cols)
        for j, chip in enumerate(chips):
            pair = 2 * (OTHER_CHIPS * t + j)
            copies.append(_remote(mine, mine, sems[pair], sems[pair + 1], (*chip, c)))
    return copies


def _gather_start(wholes, kinds, after, tag):
    n = len(wholes)
    n_sems = 2 * OTHER_CHIPS * n
    n_in = n + len(after)

    def body(*refs):
        for cp in _gather_chips_copies(refs[:n], refs[n_in + n:n_in + n + n_sems], kinds):
            cp.start()
        refs[-1][...] = jnp.zeros_like(refs[-1])

    held = [pltpu.with_memory_space_constraint(a, pltpu.HBM) for a in wholes]
    out = pl.pallas_call(
        body, name=f"gather_start_{tag}", in_specs=[HBM] * n + [ANY] * len(after),
        out_specs=(*[HBM] * n, *[SEM] * n_sems, pl.BlockSpec(memory_space=pltpu.VMEM)),
        out_shape=(*[pltpu.HBM(a.shape, a.dtype) for a in held], *[pltpu.SemaphoreType.DMA(())] * n_sems,
                   jax.ShapeDtypeStruct((8, LANES), F32)),
        input_output_aliases={i: i for i in range(n)},
        compiler_params=pltpu.CompilerParams(has_side_effects=SPLIT_COPY),
    )(*held, *after)
    return out[n:n + n_sems], out[:n], out[-1]


def _gather_wait(sems, wholes, kinds, after, tag):
    n = len(wholes)

    def body(*refs):
        for cp in _gather_chips_copies(refs[:n], refs[n:n + len(sems)], kinds):
            cp.wait_send()
            cp.wait_recv()

    return pl.pallas_call(
        body, name=f"gather_wait_{tag}", in_specs=[HBM] * n + [SEM] * len(sems) + [ANY],
        out_specs=tuple([HBM] * n), out_shape=tuple(pltpu.HBM(a.shape, a.dtype) for a in wholes),
        input_output_aliases={i: i for i in range(n)},
        compiler_params=pltpu.CompilerParams(has_side_effects=SPLIT_COPY),
    )(*wholes, *sems, after)


def _gather_forward(wholes, kinds, tag):
    n = len(wholes)

    def body(*refs):
        bufs = refs[n:2 * n]
        send_sems, recv_sems = refs[2 * n:]
        x, y, c, chips = _place()
        copies = []
        for t in range(n):
            rows, cols = _quarter_dims(bufs[t], kinds[t])
            for j, chip in enumerate(chips):
                got = _quarter(bufs[t], kinds[t], chip, c, rows, cols)
                cp = _remote(got, got, send_sems.at[t, j], recv_sems.at[t, j], (x, y, 1 - c))
                cp.start()
                copies.append(cp)
        for cp in copies:
            cp.wait_send()
        for t in range(n):
            rows, cols = _quarter_dims(bufs[t], kinds[t])
            for j, chip in enumerate(chips):
                got = _quarter(bufs[t], kinds[t], chip, 1 - c, rows, cols)
                _remote(got, got, send_sems.at[t, j], recv_sems.at[t, j], (x, y, 1 - c)).wait_recv()

    return pl.pallas_call(
        body, name=f"gather_forward_{tag}", in_specs=[ANY] * n, out_specs=[ANY] * n,
        out_shape=[jax.ShapeDtypeStruct(a.shape, a.dtype) for a in wholes],
        input_output_aliases={t: t for t in range(n)},
        scratch_shapes=[pltpu.SemaphoreType.DMA((n, OTHER_CHIPS)), pltpu.SemaphoreType.DMA((n, OTHER_CHIPS))],
        compiler_params=pltpu.CompilerParams(has_side_effects=True),
    )(*wholes)


def _canonical(a, kind):
    l, r, c = a.shape
    return a.reshape(l, 1, r, c) if kind == "col" else a.reshape(l, 4, r // 4, c)


def _add_tile(rows, cols):
    tc = cols if cols <= 1536 else cols // 4
    tr = rows
    while tr * tc * 4 > (1 << 20) and tr % 16 == 0:
        tr //= 2
    return tr, tc


def _rs_add_sibling(part, got, c, name):
    l, a, hr, cols = got.shape
    tr, tc = _add_tile(hr, cols)
    nr = hr // tr

    def body(c_ref, p_ref, g_ref, o32_ref, o16_ref):
        s = p_ref[...] + g_ref[...]
        o32_ref[...] = s
        o16_ref[...] = s.astype(BF16)

    blk = (None, None, tr, tc)
    spec = pl.BlockSpec(blk, lambda li, ai, i, j, c_ref: (li, ai, i, j))
    return pl.pallas_call(
        body, name=name,
        grid_spec=pltpu.PrefetchScalarGridSpec(
            num_scalar_prefetch=1, grid=(l, a, nr, cols // tc),
            in_specs=[pl.BlockSpec(blk, lambda li, ai, i, j, c_ref: (li, ai, c_ref[0] * nr + i, j)), spec],
            out_specs=[spec, spec]),
        out_shape=[jax.ShapeDtypeStruct(got.shape, F32), jax.ShapeDtypeStruct(got.shape, BF16)],
        compiler_params=_cparams("parallel", "parallel", "parallel", "parallel"),
    )(c, part, got)


def _quarter4(ref, kind, chip, cols):
    k = 2 * chip[0] + chip[1]
    if kind == "col":
        return ref.at[:, :, :, pl.ds(pl.multiple_of(k * cols, LANES), cols)]
    return ref.at[:, pl.ds(k, 1), :, :]


HBM = pl.BlockSpec(memory_space=pltpu.HBM)
SEM = pl.BlockSpec(memory_space=pltpu.SEMAPHORE)
SPLIT_COPY = pltpu.SideEffectType.DATAFLOW_SIDE_EFFECTING
OTHER_CHIPS = 3


def _quarter4_shape(a, kind):
    l, _, hr, cols = a.shape
    return (l, 1, hr, cols // 4 if kind == "col" else cols)


def _rs_chips_copies(srcs, lands, sems, kinds):
    x, y, c, chips = _place()
    copies = []
    for t, (src, land) in enumerate(zip(srcs, lands)):
        cols = _quarter4_shape(src, kinds[t])[3]
        for j, chip in enumerate(chips):
            pair = 2 * (OTHER_CHIPS * t + j)
            copies.append(_remote(_quarter4(src, kinds[t], chip, cols), land.at[j], sems[pair], sems[pair + 1],
                                  (*chip, c)))
    return copies


def _split_start(copies, srcs, lands, n_sems, name):
    n = len(srcs)

    def body(*refs):
        for cp in copies(refs[:n], refs[n:2 * n], refs[4 * n:4 * n + n_sems]):
            cp.start()
        refs[-1][...] = jnp.zeros_like(refs[-1])

    held = [pltpu.with_memory_space_constraint(a, pltpu.HBM) for a in (*srcs, *lands)]
    out = pl.pallas_call(
        body, name=name, in_specs=[HBM] * (2 * n),
        out_specs=(*[HBM] * (2 * n), *[SEM] * n_sems, pl.BlockSpec(memory_space=pltpu.VMEM)),
        out_shape=(*[pltpu.HBM(a.shape, a.dtype) for a in held], *[pltpu.SemaphoreType.DMA(())] * n_sems,
                   jax.ShapeDtypeStruct((8, LANES), F32)),
        input_output_aliases={i: i for i in range(2 * n)},
        compiler_params=pltpu.CompilerParams(has_side_effects=SPLIT_COPY),
    )(*held)
    return out[2 * n:2 * n + n_sems], out[:n], out[n:2 * n], out[-1]


def _split_wait(copies, sems, srcs, lands, after, name):
    n = len(srcs)

    def body(*refs):
        for cp in copies(refs[:n], refs[n:2 * n], refs[2 * n:2 * n + len(sems)]):
            cp.wait_send()
            cp.wait_recv()

    out = pl.pallas_call(
        body, name=name, in_specs=[HBM] * (2 * n) + [SEM] * len(sems) + [ANY],
        out_specs=tuple([HBM] * (2 * n)),
        out_shape=tuple(pltpu.HBM(a.shape, a.dtype) for a in (*srcs, *lands)),
        input_output_aliases={i: i for i in range(2 * n)},
        compiler_params=pltpu.CompilerParams(has_side_effects=SPLIT_COPY),
    )(*srcs, *lands, *sems, after)
    return out[n:]


def _rs_sibling_copies(srcs, lands, sems):
    x, y, c, _ = _place()
    copies = []
    for t, (src, land) in enumerate(zip(srcs, lands)):
        hr = src.shape[2] // 2
        gives = src.at[:, :, pl.ds(pl.multiple_of((1 - c) * hr, 8), hr), :]
        copies.append(_remote(gives, land, sems[2 * t], sems[2 * t + 1], (x, y, 1 - c)))
    return copies


def _rs_add_chips(sum32, got, kind, kc, name):
    _, l, _, hr, cols = got.shape
    tr, _ = _add_tile(hr, cols)
    nr = hr // tr
    k_arr, c_arr = kc
    if kind == "col":
        own = pl.BlockSpec((None, None, tr, cols), lambda li, i, k_ref, c_ref: (li, 0, i, k_ref[0]))
    else:
        own = pl.BlockSpec((None, None, tr, cols), lambda li, i, k_ref, c_ref: (li, k_ref[0], i, 0))

    def body(k_ref, c_ref, own_ref, got_ref, o_ref):
        o_ref[...] = ((own_ref[...] + got_ref[0].astype(F32)) + got_ref[1].astype(F32)) + got_ref[2].astype(F32)

    return pl.pallas_call(
        body, name=name,
        grid_spec=pltpu.PrefetchScalarGridSpec(
            num_scalar_prefetch=2, grid=(l, nr),
            in_specs=[own, pl.BlockSpec((3, None, None, tr, cols), lambda li, i, k_ref, c_ref: (0, li, 0, i, 0))],
            out_specs=pl.BlockSpec((None, tr, cols), lambda li, i, k_ref, c_ref: (li, c_ref[0] * nr + i, 0))),
        out_shape=jax.ShapeDtypeStruct((l, 2 * hr, cols), F32),
        compiler_params=_cparams("parallel", "parallel"),
    )(k_arr, c_arr, sum32, got)


def _rs_finish(quarters):
    n = len(quarters)

    def body(*refs):
        bufs = refs[n:2 * n]
        send_sems, recv_sems = refs[2 * n:]
        x, y, c, _ = _place()
        copies = []
        for t in range(n):
            hr = bufs[t].shape[1] // 2
            mine = bufs[t].at[:, pl.ds(pl.multiple_of(c * hr, 8), hr), :]
            cp = _remote(mine, mine, send_sems.at[t], recv_sems.at[t], (x, y, 1 - c))
            cp.start()
            copies.append(cp)
        for cp in copies:
            cp.wait()

    return pl.pallas_call(
        body, name="reduce_finish", in_specs=[ANY] * n, out_specs=[ANY] * n,
        out_shape=[jax.ShapeDtypeStruct(a.shape, a.dtype) for a in quarters],
        input_output_aliases={t: t for t in range(n)},
        scratch_shapes=[pltpu.SemaphoreType.DMA((n,)), pltpu.SemaphoreType.DMA((n,))],
        compiler_params=pltpu.CompilerParams(has_side_effects=True),
    )(*quarters)


def _reduce_start(parts, kinds, names, tag):
    canon = [_canonical(p, kind) for p, kind in zip(parts, kinds)]
    lands = [lax.empty(a.shape[:2] + (a.shape[2] // 2, a.shape[3]), a.dtype) for a in canon]
    sems, srcs, lands, token = _split_start(_rs_sibling_copies, canon, lands, 2 * len(canon),
                                            f"reduce_sibling_start_{tag}")
    return (sems, srcs, lands, kinds, names, tag), token


def _reduce_middle(state, after):
    sems, srcs, lands, kinds, names, tag = state
    c_arr = jnp.reshape(lax.axis_index("c"), (1,)).astype(jnp.int32)
    from_sibling = _split_wait(_rs_sibling_copies, sems, srcs, lands, after, f"reduce_sibling_wait_{tag}")
    sums = [_rs_add_sibling(p, g, c_arr, f"reduce_add_sibling_{nm}") for p, g, nm in zip(srcs, from_sibling, names)]
    sums16 = [s16 for _, s16 in sums]
    copies = functools.partial(_rs_chips_copies, kinds=kinds)
    lands = [lax.empty((OTHER_CHIPS,) + _quarter4_shape(a, k), a.dtype) for a, k in zip(sums16, kinds)]
    sems, srcs, lands, token = _split_start(copies, sums16, lands, 2 * OTHER_CHIPS * len(sums16),
                                            f"reduce_chips_start_{tag}")
    return (sems, srcs, lands, [s32 for s32, _ in sums], kinds, names, tag), token


def _reduce_finish(state, after):
    sems, srcs, lands, sums32, kinds, names, tag = state
    x, y, c = lax.axis_index("x"), lax.axis_index("y"), lax.axis_index("c")
    kc = (jnp.reshape(2 * x + y, (1,)).astype(jnp.int32), jnp.reshape(c, (1,)).astype(jnp.int32))
    copies = functools.partial(_rs_chips_copies, kinds=kinds)
    from_chips = _split_wait(copies, sems, srcs, lands, after, f"reduce_chips_wait_{tag}")
    halves = [_rs_add_chips(s32, g, kind, kc, f"reduce_add_chips_{nm}")
              for s32, g, kind, nm in zip(sums32, from_chips, kinds, names)]
    return _rs_finish(halves)


SMALL_PEERS = 7


def _small_exchange(pack):
    rows = pack.shape[0]

    def body(p_ref, slots_ref, total_ref, send_sems, recv_sems):
        x, y, c, _ = _place()
        me = 4 * x + 2 * y + c
        slots_ref[me] = p_ref[...]
        copies = []
        for p in range(1, SMALL_PEERS + 1):
            px, py, pc = (p >> 2) & 1, (p >> 1) & 1, p & 1
            peer = (1 - x if px else x, 1 - y if py else y, 1 - c if pc else c)
            cp = _remote(p_ref, slots_ref.at[me], send_sems.at[p - 1], recv_sems.at[p - 1], peer)
            cp.start()
            copies.append(cp)
        for cp in copies:
            cp.wait()
        total = slots_ref[0]
        for i in range(1, SMALL_PEERS + 1):
            total = total + slots_ref[i]
        total_ref[...] = total

    vmem = pl.BlockSpec(memory_space=pltpu.VMEM)
    return pl.pallas_call(
        body, name="small_exchange", in_specs=[vmem], out_specs=[vmem, vmem],
        out_shape=[jax.ShapeDtypeStruct((SMALL_PEERS + 1, rows, LANES), F32), jax.ShapeDtypeStruct((rows, LANES), F32)],
        scratch_shapes=[pltpu.SemaphoreType.DMA((SMALL_PEERS,)), pltpu.SemaphoreType.DMA((SMALL_PEERS,))],
        compiler_params=pltpu.CompilerParams(has_side_effects=True),
    )(pack)


def _pack(arrays):
    rows = []
    for a in arrays:
        flat = a.reshape(-1).astype(F32)
        rows.append(jnp.pad(flat, (0, (-flat.shape[0]) % LANES)).reshape(-1, LANES))
    out = jnp.concatenate(rows, axis=0)
    return jnp.pad(out, ((0, (-out.shape[0]) % 8), (0, 0)))


def _unpack(pack, shapes):
    out, r = [], 0
    for sh in shapes:
        size = math.prod(sh)
        nr = -(-size // LANES)
        out.append(pack[r:r + nr].reshape(-1)[:size].reshape(sh))
        r += nr
    return out


def _adamw(w, g, m, v, name):
    shape = w.shape
    to2d = lambda a: a.reshape(-1, shape[-1])
    rows = math.prod(shape[:-1])
    tile = 256 if rows % 256 == 0 else rows

    def fn(wb, gb, mb, vb):
        m2 = ADAM_B1 * mb + (1.0 - ADAM_B1) * gb
        v2 = ADAM_B2 * vb + (1.0 - ADAM_B2) * (gb * gb)
        m_hat = m2 / (1.0 - ADAM_B1 ** ADAM_STEP)
        v_hat = v2 / (1.0 - ADAM_B2 ** ADAM_STEP)
        return -ADAM_LR * (m_hat / (jnp.sqrt(v_hat) + ADAM_EPS) + ADAM_WD * wb), m2, v2

    res = _rowwise(fn, [to2d(w), to2d(g), to2d(m), to2d(v)], [], [(shape[-1], F32)] * 3, [], tile=tile, name=name)
    return [r.reshape(shape) for r in res]


def _adamw_layer(w, g, m, v, layer, outs, name):
    _, rows, cols = w.shape
    tile = rows
    while tile * cols * 4 > (1 << 20) and tile % 16 == 0:
        tile //= 2

    def body(w_ref, g_ref, m_ref, v_ref, *rest):
        g_out, d_out, m_out, v_out = rest[-4:]
        gb = g_ref[...]
        m2 = ADAM_B1 * m_ref[...] + (1.0 - ADAM_B1) * gb
        v2 = ADAM_B2 * v_ref[...] + (1.0 - ADAM_B2) * (gb * gb)
        m_hat = m2 / (1.0 - ADAM_B1 ** ADAM_STEP)
        v_hat = v2 / (1.0 - ADAM_B2 ** ADAM_STEP)
        g_out[...] = gb
        d_out[...] = -ADAM_LR * (m_hat / (jnp.sqrt(v_hat) + ADAM_EPS) + ADAM_WD * w_ref[...])
        m_out[...] = m2
        v_out[...] = v2

    stacked = pl.BlockSpec((None, tile, cols), lambda i: (layer, i, 0))
    return pl.pallas_call(
        body, name=name, grid=(rows // tile,),
        in_specs=[stacked, pl.BlockSpec((None, tile, cols), lambda i: (0, i, 0)), stacked, stacked] + [ANY] * 4,
        out_specs=[stacked] * 4, out_shape=[jax.ShapeDtypeStruct(w.shape, F32)] * 4,
        input_output_aliases={4 + i: i for i in range(4)}, compiler_params=_cparams("parallel"),
    )(w, g, m, v, *outs)


BIG = (("ffn1_w_gu", "col"), ("ffn1_w_down", "row"), ("w_in_even", "col"), ("w_out_even", "row"),
       ("w_in_odd", "col"), ("w_out_odd", "row"), ("ffn2_w_gu", "col"), ("ffn2_w_down", "row"))
SMALL = ("norm_ffn1", "norm_mix", "dn_conv_w", "dn_a_log", "dn_dt_bias", "dn_norm_g", "fox_q_norm_g", "fox_k_norm_g",
         "fox_f_bias", "norm_ffn2")
WEIGHTS = ("norm_ffn1", "ffn1_w_gu", "ffn1_w_down", "norm_mix", "w_in_even", "dn_conv_w", "dn_a_log", "dn_dt_bias",
           "dn_norm_g", "fox_q_norm_g", "fox_k_norm_g", "fox_f_bias", "w_out_even", "w_in_odd", "w_out_odd",
           "norm_ffn2", "ffn2_w_gu", "ffn2_w_down")


def _step(x, target, w, m, v):
    k = 2 * lax.axis_index("x") + lax.axis_index("y")
    n_conv = w["dn_conv_w"].shape[2]

    kc = jnp.reshape(k, (1,)).astype(jnp.int32)
    kinds = dict(BIG)
    quarters = {name: w[name] for name in kinds}
    quarters["w_in_even"] = jnp.pad(w["w_in_even"], ((0, 0), (0, 0), (0, EVEN_QUARTER_PAD - EVEN_QUARTER)))
    first_names = [name for name in kinds if name not in ("w_in_odd", "w_out_odd")]
    rest_names = list(kinds)

    def even_columns(whole):
        padded = whole["w_in_even"]
        ref_order = jnp.concatenate([padded[..., q * EVEN_QUARTER_PAD:q * EVEN_QUARTER_PAD + EVEN_QUARTER]
                                     for q in range(4)], axis=-1)
        return {**whole, "w_in_even": _even_to_kernel_layout(ref_order)}

    conv_slots, _ = _small_exchange(_pack([w["dn_conv_w"]]))
    placed = [_place_quarter(quarters[name], kinds[name], kc, f"place_first_{name}", 0, 1) for name in first_names]
    gathered = _gather_weights(placed, [kinds[name] for name in first_names])
    first = even_columns(dict(zip(first_names, gathered)))
    placed = [_place_quarter(quarters[name], kinds[name], kc, f"place_rest_{name}", 1 if name in first_names else 0)
              for name in rest_names]
    rest_kinds = [kinds[name] for name in rest_names]
    sems, on_their_way, token = _gather_start(placed, rest_kinds, [conv_slots, *gathered], "rest")

    def rest_after(value):
        landed = _gather_wait(sems, on_their_way, rest_kinds, value, "rest")
        return even_columns(dict(zip(rest_names, _gather_forward(landed, rest_kinds, "rest"))))

    whole = {}
    conv_rows = math.prod(w["dn_conv_w"].shape) // LANES
    conv_quarters = [conv_slots[2 * q, :conv_rows].reshape(w["dn_conv_w"].shape) for q in range(4)]
    whole["dn_conv_w"] = jnp.concatenate(conv_quarters, axis=-1)
    for name in SMALL:
        if name != "dn_conv_w":
            whole[name] = w[name]

    updated = {name: [lax.empty(w[name].shape, F32) for _ in range(4)] for name in kinds}

    def on_reduced(layer, layer_grads):
        for name, g in layer_grads.items():
            if name == "w_in_even":
                g = g[..., :EVEN_QUARTER]
            stacked_layer = layer if w[name].shape[0] == w["norm_mix"].shape[0] else layer // 2
            updated[name] = _adamw_layer(w[name], g, m[name], v[name], stacked_layer, updated[name], f"adamw_{name}")

    loss, dx, small = _forward_backward(x, target, whole, first, rest_after, token, on_reduced)

    _, small_sum = _small_exchange(_pack([small[n] for n in SMALL]))
    grads = dict(zip(SMALL, _unpack(small_sum, [small[n].shape for n in SMALL])))
    grads["dn_conv_w"] = lax.dynamic_slice_in_dim(grads["dn_conv_w"], k * n_conv, n_conv, axis=2)
    delta, new_m, new_v = {}, {}, {}
    for name in kinds:
        grads[name], delta[name], new_m[name], new_v[name] = updated[name]
    packs = [_pack([d[n] for n in SMALL]) for d in (w, grads, m, v)]
    shapes = [w[n].shape for n in SMALL]
    for out, res in zip((delta, new_m, new_v), _adamw(*packs, "adamw_small")):
        out.update(zip(SMALL, _unpack(res, shapes)))
    total_loss = lax.psum(loss[0, 0], ("x", "y", "c"))
    return total_loss, dx, grads, delta, new_m, new_v


def kernel(x, norm_ffn1, ffn1_w_gu, ffn1_w_down, norm_mix, w_in_even, dn_conv_w, dn_a_log, dn_dt_bias, dn_norm_g, fox_q_norm_g, fox_k_norm_g, fox_f_bias, w_out_even, w_in_odd, w_out_odd, norm_ffn2, ffn2_w_gu, ffn2_w_down, loss_target, m_norm_ffn1, m_ffn1_w_gu, m_ffn1_w_down, m_norm_mix, m_w_in_even, m_dn_conv_w, m_dn_a_log, m_dn_dt_bias, m_dn_norm_g, m_fox_q_norm_g, m_fox_k_norm_g, m_fox_f_bias, m_w_out_even, m_w_in_odd, m_w_out_odd, m_norm_ffn2, m_ffn2_w_gu, m_ffn2_w_down, v_norm_ffn1, v_ffn1_w_gu, v_ffn1_w_down, v_norm_mix, v_w_in_even, v_dn_conv_w, v_dn_a_log, v_dn_dt_bias, v_dn_norm_g, v_fox_q_norm_g, v_fox_k_norm_g, v_fox_f_bias, v_w_out_even, v_w_in_odd, v_w_out_odd, v_norm_ffn2, v_ffn2_w_gu, v_ffn2_w_down):
    w = dict(zip(WEIGHTS, (norm_ffn1, ffn1_w_gu, ffn1_w_down, norm_mix, w_in_even, dn_conv_w, dn_a_log, dn_dt_bias,
                           dn_norm_g, fox_q_norm_g, fox_k_norm_g, fox_f_bias, w_out_even, w_in_odd, w_out_odd,
                           norm_ffn2, ffn2_w_gu, ffn2_w_down)))
    m = dict(zip(WEIGHTS, (m_norm_ffn1, m_ffn1_w_gu, m_ffn1_w_down, m_norm_mix, m_w_in_even, m_dn_conv_w, m_dn_a_log,
                           m_dn_dt_bias, m_dn_norm_g, m_fox_q_norm_g, m_fox_k_norm_g, m_fox_f_bias, m_w_out_even,
                           m_w_in_odd, m_w_out_odd, m_norm_ffn2, m_ffn2_w_gu, m_ffn2_w_down)))
    v = dict(zip(WEIGHTS, (v_norm_ffn1, v_ffn1_w_gu, v_ffn1_w_down, v_norm_mix, v_w_in_even, v_dn_conv_w, v_dn_a_log,
                           v_dn_dt_bias, v_dn_norm_g, v_fox_q_norm_g, v_fox_k_norm_g, v_fox_f_bias, v_w_out_even,
                           v_w_in_odd, v_w_out_odd, v_norm_ffn2, v_ffn2_w_gu, v_ffn2_w_down)))
    loss, dx, grads, delta, new_m, new_v = _step(x[0], loss_target[0], w, m, v)
    return (loss, dx[None], *[grads[n] for n in WEIGHTS], *[delta[n] for n in WEIGHTS],
            *[new_m[n] for n in WEIGHTS], *[new_v[n] for n in WEIGHTS])
```

```python
import functools
import math

import jax
import jax.numpy as jnp
from jax import lax
from jax.experimental import pallas as pl
from jax.experimental.pallas import tpu as pltpu

F32 = jnp.float32
BF16 = jnp.bfloat16
HI = lax.Precision.HIGH

HEAD_DIM = 128
N_DN_HEADS = 4
N_FOX_HEADS = 4
N_SB_HEADS = 8
D_DN = N_DN_HEADS * HEAD_DIM
D_FOX = N_FOX_HEADS * HEAD_DIM
CONV_WIDTH = 4
DN_CHUNK = 64
EPS = 1e-6
ATT_SCALE = HEAD_DIM ** -0.5
ADAM_LR, ADAM_B1, ADAM_B2, ADAM_EPS, ADAM_WD, ADAM_STEP = 0.001, 0.9, 0.999, 1e-08, 0.01, 10

V7X_VMEM_LIMIT = 56 * 1024 * 1024
LANES = 128
ATT_TQ = 256
ATT_TK = 128
ATT_SUB = ATT_TQ // ATT_TK

LANE_BETA, LANE_DECAY, LANE_FORGET = 0, 4, 8


def _cparams(*sem):
    return pltpu.CompilerParams(dimension_semantics=sem, vmem_limit_bytes=V7X_VMEM_LIMIT)


def _sigmoid(x):
    return 1.0 / (1.0 + jnp.exp(-x))


def _softplus(x):
    return jnp.maximum(x, 0.0) + jnp.log(1.0 + jnp.exp(-jnp.abs(x)))


def _silu_grad(y, sg):
    return sg * (1.0 + y * (1.0 - sg))


def _rowwise(fn, rows, bcast, outs, sums, *, tile, name):
    rows = [r if isinstance(r, tuple) else (r, r.shape[1], 0) for r in rows]
    s = rows[0][0].shape[0]
    assert s % tile == 0
    n_in, n_b, n_out, n_sum = len(rows), len(bcast), len(outs), len(sums)

    def body(*refs):
        ins = [r[...] for r in refs[:n_in + n_b]]
        res = fn(*ins)
        if not isinstance(res, (tuple, list)):
            res = (res,)
        out_refs = refs[n_in + n_b:n_in + n_b + n_out]
        sum_refs = refs[n_in + n_b + n_out:]
        for o_ref, val in zip(out_refs, res[:n_out]):
            o_ref[...] = val.astype(o_ref.dtype)
        if n_sum:
            @pl.when(pl.program_id(0) == 0)
            def _():
                for s_ref in sum_refs:
                    s_ref[...] = jnp.zeros_like(s_ref)
            for s_ref, val in zip(sum_refs, res[n_out:]):
                s_ref[...] += val

    in_specs = [pl.BlockSpec((tile, w), lambda i, cb=cb: (i, cb)) for _, w, cb in rows]
    in_specs += [pl.BlockSpec(b.shape, lambda i, nd=b.ndim: (0,) * nd) for b in bcast]
    out_specs = [pl.BlockSpec((tile, c), lambda i: (i, 0)) for c, _ in outs]
    out_specs += [pl.BlockSpec(sh, lambda i: (0, 0)) for sh in sums]
    out_shape = [jax.ShapeDtypeStruct((s, c), dt) for c, dt in outs]
    out_shape += [jax.ShapeDtypeStruct(sh, F32) for sh in sums]
    return pl.pallas_call(
        body, name=name, grid=(s // tile,), in_specs=in_specs, out_specs=out_specs, out_shape=out_shape,
        compiler_params=_cparams("arbitrary" if n_sum else "parallel"),
    )(*[r[0] for r in rows], *bcast)


def _rms_fwd(x, gain, name):
    def fn(xb, g):
        r = lax.rsqrt(jnp.mean(xb * xb, axis=-1, keepdims=True) + EPS)
        return (xb * r * g,)
    return _rowwise(fn, [x], [gain], [(x.shape[1], BF16)], [], tile=512, name=name)[0]


_DIMS = {"nn": (((1,), (0,)), ((), ())), "nt": (((1,), (1,)), ((), ())), "tn": (((0,), (0,)), ((), ()))}


def _dot(a, b, kind):
    return lax.dot_general(a.astype(BF16), b.astype(BF16), _DIMS[kind], preferred_element_type=F32)


def _dot32(a, b, kind="nn"):
    return lax.dot_general(a, b, _DIMS[kind], precision=HI, preferred_element_type=F32)


def _mm(a, b, kind, *, tm, tn, out_dtype, name, scale=None, residual=None, a_lead=(), b_lead=(),
        b_spec=None, n=None, into=None, after=None):
    ash, bsh = a.shape[len(a_lead):], b.shape[len(b_lead):]
    m = ash[1] if kind == "tn" else ash[0]
    k = ash[0] if kind == "tn" else ash[1]
    if b_spec is None:
        n = bsh[0] if kind == "nt" else bsh[1]
        assert k == (bsh[1] if kind == "nt" else bsh[0]), (ash, bsh, kind)
    assert m % tm == 0 and n % tn == 0, (m, tm, n, tn)
    la, lb = (None,) * len(a_lead), (None,) * len(b_lead)
    if kind == "tn":
        a_spec = pl.BlockSpec(la + (k, tm), lambda j, i: a_lead + (0, i))
    else:
        a_spec = pl.BlockSpec(la + (tm, k), lambda j, i: a_lead + (i, 0))
    if b_spec is None:
        if kind == "nt":
            b_spec = pl.BlockSpec(lb + (tn, k), lambda j, i: b_lead + (j, 0))
        else:
            b_spec = pl.BlockSpec(lb + (k, tn), lambda j, i: b_lead + (0, j))
    in_specs, args = [a_spec, b_spec], [a, b]
    if residual is not None:
        in_specs.append(pl.BlockSpec((tm, tn), lambda j, i: (i, j)))
        args.append(residual)
    aliases = {}
    if after is not None:
        in_specs.append(pl.BlockSpec(memory_space=pl.ANY))
        args.append(after)
    if into is not None:
        buf, layer = into
        in_specs.append(pl.BlockSpec(memory_space=pl.ANY))
        args.append(buf)
        aliases = {len(args) - 1: 0}
        out_spec = pl.BlockSpec((None, tm, tn), lambda j, i: (layer, i, j))
        out_shape = jax.ShapeDtypeStruct(buf.shape, buf.dtype)
    else:
        out_spec = pl.BlockSpec((tm, tn), lambda j, i: (i, j))
        out_shape = jax.ShapeDtypeStruct((m, n), out_dtype)

    def body(a_ref, b_ref, *rest):
        acc = _dot(a_ref[...], b_ref[...], kind)
        if scale is not None:
            acc = acc * scale
        if residual is not None:
            acc = acc + rest[0][...]
        rest[-1][...] = acc.astype(rest[-1].dtype)

    return pl.pallas_call(
        body, name=name, grid=(n // tn, m // tm), in_specs=in_specs, out_specs=out_spec, out_shape=out_shape,
        input_output_aliases=aliases, compiler_params=_cparams("parallel", "parallel"),
    )(*args)


def _ffn_up(n, w_gu, layer, name):
    s, d = n.shape
    f = w_gu.shape[2] // 2
    tm, tn = 512, f // 2
    nj = f // tn

    def body(n_ref, wg_ref, wu_ref, gu_ref, a_ref):
        nv = n_ref[...]
        g = _dot(nv, wg_ref[...], "nn")
        u = _dot(nv, wu_ref[...], "nn")
        gu_ref[0] = g.astype(BF16)
        gu_ref[1] = u.astype(BF16)
        a_ref[...] = (g * _sigmoid(g) * u).astype(BF16)

    return pl.pallas_call(
        body, name=name, grid=(nj, s // tm),
        in_specs=[pl.BlockSpec((tm, d), lambda j, i: (i, 0)),
                  pl.BlockSpec((None, d, tn), lambda j, i: (layer, 0, j)),
                  pl.BlockSpec((None, d, tn), lambda j, i: (layer, 0, j + nj))],
        out_specs=[pl.BlockSpec((2, tm, tn), lambda j, i: (0, i, j)),
                   pl.BlockSpec((tm, tn), lambda j, i: (i, j))],
        out_shape=[jax.ShapeDtypeStruct((2, s, f), BF16), jax.ShapeDtypeStruct((s, f), BF16)],
        compiler_params=_cparams("parallel", "parallel"),
    )(n, w_gu, w_gu)


def _ffn_down_bwd(dxo, w_down, gu, layer, name, after=None):
    s, d = dxo.shape
    f = w_down.shape[1]
    tm, tn = 512, f // 2
    extra_specs, extra = ([ANY], [after]) if after is not None else ([], [])

    def body(dx_ref, w_ref, gu_ref, *rest):
        dgu_ref = rest[-1]
        da = 0.5 * _dot(dx_ref[...], w_ref[...], "nt")
        g = gu_ref[0].astype(F32)
        u = gu_ref[1].astype(F32)
        sg = _sigmoid(g)
        dgu_ref[0] = (da * u * _silu_grad(g, sg)).astype(BF16)
        dgu_ref[1] = (da * g * sg).astype(BF16)

    return pl.pallas_call(
        body, name=name, grid=(f // tn, s // tm),
        in_specs=[pl.BlockSpec((tm, d), lambda j, i: (i, 0)),
                  pl.BlockSpec((None, tn, d), lambda j, i: (layer, j, 0)),
                  pl.BlockSpec((2, tm, tn), lambda j, i: (0, i, j))] + extra_specs,
        out_specs=pl.BlockSpec((2, tm, tn), lambda j, i: (0, i, j)),
        out_shape=jax.ShapeDtypeStruct((2, s, f), BF16),
        compiler_params=_cparams("parallel", "parallel"),
    )(dxo, w_down, gu, *extra)


NORM_BWD_TM = 256


def _norm_bwd_after(terms, operands, specs, x, dres, gain, name):
    s, d = x.shape
    tm = NORM_BWD_TM
    n_op = len(operands)

    def body(*refs):
        x_ref, dres_ref, g_ref = refs[n_op:n_op + 3]
        dx_ref, dx16_ref, dgain_ref = refs[n_op + 3:]
        dn = None
        for a, b in terms(*refs[:n_op]):
            dn = _dot(a, b, "nt") if dn is None else dn + _dot(a, b, "nt")
        xb = x_ref[...]
        r = lax.rsqrt(jnp.mean(xb * xb, axis=-1, keepdims=True) + EPS)
        xh = xb * r
        dxh = dn * g_ref[...]
        dx = dres_ref[...] + r * (dxh - xh * jnp.mean(dxh * xh, axis=-1, keepdims=True))
        dx_ref[...] = dx
        dx16_ref[...] = dx.astype(BF16)

        @pl.when(pl.program_id(0) == 0)
        def _():
            dgain_ref[...] = jnp.zeros_like(dgain_ref)
        dgain_ref[...] += jnp.sum(dn * xh, axis=0, keepdims=True)

    rows = pl.BlockSpec((tm, d), lambda i: (i, 0))
    return pl.pallas_call(
        body, name=name, grid=(s // tm,),
        in_specs=list(specs) + [rows, rows, pl.BlockSpec((1, d), lambda i: (0, 0))],
        out_specs=[rows, rows, pl.BlockSpec((1, d), lambda i: (0, 0))],
        out_shape=[jax.ShapeDtypeStruct((s, d), F32), jax.ShapeDtypeStruct((s, d), BF16),
                   jax.ShapeDtypeStruct((1, d), F32)],
        compiler_params=_cparams("arbitrary"),
    )(*operands, x, dres, gain)


def _ffn_up_bwd(dgu, w_gu, layer, x, dres, gain, name):
    _, s, f = dgu.shape
    d = w_gu.shape[1]
    specs = [pl.BlockSpec((2, NORM_BWD_TM, f), lambda i: (0, i, 0)),
             pl.BlockSpec((None, d, f), lambda i: (layer, 0, 0)),
             pl.BlockSpec((None, d, f), lambda i: (layer, 0, 1))]
    terms = lambda dgu_ref, wg_ref, wu_ref: [(dgu_ref[0], wg_ref[...]), (dgu_ref[1], wu_ref[...])]
    return _norm_bwd_after(terms, [dgu, w_gu, w_gu], specs, x, dres, gain, name)


def _in_proj_bwd(dproj, w_in, j, x, dres, gain, name):
    k = dproj.shape[1]
    d = w_in.shape[1]
    specs = [pl.BlockSpec((NORM_BWD_TM, k), lambda i: (i, 0)), pl.BlockSpec((None, d, k), lambda i: (j, 0, 0))]
    terms = lambda a_ref, b_ref: [(a_ref[...], b_ref[...])]
    return _norm_bwd_after(terms, [dproj, w_in], specs, x, dres, gain, name)


def _ffn_fwd(x, gain, w_gu, w_down, layer, tag):
    n = _rms_fwd(x, gain, f"{tag}_norm")
    gu, a = _ffn_up(n, w_gu, layer, f"{tag}_up")
    x2 = _mm(a, w_down, "nn", tm=512, tn=x.shape[1], out_dtype=F32, name=f"{tag}_down", scale=0.5, residual=x,
             b_lead=(layer,))
    return x2, (x, n, gu, a)


def _ffn_bwd(dxo, dxo16, saved, gain, w_gu, w_down, layer, tag, g_gu, g_down, after=None):
    x, n, gu, a = saved
    s, f = a.shape
    dgu = _ffn_down_bwd(dxo16, w_down, gu, layer, f"{tag}_down_bwd", after)
    g_down = _mm(a, dxo16, "tn", tm=256, tn=dxo16.shape[1], out_dtype=F32, name=f"{tag}_down_dw", scale=0.5,
                 into=(g_down, 0))
    tn = f // 2
    nj = f // tn
    g_gu = _mm(n, dgu, "tn", tm=512, tn=tn, out_dtype=F32, name=f"{tag}_up_dw", into=(g_gu, 0), n=2 * f,
               b_spec=pl.BlockSpec((None, s, tn), lambda j, i: (j // nj, 0, j % nj)))
    dx, dx16, dgain = _ffn_up_bwd(dgu, w_gu, layer, x, dxo, gain, f"{tag}_up_bwd")
    return dx, dx16, dgain, g_gu, g_down


def _lane_col(blk, lane):
    li = lax.broadcasted_iota(jnp.int32, blk.shape, 1)
    return jnp.sum(jnp.where(li == lane, blk, 0.0), axis=1, keepdims=True)


def _split_dot(x, tri):
    hi = x.astype(BF16)
    lo = (x - hi.astype(F32)).astype(BF16)
    return (lax.dot_general(hi, tri, _DIMS["nn"], preferred_element_type=F32)
            + lax.dot_general(lo, tri, _DIMS["nn"], preferred_element_type=F32))


class _Each:
    def __init__(self, vals):
        self.vals = list(vals)

    def _with(self, other, op):
        others = other.vals if isinstance(other, _Each) else [other] * len(self.vals)
        return _Each(op(a, b) for a, b in zip(self.vals, others))

    def __add__(self, other):
        return self._with(other, lambda a, b: a + b)

    def __sub__(self, other):
        return self._with(other, lambda a, b: a - b)

    def __mul__(self, other):
        return self._with(other, lambda a, b: a * b)

    def __neg__(self):
        return _Each(-a for a in self.vals)


def _each(fn, *args):
    n = max(len(a.vals) for a in args if isinstance(a, _Each))
    res = [fn(*xs) for xs in zip(*[a.vals if isinstance(a, _Each) else [a] * n for a in args])]
    if isinstance(res[0], tuple):
        return tuple(_Each(r) for r in zip(*res))
    return _Each(res)


def _keep(cond, x):
    return _each(lambda v: jnp.where(cond, v, 0.0), x)


def _rowsum(x):
    return _each(lambda v: jnp.sum(v, axis=1, keepdims=True), x)


ATT_HEADS = 2
ATT_WIDTH = ATT_HEADS * HEAD_DIM
_HEAD_COLS = [slice(h * HEAD_DIM, (h + 1) * HEAD_DIM) for h in range(ATT_HEADS)]


def _att_specs(n_heads, s):
    groups = n_heads // ATT_HEADS
    q_spec = pl.BlockSpec((ATT_TQ, ATT_WIDTH), lambda g, i: (i, g))
    k_spec = pl.BlockSpec((s, ATT_WIDTH), lambda g, i: (0, groups + g))
    v_spec = pl.BlockSpec((s, ATT_WIDTH), lambda g, i: (0, 2 * groups + g))
    return q_spec, k_spec, v_spec


def _heads_of(ref, rows=None):
    return _Each(ref[:, cs] if rows is None else ref[rows, cs] for cs in _HEAD_COLS)


def _dot_each(a, b, kind):
    return _each(lambda x, y: _dot(x, y, kind), a, b)


def _att_iotas():
    row = lax.broadcasted_iota(jnp.int32, (ATT_TQ, ATT_TK), 0)
    col = lax.broadcasted_iota(jnp.int32, (ATT_TQ, ATT_TK), 1)
    jr = lax.broadcasted_iota(jnp.int32, (ATT_TK, ATT_TK), 0)
    jc = lax.broadcasted_iota(jnp.int32, (ATT_TK, ATT_TK), 1)
    return row, col, jr, jc


def _sb_fwd(qkv, n_heads, name):
    s = qkv.shape[0]

    def body(q_ref, k_ref, v_ref, o16_ref, o32_ref):
        i = pl.program_id(1)
        q = _heads_of(q_ref)
        row, col, jr, jc = _att_iotas()
        later = (jr > jc).astype(BF16)

        def step(jb, carry, diagonal):
            c_sp, acc = (_Each(part) for part in carry)
            work = []
            for sub in reversed(range(ATT_SUB)):
                keys = pl.ds(pl.multiple_of(jb * ATT_TQ + sub * ATT_TK, ATT_TK), ATT_TK)
                z = _dot_each(q, _heads_of(k_ref, keys), "nt") * ATT_SCALE
                sp = _each(_softplus, z)
                before = (col + sub * ATT_TK) < row if diagonal else None
                spm = _keep(before, sp) if diagonal else sp
                work.append((keys, z - sp, spm, _each(lambda x: _dot(x, later, "nn"), spm), before))
            for keys, logsig, spm, within, before in work:
                a = _each(jnp.exp, logsig - (c_sp + within))
                if diagonal:
                    a = _keep(before, a)
                acc = acc + _each(_split_dot, a, _heads_of(v_ref, keys))
                c_sp = c_sp + _rowsum(spm)
            return tuple(c_sp.vals), tuple(acc.vals)

        zeros = lambda width: tuple(jnp.zeros((ATT_TQ, width), F32) for _ in range(ATT_HEADS))
        carry = step(i, (zeros(1), zeros(HEAD_DIM)), True)
        _, acc = lax.fori_loop(0, i, lambda it, cr: step(i - 1 - it, cr, False), carry)
        for cs, acc_h in zip(_HEAD_COLS, acc):
            o16_ref[:, cs] = acc_h.astype(BF16)
            o32_ref[:, cs] = acc_h

    q_spec, k_spec, v_spec = _att_specs(n_heads, s)
    o_spec = pl.BlockSpec((ATT_TQ, ATT_WIDTH), lambda g, i: (i, g))
    return pl.pallas_call(
        body, name=name, grid=(n_heads // ATT_HEADS, s // ATT_TQ), in_specs=[q_spec, k_spec, v_spec],
        out_specs=[o_spec, o_spec],
        out_shape=[jax.ShapeDtypeStruct((s, n_heads * HEAD_DIM), BF16),
                   jax.ShapeDtypeStruct((s, n_heads * HEAD_DIM), F32)],
        compiler_params=_cparams("parallel", "arbitrary"),
    )(qkv, qkv, qkv)


def _sb_bwd(qkv, o32, do, n_heads, name):
    s = qkv.shape[0]

    def body(q_ref, k_ref, v_ref, o_ref, do_ref, dq_ref, dk_ref, dv_ref):
        i = pl.program_id(1)

        @pl.when(i == 0)
        def _():
            dk_ref[...] = jnp.zeros_like(dk_ref)
            dv_ref[...] = jnp.zeros_like(dv_ref)

        q, do = _heads_of(q_ref), _heads_of(do_ref)
        total = _rowsum(_each(lambda a, b: a.astype(F32) * b, do, _heads_of(o_ref)))
        row, col, jr, jc = _att_iotas()
        later = (jr > jc).astype(BF16)
        not_before = (jr >= jc).astype(BF16)

        def step(jb, carry, diagonal):
            c_sp, c_e, dq = (_Each(part) for part in carry)
            work = []
            for sub in reversed(range(ATT_SUB)):
                keys = pl.ds(pl.multiple_of(jb * ATT_TQ + sub * ATT_TK, ATT_TK), ATT_TK)
                k = _heads_of(k_ref, keys)
                z = _dot_each(q, k, "nt") * ATT_SCALE
                sp = _each(_softplus, z)
                before = (col + sub * ATT_TK) < row if diagonal else None
                spm = _keep(before, sp) if diagonal else sp
                work.append((keys, k, _each(jnp.exp, z - sp), spm, _each(lambda x: _dot(x, later, "nn"), spm),
                             _dot_each(do, _heads_of(v_ref, keys), "nt"), before))
            for keys, k, sig, spm, within, da, before in work:
                a = sig * _each(lambda x: jnp.exp(-x), c_sp + within)
                if diagonal:
                    a = _keep(before, a)
                e = a * da
                left = total - c_e - _each(lambda x: _split_dot(x, not_before), e)
                dz = (e - (e + left) * sig) * ATT_SCALE
                if diagonal:
                    dz = _keep(before, dz)
                dk, dv = _dot_each(dz, q, "tn"), _dot_each(a, do, "tn")
                for cs, dk_h, dv_h in zip(_HEAD_COLS, dk.vals, dv.vals):
                    dk_ref[keys, cs] += dk_h
                    dv_ref[keys, cs] += dv_h
                dq = dq + _dot_each(dz, k, "nn")
                c_sp = c_sp + _rowsum(spm)
                c_e = c_e + _rowsum(e)
            return tuple(c_sp.vals), tuple(c_e.vals), tuple(dq.vals)

        zeros = lambda width: tuple(jnp.zeros((ATT_TQ, width), F32) for _ in range(ATT_HEADS))
        carry = step(i, (zeros(1), zeros(1), zeros(HEAD_DIM)), True)
        _, _, dq = lax.fori_loop(0, i, lambda it, cr: step(i - 1 - it, cr, False), carry)
        for cs, dq_h in zip(_HEAD_COLS, dq):
            dq_ref[:, cs] = dq_h.astype(BF16)

    q_spec, k_spec, v_spec = _att_specs(n_heads, s)
    blk = pl.BlockSpec((ATT_TQ, ATT_WIDTH), lambda g, i: (i, g))
    full = pl.BlockSpec((s, ATT_WIDTH), lambda g, i: (0, g))
    wide = (s, n_heads * HEAD_DIM)
    return pl.pallas_call(
        body, name=name, grid=(n_heads // ATT_HEADS, s // ATT_TQ), in_specs=[q_spec, k_spec, v_spec, blk, blk],
        out_specs=[blk, full, full],
        out_shape=[jax.ShapeDtypeStruct(wide, BF16), jax.ShapeDtypeStruct(wide, F32), jax.ShapeDtypeStruct(wide, F32)],
        compiler_params=_cparams("parallel", "arbitrary"),
    )(qkv, qkv, qkv, o32, do)


def _fox_logits(q, k, cq, ct_ref, keys):
    ck = _Each(ct_ref[h, :, keys] for h in range(ATT_HEADS))
    return _dot_each(q, k, "nt") * ATT_SCALE + (cq - ck)


def _fox_cq(c_ref, group):
    c = c_ref[...]
    return _Each(_lane_col(c, LANE_FORGET + group * ATT_HEADS + h) for h in range(ATT_HEADS))


def _fox_fwd(qkv, c, ct, name):
    s = qkv.shape[0]
    n_heads = N_FOX_HEADS

    def body(q_ref, k_ref, v_ref, c_ref, ct_ref, o_ref, lse_ref):
        g, i = pl.program_id(0), pl.program_id(1)
        q = _heads_of(q_ref)
        cq = _fox_cq(c_ref, g)
        row, col, _, _ = _att_iotas()

        def step(jb, carry, diagonal):
            m, l, acc = (_Each(part) for part in carry)
            work = []
            m_new = m
            for sub in range(ATT_SUB):
                keys = pl.ds(pl.multiple_of(jb * ATT_TQ + sub * ATT_TK, ATT_TK), ATT_TK)
                sc = _fox_logits(q, _heads_of(k_ref, keys), cq, ct_ref, keys)
                valid = (col + sub * ATT_TK) <= row if diagonal else None
                if diagonal:
                    sc = _each(lambda x: jnp.where(valid, x, -1e30), sc)
                m_new = _each(lambda a, x: jnp.maximum(a, jnp.max(x, axis=1, keepdims=True)), m_new, sc)
                work.append((keys, sc, valid))
            w = _each(jnp.exp, m - m_new)
            l, acc = l * w, acc * w
            for keys, sc, valid in work:
                p = _each(jnp.exp, sc - m_new)
                if diagonal:
                    p = _keep(valid, p)
                l = l + _rowsum(p)
                acc = acc + _each(_split_dot, p, _heads_of(v_ref, keys))
            return tuple(m_new.vals), tuple(l.vals), tuple(acc.vals)

        per_head = lambda width, value: tuple(jnp.full((ATT_TQ, width), value, F32) for _ in range(ATT_HEADS))
        init = (per_head(1, -1e30), per_head(1, 0.0), per_head(HEAD_DIM, 0.0))
        m, l, acc = lax.fori_loop(0, i, lambda jb, cr: step(jb, cr, False), step(i, init, True))
        for h, cs in enumerate(_HEAD_COLS):
            o_ref[:, cs] = acc[h] / l[h]
            lse_ref[h] = jnp.broadcast_to(m[h] + jnp.log(l[h]), (ATT_TQ, LANES))

    q_spec, k_spec, v_spec = _att_specs(n_heads, s)
    return pl.pallas_call(
        body, name=name, grid=(n_heads // ATT_HEADS, s // ATT_TQ),
        in_specs=[q_spec, k_spec, v_spec, pl.BlockSpec((ATT_TQ, LANES), lambda g, i: (i, 0)),
                  pl.BlockSpec((ATT_HEADS, 1, s), lambda g, i: (g, 0, 0))],
        out_specs=[pl.BlockSpec((ATT_TQ, ATT_WIDTH), lambda g, i: (i, g)),
                   pl.BlockSpec((ATT_HEADS, ATT_TQ, LANES), lambda g, i: (g, i, 0))],
        out_shape=[jax.ShapeDtypeStruct((s, n_heads * HEAD_DIM), F32),
                   jax.ShapeDtypeStruct((n_heads, s, LANES), F32)],
        compiler_params=_cparams("parallel", "arbitrary"),
    )(qkv, qkv, qkv, c, ct)


def _fox_bwd(qkv, c, ct, o, lse, do, name):
    s = qkv.shape[0]
    n_heads = N_FOX_HEADS

    def body(q_ref, k_ref, v_ref, c_ref, ct_ref, o_ref, lse_ref, do_ref, dq_ref, dk_ref, dv_ref, dct_ref):
        g, i = pl.program_id(0), pl.program_id(1)

        @pl.when(i == 0)
        def _():
            dk_ref[...] = jnp.zeros_like(dk_ref)
            dv_ref[...] = jnp.zeros_like(dv_ref)
            dct_ref[...] = jnp.zeros_like(dct_ref)

        q = _heads_of(q_ref)
        do16 = _each(lambda x: x.astype(BF16), _heads_of(do_ref))
        delta = _rowsum(_each(lambda a, b: a.astype(F32) * b, do16, _heads_of(o_ref)))
        lse_col = _Each(lse_ref[h, :, 0:1] for h in range(ATT_HEADS))
        cq = _fox_cq(c_ref, g)
        row, col, _, _ = _att_iotas()

        def step(jb, dq, diagonal):
            dq = _Each(dq)
            for sub in range(ATT_SUB):
                keys = pl.ds(pl.multiple_of(jb * ATT_TQ + sub * ATT_TK, ATT_TK), ATT_TK)
                k = _heads_of(k_ref, keys)
                sc = _fox_logits(q, k, cq, ct_ref, keys)
                if diagonal:
                    valid = (col + sub * ATT_TK) <= row
                    p = _keep(valid, _each(jnp.exp, _keep(valid, sc) - lse_col))
                else:
                    p = _each(jnp.exp, sc - lse_col)
                ds = p * (_dot_each(do16, _heads_of(v_ref, keys), "nt") - delta)
                dss = ds * ATT_SCALE
                dk, dv = _dot_each(dss, q, "tn"), _dot_each(p, do16, "tn")
                for h, cs in enumerate(_HEAD_COLS):
                    dct_ref[h, :, keys] -= jnp.sum(ds.vals[h], axis=0, keepdims=True)
                    dk_ref[keys, cs] += dk.vals[h]
                    dv_ref[keys, cs] += dv.vals[h]
                dq = dq + _dot_each(dss, k, "nn")
            return tuple(dq.vals)

        dq0 = step(i, tuple(jnp.zeros((ATT_TQ, HEAD_DIM), F32) for _ in range(ATT_HEADS)), True)
        dq = lax.fori_loop(0, i, lambda jb, dq: step(jb, dq, False), dq0)
        for cs, dq_h in zip(_HEAD_COLS, dq):
            dq_ref[:, cs] = dq_h

    q_spec, k_spec, v_spec = _att_specs(n_heads, s)
    blk = pl.BlockSpec((ATT_TQ, ATT_WIDTH), lambda g, i: (i, g))
    full = pl.BlockSpec((s, ATT_WIDTH), lambda g, i: (0, g))
    wide = jax.ShapeDtypeStruct((s, n_heads * HEAD_DIM), F32)
    return pl.pallas_call(
        body, name=name, grid=(n_heads // ATT_HEADS, s // ATT_TQ),
        in_specs=[q_spec, k_spec, v_spec, pl.BlockSpec((ATT_TQ, LANES), lambda g, i: (i, 0)),
                  pl.BlockSpec((ATT_HEADS, 1, s), lambda g, i: (g, 0, 0)), blk,
                  pl.BlockSpec((ATT_HEADS, ATT_TQ, LANES), lambda g, i: (g, i, 0)), blk],
        out_specs=[blk, full, full, pl.BlockSpec((ATT_HEADS, 1, s), lambda g, i: (g, 0, 0))],
        out_shape=[wide, wide, wide, jax.ShapeDtypeStruct((n_heads, 1, s), F32)],
        compiler_params=_cparams("parallel", "arbitrary"),
    )(qkv, qkv, qkv, c, ct, o, lse, do)


def _cumsum_rows(x, reverse, name):
    s = x.shape[0]
    nb = s // LANES

    def body(x_ref, o_ref):
        r = lax.broadcasted_iota(jnp.int32, (LANES, LANES), 0)
        c = lax.broadcasted_iota(jnp.int32, (LANES, LANES), 1)
        tri = ((r <= c) if reverse else (r >= c)).astype(F32)

        def step(it, carry):
            b = (nb - 1 - it) if reverse else it
            off = pl.multiple_of(b * LANES, LANES)
            blk = x_ref[pl.ds(off, LANES), :]
            o_ref[pl.ds(off, LANES), :] = _dot32(tri, blk) + carry
            return carry + jnp.sum(blk, axis=0, keepdims=True)

        lax.fori_loop(0, nb, step, jnp.zeros((1, LANES), F32))

    return pl.pallas_call(body, name=name, out_shape=jax.ShapeDtypeStruct(x.shape, F32),
                          compiler_params=pltpu.CompilerParams(vmem_limit_bytes=V7X_VMEM_LIMIT))(x)


def _dot32_each(a, b, kind="nn"):
    return _each(lambda x, y: _dot32(x, y, kind), a, b)


def _unit_lower_inverse(m, ri, ci):
    c = ri.shape[0]
    t = -_keep(ri // 2 == ci // 2, m) + jnp.where(ri == ci, 1.0, 0.0)
    b = 4
    while b <= c:
        off_diag = (ri // b == ci // b) & (ri % b >= b // 2) & (ci % b < b // 2)
        t = t - _dot32_each(_dot32_each(t, _keep(off_diag, m)), t)
        b *= 2
    return t


def _dn_gates(g, ri, ci):
    eye = ri == ci
    incl = ri >= ci
    g_row = jnp.sum(jnp.where(eye, g, 0.0), axis=0, keepdims=True)
    gc = jnp.sum(jnp.where(incl, g_row, 0.0), axis=1, keepdims=True)
    gc_row = jnp.sum(jnp.where(eye, gc, 0.0), axis=0, keepdims=True)
    dmat = jnp.where(incl, jnp.exp(jnp.where(incl, gc - gc_row, 0.0)), 0.0)
    gc_last = jnp.sum(g, axis=0, keepdims=True)
    return gc, dmat, jnp.exp(gc), jnp.exp(gc_last - gc), jnp.exp(gc_last)


def _dn_fwd(qkv, act, name):
    s = qkv.shape[0]
    c, d, nh = DN_CHUNK, HEAD_DIM, N_DN_HEADS
    nc = s // c

    def body(q_ref, k_ref, v_ref, act_ref, o_ref, s_ref, t_ref, state):
        @pl.when(pl.program_id(0) == 0)
        def _():
            state[...] = jnp.zeros_like(state)

        ri = lax.broadcasted_iota(jnp.int32, (c, c), 0)
        ci = lax.broadcasted_iota(jnp.int32, (c, c), 1)
        act = act_ref[...]
        heads = range(nh)
        cols = [slice(h * d, (h + 1) * d) for h in heads]
        q, k, v = (_Each(ref[:, cs] for cs in cols) for ref in (q_ref, k_ref, v_ref))
        beta = _Each(_lane_col(act, LANE_BETA + h) for h in heads)
        g = _Each(_lane_col(act, LANE_DECAY + h) for h in heads)
        _, dmat, e, r, gl = _each(lambda gh: _dn_gates(gh, ri, ci), g)
        s0 = _Each(state[h] for h in heads)
        kb = beta * k
        t = _unit_lower_inverse(_keep(ri > ci, _dot32_each(kb, k, "nt") * dmat), ri, ci)
        vn = _dot32_each(t, beta * v) - _dot32_each(_dot32_each(t, kb * e), s0)
        o = _dot32_each(q * e, s0) + _dot32_each(_dot32_each(q, k, "nt") * dmat, vn)
        s1 = s0 * gl + _dot32_each(k * r, vn, "tn")
        for h in heads:
            o_ref[:, cols[h]] = o.vals[h]
            state[h] = s1.vals[h]
            s_ref[h] = s0.vals[h]
            t_ref[h] = t.vals[h]

    wide = lambda part: pl.BlockSpec((c, nh * d), lambda n: (n, part))
    return pl.pallas_call(
        body, name=name, grid=(nc,),
        in_specs=[wide(0), wide(1), wide(2), pl.BlockSpec((c, LANES), lambda n: (n, 0))],
        out_specs=[wide(0), pl.BlockSpec((nh, None, d, d), lambda n: (0, n, 0, 0)),
                   pl.BlockSpec((nh, None, c, c), lambda n: (0, n, 0, 0))],
        out_shape=[jax.ShapeDtypeStruct((s, nh * d), F32), jax.ShapeDtypeStruct((nh, nc, d, d), F32),
                   jax.ShapeDtypeStruct((nh, nc, c, c), F32)],
        scratch_shapes=[pltpu.VMEM((nh, d, d), F32)],
        compiler_params=_cparams("arbitrary"),
    )(qkv, qkv, qkv, act)


def _dn_bwd(qkv, act, states, tinv, do, name):
    s = qkv.shape[0]
    c, d, nh = DN_CHUNK, HEAD_DIM, N_DN_HEADS
    nc = s // c

    def chunk_bwd(q, k, v, do, beta, g, s0, t, ds_out):
        ri = lax.broadcasted_iota(jnp.int32, (c, c), 0)
        ci = lax.broadcasted_iota(jnp.int32, (c, c), 1)
        eye, incl, strict = ri == ci, ri >= ci, ri > ci
        gc, dmat, e, r, gl = _each(lambda gh: _dn_gates(gh, ri, ci), g)
        dot = _dot32_each
        rowsum = lambda x: _each(lambda a: jnp.sum(a, axis=1, keepdims=True), x)
        colsum = lambda x: _each(lambda a: jnp.sum(a, axis=0, keepdims=True), x)
        total = lambda x: colsum(rowsum(x))
        to_col = lambda row: rowsum(_keep(eye, row))
        to_row = lambda colv: colsum(_keep(eye, colv))

        kb, vb = beta * k, beta * v
        kbe = kb * e
        u, w = dot(t, vb), dot(t, kbe)
        vn = u - dot(w, s0)
        qk = dot(q, k, "nt")
        p = qk * dmat
        gram = dot(k, k, "nt")
        kr, qe = k * r, q * e

        d_kr = dot(vn, ds_out, "nt")
        dvn = dot(kr, ds_out)
        dgl = total(s0 * ds_out)
        ds_in = ds_out * gl
        dk = d_kr * r
        dr = rowsum(d_kr * k)
        d_qe = dot(do, s0, "nt")
        ds_in = ds_in + dot(qe, do, "tn")
        dp = _keep(incl, dot(do, vn, "nt"))
        dvn = dvn + dot(p, do, "tn")
        dq = d_qe * e
        de = rowsum(d_qe * q)
        dqk = dp * dmat
        dq = dq + dot(dqk, k)
        dk = dk + dot(dqk, q, "tn")
        dd = dp * qk
        dw = -dot(dvn, s0, "nt")
        ds_in = ds_in - dot(w, dvn, "tn")
        dvb = dot(t, dvn, "tn")
        dkbe = dot(t, dw, "tn")
        dm = -_keep(strict, dot(dvb, u, "nt") + dot(dkbe, w, "nt"))
        dbeta = rowsum(dm * gram * dmat)
        dgram = dm * beta * dmat
        dd = dd + dm * beta * gram
        dk = dk + dot(dgram, k) + dot(dgram, k, "tn")
        dkb = dkbe * e
        de = de + rowsum(dkbe * kb)
        dk = dk + beta * dkb
        dbeta = dbeta + rowsum(dkb * k) + rowsum(dvb * v)
        dv = beta * dvb
        wd = dd * dmat
        dgc = rowsum(wd) - to_col(colsum(wd)) + de * e - dr * r
        dgc_last = total(dr * r) + dgl * gl
        dgc = dgc + _keep(ri[:, 0:1] == c - 1, dgc_last)
        dg = rowsum(_keep(ri <= ci, to_row(dgc)))
        return dq, dk, dv, dbeta, dg, ds_in

    def body(q_ref, k_ref, v_ref, act_ref, s_ref, t_ref, do_ref, dq_ref, dk_ref, dv_ref, dact_ref, dstate):
        @pl.when(pl.program_id(0) == 0)
        def _():
            dstate[...] = jnp.zeros_like(dstate)

        act = act_ref[...]
        heads = range(nh)
        cols = [slice(h * d, (h + 1) * d) for h in heads]
        q, k, v, do = (_Each(ref[:, cs] for cs in cols) for ref in (q_ref, k_ref, v_ref, do_ref))
        dq, dk, dv, dbeta, dg, ds_in = chunk_bwd(
            q, k, v, do, _Each(_lane_col(act, LANE_BETA + h) for h in heads),
            _Each(_lane_col(act, LANE_DECAY + h) for h in heads), _Each(s_ref[h] for h in heads),
            _Each(t_ref[h] for h in heads), _Each(dstate[h] for h in heads))
        lane = lax.broadcasted_iota(jnp.int32, (c, LANES), 1)
        dact = jnp.zeros((c, LANES), F32)
        for h in heads:
            dstate[h] = ds_in.vals[h]
            dq_ref[:, cols[h]], dk_ref[:, cols[h]], dv_ref[:, cols[h]] = dq.vals[h], dk.vals[h], dv.vals[h]
            dact = (dact + jnp.where(lane == LANE_BETA + h, dbeta.vals[h], 0.0)
                    + jnp.where(lane == LANE_DECAY + h, dg.vals[h], 0.0))
        dact_ref[...] = dact

    part = lambda p: pl.BlockSpec((c, nh * d), lambda n: (nc - 1 - n, p))
    per = lambda a, b: pl.BlockSpec((nh, None, a, b), lambda n: (0, nc - 1 - n, 0, 0))
    wide = jax.ShapeDtypeStruct((s, nh * d), F32)
    act_spec = pl.BlockSpec((c, LANES), lambda n: (nc - 1 - n, 0))
    return pl.pallas_call(
        body, name=name, grid=(nc,),
        in_specs=[part(0), part(1), part(2), act_spec, per(d, d), per(c, c), part(0)],
        out_specs=[part(0), part(0), part(0), act_spec],
        out_shape=[wide, wide, wide, jax.ShapeDtypeStruct((s, LANES), F32)],
        scratch_shapes=[pltpu.VMEM((nh, d, d), F32)],
        compiler_params=_cparams("arbitrary"),
    )(qkv, qkv, qkv, act, states, tinv, do)


EVEN_DN_QKV, EVEN_FOX_QKV, EVEN_DN_GATE, EVEN_FOX_GATE, EVEN_NARROW = 0, 1536, 3072, 3584, 4096
EVEN_WIDTH = 4224
CONV_TILE = 256
CONV_HALO = 8


def _conv_fwd(proj, w, name):
    s = proj.shape[0]
    t, cw = CONV_TILE, 3 * D_DN

    def body(cur_ref, prev_ref, w_ref, y_ref, xs):
        i = pl.program_id(0)
        xs[0:CONV_HALO, :] = jnp.where(i > 0, prev_ref[...], 0.0)
        xs[CONV_HALO:, :] = cur_ref[...]
        y = jnp.zeros((t, cw), F32)
        for tap in range(CONV_WIDTH):
            y = y + w_ref[tap:tap + 1, :] * xs[pl.ds(CONV_HALO - CONV_WIDTH + 1 + tap, t), :]
        y_ref[...] = y

    per = t // CONV_HALO
    return pl.pallas_call(
        body, name=name, grid=(s // t,),
        in_specs=[pl.BlockSpec((t, cw), lambda i: (i, 0)),
                  pl.BlockSpec((CONV_HALO, cw), lambda i: (jnp.maximum(i * per - 1, 0), 0)),
                  pl.BlockSpec((CONV_WIDTH, cw), lambda i: (0, 0))],
        out_specs=pl.BlockSpec((t, cw), lambda i: (i, 0)),
        out_shape=jax.ShapeDtypeStruct((s, cw), F32),
        scratch_shapes=[pltpu.VMEM((t + CONV_HALO, cw), F32)],
        compiler_params=_cparams("parallel"),
    )(proj, proj, w)


def _conv_bwd(proj, w, dy, name):
    s = proj.shape[0]
    t, cw = CONV_TILE, 3 * D_DN
    nt = s // t

    def body(cur_ref, prev_ref, w_ref, dy_ref, nxt_ref, dx_ref, dw_ref, xs, dys):
        i = pl.program_id(0)

        @pl.when(i == 0)
        def _():
            dw_ref[...] = jnp.zeros_like(dw_ref)

        xs[0:CONV_HALO, :] = jnp.where(i > 0, prev_ref[...], 0.0)
        xs[CONV_HALO:, :] = cur_ref[...]
        dys[0:t, :] = dy_ref[...]
        dys[t:, :] = jnp.where(i < nt - 1, nxt_ref[...], 0.0)
        dy = dy_ref[...]
        dx = jnp.zeros((t, cw), F32)
        for tap in range(CONV_WIDTH):
            dx = dx + w_ref[tap:tap + 1, :] * dys[pl.ds(CONV_WIDTH - 1 - tap, t), :]
            dw_ref[tap:tap + 1, :] += jnp.sum(dy * xs[pl.ds(CONV_HALO - CONV_WIDTH + 1 + tap, t), :], axis=0,
                                              keepdims=True)
        dx_ref[...] = dx.astype(BF16)

    per = t // CONV_HALO
    last = s // CONV_HALO - 1
    return pl.pallas_call(
        body, name=name, grid=(nt,),
        in_specs=[pl.BlockSpec((t, cw), lambda i: (i, 0)),
                  pl.BlockSpec((CONV_HALO, cw), lambda i: (jnp.maximum(i * per - 1, 0), 0)),
                  pl.BlockSpec((CONV_WIDTH, cw), lambda i: (0, 0)),
                  pl.BlockSpec((t, cw), lambda i: (i, 0)),
                  pl.BlockSpec((CONV_HALO, cw), lambda i: (jnp.minimum((i + 1) * per, last), 0))],
        out_specs=[pl.BlockSpec((t, cw), lambda i: (i, 0)), pl.BlockSpec((CONV_WIDTH, cw), lambda i: (0, 0))],
        out_shape=[jax.ShapeDtypeStruct((s, cw), BF16), jax.ShapeDtypeStruct((CONV_WIDTH, cw), F32)],
        scratch_shapes=[pltpu.VMEM((t + CONV_HALO, cw), F32), pltpu.VMEM((t + CONV_HALO, cw), F32)],
        compiler_params=_cparams("arbitrary"),
    )(proj, proj, w, dy, dy)


def _heads(x, n):
    return [x[:, HEAD_DIM * h:HEAD_DIM * (h + 1)] for h in range(n)]


def _dn_pre_fwd(y, name):
    def fn(yb):
        cs = yb * _sigmoid(yb)
        out = []
        for idx, xh in enumerate(_heads(cs, 3 * N_DN_HEADS)):
            if idx < 2 * N_DN_HEADS:
                xh = xh * lax.rsqrt(jnp.sum(xh * xh, axis=-1, keepdims=True) + EPS)
                if idx < N_DN_HEADS:
                    xh = xh * ATT_SCALE
            out.append(xh)
        return (jnp.concatenate(out, axis=1),)
    return _rowwise(fn, [y], [], [(y.shape[1], F32)], [], tile=256, name=name)[0]


def _dn_pre_bwd(y, dq, dk, dv, name):
    def fn(yb, dqb, dkb, dvb):
        sg = _sigmoid(yb)
        cs = yb * sg
        dout = _heads(dqb, N_DN_HEADS) + _heads(dkb, N_DN_HEADS) + _heads(dvb, N_DN_HEADS)
        dcs = []
        for idx, (xh, dh) in enumerate(zip(_heads(cs, 3 * N_DN_HEADS), dout)):
            if idx < 2 * N_DN_HEADS:
                if idx < N_DN_HEADS:
                    dh = dh * ATT_SCALE
                r = lax.rsqrt(jnp.sum(xh * xh, axis=-1, keepdims=True) + EPS)
                xhat = xh * r
                dh = r * (dh - xhat * jnp.sum(xhat * dh, axis=-1, keepdims=True))
            dcs.append(dh)
        return (jnp.concatenate(dcs, axis=1) * _silu_grad(yb, sg),)
    return _rowwise(fn, [y, dq, dk, dv], [], [(y.shape[1], F32)], [], tile=256, name=name)[0]


def _narrow_params(a_log, dt_bias, f_bias):
    lanes = lambda a, first: jnp.pad(a.reshape(1, -1), ((0, 0), (first, LANES - first - a.shape[0])))
    return jnp.concatenate([lanes(a_log, LANE_DECAY), lanes(dt_bias, LANE_DECAY), lanes(f_bias, LANE_FORGET),
                            jnp.zeros((5, LANES), F32)], axis=0)


def _narrow_masks(shape):
    lane = lax.broadcasted_iota(jnp.int32, shape, 1)
    is_beta = lane < LANE_DECAY
    is_decay = (lane >= LANE_DECAY) & (lane < LANE_FORGET)
    is_forget = (lane >= LANE_FORGET) & (lane < LANE_FORGET + N_FOX_HEADS)
    return is_beta, is_decay, is_forget


def _narrow_fwd(proj, params, name):
    def fn(sm, pk):
        is_beta, is_decay, is_forget = _narrow_masks(sm.shape)
        g = -jnp.exp(pk[0:1, :]) * _softplus(sm + pk[1:2, :])
        logf = -_softplus(-(sm + pk[2:3, :]))
        return (jnp.where(is_beta, _sigmoid(sm), jnp.where(is_decay, g, jnp.where(is_forget, logf, 0.0))),)
    return _rowwise(fn, [(proj, LANES, EVEN_NARROW // LANES)], [params], [(LANES, F32)], [], tile=512, name=name)[0]


def _narrow_bwd(proj, params, act, dact, dlogf, name):
    def fn(sm, ab, da, dl, pk):
        is_beta, is_decay, is_forget = _narrow_masks(sm.shape)
        db = jnp.where(is_forget, dl, da)
        d_beta = db * ab * (1.0 - ab)
        d_decay = db * (-jnp.exp(pk[0:1, :])) * _sigmoid(sm + pk[1:2, :])
        d_forget = db * _sigmoid(-(sm + pk[2:3, :]))
        dsm = jnp.where(is_beta, d_beta, jnp.where(is_decay, d_decay, jnp.where(is_forget, d_forget, 0.0)))
        col = lambda x: jnp.sum(x, axis=0, keepdims=True)
        return (dsm, col(jnp.where(is_decay, db * ab, 0.0)), col(jnp.where(is_decay, dsm, 0.0)),
                col(jnp.where(is_forget, dsm, 0.0)))
    return _rowwise(fn, [(proj, LANES, EVEN_NARROW // LANES), act, dact, dlogf], [params], [(LANES, BF16)],
                    [(1, LANES)] * 3, tile=512, name=name)


def _head_rms(xh):
    r = lax.rsqrt(jnp.mean(xh * xh, axis=-1, keepdims=True) + EPS)
    return xh * r, r


def _fox_pre_fwd(proj, qg, kg, name):
    def fn(pf, qgb, kgb):
        out = []
        for idx, xh in enumerate(_heads(pf, 3 * N_FOX_HEADS)):
            if idx < 2 * N_FOX_HEADS:
                xh = _head_rms(xh)[0] * (qgb if idx < N_FOX_HEADS else kgb)
            out.append(xh)
        return (jnp.concatenate(out, axis=1),)
    return _rowwise(fn, [(proj, 3 * D_FOX, EVEN_FOX_QKV // (3 * D_FOX))], [qg, kg], [(3 * D_FOX, BF16)], [],
                    tile=256, name=name)[0]


def _fox_pre_bwd(proj, qg, kg, dq, dk, dv, name):
    def fn(pf, dqb, dkb, dvb, qgb, kgb):
        dout = _heads(dqb, N_FOX_HEADS) + _heads(dkb, N_FOX_HEADS) + _heads(dvb, N_FOX_HEADS)
        dg = [jnp.zeros((1, HEAD_DIM), F32), jnp.zeros((1, HEAD_DIM), F32)]
        dx = []
        for idx, (xh, dh) in enumerate(zip(_heads(pf, 3 * N_FOX_HEADS), dout)):
            if idx < 2 * N_FOX_HEADS:
                which = 0 if idx < N_FOX_HEADS else 1
                xhat, r = _head_rms(xh)
                dg[which] = dg[which] + jnp.sum(dh * xhat, axis=0, keepdims=True)
                dxh = dh * (qgb if which == 0 else kgb)
                dh = r * (dxh - xhat * jnp.mean(dxh * xhat, axis=-1, keepdims=True))
            dx.append(dh)
        return jnp.concatenate(dx, axis=1), dg[0], dg[1]
    return _rowwise(fn, [(proj, 3 * D_FOX, EVEN_FOX_QKV // (3 * D_FOX)), dq, dk, dv], [qg, kg],
                    [(3 * D_FOX, BF16)], [(1, HEAD_DIM)] * 2, tile=256, name=name)


def _mix_gate_fwd(proj, o_dn, o_fox, ng, name):
    def fn(gd, gf, od, of, ngb):
        dn = [_head_rms(xh)[0] * ngb for xh in _heads(od, N_DN_HEADS)]
        return (jnp.concatenate([jnp.concatenate(dn, axis=1) * gd * _sigmoid(gd), of * _sigmoid(gf)], axis=1),)
    return _rowwise(fn, [(proj, D_DN, EVEN_DN_GATE // D_DN), (proj, D_FOX, EVEN_FOX_GATE // D_FOX), o_dn, o_fox],
                    [ng], [(D_DN + D_FOX, BF16)], [], tile=256, name=name)[0]


def _mix_gate_bwd(proj, o_dn, o_fox, ng, dom, name):
    def fn(gd, gf, od, of, dm, ngb):
        d_dn, d_fox = dm[:, :D_DN], dm[:, D_DN:]
        sgd, sgf = _sigmoid(gd), _sigmoid(gf)
        don = d_dn * gd * sgd
        dng = jnp.zeros((1, HEAD_DIM), F32)
        dod, normed = [], []
        for xh, dh in zip(_heads(od, N_DN_HEADS), _heads(don, N_DN_HEADS)):
            xhat, r = _head_rms(xh)
            dng = dng + jnp.sum(dh * xhat, axis=0, keepdims=True)
            dxh = dh * ngb
            dod.append(r * (dxh - xhat * jnp.mean(dxh * xhat, axis=-1, keepdims=True)))
            normed.append(xhat * ngb)
        d_gd = d_dn * jnp.concatenate(normed, axis=1) * _silu_grad(gd, sgd)
        d_gf = d_fox * of * sgf * (1.0 - sgf)
        return jnp.concatenate(dod, axis=1), d_fox * sgf, d_gd, d_gf, dng
    return _rowwise(fn, [(proj, D_DN, EVEN_DN_GATE // D_DN), (proj, D_FOX, EVEN_FOX_GATE // D_FOX), o_dn, o_fox, dom],
                    [ng], [(D_DN, F32), (D_FOX, F32), (D_DN, BF16), (D_FOX, BF16)], [(1, HEAD_DIM)], tile=256,
                    name=name)


def _loss_grad(y, target, name):
    d = y.shape[1]

    def fn(yb, tb):
        diff = yb - tb
        part = jnp.sum(jnp.sum(diff * diff, axis=1, keepdims=True), axis=0, keepdims=True) * (0.5 / d)
        g = diff * (1.0 / d)
        return g, g, part
    return _rowwise(fn, [y, target], [], [(d, F32), (d, BF16)], [(1, 1)], tile=512, name=name)


_REF_EVEN = {"dn_qkv": (0, 1536), "dn_gate": (1536, 2048), "dn_ba": (2048, 2056), "fox_qkv": (2056, 3592),
             "fox_gate": (3592, 4104), "f_pre": (4104, 4108)}
D_IN_EVEN = 4108


def _even_to_kernel_layout(w):
    cut = lambda name: w[..., _REF_EVEN[name][0]:_REF_EVEN[name][1]]
    pad = jnp.zeros(w.shape[:-1] + (EVEN_WIDTH - EVEN_NARROW - 12,), w.dtype)
    return jnp.concatenate([cut("dn_qkv"), cut("fox_qkv"), cut("dn_gate"), cut("fox_gate"), cut("dn_ba"),
                            cut("f_pre"), pad], axis=-1)


def _even_from_kernel_layout(g):
    return jnp.concatenate([g[..., EVEN_DN_QKV:EVEN_FOX_QKV], g[..., EVEN_DN_GATE:EVEN_FOX_GATE],
                            g[..., EVEN_NARROW:EVEN_NARROW + 8], g[..., EVEN_FOX_QKV:EVEN_DN_GATE],
                            g[..., EVEN_FOX_GATE:EVEN_NARROW], g[..., EVEN_NARROW + 8:EVEN_NARROW + 12]], axis=-1)


EVEN_QUARTER = 1027
EVEN_QUARTER_PAD = 1152


def _even_grad_quarters(g):
    g = _even_from_kernel_layout(g)
    pad = [(0, 0)] * (g.ndim - 1) + [(0, EVEN_QUARTER_PAD - EVEN_QUARTER)]
    return jnp.concatenate([jnp.pad(g[..., q * EVEN_QUARTER:(q + 1) * EVEN_QUARTER], pad) for q in range(4)], axis=-1)


def _forget_rows(c):
    return c[:, LANE_FORGET:LANE_FORGET + N_FOX_HEADS].T.reshape(N_FOX_HEADS, 1, c.shape[0])


def _forget_lanes(rows):
    s = rows.shape[2]
    return jnp.pad(rows.reshape(-1, s).T, ((0, 0), (LANE_FORGET, LANES - LANE_FORGET - N_FOX_HEADS)))


def _even_fwd(x, gain, w_in, w_out, j, p, tag):
    h = _rms_fwd(x, gain, f"{tag}_norm")
    proj = _mm(h, w_in, "nn", tm=512, tn=EVEN_WIDTH // 3, out_dtype=F32, name=f"{tag}_in", b_lead=(j,))
    y = _conv_fwd(proj, p["conv_w"], f"{tag}_conv")
    dn_qkv = _dn_pre_fwd(y, f"{tag}_dn_pre")
    act = _narrow_fwd(proj, p["narrow"], f"{tag}_narrow")
    o_dn, states, tinv = _dn_fwd(dn_qkv, act, f"{tag}_delta")
    fox_qkv = _fox_pre_fwd(proj, p["q_g"], p["k_g"], f"{tag}_fox_pre")
    c = _cumsum_rows(act, False, f"{tag}_cumsum")
    ct = _forget_rows(c)
    o_fox, lse = _fox_fwd(fox_qkv, c, ct, f"{tag}_fox")
    om = _mix_gate_fwd(proj, o_dn, o_fox, p["dn_norm_g"], f"{tag}_gate")
    x2 = _mm(om, w_out, "nn", tm=512, tn=x.shape[1], out_dtype=F32, name=f"{tag}_out", residual=x, b_lead=(j,))
    return x2, (x, h, proj, y, dn_qkv, act, states, tinv, o_dn, fox_qkv, c, ct, o_fox, lse, om)


def _even_bwd(dxo, dxo16, saved, gain, w_in, w_out, j, p, tag, g_in, g_out, after=None):
    x, h, proj, y, dn_qkv, act, states, tinv, o_dn, fox_qkv, c, ct, o_fox, lse, om = saved
    d = x.shape[1]
    dom = _mm(dxo16, w_out, "nt", tm=512, tn=d, out_dtype=F32, name=f"{tag}_out_bwd", b_lead=(j,), after=after)
    g_out = _mm(om, dxo16, "tn", tm=512, tn=d, out_dtype=F32, name=f"{tag}_out_dw", into=(g_out, 0))
    d_odn, d_ofox, d_gd, d_gf, d_ng = _mix_gate_bwd(proj, o_dn, o_fox, p["dn_norm_g"], dom, f"{tag}_gate_bwd")
    dq, dk, dv, dct = _fox_bwd(fox_qkv, c, ct, o_fox, lse, d_ofox, f"{tag}_fox_bwd")
    d_fox_qkv, d_qg, d_kg = _fox_pre_bwd(proj, p["q_g"], p["k_g"], dq, dk, dv, f"{tag}_fox_pre_bwd")
    dlogf = _cumsum_rows(_forget_lanes(dct), True, f"{tag}_cumsum_bwd")
    dq, dk, dv, dact = _dn_bwd(dn_qkv, act, states, tinv, d_odn, f"{tag}_delta_bwd")
    dy = _dn_pre_bwd(y, dq, dk, dv, f"{tag}_dn_pre_bwd")
    d_dn_qkv, d_conv = _conv_bwd(proj, p["conv_w"], dy, f"{tag}_conv_bwd")
    d_narrow, s_alog, s_dt, s_fb = _narrow_bwd(proj, p["narrow"], act, dact, dlogf, f"{tag}_narrow_bwd")
    dproj = jnp.concatenate([d_dn_qkv, d_fox_qkv, d_gd, d_gf, d_narrow], axis=1)
    g_in = _mm(h, dproj, "tn", tm=512, tn=EVEN_WIDTH // 3, out_dtype=F32, name=f"{tag}_in_dw", into=(g_in, 0))
    dx, dx16, d_gain = _in_proj_bwd(dproj, w_in, j, x, dxo, gain, f"{tag}_in_bwd")
    small = {"conv_w": d_conv, "a_log": s_alog, "dt_bias": s_dt, "f_bias": s_fb, "dn_norm_g": d_ng, "q_g": d_qg,
             "k_g": d_kg}
    return dx, dx16, d_gain, small, g_in, g_out


def _odd_fwd(x, gain, w_in, w_out, j, tag):
    h = _rms_fwd(x, gain, f"{tag}_norm")
    qkv = _mm(h, w_in, "nn", tm=512, tn=w_in.shape[2] // 2, out_dtype=BF16, name=f"{tag}_in", b_lead=(j,))
    o16, o32 = _sb_fwd(qkv, N_SB_HEADS, f"{tag}_sb")
    x2 = _mm(o16, w_out, "nn", tm=512, tn=x.shape[1], out_dtype=F32, name=f"{tag}_out", residual=x, b_lead=(j,))
    return x2, (x, h, qkv, o16, o32)


def _odd_bwd(dxo, dxo16, saved, gain, w_in, w_out, j, tag, g_in, g_out, after=None):
    x, h, qkv, o16, o32 = saved
    d = x.shape[1]
    do = _mm(dxo16, w_out, "nt", tm=512, tn=d, out_dtype=BF16, name=f"{tag}_out_bwd", b_lead=(j,), after=after)
    g_out = _mm(o16, dxo16, "tn", tm=512, tn=d, out_dtype=F32, name=f"{tag}_out_dw", into=(g_out, 0))
    dq, dk, dv = _sb_bwd(qkv, o32, do, N_SB_HEADS, f"{tag}_sb_bwd")
    dqkv = jnp.concatenate([dq, dk.astype(BF16), dv.astype(BF16)], axis=1)
    g_in = _mm(h, dqkv, "tn", tm=512, tn=w_in.shape[2] // 2, out_dtype=F32, name=f"{tag}_in_dw", into=(g_in, 0))
    dx, dx16, d_gain = _in_proj_bwd(dqkv, w_in, j, x, dxo, gain, f"{tag}_in_bwd")
    return dx, dx16, d_gain, g_in, g_out


def _forward_backward(x, target, w, first, rest_after, token, on_reduced):
    depth = w["norm_ffn1"].shape[0]
    row = lambda a, l: a[l][None]
    rest = {}

    def mats(names, j):
        if j == 0 and names[0] in first:
            return [first[name] for name in names] + [0]
        return [rest[name] for name in names] + [j - (1 if names[0] in first else 0)]

    def even_small(j):
        return {"conv_w": w["dn_conv_w"][j], "narrow": _narrow_params(w["dn_a_log"][j], w["dn_dt_bias"][j],
                                                                     w["fox_f_bias"][j]),
                "dn_norm_g": row(w["dn_norm_g"], j), "q_g": row(w["fox_q_norm_g"], j),
                "k_g": row(w["fox_k_norm_g"], j)}

    saved = []
    for l in range(depth):
        if l == 1:
            rest.update(rest_after(x))
        gain = row(w["norm_ffn1"], l) + token[0:1, 0:1] if l == 0 else row(w["norm_ffn1"], l)
        x, s1 = _ffn_fwd(x, gain, *mats(("ffn1_w_gu", "ffn1_w_down"), l), "ffn1")
        if l % 2 == 0:
            x, s2 = _even_fwd(x, row(w["norm_mix"], l), *mats(("w_in_even", "w_out_even"), l // 2),
                              even_small(l // 2), "even")
        else:
            x, s2 = _odd_fwd(x, row(w["norm_mix"], l), *mats(("w_in_odd", "w_out_odd"), l // 2), "odd")
        x, s3 = _ffn_fwd(x, row(w["norm_ffn2"], l), *mats(("ffn2_w_gu", "ffn2_w_down"), l), "ffn2")
        saved.append((s1, s2, s3))

    dx, dx16, loss = _loss_grad(x, target, "loss")

    kind_of = dict(BIG)
    d_norm = {k: [None] * depth for k in ("norm_ffn1", "norm_mix", "norm_ffn2")}
    d_even = [None] * ((depth + 1) // 2)
    to_sibling, between_chips, token = None, None, None
    for l in reversed(range(depth)):
        s1, s2, s3 = saved[l]
        mixer = ("w_in_even", "w_out_even") if l % 2 == 0 else ("w_in_odd", "w_out_odd")
        names = ["ffn1_w_gu", "ffn1_w_down", *mixer, "ffn2_w_gu", "ffn2_w_down"]
        g = {name: lax.empty((1,) + rest[name].shape[1:], F32) for name in names}
        dx, dx16, d_norm["norm_ffn2"][l], g["ffn2_w_gu"], g["ffn2_w_down"] = _ffn_bwd(
            dx, dx16, s3, row(w["norm_ffn2"], l), *mats(("ffn2_w_gu", "ffn2_w_down"), l), "ffn2", g["ffn2_w_gu"],
            g["ffn2_w_down"], after=token)
        if to_sibling is not None:
            between_chips, token = _reduce_middle(to_sibling, dx)
        if l % 2 == 0:
            dx, dx16, d_norm["norm_mix"][l], d_even[l // 2], g["w_in_even"], g["w_out_even"] = _even_bwd(
                dx, dx16, s2, row(w["norm_mix"], l), *mats(("w_in_even", "w_out_even"), l // 2), even_small(l // 2),
                "even", g["w_in_even"], g["w_out_even"], after=token)
            g["w_in_even"] = _even_grad_quarters(g["w_in_even"])
        else:
            dx, dx16, d_norm["norm_mix"][l], g["w_in_odd"], g["w_out_odd"] = _odd_bwd(
                dx, dx16, s2, row(w["norm_mix"], l), *mats(("w_in_odd", "w_out_odd"), l // 2), "odd", g["w_in_odd"],
                g["w_out_odd"], after=token)
        dx, dx16, d_norm["norm_ffn1"][l], g["ffn1_w_gu"], g["ffn1_w_down"] = _ffn_bwd(
            dx, dx16, s1, row(w["norm_ffn1"], l), *mats(("ffn1_w_gu", "ffn1_w_down"), l), "ffn1", g["ffn1_w_gu"],
            g["ffn1_w_down"])
        to_sibling, token = _reduce_start([g[name] for name in names], [kind_of[name] for name in names], names,
                                          f"layer{l}")
        if between_chips is not None:
            on_reduced(l + 1, dict(zip(between_chips[-2], _reduce_finish(between_chips, dx))))
    between_chips, _ = _reduce_middle(to_sibling, dx)
    on_reduced(0, dict(zip(between_chips[-2], _reduce_finish(between_chips, dx))))

    small = {k: jnp.concatenate(v, axis=0) for k, v in d_norm.items()}
    dec = slice(LANE_DECAY, LANE_DECAY + N_DN_HEADS)
    fgt = slice(LANE_FORGET, LANE_FORGET + N_FOX_HEADS)
    small["dn_conv_w"] = jnp.stack([e["conv_w"] for e in d_even])
    small["dn_a_log"] = jnp.concatenate([e["a_log"][:, dec] for e in d_even], axis=0)
    small["dn_dt_bias"] = jnp.concatenate([e["dt_bias"][:, dec] for e in d_even], axis=0)
    small["fox_f_bias"] = jnp.concatenate([e["f_bias"][:, fgt] for e in d_even], axis=0)
    small["dn_norm_g"] = jnp.concatenate([e["dn_norm_g"] for e in d_even], axis=0)
    small["fox_q_norm_g"] = jnp.concatenate([e["q_g"] for e in d_even], axis=0)
    small["fox_k_norm_g"] = jnp.concatenate([e["k_g"] for e in d_even], axis=0)
    return loss, dx, small


MESH = pl.DeviceIdType.MESH
ANY = pl.BlockSpec(memory_space=pl.ANY)


def _place():
    x, y, c = lax.axis_index("x"), lax.axis_index("y"), lax.axis_index("c")
    return x, y, c, [(1 - x, y), (x, 1 - y), (1 - x, 1 - y)]


def _remote(src, dst, send_sem, recv_sem, to):
    return pltpu.make_async_remote_copy(src_ref=src, dst_ref=dst, send_sem=send_sem, recv_sem=recv_sem,
                                        device_id=to, device_id_type=MESH)


def _aligned(start, multiple):
    return start if isinstance(start, int) else pl.multiple_of(start, multiple)


def _quarter(ref, kind, chip, half, rows, cols):
    k = 2 * chip[0] + chip[1]
    hr = rows // 2
    assert hr % 16 == 0 and cols % LANES == 0
    if kind == "col":
        return ref.at[:, pl.ds(_aligned(half * hr, 16), hr), pl.ds(_aligned(k * cols, LANES), cols)]
    return ref.at[:, pl.ds(_aligned(k * rows + half * hr, 16), hr), :]


def _place_quarter(shard, kind, kc, name, first=0, count=None):
    l, rows, cols = shard.shape
    l = l - first if count is None else count
    tr = rows
    while tr * cols * 4 > (2 << 20) and tr % 32 == 0:
        tr //= 2
    nr = rows // tr
    if kind == "col":
        out_spec = pl.BlockSpec((None, tr, cols), lambda li, i, kc_ref: (li, i, kc_ref[0]))
        out_shape = (l, rows, 4 * cols)
    else:
        out_spec = pl.BlockSpec((None, tr, cols), lambda li, i, kc_ref: (li, kc_ref[0] * nr + i, 0))
        out_shape = (l, 4 * rows, cols)

    def body(kc_ref, x_ref, o_ref):
        o_ref[...] = x_ref[...].astype(BF16)

    return pl.pallas_call(
        body, name=name,
        grid_spec=pltpu.PrefetchScalarGridSpec(
            num_scalar_prefetch=1, grid=(l, nr),
            in_specs=[pl.BlockSpec((None, tr, cols), lambda li, i, kc_ref: (li + first, i, 0))],
            out_specs=out_spec),
        out_shape=jax.ShapeDtypeStruct(out_shape, BF16),
        compiler_params=_cparams("parallel", "parallel"),
    )(kc, shard)


def _gather_weights(wholes, kinds):
    n = len(wholes)

    def dims(ref, kind):
        _, r, cc = ref.shape
        return (r, cc // 4) if kind == "col" else (r // 4, cc)

    def body(*refs):
        bufs = refs[n:2 * n]
        send_sems, recv_sems = refs[2 * n:]
        x, y, c, chips = _place()
        sibling = (x, y, 1 - c)
        first, passed = [], []
        for t in range(n):
            rows, cols = dims(bufs[t], kinds[t])
            mine = _quarter(bufs[t], kinds[t], (x, y), c, rows, cols)
            for j, chip in enumerate(chips):
                cp = _remote(mine, mine, send_sems.at[t, j], recv_sems.at[t, j], (*chip, c))
                cp.start()
                first.append(cp)
        for j, chip in enumerate(chips):
            for t in range(n):
                rows, cols = dims(bufs[t], kinds[t])
                got = _quarter(bufs[t], kinds[t], chip, c, rows, cols)
                _remote(got, got, send_sems.at[t, j], recv_sems.at[t, j], (*chip, c)).wait_recv()
                cp = _remote(got, got, send_sems.at[t, 3 + j], recv_sems.at[t, 3 + j], sibling)
                cp.start()
                passed.append(cp)
        for j, chip in enumerate(chips):
            for t in range(n):
                rows, cols = dims(bufs[t], kinds[t])
                got = _quarter(bufs[t], kinds[t], chip, 1 - c, rows, cols)
                _remote(got, got, send_sems.at[t, 3 + j], recv_sems.at[t, 3 + j], sibling).wait_recv()
        for cp in first + passed:
            cp.wait_send()

    return pl.pallas_call(
        body, name="gather_weights", in_specs=[ANY] * n, out_specs=[ANY] * n,
        out_shape=[jax.ShapeDtypeStruct(a.shape, a.dtype) for a in wholes],
        input_output_aliases={t: t for t in range(n)},
        scratch_shapes=[pltpu.SemaphoreType.DMA((n, 6)), pltpu.SemaphoreType.DMA((n, 6))],
        compiler_params=pltpu.CompilerParams(has_side_effects=True),
    )(*wholes)


def _quarter_dims(ref, kind):
    _, r, cc = ref.shape
    return (r, cc // 4) if kind == "col" else (r // 4, cc)


def _gather_chips_copies(bufs, sems, kinds):
    x, y, c, chips = _place()
    copies = []
    for t, buf in enumerate(bufs):
        rows, cols = _quarter_dims(buf, kinds[t])
        mine = _quarter(buf, kinds[t], (x, y), c, rows, cols)
        for j, chip in enumerate(chips):
            pair = 2 * (OTHER_CHIPS * t + j)
            copies.append(_remote(mine, mine, sems[pair], sems[pair + 1], (*chip, c)))
    return copies


def _gather_start(wholes, kinds, after, tag):
    n = len(wholes)
    n_sems = 2 * OTHER_CHIPS * n
    n_in = n + len(after)

    def body(*refs):
        for cp in _gather_chips_copies(refs[:n], refs[n_in + n:n_in + n + n_sems], kinds):
            cp.start()
        refs[-1][...] = jnp.zeros_like(refs[-1])

    held = [pltpu.with_memory_space_constraint(a, pltpu.HBM) for a in wholes]
    out = pl.pallas_call(
        body, name=f"gather_start_{tag}", in_specs=[HBM] * n + [ANY] * len(after),
        out_specs=(*[HBM] * n, *[SEM] * n_sems, pl.BlockSpec(memory_space=pltpu.VMEM)),
        out_shape=(*[pltpu.HBM(a.shape, a.dtype) for a in held], *[pltpu.SemaphoreType.DMA(())] * n_sems,
                   jax.ShapeDtypeStruct((8, LANES), F32)),
        input_output_aliases={i: i for i in range(n)},
        compiler_params=pltpu.CompilerParams(has_side_effects=SPLIT_COPY),
    )(*held, *after)
    return out[n:n + n_sems], out[:n], out[-1]


def _gather_wait(sems, wholes, kinds, after, tag):
    n = len(wholes)

    def body(*refs):
        for cp in _gather_chips_copies(refs[:n], refs[n:n + len(sems)], kinds):
            cp.wait_send()
            cp.wait_recv()

    return pl.pallas_call(
        body, name=f"gather_wait_{tag}", in_specs=[HBM] * n + [SEM] * len(sems) + [ANY],
        out_specs=tuple([HBM] * n), out_shape=tuple(pltpu.HBM(a.shape, a.dtype) for a in wholes),
        input_output_aliases={i: i for i in range(n)},
        compiler_params=pltpu.CompilerParams(has_side_effects=SPLIT_COPY),
    )(*wholes, *sems, after)


def _gather_forward(wholes, kinds, tag):
    n = len(wholes)

    def body(*refs):
        bufs = refs[n:2 * n]
        send_sems, recv_sems = refs[2 * n:]
        x, y, c, chips = _place()
        copies = []
        for t in range(n):
            rows, cols = _quarter_dims(bufs[t], kinds[t])
            for j, chip in enumerate(chips):
                got = _quarter(bufs[t], kinds[t], chip, c, rows, cols)
                cp = _remote(got, got, send_sems.at[t, j], recv_sems.at[t, j], (x, y, 1 - c))
                cp.start()
                copies.append(cp)
        for cp in copies:
            cp.wait_send()
        for t in range(n):
            rows, cols = _quarter_dims(bufs[t], kinds[t])
            for j, chip in enumerate(chips):
                got = _quarter(bufs[t], kinds[t], chip, 1 - c, rows, cols)
                _remote(got, got, send_sems.at[t, j], recv_sems.at[t, j], (x, y, 1 - c)).wait_recv()

    return pl.pallas_call(
        body, name=f"gather_forward_{tag}", in_specs=[ANY] * n, out_specs=[ANY] * n,
        out_shape=[jax.ShapeDtypeStruct(a.shape, a.dtype) for a in wholes],
        input_output_aliases={t: t for t in range(n)},
        scratch_shapes=[pltpu.SemaphoreType.DMA((n, OTHER_CHIPS)), pltpu.SemaphoreType.DMA((n, OTHER_CHIPS))],
        compiler_params=pltpu.CompilerParams(has_side_effects=True),
    )(*wholes)


def _canonical(a, kind):
    l, r, c = a.shape
    return a.reshape(l, 1, r, c) if kind == "col" else a.reshape(l, 4, r // 4, c)


def _add_tile(rows, cols):
    tc = cols if cols <= 1536 else cols // 4
    tr = rows
    while tr * tc * 4 > (1 << 20) and tr % 16 == 0:
        tr //= 2
    return tr, tc


def _rs_add_sibling(part, got, c, name):
    l, a, hr, cols = got.shape
    tr, tc = _add_tile(hr, cols)
    nr = hr // tr

    def body(c_ref, p_ref, g_ref, o32_ref, o16_ref):
        s = p_ref[...] + g_ref[...]
        o32_ref[...] = s
        o16_ref[...] = s.astype(BF16)

    blk = (None, None, tr, tc)
    spec = pl.BlockSpec(blk, lambda li, ai, i, j, c_ref: (li, ai, i, j))
    return pl.pallas_call(
        body, name=name,
        grid_spec=pltpu.PrefetchScalarGridSpec(
            num_scalar_prefetch=1, grid=(l, a, nr, cols // tc),
            in_specs=[pl.BlockSpec(blk, lambda li, ai, i, j, c_ref: (li, ai, c_ref[0] * nr + i, j)), spec],
            out_specs=[spec, spec]),
        out_shape=[jax.ShapeDtypeStruct(got.shape, F32), jax.ShapeDtypeStruct(got.shape, BF16)],
        compiler_params=_cparams("parallel", "parallel", "parallel", "parallel"),
    )(c, part, got)


def _quarter4(ref, kind, chip, cols):
    k = 2 * chip[0] + chip[1]
    if kind == "col":
        return ref.at[:, :, :, pl.ds(pl.multiple_of(k * cols, LANES), cols)]
    return ref.at[:, pl.ds(k, 1), :, :]


HBM = pl.BlockSpec(memory_space=pltpu.HBM)
SEM = pl.BlockSpec(memory_space=pltpu.SEMAPHORE)
SPLIT_COPY = pltpu.SideEffectType.DATAFLOW_SIDE_EFFECTING
OTHER_CHIPS = 3


def _quarter4_shape(a, kind):
    l, _, hr, cols = a.shape
    return (l, 1, hr, cols // 4 if kind == "col" else cols)


def _rs_chips_copies(srcs, lands, sems, kinds):
    x, y, c, chips = _place()
    copies = []
    for t, (src, land) in enumerate(zip(srcs, lands)):
        cols = _quarter4_shape(src, kinds[t])[3]
        for j, chip in enumerate(chips):
            pair = 2 * (OTHER_CHIPS * t + j)
            copies.append(_remote(_quarter4(src, kinds[t], chip, cols), land.at[j], sems[pair], sems[pair + 1],
                                  (*chip, c)))
    return copies


def _split_start(copies, srcs, lands, n_sems, name):
    n = len(srcs)

    def body(*refs):
        for cp in copies(refs[:n], refs[n:2 * n], refs[4 * n:4 * n + n_sems]):
            cp.start()
        refs[-1][...] = jnp.zeros_like(refs[-1])

    held = [pltpu.with_memory_space_constraint(a, pltpu.HBM) for a in (*srcs, *lands)]
    out = pl.pallas_call(
        body, name=name, in_specs=[HBM] * (2 * n),
        out_specs=(*[HBM] * (2 * n), *[SEM] * n_sems, pl.BlockSpec(memory_space=pltpu.VMEM)),
        out_shape=(*[pltpu.HBM(a.shape, a.dtype) for a in held], *[pltpu.SemaphoreType.DMA(())] * n_sems,
                   jax.ShapeDtypeStruct((8, LANES), F32)),
        input_output_aliases={i: i for i in range(2 * n)},
        compiler_params=pltpu.CompilerParams(has_side_effects=SPLIT_COPY),
    )(*held)
    return out[2 * n:2 * n + n_sems], out[:n], out[n:2 * n], out[-1]


def _split_wait(copies, sems, srcs, lands, after, name):
    n = len(srcs)

    def body(*refs):
        for cp in copies(refs[:n], refs[n:2 * n], refs[2 * n:2 * n + len(sems)]):
            cp.wait_send()
            cp.wait_recv()

    out = pl.pallas_call(
        body, name=name, in_specs=[HBM] * (2 * n) + [SEM] * len(sems) + [ANY],
        out_specs=tuple([HBM] * (2 * n)),
        out_shape=tuple(pltpu.HBM(a.shape, a.dtype) for a in (*srcs, *lands)),
        input_output_aliases={i: i for i in range(2 * n)},
        compiler_params=pltpu.CompilerParams(has_side_effects=SPLIT_COPY),
    )(*srcs, *lands, *sems, after)
    return out[n:]


def _rs_sibling_copies(srcs, lands, sems):
    x, y, c, _ = _place()
    copies = []
    for t, (src, land) in enumerate(zip(srcs, lands)):
        hr = src.shape[2] // 2
        gives = src.at[:, :, pl.ds(pl.multiple_of((1 - c) * hr, 8), hr), :]
        copies.append(_remote(gives, land, sems[2 * t], sems[2 * t + 1], (x, y, 1 - c)))
    return copies


def _rs_add_chips(sum32, got, kind, kc, name):
    _, l, _, hr, cols = got.shape
    tr, _ = _add_tile(hr, cols)
    nr = hr // tr
    k_arr, c_arr = kc
    if kind == "col":
        own = pl.BlockSpec((None, None, tr, cols), lambda li, i, k_ref, c_ref: (li, 0, i, k_ref[0]))
    else:
        own = pl.BlockSpec((None, None, tr, cols), lambda li, i, k_ref, c_ref: (li, k_ref[0], i, 0))

    def body(k_ref, c_ref, own_ref, got_ref, o_ref):
        o_ref[...] = ((own_ref[...] + got_ref[0].astype(F32)) + got_ref[1].astype(F32)) + got_ref[2].astype(F32)

    return pl.pallas_call(
        body, name=name,
        grid_spec=pltpu.PrefetchScalarGridSpec(
            num_scalar_prefetch=2, grid=(l, nr),
            in_specs=[own, pl.BlockSpec((3, None, None, tr, cols), lambda li, i, k_ref, c_ref: (0, li, 0, i, 0))],
            out_specs=pl.BlockSpec((None, tr, cols), lambda li, i, k_ref, c_ref: (li, c_ref[0] * nr + i, 0))),
        out_shape=jax.ShapeDtypeStruct((l, 2 * hr, cols), F32),
        compiler_params=_cparams("parallel", "parallel"),
    )(k_arr, c_arr, sum32, got)


def _rs_finish(quarters):
    n = len(quarters)

    def body(*refs):
        bufs = refs[n:2 * n]
        send_sems, recv_sems = refs[2 * n:]
        x, y, c, _ = _place()
        copies = []
        for t in range(n):
            hr = bufs[t].shape[1] // 2
            mine = bufs[t].at[:, pl.ds(pl.multiple_of(c * hr, 8), hr), :]
            cp = _remote(mine, mine, send_sems.at[t], recv_sems.at[t], (x, y, 1 - c))
            cp.start()
            copies.append(cp)
        for cp in copies:
            cp.wait()

    return pl.pallas_call(
        body, name="reduce_finish", in_specs=[ANY] * n, out_specs=[ANY] * n,
        out_shape=[jax.ShapeDtypeStruct(a.shape, a.dtype) for a in quarters],
        input_output_aliases={t: t for t in range(n)},
        scratch_shapes=[pltpu.SemaphoreType.DMA((n,)), pltpu.SemaphoreType.DMA((n,))],
        compiler_params=pltpu.CompilerParams(has_side_effects=True),
    )(*quarters)


def _reduce_start(parts, kinds, names, tag):
    canon = [_canonical(p, kind) for p, kind in zip(parts, kinds)]
    lands = [lax.empty(a.shape[:2] + (a.shape[2] // 2, a.shape[3]), a.dtype) for a in canon]
    sems, srcs, lands, token = _split_start(_rs_sibling_copies, canon, lands, 2 * len(canon),
                                            f"reduce_sibling_start_{tag}")
    return (sems, srcs, lands, kinds, names, tag), token


def _reduce_middle(state, after):
    sems, srcs, lands, kinds, names, tag = state
    c_arr = jnp.reshape(lax.axis_index("c"), (1,)).astype(jnp.int32)
    from_sibling = _split_wait(_rs_sibling_copies, sems, srcs, lands, after, f"reduce_sibling_wait_{tag}")
    sums = [_rs_add_sibling(p, g, c_arr, f"reduce_add_sibling_{nm}") for p, g, nm in zip(srcs, from_sibling, names)]
    sums16 = [s16 for _, s16 in sums]
    copies = functools.partial(_rs_chips_copies, kinds=kinds)
    lands = [lax.empty((OTHER_CHIPS,) + _quarter4_shape(a, k), a.dtype) for a, k in zip(sums16, kinds)]
    sems, srcs, lands, token = _split_start(copies, sums16, lands, 2 * OTHER_CHIPS * len(sums16),
                                            f"reduce_chips_start_{tag}")
    return (sems, srcs, lands, [s32 for s32, _ in sums], kinds, names, tag), token


def _reduce_finish(state, after):
    sems, srcs, lands, sums32, kinds, names, tag = state
    x, y, c = lax.axis_index("x"), lax.axis_index("y"), lax.axis_index("c")
    kc = (jnp.reshape(2 * x + y, (1,)).astype(jnp.int32), jnp.reshape(c, (1,)).astype(jnp.int32))
    copies = functools.partial(_rs_chips_copies, kinds=kinds)
    from_chips = _split_wait(copies, sems, srcs, lands, after, f"reduce_chips_wait_{tag}")
    halves = [_rs_add_chips(s32, g, kind, kc, f"reduce_add_chips_{nm}")
              for s32, g, kind, nm in zip(sums32, from_chips, kinds, names)]
    return _rs_finish(halves)


SMALL_PEERS = 7


def _small_exchange(pack):
    rows = pack.shape[0]

    def body(p_ref, slots_ref, total_ref, send_sems, recv_sems):
        x, y, c, _ = _place()
        me = 4 * x + 2 * y + c
        slots_ref[me] = p_ref[...]
        copies = []
        for p in range(1, SMALL_PEERS + 1):
            px, py, pc = (p >> 2) & 1, (p >> 1) & 1, p & 1
            peer = (1 - x if px else x, 1 - y if py else y, 1 - c if pc else c)
            cp = _remote(p_ref, slots_ref.at[me], send_sems.at[p - 1], recv_sems.at[p - 1], peer)
            cp.start()
            copies.append(cp)
        for cp in copies:
            cp.wait()
        total = slots_ref[0]
        for i in range(1, SMALL_PEERS + 1):
            total = total + slots_ref[i]
        total_ref[...] = total

    vmem = pl.BlockSpec(memory_space=pltpu.VMEM)
    return pl.pallas_call(
        body, name="small_exchange", in_specs=[vmem], out_specs=[vmem, vmem],
        out_shape=[jax.ShapeDtypeStruct((SMALL_PEERS + 1, rows, LANES), F32), jax.ShapeDtypeStruct((rows, LANES), F32)],
        scratch_shapes=[pltpu.SemaphoreType.DMA((SMALL_PEERS,)), pltpu.SemaphoreType.DMA((SMALL_PEERS,))],
        compiler_params=pltpu.CompilerParams(has_side_effects=True),
    )(pack)


def _pack(arrays):
    rows = []
    for a in arrays:
        flat = a.reshape(-1).astype(F32)
        rows.append(jnp.pad(flat, (0, (-flat.shape[0]) % LANES)).reshape(-1, LANES))
    out = jnp.concatenate(rows, axis=0)
    return jnp.pad(out, ((0, (-out.shape[0]) % 8), (0, 0)))


def _unpack(pack, shapes):
    out, r = [], 0
    for sh in shapes:
        size = math.prod(sh)
        nr = -(-size // LANES)
        out.append(pack[r:r + nr].reshape(-1)[:size].reshape(sh))
        r += nr
    return out


def _adamw(w, g, m, v, name):
    shape = w.shape
    to2d = lambda a: a.reshape(-1, shape[-1])
    rows = math.prod(shape[:-1])
    tile = 256 if rows % 256 == 0 else rows

    def fn(wb, gb, mb, vb):
        m2 = ADAM_B1 * mb + (1.0 - ADAM_B1) * gb
        v2 = ADAM_B2 * vb + (1.0 - ADAM_B2) * (gb * gb)
        m_hat = m2 / (1.0 - ADAM_B1 ** ADAM_STEP)
        v_hat = v2 / (1.0 - ADAM_B2 ** ADAM_STEP)
        return -ADAM_LR * (m_hat / (jnp.sqrt(v_hat) + ADAM_EPS) + ADAM_WD * wb), m2, v2

    res = _rowwise(fn, [to2d(w), to2d(g), to2d(m), to2d(v)], [], [(shape[-1], F32)] * 3, [], tile=tile, name=name)
    return [r.reshape(shape) for r in res]


def _adamw_layer(w, g, m, v, layer, outs, name):
    _, rows, cols = w.shape
    tile = rows
    while tile * cols * 4 > (1 << 20) and tile % 16 == 0:
        tile //= 2

    def body(w_ref, g_ref, m_ref, v_ref, *rest):
        g_out, d_out, m_out, v_out = rest[-4:]
        gb = g_ref[...]
        m2 = ADAM_B1 * m_ref[...] + (1.0 - ADAM_B1) * gb
        v2 = ADAM_B2 * v_ref[...] + (1.0 - ADAM_B2) * (gb * gb)
        m_hat = m2 / (1.0 - ADAM_B1 ** ADAM_STEP)
        v_hat = v2 / (1.0 - ADAM_B2 ** ADAM_STEP)
        g_out[...] = gb
        d_out[...] = -ADAM_LR * (m_hat / (jnp.sqrt(v_hat) + ADAM_EPS) + ADAM_WD * w_ref[...])
        m_out[...] = m2
        v_out[...] = v2

    stacked = pl.BlockSpec((None, tile, cols), lambda i: (layer, i, 0))
    return pl.pallas_call(
        body, name=name, grid=(rows // tile,),
        in_specs=[stacked, pl.BlockSpec((None, tile, cols), lambda i: (0, i, 0)), stacked, stacked] + [ANY] * 4,
        out_specs=[stacked] * 4, out_shape=[jax.ShapeDtypeStruct(w.shape, F32)] * 4,
        input_output_aliases={4 + i: i for i in range(4)}, compiler_params=_cparams("parallel"),
    )(w, g, m, v, *outs)


BIG = (("ffn1_w_gu", "col"), ("ffn1_w_down", "row"), ("w_in_even", "col"), ("w_out_even", "row"),
       ("w_in_odd", "col"), ("w_out_odd", "row"), ("ffn2_w_gu", "col"), ("ffn2_w_down", "row"))
SMALL = ("norm_ffn1", "norm_mix", "dn_conv_w", "dn_a_log", "dn_dt_bias", "dn_norm_g", "fox_q_norm_g", "fox_k_norm_g",
         "fox_f_bias", "norm_ffn2")
WEIGHTS = ("norm_ffn1", "ffn1_w_gu", "ffn1_w_down", "norm_mix", "w_in_even", "dn_conv_w", "dn_a_log", "dn_dt_bias",
           "dn_norm_g", "fox_q_norm_g", "fox_k_norm_g", "fox_f_bias", "w_out_even", "w_in_odd", "w_out_odd",
           "norm_ffn2", "ffn2_w_gu", "ffn2_w_down")


def _step(x, target, w, m, v):
    k = 2 * lax.axis_index("x") + lax.axis_index("y")
    n_conv = w["dn_conv_w"].shape[2]

    kc = jnp.reshape(k, (1,)).astype(jnp.int32)
    kinds = dict(BIG)
    quarters = {name: w[name] for name in kinds}
    quarters["w_in_even"] = jnp.pad(w["w_in_even"], ((0, 0), (0, 0), (0, EVEN_QUARTER_PAD - EVEN_QUARTER)))
    first_names = [name for name in kinds if name not in ("w_in_odd", "w_out_odd")]
    rest_names = list(kinds)

    def even_columns(whole):
        padded = whole["w_in_even"]
        ref_order = jnp.concatenate([padded[..., q * EVEN_QUARTER_PAD:q * EVEN_QUARTER_PAD + EVEN_QUARTER]
                                     for q in range(4)], axis=-1)
        return {**whole, "w_in_even": _even_to_kernel_layout(ref_order)}

    conv_slots, _ = _small_exchange(_pack([w["dn_conv_w"]]))
    placed = [_place_quarter(quarters[name], kinds[name], kc, f"place_first_{name}", 0, 1) for name in first_names]
    gathered = _gather_weights(placed, [kinds[name] for name in first_names])
    first = even_columns(dict(zip(first_names, gathered)))
    placed = [_place_quarter(quarters[name], kinds[name], kc, f"place_rest_{name}", 1 if name in first_names else 0)
              for name in rest_names]
    rest_kinds = [kinds[name] for name in rest_names]
    sems, on_their_way, token = _gather_start(placed, rest_kinds, [conv_slots, *gathered], "rest")

    def rest_after(value):
        landed = _gather_wait(sems, on_their_way, rest_kinds, value, "rest")
        return even_columns(dict(zip(rest_names, _gather_forward(landed, rest_kinds, "rest"))))

    whole = {}
    conv_rows = math.prod(w["dn_conv_w"].shape) // LANES
    conv_quarters = [conv_slots[2 * q, :conv_rows].reshape(w["dn_conv_w"].shape) for q in range(4)]
    whole["dn_conv_w"] = jnp.concatenate(conv_quarters, axis=-1)
    for name in SMALL:
        if name != "dn_conv_w":
            whole[name] = w[name]

    updated = {name: [lax.empty(w[name].shape, F32) for _ in range(4)] for name in kinds}

    def on_reduced(layer, layer_grads):
        for name, g in layer_grads.items():
            if name == "w_in_even":
                g = g[..., :EVEN_QUARTER]
            stacked_layer = layer if w[name].shape[0] == w["norm_mix"].shape[0] else layer // 2
            updated[name] = _adamw_layer(w[name], g, m[name], v[name], stacked_layer, updated[name], f"adamw_{name}")

    loss, dx, small = _forward_backward(x, target, whole, first, rest_after, token, on_reduced)

    _, small_sum = _small_exchange(_pack([small[n] for n in SMALL]))
    grads = dict(zip(SMALL, _unpack(small_sum, [small[n].shape for n in SMALL])))
    grads["dn_conv_w"] = lax.dynamic_slice_in_dim(grads["dn_conv_w"], k * n_conv, n_conv, axis=2)
    delta, new_m, new_v = {}, {}, {}
    for name in kinds:
        grads[name], delta[name], new_m[name], new_v[name] = updated[name]
    packs = [_pack([d[n] for n in SMALL]) for d in (w, grads, m, v)]
    shapes = [w[n].shape for n in SMALL]
    for out, res in zip((delta, new_m, new_v), _adamw(*packs, "adamw_small")):
        out.update(zip(SMALL, _unpack(res, shapes)))
    total_loss = lax.psum(loss[0, 0], ("x", "y", "c"))
    return total_loss, dx, grads, delta, new_m, new_v


def kernel(x, norm_ffn1, ffn1_w_gu, ffn1_w_down, norm_mix, w_in_even, dn_conv_w, dn_a_log, dn_dt_bias, dn_norm_g, fox_q_norm_g, fox_k_norm_g, fox_f_bias, w_out_even, w_in_odd, w_out_odd, norm_ffn2, ffn2_w_gu, ffn2_w_down, loss_target, m_norm_ffn1, m_ffn1_w_gu, m_ffn1_w_down, m_norm_mix, m_w_in_even, m_dn_conv_w, m_dn_a_log, m_dn_dt_bias, m_dn_norm_g, m_fox_q_norm_g, m_fox_k_norm_g, m_fox_f_bias, m_w_out_even, m_w_in_odd, m_w_out_odd, m_norm_ffn2, m_ffn2_w_gu, m_ffn2_w_down, v_norm_ffn1, v_ffn1_w_gu, v_ffn1_w_down, v_norm_mix, v_w_in_even, v_dn_conv_w, v_dn_a_log, v_dn_dt_bias, v_dn_norm_g, v_fox_q_norm_g, v_fox_k_norm_g, v_fox_f_bias, v_w_out_even, v_w_in_odd, v_w_out_odd, v_norm_ffn2, v_ffn2_w_gu, v_ffn2_w_down):
    w = dict(zip(WEIGHTS, (norm_ffn1, ffn1_w_gu, ffn1_w_down, norm_mix, w_in_even, dn_conv_w, dn_a_log, dn_dt_bias,
                           dn_norm_g, fox_q_norm_g, fox_k_norm_g, fox_f_bias, w_out_even, w_in_odd, w_out_odd,
                           norm_ffn2, ffn2_w_gu, ffn2_w_down)))
    m = dict(zip(WEIGHTS, (m_norm_ffn1, m_ffn1_w_gu, m_ffn1_w_down, m_norm_mix, m_w_in_even, m_dn_conv_w, m_dn_a_log,
                           m_dn_dt_bias, m_dn_norm_g, m_fox_q_norm_g, m_fox_k_norm_g, m_fox_f_bias, m_w_out_even,
                           m_w_in_odd, m_w_out_odd, m_norm_ffn2, m_ffn2_w_gu, m_ffn2_w_down)))
    v = dict(zip(WEIGHTS, (v_norm_ffn1, v_ffn1_w_gu, v_ffn1_w_down, v_norm_mix, v_w_in_even, v_dn_conv_w, v_dn_a_log,
                           v_dn_dt_bias, v_dn_norm_g, v_fox_q_norm_g, v_fox_k_norm_g, v_fox_f_bias, v_w_out_even,
                           v_w_in_odd, v_w_out_odd, v_norm_ffn2, v_ffn2_w_gu, v_ffn2_w_down)))
    loss, dx, grads, delta, new_m, new_v = _step(x[0], loss_target[0], w, m, v)
    return (loss, dx[None], *[grads[n] for n in WEIGHTS], *[delta[n] for n in WEIGHTS],
            *[new_m[n] for n in WEIGHTS], *[new_v[n] for n in WEIGHTS])
```

```python
import functools
import math

import jax
import jax.numpy as jnp
from jax import lax
from jax.experimental import pallas as pl
from jax.experimental.pallas import tpu as pltpu

F32 = jnp.float32
BF16 = jnp.bfloat16
HI = lax.Precision.HIGH

HEAD_DIM = 128
N_DN_HEADS = 4
N_FOX_HEADS = 4
N_SB_HEADS = 8
D_DN = N_DN_HEADS * HEAD_DIM
D_FOX = N_FOX_HEADS * HEAD_DIM
CONV_WIDTH = 4
DN_CHUNK = 64
EPS = 1e-6
ATT_SCALE = HEAD_DIM ** -0.5
ADAM_LR, ADAM_B1, ADAM_B2, ADAM_EPS, ADAM_WD, ADAM_STEP = 0.001, 0.9, 0.999, 1e-08, 0.01, 10

V7X_VMEM_LIMIT = 56 * 1024 * 1024
LANES = 128
ATT_TQ = 256
ATT_TK = 128
ATT_SUB = ATT_TQ // ATT_TK

LANE_BETA, LANE_DECAY, LANE_FORGET = 0, 4, 8


def _cparams(*sem):
    return pltpu.CompilerParams(dimension_semantics=sem, vmem_limit_bytes=V7X_VMEM_LIMIT)


def _sigmoid(x):
    return 1.0 / (1.0 + jnp.exp(-x))


def _softplus(x):
    return jnp.maximum(x, 0.0) + jnp.log(1.0 + jnp.exp(-jnp.abs(x)))


def _silu_grad(y, sg):
    return sg * (1.0 + y * (1.0 - sg))


def _rowwise(fn, rows, bcast, outs, sums, *, tile, name):
    rows = [r if isinstance(r, tuple) else (r, r.shape[1], 0) for r in rows]
    s = rows[0][0].shape[0]
    assert s % tile == 0
    n_in, n_b, n_out, n_sum = len(rows), len(bcast), len(outs), len(sums)

    def body(*refs):
        ins = [r[...] for r in refs[:n_in + n_b]]
        res = fn(*ins)
        if not isinstance(res, (tuple, list)):
            res = (res,)
        out_refs = refs[n_in + n_b:n_in + n_b + n_out]
        sum_refs = refs[n_in + n_b + n_out:]
        for o_ref, val in zip(out_refs, res[:n_out]):
            o_ref[...] = val.astype(o_ref.dtype)
        if n_sum:
            @pl.when(pl.program_id(0) == 0)
            def _():
                for s_ref in sum_refs:
                    s_ref[...] = jnp.zeros_like(s_ref)
            for s_ref, val in zip(sum_refs, res[n_out:]):
                s_ref[...] += val

    in_specs = [pl.BlockSpec((tile, w), lambda i, cb=cb: (i, cb)) for _, w, cb in rows]
    in_specs += [pl.BlockSpec(b.shape, lambda i, nd=b.ndim: (0,) * nd) for b in bcast]
    out_specs = [pl.BlockSpec((tile, c), lambda i: (i, 0)) for c, _ in outs]
    out_specs += [pl.BlockSpec(sh, lambda i: (0, 0)) for sh in sums]
    out_shape = [jax.ShapeDtypeStruct((s, c), dt) for c, dt in outs]
    out_shape += [jax.ShapeDtypeStruct(sh, F32) for sh in sums]
    return pl.pallas_call(
        body, name=name, grid=(s // tile,), in_specs=in_specs, out_specs=out_specs, out_shape=out_shape,
        compiler_params=_cparams("arbitrary" if n_sum else "parallel"),
    )(*[r[0] for r in rows], *bcast)


def _rms_fwd(x, gain, name):
    def fn(xb, g):
        r = lax.rsqrt(jnp.mean(xb * xb, axis=-1, keepdims=True) + EPS)
        return (xb * r * g,)
    return _rowwise(fn, [x], [gain], [(x.shape[1], BF16)], [], tile=512, name=name)[0]


_DIMS = {"nn": (((1,), (0,)), ((), ())), "nt": (((1,), (1,)), ((), ())), "tn": (((0,), (0,)), ((), ()))}


def _dot(a, b, kind):
    return lax.dot_general(a.astype(BF16), b.astype(BF16), _DIMS[kind], preferred_element_type=F32)


def _dot32(a, b, kind="nn"):
    return lax.dot_general(a, b, _DIMS[kind], precision=HI, preferred_element_type=F32)


def _mm(a, b, kind, *, tm, tn, out_dtype, name, scale=None, residual=None, a_lead=(), b_lead=(),
        b_spec=None, n=None, into=None, after=None):
    ash, bsh = a.shape[len(a_lead):], b.shape[len(b_lead):]
    m = ash[1] if kind == "tn" else ash[0]
    k = ash[0] if kind == "tn" else ash[1]
    if b_spec is None:
        n = bsh[0] if kind == "nt" else bsh[1]
        assert k == (bsh[1] if kind == "nt" else bsh[0]), (ash, bsh, kind)
    assert m % tm == 0 and n % tn == 0, (m, tm, n, tn)
    la, lb = (None,) * len(a_lead), (None,) * len(b_lead)
    if kind == "tn":
        a_spec = pl.BlockSpec(la + (k, tm), lambda j, i: a_lead + (0, i))
    else:
        a_spec = pl.BlockSpec(la + (tm, k), lambda j, i: a_lead + (i, 0))
    if b_spec is None:
        if kind == "nt":
            b_spec = pl.BlockSpec(lb + (tn, k), lambda j, i: b_lead + (j, 0))
        else:
            b_spec = pl.BlockSpec(lb + (k, tn), lambda j, i: b_lead + (0, j))
    in_specs, args = [a_spec, b_spec], [a, b]
    if residual is not None:
        in_specs.append(pl.BlockSpec((tm, tn), lambda j, i: (i, j)))
        args.append(residual)
    aliases = {}
    if after is not None:
        in_specs.append(pl.BlockSpec(memory_space=pl.ANY))
        args.append(after)
    if into is not None:
        buf, layer = into
        in_specs.append(pl.BlockSpec(memory_space=pl.ANY))
        args.append(buf)
        aliases = {len(args) - 1: 0}
        out_spec = pl.BlockSpec((None, tm, tn), lambda j, i: (layer, i, j))
        out_shape = jax.ShapeDtypeStruct(buf.shape, buf.dtype)
    else:
        out_spec = pl.BlockSpec((tm, tn), lambda j, i: (i, j))
        out_shape = jax.ShapeDtypeStruct((m, n), out_dtype)

    def body(a_ref, b_ref, *rest):
        acc = _dot(a_ref[...], b_ref[...], kind)
        if scale is not None:
            acc = acc * scale
        if residual is not None:
            acc = acc + rest[0][...]
        rest[-1][...] = acc.astype(rest[-1].dtype)

    return pl.pallas_call(
        body, name=name, grid=(n // tn, m // tm), in_specs=in_specs, out_specs=out_spec, out_shape=out_shape,
        input_output_aliases=aliases, compiler_params=_cparams("parallel", "parallel"),
    )(*args)


def _ffn_up(n, w_gu, layer, name):
    s, d = n.shape
    f = w_gu.shape[2] // 2
    tm, tn = 512, f // 2
    nj = f // tn

    def body(n_ref, wg_ref, wu_ref, gu_ref, a_ref):
        nv = n_ref[...]
        g = _dot(nv, wg_ref[...], "nn")
        u = _dot(nv, wu_ref[...], "nn")
        gu_ref[0] = g.astype(BF16)
        gu_ref[1] = u.astype(BF16)
        a_ref[...] = (g * _sigmoid(g) * u).astype(BF16)

    return pl.pallas_call(
        body, name=name, grid=(nj, s // tm),
        in_specs=[pl.BlockSpec((tm, d), lambda j, i: (i, 0)),
                  pl.BlockSpec((None, d, tn), lambda j, i: (layer, 0, j)),
                  pl.BlockSpec((None, d, tn), lambda j, i: (layer, 0, j + nj))],
        out_specs=[pl.BlockSpec((2, tm, tn), lambda j, i: (0, i, j)),
                   pl.BlockSpec((tm, tn), lambda j, i: (i, j))],
        out_shape=[jax.ShapeDtypeStruct((2, s, f), BF16), jax.ShapeDtypeStruct((s, f), BF16)],
        compiler_params=_cparams("parallel", "parallel"),
    )(n, w_gu, w_gu)


def _ffn_down_bwd(dxo, w_down, gu, layer, name, after=None):
    s, d = dxo.shape
    f = w_down.shape[1]
    tm, tn = 512, f // 2
    extra_specs, extra = ([ANY], [after]) if after is not None else ([], [])

    def body(dx_ref, w_ref, gu_ref, *rest):
        dgu_ref = rest[-1]
        da = 0.5 * _dot(dx_ref[...], w_ref[...], "nt")
        g = gu_ref[0].astype(F32)
        u = gu_ref[1].astype(F32)
        sg = _sigmoid(g)
        dgu_ref[0] = (da * u * _silu_grad(g, sg)).astype(BF16)
        dgu_ref[1] = (da * g * sg).astype(BF16)

    return pl.pallas_call(
        body, name=name, grid=(f // tn, s // tm),
        in_specs=[pl.BlockSpec((tm, d), lambda j, i: (i, 0)),
                  pl.BlockSpec((None, tn, d), lambda j, i: (layer, j, 0)),
                  pl.BlockSpec((2, tm, tn), lambda j, i: (0, i, j))] + extra_specs,
        out_specs=pl.BlockSpec((2, tm, tn), lambda j, i: (0, i, j)),
        out_shape=jax.ShapeDtypeStruct((2, s, f), BF16),
        compiler_params=_cparams("parallel", "parallel"),
    )(dxo, w_down, gu, *extra)


NORM_BWD_TM = 256


def _norm_bwd_after(terms, operands, specs, x, dres, gain, name):
    s, d = x.shape
    tm = NORM_BWD_TM
    n_op = len(operands)

    def body(*refs):
        x_ref, dres_ref, g_ref = refs[n_op:n_op + 3]
        dx_ref, dx16_ref, dgain_ref = refs[n_op + 3:]
        dn = None
        for a, b in terms(*refs[:n_op]):
            dn = _dot(a, b, "nt") if dn is None else dn + _dot(a, b, "nt")
        xb = x_ref[...]
        r = lax.rsqrt(jnp.mean(xb * xb, axis=-1, keepdims=True) + EPS)
        xh = xb * r
        dxh = dn * g_ref[...]
        dx = dres_ref[...] + r * (dxh - xh * jnp.mean(dxh * xh, axis=-1, keepdims=True))
        dx_ref[...] = dx
        dx16_ref[...] = dx.astype(BF16)

        @pl.when(pl.program_id(0) == 0)
        def _():
            dgain_ref[...] = jnp.zeros_like(dgain_ref)
        dgain_ref[...] += jnp.sum(dn * xh, axis=0, keepdims=True)

    rows = pl.BlockSpec((tm, d), lambda i: (i, 0))
    return pl.pallas_call(
        body, name=name, grid=(s // tm,),
        in_specs=list(specs) + [rows, rows, pl.BlockSpec((1, d), lambda i: (0, 0))],
        out_specs=[rows, rows, pl.BlockSpec((1, d), lambda i: (0, 0))],
        out_shape=[jax.ShapeDtypeStruct((s, d), F32), jax.ShapeDtypeStruct((s, d), BF16),
                   jax.ShapeDtypeStruct((1, d), F32)],
        compiler_params=_cparams("arbitrary"),
    )(*operands, x, dres, gain)


def _ffn_up_bwd(dgu, w_gu, layer, x, dres, gain, name):
    _, s, f = dgu.shape
    d = w_gu.shape[1]
    specs = [pl.BlockSpec((2, NORM_BWD_TM, f), lambda i: (0, i, 0)),
             pl.BlockSpec((None, d, f), lambda i: (layer, 0, 0)),
             pl.BlockSpec((None, d, f), lambda i: (layer, 0, 1))]
    terms = lambda dgu_ref, wg_ref, wu_ref: [(dgu_ref[0], wg_ref[...]), (dgu_ref[1], wu_ref[...])]
    return _norm_bwd_after(terms, [dgu, w_gu, w_gu], specs, x, dres, gain, name)


def _in_proj_bwd(dproj, w_in, j, x, dres, gain, name):
    k = dproj.shape[1]
    d = w_in.shape[1]
    specs = [pl.BlockSpec((NORM_BWD_TM, k), lambda i: (i, 0)), pl.BlockSpec((None, d, k), lambda i: (j, 0, 0))]
    terms = lambda a_ref, b_ref: [(a_ref[...], b_ref[...])]
    return _norm_bwd_after(terms, [dproj, w_in], specs, x, dres, gain, name)


def _ffn_fwd(x, gain, w_gu, w_down, layer, tag):
    n = _rms_fwd(x, gain, f"{tag}_norm")
    gu, a = _ffn_up(n, w_gu, layer, f"{tag}_up")
    x2 = _mm(a, w_down, "nn", tm=512, tn=x.shape[1], out_dtype=F32, name=f"{tag}_down", scale=0.5, residual=x,
             b_lead=(layer,))
    return x2, (x, n, gu, a)


def _ffn_bwd(dxo, dxo16, saved, gain, w_gu, w_down, layer, tag, g_gu, g_down, after=None):
    x, n, gu, a = saved
    s, f = a.shape
    dgu = _ffn_down_bwd(dxo16, w_down, gu, layer, f"{tag}_down_bwd", after)
    g_down = _mm(a, dxo16, "tn", tm=256, tn=dxo16.shape[1], out_dtype=F32, name=f"{tag}_down_dw", scale=0.5,
                 into=(g_down, 0))
    tn = f // 2
    nj = f // tn
    g_gu = _mm(n, dgu, "tn", tm=512, tn=tn, out_dtype=F32, name=f"{tag}_up_dw", into=(g_gu, 0), n=2 * f,
               b_spec=pl.BlockSpec((None, s, tn), lambda j, i: (j // nj, 0, j % nj)))
    dx, dx16, dgain = _ffn_up_bwd(dgu, w_gu, layer, x, dxo, gain, f"{tag}_up_bwd")
    return dx, dx16, dgain, g_gu, g_down


def _lane_col(blk, lane):
    li = lax.broadcasted_iota(jnp.int32, blk.shape, 1)
    return jnp.sum(jnp.where(li == lane, blk, 0.0), axis=1, keepdims=True)


def _split_dot(x, tri):
    hi = x.astype(BF16)
    lo = (x - hi.astype(F32)).astype(BF16)
    return (lax.dot_general(hi, tri, _DIMS["nn"], preferred_element_type=F32)
            + lax.dot_general(lo, tri, _DIMS["nn"], preferred_element_type=F32))


class _Each:
    def __init__(self, vals):
        self.vals = list(vals)

    def _with(self, other, op):
        others = other.vals if isinstance(other, _Each) else [other] * len(self.vals)
        return _Each(op(a, b) for a, b in zip(self.vals, others))

    def __add__(self, other):
        return self._with(other, lambda a, b: a + b)

    def __sub__(self, other):
        return self._with(other, lambda a, b: a - b)

    def __mul__(self, other):
        return self._with(other, lambda a, b: a * b)

    def __neg__(self):
        return _Each(-a for a in self.vals)


def _each(fn, *args):
    n = max(len(a.vals) for a in args if isinstance(a, _Each))
    res = [fn(*xs) for xs in zip(*[a.vals if isinstance(a, _Each) else [a] * n for a in args])]
    if isinstance(res[0], tuple):
        return tuple(_Each(r) for r in zip(*res))
    return _Each(res)


def _keep(cond, x):
    return _each(lambda v: jnp.where(cond, v, 0.0), x)


def _rowsum(x):
    return _each(lambda v: jnp.sum(v, axis=1, keepdims=True), x)


ATT_HEADS = 2
ATT_WIDTH = ATT_HEADS * HEAD_DIM
_HEAD_COLS = [slice(h * HEAD_DIM, (h + 1) * HEAD_DIM) for h in range(ATT_HEADS)]


def _att_specs(n_heads, s):
    groups = n_heads // ATT_HEADS
    q_spec = pl.BlockSpec((ATT_TQ, ATT_WIDTH), lambda g, i: (i, g))
    k_spec = pl.BlockSpec((s, ATT_WIDTH), lambda g, i: (0, groups + g))
    v_spec = pl.BlockSpec((s, ATT_WIDTH), lambda g, i: (0, 2 * groups + g))
    return q_spec, k_spec, v_spec


def _heads_of(ref, rows=None):
    return _Each(ref[:, cs] if rows is None else ref[rows, cs] for cs in _HEAD_COLS)


def _dot_each(a, b, kind):
    return _each(lambda x, y: _dot(x, y, kind), a, b)


def _att_iotas():
    row = lax.broadcasted_iota(jnp.int32, (ATT_TQ, ATT_TK), 0)
    col = lax.broadcasted_iota(jnp.int32, (ATT_TQ, ATT_TK), 1)
    jr = lax.broadcasted_iota(jnp.int32, (ATT_TK, ATT_TK), 0)
    jc = lax.broadcasted_iota(jnp.int32, (ATT_TK, ATT_TK), 1)
    return row, col, jr, jc


def _sb_fwd(qkv, n_heads, name):
    s = qkv.shape[0]

    def body(q_ref, k_ref, v_ref, o16_ref, o32_ref):
        i = pl.program_id(1)
        q = _heads_of(q_ref)
        row, col, jr, jc = _att_iotas()
        later = (jr > jc).astype(BF16)

        def step(jb, carry, diagonal):
            c_sp, acc = (_Each(part) for part in carry)
            work = []
            for sub in reversed(range(ATT_SUB)):
                keys = pl.ds(pl.multiple_of(jb * ATT_TQ + sub * ATT_TK, ATT_TK), ATT_TK)
                z = _dot_each(q, _heads_of(k_ref, keys), "nt") * ATT_SCALE
                sp = _each(_softplus, z)
                before = (col + sub * ATT_TK) < row if diagonal else None
                spm = _keep(before, sp) if diagonal else sp
                work.append((keys, z - sp, spm, _each(lambda x: _dot(x, later, "nn"), spm), before))
            for keys, logsig, spm, within, before in work:
                a = _each(jnp.exp, logsig - (c_sp + within))
                if diagonal:
                    a = _keep(before, a)
                acc = acc + _each(_split_dot, a, _heads_of(v_ref, keys))
                c_sp = c_sp + _rowsum(spm)
            return tuple(c_sp.vals), tuple(acc.vals)

        zeros = lambda width: tuple(jnp.zeros((ATT_TQ, width), F32) for _ in range(ATT_HEADS))
        carry = step(i, (zeros(1), zeros(HEAD_DIM)), True)
        _, acc = lax.fori_loop(0, i, lambda it, cr: step(i - 1 - it, cr, False), carry)
        for cs, acc_h in zip(_HEAD_COLS, acc):
            o16_ref[:, cs] = acc_h.astype(BF16)
            o32_ref[:, cs] = acc_h

    q_spec, k_spec, v_spec = _att_specs(n_heads, s)
    o_spec = pl.BlockSpec((ATT_TQ, ATT_WIDTH), lambda g, i: (i, g))
    return pl.pallas_call(
        body, name=name, grid=(n_heads // ATT_HEADS, s // ATT_TQ), in_specs=[q_spec, k_spec, v_spec],
        out_specs=[o_spec, o_spec],
        out_shape=[jax.ShapeDtypeStruct((s, n_heads * HEAD_DIM), BF16),
                   jax.ShapeDtypeStruct((s, n_heads * HEAD_DIM), F32)],
        compiler_params=_cparams("parallel", "arbitrary"),
    )(qkv, qkv, qkv)


def _sb_bwd(qkv, o32, do, n_heads, name):
    s = qkv.shape[0]

    def body(q_ref, k_ref, v_ref, o_ref, do_ref, dq_ref, dk_ref, dv_ref):
        i = pl.program_id(1)

        @pl.when(i == 0)
        def _():
            dk_ref[...] = jnp.zeros_like(dk_ref)
            dv_ref[...] = jnp.zeros_like(dv_ref)

        q, do = _heads_of(q_ref), _heads_of(do_ref)
        total = _rowsum(_each(lambda a, b: a.astype(F32) * b, do, _heads_of(o_ref)))
        row, col, jr, jc = _att_iotas()
        later = (jr > jc).astype(BF16)
        not_before = (jr >= jc).astype(BF16)

        def step(jb, carry, diagonal):
            c_sp, c_e, dq = (_Each(part) for part in carry)
            work = []
            for sub in reversed(range(ATT_SUB)):
                keys = pl.ds(pl.multiple_of(jb * ATT_TQ + sub * ATT_TK, ATT_TK), ATT_TK)
                k = _heads_of(k_ref, keys)
                z = _dot_each(q, k, "nt") * ATT_SCALE
                sp = _each(_softplus, z)
                before = (col + sub * ATT_TK) < row if diagonal else None
                spm = _keep(before, sp) if diagonal else sp
                work.append((keys, k, _each(jnp.exp, z - sp), spm, _each(lambda x: _dot(x, later, "nn"), spm),
                             _dot_each(do, _heads_of(v_ref, keys), "nt"), before))
            for keys, k, sig, spm, within, da, before in work:
                a = sig * _each(lambda x: jnp.exp(-x), c_sp + within)
                if diagonal:
                    a = _keep(before, a)
                e = a * da
                left = total - c_e - _each(lambda x: _split_dot(x, not_before), e)
                dz = (e - (e + left) * sig) * ATT_SCALE
                if diagonal:
                    dz = _keep(before, dz)
                dk, dv = _dot_each(dz, q, "tn"), _dot_each(a, do, "tn")
                for cs, dk_h, dv_h in zip(_HEAD_COLS, dk.vals, dv.vals):
                    dk_ref[keys, cs] += dk_h
                    dv_ref[keys, cs] += dv_h
                dq = dq + _dot_each(dz, k, "nn")
                c_sp = c_sp + _rowsum(spm)
                c_e = c_e + _rowsum(e)
            return tuple(c_sp.vals), tuple(c_e.vals), tuple(dq.vals)

        zeros = lambda width: tuple(jnp.zeros((ATT_TQ, width), F32) for _ in range(ATT_HEADS))
        carry = step(i, (zeros(1), zeros(1), zeros(HEAD_DIM)), True)
        _, _, dq = lax.fori_loop(0, i, lambda it, cr: step(i - 1 - it, cr, False), carry)
        for cs, dq_h in zip(_HEAD_COLS, dq):
            dq_ref[:, cs] = dq_h.astype(BF16)

    q_spec, k_spec, v_spec = _att_specs(n_heads, s)
    blk = pl.BlockSpec((ATT_TQ, ATT_WIDTH), lambda g, i: (i, g))
    full = pl.BlockSpec((s, ATT_WIDTH), lambda g, i: (0, g))
    wide = (s, n_heads * HEAD_DIM)
    return pl.pallas_call(
        body, name=name, grid=(n_heads // ATT_HEADS, s // ATT_TQ), in_specs=[q_spec, k_spec, v_spec, blk, blk],
        out_specs=[blk, full, full],
        out_shape=[jax.ShapeDtypeStruct(wide, BF16), jax.ShapeDtypeStruct(wide, F32), jax.ShapeDtypeStruct(wide, F32)],
        compiler_params=_cparams("parallel", "arbitrary"),
    )(qkv, qkv, qkv, o32, do)


def _fox_logits(q, k, cq, ct_ref, keys):
    ck = _Each(ct_ref[h, :, keys] for h in range(ATT_HEADS))
    return _dot_each(q, k, "nt") * ATT_SCALE + (cq - ck)


def _fox_cq(c_ref, group):
    c = c_ref[...]
    return _Each(_lane_col(c, LANE_FORGET + group * ATT_HEADS + h) for h in range(ATT_HEADS))


def _fox_fwd(qkv, c, ct, name):
    s = qkv.shape[0]
    n_heads = N_FOX_HEADS

    def body(q_ref, k_ref, v_ref, c_ref, ct_ref, o_ref, lse_ref):
        g, i = pl.program_id(0), pl.program_id(1)
        q = _heads_of(q_ref)
        cq = _fox_cq(c_ref, g)
        row, col, _, _ = _att_iotas()

        def step(jb, carry, diagonal):
            m, l, acc = (_Each(part) for part in carry)
            work = []
            m_new = m
            for sub in range(ATT_SUB):
                keys = pl.ds(pl.multiple_of(jb * ATT_TQ + sub * ATT_TK, ATT_TK), ATT_TK)
                sc = _fox_logits(q, _heads_of(k_ref, keys), cq, ct_ref, keys)
                valid = (col + sub * ATT_TK) <= row if diagonal else None
                if diagonal:
                    sc = _each(lambda x: jnp.where(valid, x, -1e30), sc)
                m_new = _each(lambda a, x: jnp.maximum(a, jnp.max(x, axis=1, keepdims=True)), m_new, sc)
                work.append((keys, sc, valid))
            w = _each(jnp.exp, m - m_new)
            l, acc = l * w, acc * w
            for keys, sc, valid in work:
                p = _each(jnp.exp, sc - m_new)
                if diagonal:
                    p = _keep(valid, p)
                l = l + _rowsum(p)
                acc = acc + _each(_split_dot, p, _heads_of(v_ref, keys))
            return tuple(m_new.vals), tuple(l.vals), tuple(acc.vals)

        per_head = lambda width, value: tuple(jnp.full((ATT_TQ, width), value, F32) for _ in range(ATT_HEADS))
        init = (per_head(1, -1e30), per_head(1, 0.0), per_head(HEAD_DIM, 0.0))
        m, l, acc = lax.fori_loop(0, i, lambda jb, cr: step(jb, cr, False), step(i, init, True))
        for h, cs in enumerate(_HEAD_COLS):
            o_ref[:, cs] = acc[h] / l[h]
            lse_ref[h] = jnp.broadcast_to(m[h] + jnp.log(l[h]), (ATT_TQ, LANES))

    q_spec, k_spec, v_spec = _att_specs(n_heads, s)
    return pl.pallas_call(
        body, name=name, grid=(n_heads // ATT_HEADS, s // ATT_TQ),
        in_specs=[q_spec, k_spec, v_spec, pl.BlockSpec((ATT_TQ, LANES), lambda g, i: (i, 0)),
                  pl.BlockSpec((ATT_HEADS, 1, s), lambda g, i: (g, 0, 0))],
        out_specs=[pl.BlockSpec((ATT_TQ, ATT_WIDTH), lambda g, i: (i, g)),
                   pl.BlockSpec((ATT_HEADS, ATT_TQ, LANES), lambda g, i: (g, i, 0))],
        out_shape=[jax.ShapeDtypeStruct((s, n_heads * HEAD_DIM), F32),
                   jax.ShapeDtypeStruct((n_heads, s, LANES), F32)],
        compiler_params=_cparams("parallel", "arbitrary"),
    )(qkv, qkv, qkv, c, ct)


def _fox_bwd(qkv, c, ct, o, lse, do, name):
    s = qkv.shape[0]
    n_heads = N_FOX_HEADS

    def body(q_ref, k_ref, v_ref, c_ref, ct_ref, o_ref, lse_ref, do_ref, dq_ref, dk_ref, dv_ref, dct_ref):
        g, i = pl.program_id(0), pl.program_id(1)

        @pl.when(i == 0)
        def _():
            dk_ref[...] = jnp.zeros_like(dk_ref)
            dv_ref[...] = jnp.zeros_like(dv_ref)
            dct_ref[...] = jnp.zeros_like(dct_ref)

        q = _heads_of(q_ref)
        do16 = _each(lambda x: x.astype(BF16), _heads_of(do_ref))
        delta = _rowsum(_each(lambda a, b: a.astype(F32) * b, do16, _heads_of(o_ref)))
        lse_col = _Each(lse_ref[h, :, 0:1] for h in range(ATT_HEADS))
        cq = _fox_cq(c_ref, g)
        row, col, _, _ = _att_iotas()

        def step(jb, dq, diagonal):
            dq = _Each(dq)
            for sub in range(ATT_SUB):
                keys = pl.ds(pl.multiple_of(jb * ATT_TQ + sub * ATT_TK, ATT_TK), ATT_TK)
                k = _heads_of(k_ref, keys)
                sc = _fox_logits(q, k, cq, ct_ref, keys)
                if diagonal:
                    valid = (col + sub * ATT_TK) <= row
                    p = _keep(valid, _each(jnp.exp, _keep(valid, sc) - lse_col))
                else:
                    p = _each(jnp.exp, sc - lse_col)
                ds = p * (_dot_each(do16, _heads_of(v_ref, keys), "nt") - delta)
                dss = ds * ATT_SCALE
                dk, dv = _dot_each(dss, q, "tn"), _dot_each(p, do16, "tn")
                for h, cs in enumerate(_HEAD_COLS):
                    dct_ref[h, :, keys] -= jnp.sum(ds.vals[h], axis=0, keepdims=True)
                    dk_ref[keys, cs] += dk.vals[h]
                    dv_ref[keys, cs] += dv.vals[h]
                dq = dq + _dot_each(dss, k, "nn")
            return tuple(dq.vals)

        dq0 = step(i, tuple(jnp.zeros((ATT_TQ, HEAD_DIM), F32) for _ in range(ATT_HEADS)), True)
        dq = lax.fori_loop(0, i, lambda jb, dq: step(jb, dq, False), dq0)
        for cs, dq_h in zip(_HEAD_COLS, dq):
            dq_ref[:, cs] = dq_h

    q_spec, k_spec, v_spec = _att_specs(n_heads, s)
    blk = pl.BlockSpec((ATT_TQ, ATT_WIDTH), lambda g, i: (i, g))
    full = pl.BlockSpec((s, ATT_WIDTH), lambda g, i: (0, g))
    wide = jax.ShapeDtypeStruct((s, n_heads * HEAD_DIM), F32)
    return pl.pallas_call(
        body, name=name, grid=(n_heads // ATT_HEADS, s // ATT_TQ),
        in_specs=[q_spec, k_spec, v_spec, pl.BlockSpec((ATT_TQ, LANES), lambda g, i: (i, 0)),
                  pl.BlockSpec((ATT_HEADS, 1, s), lambda g, i: (g, 0, 0)), blk,
                  pl.BlockSpec((ATT_HEADS, ATT_TQ, LANES), lambda g, i: (g, i, 0)), blk],
        out_specs=[blk, full, full, pl.BlockSpec((ATT_HEADS, 1, s), lambda g, i: (g, 0, 0))],
        out_shape=[wide, wide, wide, jax.ShapeDtypeStruct((n_heads, 1, s), F32)],
        compiler_params=_cparams("parallel", "arbitrary"),
    )(qkv, qkv, qkv, c, ct, o, lse, do)


def _cumsum_rows(x, reverse, name):
    s = x.shape[0]
    nb = s // LANES

    def body(x_ref, o_ref):
        r = lax.broadcasted_iota(jnp.int32, (LANES, LANES), 0)
        c = lax.broadcasted_iota(jnp.int32, (LANES, LANES), 1)
        tri = ((r <= c) if reverse else (r >= c)).astype(F32)

        def step(it, carry):
            b = (nb - 1 - it) if reverse else it
            off = pl.multiple_of(b * LANES, LANES)
            blk = x_ref[pl.ds(off, LANES), :]
            o_ref[pl.ds(off, LANES), :] = _dot32(tri, blk) + carry
            return carry + jnp.sum(blk, axis=0, keepdims=True)

        lax.fori_loop(0, nb, step, jnp.zeros((1, LANES), F32))

    return pl.pallas_call(body, name=name, out_shape=jax.ShapeDtypeStruct(x.shape, F32),
                          compiler_params=pltpu.CompilerParams(vmem_limit_bytes=V7X_VMEM_LIMIT))(x)


def _dot32_each(a, b, kind="nn"):
    return _each(lambda x, y: _dot32(x, y, kind), a, b)


def _unit_lower_inverse(m, ri, ci):
    c = ri.shape[0]
    t = -_keep(ri // 2 == ci // 2, m) + jnp.where(ri == ci, 1.0, 0.0)
    b = 4
    while b <= c:
        off_diag = (ri // b == ci // b) & (ri % b >= b // 2) & (ci % b < b // 2)
        t = t - _dot32_each(_dot32_each(t, _keep(off_diag, m)), t)
        b *= 2
    return t


def _dn_gates(g, ri, ci):
    eye = ri == ci
    incl = ri >= ci
    g_row = jnp.sum(jnp.where(eye, g, 0.0), axis=0, keepdims=True)
    gc = jnp.sum(jnp.where(incl, g_row, 0.0), axis=1, keepdims=True)
    gc_row = jnp.sum(jnp.where(eye, gc, 0.0), axis=0, keepdims=True)
    dmat = jnp.where(incl, jnp.exp(jnp.where(incl, gc - gc_row, 0.0)), 0.0)
    gc_last = jnp.sum(g, axis=0, keepdims=True)
    return gc, dmat, jnp.exp(gc), jnp.exp(gc_last - gc), jnp.exp(gc_last)


def _dn_fwd(qkv, act, name):
    s = qkv.shape[0]
    c, d, nh = DN_CHUNK, HEAD_DIM, N_DN_HEADS
    nc = s // c

    def body(q_ref, k_ref, v_ref, act_ref, o_ref, s_ref, t_ref, state):
        @pl.when(pl.program_id(0) == 0)
        def _():
            state[...] = jnp.zeros_like(state)

        ri = lax.broadcasted_iota(jnp.int32, (c, c), 0)
        ci = lax.broadcasted_iota(jnp.int32, (c, c), 1)
        act = act_ref[...]
        heads = range(nh)
        cols = [slice(h * d, (h + 1) * d) for h in heads]
        q, k, v = (_Each(ref[:, cs] for cs in cols) for ref in (q_ref, k_ref, v_ref))
        beta = _Each(_lane_col(act, LANE_BETA + h) for h in heads)
        g = _Each(_lane_col(act, LANE_DECAY + h) for h in heads)
        _, dmat, e, r, gl = _each(lambda gh: _dn_gates(gh, ri, ci), g)
        s0 = _Each(state[h] for h in heads)
        kb = beta * k
        t = _unit_lower_inverse(_keep(ri > ci, _dot32_each(kb, k, "nt") * dmat), ri, ci)
        vn = _dot32_each(t, beta * v) - _dot32_each(_dot32_each(t, kb * e), s0)
        o = _dot32_each(q * e, s0) + _dot32_each(_dot32_each(q, k, "nt") * dmat, vn)
        s1 = s0 * gl + _dot32_each(k * r, vn, "tn")
        for h in heads:
            o_ref[:, cols[h]] = o.vals[h]
            state[h] = s1.vals[h]
            s_ref[h] = s0.vals[h]
            t_ref[h] = t.vals[h]

    wide = lambda part: pl.BlockSpec((c, nh * d), lambda n: (n, part))
    return pl.pallas_call(
        body, name=name, grid=(nc,),
        in_specs=[wide(0), wide(1), wide(2), pl.BlockSpec((c, LANES), lambda n: (n, 0))],
        out_specs=[wide(0), pl.BlockSpec((nh, None, d, d), lambda n: (0, n, 0, 0)),
                   pl.BlockSpec((nh, None, c, c), lambda n: (0, n, 0, 0))],
        out_shape=[jax.ShapeDtypeStruct((s, nh * d), F32), jax.ShapeDtypeStruct((nh, nc, d, d), F32),
                   jax.ShapeDtypeStruct((nh, nc, c, c), F32)],
        scratch_shapes=[pltpu.VMEM((nh, d, d), F32)],
        compiler_params=_cparams("arbitrary"),
    )(qkv, qkv, qkv, act)


def _dn_bwd(qkv, act, states, tinv, do, name):
    s = qkv.shape[0]
    c, d, nh = DN_CHUNK, HEAD_DIM, N_DN_HEADS
    nc = s // c

    def chunk_bwd(q, k, v, do, beta, g, s0, t, ds_out):
        ri = lax.broadcasted_iota(jnp.int32, (c, c), 0)
        ci = lax.broadcasted_iota(jnp.int32, (c, c), 1)
        eye, incl, strict = ri == ci, ri >= ci, ri > ci
        gc, dmat, e, r, gl = _each(lambda gh: _dn_gates(gh, ri, ci), g)
        dot = _dot32_each
        rowsum = lambda x: _each(lambda a: jnp.sum(a, axis=1, keepdims=True), x)
        colsum = lambda x: _each(lambda a: jnp.sum(a, axis=0, keepdims=True), x)
        total = lambda x: colsum(rowsum(x))
        to_col = lambda row: rowsum(_keep(eye, row))
        to_row = lambda colv: colsum(_keep(eye, colv))

        kb, vb = beta * k, beta * v
        kbe = kb * e
        u, w = dot(t, vb), dot(t, kbe)
        vn = u - dot(w, s0)
        qk = dot(q, k, "nt")
        p = qk * dmat
        gram = dot(k, k, "nt")
        kr, qe = k * r, q * e

        d_kr = dot(vn, ds_out, "nt")
        dvn = dot(kr, ds_out)
        dgl = total(s0 * ds_out)
        ds_in = ds_out * gl
        dk = d_kr * r
        dr = rowsum(d_kr * k)
        d_qe = dot(do, s0, "nt")
        ds_in = ds_in + dot(qe, do, "tn")
        dp = _keep(incl, dot(do, vn, "nt"))
        dvn = dvn + dot(p, do, "tn")
        dq = d_qe * e
        de = rowsum(d_qe * q)
        dqk = dp * dmat
        dq = dq + dot(dqk, k)
        dk = dk + dot(dqk, q, "tn")
        dd = dp * qk
        dw = -dot(dvn, s0, "nt")
        ds_in = ds_in - dot(w, dvn, "tn")
        dvb = dot(t, dvn, "tn")
        dkbe = dot(t, dw, "tn")
        dm = -_keep(strict, dot(dvb, u, "nt") + dot(dkbe, w, "nt"))
        dbeta = rowsum(dm * gram * dmat)
        dgram = dm * beta * dmat
        dd = dd + dm * beta * gram
        dk = dk + dot(dgram, k) + dot(dgram, k, "tn")
        dkb = dkbe * e
        de = de + rowsum(dkbe * kb)
        dk = dk + beta * dkb
        dbeta = dbeta + rowsum(dkb * k) + rowsum(dvb * v)
        dv = beta * dvb
        wd = dd * dmat
        dgc = rowsum(wd) - to_col(colsum(wd)) + de * e - dr * r
        dgc_last = total(dr * r) + dgl * gl
        dgc = dgc + _keep(ri[:, 0:1] == c - 1, dgc_last)
        dg = rowsum(_keep(ri <= ci, to_row(dgc)))
        return dq, dk, dv, dbeta, dg, ds_in

    def body(q_ref, k_ref, v_ref, act_ref, s_ref, t_ref, do_ref, dq_ref, dk_ref, dv_ref, dact_ref, dstate):
        @pl.when(pl.program_id(0) == 0)
        def _():
            dstate[...] = jnp.zeros_like(dstate)

        act = act_ref[...]
        heads = range(nh)
        cols = [slice(h * d, (h + 1) * d) for h in heads]
        q, k, v, do = (_Each(ref[:, cs] for cs in cols) for ref in (q_ref, k_ref, v_ref, do_ref))
        dq, dk, dv, dbeta, dg, ds_in = chunk_bwd(
            q, k, v, do, _Each(_lane_col(act, LANE_BETA + h) for h in heads),
            _Each(_lane_col(act, LANE_DECAY + h) for h in heads), _Each(s_ref[h] for h in heads),
            _Each(t_ref[h] for h in heads), _Each(dstate[h] for h in heads))
        lane = lax.broadcasted_iota(jnp.int32, (c, LANES), 1)
        dact = jnp.zeros((c, LANES), F32)
        for h in heads:
            dstate[h] = ds_in.vals[h]
            dq_ref[:, cols[h]], dk_ref[:, cols[h]], dv_ref[:, cols[h]] = dq.vals[h], dk.vals[h], dv.vals[h]
            dact = (dact + jnp.where(lane == LANE_BETA + h, dbeta.vals[h], 0.0)
                    + jnp.where(lane == LANE_DECAY + h, dg.vals[h], 0.0))
        dact_ref[...] = dact

    part = lambda p: pl.BlockSpec((c, nh * d), lambda n: (nc - 1 - n, p))
    per = lambda a, b: pl.BlockSpec((nh, None, a, b), lambda n: (0, nc - 1 - n, 0, 0))
    wide = jax.ShapeDtypeStruct((s, nh * d), F32)
    act_spec = pl.BlockSpec((c, LANES), lambda n: (nc - 1 - n, 0))
    return pl.pallas_call(
        body, name=name, grid=(nc,),
        in_specs=[part(0), part(1), part(2), act_spec, per(d, d), per(c, c), part(0)],
        out_specs=[part(0), part(0), part(0), act_spec],
        out_shape=[wide, wide, wide, jax.ShapeDtypeStruct((s, LANES), F32)],
        scratch_shapes=[pltpu.VMEM((nh, d, d), F32)],
        compiler_params=_cparams("arbitrary"),
    )(qkv, qkv, qkv, act, states, tinv, do)


EVEN_DN_QKV, EVEN_FOX_QKV, EVEN_DN_GATE, EVEN_FOX_GATE, EVEN_NARROW = 0, 1536, 3072, 3584, 4096
EVEN_WIDTH = 4224
CONV_TILE = 256
CONV_HALO = 8


def _conv_fwd(proj, w, name):
    s = proj.shape[0]
    t, cw = CONV_TILE, 3 * D_DN

    def body(cur_ref, prev_ref, w_ref, y_ref, xs):
        i = pl.program_id(0)
        xs[0:CONV_HALO, :] = jnp.where(i > 0, prev_ref[...], 0.0)
        xs[CONV_HALO:, :] = cur_ref[...]
        y = jnp.zeros((t, cw), F32)
        for tap in range(CONV_WIDTH):
            y = y + w_ref[tap:tap + 1, :] * xs[pl.ds(CONV_HALO - CONV_WIDTH + 1 + tap, t), :]
        y_ref[...] = y

    per = t // CONV_HALO
    return pl.pallas_call(
        body, name=name, grid=(s // t,),
        in_specs=[pl.BlockSpec((t, cw), lambda i: (i, 0)),
                  pl.BlockSpec((CONV_HALO, cw), lambda i: (jnp.maximum(i * per - 1, 0), 0)),
                  pl.BlockSpec((CONV_WIDTH, cw), lambda i: (0, 0))],
        out_specs=pl.BlockSpec((t, cw), lambda i: (i, 0)),
        out_shape=jax.ShapeDtypeStruct((s, cw), F32),
        scratch_shapes=[pltpu.VMEM((t + CONV_HALO, cw), F32)],
        compiler_params=_cparams("parallel"),
    )(proj, proj, w)


def _conv_bwd(proj, w, dy, name):
    s = proj.shape[0]
    t, cw = CONV_TILE, 3 * D_DN
    nt = s // t

    def body(cur_ref, prev_ref, w_ref, dy_ref, nxt_ref, dx_ref, dw_ref, xs, dys):
        i = pl.program_id(0)

        @pl.when(i == 0)
        def _():
            dw_ref[...] = jnp.zeros_like(dw_ref)

        xs[0:CONV_HALO, :] = jnp.where(i > 0, prev_ref[...], 0.0)
        xs[CONV_HALO:, :] = cur_ref[...]
        dys[0:t, :] = dy_ref[...]
        dys[t:, :] = jnp.where(i < nt - 1, nxt_ref[...], 0.0)
        dy = dy_ref[...]
        dx = jnp.zeros((t, cw), F32)
        for tap in range(CONV_WIDTH):
            dx = dx + w_ref[tap:tap + 1, :] * dys[pl.ds(CONV_WIDTH - 1 - tap, t), :]
            dw_ref[tap:tap + 1, :] += jnp.sum(dy * xs[pl.ds(CONV_HALO - CONV_WIDTH + 1 + tap, t), :], axis=0,
                                              keepdims=True)
        dx_ref[...] = dx.astype(BF16)

    per = t // CONV_HALO
    last = s // CONV_HALO - 1
    return pl.pallas_call(
        body, name=name, grid=(nt,),
        in_specs=[pl.BlockSpec((t, cw), lambda i: (i, 0)),
                  pl.BlockSpec((CONV_HALO, cw), lambda i: (jnp.maximum(i * per - 1, 0), 0)),
                  pl.BlockSpec((CONV_WIDTH, cw), lambda i: (0, 0)),
                  pl.BlockSpec((t, cw), lambda i: (i, 0)),
                  pl.BlockSpec((CONV_HALO, cw), lambda i: (jnp.minimum((i + 1) * per, last), 0))],
        out_specs=[pl.BlockSpec((t, cw), lambda i: (i, 0)), pl.BlockSpec((CONV_WIDTH, cw), lambda i: (0, 0))],
        out_shape=[jax.ShapeDtypeStruct((s, cw), BF16), jax.ShapeDtypeStruct((CONV_WIDTH, cw), F32)],
        scratch_shapes=[pltpu.VMEM((t + CONV_HALO, cw), F32), pltpu.VMEM((t + CONV_HALO, cw), F32)],
        compiler_params=_cparams("arbitrary"),
    )(proj, proj, w, dy, dy)


def _heads(x, n):
    return [x[:, HEAD_DIM * h:HEAD_DIM * (h + 1)] for h in range(n)]


def _dn_pre_fwd(y, name):
    def fn(yb):
        cs = yb * _sigmoid(yb)
        out = []
        for idx, xh in enumerate(_heads(cs, 3 * N_DN_HEADS)):
            if idx < 2 * N_DN_HEADS:
                xh = xh * lax.rsqrt(jnp.sum(xh * xh, axis=-1, keepdims=True) + EPS)
                if idx < N_DN_HEADS:
                    xh = xh * ATT_SCALE
            out.append(xh)
        return (jnp.concatenate(out, axis=1),)
    return _rowwise(fn, [y], [], [(y.shape[1], F32)], [], tile=256, name=name)[0]


def _dn_pre_bwd(y, dq, dk, dv, name):
    def fn(yb, dqb, dkb, dvb):
        sg = _sigmoid(yb)
        cs = yb * sg
        dout = _heads(dqb, N_DN_HEADS) + _heads(dkb, N_DN_HEADS) + _heads(dvb, N_DN_HEADS)
        dcs = []
        for idx, (xh, dh) in enumerate(zip(_heads(cs, 3 * N_DN_HEADS), dout)):
            if idx < 2 * N_DN_HEADS:
                if idx < N_DN_HEADS:
                    dh = dh * ATT_SCALE
                r = lax.rsqrt(jnp.sum(xh * xh, axis=-1, keepdims=True) + EPS)
                xhat = xh * r
                dh = r * (dh - xhat * jnp.sum(xhat * dh, axis=-1, keepdims=True))
            dcs.append(dh)
        return (jnp.concatenate(dcs, axis=1) * _silu_grad(yb, sg),)
    return _rowwise(fn, [y, dq, dk, dv], [], [(y.shape[1], F32)], [], tile=256, name=name)[0]


def _narrow_params(a_log, dt_bias, f_bias):
    lanes = lambda a, first: jnp.pad(a.reshape(1, -1), ((0, 0), (first, LANES - first - a.shape[0])))
    return jnp.concatenate([lanes(a_log, LANE_DECAY), lanes(dt_bias, LANE_DECAY), lanes(f_bias, LANE_FORGET),
                            jnp.zeros((5, LANES), F32)], axis=0)


def _narrow_masks(shape):
    lane = lax.broadcasted_iota(jnp.int32, shape, 1)
    is_beta = lane < LANE_DECAY
    is_decay = (lane >= LANE_DECAY) & (lane < LANE_FORGET)
    is_forget = (lane >= LANE_FORGET) & (lane < LANE_FORGET + N_FOX_HEADS)
    return is_beta, is_decay, is_forget


def _narrow_fwd(proj, params, name):
    def fn(sm, pk):
        is_beta, is_decay, is_forget = _narrow_masks(sm.shape)
        g = -jnp.exp(pk[0:1, :]) * _softplus(sm + pk[1:2, :])
        logf = -_softplus(-(sm + pk[2:3, :]))
        return (jnp.where(is_beta, _sigmoid(sm), jnp.where(is_decay, g, jnp.where(is_forget, logf, 0.0))),)
    return _rowwise(fn, [(proj, LANES, EVEN_NARROW // LANES)], [params], [(LANES, F32)], [], tile=512, name=name)[0]


def _narrow_bwd(proj, params, act, dact, dlogf, name):
    def fn(sm, ab, da, dl, pk):
        is_beta, is_decay, is_forget = _narrow_masks(sm.shape)
        db = jnp.where(is_forget, dl, da)
        d_beta = db * ab * (1.0 - ab)
        d_decay = db * (-jnp.exp(pk[0:1, :])) * _sigmoid(sm + pk[1:2, :])
        d_forget = db * _sigmoid(-(sm + pk[2:3, :]))
        dsm = jnp.where(is_beta, d_beta, jnp.where(is_decay, d_decay, jnp.where(is_forget, d_forget, 0.0)))
        col = lambda x: jnp.sum(x, axis=0, keepdims=True)
        return (dsm, col(jnp.where(is_decay, db * ab, 0.0)), col(jnp.where(is_decay, dsm, 0.0)),
                col(jnp.where(is_forget, dsm, 0.0)))
    return _rowwise(fn, [(proj, LANES, EVEN_NARROW // LANES), act, dact, dlogf], [params], [(LANES, BF16)],
                    [(1, LANES)] * 3, tile=512, name=name)


def _head_rms(xh):
    r = lax.rsqrt(jnp.mean(xh * xh, axis=-1, keepdims=True) + EPS)
    return xh * r, r


def _fox_pre_fwd(proj, qg, kg, name):
    def fn(pf, qgb, kgb):
        out = []
        for idx, xh in enumerate(_heads(pf, 3 * N_FOX_HEADS)):
            if idx < 2 * N_FOX_HEADS:
                xh = _head_rms(xh)[0] * (qgb if idx < N_FOX_HEADS else kgb)
            out.append(xh)
        return (jnp.concatenate(out, axis=1),)
    return _rowwise(fn, [(proj, 3 * D_FOX, EVEN_FOX_QKV // (3 * D_FOX))], [qg, kg], [(3 * D_FOX, BF16)], [],
                    tile=256, name=name)[0]


def _fox_pre_bwd(proj, qg, kg, dq, dk, dv, name):
    def fn(pf, dqb, dkb, dvb, qgb, kgb):
        dout = _heads(dqb, N_FOX_HEADS) + _heads(dkb, N_FOX_HEADS) + _heads(dvb, N_FOX_HEADS)
        dg = [jnp.zeros((1, HEAD_DIM), F32), jnp.zeros((1, HEAD_DIM), F32)]
        dx = []
        for idx, (xh, dh) in enumerate(zip(_heads(pf, 3 * N_FOX_HEADS), dout)):
            if idx < 2 * N_FOX_HEADS:
                which = 0 if idx < N_FOX_HEADS else 1
                xhat, r = _head_rms(xh)
                dg[which] = dg[which] + jnp.sum(dh * xhat, axis=0, keepdims=True)
                dxh = dh * (qgb if which == 0 else kgb)
                dh = r * (dxh - xhat * jnp.mean(dxh * xhat, axis=-1, keepdims=True))
            dx.append(dh)
        return jnp.concatenate(dx, axis=1), dg[0], dg[1]
    return _rowwise(fn, [(proj, 3 * D_FOX, EVEN_FOX_QKV // (3 * D_FOX)), dq, dk, dv], [qg, kg],
                    [(3 * D_FOX, BF16)], [(1, HEAD_DIM)] * 2, tile=256, name=name)


def _mix_gate_fwd(proj, o_dn, o_fox, ng, name):
    def fn(gd, gf, od, of, ngb):
        dn = [_head_rms(xh)[0] * ngb for xh in _heads(od, N_DN_HEADS)]
        return (jnp.concatenate([jnp.concatenate(dn, axis=1) * gd * _sigmoid(gd), of * _sigmoid(gf)], axis=1),)
    return _rowwise(fn, [(proj, D_DN, EVEN_DN_GATE // D_DN), (proj, D_FOX, EVEN_FOX_GATE // D_FOX), o_dn, o_fox],
                    [ng], [(D_DN + D_FOX, BF16)], [], tile=256, name=name)[0]


def _mix_gate_bwd(proj, o_dn, o_fox, ng, dom, name):
    def fn(gd, gf, od, of, dm, ngb):
        d_dn, d_fox = dm[:, :D_DN], dm[:, D_DN:]
        sgd, sgf = _sigmoid(gd), _sigmoid(gf)
        don = d_dn * gd * sgd
        dng = jnp.zeros((1, HEAD_DIM), F32)
        dod, normed = [], []
        for xh, dh in zip(_heads(od, N_DN_HEADS), _heads(don, N_DN_HEADS)):
            xhat, r = _head_rms(xh)
            dng = dng + jnp.sum(dh * xhat, axis=0, keepdims=True)
            dxh = dh * ngb
            dod.append(r * (dxh - xhat * jnp.mean(dxh * xhat, axis=-1, keepdims=True)))
            normed.append(xhat * ngb)
        d_gd = d_dn * jnp.concatenate(normed, axis=1) * _silu_grad(gd, sgd)
        d_gf = d_fox * of * sgf * (1.0 - sgf)
        return jnp.concatenate(dod, axis=1), d_fox * sgf, d_gd, d_gf, dng
    return _rowwise(fn, [(proj, D_DN, EVEN_DN_GATE // D_DN), (proj, D_FOX, EVEN_FOX_GATE // D_FOX), o_dn, o_fox, dom],
                    [ng], [(D_DN, F32), (D_FOX, F32), (D_DN, BF16), (D_FOX, BF16)], [(1, HEAD_DIM)], tile=256,
                    name=name)


def _loss_grad(y, target, name):
    d = y.shape[1]

    def fn(yb, tb):
        diff = yb - tb
        part = jnp.sum(jnp.sum(diff * diff, axis=1, keepdims=True), axis=0, keepdims=True) * (0.5 / d)
        g = diff * (1.0 / d)
        return g, g, part
    return _rowwise(fn, [y, target], [], [(d, F32), (d, BF16)], [(1, 1)], tile=512, name=name)


_REF_EVEN = {"dn_qkv": (0, 1536), "dn_gate": (1536, 2048), "dn_ba": (2048, 2056), "fox_qkv": (2056, 3592),
             "fox_gate": (3592, 4104), "f_pre": (4104, 4108)}
D_IN_EVEN = 4108


def _even_to_kernel_layout(w):
    cut = lambda name: w[..., _REF_EVEN[name][0]:_REF_EVEN[name][1]]
    pad = jnp.zeros(w.shape[:-1] + (EVEN_WIDTH - EVEN_NARROW - 12,), w.dtype)
    return jnp.concatenate([cut("dn_qkv"), cut("fox_qkv"), cut("dn_gate"), cut("fox_gate"), cut("dn_ba"),
                            cut("f_pre"), pad], axis=-1)


def _even_from_kernel_layout(g):
    return jnp.concatenate([g[..., EVEN_DN_QKV:EVEN_FOX_QKV], g[..., EVEN_DN_GATE:EVEN_FOX_GATE],
                            g[..., EVEN_NARROW:EVEN_NARROW + 8], g[..., EVEN_FOX_QKV:EVEN_DN_GATE],
                            g[..., EVEN_FOX_GATE:EVEN_NARROW], g[..., EVEN_NARROW + 8:EVEN_NARROW + 12]], axis=-1)


EVEN_QUARTER = 1027
EVEN_QUARTER_PAD = 1152


def _even_grad_quarters(g):
    g = _even_from_kernel_layout(g)
    pad = [(0, 0)] * (g.ndim - 1) + [(0, EVEN_QUARTER_PAD - EVEN_QUARTER)]
    return jnp.concatenate([jnp.pad(g[..., q * EVEN_QUARTER:(q + 1) * EVEN_QUARTER], pad) for q in range(4)], axis=-1)


def _forget_rows(c):
    return c[:, LANE_FORGET:LANE_FORGET + N_FOX_HEADS].T.reshape(N_FOX_HEADS, 1, c.shape[0])


def _forget_lanes(rows):
    s = rows.shape[2]
    return jnp.pad(rows.reshape(-1, s).T, ((0, 0), (LANE_FORGET, LANES - LANE_FORGET - N_FOX_HEADS)))


def _even_fwd(x, gain, w_in, w_out, j, p, tag):
    h = _rms_fwd(x, gain, f"{tag}_norm")
    proj = _mm(h, w_in, "nn", tm=512, tn=EVEN_WIDTH // 3, out_dtype=F32, name=f"{tag}_in", b_lead=(j,))
    y = _conv_fwd(proj, p["conv_w"], f"{tag}_conv")
    dn_qkv = _dn_pre_fwd(y, f"{tag}_dn_pre")
    act = _narrow_fwd(proj, p["narrow"], f"{tag}_narrow")
    o_dn, states, tinv = _dn_fwd(dn_qkv, act, f"{tag}_delta")
    fox_qkv = _fox_pre_fwd(proj, p["q_g"], p["k_g"], f"{tag}_fox_pre")
    c = _cumsum_rows(act, False, f"{tag}_cumsum")
    ct = _forget_rows(c)
    o_fox, lse = _fox_fwd(fox_qkv, c, ct, f"{tag}_fox")
    om = _mix_gate_fwd(proj, o_dn, o_fox, p["dn_norm_g"], f"{tag}_gate")
    x2 = _mm(om, w_out, "nn", tm=512, tn=x.shape[1], out_dtype=F32, name=f"{tag}_out", residual=x, b_lead=(j,))
    return x2, (x, h, proj, y, dn_qkv, act, states, tinv, o_dn, fox_qkv, c, ct, o_fox, lse, om)


def _even_bwd(dxo, dxo16, saved, gain, w_in, w_out, j, p, tag, g_in, g_out, after=None):
    x, h, proj, y, dn_qkv, act, states, tinv, o_dn, fox_qkv, c, ct, o_fox, lse, om = saved
    d = x.shape[1]
    dom = _mm(dxo16, w_out, "nt", tm=512, tn=d, out_dtype=F32, name=f"{tag}_out_bwd", b_lead=(j,), after=after)
    g_out = _mm(om, dxo16, "tn", tm=512, tn=d, out_dtype=F32, name=f"{tag}_out_dw", into=(g_out, 0))
    d_odn, d_ofox, d_gd, d_gf, d_ng = _mix_gate_bwd(proj, o_dn, o_fox, p["dn_norm_g"], dom, f"{tag}_gate_bwd")
    dq, dk, dv, dct = _fox_bwd(fox_qkv, c, ct, o_fox, lse, d_ofox, f"{tag}_fox_bwd")
    d_fox_qkv, d_qg, d_kg = _fox_pre_bwd(proj, p["q_g"], p["k_g"], dq, dk, dv, f"{tag}_fox_pre_bwd")
    dlogf = _cumsum_rows(_forget_lanes(dct), True, f"{tag}_cumsum_bwd")
    dq, dk, dv, dact = _dn_bwd(dn_qkv, act, states, tinv, d_odn, f"{tag}_delta_bwd")
    dy = _dn_pre_bwd(y, dq, dk, dv, f"{tag}_dn_pre_bwd")
    d_dn_qkv, d_conv = _conv_bwd(proj, p["conv_w"], dy, f"{tag}_conv_bwd")
    d_narrow, s_alog, s_dt, s_fb = _narrow_bwd(proj, p["narrow"], act, dact, dlogf, f"{tag}_narrow_bwd")
    dproj = jnp.concatenate([d_dn_qkv, d_fox_qkv, d_gd, d_gf, d_narrow], axis=1)
    g_in = _mm(h, dproj, "tn", tm=512, tn=EVEN_WIDTH // 3, out_dtype=F32, name=f"{tag}_in_dw", into=(g_in, 0))
    dx, dx16, d_gain = _in_proj_bwd(dproj, w_in, j, x, dxo, gain, f"{tag}_in_bwd")
    small = {"conv_w": d_conv, "a_log": s_alog, "dt_bias": s_dt, "f_bias": s_fb, "dn_norm_g": d_ng, "q_g": d_qg,
             "k_g": d_kg}
    return dx, dx16, d_gain, small, g_in, g_out


def _odd_fwd(x, gain, w_in, w_out, j, tag):
    h = _rms_fwd(x, gain, f"{tag}_norm")
    qkv = _mm(h, w_in, "nn", tm=512, tn=w_in.shape[2] // 2, out_dtype=BF16, name=f"{tag}_in", b_lead=(j,))
    o16, o32 = _sb_fwd(qkv, N_SB_HEADS, f"{tag}_sb")
    x2 = _mm(o16, w_out, "nn", tm=512, tn=x.shape[1], out_dtype=F32, name=f"{tag}_out", residual=x, b_lead=(j,))
    return x2, (x, h, qkv, o16, o32)


def _odd_bwd(dxo, dxo16, saved, gain, w_in, w_out, j, tag, g_in, g_out, after=None):
    x, h, qkv, o16, o32 = saved
    d = x.shape[1]
    do = _mm(dxo16, w_out, "nt", tm=512, tn=d, out_dtype=BF16, name=f"{tag}_out_bwd", b_lead=(j,), after=after)
    g_out = _mm(o16, dxo16, "tn", tm=512, tn=d, out_dtype=F32, name=f"{tag}_out_dw", into=(g_out, 0))
    dq, dk, dv = _sb_bwd(qkv, o32, do, N_SB_HEADS, f"{tag}_sb_bwd")
    dqkv = jnp.concatenate([dq, dk.astype(BF16), dv.astype(BF16)], axis=1)
    g_in = _mm(h, dqkv, "tn", tm=512, tn=w_in.shape[2] // 2, out_dtype=F32, name=f"{tag}_in_dw", into=(g_in, 0))
    dx, dx16, d_gain = _in_proj_bwd(dqkv, w_in, j, x, dxo, gain, f"{tag}_in_bwd")
    return dx, dx16, d_gain, g_in, g_out


def _forward_backward(x, target, w, first, rest_after, token, on_reduced):
    depth = w["norm_ffn1"].shape[0]
    row = lambda a, l: a[l][None]
    rest = {}

    def mats(names, j):
        if j == 0 and names[0] in first:
            return [first[name] for name in names] + [0]
        return [rest[name] for name in names] + [j - (1 if names[0] in first else 0)]

    def even_small(j):
        return {"conv_w": w["dn_conv_w"][j], "narrow": _narrow_params(w["dn_a_log"][j], w["dn_dt_bias"][j],
                                                                     w["fox_f_bias"][j]),
                "dn_norm_g": row(w["dn_norm_g"], j), "q_g": row(w["fox_q_norm_g"], j),
                "k_g": row(w["fox_k_norm_g"], j)}

    saved = []
    for l in range(depth):
        if l == 1:
            rest.update(rest_after(x))
        gain = row(w["norm_ffn1"], l) + token[0:1, 0:1] if l == 0 else row(w["norm_ffn1"], l)
        x, s1 = _ffn_fwd(x, gain, *mats(("ffn1_w_gu", "ffn1_w_down"), l), "ffn1")
        if l % 2 == 0:
            x, s2 = _even_fwd(x, row(w["norm_mix"], l), *mats(("w_in_even", "w_out_even"), l // 2),
                              even_small(l // 2), "even")
        else:
            x, s2 = _odd_fwd(x, row(w["norm_mix"], l), *mats(("w_in_odd", "w_out_odd"), l // 2), "odd")
        x, s3 = _ffn_fwd(x, row(w["norm_ffn2"], l), *mats(("ffn2_w_gu", "ffn2_w_down"), l), "ffn2")
        saved.append((s1, s2, s3))

    dx, dx16, loss = _loss_grad(x, target, "loss")

    kind_of = dict(BIG)
    d_norm = {k: [None] * depth for k in ("norm_ffn1", "norm_mix", "norm_ffn2")}
    d_even = [None] * ((depth + 1) // 2)
    to_sibling, between_chips, token = None, None, None
    for l in reversed(range(depth)):
        s1, s2, s3 = saved[l]
        mixer = ("w_in_even", "w_out_even") if l % 2 == 0 else ("w_in_odd", "w_out_odd")
        names = ["ffn1_w_gu", "ffn1_w_down", *mixer, "ffn2_w_gu", "ffn2_w_down"]
        g = {name: lax.empty((1,) + rest[name].shape[1:], F32) for name in names}
        dx, dx16, d_norm["norm_ffn2"][l], g["ffn2_w_gu"], g["ffn2_w_down"] = _ffn_bwd(
            dx, dx16, s3, row(w["norm_ffn2"], l), *mats(("ffn2_w_gu", "ffn2_w_down"), l), "ffn2", g["ffn2_w_gu"],
            g["ffn2_w_down"], after=token)
        if to_sibling is not None:
            between_chips, token = _reduce_middle(to_sibling, dx)
        if l % 2 == 0:
            dx, dx16, d_norm["norm_mix"][l], d_even[l // 2], g["w_in_even"], g["w_out_even"] = _even_bwd(
                dx, dx16, s2, row(w["norm_mix"], l), *mats(("w_in_even", "w_out_even"), l // 2), even_small(l // 2),
                "even", g["w_in_even"], g["w_out_even"], after=token)
            g["w_in_even"] = _even_grad_quarters(g["w_in_even"])
        else:
            dx, dx16, d_norm["norm_mix"][l], g["w_in_odd"], g["w_out_odd"] = _odd_bwd(
                dx, dx16, s2, row(w["norm_mix"], l), *mats(("w_in_odd", "w_out_odd"), l // 2), "odd", g["w_in_odd"],
                g["w_out_odd"], after=token)
        dx, dx16, d_norm["norm_ffn1"][l], g["ffn1_w_gu"], g["ffn1_w_down"] = _ffn_bwd(
            dx, dx16, s1, row(w["norm_ffn1"], l), *mats(("ffn1_w_gu", "ffn1_w_down"), l), "ffn1", g["ffn1_w_gu"],
            g["ffn1_w_down"])
        to_sibling, token = _reduce_start([g[name] for name in names], [kind_of[name] for name in names], names,
                                          f"layer{l}")
        if between_chips is not None:
            on_reduced(l + 1, dict(zip(between_chips[-2], _reduce_finish(between_chips, dx))))
    between_chips, _ = _reduce_middle(to_sibling, dx)
    on_reduced(0, dict(zip(between_chips[-2], _reduce_finish(between_chips, dx))))

    small = {k: jnp.concatenate(v, axis=0) for k, v in d_norm.items()}
    dec = slice(LANE_DECAY, LANE_DECAY + N_DN_HEADS)
    fgt = slice(LANE_FORGET, LANE_FORGET + N_FOX_HEADS)
    small["dn_conv_w"] = jnp.stack([e["conv_w"] for e in d_even])
    small["dn_a_log"] = jnp.concatenate([e["a_log"][:, dec] for e in d_even], axis=0)
    small["dn_dt_bias"] = jnp.concatenate([e["dt_bias"][:, dec] for e in d_even], axis=0)
    small["fox_f_bias"] = jnp.concatenate([e["f_bias"][:, fgt] for e in d_even], axis=0)
    small["dn_norm_g"] = jnp.concatenate([e["dn_norm_g"] for e in d_even], axis=0)
    small["fox_q_norm_g"] = jnp.concatenate([e["q_g"] for e in d_even], axis=0)
    small["fox_k_norm_g"] = jnp.concatenate([e["k_g"] for e in d_even], axis=0)
    return loss, dx, small


MESH = pl.DeviceIdType.MESH
ANY = pl.BlockSpec(memory_space=pl.ANY)


def _place():
    x, y, c = lax.axis_index("x"), lax.axis_index("y"), lax.axis_index("c")
    return x, y, c, [(1 - x, y), (x, 1 - y), (1 - x, 1 - y)]


def _remote(src, dst, send_sem, recv_sem, to):
    return pltpu.make_async_remote_copy(src_ref=src, dst_ref=dst, send_sem=send_sem, recv_sem=recv_sem,
                                        device_id=to, device_id_type=MESH)


def _aligned(start, multiple):
    return start if isinstance(start, int) else pl.multiple_of(start, multiple)


def _quarter(ref, kind, chip, half, rows, cols):
    k = 2 * chip[0] + chip[1]
    hr = rows // 2
    assert hr % 16 == 0 and cols % LANES == 0
    if kind == "col":
        return ref.at[:, pl.ds(_aligned(half * hr, 16), hr), pl.ds(_aligned(k * cols, LANES), cols)]
    return ref.at[:, pl.ds(_aligned(k * rows + half * hr, 16), hr), :]


def _place_quarter(shard, kind, kc, name, first=0, count=None):
    l, rows, cols = shard.shape
    l = l - first if count is None else count
    tr = rows
    while tr * cols * 4 > (2 << 20) and tr % 32 == 0:
        tr //= 2
    nr = rows // tr
    if kind == "col":
        out_spec = pl.BlockSpec((None, tr, cols), lambda li, i, kc_ref: (li, i, kc_ref[0]))
        out_shape = (l, rows, 4 * cols)
    else:
        out_spec = pl.BlockSpec((None, tr, cols), lambda li, i, kc_ref: (li, kc_ref[0] * nr + i, 0))
        out_shape = (l, 4 * rows, cols)

    def body(kc_ref, x_ref, o_ref):
        o_ref[...] = x_ref[...].astype(BF16)

    return pl.pallas_call(
        body, name=name,
        grid_spec=pltpu.PrefetchScalarGridSpec(
            num_scalar_prefetch=1, grid=(l, nr),
            in_specs=[pl.BlockSpec((None, tr, cols), lambda li, i, kc_ref: (li + first, i, 0))],
            out_specs=out_spec),
        out_shape=jax.ShapeDtypeStruct(out_shape, BF16),
        compiler_params=_cparams("parallel", "parallel"),
    )(kc, shard)


def _gather_weights(wholes, kinds):
    n = len(wholes)

    def dims(ref, kind):
        _, r, cc = ref.shape
        return (r, cc // 4) if kind == "col" else (r // 4, cc)

    def body(*refs):
        bufs = refs[n:2 * n]
        send_sems, recv_sems = refs[2 * n:]
        x, y, c, chips = _place()
        sibling = (x, y, 1 - c)
        first, passed = [], []
        for t in range(n):
            rows, cols = dims(bufs[t], kinds[t])
            mine = _quarter(bufs[t], kinds[t], (x, y), c, rows, cols)
            for j, chip in enumerate(chips):
                cp = _remote(mine, mine, send_sems.at[t, j], recv_sems.at[t, j], (*chip, c))
                cp.start()
                first.append(cp)
        for j, chip in enumerate(chips):
            for t in range(n):
                rows, cols = dims(bufs[t], kinds[t])
                got = _quarter(bufs[t], kinds[t], chip, c, rows, cols)
                _remote(got, got, send_sems.at[t, j], recv_sems.at[t, j], (*chip, c)).wait_recv()
                cp = _remote(got, got, send_sems.at[t, 3 + j], recv_sems.at[t, 3 + j], sibling)
                cp.start()
                passed.append(cp)
        for j, chip in enumerate(chips):
            for t in range(n):
                rows, cols = dims(bufs[t], kinds[t])
                got = _quarter(bufs[t], kinds[t], chip, 1 - c, rows, cols)
                _remote(got, got, send_sems.at[t, 3 + j], recv_sems.at[t, 3 + j], sibling).wait_recv()
        for cp in first + passed:
            cp.wait_send()

    return pl.pallas_call(
        body, name="gather_weights", in_specs=[ANY] * n, out_specs=[ANY] * n,
        out_shape=[jax.ShapeDtypeStruct(a.shape, a.dtype) for a in wholes],
        input_output_aliases={t: t for t in range(n)},
        scratch_shapes=[pltpu.SemaphoreType.DMA((n, 6)), pltpu.SemaphoreType.DMA((n, 6))],
        compiler_params=pltpu.CompilerParams(has_side_effects=True),
    )(*wholes)


def _quarter_dims(ref, kind):
    _, r, cc = ref.shape
    return (r, cc // 4) if kind == "col" else (r // 4, cc)


def _gather_chips_copies(bufs, sems, kinds):
    x, y, c, chips = _place()
    copies = []
    for t, buf in enumerate(bufs):
        rows, cols = _quarter_dims(buf, kinds[t])
        mine = _quarter(buf, kinds[t], (x, y), c, rows, cols)
        for j, chip in enumerate(chips):
            pair = 2 * (OTHER_CHIPS * t + j)
            copies.append(_remote(mine, mine, sems[pair], sems[pair + 1], (*chip, c)))
    return copies


def _gather_start(wholes, kinds, after, tag):
    n = len(wholes)
    n_sems = 2 * OTHER_CHIPS * n
    n_in = n + len(after)

    def body(*refs):
        for cp in _gather_chips_copies(refs[:n], refs[n_in + n:n_in + n + n_sems], kinds):
            cp.start()
        refs[-1][...] = jnp.zeros_like(refs[-1])

    held = [pltpu.with_memory_space_constraint(a, pltpu.HBM) for a in wholes]
    out = pl.pallas_call(
        body, name=f"gather_start_{tag}", in_specs=[HBM] * n + [ANY] * len(after),
        out_specs=(*[HBM] * n, *[SEM] * n_sems, pl.BlockSpec(memory_space=pltpu.VMEM)),
        out_shape=(*[pltpu.HBM(a.shape, a.dtype) for a in held], *[pltpu.SemaphoreType.DMA(())] * n_sems,
                   jax.ShapeDtypeStruct((8, LANES), F32)),
        input_output_aliases={i: i for i in range(n)},
        compiler_params=pltpu.CompilerParams(has_side_effects=SPLIT_COPY),
    )(*held, *after)
    return out[n:n + n_sems], out[:n], out[-1]


def _gather_wait(sems, wholes, kinds, after, tag):
    n = len(wholes)

    def body(*refs):
        for cp in _gather_chips_copies(refs[:n], refs[n:n + len(sems)], kinds):
            cp.wait_send()
            cp.wait_recv()

    return pl.pallas_call(
        body, name=f"gather_wait_{tag}", in_specs=[HBM] * n + [SEM] * len(sems) + [ANY],
        out_specs=tuple([HBM] * n), out_shape=tuple(pltpu.HBM(a.shape, a.dtype) for a in wholes),
        input_output_aliases={i: i for i in range(n)},
        compiler_params=pltpu.CompilerParams(has_side_effects=SPLIT_COPY),
    )(*wholes, *sems, after)


def _gather_forward(wholes, kinds, tag):
    n = len(wholes)

    def body(*refs):
        bufs = refs[n:2 * n]
        send_sems, recv_sems = refs[2 * n:]
        x, y, c, chips = _place()
        copies = []
        for t in range(n):
            rows, cols = _quarter_dims(bufs[t], kinds[t])
            for j, chip in enumerate(chips):
                got = _quarter(bufs[t], kinds[t], chip, c, rows, cols)
                cp = _remote(got, got, send_sems.at[t, j], recv_sems.at[t, j], (x, y, 1 - c))
                cp.start()
                copies.append(cp)
        for cp in copies:
            cp.wait_send()
        for t in range(n):
            rows, cols = _quarter_dims(bufs[t], kinds[t])
            for j, chip in enumerate(chips):
                got = _quarter(bufs[t], kinds[t], chip, 1 - c, rows, cols)
                _remote(got, got, send_sems.at[t, j], recv_sems.at[t, j], (x, y, 1 - c)).wait_recv()

    return pl.pallas_call(
        body, name=f"gather_forward_{tag}", in_specs=[ANY] * n, out_specs=[ANY] * n,
        out_shape=[jax.ShapeDtypeStruct(a.shape, a.dtype) for a in wholes],
        input_output_aliases={t: t for t in range(n)},
        scratch_shapes=[pltpu.SemaphoreType.DMA((n, OTHER_CHIPS)), pltpu.SemaphoreType.DMA((n, OTHER_CHIPS))],
        compiler_params=pltpu.CompilerParams(has_side_effects=True),
    )(*wholes)


def _canonical(a, kind):
    l, r, c = a.shape
    return a.reshape(l, 1, r, c) if kind == "col" else a.reshape(l, 4, r // 4, c)


def _add_tile(rows, cols):
    tc = cols if cols <= 1536 else cols // 4
    tr = rows
    while tr * tc * 4 > (1 << 20) and tr % 16 == 0:
        tr //= 2
    return tr, tc


def _rs_add_sibling(part, got, c, name):
    l, a, hr, cols = got.shape
    tr, tc = _add_tile(hr, cols)
    nr = hr // tr

    def body(c_ref, p_ref, g_ref, o32_ref, o16_ref):
        s = p_ref[...] + g_ref[...]
        o32_ref[...] = s
        o16_ref[...] = s.astype(BF16)

    blk = (None, None, tr, tc)
    spec = pl.BlockSpec(blk, lambda li, ai, i, j, c_ref: (li, ai, i, j))
    return pl.pallas_call(
        body, name=name,
        grid_spec=pltpu.PrefetchScalarGridSpec(
            num_scalar_prefetch=1, grid=(l, a, nr, cols // tc),
            in_specs=[pl.BlockSpec(blk, lambda li, ai, i, j, c_ref: (li, ai, c_ref[0] * nr + i, j)), spec],
            out_specs=[spec, spec]),
        out_shape=[jax.ShapeDtypeStruct(got.shape, F32), jax.ShapeDtypeStruct(got.shape, BF16)],
        compiler_params=_cparams("parallel", "parallel", "parallel", "parallel"),
    )(c, part, got)


def _quarter4(ref, kind, chip, cols):
    k = 2 * chip[0] + chip[1]
    if kind == "col":
        return ref.at[:, :, :, pl.ds(pl.multiple_of(k * cols, LANES), cols)]
    return ref.at[:, pl.ds(k, 1), :, :]


HBM = pl.BlockSpec(memory_space=pltpu.HBM)
SEM = pl.BlockSpec(memory_space=pltpu.SEMAPHORE)
SPLIT_COPY = pltpu.SideEffectType.DATAFLOW_SIDE_EFFECTING
OTHER_CHIPS = 3


def _quarter4_shape(a, kind):
    l, _, hr, cols = a.shape
    return (l, 1, hr, cols // 4 if kind == "col" else cols)


def _rs_chips_copies(srcs, lands, sems, kinds):
    x, y, c, chips = _place()
    copies = []
    for t, (src, land) in enumerate(zip(srcs, lands)):
        cols = _quarter4_shape(src, kinds[t])[3]
        for j, chip in enumerate(chips):
            pair = 2 * (OTHER_CHIPS * t + j)
            copies.append(_remote(_quarter4(src, kinds[t], chip, cols), land.at[j], sems[pair], sems[pair + 1],
                                  (*chip, c)))
    return copies


def _split_start(copies, srcs, lands, n_sems, name):
    n = len(srcs)

    def body(*refs):
        for cp in copies(refs[:n], refs[n:2 * n], refs[4 * n:4 * n + n_sems]):
            cp.start()
        refs[-1][...] = jnp.zeros_like(refs[-1])

    held = [pltpu.with_memory_space_constraint(a, pltpu.HBM) for a in (*srcs, *lands)]
    out = pl.pallas_call(
        body, name=name, in_specs=[HBM] * (2 * n),
        out_specs=(*[HBM] * (2 * n), *[SEM] * n_sems, pl.BlockSpec(memory_space=pltpu.VMEM)),
        out_shape=(*[pltpu.HBM(a.shape, a.dtype) for a in held], *[pltpu.SemaphoreType.DMA(())] * n_sems,
                   jax.ShapeDtypeStruct((8, LANES), F32)),
        input_output_aliases={i: i for i in range(2 * n)},
        compiler_params=pltpu.CompilerParams(has_side_effects=SPLIT_COPY),
    )(*held)
    return out[2 * n:2 * n + n_sems], out[:n], out[n:2 * n], out[-1]


def _split_wait(copies, sems, srcs, lands, after, name):
    n = len(srcs)

    def body(*refs):
        for cp in copies(refs[:n], refs[n:2 * n], refs[2 * n:2 * n + len(sems)]):
            cp.wait_send()
            cp.wait_recv()

    out = pl.pallas_call(
        body, name=name, in_specs=[HBM] * (2 * n) + [SEM] * len(sems) + [ANY],
        out_specs=tuple([HBM] * (2 * n)),
        out_shape=tuple(pltpu.HBM(a.shape, a.dtype) for a in (*srcs, *lands)),
        input_output_aliases={i: i for i in range(2 * n)},
        compiler_params=pltpu.CompilerParams(has_side_effects=SPLIT_COPY),
    )(*srcs, *lands, *sems, after)
    return out[:n], out[n:]


def _rs_sibling_copies(srcs, lands, sems):
    x, y, c, _ = _place()
    copies = []
    for t, (src, land) in enumerate(zip(srcs, lands)):
        hr = src.shape[2] // 2
        gives = src.at[:, :, pl.ds(pl.multiple_of((1 - c) * hr, 8), hr), :]
        copies.append(_remote(gives, land, sems[2 * t], sems[2 * t + 1], (x, y, 1 - c)))
    return copies


def _rs_add_chips(sum32, got, kind, kc, name):
    _, l, _, hr, cols = got.shape
    tr, _ = _add_tile(hr, cols)
    nr = hr // tr
    k_arr, c_arr = kc
    if kind == "col":
        own = pl.BlockSpec((None, None, tr, cols), lambda li, i, k_ref, c_ref: (li, 0, i, k_ref[0]))
    else:
        own = pl.BlockSpec((None, None, tr, cols), lambda li, i, k_ref, c_ref: (li, k_ref[0], i, 0))

    def body(k_ref, c_ref, own_ref, got_ref, o_ref):
        o_ref[...] = ((own_ref[...] + got_ref[0].astype(F32)) + got_ref[1].astype(F32)) + got_ref[2].astype(F32)

    return pl.pallas_call(
        body, name=name,
        grid_spec=pltpu.PrefetchScalarGridSpec(
            num_scalar_prefetch=2, grid=(l, nr),
            in_specs=[own, pl.BlockSpec((3, None, None, tr, cols), lambda li, i, k_ref, c_ref: (0, li, 0, i, 0))],
            out_specs=pl.BlockSpec((None, tr, cols), lambda li, i, k_ref, c_ref: (li, c_ref[0] * nr + i, 0))),
        out_shape=jax.ShapeDtypeStruct((l, 2 * hr, cols), F32),
        compiler_params=_cparams("parallel", "parallel"),
    )(k_arr, c_arr, sum32, got)


def _rs_finish(quarters):
    n = len(quarters)

    def body(*refs):
        bufs = refs[n:2 * n]
        send_sems, recv_sems = refs[2 * n:]
        x, y, c, _ = _place()
        copies = []
        for t in range(n):
            hr = bufs[t].shape[1] // 2
            mine = bufs[t].at[:, pl.ds(pl.multiple_of(c * hr, 8), hr), :]
            cp = _remote(mine, mine, send_sems.at[t], recv_sems.at[t], (x, y, 1 - c))
            cp.start()
            copies.append(cp)
        for cp in copies:
            cp.wait()

    return pl.pallas_call(
        body, name="reduce_finish", in_specs=[ANY] * n, out_specs=[ANY] * n,
        out_shape=[jax.ShapeDtypeStruct(a.shape, a.dtype) for a in quarters],
        input_output_aliases={t: t for t in range(n)},
        scratch_shapes=[pltpu.SemaphoreType.DMA((n,)), pltpu.SemaphoreType.DMA((n,))],
        compiler_params=pltpu.CompilerParams(has_side_effects=True),
    )(*quarters)


def _reduce_start(parts, kinds, names, tag):
    canon = [_canonical(p, kind) for p, kind in zip(parts, kinds)]
    lands = [lax.empty(a.shape[:2] + (a.shape[2] // 2, a.shape[3]), a.dtype) for a in canon]
    sems, srcs, lands, token = _split_start(_rs_sibling_copies, canon, lands, 2 * len(canon),
                                            f"reduce_sibling_start_{tag}")
    return (sems, srcs, lands, kinds, names, tag), token


def _reduce_middle(state, after):
    sems, srcs, lands, kinds, names, tag = state
    c_arr = jnp.reshape(lax.axis_index("c"), (1,)).astype(jnp.int32)
    srcs, from_sibling = _split_wait(_rs_sibling_copies, sems, srcs, lands, after, f"reduce_sibling_wait_{tag}")
    sums = [_rs_add_sibling(p, g, c_arr, f"reduce_add_sibling_{nm}") for p, g, nm in zip(srcs, from_sibling, names)]
    sums16 = [s16 for _, s16 in sums]
    copies = functools.partial(_rs_chips_copies, kinds=kinds)
    lands = [lax.empty((OTHER_CHIPS,) + _quarter4_shape(a, k), a.dtype) for a, k in zip(sums16, kinds)]
    sems, srcs, lands, token = _split_start(copies, sums16, lands, 2 * OTHER_CHIPS * len(sums16),
                                            f"reduce_chips_start_{tag}")
    return (sems, srcs, lands, [s32 for s32, _ in sums], kinds, names, tag), token


def _reduce_finish(state, after):
    sems, srcs, lands, sums32, kinds, names, tag = state
    x, y, c = lax.axis_index("x"), lax.axis_index("y"), lax.axis_index("c")
    kc = (jnp.reshape(2 * x + y, (1,)).astype(jnp.int32), jnp.reshape(c, (1,)).astype(jnp.int32))
    copies = functools.partial(_rs_chips_copies, kinds=kinds)
    _, from_chips = _split_wait(copies, sems, srcs, lands, after, f"reduce_chips_wait_{tag}")
    halves = [_rs_add_chips(s32, g, kind, kc, f"reduce_add_chips_{nm}")
              for s32, g, kind, nm in zip(sums32, from_chips, kinds, names)]
    return _rs_finish(halves)


SMALL_PEERS = 7


def _small_exchange(pack):
    rows = pack.shape[0]

    def body(p_ref, slots_ref, total_ref, send_sems, recv_sems):
        x, y, c, _ = _place()
        me = 4 * x + 2 * y + c
        slots_ref[me] = p_ref[...]
        copies = []
        for p in range(1, SMALL_PEERS + 1):
            px, py, pc = (p >> 2) & 1, (p >> 1) & 1, p & 1
            peer = (1 - x if px else x, 1 - y if py else y, 1 - c if pc else c)
            cp = _remote(p_ref, slots_ref.at[me], send_sems.at[p - 1], recv_sems.at[p - 1], peer)
            cp.start()
            copies.append(cp)
        for cp in copies:
            cp.wait()
        total = slots_ref[0]
        for i in range(1, SMALL_PEERS + 1):
            total = total + slots_ref[i]
        total_ref[...] = total

    vmem = pl.BlockSpec(memory_space=pltpu.VMEM)
    return pl.pallas_call(
        body, name="small_exchange", in_specs=[vmem], out_specs=[vmem, vmem],
        out_shape=[jax.ShapeDtypeStruct((SMALL_PEERS + 1, rows, LANES), F32), jax.ShapeDtypeStruct((rows, LANES), F32)],
        scratch_shapes=[pltpu.SemaphoreType.DMA((SMALL_PEERS,)), pltpu.SemaphoreType.DMA((SMALL_PEERS,))],
        compiler_params=pltpu.CompilerParams(has_side_effects=True),
    )(pack)


def _pack(arrays):
    rows = []
    for a in arrays:
        flat = a.reshape(-1).astype(F32)
        rows.append(jnp.pad(flat, (0, (-flat.shape[0]) % LANES)).reshape(-1, LANES))
    out = jnp.concatenate(rows, axis=0)
    return jnp.pad(out, ((0, (-out.shape[0]) % 8), (0, 0)))


def _unpack(pack, shapes):
    out, r = [], 0
    for sh in shapes:
        size = math.prod(sh)
        nr = -(-size // LANES)
        out.append(pack[r:r + nr].reshape(-1)[:size].reshape(sh))
        r += nr
    return out


def _adamw(w, g, m, v, name):
    shape = w.shape
    to2d = lambda a: a.reshape(-1, shape[-1])
    rows = math.prod(shape[:-1])
    tile = 256 if rows % 256 == 0 else rows

    def fn(wb, gb, mb, vb):
        m2 = ADAM_B1 * mb + (1.0 - ADAM_B1) * gb
        v2 = ADAM_B2 * vb + (1.0 - ADAM_B2) * (gb * gb)
        m_hat = m2 / (1.0 - ADAM_B1 ** ADAM_STEP)
        v_hat = v2 / (1.0 - ADAM_B2 ** ADAM_STEP)
        return -ADAM_LR * (m_hat / (jnp.sqrt(v_hat) + ADAM_EPS) + ADAM_WD * wb), m2, v2

    res = _rowwise(fn, [to2d(w), to2d(g), to2d(m), to2d(v)], [], [(shape[-1], F32)] * 3, [], tile=tile, name=name)
    return [r.reshape(shape) for r in res]


def _adamw_layer(w, g, m, v, layer, outs, name):
    _, rows, cols = w.shape
    tile = rows
    while tile * cols * 4 > (1 << 20) and tile % 16 == 0:
        tile //= 2

    def body(w_ref, g_ref, m_ref, v_ref, *rest):
        g_out, d_out, m_out, v_out = rest[-4:]
        gb = g_ref[...]
        m2 = ADAM_B1 * m_ref[...] + (1.0 - ADAM_B1) * gb
        v2 = ADAM_B2 * v_ref[...] + (1.0 - ADAM_B2) * (gb * gb)
        m_hat = m2 / (1.0 - ADAM_B1 ** ADAM_STEP)
        v_hat = v2 / (1.0 - ADAM_B2 ** ADAM_STEP)
        g_out[...] = gb
        d_out[...] = -ADAM_LR * (m_hat / (jnp.sqrt(v_hat) + ADAM_EPS) + ADAM_WD * w_ref[...])
        m_out[...] = m2
        v_out[...] = v2

    stacked = pl.BlockSpec((None, tile, cols), lambda i: (layer, i, 0))
    return pl.pallas_call(
        body, name=name, grid=(rows // tile,),
        in_specs=[stacked, pl.BlockSpec((None, tile, cols), lambda i: (0, i, 0)), stacked, stacked] + [ANY] * 4,
        out_specs=[stacked] * 4, out_shape=[jax.ShapeDtypeStruct(w.shape, F32)] * 4,
        input_output_aliases={4 + i: i for i in range(4)}, compiler_params=_cparams("parallel"),
    )(w, g, m, v, *outs)


BIG = (("ffn1_w_gu", "col"), ("ffn1_w_down", "row"), ("w_in_even", "col"), ("w_out_even", "row"),
       ("w_in_odd", "col"), ("w_out_odd", "row"), ("ffn2_w_gu", "col"), ("ffn2_w_down", "row"))
SMALL = ("norm_ffn1", "norm_mix", "dn_conv_w", "dn_a_log", "dn_dt_bias", "dn_norm_g", "fox_q_norm_g", "fox_k_norm_g",
         "fox_f_bias", "norm_ffn2")
WEIGHTS = ("norm_ffn1", "ffn1_w_gu", "ffn1_w_down", "norm_mix", "w_in_even", "dn_conv_w", "dn_a_log", "dn_dt_bias",
           "dn_norm_g", "fox_q_norm_g", "fox_k_norm_g", "fox_f_bias", "w_out_even", "w_in_odd", "w_out_odd",
           "norm_ffn2", "ffn2_w_gu", "ffn2_w_down")


def _step(x, target, w, m, v):
    k = 2 * lax.axis_index("x") + lax.axis_index("y")
    n_conv = w["dn_conv_w"].shape[2]

    kc = jnp.reshape(k, (1,)).astype(jnp.int32)
    kinds = dict(BIG)
    quarters = {name: w[name] for name in kinds}
    quarters["w_in_even"] = jnp.pad(w["w_in_even"], ((0, 0), (0, 0), (0, EVEN_QUARTER_PAD - EVEN_QUARTER)))
    first_names = [name for name in kinds if name not in ("w_in_odd", "w_out_odd")]
    rest_names = list(kinds)

    def even_columns(whole):
        padded = whole["w_in_even"]
        ref_order = jnp.concatenate([padded[..., q * EVEN_QUARTER_PAD:q * EVEN_QUARTER_PAD + EVEN_QUARTER]
                                     for q in range(4)], axis=-1)
        return {**whole, "w_in_even": _even_to_kernel_layout(ref_order)}

    conv_slots, _ = _small_exchange(_pack([w["dn_conv_w"]]))
    placed = [_place_quarter(quarters[name], kinds[name], kc, f"place_first_{name}", 0, 1) for name in first_names]
    gathered = _gather_weights(placed, [kinds[name] for name in first_names])
    first = even_columns(dict(zip(first_names, gathered)))
    placed = [_place_quarter(quarters[name], kinds[name], kc, f"place_rest_{name}", 1 if name in first_names else 0)
              for name in rest_names]
    rest_kinds = [kinds[name] for name in rest_names]
    sems, on_their_way, token = _gather_start(placed, rest_kinds, [conv_slots, *gathered], "rest")

    def rest_after(value):
        landed = _gather_wait(sems, on_their_way, rest_kinds, value, "rest")
        return even_columns(dict(zip(rest_names, _gather_forward(landed, rest_kinds, "rest"))))

    whole = {}
    conv_rows = math.prod(w["dn_conv_w"].shape) // LANES
    conv_quarters = [conv_slots[2 * q, :conv_rows].reshape(w["dn_conv_w"].shape) for q in range(4)]
    whole["dn_conv_w"] = jnp.concatenate(conv_quarters, axis=-1)
    for name in SMALL:
        if name != "dn_conv_w":
            whole[name] = w[name]

    updated = {name: [lax.empty(w[name].shape, F32) for _ in range(4)] for name in kinds}

    def on_reduced(layer, layer_grads):
        for name, g in layer_grads.items():
            if name == "w_in_even":
                g = g[..., :EVEN_QUARTER]
            stacked_layer = layer if w[name].shape[0] == w["norm_mix"].shape[0] else layer // 2
            updated[name] = _adamw_layer(w[name], g, m[name], v[name], stacked_layer, updated[name], f"adamw_{name}")

    loss, dx, small = _forward_backward(x, target, whole, first, rest_after, token, on_reduced)

    _, small_sum = _small_exchange(_pack([small[n] for n in SMALL]))
    grads = dict(zip(SMALL, _unpack(small_sum, [small[n].shape for n in SMALL])))
    grads["dn_conv_w"] = lax.dynamic_slice_in_dim(grads["dn_conv_w"], k * n_conv, n_conv, axis=2)
    delta, new_m, new_v = {}, {}, {}
    for name in kinds:
        grads[name], delta[name], new_m[name], new_v[name] = updated[name]
    packs = [_pack([d[n] for n in SMALL]) for d in (w, grads, m, v)]
    shapes = [w[n].shape for n in SMALL]
    for out, res in zip((delta, new_m, new_v), _adamw(*packs, "adamw_small")):
        out.update(zip(SMALL, _unpack(res, shapes)))
    total_loss = lax.psum(loss[0, 0], ("x", "y", "c"))
    return total_loss, dx, grads, delta, new_m, new_v


def kernel(x, norm_ffn1, ffn1_w_gu, ffn1_w_down, norm_mix, w_in_even, dn_conv_w, dn_a_log, dn_dt_bias, dn_norm_g, fox_q_norm_g, fox_k_norm_g, fox_f_bias, w_out_even, w_in_odd, w_out_odd, norm_ffn2, ffn2_w_gu, ffn2_w_down, loss_target, m_norm_ffn1, m_ffn1_w_gu, m_ffn1_w_down, m_norm_mix, m_w_in_even, m_dn_conv_w, m_dn_a_log, m_dn_dt_bias, m_dn_norm_g, m_fox_q_norm_g, m_fox_k_norm_g, m_fox_f_bias, m_w_out_even, m_w_in_odd, m_w_out_odd, m_norm_ffn2, m_ffn2_w_gu, m_ffn2_w_down, v_norm_ffn1, v_ffn1_w_gu, v_ffn1_w_down, v_norm_mix, v_w_in_even, v_dn_conv_w, v_dn_a_log, v_dn_dt_bias, v_dn_norm_g, v_fox_q_norm_g, v_fox_k_norm_g, v_fox_f_bias, v_w_out_even, v_w_in_odd, v_w_out_odd, v_norm_ffn2, v_ffn2_w_gu, v_ffn2_w_down):
    w = dict(zip(WEIGHTS, (norm_ffn1, ffn1_w_gu, ffn1_w_down, norm_mix, w_in_even, dn_conv_w, dn_a_log, dn_dt_bias,
                           dn_norm_g, fox_q_norm_g, fox_k_norm_g, fox_f_bias, w_out_even, w_in_odd, w_out_odd,
                           norm_ffn2, ffn2_w_gu, ffn2_w_down)))
    m = dict(zip(WEIGHTS, (m_norm_ffn1, m_ffn1_w_gu, m_ffn1_w_down, m_norm_mix, m_w_in_even, m_dn_conv_w, m_dn_a_log,
                           m_dn_dt_bias, m_dn_norm_g, m_fox_q_norm_g, m_fox_k_norm_g, m_fox_f_bias, m_w_out_even,
                           m_w_in_odd, m_w_out_odd, m_norm_ffn2, m_ffn2_w_gu, m_ffn2_w_down)))
    v = dict(zip(WEIGHTS, (v_norm_ffn1, v_ffn1_w_gu, v_ffn1_w_down, v_norm_mix, v_w_in_even, v_dn_conv_w, v_dn_a_log,
                           v_dn_dt_bias, v_dn_norm_g, v_fox_q_norm_g, v_fox_k_norm_g, v_fox_f_bias, v_w_out_even,
                           v_w_in_odd, v_w_out_odd, v_norm_ffn2, v_ffn2_w_gu, v_ffn2_w_down)))
    loss, dx, grads, delta, new_m, new_v = _step(x[0], loss_target[0], w, m, v)
    return (loss, dx[None], *[grads[n] for n in WEIGHTS], *[delta[n] for n in WEIGHTS],
            *[new_m[n] for n in WEIGHTS], *[new_v[n] for n in WEIGHTS])
```

```python
import functools
import math

import jax
import jax.numpy as jnp
from jax import lax
from jax.experimental import pallas as pl
from jax.experimental.pallas import tpu as pltpu

F32 = jnp.float32
BF16 = jnp.bfloat16
HI = lax.Precision.HIGH

HEAD_DIM = 128
N_DN_HEADS = 4
N_FOX_HEADS = 4
N_SB_HEADS = 8
D_DN = N_DN_HEADS * HEAD_DIM
D_FOX = N_FOX_HEADS * HEAD_DIM
CONV_WIDTH = 4
DN_CHUNK = 64
EPS = 1e-6
ATT_SCALE = HEAD_DIM ** -0.5
ADAM_LR, ADAM_B1, ADAM_B2, ADAM_EPS, ADAM_WD, ADAM_STEP = 0.001, 0.9, 0.999, 1e-08, 0.01, 10

V7X_VMEM_LIMIT = 56 * 1024 * 1024
LANES = 128
ATT_TQ = 512
ATT_TK = 128
ATT_SUB = ATT_TQ // ATT_TK

LANE_BETA, LANE_DECAY, LANE_FORGET = 0, 4, 8


def _cparams(*sem):
    return pltpu.CompilerParams(dimension_semantics=sem, vmem_limit_bytes=V7X_VMEM_LIMIT)


def _sigmoid(x):
    return 1.0 / (1.0 + jnp.exp(-x))


def _softplus(x):
    return jnp.maximum(x, 0.0) + jnp.log(1.0 + jnp.exp(-jnp.abs(x)))


def _silu_grad(y, sg):
    return sg * (1.0 + y * (1.0 - sg))


def _rowwise(fn, rows, bcast, outs, sums, *, tile, name):
    rows = [r if isinstance(r, tuple) else (r, r.shape[1], 0) for r in rows]
    s = rows[0][0].shape[0]
    assert s % tile == 0
    n_in, n_b, n_out, n_sum = len(rows), len(bcast), len(outs), len(sums)

    def body(*refs):
        ins = [r[...] for r in refs[:n_in + n_b]]
        res = fn(*ins)
        if not isinstance(res, (tuple, list)):
            res = (res,)
        out_refs = refs[n_in + n_b:n_in + n_b + n_out]
        sum_refs = refs[n_in + n_b + n_out:]
        for o_ref, val in zip(out_refs, res[:n_out]):
            o_ref[...] = val.astype(o_ref.dtype)
        if n_sum:
            @pl.when(pl.program_id(0) == 0)
            def _():
                for s_ref in sum_refs:
                    s_ref[...] = jnp.zeros_like(s_ref)
            for s_ref, val in zip(sum_refs, res[n_out:]):
                s_ref[...] += val

    in_specs = [pl.BlockSpec((tile, w), lambda i, cb=cb: (i, cb)) for _, w, cb in rows]
    in_specs += [pl.BlockSpec(b.shape, lambda i, nd=b.ndim: (0,) * nd) for b in bcast]
    out_specs = [pl.BlockSpec((tile, c), lambda i: (i, 0)) for c, _ in outs]
    out_specs += [pl.BlockSpec(sh, lambda i: (0, 0)) for sh in sums]
    out_shape = [jax.ShapeDtypeStruct((s, c), dt) for c, dt in outs]
    out_shape += [jax.ShapeDtypeStruct(sh, F32) for sh in sums]
    return pl.pallas_call(
        body, name=name, grid=(s // tile,), in_specs=in_specs, out_specs=out_specs, out_shape=out_shape,
        compiler_params=_cparams("arbitrary" if n_sum else "parallel"),
    )(*[r[0] for r in rows], *bcast)


def _rms_fwd(x, gain, name):
    def fn(xb, g):
        r = lax.rsqrt(jnp.mean(xb * xb, axis=-1, keepdims=True) + EPS)
        return (xb * r * g,)
    return _rowwise(fn, [x], [gain], [(x.shape[1], BF16)], [], tile=512, name=name)[0]


_DIMS = {"nn": (((1,), (0,)), ((), ())), "nt": (((1,), (1,)), ((), ())), "tn": (((0,), (0,)), ((), ()))}


def _dot(a, b, kind):
    return lax.dot_general(a.astype(BF16), b.astype(BF16), _DIMS[kind], preferred_element_type=F32)


def _dot32(a, b, kind="nn"):
    return lax.dot_general(a, b, _DIMS[kind], precision=HI, preferred_element_type=F32)


def _mm(a, b, kind, *, tm, tn, out_dtype, name, scale=None, residual=None, a_lead=(), b_lead=(),
        b_spec=None, n=None, into=None, after=None):
    ash, bsh = a.shape[len(a_lead):], b.shape[len(b_lead):]
    m = ash[1] if kind == "tn" else ash[0]
    k = ash[0] if kind == "tn" else ash[1]
    if b_spec is None:
        n = bsh[0] if kind == "nt" else bsh[1]
        assert k == (bsh[1] if kind == "nt" else bsh[0]), (ash, bsh, kind)
    assert m % tm == 0 and n % tn == 0, (m, tm, n, tn)
    la, lb = (None,) * len(a_lead), (None,) * len(b_lead)
    if kind == "tn":
        a_spec = pl.BlockSpec(la + (k, tm), lambda j, i: a_lead + (0, i))
    else:
        a_spec = pl.BlockSpec(la + (tm, k), lambda j, i: a_lead + (i, 0))
    if b_spec is None:
        if kind == "nt":
            b_spec = pl.BlockSpec(lb + (tn, k), lambda j, i: b_lead + (j, 0))
        else:
            b_spec = pl.BlockSpec(lb + (k, tn), lambda j, i: b_lead + (0, j))
    in_specs, args = [a_spec, b_spec], [a, b]
    if residual is not None:
        in_specs.append(pl.BlockSpec((tm, tn), lambda j, i: (i, j)))
        args.append(residual)
    aliases = {}
    if after is not None:
        in_specs.append(pl.BlockSpec(memory_space=pl.ANY))
        args.append(after)
    if into is not None:
        buf, layer = into
        in_specs.append(pl.BlockSpec(memory_space=pl.ANY))
        args.append(buf)
        aliases = {len(args) - 1: 0}
        out_spec = pl.BlockSpec((None, tm, tn), lambda j, i: (layer, i, j))
        out_shape = jax.ShapeDtypeStruct(buf.shape, buf.dtype)
    else:
        out_spec = pl.BlockSpec((tm, tn), lambda j, i: (i, j))
        out_shape = jax.ShapeDtypeStruct((m, n), out_dtype)

    def body(a_ref, b_ref, *rest):
        acc = _dot(a_ref[...], b_ref[...], kind)
        if scale is not None:
            acc = acc * scale
        if residual is not None:
            acc = acc + rest[0][...]
        rest[-1][...] = acc.astype(rest[-1].dtype)

    return pl.pallas_call(
        body, name=name, grid=(n // tn, m // tm), in_specs=in_specs, out_specs=out_spec, out_shape=out_shape,
        input_output_aliases=aliases, compiler_params=_cparams("parallel", "parallel"),
    )(*args)


def _ffn_up(n, w_gu, layer, name):
    s, d = n.shape
    f = w_gu.shape[2] // 2
    tm, tn = 512, f // 2
    nj = f // tn

    def body(n_ref, wg_ref, wu_ref, gu_ref, a_ref):
        nv = n_ref[...]
        g = _dot(nv, wg_ref[...], "nn")
        u = _dot(nv, wu_ref[...], "nn")
        gu_ref[0] = g.astype(BF16)
        gu_ref[1] = u.astype(BF16)
        a_ref[...] = (g * _sigmoid(g) * u).astype(BF16)

    return pl.pallas_call(
        body, name=name, grid=(nj, s // tm),
        in_specs=[pl.BlockSpec((tm, d), lambda j, i: (i, 0)),
                  pl.BlockSpec((None, d, tn), lambda j, i: (layer, 0, j)),
                  pl.BlockSpec((None, d, tn), lambda j, i: (layer, 0, j + nj))],
        out_specs=[pl.BlockSpec((2, tm, tn), lambda j, i: (0, i, j)),
                   pl.BlockSpec((tm, tn), lambda j, i: (i, j))],
        out_shape=[jax.ShapeDtypeStruct((2, s, f), BF16), jax.ShapeDtypeStruct((s, f), BF16)],
        compiler_params=_cparams("parallel", "parallel"),
    )(n, w_gu, w_gu)


def _ffn_down_bwd(dxo, w_down, gu, layer, name, after=None):
    s, d = dxo.shape
    f = w_down.shape[1]
    tm, tn = 512, f // 2
    extra_specs, extra = ([ANY], [after]) if after is not None else ([], [])

    def body(dx_ref, w_ref, gu_ref, *rest):
        dgu_ref = rest[-1]
        da = 0.5 * _dot(dx_ref[...], w_ref[...], "nt")
        g = gu_ref[0].astype(F32)
        u = gu_ref[1].astype(F32)
        sg = _sigmoid(g)
        dgu_ref[0] = (da * u * _silu_grad(g, sg)).astype(BF16)
        dgu_ref[1] = (da * g * sg).astype(BF16)

    return pl.pallas_call(
        body, name=name, grid=(f // tn, s // tm),
        in_specs=[pl.BlockSpec((tm, d), lambda j, i: (i, 0)),
                  pl.BlockSpec((None, tn, d), lambda j, i: (layer, j, 0)),
                  pl.BlockSpec((2, tm, tn), lambda j, i: (0, i, j))] + extra_specs,
        out_specs=pl.BlockSpec((2, tm, tn), lambda j, i: (0, i, j)),
        out_shape=jax.ShapeDtypeStruct((2, s, f), BF16),
        compiler_params=_cparams("parallel", "parallel"),
    )(dxo, w_down, gu, *extra)


NORM_BWD_TM = 256


def _norm_bwd_after(terms, operands, specs, x, dres, gain, name):
    s, d = x.shape
    tm = NORM_BWD_TM
    n_op = len(operands)

    def body(*refs):
        x_ref, dres_ref, g_ref = refs[n_op:n_op + 3]
        dx_ref, dx16_ref, dgain_ref = refs[n_op + 3:]
        dn = None
        for a, b in terms(*refs[:n_op]):
            dn = _dot(a, b, "nt") if dn is None else dn + _dot(a, b, "nt")
        xb = x_ref[...]
        r = lax.rsqrt(jnp.mean(xb * xb, axis=-1, keepdims=True) + EPS)
        xh = xb * r
        dxh = dn * g_ref[...]
        dx = dres_ref[...] + r * (dxh - xh * jnp.mean(dxh * xh, axis=-1, keepdims=True))
        dx_ref[...] = dx
        dx16_ref[...] = dx.astype(BF16)

        @pl.when(pl.program_id(0) == 0)
        def _():
            dgain_ref[...] = jnp.zeros_like(dgain_ref)
        dgain_ref[...] += jnp.sum(dn * xh, axis=0, keepdims=True)

    rows = pl.BlockSpec((tm, d), lambda i: (i, 0))
    return pl.pallas_call(
        body, name=name, grid=(s // tm,),
        in_specs=list(specs) + [rows, rows, pl.BlockSpec((1, d), lambda i: (0, 0))],
        out_specs=[rows, rows, pl.BlockSpec((1, d), lambda i: (0, 0))],
        out_shape=[jax.ShapeDtypeStruct((s, d), F32), jax.ShapeDtypeStruct((s, d), BF16),
                   jax.ShapeDtypeStruct((1, d), F32)],
        compiler_params=_cparams("arbitrary"),
    )(*operands, x, dres, gain)


def _ffn_up_bwd(dgu, w_gu, layer, x, dres, gain, name):
    _, s, f = dgu.shape
    d = w_gu.shape[1]
    specs = [pl.BlockSpec((2, NORM_BWD_TM, f), lambda i: (0, i, 0)),
             pl.BlockSpec((None, d, f), lambda i: (layer, 0, 0)),
             pl.BlockSpec((None, d, f), lambda i: (layer, 0, 1))]
    terms = lambda dgu_ref, wg_ref, wu_ref: [(dgu_ref[0], wg_ref[...]), (dgu_ref[1], wu_ref[...])]
    return _norm_bwd_after(terms, [dgu, w_gu, w_gu], specs, x, dres, gain, name)


def _in_proj_bwd(dproj, w_in, j, x, dres, gain, name):
    k = dproj.shape[1]
    d = w_in.shape[1]
    specs = [pl.BlockSpec((NORM_BWD_TM, k), lambda i: (i, 0)), pl.BlockSpec((None, d, k), lambda i: (j, 0, 0))]
    terms = lambda a_ref, b_ref: [(a_ref[...], b_ref[...])]
    return _norm_bwd_after(terms, [dproj, w_in], specs, x, dres, gain, name)


def _ffn_fwd(x, gain, w_gu, w_down, layer, tag):
    n = _rms_fwd(x, gain, f"{tag}_norm")
    gu, a = _ffn_up(n, w_gu, layer, f"{tag}_up")
    x2 = _mm(a, w_down, "nn", tm=512, tn=x.shape[1], out_dtype=F32, name=f"{tag}_down", scale=0.5, residual=x,
             b_lead=(layer,))
    return x2, (x, n, gu, a)


def _ffn_bwd(dxo, dxo16, saved, gain, w_gu, w_down, layer, tag, g_gu, g_down, after=None):
    x, n, gu, a = saved
    s, f = a.shape
    dgu = _ffn_down_bwd(dxo16, w_down, gu, layer, f"{tag}_down_bwd", after)
    g_down = _mm(a, dxo16, "tn", tm=256, tn=dxo16.shape[1], out_dtype=F32, name=f"{tag}_down_dw", scale=0.5,
                 into=(g_down, 0))
    tn = f // 2
    nj = f // tn
    g_gu = _mm(n, dgu, "tn", tm=512, tn=tn, out_dtype=F32, name=f"{tag}_up_dw", into=(g_gu, 0), n=2 * f,
               b_spec=pl.BlockSpec((None, s, tn), lambda j, i: (j // nj, 0, j % nj)))
    dx, dx16, dgain = _ffn_up_bwd(dgu, w_gu, layer, x, dxo, gain, f"{tag}_up_bwd")
    return dx, dx16, dgain, g_gu, g_down


def _lane_col(blk, lane):
    li = lax.broadcasted_iota(jnp.int32, blk.shape, 1)
    return jnp.sum(jnp.where(li == lane, blk, 0.0), axis=1, keepdims=True)


def _split_dot(x, tri):
    hi = x.astype(BF16)
    lo = (x - hi.astype(F32)).astype(BF16)
    return (lax.dot_general(hi, tri, _DIMS["nn"], preferred_element_type=F32)
            + lax.dot_general(lo, tri, _DIMS["nn"], preferred_element_type=F32))


class _Each:
    def __init__(self, vals):
        self.vals = list(vals)

    def _with(self, other, op):
        others = other.vals if isinstance(other, _Each) else [other] * len(self.vals)
        return _Each(op(a, b) for a, b in zip(self.vals, others))

    def __add__(self, other):
        return self._with(other, lambda a, b: a + b)

    def __sub__(self, other):
        return self._with(other, lambda a, b: a - b)

    def __mul__(self, other):
        return self._with(other, lambda a, b: a * b)

    def __neg__(self):
        return _Each(-a for a in self.vals)


def _each(fn, *args):
    n = max(len(a.vals) for a in args if isinstance(a, _Each))
    res = [fn(*xs) for xs in zip(*[a.vals if isinstance(a, _Each) else [a] * n for a in args])]
    if isinstance(res[0], tuple):
        return tuple(_Each(r) for r in zip(*res))
    return _Each(res)


def _keep(cond, x):
    return _each(lambda v: jnp.where(cond, v, 0.0), x)


def _rowsum(x):
    return _each(lambda v: jnp.sum(v, axis=1, keepdims=True), x)


ATT_HEADS = 2
ATT_WIDTH = ATT_HEADS * HEAD_DIM
_HEAD_COLS = [slice(h * HEAD_DIM, (h + 1) * HEAD_DIM) for h in range(ATT_HEADS)]


def _att_specs(n_heads, s):
    groups = n_heads // ATT_HEADS
    q_spec = pl.BlockSpec((ATT_TQ, ATT_WIDTH), lambda g, i: (i, g))
    k_spec = pl.BlockSpec((s, ATT_WIDTH), lambda g, i: (0, groups + g))
    v_spec = pl.BlockSpec((s, ATT_WIDTH), lambda g, i: (0, 2 * groups + g))
    return q_spec, k_spec, v_spec


def _heads_of(ref, rows=None):
    return _Each(ref[:, cs] if rows is None else ref[rows, cs] for cs in _HEAD_COLS)


def _dot_each(a, b, kind):
    return _each(lambda x, y: _dot(x, y, kind), a, b)


def _att_iotas():
    row = lax.broadcasted_iota(jnp.int32, (ATT_TQ, ATT_TK), 0)
    col = lax.broadcasted_iota(jnp.int32, (ATT_TQ, ATT_TK), 1)
    jr = lax.broadcasted_iota(jnp.int32, (ATT_TK, ATT_TK), 0)
    jc = lax.broadcasted_iota(jnp.int32, (ATT_TK, ATT_TK), 1)
    return row, col, jr, jc


def _sb_fwd(qkv, n_heads, name):
    s = qkv.shape[0]

    def body(q_ref, k_ref, v_ref, o16_ref, o32_ref):
        i = pl.program_id(1)
        q = _heads_of(q_ref)
        row, col, jr, jc = _att_iotas()
        later = (jr > jc).astype(BF16)

        def step(jb, carry, diagonal):
            c_sp, acc = (_Each(part) for part in carry)
            work = []
            for sub in reversed(range(ATT_SUB)):
                keys = pl.ds(pl.multiple_of(jb * ATT_TQ + sub * ATT_TK, ATT_TK), ATT_TK)
                z = _dot_each(q, _heads_of(k_ref, keys), "nt") * ATT_SCALE
                sp = _each(_softplus, z)
                before = (col + sub * ATT_TK) < row if diagonal else None
                spm = _keep(before, sp) if diagonal else sp
                work.append((keys, z - sp, spm, _each(lambda x: _dot(x, later, "nn"), spm), before))
            for keys, logsig, spm, within, before in work:
                a = _each(jnp.exp, logsig - (c_sp + within))
                if diagonal:
                    a = _keep(before, a)
                acc = acc + _each(_split_dot, a, _heads_of(v_ref, keys))
                c_sp = c_sp + _rowsum(spm)
            return tuple(c_sp.vals), tuple(acc.vals)

        zeros = lambda width: tuple(jnp.zeros((ATT_TQ, width), F32) for _ in range(ATT_HEADS))
        carry = step(i, (zeros(1), zeros(HEAD_DIM)), True)
        _, acc = lax.fori_loop(0, i, lambda it, cr: step(i - 1 - it, cr, False), carry)
        for cs, acc_h in zip(_HEAD_COLS, acc):
            o16_ref[:, cs] = acc_h.astype(BF16)
            o32_ref[:, cs] = acc_h

    q_spec, k_spec, v_spec = _att_specs(n_heads, s)
    o_spec = pl.BlockSpec((ATT_TQ, ATT_WIDTH), lambda g, i: (i, g))
    return pl.pallas_call(
        body, name=name, grid=(n_heads // ATT_HEADS, s // ATT_TQ), in_specs=[q_spec, k_spec, v_spec],
        out_specs=[o_spec, o_spec],
        out_shape=[jax.ShapeDtypeStruct((s, n_heads * HEAD_DIM), BF16),
                   jax.ShapeDtypeStruct((s, n_heads * HEAD_DIM), F32)],
        compiler_params=_cparams("parallel", "arbitrary"),
    )(qkv, qkv, qkv)


def _sb_bwd(qkv, o32, do, n_heads, name):
    s = qkv.shape[0]

    def body(q_ref, k_ref, v_ref, o_ref, do_ref, dq_ref, dk_ref, dv_ref):
        i = pl.program_id(1)

        @pl.when(i == 0)
        def _():
            dk_ref[...] = jnp.zeros_like(dk_ref)
            dv_ref[...] = jnp.zeros_like(dv_ref)

        q, do = _heads_of(q_ref), _heads_of(do_ref)
        total = _rowsum(_each(lambda a, b: a.astype(F32) * b, do, _heads_of(o_ref)))
        row, col, jr, jc = _att_iotas()
        later = (jr > jc).astype(BF16)
        not_before = (jr >= jc).astype(BF16)

        def step(jb, carry, diagonal):
            c_sp, c_e, dq = (_Each(part) for part in carry)
            work = []
            for sub in reversed(range(ATT_SUB)):
                keys = pl.ds(pl.multiple_of(jb * ATT_TQ + sub * ATT_TK, ATT_TK), ATT_TK)
                k = _heads_of(k_ref, keys)
                z = _dot_each(q, k, "nt") * ATT_SCALE
                sp = _each(_softplus, z)
                before = (col + sub * ATT_TK) < row if diagonal else None
                spm = _keep(before, sp) if diagonal else sp
                work.append((keys, k, _each(jnp.exp, z - sp), spm, _each(lambda x: _dot(x, later, "nn"), spm),
                             _dot_each(do, _heads_of(v_ref, keys), "nt"), before))
            for keys, k, sig, spm, within, da, before in work:
                a = sig * _each(lambda x: jnp.exp(-x), c_sp + within)
                if diagonal:
                    a = _keep(before, a)
                e = a * da
                left = total - c_e - _each(lambda x: _split_dot(x, not_before), e)
                dz = (e - (e + left) * sig) * ATT_SCALE
                if diagonal:
                    dz = _keep(before, dz)
                dk, dv = _dot_each(dz, q, "tn"), _dot_each(a, do, "tn")
                for cs, dk_h, dv_h in zip(_HEAD_COLS, dk.vals, dv.vals):
                    dk_ref[keys, cs] += dk_h
                    dv_ref[keys, cs] += dv_h
                dq = dq + _dot_each(dz, k, "nn")
                c_sp = c_sp + _rowsum(spm)
                c_e = c_e + _rowsum(e)
            return tuple(c_sp.vals), tuple(c_e.vals), tuple(dq.vals)

        zeros = lambda width: tuple(jnp.zeros((ATT_TQ, width), F32) for _ in range(ATT_HEADS))
        carry = step(i, (zeros(1), zeros(1), zeros(HEAD_DIM)), True)
        _, _, dq = lax.fori_loop(0, i, lambda it, cr: step(i - 1 - it, cr, False), carry)
        for cs, dq_h in zip(_HEAD_COLS, dq):
            dq_ref[:, cs] = dq_h.astype(BF16)

    q_spec, k_spec, v_spec = _att_specs(n_heads, s)
    blk = pl.BlockSpec((ATT_TQ, ATT_WIDTH), lambda g, i: (i, g))
    full = pl.BlockSpec((s, ATT_WIDTH), lambda g, i: (0, g))
    wide = (s, n_heads * HEAD_DIM)
    return pl.pallas_call(
        body, name=name, grid=(n_heads // ATT_HEADS, s // ATT_TQ), in_specs=[q_spec, k_spec, v_spec, blk, blk],
        out_specs=[blk, full, full],
        out_shape=[jax.ShapeDtypeStruct(wide, BF16), jax.ShapeDtypeStruct(wide, F32), jax.ShapeDtypeStruct(wide, F32)],
        compiler_params=_cparams("parallel", "arbitrary"),
    )(qkv, qkv, qkv, o32, do)


def _fox_logits(q, k, cq, ct_ref, keys):
    ck = _Each(ct_ref[h, :, keys] for h in range(ATT_HEADS))
    return _dot_each(q, k, "nt") * ATT_SCALE + (cq - ck)


def _fox_cq(c_ref, group):
    c = c_ref[...]
    return _Each(_lane_col(c, LANE_FORGET + group * ATT_HEADS + h) for h in range(ATT_HEADS))


def _fox_fwd(qkv, c, ct, name):
    s = qkv.shape[0]
    n_heads = N_FOX_HEADS

    def body(q_ref, k_ref, v_ref, c_ref, ct_ref, o_ref, lse_ref):
        g, i = pl.program_id(0), pl.program_id(1)
        q = _heads_of(q_ref)
        cq = _fox_cq(c_ref, g)
        row, col, _, _ = _att_iotas()

        def step(jb, carry, diagonal):
            m, l, acc = (_Each(part) for part in carry)
            work = []
            m_new = m
            for sub in range(ATT_SUB):
                keys = pl.ds(pl.multiple_of(jb * ATT_TQ + sub * ATT_TK, ATT_TK), ATT_TK)
                sc = _fox_logits(q, _heads_of(k_ref, keys), cq, ct_ref, keys)
                valid = (col + sub * ATT_TK) <= row if diagonal else None
                if diagonal:
                    sc = _each(lambda x: jnp.where(valid, x, -1e30), sc)
                m_new = _each(lambda a, x: jnp.maximum(a, jnp.max(x, axis=1, keepdims=True)), m_new, sc)
                work.append((keys, sc, valid))
            w = _each(jnp.exp, m - m_new)
            l, acc = l * w, acc * w
            for keys, sc, valid in work:
                p = _each(jnp.exp, sc - m_new)
                if diagonal:
                    p = _keep(valid, p)
                l = l + _rowsum(p)
                acc = acc + _each(_split_dot, p, _heads_of(v_ref, keys))
            return tuple(m_new.vals), tuple(l.vals), tuple(acc.vals)

        per_head = lambda width, value: tuple(jnp.full((ATT_TQ, width), value, F32) for _ in range(ATT_HEADS))
        init = (per_head(1, -1e30), per_head(1, 0.0), per_head(HEAD_DIM, 0.0))
        m, l, acc = lax.fori_loop(0, i, lambda jb, cr: step(jb, cr, False), step(i, init, True))
        for h, cs in enumerate(_HEAD_COLS):
            o_ref[:, cs] = acc[h] / l[h]
            lse_ref[h] = jnp.broadcast_to(m[h] + jnp.log(l[h]), (ATT_TQ, LANES))

    q_spec, k_spec, v_spec = _att_specs(n_heads, s)
    return pl.pallas_call(
        body, name=name, grid=(n_heads // ATT_HEADS, s // ATT_TQ),
        in_specs=[q_spec, k_spec, v_spec, pl.BlockSpec((ATT_TQ, LANES), lambda g, i: (i, 0)),
                  pl.BlockSpec((ATT_HEADS, 1, s), lambda g, i: (g, 0, 0))],
        out_specs=[pl.BlockSpec((ATT_TQ, ATT_WIDTH), lambda g, i: (i, g)),
                   pl.BlockSpec((ATT_HEADS, ATT_TQ, LANES), lambda g, i: (g, i, 0))],
        out_shape=[jax.ShapeDtypeStruct((s, n_heads * HEAD_DIM), F32),
                   jax.ShapeDtypeStruct((n_heads, s, LANES), F32)],
        compiler_params=_cparams("parallel", "arbitrary"),
    )(qkv, qkv, qkv, c, ct)


def _fox_bwd(qkv, c, ct, o, lse, do, name):
    s = qkv.shape[0]
    n_heads = N_FOX_HEADS

    def body(q_ref, k_ref, v_ref, c_ref, ct_ref, o_ref, lse_ref, do_ref, dq_ref, dk_ref, dv_ref, dct_ref):
        g, i = pl.program_id(0), pl.program_id(1)

        @pl.when(i == 0)
        def _():
            dk_ref[...] = jnp.zeros_like(dk_ref)
            dv_ref[...] = jnp.zeros_like(dv_ref)
            dct_ref[...] = jnp.zeros_like(dct_ref)

        q = _heads_of(q_ref)
        do16 = _each(lambda x: x.astype(BF16), _heads_of(do_ref))
        delta = _rowsum(_each(lambda a, b: a.astype(F32) * b, do16, _heads_of(o_ref)))
        lse_col = _Each(lse_ref[h, :, 0:1] for h in range(ATT_HEADS))
        cq = _fox_cq(c_ref, g)
        row, col, _, _ = _att_iotas()

        def step(jb, dq, diagonal):
            dq = _Each(dq)
            for sub in range(ATT_SUB):
                keys = pl.ds(pl.multiple_of(jb * ATT_TQ + sub * ATT_TK, ATT_TK), ATT_TK)
                k = _heads_of(k_ref, keys)
                sc = _fox_logits(q, k, cq, ct_ref, keys)
                if diagonal:
                    valid = (col + sub * ATT_TK) <= row
                    p = _keep(valid, _each(jnp.exp, _keep(valid, sc) - lse_col))
                else:
                    p = _each(jnp.exp, sc - lse_col)
                ds = p * (_dot_each(do16, _heads_of(v_ref, keys), "nt") - delta)
                dss = ds * ATT_SCALE
                dk, dv = _dot_each(dss, q, "tn"), _dot_each(p, do16, "tn")
                for h, cs in enumerate(_HEAD_COLS):
                    dct_ref[h, :, keys] -= jnp.sum(ds.vals[h], axis=0, keepdims=True)
                    dk_ref[keys, cs] += dk.vals[h]
                    dv_ref[keys, cs] += dv.vals[h]
                dq = dq + _dot_each(dss, k, "nn")
            return tuple(dq.vals)

        dq0 = step(i, tuple(jnp.zeros((ATT_TQ, HEAD_DIM), F32) for _ in range(ATT_HEADS)), True)
        dq = lax.fori_loop(0, i, lambda jb, dq: step(jb, dq, False), dq0)
        for cs, dq_h in zip(_HEAD_COLS, dq):
            dq_ref[:, cs] = dq_h

    q_spec, k_spec, v_spec = _att_specs(n_heads, s)
    blk = pl.BlockSpec((ATT_TQ, ATT_WIDTH), lambda g, i: (i, g))
    full = pl.BlockSpec((s, ATT_WIDTH), lambda g, i: (0, g))
    wide = jax.ShapeDtypeStruct((s, n_heads * HEAD_DIM), F32)
    return pl.pallas_call(
        body, name=name, grid=(n_heads // ATT_HEADS, s // ATT_TQ),
        in_specs=[q_spec, k_spec, v_spec, pl.BlockSpec((ATT_TQ, LANES), lambda g, i: (i, 0)),
                  pl.BlockSpec((ATT_HEADS, 1, s), lambda g, i: (g, 0, 0)), blk,
                  pl.BlockSpec((ATT_HEADS, ATT_TQ, LANES), lambda g, i: (g, i, 0)), blk],
        out_specs=[blk, full, full, pl.BlockSpec((ATT_HEADS, 1, s), lambda g, i: (g, 0, 0))],
        out_shape=[wide, wide, wide, jax.ShapeDtypeStruct((n_heads, 1, s), F32)],
        compiler_params=_cparams("parallel", "arbitrary"),
    )(qkv, qkv, qkv, c, ct, o, lse, do)


def _cumsum_rows(x, reverse, name):
    s = x.shape[0]
    nb = s // LANES

    def body(x_ref, o_ref):
        r = lax.broadcasted_iota(jnp.int32, (LANES, LANES), 0)
        c = lax.broadcasted_iota(jnp.int32, (LANES, LANES), 1)
        tri = ((r <= c) if reverse else (r >= c)).astype(F32)

        def step(it, carry):
            b = (nb - 1 - it) if reverse else it
            off = pl.multiple_of(b * LANES, LANES)
            blk = x_ref[pl.ds(off, LANES), :]
            o_ref[pl.ds(off, LANES), :] = _dot32(tri, blk) + carry
            return carry + jnp.sum(blk, axis=0, keepdims=True)

        lax.fori_loop(0, nb, step, jnp.zeros((1, LANES), F32))

    return pl.pallas_call(body, name=name, out_shape=jax.ShapeDtypeStruct(x.shape, F32),
                          compiler_params=pltpu.CompilerParams(vmem_limit_bytes=V7X_VMEM_LIMIT))(x)


def _dot32_each(a, b, kind="nn"):
    return _each(lambda x, y: _dot32(x, y, kind), a, b)


def _unit_lower_inverse(m, ri, ci):
    c = ri.shape[0]
    t = -_keep(ri // 2 == ci // 2, m) + jnp.where(ri == ci, 1.0, 0.0)
    b = 4
    while b <= c:
        off_diag = (ri // b == ci // b) & (ri % b >= b // 2) & (ci % b < b // 2)
        t = t - _dot32_each(_dot32_each(t, _keep(off_diag, m)), t)
        b *= 2
    return t


def _dn_gates(g, ri, ci):
    eye = ri == ci
    incl = ri >= ci
    g_row = jnp.sum(jnp.where(eye, g, 0.0), axis=0, keepdims=True)
    gc = jnp.sum(jnp.where(incl, g_row, 0.0), axis=1, keepdims=True)
    gc_row = jnp.sum(jnp.where(eye, gc, 0.0), axis=0, keepdims=True)
    dmat = jnp.where(incl, jnp.exp(jnp.where(incl, gc - gc_row, 0.0)), 0.0)
    gc_last = jnp.sum(g, axis=0, keepdims=True)
    return gc, dmat, jnp.exp(gc), jnp.exp(gc_last - gc), jnp.exp(gc_last)


def _dn_fwd(qkv, act, name):
    s = qkv.shape[0]
    c, d, nh = DN_CHUNK, HEAD_DIM, N_DN_HEADS
    nc = s // c

    def body(q_ref, k_ref, v_ref, act_ref, o_ref, s_ref, t_ref, state):
        @pl.when(pl.program_id(0) == 0)
        def _():
            state[...] = jnp.zeros_like(state)

        ri = lax.broadcasted_iota(jnp.int32, (c, c), 0)
        ci = lax.broadcasted_iota(jnp.int32, (c, c), 1)
        act = act_ref[...]
        heads = range(nh)
        cols = [slice(h * d, (h + 1) * d) for h in heads]
        q, k, v = (_Each(ref[:, cs] for cs in cols) for ref in (q_ref, k_ref, v_ref))
        beta = _Each(_lane_col(act, LANE_BETA + h) for h in heads)
        g = _Each(_lane_col(act, LANE_DECAY + h) for h in heads)
        _, dmat, e, r, gl = _each(lambda gh: _dn_gates(gh, ri, ci), g)
        s0 = _Each(state[h] for h in heads)
        kb = beta * k
        t = _unit_lower_inverse(_keep(ri > ci, _dot32_each(kb, k, "nt") * dmat), ri, ci)
        vn = _dot32_each(t, beta * v) - _dot32_each(_dot32_each(t, kb * e), s0)
        o = _dot32_each(q * e, s0) + _dot32_each(_dot32_each(q, k, "nt") * dmat, vn)
        s1 = s0 * gl + _dot32_each(k * r, vn, "tn")
        for h in heads:
            o_ref[:, cols[h]] = o.vals[h]
            state[h] = s1.vals[h]
            s_ref[h] = s0.vals[h]
            t_ref[h] = t.vals[h]

    wide = lambda part: pl.BlockSpec((c, nh * d), lambda n: (n, part))
    return pl.pallas_call(
        body, name=name, grid=(nc,),
        in_specs=[wide(0), wide(1), wide(2), pl.BlockSpec((c, LANES), lambda n: (n, 0))],
        out_specs=[wide(0), pl.BlockSpec((nh, None, d, d), lambda n: (0, n, 0, 0)),
                   pl.BlockSpec((nh, None, c, c), lambda n: (0, n, 0, 0))],
        out_shape=[jax.ShapeDtypeStruct((s, nh * d), F32), jax.ShapeDtypeStruct((nh, nc, d, d), F32),
                   jax.ShapeDtypeStruct((nh, nc, c, c), F32)],
        scratch_shapes=[pltpu.VMEM((nh, d, d), F32)],
        compiler_params=_cparams("arbitrary"),
    )(qkv, qkv, qkv, act)


def _dn_bwd(qkv, act, states, tinv, do, name):
    s = qkv.shape[0]
    c, d, nh = DN_CHUNK, HEAD_DIM, N_DN_HEADS
    nc = s // c

    def chunk_bwd(q, k, v, do, beta, g, s0, t, ds_out):
        ri = lax.broadcasted_iota(jnp.int32, (c, c), 0)
        ci = lax.broadcasted_iota(jnp.int32, (c, c), 1)
        eye, incl, strict = ri == ci, ri >= ci, ri > ci
        gc, dmat, e, r, gl = _each(lambda gh: _dn_gates(gh, ri, ci), g)
        dot = _dot32_each
        rowsum = lambda x: _each(lambda a: jnp.sum(a, axis=1, keepdims=True), x)
        colsum = lambda x: _each(lambda a: jnp.sum(a, axis=0, keepdims=True), x)
        total = lambda x: colsum(rowsum(x))
        to_col = lambda row: rowsum(_keep(eye, row))
        to_row = lambda colv: colsum(_keep(eye, colv))

        kb, vb = beta * k, beta * v
        kbe = kb * e
        u, w = dot(t, vb), dot(t, kbe)
        vn = u - dot(w, s0)
        qk = dot(q, k, "nt")
        p = qk * dmat
        gram = dot(k, k, "nt")
        kr, qe = k * r, q * e

        d_kr = dot(vn, ds_out, "nt")
        dvn = dot(kr, ds_out)
        dgl = total(s0 * ds_out)
        ds_in = ds_out * gl
        dk = d_kr * r
        dr = rowsum(d_kr * k)
        d_qe = dot(do, s0, "nt")
        ds_in = ds_in + dot(qe, do, "tn")
        dp = _keep(incl, dot(do, vn, "nt"))
        dvn = dvn + dot(p, do, "tn")
        dq = d_qe * e
        de = rowsum(d_qe * q)
        dqk = dp * dmat
        dq = dq + dot(dqk, k)
        dk = dk + dot(dqk, q, "tn")
        dd = dp * qk
        dw = -dot(dvn, s0, "nt")
        ds_in = ds_in - dot(w, dvn, "tn")
        dvb = dot(t, dvn, "tn")
        dkbe = dot(t, dw, "tn")
        dm = -_keep(strict, dot(dvb, u, "nt") + dot(dkbe, w, "nt"))
        dbeta = rowsum(dm * gram * dmat)
        dgram = dm * beta * dmat
        dd = dd + dm * beta * gram
        dk = dk + dot(dgram, k) + dot(dgram, k, "tn")
        dkb = dkbe * e
        de = de + rowsum(dkbe * kb)
        dk = dk + beta * dkb
        dbeta = dbeta + rowsum(dkb * k) + rowsum(dvb * v)
        dv = beta * dvb
        wd = dd * dmat
        dgc = rowsum(wd) - to_col(colsum(wd)) + de * e - dr * r
        dgc_last = total(dr * r) + dgl * gl
        dgc = dgc + _keep(ri[:, 0:1] == c - 1, dgc_last)
        dg = rowsum(_keep(ri <= ci, to_row(dgc)))
        return dq, dk, dv, dbeta, dg, ds_in

    def body(q_ref, k_ref, v_ref, act_ref, s_ref, t_ref, do_ref, dq_ref, dk_ref, dv_ref, dact_ref, dstate):
        @pl.when(pl.program_id(0) == 0)
        def _():
            dstate[...] = jnp.zeros_like(dstate)

        act = act_ref[...]
        heads = range(nh)
        cols = [slice(h * d, (h + 1) * d) for h in heads]
        q, k, v, do = (_Each(ref[:, cs] for cs in cols) for ref in (q_ref, k_ref, v_ref, do_ref))
        dq, dk, dv, dbeta, dg, ds_in = chunk_bwd(
            q, k, v, do, _Each(_lane_col(act, LANE_BETA + h) for h in heads),
            _Each(_lane_col(act, LANE_DECAY + h) for h in heads), _Each(s_ref[h] for h in heads),
            _Each(t_ref[h] for h in heads), _Each(dstate[h] for h in heads))
        lane = lax.broadcasted_iota(jnp.int32, (c, LANES), 1)
        dact = jnp.zeros((c, LANES), F32)
        for h in heads:
            dstate[h] = ds_in.vals[h]
            dq_ref[:, cols[h]], dk_ref[:, cols[h]], dv_ref[:, cols[h]] = dq.vals[h], dk.vals[h], dv.vals[h]
            dact = (dact + jnp.where(lane == LANE_BETA + h, dbeta.vals[h], 0.0)
                    + jnp.where(lane == LANE_DECAY + h, dg.vals[h], 0.0))
        dact_ref[...] = dact

    part = lambda p: pl.BlockSpec((c, nh * d), lambda n: (nc - 1 - n, p))
    per = lambda a, b: pl.BlockSpec((nh, None, a, b), lambda n: (0, nc - 1 - n, 0, 0))
    wide = jax.ShapeDtypeStruct((s, nh * d), F32)
    act_spec = pl.BlockSpec((c, LANES), lambda n: (nc - 1 - n, 0))
    return pl.pallas_call(
        body, name=name, grid=(nc,),
        in_specs=[part(0), part(1), part(2), act_spec, per(d, d), per(c, c), part(0)],
        out_specs=[part(0), part(0), part(0), act_spec],
        out_shape=[wide, wide, wide, jax.ShapeDtypeStruct((s, LANES), F32)],
        scratch_shapes=[pltpu.VMEM((nh, d, d), F32)],
        compiler_params=_cparams("arbitrary"),
    )(qkv, qkv, qkv, act, states, tinv, do)


EVEN_DN_QKV, EVEN_FOX_QKV, EVEN_DN_GATE, EVEN_FOX_GATE, EVEN_NARROW = 0, 1536, 3072, 3584, 4096
EVEN_WIDTH = 4224
CONV_TILE = 256
CONV_HALO = 8


def _conv_fwd(proj, w, name):
    s = proj.shape[0]
    t, cw = CONV_TILE, 3 * D_DN

    def body(cur_ref, prev_ref, w_ref, y_ref, xs):
        i = pl.program_id(0)
        xs[0:CONV_HALO, :] = jnp.where(i > 0, prev_ref[...], 0.0)
        xs[CONV_HALO:, :] = cur_ref[...]
        y = jnp.zeros((t, cw), F32)
        for tap in range(CONV_WIDTH):
            y = y + w_ref[tap:tap + 1, :] * xs[pl.ds(CONV_HALO - CONV_WIDTH + 1 + tap, t), :]
        y_ref[...] = y

    per = t // CONV_HALO
    return pl.pallas_call(
        body, name=name, grid=(s // t,),
        in_specs=[pl.BlockSpec((t, cw), lambda i: (i, 0)),
                  pl.BlockSpec((CONV_HALO, cw), lambda i: (jnp.maximum(i * per - 1, 0), 0)),
                  pl.BlockSpec((CONV_WIDTH, cw), lambda i: (0, 0))],
        out_specs=pl.BlockSpec((t, cw), lambda i: (i, 0)),
        out_shape=jax.ShapeDtypeStruct((s, cw), F32),
        scratch_shapes=[pltpu.VMEM((t + CONV_HALO, cw), F32)],
        compiler_params=_cparams("parallel"),
    )(proj, proj, w)


def _conv_bwd(proj, w, dy, name):
    s = proj.shape[0]
    t, cw = CONV_TILE, 3 * D_DN
    nt = s // t

    def body(cur_ref, prev_ref, w_ref, dy_ref, nxt_ref, dx_ref, dw_ref, xs, dys):
        i = pl.program_id(0)

        @pl.when(i == 0)
        def _():
            dw_ref[...] = jnp.zeros_like(dw_ref)

        xs[0:CONV_HALO, :] = jnp.where(i > 0, prev_ref[...], 0.0)
        xs[CONV_HALO:, :] = cur_ref[...]
        dys[0:t, :] = dy_ref[...]
        dys[t:, :] = jnp.where(i < nt - 1, nxt_ref[...], 0.0)
        dy = dy_ref[...]
        dx = jnp.zeros((t, cw), F32)
        for tap in range(CONV_WIDTH):
            dx = dx + w_ref[tap:tap + 1, :] * dys[pl.ds(CONV_WIDTH - 1 - tap, t), :]
            dw_ref[tap:tap + 1, :] += jnp.sum(dy * xs[pl.ds(CONV_HALO - CONV_WIDTH + 1 + tap, t), :], axis=0,
                                              keepdims=True)
        dx_ref[...] = dx.astype(BF16)

    per = t // CONV_HALO
    last = s // CONV_HALO - 1
    return pl.pallas_call(
        body, name=name, grid=(nt,),
        in_specs=[pl.BlockSpec((t, cw), lambda i: (i, 0)),
                  pl.BlockSpec((CONV_HALO, cw), lambda i: (jnp.maximum(i * per - 1, 0), 0)),
                  pl.BlockSpec((CONV_WIDTH, cw), lambda i: (0, 0)),
                  pl.BlockSpec((t, cw), lambda i: (i, 0)),
                  pl.BlockSpec((CONV_HALO, cw), lambda i: (jnp.minimum((i + 1) * per, last), 0))],
        out_specs=[pl.BlockSpec((t, cw), lambda i: (i, 0)), pl.BlockSpec((CONV_WIDTH, cw), lambda i: (0, 0))],
        out_shape=[jax.ShapeDtypeStruct((s, cw), BF16), jax.ShapeDtypeStruct((CONV_WIDTH, cw), F32)],
        scratch_shapes=[pltpu.VMEM((t + CONV_HALO, cw), F32), pltpu.VMEM((t + CONV_HALO, cw), F32)],
        compiler_params=_cparams("arbitrary"),
    )(proj, proj, w, dy, dy)


def _heads(x, n):
    return [x[:, HEAD_DIM * h:HEAD_DIM * (h + 1)] for h in range(n)]


def _dn_pre_fwd(y, name):
    def fn(yb):
        cs = yb * _sigmoid(yb)
        out = []
        for idx, xh in enumerate(_heads(cs, 3 * N_DN_HEADS)):
            if idx < 2 * N_DN_HEADS:
                xh = xh * lax.rsqrt(jnp.sum(xh * xh, axis=-1, keepdims=True) + EPS)
                if idx < N_DN_HEADS:
                    xh = xh * ATT_SCALE
            out.append(xh)
        return (jnp.concatenate(out, axis=1),)
    return _rowwise(fn, [y], [], [(y.shape[1], F32)], [], tile=256, name=name)[0]


def _dn_pre_bwd(y, dq, dk, dv, name):
    def fn(yb, dqb, dkb, dvb):
        sg = _sigmoid(yb)
        cs = yb * sg
        dout = _heads(dqb, N_DN_HEADS) + _heads(dkb, N_DN_HEADS) + _heads(dvb, N_DN_HEADS)
        dcs = []
        for idx, (xh, dh) in enumerate(zip(_heads(cs, 3 * N_DN_HEADS), dout)):
            if idx < 2 * N_DN_HEADS:
                if idx < N_DN_HEADS:
                    dh = dh * ATT_SCALE
                r = lax.rsqrt(jnp.sum(xh * xh, axis=-1, keepdims=True) + EPS)
                xhat = xh * r
                dh = r * (dh - xhat * jnp.sum(xhat * dh, axis=-1, keepdims=True))
            dcs.append(dh)
        return (jnp.concatenate(dcs, axis=1) * _silu_grad(yb, sg),)
    return _rowwise(fn, [y, dq, dk, dv], [], [(y.shape[1], F32)], [], tile=256, name=name)[0]


def _narrow_params(a_log, dt_bias, f_bias):
    lanes = lambda a, first: jnp.pad(a.reshape(1, -1), ((0, 0), (first, LANES - first - a.shape[0])))
    return jnp.concatenate([lanes(a_log, LANE_DECAY), lanes(dt_bias, LANE_DECAY), lanes(f_bias, LANE_FORGET),
                            jnp.zeros((5, LANES), F32)], axis=0)


def _narrow_masks(shape):
    lane = lax.broadcasted_iota(jnp.int32, shape, 1)
    is_beta = lane < LANE_DECAY
    is_decay = (lane >= LANE_DECAY) & (lane < LANE_FORGET)
    is_forget = (lane >= LANE_FORGET) & (lane < LANE_FORGET + N_FOX_HEADS)
    return is_beta, is_decay, is_forget


def _narrow_fwd(proj, params, name):
    def fn(sm, pk):
        is_beta, is_decay, is_forget = _narrow_masks(sm.shape)
        g = -jnp.exp(pk[0:1, :]) * _softplus(sm + pk[1:2, :])
        logf = -_softplus(-(sm + pk[2:3, :]))
        return (jnp.where(is_beta, _sigmoid(sm), jnp.where(is_decay, g, jnp.where(is_forget, logf, 0.0))),)
    return _rowwise(fn, [(proj, LANES, EVEN_NARROW // LANES)], [params], [(LANES, F32)], [], tile=512, name=name)[0]


def _narrow_bwd(proj, params, act, dact, dlogf, name):
    def fn(sm, ab, da, dl, pk):
        is_beta, is_decay, is_forget = _narrow_masks(sm.shape)
        db = jnp.where(is_forget, dl, da)
        d_beta = db * ab * (1.0 - ab)
        d_decay = db * (-jnp.exp(pk[0:1, :])) * _sigmoid(sm + pk[1:2, :])
        d_forget = db * _sigmoid(-(sm + pk[2:3, :]))
        dsm = jnp.where(is_beta, d_beta, jnp.where(is_decay, d_decay, jnp.where(is_forget, d_forget, 0.0)))
        col = lambda x: jnp.sum(x, axis=0, keepdims=True)
        return (dsm, col(jnp.where(is_decay, db * ab, 0.0)), col(jnp.where(is_decay, dsm, 0.0)),
                col(jnp.where(is_forget, dsm, 0.0)))
    return _rowwise(fn, [(proj, LANES, EVEN_NARROW // LANES), act, dact, dlogf], [params], [(LANES, BF16)],
                    [(1, LANES)] * 3, tile=512, name=name)


def _head_rms(xh):
    r = lax.rsqrt(jnp.mean(xh * xh, axis=-1, keepdims=True) + EPS)
    return xh * r, r


def _fox_pre_fwd(proj, qg, kg, name):
    def fn(pf, qgb, kgb):
        out = []
        for idx, xh in enumerate(_heads(pf, 3 * N_FOX_HEADS)):
            if idx < 2 * N_FOX_HEADS:
                xh = _head_rms(xh)[0] * (qgb if idx < N_FOX_HEADS else kgb)
            out.append(xh)
        return (jnp.concatenate(out, axis=1),)
    return _rowwise(fn, [(proj, 3 * D_FOX, EVEN_FOX_QKV // (3 * D_FOX))], [qg, kg], [(3 * D_FOX, BF16)], [],
                    tile=256, name=name)[0]


def _fox_pre_bwd(proj, qg, kg, dq, dk, dv, name):
    def fn(pf, dqb, dkb, dvb, qgb, kgb):
        dout = _heads(dqb, N_FOX_HEADS) + _heads(dkb, N_FOX_HEADS) + _heads(dvb, N_FOX_HEADS)
        dg = [jnp.zeros((1, HEAD_DIM), F32), jnp.zeros((1, HEAD_DIM), F32)]
        dx = []
        for idx, (xh, dh) in enumerate(zip(_heads(pf, 3 * N_FOX_HEADS), dout)):
            if idx < 2 * N_FOX_HEADS:
                which = 0 if idx < N_FOX_HEADS else 1
                xhat, r = _head_rms(xh)
                dg[which] = dg[which] + jnp.sum(dh * xhat, axis=0, keepdims=True)
                dxh = dh * (qgb if which == 0 else kgb)
                dh = r * (dxh - xhat * jnp.mean(dxh * xhat, axis=-1, keepdims=True))
            dx.append(dh)
        return jnp.concatenate(dx, axis=1), dg[0], dg[1]
    return _rowwise(fn, [(proj, 3 * D_FOX, EVEN_FOX_QKV // (3 * D_FOX)), dq, dk, dv], [qg, kg],
                    [(3 * D_FOX, BF16)], [(1, HEAD_DIM)] * 2, tile=256, name=name)


def _mix_gate_fwd(proj, o_dn, o_fox, ng, name):
    def fn(gd, gf, od, of, ngb):
        dn = [_head_rms(xh)[0] * ngb for xh in _heads(od, N_DN_HEADS)]
        return (jnp.concatenate([jnp.concatenate(dn, axis=1) * gd * _sigmoid(gd), of * _sigmoid(gf)], axis=1),)
    return _rowwise(fn, [(proj, D_DN, EVEN_DN_GATE // D_DN), (proj, D_FOX, EVEN_FOX_GATE // D_FOX), o_dn, o_fox],
                    [ng], [(D_DN + D_FOX, BF16)], [], tile=256, name=name)[0]


def _mix_gate_bwd(proj, o_dn, o_fox, ng, dom, name):
    def fn(gd, gf, od, of, dm, ngb):
        d_dn, d_fox = dm[:, :D_DN], dm[:, D_DN:]
        sgd, sgf = _sigmoid(gd), _sigmoid(gf)
        don = d_dn * gd * sgd
        dng = jnp.zeros((1, HEAD_DIM), F32)
        dod, normed = [], []
        for xh, dh in zip(_heads(od, N_DN_HEADS), _heads(don, N_DN_HEADS)):
            xhat, r = _head_rms(xh)
            dng = dng + jnp.sum(dh * xhat, axis=0, keepdims=True)
            dxh = dh * ngb
            dod.append(r * (dxh - xhat * jnp.mean(dxh * xhat, axis=-1, keepdims=True)))
            normed.append(xhat * ngb)
        d_gd = d_dn * jnp.concatenate(normed, axis=1) * _silu_grad(gd, sgd)
        d_gf = d_fox * of * sgf * (1.0 - sgf)
        return jnp.concatenate(dod, axis=1), d_fox * sgf, d_gd, d_gf, dng
    return _rowwise(fn, [(proj, D_DN, EVEN_DN_GATE // D_DN), (proj, D_FOX, EVEN_FOX_GATE // D_FOX), o_dn, o_fox, dom],
                    [ng], [(D_DN, F32), (D_FOX, F32), (D_DN, BF16), (D_FOX, BF16)], [(1, HEAD_DIM)], tile=256,
                    name=name)


def _loss_grad(y, target, name):
    d = y.shape[1]

    def fn(yb, tb):
        diff = yb - tb
        part = jnp.sum(jnp.sum(diff * diff, axis=1, keepdims=True), axis=0, keepdims=True) * (0.5 / d)
        g = diff * (1.0 / d)
        return g, g, part
    return _rowwise(fn, [y, target], [], [(d, F32), (d, BF16)], [(1, 1)], tile=512, name=name)


_REF_EVEN = {"dn_qkv": (0, 1536), "dn_gate": (1536, 2048), "dn_ba": (2048, 2056), "fox_qkv": (2056, 3592),
             "fox_gate": (3592, 4104), "f_pre": (4104, 4108)}
D_IN_EVEN = 4108


def _even_to_kernel_layout(w):
    cut = lambda name: w[..., _REF_EVEN[name][0]:_REF_EVEN[name][1]]
    pad = jnp.zeros(w.shape[:-1] + (EVEN_WIDTH - EVEN_NARROW - 12,), w.dtype)
    return jnp.concatenate([cut("dn_qkv"), cut("fox_qkv"), cut("dn_gate"), cut("fox_gate"), cut("dn_ba"),
                            cut("f_pre"), pad], axis=-1)


def _even_from_kernel_layout(g):
    return jnp.concatenate([g[..., EVEN_DN_QKV:EVEN_FOX_QKV], g[..., EVEN_DN_GATE:EVEN_FOX_GATE],
                            g[..., EVEN_NARROW:EVEN_NARROW + 8], g[..., EVEN_FOX_QKV:EVEN_DN_GATE],
                            g[..., EVEN_FOX_GATE:EVEN_NARROW], g[..., EVEN_NARROW + 8:EVEN_NARROW + 12]], axis=-1)


EVEN_QUARTER = 1027
EVEN_QUARTER_PAD = 1152


def _even_grad_quarters(g):
    g = _even_from_kernel_layout(g)
    pad = [(0, 0)] * (g.ndim - 1) + [(0, EVEN_QUARTER_PAD - EVEN_QUARTER)]
    return jnp.concatenate([jnp.pad(g[..., q * EVEN_QUARTER:(q + 1) * EVEN_QUARTER], pad) for q in range(4)], axis=-1)


def _forget_rows(c):
    return c[:, LANE_FORGET:LANE_FORGET + N_FOX_HEADS].T.reshape(N_FOX_HEADS, 1, c.shape[0])


def _forget_lanes(rows):
    s = rows.shape[2]
    return jnp.pad(rows.reshape(-1, s).T, ((0, 0), (LANE_FORGET, LANES - LANE_FORGET - N_FOX_HEADS)))


def _even_fwd(x, gain, w_in, w_out, j, p, tag):
    h = _rms_fwd(x, gain, f"{tag}_norm")
    proj = _mm(h, w_in, "nn", tm=512, tn=EVEN_WIDTH // 3, out_dtype=F32, name=f"{tag}_in", b_lead=(j,))
    y = _conv_fwd(proj, p["conv_w"], f"{tag}_conv")
    dn_qkv = _dn_pre_fwd(y, f"{tag}_dn_pre")
    act = _narrow_fwd(proj, p["narrow"], f"{tag}_narrow")
    o_dn, states, tinv = _dn_fwd(dn_qkv, act, f"{tag}_delta")
    fox_qkv = _fox_pre_fwd(proj, p["q_g"], p["k_g"], f"{tag}_fox_pre")
    c = _cumsum_rows(act, False, f"{tag}_cumsum")
    ct = _forget_rows(c)
    o_fox, lse = _fox_fwd(fox_qkv, c, ct, f"{tag}_fox")
    om = _mix_gate_fwd(proj, o_dn, o_fox, p["dn_norm_g"], f"{tag}_gate")
    x2 = _mm(om, w_out, "nn", tm=512, tn=x.shape[1], out_dtype=F32, name=f"{tag}_out", residual=x, b_lead=(j,))
    return x2, (x, h, proj, y, dn_qkv, act, states, tinv, o_dn, fox_qkv, c, ct, o_fox, lse, om)


def _even_bwd(dxo, dxo16, saved, gain, w_in, w_out, j, p, tag, g_in, g_out, after=None):
    x, h, proj, y, dn_qkv, act, states, tinv, o_dn, fox_qkv, c, ct, o_fox, lse, om = saved
    d = x.shape[1]
    dom = _mm(dxo16, w_out, "nt", tm=512, tn=d, out_dtype=F32, name=f"{tag}_out_bwd", b_lead=(j,), after=after)
    g_out = _mm(om, dxo16, "tn", tm=512, tn=d, out_dtype=F32, name=f"{tag}_out_dw", into=(g_out, 0))
    d_odn, d_ofox, d_gd, d_gf, d_ng = _mix_gate_bwd(proj, o_dn, o_fox, p["dn_norm_g"], dom, f"{tag}_gate_bwd")
    dq, dk, dv, dct = _fox_bwd(fox_qkv, c, ct, o_fox, lse, d_ofox, f"{tag}_fox_bwd")
    d_fox_qkv, d_qg, d_kg = _fox_pre_bwd(proj, p["q_g"], p["k_g"], dq, dk, dv, f"{tag}_fox_pre_bwd")
    dlogf = _cumsum_rows(_forget_lanes(dct), True, f"{tag}_cumsum_bwd")
    dq, dk, dv, dact = _dn_bwd(dn_qkv, act, states, tinv, d_odn, f"{tag}_delta_bwd")
    dy = _dn_pre_bwd(y, dq, dk, dv, f"{tag}_dn_pre_bwd")
    d_dn_qkv, d_conv = _conv_bwd(proj, p["conv_w"], dy, f"{tag}_conv_bwd")
    d_narrow, s_alog, s_dt, s_fb = _narrow_bwd(proj, p["narrow"], act, dact, dlogf, f"{tag}_narrow_bwd")
    dproj = jnp.concatenate([d_dn_qkv, d_fox_qkv, d_gd, d_gf, d_narrow], axis=1)
    g_in = _mm(h, dproj, "tn", tm=512, tn=EVEN_WIDTH // 3, out_dtype=F32, name=f"{tag}_in_dw", into=(g_in, 0))
    dx, dx16, d_gain = _in_proj_bwd(dproj, w_in, j, x, dxo, gain, f"{tag}_in_bwd")
    small = {"conv_w": d_conv, "a_log": s_alog, "dt_bias": s_dt, "f_bias": s_fb, "dn_norm_g": d_ng, "q_g": d_qg,
             "k_g": d_kg}
    return dx, dx16, d_gain, small, g_in, g_out


def _odd_fwd(x, gain, w_in, w_out, j, tag):
    h = _rms_fwd(x, gain, f"{tag}_norm")
    qkv = _mm(h, w_in, "nn", tm=512, tn=w_in.shape[2] // 2, out_dtype=BF16, name=f"{tag}_in", b_lead=(j,))
    o16, o32 = _sb_fwd(qkv, N_SB_HEADS, f"{tag}_sb")
    x2 = _mm(o16, w_out, "nn", tm=512, tn=x.shape[1], out_dtype=F32, name=f"{tag}_out", residual=x, b_lead=(j,))
    return x2, (x, h, qkv, o16, o32)


def _odd_bwd(dxo, dxo16, saved, gain, w_in, w_out, j, tag, g_in, g_out, after=None):
    x, h, qkv, o16, o32 = saved
    d = x.shape[1]
    do = _mm(dxo16, w_out, "nt", tm=512, tn=d, out_dtype=BF16, name=f"{tag}_out_bwd", b_lead=(j,), after=after)
    g_out = _mm(o16, dxo16, "tn", tm=512, tn=d, out_dtype=F32, name=f"{tag}_out_dw", into=(g_out, 0))
    dq, dk, dv = _sb_bwd(qkv, o32, do, N_SB_HEADS, f"{tag}_sb_bwd")
    dqkv = jnp.concatenate([dq, dk.astype(BF16), dv.astype(BF16)], axis=1)
    g_in = _mm(h, dqkv, "tn", tm=512, tn=w_in.shape[2] // 2, out_dtype=F32, name=f"{tag}_in_dw", into=(g_in, 0))
    dx, dx16, d_gain = _in_proj_bwd(dqkv, w_in, j, x, dxo, gain, f"{tag}_in_bwd")
    return dx, dx16, d_gain, g_in, g_out


def _forward_backward(x, target, w, first, rest_after, token, on_reduced):
    depth = w["norm_ffn1"].shape[0]
    row = lambda a, l: a[l][None]
    rest = {}

    def mats(names, j):
        if j == 0 and names[0] in first:
            return [first[name] for name in names] + [0]
        return [rest[name] for name in names] + [j - (1 if names[0] in first else 0)]

    def even_small(j):
        return {"conv_w": w["dn_conv_w"][j], "narrow": _narrow_params(w["dn_a_log"][j], w["dn_dt_bias"][j],
                                                                     w["fox_f_bias"][j]),
                "dn_norm_g": row(w["dn_norm_g"], j), "q_g": row(w["fox_q_norm_g"], j),
                "k_g": row(w["fox_k_norm_g"], j)}

    saved = []
    for l in range(depth):
        if l == 1:
            rest.update(rest_after(x))
        gain = row(w["norm_ffn1"], l) + token[0:1, 0:1] if l == 0 else row(w["norm_ffn1"], l)
        x, s1 = _ffn_fwd(x, gain, *mats(("ffn1_w_gu", "ffn1_w_down"), l), "ffn1")
        if l % 2 == 0:
            x, s2 = _even_fwd(x, row(w["norm_mix"], l), *mats(("w_in_even", "w_out_even"), l // 2),
                              even_small(l // 2), "even")
        else:
            x, s2 = _odd_fwd(x, row(w["norm_mix"], l), *mats(("w_in_odd", "w_out_odd"), l // 2), "odd")
        x, s3 = _ffn_fwd(x, row(w["norm_ffn2"], l), *mats(("ffn2_w_gu", "ffn2_w_down"), l), "ffn2")
        saved.append((s1, s2, s3))

    dx, dx16, loss = _loss_grad(x, target, "loss")

    kind_of = dict(BIG)
    d_norm = {k: [None] * depth for k in ("norm_ffn1", "norm_mix", "norm_ffn2")}
    d_even = [None] * ((depth + 1) // 2)
    to_sibling, between_chips, token = None, None, None
    for l in reversed(range(depth)):
        s1, s2, s3 = saved[l]
        mixer = ("w_in_even", "w_out_even") if l % 2 == 0 else ("w_in_odd", "w_out_odd")
        names = ["ffn1_w_gu", "ffn1_w_down", *mixer, "ffn2_w_gu", "ffn2_w_down"]
        g = {name: lax.empty((1,) + rest[name].shape[1:], F32) for name in names}
        dx, dx16, d_norm["norm_ffn2"][l], g["ffn2_w_gu"], g["ffn2_w_down"] = _ffn_bwd(
            dx, dx16, s3, row(w["norm_ffn2"], l), *mats(("ffn2_w_gu", "ffn2_w_down"), l), "ffn2", g["ffn2_w_gu"],
            g["ffn2_w_down"], after=token)
        if to_sibling is not None:
            between_chips, token = _reduce_middle(to_sibling, dx)
        if l % 2 == 0:
            dx, dx16, d_norm["norm_mix"][l], d_even[l // 2], g["w_in_even"], g["w_out_even"] = _even_bwd(
                dx, dx16, s2, row(w["norm_mix"], l), *mats(("w_in_even", "w_out_even"), l // 2), even_small(l // 2),
                "even", g["w_in_even"], g["w_out_even"], after=token)
            g["w_in_even"] = _even_grad_quarters(g["w_in_even"])
        else:
            dx, dx16, d_norm["norm_mix"][l], g["w_in_odd"], g["w_out_odd"] = _odd_bwd(
                dx, dx16, s2, row(w["norm_mix"], l), *mats(("w_in_odd", "w_out_odd"), l // 2), "odd", g["w_in_odd"],
                g["w_out_odd"], after=token)
        dx, dx16, d_norm["norm_ffn1"][l], g["ffn1_w_gu"], g["ffn1_w_down"] = _ffn_bwd(
            dx, dx16, s1, row(w["norm_ffn1"], l), *mats(("ffn1_w_gu", "ffn1_w_down"), l), "ffn1", g["ffn1_w_gu"],
            g["ffn1_w_down"])
        to_sibling, token = _reduce_start([g[name] for name in names], [kind_of[name] for name in names], names,
                                          f"layer{l}")
        if between_chips is not None:
            on_reduced(l + 1, dict(zip(between_chips[-2], _reduce_finish(between_chips, dx))))
    between_chips, _ = _reduce_middle(to_sibling, dx)
    on_reduced(0, dict(zip(between_chips[-2], _reduce_finish(between_chips, dx))))

    small = {k: jnp.concatenate(v, axis=0) for k, v in d_norm.items()}
    dec = slice(LANE_DECAY, LANE_DECAY + N_DN_HEADS)
    fgt = slice(LANE_FORGET, LANE_FORGET + N_FOX_HEADS)
    small["dn_conv_w"] = jnp.stack([e["conv_w"] for e in d_even])
    small["dn_a_log"] = jnp.concatenate([e["a_log"][:, dec] for e in d_even], axis=0)
    small["dn_dt_bias"] = jnp.concatenate([e["dt_bias"][:, dec] for e in d_even], axis=0)
    small["fox_f_bias"] = jnp.concatenate([e["f_bias"][:, fgt] for e in d_even], axis=0)
    small["dn_norm_g"] = jnp.concatenate([e["dn_norm_g"] for e in d_even], axis=0)
    small["fox_q_norm_g"] = jnp.concatenate([e["q_g"] for e in d_even], axis=0)
    small["fox_k_norm_g"] = jnp.concatenate([e["k_g"] for e in d_even], axis=0)
    return loss, dx, small


MESH = pl.DeviceIdType.MESH
ANY = pl.BlockSpec(memory_space=pl.ANY)


def _place():
    x, y, c = lax.axis_index("x"), lax.axis_index("y"), lax.axis_index("c")
    return x, y, c, [(1 - x, y), (x, 1 - y), (1 - x, 1 - y)]


def _remote(src, dst, send_sem, recv_sem, to):
    return pltpu.make_async_remote_copy(src_ref=src, dst_ref=dst, send_sem=send_sem, recv_sem=recv_sem,
                                        device_id=to, device_id_type=MESH)


def _aligned(start, multiple):
    return start if isinstance(start, int) else pl.multiple_of(start, multiple)


def _quarter(ref, kind, chip, half, rows, cols):
    k = 2 * chip[0] + chip[1]
    hr = rows // 2
    assert hr % 16 == 0 and cols % LANES == 0
    if kind == "col":
        return ref.at[:, pl.ds(_aligned(half * hr, 16), hr), pl.ds(_aligned(k * cols, LANES), cols)]
    return ref.at[:, pl.ds(_aligned(k * rows + half * hr, 16), hr), :]


def _place_quarter(shard, kind, kc, name, first=0, count=None):
    l, rows, cols = shard.shape
    l = l - first if count is None else count
    tr = rows
    while tr * cols * 4 > (2 << 20) and tr % 32 == 0:
        tr //= 2
    nr = rows // tr
    if kind == "col":
        out_spec = pl.BlockSpec((None, tr, cols), lambda li, i, kc_ref: (li, i, kc_ref[0]))
        out_shape = (l, rows, 4 * cols)
    else:
        out_spec = pl.BlockSpec((None, tr, cols), lambda li, i, kc_ref: (li, kc_ref[0] * nr + i, 0))
        out_shape = (l, 4 * rows, cols)

    def body(kc_ref, x_ref, o_ref):
        o_ref[...] = x_ref[...].astype(BF16)

    return pl.pallas_call(
        body, name=name,
        grid_spec=pltpu.PrefetchScalarGridSpec(
            num_scalar_prefetch=1, grid=(l, nr),
            in_specs=[pl.BlockSpec((None, tr, cols), lambda li, i, kc_ref: (li + first, i, 0))],
            out_specs=out_spec),
        out_shape=jax.ShapeDtypeStruct(out_shape, BF16),
        compiler_params=_cparams("parallel", "parallel"),
    )(kc, shard)


def _gather_weights(wholes, kinds):
    n = len(wholes)

    def dims(ref, kind):
        _, r, cc = ref.shape
        return (r, cc // 4) if kind == "col" else (r // 4, cc)

    def body(*refs):
        bufs = refs[n:2 * n]
        send_sems, recv_sems = refs[2 * n:]
        x, y, c, chips = _place()
        sibling = (x, y, 1 - c)
        first, passed = [], []
        for t in range(n):
            rows, cols = dims(bufs[t], kinds[t])
            mine = _quarter(bufs[t], kinds[t], (x, y), c, rows, cols)
            for j, chip in enumerate(chips):
                cp = _remote(mine, mine, send_sems.at[t, j], recv_sems.at[t, j], (*chip, c))
                cp.start()
                first.append(cp)
        for j, chip in enumerate(chips):
            for t in range(n):
                rows, cols = dims(bufs[t], kinds[t])
                got = _quarter(bufs[t], kinds[t], chip, c, rows, cols)
                _remote(got, got, send_sems.at[t, j], recv_sems.at[t, j], (*chip, c)).wait_recv()
                cp = _remote(got, got, send_sems.at[t, 3 + j], recv_sems.at[t, 3 + j], sibling)
                cp.start()
                passed.append(cp)
        for j, chip in enumerate(chips):
            for t in range(n):
                rows, cols = dims(bufs[t], kinds[t])
                got = _quarter(bufs[t], kinds[t], chip, 1 - c, rows, cols)
                _remote(got, got, send_sems.at[t, 3 + j], recv_sems.at[t, 3 + j], sibling).wait_recv()
        for cp in first + passed:
            cp.wait_send()

    return pl.pallas_call(
        body, name="gather_weights", in_specs=[ANY] * n, out_specs=[ANY] * n,
        out_shape=[jax.ShapeDtypeStruct(a.shape, a.dtype) for a in wholes],
        input_output_aliases={t: t for t in range(n)},
        scratch_shapes=[pltpu.SemaphoreType.DMA((n, 6)), pltpu.SemaphoreType.DMA((n, 6))],
        compiler_params=pltpu.CompilerParams(has_side_effects=True),
    )(*wholes)


def _quarter_dims(ref, kind):
    _, r, cc = ref.shape
    return (r, cc // 4) if kind == "col" else (r // 4, cc)


def _gather_chips_copies(bufs, sems, kinds):
    x, y, c, chips = _place()
    copies = []
    for t, buf in enumerate(bufs):
        rows, cols = _quarter_dims(buf, kinds[t])
        mine = _quarter(buf, kinds[t], (x, y), c, rows, cols)
        for j, chip in enumerate(chips):
            pair = 2 * (OTHER_CHIPS * t + j)
            copies.append(_remote(mine, mine, sems[pair], sems[pair + 1], (*chip, c)))
    return copies


def _gather_start(wholes, kinds, after, tag):
    n = len(wholes)
    n_sems = 2 * OTHER_CHIPS * n
    n_in = n + len(after)

    def body(*refs):
        for cp in _gather_chips_copies(refs[:n], refs[n_in + n:n_in + n + n_sems], kinds):
            cp.start()
        refs[-1][...] = jnp.zeros_like(refs[-1])

    held = [pltpu.with_memory_space_constraint(a, pltpu.HBM) for a in wholes]
    out = pl.pallas_call(
        body, name=f"gather_start_{tag}", in_specs=[HBM] * n + [ANY] * len(after),
        out_specs=(*[HBM] * n, *[SEM] * n_sems, pl.BlockSpec(memory_space=pltpu.VMEM)),
        out_shape=(*[pltpu.HBM(a.shape, a.dtype) for a in held], *[pltpu.SemaphoreType.DMA(())] * n_sems,
                   jax.ShapeDtypeStruct((8, LANES), F32)),
        input_output_aliases={i: i for i in range(n)},
        compiler_params=pltpu.CompilerParams(has_side_effects=SPLIT_COPY),
    )(*held, *after)
    return out[n:n + n_sems], out[:n], out[-1]


def _gather_wait(sems, wholes, kinds, after, tag):
    n = len(wholes)

    def body(*refs):
        for cp in _gather_chips_copies(refs[:n], refs[n:n + len(sems)], kinds):
            cp.wait_send()
            cp.wait_recv()

    return pl.pallas_call(
        body, name=f"gather_wait_{tag}", in_specs=[HBM] * n + [SEM] * len(sems) + [ANY],
        out_specs=tuple([HBM] * n), out_shape=tuple(pltpu.HBM(a.shape, a.dtype) for a in wholes),
        input_output_aliases={i: i for i in range(n)},
        compiler_params=pltpu.CompilerParams(has_side_effects=SPLIT_COPY),
    )(*wholes, *sems, after)


def _gather_forward(wholes, kinds, tag):
    n = len(wholes)

    def body(*refs):
        bufs = refs[n:2 * n]
        send_sems, recv_sems = refs[2 * n:]
        x, y, c, chips = _place()
        copies = []
        for t in range(n):
            rows, cols = _quarter_dims(bufs[t], kinds[t])
            for j, chip in enumerate(chips):
                got = _quarter(bufs[t], kinds[t], chip, c, rows, cols)
                cp = _remote(got, got, send_sems.at[t, j], recv_sems.at[t, j], (x, y, 1 - c))
                cp.start()
                copies.append(cp)
        for cp in copies:
            cp.wait_send()
        for t in range(n):
            rows, cols = _quarter_dims(bufs[t], kinds[t])
            for j, chip in enumerate(chips):
                got = _quarter(bufs[t], kinds[t], chip, 1 - c, rows, cols)
                _remote(got, got, send_sems.at[t, j], recv_sems.at[t, j], (x, y, 1 - c)).wait_recv()

    return pl.pallas_call(
        body, name=f"gather_forward_{tag}", in_specs=[ANY] * n, out_specs=[ANY] * n,
        out_shape=[jax.ShapeDtypeStruct(a.shape, a.dtype) for a in wholes],
        input_output_aliases={t: t for t in range(n)},
        scratch_shapes=[pltpu.SemaphoreType.DMA((n, OTHER_CHIPS)), pltpu.SemaphoreType.DMA((n, OTHER_CHIPS))],
        compiler_params=pltpu.CompilerParams(has_side_effects=True),
    )(*wholes)


def _canonical(a, kind):
    l, r, c = a.shape
    return a.reshape(l, 1, r, c) if kind == "col" else a.reshape(l, 4, r // 4, c)


def _add_tile(rows, cols):
    tc = cols if cols <= 1536 else cols // 4
    tr = rows
    while tr * tc * 4 > (1 << 20) and tr % 16 == 0:
        tr //= 2
    return tr, tc


def _rs_add_sibling(part, got, c, name):
    l, a, hr, cols = got.shape
    tr, tc = _add_tile(hr, cols)
    nr = hr // tr

    def body(c_ref, p_ref, g_ref, o32_ref, o16_ref):
        s = p_ref[...] + g_ref[...]
        o32_ref[...] = s
        o16_ref[...] = s.astype(BF16)

    blk = (None, None, tr, tc)
    spec = pl.BlockSpec(blk, lambda li, ai, i, j, c_ref: (li, ai, i, j))
    return pl.pallas_call(
        body, name=name,
        grid_spec=pltpu.PrefetchScalarGridSpec(
            num_scalar_prefetch=1, grid=(l, a, nr, cols // tc),
            in_specs=[pl.BlockSpec(blk, lambda li, ai, i, j, c_ref: (li, ai, c_ref[0] * nr + i, j)), spec],
            out_specs=[spec, spec]),
        out_shape=[jax.ShapeDtypeStruct(got.shape, F32), jax.ShapeDtypeStruct(got.shape, BF16)],
        compiler_params=_cparams("parallel", "parallel", "parallel", "parallel"),
    )(c, part, got)


def _quarter4(ref, kind, chip, cols):
    k = 2 * chip[0] + chip[1]
    if kind == "col":
        return ref.at[:, :, :, pl.ds(pl.multiple_of(k * cols, LANES), cols)]
    return ref.at[:, pl.ds(k, 1), :, :]


HBM = pl.BlockSpec(memory_space=pltpu.HBM)
SEM = pl.BlockSpec(memory_space=pltpu.SEMAPHORE)
SPLIT_COPY = pltpu.SideEffectType.DATAFLOW_SIDE_EFFECTING
OTHER_CHIPS = 3


def _quarter4_shape(a, kind):
    l, _, hr, cols = a.shape
    return (l, 1, hr, cols // 4 if kind == "col" else cols)


def _rs_chips_copies(srcs, lands, sems, kinds):
    x, y, c, chips = _place()
    copies = []
    for t, (src, land) in enumerate(zip(srcs, lands)):
        cols = _quarter4_shape(src, kinds[t])[3]
        for j, chip in enumerate(chips):
            pair = 2 * (OTHER_CHIPS * t + j)
            copies.append(_remote(_quarter4(src, kinds[t], chip, cols), land.at[j], sems[pair], sems[pair + 1],
                                  (*chip, c)))
    return copies


def _split_start(copies, srcs, lands, n_sems, name):
    n = len(srcs)

    def body(*refs):
        for cp in copies(refs[:n], refs[n:2 * n], refs[4 * n:4 * n + n_sems]):
            cp.start()
        refs[-1][...] = jnp.zeros_like(refs[-1])

    held = [pltpu.with_memory_space_constraint(a, pltpu.HBM) for a in (*srcs, *lands)]
    out = pl.pallas_call(
        body, name=name, in_specs=[HBM] * (2 * n),
        out_specs=(*[HBM] * (2 * n), *[SEM] * n_sems, pl.BlockSpec(memory_space=pltpu.VMEM)),
        out_shape=(*[pltpu.HBM(a.shape, a.dtype) for a in held], *[pltpu.SemaphoreType.DMA(())] * n_sems,
                   jax.ShapeDtypeStruct((8, LANES), F32)),
        input_output_aliases={i: i for i in range(2 * n)},
        compiler_params=pltpu.CompilerParams(has_side_effects=SPLIT_COPY),
    )(*held)
    return out[2 * n:2 * n + n_sems], out[:n], out[n:2 * n], out[-1]


def _split_wait(copies, sems, srcs, lands, after, name):
    n = len(srcs)

    def body(*refs):
        for cp in copies(refs[:n], refs[n:2 * n], refs[2 * n:2 * n + len(sems)]):
            cp.wait_send()
            cp.wait_recv()

    out = pl.pallas_call(
        body, name=name, in_specs=[HBM] * (2 * n) + [SEM] * len(sems) + [ANY],
        out_specs=tuple([HBM] * (2 * n)),
        out_shape=tuple(pltpu.HBM(a.shape, a.dtype) for a in (*srcs, *lands)),
        input_output_aliases={i: i for i in range(2 * n)},
        compiler_params=pltpu.CompilerParams(has_side_effects=SPLIT_COPY),
    )(*srcs, *lands, *sems, after)
    return out[:n], out[n:]


def _rs_sibling_copies(srcs, lands, sems):
    x, y, c, _ = _place()
    copies = []
    for t, (src, land) in enumerate(zip(srcs, lands)):
        hr = src.shape[2] // 2
        gives = src.at[:, :, pl.ds(pl.multiple_of((1 - c) * hr, 8), hr), :]
        copies.append(_remote(gives, land, sems[2 * t], sems[2 * t + 1], (x, y, 1 - c)))
    return copies


def _rs_add_chips(sum32, got, kind, kc, name):
    _, l, _, hr, cols = got.shape
    tr, _ = _add_tile(hr, cols)
    nr = hr // tr
    k_arr, c_arr = kc
    if kind == "col":
        own = pl.BlockSpec((None, None, tr, cols), lambda li, i, k_ref, c_ref: (li, 0, i, k_ref[0]))
    else:
        own = pl.BlockSpec((None, None, tr, cols), lambda li, i, k_ref, c_ref: (li, k_ref[0], i, 0))

    def body(k_ref, c_ref, own_ref, got_ref, o_ref):
        o_ref[...] = ((own_ref[...] + got_ref[0].astype(F32)) + got_ref[1].astype(F32)) + got_ref[2].astype(F32)

    return pl.pallas_call(
        body, name=name,
        grid_spec=pltpu.PrefetchScalarGridSpec(
            num_scalar_prefetch=2, grid=(l, nr),
            in_specs=[own, pl.BlockSpec((3, None, None, tr, cols), lambda li, i, k_ref, c_ref: (0, li, 0, i, 0))],
            out_specs=pl.BlockSpec((None, tr, cols), lambda li, i, k_ref, c_ref: (li, c_ref[0] * nr + i, 0))),
        out_shape=jax.ShapeDtypeStruct((l, 2 * hr, cols), F32),
        compiler_params=_cparams("parallel", "parallel"),
    )(k_arr, c_arr, sum32, got)


def _rs_finish(quarters):
    n = len(quarters)

    def body(*refs):
        bufs = refs[n:2 * n]
        send_sems, recv_sems = refs[2 * n:]
        x, y, c, _ = _place()
        copies = []
        for t in range(n):
            hr = bufs[t].shape[1] // 2
            mine = bufs[t].at[:, pl.ds(pl.multiple_of(c * hr, 8), hr), :]
            cp = _remote(mine, mine, send_sems.at[t], recv_sems.at[t], (x, y, 1 - c))
            cp.start()
            copies.append(cp)
        for cp in copies:
            cp.wait()

    return pl.pallas_call(
        body, name="reduce_finish", in_specs=[ANY] * n, out_specs=[ANY] * n,
        out_shape=[jax.ShapeDtypeStruct(a.shape, a.dtype) for a in quarters],
        input_output_aliases={t: t for t in range(n)},
        scratch_shapes=[pltpu.SemaphoreType.DMA((n,)), pltpu.SemaphoreType.DMA((n,))],
        compiler_params=pltpu.CompilerParams(has_side_effects=True),
    )(*quarters)


def _reduce_start(parts, kinds, names, tag):
    canon = [_canonical(p, kind) for p, kind in zip(parts, kinds)]
    lands = [lax.empty(a.shape[:2] + (a.shape[2] // 2, a.shape[3]), a.dtype) for a in canon]
    sems, srcs, lands, token = _split_start(_rs_sibling_copies, canon, lands, 2 * len(canon),
                                            f"reduce_sibling_start_{tag}")
    return (sems, srcs, lands, kinds, names, tag), token


def _reduce_middle(state, after):
    sems, srcs, lands, kinds, names, tag = state
    c_arr = jnp.reshape(lax.axis_index("c"), (1,)).astype(jnp.int32)
    srcs, from_sibling = _split_wait(_rs_sibling_copies, sems, srcs, lands, after, f"reduce_sibling_wait_{tag}")
    sums = [_rs_add_sibling(p, g, c_arr, f"reduce_add_sibling_{nm}") for p, g, nm in zip(srcs, from_sibling, names)]
    sums16 = [s16 for _, s16 in sums]
    copies = functools.partial(_rs_chips_copies, kinds=kinds)
    lands = [lax.empty((OTHER_CHIPS,) + _quarter4_shape(a, k), a.dtype) for a, k in zip(sums16, kinds)]
    sems, srcs, lands, token = _split_start(copies, sums16, lands, 2 * OTHER_CHIPS * len(sums16),
                                            f"reduce_chips_start_{tag}")
    return (sems, srcs, lands, [s32 for s32, _ in sums], kinds, names, tag), token


def _reduce_finish(state, after):
    sems, srcs, lands, sums32, kinds, names, tag = state
    x, y, c = lax.axis_index("x"), lax.axis_index("y"), lax.axis_index("c")
    kc = (jnp.reshape(2 * x + y, (1,)).astype(jnp.int32), jnp.reshape(c, (1,)).astype(jnp.int32))
    copies = functools.partial(_rs_chips_copies, kinds=kinds)
    _, from_chips = _split_wait(copies, sems, srcs, lands, after, f"reduce_chips_wait_{tag}")
    halves = [_rs_add_chips(s32, g, kind, kc, f"reduce_add_chips_{nm}")
              for s32, g, kind, nm in zip(sums32, from_chips, kinds, names)]
    return _rs_finish(halves)


SMALL_PEERS = 7


def _small_exchange(pack):
    rows = pack.shape[0]

    def body(p_ref, slots_ref, total_ref, send_sems, recv_sems):
        x, y, c, _ = _place()
        me = 4 * x + 2 * y + c
        slots_ref[me] = p_ref[...]
        copies = []
        for p in range(1, SMALL_PEERS + 1):
            px, py, pc = (p >> 2) & 1, (p >> 1) & 1, p & 1
            peer = (1 - x if px else x, 1 - y if py else y, 1 - c if pc else c)
            cp = _remote(p_ref, slots_ref.at[me], send_sems.at[p - 1], recv_sems.at[p - 1], peer)
            cp.start()
            copies.append(cp)
        for cp in copies:
            cp.wait()
        total = slots_ref[0]
        for i in range(1, SMALL_PEERS + 1):
            total = total + slots_ref[i]
        total_ref[...] = total

    vmem = pl.BlockSpec(memory_space=pltpu.VMEM)
    return pl.pallas_call(
        body, name="small_exchange", in_specs=[vmem], out_specs=[vmem, vmem],
        out_shape=[jax.ShapeDtypeStruct((SMALL_PEERS + 1, rows, LANES), F32), jax.ShapeDtypeStruct((rows, LANES), F32)],
        scratch_shapes=[pltpu.SemaphoreType.DMA((SMALL_PEERS,)), pltpu.SemaphoreType.DMA((SMALL_PEERS,))],
        compiler_params=pltpu.CompilerParams(has_side_effects=True),
    )(pack)


def _pack(arrays):
    rows = []
    for a in arrays:
        flat = a.reshape(-1).astype(F32)
        rows.append(jnp.pad(flat, (0, (-flat.shape[0]) % LANES)).reshape(-1, LANES))
    out = jnp.concatenate(rows, axis=0)
    return jnp.pad(out, ((0, (-out.shape[0]) % 8), (0, 0)))


def _unpack(pack, shapes):
    out, r = [], 0
    for sh in shapes:
        size = math.prod(sh)
        nr = -(-size // LANES)
        out.append(pack[r:r + nr].reshape(-1)[:size].reshape(sh))
        r += nr
    return out


def _adamw(w, g, m, v, name):
    shape = w.shape
    to2d = lambda a: a.reshape(-1, shape[-1])
    rows = math.prod(shape[:-1])
    tile = 256 if rows % 256 == 0 else rows

    def fn(wb, gb, mb, vb):
        m2 = ADAM_B1 * mb + (1.0 - ADAM_B1) * gb
        v2 = ADAM_B2 * vb + (1.0 - ADAM_B2) * (gb * gb)
        m_hat = m2 / (1.0 - ADAM_B1 ** ADAM_STEP)
        v_hat = v2 / (1.0 - ADAM_B2 ** ADAM_STEP)
        return -ADAM_LR * (m_hat / (jnp.sqrt(v_hat) + ADAM_EPS) + ADAM_WD * wb), m2, v2

    res = _rowwise(fn, [to2d(w), to2d(g), to2d(m), to2d(v)], [], [(shape[-1], F32)] * 3, [], tile=tile, name=name)
    return [r.reshape(shape) for r in res]


def _adamw_layer(w, g, m, v, layer, outs, name):
    _, rows, cols = w.shape
    tile = rows
    while tile * cols * 4 > (1 << 20) and tile % 16 == 0:
        tile //= 2

    def body(w_ref, g_ref, m_ref, v_ref, *rest):
        g_out, d_out, m_out, v_out = rest[-4:]
        gb = g_ref[...]
        m2 = ADAM_B1 * m_ref[...] + (1.0 - ADAM_B1) * gb
        v2 = ADAM_B2 * v_ref[...] + (1.0 - ADAM_B2) * (gb * gb)
        m_hat = m2 / (1.0 - ADAM_B1 ** ADAM_STEP)
        v_hat = v2 / (1.0 - ADAM_B2 ** ADAM_STEP)
        g_out[...] = gb
        d_out[...] = -ADAM_LR * (m_hat / (jnp.sqrt(v_hat) + ADAM_EPS) + ADAM_WD * w_ref[...])
        m_out[...] = m2
        v_out[...] = v2

    stacked = pl.BlockSpec((None, tile, cols), lambda i: (layer, i, 0))
    return pl.pallas_call(
        body, name=name, grid=(rows // tile,),
        in_specs=[stacked, pl.BlockSpec((None, tile, cols), lambda i: (0, i, 0)), stacked, stacked] + [ANY] * 4,
        out_specs=[stacked] * 4, out_shape=[jax.ShapeDtypeStruct(w.shape, F32)] * 4,
        input_output_aliases={4 + i: i for i in range(4)}, compiler_params=_cparams("parallel"),
    )(w, g, m, v, *outs)


BIG = (("ffn1_w_gu", "col"), ("ffn1_w_down", "row"), ("w_in_even", "col"), ("w_out_even", "row"),
       ("w_in_odd", "col"), ("w_out_odd", "row"), ("ffn2_w_gu", "col"), ("ffn2_w_down", "row"))
SMALL = ("norm_ffn1", "norm_mix", "dn_conv_w", "dn_a_log", "dn_dt_bias", "dn_norm_g", "fox_q_norm_g", "fox_k_norm_g",
         "fox_f_bias", "norm_ffn2")
WEIGHTS = ("norm_ffn1", "ffn1_w_gu", "ffn1_w_down", "norm_mix", "w_in_even", "dn_conv_w", "dn_a_log", "dn_dt_bias",
           "dn_norm_g", "fox_q_norm_g", "fox_k_norm_g", "fox_f_bias", "w_out_even", "w_in_odd", "w_out_odd",
           "norm_ffn2", "ffn2_w_gu", "ffn2_w_down")


def _step(x, target, w, m, v):
    k = 2 * lax.axis_index("x") + lax.axis_index("y")
    n_conv = w["dn_conv_w"].shape[2]

    kc = jnp.reshape(k, (1,)).astype(jnp.int32)
    kinds = dict(BIG)
    quarters = {name: w[name] for name in kinds}
    quarters["w_in_even"] = jnp.pad(w["w_in_even"], ((0, 0), (0, 0), (0, EVEN_QUARTER_PAD - EVEN_QUARTER)))
    first_names = [name for name in kinds if name not in ("w_in_odd", "w_out_odd")]
    rest_names = list(kinds)

    def even_columns(whole):
        padded = whole["w_in_even"]
        ref_order = jnp.concatenate([padded[..., q * EVEN_QUARTER_PAD:q * EVEN_QUARTER_PAD + EVEN_QUARTER]
                                     for q in range(4)], axis=-1)
        return {**whole, "w_in_even": _even_to_kernel_layout(ref_order)}

    conv_slots, _ = _small_exchange(_pack([w["dn_conv_w"]]))
    placed = [_place_quarter(quarters[name], kinds[name], kc, f"place_first_{name}", 0, 1) for name in first_names]
    gathered = _gather_weights(placed, [kinds[name] for name in first_names])
    first = even_columns(dict(zip(first_names, gathered)))
    placed = [_place_quarter(quarters[name], kinds[name], kc, f"place_rest_{name}", 1 if name in first_names else 0)
              for name in rest_names]
    rest_kinds = [kinds[name] for name in rest_names]
    sems, on_their_way, token = _gather_start(placed, rest_kinds, [conv_slots, *gathered], "rest")

    def rest_after(value):
        landed = _gather_wait(sems, on_their_way, rest_kinds, value, "rest")
        return even_columns(dict(zip(rest_names, _gather_forward(landed, rest_kinds, "rest"))))

    whole = {}
    conv_rows = math.prod(w["dn_conv_w"].shape) // LANES
    conv_quarters = [conv_slots[2 * q, :conv_rows].reshape(w["dn_conv_w"].shape) for q in range(4)]
    whole["dn_conv_w"] = jnp.concatenate(conv_quarters, axis=-1)
    for name in SMALL:
        if name != "dn_conv_w":
            whole[name] = w[name]

    updated = {name: [lax.empty(w[name].shape, F32) for _ in range(4)] for name in kinds}

    def on_reduced(layer, layer_grads):
        for name, g in layer_grads.items():
            if name == "w_in_even":
                g = g[..., :EVEN_QUARTER]
            stacked_layer = layer if w[name].shape[0] == w["norm_mix"].shape[0] else layer // 2
            updated[name] = _adamw_layer(w[name], g, m[name], v[name], stacked_layer, updated[name], f"adamw_{name}")

    loss, dx, small = _forward_backward(x, target, whole, first, rest_after, token, on_reduced)

    _, small_sum = _small_exchange(_pack([small[n] for n in SMALL]))
    grads = dict(zip(SMALL, _unpack(small_sum, [small[n].shape for n in SMALL])))
    grads["dn_conv_w"] = lax.dynamic_slice_in_dim(grads["dn_conv_w"], k * n_conv, n_conv, axis=2)
    delta, new_m, new_v = {}, {}, {}
    for name in kinds:
        grads[name], delta[name], new_m[name], new_v[name] = updated[name]
    packs = [_pack([d[n] for n in SMALL]) for d in (w, grads, m, v)]
    shapes = [w[n].shape for n in SMALL]
    for out, res in zip((delta, new_m, new_v), _adamw(*packs, "adamw_small")):
        out.update(zip(SMALL, _unpack(res, shapes)))
    total_loss = lax.psum(loss[0, 0], ("x", "y", "c"))
    return total_loss, dx, grads, delta, new_m, new_v


def kernel(x, norm_ffn1, ffn1_w_gu, ffn1_w_down, norm_mix, w_in_even, dn_conv_w, dn_a_log, dn_dt_bias, dn_norm_g, fox_q_norm_g, fox_k_norm_g, fox_f_bias, w_out_even, w_in_odd, w_out_odd, norm_ffn2, ffn2_w_gu, ffn2_w_down, loss_target, m_norm_ffn1, m_ffn1_w_gu, m_ffn1_w_down, m_norm_mix, m_w_in_even, m_dn_conv_w, m_dn_a_log, m_dn_dt_bias, m_dn_norm_g, m_fox_q_norm_g, m_fox_k_norm_g, m_fox_f_bias, m_w_out_even, m_w_in_odd, m_w_out_odd, m_norm_ffn2, m_ffn2_w_gu, m_ffn2_w_down, v_norm_ffn1, v_ffn1_w_gu, v_ffn1_w_down, v_norm_mix, v_w_in_even, v_dn_conv_w, v_dn_a_log, v_dn_dt_bias, v_dn_norm_g, v_fox_q_norm_g, v_fox_k_norm_g, v_fox_f_bias, v_w_out_even, v_w_in_odd, v_w_out_odd, v_norm_ffn2, v_ffn2_w_gu, v_ffn2_w_down):
    w = dict(zip(WEIGHTS, (norm_ffn1, ffn1_w_gu, ffn1_w_down, norm_mix, w_in_even, dn_conv_w, dn_a_log, dn_dt_bias,
                           dn_norm_g, fox_q_norm_g, fox_k_norm_g, fox_f_bias, w_out_even, w_in_odd, w_out_odd,
                           norm_ffn2, ffn2_w_gu, ffn2_w_down)))
    m = dict(zip(WEIGHTS, (m_norm_ffn1, m_ffn1_w_gu, m_ffn1_w_down, m_norm_mix, m_w_in_even, m_dn_conv_w, m_dn_a_log,
                           m_dn_dt_bias, m_dn_norm_g, m_fox_q_norm_g, m_fox_k_norm_g, m_fox_f_bias, m_w_out_even,
                           m_w_in_odd, m_w_out_odd, m_norm_ffn2, m_ffn2_w_gu, m_ffn2_w_down)))
    v = dict(zip(WEIGHTS, (v_norm_ffn1, v_ffn1_w_gu, v_ffn1_w_down, v_norm_mix, v_w_in_even, v_dn_conv_w, v_dn_a_log,
                           v_dn_dt_bias, v_dn_norm_g, v_fox_q_norm_g, v_fox_k_norm_g, v_fox_f_bias, v_w_out_even,
                           v_w_in_odd, v_w_out_odd, v_norm_ffn2, v_ffn2_w_gu, v_ffn2_w_down)))
    loss, dx, grads, delta, new_m, new_v = _step(x[0], loss_target[0], w, m, v)
    return (loss, dx[None], *[grads[n] for n in WEIGHTS], *[delta[n] for n in WEIGHTS],
            *[new_m[n] for n in WEIGHTS], *[new_v[n] for n in WEIGHTS])
```

```python
import functools
import math

import jax
import jax.numpy as jnp
from jax import lax
from jax.experimental import pallas as pl
from jax.experimental.pallas import tpu as pltpu

F32 = jnp.float32
BF16 = jnp.bfloat16
HI = lax.Precision.HIGH

HEAD_DIM = 128
N_DN_HEADS = 4
N_FOX_HEADS = 4
N_SB_HEADS = 8
D_DN = N_DN_HEADS * HEAD_DIM
D_FOX = N_FOX_HEADS * HEAD_DIM
CONV_WIDTH = 4
DN_CHUNK = 64
EPS = 1e-6
ATT_SCALE = HEAD_DIM ** -0.5
ADAM_LR, ADAM_B1, ADAM_B2, ADAM_EPS, ADAM_WD, ADAM_STEP = 0.001, 0.9, 0.999, 1e-08, 0.01, 10

V7X_VMEM_LIMIT = 56 * 1024 * 1024
LANES = 128
ATT_TQ = 512
ATT_TK = 256
ATT_SUB = ATT_TQ // ATT_TK

LANE_BETA, LANE_DECAY, LANE_FORGET = 0, 4, 8


def _cparams(*sem):
    return pltpu.CompilerParams(dimension_semantics=sem, vmem_limit_bytes=V7X_VMEM_LIMIT)


def _sigmoid(x):
    return 1.0 / (1.0 + jnp.exp(-x))


def _softplus(x):
    return jnp.maximum(x, 0.0) + jnp.log(1.0 + jnp.exp(-jnp.abs(x)))


def _silu_grad(y, sg):
    return sg * (1.0 + y * (1.0 - sg))


def _rowwise(fn, rows, bcast, outs, sums, *, tile, name):
    rows = [r if isinstance(r, tuple) else (r, r.shape[1], 0) for r in rows]
    s = rows[0][0].shape[0]
    assert s % tile == 0
    n_in, n_b, n_out, n_sum = len(rows), len(bcast), len(outs), len(sums)

    def body(*refs):
        ins = [r[...] for r in refs[:n_in + n_b]]
        res = fn(*ins)
        if not isinstance(res, (tuple, list)):
            res = (res,)
        out_refs = refs[n_in + n_b:n_in + n_b + n_out]
        sum_refs = refs[n_in + n_b + n_out:]
        for o_ref, val in zip(out_refs, res[:n_out]):
            o_ref[...] = val.astype(o_ref.dtype)
        if n_sum:
            @pl.when(pl.program_id(0) == 0)
            def _():
                for s_ref in sum_refs:
                    s_ref[...] = jnp.zeros_like(s_ref)
            for s_ref, val in zip(sum_refs, res[n_out:]):
                s_ref[...] += val

    in_specs = [pl.BlockSpec((tile, w), lambda i, cb=cb: (i, cb)) for _, w, cb in rows]
    in_specs += [pl.BlockSpec(b.shape, lambda i, nd=b.ndim: (0,) * nd) for b in bcast]
    out_specs = [pl.BlockSpec((tile, c), lambda i: (i, 0)) for c, _ in outs]
    out_specs += [pl.BlockSpec(sh, lambda i: (0, 0)) for sh in sums]
    out_shape = [jax.ShapeDtypeStruct((s, c), dt) for c, dt in outs]
    out_shape += [jax.ShapeDtypeStruct(sh, F32) for sh in sums]
    return pl.pallas_call(
        body, name=name, grid=(s // tile,), in_specs=in_specs, out_specs=out_specs, out_shape=out_shape,
        compiler_params=_cparams("arbitrary" if n_sum else "parallel"),
    )(*[r[0] for r in rows], *bcast)


def _rms_fwd(x, gain, name):
    def fn(xb, g):
        r = lax.rsqrt(jnp.mean(xb * xb, axis=-1, keepdims=True) + EPS)
        return (xb * r * g,)
    return _rowwise(fn, [x], [gain], [(x.shape[1], BF16)], [], tile=512, name=name)[0]


_DIMS = {"nn": (((1,), (0,)), ((), ())), "nt": (((1,), (1,)), ((), ())), "tn": (((0,), (0,)), ((), ()))}


def _dot(a, b, kind):
    return lax.dot_general(a.astype(BF16), b.astype(BF16), _DIMS[kind], preferred_element_type=F32)


def _dot32(a, b, kind="nn"):
    return lax.dot_general(a, b, _DIMS[kind], precision=HI, preferred_element_type=F32)


def _mm(a, b, kind, *, tm, tn, out_dtype, name, scale=None, residual=None, a_lead=(), b_lead=(),
        b_spec=None, n=None, into=None, after=None):
    ash, bsh = a.shape[len(a_lead):], b.shape[len(b_lead):]
    m = ash[1] if kind == "tn" else ash[0]
    k = ash[0] if kind == "tn" else ash[1]
    if b_spec is None:
        n = bsh[0] if kind == "nt" else bsh[1]
        assert k == (bsh[1] if kind == "nt" else bsh[0]), (ash, bsh, kind)
    assert m % tm == 0 and n % tn == 0, (m, tm, n, tn)
    la, lb = (None,) * len(a_lead), (None,) * len(b_lead)
    if kind == "tn":
        a_spec = pl.BlockSpec(la + (k, tm), lambda j, i: a_lead + (0, i))
    else:
        a_spec = pl.BlockSpec(la + (tm, k), lambda j, i: a_lead + (i, 0))
    if b_spec is None:
        if kind == "nt":
            b_spec = pl.BlockSpec(lb + (tn, k), lambda j, i: b_lead + (j, 0))
        else:
            b_spec = pl.BlockSpec(lb + (k, tn), lambda j, i: b_lead + (0, j))
    in_specs, args = [a_spec, b_spec], [a, b]
    if residual is not None:
        in_specs.append(pl.BlockSpec((tm, tn), lambda j, i: (i, j)))
        args.append(residual)
    aliases = {}
    if after is not None:
        in_specs.append(pl.BlockSpec(memory_space=pl.ANY))
        args.append(after)
    if into is not None:
        buf, layer = into
        in_specs.append(pl.BlockSpec(memory_space=pl.ANY))
        args.append(buf)
        aliases = {len(args) - 1: 0}
        out_spec = pl.BlockSpec((None, tm, tn), lambda j, i: (layer, i, j))
        out_shape = jax.ShapeDtypeStruct(buf.shape, buf.dtype)
    else:
        out_spec = pl.BlockSpec((tm, tn), lambda j, i: (i, j))
        out_shape = jax.ShapeDtypeStruct((m, n), out_dtype)

    def body(a_ref, b_ref, *rest):
        acc = _dot(a_ref[...], b_ref[...], kind)
        if scale is not None:
            acc = acc * scale
        if residual is not None:
            acc = acc + rest[0][...]
        rest[-1][...] = acc.astype(rest[-1].dtype)

    return pl.pallas_call(
        body, name=name, grid=(n // tn, m // tm), in_specs=in_specs, out_specs=out_spec, out_shape=out_shape,
        input_output_aliases=aliases, compiler_params=_cparams("parallel", "parallel"),
    )(*args)


def _ffn_up(n, w_gu, layer, name):
    s, d = n.shape
    f = w_gu.shape[2] // 2
    tm, tn = 512, f // 2
    nj = f // tn

    def body(n_ref, wg_ref, wu_ref, gu_ref, a_ref):
        nv = n_ref[...]
        g = _dot(nv, wg_ref[...], "nn")
        u = _dot(nv, wu_ref[...], "nn")
        gu_ref[0] = g.astype(BF16)
        gu_ref[1] = u.astype(BF16)
        a_ref[...] = (g * _sigmoid(g) * u).astype(BF16)

    return pl.pallas_call(
        body, name=name, grid=(nj, s // tm),
        in_specs=[pl.BlockSpec((tm, d), lambda j, i: (i, 0)),
                  pl.BlockSpec((None, d, tn), lambda j, i: (layer, 0, j)),
                  pl.BlockSpec((None, d, tn), lambda j, i: (layer, 0, j + nj))],
        out_specs=[pl.BlockSpec((2, tm, tn), lambda j, i: (0, i, j)),
                   pl.BlockSpec((tm, tn), lambda j, i: (i, j))],
        out_shape=[jax.ShapeDtypeStruct((2, s, f), BF16), jax.ShapeDtypeStruct((s, f), BF16)],
        compiler_params=_cparams("parallel", "parallel"),
    )(n, w_gu, w_gu)


def _ffn_down_bwd(dxo, w_down, gu, layer, name, after=None):
    s, d = dxo.shape
    f = w_down.shape[1]
    tm, tn = 512, f // 2
    extra_specs, extra = ([ANY], [after]) if after is not None else ([], [])

    def body(dx_ref, w_ref, gu_ref, *rest):
        dgu_ref = rest[-1]
        da = 0.5 * _dot(dx_ref[...], w_ref[...], "nt")
        g = gu_ref[0].astype(F32)
        u = gu_ref[1].astype(F32)
        sg = _sigmoid(g)
        dgu_ref[0] = (da * u * _silu_grad(g, sg)).astype(BF16)
        dgu_ref[1] = (da * g * sg).astype(BF16)

    return pl.pallas_call(
        body, name=name, grid=(f // tn, s // tm),
        in_specs=[pl.BlockSpec((tm, d), lambda j, i: (i, 0)),
                  pl.BlockSpec((None, tn, d), lambda j, i: (layer, j, 0)),
                  pl.BlockSpec((2, tm, tn), lambda j, i: (0, i, j))] + extra_specs,
        out_specs=pl.BlockSpec((2, tm, tn), lambda j, i: (0, i, j)),
        out_shape=jax.ShapeDtypeStruct((2, s, f), BF16),
        compiler_params=_cparams("parallel", "parallel"),
    )(dxo, w_down, gu, *extra)


NORM_BWD_TM = 256


def _norm_bwd_after(terms, operands, specs, x, dres, gain, name):
    s, d = x.shape
    tm = NORM_BWD_TM
    n_op = len(operands)

    def body(*refs):
        x_ref, dres_ref, g_ref = refs[n_op:n_op + 3]
        dx_ref, dx16_ref, dgain_ref = refs[n_op + 3:]
        dn = None
        for a, b in terms(*refs[:n_op]):
            dn = _dot(a, b, "nt") if dn is None else dn + _dot(a, b, "nt")
        xb = x_ref[...]
        r = lax.rsqrt(jnp.mean(xb * xb, axis=-1, keepdims=True) + EPS)
        xh = xb * r
        dxh = dn * g_ref[...]
        dx = dres_ref[...] + r * (dxh - xh * jnp.mean(dxh * xh, axis=-1, keepdims=True))
        dx_ref[...] = dx
        dx16_ref[...] = dx.astype(BF16)

        @pl.when(pl.program_id(0) == 0)
        def _():
            dgain_ref[...] = jnp.zeros_like(dgain_ref)
        dgain_ref[...] += jnp.sum(dn * xh, axis=0, keepdims=True)

    rows = pl.BlockSpec((tm, d), lambda i: (i, 0))
    return pl.pallas_call(
        body, name=name, grid=(s // tm,),
        in_specs=list(specs) + [rows, rows, pl.BlockSpec((1, d), lambda i: (0, 0))],
        out_specs=[rows, rows, pl.BlockSpec((1, d), lambda i: (0, 0))],
        out_shape=[jax.ShapeDtypeStruct((s, d), F32), jax.ShapeDtypeStruct((s, d), BF16),
                   jax.ShapeDtypeStruct((1, d), F32)],
        compiler_params=_cparams("arbitrary"),
    )(*operands, x, dres, gain)


def _ffn_up_bwd(dgu, w_gu, layer, x, dres, gain, name):
    _, s, f = dgu.shape
    d = w_gu.shape[1]
    specs = [pl.BlockSpec((2, NORM_BWD_TM, f), lambda i: (0, i, 0)),
             pl.BlockSpec((None, d, f), lambda i: (layer, 0, 0)),
             pl.BlockSpec((None, d, f), lambda i: (layer, 0, 1))]
    terms = lambda dgu_ref, wg_ref, wu_ref: [(dgu_ref[0], wg_ref[...]), (dgu_ref[1], wu_ref[...])]
    return _norm_bwd_after(terms, [dgu, w_gu, w_gu], specs, x, dres, gain, name)


def _in_proj_bwd(dproj, w_in, j, x, dres, gain, name):
    k = dproj.shape[1]
    d = w_in.shape[1]
    specs = [pl.BlockSpec((NORM_BWD_TM, k), lambda i: (i, 0)), pl.BlockSpec((None, d, k), lambda i: (j, 0, 0))]
    terms = lambda a_ref, b_ref: [(a_ref[...], b_ref[...])]
    return _norm_bwd_after(terms, [dproj, w_in], specs, x, dres, gain, name)


def _ffn_fwd(x, gain, w_gu, w_down, layer, tag):
    n = _rms_fwd(x, gain, f"{tag}_norm")
    gu, a = _ffn_up(n, w_gu, layer, f"{tag}_up")
    x2 = _mm(a, w_down, "nn", tm=512, tn=x.shape[1], out_dtype=F32, name=f"{tag}_down", scale=0.5, residual=x,
             b_lead=(layer,))
    return x2, (x, n, gu, a)


def _ffn_bwd(dxo, dxo16, saved, gain, w_gu, w_down, layer, tag, g_gu, g_down, after=None):
    x, n, gu, a = saved
    s, f = a.shape
    dgu = _ffn_down_bwd(dxo16, w_down, gu, layer, f"{tag}_down_bwd", after)
    g_down = _mm(a, dxo16, "tn", tm=256, tn=dxo16.shape[1], out_dtype=F32, name=f"{tag}_down_dw", scale=0.5,
                 into=(g_down, 0))
    tn = f // 2
    nj = f // tn
    g_gu = _mm(n, dgu, "tn", tm=512, tn=tn, out_dtype=F32, name=f"{tag}_up_dw", into=(g_gu, 0), n=2 * f,
               b_spec=pl.BlockSpec((None, s, tn), lambda j, i: (j // nj, 0, j % nj)))
    dx, dx16, dgain = _ffn_up_bwd(dgu, w_gu, layer, x, dxo, gain, f"{tag}_up_bwd")
    return dx, dx16, dgain, g_gu, g_down


def _lane_col(blk, lane):
    li = lax.broadcasted_iota(jnp.int32, blk.shape, 1)
    return jnp.sum(jnp.where(li == lane, blk, 0.0), axis=1, keepdims=True)


def _split_dot(x, tri):
    hi = x.astype(BF16)
    lo = (x - hi.astype(F32)).astype(BF16)
    return (lax.dot_general(hi, tri, _DIMS["nn"], preferred_element_type=F32)
            + lax.dot_general(lo, tri, _DIMS["nn"], preferred_element_type=F32))


class _Each:
    def __init__(self, vals):
        self.vals = list(vals)

    def _with(self, other, op):
        others = other.vals if isinstance(other, _Each) else [other] * len(self.vals)
        return _Each(op(a, b) for a, b in zip(self.vals, others))

    def __add__(self, other):
        return self._with(other, lambda a, b: a + b)

    def __sub__(self, other):
        return self._with(other, lambda a, b: a - b)

    def __mul__(self, other):
        return self._with(other, lambda a, b: a * b)

    def __neg__(self):
        return _Each(-a for a in self.vals)


def _each(fn, *args):
    n = max(len(a.vals) for a in args if isinstance(a, _Each))
    res = [fn(*xs) for xs in zip(*[a.vals if isinstance(a, _Each) else [a] * n for a in args])]
    if isinstance(res[0], tuple):
        return tuple(_Each(r) for r in zip(*res))
    return _Each(res)


def _keep(cond, x):
    return _each(lambda v: jnp.where(cond, v, 0.0), x)


def _rowsum(x):
    return _each(lambda v: jnp.sum(v, axis=1, keepdims=True), x)


ATT_HEADS = 2
ATT_WIDTH = ATT_HEADS * HEAD_DIM
_HEAD_COLS = [slice(h * HEAD_DIM, (h + 1) * HEAD_DIM) for h in range(ATT_HEADS)]


def _att_specs(n_heads, s):
    groups = n_heads // ATT_HEADS
    q_spec = pl.BlockSpec((ATT_TQ, ATT_WIDTH), lambda g, i: (i, g))
    k_spec = pl.BlockSpec((s, ATT_WIDTH), lambda g, i: (0, groups + g))
    v_spec = pl.BlockSpec((s, ATT_WIDTH), lambda g, i: (0, 2 * groups + g))
    return q_spec, k_spec, v_spec


def _heads_of(ref, rows=None):
    return _Each(ref[:, cs] if rows is None else ref[rows, cs] for cs in _HEAD_COLS)


def _dot_each(a, b, kind):
    return _each(lambda x, y: _dot(x, y, kind), a, b)


def _att_iotas():
    row = lax.broadcasted_iota(jnp.int32, (ATT_TQ, ATT_TK), 0)
    col = lax.broadcasted_iota(jnp.int32, (ATT_TQ, ATT_TK), 1)
    jr = lax.broadcasted_iota(jnp.int32, (ATT_TK, ATT_TK), 0)
    jc = lax.broadcasted_iota(jnp.int32, (ATT_TK, ATT_TK), 1)
    return row, col, jr, jc


def _sb_fwd(qkv, n_heads, name):
    s = qkv.shape[0]

    def body(q_ref, k_ref, v_ref, o16_ref, o32_ref):
        i = pl.program_id(1)
        q = _heads_of(q_ref)
        row, col, jr, jc = _att_iotas()
        later = (jr > jc).astype(BF16)

        def step(jb, carry, diagonal):
            c_sp, acc = (_Each(part) for part in carry)
            work = []
            for sub in reversed(range(ATT_SUB)):
                keys = pl.ds(pl.multiple_of(jb * ATT_TQ + sub * ATT_TK, ATT_TK), ATT_TK)
                z = _dot_each(q, _heads_of(k_ref, keys), "nt") * ATT_SCALE
                sp = _each(_softplus, z)
                before = (col + sub * ATT_TK) < row if diagonal else None
                spm = _keep(before, sp) if diagonal else sp
                work.append((keys, z - sp, spm, _each(lambda x: _dot(x, later, "nn"), spm), before))
            for keys, logsig, spm, within, before in work:
                a = _each(jnp.exp, logsig - (c_sp + within))
                if diagonal:
                    a = _keep(before, a)
                acc = acc + _each(_split_dot, a, _heads_of(v_ref, keys))
                c_sp = c_sp + _rowsum(spm)
            return tuple(c_sp.vals), tuple(acc.vals)

        zeros = lambda width: tuple(jnp.zeros((ATT_TQ, width), F32) for _ in range(ATT_HEADS))
        carry = step(i, (zeros(1), zeros(HEAD_DIM)), True)
        _, acc = lax.fori_loop(0, i, lambda it, cr: step(i - 1 - it, cr, False), carry)
        for cs, acc_h in zip(_HEAD_COLS, acc):
            o16_ref[:, cs] = acc_h.astype(BF16)
            o32_ref[:, cs] = acc_h

    q_spec, k_spec, v_spec = _att_specs(n_heads, s)
    o_spec = pl.BlockSpec((ATT_TQ, ATT_WIDTH), lambda g, i: (i, g))
    return pl.pallas_call(
        body, name=name, grid=(n_heads // ATT_HEADS, s // ATT_TQ), in_specs=[q_spec, k_spec, v_spec],
        out_specs=[o_spec, o_spec],
        out_shape=[jax.ShapeDtypeStruct((s, n_heads * HEAD_DIM), BF16),
                   jax.ShapeDtypeStruct((s, n_heads * HEAD_DIM), F32)],
        compiler_params=_cparams("parallel", "arbitrary"),
    )(qkv, qkv, qkv)


def _sb_bwd(qkv, o32, do, n_heads, name):
    s = qkv.shape[0]

    def body(q_ref, k_ref, v_ref, o_ref, do_ref, dq_ref, dk_ref, dv_ref):
        i = pl.program_id(1)

        @pl.when(i == 0)
        def _():
            dk_ref[...] = jnp.zeros_like(dk_ref)
            dv_ref[...] = jnp.zeros_like(dv_ref)

        q, do = _heads_of(q_ref), _heads_of(do_ref)
        total = _rowsum(_each(lambda a, b: a.astype(F32) * b, do, _heads_of(o_ref)))
        row, col, jr, jc = _att_iotas()
        later = (jr > jc).astype(BF16)
        not_before = (jr >= jc).astype(BF16)

        def step(jb, carry, diagonal):
            c_sp, c_e, dq = (_Each(part) for part in carry)
            work = []
            for sub in reversed(range(ATT_SUB)):
                keys = pl.ds(pl.multiple_of(jb * ATT_TQ + sub * ATT_TK, ATT_TK), ATT_TK)
                k = _heads_of(k_ref, keys)
                z = _dot_each(q, k, "nt") * ATT_SCALE
                sp = _each(_softplus, z)
                before = (col + sub * ATT_TK) < row if diagonal else None
                spm = _keep(before, sp) if diagonal else sp
                work.append((keys, k, _each(jnp.exp, z - sp), spm, _each(lambda x: _dot(x, later, "nn"), spm),
                             _dot_each(do, _heads_of(v_ref, keys), "nt"), before))
            for keys, k, sig, spm, within, da, before in work:
                a = sig * _each(lambda x: jnp.exp(-x), c_sp + within)
                if diagonal:
                    a = _keep(before, a)
                e = a * da
                left = total - c_e - _each(lambda x: _split_dot(x, not_before), e)
                dz = (e - (e + left) * sig) * ATT_SCALE
                if diagonal:
                    dz = _keep(before, dz)
                dk, dv = _dot_each(dz, q, "tn"), _dot_each(a, do, "tn")
                for cs, dk_h, dv_h in zip(_HEAD_COLS, dk.vals, dv.vals):
                    dk_ref[keys, cs] += dk_h
                    dv_ref[keys, cs] += dv_h
                dq = dq + _dot_each(dz, k, "nn")
                c_sp = c_sp + _rowsum(spm)
                c_e = c_e + _rowsum(e)
            return tuple(c_sp.vals), tuple(c_e.vals), tuple(dq.vals)

        zeros = lambda width: tuple(jnp.zeros((ATT_TQ, width), F32) for _ in range(ATT_HEADS))
        carry = step(i, (zeros(1), zeros(1), zeros(HEAD_DIM)), True)
        _, _, dq = lax.fori_loop(0, i, lambda it, cr: step(i - 1 - it, cr, False), carry)
        for cs, dq_h in zip(_HEAD_COLS, dq):
            dq_ref[:, cs] = dq_h.astype(BF16)

    q_spec, k_spec, v_spec = _att_specs(n_heads, s)
    blk = pl.BlockSpec((ATT_TQ, ATT_WIDTH), lambda g, i: (i, g))
    full = pl.BlockSpec((s, ATT_WIDTH), lambda g, i: (0, g))
    wide = (s, n_heads * HEAD_DIM)
    return pl.pallas_call(
        body, name=name, grid=(n_heads // ATT_HEADS, s // ATT_TQ), in_specs=[q_spec, k_spec, v_spec, blk, blk],
        out_specs=[blk, full, full],
        out_shape=[jax.ShapeDtypeStruct(wide, BF16), jax.ShapeDtypeStruct(wide, F32), jax.ShapeDtypeStruct(wide, F32)],
        compiler_params=_cparams("parallel", "arbitrary"),
    )(qkv, qkv, qkv, o32, do)


def _fox_logits(q, k, cq, ct_ref, keys):
    ck = _Each(ct_ref[h, :, keys] for h in range(ATT_HEADS))
    return _dot_each(q, k, "nt") * ATT_SCALE + (cq - ck)


def _fox_cq(c_ref, group):
    c = c_ref[...]
    return _Each(_lane_col(c, LANE_FORGET + group * ATT_HEADS + h) for h in range(ATT_HEADS))


def _fox_fwd(qkv, c, ct, name):
    s = qkv.shape[0]
    n_heads = N_FOX_HEADS

    def body(q_ref, k_ref, v_ref, c_ref, ct_ref, o_ref, lse_ref):
        g, i = pl.program_id(0), pl.program_id(1)
        q = _heads_of(q_ref)
        cq = _fox_cq(c_ref, g)
        row, col, _, _ = _att_iotas()

        def step(jb, carry, diagonal):
            m, l, acc = (_Each(part) for part in carry)
            work = []
            m_new = m
            for sub in range(ATT_SUB):
                keys = pl.ds(pl.multiple_of(jb * ATT_TQ + sub * ATT_TK, ATT_TK), ATT_TK)
                sc = _fox_logits(q, _heads_of(k_ref, keys), cq, ct_ref, keys)
                valid = (col + sub * ATT_TK) <= row if diagonal else None
                if diagonal:
                    sc = _each(lambda x: jnp.where(valid, x, -1e30), sc)
                m_new = _each(lambda a, x: jnp.maximum(a, jnp.max(x, axis=1, keepdims=True)), m_new, sc)
                work.append((keys, sc, valid))
            w = _each(jnp.exp, m - m_new)
            l, acc = l * w, acc * w
            for keys, sc, valid in work:
                p = _each(jnp.exp, sc - m_new)
                if diagonal:
                    p = _keep(valid, p)
                l = l + _rowsum(p)
                acc = acc + _each(_split_dot, p, _heads_of(v_ref, keys))
            return tuple(m_new.vals), tuple(l.vals), tuple(acc.vals)

        per_head = lambda width, value: tuple(jnp.full((ATT_TQ, width), value, F32) for _ in range(ATT_HEADS))
        init = (per_head(1, -1e30), per_head(1, 0.0), per_head(HEAD_DIM, 0.0))
        m, l, acc = lax.fori_loop(0, i, lambda jb, cr: step(jb, cr, False), step(i, init, True))
        for h, cs in enumerate(_HEAD_COLS):
            o_ref[:, cs] = acc[h] / l[h]
            lse_ref[h] = jnp.broadcast_to(m[h] + jnp.log(l[h]), (ATT_TQ, LANES))

    q_spec, k_spec, v_spec = _att_specs(n_heads, s)
    return pl.pallas_call(
        body, name=name, grid=(n_heads // ATT_HEADS, s // ATT_TQ),
        in_specs=[q_spec, k_spec, v_spec, pl.BlockSpec((ATT_TQ, LANES), lambda g, i: (i, 0)),
                  pl.BlockSpec((ATT_HEADS, 1, s), lambda g, i: (g, 0, 0))],
        out_specs=[pl.BlockSpec((ATT_TQ, ATT_WIDTH), lambda g, i: (i, g)),
                   pl.BlockSpec((ATT_HEADS, ATT_TQ, LANES), lambda g, i: (g, i, 0))],
        out_shape=[jax.ShapeDtypeStruct((s, n_heads * HEAD_DIM), F32),
                   jax.ShapeDtypeStruct((n_heads, s, LANES), F32)],
        compiler_params=_cparams("parallel", "arbitrary"),
    )(qkv, qkv, qkv, c, ct)


def _fox_bwd(qkv, c, ct, o, lse, do, name):
    s = qkv.shape[0]
    n_heads = N_FOX_HEADS

    def body(q_ref, k_ref, v_ref, c_ref, ct_ref, o_ref, lse_ref, do_ref, dq_ref, dk_ref, dv_ref, dct_ref):
        g, i = pl.program_id(0), pl.program_id(1)

        @pl.when(i == 0)
        def _():
            dk_ref[...] = jnp.zeros_like(dk_ref)
            dv_ref[...] = jnp.zeros_like(dv_ref)
            dct_ref[...] = jnp.zeros_like(dct_ref)

        q = _heads_of(q_ref)
        do16 = _each(lambda x: x.astype(BF16), _heads_of(do_ref))
        delta = _rowsum(_each(lambda a, b: a.astype(F32) * b, do16, _heads_of(o_ref)))
        lse_col = _Each(lse_ref[h, :, 0:1] for h in range(ATT_HEADS))
        cq = _fox_cq(c_ref, g)
        row, col, _, _ = _att_iotas()

        def step(jb, dq, diagonal):
            dq = _Each(dq)
            for sub in range(ATT_SUB):
                keys = pl.ds(pl.multiple_of(jb * ATT_TQ + sub * ATT_TK, ATT_TK), ATT_TK)
                k = _heads_of(k_ref, keys)
                sc = _fox_logits(q, k, cq, ct_ref, keys)
                if diagonal:
                    valid = (col + sub * ATT_TK) <= row
                    p = _keep(valid, _each(jnp.exp, _keep(valid, sc) - lse_col))
                else:
                    p = _each(jnp.exp, sc - lse_col)
                ds = p * (_dot_each(do16, _heads_of(v_ref, keys), "nt") - delta)
                dss = ds * ATT_SCALE
                dk, dv = _dot_each(dss, q, "tn"), _dot_each(p, do16, "tn")
                for h, cs in enumerate(_HEAD_COLS):
                    dct_ref[h, :, keys] -= jnp.sum(ds.vals[h], axis=0, keepdims=True)
                    dk_ref[keys, cs] += dk.vals[h]
                    dv_ref[keys, cs] += dv.vals[h]
                dq = dq + _dot_each(dss, k, "nn")
            return tuple(dq.vals)

        dq0 = step(i, tuple(jnp.zeros((ATT_TQ, HEAD_DIM), F32) for _ in range(ATT_HEADS)), True)
        dq = lax.fori_loop(0, i, lambda jb, dq: step(jb, dq, False), dq0)
        for cs, dq_h in zip(_HEAD_COLS, dq):
            dq_ref[:, cs] = dq_h

    q_spec, k_spec, v_spec = _att_specs(n_heads, s)
    blk = pl.BlockSpec((ATT_TQ, ATT_WIDTH), lambda g, i: (i, g))
    full = pl.BlockSpec((s, ATT_WIDTH), lambda g, i: (0, g))
    wide = jax.ShapeDtypeStruct((s, n_heads * HEAD_DIM), F32)
    return pl.pallas_call(
        body, name=name, grid=(n_heads // ATT_HEADS, s // ATT_TQ),
        in_specs=[q_spec, k_spec, v_spec, pl.BlockSpec((ATT_TQ, LANES), lambda g, i: (i, 0)),
                  pl.BlockSpec((ATT_HEADS, 1, s), lambda g, i: (g, 0, 0)), blk,
                  pl.BlockSpec((ATT_HEADS, ATT_TQ, LANES), lambda g, i: (g, i, 0)), blk],
        out_specs=[blk, full, full, pl.BlockSpec((ATT_HEADS, 1, s), lambda g, i: (g, 0, 0))],
        out_shape=[wide, wide, wide, jax.ShapeDtypeStruct((n_heads, 1, s), F32)],
        compiler_params=_cparams("parallel", "arbitrary"),
    )(qkv, qkv, qkv, c, ct, o, lse, do)


def _cumsum_rows(x, reverse, name):
    s = x.shape[0]
    nb = s // LANES

    def body(x_ref, o_ref):
        r = lax.broadcasted_iota(jnp.int32, (LANES, LANES), 0)
        c = lax.broadcasted_iota(jnp.int32, (LANES, LANES), 1)
        tri = ((r <= c) if reverse else (r >= c)).astype(F32)

        def step(it, carry):
            b = (nb - 1 - it) if reverse else it
            off = pl.multiple_of(b * LANES, LANES)
            blk = x_ref[pl.ds(off, LANES), :]
            o_ref[pl.ds(off, LANES), :] = _dot32(tri, blk) + carry
            return carry + jnp.sum(blk, axis=0, keepdims=True)

        lax.fori_loop(0, nb, step, jnp.zeros((1, LANES), F32))

    return pl.pallas_call(body, name=name, out_shape=jax.ShapeDtypeStruct(x.shape, F32),
                          compiler_params=pltpu.CompilerParams(vmem_limit_bytes=V7X_VMEM_LIMIT))(x)


def _dot32_each(a, b, kind="nn"):
    return _each(lambda x, y: _dot32(x, y, kind), a, b)


def _unit_lower_inverse(m, ri, ci):
    c = ri.shape[0]
    t = -_keep(ri // 2 == ci // 2, m) + jnp.where(ri == ci, 1.0, 0.0)
    b = 4
    while b <= c:
        off_diag = (ri // b == ci // b) & (ri % b >= b // 2) & (ci % b < b // 2)
        t = t - _dot32_each(_dot32_each(t, _keep(off_diag, m)), t)
        b *= 2
    return t


def _dn_gates(g, ri, ci):
    eye = ri == ci
    incl = ri >= ci
    g_row = jnp.sum(jnp.where(eye, g, 0.0), axis=0, keepdims=True)
    gc = jnp.sum(jnp.where(incl, g_row, 0.0), axis=1, keepdims=True)
    gc_row = jnp.sum(jnp.where(eye, gc, 0.0), axis=0, keepdims=True)
    dmat = jnp.where(incl, jnp.exp(jnp.where(incl, gc - gc_row, 0.0)), 0.0)
    gc_last = jnp.sum(g, axis=0, keepdims=True)
    return gc, dmat, jnp.exp(gc), jnp.exp(gc_last - gc), jnp.exp(gc_last)


def _dn_fwd(qkv, act, name):
    s = qkv.shape[0]
    c, d, nh = DN_CHUNK, HEAD_DIM, N_DN_HEADS
    nc = s // c

    def body(q_ref, k_ref, v_ref, act_ref, o_ref, s_ref, t_ref, state):
        @pl.when(pl.program_id(0) == 0)
        def _():
            state[...] = jnp.zeros_like(state)

        ri = lax.broadcasted_iota(jnp.int32, (c, c), 0)
        ci = lax.broadcasted_iota(jnp.int32, (c, c), 1)
        act = act_ref[...]
        heads = range(nh)
        cols = [slice(h * d, (h + 1) * d) for h in heads]
        q, k, v = (_Each(ref[:, cs] for cs in cols) for ref in (q_ref, k_ref, v_ref))
        beta = _Each(_lane_col(act, LANE_BETA + h) for h in heads)
        g = _Each(_lane_col(act, LANE_DECAY + h) for h in heads)
        _, dmat, e, r, gl = _each(lambda gh: _dn_gates(gh, ri, ci), g)
        s0 = _Each(state[h] for h in heads)
        kb = beta * k
        t = _unit_lower_inverse(_keep(ri > ci, _dot32_each(kb, k, "nt") * dmat), ri, ci)
        vn = _dot32_each(t, beta * v) - _dot32_each(_dot32_each(t, kb * e), s0)
        o = _dot32_each(q * e, s0) + _dot32_each(_dot32_each(q, k, "nt") * dmat, vn)
        s1 = s0 * gl + _dot32_each(k * r, vn, "tn")
        for h in heads:
            o_ref[:, cols[h]] = o.vals[h]
            state[h] = s1.vals[h]
            s_ref[h] = s0.vals[h]
            t_ref[h] = t.vals[h]

    wide = lambda part: pl.BlockSpec((c, nh * d), lambda n: (n, part))
    return pl.pallas_call(
        body, name=name, grid=(nc,),
        in_specs=[wide(0), wide(1), wide(2), pl.BlockSpec((c, LANES), lambda n: (n, 0))],
        out_specs=[wide(0), pl.BlockSpec((nh, None, d, d), lambda n: (0, n, 0, 0)),
                   pl.BlockSpec((nh, None, c, c), lambda n: (0, n, 0, 0))],
        out_shape=[jax.ShapeDtypeStruct((s, nh * d), F32), jax.ShapeDtypeStruct((nh, nc, d, d), F32),
                   jax.ShapeDtypeStruct((nh, nc, c, c), F32)],
        scratch_shapes=[pltpu.VMEM((nh, d, d), F32)],
        compiler_params=_cparams("arbitrary"),
    )(qkv, qkv, qkv, act)


def _dn_bwd(qkv, act, states, tinv, do, name):
    s = qkv.shape[0]
    c, d, nh = DN_CHUNK, HEAD_DIM, N_DN_HEADS
    nc = s // c

    def chunk_bwd(q, k, v, do, beta, g, s0, t, ds_out):
        ri = lax.broadcasted_iota(jnp.int32, (c, c), 0)
        ci = lax.broadcasted_iota(jnp.int32, (c, c), 1)
        eye, incl, strict = ri == ci, ri >= ci, ri > ci
        gc, dmat, e, r, gl = _each(lambda gh: _dn_gates(gh, ri, ci), g)
        dot = _dot32_each
        rowsum = lambda x: _each(lambda a: jnp.sum(a, axis=1, keepdims=True), x)
        colsum = lambda x: _each(lambda a: jnp.sum(a, axis=0, keepdims=True), x)
        total = lambda x: colsum(rowsum(x))
        to_col = lambda row: rowsum(_keep(eye, row))
        to_row = lambda colv: colsum(_keep(eye, colv))

        kb, vb = beta * k, beta * v
        kbe = kb * e
        u, w = dot(t, vb), dot(t, kbe)
        vn = u - dot(w, s0)
        qk = dot(q, k, "nt")
        p = qk * dmat
        gram = dot(k, k, "nt")
        kr, qe = k * r, q * e

        d_kr = dot(vn, ds_out, "nt")
        dvn = dot(kr, ds_out)
        dgl = total(s0 * ds_out)
        ds_in = ds_out * gl
        dk = d_kr * r
        dr = rowsum(d_kr * k)
        d_qe = dot(do, s0, "nt")
        ds_in = ds_in + dot(qe, do, "tn")
        dp = _keep(incl, dot(do, vn, "nt"))
        dvn = dvn + dot(p, do, "tn")
        dq = d_qe * e
        de = rowsum(d_qe * q)
        dqk = dp * dmat
        dq = dq + dot(dqk, k)
        dk = dk + dot(dqk, q, "tn")
        dd = dp * qk
        dw = -dot(dvn, s0, "nt")
        ds_in = ds_in - dot(w, dvn, "tn")
        dvb = dot(t, dvn, "tn")
        dkbe = dot(t, dw, "tn")
        dm = -_keep(strict, dot(dvb, u, "nt") + dot(dkbe, w, "nt"))
        dbeta = rowsum(dm * gram * dmat)
        dgram = dm * beta * dmat
        dd = dd + dm * beta * gram
        dk = dk + dot(dgram, k) + dot(dgram, k, "tn")
        dkb = dkbe * e
        de = de + rowsum(dkbe * kb)
        dk = dk + beta * dkb
        dbeta = dbeta + rowsum(dkb * k) + rowsum(dvb * v)
        dv = beta * dvb
        wd = dd * dmat
        dgc = rowsum(wd) - to_col(colsum(wd)) + de * e - dr * r
        dgc_last = total(dr * r) + dgl * gl
        dgc = dgc + _keep(ri[:, 0:1] == c - 1, dgc_last)
        dg = rowsum(_keep(ri <= ci, to_row(dgc)))
        return dq, dk, dv, dbeta, dg, ds_in

    def body(q_ref, k_ref, v_ref, act_ref, s_ref, t_ref, do_ref, dq_ref, dk_ref, dv_ref, dact_ref, dstate):
        @pl.when(pl.program_id(0) == 0)
        def _():
            dstate[...] = jnp.zeros_like(dstate)

        act = act_ref[...]
        heads = range(nh)
        cols = [slice(h * d, (h + 1) * d) for h in heads]
        q, k, v, do = (_Each(ref[:, cs] for cs in cols) for ref in (q_ref, k_ref, v_ref, do_ref))
        dq, dk, dv, dbeta, dg, ds_in = chunk_bwd(
            q, k, v, do, _Each(_lane_col(act, LANE_BETA + h) for h in heads),
            _Each(_lane_col(act, LANE_DECAY + h) for h in heads), _Each(s_ref[h] for h in heads),
            _Each(t_ref[h] for h in heads), _Each(dstate[h] for h in heads))
        lane = lax.broadcasted_iota(jnp.int32, (c, LANES), 1)
        dact = jnp.zeros((c, LANES), F32)
        for h in heads:
            dstate[h] = ds_in.vals[h]
            dq_ref[:, cols[h]], dk_ref[:, cols[h]], dv_ref[:, cols[h]] = dq.vals[h], dk.vals[h], dv.vals[h]
            dact = (dact + jnp.where(lane == LANE_BETA + h, dbeta.vals[h], 0.0)
                    + jnp.where(lane == LANE_DECAY + h, dg.vals[h], 0.0))
        dact_ref[...] = dact

    part = lambda p: pl.BlockSpec((c, nh * d), lambda n: (nc - 1 - n, p))
    per = lambda a, b: pl.BlockSpec((nh, None, a, b), lambda n: (0, nc - 1 - n, 0, 0))
    wide = jax.ShapeDtypeStruct((s, nh * d), F32)
    act_spec = pl.BlockSpec((c, LANES), lambda n: (nc - 1 - n, 0))
    return pl.pallas_call(
        body, name=name, grid=(nc,),
        in_specs=[part(0), part(1), part(2), act_spec, per(d, d), per(c, c), part(0)],
        out_specs=[part(0), part(0), part(0), act_spec],
        out_shape=[wide, wide, wide, jax.ShapeDtypeStruct((s, LANES), F32)],
        scratch_shapes=[pltpu.VMEM((nh, d, d), F32)],
        compiler_params=_cparams("arbitrary"),
    )(qkv, qkv, qkv, act, states, tinv, do)


EVEN_DN_QKV, EVEN_FOX_QKV, EVEN_DN_GATE, EVEN_FOX_GATE, EVEN_NARROW = 0, 1536, 3072, 3584, 4096
EVEN_WIDTH = 4224
CONV_TILE = 256
CONV_HALO = 8


def _conv_fwd(proj, w, name):
    s = proj.shape[0]
    t, cw = CONV_TILE, 3 * D_DN

    def body(cur_ref, prev_ref, w_ref, y_ref, xs):
        i = pl.program_id(0)
        xs[0:CONV_HALO, :] = jnp.where(i > 0, prev_ref[...], 0.0)
        xs[CONV_HALO:, :] = cur_ref[...]
        y = jnp.zeros((t, cw), F32)
        for tap in range(CONV_WIDTH):
            y = y + w_ref[tap:tap + 1, :] * xs[pl.ds(CONV_HALO - CONV_WIDTH + 1 + tap, t), :]
        y_ref[...] = y

    per = t // CONV_HALO
    return pl.pallas_call(
        body, name=name, grid=(s // t,),
        in_specs=[pl.BlockSpec((t, cw), lambda i: (i, 0)),
                  pl.BlockSpec((CONV_HALO, cw), lambda i: (jnp.maximum(i * per - 1, 0), 0)),
                  pl.BlockSpec((CONV_WIDTH, cw), lambda i: (0, 0))],
        out_specs=pl.BlockSpec((t, cw), lambda i: (i, 0)),
        out_shape=jax.ShapeDtypeStruct((s, cw), F32),
        scratch_shapes=[pltpu.VMEM((t + CONV_HALO, cw), F32)],
        compiler_params=_cparams("parallel"),
    )(proj, proj, w)


def _conv_bwd(proj, w, dy, name):
    s = proj.shape[0]
    t, cw = CONV_TILE, 3 * D_DN
    nt = s // t

    def body(cur_ref, prev_ref, w_ref, dy_ref, nxt_ref, dx_ref, dw_ref, xs, dys):
        i = pl.program_id(0)

        @pl.when(i == 0)
        def _():
            dw_ref[...] = jnp.zeros_like(dw_ref)

        xs[0:CONV_HALO, :] = jnp.where(i > 0, prev_ref[...], 0.0)
        xs[CONV_HALO:, :] = cur_ref[...]
        dys[0:t, :] = dy_ref[...]
        dys[t:, :] = jnp.where(i < nt - 1, nxt_ref[...], 0.0)
        dy = dy_ref[...]
        dx = jnp.zeros((t, cw), F32)
        for tap in range(CONV_WIDTH):
            dx = dx + w_ref[tap:tap + 1, :] * dys[pl.ds(CONV_WIDTH - 1 - tap, t), :]
            dw_ref[tap:tap + 1, :] += jnp.sum(dy * xs[pl.ds(CONV_HALO - CONV_WIDTH + 1 + tap, t), :], axis=0,
                                              keepdims=True)
        dx_ref[...] = dx.astype(BF16)

    per = t // CONV_HALO
    last = s // CONV_HALO - 1
    return pl.pallas_call(
        body, name=name, grid=(nt,),
        in_specs=[pl.BlockSpec((t, cw), lambda i: (i, 0)),
                  pl.BlockSpec((CONV_HALO, cw), lambda i: (jnp.maximum(i * per - 1, 0), 0)),
                  pl.BlockSpec((CONV_WIDTH, cw), lambda i: (0, 0)),
                  pl.BlockSpec((t, cw), lambda i: (i, 0)),
                  pl.BlockSpec((CONV_HALO, cw), lambda i: (jnp.minimum((i + 1) * per, last), 0))],
        out_specs=[pl.BlockSpec((t, cw), lambda i: (i, 0)), pl.BlockSpec((CONV_WIDTH, cw), lambda i: (0, 0))],
        out_shape=[jax.ShapeDtypeStruct((s, cw), BF16), jax.ShapeDtypeStruct((CONV_WIDTH, cw), F32)],
        scratch_shapes=[pltpu.VMEM((t + CONV_HALO, cw), F32), pltpu.VMEM((t + CONV_HALO, cw), F32)],
        compiler_params=_cparams("arbitrary"),
    )(proj, proj, w, dy, dy)


def _heads(x, n):
    return [x[:, HEAD_DIM * h:HEAD_DIM * (h + 1)] for h in range(n)]


def _dn_pre_fwd(y, name):
    def fn(yb):
        cs = yb * _sigmoid(yb)
        out = []
        for idx, xh in enumerate(_heads(cs, 3 * N_DN_HEADS)):
            if idx < 2 * N_DN_HEADS:
                xh = xh * lax.rsqrt(jnp.sum(xh * xh, axis=-1, keepdims=True) + EPS)
                if idx < N_DN_HEADS:
                    xh = xh * ATT_SCALE
            out.append(xh)
        return (jnp.concatenate(out, axis=1),)
    return _rowwise(fn, [y], [], [(y.shape[1], F32)], [], tile=256, name=name)[0]


def _dn_pre_bwd(y, dq, dk, dv, name):
    def fn(yb, dqb, dkb, dvb):
        sg = _sigmoid(yb)
        cs = yb * sg
        dout = _heads(dqb, N_DN_HEADS) + _heads(dkb, N_DN_HEADS) + _heads(dvb, N_DN_HEADS)
        dcs = []
        for idx, (xh, dh) in enumerate(zip(_heads(cs, 3 * N_DN_HEADS), dout)):
            if idx < 2 * N_DN_HEADS:
                if idx < N_DN_HEADS:
                    dh = dh * ATT_SCALE
                r = lax.rsqrt(jnp.sum(xh * xh, axis=-1, keepdims=True) + EPS)
                xhat = xh * r
                dh = r * (dh - xhat * jnp.sum(xhat * dh, axis=-1, keepdims=True))
            dcs.append(dh)
        return (jnp.concatenate(dcs, axis=1) * _silu_grad(yb, sg),)
    return _rowwise(fn, [y, dq, dk, dv], [], [(y.shape[1], F32)], [], tile=256, name=name)[0]


def _narrow_params(a_log, dt_bias, f_bias):
    lanes = lambda a, first: jnp.pad(a.reshape(1, -1), ((0, 0), (first, LANES - first - a.shape[0])))
    return jnp.concatenate([lanes(a_log, LANE_DECAY), lanes(dt_bias, LANE_DECAY), lanes(f_bias, LANE_FORGET),
                            jnp.zeros((5, LANES), F32)], axis=0)


def _narrow_masks(shape):
    lane = lax.broadcasted_iota(jnp.int32, shape, 1)
    is_beta = lane < LANE_DECAY
    is_decay = (lane >= LANE_DECAY) & (lane < LANE_FORGET)
    is_forget = (lane >= LANE_FORGET) & (lane < LANE_FORGET + N_FOX_HEADS)
    return is_beta, is_decay, is_forget


def _narrow_fwd(proj, params, name):
    def fn(sm, pk):
        is_beta, is_decay, is_forget = _narrow_masks(sm.shape)
        g = -jnp.exp(pk[0:1, :]) * _softplus(sm + pk[1:2, :])
        logf = -_softplus(-(sm + pk[2:3, :]))
        return (jnp.where(is_beta, _sigmoid(sm), jnp.where(is_decay, g, jnp.where(is_forget, logf, 0.0))),)
    return _rowwise(fn, [(proj, LANES, EVEN_NARROW // LANES)], [params], [(LANES, F32)], [], tile=512, name=name)[0]


def _narrow_bwd(proj, params, act, dact, dlogf, name):
    def fn(sm, ab, da, dl, pk):
        is_beta, is_decay, is_forget = _narrow_masks(sm.shape)
        db = jnp.where(is_forget, dl, da)
        d_beta = db * ab * (1.0 - ab)
        d_decay = db * (-jnp.exp(pk[0:1, :])) * _sigmoid(sm + pk[1:2, :])
        d_forget = db * _sigmoid(-(sm + pk[2:3, :]))
        dsm = jnp.where(is_beta, d_beta, jnp.where(is_decay, d_decay, jnp.where(is_forget, d_forget, 0.0)))
        col = lambda x: jnp.sum(x, axis=0, keepdims=True)
        return (dsm, col(jnp.where(is_decay, db * ab, 0.0)), col(jnp.where(is_decay, dsm, 0.0)),
                col(jnp.where(is_forget, dsm, 0.0)))
    return _rowwise(fn, [(proj, LANES, EVEN_NARROW // LANES), act, dact, dlogf], [params], [(LANES, BF16)],
                    [(1, LANES)] * 3, tile=512, name=name)


def _head_rms(xh):
    r = lax.rsqrt(jnp.mean(xh * xh, axis=-1, keepdims=True) + EPS)
    return xh * r, r


def _fox_pre_fwd(proj, qg, kg, name):
    def fn(pf, qgb, kgb):
        out = []
        for idx, xh in enumerate(_heads(pf, 3 * N_FOX_HEADS)):
            if idx < 2 * N_FOX_HEADS:
                xh = _head_rms(xh)[0] * (qgb if idx < N_FOX_HEADS else kgb)
            out.append(xh)
        return (jnp.concatenate(out, axis=1),)
    return _rowwise(fn, [(proj, 3 * D_FOX, EVEN_FOX_QKV // (3 * D_FOX))], [qg, kg], [(3 * D_FOX, BF16)], [],
                    tile=256, name=name)[0]


def _fox_pre_bwd(proj, qg, kg, dq, dk, dv, name):
    def fn(pf, dqb, dkb, dvb, qgb, kgb):
        dout = _heads(dqb, N_FOX_HEADS) + _heads(dkb, N_FOX_HEADS) + _heads(dvb, N_FOX_HEADS)
        dg = [jnp.zeros((1, HEAD_DIM), F32), jnp.zeros((1, HEAD_DIM), F32)]
        dx = []
        for idx, (xh, dh) in enumerate(zip(_heads(pf, 3 * N_FOX_HEADS), dout)):
            if idx < 2 * N_FOX_HEADS:
                which = 0 if idx < N_FOX_HEADS else 1
                xhat, r = _head_rms(xh)
                dg[which] = dg[which] + jnp.sum(dh * xhat, axis=0, keepdims=True)
                dxh = dh * (qgb if which == 0 else kgb)
                dh = r * (dxh - xhat * jnp.mean(dxh * xhat, axis=-1, keepdims=True))
            dx.append(dh)
        return jnp.concatenate(dx, axis=1), dg[0], dg[1]
    return _rowwise(fn, [(proj, 3 * D_FOX, EVEN_FOX_QKV // (3 * D_FOX)), dq, dk, dv], [qg, kg],
                    [(3 * D_FOX, BF16)], [(1, HEAD_DIM)] * 2, tile=256, name=name)


def _mix_gate_fwd(proj, o_dn, o_fox, ng, name):
    def fn(gd, gf, od, of, ngb):
        dn = [_head_rms(xh)[0] * ngb for xh in _heads(od, N_DN_HEADS)]
        return (jnp.concatenate([jnp.concatenate(dn, axis=1) * gd * _sigmoid(gd), of * _sigmoid(gf)], axis=1),)
    return _rowwise(fn, [(proj, D_DN, EVEN_DN_GATE // D_DN), (proj, D_FOX, EVEN_FOX_GATE // D_FOX), o_dn, o_fox],
                    [ng], [(D_DN + D_FOX, BF16)], [], tile=256, name=name)[0]


def _mix_gate_bwd(proj, o_dn, o_fox, ng, dom, name):
    def fn(gd, gf, od, of, dm, ngb):
        d_dn, d_fox = dm[:, :D_DN], dm[:, D_DN:]
        sgd, sgf = _sigmoid(gd), _sigmoid(gf)
        don = d_dn * gd * sgd
        dng = jnp.zeros((1, HEAD_DIM), F32)
        dod, normed = [], []
        for xh, dh in zip(_heads(od, N_DN_HEADS), _heads(don, N_DN_HEADS)):
            xhat, r = _head_rms(xh)
            dng = dng + jnp.sum(dh * xhat, axis=0, keepdims=True)
            dxh = dh * ngb
            dod.append(r * (dxh - xhat * jnp.mean(dxh * xhat, axis=-1, keepdims=True)))
            normed.append(xhat * ngb)
        d_gd = d_dn * jnp.concatenate(normed, axis=1) * _silu_grad(gd, sgd)
        d_gf = d_fox * of * sgf * (1.0 - sgf)
        return jnp.concatenate(dod, axis=1), d_fox * sgf, d_gd, d_gf, dng
    return _rowwise(fn, [(proj, D_DN, EVEN_DN_GATE // D_DN), (proj, D_FOX, EVEN_FOX_GATE // D_FOX), o_dn, o_fox, dom],
                    [ng], [(D_DN, F32), (D_FOX, F32), (D_DN, BF16), (D_FOX, BF16)], [(1, HEAD_DIM)], tile=256,
                    name=name)


def _loss_grad(y, target, name):
    d = y.shape[1]

    def fn(yb, tb):
        diff = yb - tb
        part = jnp.sum(jnp.sum(diff * diff, axis=1, keepdims=True), axis=0, keepdims=True) * (0.5 / d)
        g = diff * (1.0 / d)
        return g, g, part
    return _rowwise(fn, [y, target], [], [(d, F32), (d, BF16)], [(1, 1)], tile=512, name=name)


_REF_EVEN = {"dn_qkv": (0, 1536), "dn_gate": (1536, 2048), "dn_ba": (2048, 2056), "fox_qkv": (2056, 3592),
             "fox_gate": (3592, 4104), "f_pre": (4104, 4108)}
D_IN_EVEN = 4108


def _even_to_kernel_layout(w):
    cut = lambda name: w[..., _REF_EVEN[name][0]:_REF_EVEN[name][1]]
    pad = jnp.zeros(w.shape[:-1] + (EVEN_WIDTH - EVEN_NARROW - 12,), w.dtype)
    return jnp.concatenate([cut("dn_qkv"), cut("fox_qkv"), cut("dn_gate"), cut("fox_gate"), cut("dn_ba"),
                            cut("f_pre"), pad], axis=-1)


def _even_from_kernel_layout(g):
    return jnp.concatenate([g[..., EVEN_DN_QKV:EVEN_FOX_QKV], g[..., EVEN_DN_GATE:EVEN_FOX_GATE],
                            g[..., EVEN_NARROW:EVEN_NARROW + 8], g[..., EVEN_FOX_QKV:EVEN_DN_GATE],
                            g[..., EVEN_FOX_GATE:EVEN_NARROW], g[..., EVEN_NARROW + 8:EVEN_NARROW + 12]], axis=-1)


EVEN_QUARTER = 1027
EVEN_QUARTER_PAD = 1152


def _even_grad_quarters(g):
    g = _even_from_kernel_layout(g)
    pad = [(0, 0)] * (g.ndim - 1) + [(0, EVEN_QUARTER_PAD - EVEN_QUARTER)]
    return jnp.concatenate([jnp.pad(g[..., q * EVEN_QUARTER:(q + 1) * EVEN_QUARTER], pad) for q in range(4)], axis=-1)


def _forget_rows(c):
    return c[:, LANE_FORGET:LANE_FORGET + N_FOX_HEADS].T.reshape(N_FOX_HEADS, 1, c.shape[0])


def _forget_lanes(rows):
    s = rows.shape[2]
    return jnp.pad(rows.reshape(-1, s).T, ((0, 0), (LANE_FORGET, LANES - LANE_FORGET - N_FOX_HEADS)))


def _even_fwd(x, gain, w_in, w_out, j, p, tag):
    h = _rms_fwd(x, gain, f"{tag}_norm")
    proj = _mm(h, w_in, "nn", tm=512, tn=EVEN_WIDTH // 3, out_dtype=F32, name=f"{tag}_in", b_lead=(j,))
    y = _conv_fwd(proj, p["conv_w"], f"{tag}_conv")
    dn_qkv = _dn_pre_fwd(y, f"{tag}_dn_pre")
    act = _narrow_fwd(proj, p["narrow"], f"{tag}_narrow")
    o_dn, states, tinv = _dn_fwd(dn_qkv, act, f"{tag}_delta")
    fox_qkv = _fox_pre_fwd(proj, p["q_g"], p["k_g"], f"{tag}_fox_pre")
    c = _cumsum_rows(act, False, f"{tag}_cumsum")
    ct = _forget_rows(c)
    o_fox, lse = _fox_fwd(fox_qkv, c, ct, f"{tag}_fox")
    om = _mix_gate_fwd(proj, o_dn, o_fox, p["dn_norm_g"], f"{tag}_gate")
    x2 = _mm(om, w_out, "nn", tm=512, tn=x.shape[1], out_dtype=F32, name=f"{tag}_out", residual=x, b_lead=(j,))
    return x2, (x, h, proj, y, dn_qkv, act, states, tinv, o_dn, fox_qkv, c, ct, o_fox, lse, om)


def _even_bwd(dxo, dxo16, saved, gain, w_in, w_out, j, p, tag, g_in, g_out, after=None):
    x, h, proj, y, dn_qkv, act, states, tinv, o_dn, fox_qkv, c, ct, o_fox, lse, om = saved
    d = x.shape[1]
    dom = _mm(dxo16, w_out, "nt", tm=512, tn=d, out_dtype=F32, name=f"{tag}_out_bwd", b_lead=(j,), after=after)
    g_out = _mm(om, dxo16, "tn", tm=512, tn=d, out_dtype=F32, name=f"{tag}_out_dw", into=(g_out, 0))
    d_odn, d_ofox, d_gd, d_gf, d_ng = _mix_gate_bwd(proj, o_dn, o_fox, p["dn_norm_g"], dom, f"{tag}_gate_bwd")
    dq, dk, dv, dct = _fox_bwd(fox_qkv, c, ct, o_fox, lse, d_ofox, f"{tag}_fox_bwd")
    d_fox_qkv, d_qg, d_kg = _fox_pre_bwd(proj, p["q_g"], p["k_g"], dq, dk, dv, f"{tag}_fox_pre_bwd")
    dlogf = _cumsum_rows(_forget_lanes(dct), True, f"{tag}_cumsum_bwd")
    dq, dk, dv, dact = _dn_bwd(dn_qkv, act, states, tinv, d_odn, f"{tag}_delta_bwd")
    dy = _dn_pre_bwd(y, dq, dk, dv, f"{tag}_dn_pre_bwd")
    d_dn_qkv, d_conv = _conv_bwd(proj, p["conv_w"], dy, f"{tag}_conv_bwd")
    d_narrow, s_alog, s_dt, s_fb = _narrow_bwd(proj, p["narrow"], act, dact, dlogf, f"{tag}_narrow_bwd")
    dproj = jnp.concatenate([d_dn_qkv, d_fox_qkv, d_gd, d_gf, d_narrow], axis=1)
    g_in = _mm(h, dproj, "tn", tm=512, tn=EVEN_WIDTH // 3, out_dtype=F32, name=f"{tag}_in_dw", into=(g_in, 0))
    dx, dx16, d_gain = _in_proj_bwd(dproj, w_in, j, x, dxo, gain, f"{tag}_in_bwd")
    small = {"conv_w": d_conv, "a_log": s_alog, "dt_bias": s_dt, "f_bias": s_fb, "dn_norm_g": d_ng, "q_g": d_qg,
             "k_g": d_kg}
    return dx, dx16, d_gain, small, g_in, g_out


def _odd_fwd(x, gain, w_in, w_out, j, tag):
    h = _rms_fwd(x, gain, f"{tag}_norm")
    qkv = _mm(h, w_in, "nn", tm=512, tn=w_in.shape[2] // 2, out_dtype=BF16, name=f"{tag}_in", b_lead=(j,))
    o16, o32 = _sb_fwd(qkv, N_SB_HEADS, f"{tag}_sb")
    x2 = _mm(o16, w_out, "nn", tm=512, tn=x.shape[1], out_dtype=F32, name=f"{tag}_out", residual=x, b_lead=(j,))
    return x2, (x, h, qkv, o16, o32)


def _odd_bwd(dxo, dxo16, saved, gain, w_in, w_out, j, tag, g_in, g_out, after=None):
    x, h, qkv, o16, o32 = saved
    d = x.shape[1]
    do = _mm(dxo16, w_out, "nt", tm=512, tn=d, out_dtype=BF16, name=f"{tag}_out_bwd", b_lead=(j,), after=after)
    g_out = _mm(o16, dxo16, "tn", tm=512, tn=d, out_dtype=F32, name=f"{tag}_out_dw", into=(g_out, 0))
    dq, dk, dv = _sb_bwd(qkv, o32, do, N_SB_HEADS, f"{tag}_sb_bwd")
    dqkv = jnp.concatenate([dq, dk.astype(BF16), dv.astype(BF16)], axis=1)
    g_in = _mm(h, dqkv, "tn", tm=512, tn=w_in.shape[2] // 2, out_dtype=F32, name=f"{tag}_in_dw", into=(g_in, 0))
    dx, dx16, d_gain = _in_proj_bwd(dqkv, w_in, j, x, dxo, gain, f"{tag}_in_bwd")
    return dx, dx16, d_gain, g_in, g_out


def _forward_backward(x, target, w, first, rest_after, token, on_reduced):
    depth = w["norm_ffn1"].shape[0]
    row = lambda a, l: a[l][None]
    rest = {}

    def mats(names, j):
        if j == 0 and names[0] in first:
            return [first[name] for name in names] + [0]
        return [rest[name] for name in names] + [j - (1 if names[0] in first else 0)]

    def even_small(j):
        return {"conv_w": w["dn_conv_w"][j], "narrow": _narrow_params(w["dn_a_log"][j], w["dn_dt_bias"][j],
                                                                     w["fox_f_bias"][j]),
                "dn_norm_g": row(w["dn_norm_g"], j), "q_g": row(w["fox_q_norm_g"], j),
                "k_g": row(w["fox_k_norm_g"], j)}

    saved = []
    for l in range(depth):
        if l == 1:
            rest.update(rest_after(x))
        gain = row(w["norm_ffn1"], l) + token[0:1, 0:1] if l == 0 else row(w["norm_ffn1"], l)
        x, s1 = _ffn_fwd(x, gain, *mats(("ffn1_w_gu", "ffn1_w_down"), l), "ffn1")
        if l % 2 == 0:
            x, s2 = _even_fwd(x, row(w["norm_mix"], l), *mats(("w_in_even", "w_out_even"), l // 2),
                              even_small(l // 2), "even")
        else:
            x, s2 = _odd_fwd(x, row(w["norm_mix"], l), *mats(("w_in_odd", "w_out_odd"), l // 2), "odd")
        x, s3 = _ffn_fwd(x, row(w["norm_ffn2"], l), *mats(("ffn2_w_gu", "ffn2_w_down"), l), "ffn2")
        saved.append((s1, s2, s3))

    dx, dx16, loss = _loss_grad(x, target, "loss")

    kind_of = dict(BIG)
    d_norm = {k: [None] * depth for k in ("norm_ffn1", "norm_mix", "norm_ffn2")}
    d_even = [None] * ((depth + 1) // 2)
    to_sibling, between_chips, token = None, None, None
    for l in reversed(range(depth)):
        s1, s2, s3 = saved[l]
        mixer = ("w_in_even", "w_out_even") if l % 2 == 0 else ("w_in_odd", "w_out_odd")
        names = ["ffn1_w_gu", "ffn1_w_down", *mixer, "ffn2_w_gu", "ffn2_w_down"]
        g = {name: lax.empty((1,) + rest[name].shape[1:], F32) for name in names}
        dx, dx16, d_norm["norm_ffn2"][l], g["ffn2_w_gu"], g["ffn2_w_down"] = _ffn_bwd(
            dx, dx16, s3, row(w["norm_ffn2"], l), *mats(("ffn2_w_gu", "ffn2_w_down"), l), "ffn2", g["ffn2_w_gu"],
            g["ffn2_w_down"], after=token)
        if to_sibling is not None:
            between_chips, token = _reduce_middle(to_sibling, dx)
        if l % 2 == 0:
            dx, dx16, d_norm["norm_mix"][l], d_even[l // 2], g["w_in_even"], g["w_out_even"] = _even_bwd(
                dx, dx16, s2, row(w["norm_mix"], l), *mats(("w_in_even", "w_out_even"), l // 2), even_small(l // 2),
                "even", g["w_in_even"], g["w_out_even"], after=token)
            g["w_in_even"] = _even_grad_quarters(g["w_in_even"])
        else:
            dx, dx16, d_norm["norm_mix"][l], g["w_in_odd"], g["w_out_odd"] = _odd_bwd(
                dx, dx16, s2, row(w["norm_mix"], l), *mats(("w_in_odd", "w_out_odd"), l // 2), "odd", g["w_in_odd"],
                g["w_out_odd"], after=token)
        dx, dx16, d_norm["norm_ffn1"][l], g["ffn1_w_gu"], g["ffn1_w_down"] = _ffn_bwd(
            dx, dx16, s1, row(w["norm_ffn1"], l), *mats(("ffn1_w_gu", "ffn1_w_down"), l), "ffn1", g["ffn1_w_gu"],
            g["ffn1_w_down"])
        to_sibling, token = _reduce_start([g[name] for name in names], [kind_of[name] for name in names], names,
                                          f"layer{l}")
        if between_chips is not None:
            on_reduced(l + 1, dict(zip(between_chips[-2], _reduce_finish(between_chips, dx))))
    between_chips, _ = _reduce_middle(to_sibling, dx)
    on_reduced(0, dict(zip(between_chips[-2], _reduce_finish(between_chips, dx))))

    small = {k: jnp.concatenate(v, axis=0) for k, v in d_norm.items()}
    dec = slice(LANE_DECAY, LANE_DECAY + N_DN_HEADS)
    fgt = slice(LANE_FORGET, LANE_FORGET + N_FOX_HEADS)
    small["dn_conv_w"] = jnp.stack([e["conv_w"] for e in d_even])
    small["dn_a_log"] = jnp.concatenate([e["a_log"][:, dec] for e in d_even], axis=0)
    small["dn_dt_bias"] = jnp.concatenate([e["dt_bias"][:, dec] for e in d_even], axis=0)
    small["fox_f_bias"] = jnp.concatenate([e["f_bias"][:, fgt] for e in d_even], axis=0)
    small["dn_norm_g"] = jnp.concatenate([e["dn_norm_g"] for e in d_even], axis=0)
    small["fox_q_norm_g"] = jnp.concatenate([e["q_g"] for e in d_even], axis=0)
    small["fox_k_norm_g"] = jnp.concatenate([e["k_g"] for e in d_even], axis=0)
    return loss, dx, small


MESH = pl.DeviceIdType.MESH
ANY = pl.BlockSpec(memory_space=pl.ANY)


def _place():
    x, y, c = lax.axis_index("x"), lax.axis_index("y"), lax.axis_index("c")
    return x, y, c, [(1 - x, y), (x, 1 - y), (1 - x, 1 - y)]


def _remote(src, dst, send_sem, recv_sem, to):
    return pltpu.make_async_remote_copy(src_ref=src, dst_ref=dst, send_sem=send_sem, recv_sem=recv_sem,
                                        device_id=to, device_id_type=MESH)


def _aligned(start, multiple):
    return start if isinstance(start, int) else pl.multiple_of(start, multiple)


def _quarter(ref, kind, chip, half, rows, cols):
    k = 2 * chip[0] + chip[1]
    hr = rows // 2
    assert hr % 16 == 0 and cols % LANES == 0
    if kind == "col":
        return ref.at[:, pl.ds(_aligned(half * hr, 16), hr), pl.ds(_aligned(k * cols, LANES), cols)]
    return ref.at[:, pl.ds(_aligned(k * rows + half * hr, 16), hr), :]


def _place_quarter(shard, kind, kc, name, first=0, count=None):
    l, rows, cols = shard.shape
    l = l - first if count is None else count
    tr = rows
    while tr * cols * 4 > (2 << 20) and tr % 32 == 0:
        tr //= 2
    nr = rows // tr
    if kind == "col":
        out_spec = pl.BlockSpec((None, tr, cols), lambda li, i, kc_ref: (li, i, kc_ref[0]))
        out_shape = (l, rows, 4 * cols)
    else:
        out_spec = pl.BlockSpec((None, tr, cols), lambda li, i, kc_ref: (li, kc_ref[0] * nr + i, 0))
        out_shape = (l, 4 * rows, cols)

    def body(kc_ref, x_ref, o_ref):
        o_ref[...] = x_ref[...].astype(BF16)

    return pl.pallas_call(
        body, name=name,
        grid_spec=pltpu.PrefetchScalarGridSpec(
            num_scalar_prefetch=1, grid=(l, nr),
            in_specs=[pl.BlockSpec((None, tr, cols), lambda li, i, kc_ref: (li + first, i, 0))],
            out_specs=out_spec),
        out_shape=jax.ShapeDtypeStruct(out_shape, BF16),
        compiler_params=_cparams("parallel", "parallel"),
    )(kc, shard)


def _gather_weights(wholes, kinds):
    n = len(wholes)

    def dims(ref, kind):
        _, r, cc = ref.shape
        return (r, cc // 4) if kind == "col" else (r // 4, cc)

    def body(*refs):
        bufs = refs[n:2 * n]
        send_sems, recv_sems = refs[2 * n:]
        x, y, c, chips = _place()
        sibling = (x, y, 1 - c)
        first, passed = [], []
        for t in range(n):
            rows, cols = dims(bufs[t], kinds[t])
            mine = _quarter(bufs[t], kinds[t], (x, y), c, rows, cols)
            for j, chip in enumerate(chips):
                cp = _remote(mine, mine, send_sems.at[t, j], recv_sems.at[t, j], (*chip, c))
                cp.start()
                first.append(cp)
        for j, chip in enumerate(chips):
            for t in range(n):
                rows, cols = dims(bufs[t], kinds[t])
                got = _quarter(bufs[t], kinds[t], chip, c, rows, cols)
                _remote(got, got, send_sems.at[t, j], recv_sems.at[t, j], (*chip, c)).wait_recv()
                cp = _remote(got, got, send_sems.at[t, 3 + j], recv_sems.at[t, 3 + j], sibling)
                cp.start()
                passed.append(cp)
        for j, chip in enumerate(chips):
            for t in range(n):
                rows, cols = dims(bufs[t], kinds[t])
                got = _quarter(bufs[t], kinds[t], chip, 1 - c, rows, cols)
                _remote(got, got, send_sems.at[t, 3 + j], recv_sems.at[t, 3 + j], sibling).wait_recv()
        for cp in first + passed:
            cp.wait_send()

    return pl.pallas_call(
        body, name="gather_weights", in_specs=[ANY] * n, out_specs=[ANY] * n,
        out_shape=[jax.ShapeDtypeStruct(a.shape, a.dtype) for a in wholes],
        input_output_aliases={t: t for t in range(n)},
        scratch_shapes=[pltpu.SemaphoreType.DMA((n, 6)), pltpu.SemaphoreType.DMA((n, 6))],
        compiler_params=pltpu.CompilerParams(has_side_effects=True),
    )(*wholes)


def _quarter_dims(ref, kind):
    _, r, cc = ref.shape
    return (r, cc // 4) if kind == "col" else (r // 4, cc)


def _gather_chips_copies(bufs, sems, kinds):
    x, y, c, chips = _place()
    copies = []
    for t, buf in enumerate(bufs):
        rows, cols = _quarter_dims(buf, kinds[t])
        mine = _quarter(buf, kinds[t], (x, y), c, rows, cols)
        for j, chip in enumerate(chips):
            pair = 2 * (OTHER_CHIPS * t + j)
            copies.append(_remote(mine, mine, sems[pair], sems[pair + 1], (*chip, c)))
    return copies


def _gather_start(wholes, kinds, after, tag):
    n = len(wholes)
    n_sems = 2 * OTHER_CHIPS * n
    n_in = n + len(after)

    def body(*refs):
        for cp in _gather_chips_copies(refs[:n], refs[n_in + n:n_in + n + n_sems], kinds):
            cp.start()
        refs[-1][...] = jnp.zeros_like(refs[-1])

    held = [pltpu.with_memory_space_constraint(a, pltpu.HBM) for a in wholes]
    out = pl.pallas_call(
        body, name=f"gather_start_{tag}", in_specs=[HBM] * n + [ANY] * len(after),
        out_specs=(*[HBM] * n, *[SEM] * n_sems, pl.BlockSpec(memory_space=pltpu.VMEM)),
        out_shape=(*[pltpu.HBM(a.shape, a.dtype) for a in held], *[pltpu.SemaphoreType.DMA(())] * n_sems,
                   jax.ShapeDtypeStruct((8, LANES), F32)),
        input_output_aliases={i: i for i in range(n)},
        compiler_params=pltpu.CompilerParams(has_side_effects=SPLIT_COPY),
    )(*held, *after)
    return out[n:n + n_sems], out[:n], out[-1]


def _gather_wait(sems, wholes, kinds, after, tag):
    n = len(wholes)

    def body(*refs):
        for cp in _gather_chips_copies(refs[:n], refs[n:n + len(sems)], kinds):
            cp.wait_send()
            cp.wait_recv()

    return pl.pallas_call(
        body, name=f"gather_wait_{tag}", in_specs=[HBM] * n + [SEM] * len(sems) + [ANY],
        out_specs=tuple([HBM] * n), out_shape=tuple(pltpu.HBM(a.shape, a.dtype) for a in wholes),
        input_output_aliases={i: i for i in range(n)},
        compiler_params=pltpu.CompilerParams(has_side_effects=SPLIT_COPY),
    )(*wholes, *sems, after)


def _gather_forward(wholes, kinds, tag):
    n = len(wholes)

    def body(*refs):
        bufs = refs[n:2 * n]
        send_sems, recv_sems = refs[2 * n:]
        x, y, c, chips = _place()
        copies = []
        for t in range(n):
            rows, cols = _quarter_dims(bufs[t], kinds[t])
            for j, chip in enumerate(chips):
                got = _quarter(bufs[t], kinds[t], chip, c, rows, cols)
                cp = _remote(got, got, send_sems.at[t, j], recv_sems.at[t, j], (x, y, 1 - c))
                cp.start()
                copies.append(cp)
        for cp in copies:
            cp.wait_send()
        for t in range(n):
            rows, cols = _quarter_dims(bufs[t], kinds[t])
            for j, chip in enumerate(chips):
                got = _quarter(bufs[t], kinds[t], chip, 1 - c, rows, cols)
                _remote(got, got, send_sems.at[t, j], recv_sems.at[t, j], (x, y, 1 - c)).wait_recv()

    return pl.pallas_call(
        body, name=f"gather_forward_{tag}", in_specs=[ANY] * n, out_specs=[ANY] * n,
        out_shape=[jax.ShapeDtypeStruct(a.shape, a.dtype) for a in wholes],
        input_output_aliases={t: t for t in range(n)},
        scratch_shapes=[pltpu.SemaphoreType.DMA((n, OTHER_CHIPS)), pltpu.SemaphoreType.DMA((n, OTHER_CHIPS))],
        compiler_params=pltpu.CompilerParams(has_side_effects=True),
    )(*wholes)


def _canonical(a, kind):
    l, r, c = a.shape
    return a.reshape(l, 1, r, c) if kind == "col" else a.reshape(l, 4, r // 4, c)


def _add_tile(rows, cols):
    tc = cols if cols <= 1536 else cols // 4
    tr = rows
    while tr * tc * 4 > (1 << 20) and tr % 16 == 0:
        tr //= 2
    return tr, tc


def _rs_add_sibling(part, got, c, name):
    l, a, hr, cols = got.shape
    tr, tc = _add_tile(hr, cols)
    nr = hr // tr

    def body(c_ref, p_ref, g_ref, o32_ref, o16_ref):
        s = p_ref[...] + g_ref[...]
        o32_ref[...] = s
        o16_ref[...] = s.astype(BF16)

    blk = (None, None, tr, tc)
    spec = pl.BlockSpec(blk, lambda li, ai, i, j, c_ref: (li, ai, i, j))
    return pl.pallas_call(
        body, name=name,
        grid_spec=pltpu.PrefetchScalarGridSpec(
            num_scalar_prefetch=1, grid=(l, a, nr, cols // tc),
            in_specs=[pl.BlockSpec(blk, lambda li, ai, i, j, c_ref: (li, ai, c_ref[0] * nr + i, j)), spec],
            out_specs=[spec, spec]),
        out_shape=[jax.ShapeDtypeStruct(got.shape, F32), jax.ShapeDtypeStruct(got.shape, BF16)],
        compiler_params=_cparams("parallel", "parallel", "parallel", "parallel"),
    )(c, part, got)


def _quarter4(ref, kind, chip, cols):
    k = 2 * chip[0] + chip[1]
    if kind == "col":
        return ref.at[:, :, :, pl.ds(pl.multiple_of(k * cols, LANES), cols)]
    return ref.at[:, pl.ds(k, 1), :, :]


HBM = pl.BlockSpec(memory_space=pltpu.HBM)
SEM = pl.BlockSpec(memory_space=pltpu.SEMAPHORE)
SPLIT_COPY = pltpu.SideEffectType.DATAFLOW_SIDE_EFFECTING
OTHER_CHIPS = 3


def _quarter4_shape(a, kind):
    l, _, hr, cols = a.shape
    return (l, 1, hr, cols // 4 if kind == "col" else cols)


def _rs_chips_copies(srcs, lands, sems, kinds):
    x, y, c, chips = _place()
    copies = []
    for t, (src, land) in enumerate(zip(srcs, lands)):
        cols = _quarter4_shape(src, kinds[t])[3]
        for j, chip in enumerate(chips):
            pair = 2 * (OTHER_CHIPS * t + j)
            copies.append(_remote(_quarter4(src, kinds[t], chip, cols), land.at[j], sems[pair], sems[pair + 1],
                                  (*chip, c)))
    return copies


def _split_start(copies, srcs, lands, n_sems, name):
    n = len(srcs)

    def body(*refs):
        for cp in copies(refs[:n], refs[n:2 * n], refs[4 * n:4 * n + n_sems]):
            cp.start()
        refs[-1][...] = jnp.zeros_like(refs[-1])

    held = [pltpu.with_memory_space_constraint(a, pltpu.HBM) for a in (*srcs, *lands)]
    out = pl.pallas_call(
        body, name=name, in_specs=[HBM] * (2 * n),
        out_specs=(*[HBM] * (2 * n), *[SEM] * n_sems, pl.BlockSpec(memory_space=pltpu.VMEM)),
        out_shape=(*[pltpu.HBM(a.shape, a.dtype) for a in held], *[pltpu.SemaphoreType.DMA(())] * n_sems,
                   jax.ShapeDtypeStruct((8, LANES), F32)),
        input_output_aliases={i: i for i in range(2 * n)},
        compiler_params=pltpu.CompilerParams(has_side_effects=SPLIT_COPY),
    )(*held)
    return out[2 * n:2 * n + n_sems], out[:n], out[n:2 * n], out[-1]


def _split_wait(copies, sems, srcs, lands, after, name):
    n = len(srcs)

    def body(*refs):
        for cp in copies(refs[:n], refs[n:2 * n], refs[2 * n:2 * n + len(sems)]):
            cp.wait_send()
            cp.wait_recv()

    out = pl.pallas_call(
        body, name=name, in_specs=[HBM] * (2 * n) + [SEM] * len(sems) + [ANY],
        out_specs=tuple([HBM] * (2 * n)),
        out_shape=tuple(pltpu.HBM(a.shape, a.dtype) for a in (*srcs, *lands)),
        input_output_aliases={i: i for i in range(2 * n)},
        compiler_params=pltpu.CompilerParams(has_side_effects=SPLIT_COPY),
    )(*srcs, *lands, *sems, after)
    return out[:n], out[n:]


def _rs_sibling_copies(srcs, lands, sems):
    x, y, c, _ = _place()
    copies = []
    for t, (src, land) in enumerate(zip(srcs, lands)):
        hr = src.shape[2] // 2
        gives = src.at[:, :, pl.ds(pl.multiple_of((1 - c) * hr, 8), hr), :]
        copies.append(_remote(gives, land, sems[2 * t], sems[2 * t + 1], (x, y, 1 - c)))
    return copies


def _rs_add_chips(sum32, got, kind, kc, name):
    _, l, _, hr, cols = got.shape
    tr, _ = _add_tile(hr, cols)
    nr = hr // tr
    k_arr, c_arr = kc
    if kind == "col":
        own = pl.BlockSpec((None, None, tr, cols), lambda li, i, k_ref, c_ref: (li, 0, i, k_ref[0]))
    else:
        own = pl.BlockSpec((None, None, tr, cols), lambda li, i, k_ref, c_ref: (li, k_ref[0], i, 0))

    def body(k_ref, c_ref, own_ref, got_ref, o_ref):
        o_ref[...] = ((own_ref[...] + got_ref[0].astype(F32)) + got_ref[1].astype(F32)) + got_ref[2].astype(F32)

    return pl.pallas_call(
        body, name=name,
        grid_spec=pltpu.PrefetchScalarGridSpec(
            num_scalar_prefetch=2, grid=(l, nr),
            in_specs=[own, pl.BlockSpec((3, None, None, tr, cols), lambda li, i, k_ref, c_ref: (0, li, 0, i, 0))],
            out_specs=pl.BlockSpec((None, tr, cols), lambda li, i, k_ref, c_ref: (li, c_ref[0] * nr + i, 0))),
        out_shape=jax.ShapeDtypeStruct((l, 2 * hr, cols), F32),
        compiler_params=_cparams("parallel", "parallel"),
    )(k_arr, c_arr, sum32, got)


def _rs_finish(quarters):
    n = len(quarters)

    def body(*refs):
        bufs = refs[n:2 * n]
        send_sems, recv_sems = refs[2 * n:]
        x, y, c, _ = _place()
        copies = []
        for t in range(n):
            hr = bufs[t].shape[1] // 2
            mine = bufs[t].at[:, pl.ds(pl.multiple_of(c * hr, 8), hr), :]
            cp = _remote(mine, mine, send_sems.at[t], recv_sems.at[t], (x, y, 1 - c))
            cp.start()
            copies.append(cp)
        for cp in copies:
            cp.wait()

    return pl.pallas_call(
        body, name="reduce_finish", in_specs=[ANY] * n, out_specs=[ANY] * n,
        out_shape=[jax.ShapeDtypeStruct(a.shape, a.dtype) for a in quarters],
        input_output_aliases={t: t for t in range(n)},
        scratch_shapes=[pltpu.SemaphoreType.DMA((n,)), pltpu.SemaphoreType.DMA((n,))],
        compiler_params=pltpu.CompilerParams(has_side_effects=True),
    )(*quarters)


def _reduce_start(parts, kinds, names, tag):
    canon = [_canonical(p, kind) for p, kind in zip(parts, kinds)]
    lands = [lax.empty(a.shape[:2] + (a.shape[2] // 2, a.shape[3]), a.dtype) for a in canon]
    sems, srcs, lands, token = _split_start(_rs_sibling_copies, canon, lands, 2 * len(canon),
                                            f"reduce_sibling_start_{tag}")
    return (sems, srcs, lands, kinds, names, tag), token


def _reduce_middle(state, after):
    sems, srcs, lands, kinds, names, tag = state
    c_arr = jnp.reshape(lax.axis_index("c"), (1,)).astype(jnp.int32)
    srcs, from_sibling = _split_wait(_rs_sibling_copies, sems, srcs, lands, after, f"reduce_sibling_wait_{tag}")
    sums = [_rs_add_sibling(p, g, c_arr, f"reduce_add_sibling_{nm}") for p, g, nm in zip(srcs, from_sibling, names)]
    sums16 = [s16 for _, s16 in sums]
    copies = functools.partial(_rs_chips_copies, kinds=kinds)
    lands = [lax.empty((OTHER_CHIPS,) + _quarter4_shape(a, k), a.dtype) for a, k in zip(sums16, kinds)]
    sems, srcs, lands, token = _split_start(copies, sums16, lands, 2 * OTHER_CHIPS * len(sums16),
                                            f"reduce_chips_start_{tag}")
    return (sems, srcs, lands, [s32 for s32, _ in sums], kinds, names, tag), token


def _reduce_finish(state, after):
    sems, srcs, lands, sums32, kinds, names, tag = state
    x, y, c = lax.axis_index("x"), lax.axis_index("y"), lax.axis_index("c")
    kc = (jnp.reshape(2 * x + y, (1,)).astype(jnp.int32), jnp.reshape(c, (1,)).astype(jnp.int32))
    copies = functools.partial(_rs_chips_copies, kinds=kinds)
    _, from_chips = _split_wait(copies, sems, srcs, lands, after, f"reduce_chips_wait_{tag}")
    halves = [_rs_add_chips(s32, g, kind, kc, f"reduce_add_chips_{nm}")
              for s32, g, kind, nm in zip(sums32, from_chips, kinds, names)]
    return _rs_finish(halves)


SMALL_PEERS = 7


def _small_exchange(pack):
    rows = pack.shape[0]

    def body(p_ref, slots_ref, total_ref, send_sems, recv_sems):
        x, y, c, _ = _place()
        me = 4 * x + 2 * y + c
        slots_ref[me] = p_ref[...]
        copies = []
        for p in range(1, SMALL_PEERS + 1):
            px, py, pc = (p >> 2) & 1, (p >> 1) & 1, p & 1
            peer = (1 - x if px else x, 1 - y if py else y, 1 - c if pc else c)
            cp = _remote(p_ref, slots_ref.at[me], send_sems.at[p - 1], recv_sems.at[p - 1], peer)
            cp.start()
            copies.append(cp)
        for cp in copies:
            cp.wait()
        total = slots_ref[0]
        for i in range(1, SMALL_PEERS + 1):
            total = total + slots_ref[i]
        total_ref[...] = total

    vmem = pl.BlockSpec(memory_space=pltpu.VMEM)
    return pl.pallas_call(
        body, name="small_exchange", in_specs=[vmem], out_specs=[vmem, vmem],
        out_shape=[jax.ShapeDtypeStruct((SMALL_PEERS + 1, rows, LANES), F32), jax.ShapeDtypeStruct((rows, LANES), F32)],
        scratch_shapes=[pltpu.SemaphoreType.DMA((SMALL_PEERS,)), pltpu.SemaphoreType.DMA((SMALL_PEERS,))],
        compiler_params=pltpu.CompilerParams(has_side_effects=True),
    )(pack)


def _pack(arrays):
    rows = []
    for a in arrays:
        flat = a.reshape(-1).astype(F32)
        rows.append(jnp.pad(flat, (0, (-flat.shape[0]) % LANES)).reshape(-1, LANES))
    out = jnp.concatenate(rows, axis=0)
    return jnp.pad(out, ((0, (-out.shape[0]) % 8), (0, 0)))


def _unpack(pack, shapes):
    out, r = [], 0
    for sh in shapes:
        size = math.prod(sh)
        nr = -(-size // LANES)
        out.append(pack[r:r + nr].reshape(-1)[:size].reshape(sh))
        r += nr
    return out


def _adamw(w, g, m, v, name):
    shape = w.shape
    to2d = lambda a: a.reshape(-1, shape[-1])
    rows = math.prod(shape[:-1])
    tile = 256 if rows % 256 == 0 else rows

    def fn(wb, gb, mb, vb):
        m2 = ADAM_B1 * mb + (1.0 - ADAM_B1) * gb
        v2 = ADAM_B2 * vb + (1.0 - ADAM_B2) * (gb * gb)
        m_hat = m2 / (1.0 - ADAM_B1 ** ADAM_STEP)
        v_hat = v2 / (1.0 - ADAM_B2 ** ADAM_STEP)
        return -ADAM_LR * (m_hat / (jnp.sqrt(v_hat) + ADAM_EPS) + ADAM_WD * wb), m2, v2

    res = _rowwise(fn, [to2d(w), to2d(g), to2d(m), to2d(v)], [], [(shape[-1], F32)] * 3, [], tile=tile, name=name)
    return [r.reshape(shape) for r in res]


def _adamw_layer(w, g, m, v, layer, outs, name):
    _, rows, cols = w.shape
    tile = rows
    while tile * cols * 4 > (1 << 20) and tile % 16 == 0:
        tile //= 2

    def body(w_ref, g_ref, m_ref, v_ref, *rest):
        g_out, d_out, m_out, v_out = rest[-4:]
        gb = g_ref[...]
        m2 = ADAM_B1 * m_ref[...] + (1.0 - ADAM_B1) * gb
        v2 = ADAM_B2 * v_ref[...] + (1.0 - ADAM_B2) * (gb * gb)
        m_hat = m2 / (1.0 - ADAM_B1 ** ADAM_STEP)
        v_hat = v2 / (1.0 - ADAM_B2 ** ADAM_STEP)
        g_out[...] = gb
        d_out[...] = -ADAM_LR * (m_hat / (jnp.sqrt(v_hat) + ADAM_EPS) + ADAM_WD * w_ref[...])
        m_out[...] = m2
        v_out[...] = v2

    stacked = pl.BlockSpec((None, tile, cols), lambda i: (layer, i, 0))
    return pl.pallas_call(
        body, name=name, grid=(rows // tile,),
        in_specs=[stacked, pl.BlockSpec((None, tile, cols), lambda i: (0, i, 0)), stacked, stacked] + [ANY] * 4,
        out_specs=[stacked] * 4, out_shape=[jax.ShapeDtypeStruct(w.shape, F32)] * 4,
        input_output_aliases={4 + i: i for i in range(4)}, compiler_params=_cparams("parallel"),
    )(w, g, m, v, *outs)


BIG = (("ffn1_w_gu", "col"), ("ffn1_w_down", "row"), ("w_in_even", "col"), ("w_out_even", "row"),
       ("w_in_odd", "col"), ("w_out_odd", "row"), ("ffn2_w_gu", "col"), ("ffn2_w_down", "row"))
SMALL = ("norm_ffn1", "norm_mix", "dn_conv_w", "dn_a_log", "dn_dt_bias", "dn_norm_g", "fox_q_norm_g", "fox_k_norm_g",
         "fox_f_bias", "norm_ffn2")
WEIGHTS = ("norm_ffn1", "ffn1_w_gu", "ffn1_w_down", "norm_mix", "w_in_even", "dn_conv_w", "dn_a_log", "dn_dt_bias",
           "dn_norm_g", "fox_q_norm_g", "fox_k_norm_g", "fox_f_bias", "w_out_even", "w_in_odd", "w_out_odd",
           "norm_ffn2", "ffn2_w_gu", "ffn2_w_down")


def _step(x, target, w, m, v):
    k = 2 * lax.axis_index("x") + lax.axis_index("y")
    n_conv = w["dn_conv_w"].shape[2]

    kc = jnp.reshape(k, (1,)).astype(jnp.int32)
    kinds = dict(BIG)
    quarters = {name: w[name] for name in kinds}
    quarters["w_in_even"] = jnp.pad(w["w_in_even"], ((0, 0), (0, 0), (0, EVEN_QUARTER_PAD - EVEN_QUARTER)))
    first_names = [name for name in kinds if name not in ("w_in_odd", "w_out_odd")]
    rest_names = list(kinds)

    def even_columns(whole):
        padded = whole["w_in_even"]
        ref_order = jnp.concatenate([padded[..., q * EVEN_QUARTER_PAD:q * EVEN_QUARTER_PAD + EVEN_QUARTER]
                                     for q in range(4)], axis=-1)
        return {**whole, "w_in_even": _even_to_kernel_layout(ref_order)}

    conv_slots, _ = _small_exchange(_pack([w["dn_conv_w"]]))
    placed = [_place_quarter(quarters[name], kinds[name], kc, f"place_first_{name}", 0, 1) for name in first_names]
    gathered = _gather_weights(placed, [kinds[name] for name in first_names])
    first = even_columns(dict(zip(first_names, gathered)))
    placed = [_place_quarter(quarters[name], kinds[name], kc, f"place_rest_{name}", 1 if name in first_names else 0)
              for name in rest_names]
    rest_kinds = [kinds[name] for name in rest_names]
    sems, on_their_way, token = _gather_start(placed, rest_kinds, [conv_slots, *gathered], "rest")

    def rest_after(value):
        landed = _gather_wait(sems, on_their_way, rest_kinds, value, "rest")
        return even_columns(dict(zip(rest_names, _gather_forward(landed, rest_kinds, "rest"))))

    whole = {}
    conv_rows = math.prod(w["dn_conv_w"].shape) // LANES
    conv_quarters = [conv_slots[2 * q, :conv_rows].reshape(w["dn_conv_w"].shape) for q in range(4)]
    whole["dn_conv_w"] = jnp.concatenate(conv_quarters, axis=-1)
    for name in SMALL:
        if name != "dn_conv_w":
            whole[name] = w[name]

    updated = {name: [lax.empty(w[name].shape, F32) for _ in range(4)] for name in kinds}

    def on_reduced(layer, layer_grads):
        for name, g in layer_grads.items():
            if name == "w_in_even":
                g = g[..., :EVEN_QUARTER]
            stacked_layer = layer if w[name].shape[0] == w["norm_mix"].shape[0] else layer // 2
            updated[name] = _adamw_layer(w[name], g, m[name], v[name], stacked_layer, updated[name], f"adamw_{name}")

    loss, dx, small = _forward_backward(x, target, whole, first, rest_after, token, on_reduced)

    _, small_sum = _small_exchange(_pack([small[n] for n in SMALL]))
    grads = dict(zip(SMALL, _unpack(small_sum, [small[n].shape for n in SMALL])))
    grads["dn_conv_w"] = lax.dynamic_slice_in_dim(grads["dn_conv_w"], k * n_conv, n_conv, axis=2)
    delta, new_m, new_v = {}, {}, {}
    for name in kinds:
        grads[name], delta[name], new_m[name], new_v[name] = updated[name]
    packs = [_pack([d[n] for n in SMALL]) for d in (w, grads, m, v)]
    shapes = [w[n].shape for n in SMALL]
    for out, res in zip((delta, new_m, new_v), _adamw(*packs, "adamw_small")):
        out.update(zip(SMALL, _unpack(res, shapes)))
    total_loss = lax.psum(loss[0, 0], ("x", "y", "c"))
    return total_loss, dx, grads, delta, new_m, new_v


def kernel(x, norm_ffn1, ffn1_w_gu, ffn1_w_down, norm_mix, w_in_even, dn_conv_w, dn_a_log, dn_dt_bias, dn_norm_g, fox_q_norm_g, fox_k_norm_g, fox_f_bias, w_out_even, w_in_odd, w_out_odd, norm_ffn2, ffn2_w_gu, ffn2_w_down, loss_target, m_norm_ffn1, m_ffn1_w_gu, m_ffn1_w_down, m_norm_mix, m_w_in_even, m_dn_conv_w, m_dn_a_log, m_dn_dt_bias, m_dn_norm_g, m_fox_q_norm_g, m_fox_k_norm_g, m_fox_f_bias, m_w_out_even, m_w_in_odd, m_w_out_odd, m_norm_ffn2, m_ffn2_w_gu, m_ffn2_w_down, v_norm_ffn1, v_ffn1_w_gu, v_ffn1_w_down, v_norm_mix, v_w_in_even, v_dn_conv_w, v_dn_a_log, v_dn_dt_bias, v_dn_norm_g, v_fox_q_norm_g, v_fox_k_norm_g, v_fox_f_bias, v_w_out_even, v_w_in_odd, v_w_out_odd, v_norm_ffn2, v_ffn2_w_gu, v_ffn2_w_down):
    w = dict(zip(WEIGHTS, (norm_ffn1, ffn1_w_gu, ffn1_w_down, norm_mix, w_in_even, dn_conv_w, dn_a_log, dn_dt_bias,
                           dn_norm_g, fox_q_norm_g, fox_k_norm_g, fox_f_bias, w_out_even, w_in_odd, w_out_odd,
                           norm_ffn2, ffn2_w_gu, ffn2_w_down)))
    m = dict(zip(WEIGHTS, (m_norm_ffn1, m_ffn1_w_gu, m_ffn1_w_down, m_norm_mix, m_w_in_even, m_dn_conv_w, m_dn_a_log,
                           m_dn_dt_bias, m_dn_norm_g, m_fox_q_norm_g, m_fox_k_norm_g, m_fox_f_bias, m_w_out_even,
                           m_w_in_odd, m_w_out_odd, m_norm_ffn2, m_ffn2_w_gu, m_ffn2_w_down)))
    v = dict(zip(WEIGHTS, (v_norm_ffn1, v_ffn1_w_gu, v_ffn1_w_down, v_norm_mix, v_w_in_even, v_dn_conv_w, v_dn_a_log,
                           v_dn_dt_bias, v_dn_norm_g, v_fox_q_norm_g, v_fox_k_norm_g, v_fox_f_bias, v_w_out_even,
                           v_w_in_odd, v_w_out_odd, v_norm_ffn2, v_ffn2_w_gu, v_ffn2_w_down)))
    loss, dx, grads, delta, new_m, new_v = _step(x[0], loss_target[0], w, m, v)
    return (loss, dx[None], *[grads[n] for n in WEIGHTS], *[delta[n] for n in WEIGHTS],
            *[new_m[n] for n in WEIGHTS], *[new_v[n] for n in WEIGHTS])
```

```python
import functools
import math

import jax
import jax.numpy as jnp
from jax import lax
from jax.experimental import pallas as pl
from jax.experimental.pallas import tpu as pltpu

F32 = jnp.float32
BF16 = jnp.bfloat16
HI = lax.Precision.HIGH

HEAD_DIM = 128
N_DN_HEADS = 4
N_FOX_HEADS = 4
N_SB_HEADS = 8
D_DN = N_DN_HEADS * HEAD_DIM
D_FOX = N_FOX_HEADS * HEAD_DIM
CONV_WIDTH = 4
DN_CHUNK = 64
EPS = 1e-6
ATT_SCALE = HEAD_DIM ** -0.5
ADAM_LR, ADAM_B1, ADAM_B2, ADAM_EPS, ADAM_WD, ADAM_STEP = 0.001, 0.9, 0.999, 1e-08, 0.01, 10

V7X_VMEM_LIMIT = 56 * 1024 * 1024
LANES = 128
ATT_TQ = 512
ATT_TK = 256
ATT_SUB = ATT_TQ // ATT_TK

LANE_BETA, LANE_DECAY, LANE_FORGET = 0, 4, 8


def _cparams(*sem):
    return pltpu.CompilerParams(dimension_semantics=sem, vmem_limit_bytes=V7X_VMEM_LIMIT)


def _sigmoid(x):
    return 1.0 / (1.0 + jnp.exp(-x))


def _softplus(x):
    return jnp.maximum(x, 0.0) + jnp.log(1.0 + jnp.exp(-jnp.abs(x)))


def _silu_grad(y, sg):
    return sg * (1.0 + y * (1.0 - sg))


def _rowwise(fn, rows, bcast, outs, sums, *, tile, name):
    rows = [r if isinstance(r, tuple) else (r, r.shape[1], 0) for r in rows]
    s = rows[0][0].shape[0]
    assert s % tile == 0
    n_in, n_b, n_out, n_sum = len(rows), len(bcast), len(outs), len(sums)

    def body(*refs):
        ins = [r[...] for r in refs[:n_in + n_b]]
        res = fn(*ins)
        if not isinstance(res, (tuple, list)):
            res = (res,)
        out_refs = refs[n_in + n_b:n_in + n_b + n_out]
        sum_refs = refs[n_in + n_b + n_out:]
        for o_ref, val in zip(out_refs, res[:n_out]):
            o_ref[...] = val.astype(o_ref.dtype)
        if n_sum:
            @pl.when(pl.program_id(0) == 0)
            def _():
                for s_ref in sum_refs:
                    s_ref[...] = jnp.zeros_like(s_ref)
            for s_ref, val in zip(sum_refs, res[n_out:]):
                s_ref[...] += val

    in_specs = [pl.BlockSpec((tile, w), lambda i, cb=cb: (i, cb)) for _, w, cb in rows]
    in_specs += [pl.BlockSpec(b.shape, lambda i, nd=b.ndim: (0,) * nd) for b in bcast]
    out_specs = [pl.BlockSpec((tile, c), lambda i: (i, 0)) for c, _ in outs]
    out_specs += [pl.BlockSpec(sh, lambda i: (0, 0)) for sh in sums]
    out_shape = [jax.ShapeDtypeStruct((s, c), dt) for c, dt in outs]
    out_shape += [jax.ShapeDtypeStruct(sh, F32) for sh in sums]
    return pl.pallas_call(
        body, name=name, grid=(s // tile,), in_specs=in_specs, out_specs=out_specs, out_shape=out_shape,
        compiler_params=_cparams("arbitrary" if n_sum else "parallel"),
    )(*[r[0] for r in rows], *bcast)


def _rms_fwd(x, gain, name):
    def fn(xb, g):
        r = lax.rsqrt(jnp.mean(xb * xb, axis=-1, keepdims=True) + EPS)
        return (xb * r * g,)
    return _rowwise(fn, [x], [gain], [(x.shape[1], BF16)], [], tile=512, name=name)[0]


_DIMS = {"nn": (((1,), (0,)), ((), ())), "nt": (((1,), (1,)), ((), ())), "tn": (((0,), (0,)), ((), ()))}


def _dot(a, b, kind):
    return lax.dot_general(a.astype(BF16), b.astype(BF16), _DIMS[kind], preferred_element_type=F32)


def _dot32(a, b, kind="nn"):
    return lax.dot_general(a, b, _DIMS[kind], precision=HI, preferred_element_type=F32)


def _mm(a, b, kind, *, tm, tn, out_dtype, name, scale=None, residual=None, a_lead=(), b_lead=(),
        b_spec=None, n=None, into=None, after=None):
    ash, bsh = a.shape[len(a_lead):], b.shape[len(b_lead):]
    m = ash[1] if kind == "tn" else ash[0]
    k = ash[0] if kind == "tn" else ash[1]
    if b_spec is None:
        n = bsh[0] if kind == "nt" else bsh[1]
        assert k == (bsh[1] if kind == "nt" else bsh[0]), (ash, bsh, kind)
    assert m % tm == 0 and n % tn == 0, (m, tm, n, tn)
    la, lb = (None,) * len(a_lead), (None,) * len(b_lead)
    if kind == "tn":
        a_spec = pl.BlockSpec(la + (k, tm), lambda j, i: a_lead + (0, i))
    else:
        a_spec = pl.BlockSpec(la + (tm, k), lambda j, i: a_lead + (i, 0))
    if b_spec is None:
        if kind == "nt":
            b_spec = pl.BlockSpec(lb + (tn, k), lambda j, i: b_lead + (j, 0))
        else:
            b_spec = pl.BlockSpec(lb + (k, tn), lambda j, i: b_lead + (0, j))
    in_specs, args = [a_spec, b_spec], [a, b]
    if residual is not None:
        in_specs.append(pl.BlockSpec((tm, tn), lambda j, i: (i, j)))
        args.append(residual)
    aliases = {}
    if after is not None:
        in_specs.append(pl.BlockSpec(memory_space=pl.ANY))
        args.append(after)
    if into is not None:
        buf, layer = into
        in_specs.append(pl.BlockSpec(memory_space=pl.ANY))
        args.append(buf)
        aliases = {len(args) - 1: 0}
        out_spec = pl.BlockSpec((None, tm, tn), lambda j, i: (layer, i, j))
        out_shape = jax.ShapeDtypeStruct(buf.shape, buf.dtype)
    else:
        out_spec = pl.BlockSpec((tm, tn), lambda j, i: (i, j))
        out_shape = jax.ShapeDtypeStruct((m, n), out_dtype)

    def body(a_ref, b_ref, *rest):
        acc = _dot(a_ref[...], b_ref[...], kind)
        if scale is not None:
            acc = acc * scale
        if residual is not None:
            acc = acc + rest[0][...]
        rest[-1][...] = acc.astype(rest[-1].dtype)

    return pl.pallas_call(
        body, name=name, grid=(n // tn, m // tm), in_specs=in_specs, out_specs=out_spec, out_shape=out_shape,
        input_output_aliases=aliases, compiler_params=_cparams("parallel", "parallel"),
    )(*args)


FFN_TM = 1024


def _ffn_up(n, w_gu, layer, name):
    s, d = n.shape
    f = w_gu.shape[2] // 2
    tm, tn = FFN_TM, f // 2
    nj = f // tn

    def body(n_ref, wg_ref, wu_ref, gu_ref, a_ref):
        nv = n_ref[...]
        g = _dot(nv, wg_ref[...], "nn")
        u = _dot(nv, wu_ref[...], "nn")
        gu_ref[0] = g.astype(BF16)
        gu_ref[1] = u.astype(BF16)
        a_ref[...] = (g * _sigmoid(g) * u).astype(BF16)

    return pl.pallas_call(
        body, name=name, grid=(nj, s // tm),
        in_specs=[pl.BlockSpec((tm, d), lambda j, i: (i, 0)),
                  pl.BlockSpec((None, d, tn), lambda j, i: (layer, 0, j)),
                  pl.BlockSpec((None, d, tn), lambda j, i: (layer, 0, j + nj))],
        out_specs=[pl.BlockSpec((2, tm, tn), lambda j, i: (0, i, j)),
                   pl.BlockSpec((tm, tn), lambda j, i: (i, j))],
        out_shape=[jax.ShapeDtypeStruct((2, s, f), BF16), jax.ShapeDtypeStruct((s, f), BF16)],
        compiler_params=_cparams("parallel", "parallel"),
    )(n, w_gu, w_gu)


def _ffn_down_bwd(dxo, w_down, gu, layer, name, after=None):
    s, d = dxo.shape
    f = w_down.shape[1]
    tm, tn = FFN_TM, f // 2
    extra_specs, extra = ([ANY], [after]) if after is not None else ([], [])

    def body(dx_ref, w_ref, gu_ref, *rest):
        dgu_ref = rest[-1]
        da = 0.5 * _dot(dx_ref[...], w_ref[...], "nt")
        g = gu_ref[0].astype(F32)
        u = gu_ref[1].astype(F32)
        sg = _sigmoid(g)
        dgu_ref[0] = (da * u * _silu_grad(g, sg)).astype(BF16)
        dgu_ref[1] = (da * g * sg).astype(BF16)

    return pl.pallas_call(
        body, name=name, grid=(f // tn, s // tm),
        in_specs=[pl.BlockSpec((tm, d), lambda j, i: (i, 0)),
                  pl.BlockSpec((None, tn, d), lambda j, i: (layer, j, 0)),
                  pl.BlockSpec((2, tm, tn), lambda j, i: (0, i, j))] + extra_specs,
        out_specs=pl.BlockSpec((2, tm, tn), lambda j, i: (0, i, j)),
        out_shape=jax.ShapeDtypeStruct((2, s, f), BF16),
        compiler_params=_cparams("parallel", "parallel"),
    )(dxo, w_down, gu, *extra)


NORM_BWD_TM = 256


def _norm_bwd_after(terms, operands, specs, x, dres, gain, name):
    s, d = x.shape
    tm = NORM_BWD_TM
    n_op = len(operands)

    def body(*refs):
        x_ref, dres_ref, g_ref = refs[n_op:n_op + 3]
        dx_ref, dx16_ref, dgain_ref = refs[n_op + 3:]
        dn = None
        for a, b in terms(*refs[:n_op]):
            dn = _dot(a, b, "nt") if dn is None else dn + _dot(a, b, "nt")
        xb = x_ref[...]
        r = lax.rsqrt(jnp.mean(xb * xb, axis=-1, keepdims=True) + EPS)
        xh = xb * r
        dxh = dn * g_ref[...]
        dx = dres_ref[...] + r * (dxh - xh * jnp.mean(dxh * xh, axis=-1, keepdims=True))
        dx_ref[...] = dx
        dx16_ref[...] = dx.astype(BF16)

        @pl.when(pl.program_id(0) == 0)
        def _():
            dgain_ref[...] = jnp.zeros_like(dgain_ref)
        dgain_ref[...] += jnp.sum(dn * xh, axis=0, keepdims=True)

    rows = pl.BlockSpec((tm, d), lambda i: (i, 0))
    return pl.pallas_call(
        body, name=name, grid=(s // tm,),
        in_specs=list(specs) + [rows, rows, pl.BlockSpec((1, d), lambda i: (0, 0))],
        out_specs=[rows, rows, pl.BlockSpec((1, d), lambda i: (0, 0))],
        out_shape=[jax.ShapeDtypeStruct((s, d), F32), jax.ShapeDtypeStruct((s, d), BF16),
                   jax.ShapeDtypeStruct((1, d), F32)],
        compiler_params=_cparams("arbitrary"),
    )(*operands, x, dres, gain)


def _ffn_up_bwd(dgu, w_gu, layer, x, dres, gain, name):
    _, s, f = dgu.shape
    d = w_gu.shape[1]
    specs = [pl.BlockSpec((2, NORM_BWD_TM, f), lambda i: (0, i, 0)),
             pl.BlockSpec((None, d, f), lambda i: (layer, 0, 0)),
             pl.BlockSpec((None, d, f), lambda i: (layer, 0, 1))]
    terms = lambda dgu_ref, wg_ref, wu_ref: [(dgu_ref[0], wg_ref[...]), (dgu_ref[1], wu_ref[...])]
    return _norm_bwd_after(terms, [dgu, w_gu, w_gu], specs, x, dres, gain, name)


def _in_proj_bwd(dproj, w_in, j, x, dres, gain, name):
    k = dproj.shape[1]
    d = w_in.shape[1]
    specs = [pl.BlockSpec((NORM_BWD_TM, k), lambda i: (i, 0)), pl.BlockSpec((None, d, k), lambda i: (j, 0, 0))]
    terms = lambda a_ref, b_ref: [(a_ref[...], b_ref[...])]
    return _norm_bwd_after(terms, [dproj, w_in], specs, x, dres, gain, name)


def _ffn_fwd(x, gain, w_gu, w_down, layer, tag):
    n = _rms_fwd(x, gain, f"{tag}_norm")
    gu, a = _ffn_up(n, w_gu, layer, f"{tag}_up")
    x2 = _mm(a, w_down, "nn", tm=FFN_TM, tn=x.shape[1], out_dtype=F32, name=f"{tag}_down", scale=0.5, residual=x,
             b_lead=(layer,))
    return x2, (x, n, gu, a)


def _ffn_bwd(dxo, dxo16, saved, gain, w_gu, w_down, layer, tag, g_gu, g_down, after=None):
    x, n, gu, a = saved
    s, f = a.shape
    dgu = _ffn_down_bwd(dxo16, w_down, gu, layer, f"{tag}_down_bwd", after)
    g_down = _mm(a, dxo16, "tn", tm=256, tn=dxo16.shape[1], out_dtype=F32, name=f"{tag}_down_dw", scale=0.5,
                 into=(g_down, 0))
    tn = f // 2
    nj = f // tn
    g_gu = _mm(n, dgu, "tn", tm=512, tn=tn, out_dtype=F32, name=f"{tag}_up_dw", into=(g_gu, 0), n=2 * f,
               b_spec=pl.BlockSpec((None, s, tn), lambda j, i: (j // nj, 0, j % nj)))
    dx, dx16, dgain = _ffn_up_bwd(dgu, w_gu, layer, x, dxo, gain, f"{tag}_up_bwd")
    return dx, dx16, dgain, g_gu, g_down


def _lane_col(blk, lane):
    li = lax.broadcasted_iota(jnp.int32, blk.shape, 1)
    return jnp.sum(jnp.where(li == lane, blk, 0.0), axis=1, keepdims=True)


def _split_dot(x, tri):
    hi = x.astype(BF16)
    lo = (x - hi.astype(F32)).astype(BF16)
    return (lax.dot_general(hi, tri, _DIMS["nn"], preferred_element_type=F32)
            + lax.dot_general(lo, tri, _DIMS["nn"], preferred_element_type=F32))


class _Each:
    def __init__(self, vals):
        self.vals = list(vals)

    def _with(self, other, op):
        others = other.vals if isinstance(other, _Each) else [other] * len(self.vals)
        return _Each(op(a, b) for a, b in zip(self.vals, others))

    def __add__(self, other):
        return self._with(other, lambda a, b: a + b)

    def __sub__(self, other):
        return self._with(other, lambda a, b: a - b)

    def __mul__(self, other):
        return self._with(other, lambda a, b: a * b)

    def __neg__(self):
        return _Each(-a for a in self.vals)


def _each(fn, *args):
    n = max(len(a.vals) for a in args if isinstance(a, _Each))
    res = [fn(*xs) for xs in zip(*[a.vals if isinstance(a, _Each) else [a] * n for a in args])]
    if isinstance(res[0], tuple):
        return tuple(_Each(r) for r in zip(*res))
    return _Each(res)


def _keep(cond, x):
    return _each(lambda v: jnp.where(cond, v, 0.0), x)


def _rowsum(x):
    return _each(lambda v: jnp.sum(v, axis=1, keepdims=True), x)


ATT_HEADS = 2
ATT_WIDTH = ATT_HEADS * HEAD_DIM
_HEAD_COLS = [slice(h * HEAD_DIM, (h + 1) * HEAD_DIM) for h in range(ATT_HEADS)]


def _att_specs(n_heads, s):
    groups = n_heads // ATT_HEADS
    q_spec = pl.BlockSpec((ATT_TQ, ATT_WIDTH), lambda g, i: (i, g))
    k_spec = pl.BlockSpec((s, ATT_WIDTH), lambda g, i: (0, groups + g))
    v_spec = pl.BlockSpec((s, ATT_WIDTH), lambda g, i: (0, 2 * groups + g))
    return q_spec, k_spec, v_spec


def _heads_of(ref, rows=None):
    return _Each(ref[:, cs] if rows is None else ref[rows, cs] for cs in _HEAD_COLS)


def _dot_each(a, b, kind):
    return _each(lambda x, y: _dot(x, y, kind), a, b)


def _att_iotas():
    row = lax.broadcasted_iota(jnp.int32, (ATT_TQ, ATT_TK), 0)
    col = lax.broadcasted_iota(jnp.int32, (ATT_TQ, ATT_TK), 1)
    jr = lax.broadcasted_iota(jnp.int32, (ATT_TK, ATT_TK), 0)
    jc = lax.broadcasted_iota(jnp.int32, (ATT_TK, ATT_TK), 1)
    return row, col, jr, jc


def _sb_fwd(qkv, n_heads, name):
    s = qkv.shape[0]

    def body(q_ref, k_ref, v_ref, o16_ref, o32_ref):
        i = pl.program_id(1)
        q = _heads_of(q_ref)
        row, col, jr, jc = _att_iotas()
        later = (jr > jc).astype(BF16)

        def step(jb, carry, diagonal):
            c_sp, acc = (_Each(part) for part in carry)
            work = []
            for sub in reversed(range(ATT_SUB)):
                keys = pl.ds(pl.multiple_of(jb * ATT_TQ + sub * ATT_TK, ATT_TK), ATT_TK)
                z = _dot_each(q, _heads_of(k_ref, keys), "nt") * ATT_SCALE
                sp = _each(_softplus, z)
                before = (col + sub * ATT_TK) < row if diagonal else None
                spm = _keep(before, sp) if diagonal else sp
                work.append((keys, z - sp, spm, _each(lambda x: _dot(x, later, "nn"), spm), before))
            for keys, logsig, spm, within, before in work:
                a = _each(jnp.exp, logsig - (c_sp + within))
                if diagonal:
                    a = _keep(before, a)
                acc = acc + _each(_split_dot, a, _heads_of(v_ref, keys))
                c_sp = c_sp + _rowsum(spm)
            return tuple(c_sp.vals), tuple(acc.vals)

        zeros = lambda width: tuple(jnp.zeros((ATT_TQ, width), F32) for _ in range(ATT_HEADS))
        carry = step(i, (zeros(1), zeros(HEAD_DIM)), True)
        _, acc = lax.fori_loop(0, i, lambda it, cr: step(i - 1 - it, cr, False), carry)
        for cs, acc_h in zip(_HEAD_COLS, acc):
            o16_ref[:, cs] = acc_h.astype(BF16)
            o32_ref[:, cs] = acc_h

    q_spec, k_spec, v_spec = _att_specs(n_heads, s)
    o_spec = pl.BlockSpec((ATT_TQ, ATT_WIDTH), lambda g, i: (i, g))
    return pl.pallas_call(
        body, name=name, grid=(n_heads // ATT_HEADS, s // ATT_TQ), in_specs=[q_spec, k_spec, v_spec],
        out_specs=[o_spec, o_spec],
        out_shape=[jax.ShapeDtypeStruct((s, n_heads * HEAD_DIM), BF16),
                   jax.ShapeDtypeStruct((s, n_heads * HEAD_DIM), F32)],
        compiler_params=_cparams("parallel", "arbitrary"),
    )(qkv, qkv, qkv)


def _sb_bwd(qkv, o32, do, n_heads, name):
    s = qkv.shape[0]

    def body(q_ref, k_ref, v_ref, o_ref, do_ref, dq_ref, dk_ref, dv_ref):
        i = pl.program_id(1)

        @pl.when(i == 0)
        def _():
            dk_ref[...] = jnp.zeros_like(dk_ref)
            dv_ref[...] = jnp.zeros_like(dv_ref)

        q, do = _heads_of(q_ref), _heads_of(do_ref)
        total = _rowsum(_each(lambda a, b: a.astype(F32) * b, do, _heads_of(o_ref)))
        row, col, jr, jc = _att_iotas()
        later = (jr > jc).astype(BF16)
        not_before = (jr >= jc).astype(BF16)

        def step(jb, carry, diagonal):
            c_sp, c_e, dq = (_Each(part) for part in carry)
            work = []
            for sub in reversed(range(ATT_SUB)):
                keys = pl.ds(pl.multiple_of(jb * ATT_TQ + sub * ATT_TK, ATT_TK), ATT_TK)
                k = _heads_of(k_ref, keys)
                z = _dot_each(q, k, "nt") * ATT_SCALE
                sp = _each(_softplus, z)
                before = (col + sub * ATT_TK) < row if diagonal else None
                spm = _keep(before, sp) if diagonal else sp
                work.append((keys, k, _each(jnp.exp, z - sp), spm, _each(lambda x: _dot(x, later, "nn"), spm),
                             _dot_each(do, _heads_of(v_ref, keys), "nt"), before))
            for keys, k, sig, spm, within, da, before in work:
                a = sig * _each(lambda x: jnp.exp(-x), c_sp + within)
                if diagonal:
                    a = _keep(before, a)
                e = a * da
                left = total - c_e - _each(lambda x: _split_dot(x, not_before), e)
                dz = (e - (e + left) * sig) * ATT_SCALE
                if diagonal:
                    dz = _keep(before, dz)
                dk, dv = _dot_each(dz, q, "tn"), _dot_each(a, do, "tn")
                for cs, dk_h, dv_h in zip(_HEAD_COLS, dk.vals, dv.vals):
                    dk_ref[keys, cs] += dk_h
                    dv_ref[keys, cs] += dv_h
                dq = dq + _dot_each(dz, k, "nn")
                c_sp = c_sp + _rowsum(spm)
                c_e = c_e + _rowsum(e)
            return tuple(c_sp.vals), tuple(c_e.vals), tuple(dq.vals)

        zeros = lambda width: tuple(jnp.zeros((ATT_TQ, width), F32) for _ in range(ATT_HEADS))
        carry = step(i, (zeros(1), zeros(1), zeros(HEAD_DIM)), True)
        _, _, dq = lax.fori_loop(0, i, lambda it, cr: step(i - 1 - it, cr, False), carry)
        for cs, dq_h in zip(_HEAD_COLS, dq):
            dq_ref[:, cs] = dq_h.astype(BF16)

    q_spec, k_spec, v_spec = _att_specs(n_heads, s)
    blk = pl.BlockSpec((ATT_TQ, ATT_WIDTH), lambda g, i: (i, g))
    full = pl.BlockSpec((s, ATT_WIDTH), lambda g, i: (0, g))
    wide = (s, n_heads * HEAD_DIM)
    return pl.pallas_call(
        body, name=name, grid=(n_heads // ATT_HEADS, s // ATT_TQ), in_specs=[q_spec, k_spec, v_spec, blk, blk],
        out_specs=[blk, full, full],
        out_shape=[jax.ShapeDtypeStruct(wide, BF16), jax.ShapeDtypeStruct(wide, F32), jax.ShapeDtypeStruct(wide, F32)],
        compiler_params=_cparams("parallel", "arbitrary"),
    )(qkv, qkv, qkv, o32, do)


def _fox_logits(q, k, cq, ct_ref, keys):
    ck = _Each(ct_ref[h, :, keys] for h in range(ATT_HEADS))
    return _dot_each(q, k, "nt") * ATT_SCALE + (cq - ck)


def _fox_cq(c_ref, group):
    c = c_ref[...]
    return _Each(_lane_col(c, LANE_FORGET + group * ATT_HEADS + h) for h in range(ATT_HEADS))


def _fox_fwd(qkv, c, ct, name):
    s = qkv.shape[0]
    n_heads = N_FOX_HEADS

    def body(q_ref, k_ref, v_ref, c_ref, ct_ref, o_ref, lse_ref):
        g, i = pl.program_id(0), pl.program_id(1)
        q = _heads_of(q_ref)
        cq = _fox_cq(c_ref, g)
        row, col, _, _ = _att_iotas()

        def step(jb, carry, diagonal):
            m, l, acc = (_Each(part) for part in carry)
            work = []
            m_new = m
            for sub in range(ATT_SUB):
                keys = pl.ds(pl.multiple_of(jb * ATT_TQ + sub * ATT_TK, ATT_TK), ATT_TK)
                sc = _fox_logits(q, _heads_of(k_ref, keys), cq, ct_ref, keys)
                valid = (col + sub * ATT_TK) <= row if diagonal else None
                if diagonal:
                    sc = _each(lambda x: jnp.where(valid, x, -1e30), sc)
                m_new = _each(lambda a, x: jnp.maximum(a, jnp.max(x, axis=1, keepdims=True)), m_new, sc)
                work.append((keys, sc, valid))
            w = _each(jnp.exp, m - m_new)
            l, acc = l * w, acc * w
            for keys, sc, valid in work:
                p = _each(jnp.exp, sc - m_new)
                if diagonal:
                    p = _keep(valid, p)
                l = l + _rowsum(p)
                acc = acc + _each(_split_dot, p, _heads_of(v_ref, keys))
            return tuple(m_new.vals), tuple(l.vals), tuple(acc.vals)

        per_head = lambda width, value: tuple(jnp.full((ATT_TQ, width), value, F32) for _ in range(ATT_HEADS))
        init = (per_head(1, -1e30), per_head(1, 0.0), per_head(HEAD_DIM, 0.0))
        m, l, acc = lax.fori_loop(0, i, lambda jb, cr: step(jb, cr, False), step(i, init, True))
        for h, cs in enumerate(_HEAD_COLS):
            o_ref[:, cs] = acc[h] / l[h]
            lse_ref[h] = jnp.broadcast_to(m[h] + jnp.log(l[h]), (ATT_TQ, LANES))

    q_spec, k_spec, v_spec = _att_specs(n_heads, s)
    return pl.pallas_call(
        body, name=name, grid=(n_heads // ATT_HEADS, s // ATT_TQ),
        in_specs=[q_spec, k_spec, v_spec, pl.BlockSpec((ATT_TQ, LANES), lambda g, i: (i, 0)),
                  pl.BlockSpec((ATT_HEADS, 1, s), lambda g, i: (g, 0, 0))],
        out_specs=[pl.BlockSpec((ATT_TQ, ATT_WIDTH), lambda g, i: (i, g)),
                   pl.BlockSpec((ATT_HEADS, ATT_TQ, LANES), lambda g, i: (g, i, 0))],
        out_shape=[jax.ShapeDtypeStruct((s, n_heads * HEAD_DIM), F32),
                   jax.ShapeDtypeStruct((n_heads, s, LANES), F32)],
        compiler_params=_cparams("parallel", "arbitrary"),
    )(qkv, qkv, qkv, c, ct)


def _fox_bwd(qkv, c, ct, o, lse, do, name):
    s = qkv.shape[0]
    n_heads = N_FOX_HEADS

    def body(q_ref, k_ref, v_ref, c_ref, ct_ref, o_ref, lse_ref, do_ref, dq_ref, dk_ref, dv_ref, dct_ref):
        g, i = pl.program_id(0), pl.program_id(1)

        @pl.when(i == 0)
        def _():
            dk_ref[...] = jnp.zeros_like(dk_ref)
            dv_ref[...] = jnp.zeros_like(dv_ref)
            dct_ref[...] = jnp.zeros_like(dct_ref)

        q = _heads_of(q_ref)
        do16 = _each(lambda x: x.astype(BF16), _heads_of(do_ref))
        delta = _rowsum(_each(lambda a, b: a.astype(F32) * b, do16, _heads_of(o_ref)))
        lse_col = _Each(lse_ref[h, :, 0:1] for h in range(ATT_HEADS))
        cq = _fox_cq(c_ref, g)
        row, col, _, _ = _att_iotas()

        def step(jb, dq, diagonal):
            dq = _Each(dq)
            for sub in range(ATT_SUB):
                keys = pl.ds(pl.multiple_of(jb * ATT_TQ + sub * ATT_TK, ATT_TK), ATT_TK)
                k = _heads_of(k_ref, keys)
                sc = _fox_logits(q, k, cq, ct_ref, keys)
                if diagonal:
                    valid = (col + sub * ATT_TK) <= row
                    p = _keep(valid, _each(jnp.exp, _keep(valid, sc) - lse_col))
                else:
                    p = _each(jnp.exp, sc - lse_col)
                ds = p * (_dot_each(do16, _heads_of(v_ref, keys), "nt") - delta)
                dss = ds * ATT_SCALE
                dk, dv = _dot_each(dss, q, "tn"), _dot_each(p, do16, "tn")
                for h, cs in enumerate(_HEAD_COLS):
                    dct_ref[h, :, keys] -= jnp.sum(ds.vals[h], axis=0, keepdims=True)
                    dk_ref[keys, cs] += dk.vals[h]
                    dv_ref[keys, cs] += dv.vals[h]
                dq = dq + _dot_each(dss, k, "nn")
            return tuple(dq.vals)

        dq0 = step(i, tuple(jnp.zeros((ATT_TQ, HEAD_DIM), F32) for _ in range(ATT_HEADS)), True)
        dq = lax.fori_loop(0, i, lambda jb, dq: step(jb, dq, False), dq0)
        for cs, dq_h in zip(_HEAD_COLS, dq):
            dq_ref[:, cs] = dq_h

    q_spec, k_spec, v_spec = _att_specs(n_heads, s)
    blk = pl.BlockSpec((ATT_TQ, ATT_WIDTH), lambda g, i: (i, g))
    full = pl.BlockSpec((s, ATT_WIDTH), lambda g, i: (0, g))
    wide = jax.ShapeDtypeStruct((s, n_heads * HEAD_DIM), F32)
    return pl.pallas_call(
        body, name=name, grid=(n_heads // ATT_HEADS, s // ATT_TQ),
        in_specs=[q_spec, k_spec, v_spec, pl.BlockSpec((ATT_TQ, LANES), lambda g, i: (i, 0)),
                  pl.BlockSpec((ATT_HEADS, 1, s), lambda g, i: (g, 0, 0)), blk,
                  pl.BlockSpec((ATT_HEADS, ATT_TQ, LANES), lambda g, i: (g, i, 0)), blk],
        out_specs=[blk, full, full, pl.BlockSpec((ATT_HEADS, 1, s), lambda g, i: (g, 0, 0))],
        out_shape=[wide, wide, wide, jax.ShapeDtypeStruct((n_heads, 1, s), F32)],
        compiler_params=_cparams("parallel", "arbitrary"),
    )(qkv, qkv, qkv, c, ct, o, lse, do)


def _cumsum_rows(x, reverse, name):
    s = x.shape[0]
    nb = s // LANES

    def body(x_ref, o_ref):
        r = lax.broadcasted_iota(jnp.int32, (LANES, LANES), 0)
        c = lax.broadcasted_iota(jnp.int32, (LANES, LANES), 1)
        tri = ((r <= c) if reverse else (r >= c)).astype(F32)

        def step(it, carry):
            b = (nb - 1 - it) if reverse else it
            off = pl.multiple_of(b * LANES, LANES)
            blk = x_ref[pl.ds(off, LANES), :]
            o_ref[pl.ds(off, LANES), :] = _dot32(tri, blk) + carry
            return carry + jnp.sum(blk, axis=0, keepdims=True)

        lax.fori_loop(0, nb, step, jnp.zeros((1, LANES), F32))

    return pl.pallas_call(body, name=name, out_shape=jax.ShapeDtypeStruct(x.shape, F32),
                          compiler_params=pltpu.CompilerParams(vmem_limit_bytes=V7X_VMEM_LIMIT))(x)


def _dot32_each(a, b, kind="nn"):
    return _each(lambda x, y: _dot32(x, y, kind), a, b)


def _unit_lower_inverse(m, ri, ci):
    c = ri.shape[0]
    t = -_keep(ri // 2 == ci // 2, m) + jnp.where(ri == ci, 1.0, 0.0)
    b = 4
    while b <= c:
        off_diag = (ri // b == ci // b) & (ri % b >= b // 2) & (ci % b < b // 2)
        t = t - _dot32_each(_dot32_each(t, _keep(off_diag, m)), t)
        b *= 2
    return t


def _dn_gates(g, ri, ci):
    eye = ri == ci
    incl = ri >= ci
    g_row = jnp.sum(jnp.where(eye, g, 0.0), axis=0, keepdims=True)
    gc = jnp.sum(jnp.where(incl, g_row, 0.0), axis=1, keepdims=True)
    gc_row = jnp.sum(jnp.where(eye, gc, 0.0), axis=0, keepdims=True)
    dmat = jnp.where(incl, jnp.exp(jnp.where(incl, gc - gc_row, 0.0)), 0.0)
    gc_last = jnp.sum(g, axis=0, keepdims=True)
    return gc, dmat, jnp.exp(gc), jnp.exp(gc_last - gc), jnp.exp(gc_last)


def _dn_fwd(qkv, act, name):
    s = qkv.shape[0]
    c, d, nh = DN_CHUNK, HEAD_DIM, N_DN_HEADS
    nc = s // c

    def body(q_ref, k_ref, v_ref, act_ref, o_ref, s_ref, t_ref, state):
        @pl.when(pl.program_id(0) == 0)
        def _():
            state[...] = jnp.zeros_like(state)

        ri = lax.broadcasted_iota(jnp.int32, (c, c), 0)
        ci = lax.broadcasted_iota(jnp.int32, (c, c), 1)
        act = act_ref[...]
        heads = range(nh)
        cols = [slice(h * d, (h + 1) * d) for h in heads]
        q, k, v = (_Each(ref[:, cs] for cs in cols) for ref in (q_ref, k_ref, v_ref))
        beta = _Each(_lane_col(act, LANE_BETA + h) for h in heads)
        g = _Each(_lane_col(act, LANE_DECAY + h) for h in heads)
        _, dmat, e, r, gl = _each(lambda gh: _dn_gates(gh, ri, ci), g)
        s0 = _Each(state[h] for h in heads)
        kb = beta * k
        t = _unit_lower_inverse(_keep(ri > ci, _dot32_each(kb, k, "nt") * dmat), ri, ci)
        vn = _dot32_each(t, beta * v) - _dot32_each(_dot32_each(t, kb * e), s0)
        o = _dot32_each(q * e, s0) + _dot32_each(_dot32_each(q, k, "nt") * dmat, vn)
        s1 = s0 * gl + _dot32_each(k * r, vn, "tn")
        for h in heads:
            o_ref[:, cols[h]] = o.vals[h]
            state[h] = s1.vals[h]
            s_ref[h] = s0.vals[h]
            t_ref[h] = t.vals[h]

    wide = lambda part: pl.BlockSpec((c, nh * d), lambda n: (n, part))
    return pl.pallas_call(
        body, name=name, grid=(nc,),
        in_specs=[wide(0), wide(1), wide(2), pl.BlockSpec((c, LANES), lambda n: (n, 0))],
        out_specs=[wide(0), pl.BlockSpec((nh, None, d, d), lambda n: (0, n, 0, 0)),
                   pl.BlockSpec((nh, None, c, c), lambda n: (0, n, 0, 0))],
        out_shape=[jax.ShapeDtypeStruct((s, nh * d), F32), jax.ShapeDtypeStruct((nh, nc, d, d), F32),
                   jax.ShapeDtypeStruct((nh, nc, c, c), F32)],
        scratch_shapes=[pltpu.VMEM((nh, d, d), F32)],
        compiler_params=_cparams("arbitrary"),
    )(qkv, qkv, qkv, act)


def _dn_bwd(qkv, act, states, tinv, do, name):
    s = qkv.shape[0]
    c, d, nh = DN_CHUNK, HEAD_DIM, N_DN_HEADS
    nc = s // c

    def chunk_bwd(q, k, v, do, beta, g, s0, t, ds_out):
        ri = lax.broadcasted_iota(jnp.int32, (c, c), 0)
        ci = lax.broadcasted_iota(jnp.int32, (c, c), 1)
        eye, incl, strict = ri == ci, ri >= ci, ri > ci
        gc, dmat, e, r, gl = _each(lambda gh: _dn_gates(gh, ri, ci), g)
        dot = _dot32_each
        rowsum = lambda x: _each(lambda a: jnp.sum(a, axis=1, keepdims=True), x)
        colsum = lambda x: _each(lambda a: jnp.sum(a, axis=0, keepdims=True), x)
        total = lambda x: colsum(rowsum(x))
        to_col = lambda row: rowsum(_keep(eye, row))
        to_row = lambda colv: colsum(_keep(eye, colv))

        kb, vb = beta * k, beta * v
        kbe = kb * e
        u, w = dot(t, vb), dot(t, kbe)
        vn = u - dot(w, s0)
        qk = dot(q, k, "nt")
        p = qk * dmat
        gram = dot(k, k, "nt")
        kr, qe = k * r, q * e

        d_kr = dot(vn, ds_out, "nt")
        dvn = dot(kr, ds_out)
        dgl = total(s0 * ds_out)
        ds_in = ds_out * gl
        dk = d_kr * r
        dr = rowsum(d_kr * k)
        d_qe = dot(do, s0, "nt")
        ds_in = ds_in + dot(qe, do, "tn")
        dp = _keep(incl, dot(do, vn, "nt"))
        dvn = dvn + dot(p, do, "tn")
        dq = d_qe * e
        de = rowsum(d_qe * q)
        dqk = dp * dmat
        dq = dq + dot(dqk, k)
        dk = dk + dot(dqk, q, "tn")
        dd = dp * qk
        dw = -dot(dvn, s0, "nt")
        ds_in = ds_in - dot(w, dvn, "tn")
        dvb = dot(t, dvn, "tn")
        dkbe = dot(t, dw, "tn")
        dm = -_keep(strict, dot(dvb, u, "nt") + dot(dkbe, w, "nt"))
        dbeta = rowsum(dm * gram * dmat)
        dgram = dm * beta * dmat
        dd = dd + dm * beta * gram
        dk = dk + dot(dgram, k) + dot(dgram, k, "tn")
        dkb = dkbe * e
        de = de + rowsum(dkbe * kb)
        dk = dk + beta * dkb
        dbeta = dbeta + rowsum(dkb * k) + rowsum(dvb * v)
        dv = beta * dvb
        wd = dd * dmat
        dgc = rowsum(wd) - to_col(colsum(wd)) + de * e - dr * r
        dgc_last = total(dr * r) + dgl * gl
        dgc = dgc + _keep(ri[:, 0:1] == c - 1, dgc_last)
        dg = rowsum(_keep(ri <= ci, to_row(dgc)))
        return dq, dk, dv, dbeta, dg, ds_in

    def body(q_ref, k_ref, v_ref, act_ref, s_ref, t_ref, do_ref, dq_ref, dk_ref, dv_ref, dact_ref, dstate):
        @pl.when(pl.program_id(0) == 0)
        def _():
            dstate[...] = jnp.zeros_like(dstate)

        act = act_ref[...]
        heads = range(nh)
        cols = [slice(h * d, (h + 1) * d) for h in heads]
        q, k, v, do = (_Each(ref[:, cs] for cs in cols) for ref in (q_ref, k_ref, v_ref, do_ref))
        dq, dk, dv, dbeta, dg, ds_in = chunk_bwd(
            q, k, v, do, _Each(_lane_col(act, LANE_BETA + h) for h in heads),
            _Each(_lane_col(act, LANE_DECAY + h) for h in heads), _Each(s_ref[h] for h in heads),
            _Each(t_ref[h] for h in heads), _Each(dstate[h] for h in heads))
        lane = lax.broadcasted_iota(jnp.int32, (c, LANES), 1)
        dact = jnp.zeros((c, LANES), F32)
        for h in heads:
            dstate[h] = ds_in.vals[h]
            dq_ref[:, cols[h]], dk_ref[:, cols[h]], dv_ref[:, cols[h]] = dq.vals[h], dk.vals[h], dv.vals[h]
            dact = (dact + jnp.where(lane == LANE_BETA + h, dbeta.vals[h], 0.0)
                    + jnp.where(lane == LANE_DECAY + h, dg.vals[h], 0.0))
        dact_ref[...] = dact

    part = lambda p: pl.BlockSpec((c, nh * d), lambda n: (nc - 1 - n, p))
    per = lambda a, b: pl.BlockSpec((nh, None, a, b), lambda n: (0, nc - 1 - n, 0, 0))
    wide = jax.ShapeDtypeStruct((s, nh * d), F32)
    act_spec = pl.BlockSpec((c, LANES), lambda n: (nc - 1 - n, 0))
    return pl.pallas_call(
        body, name=name, grid=(nc,),
        in_specs=[part(0), part(1), part(2), act_spec, per(d, d), per(c, c), part(0)],
        out_specs=[part(0), part(0), part(0), act_spec],
        out_shape=[wide, wide, wide, jax.ShapeDtypeStruct((s, LANES), F32)],
        scratch_shapes=[pltpu.VMEM((nh, d, d), F32)],
        compiler_params=_cparams("arbitrary"),
    )(qkv, qkv, qkv, act, states, tinv, do)


EVEN_DN_QKV, EVEN_FOX_QKV, EVEN_DN_GATE, EVEN_FOX_GATE, EVEN_NARROW = 0, 1536, 3072, 3584, 4096
EVEN_WIDTH = 4224
CONV_TILE = 256
CONV_HALO = 8


def _conv_fwd(proj, w, name):
    s = proj.shape[0]
    t, cw = CONV_TILE, 3 * D_DN

    def body(cur_ref, prev_ref, w_ref, y_ref, xs):
        i = pl.program_id(0)
        xs[0:CONV_HALO, :] = jnp.where(i > 0, prev_ref[...], 0.0)
        xs[CONV_HALO:, :] = cur_ref[...]
        y = jnp.zeros((t, cw), F32)
        for tap in range(CONV_WIDTH):
            y = y + w_ref[tap:tap + 1, :] * xs[pl.ds(CONV_HALO - CONV_WIDTH + 1 + tap, t), :]
        y_ref[...] = y

    per = t // CONV_HALO
    return pl.pallas_call(
        body, name=name, grid=(s // t,),
        in_specs=[pl.BlockSpec((t, cw), lambda i: (i, 0)),
                  pl.BlockSpec((CONV_HALO, cw), lambda i: (jnp.maximum(i * per - 1, 0), 0)),
                  pl.BlockSpec((CONV_WIDTH, cw), lambda i: (0, 0))],
        out_specs=pl.BlockSpec((t, cw), lambda i: (i, 0)),
        out_shape=jax.ShapeDtypeStruct((s, cw), F32),
        scratch_shapes=[pltpu.VMEM((t + CONV_HALO, cw), F32)],
        compiler_params=_cparams("parallel"),
    )(proj, proj, w)


def _conv_bwd(proj, w, dy, name):
    s = proj.shape[0]
    t, cw = CONV_TILE, 3 * D_DN
    nt = s // t

    def body(cur_ref, prev_ref, w_ref, dy_ref, nxt_ref, dx_ref, dw_ref, xs, dys):
        i = pl.program_id(0)

        @pl.when(i == 0)
        def _():
            dw_ref[...] = jnp.zeros_like(dw_ref)

        xs[0:CONV_HALO, :] = jnp.where(i > 0, prev_ref[...], 0.0)
        xs[CONV_HALO:, :] = cur_ref[...]
        dys[0:t, :] = dy_ref[...]
        dys[t:, :] = jnp.where(i < nt - 1, nxt_ref[...], 0.0)
        dy = dy_ref[...]
        dx = jnp.zeros((t, cw), F32)
        for tap in range(CONV_WIDTH):
            dx = dx + w_ref[tap:tap + 1, :] * dys[pl.ds(CONV_WIDTH - 1 - tap, t), :]
            dw_ref[tap:tap + 1, :] += jnp.sum(dy * xs[pl.ds(CONV_HALO - CONV_WIDTH + 1 + tap, t), :], axis=0,
                                              keepdims=True)
        dx_ref[...] = dx.astype(BF16)

    per = t // CONV_HALO
    last = s // CONV_HALO - 1
    return pl.pallas_call(
        body, name=name, grid=(nt,),
        in_specs=[pl.BlockSpec((t, cw), lambda i: (i, 0)),
                  pl.BlockSpec((CONV_HALO, cw), lambda i: (jnp.maximum(i * per - 1, 0), 0)),
                  pl.BlockSpec((CONV_WIDTH, cw), lambda i: (0, 0)),
                  pl.BlockSpec((t, cw), lambda i: (i, 0)),
                  pl.BlockSpec((CONV_HALO, cw), lambda i: (jnp.minimum((i + 1) * per, last), 0))],
        out_specs=[pl.BlockSpec((t, cw), lambda i: (i, 0)), pl.BlockSpec((CONV_WIDTH, cw), lambda i: (0, 0))],
        out_shape=[jax.ShapeDtypeStruct((s, cw), BF16), jax.ShapeDtypeStruct((CONV_WIDTH, cw), F32)],
        scratch_shapes=[pltpu.VMEM((t + CONV_HALO, cw), F32), pltpu.VMEM((t + CONV_HALO, cw), F32)],
        compiler_params=_cparams("arbitrary"),
    )(proj, proj, w, dy, dy)


def _heads(x, n):
    return [x[:, HEAD_DIM * h:HEAD_DIM * (h + 1)] for h in range(n)]


def _dn_pre_fwd(y, name):
    def fn(yb):
        cs = yb * _sigmoid(yb)
        out = []
        for idx, xh in enumerate(_heads(cs, 3 * N_DN_HEADS)):
            if idx < 2 * N_DN_HEADS:
                xh = xh * lax.rsqrt(jnp.sum(xh * xh, axis=-1, keepdims=True) + EPS)
                if idx < N_DN_HEADS:
                    xh = xh * ATT_SCALE
            out.append(xh)
        return (jnp.concatenate(out, axis=1),)
    return _rowwise(fn, [y], [], [(y.shape[1], F32)], [], tile=256, name=name)[0]


def _dn_pre_bwd(y, dq, dk, dv, name):
    def fn(yb, dqb, dkb, dvb):
        sg = _sigmoid(yb)
        cs = yb * sg
        dout = _heads(dqb, N_DN_HEADS) + _heads(dkb, N_DN_HEADS) + _heads(dvb, N_DN_HEADS)
        dcs = []
        for idx, (xh, dh) in enumerate(zip(_heads(cs, 3 * N_DN_HEADS), dout)):
            if idx < 2 * N_DN_HEADS:
                if idx < N_DN_HEADS:
                    dh = dh * ATT_SCALE
                r = lax.rsqrt(jnp.sum(xh * xh, axis=-1, keepdims=True) + EPS)
                xhat = xh * r
                dh = r * (dh - xhat * jnp.sum(xhat * dh, axis=-1, keepdims=True))
            dcs.append(dh)
        return (jnp.concatenate(dcs, axis=1) * _silu_grad(yb, sg),)
    return _rowwise(fn, [y, dq, dk, dv], [], [(y.shape[1], F32)], [], tile=256, name=name)[0]


def _narrow_params(a_log, dt_bias, f_bias):
    lanes = lambda a, first: jnp.pad(a.reshape(1, -1), ((0, 0), (first, LANES - first - a.shape[0])))
    return jnp.concatenate([lanes(a_log, LANE_DECAY), lanes(dt_bias, LANE_DECAY), lanes(f_bias, LANE_FORGET),
                            jnp.zeros((5, LANES), F32)], axis=0)


def _narrow_masks(shape):
    lane = lax.broadcasted_iota(jnp.int32, shape, 1)
    is_beta = lane < LANE_DECAY
    is_decay = (lane >= LANE_DECAY) & (lane < LANE_FORGET)
    is_forget = (lane >= LANE_FORGET) & (lane < LANE_FORGET + N_FOX_HEADS)
    return is_beta, is_decay, is_forget


def _narrow_fwd(proj, params, name):
    def fn(sm, pk):
        is_beta, is_decay, is_forget = _narrow_masks(sm.shape)
        g = -jnp.exp(pk[0:1, :]) * _softplus(sm + pk[1:2, :])
        logf = -_softplus(-(sm + pk[2:3, :]))
        return (jnp.where(is_beta, _sigmoid(sm), jnp.where(is_decay, g, jnp.where(is_forget, logf, 0.0))),)
    return _rowwise(fn, [(proj, LANES, EVEN_NARROW // LANES)], [params], [(LANES, F32)], [], tile=512, name=name)[0]


def _narrow_bwd(proj, params, act, dact, dlogf, name):
    def fn(sm, ab, da, dl, pk):
        is_beta, is_decay, is_forget = _narrow_masks(sm.shape)
        db = jnp.where(is_forget, dl, da)
        d_beta = db * ab * (1.0 - ab)
        d_decay = db * (-jnp.exp(pk[0:1, :])) * _sigmoid(sm + pk[1:2, :])
        d_forget = db * _sigmoid(-(sm + pk[2:3, :]))
        dsm = jnp.where(is_beta, d_beta, jnp.where(is_decay, d_decay, jnp.where(is_forget, d_forget, 0.0)))
        col = lambda x: jnp.sum(x, axis=0, keepdims=True)
        return (dsm, col(jnp.where(is_decay, db * ab, 0.0)), col(jnp.where(is_decay, dsm, 0.0)),
                col(jnp.where(is_forget, dsm, 0.0)))
    return _rowwise(fn, [(proj, LANES, EVEN_NARROW // LANES), act, dact, dlogf], [params], [(LANES, BF16)],
                    [(1, LANES)] * 3, tile=512, name=name)


def _head_rms(xh):
    r = lax.rsqrt(jnp.mean(xh * xh, axis=-1, keepdims=True) + EPS)
    return xh * r, r


def _fox_pre_fwd(proj, qg, kg, name):
    def fn(pf, qgb, kgb):
        out = []
        for idx, xh in enumerate(_heads(pf, 3 * N_FOX_HEADS)):
            if idx < 2 * N_FOX_HEADS:
                xh = _head_rms(xh)[0] * (qgb if idx < N_FOX_HEADS else kgb)
            out.append(xh)
        return (jnp.concatenate(out, axis=1),)
    return _rowwise(fn, [(proj, 3 * D_FOX, EVEN_FOX_QKV // (3 * D_FOX))], [qg, kg], [(3 * D_FOX, BF16)], [],
                    tile=256, name=name)[0]


def _fox_pre_bwd(proj, qg, kg, dq, dk, dv, name):
    def fn(pf, dqb, dkb, dvb, qgb, kgb):
        dout = _heads(dqb, N_FOX_HEADS) + _heads(dkb, N_FOX_HEADS) + _heads(dvb, N_FOX_HEADS)
        dg = [jnp.zeros((1, HEAD_DIM), F32), jnp.zeros((1, HEAD_DIM), F32)]
        dx = []
        for idx, (xh, dh) in enumerate(zip(_heads(pf, 3 * N_FOX_HEADS), dout)):
            if idx < 2 * N_FOX_HEADS:
                which = 0 if idx < N_FOX_HEADS else 1
                xhat, r = _head_rms(xh)
                dg[which] = dg[which] + jnp.sum(dh * xhat, axis=0, keepdims=True)
                dxh = dh * (qgb if which == 0 else kgb)
                dh = r * (dxh - xhat * jnp.mean(dxh * xhat, axis=-1, keepdims=True))
            dx.append(dh)
        return jnp.concatenate(dx, axis=1), dg[0], dg[1]
    return _rowwise(fn, [(proj, 3 * D_FOX, EVEN_FOX_QKV // (3 * D_FOX)), dq, dk, dv], [qg, kg],
                    [(3 * D_FOX, BF16)], [(1, HEAD_DIM)] * 2, tile=256, name=name)


def _mix_gate_fwd(proj, o_dn, o_fox, ng, name):
    def fn(gd, gf, od, of, ngb):
        dn = [_head_rms(xh)[0] * ngb for xh in _heads(od, N_DN_HEADS)]
        return (jnp.concatenate([jnp.concatenate(dn, axis=1) * gd * _sigmoid(gd), of * _sigmoid(gf)], axis=1),)
    return _rowwise(fn, [(proj, D_DN, EVEN_DN_GATE // D_DN), (proj, D_FOX, EVEN_FOX_GATE // D_FOX), o_dn, o_fox],
                    [ng], [(D_DN + D_FOX, BF16)], [], tile=256, name=name)[0]


def _mix_gate_bwd(proj, o_dn, o_fox, ng, dom, name):
    def fn(gd, gf, od, of, dm, ngb):
        d_dn, d_fox = dm[:, :D_DN], dm[:, D_DN:]
        sgd, sgf = _sigmoid(gd), _sigmoid(gf)
        don = d_dn * gd * sgd
        dng = jnp.zeros((1, HEAD_DIM), F32)
        dod, normed = [], []
        for xh, dh in zip(_heads(od, N_DN_HEADS), _heads(don, N_DN_HEADS)):
            xhat, r = _head_rms(xh)
            dng = dng + jnp.sum(dh * xhat, axis=0, keepdims=True)
            dxh = dh * ngb
            dod.append(r * (dxh - xhat * jnp.mean(dxh * xhat, axis=-1, keepdims=True)))
            normed.append(xhat * ngb)
        d_gd = d_dn * jnp.concatenate(normed, axis=1) * _silu_grad(gd, sgd)
        d_gf = d_fox * of * sgf * (1.0 - sgf)
        return jnp.concatenate(dod, axis=1), d_fox * sgf, d_gd, d_gf, dng
    return _rowwise(fn, [(proj, D_DN, EVEN_DN_GATE // D_DN), (proj, D_FOX, EVEN_FOX_GATE // D_FOX), o_dn, o_fox, dom],
                    [ng], [(D_DN, F32), (D_FOX, F32), (D_DN, BF16), (D_FOX, BF16)], [(1, HEAD_DIM)], tile=256,
                    name=name)


def _loss_grad(y, target, name):
    d = y.shape[1]

    def fn(yb, tb):
        diff = yb - tb
        part = jnp.sum(jnp.sum(diff * diff, axis=1, keepdims=True), axis=0, keepdims=True) * (0.5 / d)
        g = diff * (1.0 / d)
        return g, g, part
    return _rowwise(fn, [y, target], [], [(d, F32), (d, BF16)], [(1, 1)], tile=512, name=name)


_REF_EVEN = {"dn_qkv": (0, 1536), "dn_gate": (1536, 2048), "dn_ba": (2048, 2056), "fox_qkv": (2056, 3592),
             "fox_gate": (3592, 4104), "f_pre": (4104, 4108)}
D_IN_EVEN = 4108


def _even_to_kernel_layout(w):
    cut = lambda name: w[..., _REF_EVEN[name][0]:_REF_EVEN[name][1]]
    pad = jnp.zeros(w.shape[:-1] + (EVEN_WIDTH - EVEN_NARROW - 12,), w.dtype)
    return jnp.concatenate([cut("dn_qkv"), cut("fox_qkv"), cut("dn_gate"), cut("fox_gate"), cut("dn_ba"),
                            cut("f_pre"), pad], axis=-1)


def _even_from_kernel_layout(g):
    return jnp.concatenate([g[..., EVEN_DN_QKV:EVEN_FOX_QKV], g[..., EVEN_DN_GATE:EVEN_FOX_GATE],
                            g[..., EVEN_NARROW:EVEN_NARROW + 8], g[..., EVEN_FOX_QKV:EVEN_DN_GATE],
                            g[..., EVEN_FOX_GATE:EVEN_NARROW], g[..., EVEN_NARROW + 8:EVEN_NARROW + 12]], axis=-1)


EVEN_QUARTER = 1027
EVEN_QUARTER_PAD = 1152


def _even_grad_quarters(g):
    g = _even_from_kernel_layout(g)
    pad = [(0, 0)] * (g.ndim - 1) + [(0, EVEN_QUARTER_PAD - EVEN_QUARTER)]
    return jnp.concatenate([jnp.pad(g[..., q * EVEN_QUARTER:(q + 1) * EVEN_QUARTER], pad) for q in range(4)], axis=-1)


def _forget_rows(c):
    return c[:, LANE_FORGET:LANE_FORGET + N_FOX_HEADS].T.reshape(N_FOX_HEADS, 1, c.shape[0])


def _forget_lanes(rows):
    s = rows.shape[2]
    return jnp.pad(rows.reshape(-1, s).T, ((0, 0), (LANE_FORGET, LANES - LANE_FORGET - N_FOX_HEADS)))


def _even_fwd(x, gain, w_in, w_out, j, p, tag):
    h = _rms_fwd(x, gain, f"{tag}_norm")
    proj = _mm(h, w_in, "nn", tm=512, tn=EVEN_WIDTH // 3, out_dtype=F32, name=f"{tag}_in", b_lead=(j,))
    y = _conv_fwd(proj, p["conv_w"], f"{tag}_conv")
    dn_qkv = _dn_pre_fwd(y, f"{tag}_dn_pre")
    act = _narrow_fwd(proj, p["narrow"], f"{tag}_narrow")
    o_dn, states, tinv = _dn_fwd(dn_qkv, act, f"{tag}_delta")
    fox_qkv = _fox_pre_fwd(proj, p["q_g"], p["k_g"], f"{tag}_fox_pre")
    c = _cumsum_rows(act, False, f"{tag}_cumsum")
    ct = _forget_rows(c)
    o_fox, lse = _fox_fwd(fox_qkv, c, ct, f"{tag}_fox")
    om = _mix_gate_fwd(proj, o_dn, o_fox, p["dn_norm_g"], f"{tag}_gate")
    x2 = _mm(om, w_out, "nn", tm=512, tn=x.shape[1], out_dtype=F32, name=f"{tag}_out", residual=x, b_lead=(j,))
    return x2, (x, h, proj, y, dn_qkv, act, states, tinv, o_dn, fox_qkv, c, ct, o_fox, lse, om)


def _even_bwd(dxo, dxo16, saved, gain, w_in, w_out, j, p, tag, g_in, g_out, after=None):
    x, h, proj, y, dn_qkv, act, states, tinv, o_dn, fox_qkv, c, ct, o_fox, lse, om = saved
    d = x.shape[1]
    dom = _mm(dxo16, w_out, "nt", tm=512, tn=d, out_dtype=F32, name=f"{tag}_out_bwd", b_lead=(j,), after=after)
    g_out = _mm(om, dxo16, "tn", tm=512, tn=d, out_dtype=F32, name=f"{tag}_out_dw", into=(g_out, 0))
    d_odn, d_ofox, d_gd, d_gf, d_ng = _mix_gate_bwd(proj, o_dn, o_fox, p["dn_norm_g"], dom, f"{tag}_gate_bwd")
    dq, dk, dv, dct = _fox_bwd(fox_qkv, c, ct, o_fox, lse, d_ofox, f"{tag}_fox_bwd")
    d_fox_qkv, d_qg, d_kg = _fox_pre_bwd(proj, p["q_g"], p["k_g"], dq, dk, dv, f"{tag}_fox_pre_bwd")
    dlogf = _cumsum_rows(_forget_lanes(dct), True, f"{tag}_cumsum_bwd")
    dq, dk, dv, dact = _dn_bwd(dn_qkv, act, states, tinv, d_odn, f"{tag}_delta_bwd")
    dy = _dn_pre_bwd(y, dq, dk, dv, f"{tag}_dn_pre_bwd")
    d_dn_qkv, d_conv = _conv_bwd(proj, p["conv_w"], dy, f"{tag}_conv_bwd")
    d_narrow, s_alog, s_dt, s_fb = _narrow_bwd(proj, p["narrow"], act, dact, dlogf, f"{tag}_narrow_bwd")
    dproj = jnp.concatenate([d_dn_qkv, d_fox_qkv, d_gd, d_gf, d_narrow], axis=1)
    g_in = _mm(h, dproj, "tn", tm=512, tn=EVEN_WIDTH // 3, out_dtype=F32, name=f"{tag}_in_dw", into=(g_in, 0))
    dx, dx16, d_gain = _in_proj_bwd(dproj, w_in, j, x, dxo, gain, f"{tag}_in_bwd")
    small = {"conv_w": d_conv, "a_log": s_alog, "dt_bias": s_dt, "f_bias": s_fb, "dn_norm_g": d_ng, "q_g": d_qg,
             "k_g": d_kg}
    return dx, dx16, d_gain, small, g_in, g_out


def _odd_fwd(x, gain, w_in, w_out, j, tag):
    h = _rms_fwd(x, gain, f"{tag}_norm")
    qkv = _mm(h, w_in, "nn", tm=512, tn=w_in.shape[2] // 2, out_dtype=BF16, name=f"{tag}_in", b_lead=(j,))
    o16, o32 = _sb_fwd(qkv, N_SB_HEADS, f"{tag}_sb")
    x2 = _mm(o16, w_out, "nn", tm=512, tn=x.shape[1], out_dtype=F32, name=f"{tag}_out", residual=x, b_lead=(j,))
    return x2, (x, h, qkv, o16, o32)


def _odd_bwd(dxo, dxo16, saved, gain, w_in, w_out, j, tag, g_in, g_out, after=None):
    x, h, qkv, o16, o32 = saved
    d = x.shape[1]
    do = _mm(dxo16, w_out, "nt", tm=512, tn=d, out_dtype=BF16, name=f"{tag}_out_bwd", b_lead=(j,), after=after)
    g_out = _mm(o16, dxo16, "tn", tm=512, tn=d, out_dtype=F32, name=f"{tag}_out_dw", into=(g_out, 0))
    dq, dk, dv = _sb_bwd(qkv, o32, do, N_SB_HEADS, f"{tag}_sb_bwd")
    dqkv = jnp.concatenate([dq, dk.astype(BF16), dv.astype(BF16)], axis=1)
    g_in = _mm(h, dqkv, "tn", tm=512, tn=w_in.shape[2] // 2, out_dtype=F32, name=f"{tag}_in_dw", into=(g_in, 0))
    dx, dx16, d_gain = _in_proj_bwd(dqkv, w_in, j, x, dxo, gain, f"{tag}_in_bwd")
    return dx, dx16, d_gain, g_in, g_out


def _forward_backward(x, target, w, first, rest_after, token, on_reduced):
    depth = w["norm_ffn1"].shape[0]
    row = lambda a, l: a[l][None]
    rest = {}

    def mats(names, j):
        if j == 0 and names[0] in first:
            return [first[name] for name in names] + [0]
        return [rest[name] for name in names] + [j - (1 if names[0] in first else 0)]

    def even_small(j):
        return {"conv_w": w["dn_conv_w"][j], "narrow": _narrow_params(w["dn_a_log"][j], w["dn_dt_bias"][j],
                                                                     w["fox_f_bias"][j]),
                "dn_norm_g": row(w["dn_norm_g"], j), "q_g": row(w["fox_q_norm_g"], j),
                "k_g": row(w["fox_k_norm_g"], j)}

    saved = []
    for l in range(depth):
        if l == 1:
            rest.update(rest_after(x))
        gain = row(w["norm_ffn1"], l) + token[0:1, 0:1] if l == 0 else row(w["norm_ffn1"], l)
        x, s1 = _ffn_fwd(x, gain, *mats(("ffn1_w_gu", "ffn1_w_down"), l), "ffn1")
        if l % 2 == 0:
            x, s2 = _even_fwd(x, row(w["norm_mix"], l), *mats(("w_in_even", "w_out_even"), l // 2),
                              even_small(l // 2), "even")
        else:
            x, s2 = _odd_fwd(x, row(w["norm_mix"], l), *mats(("w_in_odd", "w_out_odd"), l // 2), "odd")
        x, s3 = _ffn_fwd(x, row(w["norm_ffn2"], l), *mats(("ffn2_w_gu", "ffn2_w_down"), l), "ffn2")
        saved.append((s1, s2, s3))

    dx, dx16, loss = _loss_grad(x, target, "loss")

    kind_of = dict(BIG)
    d_norm = {k: [None] * depth for k in ("norm_ffn1", "norm_mix", "norm_ffn2")}
    d_even = [None] * ((depth + 1) // 2)
    to_sibling, between_chips, token = None, None, None
    for l in reversed(range(depth)):
        s1, s2, s3 = saved[l]
        mixer = ("w_in_even", "w_out_even") if l % 2 == 0 else ("w_in_odd", "w_out_odd")
        names = ["ffn1_w_gu", "ffn1_w_down", *mixer, "ffn2_w_gu", "ffn2_w_down"]
        g = {name: lax.empty((1,) + rest[name].shape[1:], F32) for name in names}
        dx, dx16, d_norm["norm_ffn2"][l], g["ffn2_w_gu"], g["ffn2_w_down"] = _ffn_bwd(
            dx, dx16, s3, row(w["norm_ffn2"], l), *mats(("ffn2_w_gu", "ffn2_w_down"), l), "ffn2", g["ffn2_w_gu"],
            g["ffn2_w_down"], after=token)
        if to_sibling is not None:
            between_chips, token = _reduce_middle(to_sibling, dx)
        if l % 2 == 0:
            dx, dx16, d_norm["norm_mix"][l], d_even[l // 2], g["w_in_even"], g["w_out_even"] = _even_bwd(
                dx, dx16, s2, row(w["norm_mix"], l), *mats(("w_in_even", "w_out_even"), l // 2), even_small(l // 2),
                "even", g["w_in_even"], g["w_out_even"], after=token)
            g["w_in_even"] = _even_grad_quarters(g["w_in_even"])
        else:
            dx, dx16, d_norm["norm_mix"][l], g["w_in_odd"], g["w_out_odd"] = _odd_bwd(
                dx, dx16, s2, row(w["norm_mix"], l), *mats(("w_in_odd", "w_out_odd"), l // 2), "odd", g["w_in_odd"],
                g["w_out_odd"], after=token)
        dx, dx16, d_norm["norm_ffn1"][l], g["ffn1_w_gu"], g["ffn1_w_down"] = _ffn_bwd(
            dx, dx16, s1, row(w["norm_ffn1"], l), *mats(("ffn1_w_gu", "ffn1_w_down"), l), "ffn1", g["ffn1_w_gu"],
            g["ffn1_w_down"])
        to_sibling, token = _reduce_start([g[name] for name in names], [kind_of[name] for name in names], names,
                                          f"layer{l}")
        if between_chips is not None:
            on_reduced(l + 1, dict(zip(between_chips[-2], _reduce_finish(between_chips, dx))))
    between_chips, _ = _reduce_middle(to_sibling, dx)
    on_reduced(0, dict(zip(between_chips[-2], _reduce_finish(between_chips, dx))))

    small = {k: jnp.concatenate(v, axis=0) for k, v in d_norm.items()}
    dec = slice(LANE_DECAY, LANE_DECAY + N_DN_HEADS)
    fgt = slice(LANE_FORGET, LANE_FORGET + N_FOX_HEADS)
    small["dn_conv_w"] = jnp.stack([e["conv_w"] for e in d_even])
    small["dn_a_log"] = jnp.concatenate([e["a_log"][:, dec] for e in d_even], axis=0)
    small["dn_dt_bias"] = jnp.concatenate([e["dt_bias"][:, dec] for e in d_even], axis=0)
    small["fox_f_bias"] = jnp.concatenate([e["f_bias"][:, fgt] for e in d_even], axis=0)
    small["dn_norm_g"] = jnp.concatenate([e["dn_norm_g"] for e in d_even], axis=0)
    small["fox_q_norm_g"] = jnp.concatenate([e["q_g"] for e in d_even], axis=0)
    small["fox_k_norm_g"] = jnp.concatenate([e["k_g"] for e in d_even], axis=0)
    return loss, dx, small


MESH = pl.DeviceIdType.MESH
ANY = pl.BlockSpec(memory_space=pl.ANY)


def _place():
    x, y, c = lax.axis_index("x"), lax.axis_index("y"), lax.axis_index("c")
    return x, y, c, [(1 - x, y), (x, 1 - y), (1 - x, 1 - y)]


def _remote(src, dst, send_sem, recv_sem, to):
    return pltpu.make_async_remote_copy(src_ref=src, dst_ref=dst, send_sem=send_sem, recv_sem=recv_sem,
                                        device_id=to, device_id_type=MESH)


def _aligned(start, multiple):
    return start if isinstance(start, int) else pl.multiple_of(start, multiple)


def _quarter(ref, kind, chip, half, rows, cols):
    k = 2 * chip[0] + chip[1]
    hr = rows // 2
    assert hr % 16 == 0 and cols % LANES == 0
    if kind == "col":
        return ref.at[:, pl.ds(_aligned(half * hr, 16), hr), pl.ds(_aligned(k * cols, LANES), cols)]
    return ref.at[:, pl.ds(_aligned(k * rows + half * hr, 16), hr), :]


def _place_quarter(shard, kind, kc, name, first=0, count=None):
    l, rows, cols = shard.shape
    l = l - first if count is None else count
    tr = rows
    while tr * cols * 4 > (2 << 20) and tr % 32 == 0:
        tr //= 2
    nr = rows // tr
    if kind == "col":
        out_spec = pl.BlockSpec((None, tr, cols), lambda li, i, kc_ref: (li, i, kc_ref[0]))
        out_shape = (l, rows, 4 * cols)
    else:
        out_spec = pl.BlockSpec((None, tr, cols), lambda li, i, kc_ref: (li, kc_ref[0] * nr + i, 0))
        out_shape = (l, 4 * rows, cols)

    def body(kc_ref, x_ref, o_ref):
        o_ref[...] = x_ref[...].astype(BF16)

    return pl.pallas_call(
        body, name=name,
        grid_spec=pltpu.PrefetchScalarGridSpec(
            num_scalar_prefetch=1, grid=(l, nr),
            in_specs=[pl.BlockSpec((None, tr, cols), lambda li, i, kc_ref: (li + first, i, 0))],
            out_specs=out_spec),
        out_shape=jax.ShapeDtypeStruct(out_shape, BF16),
        compiler_params=_cparams("parallel", "parallel"),
    )(kc, shard)


def _gather_weights(wholes, kinds):
    n = len(wholes)

    def dims(ref, kind):
        _, r, cc = ref.shape
        return (r, cc // 4) if kind == "col" else (r // 4, cc)

    def body(*refs):
        bufs = refs[n:2 * n]
        send_sems, recv_sems = refs[2 * n:]
        x, y, c, chips = _place()
        sibling = (x, y, 1 - c)
        first, passed = [], []
        for t in range(n):
            rows, cols = dims(bufs[t], kinds[t])
            mine = _quarter(bufs[t], kinds[t], (x, y), c, rows, cols)
            for j, chip in enumerate(chips):
                cp = _remote(mine, mine, send_sems.at[t, j], recv_sems.at[t, j], (*chip, c))
                cp.start()
                first.append(cp)
        for j, chip in enumerate(chips):
            for t in range(n):
                rows, cols = dims(bufs[t], kinds[t])
                got = _quarter(bufs[t], kinds[t], chip, c, rows, cols)
                _remote(got, got, send_sems.at[t, j], recv_sems.at[t, j], (*chip, c)).wait_recv()
                cp = _remote(got, got, send_sems.at[t, 3 + j], recv_sems.at[t, 3 + j], sibling)
                cp.start()
                passed.append(cp)
        for j, chip in enumerate(chips):
            for t in range(n):
                rows, cols = dims(bufs[t], kinds[t])
                got = _quarter(bufs[t], kinds[t], chip, 1 - c, rows, cols)
                _remote(got, got, send_sems.at[t, 3 + j], recv_sems.at[t, 3 + j], sibling).wait_recv()
        for cp in first + passed:
            cp.wait_send()

    return pl.pallas_call(
        body, name="gather_weights", in_specs=[ANY] * n, out_specs=[ANY] * n,
        out_shape=[jax.ShapeDtypeStruct(a.shape, a.dtype) for a in wholes],
        input_output_aliases={t: t for t in range(n)},
        scratch_shapes=[pltpu.SemaphoreType.DMA((n, 6)), pltpu.SemaphoreType.DMA((n, 6))],
        compiler_params=pltpu.CompilerParams(has_side_effects=True),
    )(*wholes)


def _quarter_dims(ref, kind):
    _, r, cc = ref.shape
    return (r, cc // 4) if kind == "col" else (r // 4, cc)


def _gather_chips_copies(bufs, sems, kinds):
    x, y, c, chips = _place()
    copies = []
    for t, buf in enumerate(bufs):
        rows, cols = _quarter_dims(buf, kinds[t])
        mine = _quarter(buf, kinds[t], (x, y), c, rows, cols)
        for j, chip in enumerate(chips):
            pair = 2 * (OTHER_CHIPS * t + j)
            copies.append(_remote(mine, mine, sems[pair], sems[pair + 1], (*chip, c)))
    return copies


def _gather_start(wholes, kinds, after, tag):
    n = len(wholes)
    n_sems = 2 * OTHER_CHIPS * n
    n_in = n + len(after)

    def body(*refs):
        for cp in _gather_chips_copies(refs[:n], refs[n_in + n:n_in + n + n_sems], kinds):
            cp.start()
        refs[-1][...] = jnp.zeros_like(refs[-1])

    held = [pltpu.with_memory_space_constraint(a, pltpu.HBM) for a in wholes]
    out = pl.pallas_call(
        body, name=f"gather_start_{tag}", in_specs=[HBM] * n + [ANY] * len(after),
        out_specs=(*[HBM] * n, *[SEM] * n_sems, pl.BlockSpec(memory_space=pltpu.VMEM)),
        out_shape=(*[pltpu.HBM(a.shape, a.dtype) for a in held], *[pltpu.SemaphoreType.DMA(())] * n_sems,
                   jax.ShapeDtypeStruct((8, LANES), F32)),
        input_output_aliases={i: i for i in range(n)},
        compiler_params=pltpu.CompilerParams(has_side_effects=SPLIT_COPY),
    )(*held, *after)
    return out[n:n + n_sems], out[:n], out[-1]


def _gather_wait(sems, wholes, kinds, after, tag):
    n = len(wholes)

    def body(*refs):
        for cp in _gather_chips_copies(refs[:n], refs[n:n + len(sems)], kinds):
            cp.wait_send()
            cp.wait_recv()

    return pl.pallas_call(
        body, name=f"gather_wait_{tag}", in_specs=[HBM] * n + [SEM] * len(sems) + [ANY],
        out_specs=tuple([HBM] * n), out_shape=tuple(pltpu.HBM(a.shape, a.dtype) for a in wholes),
        input_output_aliases={i: i for i in range(n)},
        compiler_params=pltpu.CompilerParams(has_side_effects=SPLIT_COPY),
    )(*wholes, *sems, after)


def _gather_forward(wholes, kinds, tag):
    n = len(wholes)

    def body(*refs):
        bufs = refs[n:2 * n]
        send_sems, recv_sems = refs[2 * n:]
        x, y, c, chips = _place()
        copies = []
        for t in range(n):
            rows, cols = _quarter_dims(bufs[t], kinds[t])
            for j, chip in enumerate(chips):
                got = _quarter(bufs[t], kinds[t], chip, c, rows, cols)
                cp = _remote(got, got, send_sems.at[t, j], recv_sems.at[t, j], (x, y, 1 - c))
                cp.start()
                copies.append(cp)
        for cp in copies:
            cp.wait_send()
        for t in range(n):
            rows, cols = _quarter_dims(bufs[t], kinds[t])
            for j, chip in enumerate(chips):
                got = _quarter(bufs[t], kinds[t], chip, 1 - c, rows, cols)
                _remote(got, got, send_sems.at[t, j], recv_sems.at[t, j], (x, y, 1 - c)).wait_recv()

    return pl.pallas_call(
        body, name=f"gather_forward_{tag}", in_specs=[ANY] * n, out_specs=[ANY] * n,
        out_shape=[jax.ShapeDtypeStruct(a.shape, a.dtype) for a in wholes],
        input_output_aliases={t: t for t in range(n)},
        scratch_shapes=[pltpu.SemaphoreType.DMA((n, OTHER_CHIPS)), pltpu.SemaphoreType.DMA((n, OTHER_CHIPS))],
        compiler_params=pltpu.CompilerParams(has_side_effects=True),
    )(*wholes)


def _canonical(a, kind):
    l, r, c = a.shape
    return a.reshape(l, 1, r, c) if kind == "col" else a.reshape(l, 4, r // 4, c)


def _add_tile(rows, cols):
    tc = cols if cols <= 1536 else cols // 4
    tr = rows
    while tr * tc * 4 > (1 << 20) and tr % 16 == 0:
        tr //= 2
    return tr, tc


def _rs_add_sibling(part, got, c, name):
    l, a, hr, cols = got.shape
    tr, tc = _add_tile(hr, cols)
    nr = hr // tr

    def body(c_ref, p_ref, g_ref, o32_ref, o16_ref):
        s = p_ref[...] + g_ref[...]
        o32_ref[...] = s
        o16_ref[...] = s.astype(BF16)

    blk = (None, None, tr, tc)
    spec = pl.BlockSpec(blk, lambda li, ai, i, j, c_ref: (li, ai, i, j))
    return pl.pallas_call(
        body, name=name,
        grid_spec=pltpu.PrefetchScalarGridSpec(
            num_scalar_prefetch=1, grid=(l, a, nr, cols // tc),
            in_specs=[pl.BlockSpec(blk, lambda li, ai, i, j, c_ref: (li, ai, c_ref[0] * nr + i, j)), spec],
            out_specs=[spec, spec]),
        out_shape=[jax.ShapeDtypeStruct(got.shape, F32), jax.ShapeDtypeStruct(got.shape, BF16)],
        compiler_params=_cparams("parallel", "parallel", "parallel", "parallel"),
    )(c, part, got)


def _quarter4(ref, kind, chip, cols):
    k = 2 * chip[0] + chip[1]
    if kind == "col":
        return ref.at[:, :, :, pl.ds(pl.multiple_of(k * cols, LANES), cols)]
    return ref.at[:, pl.ds(k, 1), :, :]


HBM = pl.BlockSpec(memory_space=pltpu.HBM)
SEM = pl.BlockSpec(memory_space=pltpu.SEMAPHORE)
SPLIT_COPY = pltpu.SideEffectType.DATAFLOW_SIDE_EFFECTING
OTHER_CHIPS = 3


def _quarter4_shape(a, kind):
    l, _, hr, cols = a.shape
    return (l, 1, hr, cols // 4 if kind == "col" else cols)


def _rs_chips_copies(srcs, lands, sems, kinds):
    x, y, c, chips = _place()
    copies = []
    for t, (src, land) in enumerate(zip(srcs, lands)):
        cols = _quarter4_shape(src, kinds[t])[3]
        for j, chip in enumerate(chips):
            pair = 2 * (OTHER_CHIPS * t + j)
            copies.append(_remote(_quarter4(src, kinds[t], chip, cols), land.at[j], sems[pair], sems[pair + 1],
                                  (*chip, c)))
    return copies


def _split_start(copies, srcs, lands, n_sems, name):
    n = len(srcs)

    def body(*refs):
        for cp in copies(refs[:n], refs[n:2 * n], refs[4 * n:4 * n + n_sems]):
            cp.start()
        refs[-1][...] = jnp.zeros_like(refs[-1])

    held = [pltpu.with_memory_space_constraint(a, pltpu.HBM) for a in (*srcs, *lands)]
    out = pl.pallas_call(
        body, name=name, in_specs=[HBM] * (2 * n),
        out_specs=(*[HBM] * (2 * n), *[SEM] * n_sems, pl.BlockSpec(memory_space=pltpu.VMEM)),
        out_shape=(*[pltpu.HBM(a.shape, a.dtype) for a in held], *[pltpu.SemaphoreType.DMA(())] * n_sems,
                   jax.ShapeDtypeStruct((8, LANES), F32)),
        input_output_aliases={i: i for i in range(2 * n)},
        compiler_params=pltpu.CompilerParams(has_side_effects=SPLIT_COPY),
    )(*held)
    return out[2 * n:2 * n + n_sems], out[:n], out[n:2 * n], out[-1]


def _split_wait(copies, sems, srcs, lands, after, name):
    n = len(srcs)

    def body(*refs):
        for cp in copies(refs[:n], refs[n:2 * n], refs[2 * n:2 * n + len(sems)]):
            cp.wait_send()
            cp.wait_recv()

    out = pl.pallas_call(
        body, name=name, in_specs=[HBM] * (2 * n) + [SEM] * len(sems) + [ANY],
        out_specs=tuple([HBM] * (2 * n)),
        out_shape=tuple(pltpu.HBM(a.shape, a.dtype) for a in (*srcs, *lands)),
        input_output_aliases={i: i for i in range(2 * n)},
        compiler_params=pltpu.CompilerParams(has_side_effects=SPLIT_COPY),
    )(*srcs, *lands, *sems, after)
    return out[:n], out[n:]


def _rs_sibling_copies(srcs, lands, sems):
    x, y, c, _ = _place()
    copies = []
    for t, (src, land) in enumerate(zip(srcs, lands)):
        hr = src.shape[2] // 2
        gives = src.at[:, :, pl.ds(pl.multiple_of((1 - c) * hr, 8), hr), :]
        copies.append(_remote(gives, land, sems[2 * t], sems[2 * t + 1], (x, y, 1 - c)))
    return copies


def _rs_add_chips(sum32, got, kind, kc, name):
    _, l, _, hr, cols = got.shape
    tr, _ = _add_tile(hr, cols)
    nr = hr // tr
    k_arr, c_arr = kc
    if kind == "col":
        own = pl.BlockSpec((None, None, tr, cols), lambda li, i, k_ref, c_ref: (li, 0, i, k_ref[0]))
    else:
        own = pl.BlockSpec((None, None, tr, cols), lambda li, i, k_ref, c_ref: (li, k_ref[0], i, 0))

    def body(k_ref, c_ref, own_ref, got_ref, o_ref):
        o_ref[...] = ((own_ref[...] + got_ref[0].astype(F32)) + got_ref[1].astype(F32)) + got_ref[2].astype(F32)

    return pl.pallas_call(
        body, name=name,
        grid_spec=pltpu.PrefetchScalarGridSpec(
            num_scalar_prefetch=2, grid=(l, nr),
            in_specs=[own, pl.BlockSpec((3, None, None, tr, cols), lambda li, i, k_ref, c_ref: (0, li, 0, i, 0))],
            out_specs=pl.BlockSpec((None, tr, cols), lambda li, i, k_ref, c_ref: (li, c_ref[0] * nr + i, 0))),
        out_shape=jax.ShapeDtypeStruct((l, 2 * hr, cols), F32),
        compiler_params=_cparams("parallel", "parallel"),
    )(k_arr, c_arr, sum32, got)


def _rs_finish(quarters):
    n = len(quarters)

    def body(*refs):
        bufs = refs[n:2 * n]
        send_sems, recv_sems = refs[2 * n:]
        x, y, c, _ = _place()
        copies = []
        for t in range(n):
            hr = bufs[t].shape[1] // 2
            mine = bufs[t].at[:, pl.ds(pl.multiple_of(c * hr, 8), hr), :]
            cp = _remote(mine, mine, send_sems.at[t], recv_sems.at[t], (x, y, 1 - c))
            cp.start()
            copies.append(cp)
        for cp in copies:
            cp.wait()

    return pl.pallas_call(
        body, name="reduce_finish", in_specs=[ANY] * n, out_specs=[ANY] * n,
        out_shape=[jax.ShapeDtypeStruct(a.shape, a.dtype) for a in quarters],
        input_output_aliases={t: t for t in range(n)},
        scratch_shapes=[pltpu.SemaphoreType.DMA((n,)), pltpu.SemaphoreType.DMA((n,))],
        compiler_params=pltpu.CompilerParams(has_side_effects=True),
    )(*quarters)


def _reduce_start(parts, kinds, names, tag):
    canon = [_canonical(p, kind) for p, kind in zip(parts, kinds)]
    lands = [lax.empty(a.shape[:2] + (a.shape[2] // 2, a.shape[3]), a.dtype) for a in canon]
    sems, srcs, lands, token = _split_start(_rs_sibling_copies, canon, lands, 2 * len(canon),
                                            f"reduce_sibling_start_{tag}")
    return (sems, srcs, lands, kinds, names, tag), token


def _reduce_middle(state, after):
    sems, srcs, lands, kinds, names, tag = state
    c_arr = jnp.reshape(lax.axis_index("c"), (1,)).astype(jnp.int32)
    srcs, from_sibling = _split_wait(_rs_sibling_copies, sems, srcs, lands, after, f"reduce_sibling_wait_{tag}")
    sums = [_rs_add_sibling(p, g, c_arr, f"reduce_add_sibling_{nm}") for p, g, nm in zip(srcs, from_sibling, names)]
    sums16 = [s16 for _, s16 in sums]
    copies = functools.partial(_rs_chips_copies, kinds=kinds)
    lands = [lax.empty((OTHER_CHIPS,) + _quarter4_shape(a, k), a.dtype) for a, k in zip(sums16, kinds)]
    sems, srcs, lands, token = _split_start(copies, sums16, lands, 2 * OTHER_CHIPS * len(sums16),
                                            f"reduce_chips_start_{tag}")
    return (sems, srcs, lands, [s32 for s32, _ in sums], kinds, names, tag), token


def _reduce_finish(state, after):
    sems, srcs, lands, sums32, kinds, names, tag = state
    x, y, c = lax.axis_index("x"), lax.axis_index("y"), lax.axis_index("c")
    kc = (jnp.reshape(2 * x + y, (1,)).astype(jnp.int32), jnp.reshape(c, (1,)).astype(jnp.int32))
    copies = functools.partial(_rs_chips_copies, kinds=kinds)
    _, from_chips = _split_wait(copies, sems, srcs, lands, after, f"reduce_chips_wait_{tag}")
    halves = [_rs_add_chips(s32, g, kind, kc, f"reduce_add_chips_{nm}")
              for s32, g, kind, nm in zip(sums32, from_chips, kinds, names)]
    return _rs_finish(halves)


SMALL_PEERS = 7


def _small_exchange(pack):
    rows = pack.shape[0]

    def body(p_ref, slots_ref, total_ref, send_sems, recv_sems):
        x, y, c, _ = _place()
        me = 4 * x + 2 * y + c
        slots_ref[me] = p_ref[...]
        copies = []
        for p in range(1, SMALL_PEERS + 1):
            px, py, pc = (p >> 2) & 1, (p >> 1) & 1, p & 1
            peer = (1 - x if px else x, 1 - y if py else y, 1 - c if pc else c)
            cp = _remote(p_ref, slots_ref.at[me], send_sems.at[p - 1], recv_sems.at[p - 1], peer)
            cp.start()
            copies.append(cp)
        for cp in copies:
            cp.wait()
        total = slots_ref[0]
        for i in range(1, SMALL_PEERS + 1):
            total = total + slots_ref[i]
        total_ref[...] = total

    vmem = pl.BlockSpec(memory_space=pltpu.VMEM)
    return pl.pallas_call(
        body, name="small_exchange", in_specs=[vmem], out_specs=[vmem, vmem],
        out_shape=[jax.ShapeDtypeStruct((SMALL_PEERS + 1, rows, LANES), F32), jax.ShapeDtypeStruct((rows, LANES), F32)],
        scratch_shapes=[pltpu.SemaphoreType.DMA((SMALL_PEERS,)), pltpu.SemaphoreType.DMA((SMALL_PEERS,))],
        compiler_params=pltpu.CompilerParams(has_side_effects=True),
    )(pack)


def _pack(arrays):
    rows = []
    for a in arrays:
        flat = a.reshape(-1).astype(F32)
        rows.append(jnp.pad(flat, (0, (-flat.shape[0]) % LANES)).reshape(-1, LANES))
    out = jnp.concatenate(rows, axis=0)
    return jnp.pad(out, ((0, (-out.shape[0]) % 8), (0, 0)))


def _unpack(pack, shapes):
    out, r = [], 0
    for sh in shapes:
        size = math.prod(sh)
        nr = -(-size // LANES)
        out.append(pack[r:r + nr].reshape(-1)[:size].reshape(sh))
        r += nr
    return out


def _adamw(w, g, m, v, name):
    shape = w.shape
    to2d = lambda a: a.reshape(-1, shape[-1])
    rows = math.prod(shape[:-1])
    tile = 256 if rows % 256 == 0 else rows

    def fn(wb, gb, mb, vb):
        m2 = ADAM_B1 * mb + (1.0 - ADAM_B1) * gb
        v2 = ADAM_B2 * vb + (1.0 - ADAM_B2) * (gb * gb)
        m_hat = m2 / (1.0 - ADAM_B1 ** ADAM_STEP)
        v_hat = v2 / (1.0 - ADAM_B2 ** ADAM_STEP)
        return -ADAM_LR * (m_hat / (jnp.sqrt(v_hat) + ADAM_EPS) + ADAM_WD * wb), m2, v2

    res = _rowwise(fn, [to2d(w), to2d(g), to2d(m), to2d(v)], [], [(shape[-1], F32)] * 3, [], tile=tile, name=name)
    return [r.reshape(shape) for r in res]


def _adamw_layer(w, g, m, v, layer, outs, name):
    _, rows, cols = w.shape
    tile = rows
    while tile * cols * 4 > (1 << 20) and tile % 16 == 0:
        tile //= 2

    def body(w_ref, g_ref, m_ref, v_ref, *rest):
        g_out, d_out, m_out, v_out = rest[-4:]
        gb = g_ref[...]
        m2 = ADAM_B1 * m_ref[...] + (1.0 - ADAM_B1) * gb
        v2 = ADAM_B2 * v_ref[...] + (1.0 - ADAM_B2) * (gb * gb)
        m_hat = m2 / (1.0 - ADAM_B1 ** ADAM_STEP)
        v_hat = v2 / (1.0 - ADAM_B2 ** ADAM_STEP)
        g_out[...] = gb
        d_out[...] = -ADAM_LR * (m_hat / (jnp.sqrt(v_hat) + ADAM_EPS) + ADAM_WD * w_ref[...])
        m_out[...] = m2
        v_out[...] = v2

    stacked = pl.BlockSpec((None, tile, cols), lambda i: (layer, i, 0))
    return pl.pallas_call(
        body, name=name, grid=(rows // tile,),
        in_specs=[stacked, pl.BlockSpec((None, tile, cols), lambda i: (0, i, 0)), stacked, stacked] + [ANY] * 4,
        out_specs=[stacked] * 4, out_shape=[jax.ShapeDtypeStruct(w.shape, F32)] * 4,
        input_output_aliases={4 + i: i for i in range(4)}, compiler_params=_cparams("parallel"),
    )(w, g, m, v, *outs)


BIG = (("ffn1_w_gu", "col"), ("ffn1_w_down", "row"), ("w_in_even", "col"), ("w_out_even", "row"),
       ("w_in_odd", "col"), ("w_out_odd", "row"), ("ffn2_w_gu", "col"), ("ffn2_w_down", "row"))
SMALL = ("norm_ffn1", "norm_mix", "dn_conv_w", "dn_a_log", "dn_dt_bias", "dn_norm_g", "fox_q_norm_g", "fox_k_norm_g",
         "fox_f_bias", "norm_ffn2")
WEIGHTS = ("norm_ffn1", "ffn1_w_gu", "ffn1_w_down", "norm_mix", "w_in_even", "dn_conv_w", "dn_a_log", "dn_dt_bias",
           "dn_norm_g", "fox_q_norm_g", "fox_k_norm_g", "fox_f_bias", "w_out_even", "w_in_odd", "w_out_odd",
           "norm_ffn2", "ffn2_w_gu", "ffn2_w_down")


def _step(x, target, w, m, v):
    k = 2 * lax.axis_index("x") + lax.axis_index("y")
    n_conv = w["dn_conv_w"].shape[2]

    kc = jnp.reshape(k, (1,)).astype(jnp.int32)
    kinds = dict(BIG)
    quarters = {name: w[name] for name in kinds}
    quarters["w_in_even"] = jnp.pad(w["w_in_even"], ((0, 0), (0, 0), (0, EVEN_QUARTER_PAD - EVEN_QUARTER)))
    first_names = [name for name in kinds if name not in ("w_in_odd", "w_out_odd")]
    rest_names = list(kinds)

    def even_columns(whole):
        padded = whole["w_in_even"]
        ref_order = jnp.concatenate([padded[..., q * EVEN_QUARTER_PAD:q * EVEN_QUARTER_PAD + EVEN_QUARTER]
                                     for q in range(4)], axis=-1)
        return {**whole, "w_in_even": _even_to_kernel_layout(ref_order)}

    conv_slots, _ = _small_exchange(_pack([w["dn_conv_w"]]))
    placed = [_place_quarter(quarters[name], kinds[name], kc, f"place_first_{name}", 0, 1) for name in first_names]
    gathered = _gather_weights(placed, [kinds[name] for name in first_names])
    first = even_columns(dict(zip(first_names, gathered)))
    placed = [_place_quarter(quarters[name], kinds[name], kc, f"place_rest_{name}", 1 if name in first_names else 0)
              for name in rest_names]
    rest_kinds = [kinds[name] for name in rest_names]
    sems, on_their_way, token = _gather_start(placed, rest_kinds, [conv_slots, *gathered], "rest")

    def rest_after(value):
        landed = _gather_wait(sems, on_their_way, rest_kinds, value, "rest")
        return even_columns(dict(zip(rest_names, _gather_forward(landed, rest_kinds, "rest"))))

    whole = {}
    conv_rows = math.prod(w["dn_conv_w"].shape) // LANES
    conv_quarters = [conv_slots[2 * q, :conv_rows].reshape(w["dn_conv_w"].shape) for q in range(4)]
    whole["dn_conv_w"] = jnp.concatenate(conv_quarters, axis=-1)
    for name in SMALL:
        if name != "dn_conv_w":
            whole[name] = w[name]

    updated = {name: [lax.empty(w[name].shape, F32) for _ in range(4)] for name in kinds}

    def on_reduced(layer, layer_grads):
        for name, g in layer_grads.items():
            if name == "w_in_even":
                g = g[..., :EVEN_QUARTER]
            stacked_layer = layer if w[name].shape[0] == w["norm_mix"].shape[0] else layer // 2
            updated[name] = _adamw_layer(w[name], g, m[name], v[name], stacked_layer, updated[name], f"adamw_{name}")

    loss, dx, small = _forward_backward(x, target, whole, first, rest_after, token, on_reduced)

    _, small_sum = _small_exchange(_pack([small[n] for n in SMALL]))
    grads = dict(zip(SMALL, _unpack(small_sum, [small[n].shape for n in SMALL])))
    grads["dn_conv_w"] = lax.dynamic_slice_in_dim(grads["dn_conv_w"], k * n_conv, n_conv, axis=2)
    delta, new_m, new_v = {}, {}, {}
    for name in kinds:
        grads[name], delta[name], new_m[name], new_v[name] = updated[name]
    packs = [_pack([d[n] for n in SMALL]) for d in (w, grads, m, v)]
    shapes = [w[n].shape for n in SMALL]
    for out, res in zip((delta, new_m, new_v), _adamw(*packs, "adamw_small")):
        out.update(zip(SMALL, _unpack(res, shapes)))
    total_loss = lax.psum(loss[0, 0], ("x", "y", "c"))
    return total_loss, dx, grads, delta, new_m, new_v


def kernel(x, norm_ffn1, ffn1_w_gu, ffn1_w_down, norm_mix, w_in_even, dn_conv_w, dn_a_log, dn_dt_bias, dn_norm_g, fox_q_norm_g, fox_k_norm_g, fox_f_bias, w_out_even, w_in_odd, w_out_odd, norm_ffn2, ffn2_w_gu, ffn2_w_down, loss_target, m_norm_ffn1, m_ffn1_w_gu, m_ffn1_w_down, m_norm_mix, m_w_in_even, m_dn_conv_w, m_dn_a_log, m_dn_dt_bias, m_dn_norm_g, m_fox_q_norm_g, m_fox_k_norm_g, m_fox_f_bias, m_w_out_even, m_w_in_odd, m_w_out_odd, m_norm_ffn2, m_ffn2_w_gu, m_ffn2_w_down, v_norm_ffn1, v_ffn1_w_gu, v_ffn1_w_down, v_norm_mix, v_w_in_even, v_dn_conv_w, v_dn_a_log, v_dn_dt_bias, v_dn_norm_g, v_fox_q_norm_g, v_fox_k_norm_g, v_fox_f_bias, v_w_out_even, v_w_in_odd, v_w_out_odd, v_norm_ffn2, v_ffn2_w_gu, v_ffn2_w_down):
    w = dict(zip(WEIGHTS, (norm_ffn1, ffn1_w_gu, ffn1_w_down, norm_mix, w_in_even, dn_conv_w, dn_a_log, dn_dt_bias,
                           dn_norm_g, fox_q_norm_g, fox_k_norm_g, fox_f_bias, w_out_even, w_in_odd, w_out_odd,
                           norm_ffn2, ffn2_w_gu, ffn2_w_down)))
    m = dict(zip(WEIGHTS, (m_norm_ffn1, m_ffn1_w_gu, m_ffn1_w_down, m_norm_mix, m_w_in_even, m_dn_conv_w, m_dn_a_log,
                           m_dn_dt_bias, m_dn_norm_g, m_fox_q_norm_g, m_fox_k_norm_g, m_fox_f_bias, m_w_out_even,
                           m_w_in_odd, m_w_out_odd, m_norm_ffn2, m_ffn2_w_gu, m_ffn2_w_down)))
    v = dict(zip(WEIGHTS, (v_norm_ffn1, v_ffn1_w_gu, v_ffn1_w_down, v_norm_mix, v_w_in_even, v_dn_conv_w, v_dn_a_log,
                           v_dn_dt_bias, v_dn_norm_g, v_fox_q_norm_g, v_fox_k_norm_g, v_fox_f_bias, v_w_out_even,
                           v_w_in_odd, v_w_out_odd, v_norm_ffn2, v_ffn2_w_gu, v_ffn2_w_down)))
    loss, dx, grads, delta, new_m, new_v = _step(x[0], loss_target[0], w, m, v)
    return (loss, dx[None], *[grads[n] for n in WEIGHTS], *[delta[n] for n in WEIGHTS],
            *[new_m[n] for n in WEIGHTS], *[new_v[n] for n in WEIGHTS])
```

```python
import functools
import math

import jax
import jax.numpy as jnp
from jax import lax
from jax.experimental import pallas as pl
from jax.experimental.pallas import tpu as pltpu

F32 = jnp.float32
BF16 = jnp.bfloat16
HI = lax.Precision.HIGH

HEAD_DIM = 128
N_DN_HEADS = 4
N_FOX_HEADS = 4
N_SB_HEADS = 8
D_DN = N_DN_HEADS * HEAD_DIM
D_FOX = N_FOX_HEADS * HEAD_DIM
CONV_WIDTH = 4
DN_CHUNK = 64
EPS = 1e-6
ATT_SCALE = HEAD_DIM ** -0.5
ADAM_LR, ADAM_B1, ADAM_B2, ADAM_EPS, ADAM_WD, ADAM_STEP = 0.001, 0.9, 0.999, 1e-08, 0.01, 10

V7X_VMEM_LIMIT = 56 * 1024 * 1024
LANES = 128
ATT_TQ = 512
ATT_TK = 256
ATT_SUB = ATT_TQ // ATT_TK

LANE_BETA, LANE_DECAY, LANE_FORGET = 0, 4, 8


def _cparams(*sem):
    return pltpu.CompilerParams(dimension_semantics=sem, vmem_limit_bytes=V7X_VMEM_LIMIT)


def _sigmoid(x):
    return 1.0 / (1.0 + jnp.exp(-x))


def _softplus(x):
    return jnp.maximum(x, 0.0) + jnp.log(1.0 + jnp.exp(-jnp.abs(x)))


def _silu_grad(y, sg):
    return sg * (1.0 + y * (1.0 - sg))


def _rowwise(fn, rows, bcast, outs, sums, *, tile, name):
    rows = [r if isinstance(r, tuple) else (r, r.shape[1], 0) for r in rows]
    s = rows[0][0].shape[0]
    assert s % tile == 0
    n_in, n_b, n_out, n_sum = len(rows), len(bcast), len(outs), len(sums)

    def body(*refs):
        ins = [r[...] for r in refs[:n_in + n_b]]
        res = fn(*ins)
        if not isinstance(res, (tuple, list)):
            res = (res,)
        out_refs = refs[n_in + n_b:n_in + n_b + n_out]
        sum_refs = refs[n_in + n_b + n_out:]
        for o_ref, val in zip(out_refs, res[:n_out]):
            o_ref[...] = val.astype(o_ref.dtype)
        if n_sum:
            @pl.when(pl.program_id(0) == 0)
            def _():
                for s_ref in sum_refs:
                    s_ref[...] = jnp.zeros_like(s_ref)
            for s_ref, val in zip(sum_refs, res[n_out:]):
                s_ref[...] += val

    in_specs = [pl.BlockSpec((tile, w), lambda i, cb=cb: (i, cb)) for _, w, cb in rows]
    in_specs += [pl.BlockSpec(b.shape, lambda i, nd=b.ndim: (0,) * nd) for b in bcast]
    out_specs = [pl.BlockSpec((tile, c), lambda i: (i, 0)) for c, _ in outs]
    out_specs += [pl.BlockSpec(sh, lambda i: (0, 0)) for sh in sums]
    out_shape = [jax.ShapeDtypeStruct((s, c), dt) for c, dt in outs]
    out_shape += [jax.ShapeDtypeStruct(sh, F32) for sh in sums]
    return pl.pallas_call(
        body, name=name, grid=(s // tile,), in_specs=in_specs, out_specs=out_specs, out_shape=out_shape,
        compiler_params=_cparams("arbitrary" if n_sum else "parallel"),
    )(*[r[0] for r in rows], *bcast)


def _rms_fwd(x, gain, name):
    def fn(xb, g):
        r = lax.rsqrt(jnp.mean(xb * xb, axis=-1, keepdims=True) + EPS)
        return (xb * r * g,)
    return _rowwise(fn, [x], [gain], [(x.shape[1], BF16)], [], tile=512, name=name)[0]


_DIMS = {"nn": (((1,), (0,)), ((), ())), "nt": (((1,), (1,)), ((), ())), "tn": (((0,), (0,)), ((), ()))}


def _dot(a, b, kind):
    return lax.dot_general(a.astype(BF16), b.astype(BF16), _DIMS[kind], preferred_element_type=F32)


def _dot32(a, b, kind="nn"):
    return lax.dot_general(a, b, _DIMS[kind], precision=HI, preferred_element_type=F32)


def _mm(a, b, kind, *, tm, tn, out_dtype, name, scale=None, residual=None, a_lead=(), b_lead=(),
        b_spec=None, n=None, into=None, after=None):
    ash, bsh = a.shape[len(a_lead):], b.shape[len(b_lead):]
    m = ash[1] if kind == "tn" else ash[0]
    k = ash[0] if kind == "tn" else ash[1]
    if b_spec is None:
        n = bsh[0] if kind == "nt" else bsh[1]
        assert k == (bsh[1] if kind == "nt" else bsh[0]), (ash, bsh, kind)
    assert m % tm == 0 and n % tn == 0, (m, tm, n, tn)
    la, lb = (None,) * len(a_lead), (None,) * len(b_lead)
    if kind == "tn":
        a_spec = pl.BlockSpec(la + (k, tm), lambda j, i: a_lead + (0, i))
    else:
        a_spec = pl.BlockSpec(la + (tm, k), lambda j, i: a_lead + (i, 0))
    if b_spec is None:
        if kind == "nt":
            b_spec = pl.BlockSpec(lb + (tn, k), lambda j, i: b_lead + (j, 0))
        else:
            b_spec = pl.BlockSpec(lb + (k, tn), lambda j, i: b_lead + (0, j))
    in_specs, args = [a_spec, b_spec], [a, b]
    if residual is not None:
        in_specs.append(pl.BlockSpec((tm, tn), lambda j, i: (i, j)))
        args.append(residual)
    aliases = {}
    if after is not None:
        in_specs.append(pl.BlockSpec(memory_space=pl.ANY))
        args.append(after)
    if into is not None:
        buf, layer = into
        in_specs.append(pl.BlockSpec(memory_space=pl.ANY))
        args.append(buf)
        aliases = {len(args) - 1: 0}
        out_spec = pl.BlockSpec((None, tm, tn), lambda j, i: (layer, i, j))
        out_shape = jax.ShapeDtypeStruct(buf.shape, buf.dtype)
    else:
        out_spec = pl.BlockSpec((tm, tn), lambda j, i: (i, j))
        out_shape = jax.ShapeDtypeStruct((m, n), out_dtype)

    def body(a_ref, b_ref, *rest):
        acc = _dot(a_ref[...], b_ref[...], kind)
        if scale is not None:
            acc = acc * scale
        if residual is not None:
            acc = acc + rest[0][...]
        rest[-1][...] = acc.astype(rest[-1].dtype)

    return pl.pallas_call(
        body, name=name, grid=(n // tn, m // tm), in_specs=in_specs, out_specs=out_spec, out_shape=out_shape,
        input_output_aliases=aliases, compiler_params=_cparams("parallel", "parallel"),
    )(*args)


FFN_TM = 1024


def _ffn_up(n, w_gu, layer, name):
    s, d = n.shape
    f = w_gu.shape[2] // 2
    tm, tn = FFN_TM, f // 2
    nj = f // tn

    def body(n_ref, wg_ref, wu_ref, gu_ref, a_ref):
        nv = n_ref[...]
        g = _dot(nv, wg_ref[...], "nn")
        u = _dot(nv, wu_ref[...], "nn")
        gu_ref[0] = g.astype(BF16)
        gu_ref[1] = u.astype(BF16)
        a_ref[...] = (g * _sigmoid(g) * u).astype(BF16)

    return pl.pallas_call(
        body, name=name, grid=(nj, s // tm),
        in_specs=[pl.BlockSpec((tm, d), lambda j, i: (i, 0)),
                  pl.BlockSpec((None, d, tn), lambda j, i: (layer, 0, j)),
                  pl.BlockSpec((None, d, tn), lambda j, i: (layer, 0, j + nj))],
        out_specs=[pl.BlockSpec((2, tm, tn), lambda j, i: (0, i, j)),
                   pl.BlockSpec((tm, tn), lambda j, i: (i, j))],
        out_shape=[jax.ShapeDtypeStruct((2, s, f), BF16), jax.ShapeDtypeStruct((s, f), BF16)],
        compiler_params=_cparams("parallel", "parallel"),
    )(n, w_gu, w_gu)


def _ffn_down_bwd(dxo, w_down, gu, layer, name, after=None):
    s, d = dxo.shape
    f = w_down.shape[1]
    tm, tn = FFN_TM, f // 2
    extra_specs, extra = ([ANY], [after]) if after is not None else ([], [])

    def body(dx_ref, w_ref, gu_ref, *rest):
        dgu_ref = rest[-1]
        da = 0.5 * _dot(dx_ref[...], w_ref[...], "nt")
        g = gu_ref[0].astype(F32)
        u = gu_ref[1].astype(F32)
        sg = _sigmoid(g)
        dgu_ref[0] = (da * u * _silu_grad(g, sg)).astype(BF16)
        dgu_ref[1] = (da * g * sg).astype(BF16)

    return pl.pallas_call(
        body, name=name, grid=(f // tn, s // tm),
        in_specs=[pl.BlockSpec((tm, d), lambda j, i: (i, 0)),
                  pl.BlockSpec((None, tn, d), lambda j, i: (layer, j, 0)),
                  pl.BlockSpec((2, tm, tn), lambda j, i: (0, i, j))] + extra_specs,
        out_specs=pl.BlockSpec((2, tm, tn), lambda j, i: (0, i, j)),
        out_shape=jax.ShapeDtypeStruct((2, s, f), BF16),
        compiler_params=_cparams("parallel", "parallel"),
    )(dxo, w_down, gu, *extra)


NORM_BWD_TM = 256


def _norm_bwd_after(terms, operands, specs, x, dres, gain, name):
    s, d = x.shape
    tm = NORM_BWD_TM
    n_op = len(operands)

    def body(*refs):
        x_ref, dres_ref, g_ref = refs[n_op:n_op + 3]
        dx_ref, dx16_ref, dgain_ref = refs[n_op + 3:]
        dn = None
        for a, b in terms(*refs[:n_op]):
            dn = _dot(a, b, "nt") if dn is None else dn + _dot(a, b, "nt")
        xb = x_ref[...]
        r = lax.rsqrt(jnp.mean(xb * xb, axis=-1, keepdims=True) + EPS)
        xh = xb * r
        dxh = dn * g_ref[...]
        dx = dres_ref[...] + r * (dxh - xh * jnp.mean(dxh * xh, axis=-1, keepdims=True))
        dx_ref[...] = dx
        dx16_ref[...] = dx.astype(BF16)

        @pl.when(pl.program_id(0) == 0)
        def _():
            dgain_ref[...] = jnp.zeros_like(dgain_ref)
        dgain_ref[...] += jnp.sum(dn * xh, axis=0, keepdims=True)

    rows = pl.BlockSpec((tm, d), lambda i: (i, 0))
    return pl.pallas_call(
        body, name=name, grid=(s // tm,),
        in_specs=list(specs) + [rows, rows, pl.BlockSpec((1, d), lambda i: (0, 0))],
        out_specs=[rows, rows, pl.BlockSpec((1, d), lambda i: (0, 0))],
        out_shape=[jax.ShapeDtypeStruct((s, d), F32), jax.ShapeDtypeStruct((s, d), BF16),
                   jax.ShapeDtypeStruct((1, d), F32)],
        compiler_params=_cparams("arbitrary"),
    )(*operands, x, dres, gain)


def _ffn_up_bwd(dgu, w_gu, layer, x, dres, gain, name):
    _, s, f = dgu.shape
    d = w_gu.shape[1]
    specs = [pl.BlockSpec((2, NORM_BWD_TM, f), lambda i: (0, i, 0)),
             pl.BlockSpec((None, d, f), lambda i: (layer, 0, 0)),
             pl.BlockSpec((None, d, f), lambda i: (layer, 0, 1))]
    terms = lambda dgu_ref, wg_ref, wu_ref: [(dgu_ref[0], wg_ref[...]), (dgu_ref[1], wu_ref[...])]
    return _norm_bwd_after(terms, [dgu, w_gu, w_gu], specs, x, dres, gain, name)


def _in_proj_bwd(dproj, w_in, j, x, dres, gain, name):
    k = dproj.shape[1]
    d = w_in.shape[1]
    specs = [pl.BlockSpec((NORM_BWD_TM, k), lambda i: (i, 0)), pl.BlockSpec((None, d, k), lambda i: (j, 0, 0))]
    terms = lambda a_ref, b_ref: [(a_ref[...], b_ref[...])]
    return _norm_bwd_after(terms, [dproj, w_in], specs, x, dres, gain, name)


def _ffn_fwd(x, gain, w_gu, w_down, layer, tag):
    n = _rms_fwd(x, gain, f"{tag}_norm")
    gu, a = _ffn_up(n, w_gu, layer, f"{tag}_up")
    x2 = _mm(a, w_down, "nn", tm=512, tn=x.shape[1], out_dtype=F32, name=f"{tag}_down", scale=0.5, residual=x,
             b_lead=(layer,))
    return x2, (x, n, gu, a)


def _ffn_bwd(dxo, dxo16, saved, gain, w_gu, w_down, layer, tag, g_gu, g_down, after=None):
    x, n, gu, a = saved
    s, f = a.shape
    dgu = _ffn_down_bwd(dxo16, w_down, gu, layer, f"{tag}_down_bwd", after)
    g_down = _mm(a, dxo16, "tn", tm=256, tn=dxo16.shape[1], out_dtype=F32, name=f"{tag}_down_dw", scale=0.5,
                 into=(g_down, 0))
    tn = f // 2
    nj = f // tn
    g_gu = _mm(n, dgu, "tn", tm=512, tn=tn, out_dtype=F32, name=f"{tag}_up_dw", into=(g_gu, 0), n=2 * f,
               b_spec=pl.BlockSpec((None, s, tn), lambda j, i: (j // nj, 0, j % nj)))
    dx, dx16, dgain = _ffn_up_bwd(dgu, w_gu, layer, x, dxo, gain, f"{tag}_up_bwd")
    return dx, dx16, dgain, g_gu, g_down


def _lane_col(blk, lane):
    li = lax.broadcasted_iota(jnp.int32, blk.shape, 1)
    return jnp.sum(jnp.where(li == lane, blk, 0.0), axis=1, keepdims=True)


def _split_dot(x, tri):
    hi = x.astype(BF16)
    lo = (x - hi.astype(F32)).astype(BF16)
    return (lax.dot_general(hi, tri, _DIMS["nn"], preferred_element_type=F32)
            + lax.dot_general(lo, tri, _DIMS["nn"], preferred_element_type=F32))


class _Each:
    def __init__(self, vals):
        self.vals = list(vals)

    def _with(self, other, op):
        others = other.vals if isinstance(other, _Each) else [other] * len(self.vals)
        return _Each(op(a, b) for a, b in zip(self.vals, others))

    def __add__(self, other):
        return self._with(other, lambda a, b: a + b)

    def __sub__(self, other):
        return self._with(other, lambda a, b: a - b)

    def __mul__(self, other):
        return self._with(other, lambda a, b: a * b)

    def __neg__(self):
        return _Each(-a for a in self.vals)


def _each(fn, *args):
    n = max(len(a.vals) for a in args if isinstance(a, _Each))
    res = [fn(*xs) for xs in zip(*[a.vals if isinstance(a, _Each) else [a] * n for a in args])]
    if isinstance(res[0], tuple):
        return tuple(_Each(r) for r in zip(*res))
    return _Each(res)


def _keep(cond, x):
    return _each(lambda v: jnp.where(cond, v, 0.0), x)


def _rowsum(x):
    return _each(lambda v: jnp.sum(v, axis=1, keepdims=True), x)


ATT_HEADS = 2
ATT_WIDTH = ATT_HEADS * HEAD_DIM
_HEAD_COLS = [slice(h * HEAD_DIM, (h + 1) * HEAD_DIM) for h in range(ATT_HEADS)]


def _att_specs(n_heads, s):
    groups = n_heads // ATT_HEADS
    q_spec = pl.BlockSpec((ATT_TQ, ATT_WIDTH), lambda g, i: (i, g))
    k_spec = pl.BlockSpec((s, ATT_WIDTH), lambda g, i: (0, groups + g))
    v_spec = pl.BlockSpec((s, ATT_WIDTH), lambda g, i: (0, 2 * groups + g))
    return q_spec, k_spec, v_spec


def _heads_of(ref, rows=None):
    return _Each(ref[:, cs] if rows is None else ref[rows, cs] for cs in _HEAD_COLS)


def _dot_each(a, b, kind):
    return _each(lambda x, y: _dot(x, y, kind), a, b)


def _att_iotas():
    row = lax.broadcasted_iota(jnp.int32, (ATT_TQ, ATT_TK), 0)
    col = lax.broadcasted_iota(jnp.int32, (ATT_TQ, ATT_TK), 1)
    jr = lax.broadcasted_iota(jnp.int32, (ATT_TK, ATT_TK), 0)
    jc = lax.broadcasted_iota(jnp.int32, (ATT_TK, ATT_TK), 1)
    return row, col, jr, jc


def _sb_fwd(qkv, n_heads, name):
    s = qkv.shape[0]

    def body(q_ref, k_ref, v_ref, o16_ref, o32_ref):
        i = pl.program_id(1)
        q = _heads_of(q_ref)
        row, col, jr, jc = _att_iotas()
        later = (jr > jc).astype(BF16)

        def step(jb, carry, diagonal):
            c_sp, acc = (_Each(part) for part in carry)
            work = []
            for sub in reversed(range(ATT_SUB)):
                keys = pl.ds(pl.multiple_of(jb * ATT_TQ + sub * ATT_TK, ATT_TK), ATT_TK)
                z = _dot_each(q, _heads_of(k_ref, keys), "nt") * ATT_SCALE
                sp = _each(_softplus, z)
                before = (col + sub * ATT_TK) < row if diagonal else None
                spm = _keep(before, sp) if diagonal else sp
                work.append((keys, z - sp, spm, _each(lambda x: _dot(x, later, "nn"), spm), before))
            for keys, logsig, spm, within, before in work:
                a = _each(jnp.exp, logsig - (c_sp + within))
                if diagonal:
                    a = _keep(before, a)
                acc = acc + _each(_split_dot, a, _heads_of(v_ref, keys))
                c_sp = c_sp + _rowsum(spm)
            return tuple(c_sp.vals), tuple(acc.vals)

        zeros = lambda width: tuple(jnp.zeros((ATT_TQ, width), F32) for _ in range(ATT_HEADS))
        carry = step(i, (zeros(1), zeros(HEAD_DIM)), True)
        _, acc = lax.fori_loop(0, i, lambda it, cr: step(i - 1 - it, cr, False), carry)
        for cs, acc_h in zip(_HEAD_COLS, acc):
            o16_ref[:, cs] = acc_h.astype(BF16)
            o32_ref[:, cs] = acc_h

    q_spec, k_spec, v_spec = _att_specs(n_heads, s)
    o_spec = pl.BlockSpec((ATT_TQ, ATT_WIDTH), lambda g, i: (i, g))
    return pl.pallas_call(
        body, name=name, grid=(n_heads // ATT_HEADS, s // ATT_TQ), in_specs=[q_spec, k_spec, v_spec],
        out_specs=[o_spec, o_spec],
        out_shape=[jax.ShapeDtypeStruct((s, n_heads * HEAD_DIM), BF16),
                   jax.ShapeDtypeStruct((s, n_heads * HEAD_DIM), F32)],
        compiler_params=_cparams("parallel", "arbitrary"),
    )(qkv, qkv, qkv)


def _sb_bwd(qkv, o32, do, n_heads, name):
    s = qkv.shape[0]

    def body(q_ref, k_ref, v_ref, o_ref, do_ref, dq_ref, dk_ref, dv_ref):
        i = pl.program_id(1)

        @pl.when(i == 0)
        def _():
            dk_ref[...] = jnp.zeros_like(dk_ref)
            dv_ref[...] = jnp.zeros_like(dv_ref)

        q, do = _heads_of(q_ref), _heads_of(do_ref)
        total = _rowsum(_each(lambda a, b: a.astype(F32) * b, do, _heads_of(o_ref)))
        row, col, jr, jc = _att_iotas()
        later = (jr > jc).astype(BF16)
        not_before = (jr >= jc).astype(BF16)

        def step(jb, carry, diagonal):
            c_sp, c_e, dq = (_Each(part) for part in carry)
            work = []
            for sub in reversed(range(ATT_SUB)):
                keys = pl.ds(pl.multiple_of(jb * ATT_TQ + sub * ATT_TK, ATT_TK), ATT_TK)
                k = _heads_of(k_ref, keys)
                z = _dot_each(q, k, "nt") * ATT_SCALE
                sp = _each(_softplus, z)
                before = (col + sub * ATT_TK) < row if diagonal else None
                spm = _keep(before, sp) if diagonal else sp
                work.append((keys, k, _each(jnp.exp, z - sp), spm, _each(lambda x: _dot(x, later, "nn"), spm),
                             _dot_each(do, _heads_of(v_ref, keys), "nt"), before))
            for keys, k, sig, spm, within, da, before in work:
                a = sig * _each(lambda x: jnp.exp(-x), c_sp + within)
                if diagonal:
                    a = _keep(before, a)
                e = a * da
                left = total - c_e - _each(lambda x: _split_dot(x, not_before), e)
                dz = (e - (e + left) * sig) * ATT_SCALE
                if diagonal:
                    dz = _keep(before, dz)
                dk, dv = _dot_each(dz, q, "tn"), _dot_each(a, do, "tn")
                for cs, dk_h, dv_h in zip(_HEAD_COLS, dk.vals, dv.vals):
                    dk_ref[keys, cs] += dk_h
                    dv_ref[keys, cs] += dv_h
                dq = dq + _dot_each(dz, k, "nn")
                c_sp = c_sp + _rowsum(spm)
                c_e = c_e + _rowsum(e)
            return tuple(c_sp.vals), tuple(c_e.vals), tuple(dq.vals)

        zeros = lambda width: tuple(jnp.zeros((ATT_TQ, width), F32) for _ in range(ATT_HEADS))
        carry = step(i, (zeros(1), zeros(1), zeros(HEAD_DIM)), True)
        _, _, dq = lax.fori_loop(0, i, lambda it, cr: step(i - 1 - it, cr, False), carry)
        for cs, dq_h in zip(_HEAD_COLS, dq):
            dq_ref[:, cs] = dq_h.astype(BF16)

    q_spec, k_spec, v_spec = _att_specs(n_heads, s)
    blk = pl.BlockSpec((ATT_TQ, ATT_WIDTH), lambda g, i: (i, g))
    full = pl.BlockSpec((s, ATT_WIDTH), lambda g, i: (0, g))
    wide = (s, n_heads * HEAD_DIM)
    return pl.pallas_call(
        body, name=name, grid=(n_heads // ATT_HEADS, s // ATT_TQ), in_specs=[q_spec, k_spec, v_spec, blk, blk],
        out_specs=[blk, full, full],
        out_shape=[jax.ShapeDtypeStruct(wide, BF16), jax.ShapeDtypeStruct(wide, F32), jax.ShapeDtypeStruct(wide, F32)],
        compiler_params=_cparams("parallel", "arbitrary"),
    )(qkv, qkv, qkv, o32, do)


def _fox_logits(q, k, cq, ct_ref, keys):
    ck = _Each(ct_ref[h, :, keys] for h in range(ATT_HEADS))
    return _dot_each(q, k, "nt") * ATT_SCALE + (cq - ck)


def _fox_cq(c_ref, group):
    c = c_ref[...]
    return _Each(_lane_col(c, LANE_FORGET + group * ATT_HEADS + h) for h in range(ATT_HEADS))


def _fox_fwd(qkv, c, ct, name):
    s = qkv.shape[0]
    n_heads = N_FOX_HEADS

    def body(q_ref, k_ref, v_ref, c_ref, ct_ref, o_ref, lse_ref):
        g, i = pl.program_id(0), pl.program_id(1)
        q = _heads_of(q_ref)
        cq = _fox_cq(c_ref, g)
        row, col, _, _ = _att_iotas()

        def step(jb, carry, diagonal):
            m, l, acc = (_Each(part) for part in carry)
            work = []
            m_new = m
            for sub in range(ATT_SUB):
                keys = pl.ds(pl.multiple_of(jb * ATT_TQ + sub * ATT_TK, ATT_TK), ATT_TK)
                sc = _fox_logits(q, _heads_of(k_ref, keys), cq, ct_ref, keys)
                valid = (col + sub * ATT_TK) <= row if diagonal else None
                if diagonal:
                    sc = _each(lambda x: jnp.where(valid, x, -1e30), sc)
                m_new = _each(lambda a, x: jnp.maximum(a, jnp.max(x, axis=1, keepdims=True)), m_new, sc)
                work.append((keys, sc, valid))
            w = _each(jnp.exp, m - m_new)
            l, acc = l * w, acc * w
            for keys, sc, valid in work:
                p = _each(jnp.exp, sc - m_new)
                if diagonal:
                    p = _keep(valid, p)
                l = l + _rowsum(p)
                acc = acc + _each(_split_dot, p, _heads_of(v_ref, keys))
            return tuple(m_new.vals), tuple(l.vals), tuple(acc.vals)

        per_head = lambda width, value: tuple(jnp.full((ATT_TQ, width), value, F32) for _ in range(ATT_HEADS))
        init = (per_head(1, -1e30), per_head(1, 0.0), per_head(HEAD_DIM, 0.0))
        m, l, acc = lax.fori_loop(0, i, lambda jb, cr: step(jb, cr, False), step(i, init, True))
        for h, cs in enumerate(_HEAD_COLS):
            o_ref[:, cs] = acc[h] / l[h]
            lse_ref[h] = jnp.broadcast_to(m[h] + jnp.log(l[h]), (ATT_TQ, LANES))

    q_spec, k_spec, v_spec = _att_specs(n_heads, s)
    return pl.pallas_call(
        body, name=name, grid=(n_heads // ATT_HEADS, s // ATT_TQ),
        in_specs=[q_spec, k_spec, v_spec, pl.BlockSpec((ATT_TQ, LANES), lambda g, i: (i, 0)),
                  pl.BlockSpec((ATT_HEADS, 1, s), lambda g, i: (g, 0, 0))],
        out_specs=[pl.BlockSpec((ATT_TQ, ATT_WIDTH), lambda g, i: (i, g)),
                   pl.BlockSpec((ATT_HEADS, ATT_TQ, LANES), lambda g, i: (g, i, 0))],
        out_shape=[jax.ShapeDtypeStruct((s, n_heads * HEAD_DIM), F32),
                   jax.ShapeDtypeStruct((n_heads, s, LANES), F32)],
        compiler_params=_cparams("parallel", "arbitrary"),
    )(qkv, qkv, qkv, c, ct)


def _fox_bwd(qkv, c, ct, o, lse, do, name):
    s = qkv.shape[0]
    n_heads = N_FOX_HEADS

    def body(q_ref, k_ref, v_ref, c_ref, ct_ref, o_ref, lse_ref, do_ref, dq_ref, dk_ref, dv_ref, dct_ref):
        g, i = pl.program_id(0), pl.program_id(1)

        @pl.when(i == 0)
        def _():
            dk_ref[...] = jnp.zeros_like(dk_ref)
            dv_ref[...] = jnp.zeros_like(dv_ref)
            dct_ref[...] = jnp.zeros_like(dct_ref)

        q = _heads_of(q_ref)
        do16 = _each(lambda x: x.astype(BF16), _heads_of(do_ref))
        delta = _rowsum(_each(lambda a, b: a.astype(F32) * b, do16, _heads_of(o_ref)))
        lse_col = _Each(lse_ref[h, :, 0:1] for h in range(ATT_HEADS))
        cq = _fox_cq(c_ref, g)
        row, col, _, _ = _att_iotas()

        def step(jb, dq, diagonal):
            dq = _Each(dq)
            for sub in range(ATT_SUB):
                keys = pl.ds(pl.multiple_of(jb * ATT_TQ + sub * ATT_TK, ATT_TK), ATT_TK)
                k = _heads_of(k_ref, keys)
                sc = _fox_logits(q, k, cq, ct_ref, keys)
                if diagonal:
                    valid = (col + sub * ATT_TK) <= row
                    p = _keep(valid, _each(jnp.exp, _keep(valid, sc) - lse_col))
                else:
                    p = _each(jnp.exp, sc - lse_col)
                ds = p * (_dot_each(do16, _heads_of(v_ref, keys), "nt") - delta)
                dss = ds * ATT_SCALE
                dk, dv = _dot_each(dss, q, "tn"), _dot_each(p, do16, "tn")
                for h, cs in enumerate(_HEAD_COLS):
                    dct_ref[h, :, keys] -= jnp.sum(ds.vals[h], axis=0, keepdims=True)
                    dk_ref[keys, cs] += dk.vals[h]
                    dv_ref[keys, cs] += dv.vals[h]
                dq = dq + _dot_each(dss, k, "nn")
            return tuple(dq.vals)

        dq0 = step(i, tuple(jnp.zeros((ATT_TQ, HEAD_DIM), F32) for _ in range(ATT_HEADS)), True)
        dq = lax.fori_loop(0, i, lambda jb, dq: step(jb, dq, False), dq0)
        for cs, dq_h in zip(_HEAD_COLS, dq):
            dq_ref[:, cs] = dq_h

    q_spec, k_spec, v_spec = _att_specs(n_heads, s)
    blk = pl.BlockSpec((ATT_TQ, ATT_WIDTH), lambda g, i: (i, g))
    full = pl.BlockSpec((s, ATT_WIDTH), lambda g, i: (0, g))
    wide = jax.ShapeDtypeStruct((s, n_heads * HEAD_DIM), F32)
    return pl.pallas_call(
        body, name=name, grid=(n_heads // ATT_HEADS, s // ATT_TQ),
        in_specs=[q_spec, k_spec, v_spec, pl.BlockSpec((ATT_TQ, LANES), lambda g, i: (i, 0)),
                  pl.BlockSpec((ATT_HEADS, 1, s), lambda g, i: (g, 0, 0)), blk,
                  pl.BlockSpec((ATT_HEADS, ATT_TQ, LANES), lambda g, i: (g, i, 0)), blk],
        out_specs=[blk, full, full, pl.BlockSpec((ATT_HEADS, 1, s), lambda g, i: (g, 0, 0))],
        out_shape=[wide, wide, wide, jax.ShapeDtypeStruct((n_heads, 1, s), F32)],
        compiler_params=_cparams("parallel", "arbitrary"),
    )(qkv, qkv, qkv, c, ct, o, lse, do)


def _cumsum_rows(x, reverse, name):
    s = x.shape[0]
    nb = s // LANES

    def body(x_ref, o_ref):
        r = lax.broadcasted_iota(jnp.int32, (LANES, LANES), 0)
        c = lax.broadcasted_iota(jnp.int32, (LANES, LANES), 1)
        tri = ((r <= c) if reverse else (r >= c)).astype(F32)

        def step(it, carry):
            b = (nb - 1 - it) if reverse else it
            off = pl.multiple_of(b * LANES, LANES)
            blk = x_ref[pl.ds(off, LANES), :]
            o_ref[pl.ds(off, LANES), :] = _dot32(tri, blk) + carry
            return carry + jnp.sum(blk, axis=0, keepdims=True)

        lax.fori_loop(0, nb, step, jnp.zeros((1, LANES), F32))

    return pl.pallas_call(body, name=name, out_shape=jax.ShapeDtypeStruct(x.shape, F32),
                          compiler_params=pltpu.CompilerParams(vmem_limit_bytes=V7X_VMEM_LIMIT))(x)


def _dot32_each(a, b, kind="nn"):
    return _each(lambda x, y: _dot32(x, y, kind), a, b)


def _unit_lower_inverse(m, ri, ci):
    c = ri.shape[0]
    t = -_keep(ri // 2 == ci // 2, m) + jnp.where(ri == ci, 1.0, 0.0)
    b = 4
    while b <= c:
        off_diag = (ri // b == ci // b) & (ri % b >= b // 2) & (ci % b < b // 2)
        t = t - _dot32_each(_dot32_each(t, _keep(off_diag, m)), t)
        b *= 2
    return t


def _dn_gates(g, ri, ci):
    eye = ri == ci
    incl = ri >= ci
    g_row = jnp.sum(jnp.where(eye, g, 0.0), axis=0, keepdims=True)
    gc = jnp.sum(jnp.where(incl, g_row, 0.0), axis=1, keepdims=True)
    gc_row = jnp.sum(jnp.where(eye, gc, 0.0), axis=0, keepdims=True)
    dmat = jnp.where(incl, jnp.exp(jnp.where(incl, gc - gc_row, 0.0)), 0.0)
    gc_last = jnp.sum(g, axis=0, keepdims=True)
    return gc, dmat, jnp.exp(gc), jnp.exp(gc_last - gc), jnp.exp(gc_last)


def _dn_fwd(qkv, act, name):
    s = qkv.shape[0]
    c, d, nh = DN_CHUNK, HEAD_DIM, N_DN_HEADS
    nc = s // c

    def body(q_ref, k_ref, v_ref, act_ref, o_ref, s_ref, t_ref, state):
        @pl.when(pl.program_id(0) == 0)
        def _():
            state[...] = jnp.zeros_like(state)

        ri = lax.broadcasted_iota(jnp.int32, (c, c), 0)
        ci = lax.broadcasted_iota(jnp.int32, (c, c), 1)
        act = act_ref[...]
        heads = range(nh)
        cols = [slice(h * d, (h + 1) * d) for h in heads]
        q, k, v = (_Each(ref[:, cs] for cs in cols) for ref in (q_ref, k_ref, v_ref))
        beta = _Each(_lane_col(act, LANE_BETA + h) for h in heads)
        g = _Each(_lane_col(act, LANE_DECAY + h) for h in heads)
        _, dmat, e, r, gl = _each(lambda gh: _dn_gates(gh, ri, ci), g)
        s0 = _Each(state[h] for h in heads)
        kb = beta * k
        t = _unit_lower_inverse(_keep(ri > ci, _dot32_each(kb, k, "nt") * dmat), ri, ci)
        vn = _dot32_each(t, beta * v) - _dot32_each(_dot32_each(t, kb * e), s0)
        o = _dot32_each(q * e, s0) + _dot32_each(_dot32_each(q, k, "nt") * dmat, vn)
        s1 = s0 * gl + _dot32_each(k * r, vn, "tn")
        for h in heads:
            o_ref[:, cols[h]] = o.vals[h]
            state[h] = s1.vals[h]
            s_ref[h] = s0.vals[h]
            t_ref[h] = t.vals[h]

    wide = lambda part: pl.BlockSpec((c, nh * d), lambda n: (n, part))
    return pl.pallas_call(
        body, name=name, grid=(nc,),
        in_specs=[wide(0), wide(1), wide(2), pl.BlockSpec((c, LANES), lambda n: (n, 0))],
        out_specs=[wide(0), pl.BlockSpec((nh, None, d, d), lambda n: (0, n, 0, 0)),
                   pl.BlockSpec((nh, None, c, c), lambda n: (0, n, 0, 0))],
        out_shape=[jax.ShapeDtypeStruct((s, nh * d), F32), jax.ShapeDtypeStruct((nh, nc, d, d), F32),
                   jax.ShapeDtypeStruct((nh, nc, c, c), F32)],
        scratch_shapes=[pltpu.VMEM((nh, d, d), F32)],
        compiler_params=_cparams("arbitrary"),
    )(qkv, qkv, qkv, act)


def _dn_bwd(qkv, act, states, tinv, do, name):
    s = qkv.shape[0]
    c, d, nh = DN_CHUNK, HEAD_DIM, N_DN_HEADS
    nc = s // c

    def chunk_bwd(q, k, v, do, beta, g, s0, t, ds_out):
        ri = lax.broadcasted_iota(jnp.int32, (c, c), 0)
        ci = lax.broadcasted_iota(jnp.int32, (c, c), 1)
        eye, incl, strict = ri == ci, ri >= ci, ri > ci
        gc, dmat, e, r, gl = _each(lambda gh: _dn_gates(gh, ri, ci), g)
        dot = _dot32_each
        rowsum = lambda x: _each(lambda a: jnp.sum(a, axis=1, keepdims=True), x)
        colsum = lambda x: _each(lambda a: jnp.sum(a, axis=0, keepdims=True), x)
        total = lambda x: colsum(rowsum(x))
        to_col = lambda row: rowsum(_keep(eye, row))
        to_row = lambda colv: colsum(_keep(eye, colv))

        kb, vb = beta * k, beta * v
        kbe = kb * e
        u, w = dot(t, vb), dot(t, kbe)
        vn = u - dot(w, s0)
        qk = dot(q, k, "nt")
        p = qk * dmat
        gram = dot(k, k, "nt")
        kr, qe = k * r, q * e

        d_kr = dot(vn, ds_out, "nt")
        dvn = dot(kr, ds_out)
        dgl = total(s0 * ds_out)
        ds_in = ds_out * gl
        dk = d_kr * r
        dr = rowsum(d_kr * k)
        d_qe = dot(do, s0, "nt")
        ds_in = ds_in + dot(qe, do, "tn")
        dp = _keep(incl, dot(do, vn, "nt"))
        dvn = dvn + dot(p, do, "tn")
        dq = d_qe * e
        de = rowsum(d_qe * q)
        dqk = dp * dmat
        dq = dq + dot(dqk, k)
        dk = dk + dot(dqk, q, "tn")
        dd = dp * qk
        dw = -dot(dvn, s0, "nt")
        ds_in = ds_in - dot(w, dvn, "tn")
        dvb = dot(t, dvn, "tn")
        dkbe = dot(t, dw, "tn")
        dm = -_keep(strict, dot(dvb, u, "nt") + dot(dkbe, w, "nt"))
        dbeta = rowsum(dm * gram * dmat)
        dgram = dm * beta * dmat
        dd = dd + dm * beta * gram
        dk = dk + dot(dgram, k) + dot(dgram, k, "tn")
        dkb = dkbe * e
        de = de + rowsum(dkbe * kb)
        dk = dk + beta * dkb
        dbeta = dbeta + rowsum(dkb * k) + rowsum(dvb * v)
        dv = beta * dvb
        wd = dd * dmat
        dgc = rowsum(wd) - to_col(colsum(wd)) + de * e - dr * r
        dgc_last = total(dr * r) + dgl * gl
        dgc = dgc + _keep(ri[:, 0:1] == c - 1, dgc_last)
        dg = rowsum(_keep(ri <= ci, to_row(dgc)))
        return dq, dk, dv, dbeta, dg, ds_in

    def body(q_ref, k_ref, v_ref, act_ref, s_ref, t_ref, do_ref, dq_ref, dk_ref, dv_ref, dact_ref, dstate):
        @pl.when(pl.program_id(0) == 0)
        def _():
            dstate[...] = jnp.zeros_like(dstate)

        act = act_ref[...]
        heads = range(nh)
        cols = [slice(h * d, (h + 1) * d) for h in heads]
        q, k, v, do = (_Each(ref[:, cs] for cs in cols) for ref in (q_ref, k_ref, v_ref, do_ref))
        dq, dk, dv, dbeta, dg, ds_in = chunk_bwd(
            q, k, v, do, _Each(_lane_col(act, LANE_BETA + h) for h in heads),
            _Each(_lane_col(act, LANE_DECAY + h) for h in heads), _Each(s_ref[h] for h in heads),
            _Each(t_ref[h] for h in heads), _Each(dstate[h] for h in heads))
        lane = lax.broadcasted_iota(jnp.int32, (c, LANES), 1)
        dact = jnp.zeros((c, LANES), F32)
        for h in heads:
            dstate[h] = ds_in.vals[h]
            dq_ref[:, cols[h]], dk_ref[:, cols[h]], dv_ref[:, cols[h]] = dq.vals[h], dk.vals[h], dv.vals[h]
            dact = (dact + jnp.where(lane == LANE_BETA + h, dbeta.vals[h], 0.0)
                    + jnp.where(lane == LANE_DECAY + h, dg.vals[h], 0.0))
        dact_ref[...] = dact

    part = lambda p: pl.BlockSpec((c, nh * d), lambda n: (nc - 1 - n, p))
    per = lambda a, b: pl.BlockSpec((nh, None, a, b), lambda n: (0, nc - 1 - n, 0, 0))
    wide = jax.ShapeDtypeStruct((s, nh * d), F32)
    act_spec = pl.BlockSpec((c, LANES), lambda n: (nc - 1 - n, 0))
    return pl.pallas_call(
        body, name=name, grid=(nc,),
        in_specs=[part(0), part(1), part(2), act_spec, per(d, d), per(c, c), part(0)],
        out_specs=[part(0), part(0), part(0), act_spec],
        out_shape=[wide, wide, wide, jax.ShapeDtypeStruct((s, LANES), F32)],
        scratch_shapes=[pltpu.VMEM((nh, d, d), F32)],
        compiler_params=_cparams("arbitrary"),
    )(qkv, qkv, qkv, act, states, tinv, do)


EVEN_DN_QKV, EVEN_FOX_QKV, EVEN_DN_GATE, EVEN_FOX_GATE, EVEN_NARROW = 0, 1536, 3072, 3584, 4096
EVEN_WIDTH = 4224
CONV_TILE = 256
CONV_HALO = 8


def _conv_fwd(proj, w, name):
    s = proj.shape[0]
    t, cw = CONV_TILE, 3 * D_DN

    def body(cur_ref, prev_ref, w_ref, y_ref, xs):
        i = pl.program_id(0)
        xs[0:CONV_HALO, :] = jnp.where(i > 0, prev_ref[...], 0.0)
        xs[CONV_HALO:, :] = cur_ref[...]
        y = jnp.zeros((t, cw), F32)
        for tap in range(CONV_WIDTH):
            y = y + w_ref[tap:tap + 1, :] * xs[pl.ds(CONV_HALO - CONV_WIDTH + 1 + tap, t), :]
        y_ref[...] = y

    per = t // CONV_HALO
    return pl.pallas_call(
        body, name=name, grid=(s // t,),
        in_specs=[pl.BlockSpec((t, cw), lambda i: (i, 0)),
                  pl.BlockSpec((CONV_HALO, cw), lambda i: (jnp.maximum(i * per - 1, 0), 0)),
                  pl.BlockSpec((CONV_WIDTH, cw), lambda i: (0, 0))],
        out_specs=pl.BlockSpec((t, cw), lambda i: (i, 0)),
        out_shape=jax.ShapeDtypeStruct((s, cw), F32),
        scratch_shapes=[pltpu.VMEM((t + CONV_HALO, cw), F32)],
        compiler_params=_cparams("parallel"),
    )(proj, proj, w)


def _conv_bwd(proj, w, dy, name):
    s = proj.shape[0]
    t, cw = CONV_TILE, 3 * D_DN
    nt = s // t

    def body(cur_ref, prev_ref, w_ref, dy_ref, nxt_ref, dx_ref, dw_ref, xs, dys):
        i = pl.program_id(0)

        @pl.when(i == 0)
        def _():
            dw_ref[...] = jnp.zeros_like(dw_ref)

        xs[0:CONV_HALO, :] = jnp.where(i > 0, prev_ref[...], 0.0)
        xs[CONV_HALO:, :] = cur_ref[...]
        dys[0:t, :] = dy_ref[...]
        dys[t:, :] = jnp.where(i < nt - 1, nxt_ref[...], 0.0)
        dy = dy_ref[...]
        dx = jnp.zeros((t, cw), F32)
        for tap in range(CONV_WIDTH):
            dx = dx + w_ref[tap:tap + 1, :] * dys[pl.ds(CONV_WIDTH - 1 - tap, t), :]
            dw_ref[tap:tap + 1, :] += jnp.sum(dy * xs[pl.ds(CONV_HALO - CONV_WIDTH + 1 + tap, t), :], axis=0,
                                              keepdims=True)
        dx_ref[...] = dx.astype(BF16)

    per = t // CONV_HALO
    last = s // CONV_HALO - 1
    return pl.pallas_call(
        body, name=name, grid=(nt,),
        in_specs=[pl.BlockSpec((t, cw), lambda i: (i, 0)),
                  pl.BlockSpec((CONV_HALO, cw), lambda i: (jnp.maximum(i * per - 1, 0), 0)),
                  pl.BlockSpec((CONV_WIDTH, cw), lambda i: (0, 0)),
                  pl.BlockSpec((t, cw), lambda i: (i, 0)),
                  pl.BlockSpec((CONV_HALO, cw), lambda i: (jnp.minimum((i + 1) * per, last), 0))],
        out_specs=[pl.BlockSpec((t, cw), lambda i: (i, 0)), pl.BlockSpec((CONV_WIDTH, cw), lambda i: (0, 0))],
        out_shape=[jax.ShapeDtypeStruct((s, cw), BF16), jax.ShapeDtypeStruct((CONV_WIDTH, cw), F32)],
        scratch_shapes=[pltpu.VMEM((t + CONV_HALO, cw), F32), pltpu.VMEM((t + CONV_HALO, cw), F32)],
        compiler_params=_cparams("arbitrary"),
    )(proj, proj, w, dy, dy)


def _heads(x, n):
    return [x[:, HEAD_DIM * h:HEAD_DIM * (h + 1)] for h in range(n)]


def _dn_pre_fwd(y, name):
    def fn(yb):
        cs = yb * _sigmoid(yb)
        out = []
        for idx, xh in enumerate(_heads(cs, 3 * N_DN_HEADS)):
            if idx < 2 * N_DN_HEADS:
                xh = xh * lax.rsqrt(jnp.sum(xh * xh, axis=-1, keepdims=True) + EPS)
                if idx < N_DN_HEADS:
                    xh = xh * ATT_SCALE
            out.append(xh)
        return (jnp.concatenate(out, axis=1),)
    return _rowwise(fn, [y], [], [(y.shape[1], F32)], [], tile=256, name=name)[0]


def _dn_pre_bwd(y, dq, dk, dv, name):
    def fn(yb, dqb, dkb, dvb):
        sg = _sigmoid(yb)
        cs = yb * sg
        dout = _heads(dqb, N_DN_HEADS) + _heads(dkb, N_DN_HEADS) + _heads(dvb, N_DN_HEADS)
        dcs = []
        for idx, (xh, dh) in enumerate(zip(_heads(cs, 3 * N_DN_HEADS), dout)):
            if idx < 2 * N_DN_HEADS:
                if idx < N_DN_HEADS:
                    dh = dh * ATT_SCALE
                r = lax.rsqrt(jnp.sum(xh * xh, axis=-1, keepdims=True) + EPS)
                xhat = xh * r
                dh = r * (dh - xhat * jnp.sum(xhat * dh, axis=-1, keepdims=True))
            dcs.append(dh)
        return (jnp.concatenate(dcs, axis=1) * _silu_grad(yb, sg),)
    return _rowwise(fn, [y, dq, dk, dv], [], [(y.shape[1], F32)], [], tile=256, name=name)[0]


def _narrow_params(a_log, dt_bias, f_bias):
    lanes = lambda a, first: jnp.pad(a.reshape(1, -1), ((0, 0), (first, LANES - first - a.shape[0])))
    return jnp.concatenate([lanes(a_log, LANE_DECAY), lanes(dt_bias, LANE_DECAY), lanes(f_bias, LANE_FORGET),
                            jnp.zeros((5, LANES), F32)], axis=0)


def _narrow_masks(shape):
    lane = lax.broadcasted_iota(jnp.int32, shape, 1)
    is_beta = lane < LANE_DECAY
    is_decay = (lane >= LANE_DECAY) & (lane < LANE_FORGET)
    is_forget = (lane >= LANE_FORGET) & (lane < LANE_FORGET + N_FOX_HEADS)
    return is_beta, is_decay, is_forget


def _narrow_fwd(proj, params, name):
    def fn(sm, pk):
        is_beta, is_decay, is_forget = _narrow_masks(sm.shape)
        g = -jnp.exp(pk[0:1, :]) * _softplus(sm + pk[1:2, :])
        logf = -_softplus(-(sm + pk[2:3, :]))
        return (jnp.where(is_beta, _sigmoid(sm), jnp.where(is_decay, g, jnp.where(is_forget, logf, 0.0))),)
    return _rowwise(fn, [(proj, LANES, EVEN_NARROW // LANES)], [params], [(LANES, F32)], [], tile=512, name=name)[0]


def _narrow_bwd(proj, params, act, dact, dlogf, name):
    def fn(sm, ab, da, dl, pk):
        is_beta, is_decay, is_forget = _narrow_masks(sm.shape)
        db = jnp.where(is_forget, dl, da)
        d_beta = db * ab * (1.0 - ab)
        d_decay = db * (-jnp.exp(pk[0:1, :])) * _sigmoid(sm + pk[1:2, :])
        d_forget = db * _sigmoid(-(sm + pk[2:3, :]))
        dsm = jnp.where(is_beta, d_beta, jnp.where(is_decay, d_decay, jnp.where(is_forget, d_forget, 0.0)))
        col = lambda x: jnp.sum(x, axis=0, keepdims=True)
        return (dsm, col(jnp.where(is_decay, db * ab, 0.0)), col(jnp.where(is_decay, dsm, 0.0)),
                col(jnp.where(is_forget, dsm, 0.0)))
    return _rowwise(fn, [(proj, LANES, EVEN_NARROW // LANES), act, dact, dlogf], [params], [(LANES, BF16)],
                    [(1, LANES)] * 3, tile=512, name=name)


def _head_rms(xh):
    r = lax.rsqrt(jnp.mean(xh * xh, axis=-1, keepdims=True) + EPS)
    return xh * r, r


def _fox_pre_fwd(proj, qg, kg, name):
    def fn(pf, qgb, kgb):
        out = []
        for idx, xh in enumerate(_heads(pf, 3 * N_FOX_HEADS)):
            if idx < 2 * N_FOX_HEADS:
                xh = _head_rms(xh)[0] * (qgb if idx < N_FOX_HEADS else kgb)
            out.append(xh)
        return (jnp.concatenate(out, axis=1),)
    return _rowwise(fn, [(proj, 3 * D_FOX, EVEN_FOX_QKV // (3 * D_FOX))], [qg, kg], [(3 * D_FOX, BF16)], [],
                    tile=256, name=name)[0]


def _fox_pre_bwd(proj, qg, kg, dq, dk, dv, name):
    def fn(pf, dqb, dkb, dvb, qgb, kgb):
        dout = _heads(dqb, N_FOX_HEADS) + _heads(dkb, N_FOX_HEADS) + _heads(dvb, N_FOX_HEADS)
        dg = [jnp.zeros((1, HEAD_DIM), F32), jnp.zeros((1, HEAD_DIM), F32)]
        dx = []
        for idx, (xh, dh) in enumerate(zip(_heads(pf, 3 * N_FOX_HEADS), dout)):
            if idx < 2 * N_FOX_HEADS:
                which = 0 if idx < N_FOX_HEADS else 1
                xhat, r = _head_rms(xh)
                dg[which] = dg[which] + jnp.sum(dh * xhat, axis=0, keepdims=True)
                dxh = dh * (qgb if which == 0 else kgb)
                dh = r * (dxh - xhat * jnp.mean(dxh * xhat, axis=-1, keepdims=True))
            dx.append(dh)
        return jnp.concatenate(dx, axis=1), dg[0], dg[1]
    return _rowwise(fn, [(proj, 3 * D_FOX, EVEN_FOX_QKV // (3 * D_FOX)), dq, dk, dv], [qg, kg],
                    [(3 * D_FOX, BF16)], [(1, HEAD_DIM)] * 2, tile=256, name=name)


def _mix_gate_fwd(proj, o_dn, o_fox, ng, name):
    def fn(gd, gf, od, of, ngb):
        dn = [_head_rms(xh)[0] * ngb for xh in _heads(od, N_DN_HEADS)]
        return (jnp.concatenate([jnp.concatenate(dn, axis=1) * gd * _sigmoid(gd), of * _sigmoid(gf)], axis=1),)
    return _rowwise(fn, [(proj, D_DN, EVEN_DN_GATE // D_DN), (proj, D_FOX, EVEN_FOX_GATE // D_FOX), o_dn, o_fox],
                    [ng], [(D_DN + D_FOX, BF16)], [], tile=256, name=name)[0]


def _mix_gate_bwd(proj, o_dn, o_fox, ng, dom, name):
    def fn(gd, gf, od, of, dm, ngb):
        d_dn, d_fox = dm[:, :D_DN], dm[:, D_DN:]
        sgd, sgf = _sigmoid(gd), _sigmoid(gf)
        don = d_dn * gd * sgd
        dng = jnp.zeros((1, HEAD_DIM), F32)
        dod, normed = [], []
        for xh, dh in zip(_heads(od, N_DN_HEADS), _heads(don, N_DN_HEADS)):
            xhat, r = _head_rms(xh)
            dng = dng + jnp.sum(dh * xhat, axis=0, keepdims=True)
            dxh = dh * ngb
            dod.append(r * (dxh - xhat * jnp.mean(dxh * xhat, axis=-1, keepdims=True)))
            normed.append(xhat * ngb)
        d_gd = d_dn * jnp.concatenate(normed, axis=1) * _silu_grad(gd, sgd)
        d_gf = d_fox * of * sgf * (1.0 - sgf)
        return jnp.concatenate(dod, axis=1), d_fox * sgf, d_gd, d_gf, dng
    return _rowwise(fn, [(proj, D_DN, EVEN_DN_GATE // D_DN), (proj, D_FOX, EVEN_FOX_GATE // D_FOX), o_dn, o_fox, dom],
                    [ng], [(D_DN, F32), (D_FOX, F32), (D_DN, BF16), (D_FOX, BF16)], [(1, HEAD_DIM)], tile=256,
                    name=name)


def _loss_grad(y, target, name):
    d = y.shape[1]

    def fn(yb, tb):
        diff = yb - tb
        part = jnp.sum(jnp.sum(diff * diff, axis=1, keepdims=True), axis=0, keepdims=True) * (0.5 / d)
        g = diff * (1.0 / d)
        return g, g, part
    return _rowwise(fn, [y, target], [], [(d, F32), (d, BF16)], [(1, 1)], tile=512, name=name)


_REF_EVEN = {"dn_qkv": (0, 1536), "dn_gate": (1536, 2048), "dn_ba": (2048, 2056), "fox_qkv": (2056, 3592),
             "fox_gate": (3592, 4104), "f_pre": (4104, 4108)}
D_IN_EVEN = 4108


def _even_to_kernel_layout(w):
    cut = lambda name: w[..., _REF_EVEN[name][0]:_REF_EVEN[name][1]]
    pad = jnp.zeros(w.shape[:-1] + (EVEN_WIDTH - EVEN_NARROW - 12,), w.dtype)
    return jnp.concatenate([cut("dn_qkv"), cut("fox_qkv"), cut("dn_gate"), cut("fox_gate"), cut("dn_ba"),
                            cut("f_pre"), pad], axis=-1)


def _even_from_kernel_layout(g):
    return jnp.concatenate([g[..., EVEN_DN_QKV:EVEN_FOX_QKV], g[..., EVEN_DN_GATE:EVEN_FOX_GATE],
                            g[..., EVEN_NARROW:EVEN_NARROW + 8], g[..., EVEN_FOX_QKV:EVEN_DN_GATE],
                            g[..., EVEN_FOX_GATE:EVEN_NARROW], g[..., EVEN_NARROW + 8:EVEN_NARROW + 12]], axis=-1)


EVEN_QUARTER = 1027
EVEN_QUARTER_PAD = 1152


def _even_grad_quarters(g):
    g = _even_from_kernel_layout(g)
    pad = [(0, 0)] * (g.ndim - 1) + [(0, EVEN_QUARTER_PAD - EVEN_QUARTER)]
    return jnp.concatenate([jnp.pad(g[..., q * EVEN_QUARTER:(q + 1) * EVEN_QUARTER], pad) for q in range(4)], axis=-1)


def _forget_rows(c):
    return c[:, LANE_FORGET:LANE_FORGET + N_FOX_HEADS].T.reshape(N_FOX_HEADS, 1, c.shape[0])


def _forget_lanes(rows):
    s = rows.shape[2]
    return jnp.pad(rows.reshape(-1, s).T, ((0, 0), (LANE_FORGET, LANES - LANE_FORGET - N_FOX_HEADS)))


def _even_fwd(x, gain, w_in, w_out, j, p, tag):
    h = _rms_fwd(x, gain, f"{tag}_norm")
    proj = _mm(h, w_in, "nn", tm=512, tn=EVEN_WIDTH // 3, out_dtype=F32, name=f"{tag}_in", b_lead=(j,))
    y = _conv_fwd(proj, p["conv_w"], f"{tag}_conv")
    dn_qkv = _dn_pre_fwd(y, f"{tag}_dn_pre")
    act = _narrow_fwd(proj, p["narrow"], f"{tag}_narrow")
    o_dn, states, tinv = _dn_fwd(dn_qkv, act, f"{tag}_delta")
    fox_qkv = _fox_pre_fwd(proj, p["q_g"], p["k_g"], f"{tag}_fox_pre")
    c = _cumsum_rows(act, False, f"{tag}_cumsum")
    ct = _forget_rows(c)
    o_fox, lse = _fox_fwd(fox_qkv, c, ct, f"{tag}_fox")
    om = _mix_gate_fwd(proj, o_dn, o_fox, p["dn_norm_g"], f"{tag}_gate")
    x2 = _mm(om, w_out, "nn", tm=512, tn=x.shape[1], out_dtype=F32, name=f"{tag}_out", residual=x, b_lead=(j,))
    return x2, (x, h, proj, y, dn_qkv, act, states, tinv, o_dn, fox_qkv, c, ct, o_fox, lse, om)


def _even_bwd(dxo, dxo16, saved, gain, w_in, w_out, j, p, tag, g_in, g_out, after=None):
    x, h, proj, y, dn_qkv, act, states, tinv, o_dn, fox_qkv, c, ct, o_fox, lse, om = saved
    d = x.shape[1]
    dom = _mm(dxo16, w_out, "nt", tm=512, tn=d, out_dtype=F32, name=f"{tag}_out_bwd", b_lead=(j,), after=after)
    g_out = _mm(om, dxo16, "tn", tm=512, tn=d, out_dtype=F32, name=f"{tag}_out_dw", into=(g_out, 0))
    d_odn, d_ofox, d_gd, d_gf, d_ng = _mix_gate_bwd(proj, o_dn, o_fox, p["dn_norm_g"], dom, f"{tag}_gate_bwd")
    dq, dk, dv, dct = _fox_bwd(fox_qkv, c, ct, o_fox, lse, d_ofox, f"{tag}_fox_bwd")
    d_fox_qkv, d_qg, d_kg = _fox_pre_bwd(proj, p["q_g"], p["k_g"], dq, dk, dv, f"{tag}_fox_pre_bwd")
    dlogf = _cumsum_rows(_forget_lanes(dct), True, f"{tag}_cumsum_bwd")
    dq, dk, dv, dact = _dn_bwd(dn_qkv, act, states, tinv, d_odn, f"{tag}_delta_bwd")
    dy = _dn_pre_bwd(y, dq, dk, dv, f"{tag}_dn_pre_bwd")
    d_dn_qkv, d_conv = _conv_bwd(proj, p["conv_w"], dy, f"{tag}_conv_bwd")
    d_narrow, s_alog, s_dt, s_fb = _narrow_bwd(proj, p["narrow"], act, dact, dlogf, f"{tag}_narrow_bwd")
    dproj = jnp.concatenate([d_dn_qkv, d_fox_qkv, d_gd, d_gf, d_narrow], axis=1)
    g_in = _mm(h, dproj, "tn", tm=512, tn=EVEN_WIDTH // 3, out_dtype=F32, name=f"{tag}_in_dw", into=(g_in, 0))
    dx, dx16, d_gain = _in_proj_bwd(dproj, w_in, j, x, dxo, gain, f"{tag}_in_bwd")
    small = {"conv_w": d_conv, "a_log": s_alog, "dt_bias": s_dt, "f_bias": s_fb, "dn_norm_g": d_ng, "q_g": d_qg,
             "k_g": d_kg}
    return dx, dx16, d_gain, small, g_in, g_out


def _odd_fwd(x, gain, w_in, w_out, j, tag):
    h = _rms_fwd(x, gain, f"{tag}_norm")
    qkv = _mm(h, w_in, "nn", tm=512, tn=w_in.shape[2] // 2, out_dtype=BF16, name=f"{tag}_in", b_lead=(j,))
    o16, o32 = _sb_fwd(qkv, N_SB_HEADS, f"{tag}_sb")
    x2 = _mm(o16, w_out, "nn", tm=512, tn=x.shape[1], out_dtype=F32, name=f"{tag}_out", residual=x, b_lead=(j,))
    return x2, (x, h, qkv, o16, o32)


def _odd_bwd(dxo, dxo16, saved, gain, w_in, w_out, j, tag, g_in, g_out, after=None):
    x, h, qkv, o16, o32 = saved
    d = x.shape[1]
    do = _mm(dxo16, w_out, "nt", tm=512, tn=d, out_dtype=BF16, name=f"{tag}_out_bwd", b_lead=(j,), after=after)
    g_out = _mm(o16, dxo16, "tn", tm=512, tn=d, out_dtype=F32, name=f"{tag}_out_dw", into=(g_out, 0))
    dq, dk, dv = _sb_bwd(qkv, o32, do, N_SB_HEADS, f"{tag}_sb_bwd")
    dqkv = jnp.concatenate([dq, dk.astype(BF16), dv.astype(BF16)], axis=1)
    g_in = _mm(h, dqkv, "tn", tm=512, tn=w_in.shape[2] // 2, out_dtype=F32, name=f"{tag}_in_dw", into=(g_in, 0))
    dx, dx16, d_gain = _in_proj_bwd(dqkv, w_in, j, x, dxo, gain, f"{tag}_in_bwd")
    return dx, dx16, d_gain, g_in, g_out


def _forward_backward(x, target, w, first, rest_after, token, on_reduced):
    depth = w["norm_ffn1"].shape[0]
    row = lambda a, l: a[l][None]
    rest = {}

    def mats(names, j):
        if j == 0 and names[0] in first:
            return [first[name] for name in names] + [0]
        return [rest[name] for name in names] + [j - (1 if names[0] in first else 0)]

    def even_small(j):
        return {"conv_w": w["dn_conv_w"][j], "narrow": _narrow_params(w["dn_a_log"][j], w["dn_dt_bias"][j],
                                                                     w["fox_f_bias"][j]),
                "dn_norm_g": row(w["dn_norm_g"], j), "q_g": row(w["fox_q_norm_g"], j),
                "k_g": row(w["fox_k_norm_g"], j)}

    saved = []
    for l in range(depth):
        if l == 1:
            rest.update(rest_after(x))
        gain = row(w["norm_ffn1"], l) + token[0:1, 0:1] if l == 0 else row(w["norm_ffn1"], l)
        x, s1 = _ffn_fwd(x, gain, *mats(("ffn1_w_gu", "ffn1_w_down"), l), "ffn1")
        if l % 2 == 0:
            x, s2 = _even_fwd(x, row(w["norm_mix"], l), *mats(("w_in_even", "w_out_even"), l // 2),
                              even_small(l // 2), "even")
        else:
            x, s2 = _odd_fwd(x, row(w["norm_mix"], l), *mats(("w_in_odd", "w_out_odd"), l // 2), "odd")
        x, s3 = _ffn_fwd(x, row(w["norm_ffn2"], l), *mats(("ffn2_w_gu", "ffn2_w_down"), l), "ffn2")
        saved.append((s1, s2, s3))

    dx, dx16, loss = _loss_grad(x, target, "loss")

    kind_of = dict(BIG)
    d_norm = {k: [None] * depth for k in ("norm_ffn1", "norm_mix", "norm_ffn2")}
    d_even = [None] * ((depth + 1) // 2)
    to_sibling, between_chips, token = None, None, None
    for l in reversed(range(depth)):
        s1, s2, s3 = saved[l]
        mixer = ("w_in_even", "w_out_even") if l % 2 == 0 else ("w_in_odd", "w_out_odd")
        names = ["ffn1_w_gu", "ffn1_w_down", *mixer, "ffn2_w_gu", "ffn2_w_down"]
        g = {name: lax.empty((1,) + rest[name].shape[1:], F32) for name in names}
        dx, dx16, d_norm["norm_ffn2"][l], g["ffn2_w_gu"], g["ffn2_w_down"] = _ffn_bwd(
            dx, dx16, s3, row(w["norm_ffn2"], l), *mats(("ffn2_w_gu", "ffn2_w_down"), l), "ffn2", g["ffn2_w_gu"],
            g["ffn2_w_down"], after=token)
        if to_sibling is not None:
            between_chips, token = _reduce_middle(to_sibling, dx)
        if l % 2 == 0:
            dx, dx16, d_norm["norm_mix"][l], d_even[l // 2], g["w_in_even"], g["w_out_even"] = _even_bwd(
                dx, dx16, s2, row(w["norm_mix"], l), *mats(("w_in_even", "w_out_even"), l // 2), even_small(l // 2),
                "even", g["w_in_even"], g["w_out_even"], after=token)
            g["w_in_even"] = _even_grad_quarters(g["w_in_even"])
        else:
            dx, dx16, d_norm["norm_mix"][l], g["w_in_odd"], g["w_out_odd"] = _odd_bwd(
                dx, dx16, s2, row(w["norm_mix"], l), *mats(("w_in_odd", "w_out_odd"), l // 2), "odd", g["w_in_odd"],
                g["w_out_odd"], after=token)
        dx, dx16, d_norm["norm_ffn1"][l], g["ffn1_w_gu"], g["ffn1_w_down"] = _ffn_bwd(
            dx, dx16, s1, row(w["norm_ffn1"], l), *mats(("ffn1_w_gu", "ffn1_w_down"), l), "ffn1", g["ffn1_w_gu"],
            g["ffn1_w_down"])
        to_sibling, token = _reduce_start([g[name] for name in names], [kind_of[name] for name in names], names,
                                          f"layer{l}")
        if between_chips is not None:
            on_reduced(l + 1, dict(zip(between_chips[-2], _reduce_finish(between_chips, dx))))
    between_chips, _ = _reduce_middle(to_sibling, dx)
    on_reduced(0, dict(zip(between_chips[-2], _reduce_finish(between_chips, dx))))

    small = {k: jnp.concatenate(v, axis=0) for k, v in d_norm.items()}
    dec = slice(LANE_DECAY, LANE_DECAY + N_DN_HEADS)
    fgt = slice(LANE_FORGET, LANE_FORGET + N_FOX_HEADS)
    small["dn_conv_w"] = jnp.stack([e["conv_w"] for e in d_even])
    small["dn_a_log"] = jnp.concatenate([e["a_log"][:, dec] for e in d_even], axis=0)
    small["dn_dt_bias"] = jnp.concatenate([e["dt_bias"][:, dec] for e in d_even], axis=0)
    small["fox_f_bias"] = jnp.concatenate([e["f_bias"][:, fgt] for e in d_even], axis=0)
    small["dn_norm_g"] = jnp.concatenate([e["dn_norm_g"] for e in d_even], axis=0)
    small["fox_q_norm_g"] = jnp.concatenate([e["q_g"] for e in d_even], axis=0)
    small["fox_k_norm_g"] = jnp.concatenate([e["k_g"] for e in d_even], axis=0)
    return loss, dx, small


MESH = pl.DeviceIdType.MESH
ANY = pl.BlockSpec(memory_space=pl.ANY)


def _place():
    x, y, c = lax.axis_index("x"), lax.axis_index("y"), lax.axis_index("c")
    return x, y, c, [(1 - x, y), (x, 1 - y), (1 - x, 1 - y)]


def _remote(src, dst, send_sem, recv_sem, to):
    return pltpu.make_async_remote_copy(src_ref=src, dst_ref=dst, send_sem=send_sem, recv_sem=recv_sem,
                                        device_id=to, device_id_type=MESH)


def _aligned(start, multiple):
    return start if isinstance(start, int) else pl.multiple_of(start, multiple)


def _quarter(ref, kind, chip, half, rows, cols):
    k = 2 * chip[0] + chip[1]
    hr = rows // 2
    assert hr % 16 == 0 and cols % LANES == 0
    if kind == "col":
        return ref.at[:, pl.ds(_aligned(half * hr, 16), hr), pl.ds(_aligned(k * cols, LANES), cols)]
    return ref.at[:, pl.ds(_aligned(k * rows + half * hr, 16), hr), :]


def _place_quarter(shard, kind, kc, name, first=0, count=None):
    l, rows, cols = shard.shape
    l = l - first if count is None else count
    tr = rows
    while tr * cols * 4 > (2 << 20) and tr % 32 == 0:
        tr //= 2
    nr = rows // tr
    if kind == "col":
        out_spec = pl.BlockSpec((None, tr, cols), lambda li, i, kc_ref: (li, i, kc_ref[0]))
        out_shape = (l, rows, 4 * cols)
    else:
        out_spec = pl.BlockSpec((None, tr, cols), lambda li, i, kc_ref: (li, kc_ref[0] * nr + i, 0))
        out_shape = (l, 4 * rows, cols)

    def body(kc_ref, x_ref, o_ref):
        o_ref[...] = x_ref[...].astype(BF16)

    return pl.pallas_call(
        body, name=name,
        grid_spec=pltpu.PrefetchScalarGridSpec(
            num_scalar_prefetch=1, grid=(l, nr),
            in_specs=[pl.BlockSpec((None, tr, cols), lambda li, i, kc_ref: (li + first, i, 0))],
            out_specs=out_spec),
        out_shape=jax.ShapeDtypeStruct(out_shape, BF16),
        compiler_params=_cparams("parallel", "parallel"),
    )(kc, shard)


def _gather_weights(wholes, kinds):
    n = len(wholes)

    def dims(ref, kind):
        _, r, cc = ref.shape
        return (r, cc // 4) if kind == "col" else (r // 4, cc)

    def body(*refs):
        bufs = refs[n:2 * n]
        send_sems, recv_sems = refs[2 * n:]
        x, y, c, chips = _place()
        sibling = (x, y, 1 - c)
        first, passed = [], []
        for t in range(n):
            rows, cols = dims(bufs[t], kinds[t])
            mine = _quarter(bufs[t], kinds[t], (x, y), c, rows, cols)
            for j, chip in enumerate(chips):
                cp = _remote(mine, mine, send_sems.at[t, j], recv_sems.at[t, j], (*chip, c))
                cp.start()
                first.append(cp)
        for j, chip in enumerate(chips):
            for t in range(n):
                rows, cols = dims(bufs[t], kinds[t])
                got = _quarter(bufs[t], kinds[t], chip, c, rows, cols)
                _remote(got, got, send_sems.at[t, j], recv_sems.at[t, j], (*chip, c)).wait_recv()
                cp = _remote(got, got, send_sems.at[t, 3 + j], recv_sems.at[t, 3 + j], sibling)
                cp.start()
                passed.append(cp)
        for j, chip in enumerate(chips):
            for t in range(n):
                rows, cols = dims(bufs[t], kinds[t])
                got = _quarter(bufs[t], kinds[t], chip, 1 - c, rows, cols)
                _remote(got, got, send_sems.at[t, 3 + j], recv_sems.at[t, 3 + j], sibling).wait_recv()
        for cp in first + passed:
            cp.wait_send()

    return pl.pallas_call(
        body, name="gather_weights", in_specs=[ANY] * n, out_specs=[ANY] * n,
        out_shape=[jax.ShapeDtypeStruct(a.shape, a.dtype) for a in wholes],
        input_output_aliases={t: t for t in range(n)},
        scratch_shapes=[pltpu.SemaphoreType.DMA((n, 6)), pltpu.SemaphoreType.DMA((n, 6))],
        compiler_params=pltpu.CompilerParams(has_side_effects=True),
    )(*wholes)


def _quarter_dims(ref, kind):
    _, r, cc = ref.shape
    return (r, cc // 4) if kind == "col" else (r // 4, cc)


def _gather_chips_copies(bufs, sems, kinds):
    x, y, c, chips = _place()
    copies = []
    for t, buf in enumerate(bufs):
        rows, cols = _quarter_dims(buf, kinds[t])
        mine = _quarter(buf, kinds[t], (x, y), c, rows, cols)
        for j, chip in enumerate(chips):
            pair = 2 * (OTHER_CHIPS * t + j)
            copies.append(_remote(mine, mine, sems[pair], sems[pair + 1], (*chip, c)))
    return copies


def _gather_start(wholes, kinds, after, tag):
    n = len(wholes)
    n_sems = 2 * OTHER_CHIPS * n
    n_in = n + len(after)

    def body(*refs):
        for cp in _gather_chips_copies(refs[:n], refs[n_in + n:n_in + n + n_sems], kinds):
            cp.start()
        refs[-1][...] = jnp.zeros_like(refs[-1])

    held = [pltpu.with_memory_space_constraint(a, pltpu.HBM) for a in wholes]
    out = pl.pallas_call(
        body, name=f"gather_start_{tag}", in_specs=[HBM] * n + [ANY] * len(after),
        out_specs=(*[HBM] * n, *[SEM] * n_sems, pl.BlockSpec(memory_space=pltpu.VMEM)),
        out_shape=(*[pltpu.HBM(a.shape, a.dtype) for a in held], *[pltpu.SemaphoreType.DMA(())] * n_sems,
                   jax.ShapeDtypeStruct((8, LANES), F32)),
        input_output_aliases={i: i for i in range(n)},
        compiler_params=pltpu.CompilerParams(has_side_effects=SPLIT_COPY),
    )(*held, *after)
    return out[n:n + n_sems], out[:n], out[-1]


def _gather_wait(sems, wholes, kinds, after, tag):
    n = len(wholes)

    def body(*refs):
        for cp in _gather_chips_copies(refs[:n], refs[n:n + len(sems)], kinds):
            cp.wait_send()
            cp.wait_recv()

    return pl.pallas_call(
        body, name=f"gather_wait_{tag}", in_specs=[HBM] * n + [SEM] * len(sems) + [ANY],
        out_specs=tuple([HBM] * n), out_shape=tuple(pltpu.HBM(a.shape, a.dtype) for a in wholes),
        input_output_aliases={i: i for i in range(n)},
        compiler_params=pltpu.CompilerParams(has_side_effects=SPLIT_COPY),
    )(*wholes, *sems, after)


def _gather_forward(wholes, kinds, tag):
    n = len(wholes)

    def body(*refs):
        bufs = refs[n:2 * n]
        send_sems, recv_sems = refs[2 * n:]
        x, y, c, chips = _place()
        copies = []
        for t in range(n):
            rows, cols = _quarter_dims(bufs[t], kinds[t])
            for j, chip in enumerate(chips):
                got = _quarter(bufs[t], kinds[t], chip, c, rows, cols)
                cp = _remote(got, got, send_sems.at[t, j], recv_sems.at[t, j], (x, y, 1 - c))
                cp.start()
                copies.append(cp)
        for cp in copies:
            cp.wait_send()
        for t in range(n):
            rows, cols = _quarter_dims(bufs[t], kinds[t])
            for j, chip in enumerate(chips):
                got = _quarter(bufs[t], kinds[t], chip, 1 - c, rows, cols)
                _remote(got, got, send_sems.at[t, j], recv_sems.at[t, j], (x, y, 1 - c)).wait_recv()

    return pl.pallas_call(
        body, name=f"gather_forward_{tag}", in_specs=[ANY] * n, out_specs=[ANY] * n,
        out_shape=[jax.ShapeDtypeStruct(a.shape, a.dtype) for a in wholes],
        input_output_aliases={t: t for t in range(n)},
        scratch_shapes=[pltpu.SemaphoreType.DMA((n, OTHER_CHIPS)), pltpu.SemaphoreType.DMA((n, OTHER_CHIPS))],
        compiler_params=pltpu.CompilerParams(has_side_effects=True),
    )(*wholes)


def _canonical(a, kind):
    l, r, c = a.shape
    return a.reshape(l, 1, r, c) if kind == "col" else a.reshape(l, 4, r // 4, c)


def _add_tile(rows, cols):
    tc = cols if cols <= 1536 else cols // 4
    tr = rows
    while tr * tc * 4 > (1 << 20) and tr % 16 == 0:
        tr //= 2
    return tr, tc


def _rs_add_sibling(part, got, c, name):
    l, a, hr, cols = got.shape
    tr, tc = _add_tile(hr, cols)
    nr = hr // tr

    def body(c_ref, p_ref, g_ref, o32_ref, o16_ref):
        s = p_ref[...] + g_ref[...]
        o32_ref[...] = s
        o16_ref[...] = s.astype(BF16)

    blk = (None, None, tr, tc)
    spec = pl.BlockSpec(blk, lambda li, ai, i, j, c_ref: (li, ai, i, j))
    return pl.pallas_call(
        body, name=name,
        grid_spec=pltpu.PrefetchScalarGridSpec(
            num_scalar_prefetch=1, grid=(l, a, nr, cols // tc),
            in_specs=[pl.BlockSpec(blk, lambda li, ai, i, j, c_ref: (li, ai, c_ref[0] * nr + i, j)), spec],
            out_specs=[spec, spec]),
        out_shape=[jax.ShapeDtypeStruct(got.shape, F32), jax.ShapeDtypeStruct(got.shape, BF16)],
        compiler_params=_cparams("parallel", "parallel", "parallel", "parallel"),
    )(c, part, got)


def _quarter4(ref, kind, chip, cols):
    k = 2 * chip[0] + chip[1]
    if kind == "col":
        return ref.at[:, :, :, pl.ds(pl.multiple_of(k * cols, LANES), cols)]
    return ref.at[:, pl.ds(k, 1), :, :]


HBM = pl.BlockSpec(memory_space=pltpu.HBM)
SEM = pl.BlockSpec(memory_space=pltpu.SEMAPHORE)
SPLIT_COPY = pltpu.SideEffectType.DATAFLOW_SIDE_EFFECTING
OTHER_CHIPS = 3


def _quarter4_shape(a, kind):
    l, _, hr, cols = a.shape
    return (l, 1, hr, cols // 4 if kind == "col" else cols)


def _rs_chips_copies(srcs, lands, sems, kinds):
    x, y, c, chips = _place()
    copies = []
    for t, (src, land) in enumerate(zip(srcs, lands)):
        cols = _quarter4_shape(src, kinds[t])[3]
        for j, chip in enumerate(chips):
            pair = 2 * (OTHER_CHIPS * t + j)
            copies.append(_remote(_quarter4(src, kinds[t], chip, cols), land.at[j], sems[pair], sems[pair + 1],
                                  (*chip, c)))
    return copies


def _split_start(copies, srcs, lands, n_sems, name):
    n = len(srcs)

    def body(*refs):
        for cp in copies(refs[:n], refs[n:2 * n], refs[4 * n:4 * n + n_sems]):
            cp.start()
        refs[-1][...] = jnp.zeros_like(refs[-1])

    held = [pltpu.with_memory_space_constraint(a, pltpu.HBM) for a in (*srcs, *lands)]
    out = pl.pallas_call(
        body, name=name, in_specs=[HBM] * (2 * n),
        out_specs=(*[HBM] * (2 * n), *[SEM] * n_sems, pl.BlockSpec(memory_space=pltpu.VMEM)),
        out_shape=(*[pltpu.HBM(a.shape, a.dtype) for a in held], *[pltpu.SemaphoreType.DMA(())] * n_sems,
                   jax.ShapeDtypeStruct((8, LANES), F32)),
        input_output_aliases={i: i for i in range(2 * n)},
        compiler_params=pltpu.CompilerParams(has_side_effects=SPLIT_COPY),
    )(*held)
    return out[2 * n:2 * n + n_sems], out[:n], out[n:2 * n], out[-1]


def _split_wait(copies, sems, srcs, lands, after, name):
    n = len(srcs)

    def body(*refs):
        for cp in copies(refs[:n], refs[n:2 * n], refs[2 * n:2 * n + len(sems)]):
            cp.wait_send()
            cp.wait_recv()

    out = pl.pallas_call(
        body, name=name, in_specs=[HBM] * (2 * n) + [SEM] * len(sems) + [ANY],
        out_specs=tuple([HBM] * (2 * n)),
        out_shape=tuple(pltpu.HBM(a.shape, a.dtype) for a in (*srcs, *lands)),
        input_output_aliases={i: i for i in range(2 * n)},
        compiler_params=pltpu.CompilerParams(has_side_effects=SPLIT_COPY),
    )(*srcs, *lands, *sems, after)
    return out[:n], out[n:]


def _rs_sibling_copies(srcs, lands, sems):
    x, y, c, _ = _place()
    copies = []
    for t, (src, land) in enumerate(zip(srcs, lands)):
        hr = src.shape[2] // 2
        gives = src.at[:, :, pl.ds(pl.multiple_of((1 - c) * hr, 8), hr), :]
        copies.append(_remote(gives, land, sems[2 * t], sems[2 * t + 1], (x, y, 1 - c)))
    return copies


def _rs_add_chips(sum32, got, kind, kc, name):
    _, l, _, hr, cols = got.shape
    tr, _ = _add_tile(hr, cols)
    nr = hr // tr
    k_arr, c_arr = kc
    if kind == "col":
        own = pl.BlockSpec((None, None, tr, cols), lambda li, i, k_ref, c_ref: (li, 0, i, k_ref[0]))
    else:
        own = pl.BlockSpec((None, None, tr, cols), lambda li, i, k_ref, c_ref: (li, k_ref[0], i, 0))

    def body(k_ref, c_ref, own_ref, got_ref, o_ref):
        o_ref[...] = ((own_ref[...] + got_ref[0].astype(F32)) + got_ref[1].astype(F32)) + got_ref[2].astype(F32)

    return pl.pallas_call(
        body, name=name,
        grid_spec=pltpu.PrefetchScalarGridSpec(
            num_scalar_prefetch=2, grid=(l, nr),
            in_specs=[own, pl.BlockSpec((3, None, None, tr, cols), lambda li, i, k_ref, c_ref: (0, li, 0, i, 0))],
            out_specs=pl.BlockSpec((None, tr, cols), lambda li, i, k_ref, c_ref: (li, c_ref[0] * nr + i, 0))),
        out_shape=jax.ShapeDtypeStruct((l, 2 * hr, cols), F32),
        compiler_params=_cparams("parallel", "parallel"),
    )(k_arr, c_arr, sum32, got)


def _rs_finish(quarters):
    n = len(quarters)

    def body(*refs):
        bufs = refs[n:2 * n]
        send_sems, recv_sems = refs[2 * n:]
        x, y, c, _ = _place()
        copies = []
        for t in range(n):
            hr = bufs[t].shape[1] // 2
            mine = bufs[t].at[:, pl.ds(pl.multiple_of(c * hr, 8), hr), :]
            cp = _remote(mine, mine, send_sems.at[t], recv_sems.at[t], (x, y, 1 - c))
            cp.start()
            copies.append(cp)
        for cp in copies:
            cp.wait()

    return pl.pallas_call(
        body, name="reduce_finish", in_specs=[ANY] * n, out_specs=[ANY] * n,
        out_shape=[jax.ShapeDtypeStruct(a.shape, a.dtype) for a in quarters],
        input_output_aliases={t: t for t in range(n)},
        scratch_shapes=[pltpu.SemaphoreType.DMA((n,)), pltpu.SemaphoreType.DMA((n,))],
        compiler_params=pltpu.CompilerParams(has_side_effects=True),
    )(*quarters)


def _reduce_start(parts, kinds, names, tag):
    canon = [_canonical(p, kind) for p, kind in zip(parts, kinds)]
    lands = [lax.empty(a.shape[:2] + (a.shape[2] // 2, a.shape[3]), a.dtype) for a in canon]
    sems, srcs, lands, token = _split_start(_rs_sibling_copies, canon, lands, 2 * len(canon),
                                            f"reduce_sibling_start_{tag}")
    return (sems, srcs, lands, kinds, names, tag), token


def _reduce_middle(state, after):
    sems, srcs, lands, kinds, names, tag = state
    c_arr = jnp.reshape(lax.axis_index("c"), (1,)).astype(jnp.int32)
    srcs, from_sibling = _split_wait(_rs_sibling_copies, sems, srcs, lands, after, f"reduce_sibling_wait_{tag}")
    sums = [_rs_add_sibling(p, g, c_arr, f"reduce_add_sibling_{nm}") for p, g, nm in zip(srcs, from_sibling, names)]
    sums16 = [s16 for _, s16 in sums]
    copies = functools.partial(_rs_chips_copies, kinds=kinds)
    lands = [lax.empty((OTHER_CHIPS,) + _quarter4_shape(a, k), a.dtype) for a, k in zip(sums16, kinds)]
    sems, srcs, lands, token = _split_start(copies, sums16, lands, 2 * OTHER_CHIPS * len(sums16),
                                            f"reduce_chips_start_{tag}")
    return (sems, srcs, lands, [s32 for s32, _ in sums], kinds, names, tag), token


def _reduce_finish(state, after):
    sems, srcs, lands, sums32, kinds, names, tag = state
    x, y, c = lax.axis_index("x"), lax.axis_index("y"), lax.axis_index("c")
    kc = (jnp.reshape(2 * x + y, (1,)).astype(jnp.int32), jnp.reshape(c, (1,)).astype(jnp.int32))
    copies = functools.partial(_rs_chips_copies, kinds=kinds)
    _, from_chips = _split_wait(copies, sems, srcs, lands, after, f"reduce_chips_wait_{tag}")
    halves = [_rs_add_chips(s32, g, kind, kc, f"reduce_add_chips_{nm}")
              for s32, g, kind, nm in zip(sums32, from_chips, kinds, names)]
    return _rs_finish(halves)


SMALL_PEERS = 7


def _small_exchange(pack):
    rows = pack.shape[0]

    def body(p_ref, slots_ref, total_ref, send_sems, recv_sems):
        x, y, c, _ = _place()
        me = 4 * x + 2 * y + c
        slots_ref[me] = p_ref[...]
        copies = []
        for p in range(1, SMALL_PEERS + 1):
            px, py, pc = (p >> 2) & 1, (p >> 1) & 1, p & 1
            peer = (1 - x if px else x, 1 - y if py else y, 1 - c if pc else c)
            cp = _remote(p_ref, slots_ref.at[me], send_sems.at[p - 1], recv_sems.at[p - 1], peer)
            cp.start()
            copies.append(cp)
        for cp in copies:
            cp.wait()
        total = slots_ref[0]
        for i in range(1, SMALL_PEERS + 1):
            total = total + slots_ref[i]
        total_ref[...] = total

    vmem = pl.BlockSpec(memory_space=pltpu.VMEM)
    return pl.pallas_call(
        body, name="small_exchange", in_specs=[vmem], out_specs=[vmem, vmem],
        out_shape=[jax.ShapeDtypeStruct((SMALL_PEERS + 1, rows, LANES), F32), jax.ShapeDtypeStruct((rows, LANES), F32)],
        scratch_shapes=[pltpu.SemaphoreType.DMA((SMALL_PEERS,)), pltpu.SemaphoreType.DMA((SMALL_PEERS,))],
        compiler_params=pltpu.CompilerParams(has_side_effects=True),
    )(pack)


def _pack(arrays):
    rows = []
    for a in arrays:
        flat = a.reshape(-1).astype(F32)
        rows.append(jnp.pad(flat, (0, (-flat.shape[0]) % LANES)).reshape(-1, LANES))
    out = jnp.concatenate(rows, axis=0)
    return jnp.pad(out, ((0, (-out.shape[0]) % 8), (0, 0)))


def _unpack(pack, shapes):
    out, r = [], 0
    for sh in shapes:
        size = math.prod(sh)
        nr = -(-size // LANES)
        out.append(pack[r:r + nr].reshape(-1)[:size].reshape(sh))
        r += nr
    return out


def _adamw(w, g, m, v, name):
    shape = w.shape
    to2d = lambda a: a.reshape(-1, shape[-1])
    rows = math.prod(shape[:-1])
    tile = 256 if rows % 256 == 0 else rows

    def fn(wb, gb, mb, vb):
        m2 = ADAM_B1 * mb + (1.0 - ADAM_B1) * gb
        v2 = ADAM_B2 * vb + (1.0 - ADAM_B2) * (gb * gb)
        m_hat = m2 / (1.0 - ADAM_B1 ** ADAM_STEP)
        v_hat = v2 / (1.0 - ADAM_B2 ** ADAM_STEP)
        return -ADAM_LR * (m_hat / (jnp.sqrt(v_hat) + ADAM_EPS) + ADAM_WD * wb), m2, v2

    res = _rowwise(fn, [to2d(w), to2d(g), to2d(m), to2d(v)], [], [(shape[-1], F32)] * 3, [], tile=tile, name=name)
    return [r.reshape(shape) for r in res]


def _adamw_layer(w, g, m, v, layer, outs, name):
    _, rows, cols = w.shape
    tile = rows
    while tile * cols * 4 > (2 << 20) and tile % 16 == 0:
        tile //= 2

    def body(w_ref, g_ref, m_ref, v_ref, *rest):
        g_out, d_out, m_out, v_out = rest[-4:]
        gb = g_ref[...]
        m2 = ADAM_B1 * m_ref[...] + (1.0 - ADAM_B1) * gb
        v2 = ADAM_B2 * v_ref[...] + (1.0 - ADAM_B2) * (gb * gb)
        m_hat = m2 / (1.0 - ADAM_B1 ** ADAM_STEP)
        v_hat = v2 / (1.0 - ADAM_B2 ** ADAM_STEP)
        g_out[...] = gb
        d_out[...] = -ADAM_LR * (m_hat / (jnp.sqrt(v_hat) + ADAM_EPS) + ADAM_WD * w_ref[...])
        m_out[...] = m2
        v_out[...] = v2

    stacked = pl.BlockSpec((None, tile, cols), lambda i: (layer, i, 0))
    return pl.pallas_call(
        body, name=name, grid=(rows // tile,),
        in_specs=[stacked, pl.BlockSpec((None, tile, cols), lambda i: (0, i, 0)), stacked, stacked] + [ANY] * 4,
        out_specs=[stacked] * 4, out_shape=[jax.ShapeDtypeStruct(w.shape, F32)] * 4,
        input_output_aliases={4 + i: i for i in range(4)}, compiler_params=_cparams("parallel"),
    )(w, g, m, v, *outs)


BIG = (("ffn1_w_gu", "col"), ("ffn1_w_down", "row"), ("w_in_even", "col"), ("w_out_even", "row"),
       ("w_in_odd", "col"), ("w_out_odd", "row"), ("ffn2_w_gu", "col"), ("ffn2_w_down", "row"))
SMALL = ("norm_ffn1", "norm_mix", "dn_conv_w", "dn_a_log", "dn_dt_bias", "dn_norm_g", "fox_q_norm_g", "fox_k_norm_g",
         "fox_f_bias", "norm_ffn2")
WEIGHTS = ("norm_ffn1", "ffn1_w_gu", "ffn1_w_down", "norm_mix", "w_in_even", "dn_conv_w", "dn_a_log", "dn_dt_bias",
           "dn_norm_g", "fox_q_norm_g", "fox_k_norm_g", "fox_f_bias", "w_out_even", "w_in_odd", "w_out_odd",
           "norm_ffn2", "ffn2_w_gu", "ffn2_w_down")


def _step(x, target, w, m, v):
    k = 2 * lax.axis_index("x") + lax.axis_index("y")
    n_conv = w["dn_conv_w"].shape[2]

    kc = jnp.reshape(k, (1,)).astype(jnp.int32)
    kinds = dict(BIG)
    quarters = {name: w[name] for name in kinds}
    quarters["w_in_even"] = jnp.pad(w["w_in_even"], ((0, 0), (0, 0), (0, EVEN_QUARTER_PAD - EVEN_QUARTER)))
    first_names = [name for name in kinds if name not in ("w_in_odd", "w_out_odd")]
    rest_names = list(kinds)

    def even_columns(whole):
        padded = whole["w_in_even"]
        ref_order = jnp.concatenate([padded[..., q * EVEN_QUARTER_PAD:q * EVEN_QUARTER_PAD + EVEN_QUARTER]
                                     for q in range(4)], axis=-1)
        return {**whole, "w_in_even": _even_to_kernel_layout(ref_order)}

    conv_slots, _ = _small_exchange(_pack([w["dn_conv_w"]]))
    placed = [_place_quarter(quarters[name], kinds[name], kc, f"place_first_{name}", 0, 1) for name in first_names]
    gathered = _gather_weights(placed, [kinds[name] for name in first_names])
    first = even_columns(dict(zip(first_names, gathered)))
    placed = [_place_quarter(quarters[name], kinds[name], kc, f"place_rest_{name}", 1 if name in first_names else 0)
              for name in rest_names]
    rest_kinds = [kinds[name] for name in rest_names]
    sems, on_their_way, token = _gather_start(placed, rest_kinds, [conv_slots, *gathered], "rest")

    def rest_after(value):
        landed = _gather_wait(sems, on_their_way, rest_kinds, value, "rest")
        return even_columns(dict(zip(rest_names, _gather_forward(landed, rest_kinds, "rest"))))

    whole = {}
    conv_rows = math.prod(w["dn_conv_w"].shape) // LANES
    conv_quarters = [conv_slots[2 * q, :conv_rows].reshape(w["dn_conv_w"].shape) for q in range(4)]
    whole["dn_conv_w"] = jnp.concatenate(conv_quarters, axis=-1)
    for name in SMALL:
        if name != "dn_conv_w":
            whole[name] = w[name]

    updated = {name: [lax.empty(w[name].shape, F32) for _ in range(4)] for name in kinds}

    def on_reduced(layer, layer_grads):
        for name, g in layer_grads.items():
            if name == "w_in_even":
                g = g[..., :EVEN_QUARTER]
            stacked_layer = layer if w[name].shape[0] == w["norm_mix"].shape[0] else layer // 2
            updated[name] = _adamw_layer(w[name], g, m[name], v[name], stacked_layer, updated[name], f"adamw_{name}")

    loss, dx, small = _forward_backward(x, target, whole, first, rest_after, token, on_reduced)

    _, small_sum = _small_exchange(_pack([small[n] for n in SMALL]))
    grads = dict(zip(SMALL, _unpack(small_sum, [small[n].shape for n in SMALL])))
    grads["dn_conv_w"] = lax.dynamic_slice_in_dim(grads["dn_conv_w"], k * n_conv, n_conv, axis=2)
    delta, new_m, new_v = {}, {}, {}
    for name in kinds:
        grads[name], delta[name], new_m[name], new_v[name] = updated[name]
    packs = [_pack([d[n] for n in SMALL]) for d in (w, grads, m, v)]
    shapes = [w[n].shape for n in SMALL]
    for out, res in zip((delta, new_m, new_v), _adamw(*packs, "adamw_small")):
        out.update(zip(SMALL, _unpack(res, shapes)))
    total_loss = lax.psum(loss[0, 0], ("x", "y", "c"))
    return total_loss, dx, grads, delta, new_m, new_v


def kernel(x, norm_ffn1, ffn1_w_gu, ffn1_w_down, norm_mix, w_in_even, dn_conv_w, dn_a_log, dn_dt_bias, dn_norm_g, fox_q_norm_g, fox_k_norm_g, fox_f_bias, w_out_even, w_in_odd, w_out_odd, norm_ffn2, ffn2_w_gu, ffn2_w_down, loss_target, m_norm_ffn1, m_ffn1_w_gu, m_ffn1_w_down, m_norm_mix, m_w_in_even, m_dn_conv_w, m_dn_a_log, m_dn_dt_bias, m_dn_norm_g, m_fox_q_norm_g, m_fox_k_norm_g, m_fox_f_bias, m_w_out_even, m_w_in_odd, m_w_out_odd, m_norm_ffn2, m_ffn2_w_gu, m_ffn2_w_down, v_norm_ffn1, v_ffn1_w_gu, v_ffn1_w_down, v_norm_mix, v_w_in_even, v_dn_conv_w, v_dn_a_log, v_dn_dt_bias, v_dn_norm_g, v_fox_q_norm_g, v_fox_k_norm_g, v_fox_f_bias, v_w_out_even, v_w_in_odd, v_w_out_odd, v_norm_ffn2, v_ffn2_w_gu, v_ffn2_w_down):
    w = dict(zip(WEIGHTS, (norm_ffn1, ffn1_w_gu, ffn1_w_down, norm_mix, w_in_even, dn_conv_w, dn_a_log, dn_dt_bias,
                           dn_norm_g, fox_q_norm_g, fox_k_norm_g, fox_f_bias, w_out_even, w_in_odd, w_out_odd,
                           norm_ffn2, ffn2_w_gu, ffn2_w_down)))
    m = dict(zip(WEIGHTS, (m_norm_ffn1, m_ffn1_w_gu, m_ffn1_w_down, m_norm_mix, m_w_in_even, m_dn_conv_w, m_dn_a_log,
                           m_dn_dt_bias, m_dn_norm_g, m_fox_q_norm_g, m_fox_k_norm_g, m_fox_f_bias, m_w_out_even,
                           m_w_in_odd, m_w_out_odd, m_norm_ffn2, m_ffn2_w_gu, m_ffn2_w_down)))
    v = dict(zip(WEIGHTS, (v_norm_ffn1, v_ffn1_w_gu, v_ffn1_w_down, v_norm_mix, v_w_in_even, v_dn_conv_w, v_dn_a_log,
                           v_dn_dt_bias, v_dn_norm_g, v_fox_q_norm_g, v_fox_k_norm_g, v_fox_f_bias, v_w_out_even,
                           v_w_in_odd, v_w_out_odd, v_norm_ffn2, v_ffn2_w_gu, v_ffn2_w_down)))
    loss, dx, grads, delta, new_m, new_v = _step(x[0], loss_target[0], w, m, v)
    return (loss, dx[None], *[grads[n] for n in WEIGHTS], *[delta[n] for n in WEIGHTS],
            *[new_m[n] for n in WEIGHTS], *[new_v[n] for n in WEIGHTS])
```

```python
import functools
import math

import jax
import jax.numpy as jnp
from jax import lax
from jax.experimental import pallas as pl
from jax.experimental.pallas import tpu as pltpu

F32 = jnp.float32
BF16 = jnp.bfloat16
HI = lax.Precision.HIGH

HEAD_DIM = 128
N_DN_HEADS = 4
N_FOX_HEADS = 4
N_SB_HEADS = 8
D_DN = N_DN_HEADS * HEAD_DIM
D_FOX = N_FOX_HEADS * HEAD_DIM
CONV_WIDTH = 4
DN_CHUNK = 64
EPS = 1e-6
ATT_SCALE = HEAD_DIM ** -0.5
ADAM_LR, ADAM_B1, ADAM_B2, ADAM_EPS, ADAM_WD, ADAM_STEP = 0.001, 0.9, 0.999, 1e-08, 0.01, 10

V7X_VMEM_LIMIT = 56 * 1024 * 1024
LANES = 128
ATT_TQ = 512
ATT_TK = 256
ATT_SUB = ATT_TQ // ATT_TK

LANE_BETA, LANE_DECAY, LANE_FORGET = 0, 4, 8


def _cparams(*sem):
    return pltpu.CompilerParams(dimension_semantics=sem, vmem_limit_bytes=V7X_VMEM_LIMIT)


def _sigmoid(x):
    return 1.0 / (1.0 + jnp.exp(-x))


def _softplus(x):
    return jnp.maximum(x, 0.0) + jnp.log(1.0 + jnp.exp(-jnp.abs(x)))


def _silu_grad(y, sg):
    return sg * (1.0 + y * (1.0 - sg))


def _rowwise(fn, rows, bcast, outs, sums, *, tile, name):
    rows = [r if isinstance(r, tuple) else (r, r.shape[1], 0) for r in rows]
    s = rows[0][0].shape[0]
    assert s % tile == 0
    n_in, n_b, n_out, n_sum = len(rows), len(bcast), len(outs), len(sums)

    def body(*refs):
        ins = [r[...] for r in refs[:n_in + n_b]]
        res = fn(*ins)
        if not isinstance(res, (tuple, list)):
            res = (res,)
        out_refs = refs[n_in + n_b:n_in + n_b + n_out]
        sum_refs = refs[n_in + n_b + n_out:]
        for o_ref, val in zip(out_refs, res[:n_out]):
            o_ref[...] = val.astype(o_ref.dtype)
        if n_sum:
            @pl.when(pl.program_id(0) == 0)
            def _():
                for s_ref in sum_refs:
                    s_ref[...] = jnp.zeros_like(s_ref)
            for s_ref, val in zip(sum_refs, res[n_out:]):
                s_ref[...] += val

    in_specs = [pl.BlockSpec((tile, w), lambda i, cb=cb: (i, cb)) for _, w, cb in rows]
    in_specs += [pl.BlockSpec(b.shape, lambda i, nd=b.ndim: (0,) * nd) for b in bcast]
    out_specs = [pl.BlockSpec((tile, c), lambda i: (i, 0)) for c, _ in outs]
    out_specs += [pl.BlockSpec(sh, lambda i: (0, 0)) for sh in sums]
    out_shape = [jax.ShapeDtypeStruct((s, c), dt) for c, dt in outs]
    out_shape += [jax.ShapeDtypeStruct(sh, F32) for sh in sums]
    return pl.pallas_call(
        body, name=name, grid=(s // tile,), in_specs=in_specs, out_specs=out_specs, out_shape=out_shape,
        compiler_params=_cparams("arbitrary" if n_sum else "parallel"),
    )(*[r[0] for r in rows], *bcast)


def _rms_fwd(x, gain, name):
    def fn(xb, g):
        r = lax.rsqrt(jnp.mean(xb * xb, axis=-1, keepdims=True) + EPS)
        return (xb * r * g,)
    return _rowwise(fn, [x], [gain], [(x.shape[1], BF16)], [], tile=512, name=name)[0]


_DIMS = {"nn": (((1,), (0,)), ((), ())), "nt": (((1,), (1,)), ((), ())), "tn": (((0,), (0,)), ((), ()))}


def _dot(a, b, kind):
    return lax.dot_general(a.astype(BF16), b.astype(BF16), _DIMS[kind], preferred_element_type=F32)


def _dot32(a, b, kind="nn"):
    return lax.dot_general(a, b, _DIMS[kind], precision=HI, preferred_element_type=F32)


def _mm(a, b, kind, *, tm, tn, out_dtype, name, scale=None, residual=None, a_lead=(), b_lead=(),
        b_spec=None, n=None, into=None, after=None):
    ash, bsh = a.shape[len(a_lead):], b.shape[len(b_lead):]
    m = ash[1] if kind == "tn" else ash[0]
    k = ash[0] if kind == "tn" else ash[1]
    if b_spec is None:
        n = bsh[0] if kind == "nt" else bsh[1]
        assert k == (bsh[1] if kind == "nt" else bsh[0]), (ash, bsh, kind)
    assert m % tm == 0 and n % tn == 0, (m, tm, n, tn)
    la, lb = (None,) * len(a_lead), (None,) * len(b_lead)
    if kind == "tn":
        a_spec = pl.BlockSpec(la + (k, tm), lambda j, i: a_lead + (0, i))
    else:
        a_spec = pl.BlockSpec(la + (tm, k), lambda j, i: a_lead + (i, 0))
    if b_spec is None:
        if kind == "nt":
            b_spec = pl.BlockSpec(lb + (tn, k), lambda j, i: b_lead + (j, 0))
        else:
            b_spec = pl.BlockSpec(lb + (k, tn), lambda j, i: b_lead + (0, j))
    in_specs, args = [a_spec, b_spec], [a, b]
    if residual is not None:
        in_specs.append(pl.BlockSpec((tm, tn), lambda j, i: (i, j)))
        args.append(residual)
    aliases = {}
    if after is not None:
        in_specs.append(pl.BlockSpec(memory_space=pl.ANY))
        args.append(after)
    if into is not None:
        buf, layer = into
        in_specs.append(pl.BlockSpec(memory_space=pl.ANY))
        args.append(buf)
        aliases = {len(args) - 1: 0}
        out_spec = pl.BlockSpec((None, tm, tn), lambda j, i: (layer, i, j))
        out_shape = jax.ShapeDtypeStruct(buf.shape, buf.dtype)
    else:
        out_spec = pl.BlockSpec((tm, tn), lambda j, i: (i, j))
        out_shape = jax.ShapeDtypeStruct((m, n), out_dtype)

    def body(a_ref, b_ref, *rest):
        acc = _dot(a_ref[...], b_ref[...], kind)
        if scale is not None:
            acc = acc * scale
        if residual is not None:
            acc = acc + rest[0][...]
        rest[-1][...] = acc.astype(rest[-1].dtype)

    return pl.pallas_call(
        body, name=name, grid=(n // tn, m // tm), in_specs=in_specs, out_specs=out_spec, out_shape=out_shape,
        input_output_aliases=aliases, compiler_params=_cparams("parallel", "parallel"),
    )(*args)


FFN_TM = 1024


def _ffn_up(x, gain, w_gu, layer, name):
    s, d = x.shape
    f = w_gu.shape[2] // 2
    tm, tn = FFN_TM, f // 2
    nj = f // tn

    def body(x_ref, gain_ref, wg_ref, wu_ref, gu_ref, a_ref, n_ref):
        xb = x_ref[...]
        r = lax.rsqrt(jnp.mean(xb * xb, axis=-1, keepdims=True) + EPS)
        nv = (xb * r * gain_ref[...]).astype(BF16)
        n_ref[...] = nv
        g = _dot(nv, wg_ref[...], "nn")
        u = _dot(nv, wu_ref[...], "nn")
        gu_ref[0] = g.astype(BF16)
        gu_ref[1] = u.astype(BF16)
        a_ref[...] = (g * _sigmoid(g) * u).astype(BF16)

    return pl.pallas_call(
        body, name=name, grid=(nj, s // tm),
        in_specs=[pl.BlockSpec((tm, d), lambda j, i: (i, 0)),
                  pl.BlockSpec((1, d), lambda j, i: (0, 0)),
                  pl.BlockSpec((None, d, tn), lambda j, i: (layer, 0, j)),
                  pl.BlockSpec((None, d, tn), lambda j, i: (layer, 0, j + nj))],
        out_specs=[pl.BlockSpec((2, tm, tn), lambda j, i: (0, i, j)),
                   pl.BlockSpec((tm, tn), lambda j, i: (i, j)),
                   pl.BlockSpec((None, tm, d), lambda j, i: (j, i, 0))],
        out_shape=[jax.ShapeDtypeStruct((2, s, f), BF16), jax.ShapeDtypeStruct((s, f), BF16),
                   jax.ShapeDtypeStruct((nj, s, d), BF16)],
        compiler_params=_cparams("parallel", "parallel"),
    )(x, gain, w_gu, w_gu)


def _ffn_down_bwd(dxo, w_down, gu, layer, name, after=None):
    s, d = dxo.shape
    f = w_down.shape[1]
    tm, tn = FFN_TM, f // 2
    extra_specs, extra = ([ANY], [after]) if after is not None else ([], [])

    def body(dx_ref, w_ref, gu_ref, *rest):
        dgu_ref = rest[-1]
        da = 0.5 * _dot(dx_ref[...], w_ref[...], "nt")
        g = gu_ref[0].astype(F32)
        u = gu_ref[1].astype(F32)
        sg = _sigmoid(g)
        dgu_ref[0] = (da * u * _silu_grad(g, sg)).astype(BF16)
        dgu_ref[1] = (da * g * sg).astype(BF16)

    return pl.pallas_call(
        body, name=name, grid=(f // tn, s // tm),
        in_specs=[pl.BlockSpec((tm, d), lambda j, i: (i, 0)),
                  pl.BlockSpec((None, tn, d), lambda j, i: (layer, j, 0)),
                  pl.BlockSpec((2, tm, tn), lambda j, i: (0, i, j))] + extra_specs,
        out_specs=pl.BlockSpec((2, tm, tn), lambda j, i: (0, i, j)),
        out_shape=jax.ShapeDtypeStruct((2, s, f), BF16),
        compiler_params=_cparams("parallel", "parallel"),
    )(dxo, w_down, gu, *extra)


NORM_BWD_TM = 256


def _norm_bwd_after(terms, operands, specs, x, dres, gain, name):
    s, d = x.shape
    tm = NORM_BWD_TM
    n_op = len(operands)

    def body(*refs):
        x_ref, dres_ref, g_ref = refs[n_op:n_op + 3]
        dx_ref, dx16_ref, dgain_ref = refs[n_op + 3:]
        dn = None
        for a, b in terms(*refs[:n_op]):
            dn = _dot(a, b, "nt") if dn is None else dn + _dot(a, b, "nt")
        xb = x_ref[...]
        r = lax.rsqrt(jnp.mean(xb * xb, axis=-1, keepdims=True) + EPS)
        xh = xb * r
        dxh = dn * g_ref[...]
        dx = dres_ref[...] + r * (dxh - xh * jnp.mean(dxh * xh, axis=-1, keepdims=True))
        dx_ref[...] = dx
        dx16_ref[...] = dx.astype(BF16)

        @pl.when(pl.program_id(0) == 0)
        def _():
            dgain_ref[...] = jnp.zeros_like(dgain_ref)
        dgain_ref[...] += jnp.sum(dn * xh, axis=0, keepdims=True)

    rows = pl.BlockSpec((tm, d), lambda i: (i, 0))
    return pl.pallas_call(
        body, name=name, grid=(s // tm,),
        in_specs=list(specs) + [rows, rows, pl.BlockSpec((1, d), lambda i: (0, 0))],
        out_specs=[rows, rows, pl.BlockSpec((1, d), lambda i: (0, 0))],
        out_shape=[jax.ShapeDtypeStruct((s, d), F32), jax.ShapeDtypeStruct((s, d), BF16),
                   jax.ShapeDtypeStruct((1, d), F32)],
        compiler_params=_cparams("arbitrary"),
    )(*operands, x, dres, gain)


def _ffn_up_bwd(dgu, w_gu, layer, x, dres, gain, name):
    _, s, f = dgu.shape
    d = w_gu.shape[1]
    specs = [pl.BlockSpec((2, NORM_BWD_TM, f), lambda i: (0, i, 0)),
             pl.BlockSpec((None, d, f), lambda i: (layer, 0, 0)),
             pl.BlockSpec((None, d, f), lambda i: (layer, 0, 1))]
    terms = lambda dgu_ref, wg_ref, wu_ref: [(dgu_ref[0], wg_ref[...]), (dgu_ref[1], wu_ref[...])]
    return _norm_bwd_after(terms, [dgu, w_gu, w_gu], specs, x, dres, gain, name)


def _in_proj_bwd(dproj, w_in, j, x, dres, gain, name):
    k = dproj.shape[1]
    d = w_in.shape[1]
    specs = [pl.BlockSpec((NORM_BWD_TM, k), lambda i: (i, 0)), pl.BlockSpec((None, d, k), lambda i: (j, 0, 0))]
    terms = lambda a_ref, b_ref: [(a_ref[...], b_ref[...])]
    return _norm_bwd_after(terms, [dproj, w_in], specs, x, dres, gain, name)


def _ffn_fwd(x, gain, w_gu, w_down, layer, tag):
    gu, a, n = _ffn_up(x, gain, w_gu, layer, f"{tag}_up")
    x2 = _mm(a, w_down, "nn", tm=FFN_TM, tn=x.shape[1], out_dtype=F32, name=f"{tag}_down", scale=0.5, residual=x,
             b_lead=(layer,))
    return x2, (x, n, gu, a)


def _ffn_bwd(dxo, dxo16, saved, gain, w_gu, w_down, layer, tag, g_gu, g_down, after=None):
    x, n, gu, a = saved
    s, f = a.shape
    dgu = _ffn_down_bwd(dxo16, w_down, gu, layer, f"{tag}_down_bwd", after)
    g_down = _mm(a, dxo16, "tn", tm=256, tn=dxo16.shape[1], out_dtype=F32, name=f"{tag}_down_dw", scale=0.5,
                 into=(g_down, 0))
    tn = f // 2
    nj = f // tn
    g_gu = _mm(n, dgu, "tn", tm=512, tn=tn, out_dtype=F32, name=f"{tag}_up_dw", into=(g_gu, 0), n=2 * f, a_lead=(0,),
               b_spec=pl.BlockSpec((None, s, tn), lambda j, i: (j // nj, 0, j % nj)))
    dx, dx16, dgain = _ffn_up_bwd(dgu, w_gu, layer, x, dxo, gain, f"{tag}_up_bwd")
    return dx, dx16, dgain, g_gu, g_down


def _lane_col(blk, lane):
    li = lax.broadcasted_iota(jnp.int32, blk.shape, 1)
    return jnp.sum(jnp.where(li == lane, blk, 0.0), axis=1, keepdims=True)


def _split_dot(x, tri):
    hi = x.astype(BF16)
    lo = (x - hi.astype(F32)).astype(BF16)
    return (lax.dot_general(hi, tri, _DIMS["nn"], preferred_element_type=F32)
            + lax.dot_general(lo, tri, _DIMS["nn"], preferred_element_type=F32))


class _Each:
    def __init__(self, vals):
        self.vals = list(vals)

    def _with(self, other, op):
        others = other.vals if isinstance(other, _Each) else [other] * len(self.vals)
        return _Each(op(a, b) for a, b in zip(self.vals, others))

    def __add__(self, other):
        return self._with(other, lambda a, b: a + b)

    def __sub__(self, other):
        return self._with(other, lambda a, b: a - b)

    def __mul__(self, other):
        return self._with(other, lambda a, b: a * b)

    def __neg__(self):
        return _Each(-a for a in self.vals)


def _each(fn, *args):
    n = max(len(a.vals) for a in args if isinstance(a, _Each))
    res = [fn(*xs) for xs in zip(*[a.vals if isinstance(a, _Each) else [a] * n for a in args])]
    if isinstance(res[0], tuple):
        return tuple(_Each(r) for r in zip(*res))
    return _Each(res)


def _keep(cond, x):
    return _each(lambda v: jnp.where(cond, v, 0.0), x)


def _rowsum(x):
    return _each(lambda v: jnp.sum(v, axis=1, keepdims=True), x)


ATT_HEADS = 2
ATT_WIDTH = ATT_HEADS * HEAD_DIM
_HEAD_COLS = [slice(h * HEAD_DIM, (h + 1) * HEAD_DIM) for h in range(ATT_HEADS)]


def _att_specs(n_heads, s):
    groups = n_heads // ATT_HEADS
    q_spec = pl.BlockSpec((ATT_TQ, ATT_WIDTH), lambda g, i: (i, g))
    k_spec = pl.BlockSpec((s, ATT_WIDTH), lambda g, i: (0, groups + g))
    v_spec = pl.BlockSpec((s, ATT_WIDTH), lambda g, i: (0, 2 * groups + g))
    return q_spec, k_spec, v_spec


def _heads_of(ref, rows=None):
    return _Each(ref[:, cs] if rows is None else ref[rows, cs] for cs in _HEAD_COLS)


def _dot_each(a, b, kind):
    return _each(lambda x, y: _dot(x, y, kind), a, b)


def _att_iotas():
    row = lax.broadcasted_iota(jnp.int32, (ATT_TQ, ATT_TK), 0)
    col = lax.broadcasted_iota(jnp.int32, (ATT_TQ, ATT_TK), 1)
    jr = lax.broadcasted_iota(jnp.int32, (ATT_TK, ATT_TK), 0)
    jc = lax.broadcasted_iota(jnp.int32, (ATT_TK, ATT_TK), 1)
    return row, col, jr, jc


def _sb_fwd(qkv, n_heads, name):
    s = qkv.shape[0]

    def body(q_ref, k_ref, v_ref, o16_ref, o32_ref):
        i = pl.program_id(1)
        q = _heads_of(q_ref)
        row, col, jr, jc = _att_iotas()
        later = (jr > jc).astype(BF16)

        def step(jb, carry, diagonal):
            c_sp, acc = (_Each(part) for part in carry)
            work = []
            for sub in reversed(range(ATT_SUB)):
                keys = pl.ds(pl.multiple_of(jb * ATT_TQ + sub * ATT_TK, ATT_TK), ATT_TK)
                z = _dot_each(q, _heads_of(k_ref, keys), "nt") * ATT_SCALE
                sp = _each(_softplus, z)
                before = (col + sub * ATT_TK) < row if diagonal else None
                spm = _keep(before, sp) if diagonal else sp
                work.append((keys, z - sp, spm, _each(lambda x: _dot(x, later, "nn"), spm), before))
            for keys, logsig, spm, within, before in work:
                a = _each(jnp.exp, logsig - (c_sp + within))
                if diagonal:
                    a = _keep(before, a)
                acc = acc + _each(_split_dot, a, _heads_of(v_ref, keys))
                c_sp = c_sp + _rowsum(spm)
            return tuple(c_sp.vals), tuple(acc.vals)

        zeros = lambda width: tuple(jnp.zeros((ATT_TQ, width), F32) for _ in range(ATT_HEADS))
        carry = step(i, (zeros(1), zeros(HEAD_DIM)), True)
        _, acc = lax.fori_loop(0, i, lambda it, cr: step(i - 1 - it, cr, False), carry)
        for cs, acc_h in zip(_HEAD_COLS, acc):
            o16_ref[:, cs] = acc_h.astype(BF16)
            o32_ref[:, cs] = acc_h

    q_spec, k_spec, v_spec = _att_specs(n_heads, s)
    o_spec = pl.BlockSpec((ATT_TQ, ATT_WIDTH), lambda g, i: (i, g))
    return pl.pallas_call(
        body, name=name, grid=(n_heads // ATT_HEADS, s // ATT_TQ), in_specs=[q_spec, k_spec, v_spec],
        out_specs=[o_spec, o_spec],
        out_shape=[jax.ShapeDtypeStruct((s, n_heads * HEAD_DIM), BF16),
                   jax.ShapeDtypeStruct((s, n_heads * HEAD_DIM), F32)],
        compiler_params=_cparams("parallel", "arbitrary"),
    )(qkv, qkv, qkv)


def _sb_bwd(qkv, o32, do, n_heads, name):
    s = qkv.shape[0]

    def body(q_ref, k_ref, v_ref, o_ref, do_ref, dq_ref, dk_ref, dv_ref):
        i = pl.program_id(1)

        @pl.when(i == 0)
        def _():
            dk_ref[...] = jnp.zeros_like(dk_ref)
            dv_ref[...] = jnp.zeros_like(dv_ref)

        q, do = _heads_of(q_ref), _heads_of(do_ref)
        total = _rowsum(_each(lambda a, b: a.astype(F32) * b, do, _heads_of(o_ref)))
        row, col, jr, jc = _att_iotas()
        later = (jr > jc).astype(BF16)
        not_before = (jr >= jc).astype(BF16)

        def step(jb, carry, diagonal):
            c_sp, c_e, dq = (_Each(part) for part in carry)
            work = []
            for sub in reversed(range(ATT_SUB)):
                keys = pl.ds(pl.multiple_of(jb * ATT_TQ + sub * ATT_TK, ATT_TK), ATT_TK)
                k = _heads_of(k_ref, keys)
                z = _dot_each(q, k, "nt") * ATT_SCALE
                sp = _each(_softplus, z)
                before = (col + sub * ATT_TK) < row if diagonal else None
                spm = _keep(before, sp) if diagonal else sp
                work.append((keys, k, _each(jnp.exp, z - sp), spm, _each(lambda x: _dot(x, later, "nn"), spm),
                             _dot_each(do, _heads_of(v_ref, keys), "nt"), before))
            for keys, k, sig, spm, within, da, before in work:
                a = sig * _each(lambda x: jnp.exp(-x), c_sp + within)
                if diagonal:
                    a = _keep(before, a)
                e = a * da
                left = total - c_e - _each(lambda x: _split_dot(x, not_before), e)
                dz = (e - (e + left) * sig) * ATT_SCALE
                if diagonal:
                    dz = _keep(before, dz)
                dk, dv = _dot_each(dz, q, "tn"), _dot_each(a, do, "tn")
                for cs, dk_h, dv_h in zip(_HEAD_COLS, dk.vals, dv.vals):
                    dk_ref[keys, cs] += dk_h
                    dv_ref[keys, cs] += dv_h
                dq = dq + _dot_each(dz, k, "nn")
                c_sp = c_sp + _rowsum(spm)
                c_e = c_e + _rowsum(e)
            return tuple(c_sp.vals), tuple(c_e.vals), tuple(dq.vals)

        zeros = lambda width: tuple(jnp.zeros((ATT_TQ, width), F32) for _ in range(ATT_HEADS))
        carry = step(i, (zeros(1), zeros(1), zeros(HEAD_DIM)), True)
        _, _, dq = lax.fori_loop(0, i, lambda it, cr: step(i - 1 - it, cr, False), carry)
        for cs, dq_h in zip(_HEAD_COLS, dq):
            dq_ref[:, cs] = dq_h.astype(BF16)

    q_spec, k_spec, v_spec = _att_specs(n_heads, s)
    blk = pl.BlockSpec((ATT_TQ, ATT_WIDTH), lambda g, i: (i, g))
    full = pl.BlockSpec((s, ATT_WIDTH), lambda g, i: (0, g))
    wide = (s, n_heads * HEAD_DIM)
    return pl.pallas_call(
        body, name=name, grid=(n_heads // ATT_HEADS, s // ATT_TQ), in_specs=[q_spec, k_spec, v_spec, blk, blk],
        out_specs=[blk, full, full],
        out_shape=[jax.ShapeDtypeStruct(wide, BF16), jax.ShapeDtypeStruct(wide, F32), jax.ShapeDtypeStruct(wide, F32)],
        compiler_params=_cparams("parallel", "arbitrary"),
    )(qkv, qkv, qkv, o32, do)


def _fox_logits(q, k, cq, ct_ref, keys):
    ck = _Each(ct_ref[h, :, keys] for h in range(ATT_HEADS))
    return _dot_each(q, k, "nt") * ATT_SCALE + (cq - ck)


def _fox_cq(c_ref, group):
    c = c_ref[...]
    return _Each(_lane_col(c, LANE_FORGET + group * ATT_HEADS + h) for h in range(ATT_HEADS))


def _fox_fwd(qkv, c, ct, name):
    s = qkv.shape[0]
    n_heads = N_FOX_HEADS

    def body(q_ref, k_ref, v_ref, c_ref, ct_ref, o_ref, lse_ref):
        g, i = pl.program_id(0), pl.program_id(1)
        q = _heads_of(q_ref)
        cq = _fox_cq(c_ref, g)
        row, col, _, _ = _att_iotas()

        def step(jb, carry, diagonal):
            m, l, acc = (_Each(part) for part in carry)
            work = []
            m_new = m
            for sub in range(ATT_SUB):
                keys = pl.ds(pl.multiple_of(jb * ATT_TQ + sub * ATT_TK, ATT_TK), ATT_TK)
                sc = _fox_logits(q, _heads_of(k_ref, keys), cq, ct_ref, keys)
                valid = (col + sub * ATT_TK) <= row if diagonal else None
                if diagonal:
                    sc = _each(lambda x: jnp.where(valid, x, -1e30), sc)
                m_new = _each(lambda a, x: jnp.maximum(a, jnp.max(x, axis=1, keepdims=True)), m_new, sc)
                work.append((keys, sc, valid))
            w = _each(jnp.exp, m - m_new)
            l, acc = l * w, acc * w
            for keys, sc, valid in work:
                p = _each(jnp.exp, sc - m_new)
                if diagonal:
                    p = _keep(valid, p)
                l = l + _rowsum(p)
                acc = acc + _each(_split_dot, p, _heads_of(v_ref, keys))
            return tuple(m_new.vals), tuple(l.vals), tuple(acc.vals)

        per_head = lambda width, value: tuple(jnp.full((ATT_TQ, width), value, F32) for _ in range(ATT_HEADS))
        init = (per_head(1, -1e30), per_head(1, 0.0), per_head(HEAD_DIM, 0.0))
        m, l, acc = lax.fori_loop(0, i, lambda jb, cr: step(jb, cr, False), step(i, init, True))
        for h, cs in enumerate(_HEAD_COLS):
            o_ref[:, cs] = acc[h] / l[h]
            lse_ref[h] = jnp.broadcast_to(m[h] + jnp.log(l[h]), (ATT_TQ, LANES))

    q_spec, k_spec, v_spec = _att_specs(n_heads, s)
    return pl.pallas_call(
        body, name=name, grid=(n_heads // ATT_HEADS, s // ATT_TQ),
        in_specs=[q_spec, k_spec, v_spec, pl.BlockSpec((ATT_TQ, LANES), lambda g, i: (i, 0)),
                  pl.BlockSpec((ATT_HEADS, 1, s), lambda g, i: (g, 0, 0))],
        out_specs=[pl.BlockSpec((ATT_TQ, ATT_WIDTH), lambda g, i: (i, g)),
                   pl.BlockSpec((ATT_HEADS, ATT_TQ, LANES), lambda g, i: (g, i, 0))],
        out_shape=[jax.ShapeDtypeStruct((s, n_heads * HEAD_DIM), F32),
                   jax.ShapeDtypeStruct((n_heads, s, LANES), F32)],
        compiler_params=_cparams("parallel", "arbitrary"),
    )(qkv, qkv, qkv, c, ct)


def _fox_bwd(qkv, c, ct, o, lse, do, name):
    s = qkv.shape[0]
    n_heads = N_FOX_HEADS

    def body(q_ref, k_ref, v_ref, c_ref, ct_ref, o_ref, lse_ref, do_ref, dq_ref, dk_ref, dv_ref, dct_ref):
        g, i = pl.program_id(0), pl.program_id(1)

        @pl.when(i == 0)
        def _():
            dk_ref[...] = jnp.zeros_like(dk_ref)
            dv_ref[...] = jnp.zeros_like(dv_ref)
            dct_ref[...] = jnp.zeros_like(dct_ref)

        q = _heads_of(q_ref)
        do16 = _each(lambda x: x.astype(BF16), _heads_of(do_ref))
        delta = _rowsum(_each(lambda a, b: a.astype(F32) * b, do16, _heads_of(o_ref)))
        lse_col = _Each(lse_ref[h, :, 0:1] for h in range(ATT_HEADS))
        cq = _fox_cq(c_ref, g)
        row, col, _, _ = _att_iotas()

        def step(jb, dq, diagonal):
            dq = _Each(dq)
            for sub in range(ATT_SUB):
                keys = pl.ds(pl.multiple_of(jb * ATT_TQ + sub * ATT_TK, ATT_TK), ATT_TK)
                k = _heads_of(k_ref, keys)
                sc = _fox_logits(q, k, cq, ct_ref, keys)
                if diagonal:
                    valid = (col + sub * ATT_TK) <= row
                    p = _keep(valid, _each(jnp.exp, _keep(valid, sc) - lse_col))
                else:
                    p = _each(jnp.exp, sc - lse_col)
                ds = p * (_dot_each(do16, _heads_of(v_ref, keys), "nt") - delta)
                dss = ds * ATT_SCALE
                dk, dv = _dot_each(dss, q, "tn"), _dot_each(p, do16, "tn")
                for h, cs in enumerate(_HEAD_COLS):
                    dct_ref[h, :, keys] -= jnp.sum(ds.vals[h], axis=0, keepdims=True)
                    dk_ref[keys, cs] += dk.vals[h]
                    dv_ref[keys, cs] += dv.vals[h]
                dq = dq + _dot_each(dss, k, "nn")
            return tuple(dq.vals)

        dq0 = step(i, tuple(jnp.zeros((ATT_TQ, HEAD_DIM), F32) for _ in range(ATT_HEADS)), True)
        dq = lax.fori_loop(0, i, lambda jb, dq: step(jb, dq, False), dq0)
        for cs, dq_h in zip(_HEAD_COLS, dq):
            dq_ref[:, cs] = dq_h

    q_spec, k_spec, v_spec = _att_specs(n_heads, s)
    blk = pl.BlockSpec((ATT_TQ, ATT_WIDTH), lambda g, i: (i, g))
    full = pl.BlockSpec((s, ATT_WIDTH), lambda g, i: (0, g))
    wide = jax.ShapeDtypeStruct((s, n_heads * HEAD_DIM), F32)
    return pl.pallas_call(
        body, name=name, grid=(n_heads // ATT_HEADS, s // ATT_TQ),
        in_specs=[q_spec, k_spec, v_spec, pl.BlockSpec((ATT_TQ, LANES), lambda g, i: (i, 0)),
                  pl.BlockSpec((ATT_HEADS, 1, s), lambda g, i: (g, 0, 0)), blk,
                  pl.BlockSpec((ATT_HEADS, ATT_TQ, LANES), lambda g, i: (g, i, 0)), blk],
        out_specs=[blk, full, full, pl.BlockSpec((ATT_HEADS, 1, s), lambda g, i: (g, 0, 0))],
        out_shape=[wide, wide, wide, jax.ShapeDtypeStruct((n_heads, 1, s), F32)],
        compiler_params=_cparams("parallel", "arbitrary"),
    )(qkv, qkv, qkv, c, ct, o, lse, do)


def _cumsum_rows(x, reverse, name):
    s = x.shape[0]
    nb = s // LANES

    def body(x_ref, o_ref):
        r = lax.broadcasted_iota(jnp.int32, (LANES, LANES), 0)
        c = lax.broadcasted_iota(jnp.int32, (LANES, LANES), 1)
        tri = ((r <= c) if reverse else (r >= c)).astype(F32)

        def step(it, carry):
            b = (nb - 1 - it) if reverse else it
            off = pl.multiple_of(b * LANES, LANES)
            blk = x_ref[pl.ds(off, LANES), :]
            o_ref[pl.ds(off, LANES), :] = _dot32(tri, blk) + carry
            return carry + jnp.sum(blk, axis=0, keepdims=True)

        lax.fori_loop(0, nb, step, jnp.zeros((1, LANES), F32))

    return pl.pallas_call(body, name=name, out_shape=jax.ShapeDtypeStruct(x.shape, F32),
                          compiler_params=pltpu.CompilerParams(vmem_limit_bytes=V7X_VMEM_LIMIT))(x)


def _dot32_each(a, b, kind="nn"):
    return _each(lambda x, y: _dot32(x, y, kind), a, b)


def _unit_lower_inverse(m, ri, ci):
    c = ri.shape[0]
    t = -_keep(ri // 2 == ci // 2, m) + jnp.where(ri == ci, 1.0, 0.0)
    b = 4
    while b <= c:
        off_diag = (ri // b == ci // b) & (ri % b >= b // 2) & (ci % b < b // 2)
        t = t - _dot32_each(_dot32_each(t, _keep(off_diag, m)), t)
        b *= 2
    return t


def _dn_gates(g, ri, ci):
    eye = ri == ci
    incl = ri >= ci
    g_row = jnp.sum(jnp.where(eye, g, 0.0), axis=0, keepdims=True)
    gc = jnp.sum(jnp.where(incl, g_row, 0.0), axis=1, keepdims=True)
    gc_row = jnp.sum(jnp.where(eye, gc, 0.0), axis=0, keepdims=True)
    dmat = jnp.where(incl, jnp.exp(jnp.where(incl, gc - gc_row, 0.0)), 0.0)
    gc_last = jnp.sum(g, axis=0, keepdims=True)
    return gc, dmat, jnp.exp(gc), jnp.exp(gc_last - gc), jnp.exp(gc_last)


def _dn_fwd(qkv, act, name):
    s = qkv.shape[0]
    c, d, nh = DN_CHUNK, HEAD_DIM, N_DN_HEADS
    nc = s // c

    def body(q_ref, k_ref, v_ref, act_ref, o_ref, s_ref, t_ref, state):
        @pl.when(pl.program_id(0) == 0)
        def _():
            state[...] = jnp.zeros_like(state)

        ri = lax.broadcasted_iota(jnp.int32, (c, c), 0)
        ci = lax.broadcasted_iota(jnp.int32, (c, c), 1)
        act = act_ref[...]
        heads = range(nh)
        cols = [slice(h * d, (h + 1) * d) for h in heads]
        q, k, v = (_Each(ref[:, cs] for cs in cols) for ref in (q_ref, k_ref, v_ref))
        beta = _Each(_lane_col(act, LANE_BETA + h) for h in heads)
        g = _Each(_lane_col(act, LANE_DECAY + h) for h in heads)
        _, dmat, e, r, gl = _each(lambda gh: _dn_gates(gh, ri, ci), g)
        s0 = _Each(state[h] for h in heads)
        kb = beta * k
        t = _unit_lower_inverse(_keep(ri > ci, _dot32_each(kb, k, "nt") * dmat), ri, ci)
        vn = _dot32_each(t, beta * v) - _dot32_each(_dot32_each(t, kb * e), s0)
        o = _dot32_each(q * e, s0) + _dot32_each(_dot32_each(q, k, "nt") * dmat, vn)
        s1 = s0 * gl + _dot32_each(k * r, vn, "tn")
        for h in heads:
            o_ref[:, cols[h]] = o.vals[h]
            state[h] = s1.vals[h]
            s_ref[h] = s0.vals[h]
            t_ref[h] = t.vals[h]

    wide = lambda part: pl.BlockSpec((c, nh * d), lambda n: (n, part))
    return pl.pallas_call(
        body, name=name, grid=(nc,),
        in_specs=[wide(0), wide(1), wide(2), pl.BlockSpec((c, LANES), lambda n: (n, 0))],
        out_specs=[wide(0), pl.BlockSpec((nh, None, d, d), lambda n: (0, n, 0, 0)),
                   pl.BlockSpec((nh, None, c, c), lambda n: (0, n, 0, 0))],
        out_shape=[jax.ShapeDtypeStruct((s, nh * d), F32), jax.ShapeDtypeStruct((nh, nc, d, d), F32),
                   jax.ShapeDtypeStruct((nh, nc, c, c), F32)],
        scratch_shapes=[pltpu.VMEM((nh, d, d), F32)],
        compiler_params=_cparams("arbitrary"),
    )(qkv, qkv, qkv, act)


def _dn_bwd(qkv, act, states, tinv, do, name):
    s = qkv.shape[0]
    c, d, nh = DN_CHUNK, HEAD_DIM, N_DN_HEADS
    nc = s // c

    def chunk_bwd(q, k, v, do, beta, g, s0, t, ds_out):
        ri = lax.broadcasted_iota(jnp.int32, (c, c), 0)
        ci = lax.broadcasted_iota(jnp.int32, (c, c), 1)
        eye, incl, strict = ri == ci, ri >= ci, ri > ci
        gc, dmat, e, r, gl = _each(lambda gh: _dn_gates(gh, ri, ci), g)
        dot = _dot32_each
        rowsum = lambda x: _each(lambda a: jnp.sum(a, axis=1, keepdims=True), x)
        colsum = lambda x: _each(lambda a: jnp.sum(a, axis=0, keepdims=True), x)
        total = lambda x: colsum(rowsum(x))
        to_col = lambda row: rowsum(_keep(eye, row))
        to_row = lambda colv: colsum(_keep(eye, colv))

        kb, vb = beta * k, beta * v
        kbe = kb * e
        u, w = dot(t, vb), dot(t, kbe)
        vn = u - dot(w, s0)
        qk = dot(q, k, "nt")
        p = qk * dmat
        gram = dot(k, k, "nt")
        kr, qe = k * r, q * e

        d_kr = dot(vn, ds_out, "nt")
        dvn = dot(kr, ds_out)
        dgl = total(s0 * ds_out)
        ds_in = ds_out * gl
        dk = d_kr * r
        dr = rowsum(d_kr * k)
        d_qe = dot(do, s0, "nt")
        ds_in = ds_in + dot(qe, do, "tn")
        dp = _keep(incl, dot(do, vn, "nt"))
        dvn = dvn + dot(p, do, "tn")
        dq = d_qe * e
        de = rowsum(d_qe * q)
        dqk = dp * dmat
        dq = dq + dot(dqk, k)
        dk = dk + dot(dqk, q, "tn")
        dd = dp * qk
        dw = -dot(dvn, s0, "nt")
        ds_in = ds_in - dot(w, dvn, "tn")
        dvb = dot(t, dvn, "tn")
        dkbe = dot(t, dw, "tn")
        dm = -_keep(strict, dot(dvb, u, "nt") + dot(dkbe, w, "nt"))
        dbeta = rowsum(dm * gram * dmat)
        dgram = dm * beta * dmat
        dd = dd + dm * beta * gram
        dk = dk + dot(dgram, k) + dot(dgram, k, "tn")
        dkb = dkbe * e
        de = de + rowsum(dkbe * kb)
        dk = dk + beta * dkb
        dbeta = dbeta + rowsum(dkb * k) + rowsum(dvb * v)
        dv = beta * dvb
        wd = dd * dmat
        dgc = rowsum(wd) - to_col(colsum(wd)) + de * e - dr * r
        dgc_last = total(dr * r) + dgl * gl
        dgc = dgc + _keep(ri[:, 0:1] == c - 1, dgc_last)
        dg = rowsum(_keep(ri <= ci, to_row(dgc)))
        return dq, dk, dv, dbeta, dg, ds_in

    def body(q_ref, k_ref, v_ref, act_ref, s_ref, t_ref, do_ref, dq_ref, dk_ref, dv_ref, dact_ref, dstate):
        @pl.when(pl.program_id(0) == 0)
        def _():
            dstate[...] = jnp.zeros_like(dstate)

        act = act_ref[...]
        heads = range(nh)
        cols = [slice(h * d, (h + 1) * d) for h in heads]
        q, k, v, do = (_Each(ref[:, cs] for cs in cols) for ref in (q_ref, k_ref, v_ref, do_ref))
        dq, dk, dv, dbeta, dg, ds_in = chunk_bwd(
            q, k, v, do, _Each(_lane_col(act, LANE_BETA + h) for h in heads),
            _Each(_lane_col(act, LANE_DECAY + h) for h in heads), _Each(s_ref[h] for h in heads),
            _Each(t_ref[h] for h in heads), _Each(dstate[h] for h in heads))
        lane = lax.broadcasted_iota(jnp.int32, (c, LANES), 1)
        dact = jnp.zeros((c, LANES), F32)
        for h in heads:
            dstate[h] = ds_in.vals[h]
            dq_ref[:, cols[h]], dk_ref[:, cols[h]], dv_ref[:, cols[h]] = dq.vals[h], dk.vals[h], dv.vals[h]
            dact = (dact + jnp.where(lane == LANE_BETA + h, dbeta.vals[h], 0.0)
                    + jnp.where(lane == LANE_DECAY + h, dg.vals[h], 0.0))
        dact_ref[...] = dact

    part = lambda p: pl.BlockSpec((c, nh * d), lambda n: (nc - 1 - n, p))
    per = lambda a, b: pl.BlockSpec((nh, None, a, b), lambda n: (0, nc - 1 - n, 0, 0))
    wide = jax.ShapeDtypeStruct((s, nh * d), F32)
    act_spec = pl.BlockSpec((c, LANES), lambda n: (nc - 1 - n, 0))
    return pl.pallas_call(
        body, name=name, grid=(nc,),
        in_specs=[part(0), part(1), part(2), act_spec, per(d, d), per(c, c), part(0)],
        out_specs=[part(0), part(0), part(0), act_spec],
        out_shape=[wide, wide, wide, jax.ShapeDtypeStruct((s, LANES), F32)],
        scratch_shapes=[pltpu.VMEM((nh, d, d), F32)],
        compiler_params=_cparams("arbitrary"),
    )(qkv, qkv, qkv, act, states, tinv, do)


EVEN_DN_QKV, EVEN_FOX_QKV, EVEN_DN_GATE, EVEN_FOX_GATE, EVEN_NARROW = 0, 1536, 3072, 3584, 4096
EVEN_WIDTH = 4224
CONV_TILE = 256
CONV_HALO = 8


def _conv_fwd(proj, w, name):
    s = proj.shape[0]
    t, cw = CONV_TILE, 3 * D_DN

    def body(cur_ref, prev_ref, w_ref, y_ref, xs):
        i = pl.program_id(0)
        xs[0:CONV_HALO, :] = jnp.where(i > 0, prev_ref[...], 0.0)
        xs[CONV_HALO:, :] = cur_ref[...]
        y = jnp.zeros((t, cw), F32)
        for tap in range(CONV_WIDTH):
            y = y + w_ref[tap:tap + 1, :] * xs[pl.ds(CONV_HALO - CONV_WIDTH + 1 + tap, t), :]
        y_ref[...] = y

    per = t // CONV_HALO
    return pl.pallas_call(
        body, name=name, grid=(s // t,),
        in_specs=[pl.BlockSpec((t, cw), lambda i: (i, 0)),
                  pl.BlockSpec((CONV_HALO, cw), lambda i: (jnp.maximum(i * per - 1, 0), 0)),
                  pl.BlockSpec((CONV_WIDTH, cw), lambda i: (0, 0))],
        out_specs=pl.BlockSpec((t, cw), lambda i: (i, 0)),
        out_shape=jax.ShapeDtypeStruct((s, cw), F32),
        scratch_shapes=[pltpu.VMEM((t + CONV_HALO, cw), F32)],
        compiler_params=_cparams("parallel"),
    )(proj, proj, w)


def _conv_bwd(proj, w, dy, name):
    s = proj.shape[0]
    t, cw = CONV_TILE, 3 * D_DN
    nt = s // t

    def body(cur_ref, prev_ref, w_ref, dy_ref, nxt_ref, dx_ref, dw_ref, xs, dys):
        i = pl.program_id(0)

        @pl.when(i == 0)
        def _():
            dw_ref[...] = jnp.zeros_like(dw_ref)

        xs[0:CONV_HALO, :] = jnp.where(i > 0, prev_ref[...], 0.0)
        xs[CONV_HALO:, :] = cur_ref[...]
        dys[0:t, :] = dy_ref[...]
        dys[t:, :] = jnp.where(i < nt - 1, nxt_ref[...], 0.0)
        dy = dy_ref[...]
        dx = jnp.zeros((t, cw), F32)
        for tap in range(CONV_WIDTH):
            dx = dx + w_ref[tap:tap + 1, :] * dys[pl.ds(CONV_WIDTH - 1 - tap, t), :]
            dw_ref[tap:tap + 1, :] += jnp.sum(dy * xs[pl.ds(CONV_HALO - CONV_WIDTH + 1 + tap, t), :], axis=0,
                                              keepdims=True)
        dx_ref[...] = dx.astype(BF16)

    per = t // CONV_HALO
    last = s // CONV_HALO - 1
    return pl.pallas_call(
        body, name=name, grid=(nt,),
        in_specs=[pl.BlockSpec((t, cw), lambda i: (i, 0)),
                  pl.BlockSpec((CONV_HALO, cw), lambda i: (jnp.maximum(i * per - 1, 0), 0)),
                  pl.BlockSpec((CONV_WIDTH, cw), lambda i: (0, 0)),
                  pl.BlockSpec((t, cw), lambda i: (i, 0)),
                  pl.BlockSpec((CONV_HALO, cw), lambda i: (jnp.minimum((i + 1) * per, last), 0))],
        out_specs=[pl.BlockSpec((t, cw), lambda i: (i, 0)), pl.BlockSpec((CONV_WIDTH, cw), lambda i: (0, 0))],
        out_shape=[jax.ShapeDtypeStruct((s, cw), BF16), jax.ShapeDtypeStruct((CONV_WIDTH, cw), F32)],
        scratch_shapes=[pltpu.VMEM((t + CONV_HALO, cw), F32), pltpu.VMEM((t + CONV_HALO, cw), F32)],
        compiler_params=_cparams("arbitrary"),
    )(proj, proj, w, dy, dy)


def _heads(x, n):
    return [x[:, HEAD_DIM * h:HEAD_DIM * (h + 1)] for h in range(n)]


def _dn_pre_fwd(y, name):
    def fn(yb):
        cs = yb * _sigmoid(yb)
        out = []
        for idx, xh in enumerate(_heads(cs, 3 * N_DN_HEADS)):
            if idx < 2 * N_DN_HEADS:
                xh = xh * lax.rsqrt(jnp.sum(xh * xh, axis=-1, keepdims=True) + EPS)
                if idx < N_DN_HEADS:
                    xh = xh * ATT_SCALE
            out.append(xh)
        return (jnp.concatenate(out, axis=1),)
    return _rowwise(fn, [y], [], [(y.shape[1], F32)], [], tile=256, name=name)[0]


def _dn_pre_bwd(y, dq, dk, dv, name):
    def fn(yb, dqb, dkb, dvb):
        sg = _sigmoid(yb)
        cs = yb * sg
        dout = _heads(dqb, N_DN_HEADS) + _heads(dkb, N_DN_HEADS) + _heads(dvb, N_DN_HEADS)
        dcs = []
        for idx, (xh, dh) in enumerate(zip(_heads(cs, 3 * N_DN_HEADS), dout)):
            if idx < 2 * N_DN_HEADS:
                if idx < N_DN_HEADS:
                    dh = dh * ATT_SCALE
                r = lax.rsqrt(jnp.sum(xh * xh, axis=-1, keepdims=True) + EPS)
                xhat = xh * r
                dh = r * (dh - xhat * jnp.sum(xhat * dh, axis=-1, keepdims=True))
            dcs.append(dh)
        return (jnp.concatenate(dcs, axis=1) * _silu_grad(yb, sg),)
    return _rowwise(fn, [y, dq, dk, dv], [], [(y.shape[1], F32)], [], tile=256, name=name)[0]


def _narrow_params(a_log, dt_bias, f_bias):
    lanes = lambda a, first: jnp.pad(a.reshape(1, -1), ((0, 0), (first, LANES - first - a.shape[0])))
    return jnp.concatenate([lanes(a_log, LANE_DECAY), lanes(dt_bias, LANE_DECAY), lanes(f_bias, LANE_FORGET),
                            jnp.zeros((5, LANES), F32)], axis=0)


def _narrow_masks(shape):
    lane = lax.broadcasted_iota(jnp.int32, shape, 1)
    is_beta = lane < LANE_DECAY
    is_decay = (lane >= LANE_DECAY) & (lane < LANE_FORGET)
    is_forget = (lane >= LANE_FORGET) & (lane < LANE_FORGET + N_FOX_HEADS)
    return is_beta, is_decay, is_forget


def _narrow_fwd(proj, params, name):
    def fn(sm, pk):
        is_beta, is_decay, is_forget = _narrow_masks(sm.shape)
        g = -jnp.exp(pk[0:1, :]) * _softplus(sm + pk[1:2, :])
        logf = -_softplus(-(sm + pk[2:3, :]))
        return (jnp.where(is_beta, _sigmoid(sm), jnp.where(is_decay, g, jnp.where(is_forget, logf, 0.0))),)
    return _rowwise(fn, [(proj, LANES, EVEN_NARROW // LANES)], [params], [(LANES, F32)], [], tile=512, name=name)[0]


def _narrow_bwd(proj, params, act, dact, dlogf, name):
    def fn(sm, ab, da, dl, pk):
        is_beta, is_decay, is_forget = _narrow_masks(sm.shape)
        db = jnp.where(is_forget, dl, da)
        d_beta = db * ab * (1.0 - ab)
        d_decay = db * (-jnp.exp(pk[0:1, :])) * _sigmoid(sm + pk[1:2, :])
        d_forget = db * _sigmoid(-(sm + pk[2:3, :]))
        dsm = jnp.where(is_beta, d_beta, jnp.where(is_decay, d_decay, jnp.where(is_forget, d_forget, 0.0)))
        col = lambda x: jnp.sum(x, axis=0, keepdims=True)
        return (dsm, col(jnp.where(is_decay, db * ab, 0.0)), col(jnp.where(is_decay, dsm, 0.0)),
                col(jnp.where(is_forget, dsm, 0.0)))
    return _rowwise(fn, [(proj, LANES, EVEN_NARROW // LANES), act, dact, dlogf], [params], [(LANES, BF16)],
                    [(1, LANES)] * 3, tile=512, name=name)


def _head_rms(xh):
    r = lax.rsqrt(jnp.mean(xh * xh, axis=-1, keepdims=True) + EPS)
    return xh * r, r


def _fox_pre_fwd(proj, qg, kg, name):
    def fn(pf, qgb, kgb):
        out = []
        for idx, xh in enumerate(_heads(pf, 3 * N_FOX_HEADS)):
            if idx < 2 * N_FOX_HEADS:
                xh = _head_rms(xh)[0] * (qgb if idx < N_FOX_HEADS else kgb)
            out.append(xh)
        return (jnp.concatenate(out, axis=1),)
    return _rowwise(fn, [(proj, 3 * D_FOX, EVEN_FOX_QKV // (3 * D_FOX))], [qg, kg], [(3 * D_FOX, BF16)], [],
                    tile=256, name=name)[0]


def _fox_pre_bwd(proj, qg, kg, dq, dk, dv, name):
    def fn(pf, dqb, dkb, dvb, qgb, kgb):
        dout = _heads(dqb, N_FOX_HEADS) + _heads(dkb, N_FOX_HEADS) + _heads(dvb, N_FOX_HEADS)
        dg = [jnp.zeros((1, HEAD_DIM), F32), jnp.zeros((1, HEAD_DIM), F32)]
        dx = []
        for idx, (xh, dh) in enumerate(zip(_heads(pf, 3 * N_FOX_HEADS), dout)):
            if idx < 2 * N_FOX_HEADS:
                which = 0 if idx < N_FOX_HEADS else 1
                xhat, r = _head_rms(xh)
                dg[which] = dg[which] + jnp.sum(dh * xhat, axis=0, keepdims=True)
                dxh = dh * (qgb if which == 0 else kgb)
                dh = r * (dxh - xhat * jnp.mean(dxh * xhat, axis=-1, keepdims=True))
            dx.append(dh)
        return jnp.concatenate(dx, axis=1), dg[0], dg[1]
    return _rowwise(fn, [(proj, 3 * D_FOX, EVEN_FOX_QKV // (3 * D_FOX)), dq, dk, dv], [qg, kg],
                    [(3 * D_FOX, BF16)], [(1, HEAD_DIM)] * 2, tile=256, name=name)


def _mix_gate_fwd(proj, o_dn, o_fox, ng, name):
    def fn(gd, gf, od, of, ngb):
        dn = [_head_rms(xh)[0] * ngb for xh in _heads(od, N_DN_HEADS)]
        return (jnp.concatenate([jnp.concatenate(dn, axis=1) * gd * _sigmoid(gd), of * _sigmoid(gf)], axis=1),)
    return _rowwise(fn, [(proj, D_DN, EVEN_DN_GATE // D_DN), (proj, D_FOX, EVEN_FOX_GATE // D_FOX), o_dn, o_fox],
                    [ng], [(D_DN + D_FOX, BF16)], [], tile=256, name=name)[0]


def _mix_gate_bwd(proj, o_dn, o_fox, ng, dom, name):
    def fn(gd, gf, od, of, dm, ngb):
        d_dn, d_fox = dm[:, :D_DN], dm[:, D_DN:]
        sgd, sgf = _sigmoid(gd), _sigmoid(gf)
        don = d_dn * gd * sgd
        dng = jnp.zeros((1, HEAD_DIM), F32)
        dod, normed = [], []
        for xh, dh in zip(_heads(od, N_DN_HEADS), _heads(don, N_DN_HEADS)):
            xhat, r = _head_rms(xh)
            dng = dng + jnp.sum(dh * xhat, axis=0, keepdims=True)
            dxh = dh * ngb
            dod.append(r * (dxh - xhat * jnp.mean(dxh * xhat, axis=-1, keepdims=True)))
            normed.append(xhat * ngb)
        d_gd = d_dn * jnp.concatenate(normed, axis=1) * _silu_grad(gd, sgd)
        d_gf = d_fox * of * sgf * (1.0 - sgf)
        return jnp.concatenate(dod, axis=1), d_fox * sgf, d_gd, d_gf, dng
    return _rowwise(fn, [(proj, D_DN, EVEN_DN_GATE // D_DN), (proj, D_FOX, EVEN_FOX_GATE // D_FOX), o_dn, o_fox, dom],
                    [ng], [(D_DN, F32), (D_FOX, F32), (D_DN, BF16), (D_FOX, BF16)], [(1, HEAD_DIM)], tile=256,
                    name=name)


def _loss_grad(y, target, name):
    d = y.shape[1]

    def fn(yb, tb):
        diff = yb - tb
        part = jnp.sum(jnp.sum(diff * diff, axis=1, keepdims=True), axis=0, keepdims=True) * (0.5 / d)
        g = diff * (1.0 / d)
        return g, g, part
    return _rowwise(fn, [y, target], [], [(d, F32), (d, BF16)], [(1, 1)], tile=512, name=name)


_REF_EVEN = {"dn_qkv": (0, 1536), "dn_gate": (1536, 2048), "dn_ba": (2048, 2056), "fox_qkv": (2056, 3592),
             "fox_gate": (3592, 4104), "f_pre": (4104, 4108)}
D_IN_EVEN = 4108


def _even_to_kernel_layout(w):
    cut = lambda name: w[..., _REF_EVEN[name][0]:_REF_EVEN[name][1]]
    pad = jnp.zeros(w.shape[:-1] + (EVEN_WIDTH - EVEN_NARROW - 12,), w.dtype)
    return jnp.concatenate([cut("dn_qkv"), cut("fox_qkv"), cut("dn_gate"), cut("fox_gate"), cut("dn_ba"),
                            cut("f_pre"), pad], axis=-1)


def _even_from_kernel_layout(g):
    return jnp.concatenate([g[..., EVEN_DN_QKV:EVEN_FOX_QKV], g[..., EVEN_DN_GATE:EVEN_FOX_GATE],
                            g[..., EVEN_NARROW:EVEN_NARROW + 8], g[..., EVEN_FOX_QKV:EVEN_DN_GATE],
                            g[..., EVEN_FOX_GATE:EVEN_NARROW], g[..., EVEN_NARROW + 8:EVEN_NARROW + 12]], axis=-1)


EVEN_QUARTER = 1027
EVEN_QUARTER_PAD = 1152


def _even_grad_quarters(g):
    g = _even_from_kernel_layout(g)
    pad = [(0, 0)] * (g.ndim - 1) + [(0, EVEN_QUARTER_PAD - EVEN_QUARTER)]
    return jnp.concatenate([jnp.pad(g[..., q * EVEN_QUARTER:(q + 1) * EVEN_QUARTER], pad) for q in range(4)], axis=-1)


def _forget_rows(c):
    return c[:, LANE_FORGET:LANE_FORGET + N_FOX_HEADS].T.reshape(N_FOX_HEADS, 1, c.shape[0])


def _forget_lanes(rows):
    s = rows.shape[2]
    return jnp.pad(rows.reshape(-1, s).T, ((0, 0), (LANE_FORGET, LANES - LANE_FORGET - N_FOX_HEADS)))


def _even_fwd(x, gain, w_in, w_out, j, p, tag):
    h = _rms_fwd(x, gain, f"{tag}_norm")
    proj = _mm(h, w_in, "nn", tm=512, tn=EVEN_WIDTH // 3, out_dtype=F32, name=f"{tag}_in", b_lead=(j,))
    y = _conv_fwd(proj, p["conv_w"], f"{tag}_conv")
    dn_qkv = _dn_pre_fwd(y, f"{tag}_dn_pre")
    act = _narrow_fwd(proj, p["narrow"], f"{tag}_narrow")
    o_dn, states, tinv = _dn_fwd(dn_qkv, act, f"{tag}_delta")
    fox_qkv = _fox_pre_fwd(proj, p["q_g"], p["k_g"], f"{tag}_fox_pre")
    c = _cumsum_rows(act, False, f"{tag}_cumsum")
    ct = _forget_rows(c)
    o_fox, lse = _fox_fwd(fox_qkv, c, ct, f"{tag}_fox")
    om = _mix_gate_fwd(proj, o_dn, o_fox, p["dn_norm_g"], f"{tag}_gate")
    x2 = _mm(om, w_out, "nn", tm=512, tn=x.shape[1], out_dtype=F32, name=f"{tag}_out", residual=x, b_lead=(j,))
    return x2, (x, h, proj, y, dn_qkv, act, states, tinv, o_dn, fox_qkv, c, ct, o_fox, lse, om)


def _even_bwd(dxo, dxo16, saved, gain, w_in, w_out, j, p, tag, g_in, g_out, after=None):
    x, h, proj, y, dn_qkv, act, states, tinv, o_dn, fox_qkv, c, ct, o_fox, lse, om = saved
    d = x.shape[1]
    dom = _mm(dxo16, w_out, "nt", tm=512, tn=d, out_dtype=F32, name=f"{tag}_out_bwd", b_lead=(j,), after=after)
    g_out = _mm(om, dxo16, "tn", tm=512, tn=d, out_dtype=F32, name=f"{tag}_out_dw", into=(g_out, 0))
    d_odn, d_ofox, d_gd, d_gf, d_ng = _mix_gate_bwd(proj, o_dn, o_fox, p["dn_norm_g"], dom, f"{tag}_gate_bwd")
    dq, dk, dv, dct = _fox_bwd(fox_qkv, c, ct, o_fox, lse, d_ofox, f"{tag}_fox_bwd")
    d_fox_qkv, d_qg, d_kg = _fox_pre_bwd(proj, p["q_g"], p["k_g"], dq, dk, dv, f"{tag}_fox_pre_bwd")
    dlogf = _cumsum_rows(_forget_lanes(dct), True, f"{tag}_cumsum_bwd")
    dq, dk, dv, dact = _dn_bwd(dn_qkv, act, states, tinv, d_odn, f"{tag}_delta_bwd")
    dy = _dn_pre_bwd(y, dq, dk, dv, f"{tag}_dn_pre_bwd")
    d_dn_qkv, d_conv = _conv_bwd(proj, p["conv_w"], dy, f"{tag}_conv_bwd")
    d_narrow, s_alog, s_dt, s_fb = _narrow_bwd(proj, p["narrow"], act, dact, dlogf, f"{tag}_narrow_bwd")
    dproj = jnp.concatenate([d_dn_qkv, d_fox_qkv, d_gd, d_gf, d_narrow], axis=1)
    g_in = _mm(h, dproj, "tn", tm=512, tn=EVEN_WIDTH // 3, out_dtype=F32, name=f"{tag}_in_dw", into=(g_in, 0))
    dx, dx16, d_gain = _in_proj_bwd(dproj, w_in, j, x, dxo, gain, f"{tag}_in_bwd")
    small = {"conv_w": d_conv, "a_log": s_alog, "dt_bias": s_dt, "f_bias": s_fb, "dn_norm_g": d_ng, "q_g": d_qg,
             "k_g": d_kg}
    return dx, dx16, d_gain, small, g_in, g_out


def _odd_fwd(x, gain, w_in, w_out, j, tag):
    h = _rms_fwd(x, gain, f"{tag}_norm")
    qkv = _mm(h, w_in, "nn", tm=512, tn=w_in.shape[2] // 2, out_dtype=BF16, name=f"{tag}_in", b_lead=(j,))
    o16, o32 = _sb_fwd(qkv, N_SB_HEADS, f"{tag}_sb")
    x2 = _mm(o16, w_out, "nn", tm=512, tn=x.shape[1], out_dtype=F32, name=f"{tag}_out", residual=x, b_lead=(j,))
    return x2, (x, h, qkv, o16, o32)


def _odd_bwd(dxo, dxo16, saved, gain, w_in, w_out, j, tag, g_in, g_out, after=None):
    x, h, qkv, o16, o32 = saved
    d = x.shape[1]
    do = _mm(dxo16, w_out, "nt", tm=512, tn=d, out_dtype=BF16, name=f"{tag}_out_bwd", b_lead=(j,), after=after)
    g_out = _mm(o16, dxo16, "tn", tm=512, tn=d, out_dtype=F32, name=f"{tag}_out_dw", into=(g_out, 0))
    dq, dk, dv = _sb_bwd(qkv, o32, do, N_SB_HEADS, f"{tag}_sb_bwd")
    dqkv = jnp.concatenate([dq, dk.astype(BF16), dv.astype(BF16)], axis=1)
    g_in = _mm(h, dqkv, "tn", tm=512, tn=w_in.shape[2] // 2, out_dtype=F32, name=f"{tag}_in_dw", into=(g_in, 0))
    dx, dx16, d_gain = _in_proj_bwd(dqkv, w_in, j, x, dxo, gain, f"{tag}_in_bwd")
    return dx, dx16, d_gain, g_in, g_out


def _forward_backward(x, target, w, first, rest_after, token, on_reduced):
    depth = w["norm_ffn1"].shape[0]
    row = lambda a, l: a[l][None]
    rest = {}

    def mats(names, j):
        if j == 0 and names[0] in first:
            return [first[name] for name in names] + [0]
        return [rest[name] for name in names] + [j - (1 if names[0] in first else 0)]

    def even_small(j):
        return {"conv_w": w["dn_conv_w"][j], "narrow": _narrow_params(w["dn_a_log"][j], w["dn_dt_bias"][j],
                                                                     w["fox_f_bias"][j]),
                "dn_norm_g": row(w["dn_norm_g"], j), "q_g": row(w["fox_q_norm_g"], j),
                "k_g": row(w["fox_k_norm_g"], j)}

    saved = []
    for l in range(depth):
        if l == 1:
            rest.update(rest_after(x))
        gain = row(w["norm_ffn1"], l) + token[0:1, 0:1] if l == 0 else row(w["norm_ffn1"], l)
        x, s1 = _ffn_fwd(x, gain, *mats(("ffn1_w_gu", "ffn1_w_down"), l), "ffn1")
        if l % 2 == 0:
            x, s2 = _even_fwd(x, row(w["norm_mix"], l), *mats(("w_in_even", "w_out_even"), l // 2),
                              even_small(l // 2), "even")
        else:
            x, s2 = _odd_fwd(x, row(w["norm_mix"], l), *mats(("w_in_odd", "w_out_odd"), l // 2), "odd")
        x, s3 = _ffn_fwd(x, row(w["norm_ffn2"], l), *mats(("ffn2_w_gu", "ffn2_w_down"), l), "ffn2")
        saved.append((s1, s2, s3))

    dx, dx16, loss = _loss_grad(x, target, "loss")

    kind_of = dict(BIG)
    d_norm = {k: [None] * depth for k in ("norm_ffn1", "norm_mix", "norm_ffn2")}
    d_even = [None] * ((depth + 1) // 2)
    to_sibling, between_chips, token = None, None, None
    for l in reversed(range(depth)):
        s1, s2, s3 = saved[l]
        mixer = ("w_in_even", "w_out_even") if l % 2 == 0 else ("w_in_odd", "w_out_odd")
        names = ["ffn1_w_gu", "ffn1_w_down", *mixer, "ffn2_w_gu", "ffn2_w_down"]
        g = {name: lax.empty((1,) + rest[name].shape[1:], F32) for name in names}
        dx, dx16, d_norm["norm_ffn2"][l], g["ffn2_w_gu"], g["ffn2_w_down"] = _ffn_bwd(
            dx, dx16, s3, row(w["norm_ffn2"], l), *mats(("ffn2_w_gu", "ffn2_w_down"), l), "ffn2", g["ffn2_w_gu"],
            g["ffn2_w_down"], after=token)
        if to_sibling is not None:
            between_chips, token = _reduce_middle(to_sibling, dx)
        if l % 2 == 0:
            dx, dx16, d_norm["norm_mix"][l], d_even[l // 2], g["w_in_even"], g["w_out_even"] = _even_bwd(
                dx, dx16, s2, row(w["norm_mix"], l), *mats(("w_in_even", "w_out_even"), l // 2), even_small(l // 2),
                "even", g["w_in_even"], g["w_out_even"], after=token)
            g["w_in_even"] = _even_grad_quarters(g["w_in_even"])
        else:
            dx, dx16, d_norm["norm_mix"][l], g["w_in_odd"], g["w_out_odd"] = _odd_bwd(
                dx, dx16, s2, row(w["norm_mix"], l), *mats(("w_in_odd", "w_out_odd"), l // 2), "odd", g["w_in_odd"],
                g["w_out_odd"], after=token)
        dx, dx16, d_norm["norm_ffn1"][l], g["ffn1_w_gu"], g["ffn1_w_down"] = _ffn_bwd(
            dx, dx16, s1, row(w["norm_ffn1"], l), *mats(("ffn1_w_gu", "ffn1_w_down"), l), "ffn1", g["ffn1_w_gu"],
            g["ffn1_w_down"])
        to_sibling, token = _reduce_start([g[name] for name in names], [kind_of[name] for name in names], names,
                                          f"layer{l}")
        if between_chips is not None:
            on_reduced(l + 1, dict(zip(between_chips[-2], _reduce_finish(between_chips, dx))))
    between_chips, _ = _reduce_middle(to_sibling, dx)
    on_reduced(0, dict(zip(between_chips[-2], _reduce_finish(between_chips, dx))))

    small = {k: jnp.concatenate(v, axis=0) for k, v in d_norm.items()}
    dec = slice(LANE_DECAY, LANE_DECAY + N_DN_HEADS)
    fgt = slice(LANE_FORGET, LANE_FORGET + N_FOX_HEADS)
    small["dn_conv_w"] = jnp.stack([e["conv_w"] for e in d_even])
    small["dn_a_log"] = jnp.concatenate([e["a_log"][:, dec] for e in d_even], axis=0)
    small["dn_dt_bias"] = jnp.concatenate([e["dt_bias"][:, dec] for e in d_even], axis=0)
    small["fox_f_bias"] = jnp.concatenate([e["f_bias"][:, fgt] for e in d_even], axis=0)
    small["dn_norm_g"] = jnp.concatenate([e["dn_norm_g"] for e in d_even], axis=0)
    small["fox_q_norm_g"] = jnp.concatenate([e["q_g"] for e in d_even], axis=0)
    small["fox_k_norm_g"] = jnp.concatenate([e["k_g"] for e in d_even], axis=0)
    return loss, dx, small


MESH = pl.DeviceIdType.MESH
ANY = pl.BlockSpec(memory_space=pl.ANY)


def _place():
    x, y, c = lax.axis_index("x"), lax.axis_index("y"), lax.axis_index("c")
    return x, y, c, [(1 - x, y), (x, 1 - y), (1 - x, 1 - y)]


def _remote(src, dst, send_sem, recv_sem, to):
    return pltpu.make_async_remote_copy(src_ref=src, dst_ref=dst, send_sem=send_sem, recv_sem=recv_sem,
                                        device_id=to, device_id_type=MESH)


def _aligned(start, multiple):
    return start if isinstance(start, int) else pl.multiple_of(start, multiple)


def _quarter(ref, kind, chip, half, rows, cols):
    k = 2 * chip[0] + chip[1]
    hr = rows // 2
    assert hr % 16 == 0 and cols % LANES == 0
    if kind == "col":
        return ref.at[:, pl.ds(_aligned(half * hr, 16), hr), pl.ds(_aligned(k * cols, LANES), cols)]
    return ref.at[:, pl.ds(_aligned(k * rows + half * hr, 16), hr), :]


def _place_quarter(shard, kind, kc, name, first=0, count=None):
    l, rows, cols = shard.shape
    l = l - first if count is None else count
    tr = rows
    while tr * cols * 4 > (2 << 20) and tr % 32 == 0:
        tr //= 2
    nr = rows // tr
    if kind == "col":
        out_spec = pl.BlockSpec((None, tr, cols), lambda li, i, kc_ref: (li, i, kc_ref[0]))
        out_shape = (l, rows, 4 * cols)
    else:
        out_spec = pl.BlockSpec((None, tr, cols), lambda li, i, kc_ref: (li, kc_ref[0] * nr + i, 0))
        out_shape = (l, 4 * rows, cols)

    def body(kc_ref, x_ref, o_ref):
        o_ref[...] = x_ref[...].astype(BF16)

    return pl.pallas_call(
        body, name=name,
        grid_spec=pltpu.PrefetchScalarGridSpec(
            num_scalar_prefetch=1, grid=(l, nr),
            in_specs=[pl.BlockSpec((None, tr, cols), lambda li, i, kc_ref: (li + first, i, 0))],
            out_specs=out_spec),
        out_shape=jax.ShapeDtypeStruct(out_shape, BF16),
        compiler_params=_cparams("parallel", "parallel"),
    )(kc, shard)


def _gather_weights(wholes, kinds):
    n = len(wholes)

    def dims(ref, kind):
        _, r, cc = ref.shape
        return (r, cc // 4) if kind == "col" else (r // 4, cc)

    def body(*refs):
        bufs = refs[n:2 * n]
        send_sems, recv_sems = refs[2 * n:]
        x, y, c, chips = _place()
        sibling = (x, y, 1 - c)
        first, passed = [], []
        for t in range(n):
            rows, cols = dims(bufs[t], kinds[t])
            mine = _quarter(bufs[t], kinds[t], (x, y), c, rows, cols)
            for j, chip in enumerate(chips):
                cp = _remote(mine, mine, send_sems.at[t, j], recv_sems.at[t, j], (*chip, c))
                cp.start()
                first.append(cp)
        for j, chip in enumerate(chips):
            for t in range(n):
                rows, cols = dims(bufs[t], kinds[t])
                got = _quarter(bufs[t], kinds[t], chip, c, rows, cols)
                _remote(got, got, send_sems.at[t, j], recv_sems.at[t, j], (*chip, c)).wait_recv()
                cp = _remote(got, got, send_sems.at[t, 3 + j], recv_sems.at[t, 3 + j], sibling)
                cp.start()
                passed.append(cp)
        for j, chip in enumerate(chips):
            for t in range(n):
                rows, cols = dims(bufs[t], kinds[t])
                got = _quarter(bufs[t], kinds[t], chip, 1 - c, rows, cols)
                _remote(got, got, send_sems.at[t, 3 + j], recv_sems.at[t, 3 + j], sibling).wait_recv()
        for cp in first + passed:
            cp.wait_send()

    return pl.pallas_call(
        body, name="gather_weights", in_specs=[ANY] * n, out_specs=[ANY] * n,
        out_shape=[jax.ShapeDtypeStruct(a.shape, a.dtype) for a in wholes],
        input_output_aliases={t: t for t in range(n)},
        scratch_shapes=[pltpu.SemaphoreType.DMA((n, 6)), pltpu.SemaphoreType.DMA((n, 6))],
        compiler_params=pltpu.CompilerParams(has_side_effects=True),
    )(*wholes)


def _quarter_dims(ref, kind):
    _, r, cc = ref.shape
    return (r, cc // 4) if kind == "col" else (r // 4, cc)


def _gather_chips_copies(bufs, sems, kinds):
    x, y, c, chips = _place()
    copies = []
    for t, buf in enumerate(bufs):
        rows, cols = _quarter_dims(buf, kinds[t])
        mine = _quarter(buf, kinds[t], (x, y), c, rows, cols)
        for j, chip in enumerate(chips):
            pair = 2 * (OTHER_CHIPS * t + j)
            copies.append(_remote(mine, mine, sems[pair], sems[pair + 1], (*chip, c)))
    return copies


def _gather_start(wholes, kinds, after, tag):
    n = len(wholes)
    n_sems = 2 * OTHER_CHIPS * n
    n_in = n + len(after)

    def body(*refs):
        for cp in _gather_chips_copies(refs[:n], refs[n_in + n:n_in + n + n_sems], kinds):
            cp.start()
        refs[-1][...] = jnp.zeros_like(refs[-1])

    held = [pltpu.with_memory_space_constraint(a, pltpu.HBM) for a in wholes]
    out = pl.pallas_call(
        body, name=f"gather_start_{tag}", in_specs=[HBM] * n + [ANY] * len(after),
        out_specs=(*[HBM] * n, *[SEM] * n_sems, pl.BlockSpec(memory_space=pltpu.VMEM)),
        out_shape=(*[pltpu.HBM(a.shape, a.dtype) for a in held], *[pltpu.SemaphoreType.DMA(())] * n_sems,
                   jax.ShapeDtypeStruct((8, LANES), F32)),
        input_output_aliases={i: i for i in range(n)},
        compiler_params=pltpu.CompilerParams(has_side_effects=SPLIT_COPY),
    )(*held, *after)
    return out[n:n + n_sems], out[:n], out[-1]


def _gather_wait(sems, wholes, kinds, after, tag):
    n = len(wholes)

    def body(*refs):
        for cp in _gather_chips_copies(refs[:n], refs[n:n + len(sems)], kinds):
            cp.wait_send()
            cp.wait_recv()

    return pl.pallas_call(
        body, name=f"gather_wait_{tag}", in_specs=[HBM] * n + [SEM] * len(sems) + [ANY],
        out_specs=tuple([HBM] * n), out_shape=tuple(pltpu.HBM(a.shape, a.dtype) for a in wholes),
        input_output_aliases={i: i for i in range(n)},
        compiler_params=pltpu.CompilerParams(has_side_effects=SPLIT_COPY),
    )(*wholes, *sems, after)


def _gather_forward(wholes, kinds, tag):
    n = len(wholes)

    def body(*refs):
        bufs = refs[n:2 * n]
        send_sems, recv_sems = refs[2 * n:]
        x, y, c, chips = _place()
        copies = []
        for t in range(n):
            rows, cols = _quarter_dims(bufs[t], kinds[t])
            for j, chip in enumerate(chips):
                got = _quarter(bufs[t], kinds[t], chip, c, rows, cols)
                cp = _remote(got, got, send_sems.at[t, j], recv_sems.at[t, j], (x, y, 1 - c))
                cp.start()
                copies.append(cp)
        for cp in copies:
            cp.wait_send()
        for t in range(n):
            rows, cols = _quarter_dims(bufs[t], kinds[t])
            for j, chip in enumerate(chips):
                got = _quarter(bufs[t], kinds[t], chip, 1 - c, rows, cols)
                _remote(got, got, send_sems.at[t, j], recv_sems.at[t, j], (x, y, 1 - c)).wait_recv()

    return pl.pallas_call(
        body, name=f"gather_forward_{tag}", in_specs=[ANY] * n, out_specs=[ANY] * n,
        out_shape=[jax.ShapeDtypeStruct(a.shape, a.dtype) for a in wholes],
        input_output_aliases={t: t for t in range(n)},
        scratch_shapes=[pltpu.SemaphoreType.DMA((n, OTHER_CHIPS)), pltpu.SemaphoreType.DMA((n, OTHER_CHIPS))],
        compiler_params=pltpu.CompilerParams(has_side_effects=True),
    )(*wholes)


def _canonical(a, kind):
    l, r, c = a.shape
    return a.reshape(l, 1, r, c) if kind == "col" else a.reshape(l, 4, r // 4, c)


def _add_tile(rows, cols):
    tc = cols if cols <= 1536 else cols // 4
    tr = rows
    while tr * tc * 4 > (1 << 20) and tr % 16 == 0:
        tr //= 2
    return tr, tc


def _rs_add_sibling(part, got, c, name):
    l, a, hr, cols = got.shape
    tr, tc = _add_tile(hr, cols)
    nr = hr // tr

    def body(c_ref, p_ref, g_ref, o32_ref, o16_ref):
        s = p_ref[...] + g_ref[...]
        o32_ref[...] = s
        o16_ref[...] = s.astype(BF16)

    blk = (None, None, tr, tc)
    spec = pl.BlockSpec(blk, lambda li, ai, i, j, c_ref: (li, ai, i, j))
    return pl.pallas_call(
        body, name=name,
        grid_spec=pltpu.PrefetchScalarGridSpec(
            num_scalar_prefetch=1, grid=(l, a, nr, cols // tc),
            in_specs=[pl.BlockSpec(blk, lambda li, ai, i, j, c_ref: (li, ai, c_ref[0] * nr + i, j)), spec],
            out_specs=[spec, spec]),
        out_shape=[jax.ShapeDtypeStruct(got.shape, F32), jax.ShapeDtypeStruct(got.shape, BF16)],
        compiler_params=_cparams("parallel", "parallel", "parallel", "parallel"),
    )(c, part, got)


def _quarter4(ref, kind, chip, cols):
    k = 2 * chip[0] + chip[1]
    if kind == "col":
        return ref.at[:, :, :, pl.ds(pl.multiple_of(k * cols, LANES), cols)]
    return ref.at[:, pl.ds(k, 1), :, :]


HBM = pl.BlockSpec(memory_space=pltpu.HBM)
SEM = pl.BlockSpec(memory_space=pltpu.SEMAPHORE)
SPLIT_COPY = pltpu.SideEffectType.DATAFLOW_SIDE_EFFECTING
OTHER_CHIPS = 3


def _quarter4_shape(a, kind):
    l, _, hr, cols = a.shape
    return (l, 1, hr, cols // 4 if kind == "col" else cols)


def _rs_chips_copies(srcs, lands, sems, kinds):
    x, y, c, chips = _place()
    copies = []
    for t, (src, land) in enumerate(zip(srcs, lands)):
        cols = _quarter4_shape(src, kinds[t])[3]
        for j, chip in enumerate(chips):
            pair = 2 * (OTHER_CHIPS * t + j)
            copies.append(_remote(_quarter4(src, kinds[t], chip, cols), land.at[j], sems[pair], sems[pair + 1],
                                  (*chip, c)))
    return copies


def _split_start(copies, srcs, lands, n_sems, name):
    n = len(srcs)

    def body(*refs):
        for cp in copies(refs[:n], refs[n:2 * n], refs[4 * n:4 * n + n_sems]):
            cp.start()
        refs[-1][...] = jnp.zeros_like(refs[-1])

    held = [pltpu.with_memory_space_constraint(a, pltpu.HBM) for a in (*srcs, *lands)]
    out = pl.pallas_call(
        body, name=name, in_specs=[HBM] * (2 * n),
        out_specs=(*[HBM] * (2 * n), *[SEM] * n_sems, pl.BlockSpec(memory_space=pltpu.VMEM)),
        out_shape=(*[pltpu.HBM(a.shape, a.dtype) for a in held], *[pltpu.SemaphoreType.DMA(())] * n_sems,
                   jax.ShapeDtypeStruct((8, LANES), F32)),
        input_output_aliases={i: i for i in range(2 * n)},
        compiler_params=pltpu.CompilerParams(has_side_effects=SPLIT_COPY),
    )(*held)
    return out[2 * n:2 * n + n_sems], out[:n], out[n:2 * n], out[-1]


def _split_wait(copies, sems, srcs, lands, after, name):
    n = len(srcs)

    def body(*refs):
        for cp in copies(refs[:n], refs[n:2 * n], refs[2 * n:2 * n + len(sems)]):
            cp.wait_send()
            cp.wait_recv()

    out = pl.pallas_call(
        body, name=name, in_specs=[HBM] * (2 * n) + [SEM] * len(sems) + [ANY],
        out_specs=tuple([HBM] * (2 * n)),
        out_shape=tuple(pltpu.HBM(a.shape, a.dtype) for a in (*srcs, *lands)),
        input_output_aliases={i: i for i in range(2 * n)},
        compiler_params=pltpu.CompilerParams(has_side_effects=SPLIT_COPY),
    )(*srcs, *lands, *sems, after)
    return out[:n], out[n:]


def _rs_sibling_copies(srcs, lands, sems):
    x, y, c, _ = _place()
    copies = []
    for t, (src, land) in enumerate(zip(srcs, lands)):
        hr = src.shape[2] // 2
        gives = src.at[:, :, pl.ds(pl.multiple_of((1 - c) * hr, 8), hr), :]
        copies.append(_remote(gives, land, sems[2 * t], sems[2 * t + 1], (x, y, 1 - c)))
    return copies


def _rs_add_chips(sum32, got, kind, kc, name):
    _, l, _, hr, cols = got.shape
    tr, _ = _add_tile(hr, cols)
    nr = hr // tr
    k_arr, c_arr = kc
    if kind == "col":
        own = pl.BlockSpec((None, None, tr, cols), lambda li, i, k_ref, c_ref: (li, 0, i, k_ref[0]))
    else:
        own = pl.BlockSpec((None, None, tr, cols), lambda li, i, k_ref, c_ref: (li, k_ref[0], i, 0))

    def body(k_ref, c_ref, own_ref, got_ref, o_ref):
        o_ref[...] = ((own_ref[...] + got_ref[0].astype(F32)) + got_ref[1].astype(F32)) + got_ref[2].astype(F32)

    return pl.pallas_call(
        body, name=name,
        grid_spec=pltpu.PrefetchScalarGridSpec(
            num_scalar_prefetch=2, grid=(l, nr),
            in_specs=[own, pl.BlockSpec((3, None, None, tr, cols), lambda li, i, k_ref, c_ref: (0, li, 0, i, 0))],
            out_specs=pl.BlockSpec((None, tr, cols), lambda li, i, k_ref, c_ref: (li, c_ref[0] * nr + i, 0))),
        out_shape=jax.ShapeDtypeStruct((l, 2 * hr, cols), F32),
        compiler_params=_cparams("parallel", "parallel"),
    )(k_arr, c_arr, sum32, got)


def _rs_finish(quarters):
    n = len(quarters)

    def body(*refs):
        bufs = refs[n:2 * n]
        send_sems, recv_sems = refs[2 * n:]
        x, y, c, _ = _place()
        copies = []
        for t in range(n):
            hr = bufs[t].shape[1] // 2
            mine = bufs[t].at[:, pl.ds(pl.multiple_of(c * hr, 8), hr), :]
            cp = _remote(mine, mine, send_sems.at[t], recv_sems.at[t], (x, y, 1 - c))
            cp.start()
            copies.append(cp)
        for cp in copies:
            cp.wait()

    return pl.pallas_call(
        body, name="reduce_finish", in_specs=[ANY] * n, out_specs=[ANY] * n,
        out_shape=[jax.ShapeDtypeStruct(a.shape, a.dtype) for a in quarters],
        input_output_aliases={t: t for t in range(n)},
        scratch_shapes=[pltpu.SemaphoreType.DMA((n,)), pltpu.SemaphoreType.DMA((n,))],
        compiler_params=pltpu.CompilerParams(has_side_effects=True),
    )(*quarters)


def _reduce_start(parts, kinds, names, tag):
    canon = [_canonical(p, kind) for p, kind in zip(parts, kinds)]
    lands = [lax.empty(a.shape[:2] + (a.shape[2] // 2, a.shape[3]), a.dtype) for a in canon]
    sems, srcs, lands, token = _split_start(_rs_sibling_copies, canon, lands, 2 * len(canon),
                                            f"reduce_sibling_start_{tag}")
    return (sems, srcs, lands, kinds, names, tag), token


def _reduce_middle(state, after):
    sems, srcs, lands, kinds, names, tag = state
    c_arr = jnp.reshape(lax.axis_index("c"), (1,)).astype(jnp.int32)
    srcs, from_sibling = _split_wait(_rs_sibling_copies, sems, srcs, lands, after, f"reduce_sibling_wait_{tag}")
    sums = [_rs_add_sibling(p, g, c_arr, f"reduce_add_sibling_{nm}") for p, g, nm in zip(srcs, from_sibling, names)]
    sums16 = [s16 for _, s16 in sums]
    copies = functools.partial(_rs_chips_copies, kinds=kinds)
    lands = [lax.empty((OTHER_CHIPS,) + _quarter4_shape(a, k), a.dtype) for a, k in zip(sums16, kinds)]
    sems, srcs, lands, token = _split_start(copies, sums16, lands, 2 * OTHER_CHIPS * len(sums16),
                                            f"reduce_chips_start_{tag}")
    return (sems, srcs, lands, [s32 for s32, _ in sums], kinds, names, tag), token


def _reduce_finish(state, after):
    sems, srcs, lands, sums32, kinds, names, tag = state
    x, y, c = lax.axis_index("x"), lax.axis_index("y"), lax.axis_index("c")
    kc = (jnp.reshape(2 * x + y, (1,)).astype(jnp.int32), jnp.reshape(c, (1,)).astype(jnp.int32))
    copies = functools.partial(_rs_chips_copies, kinds=kinds)
    _, from_chips = _split_wait(copies, sems, srcs, lands, after, f"reduce_chips_wait_{tag}")
    halves = [_rs_add_chips(s32, g, kind, kc, f"reduce_add_chips_{nm}")
              for s32, g, kind, nm in zip(sums32, from_chips, kinds, names)]
    return _rs_finish(halves)


SMALL_PEERS = 7


def _small_exchange(pack):
    rows = pack.shape[0]

    def body(p_ref, slots_ref, total_ref, send_sems, recv_sems):
        x, y, c, _ = _place()
        me = 4 * x + 2 * y + c
        slots_ref[me] = p_ref[...]
        copies = []
        for p in range(1, SMALL_PEERS + 1):
            px, py, pc = (p >> 2) & 1, (p >> 1) & 1, p & 1
            peer = (1 - x if px else x, 1 - y if py else y, 1 - c if pc else c)
            cp = _remote(p_ref, slots_ref.at[me], send_sems.at[p - 1], recv_sems.at[p - 1], peer)
            cp.start()
            copies.append(cp)
        for cp in copies:
            cp.wait()
        total = slots_ref[0]
        for i in range(1, SMALL_PEERS + 1):
            total = total + slots_ref[i]
        total_ref[...] = total

    vmem = pl.BlockSpec(memory_space=pltpu.VMEM)
    return pl.pallas_call(
        body, name="small_exchange", in_specs=[vmem], out_specs=[vmem, vmem],
        out_shape=[jax.ShapeDtypeStruct((SMALL_PEERS + 1, rows, LANES), F32), jax.ShapeDtypeStruct((rows, LANES), F32)],
        scratch_shapes=[pltpu.SemaphoreType.DMA((SMALL_PEERS,)), pltpu.SemaphoreType.DMA((SMALL_PEERS,))],
        compiler_params=pltpu.CompilerParams(has_side_effects=True),
    )(pack)


def _pack(arrays):
    rows = []
    for a in arrays:
        flat = a.reshape(-1).astype(F32)
        rows.append(jnp.pad(flat, (0, (-flat.shape[0]) % LANES)).reshape(-1, LANES))
    out = jnp.concatenate(rows, axis=0)
    return jnp.pad(out, ((0, (-out.shape[0]) % 8), (0, 0)))


def _unpack(pack, shapes):
    out, r = [], 0
    for sh in shapes:
        size = math.prod(sh)
        nr = -(-size // LANES)
        out.append(pack[r:r + nr].reshape(-1)[:size].reshape(sh))
        r += nr
    return out


def _adamw(w, g, m, v, name):
    shape = w.shape
    to2d = lambda a: a.reshape(-1, shape[-1])
    rows = math.prod(shape[:-1])
    tile = 256 if rows % 256 == 0 else rows

    def fn(wb, gb, mb, vb):
        m2 = ADAM_B1 * mb + (1.0 - ADAM_B1) * gb
        v2 = ADAM_B2 * vb + (1.0 - ADAM_B2) * (gb * gb)
        m_hat = m2 / (1.0 - ADAM_B1 ** ADAM_STEP)
        v_hat = v2 / (1.0 - ADAM_B2 ** ADAM_STEP)
        return -ADAM_LR * (m_hat / (jnp.sqrt(v_hat) + ADAM_EPS) + ADAM_WD * wb), m2, v2

    res = _rowwise(fn, [to2d(w), to2d(g), to2d(m), to2d(v)], [], [(shape[-1], F32)] * 3, [], tile=tile, name=name)
    return [r.reshape(shape) for r in res]


def _adamw_layer(w, g, m, v, layer, outs, name):
    _, rows, cols = w.shape
    tile = rows
    while tile * cols * 4 > (1 << 20) and tile % 16 == 0:
        tile //= 2

    def body(w_ref, g_ref, m_ref, v_ref, *rest):
        g_out, d_out, m_out, v_out = rest[-4:]
        gb = g_ref[...]
        m2 = ADAM_B1 * m_ref[...] + (1.0 - ADAM_B1) * gb
        v2 = ADAM_B2 * v_ref[...] + (1.0 - ADAM_B2) * (gb * gb)
        m_hat = m2 / (1.0 - ADAM_B1 ** ADAM_STEP)
        v_hat = v2 / (1.0 - ADAM_B2 ** ADAM_STEP)
        g_out[...] = gb
        d_out[...] = -ADAM_LR * (m_hat / (jnp.sqrt(v_hat) + ADAM_EPS) + ADAM_WD * w_ref[...])
        m_out[...] = m2
        v_out[...] = v2

    stacked = pl.BlockSpec((None, tile, cols), lambda i: (layer, i, 0))
    return pl.pallas_call(
        body, name=name, grid=(rows // tile,),
        in_specs=[stacked, pl.BlockSpec((None, tile, cols), lambda i: (0, i, 0)), stacked, stacked] + [ANY] * 4,
        out_specs=[stacked] * 4, out_shape=[jax.ShapeDtypeStruct(w.shape, F32)] * 4,
        input_output_aliases={4 + i: i for i in range(4)}, compiler_params=_cparams("parallel"),
    )(w, g, m, v, *outs)


BIG = (("ffn1_w_gu", "col"), ("ffn1_w_down", "row"), ("w_in_even", "col"), ("w_out_even", "row"),
       ("w_in_odd", "col"), ("w_out_odd", "row"), ("ffn2_w_gu", "col"), ("ffn2_w_down", "row"))
SMALL = ("norm_ffn1", "norm_mix", "dn_conv_w", "dn_a_log", "dn_dt_bias", "dn_norm_g", "fox_q_norm_g", "fox_k_norm_g",
         "fox_f_bias", "norm_ffn2")
WEIGHTS = ("norm_ffn1", "ffn1_w_gu", "ffn1_w_down", "norm_mix", "w_in_even", "dn_conv_w", "dn_a_log", "dn_dt_bias",
           "dn_norm_g", "fox_q_norm_g", "fox_k_norm_g", "fox_f_bias", "w_out_even", "w_in_odd", "w_out_odd",
           "norm_ffn2", "ffn2_w_gu", "ffn2_w_down")


def _step(x, target, w, m, v):
    k = 2 * lax.axis_index("x") + lax.axis_index("y")
    n_conv = w["dn_conv_w"].shape[2]

    kc = jnp.reshape(k, (1,)).astype(jnp.int32)
    kinds = dict(BIG)
    quarters = {name: w[name] for name in kinds}
    quarters["w_in_even"] = jnp.pad(w["w_in_even"], ((0, 0), (0, 0), (0, EVEN_QUARTER_PAD - EVEN_QUARTER)))
    first_names = [name for name in kinds if name not in ("w_in_odd", "w_out_odd")]
    rest_names = list(kinds)

    def even_columns(whole):
        padded = whole["w_in_even"]
        ref_order = jnp.concatenate([padded[..., q * EVEN_QUARTER_PAD:q * EVEN_QUARTER_PAD + EVEN_QUARTER]
                                     for q in range(4)], axis=-1)
        return {**whole, "w_in_even": _even_to_kernel_layout(ref_order)}

    conv_slots, _ = _small_exchange(_pack([w["dn_conv_w"]]))
    placed = [_place_quarter(quarters[name], kinds[name], kc, f"place_first_{name}", 0, 1) for name in first_names]
    gathered = _gather_weights(placed, [kinds[name] for name in first_names])
    first = even_columns(dict(zip(first_names, gathered)))
    placed = [_place_quarter(quarters[name], kinds[name], kc, f"place_rest_{name}", 1 if name in first_names else 0)
              for name in rest_names]
    rest_kinds = [kinds[name] for name in rest_names]
    sems, on_their_way, token = _gather_start(placed, rest_kinds, [conv_slots, *gathered], "rest")

    def rest_after(value):
        landed = _gather_wait(sems, on_their_way, rest_kinds, value, "rest")
        return even_columns(dict(zip(rest_names, _gather_forward(landed, rest_kinds, "rest"))))

    whole = {}
    conv_rows = math.prod(w["dn_conv_w"].shape) // LANES
    conv_quarters = [conv_slots[2 * q, :conv_rows].reshape(w["dn_conv_w"].shape) for q in range(4)]
    whole["dn_conv_w"] = jnp.concatenate(conv_quarters, axis=-1)
    for name in SMALL:
        if name != "dn_conv_w":
            whole[name] = w[name]

    updated = {name: [lax.empty(w[name].shape, F32) for _ in range(4)] for name in kinds}

    def on_reduced(layer, layer_grads):
        for name, g in layer_grads.items():
            if name == "w_in_even":
                g = g[..., :EVEN_QUARTER]
            stacked_layer = layer if w[name].shape[0] == w["norm_mix"].shape[0] else layer // 2
            updated[name] = _adamw_layer(w[name], g, m[name], v[name], stacked_layer, updated[name], f"adamw_{name}")

    loss, dx, small = _forward_backward(x, target, whole, first, rest_after, token, on_reduced)

    _, small_sum = _small_exchange(_pack([small[n] for n in SMALL]))
    grads = dict(zip(SMALL, _unpack(small_sum, [small[n].shape for n in SMALL])))
    grads["dn_conv_w"] = lax.dynamic_slice_in_dim(grads["dn_conv_w"], k * n_conv, n_conv, axis=2)
    delta, new_m, new_v = {}, {}, {}
    for name in kinds:
        grads[name], delta[name], new_m[name], new_v[name] = updated[name]
    packs = [_pack([d[n] for n in SMALL]) for d in (w, grads, m, v)]
    shapes = [w[n].shape for n in SMALL]
    for out, res in zip((delta, new_m, new_v), _adamw(*packs, "adamw_small")):
        out.update(zip(SMALL, _unpack(res, shapes)))
    total_loss = lax.psum(loss[0, 0], ("x", "y", "c"))
    return total_loss, dx, grads, delta, new_m, new_v


def kernel(x, norm_ffn1, ffn1_w_gu, ffn1_w_down, norm_mix, w_in_even, dn_conv_w, dn_a_log, dn_dt_bias, dn_norm_g, fox_q_norm_g, fox_k_norm_g, fox_f_bias, w_out_even, w_in_odd, w_out_odd, norm_ffn2, ffn2_w_gu, ffn2_w_down, loss_target, m_norm_ffn1, m_ffn1_w_gu, m_ffn1_w_down, m_norm_mix, m_w_in_even, m_dn_conv_w, m_dn_a_log, m_dn_dt_bias, m_dn_norm_g, m_fox_q_norm_g, m_fox_k_norm_g, m_fox_f_bias, m_w_out_even, m_w_in_odd, m_w_out_odd, m_norm_ffn2, m_ffn2_w_gu, m_ffn2_w_down, v_norm_ffn1, v_ffn1_w_gu, v_ffn1_w_down, v_norm_mix, v_w_in_even, v_dn_conv_w, v_dn_a_log, v_dn_dt_bias, v_dn_norm_g, v_fox_q_norm_g, v_fox_k_norm_g, v_fox_f_bias, v_w_out_even, v_w_in_odd, v_w_out_odd, v_norm_ffn2, v_ffn2_w_gu, v_ffn2_w_down):
    w = dict(zip(WEIGHTS, (norm_ffn1, ffn1_w_gu, ffn1_w_down, norm_mix, w_in_even, dn_conv_w, dn_a_log, dn_dt_bias,
                           dn_norm_g, fox_q_norm_g, fox_k_norm_g, fox_f_bias, w_out_even, w_in_odd, w_out_odd,
                           norm_ffn2, ffn2_w_gu, ffn2_w_down)))
    m = dict(zip(WEIGHTS, (m_norm_ffn1, m_ffn1_w_gu, m_ffn1_w_down, m_norm_mix, m_w_in_even, m_dn_conv_w, m_dn_a_log,
                           m_dn_dt_bias, m_dn_norm_g, m_fox_q_norm_g, m_fox_k_norm_g, m_fox_f_bias, m_w_out_even,
                           m_w_in_odd, m_w_out_odd, m_norm_ffn2, m_ffn2_w_gu, m_ffn2_w_down)))
    v = dict(zip(WEIGHTS, (v_norm_ffn1, v_ffn1_w_gu, v_ffn1_w_down, v_norm_mix, v_w_in_even, v_dn_conv_w, v_dn_a_log,
                           v_dn_dt_bias, v_dn_norm_g, v_fox_q_norm_g, v_fox_k_norm_g, v_fox_f_bias, v_w_out_even,
                           v_w_in_odd, v_w_out_odd, v_norm_ffn2, v_ffn2_w_gu, v_ffn2_w_down)))
    loss, dx, grads, delta, new_m, new_v = _step(x[0], loss_target[0], w, m, v)
    return (loss, dx[None], *[grads[n] for n in WEIGHTS], *[delta[n] for n in WEIGHTS],
            *[new_m[n] for n in WEIGHTS], *[new_v[n] for n in WEIGHTS])
```

```python
import functools
import math

import jax
import jax.numpy as jnp
from jax import lax
from jax.experimental import pallas as pl
from jax.experimental.pallas import tpu as pltpu

F32 = jnp.float32
BF16 = jnp.bfloat16
HI = lax.Precision.HIGH

HEAD_DIM = 128
N_DN_HEADS = 4
N_FOX_HEADS = 4
N_SB_HEADS = 8
D_DN = N_DN_HEADS * HEAD_DIM
D_FOX = N_FOX_HEADS * HEAD_DIM
CONV_WIDTH = 4
DN_CHUNK = 64
EPS = 1e-6
ATT_SCALE = HEAD_DIM ** -0.5
ADAM_LR, ADAM_B1, ADAM_B2, ADAM_EPS, ADAM_WD, ADAM_STEP = 0.001, 0.9, 0.999, 1e-08, 0.01, 10

V7X_VMEM_LIMIT = 56 * 1024 * 1024
LANES = 128
ATT_TQ = 512
ATT_TK = 256
ATT_SUB = ATT_TQ // ATT_TK

LANE_BETA, LANE_DECAY, LANE_FORGET = 0, 4, 8


def _cparams(*sem):
    return pltpu.CompilerParams(dimension_semantics=sem, vmem_limit_bytes=V7X_VMEM_LIMIT)


def _sigmoid(x):
    return 1.0 / (1.0 + jnp.exp(-x))


def _softplus(x):
    return jnp.maximum(x, 0.0) + jnp.log(1.0 + jnp.exp(-jnp.abs(x)))


def _silu_grad(y, sg):
    return sg * (1.0 + y * (1.0 - sg))


def _rowwise(fn, rows, bcast, outs, sums, *, tile, name):
    rows = [r if isinstance(r, tuple) else (r, r.shape[1], 0) for r in rows]
    s = rows[0][0].shape[0]
    assert s % tile == 0
    n_in, n_b, n_out, n_sum = len(rows), len(bcast), len(outs), len(sums)

    def body(*refs):
        ins = [r[...] for r in refs[:n_in + n_b]]
        res = fn(*ins)
        if not isinstance(res, (tuple, list)):
            res = (res,)
        out_refs = refs[n_in + n_b:n_in + n_b + n_out]
        sum_refs = refs[n_in + n_b + n_out:]
        for o_ref, val in zip(out_refs, res[:n_out]):
            o_ref[...] = val.astype(o_ref.dtype)
        if n_sum:
            @pl.when(pl.program_id(0) == 0)
            def _():
                for s_ref in sum_refs:
                    s_ref[...] = jnp.zeros_like(s_ref)
            for s_ref, val in zip(sum_refs, res[n_out:]):
                s_ref[...] += val

    in_specs = [pl.BlockSpec((tile, w), lambda i, cb=cb: (i, cb)) for _, w, cb in rows]
    in_specs += [pl.BlockSpec(b.shape, lambda i, nd=b.ndim: (0,) * nd) for b in bcast]
    out_specs = [pl.BlockSpec((tile, c), lambda i: (i, 0)) for c, _ in outs]
    out_specs += [pl.BlockSpec(sh, lambda i: (0, 0)) for sh in sums]
    out_shape = [jax.ShapeDtypeStruct((s, c), dt) for c, dt in outs]
    out_shape += [jax.ShapeDtypeStruct(sh, F32) for sh in sums]
    return pl.pallas_call(
        body, name=name, grid=(s // tile,), in_specs=in_specs, out_specs=out_specs, out_shape=out_shape,
        compiler_params=_cparams("arbitrary" if n_sum else "parallel"),
    )(*[r[0] for r in rows], *bcast)


def _norm_in_proj(x, gain, w_in, j, tn, out_dtype, name):
    s, d = x.shape
    n = w_in.shape[2]
    tm = 512
    assert s % tm == 0 and n % tn == 0

    def body(x_ref, gain_ref, w_ref, o_ref, h_ref):
        xb = x_ref[...]
        r = lax.rsqrt(jnp.mean(xb * xb, axis=-1, keepdims=True) + EPS)
        h = (xb * r * gain_ref[...]).astype(BF16)
        h_ref[...] = h
        o_ref[...] = lax.dot_general(h, w_ref[...], _DIMS["nn"], preferred_element_type=F32).astype(out_dtype)

    return pl.pallas_call(
        body, name=name, grid=(n // tn, s // tm),
        in_specs=[pl.BlockSpec((tm, d), lambda t, i: (i, 0)), pl.BlockSpec((1, d), lambda t, i: (0, 0)),
                  pl.BlockSpec((None, d, tn), lambda t, i: (j, 0, t))],
        out_specs=[pl.BlockSpec((tm, tn), lambda t, i: (i, t)), pl.BlockSpec((None, tm, d), lambda t, i: (t, i, 0))],
        out_shape=[jax.ShapeDtypeStruct((s, n), out_dtype), jax.ShapeDtypeStruct((n // tn, s, d), BF16)],
        compiler_params=_cparams("parallel", "parallel"),
    )(x, gain, w_in)


_DIMS = {"nn": (((1,), (0,)), ((), ())), "nt": (((1,), (1,)), ((), ())), "tn": (((0,), (0,)), ((), ()))}


def _dot(a, b, kind):
    return lax.dot_general(a.astype(BF16), b.astype(BF16), _DIMS[kind], preferred_element_type=F32)


def _dot32(a, b, kind="nn"):
    return lax.dot_general(a, b, _DIMS[kind], precision=HI, preferred_element_type=F32)


def _mm(a, b, kind, *, tm, tn, out_dtype, name, scale=None, residual=None, a_lead=(), b_lead=(),
        b_spec=None, n=None, into=None, after=None):
    ash, bsh = a.shape[len(a_lead):], b.shape[len(b_lead):]
    m = ash[1] if kind == "tn" else ash[0]
    k = ash[0] if kind == "tn" else ash[1]
    if b_spec is None:
        n = bsh[0] if kind == "nt" else bsh[1]
        assert k == (bsh[1] if kind == "nt" else bsh[0]), (ash, bsh, kind)
    assert m % tm == 0 and n % tn == 0, (m, tm, n, tn)
    la, lb = (None,) * len(a_lead), (None,) * len(b_lead)
    if kind == "tn":
        a_spec = pl.BlockSpec(la + (k, tm), lambda j, i: a_lead + (0, i))
    else:
        a_spec = pl.BlockSpec(la + (tm, k), lambda j, i: a_lead + (i, 0))
    if b_spec is None:
        if kind == "nt":
            b_spec = pl.BlockSpec(lb + (tn, k), lambda j, i: b_lead + (j, 0))
        else:
            b_spec = pl.BlockSpec(lb + (k, tn), lambda j, i: b_lead + (0, j))
    in_specs, args = [a_spec, b_spec], [a, b]
    if residual is not None:
        in_specs.append(pl.BlockSpec((tm, tn), lambda j, i: (i, j)))
        args.append(residual)
    aliases = {}
    if after is not None:
        in_specs.append(pl.BlockSpec(memory_space=pl.ANY))
        args.append(after)
    if into is not None:
        buf, layer = into
        in_specs.append(pl.BlockSpec(memory_space=pl.ANY))
        args.append(buf)
        aliases = {len(args) - 1: 0}
        out_spec = pl.BlockSpec((None, tm, tn), lambda j, i: (layer, i, j))
        out_shape = jax.ShapeDtypeStruct(buf.shape, buf.dtype)
    else:
        out_spec = pl.BlockSpec((tm, tn), lambda j, i: (i, j))
        out_shape = jax.ShapeDtypeStruct((m, n), out_dtype)

    def body(a_ref, b_ref, *rest):
        acc = _dot(a_ref[...], b_ref[...], kind)
        if scale is not None:
            acc = acc * scale
        if residual is not None:
            acc = acc + rest[0][...]
        rest[-1][...] = acc.astype(rest[-1].dtype)

    return pl.pallas_call(
        body, name=name, grid=(n // tn, m // tm), in_specs=in_specs, out_specs=out_spec, out_shape=out_shape,
        input_output_aliases=aliases, compiler_params=_cparams("parallel", "parallel"),
    )(*args)


FFN_TM = 1024


def _ffn_up(x, gain, w_gu, layer, name):
    s, d = x.shape
    f = w_gu.shape[2] // 2
    tm, tn = FFN_TM, f // 2
    nj = f // tn

    def body(x_ref, gain_ref, wg_ref, wu_ref, gu_ref, a_ref, n_ref):
        xb = x_ref[...]
        r = lax.rsqrt(jnp.mean(xb * xb, axis=-1, keepdims=True) + EPS)
        nv = (xb * r * gain_ref[...]).astype(BF16)
        n_ref[...] = nv
        g = _dot(nv, wg_ref[...], "nn")
        u = _dot(nv, wu_ref[...], "nn")
        gu_ref[0] = g.astype(BF16)
        gu_ref[1] = u.astype(BF16)
        a_ref[...] = (g * _sigmoid(g) * u).astype(BF16)

    return pl.pallas_call(
        body, name=name, grid=(nj, s // tm),
        in_specs=[pl.BlockSpec((tm, d), lambda j, i: (i, 0)),
                  pl.BlockSpec((1, d), lambda j, i: (0, 0)),
                  pl.BlockSpec((None, d, tn), lambda j, i: (layer, 0, j)),
                  pl.BlockSpec((None, d, tn), lambda j, i: (layer, 0, j + nj))],
        out_specs=[pl.BlockSpec((2, tm, tn), lambda j, i: (0, i, j)),
                   pl.BlockSpec((tm, tn), lambda j, i: (i, j)),
                   pl.BlockSpec((None, tm, d), lambda j, i: (j, i, 0))],
        out_shape=[jax.ShapeDtypeStruct((2, s, f), BF16), jax.ShapeDtypeStruct((s, f), BF16),
                   jax.ShapeDtypeStruct((nj, s, d), BF16)],
        compiler_params=_cparams("parallel", "parallel"),
    )(x, gain, w_gu, w_gu)


def _ffn_down_bwd(dxo, w_down, gu, layer, name, after=None):
    s, d = dxo.shape
    f = w_down.shape[1]
    tm, tn = FFN_TM, f // 2
    extra_specs, extra = ([ANY], [after]) if after is not None else ([], [])

    def body(dx_ref, w_ref, gu_ref, *rest):
        dgu_ref = rest[-1]
        da = 0.5 * _dot(dx_ref[...], w_ref[...], "nt")
        g = gu_ref[0].astype(F32)
        u = gu_ref[1].astype(F32)
        sg = _sigmoid(g)
        dgu_ref[0] = (da * u * _silu_grad(g, sg)).astype(BF16)
        dgu_ref[1] = (da * g * sg).astype(BF16)

    return pl.pallas_call(
        body, name=name, grid=(f // tn, s // tm),
        in_specs=[pl.BlockSpec((tm, d), lambda j, i: (i, 0)),
                  pl.BlockSpec((None, tn, d), lambda j, i: (layer, j, 0)),
                  pl.BlockSpec((2, tm, tn), lambda j, i: (0, i, j))] + extra_specs,
        out_specs=pl.BlockSpec((2, tm, tn), lambda j, i: (0, i, j)),
        out_shape=jax.ShapeDtypeStruct((2, s, f), BF16),
        compiler_params=_cparams("parallel", "parallel"),
    )(dxo, w_down, gu, *extra)


NORM_BWD_TM = 256


def _norm_bwd_after(terms, operands, specs, x, dres, gain, name):
    s, d = x.shape
    tm = NORM_BWD_TM
    n_op = len(operands)

    def body(*refs):
        x_ref, dres_ref, g_ref = refs[n_op:n_op + 3]
        dx_ref, dx16_ref, dgain_ref = refs[n_op + 3:]
        dn = None
        for a, b in terms(*refs[:n_op]):
            dn = _dot(a, b, "nt") if dn is None else dn + _dot(a, b, "nt")
        xb = x_ref[...]
        r = lax.rsqrt(jnp.mean(xb * xb, axis=-1, keepdims=True) + EPS)
        xh = xb * r
        dxh = dn * g_ref[...]
        dx = dres_ref[...] + r * (dxh - xh * jnp.mean(dxh * xh, axis=-1, keepdims=True))
        dx_ref[...] = dx
        dx16_ref[...] = dx.astype(BF16)

        @pl.when(pl.program_id(0) == 0)
        def _():
            dgain_ref[...] = jnp.zeros_like(dgain_ref)
        dgain_ref[...] += jnp.sum(dn * xh, axis=0, keepdims=True)

    rows = pl.BlockSpec((tm, d), lambda i: (i, 0))
    return pl.pallas_call(
        body, name=name, grid=(s // tm,),
        in_specs=list(specs) + [rows, rows, pl.BlockSpec((1, d), lambda i: (0, 0))],
        out_specs=[rows, rows, pl.BlockSpec((1, d), lambda i: (0, 0))],
        out_shape=[jax.ShapeDtypeStruct((s, d), F32), jax.ShapeDtypeStruct((s, d), BF16),
                   jax.ShapeDtypeStruct((1, d), F32)],
        compiler_params=_cparams("arbitrary"),
    )(*operands, x, dres, gain)


def _ffn_up_bwd(dgu, w_gu, layer, x, dres, gain, name):
    _, s, f = dgu.shape
    d = w_gu.shape[1]
    specs = [pl.BlockSpec((2, NORM_BWD_TM, f), lambda i: (0, i, 0)),
             pl.BlockSpec((None, d, f), lambda i: (layer, 0, 0)),
             pl.BlockSpec((None, d, f), lambda i: (layer, 0, 1))]
    terms = lambda dgu_ref, wg_ref, wu_ref: [(dgu_ref[0], wg_ref[...]), (dgu_ref[1], wu_ref[...])]
    return _norm_bwd_after(terms, [dgu, w_gu, w_gu], specs, x, dres, gain, name)


def _in_proj_bwd(dproj, w_in, j, x, dres, gain, name):
    k = dproj.shape[1]
    d = w_in.shape[1]
    specs = [pl.BlockSpec((NORM_BWD_TM, k), lambda i: (i, 0)), pl.BlockSpec((None, d, k), lambda i: (j, 0, 0))]
    terms = lambda a_ref, b_ref: [(a_ref[...], b_ref[...])]
    return _norm_bwd_after(terms, [dproj, w_in], specs, x, dres, gain, name)


def _ffn_fwd(x, gain, w_gu, w_down, layer, tag):
    gu, a, n = _ffn_up(x, gain, w_gu, layer, f"{tag}_up")
    x2 = _mm(a, w_down, "nn", tm=FFN_TM, tn=x.shape[1], out_dtype=F32, name=f"{tag}_down", scale=0.5, residual=x,
             b_lead=(layer,))
    return x2, (x, n, gu, a)


def _ffn_bwd(dxo, dxo16, saved, gain, w_gu, w_down, layer, tag, g_gu, g_down, after=None):
    x, n, gu, a = saved
    s, f = a.shape
    dgu = _ffn_down_bwd(dxo16, w_down, gu, layer, f"{tag}_down_bwd", after)
    g_down = _mm(a, dxo16, "tn", tm=256, tn=dxo16.shape[1], out_dtype=F32, name=f"{tag}_down_dw", scale=0.5,
                 into=(g_down, 0))
    tn = f // 2
    nj = f // tn
    g_gu = _mm(n, dgu, "tn", tm=512, tn=tn, out_dtype=F32, name=f"{tag}_up_dw", into=(g_gu, 0), n=2 * f, a_lead=(0,),
               b_spec=pl.BlockSpec((None, s, tn), lambda j, i: (j // nj, 0, j % nj)))
    dx, dx16, dgain = _ffn_up_bwd(dgu, w_gu, layer, x, dxo, gain, f"{tag}_up_bwd")
    return dx, dx16, dgain, g_gu, g_down


def _lane_col(blk, lane):
    li = lax.broadcasted_iota(jnp.int32, blk.shape, 1)
    return jnp.sum(jnp.where(li == lane, blk, 0.0), axis=1, keepdims=True)


def _split_dot(x, tri):
    hi = x.astype(BF16)
    lo = (x - hi.astype(F32)).astype(BF16)
    return (lax.dot_general(hi, tri, _DIMS["nn"], preferred_element_type=F32)
            + lax.dot_general(lo, tri, _DIMS["nn"], preferred_element_type=F32))


class _Each:
    def __init__(self, vals):
        self.vals = list(vals)

    def _with(self, other, op):
        others = other.vals if isinstance(other, _Each) else [other] * len(self.vals)
        return _Each(op(a, b) for a, b in zip(self.vals, others))

    def __add__(self, other):
        return self._with(other, lambda a, b: a + b)

    def __sub__(self, other):
        return self._with(other, lambda a, b: a - b)

    def __mul__(self, other):
        return self._with(other, lambda a, b: a * b)

    def __neg__(self):
        return _Each(-a for a in self.vals)


def _each(fn, *args):
    n = max(len(a.vals) for a in args if isinstance(a, _Each))
    res = [fn(*xs) for xs in zip(*[a.vals if isinstance(a, _Each) else [a] * n for a in args])]
    if isinstance(res[0], tuple):
        return tuple(_Each(r) for r in zip(*res))
    return _Each(res)


def _keep(cond, x):
    return _each(lambda v: jnp.where(cond, v, 0.0), x)


def _rowsum(x):
    return _each(lambda v: jnp.sum(v, axis=1, keepdims=True), x)


ATT_HEADS = 2
ATT_WIDTH = ATT_HEADS * HEAD_DIM
_HEAD_COLS = [slice(h * HEAD_DIM, (h + 1) * HEAD_DIM) for h in range(ATT_HEADS)]


def _att_specs(n_heads, s):
    groups = n_heads // ATT_HEADS
    q_spec = pl.BlockSpec((ATT_TQ, ATT_WIDTH), lambda g, i: (i, g))
    k_spec = pl.BlockSpec((s, ATT_WIDTH), lambda g, i: (0, groups + g))
    v_spec = pl.BlockSpec((s, ATT_WIDTH), lambda g, i: (0, 2 * groups + g))
    return q_spec, k_spec, v_spec


def _heads_of(ref, rows=None):
    return _Each(ref[:, cs] if rows is None else ref[rows, cs] for cs in _HEAD_COLS)


def _dot_each(a, b, kind):
    return _each(lambda x, y: _dot(x, y, kind), a, b)


def _att_iotas():
    row = lax.broadcasted_iota(jnp.int32, (ATT_TQ, ATT_TK), 0)
    col = lax.broadcasted_iota(jnp.int32, (ATT_TQ, ATT_TK), 1)
    jr = lax.broadcasted_iota(jnp.int32, (ATT_TK, ATT_TK), 0)
    jc = lax.broadcasted_iota(jnp.int32, (ATT_TK, ATT_TK), 1)
    return row, col, jr, jc


def _sb_fwd(qkv, n_heads, name):
    s = qkv.shape[0]

    def body(q_ref, k_ref, v_ref, o16_ref, o32_ref):
        i = pl.program_id(1)
        q = _heads_of(q_ref)
        row, col, jr, jc = _att_iotas()
        later = (jr > jc).astype(BF16)

        def step(jb, carry, diagonal):
            c_sp, acc = (_Each(part) for part in carry)
            work = []
            for sub in reversed(range(ATT_SUB)):
                keys = pl.ds(pl.multiple_of(jb * ATT_TQ + sub * ATT_TK, ATT_TK), ATT_TK)
                z = _dot_each(q, _heads_of(k_ref, keys), "nt") * ATT_SCALE
                sp = _each(_softplus, z)
                before = (col + sub * ATT_TK) < row if diagonal else None
                spm = _keep(before, sp) if diagonal else sp
                work.append((keys, z - sp, spm, _each(lambda x: _dot(x, later, "nn"), spm), before))
            for keys, logsig, spm, within, before in work:
                a = _each(jnp.exp, logsig - (c_sp + within))
                if diagonal:
                    a = _keep(before, a)
                acc = acc + _each(_split_dot, a, _heads_of(v_ref, keys))
                c_sp = c_sp + _rowsum(spm)
            return tuple(c_sp.vals), tuple(acc.vals)

        zeros = lambda width: tuple(jnp.zeros((ATT_TQ, width), F32) for _ in range(ATT_HEADS))
        carry = step(i, (zeros(1), zeros(HEAD_DIM)), True)
        _, acc = lax.fori_loop(0, i, lambda it, cr: step(i - 1 - it, cr, False), carry)
        for cs, acc_h in zip(_HEAD_COLS, acc):
            o16_ref[:, cs] = acc_h.astype(BF16)
            o32_ref[:, cs] = acc_h

    q_spec, k_spec, v_spec = _att_specs(n_heads, s)
    o_spec = pl.BlockSpec((ATT_TQ, ATT_WIDTH), lambda g, i: (i, g))
    return pl.pallas_call(
        body, name=name, grid=(n_heads // ATT_HEADS, s // ATT_TQ), in_specs=[q_spec, k_spec, v_spec],
        out_specs=[o_spec, o_spec],
        out_shape=[jax.ShapeDtypeStruct((s, n_heads * HEAD_DIM), BF16),
                   jax.ShapeDtypeStruct((s, n_heads * HEAD_DIM), F32)],
        compiler_params=_cparams("parallel", "arbitrary"),
    )(qkv, qkv, qkv)


def _sb_bwd(qkv, o32, do, n_heads, name):
    s = qkv.shape[0]

    def body(q_ref, k_ref, v_ref, o_ref, do_ref, dq_ref, dk_ref, dv_ref):
        i = pl.program_id(1)

        @pl.when(i == 0)
        def _():
            dk_ref[...] = jnp.zeros_like(dk_ref)
            dv_ref[...] = jnp.zeros_like(dv_ref)

        q, do = _heads_of(q_ref), _heads_of(do_ref)
        total = _rowsum(_each(lambda a, b: a.astype(F32) * b, do, _heads_of(o_ref)))
        row, col, jr, jc = _att_iotas()
        later = (jr > jc).astype(BF16)
        not_before = (jr >= jc).astype(BF16)

        def step(jb, carry, diagonal):
            c_sp, c_e, dq = (_Each(part) for part in carry)
            work = []
            for sub in reversed(range(ATT_SUB)):
                keys = pl.ds(pl.multiple_of(jb * ATT_TQ + sub * ATT_TK, ATT_TK), ATT_TK)
                k = _heads_of(k_ref, keys)
                z = _dot_each(q, k, "nt") * ATT_SCALE
                sp = _each(_softplus, z)
                before = (col + sub * ATT_TK) < row if diagonal else None
                spm = _keep(before, sp) if diagonal else sp
                work.append((keys, k, _each(jnp.exp, z - sp), spm, _each(lambda x: _dot(x, later, "nn"), spm),
                             _dot_each(do, _heads_of(v_ref, keys), "nt"), before))
            for keys, k, sig, spm, within, da, before in work:
                a = sig * _each(lambda x: jnp.exp(-x), c_sp + within)
                if diagonal:
                    a = _keep(before, a)
                e = a * da
                left = total - c_e - _each(lambda x: _split_dot(x, not_before), e)
                dz = (e - (e + left) * sig) * ATT_SCALE
                if diagonal:
                    dz = _keep(before, dz)
                dk, dv = _dot_each(dz, q, "tn"), _dot_each(a, do, "tn")
                for cs, dk_h, dv_h in zip(_HEAD_COLS, dk.vals, dv.vals):
                    dk_ref[keys, cs] += dk_h
                    dv_ref[keys, cs] += dv_h
                dq = dq + _dot_each(dz, k, "nn")
                c_sp = c_sp + _rowsum(spm)
                c_e = c_e + _rowsum(e)
            return tuple(c_sp.vals), tuple(c_e.vals), tuple(dq.vals)

        zeros = lambda width: tuple(jnp.zeros((ATT_TQ, width), F32) for _ in range(ATT_HEADS))
        carry = step(i, (zeros(1), zeros(1), zeros(HEAD_DIM)), True)
        _, _, dq = lax.fori_loop(0, i, lambda it, cr: step(i - 1 - it, cr, False), carry)
        for cs, dq_h in zip(_HEAD_COLS, dq):
            dq_ref[:, cs] = dq_h.astype(BF16)

    q_spec, k_spec, v_spec = _att_specs(n_heads, s)
    blk = pl.BlockSpec((ATT_TQ, ATT_WIDTH), lambda g, i: (i, g))
    full = pl.BlockSpec((s, ATT_WIDTH), lambda g, i: (0, g))
    wide = (s, n_heads * HEAD_DIM)
    return pl.pallas_call(
        body, name=name, grid=(n_heads // ATT_HEADS, s // ATT_TQ), in_specs=[q_spec, k_spec, v_spec, blk, blk],
        out_specs=[blk, full, full],
        out_shape=[jax.ShapeDtypeStruct(wide, BF16), jax.ShapeDtypeStruct(wide, F32), jax.ShapeDtypeStruct(wide, F32)],
        compiler_params=_cparams("parallel", "arbitrary"),
    )(qkv, qkv, qkv, o32, do)


def _fox_logits(q, k, cq, ct_ref, keys):
    ck = _Each(ct_ref[h, :, keys] for h in range(ATT_HEADS))
    return _dot_each(q, k, "nt") * ATT_SCALE + (cq - ck)


def _fox_cq(c_ref, group):
    c = c_ref[...]
    return _Each(_lane_col(c, LANE_FORGET + group * ATT_HEADS + h) for h in range(ATT_HEADS))


def _fox_fwd(qkv, c, ct, name):
    s = qkv.shape[0]
    n_heads = N_FOX_HEADS

    def body(q_ref, k_ref, v_ref, c_ref, ct_ref, o_ref, lse_ref):
        g, i = pl.program_id(0), pl.program_id(1)
        q = _heads_of(q_ref)
        cq = _fox_cq(c_ref, g)
        row, col, _, _ = _att_iotas()

        def step(jb, carry, diagonal):
            m, l, acc = (_Each(part) for part in carry)
            work = []
            m_new = m
            for sub in range(ATT_SUB):
                keys = pl.ds(pl.multiple_of(jb * ATT_TQ + sub * ATT_TK, ATT_TK), ATT_TK)
                sc = _fox_logits(q, _heads_of(k_ref, keys), cq, ct_ref, keys)
                valid = (col + sub * ATT_TK) <= row if diagonal else None
                if diagonal:
                    sc = _each(lambda x: jnp.where(valid, x, -1e30), sc)
                m_new = _each(lambda a, x: jnp.maximum(a, jnp.max(x, axis=1, keepdims=True)), m_new, sc)
                work.append((keys, sc, valid))
            w = _each(jnp.exp, m - m_new)
            l, acc = l * w, acc * w
            for keys, sc, valid in work:
                p = _each(jnp.exp, sc - m_new)
                if diagonal:
                    p = _keep(valid, p)
                l = l + _rowsum(p)
                acc = acc + _each(_split_dot, p, _heads_of(v_ref, keys))
            return tuple(m_new.vals), tuple(l.vals), tuple(acc.vals)

        per_head = lambda width, value: tuple(jnp.full((ATT_TQ, width), value, F32) for _ in range(ATT_HEADS))
        init = (per_head(1, -1e30), per_head(1, 0.0), per_head(HEAD_DIM, 0.0))
        m, l, acc = lax.fori_loop(0, i, lambda jb, cr: step(jb, cr, False), step(i, init, True))
        for h, cs in enumerate(_HEAD_COLS):
            o_ref[:, cs] = acc[h] / l[h]
            lse_ref[h] = jnp.broadcast_to(m[h] + jnp.log(l[h]), (ATT_TQ, LANES))

    q_spec, k_spec, v_spec = _att_specs(n_heads, s)
    return pl.pallas_call(
        body, name=name, grid=(n_heads // ATT_HEADS, s // ATT_TQ),
        in_specs=[q_spec, k_spec, v_spec, pl.BlockSpec((ATT_TQ, LANES), lambda g, i: (i, 0)),
                  pl.BlockSpec((ATT_HEADS, 1, s), lambda g, i: (g, 0, 0))],
        out_specs=[pl.BlockSpec((ATT_TQ, ATT_WIDTH), lambda g, i: (i, g)),
                   pl.BlockSpec((ATT_HEADS, ATT_TQ, LANES), lambda g, i: (g, i, 0))],
        out_shape=[jax.ShapeDtypeStruct((s, n_heads * HEAD_DIM), F32),
                   jax.ShapeDtypeStruct((n_heads, s, LANES), F32)],
        compiler_params=_cparams("parallel", "arbitrary"),
    )(qkv, qkv, qkv, c, ct)


def _fox_bwd(qkv, c, ct, o, lse, do, name):
    s = qkv.shape[0]
    n_heads = N_FOX_HEADS

    def body(q_ref, k_ref, v_ref, c_ref, ct_ref, o_ref, lse_ref, do_ref, dq_ref, dk_ref, dv_ref, dct_ref):
        g, i = pl.program_id(0), pl.program_id(1)

        @pl.when(i == 0)
        def _():
            dk_ref[...] = jnp.zeros_like(dk_ref)
            dv_ref[...] = jnp.zeros_like(dv_ref)
            dct_ref[...] = jnp.zeros_like(dct_ref)

        q = _heads_of(q_ref)
        do16 = _each(lambda x: x.astype(BF16), _heads_of(do_ref))
        delta = _rowsum(_each(lambda a, b: a.astype(F32) * b, do16, _heads_of(o_ref)))
        lse_col = _Each(lse_ref[h, :, 0:1] for h in range(ATT_HEADS))
        cq = _fox_cq(c_ref, g)
        row, col, _, _ = _att_iotas()

        def step(jb, dq, diagonal):
            dq = _Each(dq)
            for sub in range(ATT_SUB):
                keys = pl.ds(pl.multiple_of(jb * ATT_TQ + sub * ATT_TK, ATT_TK), ATT_TK)
                k = _heads_of(k_ref, keys)
                sc = _fox_logits(q, k, cq, ct_ref, keys)
                if diagonal:
                    valid = (col + sub * ATT_TK) <= row
                    p = _keep(valid, _each(jnp.exp, _keep(valid, sc) - lse_col))
                else:
                    p = _each(jnp.exp, sc - lse_col)
                ds = p * (_dot_each(do16, _heads_of(v_ref, keys), "nt") - delta)
                dss = ds * ATT_SCALE
                dk, dv = _dot_each(dss, q, "tn"), _dot_each(p, do16, "tn")
                for h, cs in enumerate(_HEAD_COLS):
                    dct_ref[h, :, keys] -= jnp.sum(ds.vals[h], axis=0, keepdims=True)
                    dk_ref[keys, cs] += dk.vals[h]
                    dv_ref[keys, cs] += dv.vals[h]
                dq = dq + _dot_each(dss, k, "nn")
            return tuple(dq.vals)

        dq0 = step(i, tuple(jnp.zeros((ATT_TQ, HEAD_DIM), F32) for _ in range(ATT_HEADS)), True)
        dq = lax.fori_loop(0, i, lambda jb, dq: step(jb, dq, False), dq0)
        for cs, dq_h in zip(_HEAD_COLS, dq):
            dq_ref[:, cs] = dq_h

    q_spec, k_spec, v_spec = _att_specs(n_heads, s)
    blk = pl.BlockSpec((ATT_TQ, ATT_WIDTH), lambda g, i: (i, g))
    full = pl.BlockSpec((s, ATT_WIDTH), lambda g, i: (0, g))
    wide = jax.ShapeDtypeStruct((s, n_heads * HEAD_DIM), F32)
    return pl.pallas_call(
        body, name=name, grid=(n_heads // ATT_HEADS, s // ATT_TQ),
        in_specs=[q_spec, k_spec, v_spec, pl.BlockSpec((ATT_TQ, LANES), lambda g, i: (i, 0)),
                  pl.BlockSpec((ATT_HEADS, 1, s), lambda g, i: (g, 0, 0)), blk,
                  pl.BlockSpec((ATT_HEADS, ATT_TQ, LANES), lambda g, i: (g, i, 0)), blk],
        out_specs=[blk, full, full, pl.BlockSpec((ATT_HEADS, 1, s), lambda g, i: (g, 0, 0))],
        out_shape=[wide, wide, wide, jax.ShapeDtypeStruct((n_heads, 1, s), F32)],
        compiler_params=_cparams("parallel", "arbitrary"),
    )(qkv, qkv, qkv, c, ct, o, lse, do)


def _cumsum_rows(x, reverse, name):
    s = x.shape[0]
    nb = s // LANES

    def body(x_ref, o_ref):
        r = lax.broadcasted_iota(jnp.int32, (LANES, LANES), 0)
        c = lax.broadcasted_iota(jnp.int32, (LANES, LANES), 1)
        tri = ((r <= c) if reverse else (r >= c)).astype(F32)

        def step(it, carry):
            b = (nb - 1 - it) if reverse else it
            off = pl.multiple_of(b * LANES, LANES)
            blk = x_ref[pl.ds(off, LANES), :]
            o_ref[pl.ds(off, LANES), :] = _dot32(tri, blk) + carry
            return carry + jnp.sum(blk, axis=0, keepdims=True)

        lax.fori_loop(0, nb, step, jnp.zeros((1, LANES), F32))

    return pl.pallas_call(body, name=name, out_shape=jax.ShapeDtypeStruct(x.shape, F32),
                          compiler_params=pltpu.CompilerParams(vmem_limit_bytes=V7X_VMEM_LIMIT))(x)


def _dot32_each(a, b, kind="nn"):
    return _each(lambda x, y: _dot32(x, y, kind), a, b)


def _unit_lower_inverse(m, ri, ci):
    c = ri.shape[0]
    t = -_keep(ri // 2 == ci // 2, m) + jnp.where(ri == ci, 1.0, 0.0)
    b = 4
    while b <= c:
        off_diag = (ri // b == ci // b) & (ri % b >= b // 2) & (ci % b < b // 2)
        t = t - _dot32_each(_dot32_each(t, _keep(off_diag, m)), t)
        b *= 2
    return t


def _dn_gates(g, ri, ci):
    eye = ri == ci
    incl = ri >= ci
    g_row = jnp.sum(jnp.where(eye, g, 0.0), axis=0, keepdims=True)
    gc = jnp.sum(jnp.where(incl, g_row, 0.0), axis=1, keepdims=True)
    gc_row = jnp.sum(jnp.where(eye, gc, 0.0), axis=0, keepdims=True)
    dmat = jnp.where(incl, jnp.exp(jnp.where(incl, gc - gc_row, 0.0)), 0.0)
    gc_last = jnp.sum(g, axis=0, keepdims=True)
    return gc, dmat, jnp.exp(gc), jnp.exp(gc_last - gc), jnp.exp(gc_last)


def _dn_fwd(qkv, act, name):
    s = qkv.shape[0]
    c, d, nh = DN_CHUNK, HEAD_DIM, N_DN_HEADS
    nc = s // c

    def body(q_ref, k_ref, v_ref, act_ref, o_ref, s_ref, t_ref, state):
        @pl.when(pl.program_id(0) == 0)
        def _():
            state[...] = jnp.zeros_like(state)

        ri = lax.broadcasted_iota(jnp.int32, (c, c), 0)
        ci = lax.broadcasted_iota(jnp.int32, (c, c), 1)
        act = act_ref[...]
        heads = range(nh)
        cols = [slice(h * d, (h + 1) * d) for h in heads]
        q, k, v = (_Each(ref[:, cs] for cs in cols) for ref in (q_ref, k_ref, v_ref))
        beta = _Each(_lane_col(act, LANE_BETA + h) for h in heads)
        g = _Each(_lane_col(act, LANE_DECAY + h) for h in heads)
        _, dmat, e, r, gl = _each(lambda gh: _dn_gates(gh, ri, ci), g)
        s0 = _Each(state[h] for h in heads)
        kb = beta * k
        t = _unit_lower_inverse(_keep(ri > ci, _dot32_each(kb, k, "nt") * dmat), ri, ci)
        vn = _dot32_each(t, beta * v) - _dot32_each(_dot32_each(t, kb * e), s0)
        o = _dot32_each(q * e, s0) + _dot32_each(_dot32_each(q, k, "nt") * dmat, vn)
        s1 = s0 * gl + _dot32_each(k * r, vn, "tn")
        for h in heads:
            o_ref[:, cols[h]] = o.vals[h]
            state[h] = s1.vals[h]
            s_ref[h] = s0.vals[h]
            t_ref[h] = t.vals[h]

    wide = lambda part: pl.BlockSpec((c, nh * d), lambda n: (n, part))
    return pl.pallas_call(
        body, name=name, grid=(nc,),
        in_specs=[wide(0), wide(1), wide(2), pl.BlockSpec((c, LANES), lambda n: (n, 0))],
        out_specs=[wide(0), pl.BlockSpec((nh, None, d, d), lambda n: (0, n, 0, 0)),
                   pl.BlockSpec((nh, None, c, c), lambda n: (0, n, 0, 0))],
        out_shape=[jax.ShapeDtypeStruct((s, nh * d), F32), jax.ShapeDtypeStruct((nh, nc, d, d), F32),
                   jax.ShapeDtypeStruct((nh, nc, c, c), F32)],
        scratch_shapes=[pltpu.VMEM((nh, d, d), F32)],
        compiler_params=_cparams("arbitrary"),
    )(qkv, qkv, qkv, act)


def _dn_bwd(qkv, act, states, tinv, do, name):
    s = qkv.shape[0]
    c, d, nh = DN_CHUNK, HEAD_DIM, N_DN_HEADS
    nc = s // c

    def chunk_bwd(q, k, v, do, beta, g, s0, t, ds_out):
        ri = lax.broadcasted_iota(jnp.int32, (c, c), 0)
        ci = lax.broadcasted_iota(jnp.int32, (c, c), 1)
        eye, incl, strict = ri == ci, ri >= ci, ri > ci
        gc, dmat, e, r, gl = _each(lambda gh: _dn_gates(gh, ri, ci), g)
        dot = _dot32_each
        rowsum = lambda x: _each(lambda a: jnp.sum(a, axis=1, keepdims=True), x)
        colsum = lambda x: _each(lambda a: jnp.sum(a, axis=0, keepdims=True), x)
        total = lambda x: colsum(rowsum(x))
        to_col = lambda row: rowsum(_keep(eye, row))
        to_row = lambda colv: colsum(_keep(eye, colv))

        kb, vb = beta * k, beta * v
        kbe = kb * e
        u, w = dot(t, vb), dot(t, kbe)
        vn = u - dot(w, s0)
        qk = dot(q, k, "nt")
        p = qk * dmat
        gram = dot(k, k, "nt")
        kr, qe = k * r, q * e

        d_kr = dot(vn, ds_out, "nt")
        dvn = dot(kr, ds_out)
        dgl = total(s0 * ds_out)
        ds_in = ds_out * gl
        dk = d_kr * r
        dr = rowsum(d_kr * k)
        d_qe = dot(do, s0, "nt")
        ds_in = ds_in + dot(qe, do, "tn")
        dp = _keep(incl, dot(do, vn, "nt"))
        dvn = dvn + dot(p, do, "tn")
        dq = d_qe * e
        de = rowsum(d_qe * q)
        dqk = dp * dmat
        dq = dq + dot(dqk, k)
        dk = dk + dot(dqk, q, "tn")
        dd = dp * qk
        dw = -dot(dvn, s0, "nt")
        ds_in = ds_in - dot(w, dvn, "tn")
        dvb = dot(t, dvn, "tn")
        dkbe = dot(t, dw, "tn")
        dm = -_keep(strict, dot(dvb, u, "nt") + dot(dkbe, w, "nt"))
        dbeta = rowsum(dm * gram * dmat)
        dgram = dm * beta * dmat
        dd = dd + dm * beta * gram
        dk = dk + dot(dgram, k) + dot(dgram, k, "tn")
        dkb = dkbe * e
        de = de + rowsum(dkbe * kb)
        dk = dk + beta * dkb
        dbeta = dbeta + rowsum(dkb * k) + rowsum(dvb * v)
        dv = beta * dvb
        wd = dd * dmat
        dgc = rowsum(wd) - to_col(colsum(wd)) + de * e - dr * r
        dgc_last = total(dr * r) + dgl * gl
        dgc = dgc + _keep(ri[:, 0:1] == c - 1, dgc_last)
        dg = rowsum(_keep(ri <= ci, to_row(dgc)))
        return dq, dk, dv, dbeta, dg, ds_in

    def body(q_ref, k_ref, v_ref, act_ref, s_ref, t_ref, do_ref, dq_ref, dk_ref, dv_ref, dact_ref, dstate):
        @pl.when(pl.program_id(0) == 0)
        def _():
            dstate[...] = jnp.zeros_like(dstate)

        act = act_ref[...]
        heads = range(nh)
        cols = [slice(h * d, (h + 1) * d) for h in heads]
        q, k, v, do = (_Each(ref[:, cs] for cs in cols) for ref in (q_ref, k_ref, v_ref, do_ref))
        dq, dk, dv, dbeta, dg, ds_in = chunk_bwd(
            q, k, v, do, _Each(_lane_col(act, LANE_BETA + h) for h in heads),
            _Each(_lane_col(act, LANE_DECAY + h) for h in heads), _Each(s_ref[h] for h in heads),
            _Each(t_ref[h] for h in heads), _Each(dstate[h] for h in heads))
        lane = lax.broadcasted_iota(jnp.int32, (c, LANES), 1)
        dact = jnp.zeros((c, LANES), F32)
        for h in heads:
            dstate[h] = ds_in.vals[h]
            dq_ref[:, cols[h]], dk_ref[:, cols[h]], dv_ref[:, cols[h]] = dq.vals[h], dk.vals[h], dv.vals[h]
            dact = (dact + jnp.where(lane == LANE_BETA + h, dbeta.vals[h], 0.0)
                    + jnp.where(lane == LANE_DECAY + h, dg.vals[h], 0.0))
        dact_ref[...] = dact

    part = lambda p: pl.BlockSpec((c, nh * d), lambda n: (nc - 1 - n, p))
    per = lambda a, b: pl.BlockSpec((nh, None, a, b), lambda n: (0, nc - 1 - n, 0, 0))
    wide = jax.ShapeDtypeStruct((s, nh * d), F32)
    act_spec = pl.BlockSpec((c, LANES), lambda n: (nc - 1 - n, 0))
    return pl.pallas_call(
        body, name=name, grid=(nc,),
        in_specs=[part(0), part(1), part(2), act_spec, per(d, d), per(c, c), part(0)],
        out_specs=[part(0), part(0), part(0), act_spec],
        out_shape=[wide, wide, wide, jax.ShapeDtypeStruct((s, LANES), F32)],
        scratch_shapes=[pltpu.VMEM((nh, d, d), F32)],
        compiler_params=_cparams("arbitrary"),
    )(qkv, qkv, qkv, act, states, tinv, do)


EVEN_DN_QKV, EVEN_FOX_QKV, EVEN_DN_GATE, EVEN_FOX_GATE, EVEN_NARROW = 0, 1536, 3072, 3584, 4096
EVEN_WIDTH = 4224
CONV_TILE = 256
CONV_HALO = 8


def _conv_fwd(proj, w, name):
    s = proj.shape[0]
    t, cw = CONV_TILE, 3 * D_DN

    def body(cur_ref, prev_ref, w_ref, y_ref, xs):
        i = pl.program_id(0)
        xs[0:CONV_HALO, :] = jnp.where(i > 0, prev_ref[...], 0.0)
        xs[CONV_HALO:, :] = cur_ref[...]
        y = jnp.zeros((t, cw), F32)
        for tap in range(CONV_WIDTH):
            y = y + w_ref[tap:tap + 1, :] * xs[pl.ds(CONV_HALO - CONV_WIDTH + 1 + tap, t), :]
        y_ref[...] = y

    per = t // CONV_HALO
    return pl.pallas_call(
        body, name=name, grid=(s // t,),
        in_specs=[pl.BlockSpec((t, cw), lambda i: (i, 0)),
                  pl.BlockSpec((CONV_HALO, cw), lambda i: (jnp.maximum(i * per - 1, 0), 0)),
                  pl.BlockSpec((CONV_WIDTH, cw), lambda i: (0, 0))],
        out_specs=pl.BlockSpec((t, cw), lambda i: (i, 0)),
        out_shape=jax.ShapeDtypeStruct((s, cw), F32),
        scratch_shapes=[pltpu.VMEM((t + CONV_HALO, cw), F32)],
        compiler_params=_cparams("parallel"),
    )(proj, proj, w)


def _conv_bwd(proj, w, dy, name):
    s = proj.shape[0]
    t, cw = CONV_TILE, 3 * D_DN
    nt = s // t

    def body(cur_ref, prev_ref, w_ref, dy_ref, nxt_ref, dx_ref, dw_ref, xs, dys):
        i = pl.program_id(0)

        @pl.when(i == 0)
        def _():
            dw_ref[...] = jnp.zeros_like(dw_ref)

        xs[0:CONV_HALO, :] = jnp.where(i > 0, prev_ref[...], 0.0)
        xs[CONV_HALO:, :] = cur_ref[...]
        dys[0:t, :] = dy_ref[...]
        dys[t:, :] = jnp.where(i < nt - 1, nxt_ref[...], 0.0)
        dy = dy_ref[...]
        dx = jnp.zeros((t, cw), F32)
        for tap in range(CONV_WIDTH):
            dx = dx + w_ref[tap:tap + 1, :] * dys[pl.ds(CONV_WIDTH - 1 - tap, t), :]
            dw_ref[tap:tap + 1, :] += jnp.sum(dy * xs[pl.ds(CONV_HALO - CONV_WIDTH + 1 + tap, t), :], axis=0,
                                              keepdims=True)
        dx_ref[...] = dx.astype(BF16)

    per = t // CONV_HALO
    last = s // CONV_HALO - 1
    return pl.pallas_call(
        body, name=name, grid=(nt,),
        in_specs=[pl.BlockSpec((t, cw), lambda i: (i, 0)),
                  pl.BlockSpec((CONV_HALO, cw), lambda i: (jnp.maximum(i * per - 1, 0), 0)),
                  pl.BlockSpec((CONV_WIDTH, cw), lambda i: (0, 0)),
                  pl.BlockSpec((t, cw), lambda i: (i, 0)),
                  pl.BlockSpec((CONV_HALO, cw), lambda i: (jnp.minimum((i + 1) * per, last), 0))],
        out_specs=[pl.BlockSpec((t, cw), lambda i: (i, 0)), pl.BlockSpec((CONV_WIDTH, cw), lambda i: (0, 0))],
        out_shape=[jax.ShapeDtypeStruct((s, cw), BF16), jax.ShapeDtypeStruct((CONV_WIDTH, cw), F32)],
        scratch_shapes=[pltpu.VMEM((t + CONV_HALO, cw), F32), pltpu.VMEM((t + CONV_HALO, cw), F32)],
        compiler_params=_cparams("arbitrary"),
    )(proj, proj, w, dy, dy)


def _heads(x, n):
    return [x[:, HEAD_DIM * h:HEAD_DIM * (h + 1)] for h in range(n)]


def _dn_pre_fwd(y, name):
    def fn(yb):
        cs = yb * _sigmoid(yb)
        out = []
        for idx, xh in enumerate(_heads(cs, 3 * N_DN_HEADS)):
            if idx < 2 * N_DN_HEADS:
                xh = xh * lax.rsqrt(jnp.sum(xh * xh, axis=-1, keepdims=True) + EPS)
                if idx < N_DN_HEADS:
                    xh = xh * ATT_SCALE
            out.append(xh)
        return (jnp.concatenate(out, axis=1),)
    return _rowwise(fn, [y], [], [(y.shape[1], F32)], [], tile=256, name=name)[0]


def _dn_pre_bwd(y, dq, dk, dv, name):
    def fn(yb, dqb, dkb, dvb):
        sg = _sigmoid(yb)
        cs = yb * sg
        dout = _heads(dqb, N_DN_HEADS) + _heads(dkb, N_DN_HEADS) + _heads(dvb, N_DN_HEADS)
        dcs = []
        for idx, (xh, dh) in enumerate(zip(_heads(cs, 3 * N_DN_HEADS), dout)):
            if idx < 2 * N_DN_HEADS:
                if idx < N_DN_HEADS:
                    dh = dh * ATT_SCALE
                r = lax.rsqrt(jnp.sum(xh * xh, axis=-1, keepdims=True) + EPS)
                xhat = xh * r
                dh = r * (dh - xhat * jnp.sum(xhat * dh, axis=-1, keepdims=True))
            dcs.append(dh)
        return (jnp.concatenate(dcs, axis=1) * _silu_grad(yb, sg),)
    return _rowwise(fn, [y, dq, dk, dv], [], [(y.shape[1], F32)], [], tile=256, name=name)[0]


def _narrow_params(a_log, dt_bias, f_bias):
    lanes = lambda a, first: jnp.pad(a.reshape(1, -1), ((0, 0), (first, LANES - first - a.shape[0])))
    return jnp.concatenate([lanes(a_log, LANE_DECAY), lanes(dt_bias, LANE_DECAY), lanes(f_bias, LANE_FORGET),
                            jnp.zeros((5, LANES), F32)], axis=0)


def _narrow_masks(shape):
    lane = lax.broadcasted_iota(jnp.int32, shape, 1)
    is_beta = lane < LANE_DECAY
    is_decay = (lane >= LANE_DECAY) & (lane < LANE_FORGET)
    is_forget = (lane >= LANE_FORGET) & (lane < LANE_FORGET + N_FOX_HEADS)
    return is_beta, is_decay, is_forget


def _narrow_fwd(proj, params, name):
    def fn(sm, pk):
        is_beta, is_decay, is_forget = _narrow_masks(sm.shape)
        g = -jnp.exp(pk[0:1, :]) * _softplus(sm + pk[1:2, :])
        logf = -_softplus(-(sm + pk[2:3, :]))
        return (jnp.where(is_beta, _sigmoid(sm), jnp.where(is_decay, g, jnp.where(is_forget, logf, 0.0))),)
    return _rowwise(fn, [(proj, LANES, EVEN_NARROW // LANES)], [params], [(LANES, F32)], [], tile=512, name=name)[0]


def _narrow_bwd(proj, params, act, dact, dlogf, name):
    def fn(sm, ab, da, dl, pk):
        is_beta, is_decay, is_forget = _narrow_masks(sm.shape)
        db = jnp.where(is_forget, dl, da)
        d_beta = db * ab * (1.0 - ab)
        d_decay = db * (-jnp.exp(pk[0:1, :])) * _sigmoid(sm + pk[1:2, :])
        d_forget = db * _sigmoid(-(sm + pk[2:3, :]))
        dsm = jnp.where(is_beta, d_beta, jnp.where(is_decay, d_decay, jnp.where(is_forget, d_forget, 0.0)))
        col = lambda x: jnp.sum(x, axis=0, keepdims=True)
        return (dsm, col(jnp.where(is_decay, db * ab, 0.0)), col(jnp.where(is_decay, dsm, 0.0)),
                col(jnp.where(is_forget, dsm, 0.0)))
    return _rowwise(fn, [(proj, LANES, EVEN_NARROW // LANES), act, dact, dlogf], [params], [(LANES, BF16)],
                    [(1, LANES)] * 3, tile=512, name=name)


def _head_rms(xh):
    r = lax.rsqrt(jnp.mean(xh * xh, axis=-1, keepdims=True) + EPS)
    return xh * r, r


def _fox_pre_fwd(proj, qg, kg, name):
    def fn(pf, qgb, kgb):
        out = []
        for idx, xh in enumerate(_heads(pf, 3 * N_FOX_HEADS)):
            if idx < 2 * N_FOX_HEADS:
                xh = _head_rms(xh)[0] * (qgb if idx < N_FOX_HEADS else kgb)
            out.append(xh)
        return (jnp.concatenate(out, axis=1),)
    return _rowwise(fn, [(proj, 3 * D_FOX, EVEN_FOX_QKV // (3 * D_FOX))], [qg, kg], [(3 * D_FOX, BF16)], [],
                    tile=256, name=name)[0]


def _fox_pre_bwd(proj, qg, kg, dq, dk, dv, name):
    def fn(pf, dqb, dkb, dvb, qgb, kgb):
        dout = _heads(dqb, N_FOX_HEADS) + _heads(dkb, N_FOX_HEADS) + _heads(dvb, N_FOX_HEADS)
        dg = [jnp.zeros((1, HEAD_DIM), F32), jnp.zeros((1, HEAD_DIM), F32)]
        dx = []
        for idx, (xh, dh) in enumerate(zip(_heads(pf, 3 * N_FOX_HEADS), dout)):
            if idx < 2 * N_FOX_HEADS:
                which = 0 if idx < N_FOX_HEADS else 1
                xhat, r = _head_rms(xh)
                dg[which] = dg[which] + jnp.sum(dh * xhat, axis=0, keepdims=True)
                dxh = dh * (qgb if which == 0 else kgb)
                dh = r * (dxh - xhat * jnp.mean(dxh * xhat, axis=-1, keepdims=True))
            dx.append(dh)
        return jnp.concatenate(dx, axis=1), dg[0], dg[1]
    return _rowwise(fn, [(proj, 3 * D_FOX, EVEN_FOX_QKV // (3 * D_FOX)), dq, dk, dv], [qg, kg],
                    [(3 * D_FOX, BF16)], [(1, HEAD_DIM)] * 2, tile=256, name=name)


def _mix_gate_fwd(proj, o_dn, o_fox, ng, name):
    def fn(gd, gf, od, of, ngb):
        dn = [_head_rms(xh)[0] * ngb for xh in _heads(od, N_DN_HEADS)]
        return (jnp.concatenate([jnp.concatenate(dn, axis=1) * gd * _sigmoid(gd), of * _sigmoid(gf)], axis=1),)
    return _rowwise(fn, [(proj, D_DN, EVEN_DN_GATE // D_DN), (proj, D_FOX, EVEN_FOX_GATE // D_FOX), o_dn, o_fox],
                    [ng], [(D_DN + D_FOX, BF16)], [], tile=256, name=name)[0]


def _mix_gate_bwd(proj, o_dn, o_fox, ng, dom, name):
    def fn(gd, gf, od, of, dm, ngb):
        d_dn, d_fox = dm[:, :D_DN], dm[:, D_DN:]
        sgd, sgf = _sigmoid(gd), _sigmoid(gf)
        don = d_dn * gd * sgd
        dng = jnp.zeros((1, HEAD_DIM), F32)
        dod, normed = [], []
        for xh, dh in zip(_heads(od, N_DN_HEADS), _heads(don, N_DN_HEADS)):
            xhat, r = _head_rms(xh)
            dng = dng + jnp.sum(dh * xhat, axis=0, keepdims=True)
            dxh = dh * ngb
            dod.append(r * (dxh - xhat * jnp.mean(dxh * xhat, axis=-1, keepdims=True)))
            normed.append(xhat * ngb)
        d_gd = d_dn * jnp.concatenate(normed, axis=1) * _silu_grad(gd, sgd)
        d_gf = d_fox * of * sgf * (1.0 - sgf)
        return jnp.concatenate(dod, axis=1), d_fox * sgf, d_gd, d_gf, dng
    return _rowwise(fn, [(proj, D_DN, EVEN_DN_GATE // D_DN), (proj, D_FOX, EVEN_FOX_GATE // D_FOX), o_dn, o_fox, dom],
                    [ng], [(D_DN, F32), (D_FOX, F32), (D_DN, BF16), (D_FOX, BF16)], [(1, HEAD_DIM)], tile=256,
                    name=name)


def _loss_grad(y, target, name):
    d = y.shape[1]

    def fn(yb, tb):
        diff = yb - tb
        part = jnp.sum(jnp.sum(diff * diff, axis=1, keepdims=True), axis=0, keepdims=True) * (0.5 / d)
        g = diff * (1.0 / d)
        return g, g, part
    return _rowwise(fn, [y, target], [], [(d, F32), (d, BF16)], [(1, 1)], tile=512, name=name)


_REF_EVEN = {"dn_qkv": (0, 1536), "dn_gate": (1536, 2048), "dn_ba": (2048, 2056), "fox_qkv": (2056, 3592),
             "fox_gate": (3592, 4104), "f_pre": (4104, 4108)}
D_IN_EVEN = 4108


def _even_to_kernel_layout(w):
    cut = lambda name: w[..., _REF_EVEN[name][0]:_REF_EVEN[name][1]]
    pad = jnp.zeros(w.shape[:-1] + (EVEN_WIDTH - EVEN_NARROW - 12,), w.dtype)
    return jnp.concatenate([cut("dn_qkv"), cut("fox_qkv"), cut("dn_gate"), cut("fox_gate"), cut("dn_ba"),
                            cut("f_pre"), pad], axis=-1)


def _even_from_kernel_layout(g):
    return jnp.concatenate([g[..., EVEN_DN_QKV:EVEN_FOX_QKV], g[..., EVEN_DN_GATE:EVEN_FOX_GATE],
                            g[..., EVEN_NARROW:EVEN_NARROW + 8], g[..., EVEN_FOX_QKV:EVEN_DN_GATE],
                            g[..., EVEN_FOX_GATE:EVEN_NARROW], g[..., EVEN_NARROW + 8:EVEN_NARROW + 12]], axis=-1)


EVEN_QUARTER = 1027
EVEN_QUARTER_PAD = 1152


def _even_grad_quarters(g):
    g = _even_from_kernel_layout(g)
    pad = [(0, 0)] * (g.ndim - 1) + [(0, EVEN_QUARTER_PAD - EVEN_QUARTER)]
    return jnp.concatenate([jnp.pad(g[..., q * EVEN_QUARTER:(q + 1) * EVEN_QUARTER], pad) for q in range(4)], axis=-1)


def _forget_rows(c):
    return c[:, LANE_FORGET:LANE_FORGET + N_FOX_HEADS].T.reshape(N_FOX_HEADS, 1, c.shape[0])


def _forget_lanes(rows):
    s = rows.shape[2]
    return jnp.pad(rows.reshape(-1, s).T, ((0, 0), (LANE_FORGET, LANES - LANE_FORGET - N_FOX_HEADS)))


def _even_fwd(x, gain, w_in, w_out, j, p, tag):
    proj, h = _norm_in_proj(x, gain, w_in, j, EVEN_WIDTH // 3, F32, f"{tag}_in")
    y = _conv_fwd(proj, p["conv_w"], f"{tag}_conv")
    dn_qkv = _dn_pre_fwd(y, f"{tag}_dn_pre")
    act = _narrow_fwd(proj, p["narrow"], f"{tag}_narrow")
    o_dn, states, tinv = _dn_fwd(dn_qkv, act, f"{tag}_delta")
    fox_qkv = _fox_pre_fwd(proj, p["q_g"], p["k_g"], f"{tag}_fox_pre")
    c = _cumsum_rows(act, False, f"{tag}_cumsum")
    ct = _forget_rows(c)
    o_fox, lse = _fox_fwd(fox_qkv, c, ct, f"{tag}_fox")
    om = _mix_gate_fwd(proj, o_dn, o_fox, p["dn_norm_g"], f"{tag}_gate")
    x2 = _mm(om, w_out, "nn", tm=512, tn=x.shape[1], out_dtype=F32, name=f"{tag}_out", residual=x, b_lead=(j,))
    return x2, (x, h, proj, y, dn_qkv, act, states, tinv, o_dn, fox_qkv, c, ct, o_fox, lse, om)


def _even_bwd(dxo, dxo16, saved, gain, w_in, w_out, j, p, tag, g_in, g_out, after=None):
    x, h, proj, y, dn_qkv, act, states, tinv, o_dn, fox_qkv, c, ct, o_fox, lse, om = saved
    d = x.shape[1]
    dom = _mm(dxo16, w_out, "nt", tm=512, tn=d, out_dtype=F32, name=f"{tag}_out_bwd", b_lead=(j,), after=after)
    g_out = _mm(om, dxo16, "tn", tm=512, tn=d, out_dtype=F32, name=f"{tag}_out_dw", into=(g_out, 0))
    d_odn, d_ofox, d_gd, d_gf, d_ng = _mix_gate_bwd(proj, o_dn, o_fox, p["dn_norm_g"], dom, f"{tag}_gate_bwd")
    dq, dk, dv, dct = _fox_bwd(fox_qkv, c, ct, o_fox, lse, d_ofox, f"{tag}_fox_bwd")
    d_fox_qkv, d_qg, d_kg = _fox_pre_bwd(proj, p["q_g"], p["k_g"], dq, dk, dv, f"{tag}_fox_pre_bwd")
    dlogf = _cumsum_rows(_forget_lanes(dct), True, f"{tag}_cumsum_bwd")
    dq, dk, dv, dact = _dn_bwd(dn_qkv, act, states, tinv, d_odn, f"{tag}_delta_bwd")
    dy = _dn_pre_bwd(y, dq, dk, dv, f"{tag}_dn_pre_bwd")
    d_dn_qkv, d_conv = _conv_bwd(proj, p["conv_w"], dy, f"{tag}_conv_bwd")
    d_narrow, s_alog, s_dt, s_fb = _narrow_bwd(proj, p["narrow"], act, dact, dlogf, f"{tag}_narrow_bwd")
    dproj = jnp.concatenate([d_dn_qkv, d_fox_qkv, d_gd, d_gf, d_narrow], axis=1)
    g_in = _mm(h, dproj, "tn", tm=512, tn=EVEN_WIDTH // 3, out_dtype=F32, name=f"{tag}_in_dw", into=(g_in, 0),
               a_lead=(0,))
    dx, dx16, d_gain = _in_proj_bwd(dproj, w_in, j, x, dxo, gain, f"{tag}_in_bwd")
    small = {"conv_w": d_conv, "a_log": s_alog, "dt_bias": s_dt, "f_bias": s_fb, "dn_norm_g": d_ng, "q_g": d_qg,
             "k_g": d_kg}
    return dx, dx16, d_gain, small, g_in, g_out


def _odd_fwd(x, gain, w_in, w_out, j, tag):
    qkv, h = _norm_in_proj(x, gain, w_in, j, w_in.shape[2] // 2, BF16, f"{tag}_in")
    o16, o32 = _sb_fwd(qkv, N_SB_HEADS, f"{tag}_sb")
    x2 = _mm(o16, w_out, "nn", tm=512, tn=x.shape[1], out_dtype=F32, name=f"{tag}_out", residual=x, b_lead=(j,))
    return x2, (x, h, qkv, o16, o32)


def _odd_bwd(dxo, dxo16, saved, gain, w_in, w_out, j, tag, g_in, g_out, after=None):
    x, h, qkv, o16, o32 = saved
    d = x.shape[1]
    do = _mm(dxo16, w_out, "nt", tm=512, tn=d, out_dtype=BF16, name=f"{tag}_out_bwd", b_lead=(j,), after=after)
    g_out = _mm(o16, dxo16, "tn", tm=512, tn=d, out_dtype=F32, name=f"{tag}_out_dw", into=(g_out, 0))
    dq, dk, dv = _sb_bwd(qkv, o32, do, N_SB_HEADS, f"{tag}_sb_bwd")
    dqkv = jnp.concatenate([dq, dk.astype(BF16), dv.astype(BF16)], axis=1)
    g_in = _mm(h, dqkv, "tn", tm=512, tn=w_in.shape[2] // 2, out_dtype=F32, name=f"{tag}_in_dw", into=(g_in, 0),
               a_lead=(0,))
    dx, dx16, d_gain = _in_proj_bwd(dqkv, w_in, j, x, dxo, gain, f"{tag}_in_bwd")
    return dx, dx16, d_gain, g_in, g_out


def _forward_backward(x, target, w, first, rest_after, token, on_reduced):
    depth = w["norm_ffn1"].shape[0]
    row = lambda a, l: a[l][None]
    rest = {}

    def mats(names, j):
        if j == 0 and names[0] in first:
            return [first[name] for name in names] + [0]
        return [rest[name] for name in names] + [j - (1 if names[0] in first else 0)]

    def even_small(j):
        return {"conv_w": w["dn_conv_w"][j], "narrow": _narrow_params(w["dn_a_log"][j], w["dn_dt_bias"][j],
                                                                     w["fox_f_bias"][j]),
                "dn_norm_g": row(w["dn_norm_g"], j), "q_g": row(w["fox_q_norm_g"], j),
                "k_g": row(w["fox_k_norm_g"], j)}

    saved = []
    for l in range(depth):
        if l == 1:
            rest.update(rest_after(x))
        gain = row(w["norm_ffn1"], l) + token[0:1, 0:1] if l == 0 else row(w["norm_ffn1"], l)
        x, s1 = _ffn_fwd(x, gain, *mats(("ffn1_w_gu", "ffn1_w_down"), l), "ffn1")
        if l % 2 == 0:
            x, s2 = _even_fwd(x, row(w["norm_mix"], l), *mats(("w_in_even", "w_out_even"), l // 2),
                              even_small(l // 2), "even")
        else:
            x, s2 = _odd_fwd(x, row(w["norm_mix"], l), *mats(("w_in_odd", "w_out_odd"), l // 2), "odd")
        x, s3 = _ffn_fwd(x, row(w["norm_ffn2"], l), *mats(("ffn2_w_gu", "ffn2_w_down"), l), "ffn2")
        saved.append((s1, s2, s3))

    dx, dx16, loss = _loss_grad(x, target, "loss")

    kind_of = dict(BIG)
    d_norm = {k: [None] * depth for k in ("norm_ffn1", "norm_mix", "norm_ffn2")}
    d_even = [None] * ((depth + 1) // 2)
    to_sibling, between_chips, token = None, None, None
    for l in reversed(range(depth)):
        s1, s2, s3 = saved[l]
        mixer = ("w_in_even", "w_out_even") if l % 2 == 0 else ("w_in_odd", "w_out_odd")
        names = ["ffn1_w_gu", "ffn1_w_down", *mixer, "ffn2_w_gu", "ffn2_w_down"]
        g = {name: lax.empty((1,) + rest[name].shape[1:], F32) for name in names}
        dx, dx16, d_norm["norm_ffn2"][l], g["ffn2_w_gu"], g["ffn2_w_down"] = _ffn_bwd(
            dx, dx16, s3, row(w["norm_ffn2"], l), *mats(("ffn2_w_gu", "ffn2_w_down"), l), "ffn2", g["ffn2_w_gu"],
            g["ffn2_w_down"], after=token)
        if to_sibling is not None:
            between_chips, token = _reduce_middle(to_sibling, dx)
        if l % 2 == 0:
            dx, dx16, d_norm["norm_mix"][l], d_even[l // 2], g["w_in_even"], g["w_out_even"] = _even_bwd(
                dx, dx16, s2, row(w["norm_mix"], l), *mats(("w_in_even", "w_out_even"), l // 2), even_small(l // 2),
                "even", g["w_in_even"], g["w_out_even"], after=token)
            g["w_in_even"] = _even_grad_quarters(g["w_in_even"])
        else:
            dx, dx16, d_norm["norm_mix"][l], g["w_in_odd"], g["w_out_odd"] = _odd_bwd(
                dx, dx16, s2, row(w["norm_mix"], l), *mats(("w_in_odd", "w_out_odd"), l // 2), "odd", g["w_in_odd"],
                g["w_out_odd"], after=token)
        dx, dx16, d_norm["norm_ffn1"][l], g["ffn1_w_gu"], g["ffn1_w_down"] = _ffn_bwd(
            dx, dx16, s1, row(w["norm_ffn1"], l), *mats(("ffn1_w_gu", "ffn1_w_down"), l), "ffn1", g["ffn1_w_gu"],
            g["ffn1_w_down"])
        to_sibling, token = _reduce_start([g[name] for name in names], [kind_of[name] for name in names], names,
                                          f"layer{l}")
        if between_chips is not None:
            on_reduced(l + 1, dict(zip(between_chips[-2], _reduce_finish(between_chips, dx))))
    between_chips, _ = _reduce_middle(to_sibling, dx)
    on_reduced(0, dict(zip(between_chips[-2], _reduce_finish(between_chips, dx))))

    small = {k: jnp.concatenate(v, axis=0) for k, v in d_norm.items()}
    dec = slice(LANE_DECAY, LANE_DECAY + N_DN_HEADS)
    fgt = slice(LANE_FORGET, LANE_FORGET + N_FOX_HEADS)
    small["dn_conv_w"] = jnp.stack([e["conv_w"] for e in d_even])
    small["dn_a_log"] = jnp.concatenate([e["a_log"][:, dec] for e in d_even], axis=0)
    small["dn_dt_bias"] = jnp.concatenate([e["dt_bias"][:, dec] for e in d_even], axis=0)
    small["fox_f_bias"] = jnp.concatenate([e["f_bias"][:, fgt] for e in d_even], axis=0)
    small["dn_norm_g"] = jnp.concatenate([e["dn_norm_g"] for e in d_even], axis=0)
    small["fox_q_norm_g"] = jnp.concatenate([e["q_g"] for e in d_even], axis=0)
    small["fox_k_norm_g"] = jnp.concatenate([e["k_g"] for e in d_even], axis=0)
    return loss, dx, small


MESH = pl.DeviceIdType.MESH
ANY = pl.BlockSpec(memory_space=pl.ANY)


def _place():
    x, y, c = lax.axis_index("x"), lax.axis_index("y"), lax.axis_index("c")
    return x, y, c, [(1 - x, y), (x, 1 - y), (1 - x, 1 - y)]


def _remote(src, dst, send_sem, recv_sem, to):
    return pltpu.make_async_remote_copy(src_ref=src, dst_ref=dst, send_sem=send_sem, recv_sem=recv_sem,
                                        device_id=to, device_id_type=MESH)


def _aligned(start, multiple):
    return start if isinstance(start, int) else pl.multiple_of(start, multiple)


def _quarter(ref, kind, chip, half, rows, cols):
    k = 2 * chip[0] + chip[1]
    hr = rows // 2
    assert hr % 16 == 0 and cols % LANES == 0
    if kind == "col":
        return ref.at[:, pl.ds(_aligned(half * hr, 16), hr), pl.ds(_aligned(k * cols, LANES), cols)]
    return ref.at[:, pl.ds(_aligned(k * rows + half * hr, 16), hr), :]


def _place_quarter(shard, kind, kc, name, first=0, count=None):
    l, rows, cols = shard.shape
    l = l - first if count is None else count
    tr = rows
    while tr * cols * 4 > (2 << 20) and tr % 32 == 0:
        tr //= 2
    nr = rows // tr
    if kind == "col":
        out_spec = pl.BlockSpec((None, tr, cols), lambda li, i, kc_ref: (li, i, kc_ref[0]))
        out_shape = (l, rows, 4 * cols)
    else:
        out_spec = pl.BlockSpec((None, tr, cols), lambda li, i, kc_ref: (li, kc_ref[0] * nr + i, 0))
        out_shape = (l, 4 * rows, cols)

    def body(kc_ref, x_ref, o_ref):
        o_ref[...] = x_ref[...].astype(BF16)

    return pl.pallas_call(
        body, name=name,
        grid_spec=pltpu.PrefetchScalarGridSpec(
            num_scalar_prefetch=1, grid=(l, nr),
            in_specs=[pl.BlockSpec((None, tr, cols), lambda li, i, kc_ref: (li + first, i, 0))],
            out_specs=out_spec),
        out_shape=jax.ShapeDtypeStruct(out_shape, BF16),
        compiler_params=_cparams("parallel", "parallel"),
    )(kc, shard)


def _gather_weights(wholes, kinds):
    n = len(wholes)

    def dims(ref, kind):
        _, r, cc = ref.shape
        return (r, cc // 4) if kind == "col" else (r // 4, cc)

    def body(*refs):
        bufs = refs[n:2 * n]
        send_sems, recv_sems = refs[2 * n:]
        x, y, c, chips = _place()
        sibling = (x, y, 1 - c)
        first, passed = [], []
        for t in range(n):
            rows, cols = dims(bufs[t], kinds[t])
            mine = _quarter(bufs[t], kinds[t], (x, y), c, rows, cols)
            for j, chip in enumerate(chips):
                cp = _remote(mine, mine, send_sems.at[t, j], recv_sems.at[t, j], (*chip, c))
                cp.start()
                first.append(cp)
        for j, chip in enumerate(chips):
            for t in range(n):
                rows, cols = dims(bufs[t], kinds[t])
                got = _quarter(bufs[t], kinds[t], chip, c, rows, cols)
                _remote(got, got, send_sems.at[t, j], recv_sems.at[t, j], (*chip, c)).wait_recv()
                cp = _remote(got, got, send_sems.at[t, 3 + j], recv_sems.at[t, 3 + j], sibling)
                cp.start()
                passed.append(cp)
        for j, chip in enumerate(chips):
            for t in range(n):
                rows, cols = dims(bufs[t], kinds[t])
                got = _quarter(bufs[t], kinds[t], chip, 1 - c, rows, cols)
                _remote(got, got, send_sems.at[t, 3 + j], recv_sems.at[t, 3 + j], sibling).wait_recv()
        for cp in first + passed:
            cp.wait_send()

    return pl.pallas_call(
        body, name="gather_weights", in_specs=[ANY] * n, out_specs=[ANY] * n,
        out_shape=[jax.ShapeDtypeStruct(a.shape, a.dtype) for a in wholes],
        input_output_aliases={t: t for t in range(n)},
        scratch_shapes=[pltpu.SemaphoreType.DMA((n, 6)), pltpu.SemaphoreType.DMA((n, 6))],
        compiler_params=pltpu.CompilerParams(has_side_effects=True),
    )(*wholes)


def _quarter_dims(ref, kind):
    _, r, cc = ref.shape
    return (r, cc // 4) if kind == "col" else (r // 4, cc)


def _gather_chips_copies(bufs, sems, kinds):
    x, y, c, chips = _place()
    copies = []
    for t, buf in enumerate(bufs):
        rows, cols = _quarter_dims(buf, kinds[t])
        mine = _quarter(buf, kinds[t], (x, y), c, rows, cols)
        for j, chip in enumerate(chips):
            pair = 2 * (OTHER_CHIPS * t + j)
            copies.append(_remote(mine, mine, sems[pair], sems[pair + 1], (*chip, c)))
    return copies


def _gather_start(wholes, kinds, after, tag):
    n = len(wholes)
    n_sems = 2 * OTHER_CHIPS * n
    n_in = n + len(after)

    def body(*refs):
        for cp in _gather_chips_copies(refs[:n], refs[n_in + n:n_in + n + n_sems], kinds):
            cp.start()
        refs[-1][...] = jnp.zeros_like(refs[-1])

    held = [pltpu.with_memory_space_constraint(a, pltpu.HBM) for a in wholes]
    out = pl.pallas_call(
        body, name=f"gather_start_{tag}", in_specs=[HBM] * n + [ANY] * len(after),
        out_specs=(*[HBM] * n, *[SEM] * n_sems, pl.BlockSpec(memory_space=pltpu.VMEM)),
        out_shape=(*[pltpu.HBM(a.shape, a.dtype) for a in held], *[pltpu.SemaphoreType.DMA(())] * n_sems,
                   jax.ShapeDtypeStruct((8, LANES), F32)),
        input_output_aliases={i: i for i in range(n)},
        compiler_params=pltpu.CompilerParams(has_side_effects=SPLIT_COPY),
    )(*held, *after)
    return out[n:n + n_sems], out[:n], out[-1]


def _gather_wait(sems, wholes, kinds, after, tag):
    n = len(wholes)

    def body(*refs):
        for cp in _gather_chips_copies(refs[:n], refs[n:n + len(sems)], kinds):
            cp.wait_send()
            cp.wait_recv()

    return pl.pallas_call(
        body, name=f"gather_wait_{tag}", in_specs=[HBM] * n + [SEM] * len(sems) + [ANY],
        out_specs=tuple([HBM] * n), out_shape=tuple(pltpu.HBM(a.shape, a.dtype) for a in wholes),
        input_output_aliases={i: i for i in range(n)},
        compiler_params=pltpu.CompilerParams(has_side_effects=SPLIT_COPY),
    )(*wholes, *sems, after)


def _gather_forward(wholes, kinds, tag):
    n = len(wholes)

    def body(*refs):
        bufs = refs[n:2 * n]
        send_sems, recv_sems = refs[2 * n:]
        x, y, c, chips = _place()
        copies = []
        for t in range(n):
            rows, cols = _quarter_dims(bufs[t], kinds[t])
            for j, chip in enumerate(chips):
                got = _quarter(bufs[t], kinds[t], chip, c, rows, cols)
                cp = _remote(got, got, send_sems.at[t, j], recv_sems.at[t, j], (x, y, 1 - c))
                cp.start()
                copies.append(cp)
        for cp in copies:
            cp.wait_send()
        for t in range(n):
            rows, cols = _quarter_dims(bufs[t], kinds[t])
            for j, chip in enumerate(chips):
                got = _quarter(bufs[t], kinds[t], chip, 1 - c, rows, cols)
                _remote(got, got, send_sems.at[t, j], recv_sems.at[t, j], (x, y, 1 - c)).wait_recv()

    return pl.pallas_call(
        body, name=f"gather_forward_{tag}", in_specs=[ANY] * n, out_specs=[ANY] * n,
        out_shape=[jax.ShapeDtypeStruct(a.shape, a.dtype) for a in wholes],
        input_output_aliases={t: t for t in range(n)},
        scratch_shapes=[pltpu.SemaphoreType.DMA((n, OTHER_CHIPS)), pltpu.SemaphoreType.DMA((n, OTHER_CHIPS))],
        compiler_params=pltpu.CompilerParams(has_side_effects=True),
    )(*wholes)


def _canonical(a, kind):
    l, r, c = a.shape
    return a.reshape(l, 1, r, c) if kind == "col" else a.reshape(l, 4, r // 4, c)


def _add_tile(rows, cols):
    tc = cols if cols <= 1536 else cols // 4
    tr = rows
    while tr * tc * 4 > (1 << 20) and tr % 16 == 0:
        tr //= 2
    return tr, tc


def _rs_add_sibling(part, got, c, name):
    l, a, hr, cols = got.shape
    tr, tc = _add_tile(hr, cols)
    nr = hr // tr

    def body(c_ref, p_ref, g_ref, o32_ref, o16_ref):
        s = p_ref[...] + g_ref[...]
        o32_ref[...] = s
        o16_ref[...] = s.astype(BF16)

    blk = (None, None, tr, tc)
    spec = pl.BlockSpec(blk, lambda li, ai, i, j, c_ref: (li, ai, i, j))
    return pl.pallas_call(
        body, name=name,
        grid_spec=pltpu.PrefetchScalarGridSpec(
            num_scalar_prefetch=1, grid=(l, a, nr, cols // tc),
            in_specs=[pl.BlockSpec(blk, lambda li, ai, i, j, c_ref: (li, ai, c_ref[0] * nr + i, j)), spec],
            out_specs=[spec, spec]),
        out_shape=[jax.ShapeDtypeStruct(got.shape, F32), jax.ShapeDtypeStruct(got.shape, BF16)],
        compiler_params=_cparams("parallel", "parallel", "parallel", "parallel"),
    )(c, part, got)


def _quarter4(ref, kind, chip, cols):
    k = 2 * chip[0] + chip[1]
    if kind == "col":
        return ref.at[:, :, :, pl.ds(pl.multiple_of(k * cols, LANES), cols)]
    return ref.at[:, pl.ds(k, 1), :, :]


HBM = pl.BlockSpec(memory_space=pltpu.HBM)
SEM = pl.BlockSpec(memory_space=pltpu.SEMAPHORE)
SPLIT_COPY = pltpu.SideEffectType.DATAFLOW_SIDE_EFFECTING
OTHER_CHIPS = 3


def _quarter4_shape(a, kind):
    l, _, hr, cols = a.shape
    return (l, 1, hr, cols // 4 if kind == "col" else cols)


def _rs_chips_copies(srcs, lands, sems, kinds):
    x, y, c, chips = _place()
    copies = []
    for t, (src, land) in enumerate(zip(srcs, lands)):
        cols = _quarter4_shape(src, kinds[t])[3]
        for j, chip in enumerate(chips):
            pair = 2 * (OTHER_CHIPS * t + j)
            copies.append(_remote(_quarter4(src, kinds[t], chip, cols), land.at[j], sems[pair], sems[pair + 1],
                                  (*chip, c)))
    return copies


def _split_start(copies, srcs, lands, n_sems, name):
    n = len(srcs)

    def body(*refs):
        for cp in copies(refs[:n], refs[n:2 * n], refs[4 * n:4 * n + n_sems]):
            cp.start()
        refs[-1][...] = jnp.zeros_like(refs[-1])

    held = [pltpu.with_memory_space_constraint(a, pltpu.HBM) for a in (*srcs, *lands)]
    out = pl.pallas_call(
        body, name=name, in_specs=[HBM] * (2 * n),
        out_specs=(*[HBM] * (2 * n), *[SEM] * n_sems, pl.BlockSpec(memory_space=pltpu.VMEM)),
        out_shape=(*[pltpu.HBM(a.shape, a.dtype) for a in held], *[pltpu.SemaphoreType.DMA(())] * n_sems,
                   jax.ShapeDtypeStruct((8, LANES), F32)),
        input_output_aliases={i: i for i in range(2 * n)},
        compiler_params=pltpu.CompilerParams(has_side_effects=SPLIT_COPY),
    )(*held)
    return out[2 * n:2 * n + n_sems], out[:n], out[n:2 * n], out[-1]


def _split_wait(copies, sems, srcs, lands, after, name):
    n = len(srcs)

    def body(*refs):
        for cp in copies(refs[:n], refs[n:2 * n], refs[2 * n:2 * n + len(sems)]):
            cp.wait_send()
            cp.wait_recv()

    out = pl.pallas_call(
        body, name=name, in_specs=[HBM] * (2 * n) + [SEM] * len(sems) + [ANY],
        out_specs=tuple([HBM] * (2 * n)),
        out_shape=tuple(pltpu.HBM(a.shape, a.dtype) for a in (*srcs, *lands)),
        input_output_aliases={i: i for i in range(2 * n)},
        compiler_params=pltpu.CompilerParams(has_side_effects=SPLIT_COPY),
    )(*srcs, *lands, *sems, after)
    return out[:n], out[n:]


def _rs_sibling_copies(srcs, lands, sems):
    x, y, c, _ = _place()
    copies = []
    for t, (src, land) in enumerate(zip(srcs, lands)):
        hr = src.shape[2] // 2
        gives = src.at[:, :, pl.ds(pl.multiple_of((1 - c) * hr, 8), hr), :]
        copies.append(_remote(gives, land, sems[2 * t], sems[2 * t + 1], (x, y, 1 - c)))
    return copies


def _rs_add_chips(sum32, got, kind, kc, name):
    _, l, _, hr, cols = got.shape
    tr, _ = _add_tile(hr, cols)
    nr = hr // tr
    k_arr, c_arr = kc
    if kind == "col":
        own = pl.BlockSpec((None, None, tr, cols), lambda li, i, k_ref, c_ref: (li, 0, i, k_ref[0]))
    else:
        own = pl.BlockSpec((None, None, tr, cols), lambda li, i, k_ref, c_ref: (li, k_ref[0], i, 0))

    def body(k_ref, c_ref, own_ref, got_ref, o_ref):
        o_ref[...] = ((own_ref[...] + got_ref[0].astype(F32)) + got_ref[1].astype(F32)) + got_ref[2].astype(F32)

    return pl.pallas_call(
        body, name=name,
        grid_spec=pltpu.PrefetchScalarGridSpec(
            num_scalar_prefetch=2, grid=(l, nr),
            in_specs=[own, pl.BlockSpec((3, None, None, tr, cols), lambda li, i, k_ref, c_ref: (0, li, 0, i, 0))],
            out_specs=pl.BlockSpec((None, tr, cols), lambda li, i, k_ref, c_ref: (li, c_ref[0] * nr + i, 0))),
        out_shape=jax.ShapeDtypeStruct((l, 2 * hr, cols), F32),
        compiler_params=_cparams("parallel", "parallel"),
    )(k_arr, c_arr, sum32, got)


def _rs_finish(quarters):
    n = len(quarters)

    def body(*refs):
        bufs = refs[n:2 * n]
        send_sems, recv_sems = refs[2 * n:]
        x, y, c, _ = _place()
        copies = []
        for t in range(n):
            hr = bufs[t].shape[1] // 2
            mine = bufs[t].at[:, pl.ds(pl.multiple_of(c * hr, 8), hr), :]
            cp = _remote(mine, mine, send_sems.at[t], recv_sems.at[t], (x, y, 1 - c))
            cp.start()
            copies.append(cp)
        for cp in copies:
            cp.wait()

    return pl.pallas_call(
        body, name="reduce_finish", in_specs=[ANY] * n, out_specs=[ANY] * n,
        out_shape=[jax.ShapeDtypeStruct(a.shape, a.dtype) for a in quarters],
        input_output_aliases={t: t for t in range(n)},
        scratch_shapes=[pltpu.SemaphoreType.DMA((n,)), pltpu.SemaphoreType.DMA((n,))],
        compiler_params=pltpu.CompilerParams(has_side_effects=True),
    )(*quarters)


def _reduce_start(parts, kinds, names, tag):
    canon = [_canonical(p, kind) for p, kind in zip(parts, kinds)]
    lands = [lax.empty(a.shape[:2] + (a.shape[2] // 2, a.shape[3]), a.dtype) for a in canon]
    sems, srcs, lands, token = _split_start(_rs_sibling_copies, canon, lands, 2 * len(canon),
                                            f"reduce_sibling_start_{tag}")
    return (sems, srcs, lands, kinds, names, tag), token


def _reduce_middle(state, after):
    sems, srcs, lands, kinds, names, tag = state
    c_arr = jnp.reshape(lax.axis_index("c"), (1,)).astype(jnp.int32)
    srcs, from_sibling = _split_wait(_rs_sibling_copies, sems, srcs, lands, after, f"reduce_sibling_wait_{tag}")
    sums = [_rs_add_sibling(p, g, c_arr, f"reduce_add_sibling_{nm}") for p, g, nm in zip(srcs, from_sibling, names)]
    sums16 = [s16 for _, s16 in sums]
    copies = functools.partial(_rs_chips_copies, kinds=kinds)
    lands = [lax.empty((OTHER_CHIPS,) + _quarter4_shape(a, k), a.dtype) for a, k in zip(sums16, kinds)]
    sems, srcs, lands, token = _split_start(copies, sums16, lands, 2 * OTHER_CHIPS * len(sums16),
                                            f"reduce_chips_start_{tag}")
    return (sems, srcs, lands, [s32 for s32, _ in sums], kinds, names, tag), token


def _reduce_finish(state, after):
    sems, srcs, lands, sums32, kinds, names, tag = state
    x, y, c = lax.axis_index("x"), lax.axis_index("y"), lax.axis_index("c")
    kc = (jnp.reshape(2 * x + y, (1,)).astype(jnp.int32), jnp.reshape(c, (1,)).astype(jnp.int32))
    copies = functools.partial(_rs_chips_copies, kinds=kinds)
    _, from_chips = _split_wait(copies, sems, srcs, lands, after, f"reduce_chips_wait_{tag}")
    halves = [_rs_add_chips(s32, g, kind, kc, f"reduce_add_chips_{nm}")
              for s32, g, kind, nm in zip(sums32, from_chips, kinds, names)]
    return _rs_finish(halves)


SMALL_PEERS = 7


def _small_exchange(pack):
    rows = pack.shape[0]

    def body(p_ref, slots_ref, total_ref, send_sems, recv_sems):
        x, y, c, _ = _place()
        me = 4 * x + 2 * y + c
        slots_ref[me] = p_ref[...]
        copies = []
        for p in range(1, SMALL_PEERS + 1):
            px, py, pc = (p >> 2) & 1, (p >> 1) & 1, p & 1
            peer = (1 - x if px else x, 1 - y if py else y, 1 - c if pc else c)
            cp = _remote(p_ref, slots_ref.at[me], send_sems.at[p - 1], recv_sems.at[p - 1], peer)
            cp.start()
            copies.append(cp)
        for cp in copies:
            cp.wait()
        total = slots_ref[0]
        for i in range(1, SMALL_PEERS + 1):
            total = total + slots_ref[i]
        total_ref[...] = total

    vmem = pl.BlockSpec(memory_space=pltpu.VMEM)
    return pl.pallas_call(
        body, name="small_exchange", in_specs=[vmem], out_specs=[vmem, vmem],
        out_shape=[jax.ShapeDtypeStruct((SMALL_PEERS + 1, rows, LANES), F32), jax.ShapeDtypeStruct((rows, LANES), F32)],
        scratch_shapes=[pltpu.SemaphoreType.DMA((SMALL_PEERS,)), pltpu.SemaphoreType.DMA((SMALL_PEERS,))],
        compiler_params=pltpu.CompilerParams(has_side_effects=True),
    )(pack)


def _pack(arrays):
    rows = []
    for a in arrays:
        flat = a.reshape(-1).astype(F32)
        rows.append(jnp.pad(flat, (0, (-flat.shape[0]) % LANES)).reshape(-1, LANES))
    out = jnp.concatenate(rows, axis=0)
    return jnp.pad(out, ((0, (-out.shape[0]) % 8), (0, 0)))


def _unpack(pack, shapes):
    out, r = [], 0
    for sh in shapes:
        size = math.prod(sh)
        nr = -(-size // LANES)
        out.append(pack[r:r + nr].reshape(-1)[:size].reshape(sh))
        r += nr
    return out


def _adamw(w, g, m, v, name):
    shape = w.shape
    to2d = lambda a: a.reshape(-1, shape[-1])
    rows = math.prod(shape[:-1])
    tile = 256 if rows % 256 == 0 else rows

    def fn(wb, gb, mb, vb):
        m2 = ADAM_B1 * mb + (1.0 - ADAM_B1) * gb
        v2 = ADAM_B2 * vb + (1.0 - ADAM_B2) * (gb * gb)
        m_hat = m2 / (1.0 - ADAM_B1 ** ADAM_STEP)
        v_hat = v2 / (1.0 - ADAM_B2 ** ADAM_STEP)
        return -ADAM_LR * (m_hat / (jnp.sqrt(v_hat) + ADAM_EPS) + ADAM_WD * wb), m2, v2

    res = _rowwise(fn, [to2d(w), to2d(g), to2d(m), to2d(v)], [], [(shape[-1], F32)] * 3, [], tile=tile, name=name)
    return [r.reshape(shape) for r in res]


def _adamw_layer(w, g, m, v, layer, outs, name):
    _, rows, cols = w.shape
    tile = rows
    while tile * cols * 4 > (1 << 20) and tile % 16 == 0:
        tile //= 2

    def body(w_ref, g_ref, m_ref, v_ref, *rest):
        g_out, d_out, m_out, v_out = rest[-4:]
        gb = g_ref[...]
        m2 = ADAM_B1 * m_ref[...] + (1.0 - ADAM_B1) * gb
        v2 = ADAM_B2 * v_ref[...] + (1.0 - ADAM_B2) * (gb * gb)
        m_hat = m2 / (1.0 - ADAM_B1 ** ADAM_STEP)
        v_hat = v2 / (1.0 - ADAM_B2 ** ADAM_STEP)
        g_out[...] = gb
        d_out[...] = -ADAM_LR * (m_hat / (jnp.sqrt(v_hat) + ADAM_EPS) + ADAM_WD * w_ref[...])
        m_out[...] = m2
        v_out[...] = v2

    stacked = pl.BlockSpec((None, tile, cols), lambda i: (layer, i, 0))
    return pl.pallas_call(
        body, name=name, grid=(rows // tile,),
        in_specs=[stacked, pl.BlockSpec((None, tile, cols), lambda i: (0, i, 0)), stacked, stacked] + [ANY] * 4,
        out_specs=[stacked] * 4, out_shape=[jax.ShapeDtypeStruct(w.shape, F32)] * 4,
        input_output_aliases={4 + i: i for i in range(4)}, compiler_params=_cparams("parallel"),
    )(w, g, m, v, *outs)


BIG = (("ffn1_w_gu", "col"), ("ffn1_w_down", "row"), ("w_in_even", "col"), ("w_out_even", "row"),
       ("w_in_odd", "col"), ("w_out_odd", "row"), ("ffn2_w_gu", "col"), ("ffn2_w_down", "row"))
SMALL = ("norm_ffn1", "norm_mix", "dn_conv_w", "dn_a_log", "dn_dt_bias", "dn_norm_g", "fox_q_norm_g", "fox_k_norm_g",
         "fox_f_bias", "norm_ffn2")
WEIGHTS = ("norm_ffn1", "ffn1_w_gu", "ffn1_w_down", "norm_mix", "w_in_even", "dn_conv_w", "dn_a_log", "dn_dt_bias",
           "dn_norm_g", "fox_q_norm_g", "fox_k_norm_g", "fox_f_bias", "w_out_even", "w_in_odd", "w_out_odd",
           "norm_ffn2", "ffn2_w_gu", "ffn2_w_down")


def _step(x, target, w, m, v):
    k = 2 * lax.axis_index("x") + lax.axis_index("y")
    n_conv = w["dn_conv_w"].shape[2]

    kc = jnp.reshape(k, (1,)).astype(jnp.int32)
    kinds = dict(BIG)
    quarters = {name: w[name] for name in kinds}
    quarters["w_in_even"] = jnp.pad(w["w_in_even"], ((0, 0), (0, 0), (0, EVEN_QUARTER_PAD - EVEN_QUARTER)))
    first_names = [name for name in kinds if name not in ("w_in_odd", "w_out_odd")]
    rest_names = list(kinds)

    def even_columns(whole):
        padded = whole["w_in_even"]
        ref_order = jnp.concatenate([padded[..., q * EVEN_QUARTER_PAD:q * EVEN_QUARTER_PAD + EVEN_QUARTER]
                                     for q in range(4)], axis=-1)
        return {**whole, "w_in_even": _even_to_kernel_layout(ref_order)}

    conv_slots, _ = _small_exchange(_pack([w["dn_conv_w"]]))
    placed = [_place_quarter(quarters[name], kinds[name], kc, f"place_first_{name}", 0, 1) for name in first_names]
    gathered = _gather_weights(placed, [kinds[name] for name in first_names])
    first = even_columns(dict(zip(first_names, gathered)))
    placed = [_place_quarter(quarters[name], kinds[name], kc, f"place_rest_{name}", 1 if name in first_names else 0)
              for name in rest_names]
    rest_kinds = [kinds[name] for name in rest_names]
    sems, on_their_way, token = _gather_start(placed, rest_kinds, [conv_slots, *gathered], "rest")

    def rest_after(value):
        landed = _gather_wait(sems, on_their_way, rest_kinds, value, "rest")
        return even_columns(dict(zip(rest_names, _gather_forward(landed, rest_kinds, "rest"))))

    whole = {}
    conv_rows = math.prod(w["dn_conv_w"].shape) // LANES
    conv_quarters = [conv_slots[2 * q, :conv_rows].reshape(w["dn_conv_w"].shape) for q in range(4)]
    whole["dn_conv_w"] = jnp.concatenate(conv_quarters, axis=-1)
    for name in SMALL:
        if name != "dn_conv_w":
            whole[name] = w[name]

    updated = {name: [lax.empty(w[name].shape, F32) for _ in range(4)] for name in kinds}

    def on_reduced(layer, layer_grads):
        for name, g in layer_grads.items():
            if name == "w_in_even":
                g = g[..., :EVEN_QUARTER]
            stacked_layer = layer if w[name].shape[0] == w["norm_mix"].shape[0] else layer // 2
            updated[name] = _adamw_layer(w[name], g, m[name], v[name], stacked_layer, updated[name], f"adamw_{name}")

    loss, dx, small = _forward_backward(x, target, whole, first, rest_after, token, on_reduced)

    _, small_sum = _small_exchange(_pack([small[n] for n in SMALL]))
    grads = dict(zip(SMALL, _unpack(small_sum, [small[n].shape for n in SMALL])))
    grads["dn_conv_w"] = lax.dynamic_slice_in_dim(grads["dn_conv_w"], k * n_conv, n_conv, axis=2)
    delta, new_m, new_v = {}, {}, {}
    for name in kinds:
        grads[name], delta[name], new_m[name], new_v[name] = updated[name]
    packs = [_pack([d[n] for n in SMALL]) for d in (w, grads, m, v)]
    shapes = [w[n].shape for n in SMALL]
    for out, res in zip((delta, new_m, new_v), _adamw(*packs, "adamw_small")):
        out.update(zip(SMALL, _unpack(res, shapes)))
    total_loss = lax.psum(loss[0, 0], ("x", "y", "c"))
    return total_loss, dx, grads, delta, new_m, new_v


def kernel(x, norm_ffn1, ffn1_w_gu, ffn1_w_down, norm_mix, w_in_even, dn_conv_w, dn_a_log, dn_dt_bias, dn_norm_g, fox_q_norm_g, fox_k_norm_g, fox_f_bias, w_out_even, w_in_odd, w_out_odd, norm_ffn2, ffn2_w_gu, ffn2_w_down, loss_target, m_norm_ffn1, m_ffn1_w_gu, m_ffn1_w_down, m_norm_mix, m_w_in_even, m_dn_conv_w, m_dn_a_log, m_dn_dt_bias, m_dn_norm_g, m_fox_q_norm_g, m_fox_k_norm_g, m_fox_f_bias, m_w_out_even, m_w_in_odd, m_w_out_odd, m_norm_ffn2, m_ffn2_w_gu, m_ffn2_w_down, v_norm_ffn1, v_ffn1_w_gu, v_ffn1_w_down, v_norm_mix, v_w_in_even, v_dn_conv_w, v_dn_a_log, v_dn_dt_bias, v_dn_norm_g, v_fox_q_norm_g, v_fox_k_norm_g, v_fox_f_bias, v_w_out_even, v_w_in_odd, v_w_out_odd, v_norm_ffn2, v_ffn2_w_gu, v_ffn2_w_down):
    w = dict(zip(WEIGHTS, (norm_ffn1, ffn1_w_gu, ffn1_w_down, norm_mix, w_in_even, dn_conv_w, dn_a_log, dn_dt_bias,
                           dn_norm_g, fox_q_norm_g, fox_k_norm_g, fox_f_bias, w_out_even, w_in_odd, w_out_odd,
                           norm_ffn2, ffn2_w_gu, ffn2_w_down)))
    m = dict(zip(WEIGHTS, (m_norm_ffn1, m_ffn1_w_gu, m_ffn1_w_down, m_norm_mix, m_w_in_even, m_dn_conv_w, m_dn_a_log,
                           m_dn_dt_bias, m_dn_norm_g, m_fox_q_norm_g, m_fox_k_norm_g, m_fox_f_bias, m_w_out_even,
                           m_w_in_odd, m_w_out_odd, m_norm_ffn2, m_ffn2_w_gu, m_ffn2_w_down)))
    v = dict(zip(WEIGHTS, (v_norm_ffn1, v_ffn1_w_gu, v_ffn1_w_down, v_norm_mix, v_w_in_even, v_dn_conv_w, v_dn_a_log,
                           v_dn_dt_bias, v_dn_norm_g, v_fox_q_norm_g, v_fox_k_norm_g, v_fox_f_bias, v_w_out_even,
                           v_w_in_odd, v_w_out_odd, v_norm_ffn2, v_ffn2_w_gu, v_ffn2_w_down)))
    loss, dx, grads, delta, new_m, new_v = _step(x[0], loss_target[0], w, m, v)
    return (loss, dx[None], *[grads[n] for n in WEIGHTS], *[delta[n] for n in WEIGHTS],
            *[new_m[n] for n in WEIGHTS], *[new_v[n] for n in WEIGHTS])
```

```python
import functools
import math

import jax
import jax.numpy as jnp
from jax import lax
from jax.experimental import pallas as pl
from jax.experimental.pallas import tpu as pltpu

F32 = jnp.float32
BF16 = jnp.bfloat16
HI = lax.Precision.HIGH

HEAD_DIM = 128
N_DN_HEADS = 4
N_FOX_HEADS = 4
N_SB_HEADS = 8
D_DN = N_DN_HEADS * HEAD_DIM
D_FOX = N_FOX_HEADS * HEAD_DIM
CONV_WIDTH = 4
DN_CHUNK = 64
EPS = 1e-6
ATT_SCALE = HEAD_DIM ** -0.5
ADAM_LR, ADAM_B1, ADAM_B2, ADAM_EPS, ADAM_WD, ADAM_STEP = 0.001, 0.9, 0.999, 1e-08, 0.01, 10

V7X_VMEM_LIMIT = 56 * 1024 * 1024
LANES = 128
ATT_TQ = 512
ATT_TK = 256
ATT_SUB = ATT_TQ // ATT_TK

LANE_BETA, LANE_DECAY, LANE_FORGET = 0, 4, 8


def _cparams(*sem):
    return pltpu.CompilerParams(dimension_semantics=sem, vmem_limit_bytes=V7X_VMEM_LIMIT)


def _sigmoid(x):
    return 1.0 / (1.0 + jnp.exp(-x))


def _softplus(x):
    return jnp.maximum(x, 0.0) + jnp.log(1.0 + jnp.exp(-jnp.abs(x)))


def _silu_grad(y, sg):
    return sg * (1.0 + y * (1.0 - sg))


def _rowwise(fn, rows, bcast, outs, sums, *, tile, name):
    rows = [r if isinstance(r, tuple) else (r, r.shape[1], 0) for r in rows]
    s = rows[0][0].shape[0]
    assert s % tile == 0
    n_in, n_b, n_out, n_sum = len(rows), len(bcast), len(outs), len(sums)

    def body(*refs):
        ins = [r[...] for r in refs[:n_in + n_b]]
        res = fn(*ins)
        if not isinstance(res, (tuple, list)):
            res = (res,)
        out_refs = refs[n_in + n_b:n_in + n_b + n_out]
        sum_refs = refs[n_in + n_b + n_out:]
        for o_ref, val in zip(out_refs, res[:n_out]):
            o_ref[...] = val.astype(o_ref.dtype)
        if n_sum:
            @pl.when(pl.program_id(0) == 0)
            def _():
                for s_ref in sum_refs:
                    s_ref[...] = jnp.zeros_like(s_ref)
            for s_ref, val in zip(sum_refs, res[n_out:]):
                s_ref[...] += val

    in_specs = [pl.BlockSpec((tile, w), lambda i, cb=cb: (i, cb)) for _, w, cb in rows]
    in_specs += [pl.BlockSpec(b.shape, lambda i, nd=b.ndim: (0,) * nd) for b in bcast]
    out_specs = [pl.BlockSpec((tile, c), lambda i: (i, 0)) for c, _ in outs]
    out_specs += [pl.BlockSpec(sh, lambda i: (0, 0)) for sh in sums]
    out_shape = [jax.ShapeDtypeStruct((s, c), dt) for c, dt in outs]
    out_shape += [jax.ShapeDtypeStruct(sh, F32) for sh in sums]
    return pl.pallas_call(
        body, name=name, grid=(s // tile,), in_specs=in_specs, out_specs=out_specs, out_shape=out_shape,
        compiler_params=_cparams("arbitrary" if n_sum else "parallel"),
    )(*[r[0] for r in rows], *bcast)


def _norm_in_proj(x, gain, w_in, j, tn, out_dtype, name):
    s, d = x.shape
    n = w_in.shape[2]
    tm = 512
    assert s % tm == 0 and n % tn == 0

    def body(x_ref, gain_ref, w_ref, o_ref, h_ref):
        xb = x_ref[...]
        r = lax.rsqrt(jnp.mean(xb * xb, axis=-1, keepdims=True) + EPS)
        h = (xb * r * gain_ref[...]).astype(BF16)
        h_ref[...] = h
        o_ref[...] = lax.dot_general(h, w_ref[...], _DIMS["nn"], preferred_element_type=F32).astype(out_dtype)

    return pl.pallas_call(
        body, name=name, grid=(n // tn, s // tm),
        in_specs=[pl.BlockSpec((tm, d), lambda t, i: (i, 0)), pl.BlockSpec((1, d), lambda t, i: (0, 0)),
                  pl.BlockSpec((None, d, tn), lambda t, i: (j, 0, t))],
        out_specs=[pl.BlockSpec((tm, tn), lambda t, i: (i, t)), pl.BlockSpec((None, tm, d), lambda t, i: (t, i, 0))],
        out_shape=[jax.ShapeDtypeStruct((s, n), out_dtype), jax.ShapeDtypeStruct((n // tn, s, d), BF16)],
        compiler_params=_cparams("parallel", "parallel"),
    )(x, gain, w_in)


_DIMS = {"nn": (((1,), (0,)), ((), ())), "nt": (((1,), (1,)), ((), ())), "tn": (((0,), (0,)), ((), ()))}


def _dot(a, b, kind):
    return lax.dot_general(a.astype(BF16), b.astype(BF16), _DIMS[kind], preferred_element_type=F32)


def _dot32(a, b, kind="nn"):
    return lax.dot_general(a, b, _DIMS[kind], precision=HI, preferred_element_type=F32)


def _mm(a, b, kind, *, tm, tn, out_dtype, name, scale=None, residual=None, a_lead=(), b_lead=(),
        b_spec=None, n=None, into=None, after=None):
    ash, bsh = a.shape[len(a_lead):], b.shape[len(b_lead):]
    m = ash[1] if kind == "tn" else ash[0]
    k = ash[0] if kind == "tn" else ash[1]
    if b_spec is None:
        n = bsh[0] if kind == "nt" else bsh[1]
        assert k == (bsh[1] if kind == "nt" else bsh[0]), (ash, bsh, kind)
    assert m % tm == 0 and n % tn == 0, (m, tm, n, tn)
    la, lb = (None,) * len(a_lead), (None,) * len(b_lead)
    if kind == "tn":
        a_spec = pl.BlockSpec(la + (k, tm), lambda j, i: a_lead + (0, i))
    else:
        a_spec = pl.BlockSpec(la + (tm, k), lambda j, i: a_lead + (i, 0))
    if b_spec is None:
        if kind == "nt":
            b_spec = pl.BlockSpec(lb + (tn, k), lambda j, i: b_lead + (j, 0))
        else:
            b_spec = pl.BlockSpec(lb + (k, tn), lambda j, i: b_lead + (0, j))
    in_specs, args = [a_spec, b_spec], [a, b]
    if residual is not None:
        in_specs.append(pl.BlockSpec((tm, tn), lambda j, i: (i, j)))
        args.append(residual)
    aliases = {}
    if after is not None:
        in_specs.append(pl.BlockSpec(memory_space=pl.ANY))
        args.append(after)
    if into is not None:
        buf, layer = into
        in_specs.append(pl.BlockSpec(memory_space=pl.ANY))
        args.append(buf)
        aliases = {len(args) - 1: 0}
        out_spec = pl.BlockSpec((None, tm, tn), lambda j, i: (layer, i, j))
        out_shape = jax.ShapeDtypeStruct(buf.shape, buf.dtype)
    else:
        out_spec = pl.BlockSpec((tm, tn), lambda j, i: (i, j))
        out_shape = jax.ShapeDtypeStruct((m, n), out_dtype)

    def body(a_ref, b_ref, *rest):
        acc = _dot(a_ref[...], b_ref[...], kind)
        if scale is not None:
            acc = acc * scale
        if residual is not None:
            acc = acc + rest[0][...]
        rest[-1][...] = acc.astype(rest[-1].dtype)

    return pl.pallas_call(
        body, name=name, grid=(n // tn, m // tm), in_specs=in_specs, out_specs=out_spec, out_shape=out_shape,
        input_output_aliases=aliases, compiler_params=_cparams("parallel", "parallel"),
    )(*args)


FFN_TM = 1024


def _ffn_up(x, gain, w_gu, layer, name):
    s, d = x.shape
    f = w_gu.shape[2] // 2
    tm, tn = FFN_TM, f // 2
    nj = f // tn

    def body(x_ref, gain_ref, wg_ref, wu_ref, gu_ref, a_ref, n_ref):
        xb = x_ref[...]
        r = lax.rsqrt(jnp.mean(xb * xb, axis=-1, keepdims=True) + EPS)
        nv = (xb * r * gain_ref[...]).astype(BF16)
        n_ref[...] = nv
        g = _dot(nv, wg_ref[...], "nn")
        u = _dot(nv, wu_ref[...], "nn")
        gu_ref[0] = g.astype(BF16)
        gu_ref[1] = u.astype(BF16)
        a_ref[...] = (g * _sigmoid(g) * u).astype(BF16)

    return pl.pallas_call(
        body, name=name, grid=(nj, s // tm),
        in_specs=[pl.BlockSpec((tm, d), lambda j, i: (i, 0)),
                  pl.BlockSpec((1, d), lambda j, i: (0, 0)),
                  pl.BlockSpec((None, d, tn), lambda j, i: (layer, 0, j)),
                  pl.BlockSpec((None, d, tn), lambda j, i: (layer, 0, j + nj))],
        out_specs=[pl.BlockSpec((2, tm, tn), lambda j, i: (0, i, j)),
                   pl.BlockSpec((tm, tn), lambda j, i: (i, j)),
                   pl.BlockSpec((None, tm, d), lambda j, i: (j, i, 0))],
        out_shape=[jax.ShapeDtypeStruct((2, s, f), BF16), jax.ShapeDtypeStruct((s, f), BF16),
                   jax.ShapeDtypeStruct((nj, s, d), BF16)],
        compiler_params=_cparams("parallel", "parallel"),
    )(x, gain, w_gu, w_gu)


def _ffn_down_bwd(dxo, w_down, gu, layer, name, after=None):
    s, d = dxo.shape
    f = w_down.shape[1]
    tm, tn = FFN_TM, f // 2
    extra_specs, extra = ([ANY], [after]) if after is not None else ([], [])

    def body(dx_ref, w_ref, gu_ref, *rest):
        dgu_ref = rest[-1]
        da = 0.5 * _dot(dx_ref[...], w_ref[...], "nt")
        g = gu_ref[0].astype(F32)
        u = gu_ref[1].astype(F32)
        sg = _sigmoid(g)
        dgu_ref[0] = (da * u * _silu_grad(g, sg)).astype(BF16)
        dgu_ref[1] = (da * g * sg).astype(BF16)

    return pl.pallas_call(
        body, name=name, grid=(f // tn, s // tm),
        in_specs=[pl.BlockSpec((tm, d), lambda j, i: (i, 0)),
                  pl.BlockSpec((None, tn, d), lambda j, i: (layer, j, 0)),
                  pl.BlockSpec((2, tm, tn), lambda j, i: (0, i, j))] + extra_specs,
        out_specs=pl.BlockSpec((2, tm, tn), lambda j, i: (0, i, j)),
        out_shape=jax.ShapeDtypeStruct((2, s, f), BF16),
        compiler_params=_cparams("parallel", "parallel"),
    )(dxo, w_down, gu, *extra)


NORM_BWD_TM = 512


def _norm_bwd_after(terms, operands, specs, x, dres, gain, name):
    s, d = x.shape
    tm = NORM_BWD_TM
    n_op = len(operands)

    def body(*refs):
        x_ref, dres_ref, g_ref = refs[n_op:n_op + 3]
        dx_ref, dx16_ref, dgain_ref = refs[n_op + 3:]
        dn = None
        for a, b in terms(*refs[:n_op]):
            dn = _dot(a, b, "nt") if dn is None else dn + _dot(a, b, "nt")
        xb = x_ref[...]
        r = lax.rsqrt(jnp.mean(xb * xb, axis=-1, keepdims=True) + EPS)
        xh = xb * r
        dxh = dn * g_ref[...]
        dx = dres_ref[...] + r * (dxh - xh * jnp.mean(dxh * xh, axis=-1, keepdims=True))
        dx_ref[...] = dx
        dx16_ref[...] = dx.astype(BF16)

        @pl.when(pl.program_id(0) == 0)
        def _():
            dgain_ref[...] = jnp.zeros_like(dgain_ref)
        dgain_ref[...] += jnp.sum(dn * xh, axis=0, keepdims=True)

    rows = pl.BlockSpec((tm, d), lambda i: (i, 0))
    return pl.pallas_call(
        body, name=name, grid=(s // tm,),
        in_specs=list(specs) + [rows, rows, pl.BlockSpec((1, d), lambda i: (0, 0))],
        out_specs=[rows, rows, pl.BlockSpec((1, d), lambda i: (0, 0))],
        out_shape=[jax.ShapeDtypeStruct((s, d), F32), jax.ShapeDtypeStruct((s, d), BF16),
                   jax.ShapeDtypeStruct((1, d), F32)],
        compiler_params=_cparams("arbitrary"),
    )(*operands, x, dres, gain)


def _ffn_up_bwd(dgu, w_gu, layer, x, dres, gain, name):
    _, s, f = dgu.shape
    d = w_gu.shape[1]
    specs = [pl.BlockSpec((2, NORM_BWD_TM, f), lambda i: (0, i, 0)),
             pl.BlockSpec((None, d, f), lambda i: (layer, 0, 0)),
             pl.BlockSpec((None, d, f), lambda i: (layer, 0, 1))]
    terms = lambda dgu_ref, wg_ref, wu_ref: [(dgu_ref[0], wg_ref[...]), (dgu_ref[1], wu_ref[...])]
    return _norm_bwd_after(terms, [dgu, w_gu, w_gu], specs, x, dres, gain, name)


def _in_proj_bwd(dproj, w_in, j, x, dres, gain, name):
    k = dproj.shape[1]
    d = w_in.shape[1]
    specs = [pl.BlockSpec((NORM_BWD_TM, k), lambda i: (i, 0)), pl.BlockSpec((None, d, k), lambda i: (j, 0, 0))]
    terms = lambda a_ref, b_ref: [(a_ref[...], b_ref[...])]
    return _norm_bwd_after(terms, [dproj, w_in], specs, x, dres, gain, name)


def _ffn_fwd(x, gain, w_gu, w_down, layer, tag):
    gu, a, n = _ffn_up(x, gain, w_gu, layer, f"{tag}_up")
    x2 = _mm(a, w_down, "nn", tm=FFN_TM, tn=x.shape[1], out_dtype=F32, name=f"{tag}_down", scale=0.5, residual=x,
             b_lead=(layer,))
    return x2, (x, n, gu, a)


def _ffn_bwd(dxo, dxo16, saved, gain, w_gu, w_down, layer, tag, g_gu, g_down, after=None):
    x, n, gu, a = saved
    s, f = a.shape
    dgu = _ffn_down_bwd(dxo16, w_down, gu, layer, f"{tag}_down_bwd", after)
    g_down = _mm(a, dxo16, "tn", tm=256, tn=dxo16.shape[1], out_dtype=F32, name=f"{tag}_down_dw", scale=0.5,
                 into=(g_down, 0))
    tn = f // 2
    nj = f // tn
    g_gu = _mm(n, dgu, "tn", tm=512, tn=tn, out_dtype=F32, name=f"{tag}_up_dw", into=(g_gu, 0), n=2 * f, a_lead=(0,),
               b_spec=pl.BlockSpec((None, s, tn), lambda j, i: (j // nj, 0, j % nj)))
    dx, dx16, dgain = _ffn_up_bwd(dgu, w_gu, layer, x, dxo, gain, f"{tag}_up_bwd")
    return dx, dx16, dgain, g_gu, g_down


def _lane_col(blk, lane):
    li = lax.broadcasted_iota(jnp.int32, blk.shape, 1)
    return jnp.sum(jnp.where(li == lane, blk, 0.0), axis=1, keepdims=True)


def _split_dot(x, tri):
    hi = x.astype(BF16)
    lo = (x - hi.astype(F32)).astype(BF16)
    return (lax.dot_general(hi, tri, _DIMS["nn"], preferred_element_type=F32)
            + lax.dot_general(lo, tri, _DIMS["nn"], preferred_element_type=F32))


class _Each:
    def __init__(self, vals):
        self.vals = list(vals)

    def _with(self, other, op):
        others = other.vals if isinstance(other, _Each) else [other] * len(self.vals)
        return _Each(op(a, b) for a, b in zip(self.vals, others))

    def __add__(self, other):
        return self._with(other, lambda a, b: a + b)

    def __sub__(self, other):
        return self._with(other, lambda a, b: a - b)

    def __mul__(self, other):
        return self._with(other, lambda a, b: a * b)

    def __neg__(self):
        return _Each(-a for a in self.vals)


def _each(fn, *args):
    n = max(len(a.vals) for a in args if isinstance(a, _Each))
    res = [fn(*xs) for xs in zip(*[a.vals if isinstance(a, _Each) else [a] * n for a in args])]
    if isinstance(res[0], tuple):
        return tuple(_Each(r) for r in zip(*res))
    return _Each(res)


def _keep(cond, x):
    return _each(lambda v: jnp.where(cond, v, 0.0), x)


def _rowsum(x):
    return _each(lambda v: jnp.sum(v, axis=1, keepdims=True), x)


ATT_HEADS = 2
ATT_WIDTH = ATT_HEADS * HEAD_DIM
_HEAD_COLS = [slice(h * HEAD_DIM, (h + 1) * HEAD_DIM) for h in range(ATT_HEADS)]


def _att_specs(n_heads, s):
    groups = n_heads // ATT_HEADS
    q_spec = pl.BlockSpec((ATT_TQ, ATT_WIDTH), lambda g, i: (i, g))
    k_spec = pl.BlockSpec((s, ATT_WIDTH), lambda g, i: (0, groups + g))
    v_spec = pl.BlockSpec((s, ATT_WIDTH), lambda g, i: (0, 2 * groups + g))
    return q_spec, k_spec, v_spec


def _heads_of(ref, rows=None):
    return _Each(ref[:, cs] if rows is None else ref[rows, cs] for cs in _HEAD_COLS)


def _dot_each(a, b, kind):
    return _each(lambda x, y: _dot(x, y, kind), a, b)


def _att_iotas():
    row = lax.broadcasted_iota(jnp.int32, (ATT_TQ, ATT_TK), 0)
    col = lax.broadcasted_iota(jnp.int32, (ATT_TQ, ATT_TK), 1)
    jr = lax.broadcasted_iota(jnp.int32, (ATT_TK, ATT_TK), 0)
    jc = lax.broadcasted_iota(jnp.int32, (ATT_TK, ATT_TK), 1)
    return row, col, jr, jc


def _sb_fwd(qkv, n_heads, name):
    s = qkv.shape[0]

    def body(q_ref, k_ref, v_ref, o16_ref, o32_ref):
        i = pl.program_id(1)
        q = _heads_of(q_ref)
        row, col, jr, jc = _att_iotas()
        later = (jr > jc).astype(BF16)

        def step(jb, carry, diagonal):
            c_sp, acc = (_Each(part) for part in carry)
            work = []
            for sub in reversed(range(ATT_SUB)):
                keys = pl.ds(pl.multiple_of(jb * ATT_TQ + sub * ATT_TK, ATT_TK), ATT_TK)
                z = _dot_each(q, _heads_of(k_ref, keys), "nt") * ATT_SCALE
                sp = _each(_softplus, z)
                before = (col + sub * ATT_TK) < row if diagonal else None
                spm = _keep(before, sp) if diagonal else sp
                work.append((keys, z - sp, spm, _each(lambda x: _dot(x, later, "nn"), spm), before))
            for keys, logsig, spm, within, before in work:
                a = _each(jnp.exp, logsig - (c_sp + within))
                if diagonal:
                    a = _keep(before, a)
                acc = acc + _each(_split_dot, a, _heads_of(v_ref, keys))
                c_sp = c_sp + _rowsum(spm)
            return tuple(c_sp.vals), tuple(acc.vals)

        zeros = lambda width: tuple(jnp.zeros((ATT_TQ, width), F32) for _ in range(ATT_HEADS))
        carry = step(i, (zeros(1), zeros(HEAD_DIM)), True)
        _, acc = lax.fori_loop(0, i, lambda it, cr: step(i - 1 - it, cr, False), carry)
        for cs, acc_h in zip(_HEAD_COLS, acc):
            o16_ref[:, cs] = acc_h.astype(BF16)
            o32_ref[:, cs] = acc_h

    q_spec, k_spec, v_spec = _att_specs(n_heads, s)
    o_spec = pl.BlockSpec((ATT_TQ, ATT_WIDTH), lambda g, i: (i, g))
    return pl.pallas_call(
        body, name=name, grid=(n_heads // ATT_HEADS, s // ATT_TQ), in_specs=[q_spec, k_spec, v_spec],
        out_specs=[o_spec, o_spec],
        out_shape=[jax.ShapeDtypeStruct((s, n_heads * HEAD_DIM), BF16),
                   jax.ShapeDtypeStruct((s, n_heads * HEAD_DIM), F32)],
        compiler_params=_cparams("parallel", "arbitrary"),
    )(qkv, qkv, qkv)


def _sb_bwd(qkv, o32, do, n_heads, name):
    s = qkv.shape[0]

    def body(q_ref, k_ref, v_ref, o_ref, do_ref, dq_ref, dk_ref, dv_ref):
        i = pl.program_id(1)

        @pl.when(i == 0)
        def _():
            dk_ref[...] = jnp.zeros_like(dk_ref)
            dv_ref[...] = jnp.zeros_like(dv_ref)

        q, do = _heads_of(q_ref), _heads_of(do_ref)
        total = _rowsum(_each(lambda a, b: a.astype(F32) * b, do, _heads_of(o_ref)))
        row, col, jr, jc = _att_iotas()
        later = (jr > jc).astype(BF16)
        not_before = (jr >= jc).astype(BF16)

        def step(jb, carry, diagonal):
            c_sp, c_e, dq = (_Each(part) for part in carry)
            work = []
            for sub in reversed(range(ATT_SUB)):
                keys = pl.ds(pl.multiple_of(jb * ATT_TQ + sub * ATT_TK, ATT_TK), ATT_TK)
                k = _heads_of(k_ref, keys)
                z = _dot_each(q, k, "nt") * ATT_SCALE
                sp = _each(_softplus, z)
                before = (col + sub * ATT_TK) < row if diagonal else None
                spm = _keep(before, sp) if diagonal else sp
                work.append((keys, k, _each(jnp.exp, z - sp), spm, _each(lambda x: _dot(x, later, "nn"), spm),
                             _dot_each(do, _heads_of(v_ref, keys), "nt"), before))
            for keys, k, sig, spm, within, da, before in work:
                a = sig * _each(lambda x: jnp.exp(-x), c_sp + within)
                if diagonal:
                    a = _keep(before, a)
                e = a * da
                left = total - c_e - _each(lambda x: _split_dot(x, not_before), e)
                dz = (e - (e + left) * sig) * ATT_SCALE
                if diagonal:
                    dz = _keep(before, dz)
                dk, dv = _dot_each(dz, q, "tn"), _dot_each(a, do, "tn")
                for cs, dk_h, dv_h in zip(_HEAD_COLS, dk.vals, dv.vals):
                    dk_ref[keys, cs] += dk_h
                    dv_ref[keys, cs] += dv_h
                dq = dq + _dot_each(dz, k, "nn")
                c_sp = c_sp + _rowsum(spm)
                c_e = c_e + _rowsum(e)
            return tuple(c_sp.vals), tuple(c_e.vals), tuple(dq.vals)

        zeros = lambda width: tuple(jnp.zeros((ATT_TQ, width), F32) for _ in range(ATT_HEADS))
        carry = step(i, (zeros(1), zeros(1), zeros(HEAD_DIM)), True)
        _, _, dq = lax.fori_loop(0, i, lambda it, cr: step(i - 1 - it, cr, False), carry)
        for cs, dq_h in zip(_HEAD_COLS, dq):
            dq_ref[:, cs] = dq_h.astype(BF16)

    q_spec, k_spec, v_spec = _att_specs(n_heads, s)
    blk = pl.BlockSpec((ATT_TQ, ATT_WIDTH), lambda g, i: (i, g))
    full = pl.BlockSpec((s, ATT_WIDTH), lambda g, i: (0, g))
    wide = (s, n_heads * HEAD_DIM)
    return pl.pallas_call(
        body, name=name, grid=(n_heads // ATT_HEADS, s // ATT_TQ), in_specs=[q_spec, k_spec, v_spec, blk, blk],
        out_specs=[blk, full, full],
        out_shape=[jax.ShapeDtypeStruct(wide, BF16), jax.ShapeDtypeStruct(wide, F32), jax.ShapeDtypeStruct(wide, F32)],
        compiler_params=_cparams("parallel", "arbitrary"),
    )(qkv, qkv, qkv, o32, do)


def _fox_logits(q, k, cq, ct_ref, keys):
    ck = _Each(ct_ref[h, :, keys] for h in range(ATT_HEADS))
    return _dot_each(q, k, "nt") * ATT_SCALE + (cq - ck)


def _fox_cq(c_ref, group):
    c = c_ref[...]
    return _Each(_lane_col(c, LANE_FORGET + group * ATT_HEADS + h) for h in range(ATT_HEADS))


def _fox_fwd(qkv, c, ct, name):
    s = qkv.shape[0]
    n_heads = N_FOX_HEADS

    def body(q_ref, k_ref, v_ref, c_ref, ct_ref, o_ref, lse_ref):
        g, i = pl.program_id(0), pl.program_id(1)
        q = _heads_of(q_ref)
        cq = _fox_cq(c_ref, g)
        row, col, _, _ = _att_iotas()

        def step(jb, carry, diagonal):
            m, l, acc = (_Each(part) for part in carry)
            work = []
            m_new = m
            for sub in range(ATT_SUB):
                keys = pl.ds(pl.multiple_of(jb * ATT_TQ + sub * ATT_TK, ATT_TK), ATT_TK)
                sc = _fox_logits(q, _heads_of(k_ref, keys), cq, ct_ref, keys)
                valid = (col + sub * ATT_TK) <= row if diagonal else None
                if diagonal:
                    sc = _each(lambda x: jnp.where(valid, x, -1e30), sc)
                m_new = _each(lambda a, x: jnp.maximum(a, jnp.max(x, axis=1, keepdims=True)), m_new, sc)
                work.append((keys, sc, valid))
            w = _each(jnp.exp, m - m_new)
            l, acc = l * w, acc * w
            for keys, sc, valid in work:
                p = _each(jnp.exp, sc - m_new)
                if diagonal:
                    p = _keep(valid, p)
                l = l + _rowsum(p)
                acc = acc + _each(_split_dot, p, _heads_of(v_ref, keys))
            return tuple(m_new.vals), tuple(l.vals), tuple(acc.vals)

        per_head = lambda width, value: tuple(jnp.full((ATT_TQ, width), value, F32) for _ in range(ATT_HEADS))
        init = (per_head(1, -1e30), per_head(1, 0.0), per_head(HEAD_DIM, 0.0))
        m, l, acc = lax.fori_loop(0, i, lambda jb, cr: step(jb, cr, False), step(i, init, True))
        for h, cs in enumerate(_HEAD_COLS):
            o_ref[:, cs] = acc[h] / l[h]
            lse_ref[h] = jnp.broadcast_to(m[h] + jnp.log(l[h]), (ATT_TQ, LANES))

    q_spec, k_spec, v_spec = _att_specs(n_heads, s)
    return pl.pallas_call(
        body, name=name, grid=(n_heads // ATT_HEADS, s // ATT_TQ),
        in_specs=[q_spec, k_spec, v_spec, pl.BlockSpec((ATT_TQ, LANES), lambda g, i: (i, 0)),
                  pl.BlockSpec((ATT_HEADS, 1, s), lambda g, i: (g, 0, 0))],
        out_specs=[pl.BlockSpec((ATT_TQ, ATT_WIDTH), lambda g, i: (i, g)),
                   pl.BlockSpec((ATT_HEADS, ATT_TQ, LANES), lambda g, i: (g, i, 0))],
        out_shape=[jax.ShapeDtypeStruct((s, n_heads * HEAD_DIM), F32),
                   jax.ShapeDtypeStruct((n_heads, s, LANES), F32)],
        compiler_params=_cparams("parallel", "arbitrary"),
    )(qkv, qkv, qkv, c, ct)


def _fox_bwd(qkv, c, ct, o, lse, do, name):
    s = qkv.shape[0]
    n_heads = N_FOX_HEADS

    def body(q_ref, k_ref, v_ref, c_ref, ct_ref, o_ref, lse_ref, do_ref, dq_ref, dk_ref, dv_ref, dct_ref):
        g, i = pl.program_id(0), pl.program_id(1)

        @pl.when(i == 0)
        def _():
            dk_ref[...] = jnp.zeros_like(dk_ref)
            dv_ref[...] = jnp.zeros_like(dv_ref)
            dct_ref[...] = jnp.zeros_like(dct_ref)

        q = _heads_of(q_ref)
        do16 = _each(lambda x: x.astype(BF16), _heads_of(do_ref))
        delta = _rowsum(_each(lambda a, b: a.astype(F32) * b, do16, _heads_of(o_ref)))
        lse_col = _Each(lse_ref[h, :, 0:1] for h in range(ATT_HEADS))
        cq = _fox_cq(c_ref, g)
        row, col, _, _ = _att_iotas()

        def step(jb, dq, diagonal):
            dq = _Each(dq)
            for sub in range(ATT_SUB):
                keys = pl.ds(pl.multiple_of(jb * ATT_TQ + sub * ATT_TK, ATT_TK), ATT_TK)
                k = _heads_of(k_ref, keys)
                sc = _fox_logits(q, k, cq, ct_ref, keys)
                if diagonal:
                    valid = (col + sub * ATT_TK) <= row
                    p = _keep(valid, _each(jnp.exp, _keep(valid, sc) - lse_col))
                else:
                    p = _each(jnp.exp, sc - lse_col)
                ds = p * (_dot_each(do16, _heads_of(v_ref, keys), "nt") - delta)
                dss = ds * ATT_SCALE
                dk, dv = _dot_each(dss, q, "tn"), _dot_each(p, do16, "tn")
                for h, cs in enumerate(_HEAD_COLS):
                    dct_ref[h, :, keys] -= jnp.sum(ds.vals[h], axis=0, keepdims=True)
                    dk_ref[keys, cs] += dk.vals[h]
                    dv_ref[keys, cs] += dv.vals[h]
                dq = dq + _dot_each(dss, k, "nn")
            return tuple(dq.vals)

        dq0 = step(i, tuple(jnp.zeros((ATT_TQ, HEAD_DIM), F32) for _ in range(ATT_HEADS)), True)
        dq = lax.fori_loop(0, i, lambda jb, dq: step(jb, dq, False), dq0)
        for cs, dq_h in zip(_HEAD_COLS, dq):
            dq_ref[:, cs] = dq_h

    q_spec, k_spec, v_spec = _att_specs(n_heads, s)
    blk = pl.BlockSpec((ATT_TQ, ATT_WIDTH), lambda g, i: (i, g))
    full = pl.BlockSpec((s, ATT_WIDTH), lambda g, i: (0, g))
    wide = jax.ShapeDtypeStruct((s, n_heads * HEAD_DIM), F32)
    return pl.pallas_call(
        body, name=name, grid=(n_heads // ATT_HEADS, s // ATT_TQ),
        in_specs=[q_spec, k_spec, v_spec, pl.BlockSpec((ATT_TQ, LANES), lambda g, i: (i, 0)),
                  pl.BlockSpec((ATT_HEADS, 1, s), lambda g, i: (g, 0, 0)), blk,
                  pl.BlockSpec((ATT_HEADS, ATT_TQ, LANES), lambda g, i: (g, i, 0)), blk],
        out_specs=[blk, full, full, pl.BlockSpec((ATT_HEADS, 1, s), lambda g, i: (g, 0, 0))],
        out_shape=[wide, wide, wide, jax.ShapeDtypeStruct((n_heads, 1, s), F32)],
        compiler_params=_cparams("parallel", "arbitrary"),
    )(qkv, qkv, qkv, c, ct, o, lse, do)


def _cumsum_rows(x, reverse, name):
    s = x.shape[0]
    nb = s // LANES

    def body(x_ref, o_ref):
        r = lax.broadcasted_iota(jnp.int32, (LANES, LANES), 0)
        c = lax.broadcasted_iota(jnp.int32, (LANES, LANES), 1)
        tri = ((r <= c) if reverse else (r >= c)).astype(F32)

        def step(it, carry):
            b = (nb - 1 - it) if reverse else it
            off = pl.multiple_of(b * LANES, LANES)
            blk = x_ref[pl.ds(off, LANES), :]
            o_ref[pl.ds(off, LANES), :] = _dot32(tri, blk) + carry
            return carry + jnp.sum(blk, axis=0, keepdims=True)

        lax.fori_loop(0, nb, step, jnp.zeros((1, LANES), F32))

    return pl.pallas_call(body, name=name, out_shape=jax.ShapeDtypeStruct(x.shape, F32),
                          compiler_params=pltpu.CompilerParams(vmem_limit_bytes=V7X_VMEM_LIMIT))(x)


def _dot32_each(a, b, kind="nn"):
    return _each(lambda x, y: _dot32(x, y, kind), a, b)


def _unit_lower_inverse(m, ri, ci):
    c = ri.shape[0]
    t = -_keep(ri // 2 == ci // 2, m) + jnp.where(ri == ci, 1.0, 0.0)
    b = 4
    while b <= c:
        off_diag = (ri // b == ci // b) & (ri % b >= b // 2) & (ci % b < b // 2)
        t = t - _dot32_each(_dot32_each(t, _keep(off_diag, m)), t)
        b *= 2
    return t


def _dn_gates(g, ri, ci):
    eye = ri == ci
    incl = ri >= ci
    g_row = jnp.sum(jnp.where(eye, g, 0.0), axis=0, keepdims=True)
    gc = jnp.sum(jnp.where(incl, g_row, 0.0), axis=1, keepdims=True)
    gc_row = jnp.sum(jnp.where(eye, gc, 0.0), axis=0, keepdims=True)
    dmat = jnp.where(incl, jnp.exp(jnp.where(incl, gc - gc_row, 0.0)), 0.0)
    gc_last = jnp.sum(g, axis=0, keepdims=True)
    return gc, dmat, jnp.exp(gc), jnp.exp(gc_last - gc), jnp.exp(gc_last)


def _dn_fwd(qkv, act, name):
    s = qkv.shape[0]
    c, d, nh = DN_CHUNK, HEAD_DIM, N_DN_HEADS
    nc = s // c

    def body(q_ref, k_ref, v_ref, act_ref, o_ref, s_ref, t_ref, state):
        @pl.when(pl.program_id(0) == 0)
        def _():
            state[...] = jnp.zeros_like(state)

        ri = lax.broadcasted_iota(jnp.int32, (c, c), 0)
        ci = lax.broadcasted_iota(jnp.int32, (c, c), 1)
        act = act_ref[...]
        heads = range(nh)
        cols = [slice(h * d, (h + 1) * d) for h in heads]
        q, k, v = (_Each(ref[:, cs] for cs in cols) for ref in (q_ref, k_ref, v_ref))
        beta = _Each(_lane_col(act, LANE_BETA + h) for h in heads)
        g = _Each(_lane_col(act, LANE_DECAY + h) for h in heads)
        _, dmat, e, r, gl = _each(lambda gh: _dn_gates(gh, ri, ci), g)
        s0 = _Each(state[h] for h in heads)
        kb = beta * k
        t = _unit_lower_inverse(_keep(ri > ci, _dot32_each(kb, k, "nt") * dmat), ri, ci)
        vn = _dot32_each(t, beta * v) - _dot32_each(_dot32_each(t, kb * e), s0)
        o = _dot32_each(q * e, s0) + _dot32_each(_dot32_each(q, k, "nt") * dmat, vn)
        s1 = s0 * gl + _dot32_each(k * r, vn, "tn")
        for h in heads:
            o_ref[:, cols[h]] = o.vals[h]
            state[h] = s1.vals[h]
            s_ref[h] = s0.vals[h]
            t_ref[h] = t.vals[h]

    wide = lambda part: pl.BlockSpec((c, nh * d), lambda n: (n, part))
    return pl.pallas_call(
        body, name=name, grid=(nc,),
        in_specs=[wide(0), wide(1), wide(2), pl.BlockSpec((c, LANES), lambda n: (n, 0))],
        out_specs=[wide(0), pl.BlockSpec((nh, None, d, d), lambda n: (0, n, 0, 0)),
                   pl.BlockSpec((nh, None, c, c), lambda n: (0, n, 0, 0))],
        out_shape=[jax.ShapeDtypeStruct((s, nh * d), F32), jax.ShapeDtypeStruct((nh, nc, d, d), F32),
                   jax.ShapeDtypeStruct((nh, nc, c, c), F32)],
        scratch_shapes=[pltpu.VMEM((nh, d, d), F32)],
        compiler_params=_cparams("arbitrary"),
    )(qkv, qkv, qkv, act)


def _dn_bwd(qkv, act, states, tinv, do, name):
    s = qkv.shape[0]
    c, d, nh = DN_CHUNK, HEAD_DIM, N_DN_HEADS
    nc = s // c

    def chunk_bwd(q, k, v, do, beta, g, s0, t, ds_out):
        ri = lax.broadcasted_iota(jnp.int32, (c, c), 0)
        ci = lax.broadcasted_iota(jnp.int32, (c, c), 1)
        eye, incl, strict = ri == ci, ri >= ci, ri > ci
        gc, dmat, e, r, gl = _each(lambda gh: _dn_gates(gh, ri, ci), g)
        dot = _dot32_each
        rowsum = lambda x: _each(lambda a: jnp.sum(a, axis=1, keepdims=True), x)
        colsum = lambda x: _each(lambda a: jnp.sum(a, axis=0, keepdims=True), x)
        total = lambda x: colsum(rowsum(x))
        to_col = lambda row: rowsum(_keep(eye, row))
        to_row = lambda colv: colsum(_keep(eye, colv))

        kb, vb = beta * k, beta * v
        kbe = kb * e
        u, w = dot(t, vb), dot(t, kbe)
        vn = u - dot(w, s0)
        qk = dot(q, k, "nt")
        p = qk * dmat
        gram = dot(k, k, "nt")
        kr, qe = k * r, q * e

        d_kr = dot(vn, ds_out, "nt")
        dvn = dot(kr, ds_out)
        dgl = total(s0 * ds_out)
        ds_in = ds_out * gl
        dk = d_kr * r
        dr = rowsum(d_kr * k)
        d_qe = dot(do, s0, "nt")
        ds_in = ds_in + dot(qe, do, "tn")
        dp = _keep(incl, dot(do, vn, "nt"))
        dvn = dvn + dot(p, do, "tn")
        dq = d_qe * e
        de = rowsum(d_qe * q)
        dqk = dp * dmat
        dq = dq + dot(dqk, k)
        dk = dk + dot(dqk, q, "tn")
        dd = dp * qk
        dw = -dot(dvn, s0, "nt")
        ds_in = ds_in - dot(w, dvn, "tn")
        dvb = dot(t, dvn, "tn")
        dkbe = dot(t, dw, "tn")
        dm = -_keep(strict, dot(dvb, u, "nt") + dot(dkbe, w, "nt"))
        dbeta = rowsum(dm * gram * dmat)
        dgram = dm * beta * dmat
        dd = dd + dm * beta * gram
        dk = dk + dot(dgram, k) + dot(dgram, k, "tn")
        dkb = dkbe * e
        de = de + rowsum(dkbe * kb)
        dk = dk + beta * dkb
        dbeta = dbeta + rowsum(dkb * k) + rowsum(dvb * v)
        dv = beta * dvb
        wd = dd * dmat
        dgc = rowsum(wd) - to_col(colsum(wd)) + de * e - dr * r
        dgc_last = total(dr * r) + dgl * gl
        dgc = dgc + _keep(ri[:, 0:1] == c - 1, dgc_last)
        dg = rowsum(_keep(ri <= ci, to_row(dgc)))
        return dq, dk, dv, dbeta, dg, ds_in

    def body(q_ref, k_ref, v_ref, act_ref, s_ref, t_ref, do_ref, dq_ref, dk_ref, dv_ref, dact_ref, dstate):
        @pl.when(pl.program_id(0) == 0)
        def _():
            dstate[...] = jnp.zeros_like(dstate)

        act = act_ref[...]
        heads = range(nh)
        cols = [slice(h * d, (h + 1) * d) for h in heads]
        q, k, v, do = (_Each(ref[:, cs] for cs in cols) for ref in (q_ref, k_ref, v_ref, do_ref))
        dq, dk, dv, dbeta, dg, ds_in = chunk_bwd(
            q, k, v, do, _Each(_lane_col(act, LANE_BETA + h) for h in heads),
            _Each(_lane_col(act, LANE_DECAY + h) for h in heads), _Each(s_ref[h] for h in heads),
            _Each(t_ref[h] for h in heads), _Each(dstate[h] for h in heads))
        lane = lax.broadcasted_iota(jnp.int32, (c, LANES), 1)
        dact = jnp.zeros((c, LANES), F32)
        for h in heads:
            dstate[h] = ds_in.vals[h]
            dq_ref[:, cols[h]], dk_ref[:, cols[h]], dv_ref[:, cols[h]] = dq.vals[h], dk.vals[h], dv.vals[h]
            dact = (dact + jnp.where(lane == LANE_BETA + h, dbeta.vals[h], 0.0)
                    + jnp.where(lane == LANE_DECAY + h, dg.vals[h], 0.0))
        dact_ref[...] = dact

    part = lambda p: pl.BlockSpec((c, nh * d), lambda n: (nc - 1 - n, p))
    per = lambda a, b: pl.BlockSpec((nh, None, a, b), lambda n: (0, nc - 1 - n, 0, 0))
    wide = jax.ShapeDtypeStruct((s, nh * d), F32)
    act_spec = pl.BlockSpec((c, LANES), lambda n: (nc - 1 - n, 0))
    return pl.pallas_call(
        body, name=name, grid=(nc,),
        in_specs=[part(0), part(1), part(2), act_spec, per(d, d), per(c, c), part(0)],
        out_specs=[part(0), part(0), part(0), act_spec],
        out_shape=[wide, wide, wide, jax.ShapeDtypeStruct((s, LANES), F32)],
        scratch_shapes=[pltpu.VMEM((nh, d, d), F32)],
        compiler_params=_cparams("arbitrary"),
    )(qkv, qkv, qkv, act, states, tinv, do)


EVEN_DN_QKV, EVEN_FOX_QKV, EVEN_DN_GATE, EVEN_FOX_GATE, EVEN_NARROW = 0, 1536, 3072, 3584, 4096
EVEN_WIDTH = 4224
CONV_TILE = 256
CONV_HALO = 8


def _conv_fwd(proj, w, name):
    s = proj.shape[0]
    t, cw = CONV_TILE, 3 * D_DN

    def body(cur_ref, prev_ref, w_ref, y_ref, xs):
        i = pl.program_id(0)
        xs[0:CONV_HALO, :] = jnp.where(i > 0, prev_ref[...], 0.0)
        xs[CONV_HALO:, :] = cur_ref[...]
        y = jnp.zeros((t, cw), F32)
        for tap in range(CONV_WIDTH):
            y = y + w_ref[tap:tap + 1, :] * xs[pl.ds(CONV_HALO - CONV_WIDTH + 1 + tap, t), :]
        y_ref[...] = y

    per = t // CONV_HALO
    return pl.pallas_call(
        body, name=name, grid=(s // t,),
        in_specs=[pl.BlockSpec((t, cw), lambda i: (i, 0)),
                  pl.BlockSpec((CONV_HALO, cw), lambda i: (jnp.maximum(i * per - 1, 0), 0)),
                  pl.BlockSpec((CONV_WIDTH, cw), lambda i: (0, 0))],
        out_specs=pl.BlockSpec((t, cw), lambda i: (i, 0)),
        out_shape=jax.ShapeDtypeStruct((s, cw), F32),
        scratch_shapes=[pltpu.VMEM((t + CONV_HALO, cw), F32)],
        compiler_params=_cparams("parallel"),
    )(proj, proj, w)


def _conv_bwd(proj, w, dy, name):
    s = proj.shape[0]
    t, cw = CONV_TILE, 3 * D_DN
    nt = s // t

    def body(cur_ref, prev_ref, w_ref, dy_ref, nxt_ref, dx_ref, dw_ref, xs, dys):
        i = pl.program_id(0)

        @pl.when(i == 0)
        def _():
            dw_ref[...] = jnp.zeros_like(dw_ref)

        xs[0:CONV_HALO, :] = jnp.where(i > 0, prev_ref[...], 0.0)
        xs[CONV_HALO:, :] = cur_ref[...]
        dys[0:t, :] = dy_ref[...]
        dys[t:, :] = jnp.where(i < nt - 1, nxt_ref[...], 0.0)
        dy = dy_ref[...]
        dx = jnp.zeros((t, cw), F32)
        for tap in range(CONV_WIDTH):
            dx = dx + w_ref[tap:tap + 1, :] * dys[pl.ds(CONV_WIDTH - 1 - tap, t), :]
            dw_ref[tap:tap + 1, :] += jnp.sum(dy * xs[pl.ds(CONV_HALO - CONV_WIDTH + 1 + tap, t), :], axis=0,
                                              keepdims=True)
        dx_ref[...] = dx.astype(BF16)

    per = t // CONV_HALO
    last = s // CONV_HALO - 1
    return pl.pallas_call(
        body, name=name, grid=(nt,),
        in_specs=[pl.BlockSpec((t, cw), lambda i: (i, 0)),
                  pl.BlockSpec((CONV_HALO, cw), lambda i: (jnp.maximum(i * per - 1, 0), 0)),
                  pl.BlockSpec((CONV_WIDTH, cw), lambda i: (0, 0)),
                  pl.BlockSpec((t, cw), lambda i: (i, 0)),
                  pl.BlockSpec((CONV_HALO, cw), lambda i: (jnp.minimum((i + 1) * per, last), 0))],
        out_specs=[pl.BlockSpec((t, cw), lambda i: (i, 0)), pl.BlockSpec((CONV_WIDTH, cw), lambda i: (0, 0))],
        out_shape=[jax.ShapeDtypeStruct((s, cw), BF16), jax.ShapeDtypeStruct((CONV_WIDTH, cw), F32)],
        scratch_shapes=[pltpu.VMEM((t + CONV_HALO, cw), F32), pltpu.VMEM((t + CONV_HALO, cw), F32)],
        compiler_params=_cparams("arbitrary"),
    )(proj, proj, w, dy, dy)


def _heads(x, n):
    return [x[:, HEAD_DIM * h:HEAD_DIM * (h + 1)] for h in range(n)]


def _dn_pre_fwd(y, name):
    def fn(yb):
        cs = yb * _sigmoid(yb)
        out = []
        for idx, xh in enumerate(_heads(cs, 3 * N_DN_HEADS)):
            if idx < 2 * N_DN_HEADS:
                xh = xh * lax.rsqrt(jnp.sum(xh * xh, axis=-1, keepdims=True) + EPS)
                if idx < N_DN_HEADS:
                    xh = xh * ATT_SCALE
            out.append(xh)
        return (jnp.concatenate(out, axis=1),)
    return _rowwise(fn, [y], [], [(y.shape[1], F32)], [], tile=256, name=name)[0]


def _dn_pre_bwd(y, dq, dk, dv, name):
    def fn(yb, dqb, dkb, dvb):
        sg = _sigmoid(yb)
        cs = yb * sg
        dout = _heads(dqb, N_DN_HEADS) + _heads(dkb, N_DN_HEADS) + _heads(dvb, N_DN_HEADS)
        dcs = []
        for idx, (xh, dh) in enumerate(zip(_heads(cs, 3 * N_DN_HEADS), dout)):
            if idx < 2 * N_DN_HEADS:
                if idx < N_DN_HEADS:
                    dh = dh * ATT_SCALE
                r = lax.rsqrt(jnp.sum(xh * xh, axis=-1, keepdims=True) + EPS)
                xhat = xh * r
                dh = r * (dh - xhat * jnp.sum(xhat * dh, axis=-1, keepdims=True))
            dcs.append(dh)
        return (jnp.concatenate(dcs, axis=1) * _silu_grad(yb, sg),)
    return _rowwise(fn, [y, dq, dk, dv], [], [(y.shape[1], F32)], [], tile=256, name=name)[0]


def _narrow_params(a_log, dt_bias, f_bias):
    lanes = lambda a, first: jnp.pad(a.reshape(1, -1), ((0, 0), (first, LANES - first - a.shape[0])))
    return jnp.concatenate([lanes(a_log, LANE_DECAY), lanes(dt_bias, LANE_DECAY), lanes(f_bias, LANE_FORGET),
                            jnp.zeros((5, LANES), F32)], axis=0)


def _narrow_masks(shape):
    lane = lax.broadcasted_iota(jnp.int32, shape, 1)
    is_beta = lane < LANE_DECAY
    is_decay = (lane >= LANE_DECAY) & (lane < LANE_FORGET)
    is_forget = (lane >= LANE_FORGET) & (lane < LANE_FORGET + N_FOX_HEADS)
    return is_beta, is_decay, is_forget


def _narrow_fwd(proj, params, name):
    def fn(sm, pk):
        is_beta, is_decay, is_forget = _narrow_masks(sm.shape)
        g = -jnp.exp(pk[0:1, :]) * _softplus(sm + pk[1:2, :])
        logf = -_softplus(-(sm + pk[2:3, :]))
        return (jnp.where(is_beta, _sigmoid(sm), jnp.where(is_decay, g, jnp.where(is_forget, logf, 0.0))),)
    return _rowwise(fn, [(proj, LANES, EVEN_NARROW // LANES)], [params], [(LANES, F32)], [], tile=512, name=name)[0]


def _narrow_bwd(proj, params, act, dact, dlogf, name):
    def fn(sm, ab, da, dl, pk):
        is_beta, is_decay, is_forget = _narrow_masks(sm.shape)
        db = jnp.where(is_forget, dl, da)
        d_beta = db * ab * (1.0 - ab)
        d_decay = db * (-jnp.exp(pk[0:1, :])) * _sigmoid(sm + pk[1:2, :])
        d_forget = db * _sigmoid(-(sm + pk[2:3, :]))
        dsm = jnp.where(is_beta, d_beta, jnp.where(is_decay, d_decay, jnp.where(is_forget, d_forget, 0.0)))
        col = lambda x: jnp.sum(x, axis=0, keepdims=True)
        return (dsm, col(jnp.where(is_decay, db * ab, 0.0)), col(jnp.where(is_decay, dsm, 0.0)),
                col(jnp.where(is_forget, dsm, 0.0)))
    return _rowwise(fn, [(proj, LANES, EVEN_NARROW // LANES), act, dact, dlogf], [params], [(LANES, BF16)],
                    [(1, LANES)] * 3, tile=512, name=name)


def _head_rms(xh):
    r = lax.rsqrt(jnp.mean(xh * xh, axis=-1, keepdims=True) + EPS)
    return xh * r, r


def _fox_pre_fwd(proj, qg, kg, name):
    def fn(pf, qgb, kgb):
        out = []
        for idx, xh in enumerate(_heads(pf, 3 * N_FOX_HEADS)):
            if idx < 2 * N_FOX_HEADS:
                xh = _head_rms(xh)[0] * (qgb if idx < N_FOX_HEADS else kgb)
            out.append(xh)
        return (jnp.concatenate(out, axis=1),)
    return _rowwise(fn, [(proj, 3 * D_FOX, EVEN_FOX_QKV // (3 * D_FOX))], [qg, kg], [(3 * D_FOX, BF16)], [],
                    tile=256, name=name)[0]


def _fox_pre_bwd(proj, qg, kg, dq, dk, dv, name):
    def fn(pf, dqb, dkb, dvb, qgb, kgb):
        dout = _heads(dqb, N_FOX_HEADS) + _heads(dkb, N_FOX_HEADS) + _heads(dvb, N_FOX_HEADS)
        dg = [jnp.zeros((1, HEAD_DIM), F32), jnp.zeros((1, HEAD_DIM), F32)]
        dx = []
        for idx, (xh, dh) in enumerate(zip(_heads(pf, 3 * N_FOX_HEADS), dout)):
            if idx < 2 * N_FOX_HEADS:
                which = 0 if idx < N_FOX_HEADS else 1
                xhat, r = _head_rms(xh)
                dg[which] = dg[which] + jnp.sum(dh * xhat, axis=0, keepdims=True)
                dxh = dh * (qgb if which == 0 else kgb)
                dh = r * (dxh - xhat * jnp.mean(dxh * xhat, axis=-1, keepdims=True))
            dx.append(dh)
        return jnp.concatenate(dx, axis=1), dg[0], dg[1]
    return _rowwise(fn, [(proj, 3 * D_FOX, EVEN_FOX_QKV // (3 * D_FOX)), dq, dk, dv], [qg, kg],
                    [(3 * D_FOX, BF16)], [(1, HEAD_DIM)] * 2, tile=256, name=name)


def _mix_gate_fwd(proj, o_dn, o_fox, ng, name):
    def fn(gd, gf, od, of, ngb):
        dn = [_head_rms(xh)[0] * ngb for xh in _heads(od, N_DN_HEADS)]
        return (jnp.concatenate([jnp.concatenate(dn, axis=1) * gd * _sigmoid(gd), of * _sigmoid(gf)], axis=1),)
    return _rowwise(fn, [(proj, D_DN, EVEN_DN_GATE // D_DN), (proj, D_FOX, EVEN_FOX_GATE // D_FOX), o_dn, o_fox],
                    [ng], [(D_DN + D_FOX, BF16)], [], tile=256, name=name)[0]


def _mix_gate_bwd(proj, o_dn, o_fox, ng, dom, name):
    def fn(gd, gf, od, of, dm, ngb):
        d_dn, d_fox = dm[:, :D_DN], dm[:, D_DN:]
        sgd, sgf = _sigmoid(gd), _sigmoid(gf)
        don = d_dn * gd * sgd
        dng = jnp.zeros((1, HEAD_DIM), F32)
        dod, normed = [], []
        for xh, dh in zip(_heads(od, N_DN_HEADS), _heads(don, N_DN_HEADS)):
            xhat, r = _head_rms(xh)
            dng = dng + jnp.sum(dh * xhat, axis=0, keepdims=True)
            dxh = dh * ngb
            dod.append(r * (dxh - xhat * jnp.mean(dxh * xhat, axis=-1, keepdims=True)))
            normed.append(xhat * ngb)
        d_gd = d_dn * jnp.concatenate(normed, axis=1) * _silu_grad(gd, sgd)
        d_gf = d_fox * of * sgf * (1.0 - sgf)
        return jnp.concatenate(dod, axis=1), d_fox * sgf, d_gd, d_gf, dng
    return _rowwise(fn, [(proj, D_DN, EVEN_DN_GATE // D_DN), (proj, D_FOX, EVEN_FOX_GATE // D_FOX), o_dn, o_fox, dom],
                    [ng], [(D_DN, F32), (D_FOX, F32), (D_DN, BF16), (D_FOX, BF16)], [(1, HEAD_DIM)], tile=256,
                    name=name)


def _loss_grad(y, target, name):
    d = y.shape[1]

    def fn(yb, tb):
        diff = yb - tb
        part = jnp.sum(jnp.sum(diff * diff, axis=1, keepdims=True), axis=0, keepdims=True) * (0.5 / d)
        g = diff * (1.0 / d)
        return g, g, part
    return _rowwise(fn, [y, target], [], [(d, F32), (d, BF16)], [(1, 1)], tile=512, name=name)


_REF_EVEN = {"dn_qkv": (0, 1536), "dn_gate": (1536, 2048), "dn_ba": (2048, 2056), "fox_qkv": (2056, 3592),
             "fox_gate": (3592, 4104), "f_pre": (4104, 4108)}
D_IN_EVEN = 4108


def _even_to_kernel_layout(w):
    cut = lambda name: w[..., _REF_EVEN[name][0]:_REF_EVEN[name][1]]
    pad = jnp.zeros(w.shape[:-1] + (EVEN_WIDTH - EVEN_NARROW - 12,), w.dtype)
    return jnp.concatenate([cut("dn_qkv"), cut("fox_qkv"), cut("dn_gate"), cut("fox_gate"), cut("dn_ba"),
                            cut("f_pre"), pad], axis=-1)


def _even_from_kernel_layout(g):
    return jnp.concatenate([g[..., EVEN_DN_QKV:EVEN_FOX_QKV], g[..., EVEN_DN_GATE:EVEN_FOX_GATE],
                            g[..., EVEN_NARROW:EVEN_NARROW + 8], g[..., EVEN_FOX_QKV:EVEN_DN_GATE],
                            g[..., EVEN_FOX_GATE:EVEN_NARROW], g[..., EVEN_NARROW + 8:EVEN_NARROW + 12]], axis=-1)


EVEN_QUARTER = 1027
EVEN_QUARTER_PAD = 1152


def _even_grad_quarters(g):
    g = _even_from_kernel_layout(g)
    pad = [(0, 0)] * (g.ndim - 1) + [(0, EVEN_QUARTER_PAD - EVEN_QUARTER)]
    return jnp.concatenate([jnp.pad(g[..., q * EVEN_QUARTER:(q + 1) * EVEN_QUARTER], pad) for q in range(4)], axis=-1)


def _forget_rows(c):
    return c[:, LANE_FORGET:LANE_FORGET + N_FOX_HEADS].T.reshape(N_FOX_HEADS, 1, c.shape[0])


def _forget_lanes(rows):
    s = rows.shape[2]
    return jnp.pad(rows.reshape(-1, s).T, ((0, 0), (LANE_FORGET, LANES - LANE_FORGET - N_FOX_HEADS)))


def _even_fwd(x, gain, w_in, w_out, j, p, tag):
    proj, h = _norm_in_proj(x, gain, w_in, j, EVEN_WIDTH // 3, F32, f"{tag}_in")
    y = _conv_fwd(proj, p["conv_w"], f"{tag}_conv")
    dn_qkv = _dn_pre_fwd(y, f"{tag}_dn_pre")
    act = _narrow_fwd(proj, p["narrow"], f"{tag}_narrow")
    o_dn, states, tinv = _dn_fwd(dn_qkv, act, f"{tag}_delta")
    fox_qkv = _fox_pre_fwd(proj, p["q_g"], p["k_g"], f"{tag}_fox_pre")
    c = _cumsum_rows(act, False, f"{tag}_cumsum")
    ct = _forget_rows(c)
    o_fox, lse = _fox_fwd(fox_qkv, c, ct, f"{tag}_fox")
    om = _mix_gate_fwd(proj, o_dn, o_fox, p["dn_norm_g"], f"{tag}_gate")
    x2 = _mm(om, w_out, "nn", tm=512, tn=x.shape[1], out_dtype=F32, name=f"{tag}_out", residual=x, b_lead=(j,))
    return x2, (x, h, proj, y, dn_qkv, act, states, tinv, o_dn, fox_qkv, c, ct, o_fox, lse, om)


def _even_bwd(dxo, dxo16, saved, gain, w_in, w_out, j, p, tag, g_in, g_out, after=None):
    x, h, proj, y, dn_qkv, act, states, tinv, o_dn, fox_qkv, c, ct, o_fox, lse, om = saved
    d = x.shape[1]
    dom = _mm(dxo16, w_out, "nt", tm=512, tn=d, out_dtype=F32, name=f"{tag}_out_bwd", b_lead=(j,), after=after)
    g_out = _mm(om, dxo16, "tn", tm=512, tn=d, out_dtype=F32, name=f"{tag}_out_dw", into=(g_out, 0))
    d_odn, d_ofox, d_gd, d_gf, d_ng = _mix_gate_bwd(proj, o_dn, o_fox, p["dn_norm_g"], dom, f"{tag}_gate_bwd")
    dq, dk, dv, dct = _fox_bwd(fox_qkv, c, ct, o_fox, lse, d_ofox, f"{tag}_fox_bwd")
    d_fox_qkv, d_qg, d_kg = _fox_pre_bwd(proj, p["q_g"], p["k_g"], dq, dk, dv, f"{tag}_fox_pre_bwd")
    dlogf = _cumsum_rows(_forget_lanes(dct), True, f"{tag}_cumsum_bwd")
    dq, dk, dv, dact = _dn_bwd(dn_qkv, act, states, tinv, d_odn, f"{tag}_delta_bwd")
    dy = _dn_pre_bwd(y, dq, dk, dv, f"{tag}_dn_pre_bwd")
    d_dn_qkv, d_conv = _conv_bwd(proj, p["conv_w"], dy, f"{tag}_conv_bwd")
    d_narrow, s_alog, s_dt, s_fb = _narrow_bwd(proj, p["narrow"], act, dact, dlogf, f"{tag}_narrow_bwd")
    dproj = jnp.concatenate([d_dn_qkv, d_fox_qkv, d_gd, d_gf, d_narrow], axis=1)
    g_in = _mm(h, dproj, "tn", tm=512, tn=EVEN_WIDTH // 3, out_dtype=F32, name=f"{tag}_in_dw", into=(g_in, 0),
               a_lead=(0,))
    dx, dx16, d_gain = _in_proj_bwd(dproj, w_in, j, x, dxo, gain, f"{tag}_in_bwd")
    small = {"conv_w": d_conv, "a_log": s_alog, "dt_bias": s_dt, "f_bias": s_fb, "dn_norm_g": d_ng, "q_g": d_qg,
             "k_g": d_kg}
    return dx, dx16, d_gain, small, g_in, g_out


def _odd_fwd(x, gain, w_in, w_out, j, tag):
    qkv, h = _norm_in_proj(x, gain, w_in, j, w_in.shape[2] // 2, BF16, f"{tag}_in")
    o16, o32 = _sb_fwd(qkv, N_SB_HEADS, f"{tag}_sb")
    x2 = _mm(o16, w_out, "nn", tm=512, tn=x.shape[1], out_dtype=F32, name=f"{tag}_out", residual=x, b_lead=(j,))
    return x2, (x, h, qkv, o16, o32)


def _odd_bwd(dxo, dxo16, saved, gain, w_in, w_out, j, tag, g_in, g_out, after=None):
    x, h, qkv, o16, o32 = saved
    d = x.shape[1]
    do = _mm(dxo16, w_out, "nt", tm=512, tn=d, out_dtype=BF16, name=f"{tag}_out_bwd", b_lead=(j,), after=after)
    g_out = _mm(o16, dxo16, "tn", tm=512, tn=d, out_dtype=F32, name=f"{tag}_out_dw", into=(g_out, 0))
    dq, dk, dv = _sb_bwd(qkv, o32, do, N_SB_HEADS, f"{tag}_sb_bwd")
    dqkv = jnp.concatenate([dq, dk.astype(BF16), dv.astype(BF16)], axis=1)
    g_in = _mm(h, dqkv, "tn", tm=512, tn=w_in.shape[2] // 2, out_dtype=F32, name=f"{tag}_in_dw", into=(g_in, 0),
               a_lead=(0,))
    dx, dx16, d_gain = _in_proj_bwd(dqkv, w_in, j, x, dxo, gain, f"{tag}_in_bwd")
    return dx, dx16, d_gain, g_in, g_out


def _forward_backward(x, target, w, first, rest_after, token, on_reduced):
    depth = w["norm_ffn1"].shape[0]
    row = lambda a, l: a[l][None]
    rest = {}

    def mats(names, j):
        if j == 0 and names[0] in first:
            return [first[name] for name in names] + [0]
        return [rest[name] for name in names] + [j - (1 if names[0] in first else 0)]

    def even_small(j):
        return {"conv_w": w["dn_conv_w"][j], "narrow": _narrow_params(w["dn_a_log"][j], w["dn_dt_bias"][j],
                                                                     w["fox_f_bias"][j]),
                "dn_norm_g": row(w["dn_norm_g"], j), "q_g": row(w["fox_q_norm_g"], j),
                "k_g": row(w["fox_k_norm_g"], j)}

    saved = []
    for l in range(depth):
        if l == 1:
            rest.update(rest_after(x))
        gain = row(w["norm_ffn1"], l) + token[0:1, 0:1] if l == 0 else row(w["norm_ffn1"], l)
        x, s1 = _ffn_fwd(x, gain, *mats(("ffn1_w_gu", "ffn1_w_down"), l), "ffn1")
        if l % 2 == 0:
            x, s2 = _even_fwd(x, row(w["norm_mix"], l), *mats(("w_in_even", "w_out_even"), l // 2),
                              even_small(l // 2), "even")
        else:
            x, s2 = _odd_fwd(x, row(w["norm_mix"], l), *mats(("w_in_odd", "w_out_odd"), l // 2), "odd")
        x, s3 = _ffn_fwd(x, row(w["norm_ffn2"], l), *mats(("ffn2_w_gu", "ffn2_w_down"), l), "ffn2")
        saved.append((s1, s2, s3))

    dx, dx16, loss = _loss_grad(x, target, "loss")

    kind_of = dict(BIG)
    d_norm = {k: [None] * depth for k in ("norm_ffn1", "norm_mix", "norm_ffn2")}
    d_even = [None] * ((depth + 1) // 2)
    to_sibling, between_chips, token = None, None, None
    for l in reversed(range(depth)):
        s1, s2, s3 = saved[l]
        mixer = ("w_in_even", "w_out_even") if l % 2 == 0 else ("w_in_odd", "w_out_odd")
        names = ["ffn1_w_gu", "ffn1_w_down", *mixer, "ffn2_w_gu", "ffn2_w_down"]
        g = {name: lax.empty((1,) + rest[name].shape[1:], F32) for name in names}
        dx, dx16, d_norm["norm_ffn2"][l], g["ffn2_w_gu"], g["ffn2_w_down"] = _ffn_bwd(
            dx, dx16, s3, row(w["norm_ffn2"], l), *mats(("ffn2_w_gu", "ffn2_w_down"), l), "ffn2", g["ffn2_w_gu"],
            g["ffn2_w_down"], after=token)
        if to_sibling is not None:
            between_chips, token = _reduce_middle(to_sibling, dx)
        if l % 2 == 0:
            dx, dx16, d_norm["norm_mix"][l], d_even[l // 2], g["w_in_even"], g["w_out_even"] = _even_bwd(
                dx, dx16, s2, row(w["norm_mix"], l), *mats(("w_in_even", "w_out_even"), l // 2), even_small(l // 2),
                "even", g["w_in_even"], g["w_out_even"], after=token)
            g["w_in_even"] = _even_grad_quarters(g["w_in_even"])
        else:
            dx, dx16, d_norm["norm_mix"][l], g["w_in_odd"], g["w_out_odd"] = _odd_bwd(
                dx, dx16, s2, row(w["norm_mix"], l), *mats(("w_in_odd", "w_out_odd"), l // 2), "odd", g["w_in_odd"],
                g["w_out_odd"], after=token)
        dx, dx16, d_norm["norm_ffn1"][l], g["ffn1_w_gu"], g["ffn1_w_down"] = _ffn_bwd(
            dx, dx16, s1, row(w["norm_ffn1"], l), *mats(("ffn1_w_gu", "ffn1_w_down"), l), "ffn1", g["ffn1_w_gu"],
            g["ffn1_w_down"])
        to_sibling, token = _reduce_start([g[name] for name in names], [kind_of[name] for name in names], names,
                                          f"layer{l}")
        if between_chips is not None:
            on_reduced(l + 1, dict(zip(between_chips[-2], _reduce_finish(between_chips, dx))))
    between_chips, _ = _reduce_middle(to_sibling, dx)
    on_reduced(0, dict(zip(between_chips[-2], _reduce_finish(between_chips, dx))))

    small = {k: jnp.concatenate(v, axis=0) for k, v in d_norm.items()}
    dec = slice(LANE_DECAY, LANE_DECAY + N_DN_HEADS)
    fgt = slice(LANE_FORGET, LANE_FORGET + N_FOX_HEADS)
    small["dn_conv_w"] = jnp.stack([e["conv_w"] for e in d_even])
    small["dn_a_log"] = jnp.concatenate([e["a_log"][:, dec] for e in d_even], axis=0)
    small["dn_dt_bias"] = jnp.concatenate([e["dt_bias"][:, dec] for e in d_even], axis=0)
    small["fox_f_bias"] = jnp.concatenate([e["f_bias"][:, fgt] for e in d_even], axis=0)
    small["dn_norm_g"] = jnp.concatenate([e["dn_norm_g"] for e in d_even], axis=0)
    small["fox_q_norm_g"] = jnp.concatenate([e["q_g"] for e in d_even], axis=0)
    small["fox_k_norm_g"] = jnp.concatenate([e["k_g"] for e in d_even], axis=0)
    return loss, dx, small


MESH = pl.DeviceIdType.MESH
ANY = pl.BlockSpec(memory_space=pl.ANY)


def _place():
    x, y, c = lax.axis_index("x"), lax.axis_index("y"), lax.axis_index("c")
    return x, y, c, [(1 - x, y), (x, 1 - y), (1 - x, 1 - y)]


def _remote(src, dst, send_sem, recv_sem, to):
    return pltpu.make_async_remote_copy(src_ref=src, dst_ref=dst, send_sem=send_sem, recv_sem=recv_sem,
                                        device_id=to, device_id_type=MESH)


def _aligned(start, multiple):
    return start if isinstance(start, int) else pl.multiple_of(start, multiple)


def _quarter(ref, kind, chip, half, rows, cols):
    k = 2 * chip[0] + chip[1]
    hr = rows // 2
    assert hr % 16 == 0 and cols % LANES == 0
    if kind == "col":
        return ref.at[:, pl.ds(_aligned(half * hr, 16), hr), pl.ds(_aligned(k * cols, LANES), cols)]
    return ref.at[:, pl.ds(_aligned(k * rows + half * hr, 16), hr), :]


def _place_quarter(shard, kind, kc, name, first=0, count=None):
    l, rows, cols = shard.shape
    l = l - first if count is None else count
    tr = rows
    while tr * cols * 4 > (2 << 20) and tr % 32 == 0:
        tr //= 2
    nr = rows // tr
    if kind == "col":
        out_spec = pl.BlockSpec((None, tr, cols), lambda li, i, kc_ref: (li, i, kc_ref[0]))
        out_shape = (l, rows, 4 * cols)
    else:
        out_spec = pl.BlockSpec((None, tr, cols), lambda li, i, kc_ref: (li, kc_ref[0] * nr + i, 0))
        out_shape = (l, 4 * rows, cols)

    def body(kc_ref, x_ref, o_ref):
        o_ref[...] = x_ref[...].astype(BF16)

    return pl.pallas_call(
        body, name=name,
        grid_spec=pltpu.PrefetchScalarGridSpec(
            num_scalar_prefetch=1, grid=(l, nr),
            in_specs=[pl.BlockSpec((None, tr, cols), lambda li, i, kc_ref: (li + first, i, 0))],
            out_specs=out_spec),
        out_shape=jax.ShapeDtypeStruct(out_shape, BF16),
        compiler_params=_cparams("parallel", "parallel"),
    )(kc, shard)


def _gather_weights(wholes, kinds):
    n = len(wholes)

    def dims(ref, kind):
        _, r, cc = ref.shape
        return (r, cc // 4) if kind == "col" else (r // 4, cc)

    def body(*refs):
        bufs = refs[n:2 * n]
        send_sems, recv_sems = refs[2 * n:]
        x, y, c, chips = _place()
        sibling = (x, y, 1 - c)
        first, passed = [], []
        for t in range(n):
            rows, cols = dims(bufs[t], kinds[t])
            mine = _quarter(bufs[t], kinds[t], (x, y), c, rows, cols)
            for j, chip in enumerate(chips):
                cp = _remote(mine, mine, send_sems.at[t, j], recv_sems.at[t, j], (*chip, c))
                cp.start()
                first.append(cp)
        for j, chip in enumerate(chips):
            for t in range(n):
                rows, cols = dims(bufs[t], kinds[t])
                got = _quarter(bufs[t], kinds[t], chip, c, rows, cols)
                _remote(got, got, send_sems.at[t, j], recv_sems.at[t, j], (*chip, c)).wait_recv()
                cp = _remote(got, got, send_sems.at[t, 3 + j], recv_sems.at[t, 3 + j], sibling)
                cp.start()
                passed.append(cp)
        for j, chip in enumerate(chips):
            for t in range(n):
                rows, cols = dims(bufs[t], kinds[t])
                got = _quarter(bufs[t], kinds[t], chip, 1 - c, rows, cols)
                _remote(got, got, send_sems.at[t, 3 + j], recv_sems.at[t, 3 + j], sibling).wait_recv()
        for cp in first + passed:
            cp.wait_send()

    return pl.pallas_call(
        body, name="gather_weights", in_specs=[ANY] * n, out_specs=[ANY] * n,
        out_shape=[jax.ShapeDtypeStruct(a.shape, a.dtype) for a in wholes],
        input_output_aliases={t: t for t in range(n)},
        scratch_shapes=[pltpu.SemaphoreType.DMA((n, 6)), pltpu.SemaphoreType.DMA((n, 6))],
        compiler_params=pltpu.CompilerParams(has_side_effects=True),
    )(*wholes)


def _quarter_dims(ref, kind):
    _, r, cc = ref.shape
    return (r, cc // 4) if kind == "col" else (r // 4, cc)


def _gather_chips_copies(bufs, sems, kinds):
    x, y, c, chips = _place()
    copies = []
    for t, buf in enumerate(bufs):
        rows, cols = _quarter_dims(buf, kinds[t])
        mine = _quarter(buf, kinds[t], (x, y), c, rows, cols)
        for j, chip in enumerate(chips):
            pair = 2 * (OTHER_CHIPS * t + j)
            copies.append(_remote(mine, mine, sems[pair], sems[pair + 1], (*chip, c)))
    return copies


def _gather_start(wholes, kinds, after, tag):
    n = len(wholes)
    n_sems = 2 * OTHER_CHIPS * n
    n_in = n + len(after)

    def body(*refs):
        for cp in _gather_chips_copies(refs[:n], refs[n_in + n:n_in + n + n_sems], kinds):
            cp.start()
        refs[-1][...] = jnp.zeros_like(refs[-1])

    held = [pltpu.with_memory_space_constraint(a, pltpu.HBM) for a in wholes]
    out = pl.pallas_call(
        body, name=f"gather_start_{tag}", in_specs=[HBM] * n + [ANY] * len(after),
        out_specs=(*[HBM] * n, *[SEM] * n_sems, pl.BlockSpec(memory_space=pltpu.VMEM)),
        out_shape=(*[pltpu.HBM(a.shape, a.dtype) for a in held], *[pltpu.SemaphoreType.DMA(())] * n_sems,
                   jax.ShapeDtypeStruct((8, LANES), F32)),
        input_output_aliases={i: i for i in range(n)},
        compiler_params=pltpu.CompilerParams(has_side_effects=SPLIT_COPY),
    )(*held, *after)
    return out[n:n + n_sems], out[:n], out[-1]


def _gather_wait(sems, wholes, kinds, after, tag):
    n = len(wholes)

    def body(*refs):
        for cp in _gather_chips_copies(refs[:n], refs[n:n + len(sems)], kinds):
            cp.wait_send()
            cp.wait_recv()

    return pl.pallas_call(
        body, name=f"gather_wait_{tag}", in_specs=[HBM] * n + [SEM] * len(sems) + [ANY],
        out_specs=tuple([HBM] * n), out_shape=tuple(pltpu.HBM(a.shape, a.dtype) for a in wholes),
        input_output_aliases={i: i for i in range(n)},
        compiler_params=pltpu.CompilerParams(has_side_effects=SPLIT_COPY),
    )(*wholes, *sems, after)


def _gather_forward(wholes, kinds, tag):
    n = len(wholes)

    def body(*refs):
        bufs = refs[n:2 * n]
        send_sems, recv_sems = refs[2 * n:]
        x, y, c, chips = _place()
        copies = []
        for t in range(n):
            rows, cols = _quarter_dims(bufs[t], kinds[t])
            for j, chip in enumerate(chips):
                got = _quarter(bufs[t], kinds[t], chip, c, rows, cols)
                cp = _remote(got, got, send_sems.at[t, j], recv_sems.at[t, j], (x, y, 1 - c))
                cp.start()
                copies.append(cp)
        for cp in copies:
            cp.wait_send()
        for t in range(n):
            rows, cols = _quarter_dims(bufs[t], kinds[t])
            for j, chip in enumerate(chips):
                got = _quarter(bufs[t], kinds[t], chip, 1 - c, rows, cols)
                _remote(got, got, send_sems.at[t, j], recv_sems.at[t, j], (x, y, 1 - c)).wait_recv()

    return pl.pallas_call(
        body, name=f"gather_forward_{tag}", in_specs=[ANY] * n, out_specs=[ANY] * n,
        out_shape=[jax.ShapeDtypeStruct(a.shape, a.dtype) for a in wholes],
        input_output_aliases={t: t for t in range(n)},
        scratch_shapes=[pltpu.SemaphoreType.DMA((n, OTHER_CHIPS)), pltpu.SemaphoreType.DMA((n, OTHER_CHIPS))],
        compiler_params=pltpu.CompilerParams(has_side_effects=True),
    )(*wholes)


def _canonical(a, kind):
    l, r, c = a.shape
    return a.reshape(l, 1, r, c) if kind == "col" else a.reshape(l, 4, r // 4, c)


def _add_tile(rows, cols):
    tc = cols if cols <= 1536 else cols // 4
    tr = rows
    while tr * tc * 4 > (1 << 20) and tr % 16 == 0:
        tr //= 2
    return tr, tc


def _rs_add_sibling(part, got, c, name):
    l, a, hr, cols = got.shape
    tr, tc = _add_tile(hr, cols)
    nr = hr // tr

    def body(c_ref, p_ref, g_ref, o32_ref, o16_ref):
        s = p_ref[...] + g_ref[...]
        o32_ref[...] = s
        o16_ref[...] = s.astype(BF16)

    blk = (None, None, tr, tc)
    spec = pl.BlockSpec(blk, lambda li, ai, i, j, c_ref: (li, ai, i, j))
    return pl.pallas_call(
        body, name=name,
        grid_spec=pltpu.PrefetchScalarGridSpec(
            num_scalar_prefetch=1, grid=(l, a, nr, cols // tc),
            in_specs=[pl.BlockSpec(blk, lambda li, ai, i, j, c_ref: (li, ai, c_ref[0] * nr + i, j)), spec],
            out_specs=[spec, spec]),
        out_shape=[jax.ShapeDtypeStruct(got.shape, F32), jax.ShapeDtypeStruct(got.shape, BF16)],
        compiler_params=_cparams("parallel", "parallel", "parallel", "parallel"),
    )(c, part, got)


def _quarter4(ref, kind, chip, cols):
    k = 2 * chip[0] + chip[1]
    if kind == "col":
        return ref.at[:, :, :, pl.ds(pl.multiple_of(k * cols, LANES), cols)]
    return ref.at[:, pl.ds(k, 1), :, :]


HBM = pl.BlockSpec(memory_space=pltpu.HBM)
SEM = pl.BlockSpec(memory_space=pltpu.SEMAPHORE)
SPLIT_COPY = pltpu.SideEffectType.DATAFLOW_SIDE_EFFECTING
OTHER_CHIPS = 3


def _quarter4_shape(a, kind):
    l, _, hr, cols = a.shape
    return (l, 1, hr, cols // 4 if kind == "col" else cols)


def _rs_chips_copies(srcs, lands, sems, kinds):
    x, y, c, chips = _place()
    copies = []
    for t, (src, land) in enumerate(zip(srcs, lands)):
        cols = _quarter4_shape(src, kinds[t])[3]
        for j, chip in enumerate(chips):
            pair = 2 * (OTHER_CHIPS * t + j)
            copies.append(_remote(_quarter4(src, kinds[t], chip, cols), land.at[j], sems[pair], sems[pair + 1],
                                  (*chip, c)))
    return copies


def _split_start(copies, srcs, lands, n_sems, name):
    n = len(srcs)

    def body(*refs):
        for cp in copies(refs[:n], refs[n:2 * n], refs[4 * n:4 * n + n_sems]):
            cp.start()
        refs[-1][...] = jnp.zeros_like(refs[-1])

    held = [pltpu.with_memory_space_constraint(a, pltpu.HBM) for a in (*srcs, *lands)]
    out = pl.pallas_call(
        body, name=name, in_specs=[HBM] * (2 * n),
        out_specs=(*[HBM] * (2 * n), *[SEM] * n_sems, pl.BlockSpec(memory_space=pltpu.VMEM)),
        out_shape=(*[pltpu.HBM(a.shape, a.dtype) for a in held], *[pltpu.SemaphoreType.DMA(())] * n_sems,
                   jax.ShapeDtypeStruct((8, LANES), F32)),
        input_output_aliases={i: i for i in range(2 * n)},
        compiler_params=pltpu.CompilerParams(has_side_effects=SPLIT_COPY),
    )(*held)
    return out[2 * n:2 * n + n_sems], out[:n], out[n:2 * n], out[-1]


def _split_wait(copies, sems, srcs, lands, after, name):
    n = len(srcs)

    def body(*refs):
        for cp in copies(refs[:n], refs[n:2 * n], refs[2 * n:2 * n + len(sems)]):
            cp.wait_send()
            cp.wait_recv()

    out = pl.pallas_call(
        body, name=name, in_specs=[HBM] * (2 * n) + [SEM] * len(sems) + [ANY],
        out_specs=tuple([HBM] * (2 * n)),
        out_shape=tuple(pltpu.HBM(a.shape, a.dtype) for a in (*srcs, *lands)),
        input_output_aliases={i: i for i in range(2 * n)},
        compiler_params=pltpu.CompilerParams(has_side_effects=SPLIT_COPY),
    )(*srcs, *lands, *sems, after)
    return out[:n], out[n:]


def _rs_sibling_copies(srcs, lands, sems):
    x, y, c, _ = _place()
    copies = []
    for t, (src, land) in enumerate(zip(srcs, lands)):
        hr = src.shape[2] // 2
        gives = src.at[:, :, pl.ds(pl.multiple_of((1 - c) * hr, 8), hr), :]
        copies.append(_remote(gives, land, sems[2 * t], sems[2 * t + 1], (x, y, 1 - c)))
    return copies


def _rs_add_chips(sum32, got, kind, kc, name):
    _, l, _, hr, cols = got.shape
    tr, _ = _add_tile(hr, cols)
    nr = hr // tr
    k_arr, c_arr = kc
    if kind == "col":
        own = pl.BlockSpec((None, None, tr, cols), lambda li, i, k_ref, c_ref: (li, 0, i, k_ref[0]))
    else:
        own = pl.BlockSpec((None, None, tr, cols), lambda li, i, k_ref, c_ref: (li, k_ref[0], i, 0))

    def body(k_ref, c_ref, own_ref, got_ref, o_ref):
        o_ref[...] = ((own_ref[...] + got_ref[0].astype(F32)) + got_ref[1].astype(F32)) + got_ref[2].astype(F32)

    return pl.pallas_call(
        body, name=name,
        grid_spec=pltpu.PrefetchScalarGridSpec(
            num_scalar_prefetch=2, grid=(l, nr),
            in_specs=[own, pl.BlockSpec((3, None, None, tr, cols), lambda li, i, k_ref, c_ref: (0, li, 0, i, 0))],
            out_specs=pl.BlockSpec((None, tr, cols), lambda li, i, k_ref, c_ref: (li, c_ref[0] * nr + i, 0))),
        out_shape=jax.ShapeDtypeStruct((l, 2 * hr, cols), F32),
        compiler_params=_cparams("parallel", "parallel"),
    )(k_arr, c_arr, sum32, got)


def _rs_finish(quarters):
    n = len(quarters)

    def body(*refs):
        bufs = refs[n:2 * n]
        send_sems, recv_sems = refs[2 * n:]
        x, y, c, _ = _place()
        copies = []
        for t in range(n):
            hr = bufs[t].shape[1] // 2
            mine = bufs[t].at[:, pl.ds(pl.multiple_of(c * hr, 8), hr), :]
            cp = _remote(mine, mine, send_sems.at[t], recv_sems.at[t], (x, y, 1 - c))
            cp.start()
            copies.append(cp)
        for cp in copies:
            cp.wait()

    return pl.pallas_call(
        body, name="reduce_finish", in_specs=[ANY] * n, out_specs=[ANY] * n,
        out_shape=[jax.ShapeDtypeStruct(a.shape, a.dtype) for a in quarters],
        input_output_aliases={t: t for t in range(n)},
        scratch_shapes=[pltpu.SemaphoreType.DMA((n,)), pltpu.SemaphoreType.DMA((n,))],
        compiler_params=pltpu.CompilerParams(has_side_effects=True),
    )(*quarters)


def _reduce_start(parts, kinds, names, tag):
    canon = [_canonical(p, kind) for p, kind in zip(parts, kinds)]
    lands = [lax.empty(a.shape[:2] + (a.shape[2] // 2, a.shape[3]), a.dtype) for a in canon]
    sems, srcs, lands, token = _split_start(_rs_sibling_copies, canon, lands, 2 * len(canon),
                                            f"reduce_sibling_start_{tag}")
    return (sems, srcs, lands, kinds, names, tag), token


def _reduce_middle(state, after):
    sems, srcs, lands, kinds, names, tag = state
    c_arr = jnp.reshape(lax.axis_index("c"), (1,)).astype(jnp.int32)
    srcs, from_sibling = _split_wait(_rs_sibling_copies, sems, srcs, lands, after, f"reduce_sibling_wait_{tag}")
    sums = [_rs_add_sibling(p, g, c_arr, f"reduce_add_sibling_{nm}") for p, g, nm in zip(srcs, from_sibling, names)]
    sums16 = [s16 for _, s16 in sums]
    copies = functools.partial(_rs_chips_copies, kinds=kinds)
    lands = [lax.empty((OTHER_CHIPS,) + _quarter4_shape(a, k), a.dtype) for a, k in zip(sums16, kinds)]
    sems, srcs, lands, token = _split_start(copies, sums16, lands, 2 * OTHER_CHIPS * len(sums16),
                                            f"reduce_chips_start_{tag}")
    return (sems, srcs, lands, [s32 for s32, _ in sums], kinds, names, tag), token


def _reduce_finish(state, after):
    sems, srcs, lands, sums32, kinds, names, tag = state
    x, y, c = lax.axis_index("x"), lax.axis_index("y"), lax.axis_index("c")
    kc = (jnp.reshape(2 * x + y, (1,)).astype(jnp.int32), jnp.reshape(c, (1,)).astype(jnp.int32))
    copies = functools.partial(_rs_chips_copies, kinds=kinds)
    _, from_chips = _split_wait(copies, sems, srcs, lands, after, f"reduce_chips_wait_{tag}")
    halves = [_rs_add_chips(s32, g, kind, kc, f"reduce_add_chips_{nm}")
              for s32, g, kind, nm in zip(sums32, from_chips, kinds, names)]
    return _rs_finish(halves)


SMALL_PEERS = 7


def _small_exchange(pack):
    rows = pack.shape[0]

    def body(p_ref, slots_ref, total_ref, send_sems, recv_sems):
        x, y, c, _ = _place()
        me = 4 * x + 2 * y + c
        slots_ref[me] = p_ref[...]
        copies = []
        for p in range(1, SMALL_PEERS + 1):
            px, py, pc = (p >> 2) & 1, (p >> 1) & 1, p & 1
            peer = (1 - x if px else x, 1 - y if py else y, 1 - c if pc else c)
            cp = _remote(p_ref, slots_ref.at[me], send_sems.at[p - 1], recv_sems.at[p - 1], peer)
            cp.start()
            copies.append(cp)
        for cp in copies:
            cp.wait()
        total = slots_ref[0]
        for i in range(1, SMALL_PEERS + 1):
            total = total + slots_ref[i]
        total_ref[...] = total

    vmem = pl.BlockSpec(memory_space=pltpu.VMEM)
    return pl.pallas_call(
        body, name="small_exchange", in_specs=[vmem], out_specs=[vmem, vmem],
        out_shape=[jax.ShapeDtypeStruct((SMALL_PEERS + 1, rows, LANES), F32), jax.ShapeDtypeStruct((rows, LANES), F32)],
        scratch_shapes=[pltpu.SemaphoreType.DMA((SMALL_PEERS,)), pltpu.SemaphoreType.DMA((SMALL_PEERS,))],
        compiler_params=pltpu.CompilerParams(has_side_effects=True),
    )(pack)


def _pack(arrays):
    rows = []
    for a in arrays:
        flat = a.reshape(-1).astype(F32)
        rows.append(jnp.pad(flat, (0, (-flat.shape[0]) % LANES)).reshape(-1, LANES))
    out = jnp.concatenate(rows, axis=0)
    return jnp.pad(out, ((0, (-out.shape[0]) % 8), (0, 0)))


def _unpack(pack, shapes):
    out, r = [], 0
    for sh in shapes:
        size = math.prod(sh)
        nr = -(-size // LANES)
        out.append(pack[r:r + nr].reshape(-1)[:size].reshape(sh))
        r += nr
    return out


def _adamw(w, g, m, v, name):
    shape = w.shape
    to2d = lambda a: a.reshape(-1, shape[-1])
    rows = math.prod(shape[:-1])
    tile = 256 if rows % 256 == 0 else rows

    def fn(wb, gb, mb, vb):
        m2 = ADAM_B1 * mb + (1.0 - ADAM_B1) * gb
        v2 = ADAM_B2 * vb + (1.0 - ADAM_B2) * (gb * gb)
        m_hat = m2 / (1.0 - ADAM_B1 ** ADAM_STEP)
        v_hat = v2 / (1.0 - ADAM_B2 ** ADAM_STEP)
        return -ADAM_LR * (m_hat / (jnp.sqrt(v_hat) + ADAM_EPS) + ADAM_WD * wb), m2, v2

    res = _rowwise(fn, [to2d(w), to2d(g), to2d(m), to2d(v)], [], [(shape[-1], F32)] * 3, [], tile=tile, name=name)
    return [r.reshape(shape) for r in res]


def _adamw_layer(w, g, m, v, layer, outs, name):
    _, rows, cols = w.shape
    tile = rows
    while tile * cols * 4 > (1 << 20) and tile % 16 == 0:
        tile //= 2

    def body(w_ref, g_ref, m_ref, v_ref, *rest):
        g_out, d_out, m_out, v_out = rest[-4:]
        gb = g_ref[...]
        m2 = ADAM_B1 * m_ref[...] + (1.0 - ADAM_B1) * gb
        v2 = ADAM_B2 * v_ref[...] + (1.0 - ADAM_B2) * (gb * gb)
        m_hat = m2 / (1.0 - ADAM_B1 ** ADAM_STEP)
        v_hat = v2 / (1.0 - ADAM_B2 ** ADAM_STEP)
        g_out[...] = gb
        d_out[...] = -ADAM_LR * (m_hat / (jnp.sqrt(v_hat) + ADAM_EPS) + ADAM_WD * w_ref[...])
        m_out[...] = m2
        v_out[...] = v2

    stacked = pl.BlockSpec((None, tile, cols), lambda i: (layer, i, 0))
    return pl.pallas_call(
        body, name=name, grid=(rows // tile,),
        in_specs=[stacked, pl.BlockSpec((None, tile, cols), lambda i: (0, i, 0)), stacked, stacked] + [ANY] * 4,
        out_specs=[stacked] * 4, out_shape=[jax.ShapeDtypeStruct(w.shape, F32)] * 4,
        input_output_aliases={4 + i: i for i in range(4)}, compiler_params=_cparams("parallel"),
    )(w, g, m, v, *outs)


BIG = (("ffn1_w_gu", "col"), ("ffn1_w_down", "row"), ("w_in_even", "col"), ("w_out_even", "row"),
       ("w_in_odd", "col"), ("w_out_odd", "row"), ("ffn2_w_gu", "col"), ("ffn2_w_down", "row"))
SMALL = ("norm_ffn1", "norm_mix", "dn_conv_w", "dn_a_log", "dn_dt_bias", "dn_norm_g", "fox_q_norm_g", "fox_k_norm_g",
         "fox_f_bias", "norm_ffn2")
WEIGHTS = ("norm_ffn1", "ffn1_w_gu", "ffn1_w_down", "norm_mix", "w_in_even", "dn_conv_w", "dn_a_log", "dn_dt_bias",
           "dn_norm_g", "fox_q_norm_g", "fox_k_norm_g", "fox_f_bias", "w_out_even", "w_in_odd", "w_out_odd",
           "norm_ffn2", "ffn2_w_gu", "ffn2_w_down")


def _step(x, target, w, m, v):
    k = 2 * lax.axis_index("x") + lax.axis_index("y")
    n_conv = w["dn_conv_w"].shape[2]

    kc = jnp.reshape(k, (1,)).astype(jnp.int32)
    kinds = dict(BIG)
    quarters = {name: w[name] for name in kinds}
    quarters["w_in_even"] = jnp.pad(w["w_in_even"], ((0, 0), (0, 0), (0, EVEN_QUARTER_PAD - EVEN_QUARTER)))
    first_names = [name for name in kinds if name not in ("w_in_odd", "w_out_odd")]
    rest_names = list(kinds)

    def even_columns(whole):
        padded = whole["w_in_even"]
        ref_order = jnp.concatenate([padded[..., q * EVEN_QUARTER_PAD:q * EVEN_QUARTER_PAD + EVEN_QUARTER]
                                     for q in range(4)], axis=-1)
        return {**whole, "w_in_even": _even_to_kernel_layout(ref_order)}

    conv_slots, _ = _small_exchange(_pack([w["dn_conv_w"]]))
    placed = [_place_quarter(quarters[name], kinds[name], kc, f"place_first_{name}", 0, 1) for name in first_names]
    gathered = _gather_weights(placed, [kinds[name] for name in first_names])
    first = even_columns(dict(zip(first_names, gathered)))
    placed = [_place_quarter(quarters[name], kinds[name], kc, f"place_rest_{name}", 1 if name in first_names else 0)
              for name in rest_names]
    rest_kinds = [kinds[name] for name in rest_names]
    sems, on_their_way, token = _gather_start(placed, rest_kinds, [conv_slots, *gathered], "rest")

    def rest_after(value):
        landed = _gather_wait(sems, on_their_way, rest_kinds, value, "rest")
        return even_columns(dict(zip(rest_names, _gather_forward(landed, rest_kinds, "rest"))))

    whole = {}
    conv_rows = math.prod(w["dn_conv_w"].shape) // LANES
    conv_quarters = [conv_slots[2 * q, :conv_rows].reshape(w["dn_conv_w"].shape) for q in range(4)]
    whole["dn_conv_w"] = jnp.concatenate(conv_quarters, axis=-1)
    for name in SMALL:
        if name != "dn_conv_w":
            whole[name] = w[name]

    updated = {name: [lax.empty(w[name].shape, F32) for _ in range(4)] for name in kinds}

    def on_reduced(layer, layer_grads):
        for name, g in layer_grads.items():
            if name == "w_in_even":
                g = g[..., :EVEN_QUARTER]
            stacked_layer = layer if w[name].shape[0] == w["norm_mix"].shape[0] else layer // 2
            updated[name] = _adamw_layer(w[name], g, m[name], v[name], stacked_layer, updated[name], f"adamw_{name}")

    loss, dx, small = _forward_backward(x, target, whole, first, rest_after, token, on_reduced)

    _, small_sum = _small_exchange(_pack([small[n] for n in SMALL]))
    grads = dict(zip(SMALL, _unpack(small_sum, [small[n].shape for n in SMALL])))
    grads["dn_conv_w"] = lax.dynamic_slice_in_dim(grads["dn_conv_w"], k * n_conv, n_conv, axis=2)
    delta, new_m, new_v = {}, {}, {}
    for name in kinds:
        grads[name], delta[name], new_m[name], new_v[name] = updated[name]
    packs = [_pack([d[n] for n in SMALL]) for d in (w, grads, m, v)]
    shapes = [w[n].shape for n in SMALL]
    for out, res in zip((delta, new_m, new_v), _adamw(*packs, "adamw_small")):
        out.update(zip(SMALL, _unpack(res, shapes)))
    total_loss = lax.psum(loss[0, 0], ("x", "y", "c"))
    return total_loss, dx, grads, delta, new_m, new_v


def kernel(x, norm_ffn1, ffn1_w_gu, ffn1_w_down, norm_mix, w_in_even, dn_conv_w, dn_a_log, dn_dt_bias, dn_norm_g, fox_q_norm_g, fox_k_norm_g, fox_f_bias, w_out_even, w_in_odd, w_out_odd, norm_ffn2, ffn2_w_gu, ffn2_w_down, loss_target, m_norm_ffn1, m_ffn1_w_gu, m_ffn1_w_down, m_norm_mix, m_w_in_even, m_dn_conv_w, m_dn_a_log, m_dn_dt_bias, m_dn_norm_g, m_fox_q_norm_g, m_fox_k_norm_g, m_fox_f_bias, m_w_out_even, m_w_in_odd, m_w_out_odd, m_norm_ffn2, m_ffn2_w_gu, m_ffn2_w_down, v_norm_ffn1, v_ffn1_w_gu, v_ffn1_w_down, v_norm_mix, v_w_in_even, v_dn_conv_w, v_dn_a_log, v_dn_dt_bias, v_dn_norm_g, v_fox_q_norm_g, v_fox_k_norm_g, v_fox_f_bias, v_w_out_even, v_w_in_odd, v_w_out_odd, v_norm_ffn2, v_ffn2_w_gu, v_ffn2_w_down):
    w = dict(zip(WEIGHTS, (norm_ffn1, ffn1_w_gu, ffn1_w_down, norm_mix, w_in_even, dn_conv_w, dn_a_log, dn_dt_bias,
                           dn_norm_g, fox_q_norm_g, fox_k_norm_g, fox_f_bias, w_out_even, w_in_odd, w_out_odd,
                           norm_ffn2, ffn2_w_gu, ffn2_w_down)))
    m = dict(zip(WEIGHTS, (m_norm_ffn1, m_ffn1_w_gu, m_ffn1_w_down, m_norm_mix, m_w_in_even, m_dn_conv_w, m_dn_a_log,
                           m_dn_dt_bias, m_dn_norm_g, m_fox_q_norm_g, m_fox_k_norm_g, m_fox_f_bias, m_w_out_even,
                           m_w_in_odd, m_w_out_odd, m_norm_ffn2, m_ffn2_w_gu, m_ffn2_w_down)))
    v = dict(zip(WEIGHTS, (v_norm_ffn1, v_ffn1_w_gu, v_ffn1_w_down, v_norm_mix, v_w_in_even, v_dn_conv_w, v_dn_a_log,
                           v_dn_dt_bias, v_dn_norm_g, v_fox_q_norm_g, v_fox_k_norm_g, v_fox_f_bias, v_w_out_even,
                           v_w_in_odd, v_w_out_odd, v_norm_ffn2, v_ffn2_w_gu, v_ffn2_w_down)))
    loss, dx, grads, delta, new_m, new_v = _step(x[0], loss_target[0], w, m, v)
    return (loss, dx[None], *[grads[n] for n in WEIGHTS], *[delta[n] for n in WEIGHTS],
            *[new_m[n] for n in WEIGHTS], *[new_v[n] for n in WEIGHTS])
```
